```python
import math
import jax, jax.numpy as jnp
from jax import lax
import numpy as np

D_MODEL = 1024
BATCH = 8
SEQ = 16384
DEPTH = 2

GRID_W = 64
CTX_LEN = 256
N_MIXERS = 2
EXPAND = 2
E_CONV = EXPAND * D_MODEL
CONV_W = 3
E_SSM = EXPAND * D_MODEL
SSM_GROUP = 16
N_GROUPS = E_SSM // SSM_GROUP
SSM_STATE = 64
SCAN_CHUNK = 128
LN_EPS = 1e-5
DN_ALPHA = (2 * DEPTH) ** 0.25
DN_BETA = (8 * DEPTH) ** -0.25
N_CONV_LAYERS = (DEPTH + 1) // 2
N_SSM_LAYERS = DEPTH // 2

kernel_name = "hybrid_shortconv_s5_prefix_dit"


def _layernorm(x, g, b):
    xf = x.astype(jnp.float32)
    mu = jnp.mean(xf, axis=-1, keepdims=True)
    var = jnp.mean(jnp.square(xf - mu), axis=-1, keepdims=True)
    return ((xf - mu) * lax.rsqrt(var + LN_EPS) * g.astype(jnp.float32) + b.astype(jnp.float32)).astype(x.dtype)


def _ada(cvec, w, b):
    m = jax.nn.silu(cvec) @ w + b
    shift, scale, gate = jnp.split(m, 3, axis=-1)
    return shift[..., None, :], scale[..., None, :], gate[..., None, :]


def _shift_conv(u, w, axis):
    n = u.shape[axis]
    pad = [(0, 0)] * u.ndim
    pad[axis] = (1, 1)
    up = jnp.pad(u, pad)
    sl = lambda k: lax.slice_in_dim(up, k, k + n, axis=axis)
    return sl(0) * w[0] + sl(1) * w[1] + sl(2) * w[2]


def _conv_grid(u, w):
    bsz, L, E = u.shape
    rows = L // GRID_W
    half = E // 2
    ug = u.reshape(bsz, rows, GRID_W, E)
    yh = _shift_conv(ug[..., :half], w[:, :half], axis=2)
    yv = _shift_conv(ug[..., half:], w[:, half:], axis=1)
    return jnp.concatenate([yh, yv], axis=-1).reshape(bsz, L, E)


def _conv_mixer(h, w_in, w_conv, w_out, grid):
    bg, cg, v, z = jnp.split(h @ w_in, 4, axis=-1)
    u = cg * v
    yc = _conv_grid(u, w_conv) if grid else _shift_conv(u, w_conv, axis=1)
    return (bg * yc * jax.nn.silu(z)) @ w_out


def _zoh(lam_re, lam_im, log_step, b_re, b_im):
    dt = jnp.exp(log_step)[:, None]
    mag = jnp.exp(lam_re * dt)
    ar = mag * jnp.cos(lam_im * dt)
    ai = mag * jnp.sin(lam_im * dt)
    qr, qi = ar - 1.0, ai
    den = lam_re * lam_re + lam_im * lam_im
    fr = (qr * lam_re + qi * lam_im) / den
    fi = (qi * lam_re - qr * lam_im) / den
    bbr = fr[..., None] * b_re - fi[..., None] * b_im
    bbi = fr[..., None] * b_im + fi[..., None] * b_re
    return ar, ai, bbr, bbi


def _binop(e1, e2):
    a1r, a1i, b1r, b1i = e1
    a2r, a2i, b2r, b2i = e2
    return (a2r * a1r - a2i * a1i,
            a2r * a1i + a2i * a1r,
            a2r * b1r - a2i * b1i + b2r,
            a2r * b1i + a2i * b1r + b2i)


def _s5_scan(u, h0r, h0i, ar, ai, bbr, bbi, c_re, c_im, with_output):
    bsz, L, _ = u.shape
    n_blk = L // SCAN_CHUNK
    ub = u.reshape(bsz, n_blk, SCAN_CHUNK, N_GROUPS, SSM_GROUP).transpose(1, 0, 2, 3, 4)

    def step(carry, u_blk):
        hr, hi = carry
        bur = jnp.einsum('btgp,gnp->btgn', u_blk, bbr)
        bui = jnp.einsum('btgp,gnp->btgn', u_blk, bbi)
        bur = bur.at[:, 0].add(ar * hr - ai * hi)
        bui = bui.at[:, 0].add(ar * hi + ai * hr)
        a_r = jnp.broadcast_to(ar, bur.shape)
        a_i = jnp.broadcast_to(ai, bur.shape)
        _, _, sr, si = lax.associative_scan(_binop, (a_r, a_i, bur, bui), axis=1)
        new = (sr[:, -1], si[:, -1])
        if with_output:
            y = jnp.einsum('btgn,gpn->btgp', sr, c_re) - jnp.einsum('btgn,gpn->btgp', si, c_im)
            return new, y
        return new, None

    h_final, ys = lax.scan(step, (h0r, h0i), ub)
    if with_output:
        ys = ys.transpose(1, 0, 2, 3, 4).reshape(bsz, L, E_SSM)
    return ys, h_final


def _glu_gate_out(y, z, w_glu, b_glu, w_out):
    g = jax.nn.gelu(y)
    g = g * jax.nn.sigmoid(g @ w_glu + b_glu)
    return (g * jax.nn.silu(z)) @ w_out


def _s5_mixer(h_lat, h_ctx, w_in, lam_re, lam_im, log_step, b_re, b_im, c_re, c_im, d,
              w_glu, b_glu, w_out, ctx_out):
    u_l, z_l = jnp.split(h_lat @ w_in, 2, axis=-1)
    pc = h_ctx @ w_in
    u_c = pc[..., :E_SSM]
    y_l = d * u_l
    y_c = d * u_c if ctx_out else None
    for r in range(2):
        ar, ai, bbr, bbi = _zoh(lam_re[r], lam_im[r], log_step[r], b_re[r], b_im[r])
        seq = (lambda t: t[:, ::-1]) if r == 1 else (lambda t: t)
        dtype = jnp.result_type(u_c.dtype, bbr.dtype)
        h0 = jnp.zeros((u_c.shape[0], N_GROUPS, SSM_STATE), dtype)
        yc, hc = _s5_scan(seq(u_c), h0, h0, ar, ai, bbr, bbi, c_re[r], c_im[r], ctx_out)
        yl, _ = _s5_scan(seq(u_l), hc[0], hc[1], ar, ai, bbr, bbi, c_re[r], c_im[r], True)
        y_l = y_l + seq(yl)
        if ctx_out:
            y_c = y_c + seq(yc)
    out_l = _glu_gate_out(y_l, z_l, w_glu, b_glu, w_out)
    out_c = _glu_gate_out(y_c, pc[..., E_SSM:], w_glu, b_glu, w_out) if ctx_out else None
    return out_l, out_c


def _fwd_setup_inputs(seed: int = 0) -> dict:
    key = jax.random.key(seed)
    ks = jax.random.split(key, 24)
    f32 = jnp.float32
    nrm = lambda k, shape, s: jax.random.normal(k, shape, f32) * s
    nA, nB = N_CONV_LAYERS, N_SSM_LAYERS
    G, N, P = N_GROUPS, SSM_STATE, SSM_GROUP
    return {
        "x": nrm(ks[0], (BATCH, SEQ, D_MODEL), 1.0),
        "c": nrm(ks[1], (BATCH, D_MODEL), 1.0),
        "ctx": nrm(ks[2], (BATCH, CTX_LEN, D_MODEL), 1.0),
        "c_ctx": nrm(ks[3], (D_MODEL,), 1.0),
        "ada_w": nrm(ks[4], (DEPTH, D_MODEL, 3 * D_MODEL), D_MODEL ** -0.5),
        "ada_b": nrm(ks[5], (DEPTH, 3 * D_MODEL), 0.02),
        "ln_g": 1.0 + nrm(ks[6], (DEPTH, D_MODEL), 0.02),
        "ln_b": nrm(ks[7], (DEPTH, D_MODEL), 0.02),
        "conv_w_in": nrm(ks[8], (nA, D_MODEL, 4 * E_CONV), D_MODEL ** -0.5),
        "conv_w": nrm(ks[9], (nA, CONV_W, E_CONV), CONV_W ** -0.5),
        "conv_w_out": nrm(ks[10], (nA, E_CONV, D_MODEL), DN_BETA * E_CONV ** -0.5),
        "ssm_w_in": nrm(ks[11], (nB, D_MODEL, 2 * E_SSM), D_MODEL ** -0.5),
        "ssm_lam_re": -0.5 * jnp.exp(nrm(ks[12], (nB, 2, G, N), 0.05)),
        "ssm_lam_im": jnp.pi * jnp.arange(N, dtype=f32) + nrm(ks[13], (nB, 2, G, N), 0.05),
        "ssm_log_step": jax.random.uniform(ks[14], (nB, 2, G), f32, math.log(1e-3), math.log(1e-1)),
        "ssm_b_re": nrm(ks[15], (nB, 2, G, N, P), (2 * P) ** -0.5),
        "ssm_b_im": nrm(ks[16], (nB, 2, G, N, P), (2 * P) ** -0.5),
        "ssm_c_re": nrm(ks[17], (nB, 2, G, P, N), N ** -0.5),
        "ssm_c_im": nrm(ks[18], (nB, 2, G, P, N), N ** -0.5),
        "ssm_d": nrm(ks[19], (nB, E_SSM), 1.0),
        "ssm_w_glu": nrm(ks[20], (nB, E_SSM, E_SSM), E_SSM ** -0.5),
        "ssm_b_glu": nrm(ks[21], (nB, E_SSM), 0.02),
        "ssm_w_out": nrm(ks[22], (nB, E_SSM, D_MODEL), DN_BETA * E_SSM ** -0.5),
    }


def _fwd_reference(x, c, ctx, c_ctx, ada_w, ada_b, ln_g, ln_b, conv_w_in, conv_w, conv_w_out,
              ssm_w_in, ssm_lam_re, ssm_lam_im, ssm_log_step, ssm_b_re, ssm_b_im,
              ssm_c_re, ssm_c_im, ssm_d, ssm_w_glu, ssm_b_glu, ssm_w_out):
    for i in range(DEPTH):
        last = i == DEPTH - 1
        is_conv = (i % N_MIXERS) == 0
        j = i // N_MIXERS
        need_ctx_out = not last
        need_ctx_in = need_ctx_out or not is_conv
        sh, sc, gt = _ada(c, ada_w[i], ada_b[i])
        hx = x * (1.0 + sc) + sh
        if need_ctx_in:
            sh_c, sc_c, gt_c = _ada(c_ctx, ada_w[i], ada_b[i])
            hc = ctx * (1.0 + sc_c) + sh_c
        if is_conv:
            fx = _conv_mixer(hx, conv_w_in[j], conv_w[j], conv_w_out[j], True)
            fc = _conv_mixer(hc, conv_w_in[j], conv_w[j], conv_w_out[j], False) if need_ctx_out else None
        else:
            fx, fc = _s5_mixer(hx, hc, ssm_w_in[j], ssm_lam_re[j], ssm_lam_im[j], ssm_log_step[j],
                               ssm_b_re[j], ssm_b_im[j], ssm_c_re[j], ssm_c_im[j], ssm_d[j],
                               ssm_w_glu[j], ssm_b_glu[j], ssm_w_out[j], need_ctx_out)
        x = _layernorm(DN_ALPHA * x + gt * fx, ln_g[i], ln_b[i])
        if need_ctx_out:
            ctx = _layernorm(DN_ALPHA * ctx + gt_c * fc, ln_g[i], ln_b[i])
    return x


import jax as _jax
import jax.numpy as _jnp

TWIN_FORMAT = 'train_step'
FWD_PARAMS = ['x', 'c', 'ctx', 'c_ctx', 'ada_w', 'ada_b', 'ln_g', 'ln_b', 'conv_w_in', 'conv_w', 'conv_w_out', 'ssm_w_in', 'ssm_lam_re', 'ssm_lam_im', 'ssm_log_step', 'ssm_b_re', 'ssm_b_im', 'ssm_c_re', 'ssm_c_im', 'ssm_d', 'ssm_w_glu', 'ssm_b_glu', 'ssm_w_out']
TWIN_WEIGHTS = ['c_ctx', 'ada_w', 'ada_b', 'ln_g', 'ln_b', 'conv_w_in', 'conv_w', 'conv_w_out', 'ssm_w_in', 'ssm_lam_re', 'ssm_lam_im', 'ssm_log_step', 'ssm_b_re', 'ssm_b_im', 'ssm_c_re', 'ssm_c_im', 'ssm_d', 'ssm_w_glu', 'ssm_b_glu', 'ssm_w_out']
TWIN_DIFF_INPUT = 'x'
TWIN_INPUTS = ['x', 'c', 'ctx', 'c_ctx', 'ada_w', 'ada_b', 'ln_g', 'ln_b', 'conv_w_in', 'conv_w', 'conv_w_out', 'ssm_w_in', 'ssm_lam_re', 'ssm_lam_im', 'ssm_log_step', 'ssm_b_re', 'ssm_b_im', 'ssm_c_re', 'ssm_c_im', 'ssm_d', 'ssm_w_glu', 'ssm_b_glu', 'ssm_w_out', 'loss_target', 'm_c_ctx', 'm_ada_w', 'm_ada_b', 'm_ln_g', 'm_ln_b', 'm_conv_w_in', 'm_conv_w', 'm_conv_w_out', 'm_ssm_w_in', 'm_ssm_lam_re', 'm_ssm_lam_im', 'm_ssm_log_step', 'm_ssm_b_re', 'm_ssm_b_im', 'm_ssm_c_re', 'm_ssm_c_im', 'm_ssm_d', 'm_ssm_w_glu', 'm_ssm_b_glu', 'm_ssm_w_out', 'v_c_ctx', 'v_ada_w', 'v_ada_b', 'v_ln_g', 'v_ln_b', 'v_conv_w_in', 'v_conv_w', 'v_conv_w_out', 'v_ssm_w_in', 'v_ssm_lam_re', 'v_ssm_lam_im', 'v_ssm_log_step', 'v_ssm_b_re', 'v_ssm_b_im', 'v_ssm_c_re', 'v_ssm_c_im', 'v_ssm_d', 'v_ssm_w_glu', 'v_ssm_b_glu', 'v_ssm_w_out']
TWIN_OUTPUTS = ['loss', 'grad_x', 'grad_c_ctx', 'grad_ada_w', 'grad_ada_b', 'grad_ln_g', 'grad_ln_b', 'grad_conv_w_in', 'grad_conv_w', 'grad_conv_w_out', 'grad_ssm_w_in', 'grad_ssm_lam_re', 'grad_ssm_lam_im', 'grad_ssm_log_step', 'grad_ssm_b_re', 'grad_ssm_b_im', 'grad_ssm_c_re', 'grad_ssm_c_im', 'grad_ssm_d', 'grad_ssm_w_glu', 'grad_ssm_b_glu', 'grad_ssm_w_out', 'delta_c_ctx', 'delta_ada_w', 'delta_ada_b', 'delta_ln_g', 'delta_ln_b', 'delta_conv_w_in', 'delta_conv_w', 'delta_conv_w_out', 'delta_ssm_w_in', 'delta_ssm_lam_re', 'delta_ssm_lam_im', 'delta_ssm_log_step', 'delta_ssm_b_re', 'delta_ssm_b_im', 'delta_ssm_c_re', 'delta_ssm_c_im', 'delta_ssm_d', 'delta_ssm_w_glu', 'delta_ssm_b_glu', 'delta_ssm_w_out', 'new_m_c_ctx', 'new_m_ada_w', 'new_m_ada_b', 'new_m_ln_g', 'new_m_ln_b', 'new_m_conv_w_in', 'new_m_conv_w', 'new_m_conv_w_out', 'new_m_ssm_w_in', 'new_m_ssm_lam_re', 'new_m_ssm_lam_im', 'new_m_ssm_log_step', 'new_m_ssm_b_re', 'new_m_ssm_b_im', 'new_m_ssm_c_re', 'new_m_ssm_c_im', 'new_m_ssm_d', 'new_m_ssm_w_glu', 'new_m_ssm_b_glu', 'new_m_ssm_w_out', 'new_v_c_ctx', 'new_v_ada_w', 'new_v_ada_b', 'new_v_ln_g', 'new_v_ln_b', 'new_v_conv_w_in', 'new_v_conv_w', 'new_v_conv_w_out', 'new_v_ssm_w_in', 'new_v_ssm_lam_re', 'new_v_ssm_lam_im', 'new_v_ssm_log_step', 'new_v_ssm_b_re', 'new_v_ssm_b_im', 'new_v_ssm_c_re', 'new_v_ssm_c_im', 'new_v_ssm_d', 'new_v_ssm_w_glu', 'new_v_ssm_b_glu', 'new_v_ssm_w_out']
TWIN_LEAF_KINDS = {'loss': 'loss', 'grad_x': 'grad_x', 'grad_c_ctx': 'grad_w', 'grad_ada_w': 'grad_w', 'grad_ada_b': 'grad_w', 'grad_ln_g': 'grad_w', 'grad_ln_b': 'grad_w', 'grad_conv_w_in': 'grad_w', 'grad_conv_w': 'grad_w', 'grad_conv_w_out': 'grad_w', 'grad_ssm_w_in': 'grad_w', 'grad_ssm_lam_re': 'grad_w', 'grad_ssm_lam_im': 'grad_w', 'grad_ssm_log_step': 'grad_w', 'grad_ssm_b_re': 'grad_w', 'grad_ssm_b_im': 'grad_w', 'grad_ssm_c_re': 'grad_w', 'grad_ssm_c_im': 'grad_w', 'grad_ssm_d': 'grad_w', 'grad_ssm_w_glu': 'grad_w', 'grad_ssm_b_glu': 'grad_w', 'grad_ssm_w_out': 'grad_w', 'delta_c_ctx': 'delta_w', 'delta_ada_w': 'delta_w', 'delta_ada_b': 'delta_w', 'delta_ln_g': 'delta_w', 'delta_ln_b': 'delta_w', 'delta_conv_w_in': 'delta_w', 'delta_conv_w': 'delta_w', 'delta_conv_w_out': 'delta_w', 'delta_ssm_w_in': 'delta_w', 'delta_ssm_lam_re': 'delta_w', 'delta_ssm_lam_im': 'delta_w', 'delta_ssm_log_step': 'delta_w', 'delta_ssm_b_re': 'delta_w', 'delta_ssm_b_im': 'delta_w', 'delta_ssm_c_re': 'delta_w', 'delta_ssm_c_im': 'delta_w', 'delta_ssm_d': 'delta_w', 'delta_ssm_w_glu': 'delta_w', 'delta_ssm_b_glu': 'delta_w', 'delta_ssm_w_out': 'delta_w', 'new_m_c_ctx': 'new_m', 'new_m_ada_w': 'new_m', 'new_m_ada_b': 'new_m', 'new_m_ln_g': 'new_m', 'new_m_ln_b': 'new_m', 'new_m_conv_w_in': 'new_m', 'new_m_conv_w': 'new_m', 'new_m_conv_w_out': 'new_m', 'new_m_ssm_w_in': 'new_m', 'new_m_ssm_lam_re': 'new_m', 'new_m_ssm_lam_im': 'new_m', 'new_m_ssm_log_step': 'new_m', 'new_m_ssm_b_re': 'new_m', 'new_m_ssm_b_im': 'new_m', 'new_m_ssm_c_re': 'new_m', 'new_m_ssm_c_im': 'new_m', 'new_m_ssm_d': 'new_m', 'new_m_ssm_w_glu': 'new_m', 'new_m_ssm_b_glu': 'new_m', 'new_m_ssm_w_out': 'new_m', 'new_v_c_ctx': 'new_v', 'new_v_ada_w': 'new_v', 'new_v_ada_b': 'new_v', 'new_v_ln_g': 'new_v', 'new_v_ln_b': 'new_v', 'new_v_conv_w_in': 'new_v', 'new_v_conv_w': 'new_v', 'new_v_conv_w_out': 'new_v', 'new_v_ssm_w_in': 'new_v', 'new_v_ssm_lam_re': 'new_v', 'new_v_ssm_lam_im': 'new_v', 'new_v_ssm_log_step': 'new_v', 'new_v_ssm_b_re': 'new_v', 'new_v_ssm_b_im': 'new_v', 'new_v_ssm_c_re': 'new_v', 'new_v_ssm_c_im': 'new_v', 'new_v_ssm_d': 'new_v', 'new_v_ssm_w_glu': 'new_v', 'new_v_ssm_b_glu': 'new_v', 'new_v_ssm_w_out': 'new_v'}


def _forward(args):
    return _fwd_reference(*[args[k] for k in FWD_PARAMS])


def _output_shape():
    def fwd():
        inp = _fwd_setup_inputs(0)
        return _fwd_reference(*[inp[k] for k in FWD_PARAMS])
    out = _jax.eval_shape(fwd)
    return out.shape, out.dtype

N_MICROBATCH = 1
ADAM_LR = 0.001
ADAM_B1 = 0.9
ADAM_B2 = 0.999
ADAM_EPS = 1e-08
ADAM_WD = 0.01
ADAM_STEP = 10
PER_EXAMPLE_BATCH_AXIS = {'x': 0, 'c': 0, 'ctx': 0, 'loss_target': 0}
SHARED_INPUTS = []
_WEIGHT_DTYPES = {'c_ctx': _jnp.float32, 'ada_w': _jnp.float32, 'ada_b': _jnp.float32, 'ln_g': _jnp.float32, 'ln_b': _jnp.float32, 'conv_w_in': _jnp.float32, 'conv_w': _jnp.float32, 'conv_w_out': _jnp.float32, 'ssm_w_in': _jnp.float32, 'ssm_lam_re': _jnp.float32, 'ssm_lam_im': _jnp.float32, 'ssm_log_step': _jnp.float32, 'ssm_b_re': _jnp.float32, 'ssm_b_im': _jnp.float32, 'ssm_c_re': _jnp.float32, 'ssm_c_im': _jnp.float32, 'ssm_d': _jnp.float32, 'ssm_w_glu': _jnp.float32, 'ssm_b_glu': _jnp.float32, 'ssm_w_out': _jnp.float32}
MOMENT_SCALE = {'c_ctx': 2.272789e-03, 'ada_w': 1.063205e-01, 'ada_b': 1.971440e-01, 'ln_g': 9.087309e+01, 'ln_b': 2.158541e+00, 'conv_w_in': 1.011239e-01, 'conv_w': 1.045331e-01, 'conv_w_out': 2.864180e-01, 'ssm_w_in': 2.222736e-02, 'ssm_lam_re': 2.600570e-03, 'ssm_lam_im': 3.199035e-03, 'ssm_log_step': 1.288140e+00, 'ssm_b_re': 1.778036e-03, 'ssm_b_im': 1.619161e-03, 'ssm_c_re': 2.272913e-03, 'ssm_c_im': 2.525693e-03, 'ssm_d': 2.551597e-02, 'ssm_w_glu': 7.686185e-03, 'ssm_b_glu': 8.692718e-03, 'ssm_w_out': 6.235746e-02}


def _to_microbatches(a, axis):
    t = _jnp.moveaxis(a, axis, 0)
    t = t.reshape((N_MICROBATCH, t.shape[0] // N_MICROBATCH) + t.shape[1:])
    return _jnp.moveaxis(t, 1, axis + 1)


def setup_inputs(seed: int = 0) -> dict:
    inp = _fwd_setup_inputs(seed)
    key = _jax.random.fold_in(_jax.random.key(seed), 7919)
    shape, _ = _output_shape()
    out = dict(inp)
    out["loss_target"] = _jax.random.normal(_jax.random.fold_in(key, 0), shape, _jnp.float32)
    for i, name in enumerate(TWIN_WEIGHTS):
        w = inp[name].astype(_jnp.float32)
        if MOMENT_SCALE is None:
            s = _jnp.sqrt(_jnp.mean(_jnp.square(w)) + 1e-30)
        else:
            s = MOMENT_SCALE[name]
        km, kv = _jax.random.split(_jax.random.fold_in(key, i + 1))
        out[name] = w
        out["m_" + name] = s * _jax.random.normal(km, w.shape, _jnp.float32)
        out["v_" + name] = (s * s) * _jax.random.uniform(kv, w.shape, _jnp.float32, 0.5, 1.5)
    if N_MICROBATCH > 1:
        for name, axis in PER_EXAMPLE_BATCH_AXIS.items():
            out[name] = _to_microbatches(out[name], axis)
    return {'x': out['x'], 'c': out['c'], 'ctx': out['ctx'], 'c_ctx': out['c_ctx'], 'ada_w': out['ada_w'], 'ada_b': out['ada_b'], 'ln_g': out['ln_g'], 'ln_b': out['ln_b'], 'conv_w_in': out['conv_w_in'], 'conv_w': out['conv_w'], 'conv_w_out': out['conv_w_out'], 'ssm_w_in': out['ssm_w_in'], 'ssm_lam_re': out['ssm_lam_re'], 'ssm_lam_im': out['ssm_lam_im'], 'ssm_log_step': out['ssm_log_step'], 'ssm_b_re': out['ssm_b_re'], 'ssm_b_im': out['ssm_b_im'], 'ssm_c_re': out['ssm_c_re'], 'ssm_c_im': out['ssm_c_im'], 'ssm_d': out['ssm_d'], 'ssm_w_glu': out['ssm_w_glu'], 'ssm_b_glu': out['ssm_b_glu'], 'ssm_w_out': out['ssm_w_out'], 'loss_target': out['loss_target'], 'm_c_ctx': out['m_c_ctx'], 'm_ada_w': out['m_ada_w'], 'm_ada_b': out['m_ada_b'], 'm_ln_g': out['m_ln_g'], 'm_ln_b': out['m_ln_b'], 'm_conv_w_in': out['m_conv_w_in'], 'm_conv_w': out['m_conv_w'], 'm_conv_w_out': out['m_conv_w_out'], 'm_ssm_w_in': out['m_ssm_w_in'], 'm_ssm_lam_re': out['m_ssm_lam_re'], 'm_ssm_lam_im': out['m_ssm_lam_im'], 'm_ssm_log_step': out['m_ssm_log_step'], 'm_ssm_b_re': out['m_ssm_b_re'], 'm_ssm_b_im': out['m_ssm_b_im'], 'm_ssm_c_re': out['m_ssm_c_re'], 'm_ssm_c_im': out['m_ssm_c_im'], 'm_ssm_d': out['m_ssm_d'], 'm_ssm_w_glu': out['m_ssm_w_glu'], 'm_ssm_b_glu': out['m_ssm_b_glu'], 'm_ssm_w_out': out['m_ssm_w_out'], 'v_c_ctx': out['v_c_ctx'], 'v_ada_w': out['v_ada_w'], 'v_ada_b': out['v_ada_b'], 'v_ln_g': out['v_ln_g'], 'v_ln_b': out['v_ln_b'], 'v_conv_w_in': out['v_conv_w_in'], 'v_conv_w': out['v_conv_w'], 'v_conv_w_out': out['v_conv_w_out'], 'v_ssm_w_in': out['v_ssm_w_in'], 'v_ssm_lam_re': out['v_ssm_lam_re'], 'v_ssm_lam_im': out['v_ssm_lam_im'], 'v_ssm_log_step': out['v_ssm_log_step'], 'v_ssm_b_re': out['v_ssm_b_re'], 'v_ssm_b_im': out['v_ssm_b_im'], 'v_ssm_c_re': out['v_ssm_c_re'], 'v_ssm_c_im': out['v_ssm_c_im'], 'v_ssm_d': out['v_ssm_d'], 'v_ssm_w_glu': out['v_ssm_w_glu'], 'v_ssm_b_glu': out['v_ssm_b_glu'], 'v_ssm_w_out': out['v_ssm_w_out']}


def _loss(weights, diff, rest, loss_target):
    with _jax.named_scope("forward"):
        args = {**rest, TWIN_DIFF_INPUT: diff, **{k: w.astype(_WEIGHT_DTYPES[k]) for k, w in weights.items()}}
        y = _forward(args)
    with _jax.named_scope("loss_head"):
        err = _jnp.square(y.astype(_jnp.float32) - loss_target)
        return 0.5 * _jnp.sum(_jnp.mean(err, axis=-1)) if err.ndim else 0.5 * err


def _adamw(w, g, m, v):
    m = ADAM_B1 * m + (1.0 - ADAM_B1) * g
    v = ADAM_B2 * v + (1.0 - ADAM_B2) * _jnp.square(g)
    m_hat = m / (1.0 - ADAM_B1 ** ADAM_STEP)
    v_hat = v / (1.0 - ADAM_B2 ** ADAM_STEP)
    delta = -ADAM_LR * (m_hat / (_jnp.sqrt(v_hat) + ADAM_EPS) + ADAM_WD * w)
    return delta, m, v


def reference(x, c, ctx, c_ctx, ada_w, ada_b, ln_g, ln_b, conv_w_in, conv_w, conv_w_out, ssm_w_in, ssm_lam_re, ssm_lam_im, ssm_log_step, ssm_b_re, ssm_b_im, ssm_c_re, ssm_c_im, ssm_d, ssm_w_glu, ssm_b_glu, ssm_w_out, loss_target, m_c_ctx, m_ada_w, m_ada_b, m_ln_g, m_ln_b, m_conv_w_in, m_conv_w, m_conv_w_out, m_ssm_w_in, m_ssm_lam_re, m_ssm_lam_im, m_ssm_log_step, m_ssm_b_re, m_ssm_b_im, m_ssm_c_re, m_ssm_c_im, m_ssm_d, m_ssm_w_glu, m_ssm_b_glu, m_ssm_w_out, v_c_ctx, v_ada_w, v_ada_b, v_ln_g, v_ln_b, v_conv_w_in, v_conv_w, v_conv_w_out, v_ssm_w_in, v_ssm_lam_re, v_ssm_lam_im, v_ssm_log_step, v_ssm_b_re, v_ssm_b_im, v_ssm_c_re, v_ssm_c_im, v_ssm_d, v_ssm_w_glu, v_ssm_b_glu, v_ssm_w_out):
    given = dict(x=x, c=c, ctx=ctx, c_ctx=c_ctx, ada_w=ada_w, ada_b=ada_b, ln_g=ln_g, ln_b=ln_b, conv_w_in=conv_w_in, conv_w=conv_w, conv_w_out=conv_w_out, ssm_w_in=ssm_w_in, ssm_lam_re=ssm_lam_re, ssm_lam_im=ssm_lam_im, ssm_log_step=ssm_log_step, ssm_b_re=ssm_b_re, ssm_b_im=ssm_b_im, ssm_c_re=ssm_c_re, ssm_c_im=ssm_c_im, ssm_d=ssm_d, ssm_w_glu=ssm_w_glu, ssm_b_glu=ssm_b_glu, ssm_w_out=ssm_w_out, loss_target=loss_target, m_c_ctx=m_c_ctx, m_ada_w=m_ada_w, m_ada_b=m_ada_b, m_ln_g=m_ln_g, m_ln_b=m_ln_b, m_conv_w_in=m_conv_w_in, m_conv_w=m_conv_w, m_conv_w_out=m_conv_w_out, m_ssm_w_in=m_ssm_w_in, m_ssm_lam_re=m_ssm_lam_re, m_ssm_lam_im=m_ssm_lam_im, m_ssm_log_step=m_ssm_log_step, m_ssm_b_re=m_ssm_b_re, m_ssm_b_im=m_ssm_b_im, m_ssm_c_re=m_ssm_c_re, m_ssm_c_im=m_ssm_c_im, m_ssm_d=m_ssm_d, m_ssm_w_glu=m_ssm_w_glu, m_ssm_b_glu=m_ssm_b_glu, m_ssm_w_out=m_ssm_w_out, v_c_ctx=v_c_ctx, v_ada_w=v_ada_w, v_ada_b=v_ada_b, v_ln_g=v_ln_g, v_ln_b=v_ln_b, v_conv_w_in=v_conv_w_in, v_conv_w=v_conv_w, v_conv_w_out=v_conv_w_out, v_ssm_w_in=v_ssm_w_in, v_ssm_lam_re=v_ssm_lam_re, v_ssm_lam_im=v_ssm_lam_im, v_ssm_log_step=v_ssm_log_step, v_ssm_b_re=v_ssm_b_re, v_ssm_b_im=v_ssm_b_im, v_ssm_c_re=v_ssm_c_re, v_ssm_c_im=v_ssm_c_im, v_ssm_d=v_ssm_d, v_ssm_w_glu=v_ssm_w_glu, v_ssm_b_glu=v_ssm_b_glu, v_ssm_w_out=v_ssm_w_out)
    weights = {n: given[n] for n in TWIN_WEIGHTS}
    shared = {n: given[n] for n in SHARED_INPUTS}
    per_example = {n: given[n] for n in ['x', 'c', 'ctx']}
    grad_fn = _jax.value_and_grad(_loss, argnums=(0, 1))

    def one_microbatch(ex, loss_target):
        ex = dict(ex)
        diff = ex.pop(TWIN_DIFF_INPUT)
        return grad_fn(weights, diff, {**shared, **ex}, loss_target)

    if N_MICROBATCH == 1:
        loss, (grad_w, grad_x) = one_microbatch(per_example, given["loss_target"])
    else:
        def body(carry, xs):
            loss_sum, grad_sum = carry
            l_k, (gw_k, gx_k) = one_microbatch(xs[0], xs[1])
            with _jax.named_scope("update"):
                return (loss_sum + l_k, _jax.tree.map(_jnp.add, grad_sum, gw_k)), gx_k

        init = (_jnp.zeros((), _jnp.float32), _jax.tree.map(_jnp.zeros_like, weights))
        (loss, grad_w), grad_x = _jax.lax.scan(body, init, (per_example, given["loss_target"]))
    with _jax.named_scope("update"):
        delta_w, new_m, new_v = {}, {}, {}
        for n in TWIN_WEIGHTS:
            delta_w[n], new_m[n], new_v[n] = _adamw(weights[n], grad_w[n], given["m_" + n], given["v_" + n])
    return (loss, grad_x, *[grad_w[n] for n in TWIN_WEIGHTS], *[delta_w[n] for n in TWIN_WEIGHTS],
            *[new_m[n] for n in TWIN_WEIGHTS], *[new_v[n] for n in TWIN_WEIGHTS])
```

```python
import math

import jax
import jax.numpy as jnp
from jax import lax
from jax.experimental import pallas as pl
from jax.experimental.pallas import tpu as pltpu

F32 = jnp.float32
BF = jnp.bfloat16
MESH = pl.DeviceIdType.MESH
N_DEV = 8

GRID_W = 64
CHUNK = 16
S5_P = 16
S5_N = 64
LANE_BLOCK = 128
GROUPS_PER_BLOCK = LANE_BLOCK // S5_P
BCR_W = CHUNK * LANE_BLOCK
ZL_W = 2 * 2 * GROUPS_PER_BLOCK * S5_N
ZH = ZL_W // 4
LN_EPS = 1e-5
DN_ALPHA = 4.0 ** 0.25
ADAM_LR, ADAM_B1, ADAM_B2, ADAM_EPS, ADAM_WD, ADAM_STEP = 1e-3, 0.9, 0.999, 1e-8, 0.01, 10
GELU_C0 = math.sqrt(2.0 / math.pi)
GELU_C1 = 0.044715
VMEM_MB = 52

ANY = pl.BlockSpec(memory_space=pl.ANY)


def _cparams():
    return pltpu.CompilerParams(vmem_limit_bytes=VMEM_MB << 20)


def _dot(a, b):
    return jnp.dot(a, b, preferred_element_type=F32)


def _dot_nt(a, b):
    return lax.dot_general(a, b, (((1,), (1,)), ((), ())), preferred_element_type=F32)


def _dot_tn(a, b):
    return lax.dot_general(a, b, (((0,), (0,)), ((), ())), preferred_element_type=F32)


def _sigmoid(x):
    return 1.0 / (1.0 + jnp.exp(-x))


def _gelu_parts(y):
    th = jnp.tanh(GELU_C0 * (y + GELU_C1 * y * y * y))
    g = 0.5 * y * (1.0 + th)
    dg = 0.5 * (1.0 + th) + 0.5 * y * (1.0 - th * th) * GELU_C0 * (1.0 + 3.0 * GELU_C1 * y * y)
    return g, dg


def _full(shape):
    nd = len(shape)
    return pl.BlockSpec(shape, lambda *_: (0,) * nd)


def _mesh_pos():
    x, y, c = lax.axis_index("x"), lax.axis_index("y"), lax.axis_index("c")
    return x, y, c


def _peer(pos, k):
    x, y, c = pos
    px = 1 - x if (k >> 2) & 1 else x
    py = 1 - y if (k >> 1) & 1 else y
    pc = 1 - c if k & 1 else c
    return (px, py, pc), 4 * px + 2 * py + pc


def _shard_at(ref, axis, idx, n):
    if axis == 0:
        return ref.at[pl.ds(idx * n, n)]
    return ref.at[:, pl.ds(idx * n, n)]


def _all_gather(shards, axes, name):
    n = len(shards)
    out_shape = []
    for s, ax in zip(shards, axes):
        shp = list(s.shape)
        shp[ax] *= N_DEV
        out_shape.append(jax.ShapeDtypeStruct(tuple(shp), s.dtype))

    def body(*refs):
        ins, outs = refs[:n], refs[n:2 * n]
        send_sems, recv_sems, local_sems = refs[2 * n:]
        pos = _mesh_pos()
        me = 4 * pos[0] + 2 * pos[1] + pos[2]
        local, sends = [], []
        for i in range(n):
            size = ins[i].shape[axes[i]]
            cp = pltpu.make_async_copy(ins[i], _shard_at(outs[i], axes[i], me, size), local_sems.at[i])
            cp.start()
            local.append(cp)
            for k in range(1, N_DEV):
                peer, _ = _peer(pos, k)
                cp = pltpu.make_async_remote_copy(
                    src_ref=ins[i], dst_ref=_shard_at(outs[i], axes[i], me, size),
                    send_sem=send_sems.at[i, k - 1], recv_sem=recv_sems.at[i, k - 1],
                    device_id=peer, device_id_type=MESH)
                cp.start()
                sends.append(cp)
        for i in range(n):
            size = ins[i].shape[axes[i]]
            for k in range(1, N_DEV):
                peer, pidx = _peer(pos, k)
                pltpu.make_async_remote_copy(
                    src_ref=ins[i], dst_ref=_shard_at(outs[i], axes[i], pidx, size),
                    send_sem=send_sems.at[i, k - 1], recv_sem=recv_sems.at[i, k - 1],
                    device_id=peer, device_id_type=MESH).wait_recv()
        for cp in sends:
            cp.wait_send()
        for cp in local:
            cp.wait()

    return pl.pallas_call(
        body, name=name, out_shape=out_shape, in_specs=[ANY] * n, out_specs=[ANY] * n,
        scratch_shapes=[pltpu.SemaphoreType.DMA((n, N_DEV - 1)), pltpu.SemaphoreType.DMA((n, N_DEV - 1)),
                        pltpu.SemaphoreType.DMA((n,))],
    )(*shards)


def _all_to_all(parts, axes, name):
    n = len(parts)
    out_shape = []
    for s, ax in zip(parts, axes):
        shp = list(s.shape)
        shp[ax] //= N_DEV
        out_shape.append(jax.ShapeDtypeStruct((N_DEV, *shp), s.dtype))

    def body(*refs):
        ins, outs = refs[:n], refs[n:2 * n]
        send_sems, recv_sems, local_sems = refs[2 * n:]
        pos = _mesh_pos()
        me = 4 * pos[0] + 2 * pos[1] + pos[2]
        local, sends = [], []
        for i in range(n):
            size = ins[i].shape[axes[i]] // N_DEV
            cp = pltpu.make_async_copy(_shard_at(ins[i], axes[i], me, size), outs[i].at[me], local_sems.at[i])
            cp.start()
            local.append(cp)
            for k in range(1, N_DEV):
                peer, pidx = _peer(pos, k)
                cp = pltpu.make_async_remote_copy(
                    src_ref=_shard_at(ins[i], axes[i], pidx, size), dst_ref=outs[i].at[me],
                    send_sem=send_sems.at[i, k - 1], recv_sem=recv_sems.at[i, k - 1],
                    device_id=peer, device_id_type=MESH)
                cp.start()
                sends.append(cp)
        for i in range(n):
            size = ins[i].shape[axes[i]] // N_DEV
            for k in range(1, N_DEV):
                peer, pidx = _peer(pos, k)
                pltpu.make_async_remote_copy(
                    src_ref=_shard_at(ins[i], axes[i], pidx, size), dst_ref=outs[i].at[pidx],
                    send_sem=send_sems.at[i, k - 1], recv_sem=recv_sems.at[i, k - 1],
                    device_id=peer, device_id_type=MESH).wait_recv()
        for cp in sends:
            cp.wait_send()
        for cp in local:
            cp.wait()

    return pl.pallas_call(
        body, name=name, out_shape=out_shape, in_specs=[ANY] * n, out_specs=[ANY] * n,
        scratch_shapes=[pltpu.SemaphoreType.DMA((n, N_DEV - 1)), pltpu.SemaphoreType.DMA((n, N_DEV - 1)),
                        pltpu.SemaphoreType.DMA((n,))],
    )(*parts)


def _ada_fwd(c8, ada_w, ada_b):
    nl, d, d3 = ada_w.shape

    def body(c_ref, w_ref, b_ref, o_ref):
        cv = c_ref[...]
        s = (cv * _sigmoid(cv)).astype(BF)
        o_ref[0] = _dot(s, w_ref[0]) + b_ref[0]

    return pl.pallas_call(
        body, name="ada_fwd", grid=(nl,),
        in_specs=[_full((8, d)), pl.BlockSpec((1, d, d3), lambda l: (l, 0, 0)), pl.BlockSpec((1, 1, d3), lambda l: (l, 0, 0))],
        out_specs=pl.BlockSpec((1, 8, d3), lambda l: (l, 0, 0)),
        out_shape=jax.ShapeDtypeStruct((nl, 8, d3), F32), compiler_params=_cparams(),
    )(c8, ada_w, ada_b.reshape(nl, 1, d3))


def _ada_bwd(c8, ada_w, dm8):
    nl, d, d3 = ada_w.shape

    def body(c_ref, w_ref, dm_ref, dw_ref, dc_ref):
        cv = c_ref[...]
        sg = _sigmoid(cv)
        s = (cv * sg).astype(BF)
        dm = dm_ref[0].astype(BF)
        dw_ref[0] = _dot_tn(s, dm).astype(BF)
        dc_ref[0] = _dot_nt(dm, w_ref[0]) * (sg * (1.0 + cv * (1.0 - sg)))

    return pl.pallas_call(
        body, name="ada_bwd", grid=(nl,),
        in_specs=[_full((8, d)), pl.BlockSpec((1, d, d3), lambda l: (l, 0, 0)), pl.BlockSpec((1, 8, d3), lambda l: (l, 0, 0))],
        out_specs=[pl.BlockSpec((1, d, d3), lambda l: (l, 0, 0)), pl.BlockSpec((1, 8, d), lambda l: (l, 0, 0))],
        out_shape=[jax.ShapeDtypeStruct((nl, d, d3), BF), jax.ShapeDtypeStruct((nl, 8, d), F32)],
        compiler_params=_cparams(),
    )(c8, ada_w, dm8)


def _sum_partials(stack):
    _, r, c = stack.shape

    def body(s_ref, o_ref):
        acc = s_ref[0]
        for p in range(1, N_DEV):
            acc = acc + s_ref[p]
        o_ref[...] = acc

    return pl.pallas_call(body, name="sum_partials", out_shape=jax.ShapeDtypeStruct((r, c), F32),
                          in_specs=[_full(stack.shape)], out_specs=_full((r, c)), grid=(1,),
                          compiler_params=_cparams())(stack)


def _adamw(gstack, w, m, v, name):
    p, r, c = gstack.shape
    tr = r
    for cand in (512 if c <= 256 else 256, 128, 64, 32, 16, 8):
        if r % cand == 0 and r > cand:
            tr = cand
            break
    bc1 = 1.0 - ADAM_B1 ** ADAM_STEP
    bc2 = 1.0 - ADAM_B2 ** ADAM_STEP

    def body(g_ref, w_ref, m_ref, v_ref, go_ref, d_ref, mo_ref, vo_ref):
        g = g_ref[0].astype(F32)
        for q in range(1, p):
            g = g + g_ref[q].astype(F32)
        mn = ADAM_B1 * m_ref[...] + (1.0 - ADAM_B1) * g
        vn = ADAM_B2 * v_ref[...] + (1.0 - ADAM_B2) * (g * g)
        go_ref[...] = g
        mo_ref[...] = mn
        vo_ref[...] = vn
        d_ref[...] = -ADAM_LR * ((mn / bc1) / (jnp.sqrt(vn / bc2) + ADAM_EPS) + ADAM_WD * w_ref[...])

    row = pl.BlockSpec((tr, c), lambda i: (i, 0))
    sds = jax.ShapeDtypeStruct((r, c), F32)
    return pl.pallas_call(
        body, name=name, grid=(r // tr,),
        in_specs=[pl.BlockSpec((p, tr, c), lambda i: (0, i, 0)), row, row, row],
        out_specs=[row, row, row, row], out_shape=[sds, sds, sds, sds], compiler_params=_cparams(),
    )(gstack, w, m, v)


def _lat_or_ctx_specs(tm, d, nl, grid_rank, row_axis):
    def lat(*ids):
        return (jnp.minimum(ids[row_axis], nl - 1), 0)

    def ctx(*ids):
        return (jnp.maximum(ids[row_axis] - nl, 0), 0)

    return pl.BlockSpec((tm, d), lat), pl.BlockSpec((tm, d), ctx)


def _sel_row(ref, is_ctx):
    return jnp.where(is_ctx, ref[1:2, :], ref[0:1, :])


def _inproj0(x, ctx, a2, b2, w, tm):
    l, d = x.shape
    nl, nc = l // tm, ctx.shape[0] // tm
    e = w.shape[1] // 4
    half = e // 2

    def body(x_ref, c_ref, a_ref, b_ref, w_ref, o_ref):
        is_ctx = pl.program_id(1) >= nl
        xv = jnp.where(is_ctx, c_ref[...], x_ref[...])
        h = (xv * _sel_row(a_ref, is_ctx) + _sel_row(b_ref, is_ctx)).astype(BF)
        r = _dot(h, w_ref[...])
        o_ref[0, 0] = r[:, :half].astype(BF)
        o_ref[0, 1] = r[:, half:].astype(BF)

    lat, cx = _lat_or_ctx_specs(tm, d, nl, 2, 1)
    return pl.pallas_call(
        body, name="l0_inproj", grid=(4, nl + nc),
        in_specs=[lat, cx, _full((2, d)), _full((2, d)), pl.BlockSpec((d, e), lambda n, i: (0, n))],
        out_specs=pl.BlockSpec((1, 2, tm, half), lambda n, i: (n, 0, i, 0)),
        out_shape=jax.ShapeDtypeStruct((4, 2, l + ctx.shape[0], half), BF), compiler_params=_cparams(),
    )(x, ctx, a2, b2, w)


def _conv_taps(u, w_up, w_mid, w_dn, pos, rl, tm):
    up = jnp.where(pos == 0, 0.0, pltpu.roll(u, 1, 0))
    dn = jnp.where(pos == rl - 1, 0.0, pltpu.roll(u, tm - 1, 0))
    return w_up * up + w_mid * u + w_dn * dn, up, dn


def _conv_halo_specs(tm, tc, nl, lead):
    hb = tm // GRID_W

    def prev(j, i):
        return (0, 1, jnp.maximum(jnp.minimum(i, nl - 1) * hb - 1, 0), j)

    def nxt(j, i):
        return (0, 1, jnp.minimum((jnp.minimum(i, nl - 1) + 1) * hb, nl * hb - 1), j)

    return pl.BlockSpec((lead, 1, GRID_W, tc), prev), pl.BlockSpec((lead, 1, GRID_W, tc), nxt)


def _conv_fwd(p42, cw, nl, tm, tc):
    _, _, r, half = p42.shape
    nt = r // tm

    def body(p_ref, hp_ref, hn_ref, cw_ref, o_ref):
        i = pl.program_id(1)
        is_ctx = i >= nl
        row = lax.broadcasted_iota(jnp.int32, (tm, tc), 0)
        rl = jnp.where(is_ctx, tm, GRID_W)
        pos = jnp.bitwise_and(row, rl - 1)

        def gate(hv, yc):
            bg = p_ref[0, hv].astype(F32)
            z = p_ref[3, hv].astype(F32)
            return (bg * yc * (z * _sigmoid(z))).astype(BF)

        u_h = p_ref[1, 0].astype(F32) * p_ref[2, 0].astype(F32)
        w_h = cw_ref[:, 0, :]
        o_ref[0] = gate(0, _conv_taps(u_h, w_h[0:1], w_h[1:2], w_h[2:3], pos, rl, tm)[0])
        u_v = p_ref[1, 1].astype(F32) * p_ref[2, 1].astype(F32)
        w_v = cw_ref[:, 1, :]

        @pl.when(is_ctx)
        def _():
            o_ref[1] = gate(1, _conv_taps(u_v, w_v[0:1], w_v[1:2], w_v[2:3], pos, rl, tm)[0])

        @pl.when(jnp.logical_not(is_ctx))
        def _():
            up = hp_ref[1, 0].astype(F32) * hp_ref[2, 0].astype(F32) * (i > 0).astype(F32)
            dn = hn_ref[1, 0].astype(F32) * hn_ref[2, 0].astype(F32) * (i < nl - 1).astype(F32)
            ext = jnp.concatenate([up, u_v, dn], axis=0)
            yc = w_v[0:1] * ext[0:tm] + w_v[1:2] * u_v + w_v[2:3] * ext[2 * GRID_W:tm + 2 * GRID_W]
            o_ref[1] = gate(1, yc)

    hp, hn = _conv_halo_specs(tm, tc, nl, 4)
    return pl.pallas_call(
        body, name="l0_conv_fwd", grid=(half // tc, nt),
        in_specs=[pl.BlockSpec((4, 2, tm, tc), lambda j, i: (0, 0, i, j)), hp, hn,
                  pl.BlockSpec((3, 2, tc), lambda j, i: (0, 0, j))],
        out_specs=pl.BlockSpec((2, tm, tc), lambda j, i: (0, i, j)),
        out_shape=jax.ShapeDtypeStruct((2, r, half), BF), compiler_params=_cparams(),
    )(p42, p42, p42, cw)


def _outproj_ln0(q3, w_out, x, ctx, gt2, tm):
    l, d = x.shape
    nl, nc = l // tm, ctx.shape[0] // tm
    _, r, half = q3.shape

    def body(q_ref, w_hbm, x_ref, c_ref, g_ref, xh_ref, rs_ref, fx_ref, w_ref):
        i = pl.program_id(0)

        @pl.when(i == 0)
        def _():
            pltpu.sync_copy(w_hbm, w_ref)

        is_ctx = i >= nl
        fx = _dot(q_ref[0], w_ref[:half, :]) + _dot(q_ref[1], w_ref[half:, :])
        xv = jnp.where(is_ctx, c_ref[...], x_ref[...])
        rr = DN_ALPHA * xv + _sel_row(g_ref, is_ctx) * fx
        mu = jnp.mean(rr, axis=-1, keepdims=True)
        cen = rr - mu
        rstd = lax.rsqrt(jnp.mean(cen * cen, axis=-1, keepdims=True) + LN_EPS)
        xh_ref[...] = cen * rstd
        rs_ref[...] = jnp.broadcast_to(rstd, (tm, 128))
        fx_ref[...] = fx.astype(BF)

    lat, cx = _lat_or_ctx_specs(tm, d, nl, 1, 0)
    return pl.pallas_call(
        body, name="l0_outproj_ln", grid=(nl + nc,),
        in_specs=[pl.BlockSpec((2, tm, half), lambda i: (0, i, 0)), ANY, lat, cx, _full((2, d))],
        out_specs=[pl.BlockSpec((tm, d), lambda i: (i, 0)), pl.BlockSpec((tm, 128), lambda i: (i, 0)),
                   pl.BlockSpec((tm, d), lambda i: (i, 0))],
        out_shape=[jax.ShapeDtypeStruct((r, d), F32), jax.ShapeDtypeStruct((r, 128), F32), jax.ShapeDtypeStruct((r, d), BF)],
        scratch_shapes=[pltpu.VMEM(w_out.shape, BF)], compiler_params=_cparams(),
    )(q3, w_out, x, ctx, gt2)


def _bwd_outproj0(dr_l, dr_c, gt2, w_out, fx, tm):
    l, d = dr_l.shape
    nl, nc = l // tm, dr_c.shape[0] // tm
    e = w_out.shape[0]
    half = e // 2
    r = l + dr_c.shape[0]

    def body(dl_ref, dc_ref, g_ref, w_hbm, fx_ref, dq_ref, acc_ref, w_ref):
        i = pl.program_id(0)

        @pl.when(i == 0)
        def _():
            pltpu.sync_copy(w_hbm, w_ref)
            acc_ref[...] = jnp.zeros_like(acc_ref)

        is_ctx = i >= nl
        dr = jnp.where(is_ctx, dc_ref[...], dl_ref[...]).astype(F32)
        dfx = (dr * _sel_row(g_ref, is_ctx)).astype(BF)
        dq_ref[0] = _dot_nt(dfx, w_ref[:half, :]).astype(BF)
        dq_ref[1] = _dot_nt(dfx, w_ref[half:, :]).astype(BF)
        s = jnp.sum(dr * fx_ref[...].astype(F32), axis=0, keepdims=True)
        sel = is_ctx.astype(F32)
        acc_ref[0:1, :] += s * (1.0 - sel)
        acc_ref[1:2, :] += s * sel

    lat, cx = _lat_or_ctx_specs(tm, d, nl, 1, 0)
    return pl.pallas_call(
        body, name="l0_bwd_outproj", grid=(nl + nc,),
        in_specs=[lat, cx, _full((2, d)), ANY, pl.BlockSpec((tm, d), lambda i: (i, 0))],
        out_specs=[pl.BlockSpec((2, tm, half), lambda i: (0, i, 0)), _full((8, d))],
        out_shape=[jax.ShapeDtypeStruct((2, r, half), BF), jax.ShapeDtypeStruct((8, d), F32)],
        scratch_shapes=[pltpu.VMEM(w_out.shape, BF)], compiler_params=_cparams(),
    )(dr_l, dr_c, gt2, w_out, fx)


def _conv_bwd(dq3, p42, cw, nl, tm, tc):
    _, _, r, half = p42.shape
    nt = r // tm

    def body(dq_ref, dqp_ref, dqn_ref, p_ref, hp_ref, hn_ref, cw_ref, dp_ref, dw_ref):
        i = pl.program_id(1)
        is_ctx = i >= nl

        @pl.when(i == 0)
        def _():
            dw_ref[...] = jnp.zeros_like(dw_ref)

        row = lax.broadcasted_iota(jnp.int32, (tm, tc), 0)
        rl = jnp.where(is_ctx, tm, GRID_W)
        pos = jnp.bitwise_and(row, rl - 1)

        def pieces(dq, bg, z):
            sz = _sigmoid(z)
            sil = z * sz
            return dq * bg * sil, dq * sil, dq * bg * (sz * (1.0 + z * (1.0 - sz)))

        def seq_half(hv):
            bg, cg = p_ref[0, hv].astype(F32), p_ref[1, hv].astype(F32)
            v, z = p_ref[2, hv].astype(F32), p_ref[3, hv].astype(F32)
            w = cw_ref[:, hv, :]
            u = cg * v
            yc, u_up, u_dn = _conv_taps(u, w[0:1], w[1:2], w[2:3], pos, rl, tm)
            dyc, dbg_f, dz_f = pieces(dq_ref[hv].astype(F32), bg, z)
            du = _conv_taps(dyc, w[2:3], w[1:2], w[0:1], pos, rl, tm)[0]
            dp_ref[0, hv] = (dbg_f * yc).astype(BF)
            dp_ref[1, hv] = (du * v).astype(BF)
            dp_ref[2, hv] = (du * cg).astype(BF)
            dp_ref[3, hv] = (dz_f * yc).astype(BF)
            dw_ref[0:1, hv, :] += jnp.sum(dyc * u_up, axis=0, keepdims=True)
            dw_ref[1:2, hv, :] += jnp.sum(dyc * u, axis=0, keepdims=True)
            dw_ref[2:3, hv, :] += jnp.sum(dyc * u_dn, axis=0, keepdims=True)

        seq_half(0)

        @pl.when(is_ctx)
        def _():
            seq_half(1)

        @pl.when(jnp.logical_not(is_ctx))
        def _():
            bg, cg = p_ref[0, 1].astype(F32), p_ref[1, 1].astype(F32)
            v, z = p_ref[2, 1].astype(F32), p_ref[3, 1].astype(F32)
            w = cw_ref[:, 1, :]
            u = cg * v
            m_up = (i > 0).astype(F32)
            m_dn = (i < nl - 1).astype(F32)

            def halo(h_ref, dqh_ref, msk):
                hb, hc = h_ref[0, 0].astype(F32), h_ref[1, 0].astype(F32)
                hv_, hz = h_ref[2, 0].astype(F32), h_ref[3, 0].astype(F32)
                return hc * hv_ * msk, pieces(dqh_ref[0].astype(F32), hb, hz)[0] * msk

            u_p, dyc_p = halo(hp_ref, dqp_ref, m_up)
            u_n, dyc_n = halo(hn_ref, dqn_ref, m_dn)
            u_ext = jnp.concatenate([u_p, u, u_n], axis=0)
            u_up, u_dn = u_ext[0:tm], u_ext[2 * GRID_W:tm + 2 * GRID_W]
            yc = w[0:1] * u_up + w[1:2] * u + w[2:3] * u_dn
            dyc, dbg_f, dz_f = pieces(dq_ref[1].astype(F32), bg, z)
            d_ext = jnp.concatenate([dyc_p, dyc, dyc_n], axis=0)
            du = w[0:1] * d_ext[2 * GRID_W:tm + 2 * GRID_W] + w[1:2] * dyc + w[2:3] * d_ext[0:tm]
            dp_ref[0, 1] = (dbg_f * yc).astype(BF)
            dp_ref[1, 1] = (du * v).astype(BF)
            dp_ref[2, 1] = (du * cg).astype(BF)
            dp_ref[3, 1] = (dz_f * yc).astype(BF)
            dw_ref[0:1, 1, :] += jnp.sum(dyc * u_up, axis=0, keepdims=True)
            dw_ref[1:2, 1, :] += jnp.sum(dyc * u, axis=0, keepdims=True)
            dw_ref[2:3, 1, :] += jnp.sum(dyc * u_dn, axis=0, keepdims=True)

    hb = tm // GRID_W

    def dq_prev(j, i):
        return (1, jnp.maximum(jnp.minimum(i, nl - 1) * hb - 1, 0), j)

    def dq_next(j, i):
        return (1, jnp.minimum((jnp.minimum(i, nl - 1) + 1) * hb, nl * hb - 1), j)

    hp, hn = _conv_halo_specs(tm, tc, nl, 4)
    return pl.pallas_call(
        body, name="l0_conv_bwd", grid=(half // tc, nt),
        in_specs=[pl.BlockSpec((2, tm, tc), lambda j, i: (0, i, j)),
                  pl.BlockSpec((1, GRID_W, tc), dq_prev), pl.BlockSpec((1, GRID_W, tc), dq_next),
                  pl.BlockSpec((4, 2, tm, tc), lambda j, i: (0, 0, i, j)), hp, hn,
                  pl.BlockSpec((3, 2, tc), lambda j, i: (0, 0, j))],
        out_specs=[pl.BlockSpec((4, 2, tm, tc), lambda j, i: (0, 0, i, j)), pl.BlockSpec((8, 2, tc), lambda j, i: (0, 0, j))],
        out_shape=[jax.ShapeDtypeStruct(p42.shape, BF), jax.ShapeDtypeStruct((8, 2, half), F32)],
        compiler_params=_cparams(),
    )(dq3, dq3, dq3, p42, p42, p42, cw)


def _bwd_inproj0(dp42, w_in, x, ctx, dr_l, dr_c, a2, tm):
    l, d = x.shape
    nl, nc = l // tm, ctx.shape[0] // tm
    e = w_in.shape[1] // 4
    half = e // 2

    def body(dp_ref, w_hbm, x_ref, c_ref, dl_ref, dc_ref, a_ref, gx_ref, acc_ref, w_ref):
        i = pl.program_id(0)

        @pl.when(i == 0)
        def _():
            pltpu.sync_copy(w_hbm, w_ref)
            acc_ref[...] = jnp.zeros_like(acc_ref)

        is_ctx = i >= nl
        dh = jnp.zeros((tm, d), F32)
        for k in range(4):
            for hv in range(2):
                c0 = k * e + hv * half
                dh = dh + _dot_nt(dp_ref[k, hv], w_ref[:, c0:c0 + half])
        xv = jnp.where(is_ctx, c_ref[...], x_ref[...])
        s_sc = jnp.sum(dh * xv, axis=0, keepdims=True)
        s_sh = jnp.sum(dh, axis=0, keepdims=True)
        sel = is_ctx.astype(F32)
        acc_ref[0:1, :] += s_sc * (1.0 - sel)
        acc_ref[1:2, :] += s_sc * sel
        acc_ref[2:3, :] += s_sh * (1.0 - sel)
        acc_ref[3:4, :] += s_sh * sel

        @pl.when(jnp.logical_not(is_ctx))
        def _():
            gx_ref[...] = DN_ALPHA * dl_ref[...].astype(F32) + dh * a_ref[0:1, :]

    lat, cx = _lat_or_ctx_specs(tm, d, nl, 1, 0)
    return pl.pallas_call(
        body, name="l0_bwd_inproj", grid=(nl + nc,),
        in_specs=[pl.BlockSpec((4, 2, tm, half), lambda i: (0, 0, i, 0)), ANY, lat, cx, lat, cx, _full((2, d))],
        out_specs=[pl.BlockSpec((tm, d), lambda i: (jnp.minimum(i, nl - 1), 0)), _full((8, d))],
        out_shape=[jax.ShapeDtypeStruct((l, d), F32), jax.ShapeDtypeStruct((8, d), F32)],
        scratch_shapes=[pltpu.VMEM(w_in.shape, BF)], compiler_params=_cparams(),
    )(dp42, w_in, x, ctx, dr_l, dr_c, a2)


def _dw_inproj0(x, ctx, a2, b2, dp42, tm):
    l, d = x.shape
    nl, nc = l // tm, ctx.shape[0] // tm
    half = dp42.shape[-1]
    e = 2 * half
    nt = nl + nc

    def body(x_ref, c_ref, a_ref, b_ref, dp_ref, o_ref, acc_ref):
        i = pl.program_id(1)
        is_ctx = i >= nl

        @pl.when(i == 0)
        def _():
            acc_ref[...] = jnp.zeros_like(acc_ref)

        xv = jnp.where(is_ctx, c_ref[...], x_ref[...])
        h = (xv * _sel_row(a_ref, is_ctx) + _sel_row(b_ref, is_ctx)).astype(BF)
        acc_ref[:, :half] += _dot_tn(h, dp_ref[0, 0])
        acc_ref[:, half:] += _dot_tn(h, dp_ref[0, 1])

        @pl.when(i == nt - 1)
        def _():
            o_ref[...] = acc_ref[...].astype(BF)

    lat, cx = _lat_or_ctx_specs(tm, d, nl, 2, 1)
    return pl.pallas_call(
        body, name="l0_dw_inproj", grid=(4, nt),
        in_specs=[lat, cx, _full((2, d)), _full((2, d)), pl.BlockSpec((1, 2, tm, half), lambda k, i: (k, 0, i, 0))],
        out_specs=pl.BlockSpec((d, e), lambda k, i: (0, k)),
        out_shape=jax.ShapeDtypeStruct((d, 4 * e), BF),
        scratch_shapes=[pltpu.VMEM((d, e), F32)], compiler_params=_cparams(),
    )(x, ctx, a2, b2, dp42)


def _dw_outproj0(q3, dr_l, dr_c, gt2, tm):
    l, d = dr_l.shape
    nl, nc = l // tm, dr_c.shape[0] // tm
    _, r, half = q3.shape
    nt = nl + nc

    def body(q_ref, dl_ref, dc_ref, g_ref, o_ref, acc_ref):
        i = pl.program_id(0)
        is_ctx = i >= nl

        @pl.when(i == 0)
        def _():
            acc_ref[...] = jnp.zeros_like(acc_ref)

        dr = jnp.where(is_ctx, dc_ref[...], dl_ref[...]).astype(F32)
        dfx = (dr * _sel_row(g_ref, is_ctx)).astype(BF)
        acc_ref[:half, :] += _dot_tn(q_ref[0], dfx)
        acc_ref[half:, :] += _dot_tn(q_ref[1], dfx)

        @pl.when(i == nt - 1)
        def _():
            o_ref[...] = acc_ref[...].astype(BF)

    lat, cx = _lat_or_ctx_specs(tm, d, nl, 1, 0)
    return pl.pallas_call(
        body, name="l0_dw_outproj", grid=(nt,),
        in_specs=[pl.BlockSpec((2, tm, half), lambda i: (0, i, 0)), lat, cx, _full((2, d))],
        out_specs=_full((2 * half, d)), out_shape=jax.ShapeDtypeStruct((2 * half, d), BF),
        scratch_shapes=[pltpu.VMEM((2 * half, d), F32)], compiler_params=_cparams(),
    )(q3, dr_l, dr_c, gt2)


def _cr_tile(j, cap=256):
    for cand in (512, 256, 128, 64, 32, 16, 8):
        if cand <= cap and j % cand == 0:
            return cand
    raise ValueError(j)


def _inproj1(xh_cr, a1, b1, w, tag):
    j, d16 = xh_cr.shape
    d = d16 // CHUNK
    e = w.shape[1] // 2
    nb = e // LANE_BLOCK
    tj = _cr_tile(j)

    def body(x_ref, a_ref, b_ref, w_hbm, u_ref, z_ref, w_ref):
        @pl.when(jnp.logical_and(pl.program_id(0) == 0, pl.program_id(1) == 0))
        def _():
            pltpu.sync_copy(w_hbm, w_ref)

        h = (x_ref[...] * a_ref[...] + b_ref[...]).astype(BF)
        r = _dot(h, w_ref[...])
        for b in range(nb):
            u_ref[b] = r[:, b * LANE_BLOCK:(b + 1) * LANE_BLOCK].astype(BF)
        z_ref[...] = r[:, e:].astype(BF)

    return pl.pallas_call(
        body, name="l1_inproj_" + tag, grid=(j // tj, CHUNK),
        in_specs=[pl.BlockSpec((tj, d), lambda t, s: (t, s)), _full((1, d)), _full((1, d)), ANY],
        out_specs=[pl.BlockSpec((nb, tj, LANE_BLOCK), lambda t, s: (0, t, s)), pl.BlockSpec((tj, e), lambda t, s: (t, s))],
        out_shape=[jax.ShapeDtypeStruct((nb, j, BCR_W), BF), jax.ShapeDtypeStruct((j, CHUNK * e), BF)],
        scratch_shapes=[pltpu.VMEM(w.shape, BF)], compiler_params=_cparams(),
    )(xh_cr, a1, b1, w)


def _bmm(a_list, w_list, trans, out_dtype, name):
    nb, j, ka = a_list[0].shape
    n_out = w_list[0].shape[1] if trans[0] else w_list[0].shape[2]
    tn = n_out // 2
    tj = _cr_tile(j, 512)
    n = len(a_list)

    def body(*refs):
        o_ref = refs[2 * n]
        acc = None
        for i in range(n):
            a = refs[i][0].astype(BF)
            w = refs[n + i][0]
            t = _dot_nt(a, w) if trans[i] else _dot(a, w)
            acc = t if acc is None else acc + t
        o_ref[0] = acc.astype(out_dtype)

    a_specs = [pl.BlockSpec((1, tj, a.shape[2]), lambda b, h, t: (b, t, 0)) for a in a_list]
    w_specs = [pl.BlockSpec((1, tn, w.shape[2]), lambda b, h, t: (b, h, 0)) if tr
               else pl.BlockSpec((1, w.shape[1], tn), lambda b, h, t: (b, 0, h)) for w, tr in zip(w_list, trans)]
    return pl.pallas_call(
        body, name=name, grid=(nb, 2, j // tj), in_specs=a_specs + w_specs,
        out_specs=pl.BlockSpec((1, tj, tn), lambda b, h, t: (b, t, h)),
        out_shape=jax.ShapeDtypeStruct((nb, j, n_out), out_dtype), compiler_params=_cparams(),
    )(*a_list, *w_list)


def _group_mask(lane_groups):
    row = lax.broadcasted_iota(jnp.int32, (LANE_BLOCK, LANE_BLOCK), 0) // S5_P
    lane = lax.broadcasted_iota(jnp.int32, (LANE_BLOCK, LANE_BLOCK), 1)
    return row == lane_groups(lane)


def _expand_toeplitz(wcomp):
    nb = wcomp.shape[0]
    nd = 2 * CHUNK - 1

    def body(c_ref, o_ref):
        mask = _group_mask(lambda lane: lane // S5_P)
        tiles = [jnp.where(mask, c_ref[0, dd], 0.0).astype(BF) for dd in range(nd)]
        for s in range(CHUNK):
            for t in range(CHUNK):
                o_ref[0, s * LANE_BLOCK:(s + 1) * LANE_BLOCK, t * LANE_BLOCK:(t + 1) * LANE_BLOCK] = tiles[t - s + CHUNK - 1]

    return pl.pallas_call(
        body, name="l1_expand_toeplitz", grid=(nb,),
        in_specs=[pl.BlockSpec((1, nd, LANE_BLOCK, LANE_BLOCK), lambda b: (b, 0, 0, 0))],
        out_specs=pl.BlockSpec((1, BCR_W, BCR_W), lambda b: (b, 0, 0)),
        out_shape=jax.ShapeDtypeStruct((nb, BCR_W, BCR_W), BF), compiler_params=_cparams(),
    )(wcomp)


def _expand_blocks(comp, name):
    nb = comp.shape[0]
    lanes_per_dir = ZL_W // 2

    def body(c_ref, o_ref):
        masks = [_group_mask(lambda lane, lb=lb: 2 * lb + lane // S5_N) for lb in range(4)]
        for r in range(2):
            for s in range(CHUNK):
                for ri in range(2):
                    m = c_ref[0, r, s, :, ri * S5_N:(ri + 1) * S5_N]
                    mm = jnp.concatenate([m, m], axis=1)
                    for lb in range(4):
                        c0 = r * lanes_per_dir + ri * ZH + lb * LANE_BLOCK
                        o_ref[0, s * LANE_BLOCK:(s + 1) * LANE_BLOCK, c0:c0 + LANE_BLOCK] = (
                            jnp.where(masks[lb], mm, 0.0).astype(BF))

    return pl.pallas_call(
        body, name=name, grid=(nb,),
        in_specs=[pl.BlockSpec((1, 2, CHUNK, LANE_BLOCK, LANE_BLOCK), lambda b: (b, 0, 0, 0, 0))],
        out_specs=pl.BlockSpec((1, BCR_W, ZL_W), lambda b: (b, 0, 0)),
        out_shape=jax.ShapeDtypeStruct((nb, BCR_W, ZL_W), BF), compiler_params=_cparams(),
    )(comp)


def _bdw(a, b_, kind, init, name):
    nb, j, ka = a.shape
    kb = b_.shape[2]
    tn = kb // 2
    tj = _cr_tile(j)
    nt = j // tj
    has_init = init is not None
    nd = 2 * CHUNK - 1

    def body(*refs):
        a_ref, b_ref = refs[0], refs[1]
        o_ref, acc_ref = refs[-2], refs[-1]
        h, t = pl.program_id(1), pl.program_id(2)

        @pl.when(t == 0)
        def _():
            acc_ref[...] = jnp.zeros_like(acc_ref)

        acc_ref[...] += _dot_tn(a_ref[0].astype(BF), b_ref[0].astype(BF))

        if kind == "toeplitz":
            @pl.when(jnp.logical_and(t == 0, h == 0))
            def _():
                o_ref[...] = jnp.zeros_like(o_ref)

            @pl.when(t == nt - 1)
            def _():
                mask = _group_mask(lambda lane: lane // S5_P)
                for s in range(CHUNK):
                    for tl in range(CHUNK // 2):
                        dd = h * (CHUNK // 2) + (tl - s + CHUNK - 1)
                        blk = acc_ref[s * LANE_BLOCK:(s + 1) * LANE_BLOCK, tl * LANE_BLOCK:(tl + 1) * LANE_BLOCK]
                        o_ref[0, dd] += jnp.where(mask, blk, 0.0)
        else:
            @pl.when(t == nt - 1)
            def _():
                masks = [_group_mask(lambda lane, lb=lb: 2 * lb + lane // S5_N) for lb in range(4)]
                for s in range(CHUNK):
                    for ri in range(2):
                        v = None
                        for lb in range(4):
                            c0 = ri * ZH + lb * LANE_BLOCK
                            blk = acc_ref[s * LANE_BLOCK:(s + 1) * LANE_BLOCK, c0:c0 + LANE_BLOCK]
                            blk = jnp.where(masks[lb], blk, 0.0)
                            v = blk if v is None else v + blk
                        folded = v[:, :S5_N] + v[:, S5_N:]
                        if has_init:
                            folded = folded + refs[2][0, 0, s, :, ri * S5_N:(ri + 1) * S5_N]
                        o_ref[0, 0, s, :, ri * S5_N:(ri + 1) * S5_N] = folded

    in_specs = [pl.BlockSpec((1, tj, ka), lambda b, h, t: (b, t, 0)), pl.BlockSpec((1, tj, tn), lambda b, h, t: (b, t, h))]
    args = [a, b_]
    if kind == "toeplitz":
        ospec = pl.BlockSpec((1, nd, LANE_BLOCK, LANE_BLOCK), lambda b, h, t: (b, 0, 0, 0))
        oshape = jax.ShapeDtypeStruct((nb, nd, LANE_BLOCK, LANE_BLOCK), F32)
    else:
        ospec = pl.BlockSpec((1, 1, CHUNK, LANE_BLOCK, LANE_BLOCK), lambda b, h, t: (b, h, 0, 0, 0))
        oshape = jax.ShapeDtypeStruct((nb, 2, CHUNK, LANE_BLOCK, LANE_BLOCK), F32)
        if has_init:
            in_specs.append(ospec)
            args.append(init)
    return pl.pallas_call(
        body, name=name, grid=(nb, 2, nt), in_specs=in_specs, out_specs=ospec, out_shape=oshape,
        scratch_shapes=[pltpu.VMEM((ka, tn), F32)], compiler_params=_cparams(),
    )(*args)


def _scan(z_l, z_c, coef, chains, conj, s_l=None, s_c=None, name="l1_scan"):
    nb, jl, _ = z_l.shape
    jc = z_c.shape[1]
    with_da = s_l is not None
    sign = -1.0 if conj else 1.0
    hw = 2 * ZH

    def body(*refs):
        zl_ref, zc_ref, cf_ref = refs[:3]
        k = 3
        if with_da:
            sl_ref, sc_ref = refs[3:5]
            k = 5
        ol_ref, oc_ref = refs[k:k + 2]
        d = pl.program_id(1)
        rowi = lax.broadcasted_iota(jnp.int32, (8, ZH), 0)

        def coef_rows(r0, nr):
            return cf_ref[0, 0, r0:r0 + nr, :ZH], sign * cf_ref[0, 0, r0:r0 + nr, ZH:]

        steps = [(1, coef_rows(0, 1)), (2, coef_rows(1, 1)), (4, coef_rows(2, 1))]

        def run(chain):
            carry = (jnp.zeros((1, ZH), F32), jnp.zeros((1, ZH), F32))
            da = (jnp.zeros((8, ZH), F32), jnp.zeros((8, ZH), F32))
            for which, rev in chain:
                src, dst = (zc_ref, oc_ref) if which == "c" else (zl_ref, ol_ref)
                sref = (sc_ref if which == "c" else sl_ref) if with_da else None
                ng = (jc if which == "c" else jl) // 8
                tr, ti = coef_rows(16, 8) if rev else coef_rows(8, 8)

                def step(it, st, src=src, dst=dst, sref=sref, ng=ng, tr=tr, ti=ti, rev=rev):
                    cr_, ci_, dar, dai = st
                    g = (ng - 1 - it) if rev else it
                    off = pl.multiple_of(g * 8, 8)
                    xr = src[0, pl.ds(off, 8), :ZH]
                    xi = src[0, pl.ds(off, 8), ZH:]
                    for sh, (ar, ai) in steps:
                        if rev:
                            keep = rowi < 8 - sh
                            sr = jnp.where(keep, pltpu.roll(xr, 8 - sh, 0), 0.0)
                            si = jnp.where(keep, pltpu.roll(xi, 8 - sh, 0), 0.0)
                        else:
                            keep = rowi >= sh
                            sr = jnp.where(keep, pltpu.roll(xr, sh, 0), 0.0)
                            si = jnp.where(keep, pltpu.roll(xi, sh, 0), 0.0)
                        xr, xi = xr + ar * sr - ai * si, xi + ar * si + ai * sr
                    ir = xr + tr * cr_ - ti * ci_
                    ii = xi + tr * ci_ + ti * cr_
                    if rev:
                        er = jnp.where(rowi == 7, cr_, pltpu.roll(ir, 7, 0))
                        ei = jnp.where(rowi == 7, ci_, pltpu.roll(ii, 7, 0))
                        ncr, nci = ir[0:1], ii[0:1]
                    else:
                        er = jnp.where(rowi == 0, cr_, pltpu.roll(ir, 1, 0))
                        ei = jnp.where(rowi == 0, ci_, pltpu.roll(ii, 1, 0))
                        ncr, nci = ir[7:8], ii[7:8]
                    dst[0, pl.ds(off, 8), :ZH] = er
                    dst[0, pl.ds(off, 8), ZH:] = ei
                    if sref is not None:
                        s_r = sref[0, pl.ds(off, 8), :ZH]
                        s_i = sref[0, pl.ds(off, 8), ZH:]
                        dar = dar + s_r * er + s_i * ei
                        dai = dai + s_r * ei - s_i * er
                    return ncr, nci, dar, dai

                carry_da = lax.fori_loop(0, ng, step, (*carry, *da))
                carry, da = carry_da[:2], carry_da[2:]
            if with_da:
                refs[k + 2][0, 0] = jnp.concatenate([da[0], da[1]], axis=1)

        for dd in range(2):
            @pl.when(d == dd)
            def _(dd=dd):
                run(chains[dd])

    zspec_l = pl.BlockSpec((1, jl, hw), lambda b, d: (b, 0, d))
    zspec_c = pl.BlockSpec((1, jc, hw), lambda b, d: (b, 0, d))
    in_specs = [zspec_l, zspec_c, pl.BlockSpec((1, 1, 24, hw), lambda b, d: (b, d, 0, 0))]
    args = [z_l, z_c, coef]
    out_specs = [zspec_l, zspec_c]
    out_shape = [jax.ShapeDtypeStruct(z_l.shape, F32), jax.ShapeDtypeStruct(z_c.shape, F32)]
    if with_da:
        in_specs += [zspec_l, zspec_c]
        args += [s_l, s_c]
        out_specs.append(pl.BlockSpec((1, 1, 8, hw), lambda b, d: (b, d, 0, 0)))
        out_shape.append(jax.ShapeDtypeStruct((nb, 2, 8, hw), F32))
    return pl.pallas_call(body, name=name, grid=(nb, 2), in_specs=in_specs, out_specs=out_specs,
                          out_shape=out_shape, compiler_params=_cparams())(*args)


def _glu_fwd(y_bcr, z_cr, w_glu, b_glu):
    nb, j, _ = y_bcr.shape
    e = nb * LANE_BLOCK
    tj = _cr_tile(j)

    def body(y_ref, z_ref, w_hbm, b_ref, o_ref, sg_ref, w_ref):
        @pl.when(jnp.logical_and(pl.program_id(0) == 0, pl.program_id(1) == 0))
        def _():
            pltpu.sync_copy(w_hbm, w_ref)

        y = jnp.concatenate([y_ref[b] for b in range(nb)], axis=1).astype(F32)
        g = _gelu_parts(y)[0]
        sg = _sigmoid(_dot(g.astype(BF), w_ref[...]) + b_ref[...])
        z = z_ref[...].astype(F32)
        o_ref[...] = (g * sg * (z * _sigmoid(z))).astype(BF)
        sg_ref[...] = sg.astype(BF)

    tok = pl.BlockSpec((tj, e), lambda t, s: (t, s))
    return pl.pallas_call(
        body, name="l1_glu_fwd", grid=(j // tj, CHUNK),
        in_specs=[pl.BlockSpec((nb, tj, LANE_BLOCK), lambda t, s: (0, t, s)), tok, ANY, _full((1, e))],
        out_specs=[tok, tok],
        out_shape=[jax.ShapeDtypeStruct((j, CHUNK * e), BF), jax.ShapeDtypeStruct((j, CHUNK * e), BF)],
        scratch_shapes=[pltpu.VMEM(w_glu.shape, BF)], compiler_params=_cparams(),
    )(y_bcr, z_cr, w_glu, b_glu)


def _final(w_cr, w_out, xh_cr, tgt_cr, vecs):
    j, e16 = w_cr.shape
    e = e16 // CHUNK
    d = w_out.shape[1]
    tj = _cr_tile(j)

    def body(w_ref, wo_hbm, xh_ref, t_ref, v_ref, dr_ref, acc_ref, wo_ref):
        @pl.when(jnp.logical_and(pl.program_id(0) == 0, pl.program_id(1) == 0))
        def _():
            pltpu.sync_copy(wo_hbm, wo_ref)
            acc_ref[...] = jnp.zeros_like(acc_ref)

        o = _dot(w_ref[...], wo_ref[...])
        x1 = xh_ref[...] * v_ref[0:1, :] + v_ref[1:2, :]
        rr = DN_ALPHA * x1 + v_ref[2:3, :] * o
        mu = jnp.mean(rr, axis=-1, keepdims=True)
        cen = rr - mu
        rstd = lax.rsqrt(jnp.mean(cen * cen, axis=-1, keepdims=True) + LN_EPS)
        xh2 = cen * rstd
        err = xh2 * v_ref[3:4, :] + v_ref[4:5, :] - t_ref[...]
        dy = err * (1.0 / d)
        dxh = dy * v_ref[3:4, :]
        dr = rstd * (dxh - jnp.mean(dxh, axis=-1, keepdims=True) - xh2 * jnp.mean(dxh * xh2, axis=-1, keepdims=True))
        dr_ref[...] = dr.astype(BF)
        acc_ref[0:1, :] += jnp.sum(dy * xh2, axis=0, keepdims=True)
        acc_ref[1:2, :] += jnp.sum(dy, axis=0, keepdims=True)
        acc_ref[2:3, :] += jnp.sum(dr * o, axis=0, keepdims=True)
        acc_ref[3:4, :] += (0.5 / d) * jnp.sum(err * err, axis=0, keepdims=True)

    tok_d = pl.BlockSpec((tj, d), lambda t, s: (t, s))
    return pl.pallas_call(
        body, name="l1_final", grid=(j // tj, CHUNK),
        in_specs=[pl.BlockSpec((tj, e), lambda t, s: (t, s)), ANY, tok_d, tok_d, _full((8, d))],
        out_specs=[tok_d, _full((8, d))],
        out_shape=[jax.ShapeDtypeStruct((j, CHUNK * d), BF), jax.ShapeDtypeStruct((8, d), F32)],
        scratch_shapes=[pltpu.VMEM(w_out.shape, BF)], compiler_params=_cparams(),
    )(w_cr, w_out, xh_cr, tgt_cr, vecs)


def _glu_bwd(dr_cr, gt1, w_out, w_glu, y_bcr, z_cr, sg_cr):
    nb, j, _ = y_bcr.shape
    e, d = w_out.shape
    tj = _cr_tile(j)

    def body(dr_ref, g_ref, wo_hbm, wg_hbm, y_ref, z_ref, sg_ref, dz_ref, dt_ref, dy_ref, wo_ref, wg_ref):
        @pl.when(jnp.logical_and(pl.program_id(0) == 0, pl.program_id(1) == 0))
        def _():
            pltpu.sync_copy(wo_hbm, wo_ref)
            pltpu.sync_copy(wg_hbm, wg_ref)

        do = (dr_ref[...].astype(F32) * g_ref[...]).astype(BF)
        dw = _dot_nt(do, wo_ref[...])
        y = jnp.concatenate([y_ref[b] for b in range(nb)], axis=1).astype(F32)
        g, dgel = _gelu_parts(y)
        z = z_ref[...].astype(F32)
        sz = _sigmoid(z)
        sg = sg_ref[...].astype(F32)
        dg2 = dw * (z * sz)
        dz_ref[...] = (dw * g * sg * (sz * (1.0 + z * (1.0 - sz)))).astype(BF)
        dt = (dg2 * g * sg * (1.0 - sg)).astype(BF)
        dt_ref[...] = dt
        dy = (dg2 * sg + _dot_nt(dt, wg_ref[...])) * dgel
        for b in range(nb):
            dy_ref[b] = dy[:, b * LANE_BLOCK:(b + 1) * LANE_BLOCK].astype(BF)

    tok_e = pl.BlockSpec((tj, e), lambda t, s: (t, s))
    blk = pl.BlockSpec((nb, tj, LANE_BLOCK), lambda t, s: (0, t, s))
    return pl.pallas_call(
        body, name="l1_glu_bwd", grid=(j // tj, CHUNK),
        in_specs=[pl.BlockSpec((tj, d), lambda t, s: (t, s)), _full((1, d)), ANY, ANY, blk, tok_e, tok_e],
        out_specs=[tok_e, tok_e, blk],
        out_shape=[jax.ShapeDtypeStruct((j, CHUNK * e), BF), jax.ShapeDtypeStruct((j, CHUNK * e), BF),
                   jax.ShapeDtypeStruct((nb, j, BCR_W), BF)],
        scratch_shapes=[pltpu.VMEM(w_out.shape, BF), pltpu.VMEM(w_glu.shape, BF)], compiler_params=_cparams(),
    )(dr_cr, gt1, w_out, w_glu, y_bcr, z_cr, sg_cr)


def _bwd_inproj1(du_bcr, dz_cr, w, xh_cr, rs_cr, dr2_cr, vecs, tag):
    nb, j, _ = du_bcr.shape
    d = w.shape[0]
    e = w.shape[1] // 2
    tj = _cr_tile(j)

    def body(du_ref, dz_ref, w_hbm, xh_ref, rs_ref, dr2_ref, v_ref, dr1_ref, acc_ref, w_ref):
        @pl.when(jnp.logical_and(pl.program_id(0) == 0, pl.program_id(1) == 0))
        def _():
            pltpu.sync_copy(w_hbm, w_ref)
            acc_ref[...] = jnp.zeros_like(acc_ref)

        du = jnp.concatenate([du_ref[b] for b in range(nb)], axis=1)
        dh = _dot_nt(du, w_ref[:, :e]) + _dot_nt(dz_ref[...], w_ref[:, e:])
        xh = xh_ref[...]
        x1 = xh * v_ref[0:1, :] + v_ref[1:2, :]
        dx1 = DN_ALPHA * dr2_ref[...].astype(F32) + dh * v_ref[2:3, :]
        dxh = dx1 * v_ref[0:1, :]
        rstd = rs_ref[:, 0:1]
        dr1 = rstd * (dxh - jnp.mean(dxh, axis=-1, keepdims=True) - xh * jnp.mean(dxh * xh, axis=-1, keepdims=True))
        dr1_ref[...] = dr1.astype(BF)
        acc_ref[0:1, :] += jnp.sum(dh * x1, axis=0, keepdims=True)
        acc_ref[1:2, :] += jnp.sum(dh, axis=0, keepdims=True)
        acc_ref[2:3, :] += jnp.sum(dx1 * xh, axis=0, keepdims=True)
        acc_ref[3:4, :] += jnp.sum(dx1, axis=0, keepdims=True)

    tok_d = pl.BlockSpec((tj, d), lambda t, s: (t, s))
    return pl.pallas_call(
        body, name="l1_bwd_inproj_" + tag, grid=(j // tj, CHUNK),
        in_specs=[pl.BlockSpec((nb, tj, LANE_BLOCK), lambda t, s: (0, t, s)), pl.BlockSpec((tj, e), lambda t, s: (t, s)),
                  ANY, tok_d, pl.BlockSpec((tj, 128), lambda t, s: (t, s)), tok_d, _full((8, d))],
        out_specs=[tok_d, _full((8, d))],
        out_shape=[jax.ShapeDtypeStruct((j, CHUNK * d), BF), jax.ShapeDtypeStruct((8, d), F32)],
        scratch_shapes=[pltpu.VMEM(w.shape, BF)], compiler_params=_cparams(),
    )(du_bcr, dz_cr, w, xh_cr, rs_cr, dr2_cr, vecs)


def _dw_cr(lhs, rhs, lhs_kind, rhs_kind, vec, bias_sum, init, name):
    if lhs_kind == "gelu_bcr":
        nb_l, j, _ = lhs.shape
        k = nb_l * LANE_BLOCK
    else:
        j = lhs.shape[0]
        k = lhs.shape[1] // CHUNK
    if rhs_kind == "bcr":
        nb_r = rhs.shape[0]
        n = nb_r * LANE_BLOCK
    else:
        n = rhs.shape[1] // CHUNK
    tj = _cr_tile(j)
    nh = 2 if k * n * 4 > (8 << 20) else 1
    tn = n // nh
    nbh = tn // LANE_BLOCK
    nt = j // tj
    has_init = init is not None

    def body(*refs):
        refs = list(refs)
        l_ref, r_ref = refs[0], refs[1]
        pos = 2
        v_ref = None
        if vec is not None:
            v_ref = refs[pos]
            pos += 1
        i_ref = None
        if has_init:
            i_ref = refs[pos]
            pos += 1
        o_ref = refs[pos]
        pos += 1
        bs_ref = None
        if bias_sum:
            bs_ref = refs[pos]
            pos += 1
        acc_ref = refs[pos]
        t, s = pl.program_id(1), pl.program_id(2)
        first = jnp.logical_and(t == 0, s == 0)

        @pl.when(first)
        def _():
            acc_ref[...] = i_ref[...] if has_init else jnp.zeros_like(acc_ref)
            if bias_sum:
                bs_ref[...] = jnp.zeros_like(bs_ref)

        if lhs_kind == "gelu_bcr":
            y = jnp.concatenate([l_ref[b] for b in range(nb_l)], axis=1).astype(F32)
            lv = _gelu_parts(y)[0].astype(BF)
        elif lhs_kind == "mod":
            lv = (l_ref[...] * v_ref[0:1, :] + v_ref[1:2, :]).astype(BF)
        else:
            lv = l_ref[...]
        if rhs_kind == "bcr":
            rv = jnp.concatenate([r_ref[b] for b in range(nbh)], axis=1)
        elif rhs_kind == "scaled":
            rv = (r_ref[...].astype(F32) * v_ref[0:1, :]).astype(BF)
        else:
            rv = r_ref[...]
        acc_ref[...] += _dot_tn(lv, rv)
        if bias_sum:
            bs_ref[0:1, :] += jnp.sum(rv.astype(F32), axis=0, keepdims=True)

        @pl.when(jnp.logical_and(t == nt - 1, s == CHUNK - 1))
        def _():
            o_ref[...] = acc_ref[...].astype(BF)

    if lhs_kind == "gelu_bcr":
        l_spec = pl.BlockSpec((nb_l, tj, LANE_BLOCK), lambda h, t, s: (0, t, s))
    else:
        l_spec = pl.BlockSpec((tj, k), lambda h, t, s: (t, s))
    if rhs_kind == "bcr":
        r_spec = pl.BlockSpec((nbh, tj, LANE_BLOCK), lambda h, t, s: (h, t, s))
    else:
        r_spec = pl.BlockSpec((tj, tn), lambda h, t, s: (t, s * nh + h))
    in_specs, args = [l_spec, r_spec], [lhs, rhs]
    if vec is not None:
        in_specs.append(_full(vec.shape))
        args.append(vec)
    o_spec = pl.BlockSpec((k, tn), lambda h, t, s: (0, h))
    if has_init:
        in_specs.append(o_spec)
        args.append(init)
    out_specs, out_shape = [o_spec], [jax.ShapeDtypeStruct((k, n), BF)]
    if bias_sum:
        out_specs.append(pl.BlockSpec((8, tn), lambda h, t, s: (0, h)))
        out_shape.append(jax.ShapeDtypeStruct((8, n), F32))
    res = pl.pallas_call(
        body, name=name, grid=(nh, nt, CHUNK), in_specs=in_specs, out_specs=out_specs, out_shape=out_shape,
        scratch_shapes=[pltpu.VMEM((k, tn), F32)], compiler_params=_cparams(),
    )(*args)
    return res if bias_sum else res[0]


def _dw_cr_f32(lhs, rhs, vec, name):
    j = lhs.shape[0]
    k = lhs.shape[1] // CHUNK
    nb_r = rhs.shape[0]
    n = nb_r * LANE_BLOCK
    tj = _cr_tile(j)
    nt = j // tj

    def body(l_ref, r_ref, v_ref, o_ref):
        @pl.when(jnp.logical_and(pl.program_id(0) == 0, pl.program_id(1) == 0))
        def _():
            o_ref[...] = jnp.zeros_like(o_ref)

        lv = (l_ref[...] * v_ref[0:1, :] + v_ref[1:2, :]).astype(BF)
        rv = jnp.concatenate([r_ref[b] for b in range(nb_r)], axis=1)
        o_ref[...] += _dot_tn(lv, rv)

    return pl.pallas_call(
        body, name=name, grid=(nt, CHUNK),
        in_specs=[pl.BlockSpec((tj, k), lambda t, s: (t, s)), pl.BlockSpec((nb_r, tj, LANE_BLOCK), lambda t, s: (0, t, s)),
                  _full(vec.shape)],
        out_specs=_full((k, n)), out_shape=jax.ShapeDtypeStruct((k, n), F32), compiler_params=_cparams(),
    )(lhs, rhs, vec)


def _s5_compact(lam_re, lam_im, log_step, b_re, b_im, c_re, c_im, d_skip):
    hp = lax.Precision.HIGHEST
    g = lam_re.shape[1]
    nb = g // GROUPS_PER_BLOCK
    t, p, n = CHUNK, S5_P, S5_N
    dt = jnp.exp(log_step)[..., None]
    ks = jnp.arange(t + 1, dtype=F32).reshape(t + 1, 1, 1, 1)
    mag = jnp.exp(ks * (lam_re * dt)[None])
    ang = ks * (lam_im * dt)[None]
    pr, pi = mag * jnp.cos(ang), mag * jnp.sin(ang)
    ar, ai = pr[1], pi[1]
    qr, qi = ar - 1.0, ai
    den = lam_re * lam_re + lam_im * lam_im
    fr = (qr * lam_re + qi * lam_im) / den
    fi = (qi * lam_re - qr * lam_im) / den
    bbr = fr[..., None] * b_re - fi[..., None] * b_im
    bbi = fr[..., None] * b_im + fi[..., None] * b_re
    abr = pr[:t, ..., None] * bbr[None] - pi[:t, ..., None] * bbi[None]
    abi = pr[:t, ..., None] * bbi[None] + pi[:t, ..., None] * bbr[None]
    kd = (jnp.einsum("rgpn,krgnq->rgkpq", c_re, abr, precision=hp)
          - jnp.einsum("rgpn,krgnq->rgkpq", c_im, abi, precision=hp))
    skip = jnp.eye(p, dtype=F32)[None] * d_skip.reshape(g, p)[:, :, None]
    diag = kd[0][:, 0] + kd[1][:, 0] + skip
    qd = jnp.concatenate([jnp.flip(kd[1][:, 1:], axis=1), diag[:, None], kd[0][:, 1:]], axis=1)
    nd = 2 * t - 1
    wc = qd.transpose(0, 1, 3, 2).reshape(nb, GROUPS_PER_BLOCK, nd, p, p).transpose(0, 2, 1, 3, 4)
    wcomp = jnp.broadcast_to(wc[:, :, :, :, None, :], (nb, nd, GROUPS_PER_BLOCK, p, GROUPS_PER_BLOCK, p))
    wcomp = wcomp.reshape(nb, nd, LANE_BLOCK, LANE_BLOCK)
    bcr = jnp.stack([jnp.flip(abr[:, 0], axis=0), abr[:, 1]])
    bci = jnp.stack([jnp.flip(abi[:, 0], axis=0), abi[:, 1]])
    bcc = jnp.stack([bcr, bci], axis=2)
    bcomp = bcc.reshape(2, t, 2, nb, GROUPS_PER_BLOCK, n, p).transpose(3, 0, 1, 4, 6, 2, 5)
    bcomp = bcomp.reshape(nb, 2, t, LANE_BLOCK, 2 * n)
    prf = jnp.stack([pr[1:, 0], jnp.flip(pr[1:, 1], axis=0)])
    pif = jnp.stack([pi[1:, 0], jnp.flip(pi[1:, 1], axis=0)])
    cr_t = c_re[:, None]
    ci_t = c_im[:, None]
    ccr = cr_t * prf[:, :, :, None, :] - ci_t * pif[:, :, :, None, :]
    cci = -(cr_t * pif[:, :, :, None, :] + ci_t * prf[:, :, :, None, :])
    ccc = jnp.stack([ccr, cci], axis=2)
    ccomp = ccc.reshape(2, t, 2, nb, GROUPS_PER_BLOCK, p, n).transpose(3, 0, 1, 4, 5, 2, 6)
    ccomp = ccomp.reshape(nb, 2, t, LANE_BLOCK, 2 * n)
    return wcomp, bcomp, ccomp, pr[t], pi[t]


def _scan_coef(lam_re, lam_im, log_step):
    g = lam_re.shape[1]
    nb = g // GROUPS_PER_BLOCK
    ms = jnp.array([1, 2, 4, 0, 0, 0, 0, 0] + list(range(1, 9)) + list(range(8, 0, -1)), F32) * CHUNK
    dt = jnp.exp(log_step)[..., None]
    mag = jnp.exp(ms.reshape(-1, 1, 1, 1) * (lam_re * dt)[None])
    ang = ms.reshape(-1, 1, 1, 1) * (lam_im * dt)[None]
    cr, ci = mag * jnp.cos(ang), mag * jnp.sin(ang)
    lay = lambda a: a.reshape(24, 2, nb, ZH).transpose(2, 1, 0, 3)
    return jnp.concatenate([lay(cr), lay(ci)], axis=-1)


def _to_cr(a):
    return a.reshape(a.shape[0] // CHUNK, CHUNK * a.shape[1])


def _from_cr(a, c):
    return a.reshape(a.shape[0] * CHUNK, c)


def _pad8(v):
    return jnp.concatenate([v, jnp.zeros((8 - v.shape[0], v.shape[1]), v.dtype)], axis=0)


def _local_step(x, c, ctx, c_ctx, loss_target, w):
    l, d = x.shape
    lc = ctx.shape[0]
    e = w["conv_w_out"].shape[0]
    half = e // 2
    nb = e // LANE_BLOCK
    tm = min(256, lc)
    assert lc == tm and l % tm == 0 and tm % GRID_W == 0 and (tm & (tm - 1)) == 0
    tc = min(512, half)
    nl = l // tm

    c8 = _pad8(jnp.stack([c, c_ctx]))
    mod = _ada_fwd(c8, w["ada_w"], w["ada_b"])
    sh = mod[:, :2, :d]
    sc = mod[:, :2, d:2 * d]
    gt = mod[:, :2, 2 * d:]
    ln_g, ln_b = w["ln_g"], w["ln_b"]

    a0, b0 = 1.0 + sc[0], sh[0]
    p42 = _inproj0(x, ctx, a0, b0, w["conv_w_in"], tm)
    cw = w["conv_w"].reshape(3, 2, half)
    q3 = _conv_fwd(p42, cw, nl, tm, tc)
    xh1, rs1, fx = _outproj_ln0(q3, w["conv_w_out"], x, ctx, gt[0], tm)
    xh1_l, xh1_c = _to_cr(xh1[:l]), _to_cr(xh1[l:])
    rs1_l, rs1_c = _to_cr(rs1[:l]), _to_cr(rs1[l:])
    jl, jc = l // CHUNK, lc // CHUNK

    g0, bb0 = ln_g[0:1], ln_b[0:1]
    a1 = g0 * (1.0 + sc[1])
    b1 = bb0 * (1.0 + sc[1]) + sh[1]
    u_l, z_l = _inproj1(xh1_l, a1[0:1], b1[0:1], w["ssm_w_in"], "lat")
    u_c, _ = _inproj1(xh1_c, a1[1:2], b1[1:2], w["ssm_w_in"], "ctx")
    s5 = (w["ssm_lam_re"], w["ssm_lam_im"], w["ssm_log_step"], w["ssm_b_re"], w["ssm_b_im"],
          w["ssm_c_re"], w["ssm_c_im"], w["ssm_d"])
    (wcomp, bcomp, ccomp, _, _), s5_vjp = jax.vjp(_s5_compact, *s5)
    wbig_b = _expand_toeplitz(wcomp)
    bc_b = _expand_blocks(bcomp, "l1_expand_bc")
    cct_b = _expand_blocks(ccomp, "l1_expand_cc")
    coef = lax.stop_gradient(_scan_coef(*s5[:3]))
    zz_l = _bmm([u_l], [bc_b], [False], F32, "l1_s5_z_lat")
    zz_c = _bmm([u_c], [bc_b], [False], F32, "l1_s5_z_ctx")
    fwd_chains = ((("c", False), ("l", False)), (("c", True), ("l", True)))
    st_l, st_c = _scan(zz_l, zz_c, coef, fwd_chains, False, name="l1_scan_fwd")
    y_l = _bmm([u_l, st_l], [wbig_b, cct_b], [False, True], BF, "l1_s5_y")
    b_glu = w["ssm_b_glu"].reshape(1, e)
    w_cr, sg_cr = _glu_fwd(y_l, z_l, w["ssm_w_glu"], b_glu)
    vec_f = _pad8(jnp.concatenate([g0, bb0, gt[1][0:1], ln_g[1:2], ln_b[1:2]], axis=0))
    dr2, acc_f = _final(w_cr, w["ssm_w_out"], xh1_l, _to_cr(loss_target), vec_f)
    loss = jnp.sum(acc_f[3])

    gt1 = gt[1][0:1]
    dz_l, dt_l, dy_l = _glu_bwd(dr2, gt1, w["ssm_w_out"], w["ssm_w_glu"], y_l, z_l, sg_cr)
    g_w_out = _dw_cr(w_cr, dr2, "cr", "scaled", gt1, False, None, "l1_dw_out")
    g_w_glu, bsum = _dw_cr(y_l, dt_l, "gelu_bcr", "cr", None, True, None, "l1_dw_glu")
    g_b_glu = bsum[0]
    ds_l = _bmm([dy_l], [cct_b], [False], F32, "l1_s5_ds")
    bwd_chains = ((("l", True), ("c", True)), (("l", False), ("c", False)))
    dzz_l, dzz_c, da = _scan(ds_l, jnp.zeros_like(zz_c), coef, bwd_chains, True, st_l, st_c, name="l1_scan_bwd")
    du_l = _bmm([dy_l, dzz_l], [wbig_b, bc_b], [True, True], BF, "l1_s5_dx_lat")
    du_c = _bmm([dzz_c], [bc_b], [True], BF, "l1_s5_dx_ctx")
    d_wbig = _bdw(u_l, dy_l, "toeplitz", None, "l1_s5_dwbig")
    d_cc = _bdw(dy_l, st_l, "blocks", None, "l1_s5_dcc")
    d_bc = _bdw(u_l, dzz_l, "blocks", _bdw(u_c, dzz_c, "blocks", None, "l1_s5_dbc_ctx"), "l1_s5_dbc")
    da = jnp.sum(da, axis=2)
    unlay = lambda a: a.reshape(nb, 2, GROUPS_PER_BLOCK, S5_N).transpose(1, 0, 2, 3).reshape(2, nb * GROUPS_PER_BLOCK, S5_N)
    g_s5 = s5_vjp((d_wbig, d_bc, d_cc, unlay(da[..., :ZH]), unlay(da[..., ZH:])))

    vec_l = _pad8(jnp.concatenate([g0, bb0, 1.0 + sc[1][0:1]], axis=0))
    vec_c = _pad8(jnp.concatenate([g0, bb0, 1.0 + sc[1][1:2]], axis=0))
    dr1_l, acc_l = _bwd_inproj1(du_l, dz_l, w["ssm_w_in"], xh1_l, rs1_l, dr2, vec_l, "lat")
    dr1_c, acc_c = _bwd_inproj1(du_c, jnp.zeros((jc, CHUNK * e), BF), w["ssm_w_in"], xh1_c, rs1_c,
                                jnp.zeros((jc, CHUNK * d), BF), vec_c, "ctx")
    mod_l = jnp.concatenate([a1[0:1], b1[0:1]], axis=0)
    mod_c = jnp.concatenate([a1[1:2], b1[1:2]], axis=0)
    g_in_u = _dw_cr(xh1_l, du_l, "mod", "bcr", mod_l, False, _dw_cr_f32(xh1_c, du_c, mod_c, "l1_dw_in_u_ctx"), "l1_dw_in_u")
    g_in_z = _dw_cr(xh1_l, dz_l, "mod", "cr", mod_l, False, None, "l1_dw_in_z")
    g_w_in1 = jnp.concatenate([g_in_u, g_in_z], axis=1)

    dr1_ln, dr1_cn = _from_cr(dr1_l, d), _from_cr(dr1_c, d)
    dq3, acc_g0 = _bwd_outproj0(dr1_ln, dr1_cn, gt[0], w["conv_w_out"], fx, tm)
    dp42, dcw = _conv_bwd(dq3, p42, cw, nl, tm, tc)
    grad_x, acc_0 = _bwd_inproj0(dp42, w["conv_w_in"], x, ctx, dr1_ln, dr1_cn, a0, tm)
    g_w_in0 = _dw_inproj0(x, ctx, a0, b0, dp42, tm)
    g_w_out0 = _dw_outproj0(q3, dr1_ln, dr1_cn, gt[0], tm)

    zero = jnp.zeros((d,), F32)
    dm0 = jnp.stack([jnp.concatenate([acc_0[2], acc_0[0], acc_g0[0]]), jnp.concatenate([acc_0[3], acc_0[1], acc_g0[1]])])
    dm1 = jnp.stack([jnp.concatenate([acc_l[1], acc_l[0], acc_f[2]]), jnp.concatenate([acc_c[1], acc_c[0], zero])])
    dm8 = jnp.stack([_pad8(dm0), _pad8(dm1)])
    g_ada_w, dc8 = _ada_bwd(c8, w["ada_w"], dm8)

    grads = {
        "c_ctx": dc8[0, 1] + dc8[1, 1],
        "ada_w": g_ada_w,
        "ada_b": jnp.stack([dm0[0] + dm0[1], dm1[0] + dm1[1]]),
        "ln_g": jnp.stack([acc_l[2] + acc_c[2], acc_f[0]]),
        "ln_b": jnp.stack([acc_l[3] + acc_c[3], acc_f[1]]),
        "conv_w_in": g_w_in0, "conv_w": dcw[:3].reshape(3, e), "conv_w_out": g_w_out0,
        "ssm_w_in": g_w_in1,
        "ssm_lam_re": g_s5[0], "ssm_lam_im": g_s5[1], "ssm_log_step": g_s5[2],
        "ssm_b_re": g_s5[3], "ssm_b_im": g_s5[4], "ssm_c_re": g_s5[5], "ssm_c_im": g_s5[6], "ssm_d": g_s5[7],
        "ssm_w_glu": g_w_glu, "ssm_b_glu": g_b_glu, "ssm_w_out": g_w_out,
    }
    return loss, grad_x, grads


WEIGHTS = ["c_ctx", "ada_w", "ada_b", "ln_g", "ln_b", "conv_w_in", "conv_w", "conv_w_out", "ssm_w_in",
           "ssm_lam_re", "ssm_lam_im", "ssm_log_step", "ssm_b_re", "ssm_b_im", "ssm_c_re", "ssm_c_im",
           "ssm_d", "ssm_w_glu", "ssm_b_glu", "ssm_w_out"]
BIG = {"ada_w": 1, "conv_w_in": 1, "conv_w_out": 0, "ssm_w_in": 1, "ssm_w_glu": 0, "ssm_w_out": 0}
SMALL_SHARDED = ["conv_w", "ssm_d", "ssm_b_glu"]
REPLICATED = ["c_ctx", "ada_b", "ln_g", "ln_b", "ssm_lam_re", "ssm_lam_im", "ssm_log_step",
              "ssm_b_re", "ssm_b_im", "ssm_c_re", "ssm_c_im"]


def _view2d(name, a):
    return a.reshape(-1, a.shape[-1])


def kernel(x, c, ctx, c_ctx, ada_w, ada_b, ln_g, ln_b, conv_w_in, conv_w, conv_w_out, ssm_w_in, ssm_lam_re, ssm_lam_im, ssm_log_step, ssm_b_re, ssm_b_im, ssm_c_re, ssm_c_im, ssm_d, ssm_w_glu, ssm_b_glu, ssm_w_out, loss_target, m_c_ctx, m_ada_w, m_ada_b, m_ln_g, m_ln_b, m_conv_w_in, m_conv_w, m_conv_w_out, m_ssm_w_in, m_ssm_lam_re, m_ssm_lam_im, m_ssm_log_step, m_ssm_b_re, m_ssm_b_im, m_ssm_c_re, m_ssm_c_im, m_ssm_d, m_ssm_w_glu, m_ssm_b_glu, m_ssm_w_out, v_c_ctx, v_ada_w, v_ada_b, v_ln_g, v_ln_b, v_conv_w_in, v_conv_w, v_conv_w_out, v_ssm_w_in, v_ssm_lam_re, v_ssm_lam_im, v_ssm_log_step, v_ssm_b_re, v_ssm_b_im, v_ssm_c_re, v_ssm_c_im, v_ssm_d, v_ssm_w_glu, v_ssm_b_glu, v_ssm_w_out):
    args = locals()
    wt = {n: args[n] for n in WEIGHTS}
    mt = {n: args["m_" + n] for n in WEIGHTS}
    vt = {n: args["v_" + n] for n in WEIGHTS}

    big_names = list(BIG)
    shards = [_view2d(n, wt[n]).astype(BF) for n in big_names]
    small = jnp.concatenate([wt["conv_w"][0], wt["ssm_d"], wt["ssm_b_glu"]], axis=0)
    small = jnp.concatenate([small, jnp.zeros((3, small.shape[1]), F32)], axis=0)
    gathered = _all_gather(shards + [small], [BIG[n] for n in big_names] + [1], "gather_weights")
    full = dict(zip(big_names, gathered[:-1]))
    small_full = gathered[-1]
    d = x.shape[-1]
    w = {
        "ada_w": full["ada_w"].reshape(2, d, 3 * d), "ada_b": ada_b, "ln_g": ln_g, "ln_b": ln_b,
        "conv_w_in": full["conv_w_in"], "conv_w": small_full[0:3], "conv_w_out": full["conv_w_out"],
        "ssm_w_in": full["ssm_w_in"], "ssm_lam_re": ssm_lam_re[0], "ssm_lam_im": ssm_lam_im[0],
        "ssm_log_step": ssm_log_step[0], "ssm_b_re": ssm_b_re[0], "ssm_b_im": ssm_b_im[0],
        "ssm_c_re": ssm_c_re[0], "ssm_c_im": ssm_c_im[0], "ssm_d": small_full[3],
        "ssm_w_glu": full["ssm_w_glu"], "ssm_b_glu": small_full[4], "ssm_w_out": full["ssm_w_out"],
    }

    loss, grad_x, g = _local_step(x[0], c[0], ctx[0], c_ctx, loss_target[0], w)
    loss = lax.psum(loss, ("x", "y", "c"))

    big_parts = [_view2d(n, g[n]) for n in big_names]
    blob_names = REPLICATED + SMALL_SHARDED
    flat = jnp.concatenate([g[n].reshape(-1).astype(F32) for n in blob_names])
    nflat = flat.shape[0]
    rows = -(-nflat // (N_DEV * 128 * 8)) * 8
    flat = jnp.concatenate([flat, jnp.zeros((N_DEV * rows * 128 - nflat,), F32)]).reshape(N_DEV * rows, 128)
    recv = _all_to_all(big_parts + [flat], [BIG[n] for n in big_names] + [0], "scatter_grads")
    blob_sum = _sum_partials(recv[-1])
    blob = _all_gather([blob_sum], [0], "gather_small_grads")[0].reshape(-1)
    small_g, off = {}, 0
    for n in blob_names:
        shape = wt[n].shape if n in REPLICATED else (*wt[n].shape[:-1], wt[n].shape[-1] * N_DEV)
        size = math.prod(shape)
        small_g[n] = blob[off:off + size].reshape(shape)
        off += size
    me = 4 * lax.axis_index("x") + 2 * lax.axis_index("y") + lax.axis_index("c")
    for n in SMALL_SHARDED:
        size = wt[n].shape[-1]
        small_g[n] = lax.dynamic_slice_in_dim(small_g[n], me * size, size, axis=small_g[n].ndim - 1)

    out_g, out_d, out_m, out_v = {}, {}, {}, {}
    for n, stack in zip(big_names, recv[:-1]):
        shp = wt[n].shape
        res = _adamw(stack, _view2d(n, wt[n]), _view2d(n, mt[n]), _view2d(n, vt[n]), "adamw_" + n)
        out_g[n], out_d[n], out_m[n], out_v[n] = [r.reshape(shp) for r in res]
    names = list(small_g)
    cat = lambda t: jnp.concatenate([t[n].reshape(-1) for n in names])
    gs, ws, ms, vs = cat(small_g), cat(wt), cat(mt), cat(vt)
    ns = gs.shape[0]
    rs = -(-ns // (128 * 512)) * 512
    padr = lambda a: jnp.concatenate([a, jnp.ones((rs * 128 - ns,), F32)]).reshape(rs, 128)
    res = _adamw(padr(gs)[None], padr(ws), padr(ms), padr(vs), "adamw_small")
    off = 0
    for n in names:
        size = math.prod(wt[n].shape)
        out_g[n], out_d[n], out_m[n], out_v[n] = [r.reshape(-1)[off:off + size].reshape(wt[n].shape) for r in res]
        off += size

    return (loss, grad_x[None], *[out_g[n] for n in WEIGHTS], *[out_d[n] for n in WEIGHTS],
            *[out_m[n] for n in WEIGHTS], *[out_v[n] for n in WEIGHTS])
```

```python
import math

import jax
import jax.numpy as jnp
from jax import lax
from jax.experimental import pallas as pl
from jax.experimental.pallas import tpu as pltpu

F32 = jnp.float32
BF = jnp.bfloat16
MESH = pl.DeviceIdType.MESH
N_DEV = 8

GRID_W = 64
CHUNK = 16
S5_P = 16
S5_N = 64
LANE_BLOCK = 128
GROUPS_PER_BLOCK = LANE_BLOCK // S5_P
BCR_W = CHUNK * LANE_BLOCK
ZL_W = 2 * 2 * GROUPS_PER_BLOCK * S5_N
ZH = ZL_W // 4
LN_EPS = 1e-5
DN_ALPHA = 4.0 ** 0.25
ADAM_LR, ADAM_B1, ADAM_B2, ADAM_EPS, ADAM_WD, ADAM_STEP = 1e-3, 0.9, 0.999, 1e-8, 0.01, 10
GELU_C0 = math.sqrt(2.0 / math.pi)
GELU_C1 = 0.044715
VMEM_MB = 52

ANY = pl.BlockSpec(memory_space=pl.ANY)


def _cparams():
    return pltpu.CompilerParams(vmem_limit_bytes=VMEM_MB << 20)


def _dot(a, b):
    return jnp.dot(a, b, preferred_element_type=F32)


def _dot_nt(a, b):
    return lax.dot_general(a, b, (((1,), (1,)), ((), ())), preferred_element_type=F32)


def _dot_tn(a, b):
    return lax.dot_general(a, b, (((0,), (0,)), ((), ())), preferred_element_type=F32)


def _sigmoid(x):
    return 1.0 / (1.0 + jnp.exp(-x))


def _gelu_parts(y):
    th = jnp.tanh(GELU_C0 * (y + GELU_C1 * y * y * y))
    g = 0.5 * y * (1.0 + th)
    dg = 0.5 * (1.0 + th) + 0.5 * y * (1.0 - th * th) * GELU_C0 * (1.0 + 3.0 * GELU_C1 * y * y)
    return g, dg


def _full(shape):
    nd = len(shape)
    return pl.BlockSpec(shape, lambda *_: (0,) * nd)


def _mesh_pos():
    x, y, c = lax.axis_index("x"), lax.axis_index("y"), lax.axis_index("c")
    return x, y, c


def _peer(pos, k):
    x, y, c = pos
    px = 1 - x if (k >> 2) & 1 else x
    py = 1 - y if (k >> 1) & 1 else y
    pc = 1 - c if k & 1 else c
    return (px, py, pc), 4 * px + 2 * py + pc


def _shard_at(ref, axis, idx, n):
    if axis == 0:
        return ref.at[pl.ds(idx * n, n)]
    return ref.at[:, pl.ds(idx * n, n)]


def _all_gather(shards, axes, name):
    n = len(shards)
    out_shape = []
    for s, ax in zip(shards, axes):
        shp = list(s.shape)
        shp[ax] *= N_DEV
        out_shape.append(jax.ShapeDtypeStruct(tuple(shp), s.dtype))

    def body(*refs):
        ins, outs = refs[:n], refs[n:2 * n]
        send_sems, recv_sems, local_sems = refs[2 * n:]
        pos = _mesh_pos()
        me = 4 * pos[0] + 2 * pos[1] + pos[2]
        local, sends = [], []
        for i in range(n):
            size = ins[i].shape[axes[i]]
            cp = pltpu.make_async_copy(ins[i], _shard_at(outs[i], axes[i], me, size), local_sems.at[i])
            cp.start()
            local.append(cp)
            for k in range(1, N_DEV):
                peer, _ = _peer(pos, k)
                cp = pltpu.make_async_remote_copy(
                    src_ref=ins[i], dst_ref=_shard_at(outs[i], axes[i], me, size),
                    send_sem=send_sems.at[i, k - 1], recv_sem=recv_sems.at[i, k - 1],
                    device_id=peer, device_id_type=MESH)
                cp.start()
                sends.append(cp)
        for i in range(n):
            size = ins[i].shape[axes[i]]
            for k in range(1, N_DEV):
                peer, pidx = _peer(pos, k)
                pltpu.make_async_remote_copy(
                    src_ref=ins[i], dst_ref=_shard_at(outs[i], axes[i], pidx, size),
                    send_sem=send_sems.at[i, k - 1], recv_sem=recv_sems.at[i, k - 1],
                    device_id=peer, device_id_type=MESH).wait_recv()
        for cp in sends:
            cp.wait_send()
        for cp in local:
            cp.wait()

    return pl.pallas_call(
        body, name=name, out_shape=out_shape, in_specs=[ANY] * n, out_specs=[ANY] * n,
        scratch_shapes=[pltpu.SemaphoreType.DMA((n, N_DEV - 1)), pltpu.SemaphoreType.DMA((n, N_DEV - 1)),
                        pltpu.SemaphoreType.DMA((n,))],
    )(*shards)


def _all_to_all(parts, axes, name):
    n = len(parts)
    out_shape = []
    for s, ax in zip(parts, axes):
        shp = list(s.shape)
        shp[ax] //= N_DEV
        out_shape.append(jax.ShapeDtypeStruct((N_DEV, *shp), s.dtype))

    def body(*refs):
        ins, outs = refs[:n], refs[n:2 * n]
        send_sems, recv_sems, local_sems = refs[2 * n:]
        pos = _mesh_pos()
        me = 4 * pos[0] + 2 * pos[1] + pos[2]
        local, sends = [], []
        for i in range(n):
            size = ins[i].shape[axes[i]] // N_DEV
            cp = pltpu.make_async_copy(_shard_at(ins[i], axes[i], me, size), outs[i].at[me], local_sems.at[i])
            cp.start()
            local.append(cp)
            for k in range(1, N_DEV):
                peer, pidx = _peer(pos, k)
                cp = pltpu.make_async_remote_copy(
                    src_ref=_shard_at(ins[i], axes[i], pidx, size), dst_ref=outs[i].at[me],
                    send_sem=send_sems.at[i, k - 1], recv_sem=recv_sems.at[i, k - 1],
                    device_id=peer, device_id_type=MESH)
                cp.start()
                sends.append(cp)
        for i in range(n):
            size = ins[i].shape[axes[i]] // N_DEV
            for k in range(1, N_DEV):
                peer, pidx = _peer(pos, k)
                pltpu.make_async_remote_copy(
                    src_ref=_shard_at(ins[i], axes[i], pidx, size), dst_ref=outs[i].at[pidx],
                    send_sem=send_sems.at[i, k - 1], recv_sem=recv_sems.at[i, k - 1],
                    device_id=peer, device_id_type=MESH).wait_recv()
        for cp in sends:
            cp.wait_send()
        for cp in local:
            cp.wait()

    return pl.pallas_call(
        body, name=name, out_shape=out_shape, in_specs=[ANY] * n, out_specs=[ANY] * n,
        scratch_shapes=[pltpu.SemaphoreType.DMA((n, N_DEV - 1)), pltpu.SemaphoreType.DMA((n, N_DEV - 1)),
                        pltpu.SemaphoreType.DMA((n,))],
    )(*parts)


def _ada_fwd(c8, ada_w, ada_b):
    nl, d, d3 = ada_w.shape

    def body(c_ref, w_ref, b_ref, o_ref):
        cv = c_ref[...]
        s = (cv * _sigmoid(cv)).astype(BF)
        o_ref[0] = _dot(s, w_ref[0]) + b_ref[0]

    return pl.pallas_call(
        body, name="ada_fwd", grid=(nl,),
        in_specs=[_full((8, d)), pl.BlockSpec((1, d, d3), lambda l: (l, 0, 0)), pl.BlockSpec((1, 1, d3), lambda l: (l, 0, 0))],
        out_specs=pl.BlockSpec((1, 8, d3), lambda l: (l, 0, 0)),
        out_shape=jax.ShapeDtypeStruct((nl, 8, d3), F32), compiler_params=_cparams(),
    )(c8, ada_w, ada_b.reshape(nl, 1, d3))


def _ada_bwd(c8, ada_w, dm8):
    nl, d, d3 = ada_w.shape

    def body(c_ref, w_ref, dm_ref, dw_ref, dc_ref):
        cv = c_ref[...]
        sg = _sigmoid(cv)
        s = (cv * sg).astype(BF)
        dm = dm_ref[0].astype(BF)
        dw_ref[0] = _dot_tn(s, dm).astype(BF)
        dc_ref[0] = _dot_nt(dm, w_ref[0]) * (sg * (1.0 + cv * (1.0 - sg)))

    return pl.pallas_call(
        body, name="ada_bwd", grid=(nl,),
        in_specs=[_full((8, d)), pl.BlockSpec((1, d, d3), lambda l: (l, 0, 0)), pl.BlockSpec((1, 8, d3), lambda l: (l, 0, 0))],
        out_specs=[pl.BlockSpec((1, d, d3), lambda l: (l, 0, 0)), pl.BlockSpec((1, 8, d), lambda l: (l, 0, 0))],
        out_shape=[jax.ShapeDtypeStruct((nl, d, d3), BF), jax.ShapeDtypeStruct((nl, 8, d), F32)],
        compiler_params=_cparams(),
    )(c8, ada_w, dm8)


def _sum_partials(stack):
    _, r, c = stack.shape

    def body(s_ref, o_ref):
        acc = s_ref[0]
        for p in range(1, N_DEV):
            acc = acc + s_ref[p]
        o_ref[...] = acc

    return pl.pallas_call(body, name="sum_partials", out_shape=jax.ShapeDtypeStruct((r, c), F32),
                          in_specs=[_full(stack.shape)], out_specs=_full((r, c)), grid=(1,),
                          compiler_params=_cparams())(stack)


def _adamw(gstack, w, m, v, name):
    p, r, c = gstack.shape
    tr = r
    for cand in (512 if c <= 256 else 256, 128, 64, 32, 16, 8):
        if r % cand == 0 and r > cand:
            tr = cand
            break
    bc1 = 1.0 - ADAM_B1 ** ADAM_STEP
    bc2 = 1.0 - ADAM_B2 ** ADAM_STEP

    def body(g_ref, w_ref, m_ref, v_ref, go_ref, d_ref, mo_ref, vo_ref):
        g = g_ref[0].astype(F32)
        for q in range(1, p):
            g = g + g_ref[q].astype(F32)
        mn = ADAM_B1 * m_ref[...] + (1.0 - ADAM_B1) * g
        vn = ADAM_B2 * v_ref[...] + (1.0 - ADAM_B2) * (g * g)
        go_ref[...] = g
        mo_ref[...] = mn
        vo_ref[...] = vn
        d_ref[...] = -ADAM_LR * ((mn / bc1) / (jnp.sqrt(vn / bc2) + ADAM_EPS) + ADAM_WD * w_ref[...])

    row = pl.BlockSpec((tr, c), lambda i: (i, 0))
    sds = jax.ShapeDtypeStruct((r, c), F32)
    return pl.pallas_call(
        body, name=name, grid=(r // tr,),
        in_specs=[pl.BlockSpec((p, tr, c), lambda i: (0, i, 0)), row, row, row],
        out_specs=[row, row, row, row], out_shape=[sds, sds, sds, sds], compiler_params=_cparams(),
    )(gstack, w, m, v)


def _lat_or_ctx_specs(tm, d, nl, grid_rank, row_axis):
    def lat(*ids):
        return (jnp.minimum(ids[row_axis], nl - 1), 0)

    def ctx(*ids):
        return (jnp.maximum(ids[row_axis] - nl, 0), 0)

    return pl.BlockSpec((tm, d), lat), pl.BlockSpec((tm, d), ctx)


def _sel_row(ref, is_ctx):
    return jnp.where(is_ctx, ref[1:2, :], ref[0:1, :])


def _inproj0(x, ctx, a2, b2, w, tm):
    l, d = x.shape
    nl, nc = l // tm, ctx.shape[0] // tm
    e = w.shape[1] // 4
    half = e // 2

    def body(x_ref, c_ref, a_ref, b_ref, w_hbm, o_ref, w_ref):
        i = pl.program_id(0)

        @pl.when(i == 0)
        def _():
            pltpu.sync_copy(w_hbm, w_ref)

        is_ctx = i >= nl
        xv = jnp.where(is_ctx, c_ref[...], x_ref[...])
        h = (xv * _sel_row(a_ref, is_ctx) + _sel_row(b_ref, is_ctx)).astype(BF)
        for k in range(4):
            r = _dot(h, w_ref[:, k * e:(k + 1) * e])
            o_ref[k, 0] = r[:, :half].astype(BF)
            o_ref[k, 1] = r[:, half:].astype(BF)

    lat, cx = _lat_or_ctx_specs(tm, d, nl, 1, 0)
    return pl.pallas_call(
        body, name="l0_inproj", grid=(nl + nc,),
        in_specs=[lat, cx, _full((2, d)), _full((2, d)), ANY],
        out_specs=pl.BlockSpec((4, 2, tm, half), lambda i: (0, 0, i, 0)),
        out_shape=jax.ShapeDtypeStruct((4, 2, l + ctx.shape[0], half), BF),
        scratch_shapes=[pltpu.VMEM(w.shape, BF)], compiler_params=_cparams(),
    )(x, ctx, a2, b2, w)


def _conv_taps(u, w_up, w_mid, w_dn, pos, rl, tm):
    up = jnp.where(pos == 0, 0.0, pltpu.roll(u, 1, 0))
    dn = jnp.where(pos == rl - 1, 0.0, pltpu.roll(u, tm - 1, 0))
    return w_up * up + w_mid * u + w_dn * dn, up, dn


def _conv_halo_specs(tm, tc, nl, lead):
    hb = tm // GRID_W

    def prev(j, i):
        return (0, 1, jnp.maximum(jnp.minimum(i, nl - 1) * hb - 1, 0), j)

    def nxt(j, i):
        return (0, 1, jnp.minimum((jnp.minimum(i, nl - 1) + 1) * hb, nl * hb - 1), j)

    return pl.BlockSpec((lead, 1, GRID_W, tc), prev), pl.BlockSpec((lead, 1, GRID_W, tc), nxt)


def _conv_fwd(p42, cw, nl, tm, tc):
    _, _, r, half = p42.shape
    nt = r // tm

    def body(p_ref, hp_ref, hn_ref, cw_ref, o_ref):
        i = pl.program_id(1)
        is_ctx = i >= nl
        row = lax.broadcasted_iota(jnp.int32, (tm, tc), 0)
        rl = jnp.where(is_ctx, tm, GRID_W)
        pos = jnp.bitwise_and(row, rl - 1)

        def gate(hv, yc):
            bg = p_ref[0, hv].astype(F32)
            z = p_ref[3, hv].astype(F32)
            return (bg * yc * (z * _sigmoid(z))).astype(BF)

        u_h = p_ref[1, 0].astype(F32) * p_ref[2, 0].astype(F32)
        w_h = cw_ref[:, 0, :]
        o_ref[0] = gate(0, _conv_taps(u_h, w_h[0:1], w_h[1:2], w_h[2:3], pos, rl, tm)[0])
        u_v = p_ref[1, 1].astype(F32) * p_ref[2, 1].astype(F32)
        w_v = cw_ref[:, 1, :]

        @pl.when(is_ctx)
        def _():
            o_ref[1] = gate(1, _conv_taps(u_v, w_v[0:1], w_v[1:2], w_v[2:3], pos, rl, tm)[0])

        @pl.when(jnp.logical_not(is_ctx))
        def _():
            up = hp_ref[1, 0].astype(F32) * hp_ref[2, 0].astype(F32) * (i > 0).astype(F32)
            dn = hn_ref[1, 0].astype(F32) * hn_ref[2, 0].astype(F32) * (i < nl - 1).astype(F32)
            ext = jnp.concatenate([up, u_v, dn], axis=0)
            yc = w_v[0:1] * ext[0:tm] + w_v[1:2] * u_v + w_v[2:3] * ext[2 * GRID_W:tm + 2 * GRID_W]
            o_ref[1] = gate(1, yc)

    hp, hn = _conv_halo_specs(tm, tc, nl, 4)
    return pl.pallas_call(
        body, name="l0_conv_fwd", grid=(half // tc, nt),
        in_specs=[pl.BlockSpec((4, 2, tm, tc), lambda j, i: (0, 0, i, j)), hp, hn,
                  pl.BlockSpec((3, 2, tc), lambda j, i: (0, 0, j))],
        out_specs=pl.BlockSpec((2, tm, tc), lambda j, i: (0, i, j)),
        out_shape=jax.ShapeDtypeStruct((2, r, half), BF), compiler_params=_cparams(),
    )(p42, p42, p42, cw)


def _outproj_ln0(q3, w_out, x, ctx, gt2, tm):
    l, d = x.shape
    nl, nc = l // tm, ctx.shape[0] // tm
    _, r, half = q3.shape

    def body(q_ref, w_hbm, x_ref, c_ref, g_ref, xh_ref, rs_ref, fx_ref, w_ref):
        i = pl.program_id(0)

        @pl.when(i == 0)
        def _():
            pltpu.sync_copy(w_hbm, w_ref)

        is_ctx = i >= nl
        fx = _dot(q_ref[0], w_ref[:half, :]) + _dot(q_ref[1], w_ref[half:, :])
        xv = jnp.where(is_ctx, c_ref[...], x_ref[...])
        rr = DN_ALPHA * xv + _sel_row(g_ref, is_ctx) * fx
        mu = jnp.mean(rr, axis=-1, keepdims=True)
        cen = rr - mu
        rstd = lax.rsqrt(jnp.mean(cen * cen, axis=-1, keepdims=True) + LN_EPS)
        xh_ref[...] = cen * rstd
        rs_ref[...] = jnp.broadcast_to(rstd, (tm, 128))
        fx_ref[...] = fx.astype(BF)

    lat, cx = _lat_or_ctx_specs(tm, d, nl, 1, 0)
    return pl.pallas_call(
        body, name="l0_outproj_ln", grid=(nl + nc,),
        in_specs=[pl.BlockSpec((2, tm, half), lambda i: (0, i, 0)), ANY, lat, cx, _full((2, d))],
        out_specs=[pl.BlockSpec((tm, d), lambda i: (i, 0)), pl.BlockSpec((tm, 128), lambda i: (i, 0)),
                   pl.BlockSpec((tm, d), lambda i: (i, 0))],
        out_shape=[jax.ShapeDtypeStruct((r, d), F32), jax.ShapeDtypeStruct((r, 128), F32), jax.ShapeDtypeStruct((r, d), BF)],
        scratch_shapes=[pltpu.VMEM(w_out.shape, BF)], compiler_params=_cparams(),
    )(q3, w_out, x, ctx, gt2)


def _bwd_outproj0(dr_l, dr_c, gt2, w_out, fx, tm):
    l, d = dr_l.shape
    nl, nc = l // tm, dr_c.shape[0] // tm
    e = w_out.shape[0]
    half = e // 2
    r = l + dr_c.shape[0]

    def body(dl_ref, dc_ref, g_ref, w_hbm, fx_ref, dq_ref, acc_ref, w_ref):
        i = pl.program_id(0)

        @pl.when(i == 0)
        def _():
            pltpu.sync_copy(w_hbm, w_ref)
            acc_ref[...] = jnp.zeros_like(acc_ref)

        is_ctx = i >= nl
        dr = jnp.where(is_ctx, dc_ref[...], dl_ref[...]).astype(F32)
        dfx = (dr * _sel_row(g_ref, is_ctx)).astype(BF)
        dq_ref[0] = _dot_nt(dfx, w_ref[:half, :]).astype(BF)
        dq_ref[1] = _dot_nt(dfx, w_ref[half:, :]).astype(BF)
        s = jnp.sum(dr * fx_ref[...].astype(F32), axis=0, keepdims=True)
        sel = is_ctx.astype(F32)
        acc_ref[0:1, :] += s * (1.0 - sel)
        acc_ref[1:2, :] += s * sel

    lat, cx = _lat_or_ctx_specs(tm, d, nl, 1, 0)
    return pl.pallas_call(
        body, name="l0_bwd_outproj", grid=(nl + nc,),
        in_specs=[lat, cx, _full((2, d)), ANY, pl.BlockSpec((tm, d), lambda i: (i, 0))],
        out_specs=[pl.BlockSpec((2, tm, half), lambda i: (0, i, 0)), _full((8, d))],
        out_shape=[jax.ShapeDtypeStruct((2, r, half), BF), jax.ShapeDtypeStruct((8, d), F32)],
        scratch_shapes=[pltpu.VMEM(w_out.shape, BF)], compiler_params=_cparams(),
    )(dr_l, dr_c, gt2, w_out, fx)


def _conv_bwd(dq3, p42, cw, nl, tm, tc):
    _, _, r, half = p42.shape
    nt = r // tm

    def body(dq_ref, dqp_ref, dqn_ref, p_ref, hp_ref, hn_ref, cw_ref, dp_ref, dw_ref):
        i = pl.program_id(1)
        is_ctx = i >= nl

        @pl.when(i == 0)
        def _():
            dw_ref[...] = jnp.zeros_like(dw_ref)

        row = lax.broadcasted_iota(jnp.int32, (tm, tc), 0)
        rl = jnp.where(is_ctx, tm, GRID_W)
        pos = jnp.bitwise_and(row, rl - 1)

        def pieces(dq, bg, z):
            sz = _sigmoid(z)
            sil = z * sz
            return dq * bg * sil, dq * sil, dq * bg * (sz * (1.0 + z * (1.0 - sz)))

        def seq_half(hv):
            bg, cg = p_ref[0, hv].astype(F32), p_ref[1, hv].astype(F32)
            v, z = p_ref[2, hv].astype(F32), p_ref[3, hv].astype(F32)
            w = cw_ref[:, hv, :]
            u = cg * v
            yc, u_up, u_dn = _conv_taps(u, w[0:1], w[1:2], w[2:3], pos, rl, tm)
            dyc, dbg_f, dz_f = pieces(dq_ref[hv].astype(F32), bg, z)
            du = _conv_taps(dyc, w[2:3], w[1:2], w[0:1], pos, rl, tm)[0]
            dp_ref[0, hv] = (dbg_f * yc).astype(BF)
            dp_ref[1, hv] = (du * v).astype(BF)
            dp_ref[2, hv] = (du * cg).astype(BF)
            dp_ref[3, hv] = (dz_f * yc).astype(BF)
            dw_ref[0:1, hv, :] += jnp.sum(dyc * u_up, axis=0, keepdims=True)
            dw_ref[1:2, hv, :] += jnp.sum(dyc * u, axis=0, keepdims=True)
            dw_ref[2:3, hv, :] += jnp.sum(dyc * u_dn, axis=0, keepdims=True)

        seq_half(0)

        @pl.when(is_ctx)
        def _():
            seq_half(1)

        @pl.when(jnp.logical_not(is_ctx))
        def _():
            bg, cg = p_ref[0, 1].astype(F32), p_ref[1, 1].astype(F32)
            v, z = p_ref[2, 1].astype(F32), p_ref[3, 1].astype(F32)
            w = cw_ref[:, 1, :]
            u = cg * v
            m_up = (i > 0).astype(F32)
            m_dn = (i < nl - 1).astype(F32)

            def halo(h_ref, dqh_ref, msk):
                hb, hc = h_ref[0, 0].astype(F32), h_ref[1, 0].astype(F32)
                hv_, hz = h_ref[2, 0].astype(F32), h_ref[3, 0].astype(F32)
                return hc * hv_ * msk, pieces(dqh_ref[0].astype(F32), hb, hz)[0] * msk

            u_p, dyc_p = halo(hp_ref, dqp_ref, m_up)
            u_n, dyc_n = halo(hn_ref, dqn_ref, m_dn)
            u_ext = jnp.concatenate([u_p, u, u_n], axis=0)
            u_up, u_dn = u_ext[0:tm], u_ext[2 * GRID_W:tm + 2 * GRID_W]
            yc = w[0:1] * u_up + w[1:2] * u + w[2:3] * u_dn
            dyc, dbg_f, dz_f = pieces(dq_ref[1].astype(F32), bg, z)
            d_ext = jnp.concatenate([dyc_p, dyc, dyc_n], axis=0)
            du = w[0:1] * d_ext[2 * GRID_W:tm + 2 * GRID_W] + w[1:2] * dyc + w[2:3] * d_ext[0:tm]
            dp_ref[0, 1] = (dbg_f * yc).astype(BF)
            dp_ref[1, 1] = (du * v).astype(BF)
            dp_ref[2, 1] = (du * cg).astype(BF)
            dp_ref[3, 1] = (dz_f * yc).astype(BF)
            dw_ref[0:1, 1, :] += jnp.sum(dyc * u_up, axis=0, keepdims=True)
            dw_ref[1:2, 1, :] += jnp.sum(dyc * u, axis=0, keepdims=True)
            dw_ref[2:3, 1, :] += jnp.sum(dyc * u_dn, axis=0, keepdims=True)

    hb = tm // GRID_W

    def dq_prev(j, i):
        return (1, jnp.maximum(jnp.minimum(i, nl - 1) * hb - 1, 0), j)

    def dq_next(j, i):
        return (1, jnp.minimum((jnp.minimum(i, nl - 1) + 1) * hb, nl * hb - 1), j)

    hp, hn = _conv_halo_specs(tm, tc, nl, 4)
    return pl.pallas_call(
        body, name="l0_conv_bwd", grid=(half // tc, nt),
        in_specs=[pl.BlockSpec((2, tm, tc), lambda j, i: (0, i, j)),
                  pl.BlockSpec((1, GRID_W, tc), dq_prev), pl.BlockSpec((1, GRID_W, tc), dq_next),
                  pl.BlockSpec((4, 2, tm, tc), lambda j, i: (0, 0, i, j)), hp, hn,
                  pl.BlockSpec((3, 2, tc), lambda j, i: (0, 0, j))],
        out_specs=[pl.BlockSpec((4, 2, tm, tc), lambda j, i: (0, 0, i, j)), pl.BlockSpec((8, 2, tc), lambda j, i: (0, 0, j))],
        out_shape=[jax.ShapeDtypeStruct(p42.shape, BF), jax.ShapeDtypeStruct((8, 2, half), F32)],
        compiler_params=_cparams(),
    )(dq3, dq3, dq3, p42, p42, p42, cw)


def _bwd_inproj0(dp42, w_in, x, ctx, dr_l, dr_c, a2, tm):
    l, d = x.shape
    nl, nc = l // tm, ctx.shape[0] // tm
    e = w_in.shape[1] // 4
    half = e // 2

    def body(dp_ref, w_hbm, x_ref, c_ref, dl_ref, dc_ref, a_ref, gx_ref, acc_ref, w_ref):
        i = pl.program_id(0)

        @pl.when(i == 0)
        def _():
            pltpu.sync_copy(w_hbm, w_ref)
            acc_ref[...] = jnp.zeros_like(acc_ref)

        is_ctx = i >= nl
        dh = jnp.zeros((tm, d), F32)
        for k in range(4):
            for hv in range(2):
                c0 = k * e + hv * half
                dh = dh + _dot_nt(dp_ref[k, hv], w_ref[:, c0:c0 + half])
        xv = jnp.where(is_ctx, c_ref[...], x_ref[...])
        s_sc = jnp.sum(dh * xv, axis=0, keepdims=True)
        s_sh = jnp.sum(dh, axis=0, keepdims=True)
        sel = is_ctx.astype(F32)
        acc_ref[0:1, :] += s_sc * (1.0 - sel)
        acc_ref[1:2, :] += s_sc * sel
        acc_ref[2:3, :] += s_sh * (1.0 - sel)
        acc_ref[3:4, :] += s_sh * sel

        @pl.when(jnp.logical_not(is_ctx))
        def _():
            gx_ref[...] = DN_ALPHA * dl_ref[...].astype(F32) + dh * a_ref[0:1, :]

    lat, cx = _lat_or_ctx_specs(tm, d, nl, 1, 0)
    return pl.pallas_call(
        body, name="l0_bwd_inproj", grid=(nl + nc,),
        in_specs=[pl.BlockSpec((4, 2, tm, half), lambda i: (0, 0, i, 0)), ANY, lat, cx, lat, cx, _full((2, d))],
        out_specs=[pl.BlockSpec((tm, d), lambda i: (jnp.minimum(i, nl - 1), 0)), _full((8, d))],
        out_shape=[jax.ShapeDtypeStruct((l, d), F32), jax.ShapeDtypeStruct((8, d), F32)],
        scratch_shapes=[pltpu.VMEM(w_in.shape, BF)], compiler_params=_cparams(),
    )(dp42, w_in, x, ctx, dr_l, dr_c, a2)


def _dw_inproj0(x, ctx, a2, b2, dp42, tm):
    l, d = x.shape
    lc = ctx.shape[0]
    assert lc == tm
    tl = 4 * tm if l % (4 * tm) == 0 else tm
    nl = l // tl
    half = dp42.shape[-1]
    e = 2 * half

    def body(x_ref, c_ref, a_ref, b_ref, dpl_ref, dpc_ref, o_ref, acc_ref):
        i = pl.program_id(1)

        @pl.when(i == 0)
        def _():
            acc_ref[...] = jnp.zeros_like(acc_ref)

        def add(rows_ref, dp_ref, sel):
            h = (rows_ref[...] * a_ref[sel:sel + 1, :] + b_ref[sel:sel + 1, :]).astype(BF)
            acc_ref[:, :half] += _dot_tn(h, dp_ref[0, 0])
            acc_ref[:, half:] += _dot_tn(h, dp_ref[0, 1])

        @pl.when(i < nl)
        def _():
            add(x_ref, dpl_ref, 0)

        @pl.when(i == nl)
        def _():
            add(c_ref, dpc_ref, 1)
            o_ref[...] = acc_ref[...].astype(BF)

    return pl.pallas_call(
        body, name="l0_dw_inproj", grid=(4, nl + 1),
        in_specs=[pl.BlockSpec((tl, d), lambda k, i: (jnp.minimum(i, nl - 1), 0)), _full((lc, d)),
                  _full((2, d)), _full((2, d)),
                  pl.BlockSpec((1, 2, tl, half), lambda k, i: (k, 0, jnp.minimum(i, nl - 1), 0)),
                  pl.BlockSpec((1, 2, lc, half), lambda k, i: (k, 0, l // lc, 0))],
        out_specs=pl.BlockSpec((d, e), lambda k, i: (0, k)),
        out_shape=jax.ShapeDtypeStruct((d, 4 * e), BF),
        scratch_shapes=[pltpu.VMEM((d, e), F32)], compiler_params=_cparams(),
    )(x, ctx, a2, b2, dp42, dp42)


def _dw_outproj0(q3, dr_l, dr_c, gt2, tm):
    l, d = dr_l.shape
    nl, nc = l // tm, dr_c.shape[0] // tm
    _, r, half = q3.shape
    nt = nl + nc

    def body(q_ref, dl_ref, dc_ref, g_ref, o_ref, acc_ref):
        i = pl.program_id(0)
        is_ctx = i >= nl

        @pl.when(i == 0)
        def _():
            acc_ref[...] = jnp.zeros_like(acc_ref)

        dr = jnp.where(is_ctx, dc_ref[...], dl_ref[...]).astype(F32)
        dfx = (dr * _sel_row(g_ref, is_ctx)).astype(BF)
        acc_ref[:half, :] += _dot_tn(q_ref[0], dfx)
        acc_ref[half:, :] += _dot_tn(q_ref[1], dfx)

        @pl.when(i == nt - 1)
        def _():
            o_ref[...] = acc_ref[...].astype(BF)

    lat, cx = _lat_or_ctx_specs(tm, d, nl, 1, 0)
    return pl.pallas_call(
        body, name="l0_dw_outproj", grid=(nt,),
        in_specs=[pl.BlockSpec((2, tm, half), lambda i: (0, i, 0)), lat, cx, _full((2, d))],
        out_specs=_full((2 * half, d)), out_shape=jax.ShapeDtypeStruct((2 * half, d), BF),
        scratch_shapes=[pltpu.VMEM((2 * half, d), F32)], compiler_params=_cparams(),
    )(q3, dr_l, dr_c, gt2)


def _cr_tile(j, cap=256):
    for cand in (1024, 512, 256, 128, 64, 32, 16, 8):
        if cand <= cap and j % cand == 0:
            return cand
    raise ValueError(j)


def _inproj1(xh_cr, a1, b1, w, tag):
    j, d16 = xh_cr.shape
    d = d16 // CHUNK
    e = w.shape[1] // 2
    nb = e // LANE_BLOCK
    tj = _cr_tile(j)

    def body(x_ref, a_ref, b_ref, w_hbm, u_ref, z_ref, w_ref):
        @pl.when(jnp.logical_and(pl.program_id(0) == 0, pl.program_id(1) == 0))
        def _():
            pltpu.sync_copy(w_hbm, w_ref)

        h = (x_ref[...] * a_ref[...] + b_ref[...]).astype(BF)
        r = _dot(h, w_ref[...])
        for b in range(nb):
            u_ref[b] = r[:, b * LANE_BLOCK:(b + 1) * LANE_BLOCK].astype(BF)
        z_ref[...] = r[:, e:].astype(BF)

    return pl.pallas_call(
        body, name="l1_inproj_" + tag, grid=(j // tj, CHUNK),
        in_specs=[pl.BlockSpec((tj, d), lambda t, s: (t, s)), _full((1, d)), _full((1, d)), ANY],
        out_specs=[pl.BlockSpec((nb, tj, LANE_BLOCK), lambda t, s: (0, t, s)), pl.BlockSpec((tj, e), lambda t, s: (t, s))],
        out_shape=[jax.ShapeDtypeStruct((nb, j, BCR_W), BF), jax.ShapeDtypeStruct((j, CHUNK * e), BF)],
        scratch_shapes=[pltpu.VMEM(w.shape, BF)], compiler_params=_cparams(),
    )(xh_cr, a1, b1, w)


def _bmm(a_list, w_list, trans, out_dtype, name):
    nb, j, ka = a_list[0].shape
    n_out = w_list[0].shape[1] if trans[0] else w_list[0].shape[2]
    tn = n_out // 2
    tj = _cr_tile(j, 512)
    n = len(a_list)

    def body(*refs):
        o_ref = refs[2 * n]
        acc = None
        for i in range(n):
            a = refs[i][0].astype(BF)
            w = refs[n + i][0]
            t = _dot_nt(a, w) if trans[i] else _dot(a, w)
            acc = t if acc is None else acc + t
        o_ref[0] = acc.astype(out_dtype)

    a_specs = [pl.BlockSpec((1, tj, a.shape[2]), lambda b, h, t: (b, t, 0)) for a in a_list]
    w_specs = [pl.BlockSpec((1, tn, w.shape[2]), lambda b, h, t: (b, h, 0)) if tr
               else pl.BlockSpec((1, w.shape[1], tn), lambda b, h, t: (b, 0, h)) for w, tr in zip(w_list, trans)]
    return pl.pallas_call(
        body, name=name, grid=(nb, 2, j // tj), in_specs=a_specs + w_specs,
        out_specs=pl.BlockSpec((1, tj, tn), lambda b, h, t: (b, t, h)),
        out_shape=jax.ShapeDtypeStruct((nb, j, n_out), out_dtype), compiler_params=_cparams(),
    )(*a_list, *w_list)


def _group_mask(lane_groups):
    row = lax.broadcasted_iota(jnp.int32, (LANE_BLOCK, LANE_BLOCK), 0) // S5_P
    lane = lax.broadcasted_iota(jnp.int32, (LANE_BLOCK, LANE_BLOCK), 1)
    return row == lane_groups(lane)


def _expand_toeplitz(wcomp):
    nb = wcomp.shape[0]
    nd = 2 * CHUNK - 1

    def body(c_ref, o_ref):
        mask = _group_mask(lambda lane: lane // S5_P)
        tiles = [jnp.where(mask, c_ref[0, dd], 0.0).astype(BF) for dd in range(nd)]
        for s in range(CHUNK):
            for t in range(CHUNK):
                o_ref[0, s * LANE_BLOCK:(s + 1) * LANE_BLOCK, t * LANE_BLOCK:(t + 1) * LANE_BLOCK] = tiles[t - s + CHUNK - 1]

    return pl.pallas_call(
        body, name="l1_expand_toeplitz", grid=(nb,),
        in_specs=[pl.BlockSpec((1, nd, LANE_BLOCK, LANE_BLOCK), lambda b: (b, 0, 0, 0))],
        out_specs=pl.BlockSpec((1, BCR_W, BCR_W), lambda b: (b, 0, 0)),
        out_shape=jax.ShapeDtypeStruct((nb, BCR_W, BCR_W), BF), compiler_params=_cparams(),
    )(wcomp)


def _expand_blocks(comp, name):
    nb = comp.shape[0]
    lanes_per_dir = ZL_W // 2

    def body(c_ref, o_ref):
        masks = [_group_mask(lambda lane, lb=lb: 2 * lb + lane // S5_N) for lb in range(4)]
        for r in range(2):
            for s in range(CHUNK):
                for ri in range(2):
                    m = c_ref[0, r, s, :, ri * S5_N:(ri + 1) * S5_N]
                    mm = jnp.concatenate([m, m], axis=1)
                    for lb in range(4):
                        c0 = r * lanes_per_dir + ri * ZH + lb * LANE_BLOCK
                        o_ref[0, s * LANE_BLOCK:(s + 1) * LANE_BLOCK, c0:c0 + LANE_BLOCK] = (
                            jnp.where(masks[lb], mm, 0.0).astype(BF))

    return pl.pallas_call(
        body, name=name, grid=(nb,),
        in_specs=[pl.BlockSpec((1, 2, CHUNK, LANE_BLOCK, LANE_BLOCK), lambda b: (b, 0, 0, 0, 0))],
        out_specs=pl.BlockSpec((1, BCR_W, ZL_W), lambda b: (b, 0, 0)),
        out_shape=jax.ShapeDtypeStruct((nb, BCR_W, ZL_W), BF), compiler_params=_cparams(),
    )(comp)


def _bdw(a, b_, kind, init, name):
    nb, j, ka = a.shape
    kb = b_.shape[2]
    tn = kb // 2
    tj = _cr_tile(j, 1024)
    nt = j // tj
    has_init = init is not None
    nd = 2 * CHUNK - 1

    def body(*refs):
        a_ref, b_ref = refs[0], refs[1]
        o_ref, acc_ref = refs[-2], refs[-1]
        h, t = pl.program_id(1), pl.program_id(2)

        @pl.when(t == 0)
        def _():
            acc_ref[...] = jnp.zeros_like(acc_ref)

        acc_ref[...] += _dot_tn(a_ref[0].astype(BF), b_ref[0].astype(BF))

        if kind == "toeplitz":
            @pl.when(jnp.logical_and(t == 0, h == 0))
            def _():
                o_ref[...] = jnp.zeros_like(o_ref)

            @pl.when(t == nt - 1)
            def _():
                mask = _group_mask(lambda lane: lane // S5_P)
                for s in range(CHUNK):
                    for tl in range(CHUNK // 2):
                        dd = h * (CHUNK // 2) + (tl - s + CHUNK - 1)
                        blk = acc_ref[s * LANE_BLOCK:(s + 1) * LANE_BLOCK, tl * LANE_BLOCK:(tl + 1) * LANE_BLOCK]
                        o_ref[0, dd] += jnp.where(mask, blk, 0.0)
        else:
            @pl.when(t == nt - 1)
            def _():
                masks = [_group_mask(lambda lane, lb=lb: 2 * lb + lane // S5_N) for lb in range(4)]
                for s in range(CHUNK):
                    for ri in range(2):
                        v = None
                        for lb in range(4):
                            c0 = ri * ZH + lb * LANE_BLOCK
                            blk = acc_ref[s * LANE_BLOCK:(s + 1) * LANE_BLOCK, c0:c0 + LANE_BLOCK]
                            blk = jnp.where(masks[lb], blk, 0.0)
                            v = blk if v is None else v + blk
                        folded = v[:, :S5_N] + v[:, S5_N:]
                        if has_init:
                            folded = folded + refs[2][0, 0, s, :, ri * S5_N:(ri + 1) * S5_N]
                        o_ref[0, 0, s, :, ri * S5_N:(ri + 1) * S5_N] = folded

    in_specs = [pl.BlockSpec((1, tj, ka), lambda b, h, t: (b, t, 0)), pl.BlockSpec((1, tj, tn), lambda b, h, t: (b, t, h))]
    args = [a, b_]
    if kind == "toeplitz":
        ospec = pl.BlockSpec((1, nd, LANE_BLOCK, LANE_BLOCK), lambda b, h, t: (b, 0, 0, 0))
        oshape = jax.ShapeDtypeStruct((nb, nd, LANE_BLOCK, LANE_BLOCK), F32)
    else:
        ospec = pl.BlockSpec((1, 1, CHUNK, LANE_BLOCK, LANE_BLOCK), lambda b, h, t: (b, h, 0, 0, 0))
        oshape = jax.ShapeDtypeStruct((nb, 2, CHUNK, LANE_BLOCK, LANE_BLOCK), F32)
        if has_init:
            in_specs.append(ospec)
            args.append(init)
    return pl.pallas_call(
        body, name=name, grid=(nb, 2, nt), in_specs=in_specs, out_specs=ospec, out_shape=oshape,
        scratch_shapes=[pltpu.VMEM((ka, tn), F32)], compiler_params=_cparams(),
    )(*args)


def _scan(z_l, z_c, coef, chains, conj, s_l=None, s_c=None, name="l1_scan"):
    nb, jl, _ = z_l.shape
    jc = z_c.shape[1]
    with_da = s_l is not None
    sign = -1.0 if conj else 1.0
    hw = 2 * ZH

    def body(*refs):
        zl_ref, zc_ref, cf_ref = refs[:3]
        k = 3
        if with_da:
            sl_ref, sc_ref = refs[3:5]
            k = 5
        ol_ref, oc_ref = refs[k:k + 2]
        d = pl.program_id(1)
        rowi = lax.broadcasted_iota(jnp.int32, (8, ZH), 0)

        def coef_rows(r0, nr):
            return cf_ref[0, 0, r0:r0 + nr, :ZH], sign * cf_ref[0, 0, r0:r0 + nr, ZH:]

        steps = [(1, coef_rows(0, 1)), (2, coef_rows(1, 1)), (4, coef_rows(2, 1))]

        def run(chain):
            carry = (jnp.zeros((1, ZH), F32), jnp.zeros((1, ZH), F32))
            da = (jnp.zeros((8, ZH), F32), jnp.zeros((8, ZH), F32))
            for which, rev in chain:
                src, dst = (zc_ref, oc_ref) if which == "c" else (zl_ref, ol_ref)
                sref = (sc_ref if which == "c" else sl_ref) if with_da else None
                ng = (jc if which == "c" else jl) // 8
                tr, ti = coef_rows(16, 8) if rev else coef_rows(8, 8)

                def step(it, st, src=src, dst=dst, sref=sref, ng=ng, tr=tr, ti=ti, rev=rev):
                    cr_, ci_, dar, dai = st
                    g = (ng - 1 - it) if rev else it
                    off = pl.multiple_of(g * 8, 8)
                    xr = src[0, pl.ds(off, 8), :ZH]
                    xi = src[0, pl.ds(off, 8), ZH:]
                    for sh, (ar, ai) in steps:
                        if rev:
                            keep = rowi < 8 - sh
                            sr = jnp.where(keep, pltpu.roll(xr, 8 - sh, 0), 0.0)
                            si = jnp.where(keep, pltpu.roll(xi, 8 - sh, 0), 0.0)
                        else:
                            keep = rowi >= sh
                            sr = jnp.where(keep, pltpu.roll(xr, sh, 0), 0.0)
                            si = jnp.where(keep, pltpu.roll(xi, sh, 0), 0.0)
                        xr, xi = xr + ar * sr - ai * si, xi + ar * si + ai * sr
                    ir = xr + tr * cr_ - ti * ci_
                    ii = xi + tr * ci_ + ti * cr_
                    if rev:
                        er = jnp.where(rowi == 7, cr_, pltpu.roll(ir, 7, 0))
                        ei = jnp.where(rowi == 7, ci_, pltpu.roll(ii, 7, 0))
                        ncr, nci = ir[0:1], ii[0:1]
                    else:
                        er = jnp.where(rowi == 0, cr_, pltpu.roll(ir, 1, 0))
                        ei = jnp.where(rowi == 0, ci_, pltpu.roll(ii, 1, 0))
                        ncr, nci = ir[7:8], ii[7:8]
                    dst[0, pl.ds(off, 8), :ZH] = er
                    dst[0, pl.ds(off, 8), ZH:] = ei
                    if sref is not None:
                        s_r = sref[0, pl.ds(off, 8), :ZH]
                        s_i = sref[0, pl.ds(off, 8), ZH:]
                        dar = dar + s_r * er + s_i * ei
                        dai = dai + s_r * ei - s_i * er
                    return ncr, nci, dar, dai

                carry_da = lax.fori_loop(0, ng, step, (*carry, *da))
                carry, da = carry_da[:2], carry_da[2:]
            if with_da:
                refs[k + 2][0, 0] = jnp.concatenate([da[0], da[1]], axis=1)

        for dd in range(2):
            @pl.when(d == dd)
            def _(dd=dd):
                run(chains[dd])

    zspec_l = pl.BlockSpec((1, jl, hw), lambda b, d: (b, 0, d))
    zspec_c = pl.BlockSpec((1, jc, hw), lambda b, d: (b, 0, d))
    in_specs = [zspec_l, zspec_c, pl.BlockSpec((1, 1, 24, hw), lambda b, d: (b, d, 0, 0))]
    args = [z_l, z_c, coef]
    out_specs = [zspec_l, zspec_c]
    out_shape = [jax.ShapeDtypeStruct(z_l.shape, F32), jax.ShapeDtypeStruct(z_c.shape, F32)]
    if with_da:
        in_specs += [zspec_l, zspec_c]
        args += [s_l, s_c]
        out_specs.append(pl.BlockSpec((1, 1, 8, hw), lambda b, d: (b, d, 0, 0)))
        out_shape.append(jax.ShapeDtypeStruct((nb, 2, 8, hw), F32))
    return pl.pallas_call(body, name=name, grid=(nb, 2), in_specs=in_specs, out_specs=out_specs,
                          out_shape=out_shape, compiler_params=_cparams())(*args)


def _glu_fwd(y_bcr, z_cr, w_glu, b_glu):
    nb, j, _ = y_bcr.shape
    e = nb * LANE_BLOCK
    tj = _cr_tile(j)

    def body(y_ref, z_ref, w_hbm, b_ref, o_ref, sg_ref, w_ref):
        @pl.when(jnp.logical_and(pl.program_id(0) == 0, pl.program_id(1) == 0))
        def _():
            pltpu.sync_copy(w_hbm, w_ref)

        y = jnp.concatenate([y_ref[b] for b in range(nb)], axis=1).astype(F32)
        g = _gelu_parts(y)[0]
        sg = _sigmoid(_dot(g.astype(BF), w_ref[...]) + b_ref[...])
        z = z_ref[...].astype(F32)
        o_ref[...] = (g * sg * (z * _sigmoid(z))).astype(BF)
        sg_ref[...] = sg.astype(BF)

    tok = pl.BlockSpec((tj, e), lambda t, s: (t, s))
    return pl.pallas_call(
        body, name="l1_glu_fwd", grid=(j // tj, CHUNK),
        in_specs=[pl.BlockSpec((nb, tj, LANE_BLOCK), lambda t, s: (0, t, s)), tok, ANY, _full((1, e))],
        out_specs=[tok, tok],
        out_shape=[jax.ShapeDtypeStruct((j, CHUNK * e), BF), jax.ShapeDtypeStruct((j, CHUNK * e), BF)],
        scratch_shapes=[pltpu.VMEM(w_glu.shape, BF)], compiler_params=_cparams(),
    )(y_bcr, z_cr, w_glu, b_glu)


def _final(w_cr, w_out, xh_cr, tgt_cr, vecs):
    j, e16 = w_cr.shape
    e = e16 // CHUNK
    d = w_out.shape[1]
    tj = _cr_tile(j)

    def body(w_ref, wo_hbm, xh_ref, t_ref, v_ref, dr_ref, acc_ref, wo_ref):
        @pl.when(jnp.logical_and(pl.program_id(0) == 0, pl.program_id(1) == 0))
        def _():
            pltpu.sync_copy(wo_hbm, wo_ref)
            acc_ref[...] = jnp.zeros_like(acc_ref)

        o = _dot(w_ref[...], wo_ref[...])
        x1 = xh_ref[...] * v_ref[0:1, :] + v_ref[1:2, :]
        rr = DN_ALPHA * x1 + v_ref[2:3, :] * o
        mu = jnp.mean(rr, axis=-1, keepdims=True)
        cen = rr - mu
        rstd = lax.rsqrt(jnp.mean(cen * cen, axis=-1, keepdims=True) + LN_EPS)
        xh2 = cen * rstd
        err = xh2 * v_ref[3:4, :] + v_ref[4:5, :] - t_ref[...]
        dy = err * (1.0 / d)
        dxh = dy * v_ref[3:4, :]
        dr = rstd * (dxh - jnp.mean(dxh, axis=-1, keepdims=True) - xh2 * jnp.mean(dxh * xh2, axis=-1, keepdims=True))
        dr_ref[...] = dr.astype(BF)
        acc_ref[0:1, :] += jnp.sum(dy * xh2, axis=0, keepdims=True)
        acc_ref[1:2, :] += jnp.sum(dy, axis=0, keepdims=True)
        acc_ref[2:3, :] += jnp.sum(dr * o, axis=0, keepdims=True)
        acc_ref[3:4, :] += (0.5 / d) * jnp.sum(err * err, axis=0, keepdims=True)

    tok_d = pl.BlockSpec((tj, d), lambda t, s: (t, s))
    return pl.pallas_call(
        body, name="l1_final", grid=(j // tj, CHUNK),
        in_specs=[pl.BlockSpec((tj, e), lambda t, s: (t, s)), ANY, tok_d, tok_d, _full((8, d))],
        out_specs=[tok_d, _full((8, d))],
        out_shape=[jax.ShapeDtypeStruct((j, CHUNK * d), BF), jax.ShapeDtypeStruct((8, d), F32)],
        scratch_shapes=[pltpu.VMEM(w_out.shape, BF)], compiler_params=_cparams(),
    )(w_cr, w_out, xh_cr, tgt_cr, vecs)


def _glu_bwd(dr_cr, gt1, w_out, w_glu, y_bcr, z_cr, sg_cr):
    nb, j, _ = y_bcr.shape
    e, d = w_out.shape
    tj = _cr_tile(j)

    def body(dr_ref, g_ref, wo_hbm, wg_hbm, y_ref, z_ref, sg_ref, dz_ref, dt_ref, dy_ref, wo_ref, wg_ref):
        @pl.when(jnp.logical_and(pl.program_id(0) == 0, pl.program_id(1) == 0))
        def _():
            pltpu.sync_copy(wo_hbm, wo_ref)
            pltpu.sync_copy(wg_hbm, wg_ref)

        do = (dr_ref[...].astype(F32) * g_ref[...]).astype(BF)
        dw = _dot_nt(do, wo_ref[...])
        y = jnp.concatenate([y_ref[b] for b in range(nb)], axis=1).astype(F32)
        g, dgel = _gelu_parts(y)
        z = z_ref[...].astype(F32)
        sz = _sigmoid(z)
        sg = sg_ref[...].astype(F32)
        dg2 = dw * (z * sz)
        dz_ref[...] = (dw * g * sg * (sz * (1.0 + z * (1.0 - sz)))).astype(BF)
        dt = (dg2 * g * sg * (1.0 - sg)).astype(BF)
        dt_ref[...] = dt
        dy = (dg2 * sg + _dot_nt(dt, wg_ref[...])) * dgel
        for b in range(nb):
            dy_ref[b] = dy[:, b * LANE_BLOCK:(b + 1) * LANE_BLOCK].astype(BF)

    tok_e = pl.BlockSpec((tj, e), lambda t, s: (t, s))
    blk = pl.BlockSpec((nb, tj, LANE_BLOCK), lambda t, s: (0, t, s))
    return pl.pallas_call(
        body, name="l1_glu_bwd", grid=(j // tj, CHUNK),
        in_specs=[pl.BlockSpec((tj, d), lambda t, s: (t, s)), _full((1, d)), ANY, ANY, blk, tok_e, tok_e],
        out_specs=[tok_e, tok_e, blk],
        out_shape=[jax.ShapeDtypeStruct((j, CHUNK * e), BF), jax.ShapeDtypeStruct((j, CHUNK * e), BF),
                   jax.ShapeDtypeStruct((nb, j, BCR_W), BF)],
        scratch_shapes=[pltpu.VMEM(w_out.shape, BF), pltpu.VMEM(w_glu.shape, BF)], compiler_params=_cparams(),
    )(dr_cr, gt1, w_out, w_glu, y_bcr, z_cr, sg_cr)


def _bwd_inproj1(du_bcr, dz_cr, w, xh_cr, rs_cr, dr2_cr, vecs, tag):
    nb, j, _ = du_bcr.shape
    d = w.shape[0]
    e = w.shape[1] // 2
    tj = _cr_tile(j)

    def body(du_ref, dz_ref, w_hbm, xh_ref, rs_ref, dr2_ref, v_ref, dr1_ref, acc_ref, w_ref):
        @pl.when(jnp.logical_and(pl.program_id(0) == 0, pl.program_id(1) == 0))
        def _():
            pltpu.sync_copy(w_hbm, w_ref)
            acc_ref[...] = jnp.zeros_like(acc_ref)

        du = jnp.concatenate([du_ref[b] for b in range(nb)], axis=1)
        dh = _dot_nt(du, w_ref[:, :e]) + _dot_nt(dz_ref[...], w_ref[:, e:])
        xh = xh_ref[...]
        x1 = xh * v_ref[0:1, :] + v_ref[1:2, :]
        dx1 = DN_ALPHA * dr2_ref[...].astype(F32) + dh * v_ref[2:3, :]
        dxh = dx1 * v_ref[0:1, :]
        rstd = rs_ref[:, 0:1]
        dr1 = rstd * (dxh - jnp.mean(dxh, axis=-1, keepdims=True) - xh * jnp.mean(dxh * xh, axis=-1, keepdims=True))
        dr1_ref[...] = dr1.astype(BF)
        acc_ref[0:1, :] += jnp.sum(dh * x1, axis=0, keepdims=True)
        acc_ref[1:2, :] += jnp.sum(dh, axis=0, keepdims=True)
        acc_ref[2:3, :] += jnp.sum(dx1 * xh, axis=0, keepdims=True)
        acc_ref[3:4, :] += jnp.sum(dx1, axis=0, keepdims=True)

    tok_d = pl.BlockSpec((tj, d), lambda t, s: (t, s))
    return pl.pallas_call(
        body, name="l1_bwd_inproj_" + tag, grid=(j // tj, CHUNK),
        in_specs=[pl.BlockSpec((nb, tj, LANE_BLOCK), lambda t, s: (0, t, s)), pl.BlockSpec((tj, e), lambda t, s: (t, s)),
                  ANY, tok_d, pl.BlockSpec((tj, 128), lambda t, s: (t, s)), tok_d, _full((8, d))],
        out_specs=[tok_d, _full((8, d))],
        out_shape=[jax.ShapeDtypeStruct((j, CHUNK * d), BF), jax.ShapeDtypeStruct((8, d), F32)],
        scratch_shapes=[pltpu.VMEM(w.shape, BF)], compiler_params=_cparams(),
    )(du_bcr, dz_cr, w, xh_cr, rs_cr, dr2_cr, vecs)


def _dw_cr(lhs, rhs, lhs_kind, rhs_kind, vec, bias_sum, init, name):
    if lhs_kind == "gelu_bcr":
        nb_l, j, _ = lhs.shape
        k = nb_l * LANE_BLOCK
    else:
        j = lhs.shape[0]
        k = lhs.shape[1] // CHUNK
    if rhs_kind == "bcr":
        nb_r = rhs.shape[0]
        n = nb_r * LANE_BLOCK
    else:
        n = rhs.shape[1] // CHUNK
    tj = _cr_tile(j, 512)
    nh = 2 if k * n * 4 > (8 << 20) else 1
    tn = n // nh
    nbh = tn // LANE_BLOCK
    nt = j // tj
    has_init = init is not None

    def body(*refs):
        refs = list(refs)
        l_ref, r_ref = refs[0], refs[1]
        pos = 2
        v_ref = None
        if vec is not None:
            v_ref = refs[pos]
            pos += 1
        i_ref = None
        if has_init:
            i_ref = refs[pos]
            pos += 1
        o_ref = refs[pos]
        pos += 1
        bs_ref = None
        if bias_sum:
            bs_ref = refs[pos]
            pos += 1
        acc_ref = refs[pos]
        t, s = pl.program_id(1), pl.program_id(2)
        first = jnp.logical_and(t == 0, s == 0)

        @pl.when(first)
        def _():
            acc_ref[...] = i_ref[...] if has_init else jnp.zeros_like(acc_ref)
            if bias_sum:
                bs_ref[...] = jnp.zeros_like(bs_ref)

        if lhs_kind == "gelu_bcr":
            y = jnp.concatenate([l_ref[b] for b in range(nb_l)], axis=1).astype(F32)
            lv = _gelu_parts(y)[0].astype(BF)
        elif lhs_kind == "mod":
            lv = (l_ref[...] * v_ref[0:1, :] + v_ref[1:2, :]).astype(BF)
        else:
            lv = l_ref[...]
        if rhs_kind == "bcr":
            rv = jnp.concatenate([r_ref[b] for b in range(nbh)], axis=1)
        elif rhs_kind == "scaled":
            rv = (r_ref[...].astype(F32) * v_ref[0:1, :]).astype(BF)
        else:
            rv = r_ref[...]
        acc_ref[...] += _dot_tn(lv, rv)
        if bias_sum:
            bs_ref[0:1, :] += jnp.sum(rv.astype(F32), axis=0, keepdims=True)

        @pl.when(jnp.logical_and(t == nt - 1, s == CHUNK - 1))
        def _():
            o_ref[...] = acc_ref[...].astype(BF)

    if lhs_kind == "gelu_bcr":
        l_spec = pl.BlockSpec((nb_l, tj, LANE_BLOCK), lambda h, t, s: (0, t, s))
    else:
        l_spec = pl.BlockSpec((tj, k), lambda h, t, s: (t, s))
    if rhs_kind == "bcr":
        r_spec = pl.BlockSpec((nbh, tj, LANE_BLOCK), lambda h, t, s: (h, t, s))
    else:
        r_spec = pl.BlockSpec((tj, tn), lambda h, t, s: (t, s * nh + h))
    in_specs, args = [l_spec, r_spec], [lhs, rhs]
    if vec is not None:
        in_specs.append(_full(vec.shape))
        args.append(vec)
    o_spec = pl.BlockSpec((k, tn), lambda h, t, s: (0, h))
    if has_init:
        in_specs.append(o_spec)
        args.append(init)
    out_specs, out_shape = [o_spec], [jax.ShapeDtypeStruct((k, n), BF)]
    if bias_sum:
        out_specs.append(pl.BlockSpec((8, tn), lambda h, t, s: (0, h)))
        out_shape.append(jax.ShapeDtypeStruct((8, n), F32))
    res = pl.pallas_call(
        body, name=name, grid=(nh, nt, CHUNK), in_specs=in_specs, out_specs=out_specs, out_shape=out_shape,
        scratch_shapes=[pltpu.VMEM((k, tn), F32)], compiler_params=_cparams(),
    )(*args)
    return res if bias_sum else res[0]


def _dw_cr_f32(lhs, rhs, vec, name):
    j = lhs.shape[0]
    k = lhs.shape[1] // CHUNK
    nb_r = rhs.shape[0]
    n = nb_r * LANE_BLOCK
    tj = _cr_tile(j)
    nt = j // tj

    def body(l_ref, r_ref, v_ref, o_ref):
        @pl.when(jnp.logical_and(pl.program_id(0) == 0, pl.program_id(1) == 0))
        def _():
            o_ref[...] = jnp.zeros_like(o_ref)

        lv = (l_ref[...] * v_ref[0:1, :] + v_ref[1:2, :]).astype(BF)
        rv = jnp.concatenate([r_ref[b] for b in range(nb_r)], axis=1)
        o_ref[...] += _dot_tn(lv, rv)

    return pl.pallas_call(
        body, name=name, grid=(nt, CHUNK),
        in_specs=[pl.BlockSpec((tj, k), lambda t, s: (t, s)), pl.BlockSpec((nb_r, tj, LANE_BLOCK), lambda t, s: (0, t, s)),
                  _full(vec.shape)],
        out_specs=_full((k, n)), out_shape=jax.ShapeDtypeStruct((k, n), F32), compiler_params=_cparams(),
    )(lhs, rhs, vec)


def _s5_compact(lam_re, lam_im, log_step, b_re, b_im, c_re, c_im, d_skip):
    hp = lax.Precision.HIGHEST
    g = lam_re.shape[1]
    nb = g // GROUPS_PER_BLOCK
    t, p, n = CHUNK, S5_P, S5_N
    dt = jnp.exp(log_step)[..., None]
    ks = jnp.arange(t + 1, dtype=F32).reshape(t + 1, 1, 1, 1)
    mag = jnp.exp(ks * (lam_re * dt)[None])
    ang = ks * (lam_im * dt)[None]
    pr, pi = mag * jnp.cos(ang), mag * jnp.sin(ang)
    ar, ai = pr[1], pi[1]
    qr, qi = ar - 1.0, ai
    den = lam_re * lam_re + lam_im * lam_im
    fr = (qr * lam_re + qi * lam_im) / den
    fi = (qi * lam_re - qr * lam_im) / den
    bt_re, bt_im = b_re.transpose(0, 1, 3, 2), b_im.transpose(0, 1, 3, 2)
    bbr = fr[:, :, None, :] * bt_re - fi[:, :, None, :] * bt_im
    bbi = fr[:, :, None, :] * bt_im + fi[:, :, None, :] * bt_re
    pk_r, pk_i = pr[:t, :, :, None, :], pi[:t, :, :, None, :]
    abr = pk_r * bbr[None] - pk_i * bbi[None]
    abi = pk_r * bbi[None] + pk_i * bbr[None]
    kd = (jnp.einsum("rgpn,krgqn->rgkpq", c_re, abr, precision=hp)
          - jnp.einsum("rgpn,krgqn->rgkpq", c_im, abi, precision=hp))
    skip = jnp.eye(p, dtype=F32)[None] * d_skip.reshape(g, p)[:, :, None]
    diag = kd[0][:, 0] + kd[1][:, 0] + skip
    qd = jnp.concatenate([jnp.flip(kd[1][:, 1:], axis=1), diag[:, None], kd[0][:, 1:]], axis=1)
    nd = 2 * t - 1
    wc = qd.transpose(0, 1, 3, 2).reshape(nb, GROUPS_PER_BLOCK, nd, p, p).transpose(0, 2, 1, 3, 4)
    wcomp = jnp.broadcast_to(wc[:, :, :, :, None, :], (nb, nd, GROUPS_PER_BLOCK, p, GROUPS_PER_BLOCK, p))
    wcomp = wcomp.reshape(nb, nd, LANE_BLOCK, LANE_BLOCK)
    ab = jnp.concatenate([abr, abi], axis=-1)
    bcc = jnp.stack([jnp.flip(ab[:, 0], axis=0), ab[:, 1]])
    bcomp = bcc.reshape(2, t, nb, LANE_BLOCK, 2 * n).transpose(2, 0, 1, 3, 4)
    prf = jnp.stack([pr[1:, 0], jnp.flip(pr[1:, 1], axis=0)])[:, :, :, None, :]
    pif = jnp.stack([pi[1:, 0], jnp.flip(pi[1:, 1], axis=0)])[:, :, :, None, :]
    cr_t = c_re[:, None]
    ci_t = c_im[:, None]
    ccc = jnp.concatenate([cr_t * prf - ci_t * pif, -(cr_t * pif + ci_t * prf)], axis=-1)
    ccomp = ccc.reshape(2, t, nb, LANE_BLOCK, 2 * n).transpose(2, 0, 1, 3, 4)
    return wcomp, bcomp, ccomp, pr[t], pi[t]


def _scan_coef(lam_re, lam_im, log_step):
    g = lam_re.shape[1]
    nb = g // GROUPS_PER_BLOCK
    ms = jnp.array([1, 2, 4, 0, 0, 0, 0, 0] + list(range(1, 9)) + list(range(8, 0, -1)), F32) * CHUNK
    dt = jnp.exp(log_step)[..., None]
    mag = jnp.exp(ms.reshape(-1, 1, 1, 1) * (lam_re * dt)[None])
    ang = ms.reshape(-1, 1, 1, 1) * (lam_im * dt)[None]
    cr, ci = mag * jnp.cos(ang), mag * jnp.sin(ang)
    lay = lambda a: a.reshape(24, 2, nb, ZH).transpose(2, 1, 0, 3)
    return jnp.concatenate([lay(cr), lay(ci)], axis=-1)


def _to_cr(a):
    return a.reshape(a.shape[0] // CHUNK, CHUNK * a.shape[1])


def _from_cr(a, c):
    return a.reshape(a.shape[0] * CHUNK, c)


def _pad8(v):
    return jnp.concatenate([v, jnp.zeros((8 - v.shape[0], v.shape[1]), v.dtype)], axis=0)


def _local_step(x, c, ctx, c_ctx, loss_target, w):
    l, d = x.shape
    lc = ctx.shape[0]
    e = w["conv_w_out"].shape[0]
    half = e // 2
    nb = e // LANE_BLOCK
    tm = min(256, lc)
    assert lc == tm and l % tm == 0 and tm % GRID_W == 0 and (tm & (tm - 1)) == 0
    tc = min(512, half)
    nl = l // tm

    c8 = _pad8(jnp.stack([c, c_ctx]))
    mod = _ada_fwd(c8, w["ada_w"], w["ada_b"])
    sh = mod[:, :2, :d]
    sc = mod[:, :2, d:2 * d]
    gt = mod[:, :2, 2 * d:]
    ln_g, ln_b = w["ln_g"], w["ln_b"]

    a0, b0 = 1.0 + sc[0], sh[0]
    p42 = _inproj0(x, ctx, a0, b0, w["conv_w_in"], tm)
    cw = w["conv_w"].reshape(3, 2, half)
    q3 = _conv_fwd(p42, cw, nl, tm, tc)
    xh1, rs1, fx = _outproj_ln0(q3, w["conv_w_out"], x, ctx, gt[0], tm)
    xh1_l, xh1_c = _to_cr(xh1[:l]), _to_cr(xh1[l:])
    rs1_l, rs1_c = _to_cr(rs1[:l]), _to_cr(rs1[l:])
    jl, jc = l // CHUNK, lc // CHUNK

    g0, bb0 = ln_g[0:1], ln_b[0:1]
    a1 = g0 * (1.0 + sc[1])
    b1 = bb0 * (1.0 + sc[1]) + sh[1]
    u_l, z_l = _inproj1(xh1_l, a1[0:1], b1[0:1], w["ssm_w_in"], "lat")
    u_c, _ = _inproj1(xh1_c, a1[1:2], b1[1:2], w["ssm_w_in"], "ctx")
    s5 = (w["ssm_lam_re"], w["ssm_lam_im"], w["ssm_log_step"], w["ssm_b_re"], w["ssm_b_im"],
          w["ssm_c_re"], w["ssm_c_im"], w["ssm_d"])
    (wcomp, bcomp, ccomp, _, _), s5_vjp = jax.vjp(_s5_compact, *s5)
    wbig_b = _expand_toeplitz(wcomp)
    bc_b = _expand_blocks(bcomp, "l1_expand_bc")
    cct_b = _expand_blocks(ccomp, "l1_expand_cc")
    coef = lax.stop_gradient(_scan_coef(*s5[:3]))
    zz_l = _bmm([u_l], [bc_b], [False], F32, "l1_s5_z_lat")
    zz_c = _bmm([u_c], [bc_b], [False], F32, "l1_s5_z_ctx")
    fwd_chains = ((("c", False), ("l", False)), (("c", True), ("l", True)))
    st_l, st_c = _scan(zz_l, zz_c, coef, fwd_chains, False, name="l1_scan_fwd")
    y_l = _bmm([u_l, st_l], [wbig_b, cct_b], [False, True], BF, "l1_s5_y")
    b_glu = w["ssm_b_glu"].reshape(1, e)
    w_cr, sg_cr = _glu_fwd(y_l, z_l, w["ssm_w_glu"], b_glu)
    vec_f = _pad8(jnp.concatenate([g0, bb0, gt[1][0:1], ln_g[1:2], ln_b[1:2]], axis=0))
    dr2, acc_f = _final(w_cr, w["ssm_w_out"], xh1_l, _to_cr(loss_target), vec_f)
    loss = jnp.sum(acc_f[3])

    gt1 = gt[1][0:1]
    dz_l, dt_l, dy_l = _glu_bwd(dr2, gt1, w["ssm_w_out"], w["ssm_w_glu"], y_l, z_l, sg_cr)
    g_w_out = _dw_cr(w_cr, dr2, "cr", "scaled", gt1, False, None, "l1_dw_out")
    g_w_glu, bsum = _dw_cr(y_l, dt_l, "gelu_bcr", "cr", None, True, None, "l1_dw_glu")
    g_b_glu = bsum[0]
    ds_l = _bmm([dy_l], [cct_b], [False], F32, "l1_s5_ds")
    bwd_chains = ((("l", True), ("c", True)), (("l", False), ("c", False)))
    dzz_l, dzz_c, da = _scan(ds_l, jnp.zeros_like(zz_c), coef, bwd_chains, True, st_l, st_c, name="l1_scan_bwd")
    du_l = _bmm([dy_l, dzz_l], [wbig_b, bc_b], [True, True], BF, "l1_s5_dx_lat")
    du_c = _bmm([dzz_c], [bc_b], [True], BF, "l1_s5_dx_ctx")
    d_wbig = _bdw(u_l, dy_l, "toeplitz", None, "l1_s5_dwbig")
    d_cc = _bdw(dy_l, st_l, "blocks", None, "l1_s5_dcc")
    d_bc = _bdw(u_l, dzz_l, "blocks", _bdw(u_c, dzz_c, "blocks", None, "l1_s5_dbc_ctx"), "l1_s5_dbc")
    da = jnp.sum(da, axis=2)
    unlay = lambda a: a.reshape(nb, 2, GROUPS_PER_BLOCK, S5_N).transpose(1, 0, 2, 3).reshape(2, nb * GROUPS_PER_BLOCK, S5_N)
    g_s5 = s5_vjp((d_wbig, d_bc, d_cc, unlay(da[..., :ZH]), unlay(da[..., ZH:])))

    vec_l = _pad8(jnp.concatenate([g0, bb0, 1.0 + sc[1][0:1]], axis=0))
    vec_c = _pad8(jnp.concatenate([g0, bb0, 1.0 + sc[1][1:2]], axis=0))
    dr1_l, acc_l = _bwd_inproj1(du_l, dz_l, w["ssm_w_in"], xh1_l, rs1_l, dr2, vec_l, "lat")
    dr1_c, acc_c = _bwd_inproj1(du_c, jnp.zeros((jc, CHUNK * e), BF), w["ssm_w_in"], xh1_c, rs1_c,
                                jnp.zeros((jc, CHUNK * d), BF), vec_c, "ctx")
    mod_l = jnp.concatenate([a1[0:1], b1[0:1]], axis=0)
    mod_c = jnp.concatenate([a1[1:2], b1[1:2]], axis=0)
    g_in_u = _dw_cr(xh1_l, du_l, "mod", "bcr", mod_l, False, _dw_cr_f32(xh1_c, du_c, mod_c, "l1_dw_in_u_ctx"), "l1_dw_in_u")
    g_in_z = _dw_cr(xh1_l, dz_l, "mod", "cr", mod_l, False, None, "l1_dw_in_z")
    g_w_in1 = jnp.concatenate([g_in_u, g_in_z], axis=1)

    dr1_ln, dr1_cn = _from_cr(dr1_l, d), _from_cr(dr1_c, d)
    dq3, acc_g0 = _bwd_outproj0(dr1_ln, dr1_cn, gt[0], w["conv_w_out"], fx, tm)
    dp42, dcw = _conv_bwd(dq3, p42, cw, nl, tm, tc)
    grad_x, acc_0 = _bwd_inproj0(dp42, w["conv_w_in"], x, ctx, dr1_ln, dr1_cn, a0, tm)
    g_w_in0 = _dw_inproj0(x, ctx, a0, b0, dp42, tm)
    g_w_out0 = _dw_outproj0(q3, dr1_ln, dr1_cn, gt[0], tm)

    zero = jnp.zeros((d,), F32)
    dm0 = jnp.stack([jnp.concatenate([acc_0[2], acc_0[0], acc_g0[0]]), jnp.concatenate([acc_0[3], acc_0[1], acc_g0[1]])])
    dm1 = jnp.stack([jnp.concatenate([acc_l[1], acc_l[0], acc_f[2]]), jnp.concatenate([acc_c[1], acc_c[0], zero])])
    dm8 = jnp.stack([_pad8(dm0), _pad8(dm1)])
    g_ada_w, dc8 = _ada_bwd(c8, w["ada_w"], dm8)

    grads = {
        "c_ctx": dc8[0, 1] + dc8[1, 1],
        "ada_w": g_ada_w,
        "ada_b": jnp.stack([dm0[0] + dm0[1], dm1[0] + dm1[1]]),
        "ln_g": jnp.stack([acc_l[2] + acc_c[2], acc_f[0]]),
        "ln_b": jnp.stack([acc_l[3] + acc_c[3], acc_f[1]]),
        "conv_w_in": g_w_in0, "conv_w": dcw[:3].reshape(3, e), "conv_w_out": g_w_out0,
        "ssm_w_in": g_w_in1,
        "ssm_lam_re": g_s5[0], "ssm_lam_im": g_s5[1], "ssm_log_step": g_s5[2],
        "ssm_b_re": g_s5[3], "ssm_b_im": g_s5[4], "ssm_c_re": g_s5[5], "ssm_c_im": g_s5[6], "ssm_d": g_s5[7],
        "ssm_w_glu": g_w_glu, "ssm_b_glu": g_b_glu, "ssm_w_out": g_w_out,
    }
    return loss, grad_x, grads


WEIGHTS = ["c_ctx", "ada_w", "ada_b", "ln_g", "ln_b", "conv_w_in", "conv_w", "conv_w_out", "ssm_w_in",
           "ssm_lam_re", "ssm_lam_im", "ssm_log_step", "ssm_b_re", "ssm_b_im", "ssm_c_re", "ssm_c_im",
           "ssm_d", "ssm_w_glu", "ssm_b_glu", "ssm_w_out"]
BIG = {"ada_w": 1, "conv_w_in": 1, "conv_w_out": 0, "ssm_w_in": 1, "ssm_w_glu": 0, "ssm_w_out": 0}
SMALL_SHARDED = ["conv_w", "ssm_d", "ssm_b_glu"]
REPLICATED = ["c_ctx", "ada_b", "ln_g", "ln_b", "ssm_lam_re", "ssm_lam_im", "ssm_log_step",
              "ssm_b_re", "ssm_b_im", "ssm_c_re", "ssm_c_im"]


def _view2d(name, a):
    return a.reshape(-1, a.shape[-1])


def kernel(x, c, ctx, c_ctx, ada_w, ada_b, ln_g, ln_b, conv_w_in, conv_w, conv_w_out, ssm_w_in, ssm_lam_re, ssm_lam_im, ssm_log_step, ssm_b_re, ssm_b_im, ssm_c_re, ssm_c_im, ssm_d, ssm_w_glu, ssm_b_glu, ssm_w_out, loss_target, m_c_ctx, m_ada_w, m_ada_b, m_ln_g, m_ln_b, m_conv_w_in, m_conv_w, m_conv_w_out, m_ssm_w_in, m_ssm_lam_re, m_ssm_lam_im, m_ssm_log_step, m_ssm_b_re, m_ssm_b_im, m_ssm_c_re, m_ssm_c_im, m_ssm_d, m_ssm_w_glu, m_ssm_b_glu, m_ssm_w_out, v_c_ctx, v_ada_w, v_ada_b, v_ln_g, v_ln_b, v_conv_w_in, v_conv_w, v_conv_w_out, v_ssm_w_in, v_ssm_lam_re, v_ssm_lam_im, v_ssm_log_step, v_ssm_b_re, v_ssm_b_im, v_ssm_c_re, v_ssm_c_im, v_ssm_d, v_ssm_w_glu, v_ssm_b_glu, v_ssm_w_out):
    args = locals()
    wt = {n: args[n] for n in WEIGHTS}
    mt = {n: args["m_" + n] for n in WEIGHTS}
    vt = {n: args["v_" + n] for n in WEIGHTS}

    big_names = list(BIG)
    shards = [_view2d(n, wt[n]).astype(BF) for n in big_names]
    small = jnp.concatenate([wt["conv_w"][0], wt["ssm_d"], wt["ssm_b_glu"]], axis=0)
    small = jnp.concatenate([small, jnp.zeros((3, small.shape[1]), F32)], axis=0)
    gathered = _all_gather(shards + [small], [BIG[n] for n in big_names] + [1], "gather_weights")
    full = dict(zip(big_names, gathered[:-1]))
    small_full = gathered[-1]
    d = x.shape[-1]
    w = {
        "ada_w": full["ada_w"].reshape(2, d, 3 * d), "ada_b": ada_b, "ln_g": ln_g, "ln_b": ln_b,
        "conv_w_in": full["conv_w_in"], "conv_w": small_full[0:3], "conv_w_out": full["conv_w_out"],
        "ssm_w_in": full["ssm_w_in"], "ssm_lam_re": ssm_lam_re[0], "ssm_lam_im": ssm_lam_im[0],
        "ssm_log_step": ssm_log_step[0], "ssm_b_re": ssm_b_re[0], "ssm_b_im": ssm_b_im[0],
        "ssm_c_re": ssm_c_re[0], "ssm_c_im": ssm_c_im[0], "ssm_d": small_full[3],
        "ssm_w_glu": full["ssm_w_glu"], "ssm_b_glu": small_full[4], "ssm_w_out": full["ssm_w_out"],
    }

    loss, grad_x, g = _local_step(x[0], c[0], ctx[0], c_ctx, loss_target[0], w)
    loss = lax.psum(loss, ("x", "y", "c"))

    big_parts = [_view2d(n, g[n]) for n in big_names]
    blob_names = REPLICATED + SMALL_SHARDED
    flat = jnp.concatenate([g[n].reshape(-1).astype(F32) for n in blob_names])
    nflat = flat.shape[0]
    rows = -(-nflat // (N_DEV * 128 * 8)) * 8
    flat = jnp.concatenate([flat, jnp.zeros((N_DEV * rows * 128 - nflat,), F32)]).reshape(N_DEV * rows, 128)
    recv = _all_to_all(big_parts + [flat], [BIG[n] for n in big_names] + [0], "scatter_grads")
    blob_sum = _sum_partials(recv[-1])
    blob = _all_gather([blob_sum], [0], "gather_small_grads")[0].reshape(-1)
    small_g, off = {}, 0
    for n in blob_names:
        shape = wt[n].shape if n in REPLICATED else (*wt[n].shape[:-1], wt[n].shape[-1] * N_DEV)
        size = math.prod(shape)
        small_g[n] = blob[off:off + size].reshape(shape)
        off += size
    me = 4 * lax.axis_index("x") + 2 * lax.axis_index("y") + lax.axis_index("c")
    for n in SMALL_SHARDED:
        size = wt[n].shape[-1]
        small_g[n] = lax.dynamic_slice_in_dim(small_g[n], me * size, size, axis=small_g[n].ndim - 1)

    out_g, out_d, out_m, out_v = {}, {}, {}, {}
    for n, stack in zip(big_names, recv[:-1]):
        shp = wt[n].shape
        res = _adamw(stack, _view2d(n, wt[n]), _view2d(n, mt[n]), _view2d(n, vt[n]), "adamw_" + n)
        out_g[n], out_d[n], out_m[n], out_v[n] = [r.reshape(shp) for r in res]
    names = list(small_g)
    cat = lambda t: jnp.concatenate([t[n].reshape(-1) for n in names])
    gs, ws, ms, vs = cat(small_g), cat(wt), cat(mt), cat(vt)
    ns = gs.shape[0]
    rs = -(-ns // (128 * 512)) * 512
    padr = lambda a: jnp.concatenate([a, jnp.ones((rs * 128 - ns,), F32)]).reshape(rs, 128)
    res = _adamw(padr(gs)[None], padr(ws), padr(ms), padr(vs), "adamw_small")
    off = 0
    for n in names:
        size = math.prod(wt[n].shape)
        out_g[n], out_d[n], out_m[n], out_v[n] = [r.reshape(-1)[off:off + size].reshape(wt[n].shape) for r in res]
        off += size

    return (loss, grad_x[None], *[out_g[n] for n in WEIGHTS], *[out_d[n] for n in WEIGHTS],
            *[out_m[n] for n in WEIGHTS], *[out_v[n] for n in WEIGHTS])
```

```python
import math

import jax
import jax.numpy as jnp
from jax import lax
from jax.experimental import pallas as pl
from jax.experimental.pallas import tpu as pltpu

F32 = jnp.float32
BF = jnp.bfloat16
MESH = pl.DeviceIdType.MESH
N_DEV = 8

GRID_W = 64
CHUNK = 16
S5_P = 16
S5_N = 64
LANE_BLOCK = 128
GROUPS_PER_BLOCK = LANE_BLOCK // S5_P
BCR_W = CHUNK * LANE_BLOCK
ZL_W = 2 * 2 * GROUPS_PER_BLOCK * S5_N
ZH = ZL_W // 4
LN_EPS = 1e-5
DN_ALPHA = 4.0 ** 0.25
ADAM_LR, ADAM_B1, ADAM_B2, ADAM_EPS, ADAM_WD, ADAM_STEP = 1e-3, 0.9, 0.999, 1e-8, 0.01, 10
GELU_C0 = math.sqrt(2.0 / math.pi)
GELU_C1 = 0.044715
VMEM_MB = 52

ANY = pl.BlockSpec(memory_space=pl.ANY)


def _cparams():
    return pltpu.CompilerParams(vmem_limit_bytes=VMEM_MB << 20)


def _dot(a, b):
    return jnp.dot(a, b, preferred_element_type=F32)


def _dot_nt(a, b):
    return lax.dot_general(a, b, (((1,), (1,)), ((), ())), preferred_element_type=F32)


def _dot_tn(a, b):
    return lax.dot_general(a, b, (((0,), (0,)), ((), ())), preferred_element_type=F32)


def _sigmoid(x):
    return 1.0 / (1.0 + jnp.exp(-x))


def _gelu_parts(y):
    th = jnp.tanh(GELU_C0 * (y + GELU_C1 * y * y * y))
    g = 0.5 * y * (1.0 + th)
    dg = 0.5 * (1.0 + th) + 0.5 * y * (1.0 - th * th) * GELU_C0 * (1.0 + 3.0 * GELU_C1 * y * y)
    return g, dg


def _full(shape):
    nd = len(shape)
    return pl.BlockSpec(shape, lambda *_: (0,) * nd)


def _mesh_pos():
    x, y, c = lax.axis_index("x"), lax.axis_index("y"), lax.axis_index("c")
    return x, y, c


def _peer(pos, k):
    x, y, c = pos
    px = 1 - x if (k >> 2) & 1 else x
    py = 1 - y if (k >> 1) & 1 else y
    pc = 1 - c if k & 1 else c
    return (px, py, pc), 4 * px + 2 * py + pc


def _shard_at(ref, axis, idx, n):
    if axis == 0:
        return ref.at[pl.ds(idx * n, n)]
    return ref.at[:, pl.ds(idx * n, n)]


class _Exchange:
    def __init__(self, kind, arrays, axes):
        self.kind, self.axes, self.n = kind, list(axes), len(arrays)
        self.arrays = list(arrays)
        self.out_shape = []
        for s, ax in zip(arrays, axes):
            shp = list(s.shape)
            if kind == "gather":
                shp[ax] *= N_DEV
                self.out_shape.append(jax.ShapeDtypeStruct(tuple(shp), s.dtype))
            else:
                shp[ax] //= N_DEV
                self.out_shape.append(jax.ShapeDtypeStruct((N_DEV, *shp), s.dtype))
        self.scratch = [pltpu.SemaphoreType.DMA((self.n, N_DEV - 1)), pltpu.SemaphoreType.DMA((self.n, N_DEV - 1)),
                        pltpu.SemaphoreType.DMA((self.n,))]

    def _copies(self, ins, outs, sems):
        send_sems, recv_sems, local_sems = sems
        pos = _mesh_pos()
        me = 4 * pos[0] + 2 * pos[1] + pos[2]
        local, sends, recvs = [], [], []
        for i in range(self.n):
            ax = self.axes[i]
            if self.kind == "gather":
                size = ins[i].shape[ax]
                src = lambda idx, i=i: ins[i]
                dst = lambda idx, i=i, ax=ax, size=size: _shard_at(outs[i], ax, idx, size)
                mine, theirs = (lambda pidx: me), (lambda pidx: pidx)
            else:
                size = ins[i].shape[ax] // N_DEV
                src = lambda idx, i=i, ax=ax, size=size: _shard_at(ins[i], ax, idx, size)
                dst = lambda idx, i=i: outs[i].at[idx]
                mine, theirs = (lambda pidx: me), (lambda pidx: pidx)
            src_own = src(me)
            local.append(pltpu.make_async_copy(src_own, dst(me), local_sems.at[i]))
            for k in range(1, N_DEV):
                peer, pidx = _peer(pos, k)
                out_src = src(me) if self.kind == "gather" else src(pidx)
                sends.append(pltpu.make_async_remote_copy(
                    src_ref=out_src, dst_ref=dst(mine(pidx)), send_sem=send_sems.at[i, k - 1],
                    recv_sem=recv_sems.at[i, k - 1], device_id=peer, device_id_type=MESH))
                recvs.append(pltpu.make_async_remote_copy(
                    src_ref=out_src, dst_ref=dst(theirs(pidx)), send_sem=send_sems.at[i, k - 1],
                    recv_sem=recv_sems.at[i, k - 1], device_id=peer, device_id_type=MESH))
        return local, sends, recvs

    def start(self, ins, outs, sems):
        local, sends, _ = self._copies(ins, outs, sems)
        for cp in local + sends:
            cp.start()

    def wait(self, ins, outs, sems):
        local, sends, recvs = self._copies(ins, outs, sems)
        for cp in recvs:
            cp.wait_recv()
        for cp in sends:
            cp.wait_send()
        for cp in local:
            cp.wait()

    def run(self, name):
        n = self.n

        def body(*refs):
            ins, outs, sems = refs[:n], refs[n:2 * n], refs[2 * n:]
            self.start(ins, outs, sems)
            self.wait(ins, outs, sems)

        return pl.pallas_call(body, name=name, out_shape=self.out_shape, in_specs=[ANY] * n, out_specs=[ANY] * n,
                              scratch_shapes=self.scratch)(*self.arrays)


def _hosted_call(body, xch, grid, in_specs, out_specs, out_shape, scratch, args, name):
    out_specs, out_shape = list(out_specs), list(out_shape)
    n_in, n_out = len(in_specs), len(out_specs)
    if xch is None:
        res = pl.pallas_call(body, name=name, grid=grid, in_specs=in_specs, out_specs=out_specs, out_shape=out_shape,
                             scratch_shapes=list(scratch), compiler_params=_cparams())(*args)
        return list(res), []
    n = xch.n
    rank = len(grid)

    def wrapped(*refs):
        ins, x_ins = refs[:n_in], refs[n_in:n_in + n]
        outs = refs[n_in + n:n_in + n + n_out]
        x_outs = refs[n_in + n + n_out:n_in + 2 * n + n_out]
        rest = refs[n_in + 2 * n + n_out:]
        own, sems = rest[:len(rest) - 3], rest[len(rest) - 3:]
        ids = [pl.program_id(a) for a in range(rank)]
        first, last = ids[0] == 0, ids[0] == grid[0] - 1
        for a in range(1, rank):
            first = jnp.logical_and(first, ids[a] == 0)
            last = jnp.logical_and(last, ids[a] == grid[a] - 1)

        @pl.when(first)
        def _():
            xch.start(x_ins, x_outs, sems)

        body(*ins, *outs, *own)

        @pl.when(last)
        def _():
            xch.wait(x_ins, x_outs, sems)

    res = pl.pallas_call(
        wrapped, name=name, grid=grid, in_specs=list(in_specs) + [ANY] * n, out_specs=out_specs + [ANY] * n,
        out_shape=out_shape + xch.out_shape, scratch_shapes=list(scratch) + xch.scratch, compiler_params=_cparams(),
    )(*args, *xch.arrays)
    return list(res[:n_out]), list(res[n_out:])


def _all_gather(shards, axes, name):
    return _Exchange("gather", shards, axes).run(name)


def _all_to_all(parts, axes, name):
    return _Exchange("scatter", parts, axes).run(name)


def _ada_fwd(c8, ada_w, ada_b):
    nl, d, d3 = ada_w.shape

    def body(c_ref, w_ref, b_ref, o_ref):
        cv = c_ref[...]
        s = (cv * _sigmoid(cv)).astype(BF)
        o_ref[0] = _dot(s, w_ref[0]) + b_ref[0]

    return pl.pallas_call(
        body, name="ada_fwd", grid=(nl,),
        in_specs=[_full((8, d)), pl.BlockSpec((1, d, d3), lambda l: (l, 0, 0)), pl.BlockSpec((1, 1, d3), lambda l: (l, 0, 0))],
        out_specs=pl.BlockSpec((1, 8, d3), lambda l: (l, 0, 0)),
        out_shape=jax.ShapeDtypeStruct((nl, 8, d3), F32), compiler_params=_cparams(),
    )(c8, ada_w, ada_b.reshape(nl, 1, d3))


def _ada_bwd(c8, ada_w, dm8):
    nl, d, d3 = ada_w.shape

    def body(c_ref, w_ref, dm_ref, dw_ref, dc_ref):
        cv = c_ref[...]
        sg = _sigmoid(cv)
        s = (cv * sg).astype(BF)
        dm = dm_ref[0].astype(BF)
        dw_ref[0] = _dot_tn(s, dm).astype(BF)
        dc_ref[0] = _dot_nt(dm, w_ref[0]) * (sg * (1.0 + cv * (1.0 - sg)))

    return pl.pallas_call(
        body, name="ada_bwd", grid=(nl,),
        in_specs=[_full((8, d)), pl.BlockSpec((1, d, d3), lambda l: (l, 0, 0)), pl.BlockSpec((1, 8, d3), lambda l: (l, 0, 0))],
        out_specs=[pl.BlockSpec((1, d, d3), lambda l: (l, 0, 0)), pl.BlockSpec((1, 8, d), lambda l: (l, 0, 0))],
        out_shape=[jax.ShapeDtypeStruct((nl, d, d3), BF), jax.ShapeDtypeStruct((nl, 8, d), F32)],
        compiler_params=_cparams(),
    )(c8, ada_w, dm8)


def _sum_partials(stack):
    _, r, c = stack.shape

    def body(s_ref, o_ref):
        acc = s_ref[0]
        for p in range(1, N_DEV):
            acc = acc + s_ref[p]
        o_ref[...] = acc

    return pl.pallas_call(body, name="sum_partials", out_shape=jax.ShapeDtypeStruct((r, c), F32),
                          in_specs=[_full(stack.shape)], out_specs=_full((r, c)), grid=(1,),
                          compiler_params=_cparams())(stack)


def _adamw(gstack, w, m, v, name):
    p, r, c = gstack.shape
    tr = r
    for cand in (512 if c <= 256 else 256, 128, 64, 32, 16, 8):
        if r % cand == 0 and r > cand:
            tr = cand
            break
    bc1 = 1.0 - ADAM_B1 ** ADAM_STEP
    bc2 = 1.0 - ADAM_B2 ** ADAM_STEP

    def body(g_ref, w_ref, m_ref, v_ref, go_ref, d_ref, mo_ref, vo_ref):
        g = g_ref[0].astype(F32)
        for q in range(1, p):
            g = g + g_ref[q].astype(F32)
        mn = ADAM_B1 * m_ref[...] + (1.0 - ADAM_B1) * g
        vn = ADAM_B2 * v_ref[...] + (1.0 - ADAM_B2) * (g * g)
        go_ref[...] = g
        mo_ref[...] = mn
        vo_ref[...] = vn
        d_ref[...] = -ADAM_LR * ((mn / bc1) / (jnp.sqrt(vn / bc2) + ADAM_EPS) + ADAM_WD * w_ref[...])

    row = pl.BlockSpec((tr, c), lambda i: (i, 0))
    sds = jax.ShapeDtypeStruct((r, c), F32)
    return pl.pallas_call(
        body, name=name, grid=(r // tr,),
        in_specs=[pl.BlockSpec((p, tr, c), lambda i: (0, i, 0)), row, row, row],
        out_specs=[row, row, row, row], out_shape=[sds, sds, sds, sds], compiler_params=_cparams(),
    )(gstack, w, m, v)


def _lat_or_ctx_specs(tm, d, nl, grid_rank, row_axis):
    def lat(*ids):
        return (jnp.minimum(ids[row_axis], nl - 1), 0)

    def ctx(*ids):
        return (jnp.maximum(ids[row_axis] - nl, 0), 0)

    return pl.BlockSpec((tm, d), lat), pl.BlockSpec((tm, d), ctx)


def _sel_row(ref, is_ctx):
    return jnp.where(is_ctx, ref[1:2, :], ref[0:1, :])


def _inproj0(x, ctx, a2, b2, w, tm, xch=None):
    l, d = x.shape
    nl, nc = l // tm, ctx.shape[0] // tm
    e = w.shape[1] // 4
    half = e // 2

    def body(x_ref, c_ref, a_ref, b_ref, w_hbm, o_ref, w_ref):
        i = pl.program_id(0)

        @pl.when(i == 0)
        def _():
            pltpu.sync_copy(w_hbm, w_ref)

        is_ctx = i >= nl
        xv = jnp.where(is_ctx, c_ref[...], x_ref[...])
        h = (xv * _sel_row(a_ref, is_ctx) + _sel_row(b_ref, is_ctx)).astype(BF)
        for k in range(4):
            r = _dot(h, w_ref[:, k * e:(k + 1) * e])
            o_ref[k, 0] = r[:, :half].astype(BF)
            o_ref[k, 1] = r[:, half:].astype(BF)

    lat, cx = _lat_or_ctx_specs(tm, d, nl, 1, 0)
    (p42,), extra = _hosted_call(
        body, xch, grid=(nl + nc,),
        in_specs=[lat, cx, _full((2, d)), _full((2, d)), ANY],
        out_specs=[pl.BlockSpec((4, 2, tm, half), lambda i: (0, 0, i, 0))],
        out_shape=[jax.ShapeDtypeStruct((4, 2, l + ctx.shape[0], half), BF)],
        scratch=[pltpu.VMEM(w.shape, BF)], args=(x, ctx, a2, b2, w), name="l0_inproj")
    return p42, extra


def _conv_taps(u, w_up, w_mid, w_dn, pos, rl, tm):
    up = jnp.where(pos == 0, 0.0, pltpu.roll(u, 1, 0))
    dn = jnp.where(pos == rl - 1, 0.0, pltpu.roll(u, tm - 1, 0))
    return w_up * up + w_mid * u + w_dn * dn, up, dn


def _conv_halo_specs(tm, tc, nl, lead):
    hb = tm // GRID_W

    def prev(j, i):
        return (0, 1, jnp.maximum(jnp.minimum(i, nl - 1) * hb - 1, 0), j)

    def nxt(j, i):
        return (0, 1, jnp.minimum((jnp.minimum(i, nl - 1) + 1) * hb, nl * hb - 1), j)

    return pl.BlockSpec((lead, 1, GRID_W, tc), prev), pl.BlockSpec((lead, 1, GRID_W, tc), nxt)


def _conv_fwd(p42, cw, nl, tm, tc):
    _, _, r, half = p42.shape
    nt = r // tm

    def body(p_ref, hp_ref, hn_ref, cw_ref, o_ref):
        i = pl.program_id(1)
        is_ctx = i >= nl
        row = lax.broadcasted_iota(jnp.int32, (tm, tc), 0)
        rl = jnp.where(is_ctx, tm, GRID_W)
        pos = jnp.bitwise_and(row, rl - 1)

        def gate(hv, yc):
            bg = p_ref[0, hv].astype(F32)
            z = p_ref[3, hv].astype(F32)
            return (bg * yc * (z * _sigmoid(z))).astype(BF)

        u_h = p_ref[1, 0].astype(F32) * p_ref[2, 0].astype(F32)
        w_h = cw_ref[:, 0, :]
        o_ref[0] = gate(0, _conv_taps(u_h, w_h[0:1], w_h[1:2], w_h[2:3], pos, rl, tm)[0])
        u_v = p_ref[1, 1].astype(F32) * p_ref[2, 1].astype(F32)
        w_v = cw_ref[:, 1, :]

        @pl.when(is_ctx)
        def _():
            o_ref[1] = gate(1, _conv_taps(u_v, w_v[0:1], w_v[1:2], w_v[2:3], pos, rl, tm)[0])

        @pl.when(jnp.logical_not(is_ctx))
        def _():
            up = hp_ref[1, 0].astype(F32) * hp_ref[2, 0].astype(F32) * (i > 0).astype(F32)
            dn = hn_ref[1, 0].astype(F32) * hn_ref[2, 0].astype(F32) * (i < nl - 1).astype(F32)
            ext = jnp.concatenate([up, u_v, dn], axis=0)
            yc = w_v[0:1] * ext[0:tm] + w_v[1:2] * u_v + w_v[2:3] * ext[2 * GRID_W:tm + 2 * GRID_W]
            o_ref[1] = gate(1, yc)

    hp, hn = _conv_halo_specs(tm, tc, nl, 4)
    return pl.pallas_call(
        body, name="l0_conv_fwd", grid=(half // tc, nt),
        in_specs=[pl.BlockSpec((4, 2, tm, tc), lambda j, i: (0, 0, i, j)), hp, hn,
                  pl.BlockSpec((3, 2, tc), lambda j, i: (0, 0, j))],
        out_specs=pl.BlockSpec((2, tm, tc), lambda j, i: (0, i, j)),
        out_shape=jax.ShapeDtypeStruct((2, r, half), BF), compiler_params=_cparams(),
    )(p42, p42, p42, cw)


def _outproj_ln0(q3, w_out, x, ctx, gt2, tm):
    l, d = x.shape
    nl, nc = l // tm, ctx.shape[0] // tm
    _, r, half = q3.shape

    def body(q_ref, w_hbm, x_ref, c_ref, g_ref, xh_ref, rs_ref, fx_ref, w_ref):
        i = pl.program_id(0)

        @pl.when(i == 0)
        def _():
            pltpu.sync_copy(w_hbm, w_ref)

        is_ctx = i >= nl
        fx = _dot(q_ref[0], w_ref[:half, :]) + _dot(q_ref[1], w_ref[half:, :])
        xv = jnp.where(is_ctx, c_ref[...], x_ref[...])
        rr = DN_ALPHA * xv + _sel_row(g_ref, is_ctx) * fx
        mu = jnp.mean(rr, axis=-1, keepdims=True)
        cen = rr - mu
        rstd = lax.rsqrt(jnp.mean(cen * cen, axis=-1, keepdims=True) + LN_EPS)
        xh_ref[...] = cen * rstd
        rs_ref[...] = jnp.broadcast_to(rstd, (tm, 128))
        fx_ref[...] = fx.astype(BF)

    lat, cx = _lat_or_ctx_specs(tm, d, nl, 1, 0)
    return pl.pallas_call(
        body, name="l0_outproj_ln", grid=(nl + nc,),
        in_specs=[pl.BlockSpec((2, tm, half), lambda i: (0, i, 0)), ANY, lat, cx, _full((2, d))],
        out_specs=[pl.BlockSpec((tm, d), lambda i: (i, 0)), pl.BlockSpec((tm, 128), lambda i: (i, 0)),
                   pl.BlockSpec((tm, d), lambda i: (i, 0))],
        out_shape=[jax.ShapeDtypeStruct((r, d), F32), jax.ShapeDtypeStruct((r, 128), F32), jax.ShapeDtypeStruct((r, d), BF)],
        scratch_shapes=[pltpu.VMEM(w_out.shape, BF)], compiler_params=_cparams(),
    )(q3, w_out, x, ctx, gt2)


def _bwd_outproj0(dr_l, dr_c, gt2, w_out, fx, tm):
    l, d = dr_l.shape
    nl, nc = l // tm, dr_c.shape[0] // tm
    e = w_out.shape[0]
    half = e // 2
    r = l + dr_c.shape[0]

    def body(dl_ref, dc_ref, g_ref, w_hbm, fx_ref, dq_ref, acc_ref, w_ref):
        i = pl.program_id(0)

        @pl.when(i == 0)
        def _():
            pltpu.sync_copy(w_hbm, w_ref)
            acc_ref[...] = jnp.zeros_like(acc_ref)

        is_ctx = i >= nl
        dr = jnp.where(is_ctx, dc_ref[...], dl_ref[...]).astype(F32)
        dfx = (dr * _sel_row(g_ref, is_ctx)).astype(BF)
        dq_ref[0] = _dot_nt(dfx, w_ref[:half, :]).astype(BF)
        dq_ref[1] = _dot_nt(dfx, w_ref[half:, :]).astype(BF)
        s = jnp.sum(dr * fx_ref[...].astype(F32), axis=0, keepdims=True)
        sel = is_ctx.astype(F32)
        acc_ref[0:1, :] += s * (1.0 - sel)
        acc_ref[1:2, :] += s * sel

    lat, cx = _lat_or_ctx_specs(tm, d, nl, 1, 0)
    return pl.pallas_call(
        body, name="l0_bwd_outproj", grid=(nl + nc,),
        in_specs=[lat, cx, _full((2, d)), ANY, pl.BlockSpec((tm, d), lambda i: (i, 0))],
        out_specs=[pl.BlockSpec((2, tm, half), lambda i: (0, i, 0)), _full((8, d))],
        out_shape=[jax.ShapeDtypeStruct((2, r, half), BF), jax.ShapeDtypeStruct((8, d), F32)],
        scratch_shapes=[pltpu.VMEM(w_out.shape, BF)], compiler_params=_cparams(),
    )(dr_l, dr_c, gt2, w_out, fx)


def _conv_bwd(dq3, p42, cw, nl, tm, tc, xch=None):
    _, _, r, half = p42.shape
    nt = r // tm

    def body(dq_ref, dqp_ref, dqn_ref, p_ref, hp_ref, hn_ref, cw_ref, dp_ref, dw_ref):
        i = pl.program_id(1)
        is_ctx = i >= nl

        @pl.when(i == 0)
        def _():
            dw_ref[...] = jnp.zeros_like(dw_ref)

        row = lax.broadcasted_iota(jnp.int32, (tm, tc), 0)
        rl = jnp.where(is_ctx, tm, GRID_W)
        pos = jnp.bitwise_and(row, rl - 1)

        def pieces(dq, bg, z):
            sz = _sigmoid(z)
            sil = z * sz
            return dq * bg * sil, dq * sil, dq * bg * (sz * (1.0 + z * (1.0 - sz)))

        def seq_half(hv):
            bg, cg = p_ref[0, hv].astype(F32), p_ref[1, hv].astype(F32)
            v, z = p_ref[2, hv].astype(F32), p_ref[3, hv].astype(F32)
            w = cw_ref[:, hv, :]
            u = cg * v
            yc, u_up, u_dn = _conv_taps(u, w[0:1], w[1:2], w[2:3], pos, rl, tm)
            dyc, dbg_f, dz_f = pieces(dq_ref[hv].astype(F32), bg, z)
            du = _conv_taps(dyc, w[2:3], w[1:2], w[0:1], pos, rl, tm)[0]
            dp_ref[0, hv] = (dbg_f * yc).astype(BF)
            dp_ref[1, hv] = (du * v).astype(BF)
            dp_ref[2, hv] = (du * cg).astype(BF)
            dp_ref[3, hv] = (dz_f * yc).astype(BF)
            dw_ref[0:1, hv, :] += jnp.sum(dyc * u_up, axis=0, keepdims=True)
            dw_ref[1:2, hv, :] += jnp.sum(dyc * u, axis=0, keepdims=True)
            dw_ref[2:3, hv, :] += jnp.sum(dyc * u_dn, axis=0, keepdims=True)

        seq_half(0)

        @pl.when(is_ctx)
        def _():
            seq_half(1)

        @pl.when(jnp.logical_not(is_ctx))
        def _():
            bg, cg = p_ref[0, 1].astype(F32), p_ref[1, 1].astype(F32)
            v, z = p_ref[2, 1].astype(F32), p_ref[3, 1].astype(F32)
            w = cw_ref[:, 1, :]
            u = cg * v
            m_up = (i > 0).astype(F32)
            m_dn = (i < nl - 1).astype(F32)

            def halo(h_ref, dqh_ref, msk):
                hb, hc = h_ref[0, 0].astype(F32), h_ref[1, 0].astype(F32)
                hv_, hz = h_ref[2, 0].astype(F32), h_ref[3, 0].astype(F32)
                return hc * hv_ * msk, pieces(dqh_ref[0].astype(F32), hb, hz)[0] * msk

            u_p, dyc_p = halo(hp_ref, dqp_ref, m_up)
            u_n, dyc_n = halo(hn_ref, dqn_ref, m_dn)
            u_ext = jnp.concatenate([u_p, u, u_n], axis=0)
            u_up, u_dn = u_ext[0:tm], u_ext[2 * GRID_W:tm + 2 * GRID_W]
            yc = w[0:1] * u_up + w[1:2] * u + w[2:3] * u_dn
            dyc, dbg_f, dz_f = pieces(dq_ref[1].astype(F32), bg, z)
            d_ext = jnp.concatenate([dyc_p, dyc, dyc_n], axis=0)
            du = w[0:1] * d_ext[2 * GRID_W:tm + 2 * GRID_W] + w[1:2] * dyc + w[2:3] * d_ext[0:tm]
            dp_ref[0, 1] = (dbg_f * yc).astype(BF)
            dp_ref[1, 1] = (du * v).astype(BF)
            dp_ref[2, 1] = (du * cg).astype(BF)
            dp_ref[3, 1] = (dz_f * yc).astype(BF)
            dw_ref[0:1, 1, :] += jnp.sum(dyc * u_up, axis=0, keepdims=True)
            dw_ref[1:2, 1, :] += jnp.sum(dyc * u, axis=0, keepdims=True)
            dw_ref[2:3, 1, :] += jnp.sum(dyc * u_dn, axis=0, keepdims=True)

    hb = tm // GRID_W

    def dq_prev(j, i):
        return (1, jnp.maximum(jnp.minimum(i, nl - 1) * hb - 1, 0), j)

    def dq_next(j, i):
        return (1, jnp.minimum((jnp.minimum(i, nl - 1) + 1) * hb, nl * hb - 1), j)

    hp, hn = _conv_halo_specs(tm, tc, nl, 4)
    (dp42, dcw), extra = _hosted_call(
        body, xch, grid=(half // tc, nt),
        in_specs=[pl.BlockSpec((2, tm, tc), lambda j, i: (0, i, j)),
                  pl.BlockSpec((1, GRID_W, tc), dq_prev), pl.BlockSpec((1, GRID_W, tc), dq_next),
                  pl.BlockSpec((4, 2, tm, tc), lambda j, i: (0, 0, i, j)), hp, hn,
                  pl.BlockSpec((3, 2, tc), lambda j, i: (0, 0, j))],
        out_specs=[pl.BlockSpec((4, 2, tm, tc), lambda j, i: (0, 0, i, j)), pl.BlockSpec((8, 2, tc), lambda j, i: (0, 0, j))],
        out_shape=[jax.ShapeDtypeStruct(p42.shape, BF), jax.ShapeDtypeStruct((8, 2, half), F32)],
        scratch=[], args=(dq3, dq3, dq3, p42, p42, p42, cw), name="l0_conv_bwd")
    return dp42, dcw, extra


def _bwd_inproj0(dp42, w_in, x, ctx, dr_l, dr_c, a2, tm, xch=None):
    l, d = x.shape
    nl, nc = l // tm, ctx.shape[0] // tm
    e = w_in.shape[1] // 4
    half = e // 2

    def body(dp_ref, w_hbm, x_ref, c_ref, dl_ref, dc_ref, a_ref, gx_ref, acc_ref, w_ref):
        i = pl.program_id(0)

        @pl.when(i == 0)
        def _():
            pltpu.sync_copy(w_hbm, w_ref)
            acc_ref[...] = jnp.zeros_like(acc_ref)

        is_ctx = i >= nl
        dh = jnp.zeros((tm, d), F32)
        for k in range(4):
            for hv in range(2):
                c0 = k * e + hv * half
                dh = dh + _dot_nt(dp_ref[k, hv], w_ref[:, c0:c0 + half])
        xv = jnp.where(is_ctx, c_ref[...], x_ref[...])
        s_sc = jnp.sum(dh * xv, axis=0, keepdims=True)
        s_sh = jnp.sum(dh, axis=0, keepdims=True)
        sel = is_ctx.astype(F32)
        acc_ref[0:1, :] += s_sc * (1.0 - sel)
        acc_ref[1:2, :] += s_sc * sel
        acc_ref[2:3, :] += s_sh * (1.0 - sel)
        acc_ref[3:4, :] += s_sh * sel

        @pl.when(jnp.logical_not(is_ctx))
        def _():
            gx_ref[...] = DN_ALPHA * dl_ref[...].astype(F32) + dh * a_ref[0:1, :]

    lat, cx = _lat_or_ctx_specs(tm, d, nl, 1, 0)
    (gx, acc), extra = _hosted_call(
        body, xch, grid=(nl + nc,),
        in_specs=[pl.BlockSpec((4, 2, tm, half), lambda i: (0, 0, i, 0)), ANY, lat, cx, lat, cx, _full((2, d))],
        out_specs=[pl.BlockSpec((tm, d), lambda i: (jnp.minimum(i, nl - 1), 0)), _full((8, d))],
        out_shape=[jax.ShapeDtypeStruct((l, d), F32), jax.ShapeDtypeStruct((8, d), F32)],
        scratch=[pltpu.VMEM(w_in.shape, BF)], args=(dp42, w_in, x, ctx, dr_l, dr_c, a2), name="l0_bwd_inproj")
    return gx, acc, extra


def _dw_inproj0(x, ctx, a2, b2, dp42, tm):
    l, d = x.shape
    lc = ctx.shape[0]
    assert lc == tm
    tl = 4 * tm if l % (4 * tm) == 0 else tm
    nl = l // tl
    half = dp42.shape[-1]
    e = 2 * half

    def body(x_ref, c_ref, a_ref, b_ref, dpl_ref, dpc_ref, o_ref, acc_ref):
        i = pl.program_id(1)

        @pl.when(i == 0)
        def _():
            acc_ref[...] = jnp.zeros_like(acc_ref)

        def add(rows_ref, dp_ref, sel):
            h = (rows_ref[...] * a_ref[sel:sel + 1, :] + b_ref[sel:sel + 1, :]).astype(BF)
            acc_ref[:, :half] += _dot_tn(h, dp_ref[0, 0])
            acc_ref[:, half:] += _dot_tn(h, dp_ref[0, 1])

        @pl.when(i < nl)
        def _():
            add(x_ref, dpl_ref, 0)

        @pl.when(i == nl)
        def _():
            add(c_ref, dpc_ref, 1)
            o_ref[...] = acc_ref[...].astype(BF)

    return pl.pallas_call(
        body, name="l0_dw_inproj", grid=(4, nl + 1),
        in_specs=[pl.BlockSpec((tl, d), lambda k, i: (jnp.minimum(i, nl - 1), 0)), _full((lc, d)),
                  _full((2, d)), _full((2, d)),
                  pl.BlockSpec((1, 2, tl, half), lambda k, i: (k, 0, jnp.minimum(i, nl - 1), 0)),
                  pl.BlockSpec((1, 2, lc, half), lambda k, i: (k, 0, l // lc, 0))],
        out_specs=pl.BlockSpec((d, e), lambda k, i: (0, k)),
        out_shape=jax.ShapeDtypeStruct((d, 4 * e), BF),
        scratch_shapes=[pltpu.VMEM((d, e), F32)], compiler_params=_cparams(),
    )(x, ctx, a2, b2, dp42, dp42)


def _dw_outproj0(q3, dr_l, dr_c, gt2, tm):
    l, d = dr_l.shape
    nl, nc = l // tm, dr_c.shape[0] // tm
    _, r, half = q3.shape
    nt = nl + nc

    def body(q_ref, dl_ref, dc_ref, g_ref, o_ref, acc_ref):
        i = pl.program_id(0)
        is_ctx = i >= nl

        @pl.when(i == 0)
        def _():
            acc_ref[...] = jnp.zeros_like(acc_ref)

        dr = jnp.where(is_ctx, dc_ref[...], dl_ref[...]).astype(F32)
        dfx = (dr * _sel_row(g_ref, is_ctx)).astype(BF)
        acc_ref[:half, :] += _dot_tn(q_ref[0], dfx)
        acc_ref[half:, :] += _dot_tn(q_ref[1], dfx)

        @pl.when(i == nt - 1)
        def _():
            o_ref[...] = acc_ref[...].astype(BF)

    lat, cx = _lat_or_ctx_specs(tm, d, nl, 1, 0)
    return pl.pallas_call(
        body, name="l0_dw_outproj", grid=(nt,),
        in_specs=[pl.BlockSpec((2, tm, half), lambda i: (0, i, 0)), lat, cx, _full((2, d))],
        out_specs=_full((2 * half, d)), out_shape=jax.ShapeDtypeStruct((2 * half, d), BF),
        scratch_shapes=[pltpu.VMEM((2 * half, d), F32)], compiler_params=_cparams(),
    )(q3, dr_l, dr_c, gt2)


def _cr_tile(j, cap=256):
    for cand in (1024, 512, 256, 128, 64, 32, 16, 8):
        if cand <= cap and j % cand == 0:
            return cand
    raise ValueError(j)


def _inproj1(xh_cr, a1, b1, w, tag):
    j, d16 = xh_cr.shape
    d = d16 // CHUNK
    e = w.shape[1] // 2
    nb = e // LANE_BLOCK
    tj = _cr_tile(j)

    def body(x_ref, a_ref, b_ref, w_hbm, u_ref, z_ref, w_ref):
        @pl.when(jnp.logical_and(pl.program_id(0) == 0, pl.program_id(1) == 0))
        def _():
            pltpu.sync_copy(w_hbm, w_ref)

        h = (x_ref[...] * a_ref[...] + b_ref[...]).astype(BF)
        r = _dot(h, w_ref[...])
        for b in range(nb):
            u_ref[b] = r[:, b * LANE_BLOCK:(b + 1) * LANE_BLOCK].astype(BF)
        z_ref[...] = r[:, e:].astype(BF)

    return pl.pallas_call(
        body, name="l1_inproj_" + tag, grid=(j // tj, CHUNK),
        in_specs=[pl.BlockSpec((tj, d), lambda t, s: (t, s)), _full((1, d)), _full((1, d)), ANY],
        out_specs=[pl.BlockSpec((nb, tj, LANE_BLOCK), lambda t, s: (0, t, s)), pl.BlockSpec((tj, e), lambda t, s: (t, s))],
        out_shape=[jax.ShapeDtypeStruct((nb, j, BCR_W), BF), jax.ShapeDtypeStruct((j, CHUNK * e), BF)],
        scratch_shapes=[pltpu.VMEM(w.shape, BF)], compiler_params=_cparams(),
    )(xh_cr, a1, b1, w)


def _bmm(a_list, w_list, trans, out_dtype, name):
    nb, j, ka = a_list[0].shape
    n_out = w_list[0].shape[1] if trans[0] else w_list[0].shape[2]
    tn = n_out // 2
    tj = _cr_tile(j, 512)
    n = len(a_list)

    def body(*refs):
        o_ref = refs[2 * n]
        acc = None
        for i in range(n):
            a = refs[i][0].astype(BF)
            w = refs[n + i][0]
            t = _dot_nt(a, w) if trans[i] else _dot(a, w)
            acc = t if acc is None else acc + t
        o_ref[0] = acc.astype(out_dtype)

    a_specs = [pl.BlockSpec((1, tj, a.shape[2]), lambda b, h, t: (b, t, 0)) for a in a_list]
    w_specs = [pl.BlockSpec((1, tn, w.shape[2]), lambda b, h, t: (b, h, 0)) if tr
               else pl.BlockSpec((1, w.shape[1], tn), lambda b, h, t: (b, 0, h)) for w, tr in zip(w_list, trans)]
    return pl.pallas_call(
        body, name=name, grid=(nb, 2, j // tj), in_specs=a_specs + w_specs,
        out_specs=pl.BlockSpec((1, tj, tn), lambda b, h, t: (b, t, h)),
        out_shape=jax.ShapeDtypeStruct((nb, j, n_out), out_dtype), compiler_params=_cparams(),
    )(*a_list, *w_list)


def _group_mask(lane_groups):
    row = lax.broadcasted_iota(jnp.int32, (LANE_BLOCK, LANE_BLOCK), 0) // S5_P
    lane = lax.broadcasted_iota(jnp.int32, (LANE_BLOCK, LANE_BLOCK), 1)
    return row == lane_groups(lane)


def _expand_toeplitz(wcomp):
    nb = wcomp.shape[0]
    nd = 2 * CHUNK - 1

    def body(c_ref, o_ref):
        mask = _group_mask(lambda lane: lane // S5_P)
        tiles = [jnp.where(mask, c_ref[0, dd], 0.0).astype(BF) for dd in range(nd)]
        for s in range(CHUNK):
            for t in range(CHUNK):
                o_ref[0, s * LANE_BLOCK:(s + 1) * LANE_BLOCK, t * LANE_BLOCK:(t + 1) * LANE_BLOCK] = tiles[t - s + CHUNK - 1]

    return pl.pallas_call(
        body, name="l1_expand_toeplitz", grid=(nb,),
        in_specs=[pl.BlockSpec((1, nd, LANE_BLOCK, LANE_BLOCK), lambda b: (b, 0, 0, 0))],
        out_specs=pl.BlockSpec((1, BCR_W, BCR_W), lambda b: (b, 0, 0)),
        out_shape=jax.ShapeDtypeStruct((nb, BCR_W, BCR_W), BF), compiler_params=_cparams(),
    )(wcomp)


def _expand_blocks(comp, name):
    nb = comp.shape[0]
    lanes_per_dir = ZL_W // 2

    def body(c_ref, o_ref):
        masks = [_group_mask(lambda lane, lb=lb: 2 * lb + lane // S5_N) for lb in range(4)]
        for r in range(2):
            for s in range(CHUNK):
                for ri in range(2):
                    m = c_ref[0, r, s, :, ri * S5_N:(ri + 1) * S5_N]
                    mm = jnp.concatenate([m, m], axis=1)
                    for lb in range(4):
                        c0 = r * lanes_per_dir + ri * ZH + lb * LANE_BLOCK
                        o_ref[0, s * LANE_BLOCK:(s + 1) * LANE_BLOCK, c0:c0 + LANE_BLOCK] = (
                            jnp.where(masks[lb], mm, 0.0).astype(BF))

    return pl.pallas_call(
        body, name=name, grid=(nb,),
        in_specs=[pl.BlockSpec((1, 2, CHUNK, LANE_BLOCK, LANE_BLOCK), lambda b: (b, 0, 0, 0, 0))],
        out_specs=pl.BlockSpec((1, BCR_W, ZL_W), lambda b: (b, 0, 0)),
        out_shape=jax.ShapeDtypeStruct((nb, BCR_W, ZL_W), BF), compiler_params=_cparams(),
    )(comp)


def _bdw(a, b_, kind, init, name):
    nb, j, ka = a.shape
    kb = b_.shape[2]
    tn = kb // 2
    tj = _cr_tile(j, 1024)
    nt = j // tj
    has_init = init is not None
    nd = 2 * CHUNK - 1

    def body(*refs):
        a_ref, b_ref = refs[0], refs[1]
        o_ref, acc_ref = refs[-2], refs[-1]
        h, t = pl.program_id(1), pl.program_id(2)

        @pl.when(t == 0)
        def _():
            acc_ref[...] = jnp.zeros_like(acc_ref)

        acc_ref[...] += _dot_tn(a_ref[0].astype(BF), b_ref[0].astype(BF))

        if kind == "toeplitz":
            @pl.when(jnp.logical_and(t == 0, h == 0))
            def _():
                o_ref[...] = jnp.zeros_like(o_ref)

            @pl.when(t == nt - 1)
            def _():
                mask = _group_mask(lambda lane: lane // S5_P)
                for s in range(CHUNK):
                    for tl in range(CHUNK // 2):
                        dd = h * (CHUNK // 2) + (tl - s + CHUNK - 1)
                        blk = acc_ref[s * LANE_BLOCK:(s + 1) * LANE_BLOCK, tl * LANE_BLOCK:(tl + 1) * LANE_BLOCK]
                        o_ref[0, dd] += jnp.where(mask, blk, 0.0)
        else:
            @pl.when(t == nt - 1)
            def _():
                masks = [_group_mask(lambda lane, lb=lb: 2 * lb + lane // S5_N) for lb in range(4)]
                for s in range(CHUNK):
                    for ri in range(2):
                        v = None
                        for lb in range(4):
                            c0 = ri * ZH + lb * LANE_BLOCK
                            blk = acc_ref[s * LANE_BLOCK:(s + 1) * LANE_BLOCK, c0:c0 + LANE_BLOCK]
                            blk = jnp.where(masks[lb], blk, 0.0)
                            v = blk if v is None else v + blk
                        folded = v[:, :S5_N] + v[:, S5_N:]
                        if has_init:
                            folded = folded + refs[2][0, 0, s, :, ri * S5_N:(ri + 1) * S5_N]
                        o_ref[0, 0, s, :, ri * S5_N:(ri + 1) * S5_N] = folded

    in_specs = [pl.BlockSpec((1, tj, ka), lambda b, h, t: (b, t, 0)), pl.BlockSpec((1, tj, tn), lambda b, h, t: (b, t, h))]
    args = [a, b_]
    if kind == "toeplitz":
        ospec = pl.BlockSpec((1, nd, LANE_BLOCK, LANE_BLOCK), lambda b, h, t: (b, 0, 0, 0))
        oshape = jax.ShapeDtypeStruct((nb, nd, LANE_BLOCK, LANE_BLOCK), F32)
    else:
        ospec = pl.BlockSpec((1, 1, CHUNK, LANE_BLOCK, LANE_BLOCK), lambda b, h, t: (b, h, 0, 0, 0))
        oshape = jax.ShapeDtypeStruct((nb, 2, CHUNK, LANE_BLOCK, LANE_BLOCK), F32)
        if has_init:
            in_specs.append(ospec)
            args.append(init)
    return pl.pallas_call(
        body, name=name, grid=(nb, 2, nt), in_specs=in_specs, out_specs=ospec, out_shape=oshape,
        scratch_shapes=[pltpu.VMEM((ka, tn), F32)], compiler_params=_cparams(),
    )(*args)


def _scan(z_l, z_c, coef, chains, conj, s_l=None, s_c=None, name="l1_scan"):
    nb, jl, _ = z_l.shape
    jc = z_c.shape[1]
    with_da = s_l is not None
    sign = -1.0 if conj else 1.0
    hw = 2 * ZH

    def body(*refs):
        zl_ref, zc_ref, cf_ref = refs[:3]
        k = 3
        if with_da:
            sl_ref, sc_ref = refs[3:5]
            k = 5
        ol_ref, oc_ref = refs[k:k + 2]
        d = pl.program_id(1)
        rowi = lax.broadcasted_iota(jnp.int32, (8, ZH), 0)

        def coef_rows(r0, nr):
            return cf_ref[0, 0, r0:r0 + nr, :ZH], sign * cf_ref[0, 0, r0:r0 + nr, ZH:]

        steps = [(1, coef_rows(0, 1)), (2, coef_rows(1, 1)), (4, coef_rows(2, 1))]

        def run(chain):
            carry = (jnp.zeros((1, ZH), F32), jnp.zeros((1, ZH), F32))
            da = (jnp.zeros((8, ZH), F32), jnp.zeros((8, ZH), F32))
            for which, rev in chain:
                src, dst = (zc_ref, oc_ref) if which == "c" else (zl_ref, ol_ref)
                sref = (sc_ref if which == "c" else sl_ref) if with_da else None
                ng = (jc if which == "c" else jl) // 8
                tr, ti = coef_rows(16, 8) if rev else coef_rows(8, 8)

                def step(it, st, src=src, dst=dst, sref=sref, ng=ng, tr=tr, ti=ti, rev=rev):
                    cr_, ci_, dar, dai = st
                    g = (ng - 1 - it) if rev else it
                    off = pl.multiple_of(g * 8, 8)
                    xr = src[0, pl.ds(off, 8), :ZH]
                    xi = src[0, pl.ds(off, 8), ZH:]
                    for sh, (ar, ai) in steps:
                        if rev:
                            keep = rowi < 8 - sh
                            sr = jnp.where(keep, pltpu.roll(xr, 8 - sh, 0), 0.0)
                            si = jnp.where(keep, pltpu.roll(xi, 8 - sh, 0), 0.0)
                        else:
                            keep = rowi >= sh
                            sr = jnp.where(keep, pltpu.roll(xr, sh, 0), 0.0)
                            si = jnp.where(keep, pltpu.roll(xi, sh, 0), 0.0)
                        xr, xi = xr + ar * sr - ai * si, xi + ar * si + ai * sr
                    ir = xr + tr * cr_ - ti * ci_
                    ii = xi + tr * ci_ + ti * cr_
                    if rev:
                        er = jnp.where(rowi == 7, cr_, pltpu.roll(ir, 7, 0))
                        ei = jnp.where(rowi == 7, ci_, pltpu.roll(ii, 7, 0))
                        ncr, nci = ir[0:1], ii[0:1]
                    else:
                        er = jnp.where(rowi == 0, cr_, pltpu.roll(ir, 1, 0))
                        ei = jnp.where(rowi == 0, ci_, pltpu.roll(ii, 1, 0))
                        ncr, nci = ir[7:8], ii[7:8]
                    dst[0, pl.ds(off, 8), :ZH] = er
                    dst[0, pl.ds(off, 8), ZH:] = ei
                    if sref is not None:
                        s_r = sref[0, pl.ds(off, 8), :ZH]
                        s_i = sref[0, pl.ds(off, 8), ZH:]
                        dar = dar + s_r * er + s_i * ei
                        dai = dai + s_r * ei - s_i * er
                    return ncr, nci, dar, dai

                carry_da = lax.fori_loop(0, ng, step, (*carry, *da))
                carry, da = carry_da[:2], carry_da[2:]
            if with_da:
                refs[k + 2][0, 0] = jnp.concatenate([da[0], da[1]], axis=1)

        for dd in range(2):
            @pl.when(d == dd)
            def _(dd=dd):
                run(chains[dd])

    zspec_l = pl.BlockSpec((1, jl, hw), lambda b, d: (b, 0, d))
    zspec_c = pl.BlockSpec((1, jc, hw), lambda b, d: (b, 0, d))
    in_specs = [zspec_l, zspec_c, pl.BlockSpec((1, 1, 24, hw), lambda b, d: (b, d, 0, 0))]
    args = [z_l, z_c, coef]
    out_specs = [zspec_l, zspec_c]
    out_shape = [jax.ShapeDtypeStruct(z_l.shape, F32), jax.ShapeDtypeStruct(z_c.shape, F32)]
    if with_da:
        in_specs += [zspec_l, zspec_c]
        args += [s_l, s_c]
        out_specs.append(pl.BlockSpec((1, 1, 8, hw), lambda b, d: (b, d, 0, 0)))
        out_shape.append(jax.ShapeDtypeStruct((nb, 2, 8, hw), F32))
    return pl.pallas_call(body, name=name, grid=(nb, 2), in_specs=in_specs, out_specs=out_specs,
                          out_shape=out_shape, compiler_params=_cparams())(*args)


def _glu_fwd(y_bcr, z_cr, w_glu, b_glu):
    nb, j, _ = y_bcr.shape
    e = nb * LANE_BLOCK
    tj = _cr_tile(j)

    def body(y_ref, z_ref, w_hbm, b_ref, o_ref, sg_ref, w_ref):
        @pl.when(jnp.logical_and(pl.program_id(0) == 0, pl.program_id(1) == 0))
        def _():
            pltpu.sync_copy(w_hbm, w_ref)

        y = jnp.concatenate([y_ref[b] for b in range(nb)], axis=1).astype(F32)
        g = _gelu_parts(y)[0]
        sg = _sigmoid(_dot(g.astype(BF), w_ref[...]) + b_ref[...])
        z = z_ref[...].astype(F32)
        o_ref[...] = (g * sg * (z * _sigmoid(z))).astype(BF)
        sg_ref[...] = sg.astype(BF)

    tok = pl.BlockSpec((tj, e), lambda t, s: (t, s))
    return pl.pallas_call(
        body, name="l1_glu_fwd", grid=(j // tj, CHUNK),
        in_specs=[pl.BlockSpec((nb, tj, LANE_BLOCK), lambda t, s: (0, t, s)), tok, ANY, _full((1, e))],
        out_specs=[tok, tok],
        out_shape=[jax.ShapeDtypeStruct((j, CHUNK * e), BF), jax.ShapeDtypeStruct((j, CHUNK * e), BF)],
        scratch_shapes=[pltpu.VMEM(w_glu.shape, BF)], compiler_params=_cparams(),
    )(y_bcr, z_cr, w_glu, b_glu)


def _final(w_cr, w_out, xh_cr, tgt_cr, vecs):
    j, e16 = w_cr.shape
    e = e16 // CHUNK
    d = w_out.shape[1]
    tj = _cr_tile(j)

    def body(w_ref, wo_hbm, xh_ref, t_ref, v_ref, dr_ref, acc_ref, wo_ref):
        @pl.when(jnp.logical_and(pl.program_id(0) == 0, pl.program_id(1) == 0))
        def _():
            pltpu.sync_copy(wo_hbm, wo_ref)
            acc_ref[...] = jnp.zeros_like(acc_ref)

        o = _dot(w_ref[...], wo_ref[...])
        x1 = xh_ref[...] * v_ref[0:1, :] + v_ref[1:2, :]
        rr = DN_ALPHA * x1 + v_ref[2:3, :] * o
        mu = jnp.mean(rr, axis=-1, keepdims=True)
        cen = rr - mu
        rstd = lax.rsqrt(jnp.mean(cen * cen, axis=-1, keepdims=True) + LN_EPS)
        xh2 = cen * rstd
        err = xh2 * v_ref[3:4, :] + v_ref[4:5, :] - t_ref[...]
        dy = err * (1.0 / d)
        dxh = dy * v_ref[3:4, :]
        dr = rstd * (dxh - jnp.mean(dxh, axis=-1, keepdims=True) - xh2 * jnp.mean(dxh * xh2, axis=-1, keepdims=True))
        dr_ref[...] = dr.astype(BF)
        acc_ref[0:1, :] += jnp.sum(dy * xh2, axis=0, keepdims=True)
        acc_ref[1:2, :] += jnp.sum(dy, axis=0, keepdims=True)
        acc_ref[2:3, :] += jnp.sum(dr * o, axis=0, keepdims=True)
        acc_ref[3:4, :] += (0.5 / d) * jnp.sum(err * err, axis=0, keepdims=True)

    tok_d = pl.BlockSpec((tj, d), lambda t, s: (t, s))
    return pl.pallas_call(
        body, name="l1_final", grid=(j // tj, CHUNK),
        in_specs=[pl.BlockSpec((tj, e), lambda t, s: (t, s)), ANY, tok_d, tok_d, _full((8, d))],
        out_specs=[tok_d, _full((8, d))],
        out_shape=[jax.ShapeDtypeStruct((j, CHUNK * d), BF), jax.ShapeDtypeStruct((8, d), F32)],
        scratch_shapes=[pltpu.VMEM(w_out.shape, BF)], compiler_params=_cparams(),
    )(w_cr, w_out, xh_cr, tgt_cr, vecs)


def _glu_bwd(dr_cr, gt1, w_out, w_glu, y_bcr, z_cr, sg_cr):
    nb, j, _ = y_bcr.shape
    e, d = w_out.shape
    tj = _cr_tile(j)

    def body(dr_ref, g_ref, wo_hbm, wg_hbm, y_ref, z_ref, sg_ref, dz_ref, dt_ref, dy_ref, wo_ref, wg_ref):
        @pl.when(jnp.logical_and(pl.program_id(0) == 0, pl.program_id(1) == 0))
        def _():
            pltpu.sync_copy(wo_hbm, wo_ref)
            pltpu.sync_copy(wg_hbm, wg_ref)

        do = (dr_ref[...].astype(F32) * g_ref[...]).astype(BF)
        dw = _dot_nt(do, wo_ref[...])
        y = jnp.concatenate([y_ref[b] for b in range(nb)], axis=1).astype(F32)
        g, dgel = _gelu_parts(y)
        z = z_ref[...].astype(F32)
        sz = _sigmoid(z)
        sg = sg_ref[...].astype(F32)
        dg2 = dw * (z * sz)
        dz_ref[...] = (dw * g * sg * (sz * (1.0 + z * (1.0 - sz)))).astype(BF)
        dt = (dg2 * g * sg * (1.0 - sg)).astype(BF)
        dt_ref[...] = dt
        dy = (dg2 * sg + _dot_nt(dt, wg_ref[...])) * dgel
        for b in range(nb):
            dy_ref[b] = dy[:, b * LANE_BLOCK:(b + 1) * LANE_BLOCK].astype(BF)

    tok_e = pl.BlockSpec((tj, e), lambda t, s: (t, s))
    blk = pl.BlockSpec((nb, tj, LANE_BLOCK), lambda t, s: (0, t, s))
    return pl.pallas_call(
        body, name="l1_glu_bwd", grid=(j // tj, CHUNK),
        in_specs=[pl.BlockSpec((tj, d), lambda t, s: (t, s)), _full((1, d)), ANY, ANY, blk, tok_e, tok_e],
        out_specs=[tok_e, tok_e, blk],
        out_shape=[jax.ShapeDtypeStruct((j, CHUNK * e), BF), jax.ShapeDtypeStruct((j, CHUNK * e), BF),
                   jax.ShapeDtypeStruct((nb, j, BCR_W), BF)],
        scratch_shapes=[pltpu.VMEM(w_out.shape, BF), pltpu.VMEM(w_glu.shape, BF)], compiler_params=_cparams(),
    )(dr_cr, gt1, w_out, w_glu, y_bcr, z_cr, sg_cr)


def _bwd_inproj1(du_bcr, dz_cr, w, xh_cr, rs_cr, dr2_cr, vecs, tag):
    nb, j, _ = du_bcr.shape
    d = w.shape[0]
    e = w.shape[1] // 2
    tj = _cr_tile(j)

    def body(du_ref, dz_ref, w_hbm, xh_ref, rs_ref, dr2_ref, v_ref, dr1_ref, acc_ref, w_ref):
        @pl.when(jnp.logical_and(pl.program_id(0) == 0, pl.program_id(1) == 0))
        def _():
            pltpu.sync_copy(w_hbm, w_ref)
            acc_ref[...] = jnp.zeros_like(acc_ref)

        du = jnp.concatenate([du_ref[b] for b in range(nb)], axis=1)
        dh = _dot_nt(du, w_ref[:, :e]) + _dot_nt(dz_ref[...], w_ref[:, e:])
        xh = xh_ref[...]
        x1 = xh * v_ref[0:1, :] + v_ref[1:2, :]
        dx1 = DN_ALPHA * dr2_ref[...].astype(F32) + dh * v_ref[2:3, :]
        dxh = dx1 * v_ref[0:1, :]
        rstd = rs_ref[:, 0:1]
        dr1 = rstd * (dxh - jnp.mean(dxh, axis=-1, keepdims=True) - xh * jnp.mean(dxh * xh, axis=-1, keepdims=True))
        dr1_ref[...] = dr1.astype(BF)
        acc_ref[0:1, :] += jnp.sum(dh * x1, axis=0, keepdims=True)
        acc_ref[1:2, :] += jnp.sum(dh, axis=0, keepdims=True)
        acc_ref[2:3, :] += jnp.sum(dx1 * xh, axis=0, keepdims=True)
        acc_ref[3:4, :] += jnp.sum(dx1, axis=0, keepdims=True)

    tok_d = pl.BlockSpec((tj, d), lambda t, s: (t, s))
    return pl.pallas_call(
        body, name="l1_bwd_inproj_" + tag, grid=(j // tj, CHUNK),
        in_specs=[pl.BlockSpec((nb, tj, LANE_BLOCK), lambda t, s: (0, t, s)), pl.BlockSpec((tj, e), lambda t, s: (t, s)),
                  ANY, tok_d, pl.BlockSpec((tj, 128), lambda t, s: (t, s)), tok_d, _full((8, d))],
        out_specs=[tok_d, _full((8, d))],
        out_shape=[jax.ShapeDtypeStruct((j, CHUNK * d), BF), jax.ShapeDtypeStruct((8, d), F32)],
        scratch_shapes=[pltpu.VMEM(w.shape, BF)], compiler_params=_cparams(),
    )(du_bcr, dz_cr, w, xh_cr, rs_cr, dr2_cr, vecs)


def _dw_cr(lhs, rhs, lhs_kind, rhs_kind, vec, bias_sum, init, name):
    if lhs_kind == "gelu_bcr":
        nb_l, j, _ = lhs.shape
        k = nb_l * LANE_BLOCK
    else:
        j = lhs.shape[0]
        k = lhs.shape[1] // CHUNK
    if rhs_kind == "bcr":
        nb_r = rhs.shape[0]
        n = nb_r * LANE_BLOCK
    else:
        n = rhs.shape[1] // CHUNK
    tj = _cr_tile(j, 512)
    nh = 2 if k * n * 4 > (8 << 20) else 1
    tn = n // nh
    nbh = tn // LANE_BLOCK
    nt = j // tj
    has_init = init is not None

    def body(*refs):
        refs = list(refs)
        l_ref, r_ref = refs[0], refs[1]
        pos = 2
        v_ref = None
        if vec is not None:
            v_ref = refs[pos]
            pos += 1
        i_ref = None
        if has_init:
            i_ref = refs[pos]
            pos += 1
        o_ref = refs[pos]
        pos += 1
        bs_ref = None
        if bias_sum:
            bs_ref = refs[pos]
            pos += 1
        acc_ref = refs[pos]
        t, s = pl.program_id(1), pl.program_id(2)
        first = jnp.logical_and(t == 0, s == 0)

        @pl.when(first)
        def _():
            acc_ref[...] = i_ref[...] if has_init else jnp.zeros_like(acc_ref)
            if bias_sum:
                bs_ref[...] = jnp.zeros_like(bs_ref)

        if lhs_kind == "gelu_bcr":
            y = jnp.concatenate([l_ref[b] for b in range(nb_l)], axis=1).astype(F32)
            lv = _gelu_parts(y)[0].astype(BF)
        elif lhs_kind == "mod":
            lv = (l_ref[...] * v_ref[0:1, :] + v_ref[1:2, :]).astype(BF)
        else:
            lv = l_ref[...]
        if rhs_kind == "bcr":
            rv = jnp.concatenate([r_ref[b] for b in range(nbh)], axis=1)
        elif rhs_kind == "scaled":
            rv = (r_ref[...].astype(F32) * v_ref[0:1, :]).astype(BF)
        else:
            rv = r_ref[...]
        acc_ref[...] += _dot_tn(lv, rv)
        if bias_sum:
            bs_ref[0:1, :] += jnp.sum(rv.astype(F32), axis=0, keepdims=True)

        @pl.when(jnp.logical_and(t == nt - 1, s == CHUNK - 1))
        def _():
            o_ref[...] = acc_ref[...].astype(BF)

    if lhs_kind == "gelu_bcr":
        l_spec = pl.BlockSpec((nb_l, tj, LANE_BLOCK), lambda h, t, s: (0, t, s))
    else:
        l_spec = pl.BlockSpec((tj, k), lambda h, t, s: (t, s))
    if rhs_kind == "bcr":
        r_spec = pl.BlockSpec((nbh, tj, LANE_BLOCK), lambda h, t, s: (h, t, s))
    else:
        r_spec = pl.BlockSpec((tj, tn), lambda h, t, s: (t, s * nh + h))
    in_specs, args = [l_spec, r_spec], [lhs, rhs]
    if vec is not None:
        in_specs.append(_full(vec.shape))
        args.append(vec)
    o_spec = pl.BlockSpec((k, tn), lambda h, t, s: (0, h))
    if has_init:
        in_specs.append(o_spec)
        args.append(init)
    out_specs, out_shape = [o_spec], [jax.ShapeDtypeStruct((k, n), BF)]
    if bias_sum:
        out_specs.append(pl.BlockSpec((8, tn), lambda h, t, s: (0, h)))
        out_shape.append(jax.ShapeDtypeStruct((8, n), F32))
    res = pl.pallas_call(
        body, name=name, grid=(nh, nt, CHUNK), in_specs=in_specs, out_specs=out_specs, out_shape=out_shape,
        scratch_shapes=[pltpu.VMEM((k, tn), F32)], compiler_params=_cparams(),
    )(*args)
    return res if bias_sum else res[0]


def _dw_cr_f32(lhs, rhs, vec, name):
    j = lhs.shape[0]
    k = lhs.shape[1] // CHUNK
    nb_r = rhs.shape[0]
    n = nb_r * LANE_BLOCK
    tj = _cr_tile(j)
    nt = j // tj

    def body(l_ref, r_ref, v_ref, o_ref):
        @pl.when(jnp.logical_and(pl.program_id(0) == 0, pl.program_id(1) == 0))
        def _():
            o_ref[...] = jnp.zeros_like(o_ref)

        lv = (l_ref[...] * v_ref[0:1, :] + v_ref[1:2, :]).astype(BF)
        rv = jnp.concatenate([r_ref[b] for b in range(nb_r)], axis=1)
        o_ref[...] += _dot_tn(lv, rv)

    return pl.pallas_call(
        body, name=name, grid=(nt, CHUNK),
        in_specs=[pl.BlockSpec((tj, k), lambda t, s: (t, s)), pl.BlockSpec((nb_r, tj, LANE_BLOCK), lambda t, s: (0, t, s)),
                  _full(vec.shape)],
        out_specs=_full((k, n)), out_shape=jax.ShapeDtypeStruct((k, n), F32), compiler_params=_cparams(),
    )(lhs, rhs, vec)


def _s5_compact(lam_re, lam_im, log_step, b_re, b_im, c_re, c_im, d_skip):
    hp = lax.Precision.HIGHEST
    g = lam_re.shape[1]
    nb = g // GROUPS_PER_BLOCK
    t, p, n = CHUNK, S5_P, S5_N
    dt = jnp.exp(log_step)[..., None]
    ks = jnp.arange(t + 1, dtype=F32).reshape(t + 1, 1, 1, 1)
    mag = jnp.exp(ks * (lam_re * dt)[None])
    ang = ks * (lam_im * dt)[None]
    pr, pi = mag * jnp.cos(ang), mag * jnp.sin(ang)
    ar, ai = pr[1], pi[1]
    qr, qi = ar - 1.0, ai
    den = lam_re * lam_re + lam_im * lam_im
    fr = (qr * lam_re + qi * lam_im) / den
    fi = (qi * lam_re - qr * lam_im) / den
    bt_re, bt_im = b_re.transpose(0, 1, 3, 2), b_im.transpose(0, 1, 3, 2)
    bbr = fr[:, :, None, :] * bt_re - fi[:, :, None, :] * bt_im
    bbi = fr[:, :, None, :] * bt_im + fi[:, :, None, :] * bt_re
    pk_r, pk_i = pr[:t, :, :, None, :], pi[:t, :, :, None, :]
    abr = pk_r * bbr[None] - pk_i * bbi[None]
    abi = pk_r * bbi[None] + pk_i * bbr[None]
    kd = (jnp.einsum("rgpn,krgqn->rgkpq", c_re, abr, precision=hp)
          - jnp.einsum("rgpn,krgqn->rgkpq", c_im, abi, precision=hp))
    skip = jnp.eye(p, dtype=F32)[None] * d_skip.reshape(g, p)[:, :, None]
    diag = kd[0][:, 0] + kd[1][:, 0] + skip
    qd = jnp.concatenate([jnp.flip(kd[1][:, 1:], axis=1), diag[:, None], kd[0][:, 1:]], axis=1)
    nd = 2 * t - 1
    wc = qd.transpose(0, 1, 3, 2).reshape(nb, GROUPS_PER_BLOCK, nd, p, p).transpose(0, 2, 1, 3, 4)
    wcomp = jnp.broadcast_to(wc[:, :, :, :, None, :], (nb, nd, GROUPS_PER_BLOCK, p, GROUPS_PER_BLOCK, p))
    wcomp = wcomp.reshape(nb, nd, LANE_BLOCK, LANE_BLOCK)
    ab = jnp.concatenate([abr, abi], axis=-1)
    bcc = jnp.stack([jnp.flip(ab[:, 0], axis=0), ab[:, 1]])
    bcomp = bcc.reshape(2, t, nb, LANE_BLOCK, 2 * n).transpose(2, 0, 1, 3, 4)
    prf = jnp.stack([pr[1:, 0], jnp.flip(pr[1:, 1], axis=0)])[:, :, :, None, :]
    pif = jnp.stack([pi[1:, 0], jnp.flip(pi[1:, 1], axis=0)])[:, :, :, None, :]
    cr_t = c_re[:, None]
    ci_t = c_im[:, None]
    ccc = jnp.concatenate([cr_t * prf - ci_t * pif, -(cr_t * pif + ci_t * prf)], axis=-1)
    ccomp = ccc.reshape(2, t, nb, LANE_BLOCK, 2 * n).transpose(2, 0, 1, 3, 4)
    return wcomp, bcomp, ccomp, pr[t], pi[t]


def _scan_coef(lam_re, lam_im, log_step):
    g = lam_re.shape[1]
    nb = g // GROUPS_PER_BLOCK
    ms = jnp.array([1, 2, 4, 0, 0, 0, 0, 0] + list(range(1, 9)) + list(range(8, 0, -1)), F32) * CHUNK
    dt = jnp.exp(log_step)[..., None]
    mag = jnp.exp(ms.reshape(-1, 1, 1, 1) * (lam_re * dt)[None])
    ang = ms.reshape(-1, 1, 1, 1) * (lam_im * dt)[None]
    cr, ci = mag * jnp.cos(ang), mag * jnp.sin(ang)
    lay = lambda a: a.reshape(24, 2, nb, ZH).transpose(2, 1, 0, 3)
    return jnp.concatenate([lay(cr), lay(ci)], axis=-1)


def _to_cr(a):
    return a.reshape(a.shape[0] // CHUNK, CHUNK * a.shape[1])


def _from_cr(a, c):
    return a.reshape(a.shape[0] * CHUNK, c)


def _pad8(v):
    return jnp.concatenate([v, jnp.zeros((8 - v.shape[0], v.shape[1]), v.dtype)], axis=0)


def _local_step(x, c, ctx, c_ctx, loss_target, w, late=None, scatter=False):
    l, d = x.shape
    lc = ctx.shape[0]
    tm = min(256, lc)
    assert lc == tm and l % tm == 0 and tm % GRID_W == 0 and (tm & (tm - 1)) == 0
    nl = l // tm

    c8 = _pad8(jnp.stack([c, c_ctx]))
    mod = _ada_fwd(c8, w["ada_w"], w["ada_b"])
    sh = mod[:, :2, :d]
    sc = mod[:, :2, d:2 * d]
    gt = mod[:, :2, 2 * d:]
    ln_g, ln_b = w["ln_g"], w["ln_b"]

    a0, b0 = 1.0 + sc[0], sh[0]
    xch = _Exchange("gather", [late[n][0] for n in late], [late[n][1] for n in late]) if late else None
    p42, got = _inproj0(x, ctx, a0, b0, w["conv_w_in"], tm, xch)
    if late:
        w = dict(w, **dict(zip(late, got)))
    e = w["conv_w_out"].shape[0]
    half = e // 2
    nb = e // LANE_BLOCK
    tc = min(512, half)
    cw = w["conv_w"].reshape(3, 2, half)
    q3 = _conv_fwd(p42, cw, nl, tm, tc)
    xh1, rs1, fx = _outproj_ln0(q3, w["conv_w_out"], x, ctx, gt[0], tm)
    xh1_l, xh1_c = _to_cr(xh1[:l]), _to_cr(xh1[l:])
    rs1_l, rs1_c = _to_cr(rs1[:l]), _to_cr(rs1[l:])
    jl, jc = l // CHUNK, lc // CHUNK

    g0, bb0 = ln_g[0:1], ln_b[0:1]
    a1 = g0 * (1.0 + sc[1])
    b1 = bb0 * (1.0 + sc[1]) + sh[1]
    u_l, z_l = _inproj1(xh1_l, a1[0:1], b1[0:1], w["ssm_w_in"], "lat")
    u_c, _ = _inproj1(xh1_c, a1[1:2], b1[1:2], w["ssm_w_in"], "ctx")
    s5 = (w["ssm_lam_re"], w["ssm_lam_im"], w["ssm_log_step"], w["ssm_b_re"], w["ssm_b_im"],
          w["ssm_c_re"], w["ssm_c_im"], w["ssm_d"])
    (wcomp, bcomp, ccomp, _, _), s5_vjp = jax.vjp(_s5_compact, *s5)
    wbig_b = _expand_toeplitz(wcomp)
    bc_b = _expand_blocks(bcomp, "l1_expand_bc")
    cct_b = _expand_blocks(ccomp, "l1_expand_cc")
    coef = lax.stop_gradient(_scan_coef(*s5[:3]))
    zz_l = _bmm([u_l], [bc_b], [False], F32, "l1_s5_z_lat")
    zz_c = _bmm([u_c], [bc_b], [False], F32, "l1_s5_z_ctx")
    fwd_chains = ((("c", False), ("l", False)), (("c", True), ("l", True)))
    st_l, st_c = _scan(zz_l, zz_c, coef, fwd_chains, False, name="l1_scan_fwd")
    y_l = _bmm([u_l, st_l], [wbig_b, cct_b], [False, True], BF, "l1_s5_y")
    b_glu = w["ssm_b_glu"].reshape(1, e)
    w_cr, sg_cr = _glu_fwd(y_l, z_l, w["ssm_w_glu"], b_glu)
    vec_f = _pad8(jnp.concatenate([g0, bb0, gt[1][0:1], ln_g[1:2], ln_b[1:2]], axis=0))
    dr2, acc_f = _final(w_cr, w["ssm_w_out"], xh1_l, _to_cr(loss_target), vec_f)
    loss = jnp.sum(acc_f[3])

    gt1 = gt[1][0:1]
    dz_l, dt_l, dy_l = _glu_bwd(dr2, gt1, w["ssm_w_out"], w["ssm_w_glu"], y_l, z_l, sg_cr)
    g_w_out = _dw_cr(w_cr, dr2, "cr", "scaled", gt1, False, None, "l1_dw_out")
    g_w_glu, bsum = _dw_cr(y_l, dt_l, "gelu_bcr", "cr", None, True, None, "l1_dw_glu")
    g_b_glu = bsum[0]
    ds_l = _bmm([dy_l], [cct_b], [False], F32, "l1_s5_ds")
    bwd_chains = ((("l", True), ("c", True)), (("l", False), ("c", False)))
    dzz_l, dzz_c, da = _scan(ds_l, jnp.zeros_like(zz_c), coef, bwd_chains, True, st_l, st_c, name="l1_scan_bwd")
    du_l = _bmm([dy_l, dzz_l], [wbig_b, bc_b], [True, True], BF, "l1_s5_dx_lat")
    du_c = _bmm([dzz_c], [bc_b], [True], BF, "l1_s5_dx_ctx")
    d_wbig = _bdw(u_l, dy_l, "toeplitz", None, "l1_s5_dwbig")
    d_cc = _bdw(dy_l, st_l, "blocks", None, "l1_s5_dcc")
    d_bc = _bdw(u_l, dzz_l, "blocks", _bdw(u_c, dzz_c, "blocks", None, "l1_s5_dbc_ctx"), "l1_s5_dbc")
    da = jnp.sum(da, axis=2)
    unlay = lambda a: a.reshape(nb, 2, GROUPS_PER_BLOCK, S5_N).transpose(1, 0, 2, 3).reshape(2, nb * GROUPS_PER_BLOCK, S5_N)
    g_s5 = s5_vjp((d_wbig, d_bc, d_cc, unlay(da[..., :ZH]), unlay(da[..., ZH:])))

    vec_l = _pad8(jnp.concatenate([g0, bb0, 1.0 + sc[1][0:1]], axis=0))
    vec_c = _pad8(jnp.concatenate([g0, bb0, 1.0 + sc[1][1:2]], axis=0))
    dr1_l, acc_l = _bwd_inproj1(du_l, dz_l, w["ssm_w_in"], xh1_l, rs1_l, dr2, vec_l, "lat")
    dr1_c, acc_c = _bwd_inproj1(du_c, jnp.zeros((jc, CHUNK * e), BF), w["ssm_w_in"], xh1_c, rs1_c,
                                jnp.zeros((jc, CHUNK * d), BF), vec_c, "ctx")
    mod_l = jnp.concatenate([a1[0:1], b1[0:1]], axis=0)
    mod_c = jnp.concatenate([a1[1:2], b1[1:2]], axis=0)
    g_in_u = _dw_cr(xh1_l, du_l, "mod", "bcr", mod_l, False, _dw_cr_f32(xh1_c, du_c, mod_c, "l1_dw_in_u_ctx"), "l1_dw_in_u")
    g_in_z = _dw_cr(xh1_l, dz_l, "mod", "cr", mod_l, False, None, "l1_dw_in_z")
    g_w_in1 = jnp.concatenate([g_in_u, g_in_z], axis=1)

    dr1_ln, dr1_cn = _from_cr(dr1_l, d), _from_cr(dr1_c, d)
    dq3, acc_g0 = _bwd_outproj0(dr1_ln, dr1_cn, gt[0], w["conv_w_out"], fx, tm)
    sent1 = ["ssm_w_in", "ssm_w_glu", "ssm_w_out"]
    xch1 = _Exchange("scatter", [g_w_in1, g_w_glu, g_w_out], [BIG[n] for n in sent1]) if scatter else None
    dp42, dcw, recv1 = _conv_bwd(dq3, p42, cw, nl, tm, tc, xch1)
    g_w_in0 = _dw_inproj0(x, ctx, a0, b0, dp42, tm)
    g_w_out0 = _dw_outproj0(q3, dr1_ln, dr1_cn, gt[0], tm)
    sent0 = ["conv_w_in", "conv_w_out"]
    xch0 = _Exchange("scatter", [g_w_in0, g_w_out0], [BIG[n] for n in sent0]) if scatter else None
    grad_x, acc_0, recv0 = _bwd_inproj0(dp42, w["conv_w_in"], x, ctx, dr1_ln, dr1_cn, a0, tm, xch0)
    recv = dict(zip(sent1 + sent0, recv1 + recv0))

    zero = jnp.zeros((d,), F32)
    dm0 = jnp.stack([jnp.concatenate([acc_0[2], acc_0[0], acc_g0[0]]), jnp.concatenate([acc_0[3], acc_0[1], acc_g0[1]])])
    dm1 = jnp.stack([jnp.concatenate([acc_l[1], acc_l[0], acc_f[2]]), jnp.concatenate([acc_c[1], acc_c[0], zero])])
    dm8 = jnp.stack([_pad8(dm0), _pad8(dm1)])
    g_ada_w, dc8 = _ada_bwd(c8, w["ada_w"], dm8)

    grads = {
        "c_ctx": dc8[0, 1] + dc8[1, 1],
        "ada_w": g_ada_w,
        "ada_b": jnp.stack([dm0[0] + dm0[1], dm1[0] + dm1[1]]),
        "ln_g": jnp.stack([acc_l[2] + acc_c[2], acc_f[0]]),
        "ln_b": jnp.stack([acc_l[3] + acc_c[3], acc_f[1]]),
        "conv_w_in": g_w_in0, "conv_w": dcw[:3].reshape(3, e), "conv_w_out": g_w_out0,
        "ssm_w_in": g_w_in1,
        "ssm_lam_re": g_s5[0], "ssm_lam_im": g_s5[1], "ssm_log_step": g_s5[2],
        "ssm_b_re": g_s5[3], "ssm_b_im": g_s5[4], "ssm_c_re": g_s5[5], "ssm_c_im": g_s5[6], "ssm_d": g_s5[7],
        "ssm_w_glu": g_w_glu, "ssm_b_glu": g_b_glu, "ssm_w_out": g_w_out,
    }
    for n in recv:
        del grads[n]
    return loss, grad_x, grads, recv


WEIGHTS = ["c_ctx", "ada_w", "ada_b", "ln_g", "ln_b", "conv_w_in", "conv_w", "conv_w_out", "ssm_w_in",
           "ssm_lam_re", "ssm_lam_im", "ssm_log_step", "ssm_b_re", "ssm_b_im", "ssm_c_re", "ssm_c_im",
           "ssm_d", "ssm_w_glu", "ssm_b_glu", "ssm_w_out"]
BIG = {"ada_w": 1, "conv_w_in": 1, "conv_w_out": 0, "ssm_w_in": 1, "ssm_w_glu": 0, "ssm_w_out": 0}
SMALL_SHARDED = ["conv_w", "ssm_d", "ssm_b_glu"]
REPLICATED = ["c_ctx", "ada_b", "ln_g", "ln_b", "ssm_lam_re", "ssm_lam_im", "ssm_log_step",
              "ssm_b_re", "ssm_b_im", "ssm_c_re", "ssm_c_im"]


def _view2d(name, a):
    return a.reshape(-1, a.shape[-1])


def kernel(x, c, ctx, c_ctx, ada_w, ada_b, ln_g, ln_b, conv_w_in, conv_w, conv_w_out, ssm_w_in, ssm_lam_re, ssm_lam_im, ssm_log_step, ssm_b_re, ssm_b_im, ssm_c_re, ssm_c_im, ssm_d, ssm_w_glu, ssm_b_glu, ssm_w_out, loss_target, m_c_ctx, m_ada_w, m_ada_b, m_ln_g, m_ln_b, m_conv_w_in, m_conv_w, m_conv_w_out, m_ssm_w_in, m_ssm_lam_re, m_ssm_lam_im, m_ssm_log_step, m_ssm_b_re, m_ssm_b_im, m_ssm_c_re, m_ssm_c_im, m_ssm_d, m_ssm_w_glu, m_ssm_b_glu, m_ssm_w_out, v_c_ctx, v_ada_w, v_ada_b, v_ln_g, v_ln_b, v_conv_w_in, v_conv_w, v_conv_w_out, v_ssm_w_in, v_ssm_lam_re, v_ssm_lam_im, v_ssm_log_step, v_ssm_b_re, v_ssm_b_im, v_ssm_c_re, v_ssm_c_im, v_ssm_d, v_ssm_w_glu, v_ssm_b_glu, v_ssm_w_out):
    args = locals()
    wt = {n: args[n] for n in WEIGHTS}
    mt = {n: args["m_" + n] for n in WEIGHTS}
    vt = {n: args["v_" + n] for n in WEIGHTS}

    big_names = list(BIG)
    shard = {n: _view2d(n, wt[n]).astype(BF) for n in big_names}
    first = ["ada_w", "conv_w_in"]
    small = jnp.concatenate([wt["conv_w"][0], wt["ssm_d"], wt["ssm_b_glu"]], axis=0)
    small = jnp.concatenate([small, jnp.zeros((3, small.shape[1]), F32)], axis=0)
    gathered = _all_gather([shard[n] for n in first] + [small], [BIG[n] for n in first] + [1], "gather_weights")
    full = dict(zip(first, gathered[:-1]))
    small_full = gathered[-1]
    late = {n: (shard[n], BIG[n]) for n in big_names if n not in first}
    d = x.shape[-1]
    w = {
        "ada_w": full["ada_w"].reshape(2, d, 3 * d), "ada_b": ada_b, "ln_g": ln_g, "ln_b": ln_b,
        "conv_w_in": full["conv_w_in"], "conv_w": small_full[0:3],
        "ssm_lam_re": ssm_lam_re[0], "ssm_lam_im": ssm_lam_im[0],
        "ssm_log_step": ssm_log_step[0], "ssm_b_re": ssm_b_re[0], "ssm_b_im": ssm_b_im[0],
        "ssm_c_re": ssm_c_re[0], "ssm_c_im": ssm_c_im[0], "ssm_d": small_full[3], "ssm_b_glu": small_full[4],
    }

    loss, grad_x, g, recv_big = _local_step(x[0], c[0], ctx[0], c_ctx, loss_target[0], w, late, True)
    loss = lax.psum(loss, ("x", "y", "c"))

    blob_names = REPLICATED + SMALL_SHARDED
    flat = jnp.concatenate([g[n].reshape(-1).astype(F32) for n in blob_names])
    nflat = flat.shape[0]
    rows = -(-nflat // (N_DEV * 128 * 8)) * 8
    flat = jnp.concatenate([flat, jnp.zeros((N_DEV * rows * 128 - nflat,), F32)]).reshape(N_DEV * rows, 128)
    last = [n for n in big_names if n not in recv_big]
    recv = _all_to_all([_view2d(n, g[n]) for n in last] + [flat], [BIG[n] for n in last] + [0], "scatter_grads")
    recv_big.update(zip(last, recv[:-1]))
    blob_sum = _sum_partials(recv[-1])
    blob = _all_gather([blob_sum], [0], "gather_small_grads")[0].reshape(-1)
    small_g, off = {}, 0
    for n in blob_names:
        shape = wt[n].shape if n in REPLICATED else (*wt[n].shape[:-1], wt[n].shape[-1] * N_DEV)
        size = math.prod(shape)
        small_g[n] = blob[off:off + size].reshape(shape)
        off += size
    me = 4 * lax.axis_index("x") + 2 * lax.axis_index("y") + lax.axis_index("c")
    for n in SMALL_SHARDED:
        size = wt[n].shape[-1]
        small_g[n] = lax.dynamic_slice_in_dim(small_g[n], me * size, size, axis=small_g[n].ndim - 1)

    out_g, out_d, out_m, out_v = {}, {}, {}, {}
    for n in big_names:
        stack = recv_big[n]
        shp = wt[n].shape
        res = _adamw(stack, _view2d(n, wt[n]), _view2d(n, mt[n]), _view2d(n, vt[n]), "adamw_" + n)
        out_g[n], out_d[n], out_m[n], out_v[n] = [r.reshape(shp) for r in res]
    names = list(small_g)
    cat = lambda t: jnp.concatenate([t[n].reshape(-1) for n in names])
    gs, ws, ms, vs = cat(small_g), cat(wt), cat(mt), cat(vt)
    ns = gs.shape[0]
    rs = -(-ns // (128 * 512)) * 512
    padr = lambda a: jnp.concatenate([a, jnp.ones((rs * 128 - ns,), F32)]).reshape(rs, 128)
    res = _adamw(padr(gs)[None], padr(ws), padr(ms), padr(vs), "adamw_small")
    off = 0
    for n in names:
        size = math.prod(wt[n].shape)
        out_g[n], out_d[n], out_m[n], out_v[n] = [r.reshape(-1)[off:off + size].reshape(wt[n].shape) for r in res]
        off += size

    return (loss, grad_x[None], *[out_g[n] for n in WEIGHTS], *[out_d[n] for n in WEIGHTS],
            *[out_m[n] for n in WEIGHTS], *[out_v[n] for n in WEIGHTS])
```

```python
import math

import jax
import jax.numpy as jnp
from jax import lax
from jax.experimental import pallas as pl
from jax.experimental.pallas import tpu as pltpu

F32 = jnp.float32
BF = jnp.bfloat16
MESH = pl.DeviceIdType.MESH
N_DEV = 8

GRID_W = 64
CHUNK = 16
S5_P = 16
S5_N = 64
LANE_BLOCK = 128
GROUPS_PER_BLOCK = LANE_BLOCK // S5_P
BCR_W = CHUNK * LANE_BLOCK
ZL_W = 2 * 2 * GROUPS_PER_BLOCK * S5_N
ZH = ZL_W // 4
LN_EPS = 1e-5
DN_ALPHA = 4.0 ** 0.25
ADAM_LR, ADAM_B1, ADAM_B2, ADAM_EPS, ADAM_WD, ADAM_STEP = 1e-3, 0.9, 0.999, 1e-8, 0.01, 10
GELU_C0 = math.sqrt(2.0 / math.pi)
GELU_C1 = 0.044715
VMEM_MB = 52

ANY = pl.BlockSpec(memory_space=pl.ANY)


def _cparams():
    return pltpu.CompilerParams(vmem_limit_bytes=VMEM_MB << 20)


def _dot(a, b):
    return jnp.dot(a, b, preferred_element_type=F32)


def _dot_nt(a, b):
    return lax.dot_general(a, b, (((1,), (1,)), ((), ())), preferred_element_type=F32)


def _dot_tn(a, b):
    return lax.dot_general(a, b, (((0,), (0,)), ((), ())), preferred_element_type=F32)


def _sigmoid(x):
    return 1.0 / (1.0 + jnp.exp(-x))


def _gelu_parts(y):
    th = jnp.tanh(GELU_C0 * (y + GELU_C1 * y * y * y))
    g = 0.5 * y * (1.0 + th)
    dg = 0.5 * (1.0 + th) + 0.5 * y * (1.0 - th * th) * GELU_C0 * (1.0 + 3.0 * GELU_C1 * y * y)
    return g, dg


def _full(shape):
    nd = len(shape)
    return pl.BlockSpec(shape, lambda *_: (0,) * nd)


def _mesh_pos():
    x, y, c = lax.axis_index("x"), lax.axis_index("y"), lax.axis_index("c")
    return x, y, c


def _peer(pos, k):
    x, y, c = pos
    px = 1 - x if (k >> 2) & 1 else x
    py = 1 - y if (k >> 1) & 1 else y
    pc = 1 - c if k & 1 else c
    return (px, py, pc), 4 * px + 2 * py + pc


def _shard_at(ref, axis, idx, n):
    if axis == 0:
        return ref.at[pl.ds(idx * n, n)]
    return ref.at[:, pl.ds(idx * n, n)]


class _Exchange:
    def __init__(self, kind, arrays, axes):
        self.kind, self.axes, self.n = kind, list(axes), len(arrays)
        self.arrays = list(arrays)
        self.out_shape = []
        for s, ax in zip(arrays, axes):
            shp = list(s.shape)
            if kind == "gather":
                shp[ax] *= N_DEV
                self.out_shape.append(jax.ShapeDtypeStruct(tuple(shp), s.dtype))
            else:
                shp[ax] //= N_DEV
                self.out_shape.append(jax.ShapeDtypeStruct((N_DEV, *shp), s.dtype))
        self.scratch = [pltpu.SemaphoreType.DMA((self.n, N_DEV - 1)), pltpu.SemaphoreType.DMA((self.n, N_DEV - 1)),
                        pltpu.SemaphoreType.DMA((self.n,))]

    def _copies(self, ins, outs, sems):
        send_sems, recv_sems, local_sems = sems
        pos = _mesh_pos()
        me = 4 * pos[0] + 2 * pos[1] + pos[2]
        local, sends, recvs = [], [], []
        for i in range(self.n):
            ax = self.axes[i]
            if self.kind == "gather":
                size = ins[i].shape[ax]
                src = lambda idx, i=i: ins[i]
                dst = lambda idx, i=i, ax=ax, size=size: _shard_at(outs[i], ax, idx, size)
                mine, theirs = (lambda pidx: me), (lambda pidx: pidx)
            else:
                size = ins[i].shape[ax] // N_DEV
                src = lambda idx, i=i, ax=ax, size=size: _shard_at(ins[i], ax, idx, size)
                dst = lambda idx, i=i: outs[i].at[idx]
                mine, theirs = (lambda pidx: me), (lambda pidx: pidx)
            src_own = src(me)
            local.append(pltpu.make_async_copy(src_own, dst(me), local_sems.at[i]))
            for k in range(1, N_DEV):
                peer, pidx = _peer(pos, k)
                out_src = src(me) if self.kind == "gather" else src(pidx)
                sends.append(pltpu.make_async_remote_copy(
                    src_ref=out_src, dst_ref=dst(mine(pidx)), send_sem=send_sems.at[i, k - 1],
                    recv_sem=recv_sems.at[i, k - 1], device_id=peer, device_id_type=MESH))
                recvs.append(pltpu.make_async_remote_copy(
                    src_ref=out_src, dst_ref=dst(theirs(pidx)), send_sem=send_sems.at[i, k - 1],
                    recv_sem=recv_sems.at[i, k - 1], device_id=peer, device_id_type=MESH))
        return local, sends, recvs

    def start(self, ins, outs, sems):
        local, sends, _ = self._copies(ins, outs, sems)
        for cp in local + sends:
            cp.start()

    def wait(self, ins, outs, sems):
        local, sends, recvs = self._copies(ins, outs, sems)
        for cp in recvs:
            cp.wait_recv()
        for cp in sends:
            cp.wait_send()
        for cp in local:
            cp.wait()

    def run(self, name):
        n = self.n

        def body(*refs):
            ins, outs, sems = refs[:n], refs[n:2 * n], refs[2 * n:]
            self.start(ins, outs, sems)
            self.wait(ins, outs, sems)

        return pl.pallas_call(body, name=name, out_shape=self.out_shape, in_specs=[ANY] * n, out_specs=[ANY] * n,
                              scratch_shapes=self.scratch)(*self.arrays)


def _hosted_call(body, xch, grid, in_specs, out_specs, out_shape, scratch, args, name):
    out_specs, out_shape = list(out_specs), list(out_shape)
    n_in, n_out = len(in_specs), len(out_specs)
    if xch is None:
        res = pl.pallas_call(body, name=name, grid=grid, in_specs=in_specs, out_specs=out_specs, out_shape=out_shape,
                             scratch_shapes=list(scratch), compiler_params=_cparams())(*args)
        return list(res), []
    n = xch.n
    rank = len(grid)

    def wrapped(*refs):
        ins, x_ins = refs[:n_in], refs[n_in:n_in + n]
        outs = refs[n_in + n:n_in + n + n_out]
        x_outs = refs[n_in + n + n_out:n_in + 2 * n + n_out]
        rest = refs[n_in + 2 * n + n_out:]
        own, sems = rest[:len(rest) - 3], rest[len(rest) - 3:]
        ids = [pl.program_id(a) for a in range(rank)]
        first, last = ids[0] == 0, ids[0] == grid[0] - 1
        for a in range(1, rank):
            first = jnp.logical_and(first, ids[a] == 0)
            last = jnp.logical_and(last, ids[a] == grid[a] - 1)

        @pl.when(first)
        def _():
            xch.start(x_ins, x_outs, sems)

        body(*ins, *outs, *own)

        @pl.when(last)
        def _():
            xch.wait(x_ins, x_outs, sems)

    res = pl.pallas_call(
        wrapped, name=name, grid=grid, in_specs=list(in_specs) + [ANY] * n, out_specs=out_specs + [ANY] * n,
        out_shape=out_shape + xch.out_shape, scratch_shapes=list(scratch) + xch.scratch, compiler_params=_cparams(),
    )(*args, *xch.arrays)
    return list(res[:n_out]), list(res[n_out:])


def _all_gather(shards, axes, name):
    return _Exchange("gather", shards, axes).run(name)


def _all_to_all(parts, axes, name):
    return _Exchange("scatter", parts, axes).run(name)


def _ada_fwd(c8, ada_w, ada_b):
    nl, d, d3 = ada_w.shape

    def body(c_ref, w_ref, b_ref, o_ref):
        cv = c_ref[...]
        s = (cv * _sigmoid(cv)).astype(BF)
        o_ref[0] = _dot(s, w_ref[0]) + b_ref[0]

    return pl.pallas_call(
        body, name="ada_fwd", grid=(nl,),
        in_specs=[_full((8, d)), pl.BlockSpec((1, d, d3), lambda l: (l, 0, 0)), pl.BlockSpec((1, 1, d3), lambda l: (l, 0, 0))],
        out_specs=pl.BlockSpec((1, 8, d3), lambda l: (l, 0, 0)),
        out_shape=jax.ShapeDtypeStruct((nl, 8, d3), F32), compiler_params=_cparams(),
    )(c8, ada_w, ada_b.reshape(nl, 1, d3))


def _ada_bwd(c8, ada_w, dm8):
    nl, d, d3 = ada_w.shape

    def body(c_ref, w_ref, dm_ref, dw_ref, dc_ref):
        cv = c_ref[...]
        sg = _sigmoid(cv)
        s = (cv * sg).astype(BF)
        dm = dm_ref[0].astype(BF)
        dw_ref[0] = _dot_tn(s, dm).astype(BF)
        dc_ref[0] = _dot_nt(dm, w_ref[0]) * (sg * (1.0 + cv * (1.0 - sg)))

    return pl.pallas_call(
        body, name="ada_bwd", grid=(nl,),
        in_specs=[_full((8, d)), pl.BlockSpec((1, d, d3), lambda l: (l, 0, 0)), pl.BlockSpec((1, 8, d3), lambda l: (l, 0, 0))],
        out_specs=[pl.BlockSpec((1, d, d3), lambda l: (l, 0, 0)), pl.BlockSpec((1, 8, d), lambda l: (l, 0, 0))],
        out_shape=[jax.ShapeDtypeStruct((nl, d, d3), BF), jax.ShapeDtypeStruct((nl, 8, d), F32)],
        compiler_params=_cparams(),
    )(c8, ada_w, dm8)


def _sum_partials(stack):
    _, r, c = stack.shape

    def body(s_ref, o_ref):
        acc = s_ref[0]
        for p in range(1, N_DEV):
            acc = acc + s_ref[p]
        o_ref[...] = acc

    return pl.pallas_call(body, name="sum_partials", out_shape=jax.ShapeDtypeStruct((r, c), F32),
                          in_specs=[_full(stack.shape)], out_specs=_full((r, c)), grid=(1,),
                          compiler_params=_cparams())(stack)


def _adamw(gstack, w, m, v, name):
    p, r, c = gstack.shape
    tr = r
    for cand in (512 if c <= 256 else 256, 128, 64, 32, 16, 8):
        if r % cand == 0 and r > cand:
            tr = cand
            break
    bc1 = 1.0 - ADAM_B1 ** ADAM_STEP
    bc2 = 1.0 - ADAM_B2 ** ADAM_STEP

    def body(g_ref, w_ref, m_ref, v_ref, go_ref, d_ref, mo_ref, vo_ref):
        g = g_ref[0].astype(F32)
        for q in range(1, p):
            g = g + g_ref[q].astype(F32)
        mn = ADAM_B1 * m_ref[...] + (1.0 - ADAM_B1) * g
        vn = ADAM_B2 * v_ref[...] + (1.0 - ADAM_B2) * (g * g)
        go_ref[...] = g
        mo_ref[...] = mn
        vo_ref[...] = vn
        d_ref[...] = -ADAM_LR * ((mn / bc1) / (jnp.sqrt(vn / bc2) + ADAM_EPS) + ADAM_WD * w_ref[...])

    row = pl.BlockSpec((tr, c), lambda i: (i, 0))
    sds = jax.ShapeDtypeStruct((r, c), F32)
    return pl.pallas_call(
        body, name=name, grid=(r // tr,),
        in_specs=[pl.BlockSpec((p, tr, c), lambda i: (0, i, 0)), row, row, row],
        out_specs=[row, row, row, row], out_shape=[sds, sds, sds, sds], compiler_params=_cparams(),
    )(gstack, w, m, v)


def _lat_or_ctx_specs(tm, d, nl, grid_rank, row_axis):
    def lat(*ids):
        return (jnp.minimum(ids[row_axis], nl - 1), 0)

    def ctx(*ids):
        return (jnp.maximum(ids[row_axis] - nl, 0), 0)

    return pl.BlockSpec((tm, d), lat), pl.BlockSpec((tm, d), ctx)


def _sel_row(ref, is_ctx):
    return jnp.where(is_ctx, ref[1:2, :], ref[0:1, :])


def _inproj0(x, ctx, a2, b2, w, tm, xch=None):
    l, d = x.shape
    nl, nc = l // tm, ctx.shape[0] // tm
    e = w.shape[1] // 4
    half = e // 2

    def body(x_ref, c_ref, a_ref, b_ref, w_hbm, o_ref, w_ref):
        i = pl.program_id(0)

        @pl.when(i == 0)
        def _():
            pltpu.sync_copy(w_hbm, w_ref)

        is_ctx = i >= nl
        xv = jnp.where(is_ctx, c_ref[...], x_ref[...])
        h = (xv * _sel_row(a_ref, is_ctx) + _sel_row(b_ref, is_ctx)).astype(BF)
        for k in range(4):
            r = _dot(h, w_ref[:, k * e:(k + 1) * e])
            o_ref[k, 0] = r[:, :half].astype(BF)
            o_ref[k, 1] = r[:, half:].astype(BF)

    lat, cx = _lat_or_ctx_specs(tm, d, nl, 1, 0)
    (p42,), extra = _hosted_call(
        body, xch, grid=(nl + nc,),
        in_specs=[lat, cx, _full((2, d)), _full((2, d)), ANY],
        out_specs=[pl.BlockSpec((4, 2, tm, half), lambda i: (0, 0, i, 0))],
        out_shape=[jax.ShapeDtypeStruct((4, 2, l + ctx.shape[0], half), BF)],
        scratch=[pltpu.VMEM(w.shape, BF)], args=(x, ctx, a2, b2, w), name="l0_inproj")
    return p42, extra


def _conv_taps(u, w_up, w_mid, w_dn, pos, rl, tm):
    up = jnp.where(pos == 0, 0.0, pltpu.roll(u, 1, 0))
    dn = jnp.where(pos == rl - 1, 0.0, pltpu.roll(u, tm - 1, 0))
    return w_up * up + w_mid * u + w_dn * dn, up, dn


def _conv_halo_specs(tm, tc, nl, lead):
    hb = tm // GRID_W

    def prev(j, i):
        return (0, 1, jnp.maximum(jnp.minimum(i, nl - 1) * hb - 1, 0), j)

    def nxt(j, i):
        return (0, 1, jnp.minimum((jnp.minimum(i, nl - 1) + 1) * hb, nl * hb - 1), j)

    return pl.BlockSpec((lead, 1, GRID_W, tc), prev), pl.BlockSpec((lead, 1, GRID_W, tc), nxt)


def _conv_fwd(p42, cw, nl, tm, tc):
    _, _, r, half = p42.shape
    nt = r // tm

    def body(p_ref, hp_ref, hn_ref, cw_ref, o_ref):
        i = pl.program_id(1)
        is_ctx = i >= nl
        row = lax.broadcasted_iota(jnp.int32, (tm, tc), 0)
        rl = jnp.where(is_ctx, tm, GRID_W)
        pos = jnp.bitwise_and(row, rl - 1)

        def gate(hv, yc):
            bg = p_ref[0, hv].astype(F32)
            z = p_ref[3, hv].astype(F32)
            return (bg * yc * (z * _sigmoid(z))).astype(BF)

        u_h = p_ref[1, 0].astype(F32) * p_ref[2, 0].astype(F32)
        w_h = cw_ref[:, 0, :]
        o_ref[0] = gate(0, _conv_taps(u_h, w_h[0:1], w_h[1:2], w_h[2:3], pos, rl, tm)[0])
        u_v = p_ref[1, 1].astype(F32) * p_ref[2, 1].astype(F32)
        w_v = cw_ref[:, 1, :]

        @pl.when(is_ctx)
        def _():
            o_ref[1] = gate(1, _conv_taps(u_v, w_v[0:1], w_v[1:2], w_v[2:3], pos, rl, tm)[0])

        @pl.when(jnp.logical_not(is_ctx))
        def _():
            up = hp_ref[1, 0].astype(F32) * hp_ref[2, 0].astype(F32) * (i > 0).astype(F32)
            dn = hn_ref[1, 0].astype(F32) * hn_ref[2, 0].astype(F32) * (i < nl - 1).astype(F32)
            ext = jnp.concatenate([up, u_v, dn], axis=0)
            yc = w_v[0:1] * ext[0:tm] + w_v[1:2] * u_v + w_v[2:3] * ext[2 * GRID_W:tm + 2 * GRID_W]
            o_ref[1] = gate(1, yc)

    hp, hn = _conv_halo_specs(tm, tc, nl, 4)
    return pl.pallas_call(
        body, name="l0_conv_fwd", grid=(half // tc, nt),
        in_specs=[pl.BlockSpec((4, 2, tm, tc), lambda j, i: (0, 0, i, j)), hp, hn,
                  pl.BlockSpec((3, 2, tc), lambda j, i: (0, 0, j))],
        out_specs=pl.BlockSpec((2, tm, tc), lambda j, i: (0, i, j)),
        out_shape=jax.ShapeDtypeStruct((2, r, half), BF), compiler_params=_cparams(),
    )(p42, p42, p42, cw)


def _outproj_ln0(q3, w_out, x, ctx, gt2, tm):
    l, d = x.shape
    lc = ctx.shape[0]
    nl, nc = l // tm, lc // tm
    _, r, half = q3.shape
    tjo = tm // CHUNK

    def body(q_ref, w_hbm, x_ref, c_ref, g_ref, xl_ref, xc_ref, rl_ref, rc_ref, fx_ref, w_ref, xs_ref, rs_ref):
        i = pl.program_id(0)

        @pl.when(i == 0)
        def _():
            pltpu.sync_copy(w_hbm, w_ref)

        is_ctx = i >= nl
        fx = _dot(q_ref[0], w_ref[:half, :]) + _dot(q_ref[1], w_ref[half:, :])
        xv = jnp.where(is_ctx, c_ref[...], x_ref[...])
        rr = DN_ALPHA * xv + _sel_row(g_ref, is_ctx) * fx
        mu = jnp.mean(rr, axis=-1, keepdims=True)
        cen = rr - mu
        rstd = lax.rsqrt(jnp.mean(cen * cen, axis=-1, keepdims=True) + LN_EPS)
        xh = cen * rstd
        for lb in range(d // 128):
            xs_ref[lb] = xh[:, lb * 128:(lb + 1) * 128]
        rs_ref[...] = jnp.broadcast_to(rstd, (tm, 128))
        fx_ref[...] = fx.astype(BF)

        def to_cr(xo_ref, ro_ref):
            for s in range(CHUNK):
                for lb in range(d // 128):
                    xo_ref[:, s * d + lb * 128:s * d + (lb + 1) * 128] = xs_ref.at[lb][pl.ds(s, tjo, stride=CHUNK), :]
                ro_ref[:, s * 128:(s + 1) * 128] = rs_ref[pl.ds(s, tjo, stride=CHUNK), :]

        @pl.when(jnp.logical_not(is_ctx))
        def _():
            to_cr(xl_ref, rl_ref)

        @pl.when(is_ctx)
        def _():
            to_cr(xc_ref, rc_ref)

    lat, cx = _lat_or_ctx_specs(tm, d, nl, 1, 0)
    lat_o = lambda w_: pl.BlockSpec((tjo, CHUNK * w_), lambda i: (jnp.minimum(i, nl - 1), 0))
    ctx_o = lambda w_: pl.BlockSpec((tjo, CHUNK * w_), lambda i: (jnp.maximum(i - nl, 0), 0))
    return pl.pallas_call(
        body, name="l0_outproj_ln", grid=(nl + nc,),
        in_specs=[pl.BlockSpec((2, tm, half), lambda i: (0, i, 0)), ANY, lat, cx, _full((2, d))],
        out_specs=[lat_o(d), ctx_o(d), lat_o(128), ctx_o(128), pl.BlockSpec((tm, d), lambda i: (i, 0))],
        out_shape=[jax.ShapeDtypeStruct((l // CHUNK, CHUNK * d), F32), jax.ShapeDtypeStruct((lc // CHUNK, CHUNK * d), F32),
                   jax.ShapeDtypeStruct((l // CHUNK, CHUNK * 128), F32), jax.ShapeDtypeStruct((lc // CHUNK, CHUNK * 128), F32),
                   jax.ShapeDtypeStruct((r, d), BF)],
        scratch_shapes=[pltpu.VMEM(w_out.shape, BF), pltpu.VMEM((d // 128, tm, 128), F32), pltpu.VMEM((tm, 128), F32)],
        compiler_params=_cparams(),
    )(q3, w_out, x, ctx, gt2)


def _bwd_outproj0(dr_l, dr_c, gt2, w_out, fx, tm):
    l, d = dr_l.shape
    nl, nc = l // tm, dr_c.shape[0] // tm
    e = w_out.shape[0]
    half = e // 2
    r = l + dr_c.shape[0]

    def body(dl_ref, dc_ref, g_ref, w_hbm, fx_ref, dq_ref, acc_ref, w_ref):
        i = pl.program_id(0)

        @pl.when(i == 0)
        def _():
            pltpu.sync_copy(w_hbm, w_ref)
            acc_ref[...] = jnp.zeros_like(acc_ref)

        is_ctx = i >= nl
        dr = jnp.where(is_ctx, dc_ref[...], dl_ref[...]).astype(F32)
        dfx = (dr * _sel_row(g_ref, is_ctx)).astype(BF)
        dq_ref[0] = _dot_nt(dfx, w_ref[:half, :]).astype(BF)
        dq_ref[1] = _dot_nt(dfx, w_ref[half:, :]).astype(BF)
        s = jnp.sum(dr * fx_ref[...].astype(F32), axis=0, keepdims=True)
        sel = is_ctx.astype(F32)
        acc_ref[0:1, :] += s * (1.0 - sel)
        acc_ref[1:2, :] += s * sel

    lat, cx = _lat_or_ctx_specs(tm, d, nl, 1, 0)
    return pl.pallas_call(
        body, name="l0_bwd_outproj", grid=(nl + nc,),
        in_specs=[lat, cx, _full((2, d)), ANY, pl.BlockSpec((tm, d), lambda i: (i, 0))],
        out_specs=[pl.BlockSpec((2, tm, half), lambda i: (0, i, 0)), _full((8, d))],
        out_shape=[jax.ShapeDtypeStruct((2, r, half), BF), jax.ShapeDtypeStruct((8, d), F32)],
        scratch_shapes=[pltpu.VMEM(w_out.shape, BF)], compiler_params=_cparams(),
    )(dr_l, dr_c, gt2, w_out, fx)


def _conv_bwd(dq3, p42, cw, nl, tm, tc, xch=None):
    _, _, r, half = p42.shape
    nt = r // tm

    def body(dq_ref, dqp_ref, dqn_ref, p_ref, hp_ref, hn_ref, cw_ref, dp_ref, dw_ref):
        i = pl.program_id(1)
        is_ctx = i >= nl

        @pl.when(i == 0)
        def _():
            dw_ref[...] = jnp.zeros_like(dw_ref)

        row = lax.broadcasted_iota(jnp.int32, (tm, tc), 0)
        rl = jnp.where(is_ctx, tm, GRID_W)
        pos = jnp.bitwise_and(row, rl - 1)

        def pieces(dq, bg, z):
            sz = _sigmoid(z)
            sil = z * sz
            return dq * bg * sil, dq * sil, dq * bg * (sz * (1.0 + z * (1.0 - sz)))

        def seq_half(hv):
            bg, cg = p_ref[0, hv].astype(F32), p_ref[1, hv].astype(F32)
            v, z = p_ref[2, hv].astype(F32), p_ref[3, hv].astype(F32)
            w = cw_ref[:, hv, :]
            u = cg * v
            yc, u_up, u_dn = _conv_taps(u, w[0:1], w[1:2], w[2:3], pos, rl, tm)
            dyc, dbg_f, dz_f = pieces(dq_ref[hv].astype(F32), bg, z)
            du = _conv_taps(dyc, w[2:3], w[1:2], w[0:1], pos, rl, tm)[0]
            dp_ref[0, hv] = (dbg_f * yc).astype(BF)
            dp_ref[1, hv] = (du * v).astype(BF)
            dp_ref[2, hv] = (du * cg).astype(BF)
            dp_ref[3, hv] = (dz_f * yc).astype(BF)
            dw_ref[0:1, hv, :] += jnp.sum(dyc * u_up, axis=0, keepdims=True)
            dw_ref[1:2, hv, :] += jnp.sum(dyc * u, axis=0, keepdims=True)
            dw_ref[2:3, hv, :] += jnp.sum(dyc * u_dn, axis=0, keepdims=True)

        seq_half(0)

        @pl.when(is_ctx)
        def _():
            seq_half(1)

        @pl.when(jnp.logical_not(is_ctx))
        def _():
            bg, cg = p_ref[0, 1].astype(F32), p_ref[1, 1].astype(F32)
            v, z = p_ref[2, 1].astype(F32), p_ref[3, 1].astype(F32)
            w = cw_ref[:, 1, :]
            u = cg * v
            m_up = (i > 0).astype(F32)
            m_dn = (i < nl - 1).astype(F32)

            def halo(h_ref, dqh_ref, msk):
                hb, hc = h_ref[0, 0].astype(F32), h_ref[1, 0].astype(F32)
                hv_, hz = h_ref[2, 0].astype(F32), h_ref[3, 0].astype(F32)
                return hc * hv_ * msk, pieces(dqh_ref[0].astype(F32), hb, hz)[0] * msk

            u_p, dyc_p = halo(hp_ref, dqp_ref, m_up)
            u_n, dyc_n = halo(hn_ref, dqn_ref, m_dn)
            u_ext = jnp.concatenate([u_p, u, u_n], axis=0)
            u_up, u_dn = u_ext[0:tm], u_ext[2 * GRID_W:tm + 2 * GRID_W]
            yc = w[0:1] * u_up + w[1:2] * u + w[2:3] * u_dn
            dyc, dbg_f, dz_f = pieces(dq_ref[1].astype(F32), bg, z)
            d_ext = jnp.concatenate([dyc_p, dyc, dyc_n], axis=0)
            du = w[0:1] * d_ext[2 * GRID_W:tm + 2 * GRID_W] + w[1:2] * dyc + w[2:3] * d_ext[0:tm]
            dp_ref[0, 1] = (dbg_f * yc).astype(BF)
            dp_ref[1, 1] = (du * v).astype(BF)
            dp_ref[2, 1] = (du * cg).astype(BF)
            dp_ref[3, 1] = (dz_f * yc).astype(BF)
            dw_ref[0:1, 1, :] += jnp.sum(dyc * u_up, axis=0, keepdims=True)
            dw_ref[1:2, 1, :] += jnp.sum(dyc * u, axis=0, keepdims=True)
            dw_ref[2:3, 1, :] += jnp.sum(dyc * u_dn, axis=0, keepdims=True)

    hb = tm // GRID_W

    def dq_prev(j, i):
        return (1, jnp.maximum(jnp.minimum(i, nl - 1) * hb - 1, 0), j)

    def dq_next(j, i):
        return (1, jnp.minimum((jnp.minimum(i, nl - 1) + 1) * hb, nl * hb - 1), j)

    hp, hn = _conv_halo_specs(tm, tc, nl, 4)
    (dp42, dcw), extra = _hosted_call(
        body, xch, grid=(half // tc, nt),
        in_specs=[pl.BlockSpec((2, tm, tc), lambda j, i: (0, i, j)),
                  pl.BlockSpec((1, GRID_W, tc), dq_prev), pl.BlockSpec((1, GRID_W, tc), dq_next),
                  pl.BlockSpec((4, 2, tm, tc), lambda j, i: (0, 0, i, j)), hp, hn,
                  pl.BlockSpec((3, 2, tc), lambda j, i: (0, 0, j))],
        out_specs=[pl.BlockSpec((4, 2, tm, tc), lambda j, i: (0, 0, i, j)), pl.BlockSpec((8, 2, tc), lambda j, i: (0, 0, j))],
        out_shape=[jax.ShapeDtypeStruct(p42.shape, BF), jax.ShapeDtypeStruct((8, 2, half), F32)],
        scratch=[], args=(dq3, dq3, dq3, p42, p42, p42, cw), name="l0_conv_bwd")
    return dp42, dcw, extra


def _bwd_inproj0(dp42, w_in, x, ctx, dr_l, dr_c, a2, tm, xch=None):
    l, d = x.shape
    nl, nc = l // tm, ctx.shape[0] // tm
    e = w_in.shape[1] // 4
    half = e // 2

    def body(dp_ref, w_hbm, x_ref, c_ref, dl_ref, dc_ref, a_ref, gx_ref, acc_ref, w_ref):
        i = pl.program_id(0)

        @pl.when(i == 0)
        def _():
            pltpu.sync_copy(w_hbm, w_ref)
            acc_ref[...] = jnp.zeros_like(acc_ref)

        is_ctx = i >= nl
        dh = jnp.zeros((tm, d), F32)
        for k in range(4):
            for hv in range(2):
                c0 = k * e + hv * half
                dh = dh + _dot_nt(dp_ref[k, hv], w_ref[:, c0:c0 + half])
        xv = jnp.where(is_ctx, c_ref[...], x_ref[...])
        s_sc = jnp.sum(dh * xv, axis=0, keepdims=True)
        s_sh = jnp.sum(dh, axis=0, keepdims=True)
        sel = is_ctx.astype(F32)
        acc_ref[0:1, :] += s_sc * (1.0 - sel)
        acc_ref[1:2, :] += s_sc * sel
        acc_ref[2:3, :] += s_sh * (1.0 - sel)
        acc_ref[3:4, :] += s_sh * sel

        @pl.when(jnp.logical_not(is_ctx))
        def _():
            gx_ref[...] = DN_ALPHA * dl_ref[...].astype(F32) + dh * a_ref[0:1, :]

    lat, cx = _lat_or_ctx_specs(tm, d, nl, 1, 0)
    (gx, acc), extra = _hosted_call(
        body, xch, grid=(nl + nc,),
        in_specs=[pl.BlockSpec((4, 2, tm, half), lambda i: (0, 0, i, 0)), ANY, lat, cx, lat, cx, _full((2, d))],
        out_specs=[pl.BlockSpec((tm, d), lambda i: (jnp.minimum(i, nl - 1), 0)), _full((8, d))],
        out_shape=[jax.ShapeDtypeStruct((l, d), F32), jax.ShapeDtypeStruct((8, d), F32)],
        scratch=[pltpu.VMEM(w_in.shape, BF)], args=(dp42, w_in, x, ctx, dr_l, dr_c, a2), name="l0_bwd_inproj")
    return gx, acc, extra


def _dw_inproj0(x, ctx, a2, b2, dp42, tm):
    l, d = x.shape
    lc = ctx.shape[0]
    assert lc == tm
    tl = 4 * tm if l % (4 * tm) == 0 else tm
    nl = l // tl
    half = dp42.shape[-1]
    e = 2 * half

    def body(x_ref, c_ref, a_ref, b_ref, dpl_ref, dpc_ref, o_ref, acc_ref):
        i = pl.program_id(1)

        @pl.when(i == 0)
        def _():
            acc_ref[...] = jnp.zeros_like(acc_ref)

        def add(rows_ref, dp_ref, sel):
            h = (rows_ref[...] * a_ref[sel:sel + 1, :] + b_ref[sel:sel + 1, :]).astype(BF)
            acc_ref[:, :half] += _dot_tn(h, dp_ref[0, 0])
            acc_ref[:, half:] += _dot_tn(h, dp_ref[0, 1])

        @pl.when(i < nl)
        def _():
            add(x_ref, dpl_ref, 0)

        @pl.when(i == nl)
        def _():
            add(c_ref, dpc_ref, 1)
            o_ref[...] = acc_ref[...].astype(BF)

    return pl.pallas_call(
        body, name="l0_dw_inproj", grid=(4, nl + 1),
        in_specs=[pl.BlockSpec((tl, d), lambda k, i: (jnp.minimum(i, nl - 1), 0)), _full((lc, d)),
                  _full((2, d)), _full((2, d)),
                  pl.BlockSpec((1, 2, tl, half), lambda k, i: (k, 0, jnp.minimum(i, nl - 1), 0)),
                  pl.BlockSpec((1, 2, lc, half), lambda k, i: (k, 0, l // lc, 0))],
        out_specs=pl.BlockSpec((d, e), lambda k, i: (0, k)),
        out_shape=jax.ShapeDtypeStruct((d, 4 * e), BF),
        scratch_shapes=[pltpu.VMEM((d, e), F32)], compiler_params=_cparams(),
    )(x, ctx, a2, b2, dp42, dp42)


def _dw_outproj0(q3, dr_l, dr_c, gt2, tm):
    l, d = dr_l.shape
    nl, nc = l // tm, dr_c.shape[0] // tm
    _, r, half = q3.shape
    nt = nl + nc

    def body(q_ref, dl_ref, dc_ref, g_ref, o_ref, acc_ref):
        i = pl.program_id(0)
        is_ctx = i >= nl

        @pl.when(i == 0)
        def _():
            acc_ref[...] = jnp.zeros_like(acc_ref)

        dr = jnp.where(is_ctx, dc_ref[...], dl_ref[...]).astype(F32)
        dfx = (dr * _sel_row(g_ref, is_ctx)).astype(BF)
        acc_ref[:half, :] += _dot_tn(q_ref[0], dfx)
        acc_ref[half:, :] += _dot_tn(q_ref[1], dfx)

        @pl.when(i == nt - 1)
        def _():
            o_ref[...] = acc_ref[...].astype(BF)

    lat, cx = _lat_or_ctx_specs(tm, d, nl, 1, 0)
    return pl.pallas_call(
        body, name="l0_dw_outproj", grid=(nt,),
        in_specs=[pl.BlockSpec((2, tm, half), lambda i: (0, i, 0)), lat, cx, _full((2, d))],
        out_specs=_full((2 * half, d)), out_shape=jax.ShapeDtypeStruct((2 * half, d), BF),
        scratch_shapes=[pltpu.VMEM((2 * half, d), F32)], compiler_params=_cparams(),
    )(q3, dr_l, dr_c, gt2)


def _cr_tile(j, cap=256):
    for cand in (1024, 512, 256, 128, 64, 32, 16, 8):
        if cand <= cap and j % cand == 0:
            return cand
    raise ValueError(j)


def _inproj1(xh_cr, a1, b1, w, tag):
    j, d16 = xh_cr.shape
    d = d16 // CHUNK
    e = w.shape[1] // 2
    nb = e // LANE_BLOCK
    tj = _cr_tile(j)

    def body(x_ref, a_ref, b_ref, w_hbm, u_ref, z_ref, w_ref):
        @pl.when(jnp.logical_and(pl.program_id(0) == 0, pl.program_id(1) == 0))
        def _():
            pltpu.sync_copy(w_hbm, w_ref)

        h = (x_ref[...] * a_ref[...] + b_ref[...]).astype(BF)
        r = _dot(h, w_ref[...])
        for b in range(nb):
            u_ref[b] = r[:, b * LANE_BLOCK:(b + 1) * LANE_BLOCK].astype(BF)
        z_ref[...] = r[:, e:].astype(BF)

    return pl.pallas_call(
        body, name="l1_inproj_" + tag, grid=(j // tj, CHUNK),
        in_specs=[pl.BlockSpec((tj, d), lambda t, s: (t, s)), _full((1, d)), _full((1, d)), ANY],
        out_specs=[pl.BlockSpec((nb, tj, LANE_BLOCK), lambda t, s: (0, t, s)), pl.BlockSpec((tj, e), lambda t, s: (t, s))],
        out_shape=[jax.ShapeDtypeStruct((nb, j, BCR_W), BF), jax.ShapeDtypeStruct((j, CHUNK * e), BF)],
        scratch_shapes=[pltpu.VMEM(w.shape, BF)], compiler_params=_cparams(),
    )(xh_cr, a1, b1, w)


def _bmm(a_list, w_list, trans, out_dtype, name, ctx=None):
    nb, j, ka = a_list[0].shape
    n_out = w_list[0].shape[1] if trans[0] else w_list[0].shape[2]
    tn = n_out // 2
    tj = _cr_tile(j, 512)
    n = len(a_list)
    c_idx = [i for i in range(n) if ctx is not None and ctx[i] is not None]
    c_list = [ctx[i] for i in c_idx]
    nc = len(c_list)

    def body(*refs):
        w_refs = refs[n:2 * n]

        def product(a_refs, idx):
            acc = None
            for a_ref, i in zip(a_refs, idx):
                a = a_ref[0].astype(BF)
                t = _dot_nt(a, w_refs[i][0]) if trans[i] else _dot(a, w_refs[i][0])
                acc = t if acc is None else acc + t
            return acc.astype(out_dtype)

        refs[2 * n + nc][0] = product(refs[:n], range(n))
        if nc:
            @pl.when(pl.program_id(2) == 0)
            def _():
                refs[2 * n + nc + 1][0] = product(refs[2 * n:2 * n + nc], c_idx)

    a_specs = [pl.BlockSpec((1, tj, a.shape[2]), lambda b, h, t: (b, t, 0)) for a in a_list]
    w_specs = [pl.BlockSpec((1, tn, w.shape[2]), lambda b, h, t: (b, h, 0)) if tr
               else pl.BlockSpec((1, w.shape[1], tn), lambda b, h, t: (b, 0, h)) for w, tr in zip(w_list, trans)]
    c_specs = [pl.BlockSpec((1, a.shape[1], a.shape[2]), lambda b, h, t: (b, 0, 0)) for a in c_list]
    out_specs = [pl.BlockSpec((1, tj, tn), lambda b, h, t: (b, t, h))]
    out_shape = [jax.ShapeDtypeStruct((nb, j, n_out), out_dtype)]
    if nc:
        jc = c_list[0].shape[1]
        out_specs.append(pl.BlockSpec((1, jc, tn), lambda b, h, t: (b, 0, h)))
        out_shape.append(jax.ShapeDtypeStruct((nb, jc, n_out), out_dtype))
    res = pl.pallas_call(
        body, name=name, grid=(nb, 2, j // tj), in_specs=a_specs + w_specs + c_specs,
        out_specs=out_specs, out_shape=out_shape, compiler_params=_cparams(),
    )(*a_list, *w_list, *c_list)
    return res if nc else res[0]


def _group_mask(lane_groups):
    row = lax.broadcasted_iota(jnp.int32, (LANE_BLOCK, LANE_BLOCK), 0) // S5_P
    lane = lax.broadcasted_iota(jnp.int32, (LANE_BLOCK, LANE_BLOCK), 1)
    return row == lane_groups(lane)


def _expand_toeplitz(wcomp):
    nb = wcomp.shape[0]
    nd = 2 * CHUNK - 1

    def body(c_ref, o_ref):
        mask = _group_mask(lambda lane: lane // S5_P)
        tiles = []
        for dd in range(nd):
            m = c_ref[0, dd]
            tiles.append(jnp.where(mask, jnp.concatenate([m] * GROUPS_PER_BLOCK, axis=1), 0.0).astype(BF))
        for s in range(CHUNK):
            for t in range(CHUNK):
                o_ref[0, s * LANE_BLOCK:(s + 1) * LANE_BLOCK, t * LANE_BLOCK:(t + 1) * LANE_BLOCK] = tiles[t - s + CHUNK - 1]

    return pl.pallas_call(
        body, name="l1_expand_toeplitz", grid=(nb,),
        in_specs=[pl.BlockSpec((1, nd, LANE_BLOCK, S5_P), lambda b: (b, 0, 0, 0))],
        out_specs=pl.BlockSpec((1, BCR_W, BCR_W), lambda b: (b, 0, 0)),
        out_shape=jax.ShapeDtypeStruct((nb, BCR_W, BCR_W), BF), compiler_params=_cparams(),
    )(wcomp)


def _expand_blocks(comp, name):
    nb = comp.shape[2]
    lanes_per_dir = ZL_W // 2

    def body(c_ref, o_ref):
        masks = [_group_mask(lambda lane, lb=lb: 2 * lb + lane // S5_N) for lb in range(4)]
        for r in range(2):
            for s in range(CHUNK):
                for ri in range(2):
                    m = c_ref[r, s, 0, :, ri * S5_N:(ri + 1) * S5_N]
                    mm = jnp.concatenate([m, m], axis=1)
                    for lb in range(4):
                        c0 = r * lanes_per_dir + ri * ZH + lb * LANE_BLOCK
                        o_ref[0, s * LANE_BLOCK:(s + 1) * LANE_BLOCK, c0:c0 + LANE_BLOCK] = (
                            jnp.where(masks[lb], mm, 0.0).astype(BF))

    return pl.pallas_call(
        body, name=name, grid=(nb,),
        in_specs=[pl.BlockSpec((2, CHUNK, 1, LANE_BLOCK, LANE_BLOCK), lambda b: (0, 0, b, 0, 0))],
        out_specs=pl.BlockSpec((1, BCR_W, ZL_W), lambda b: (b, 0, 0)),
        out_shape=jax.ShapeDtypeStruct((nb, BCR_W, ZL_W), BF), compiler_params=_cparams(),
    )(comp)


def _bdw(a, b_, kind, ctx, name):
    nb, j, ka = a.shape
    kb = b_.shape[2]
    tn = kb // 2
    tj = _cr_tile(j, 1024)
    nt = j // tj
    has_ctx = ctx is not None
    nd = 2 * CHUNK - 1

    def body(*refs):
        a_ref, b_ref = refs[0], refs[1]
        o_ref, acc_ref = refs[2 + 2 * has_ctx], refs[3 + 2 * has_ctx]
        h, t = pl.program_id(1), pl.program_id(2)

        @pl.when(t == 0)
        def _():
            if has_ctx:
                acc_ref[...] = _dot_tn(refs[2][0].astype(BF), refs[3][0].astype(BF))
            else:
                acc_ref[...] = jnp.zeros_like(acc_ref)

        acc_ref[...] += _dot_tn(a_ref[0].astype(BF), b_ref[0].astype(BF))

        if kind == "toeplitz":
            diag_ref = refs[4 + 2 * has_ctx]

            @pl.when(jnp.logical_and(t == 0, h == 0))
            def _():
                diag_ref[...] = jnp.zeros_like(diag_ref)

            @pl.when(t == nt - 1)
            def _():
                for s in range(CHUNK):
                    for tl in range(CHUNK // 2):
                        dd = h * (CHUNK // 2) + (tl - s + CHUNK - 1)
                        diag_ref[dd] += acc_ref[s * LANE_BLOCK:(s + 1) * LANE_BLOCK, tl * LANE_BLOCK:(tl + 1) * LANE_BLOCK]

            @pl.when(jnp.logical_and(t == nt - 1, h == 1))
            def _():
                mask = _group_mask(lambda lane: lane // S5_P)
                for dd in range(nd):
                    v = jnp.where(mask, diag_ref[dd], 0.0)
                    acc = v[:, :S5_P]
                    for k in range(1, GROUPS_PER_BLOCK):
                        acc = acc + v[:, k * S5_P:(k + 1) * S5_P]
                    o_ref[0, dd] = acc
        else:
            @pl.when(t == nt - 1)
            def _():
                masks = [_group_mask(lambda lane, lb=lb: 2 * lb + lane // S5_N) for lb in range(4)]
                for s in range(CHUNK):
                    for ri in range(2):
                        v = None
                        for lb in range(4):
                            c0 = ri * ZH + lb * LANE_BLOCK
                            blk = acc_ref[s * LANE_BLOCK:(s + 1) * LANE_BLOCK, c0:c0 + LANE_BLOCK]
                            blk = jnp.where(masks[lb], blk, 0.0)
                            v = blk if v is None else v + blk
                        o_ref[0, s, 0, :, ri * S5_N:(ri + 1) * S5_N] = v[:, :S5_N] + v[:, S5_N:]

    in_specs = [pl.BlockSpec((1, tj, ka), lambda b, h, t: (b, t, 0)), pl.BlockSpec((1, tj, tn), lambda b, h, t: (b, t, h))]
    args = [a, b_]
    if has_ctx:
        jc = ctx[0].shape[1]
        in_specs += [pl.BlockSpec((1, jc, ka), lambda b, h, t: (b, 0, 0)), pl.BlockSpec((1, jc, tn), lambda b, h, t: (b, 0, h))]
        args += list(ctx)
    scratch = [pltpu.VMEM((ka, tn), F32)]
    if kind == "toeplitz":
        ospec = pl.BlockSpec((1, nd, LANE_BLOCK, S5_P), lambda b, h, t: (b, 0, 0, 0))
        oshape = jax.ShapeDtypeStruct((nb, nd, LANE_BLOCK, S5_P), F32)
        scratch.append(pltpu.VMEM((nd, LANE_BLOCK, LANE_BLOCK), F32))
    else:
        ospec = pl.BlockSpec((1, CHUNK, 1, LANE_BLOCK, LANE_BLOCK), lambda b, h, t: (h, 0, b, 0, 0))
        oshape = jax.ShapeDtypeStruct((2, CHUNK, nb, LANE_BLOCK, LANE_BLOCK), F32)
    return pl.pallas_call(
        body, name=name, grid=(nb, 2, nt), in_specs=in_specs, out_specs=ospec, out_shape=oshape,
        scratch_shapes=scratch, compiler_params=_cparams(),
    )(*args)


def _scan(z_l, z_c, coef, chains, conj, s_l=None, s_c=None, name="l1_scan"):
    nb, jl, _ = z_l.shape
    jc = z_c.shape[1]
    with_da = s_l is not None
    sign = -1.0 if conj else 1.0
    hw = 2 * ZH

    def body(*refs):
        zl_ref, zc_ref, cf_ref = refs[:3]
        k = 3
        if with_da:
            sl_ref, sc_ref = refs[3:5]
            k = 5
        ol_ref, oc_ref = refs[k:k + 2]
        d = pl.program_id(1)
        rowi = lax.broadcasted_iota(jnp.int32, (8, ZH), 0)

        def coef_rows(r0, nr):
            return cf_ref[0, 0, r0:r0 + nr, :ZH], sign * cf_ref[0, 0, r0:r0 + nr, ZH:]

        steps = [(1, coef_rows(0, 1)), (2, coef_rows(1, 1)), (4, coef_rows(2, 1))]

        def run(chain):
            carry = (jnp.zeros((1, ZH), F32), jnp.zeros((1, ZH), F32))
            da = (jnp.zeros((8, ZH), F32), jnp.zeros((8, ZH), F32))
            for which, rev in chain:
                src, dst = (zc_ref, oc_ref) if which == "c" else (zl_ref, ol_ref)
                sref = (sc_ref if which == "c" else sl_ref) if with_da else None
                ng = (jc if which == "c" else jl) // 8
                tr, ti = coef_rows(16, 8) if rev else coef_rows(8, 8)

                def step(it, st, src=src, dst=dst, sref=sref, ng=ng, tr=tr, ti=ti, rev=rev):
                    cr_, ci_, dar, dai = st
                    g = (ng - 1 - it) if rev else it
                    off = pl.multiple_of(g * 8, 8)
                    xr = src[0, pl.ds(off, 8), :ZH]
                    xi = src[0, pl.ds(off, 8), ZH:]
                    for sh, (ar, ai) in steps:
                        if rev:
                            keep = rowi < 8 - sh
                            sr = jnp.where(keep, pltpu.roll(xr, 8 - sh, 0), 0.0)
                            si = jnp.where(keep, pltpu.roll(xi, 8 - sh, 0), 0.0)
                        else:
                            keep = rowi >= sh
                            sr = jnp.where(keep, pltpu.roll(xr, sh, 0), 0.0)
                            si = jnp.where(keep, pltpu.roll(xi, sh, 0), 0.0)
                        xr, xi = xr + ar * sr - ai * si, xi + ar * si + ai * sr
                    ir = xr + tr * cr_ - ti * ci_
                    ii = xi + tr * ci_ + ti * cr_
                    if rev:
                        er = jnp.where(rowi == 7, cr_, pltpu.roll(ir, 7, 0))
                        ei = jnp.where(rowi == 7, ci_, pltpu.roll(ii, 7, 0))
                        ncr, nci = ir[0:1], ii[0:1]
                    else:
                        er = jnp.where(rowi == 0, cr_, pltpu.roll(ir, 1, 0))
                        ei = jnp.where(rowi == 0, ci_, pltpu.roll(ii, 1, 0))
                        ncr, nci = ir[7:8], ii[7:8]
                    dst[0, pl.ds(off, 8), :ZH] = er
                    dst[0, pl.ds(off, 8), ZH:] = ei
                    if sref is not None:
                        s_r = sref[0, pl.ds(off, 8), :ZH]
                        s_i = sref[0, pl.ds(off, 8), ZH:]
                        dar = dar + s_r * er + s_i * ei
                        dai = dai + s_r * ei - s_i * er
                    return ncr, nci, dar, dai

                carry_da = lax.fori_loop(0, ng, step, (*carry, *da))
                carry, da = carry_da[:2], carry_da[2:]
            if with_da:
                refs[k + 2][0, 0] = jnp.concatenate([da[0], da[1]], axis=1)

        for dd in range(2):
            @pl.when(d == dd)
            def _(dd=dd):
                run(chains[dd])

    zspec_l = pl.BlockSpec((1, jl, hw), lambda b, d: (b, 0, d))
    zspec_c = pl.BlockSpec((1, jc, hw), lambda b, d: (b, 0, d))
    in_specs = [zspec_l, zspec_c, pl.BlockSpec((1, 1, 24, hw), lambda b, d: (b, d, 0, 0))]
    args = [z_l, z_c, coef]
    out_specs = [zspec_l, zspec_c]
    out_shape = [jax.ShapeDtypeStruct(z_l.shape, F32), jax.ShapeDtypeStruct(z_c.shape, F32)]
    if with_da:
        in_specs += [zspec_l, zspec_c]
        args += [s_l, s_c]
        out_specs.append(pl.BlockSpec((1, 1, 8, hw), lambda b, d: (b, d, 0, 0)))
        out_shape.append(jax.ShapeDtypeStruct((nb, 2, 8, hw), F32))
    return pl.pallas_call(body, name=name, grid=(nb, 2), in_specs=in_specs, out_specs=out_specs,
                          out_shape=out_shape, compiler_params=_cparams())(*args)


def _glu_fwd(y_bcr, z_cr, w_glu, b_glu):
    nb, j, _ = y_bcr.shape
    e = nb * LANE_BLOCK
    tj = _cr_tile(j)

    def body(y_ref, z_ref, w_hbm, b_ref, o_ref, sg_ref, w_ref):
        @pl.when(jnp.logical_and(pl.program_id(0) == 0, pl.program_id(1) == 0))
        def _():
            pltpu.sync_copy(w_hbm, w_ref)

        y = jnp.concatenate([y_ref[b] for b in range(nb)], axis=1).astype(F32)
        g = _gelu_parts(y)[0]
        sg = _sigmoid(_dot(g.astype(BF), w_ref[...]) + b_ref[...])
        z = z_ref[...].astype(F32)
        o_ref[...] = (g * sg * (z * _sigmoid(z))).astype(BF)
        sg_ref[...] = sg.astype(BF)

    tok = pl.BlockSpec((tj, e), lambda t, s: (t, s))
    return pl.pallas_call(
        body, name="l1_glu_fwd", grid=(j // tj, CHUNK),
        in_specs=[pl.BlockSpec((nb, tj, LANE_BLOCK), lambda t, s: (0, t, s)), tok, ANY, _full((1, e))],
        out_specs=[tok, tok],
        out_shape=[jax.ShapeDtypeStruct((j, CHUNK * e), BF), jax.ShapeDtypeStruct((j, CHUNK * e), BF)],
        scratch_shapes=[pltpu.VMEM(w_glu.shape, BF)], compiler_params=_cparams(),
    )(y_bcr, z_cr, w_glu, b_glu)


def _final(w_cr, w_out, xh_cr, tgt_cr, vecs):
    j, e16 = w_cr.shape
    e = e16 // CHUNK
    d = w_out.shape[1]
    tj = _cr_tile(j)

    def body(w_ref, wo_hbm, xh_ref, t_ref, v_ref, dr_ref, acc_ref, wo_ref):
        @pl.when(jnp.logical_and(pl.program_id(0) == 0, pl.program_id(1) == 0))
        def _():
            pltpu.sync_copy(wo_hbm, wo_ref)
            acc_ref[...] = jnp.zeros_like(acc_ref)

        o = _dot(w_ref[...], wo_ref[...])
        x1 = xh_ref[...] * v_ref[0:1, :] + v_ref[1:2, :]
        rr = DN_ALPHA * x1 + v_ref[2:3, :] * o
        mu = jnp.mean(rr, axis=-1, keepdims=True)
        cen = rr - mu
        rstd = lax.rsqrt(jnp.mean(cen * cen, axis=-1, keepdims=True) + LN_EPS)
        xh2 = cen * rstd
        err = xh2 * v_ref[3:4, :] + v_ref[4:5, :] - t_ref[...]
        dy = err * (1.0 / d)
        dxh = dy * v_ref[3:4, :]
        dr = rstd * (dxh - jnp.mean(dxh, axis=-1, keepdims=True) - xh2 * jnp.mean(dxh * xh2, axis=-1, keepdims=True))
        dr_ref[...] = dr.astype(BF)
        acc_ref[0:1, :] += jnp.sum(dy * xh2, axis=0, keepdims=True)
        acc_ref[1:2, :] += jnp.sum(dy, axis=0, keepdims=True)
        acc_ref[2:3, :] += jnp.sum(dr * o, axis=0, keepdims=True)
        acc_ref[3:4, :] += (0.5 / d) * jnp.sum(err * err, axis=0, keepdims=True)

    tok_d = pl.BlockSpec((tj, d), lambda t, s: (t, s))
    return pl.pallas_call(
        body, name="l1_final", grid=(j // tj, CHUNK),
        in_specs=[pl.BlockSpec((tj, e), lambda t, s: (t, s)), ANY, tok_d, tok_d, _full((8, d))],
        out_specs=[tok_d, _full((8, d))],
        out_shape=[jax.ShapeDtypeStruct((j, CHUNK * d), BF), jax.ShapeDtypeStruct((8, d), F32)],
        scratch_shapes=[pltpu.VMEM(w_out.shape, BF)], compiler_params=_cparams(),
    )(w_cr, w_out, xh_cr, tgt_cr, vecs)


def _glu_bwd(dr_cr, gt1, w_out, w_glu, y_bcr, z_cr, sg_cr):
    nb, j, _ = y_bcr.shape
    e, d = w_out.shape
    tj = _cr_tile(j)

    def body(dr_ref, g_ref, wo_hbm, wg_hbm, y_ref, z_ref, sg_ref, dz_ref, dt_ref, dy_ref, wo_ref, wg_ref):
        @pl.when(jnp.logical_and(pl.program_id(0) == 0, pl.program_id(1) == 0))
        def _():
            pltpu.sync_copy(wo_hbm, wo_ref)
            pltpu.sync_copy(wg_hbm, wg_ref)

        do = (dr_ref[...].astype(F32) * g_ref[...]).astype(BF)
        dw = _dot_nt(do, wo_ref[...])
        y = jnp.concatenate([y_ref[b] for b in range(nb)], axis=1).astype(F32)
        g, dgel = _gelu_parts(y)
        z = z_ref[...].astype(F32)
        sz = _sigmoid(z)
        sg = sg_ref[...].astype(F32)
        dg2 = dw * (z * sz)
        dz_ref[...] = (dw * g * sg * (sz * (1.0 + z * (1.0 - sz)))).astype(BF)
        dt = (dg2 * g * sg * (1.0 - sg)).astype(BF)
        dt_ref[...] = dt
        dy = (dg2 * sg + _dot_nt(dt, wg_ref[...])) * dgel
        for b in range(nb):
            dy_ref[b] = dy[:, b * LANE_BLOCK:(b + 1) * LANE_BLOCK].astype(BF)

    tok_e = pl.BlockSpec((tj, e), lambda t, s: (t, s))
    blk = pl.BlockSpec((nb, tj, LANE_BLOCK), lambda t, s: (0, t, s))
    return pl.pallas_call(
        body, name="l1_glu_bwd", grid=(j // tj, CHUNK),
        in_specs=[pl.BlockSpec((tj, d), lambda t, s: (t, s)), _full((1, d)), ANY, ANY, blk, tok_e, tok_e],
        out_specs=[tok_e, tok_e, blk],
        out_shape=[jax.ShapeDtypeStruct((j, CHUNK * e), BF), jax.ShapeDtypeStruct((j, CHUNK * e), BF),
                   jax.ShapeDtypeStruct((nb, j, BCR_W), BF)],
        scratch_shapes=[pltpu.VMEM(w_out.shape, BF), pltpu.VMEM(w_glu.shape, BF)], compiler_params=_cparams(),
    )(dr_cr, gt1, w_out, w_glu, y_bcr, z_cr, sg_cr)


def _bwd_inproj1(du_bcr, dz_cr, w, xh_cr, rs_cr, dr2_cr, vecs, tag):
    nb, j, _ = du_bcr.shape
    d = w.shape[0]
    e = w.shape[1] // 2
    tj = _cr_tile(j)

    def body(du_ref, dz_ref, w_hbm, xh_ref, rs_ref, dr2_ref, v_ref, dr1_ref, acc_ref, w_ref):
        @pl.when(jnp.logical_and(pl.program_id(0) == 0, pl.program_id(1) == 0))
        def _():
            pltpu.sync_copy(w_hbm, w_ref)
            acc_ref[...] = jnp.zeros_like(acc_ref)

        du = jnp.concatenate([du_ref[b] for b in range(nb)], axis=1)
        dh = _dot_nt(du, w_ref[:, :e]) + _dot_nt(dz_ref[...], w_ref[:, e:])
        xh = xh_ref[...]
        x1 = xh * v_ref[0:1, :] + v_ref[1:2, :]
        dx1 = DN_ALPHA * dr2_ref[...].astype(F32) + dh * v_ref[2:3, :]
        dxh = dx1 * v_ref[0:1, :]
        rstd = rs_ref[:, 0:1]
        dr1 = rstd * (dxh - jnp.mean(dxh, axis=-1, keepdims=True) - xh * jnp.mean(dxh * xh, axis=-1, keepdims=True))
        dr1_ref[...] = dr1.astype(BF)
        acc_ref[0:1, :] += jnp.sum(dh * x1, axis=0, keepdims=True)
        acc_ref[1:2, :] += jnp.sum(dh, axis=0, keepdims=True)
        acc_ref[2:3, :] += jnp.sum(dx1 * xh, axis=0, keepdims=True)
        acc_ref[3:4, :] += jnp.sum(dx1, axis=0, keepdims=True)

    tok_d = pl.BlockSpec((tj, d), lambda t, s: (t, s))
    return pl.pallas_call(
        body, name="l1_bwd_inproj_" + tag, grid=(j // tj, CHUNK),
        in_specs=[pl.BlockSpec((nb, tj, LANE_BLOCK), lambda t, s: (0, t, s)), pl.BlockSpec((tj, e), lambda t, s: (t, s)),
                  ANY, tok_d, pl.BlockSpec((tj, 128), lambda t, s: (t, s)), tok_d, _full((8, d))],
        out_specs=[tok_d, _full((8, d))],
        out_shape=[jax.ShapeDtypeStruct((j, CHUNK * d), BF), jax.ShapeDtypeStruct((8, d), F32)],
        scratch_shapes=[pltpu.VMEM(w.shape, BF)], compiler_params=_cparams(),
    )(du_bcr, dz_cr, w, xh_cr, rs_cr, dr2_cr, vecs)


def _dw_cr(lhs, rhs, lhs_kind, rhs_kind, vec, bias_sum, init, name):
    if lhs_kind == "gelu_bcr":
        nb_l, j, _ = lhs.shape
        k = nb_l * LANE_BLOCK
    else:
        j = lhs.shape[0]
        k = lhs.shape[1] // CHUNK
    if rhs_kind == "bcr":
        nb_r = rhs.shape[0]
        n = nb_r * LANE_BLOCK
    else:
        n = rhs.shape[1] // CHUNK
    tj = _cr_tile(j, 512)
    nh = 2 if k * n * 4 > (8 << 20) else 1
    tn = n // nh
    nbh = tn // LANE_BLOCK
    nt = j // tj
    has_init = init is not None

    def body(*refs):
        refs = list(refs)
        l_ref, r_ref = refs[0], refs[1]
        pos = 2
        v_ref = None
        if vec is not None:
            v_ref = refs[pos]
            pos += 1
        i_ref = None
        if has_init:
            i_ref = refs[pos]
            pos += 1
        o_ref = refs[pos]
        pos += 1
        bs_ref = None
        if bias_sum:
            bs_ref = refs[pos]
            pos += 1
        acc_ref = refs[pos]
        t, s = pl.program_id(1), pl.program_id(2)
        first = jnp.logical_and(t == 0, s == 0)

        @pl.when(first)
        def _():
            acc_ref[...] = i_ref[...] if has_init else jnp.zeros_like(acc_ref)
            if bias_sum:
                bs_ref[...] = jnp.zeros_like(bs_ref)

        if lhs_kind == "gelu_bcr":
            y = jnp.concatenate([l_ref[b] for b in range(nb_l)], axis=1).astype(F32)
            lv = _gelu_parts(y)[0].astype(BF)
        elif lhs_kind == "mod":
            lv = (l_ref[...] * v_ref[0:1, :] + v_ref[1:2, :]).astype(BF)
        else:
            lv = l_ref[...]
        if rhs_kind == "bcr":
            rv = jnp.concatenate([r_ref[b] for b in range(nbh)], axis=1)
        elif rhs_kind == "scaled":
            rv = (r_ref[...].astype(F32) * v_ref[0:1, :]).astype(BF)
        else:
            rv = r_ref[...]
        acc_ref[...] += _dot_tn(lv, rv)
        if bias_sum:
            bs_ref[0:1, :] += jnp.sum(rv.astype(F32), axis=0, keepdims=True)

        @pl.when(jnp.logical_and(t == nt - 1, s == CHUNK - 1))
        def _():
            o_ref[...] = acc_ref[...].astype(BF)

    if lhs_kind == "gelu_bcr":
        l_spec = pl.BlockSpec((nb_l, tj, LANE_BLOCK), lambda h, t, s: (0, t, s))
    else:
        l_spec = pl.BlockSpec((tj, k), lambda h, t, s: (t, s))
    if rhs_kind == "bcr":
        r_spec = pl.BlockSpec((nbh, tj, LANE_BLOCK), lambda h, t, s: (h, t, s))
    else:
        r_spec = pl.BlockSpec((tj, tn), lambda h, t, s: (t, s * nh + h))
    in_specs, args = [l_spec, r_spec], [lhs, rhs]
    if vec is not None:
        in_specs.append(_full(vec.shape))
        args.append(vec)
    o_spec = pl.BlockSpec((k, tn), lambda h, t, s: (0, h))
    if has_init:
        in_specs.append(o_spec)
        args.append(init)
    out_specs, out_shape = [o_spec], [jax.ShapeDtypeStruct((k, n), BF)]
    if bias_sum:
        out_specs.append(pl.BlockSpec((8, tn), lambda h, t, s: (0, h)))
        out_shape.append(jax.ShapeDtypeStruct((8, n), F32))
    res = pl.pallas_call(
        body, name=name, grid=(nh, nt, CHUNK), in_specs=in_specs, out_specs=out_specs, out_shape=out_shape,
        scratch_shapes=[pltpu.VMEM((k, tn), F32)], compiler_params=_cparams(),
    )(*args)
    return res if bias_sum else res[0]


def _dw_cr_f32(lhs, rhs, vec, name):
    j = lhs.shape[0]
    k = lhs.shape[1] // CHUNK
    nb_r = rhs.shape[0]
    n = nb_r * LANE_BLOCK
    tj = _cr_tile(j)
    nt = j // tj

    def body(l_ref, r_ref, v_ref, o_ref):
        @pl.when(jnp.logical_and(pl.program_id(0) == 0, pl.program_id(1) == 0))
        def _():
            o_ref[...] = jnp.zeros_like(o_ref)

        lv = (l_ref[...] * v_ref[0:1, :] + v_ref[1:2, :]).astype(BF)
        rv = jnp.concatenate([r_ref[b] for b in range(nb_r)], axis=1)
        o_ref[...] += _dot_tn(lv, rv)

    return pl.pallas_call(
        body, name=name, grid=(nt, CHUNK),
        in_specs=[pl.BlockSpec((tj, k), lambda t, s: (t, s)), pl.BlockSpec((nb_r, tj, LANE_BLOCK), lambda t, s: (0, t, s)),
                  _full(vec.shape)],
        out_specs=_full((k, n)), out_shape=jax.ShapeDtypeStruct((k, n), F32), compiler_params=_cparams(),
    )(lhs, rhs, vec)


def _s5_compact(lam_re, lam_im, log_step, b_re, b_im, c_re, c_im, d_skip):
    hp = lax.Precision.HIGHEST
    g = lam_re.shape[1]
    nb = g // GROUPS_PER_BLOCK
    t, p, n = CHUNK, S5_P, S5_N
    dt = jnp.exp(log_step)[..., None]
    ks = jnp.arange(t + 1, dtype=F32).reshape(t + 1, 1, 1, 1)
    mag = jnp.exp(ks * (lam_re * dt)[None])
    ang = ks * (lam_im * dt)[None]
    pr, pi = mag * jnp.cos(ang), mag * jnp.sin(ang)
    ar, ai = pr[1], pi[1]
    qr, qi = ar - 1.0, ai
    den = lam_re * lam_re + lam_im * lam_im
    fr = (qr * lam_re + qi * lam_im) / den
    fi = (qi * lam_re - qr * lam_im) / den
    bt_re, bt_im = b_re.transpose(0, 1, 3, 2), b_im.transpose(0, 1, 3, 2)
    bbr = fr[:, :, None, :] * bt_re - fi[:, :, None, :] * bt_im
    bbi = fr[:, :, None, :] * bt_im + fi[:, :, None, :] * bt_re
    pk_r, pk_i = pr[:t, :, :, None, :], pi[:t, :, :, None, :]
    abr = pk_r * bbr[None] - pk_i * bbi[None]
    abi = pk_r * bbi[None] + pk_i * bbr[None]
    kd = (jnp.einsum("rgpn,krgqn->rgkpq", c_re, abr, precision=hp)
          - jnp.einsum("rgpn,krgqn->rgkpq", c_im, abi, precision=hp))
    skip = jnp.eye(p, dtype=F32)[None] * d_skip.reshape(g, p)[:, :, None]
    diag = kd[0][:, 0] + kd[1][:, 0] + skip
    qd = jnp.concatenate([jnp.flip(kd[1][:, 1:], axis=1), diag[:, None], kd[0][:, 1:]], axis=1)
    nd = 2 * t - 1
    wc = qd.transpose(0, 1, 3, 2).reshape(nb, GROUPS_PER_BLOCK, nd, p, p).transpose(0, 2, 1, 3, 4)
    wcomp = wc.reshape(nb, nd, LANE_BLOCK, p)
    ab = jnp.concatenate([abr, abi], axis=-1)
    bcc = jnp.stack([jnp.flip(ab[:, 0], axis=0), ab[:, 1]])
    bcomp = bcc.reshape(2, t, nb, LANE_BLOCK, 2 * n)
    prf = jnp.stack([pr[1:, 0], jnp.flip(pr[1:, 1], axis=0)])[:, :, :, None, :]
    pif = jnp.stack([pi[1:, 0], jnp.flip(pi[1:, 1], axis=0)])[:, :, :, None, :]
    cr_t = c_re[:, None]
    ci_t = c_im[:, None]
    ccc = jnp.concatenate([cr_t * prf - ci_t * pif, -(cr_t * pif + ci_t * prf)], axis=-1)
    ccomp = ccc.reshape(2, t, nb, LANE_BLOCK, 2 * n)
    return wcomp, bcomp, ccomp, pr[t], pi[t]


def _scan_coef(lam_re, lam_im, log_step):
    g = lam_re.shape[1]
    nb = g // GROUPS_PER_BLOCK
    ms = jnp.array([1, 2, 4, 0, 0, 0, 0, 0] + list(range(1, 9)) + list(range(8, 0, -1)), F32) * CHUNK
    dt = jnp.exp(log_step)[..., None]
    mag = jnp.exp(ms.reshape(-1, 1, 1, 1) * (lam_re * dt)[None])
    ang = ms.reshape(-1, 1, 1, 1) * (lam_im * dt)[None]
    cr, ci = mag * jnp.cos(ang), mag * jnp.sin(ang)
    lay = lambda a: a.reshape(24, 2, nb, ZH).transpose(2, 1, 0, 3)
    return jnp.concatenate([lay(cr), lay(ci)], axis=-1)


def _to_cr(a):
    return a.reshape(a.shape[0] // CHUNK, CHUNK * a.shape[1])


def _from_cr(a, c):
    return a.reshape(a.shape[0] * CHUNK, c)


def _pad8(v):
    return jnp.concatenate([v, jnp.zeros((8 - v.shape[0], v.shape[1]), v.dtype)], axis=0)


def _local_step(x, c, ctx, c_ctx, loss_target, w, late=None, scatter=False):
    l, d = x.shape
    lc = ctx.shape[0]
    tm = min(256, lc)
    assert lc == tm and l % tm == 0 and tm % GRID_W == 0 and (tm & (tm - 1)) == 0
    nl = l // tm

    c8 = _pad8(jnp.stack([c, c_ctx]))
    mod = _ada_fwd(c8, w["ada_w"], w["ada_b"])
    sh = mod[:, :2, :d]
    sc = mod[:, :2, d:2 * d]
    gt = mod[:, :2, 2 * d:]
    ln_g, ln_b = w["ln_g"], w["ln_b"]

    a0, b0 = 1.0 + sc[0], sh[0]
    xch = _Exchange("gather", [late[n][0] for n in late], [late[n][1] for n in late]) if late else None
    p42, got = _inproj0(x, ctx, a0, b0, w["conv_w_in"], tm, xch)
    if late:
        w = dict(w, **dict(zip(late, got)))
    e = w["conv_w_out"].shape[0]
    half = e // 2
    nb = e // LANE_BLOCK
    tc = min(512, half)
    cw = w["conv_w"].reshape(3, 2, half)
    q3 = _conv_fwd(p42, cw, nl, tm, tc)
    xh1_l, xh1_c, rs1_l, rs1_c, fx = _outproj_ln0(q3, w["conv_w_out"], x, ctx, gt[0], tm)
    jl, jc = l // CHUNK, lc // CHUNK

    g0, bb0 = ln_g[0:1], ln_b[0:1]
    a1 = g0 * (1.0 + sc[1])
    b1 = bb0 * (1.0 + sc[1]) + sh[1]
    u_l, z_l = _inproj1(xh1_l, a1[0:1], b1[0:1], w["ssm_w_in"], "lat")
    u_c, _ = _inproj1(xh1_c, a1[1:2], b1[1:2], w["ssm_w_in"], "ctx")
    s5 = (w["ssm_lam_re"], w["ssm_lam_im"], w["ssm_log_step"], w["ssm_b_re"], w["ssm_b_im"],
          w["ssm_c_re"], w["ssm_c_im"], w["ssm_d"])
    (wcomp, bcomp, ccomp, _, _), s5_vjp = jax.vjp(_s5_compact, *s5)
    wbig_b = _expand_toeplitz(wcomp)
    bc_b = _expand_blocks(bcomp, "l1_expand_bc")
    cct_b = _expand_blocks(ccomp, "l1_expand_cc")
    coef = lax.stop_gradient(_scan_coef(*s5[:3]))
    zz_l, zz_c = _bmm([u_l], [bc_b], [False], F32, "l1_s5_z", ctx=[u_c])
    fwd_chains = ((("c", False), ("l", False)), (("c", True), ("l", True)))
    st_l, st_c = _scan(zz_l, zz_c, coef, fwd_chains, False, name="l1_scan_fwd")
    y_l = _bmm([u_l, st_l], [wbig_b, cct_b], [False, True], BF, "l1_s5_y")
    b_glu = w["ssm_b_glu"].reshape(1, e)
    w_cr, sg_cr = _glu_fwd(y_l, z_l, w["ssm_w_glu"], b_glu)
    vec_f = _pad8(jnp.concatenate([g0, bb0, gt[1][0:1], ln_g[1:2], ln_b[1:2]], axis=0))
    dr2, acc_f = _final(w_cr, w["ssm_w_out"], xh1_l, _to_cr(loss_target), vec_f)
    loss = jnp.sum(acc_f[3])

    gt1 = gt[1][0:1]
    dz_l, dt_l, dy_l = _glu_bwd(dr2, gt1, w["ssm_w_out"], w["ssm_w_glu"], y_l, z_l, sg_cr)
    g_w_out = _dw_cr(w_cr, dr2, "cr", "scaled", gt1, False, None, "l1_dw_out")
    g_w_glu, bsum = _dw_cr(y_l, dt_l, "gelu_bcr", "cr", None, True, None, "l1_dw_glu")
    g_b_glu = bsum[0]
    ds_l = _bmm([dy_l], [cct_b], [False], F32, "l1_s5_ds")
    bwd_chains = ((("l", True), ("c", True)), (("l", False), ("c", False)))
    dzz_l, dzz_c, da = _scan(ds_l, jnp.zeros_like(zz_c), coef, bwd_chains, True, st_l, st_c, name="l1_scan_bwd")
    du_l, du_c = _bmm([dy_l, dzz_l], [wbig_b, bc_b], [True, True], BF, "l1_s5_dx", ctx=[None, dzz_c])
    d_wbig = _bdw(u_l, dy_l, "toeplitz", None, "l1_s5_dwbig")
    d_cc = _bdw(dy_l, st_l, "blocks", None, "l1_s5_dcc")
    d_bc = _bdw(u_l, dzz_l, "blocks", (u_c, dzz_c), "l1_s5_dbc")
    da = jnp.sum(da, axis=2)
    unlay = lambda a: a.reshape(nb, 2, GROUPS_PER_BLOCK, S5_N).transpose(1, 0, 2, 3).reshape(2, nb * GROUPS_PER_BLOCK, S5_N)
    g_s5 = s5_vjp((d_wbig, d_bc, d_cc, unlay(da[..., :ZH]), unlay(da[..., ZH:])))

    vec_l = _pad8(jnp.concatenate([g0, bb0, 1.0 + sc[1][0:1]], axis=0))
    vec_c = _pad8(jnp.concatenate([g0, bb0, 1.0 + sc[1][1:2]], axis=0))
    dr1_l, acc_l = _bwd_inproj1(du_l, dz_l, w["ssm_w_in"], xh1_l, rs1_l, dr2, vec_l, "lat")
    dr1_c, acc_c = _bwd_inproj1(du_c, jnp.zeros((jc, CHUNK * e), BF), w["ssm_w_in"], xh1_c, rs1_c,
                                jnp.zeros((jc, CHUNK * d), BF), vec_c, "ctx")
    mod_l = jnp.concatenate([a1[0:1], b1[0:1]], axis=0)
    mod_c = jnp.concatenate([a1[1:2], b1[1:2]], axis=0)
    g_in_u = _dw_cr(xh1_l, du_l, "mod", "bcr", mod_l, False, _dw_cr_f32(xh1_c, du_c, mod_c, "l1_dw_in_u_ctx"), "l1_dw_in_u")
    g_in_z = _dw_cr(xh1_l, dz_l, "mod", "cr", mod_l, False, None, "l1_dw_in_z")
    g_w_in1 = jnp.concatenate([g_in_u, g_in_z], axis=1)

    dr1_ln, dr1_cn = _from_cr(dr1_l, d), _from_cr(dr1_c, d)
    dq3, acc_g0 = _bwd_outproj0(dr1_ln, dr1_cn, gt[0], w["conv_w_out"], fx, tm)
    sent1 = ["ssm_w_in", "ssm_w_glu", "ssm_w_out"]
    xch1 = _Exchange("scatter", [g_w_in1, g_w_glu, g_w_out], [BIG[n] for n in sent1]) if scatter else None
    dp42, dcw, recv1 = _conv_bwd(dq3, p42, cw, nl, tm, tc, xch1)
    g_w_in0 = _dw_inproj0(x, ctx, a0, b0, dp42, tm)
    g_w_out0 = _dw_outproj0(q3, dr1_ln, dr1_cn, gt[0], tm)
    sent0 = ["conv_w_in", "conv_w_out"]
    xch0 = _Exchange("scatter", [g_w_in0, g_w_out0], [BIG[n] for n in sent0]) if scatter else None
    grad_x, acc_0, recv0 = _bwd_inproj0(dp42, w["conv_w_in"], x, ctx, dr1_ln, dr1_cn, a0, tm, xch0)
    recv = dict(zip(sent1 + sent0, recv1 + recv0))

    zero = jnp.zeros((d,), F32)
    dm0 = jnp.stack([jnp.concatenate([acc_0[2], acc_0[0], acc_g0[0]]), jnp.concatenate([acc_0[3], acc_0[1], acc_g0[1]])])
    dm1 = jnp.stack([jnp.concatenate([acc_l[1], acc_l[0], acc_f[2]]), jnp.concatenate([acc_c[1], acc_c[0], zero])])
    dm8 = jnp.stack([_pad8(dm0), _pad8(dm1)])
    g_ada_w, dc8 = _ada_bwd(c8, w["ada_w"], dm8)

    grads = {
        "c_ctx": dc8[0, 1] + dc8[1, 1],
        "ada_w": g_ada_w,
        "ada_b": jnp.stack([dm0[0] + dm0[1], dm1[0] + dm1[1]]),
        "ln_g": jnp.stack([acc_l[2] + acc_c[2], acc_f[0]]),
        "ln_b": jnp.stack([acc_l[3] + acc_c[3], acc_f[1]]),
        "conv_w_in": g_w_in0, "conv_w": dcw[:3].reshape(3, e), "conv_w_out": g_w_out0,
        "ssm_w_in": g_w_in1,
        "ssm_lam_re": g_s5[0], "ssm_lam_im": g_s5[1], "ssm_log_step": g_s5[2],
        "ssm_b_re": g_s5[3], "ssm_b_im": g_s5[4], "ssm_c_re": g_s5[5], "ssm_c_im": g_s5[6], "ssm_d": g_s5[7],
        "ssm_w_glu": g_w_glu, "ssm_b_glu": g_b_glu, "ssm_w_out": g_w_out,
    }
    for n in recv:
        del grads[n]
    return loss, grad_x, grads, recv


WEIGHTS = ["c_ctx", "ada_w", "ada_b", "ln_g", "ln_b", "conv_w_in", "conv_w", "conv_w_out", "ssm_w_in",
           "ssm_lam_re", "ssm_lam_im", "ssm_log_step", "ssm_b_re", "ssm_b_im", "ssm_c_re", "ssm_c_im",
           "ssm_d", "ssm_w_glu", "ssm_b_glu", "ssm_w_out"]
BIG = {"ada_w": 1, "conv_w_in": 1, "conv_w_out": 0, "ssm_w_in": 1, "ssm_w_glu": 0, "ssm_w_out": 0}
SMALL_SHARDED = ["conv_w", "ssm_d", "ssm_b_glu"]
REPLICATED = ["c_ctx", "ada_b", "ln_g", "ln_b", "ssm_lam_re", "ssm_lam_im", "ssm_log_step",
              "ssm_b_re", "ssm_b_im", "ssm_c_re", "ssm_c_im"]


def _view2d(name, a):
    return a.reshape(-1, a.shape[-1])


def kernel(x, c, ctx, c_ctx, ada_w, ada_b, ln_g, ln_b, conv_w_in, conv_w, conv_w_out, ssm_w_in, ssm_lam_re, ssm_lam_im, ssm_log_step, ssm_b_re, ssm_b_im, ssm_c_re, ssm_c_im, ssm_d, ssm_w_glu, ssm_b_glu, ssm_w_out, loss_target, m_c_ctx, m_ada_w, m_ada_b, m_ln_g, m_ln_b, m_conv_w_in, m_conv_w, m_conv_w_out, m_ssm_w_in, m_ssm_lam_re, m_ssm_lam_im, m_ssm_log_step, m_ssm_b_re, m_ssm_b_im, m_ssm_c_re, m_ssm_c_im, m_ssm_d, m_ssm_w_glu, m_ssm_b_glu, m_ssm_w_out, v_c_ctx, v_ada_w, v_ada_b, v_ln_g, v_ln_b, v_conv_w_in, v_conv_w, v_conv_w_out, v_ssm_w_in, v_ssm_lam_re, v_ssm_lam_im, v_ssm_log_step, v_ssm_b_re, v_ssm_b_im, v_ssm_c_re, v_ssm_c_im, v_ssm_d, v_ssm_w_glu, v_ssm_b_glu, v_ssm_w_out):
    args = locals()
    wt = {n: args[n] for n in WEIGHTS}
    mt = {n: args["m_" + n] for n in WEIGHTS}
    vt = {n: args["v_" + n] for n in WEIGHTS}

    big_names = list(BIG)
    shard = {n: _view2d(n, wt[n]).astype(BF) for n in big_names}
    first = ["ada_w", "conv_w_in"]
    small = jnp.concatenate([wt["conv_w"][0], wt["ssm_d"], wt["ssm_b_glu"]], axis=0)
    small = jnp.concatenate([small, jnp.zeros((3, small.shape[1]), F32)], axis=0)
    gathered = _all_gather([shard[n] for n in first] + [small], [BIG[n] for n in first] + [1], "gather_weights")
    full = dict(zip(first, gathered[:-1]))
    small_full = gathered[-1]
    late = {n: (shard[n], BIG[n]) for n in big_names if n not in first}
    d = x.shape[-1]
    w = {
        "ada_w": full["ada_w"].reshape(2, d, 3 * d), "ada_b": ada_b, "ln_g": ln_g, "ln_b": ln_b,
        "conv_w_in": full["conv_w_in"], "conv_w": small_full[0:3],
        "ssm_lam_re": ssm_lam_re[0], "ssm_lam_im": ssm_lam_im[0],
        "ssm_log_step": ssm_log_step[0], "ssm_b_re": ssm_b_re[0], "ssm_b_im": ssm_b_im[0],
        "ssm_c_re": ssm_c_re[0], "ssm_c_im": ssm_c_im[0], "ssm_d": small_full[3], "ssm_b_glu": small_full[4],
    }

    loss, grad_x, g, recv_big = _local_step(x[0], c[0], ctx[0], c_ctx, loss_target[0], w, late, True)
    loss = lax.psum(loss, ("x", "y", "c"))

    blob_names = REPLICATED + SMALL_SHARDED
    flat = jnp.concatenate([g[n].reshape(-1).astype(F32) for n in blob_names])
    nflat = flat.shape[0]
    rows = -(-nflat // (N_DEV * 128 * 8)) * 8
    flat = jnp.concatenate([flat, jnp.zeros((N_DEV * rows * 128 - nflat,), F32)]).reshape(N_DEV * rows, 128)
    last = [n for n in big_names if n not in recv_big]
    recv = _all_to_all([_view2d(n, g[n]) for n in last] + [flat], [BIG[n] for n in last] + [0], "scatter_grads")
    recv_big.update(zip(last, recv[:-1]))
    blob_sum = _sum_partials(recv[-1])
    blob = _all_gather([blob_sum], [0], "gather_small_grads")[0].reshape(-1)
    small_g, off = {}, 0
    for n in blob_names:
        shape = wt[n].shape if n in REPLICATED else (*wt[n].shape[:-1], wt[n].shape[-1] * N_DEV)
        size = math.prod(shape)
        small_g[n] = blob[off:off + size].reshape(shape)
        off += size
    me = 4 * lax.axis_index("x") + 2 * lax.axis_index("y") + lax.axis_index("c")
    for n in SMALL_SHARDED:
        size = wt[n].shape[-1]
        small_g[n] = lax.dynamic_slice_in_dim(small_g[n], me * size, size, axis=small_g[n].ndim - 1)

    out_g, out_d, out_m, out_v = {}, {}, {}, {}
    for n in big_names:
        stack = recv_big[n]
        shp = wt[n].shape
        res = _adamw(stack, _view2d(n, wt[n]), _view2d(n, mt[n]), _view2d(n, vt[n]), "adamw_" + n)
        out_g[n], out_d[n], out_m[n], out_v[n] = [r.reshape(shp) for r in res]
    names = list(small_g)
    cat = lambda t: jnp.concatenate([t[n].reshape(-1) for n in names])
    gs, ws, ms, vs = cat(small_g), cat(wt), cat(mt), cat(vt)
    ns = gs.shape[0]
    rs = -(-ns // (128 * 512)) * 512
    padr = lambda a: jnp.concatenate([a, jnp.ones((rs * 128 - ns,), F32)]).reshape(rs, 128)
    res = _adamw(padr(gs)[None], padr(ws), padr(ms), padr(vs), "adamw_small")
    off = 0
    for n in names:
        size = math.prod(wt[n].shape)
        out_g[n], out_d[n], out_m[n], out_v[n] = [r.reshape(-1)[off:off + size].reshape(wt[n].shape) for r in res]
        off += size

    return (loss, grad_x[None], *[out_g[n] for n in WEIGHTS], *[out_d[n] for n in WEIGHTS],
            *[out_m[n] for n in WEIGHTS], *[out_v[n] for n in WEIGHTS])
```

```python
import math

import jax
import jax.numpy as jnp
from jax import lax
from jax.experimental import pallas as pl
from jax.experimental.pallas import tpu as pltpu

F32 = jnp.float32
BF = jnp.bfloat16
MESH = pl.DeviceIdType.MESH
N_DEV = 8

GRID_W = 64
CHUNK = 16
S5_P = 16
S5_N = 64
LANE_BLOCK = 128
GROUPS_PER_BLOCK = LANE_BLOCK // S5_P
BCR_W = CHUNK * LANE_BLOCK
ZL_W = 2 * 2 * GROUPS_PER_BLOCK * S5_N
ZH = ZL_W // 4
LN_EPS = 1e-5
DN_ALPHA = 4.0 ** 0.25
ADAM_LR, ADAM_B1, ADAM_B2, ADAM_EPS, ADAM_WD, ADAM_STEP = 1e-3, 0.9, 0.999, 1e-8, 0.01, 10
GELU_C0 = math.sqrt(2.0 / math.pi)
GELU_C1 = 0.044715
VMEM_MB = 52

ANY = pl.BlockSpec(memory_space=pl.ANY)


def _cparams():
    return pltpu.CompilerParams(vmem_limit_bytes=VMEM_MB << 20)


def _dot(a, b):
    return jnp.dot(a, b, preferred_element_type=F32)


def _dot_nt(a, b):
    return lax.dot_general(a, b, (((1,), (1,)), ((), ())), preferred_element_type=F32)


def _dot_tn(a, b):
    return lax.dot_general(a, b, (((0,), (0,)), ((), ())), preferred_element_type=F32)


def _sigmoid(x):
    return 1.0 / (1.0 + jnp.exp(-x))


def _gelu_parts(y):
    th = jnp.tanh(GELU_C0 * (y + GELU_C1 * y * y * y))
    g = 0.5 * y * (1.0 + th)
    dg = 0.5 * (1.0 + th) + 0.5 * y * (1.0 - th * th) * GELU_C0 * (1.0 + 3.0 * GELU_C1 * y * y)
    return g, dg


def _full(shape):
    nd = len(shape)
    return pl.BlockSpec(shape, lambda *_: (0,) * nd)


def _mesh_pos():
    x, y, c = lax.axis_index("x"), lax.axis_index("y"), lax.axis_index("c")
    return x, y, c


def _peer(pos, k):
    x, y, c = pos
    px = 1 - x if (k >> 2) & 1 else x
    py = 1 - y if (k >> 1) & 1 else y
    pc = 1 - c if k & 1 else c
    return (px, py, pc), 4 * px + 2 * py + pc


def _shard_at(ref, axis, idx, n):
    if axis == 0:
        return ref.at[pl.ds(idx * n, n)]
    return ref.at[:, pl.ds(idx * n, n)]


class _Exchange:
    def __init__(self, kind, arrays, axes):
        self.kind, self.axes, self.n = kind, list(axes), len(arrays)
        self.arrays = list(arrays)
        self.out_shape = []
        for s, ax in zip(arrays, axes):
            shp = list(s.shape)
            if kind == "gather":
                shp[ax] *= N_DEV
                self.out_shape.append(jax.ShapeDtypeStruct(tuple(shp), s.dtype))
            else:
                shp[ax] //= N_DEV
                self.out_shape.append(jax.ShapeDtypeStruct((N_DEV, *shp), s.dtype))
        self.scratch = [pltpu.SemaphoreType.DMA((self.n, N_DEV - 1)), pltpu.SemaphoreType.DMA((self.n, N_DEV - 1)),
                        pltpu.SemaphoreType.DMA((self.n,))]

    def _copies(self, ins, outs, sems):
        send_sems, recv_sems, local_sems = sems
        pos = _mesh_pos()
        me = 4 * pos[0] + 2 * pos[1] + pos[2]
        local, sends, recvs = [], [], []
        for i in range(self.n):
            ax = self.axes[i]
            if self.kind == "gather":
                size = ins[i].shape[ax]
                src = lambda idx, i=i: ins[i]
                dst = lambda idx, i=i, ax=ax, size=size: _shard_at(outs[i], ax, idx, size)
                mine, theirs = (lambda pidx: me), (lambda pidx: pidx)
            else:
                size = ins[i].shape[ax] // N_DEV
                src = lambda idx, i=i, ax=ax, size=size: _shard_at(ins[i], ax, idx, size)
                dst = lambda idx, i=i: outs[i].at[idx]
                mine, theirs = (lambda pidx: me), (lambda pidx: pidx)
            src_own = src(me)
            local.append(pltpu.make_async_copy(src_own, dst(me), local_sems.at[i]))
            for k in range(1, N_DEV):
                peer, pidx = _peer(pos, k)
                out_src = src(me) if self.kind == "gather" else src(pidx)
                sends.append(pltpu.make_async_remote_copy(
                    src_ref=out_src, dst_ref=dst(mine(pidx)), send_sem=send_sems.at[i, k - 1],
                    recv_sem=recv_sems.at[i, k - 1], device_id=peer, device_id_type=MESH))
                recvs.append(pltpu.make_async_remote_copy(
                    src_ref=out_src, dst_ref=dst(theirs(pidx)), send_sem=send_sems.at[i, k - 1],
                    recv_sem=recv_sems.at[i, k - 1], device_id=peer, device_id_type=MESH))
        return local, sends, recvs

    def start(self, ins, outs, sems):
        local, sends, _ = self._copies(ins, outs, sems)
        for cp in local + sends:
            cp.start()

    def wait(self, ins, outs, sems):
        local, sends, recvs = self._copies(ins, outs, sems)
        for cp in recvs:
            cp.wait_recv()
        for cp in sends:
            cp.wait_send()
        for cp in local:
            cp.wait()

    def run(self, name):
        n = self.n

        def body(*refs):
            ins, outs, sems = refs[:n], refs[n:2 * n], refs[2 * n:]
            self.start(ins, outs, sems)
            self.wait(ins, outs, sems)

        return pl.pallas_call(body, name=name, out_shape=self.out_shape, in_specs=[ANY] * n, out_specs=[ANY] * n,
                              scratch_shapes=self.scratch)(*self.arrays)


def _hosted_call(body, xch, grid, in_specs, out_specs, out_shape, scratch, args, name):
    out_specs, out_shape = list(out_specs), list(out_shape)
    n_in, n_out = len(in_specs), len(out_specs)
    if xch is None:
        res = pl.pallas_call(body, name=name, grid=grid, in_specs=in_specs, out_specs=out_specs, out_shape=out_shape,
                             scratch_shapes=list(scratch), compiler_params=_cparams())(*args)
        return list(res), []
    n = xch.n
    rank = len(grid)

    def wrapped(*refs):
        ins, x_ins = refs[:n_in], refs[n_in:n_in + n]
        outs = refs[n_in + n:n_in + n + n_out]
        x_outs = refs[n_in + n + n_out:n_in + 2 * n + n_out]
        rest = refs[n_in + 2 * n + n_out:]
        own, sems = rest[:len(rest) - 3], rest[len(rest) - 3:]
        ids = [pl.program_id(a) for a in range(rank)]
        first, last = ids[0] == 0, ids[0] == grid[0] - 1
        for a in range(1, rank):
            first = jnp.logical_and(first, ids[a] == 0)
            last = jnp.logical_and(last, ids[a] == grid[a] - 1)

        @pl.when(first)
        def _():
            xch.start(x_ins, x_outs, sems)

        body(*ins, *outs, *own)

        @pl.when(last)
        def _():
            xch.wait(x_ins, x_outs, sems)

    res = pl.pallas_call(
        wrapped, name=name, grid=grid, in_specs=list(in_specs) + [ANY] * n, out_specs=out_specs + [ANY] * n,
        out_shape=out_shape + xch.out_shape, scratch_shapes=list(scratch) + xch.scratch, compiler_params=_cparams(),
    )(*args, *xch.arrays)
    return list(res[:n_out]), list(res[n_out:])


def _all_gather(shards, axes, name):
    return _Exchange("gather", shards, axes).run(name)


def _all_to_all(parts, axes, name):
    return _Exchange("scatter", parts, axes).run(name)


def _ada_fwd(c8, ada_w, ada_b):
    nl, d, d3 = ada_w.shape

    def body(c_ref, w_ref, b_ref, o_ref):
        cv = c_ref[...]
        s = (cv * _sigmoid(cv)).astype(BF)
        o_ref[0] = _dot(s, w_ref[0]) + b_ref[0]

    return pl.pallas_call(
        body, name="ada_fwd", grid=(nl,),
        in_specs=[_full((8, d)), pl.BlockSpec((1, d, d3), lambda l: (l, 0, 0)), pl.BlockSpec((1, 1, d3), lambda l: (l, 0, 0))],
        out_specs=pl.BlockSpec((1, 8, d3), lambda l: (l, 0, 0)),
        out_shape=jax.ShapeDtypeStruct((nl, 8, d3), F32), compiler_params=_cparams(),
    )(c8, ada_w, ada_b.reshape(nl, 1, d3))


def _ada_bwd(c8, ada_w, dm8):
    nl, d, d3 = ada_w.shape

    def body(c_ref, w_ref, dm_ref, dw_ref, dc_ref):
        cv = c_ref[...]
        sg = _sigmoid(cv)
        s = (cv * sg).astype(BF)
        dm = dm_ref[0].astype(BF)
        dw_ref[0] = _dot_tn(s, dm).astype(BF)
        dc_ref[0] = _dot_nt(dm, w_ref[0]) * (sg * (1.0 + cv * (1.0 - sg)))

    return pl.pallas_call(
        body, name="ada_bwd", grid=(nl,),
        in_specs=[_full((8, d)), pl.BlockSpec((1, d, d3), lambda l: (l, 0, 0)), pl.BlockSpec((1, 8, d3), lambda l: (l, 0, 0))],
        out_specs=[pl.BlockSpec((1, d, d3), lambda l: (l, 0, 0)), pl.BlockSpec((1, 8, d), lambda l: (l, 0, 0))],
        out_shape=[jax.ShapeDtypeStruct((nl, d, d3), BF), jax.ShapeDtypeStruct((nl, 8, d), F32)],
        compiler_params=_cparams(),
    )(c8, ada_w, dm8)


def _sum_partials(stack):
    _, r, c = stack.shape

    def body(s_ref, o_ref):
        acc = s_ref[0]
        for p in range(1, N_DEV):
            acc = acc + s_ref[p]
        o_ref[...] = acc

    return pl.pallas_call(body, name="sum_partials", out_shape=jax.ShapeDtypeStruct((r, c), F32),
                          in_specs=[_full(stack.shape)], out_specs=_full((r, c)), grid=(1,),
                          compiler_params=_cparams())(stack)


def _adamw(gstack, w, m, v, name):
    p, r, c = gstack.shape
    tr = r
    for cand in (512 if c <= 256 else 256, 128, 64, 32, 16, 8):
        if r % cand == 0 and r > cand:
            tr = cand
            break
    bc1 = 1.0 - ADAM_B1 ** ADAM_STEP
    bc2 = 1.0 - ADAM_B2 ** ADAM_STEP

    def body(g_ref, w_ref, m_ref, v_ref, go_ref, d_ref, mo_ref, vo_ref):
        g = g_ref[0].astype(F32)
        for q in range(1, p):
            g = g + g_ref[q].astype(F32)
        mn = ADAM_B1 * m_ref[...] + (1.0 - ADAM_B1) * g
        vn = ADAM_B2 * v_ref[...] + (1.0 - ADAM_B2) * (g * g)
        go_ref[...] = g
        mo_ref[...] = mn
        vo_ref[...] = vn
        d_ref[...] = -ADAM_LR * ((mn / bc1) / (jnp.sqrt(vn / bc2) + ADAM_EPS) + ADAM_WD * w_ref[...])

    row = pl.BlockSpec((tr, c), lambda i: (i, 0))
    sds = jax.ShapeDtypeStruct((r, c), F32)
    return pl.pallas_call(
        body, name=name, grid=(r // tr,),
        in_specs=[pl.BlockSpec((p, tr, c), lambda i: (0, i, 0)), row, row, row],
        out_specs=[row, row, row, row], out_shape=[sds, sds, sds, sds], compiler_params=_cparams(),
    )(gstack, w, m, v)


def _lat_or_ctx_specs(tm, d, nl, grid_rank, row_axis):
    def lat(*ids):
        return (jnp.minimum(ids[row_axis], nl - 1), 0)

    def ctx(*ids):
        return (jnp.maximum(ids[row_axis] - nl, 0), 0)

    return pl.BlockSpec((tm, d), lat), pl.BlockSpec((tm, d), ctx)


def _sel_row(ref, is_ctx):
    return jnp.where(is_ctx, ref[1:2, :], ref[0:1, :])


def _inproj0(x, ctx, a2, b2, w, tm, xch=None):
    l, d = x.shape
    nl, nc = l // tm, ctx.shape[0] // tm
    e = w.shape[1] // 4
    half = e // 2

    def body(x_ref, c_ref, a_ref, b_ref, w_hbm, o_ref, w_ref):
        i = pl.program_id(0)

        @pl.when(i == 0)
        def _():
            pltpu.sync_copy(w_hbm, w_ref)

        is_ctx = i >= nl
        xv = jnp.where(is_ctx, c_ref[...], x_ref[...])
        h = (xv * _sel_row(a_ref, is_ctx) + _sel_row(b_ref, is_ctx)).astype(BF)
        for k in range(4):
            r = _dot(h, w_ref[:, k * e:(k + 1) * e])
            o_ref[k, 0] = r[:, :half].astype(BF)
            o_ref[k, 1] = r[:, half:].astype(BF)

    lat, cx = _lat_or_ctx_specs(tm, d, nl, 1, 0)
    (p42,), extra = _hosted_call(
        body, xch, grid=(nl + nc,),
        in_specs=[lat, cx, _full((2, d)), _full((2, d)), ANY],
        out_specs=[pl.BlockSpec((4, 2, tm, half), lambda i: (0, 0, i, 0))],
        out_shape=[jax.ShapeDtypeStruct((4, 2, l + ctx.shape[0], half), BF)],
        scratch=[pltpu.VMEM(w.shape, BF)], args=(x, ctx, a2, b2, w), name="l0_inproj")
    return p42, extra


def _conv_taps(u, w_up, w_mid, w_dn, pos, rl, tm):
    up = jnp.where(pos == 0, 0.0, pltpu.roll(u, 1, 0))
    dn = jnp.where(pos == rl - 1, 0.0, pltpu.roll(u, tm - 1, 0))
    return w_up * up + w_mid * u + w_dn * dn, up, dn


def _conv_halo_specs(tm, tc, nl, lead):
    hb = tm // GRID_W

    def prev(j, i):
        return (0, 1, jnp.maximum(jnp.minimum(i, nl - 1) * hb - 1, 0), j)

    def nxt(j, i):
        return (0, 1, jnp.minimum((jnp.minimum(i, nl - 1) + 1) * hb, nl * hb - 1), j)

    return pl.BlockSpec((lead, 1, GRID_W, tc), prev), pl.BlockSpec((lead, 1, GRID_W, tc), nxt)


def _conv_fwd(p42, cw, nl, tm, tc):
    _, _, r, half = p42.shape
    nt = r // tm

    def body(p_ref, hp_ref, hn_ref, cw_ref, o_ref):
        i = pl.program_id(1)
        is_ctx = i >= nl
        row = lax.broadcasted_iota(jnp.int32, (tm, tc), 0)
        rl = jnp.where(is_ctx, tm, GRID_W)
        pos = jnp.bitwise_and(row, rl - 1)

        def gate(hv, yc):
            bg = p_ref[0, hv].astype(F32)
            z = p_ref[3, hv].astype(F32)
            return (bg * yc * (z * _sigmoid(z))).astype(BF)

        u_h = p_ref[1, 0].astype(F32) * p_ref[2, 0].astype(F32)
        w_h = cw_ref[:, 0, :]
        o_ref[0] = gate(0, _conv_taps(u_h, w_h[0:1], w_h[1:2], w_h[2:3], pos, rl, tm)[0])
        u_v = p_ref[1, 1].astype(F32) * p_ref[2, 1].astype(F32)
        w_v = cw_ref[:, 1, :]

        @pl.when(is_ctx)
        def _():
            o_ref[1] = gate(1, _conv_taps(u_v, w_v[0:1], w_v[1:2], w_v[2:3], pos, rl, tm)[0])

        @pl.when(jnp.logical_not(is_ctx))
        def _():
            up = hp_ref[1, 0].astype(F32) * hp_ref[2, 0].astype(F32) * (i > 0).astype(F32)
            dn = hn_ref[1, 0].astype(F32) * hn_ref[2, 0].astype(F32) * (i < nl - 1).astype(F32)
            ext = jnp.concatenate([up, u_v, dn], axis=0)
            yc = w_v[0:1] * ext[0:tm] + w_v[1:2] * u_v + w_v[2:3] * ext[2 * GRID_W:tm + 2 * GRID_W]
            o_ref[1] = gate(1, yc)

    hp, hn = _conv_halo_specs(tm, tc, nl, 4)
    return pl.pallas_call(
        body, name="l0_conv_fwd", grid=(half // tc, nt),
        in_specs=[pl.BlockSpec((4, 2, tm, tc), lambda j, i: (0, 0, i, j)), hp, hn,
                  pl.BlockSpec((3, 2, tc), lambda j, i: (0, 0, j))],
        out_specs=pl.BlockSpec((2, tm, tc), lambda j, i: (0, i, j)),
        out_shape=jax.ShapeDtypeStruct((2, r, half), BF), compiler_params=_cparams(),
    )(p42, p42, p42, cw)


def _outproj_ln0(q3, w_out, x, ctx, gt2, tm):
    l, d = x.shape
    lc = ctx.shape[0]
    nl, nc = l // tm, lc // tm
    _, r, half = q3.shape
    tjo = tm // CHUNK

    def body(q_ref, w_hbm, x_ref, c_ref, g_ref, xl_ref, xc_ref, rl_ref, rc_ref, fx_ref, w_ref, xs_ref, rs_ref):
        i = pl.program_id(0)

        @pl.when(i == 0)
        def _():
            pltpu.sync_copy(w_hbm, w_ref)

        is_ctx = i >= nl
        fx = _dot(q_ref[0], w_ref[:half, :]) + _dot(q_ref[1], w_ref[half:, :])
        xv = jnp.where(is_ctx, c_ref[...], x_ref[...])
        rr = DN_ALPHA * xv + _sel_row(g_ref, is_ctx) * fx
        mu = jnp.mean(rr, axis=-1, keepdims=True)
        cen = rr - mu
        rstd = lax.rsqrt(jnp.mean(cen * cen, axis=-1, keepdims=True) + LN_EPS)
        xh = cen * rstd
        for lb in range(d // 128):
            xs_ref[lb] = xh[:, lb * 128:(lb + 1) * 128]
        rs_ref[...] = jnp.broadcast_to(rstd, (tm, 128))
        fx_ref[...] = fx.astype(BF)

        def to_cr(xo_ref, ro_ref):
            for s in range(CHUNK):
                for lb in range(d // 128):
                    xo_ref[:, s * d + lb * 128:s * d + (lb + 1) * 128] = xs_ref.at[lb][pl.ds(s, tjo, stride=CHUNK), :]
                ro_ref[:, s * 128:(s + 1) * 128] = rs_ref[pl.ds(s, tjo, stride=CHUNK), :]

        @pl.when(jnp.logical_not(is_ctx))
        def _():
            to_cr(xl_ref, rl_ref)

        @pl.when(is_ctx)
        def _():
            to_cr(xc_ref, rc_ref)

    lat, cx = _lat_or_ctx_specs(tm, d, nl, 1, 0)
    lat_o = lambda w_: pl.BlockSpec((tjo, CHUNK * w_), lambda i: (jnp.minimum(i, nl - 1), 0))
    ctx_o = lambda w_: pl.BlockSpec((tjo, CHUNK * w_), lambda i: (jnp.maximum(i - nl, 0), 0))
    return pl.pallas_call(
        body, name="l0_outproj_ln", grid=(nl + nc,),
        in_specs=[pl.BlockSpec((2, tm, half), lambda i: (0, i, 0)), ANY, lat, cx, _full((2, d))],
        out_specs=[lat_o(d), ctx_o(d), lat_o(128), ctx_o(128), pl.BlockSpec((tm, d), lambda i: (i, 0))],
        out_shape=[jax.ShapeDtypeStruct((l // CHUNK, CHUNK * d), F32), jax.ShapeDtypeStruct((lc // CHUNK, CHUNK * d), F32),
                   jax.ShapeDtypeStruct((l // CHUNK, CHUNK * 128), F32), jax.ShapeDtypeStruct((lc // CHUNK, CHUNK * 128), F32),
                   jax.ShapeDtypeStruct((r, d), BF)],
        scratch_shapes=[pltpu.VMEM(w_out.shape, BF), pltpu.VMEM((d // 128, tm, 128), F32), pltpu.VMEM((tm, 128), F32)],
        compiler_params=_cparams(),
    )(q3, w_out, x, ctx, gt2)


def _bwd_outproj0(dr_l, dr_c, gt2, w_out, fx, tm):
    l, d = dr_l.shape
    nl, nc = l // tm, dr_c.shape[0] // tm
    e = w_out.shape[0]
    half = e // 2
    r = l + dr_c.shape[0]

    def body(dl_ref, dc_ref, g_ref, w_hbm, fx_ref, dq_ref, acc_ref, w_ref):
        i = pl.program_id(0)

        @pl.when(i == 0)
        def _():
            pltpu.sync_copy(w_hbm, w_ref)
            acc_ref[...] = jnp.zeros_like(acc_ref)

        is_ctx = i >= nl
        dr = jnp.where(is_ctx, dc_ref[...], dl_ref[...]).astype(F32)
        dfx = (dr * _sel_row(g_ref, is_ctx)).astype(BF)
        dq_ref[0] = _dot_nt(dfx, w_ref[:half, :]).astype(BF)
        dq_ref[1] = _dot_nt(dfx, w_ref[half:, :]).astype(BF)
        s = jnp.sum(dr * fx_ref[...].astype(F32), axis=0, keepdims=True)
        sel = is_ctx.astype(F32)
        acc_ref[0:1, :] += s * (1.0 - sel)
        acc_ref[1:2, :] += s * sel

    lat, cx = _lat_or_ctx_specs(tm, d, nl, 1, 0)
    return pl.pallas_call(
        body, name="l0_bwd_outproj", grid=(nl + nc,),
        in_specs=[lat, cx, _full((2, d)), ANY, pl.BlockSpec((tm, d), lambda i: (i, 0))],
        out_specs=[pl.BlockSpec((2, tm, half), lambda i: (0, i, 0)), _full((8, d))],
        out_shape=[jax.ShapeDtypeStruct((2, r, half), BF), jax.ShapeDtypeStruct((8, d), F32)],
        scratch_shapes=[pltpu.VMEM(w_out.shape, BF)], compiler_params=_cparams(),
    )(dr_l, dr_c, gt2, w_out, fx)


def _conv_bwd(dq3, p42, cw, nl, tm, tc, xch=None):
    _, _, r, half = p42.shape
    nt = r // tm

    def body(dq_ref, dqp_ref, dqn_ref, p_ref, hp_ref, hn_ref, cw_ref, dp_ref, dw_ref):
        i = pl.program_id(1)
        is_ctx = i >= nl

        @pl.when(i == 0)
        def _():
            dw_ref[...] = jnp.zeros_like(dw_ref)

        row = lax.broadcasted_iota(jnp.int32, (tm, tc), 0)
        rl = jnp.where(is_ctx, tm, GRID_W)
        pos = jnp.bitwise_and(row, rl - 1)

        def pieces(dq, bg, z):
            sz = _sigmoid(z)
            sil = z * sz
            return dq * bg * sil, dq * sil, dq * bg * (sz * (1.0 + z * (1.0 - sz)))

        def seq_half(hv):
            bg, cg = p_ref[0, hv].astype(F32), p_ref[1, hv].astype(F32)
            v, z = p_ref[2, hv].astype(F32), p_ref[3, hv].astype(F32)
            w = cw_ref[:, hv, :]
            u = cg * v
            yc, u_up, u_dn = _conv_taps(u, w[0:1], w[1:2], w[2:3], pos, rl, tm)
            dyc, dbg_f, dz_f = pieces(dq_ref[hv].astype(F32), bg, z)
            du = _conv_taps(dyc, w[2:3], w[1:2], w[0:1], pos, rl, tm)[0]
            dp_ref[0, hv] = (dbg_f * yc).astype(BF)
            dp_ref[1, hv] = (du * v).astype(BF)
            dp_ref[2, hv] = (du * cg).astype(BF)
            dp_ref[3, hv] = (dz_f * yc).astype(BF)
            dw_ref[0:1, hv, :] += jnp.sum(dyc * u_up, axis=0, keepdims=True)
            dw_ref[1:2, hv, :] += jnp.sum(dyc * u, axis=0, keepdims=True)
            dw_ref[2:3, hv, :] += jnp.sum(dyc * u_dn, axis=0, keepdims=True)

        seq_half(0)

        @pl.when(is_ctx)
        def _():
            seq_half(1)

        @pl.when(jnp.logical_not(is_ctx))
        def _():
            bg, cg = p_ref[0, 1].astype(F32), p_ref[1, 1].astype(F32)
            v, z = p_ref[2, 1].astype(F32), p_ref[3, 1].astype(F32)
            w = cw_ref[:, 1, :]
            u = cg * v
            m_up = (i > 0).astype(F32)
            m_dn = (i < nl - 1).astype(F32)

            def halo(h_ref, dqh_ref, msk):
                hb, hc = h_ref[0, 0].astype(F32), h_ref[1, 0].astype(F32)
                hv_, hz = h_ref[2, 0].astype(F32), h_ref[3, 0].astype(F32)
                return hc * hv_ * msk, pieces(dqh_ref[0].astype(F32), hb, hz)[0] * msk

            u_p, dyc_p = halo(hp_ref, dqp_ref, m_up)
            u_n, dyc_n = halo(hn_ref, dqn_ref, m_dn)
            u_ext = jnp.concatenate([u_p, u, u_n], axis=0)
            u_up, u_dn = u_ext[0:tm], u_ext[2 * GRID_W:tm + 2 * GRID_W]
            yc = w[0:1] * u_up + w[1:2] * u + w[2:3] * u_dn
            dyc, dbg_f, dz_f = pieces(dq_ref[1].astype(F32), bg, z)
            d_ext = jnp.concatenate([dyc_p, dyc, dyc_n], axis=0)
            du = w[0:1] * d_ext[2 * GRID_W:tm + 2 * GRID_W] + w[1:2] * dyc + w[2:3] * d_ext[0:tm]
            dp_ref[0, 1] = (dbg_f * yc).astype(BF)
            dp_ref[1, 1] = (du * v).astype(BF)
            dp_ref[2, 1] = (du * cg).astype(BF)
            dp_ref[3, 1] = (dz_f * yc).astype(BF)
            dw_ref[0:1, 1, :] += jnp.sum(dyc * u_up, axis=0, keepdims=True)
            dw_ref[1:2, 1, :] += jnp.sum(dyc * u, axis=0, keepdims=True)
            dw_ref[2:3, 1, :] += jnp.sum(dyc * u_dn, axis=0, keepdims=True)

    hb = tm // GRID_W

    def dq_prev(j, i):
        return (1, jnp.maximum(jnp.minimum(i, nl - 1) * hb - 1, 0), j)

    def dq_next(j, i):
        return (1, jnp.minimum((jnp.minimum(i, nl - 1) + 1) * hb, nl * hb - 1), j)

    hp, hn = _conv_halo_specs(tm, tc, nl, 4)
    (dp42, dcw), extra = _hosted_call(
        body, xch, grid=(half // tc, nt),
        in_specs=[pl.BlockSpec((2, tm, tc), lambda j, i: (0, i, j)),
                  pl.BlockSpec((1, GRID_W, tc), dq_prev), pl.BlockSpec((1, GRID_W, tc), dq_next),
                  pl.BlockSpec((4, 2, tm, tc), lambda j, i: (0, 0, i, j)), hp, hn,
                  pl.BlockSpec((3, 2, tc), lambda j, i: (0, 0, j))],
        out_specs=[pl.BlockSpec((4, 2, tm, tc), lambda j, i: (0, 0, i, j)), pl.BlockSpec((8, 2, tc), lambda j, i: (0, 0, j))],
        out_shape=[jax.ShapeDtypeStruct(p42.shape, BF), jax.ShapeDtypeStruct((8, 2, half), F32)],
        scratch=[], args=(dq3, dq3, dq3, p42, p42, p42, cw), name="l0_conv_bwd")
    return dp42, dcw, extra


def _bwd_inproj0(dp42, w_in, x, ctx, dr_l, dr_c, a2, tm, xch=None):
    l, d = x.shape
    nl, nc = l // tm, ctx.shape[0] // tm
    e = w_in.shape[1] // 4
    half = e // 2

    def body(dp_ref, w_hbm, x_ref, c_ref, dl_ref, dc_ref, a_ref, gx_ref, acc_ref, w_ref):
        i = pl.program_id(0)

        @pl.when(i == 0)
        def _():
            pltpu.sync_copy(w_hbm, w_ref)
            acc_ref[...] = jnp.zeros_like(acc_ref)

        is_ctx = i >= nl
        dh = jnp.zeros((tm, d), F32)
        for k in range(4):
            for hv in range(2):
                c0 = k * e + hv * half
                dh = dh + _dot_nt(dp_ref[k, hv], w_ref[:, c0:c0 + half])
        xv = jnp.where(is_ctx, c_ref[...], x_ref[...])
        s_sc = jnp.sum(dh * xv, axis=0, keepdims=True)
        s_sh = jnp.sum(dh, axis=0, keepdims=True)
        sel = is_ctx.astype(F32)
        acc_ref[0:1, :] += s_sc * (1.0 - sel)
        acc_ref[1:2, :] += s_sc * sel
        acc_ref[2:3, :] += s_sh * (1.0 - sel)
        acc_ref[3:4, :] += s_sh * sel

        @pl.when(jnp.logical_not(is_ctx))
        def _():
            gx_ref[...] = DN_ALPHA * dl_ref[...].astype(F32) + dh * a_ref[0:1, :]

    lat, cx = _lat_or_ctx_specs(tm, d, nl, 1, 0)
    (gx, acc), extra = _hosted_call(
        body, xch, grid=(nl + nc,),
        in_specs=[pl.BlockSpec((4, 2, tm, half), lambda i: (0, 0, i, 0)), ANY, lat, cx, lat, cx, _full((2, d))],
        out_specs=[pl.BlockSpec((tm, d), lambda i: (jnp.minimum(i, nl - 1), 0)), _full((8, d))],
        out_shape=[jax.ShapeDtypeStruct((l, d), F32), jax.ShapeDtypeStruct((8, d), F32)],
        scratch=[pltpu.VMEM(w_in.shape, BF)], args=(dp42, w_in, x, ctx, dr_l, dr_c, a2), name="l0_bwd_inproj")
    return gx, acc, extra


def _dw_inproj0(x, ctx, a2, b2, dp42, tm):
    l, d = x.shape
    lc = ctx.shape[0]
    assert lc == tm
    tl = 4 * tm if l % (4 * tm) == 0 else tm
    nl = l // tl
    half = dp42.shape[-1]
    e = 2 * half

    def body(x_ref, c_ref, a_ref, b_ref, dpl_ref, dpc_ref, o_ref, acc_ref):
        i = pl.program_id(1)

        @pl.when(i == 0)
        def _():
            acc_ref[...] = jnp.zeros_like(acc_ref)

        def add(rows_ref, dp_ref, sel):
            h = (rows_ref[...] * a_ref[sel:sel + 1, :] + b_ref[sel:sel + 1, :]).astype(BF)
            acc_ref[:, :half] += _dot_tn(h, dp_ref[0, 0])
            acc_ref[:, half:] += _dot_tn(h, dp_ref[0, 1])

        @pl.when(i < nl)
        def _():
            add(x_ref, dpl_ref, 0)

        @pl.when(i == nl)
        def _():
            add(c_ref, dpc_ref, 1)
            o_ref[...] = acc_ref[...].astype(BF)

    return pl.pallas_call(
        body, name="l0_dw_inproj", grid=(4, nl + 1),
        in_specs=[pl.BlockSpec((tl, d), lambda k, i: (jnp.minimum(i, nl - 1), 0)), _full((lc, d)),
                  _full((2, d)), _full((2, d)),
                  pl.BlockSpec((1, 2, tl, half), lambda k, i: (k, 0, jnp.minimum(i, nl - 1), 0)),
                  pl.BlockSpec((1, 2, lc, half), lambda k, i: (k, 0, l // lc, 0))],
        out_specs=pl.BlockSpec((d, e), lambda k, i: (0, k)),
        out_shape=jax.ShapeDtypeStruct((d, 4 * e), BF),
        scratch_shapes=[pltpu.VMEM((d, e), F32)], compiler_params=_cparams(),
    )(x, ctx, a2, b2, dp42, dp42)


def _dw_outproj0(q3, dr_l, dr_c, gt2, tm):
    l, d = dr_l.shape
    nl, nc = l // tm, dr_c.shape[0] // tm
    _, r, half = q3.shape
    nt = nl + nc

    def body(q_ref, dl_ref, dc_ref, g_ref, o_ref, acc_ref):
        i = pl.program_id(0)
        is_ctx = i >= nl

        @pl.when(i == 0)
        def _():
            acc_ref[...] = jnp.zeros_like(acc_ref)

        dr = jnp.where(is_ctx, dc_ref[...], dl_ref[...]).astype(F32)
        dfx = (dr * _sel_row(g_ref, is_ctx)).astype(BF)
        acc_ref[:half, :] += _dot_tn(q_ref[0], dfx)
        acc_ref[half:, :] += _dot_tn(q_ref[1], dfx)

        @pl.when(i == nt - 1)
        def _():
            o_ref[...] = acc_ref[...].astype(BF)

    lat, cx = _lat_or_ctx_specs(tm, d, nl, 1, 0)
    return pl.pallas_call(
        body, name="l0_dw_outproj", grid=(nt,),
        in_specs=[pl.BlockSpec((2, tm, half), lambda i: (0, i, 0)), lat, cx, _full((2, d))],
        out_specs=_full((2 * half, d)), out_shape=jax.ShapeDtypeStruct((2 * half, d), BF),
        scratch_shapes=[pltpu.VMEM((2 * half, d), F32)], compiler_params=_cparams(),
    )(q3, dr_l, dr_c, gt2)


def _cr_tile(j, cap=256):
    for cand in (1024, 512, 256, 128, 64, 32, 16, 8):
        if cand <= cap and j % cand == 0:
            return cand
    raise ValueError(j)


def _inproj1(xh_cr, a1, b1, w, tag):
    j, d16 = xh_cr.shape
    d = d16 // CHUNK
    e = w.shape[1] // 2
    nb = e // LANE_BLOCK
    tj = _cr_tile(j)

    def body(x_ref, a_ref, b_ref, w_hbm, u_ref, z_ref, w_ref):
        @pl.when(jnp.logical_and(pl.program_id(0) == 0, pl.program_id(1) == 0))
        def _():
            pltpu.sync_copy(w_hbm, w_ref)

        h = (x_ref[...] * a_ref[...] + b_ref[...]).astype(BF)
        r = _dot(h, w_ref[...])
        for b in range(nb):
            u_ref[b] = r[:, b * LANE_BLOCK:(b + 1) * LANE_BLOCK].astype(BF)
        z_ref[...] = r[:, e:].astype(BF)

    return pl.pallas_call(
        body, name="l1_inproj_" + tag, grid=(j // tj, CHUNK),
        in_specs=[pl.BlockSpec((tj, d), lambda t, s: (t, s)), _full((1, d)), _full((1, d)), ANY],
        out_specs=[pl.BlockSpec((nb, tj, LANE_BLOCK), lambda t, s: (0, t, s)), pl.BlockSpec((tj, e), lambda t, s: (t, s))],
        out_shape=[jax.ShapeDtypeStruct((nb, j, BCR_W), BF), jax.ShapeDtypeStruct((j, CHUNK * e), BF)],
        scratch_shapes=[pltpu.VMEM(w.shape, BF)], compiler_params=_cparams(),
    )(xh_cr, a1, b1, w)


def _bmm(a_list, w_list, trans, out_dtype, name, ctx=None):
    nb, j, ka = a_list[0].shape
    n_out = w_list[0].shape[1] if trans[0] else w_list[0].shape[2]
    tn = n_out // 2
    tj = _cr_tile(j, 512)
    n = len(a_list)
    c_idx = [i for i in range(n) if ctx is not None and ctx[i] is not None]
    c_list = [ctx[i] for i in c_idx]
    nc = len(c_list)

    def body(*refs):
        w_refs = refs[n:2 * n]

        def product(a_refs, idx):
            acc = None
            for a_ref, i in zip(a_refs, idx):
                a = a_ref[0].astype(BF)
                t = _dot_nt(a, w_refs[i][0]) if trans[i] else _dot(a, w_refs[i][0])
                acc = t if acc is None else acc + t
            return acc.astype(out_dtype)

        refs[2 * n + nc][0] = product(refs[:n], range(n))
        if nc:
            @pl.when(pl.program_id(2) == 0)
            def _():
                refs[2 * n + nc + 1][0] = product(refs[2 * n:2 * n + nc], c_idx)

    a_specs = [pl.BlockSpec((1, tj, a.shape[2]), lambda b, h, t: (b, t, 0)) for a in a_list]
    w_specs = [pl.BlockSpec((1, tn, w.shape[2]), lambda b, h, t: (b, h, 0)) if tr
               else pl.BlockSpec((1, w.shape[1], tn), lambda b, h, t: (b, 0, h)) for w, tr in zip(w_list, trans)]
    c_specs = [pl.BlockSpec((1, a.shape[1], a.shape[2]), lambda b, h, t: (b, 0, 0)) for a in c_list]
    out_specs = [pl.BlockSpec((1, tj, tn), lambda b, h, t: (b, t, h))]
    out_shape = [jax.ShapeDtypeStruct((nb, j, n_out), out_dtype)]
    if nc:
        jc = c_list[0].shape[1]
        out_specs.append(pl.BlockSpec((1, jc, tn), lambda b, h, t: (b, 0, h)))
        out_shape.append(jax.ShapeDtypeStruct((nb, jc, n_out), out_dtype))
    res = pl.pallas_call(
        body, name=name, grid=(nb, 2, j // tj), in_specs=a_specs + w_specs + c_specs,
        out_specs=out_specs, out_shape=out_shape, compiler_params=_cparams(),
    )(*a_list, *w_list, *c_list)
    return res if nc else res[0]


def _group_mask(lane_groups):
    row = lax.broadcasted_iota(jnp.int32, (LANE_BLOCK, LANE_BLOCK), 0) // S5_P
    lane = lax.broadcasted_iota(jnp.int32, (LANE_BLOCK, LANE_BLOCK), 1)
    return row == lane_groups(lane)


def _expand_toeplitz(wcomp):
    nb = wcomp.shape[0]
    nd = 2 * CHUNK - 1

    def body(c_ref, o_ref):
        mask = _group_mask(lambda lane: lane // S5_P)
        tiles = []
        for dd in range(nd):
            m = c_ref[0, dd]
            tiles.append(jnp.where(mask, jnp.concatenate([m] * GROUPS_PER_BLOCK, axis=1), 0.0).astype(BF))
        for s in range(CHUNK):
            for t in range(CHUNK):
                o_ref[0, s * LANE_BLOCK:(s + 1) * LANE_BLOCK, t * LANE_BLOCK:(t + 1) * LANE_BLOCK] = tiles[t - s + CHUNK - 1]

    return pl.pallas_call(
        body, name="l1_expand_toeplitz", grid=(nb,),
        in_specs=[pl.BlockSpec((1, nd, LANE_BLOCK, S5_P), lambda b: (b, 0, 0, 0))],
        out_specs=pl.BlockSpec((1, BCR_W, BCR_W), lambda b: (b, 0, 0)),
        out_shape=jax.ShapeDtypeStruct((nb, BCR_W, BCR_W), BF), compiler_params=_cparams(),
    )(wcomp)


def _expand_blocks(comp, name):
    nb = comp.shape[2]
    lanes_per_dir = ZL_W // 2

    def body(c_ref, o_ref):
        masks = [_group_mask(lambda lane, lb=lb: 2 * lb + lane // S5_N) for lb in range(4)]
        for r in range(2):
            for s in range(CHUNK):
                for ri in range(2):
                    m = c_ref[r, s, 0, :, ri * S5_N:(ri + 1) * S5_N]
                    mm = jnp.concatenate([m, m], axis=1)
                    for lb in range(4):
                        c0 = r * lanes_per_dir + ri * ZH + lb * LANE_BLOCK
                        o_ref[0, s * LANE_BLOCK:(s + 1) * LANE_BLOCK, c0:c0 + LANE_BLOCK] = (
                            jnp.where(masks[lb], mm, 0.0).astype(BF))

    return pl.pallas_call(
        body, name=name, grid=(nb,),
        in_specs=[pl.BlockSpec((2, CHUNK, 1, LANE_BLOCK, LANE_BLOCK), lambda b: (0, 0, b, 0, 0))],
        out_specs=pl.BlockSpec((1, BCR_W, ZL_W), lambda b: (b, 0, 0)),
        out_shape=jax.ShapeDtypeStruct((nb, BCR_W, ZL_W), BF), compiler_params=_cparams(),
    )(comp)


def _bdw(a, b_, kind, ctx, name):
    nb, j, ka = a.shape
    kb = b_.shape[2]
    tn = kb // 2
    tj = _cr_tile(j, 1024)
    nt = j // tj
    has_ctx = ctx is not None
    nd = 2 * CHUNK - 1

    def body(*refs):
        a_ref, b_ref = refs[0], refs[1]
        o_ref, acc_ref = refs[2 + 2 * has_ctx], refs[3 + 2 * has_ctx]
        h, t = pl.program_id(1), pl.program_id(2)

        @pl.when(t == 0)
        def _():
            if has_ctx:
                acc_ref[...] = _dot_tn(refs[2][0].astype(BF), refs[3][0].astype(BF))
            else:
                acc_ref[...] = jnp.zeros_like(acc_ref)

        acc_ref[...] += _dot_tn(a_ref[0].astype(BF), b_ref[0].astype(BF))

        if kind == "toeplitz":
            diag_ref = refs[4 + 2 * has_ctx]

            @pl.when(jnp.logical_and(t == 0, h == 0))
            def _():
                diag_ref[...] = jnp.zeros_like(diag_ref)

            @pl.when(t == nt - 1)
            def _():
                for s in range(CHUNK):
                    for tl in range(CHUNK // 2):
                        dd = h * (CHUNK // 2) + (tl - s + CHUNK - 1)
                        diag_ref[dd] += acc_ref[s * LANE_BLOCK:(s + 1) * LANE_BLOCK, tl * LANE_BLOCK:(tl + 1) * LANE_BLOCK]

            @pl.when(jnp.logical_and(t == nt - 1, h == 1))
            def _():
                mask = _group_mask(lambda lane: lane // S5_P)
                for dd in range(nd):
                    v = jnp.where(mask, diag_ref[dd], 0.0)
                    acc = v[:, :S5_P]
                    for k in range(1, GROUPS_PER_BLOCK):
                        acc = acc + v[:, k * S5_P:(k + 1) * S5_P]
                    o_ref[0, dd] = acc
        else:
            @pl.when(t == nt - 1)
            def _():
                masks = [_group_mask(lambda lane, lb=lb: 2 * lb + lane // S5_N) for lb in range(4)]
                for s in range(CHUNK):
                    for ri in range(2):
                        v = None
                        for lb in range(4):
                            c0 = ri * ZH + lb * LANE_BLOCK
                            blk = acc_ref[s * LANE_BLOCK:(s + 1) * LANE_BLOCK, c0:c0 + LANE_BLOCK]
                            blk = jnp.where(masks[lb], blk, 0.0)
                            v = blk if v is None else v + blk
                        o_ref[0, s, 0, :, ri * S5_N:(ri + 1) * S5_N] = v[:, :S5_N] + v[:, S5_N:]

    in_specs = [pl.BlockSpec((1, tj, ka), lambda b, h, t: (b, t, 0)), pl.BlockSpec((1, tj, tn), lambda b, h, t: (b, t, h))]
    args = [a, b_]
    if has_ctx:
        jc = ctx[0].shape[1]
        in_specs += [pl.BlockSpec((1, jc, ka), lambda b, h, t: (b, 0, 0)), pl.BlockSpec((1, jc, tn), lambda b, h, t: (b, 0, h))]
        args += list(ctx)
    scratch = [pltpu.VMEM((ka, tn), F32)]
    if kind == "toeplitz":
        ospec = pl.BlockSpec((1, nd, LANE_BLOCK, S5_P), lambda b, h, t: (b, 0, 0, 0))
        oshape = jax.ShapeDtypeStruct((nb, nd, LANE_BLOCK, S5_P), F32)
        scratch.append(pltpu.VMEM((nd, LANE_BLOCK, LANE_BLOCK), F32))
    else:
        ospec = pl.BlockSpec((1, CHUNK, 1, LANE_BLOCK, LANE_BLOCK), lambda b, h, t: (h, 0, b, 0, 0))
        oshape = jax.ShapeDtypeStruct((2, CHUNK, nb, LANE_BLOCK, LANE_BLOCK), F32)
    return pl.pallas_call(
        body, name=name, grid=(nb, 2, nt), in_specs=in_specs, out_specs=ospec, out_shape=oshape,
        scratch_shapes=scratch, compiler_params=_cparams(),
    )(*args)


def _scan(z_l, z_c, coef, chains, conj, s_l=None, s_c=None, name="l1_scan"):
    nb, jl, _ = z_l.shape
    jc = z_c.shape[1]
    with_da = s_l is not None
    sign = -1.0 if conj else 1.0
    hw = 2 * ZH

    def body(*refs):
        zl_ref, zc_ref, cf_ref = refs[:3]
        k = 3
        if with_da:
            sl_ref, sc_ref = refs[3:5]
            k = 5
        ol_ref, oc_ref = refs[k:k + 2]
        d = pl.program_id(1)
        rowi = lax.broadcasted_iota(jnp.int32, (8, ZH), 0)

        def coef_rows(r0, nr):
            return cf_ref[0, 0, r0:r0 + nr, :ZH], sign * cf_ref[0, 0, r0:r0 + nr, ZH:]

        steps = [(1, coef_rows(0, 1)), (2, coef_rows(1, 1)), (4, coef_rows(2, 1))]

        def run(chain):
            carry = (jnp.zeros((1, ZH), F32), jnp.zeros((1, ZH), F32))
            da = (jnp.zeros((8, ZH), F32), jnp.zeros((8, ZH), F32))
            for which, rev in chain:
                src, dst = (zc_ref, oc_ref) if which == "c" else (zl_ref, ol_ref)
                sref = (sc_ref if which == "c" else sl_ref) if with_da else None
                ng = (jc if which == "c" else jl) // 8
                tr, ti = coef_rows(16, 8) if rev else coef_rows(8, 8)

                def step(it, st, src=src, dst=dst, sref=sref, ng=ng, tr=tr, ti=ti, rev=rev):
                    cr_, ci_, dar, dai = st
                    g = (ng - 1 - it) if rev else it
                    off = pl.multiple_of(g * 8, 8)
                    xr = src[0, pl.ds(off, 8), :ZH]
                    xi = src[0, pl.ds(off, 8), ZH:]
                    for sh, (ar, ai) in steps:
                        if rev:
                            keep = rowi < 8 - sh
                            sr = jnp.where(keep, pltpu.roll(xr, 8 - sh, 0), 0.0)
                            si = jnp.where(keep, pltpu.roll(xi, 8 - sh, 0), 0.0)
                        else:
                            keep = rowi >= sh
                            sr = jnp.where(keep, pltpu.roll(xr, sh, 0), 0.0)
                            si = jnp.where(keep, pltpu.roll(xi, sh, 0), 0.0)
                        xr, xi = xr + ar * sr - ai * si, xi + ar * si + ai * sr
                    ir = xr + tr * cr_ - ti * ci_
                    ii = xi + tr * ci_ + ti * cr_
                    if rev:
                        er = jnp.where(rowi == 7, cr_, pltpu.roll(ir, 7, 0))
                        ei = jnp.where(rowi == 7, ci_, pltpu.roll(ii, 7, 0))
                        ncr, nci = ir[0:1], ii[0:1]
                    else:
                        er = jnp.where(rowi == 0, cr_, pltpu.roll(ir, 1, 0))
                        ei = jnp.where(rowi == 0, ci_, pltpu.roll(ii, 1, 0))
                        ncr, nci = ir[7:8], ii[7:8]
                    dst[0, pl.ds(off, 8), :ZH] = er
                    dst[0, pl.ds(off, 8), ZH:] = ei
                    if sref is not None:
                        s_r = sref[0, pl.ds(off, 8), :ZH]
                        s_i = sref[0, pl.ds(off, 8), ZH:]
                        dar = dar + s_r * er + s_i * ei
                        dai = dai + s_r * ei - s_i * er
                    return ncr, nci, dar, dai

                carry_da = lax.fori_loop(0, ng, step, (*carry, *da))
                carry, da = carry_da[:2], carry_da[2:]
            if with_da:
                refs[k + 2][0, 0] = jnp.concatenate([da[0], da[1]], axis=1)

        for dd in range(2):
            @pl.when(d == dd)
            def _(dd=dd):
                run(chains[dd])

    zspec_l = pl.BlockSpec((1, jl, hw), lambda b, d: (b, 0, d))
    zspec_c = pl.BlockSpec((1, jc, hw), lambda b, d: (b, 0, d))
    in_specs = [zspec_l, zspec_c, pl.BlockSpec((1, 1, 24, hw), lambda b, d: (b, d, 0, 0))]
    args = [z_l, z_c, coef]
    out_specs = [zspec_l, zspec_c]
    out_shape = [jax.ShapeDtypeStruct(z_l.shape, F32), jax.ShapeDtypeStruct(z_c.shape, F32)]
    if with_da:
        in_specs += [zspec_l, zspec_c]
        args += [s_l, s_c]
        out_specs.append(pl.BlockSpec((1, 1, 8, hw), lambda b, d: (b, d, 0, 0)))
        out_shape.append(jax.ShapeDtypeStruct((nb, 2, 8, hw), F32))
    return pl.pallas_call(body, name=name, grid=(nb, 2), in_specs=in_specs, out_specs=out_specs,
                          out_shape=out_shape, compiler_params=_cparams())(*args)


def _glu_fwd(y_bcr, z_cr, w_glu, b_glu):
    nb, j, _ = y_bcr.shape
    e = nb * LANE_BLOCK
    tj = _cr_tile(j)

    def body(y_ref, z_ref, w_hbm, b_ref, o_ref, sg_ref, w_ref):
        @pl.when(jnp.logical_and(pl.program_id(0) == 0, pl.program_id(1) == 0))
        def _():
            pltpu.sync_copy(w_hbm, w_ref)

        y = jnp.concatenate([y_ref[b] for b in range(nb)], axis=1).astype(F32)
        g = _gelu_parts(y)[0]
        sg = _sigmoid(_dot(g.astype(BF), w_ref[...]) + b_ref[...])
        z = z_ref[...].astype(F32)
        o_ref[...] = (g * sg * (z * _sigmoid(z))).astype(BF)
        sg_ref[...] = sg.astype(BF)

    tok = pl.BlockSpec((tj, e), lambda t, s: (t, s))
    return pl.pallas_call(
        body, name="l1_glu_fwd", grid=(j // tj, CHUNK),
        in_specs=[pl.BlockSpec((nb, tj, LANE_BLOCK), lambda t, s: (0, t, s)), tok, ANY, _full((1, e))],
        out_specs=[tok, tok],
        out_shape=[jax.ShapeDtypeStruct((j, CHUNK * e), BF), jax.ShapeDtypeStruct((j, CHUNK * e), BF)],
        scratch_shapes=[pltpu.VMEM(w_glu.shape, BF)], compiler_params=_cparams(),
    )(y_bcr, z_cr, w_glu, b_glu)


def _final(w_cr, w_out, xh_cr, tgt_cr, vecs):
    j, e16 = w_cr.shape
    e = e16 // CHUNK
    d = w_out.shape[1]
    tj = _cr_tile(j)

    def body(w_ref, wo_hbm, xh_ref, t_ref, v_ref, dr_ref, acc_ref, wo_ref):
        @pl.when(jnp.logical_and(pl.program_id(0) == 0, pl.program_id(1) == 0))
        def _():
            pltpu.sync_copy(wo_hbm, wo_ref)
            acc_ref[...] = jnp.zeros_like(acc_ref)

        o = _dot(w_ref[...], wo_ref[...])
        x1 = xh_ref[...] * v_ref[0:1, :] + v_ref[1:2, :]
        rr = DN_ALPHA * x1 + v_ref[2:3, :] * o
        mu = jnp.mean(rr, axis=-1, keepdims=True)
        cen = rr - mu
        rstd = lax.rsqrt(jnp.mean(cen * cen, axis=-1, keepdims=True) + LN_EPS)
        xh2 = cen * rstd
        err = xh2 * v_ref[3:4, :] + v_ref[4:5, :] - t_ref[...]
        dy = err * (1.0 / d)
        dxh = dy * v_ref[3:4, :]
        dr = rstd * (dxh - jnp.mean(dxh, axis=-1, keepdims=True) - xh2 * jnp.mean(dxh * xh2, axis=-1, keepdims=True))
        dr_ref[...] = dr.astype(BF)
        acc_ref[0:1, :] += jnp.sum(dy * xh2, axis=0, keepdims=True)
        acc_ref[1:2, :] += jnp.sum(dy, axis=0, keepdims=True)
        acc_ref[2:3, :] += jnp.sum(dr * o, axis=0, keepdims=True)
        acc_ref[3:4, :] += (0.5 / d) * jnp.sum(err * err, axis=0, keepdims=True)

    tok_d = pl.BlockSpec((tj, d), lambda t, s: (t, s))
    return pl.pallas_call(
        body, name="l1_final", grid=(j // tj, CHUNK),
        in_specs=[pl.BlockSpec((tj, e), lambda t, s: (t, s)), ANY, tok_d, tok_d, _full((8, d))],
        out_specs=[tok_d, _full((8, d))],
        out_shape=[jax.ShapeDtypeStruct((j, CHUNK * d), BF), jax.ShapeDtypeStruct((8, d), F32)],
        scratch_shapes=[pltpu.VMEM(w_out.shape, BF)], compiler_params=_cparams(),
    )(w_cr, w_out, xh_cr, tgt_cr, vecs)


def _glu_bwd(dr_cr, gt1, w_out, w_glu, y_bcr, z_cr, sg_cr):
    nb, j, _ = y_bcr.shape
    e, d = w_out.shape
    tj = _cr_tile(j)

    def body(dr_ref, g_ref, wo_hbm, wg_hbm, y_ref, z_ref, sg_ref, dz_ref, dt_ref, dy_ref, wo_ref, wg_ref):
        @pl.when(jnp.logical_and(pl.program_id(0) == 0, pl.program_id(1) == 0))
        def _():
            pltpu.sync_copy(wo_hbm, wo_ref)
            pltpu.sync_copy(wg_hbm, wg_ref)

        do = (dr_ref[...].astype(F32) * g_ref[...]).astype(BF)
        dw = _dot_nt(do, wo_ref[...])
        y = jnp.concatenate([y_ref[b] for b in range(nb)], axis=1).astype(F32)
        g, dgel = _gelu_parts(y)
        z = z_ref[...].astype(F32)
        sz = _sigmoid(z)
        sg = sg_ref[...].astype(F32)
        dg2 = dw * (z * sz)
        dz_ref[...] = (dw * g * sg * (sz * (1.0 + z * (1.0 - sz)))).astype(BF)
        dt = (dg2 * g * sg * (1.0 - sg)).astype(BF)
        dt_ref[...] = dt
        dy = (dg2 * sg + _dot_nt(dt, wg_ref[...])) * dgel
        for b in range(nb):
            dy_ref[b] = dy[:, b * LANE_BLOCK:(b + 1) * LANE_BLOCK].astype(BF)

    tok_e = pl.BlockSpec((tj, e), lambda t, s: (t, s))
    blk = pl.BlockSpec((nb, tj, LANE_BLOCK), lambda t, s: (0, t, s))
    return pl.pallas_call(
        body, name="l1_glu_bwd", grid=(j // tj, CHUNK),
        in_specs=[pl.BlockSpec((tj, d), lambda t, s: (t, s)), _full((1, d)), ANY, ANY, blk, tok_e, tok_e],
        out_specs=[tok_e, tok_e, blk],
        out_shape=[jax.ShapeDtypeStruct((j, CHUNK * e), BF), jax.ShapeDtypeStruct((j, CHUNK * e), BF),
                   jax.ShapeDtypeStruct((nb, j, BCR_W), BF)],
        scratch_shapes=[pltpu.VMEM(w_out.shape, BF), pltpu.VMEM(w_glu.shape, BF)], compiler_params=_cparams(),
    )(dr_cr, gt1, w_out, w_glu, y_bcr, z_cr, sg_cr)


def _bwd_inproj1(du_bcr, dz_cr, w, xh_cr, rs_cr, dr2_cr, vecs, tag):
    nb, j, _ = du_bcr.shape
    d = w.shape[0]
    e = w.shape[1] // 2
    tj = _cr_tile(j)

    def body(du_ref, dz_ref, w_hbm, xh_ref, rs_ref, dr2_ref, v_ref, dr1_ref, acc_ref, w_ref):
        @pl.when(jnp.logical_and(pl.program_id(0) == 0, pl.program_id(1) == 0))
        def _():
            pltpu.sync_copy(w_hbm, w_ref)
            acc_ref[...] = jnp.zeros_like(acc_ref)

        du = jnp.concatenate([du_ref[b] for b in range(nb)], axis=1)
        dh = _dot_nt(du, w_ref[:, :e]) + _dot_nt(dz_ref[...], w_ref[:, e:])
        xh = xh_ref[...]
        x1 = xh * v_ref[0:1, :] + v_ref[1:2, :]
        dx1 = DN_ALPHA * dr2_ref[...].astype(F32) + dh * v_ref[2:3, :]
        dxh = dx1 * v_ref[0:1, :]
        rstd = rs_ref[:, 0:1]
        dr1 = rstd * (dxh - jnp.mean(dxh, axis=-1, keepdims=True) - xh * jnp.mean(dxh * xh, axis=-1, keepdims=True))
        dr1_ref[...] = dr1.astype(BF)
        acc_ref[0:1, :] += jnp.sum(dh * x1, axis=0, keepdims=True)
        acc_ref[1:2, :] += jnp.sum(dh, axis=0, keepdims=True)
        acc_ref[2:3, :] += jnp.sum(dx1 * xh, axis=0, keepdims=True)
        acc_ref[3:4, :] += jnp.sum(dx1, axis=0, keepdims=True)

    tok_d = pl.BlockSpec((tj, d), lambda t, s: (t, s))
    return pl.pallas_call(
        body, name="l1_bwd_inproj_" + tag, grid=(j // tj, CHUNK),
        in_specs=[pl.BlockSpec((nb, tj, LANE_BLOCK), lambda t, s: (0, t, s)), pl.BlockSpec((tj, e), lambda t, s: (t, s)),
                  ANY, tok_d, pl.BlockSpec((tj, 128), lambda t, s: (t, s)), tok_d, _full((8, d))],
        out_specs=[tok_d, _full((8, d))],
        out_shape=[jax.ShapeDtypeStruct((j, CHUNK * d), BF), jax.ShapeDtypeStruct((8, d), F32)],
        scratch_shapes=[pltpu.VMEM(w.shape, BF)], compiler_params=_cparams(),
    )(du_bcr, dz_cr, w, xh_cr, rs_cr, dr2_cr, vecs)


def _dw_cr(lhs, rhs, lhs_kind, rhs_kind, vec, bias_sum, init, name):
    if lhs_kind == "gelu_bcr":
        nb_l, j, _ = lhs.shape
        k = nb_l * LANE_BLOCK
    else:
        j = lhs.shape[0]
        k = lhs.shape[1] // CHUNK
    if rhs_kind == "bcr":
        nb_r = rhs.shape[0]
        n = nb_r * LANE_BLOCK
    else:
        n = rhs.shape[1] // CHUNK
    tj = _cr_tile(j, 512)
    nh = 2 if k * n * 4 > (8 << 20) else 1
    tn = n // nh
    nbh = tn // LANE_BLOCK
    nt = j // tj
    has_init = init is not None

    def body(*refs):
        refs = list(refs)
        l_ref, r_ref = refs[0], refs[1]
        pos = 2
        v_ref = None
        if vec is not None:
            v_ref = refs[pos]
            pos += 1
        i_ref = None
        if has_init:
            i_ref = refs[pos]
            pos += 1
        o_ref = refs[pos]
        pos += 1
        bs_ref = None
        if bias_sum:
            bs_ref = refs[pos]
            pos += 1
        acc_ref = refs[pos]
        t, s = pl.program_id(1), pl.program_id(2)
        first = jnp.logical_and(t == 0, s == 0)

        @pl.when(first)
        def _():
            acc_ref[...] = i_ref[...] if has_init else jnp.zeros_like(acc_ref)
            if bias_sum:
                bs_ref[...] = jnp.zeros_like(bs_ref)

        if lhs_kind == "gelu_bcr":
            y = jnp.concatenate([l_ref[b] for b in range(nb_l)], axis=1).astype(F32)
            lv = _gelu_parts(y)[0].astype(BF)
        elif lhs_kind == "mod":
            lv = (l_ref[...] * v_ref[0:1, :] + v_ref[1:2, :]).astype(BF)
        else:
            lv = l_ref[...]
        if rhs_kind == "bcr":
            rv = jnp.concatenate([r_ref[b] for b in range(nbh)], axis=1)
        elif rhs_kind == "scaled":
            rv = (r_ref[...].astype(F32) * v_ref[0:1, :]).astype(BF)
        else:
            rv = r_ref[...]
        acc_ref[...] += _dot_tn(lv, rv)
        if bias_sum:
            bs_ref[0:1, :] += jnp.sum(rv.astype(F32), axis=0, keepdims=True)

        @pl.when(jnp.logical_and(t == nt - 1, s == CHUNK - 1))
        def _():
            o_ref[...] = acc_ref[...].astype(BF)

    if lhs_kind == "gelu_bcr":
        l_spec = pl.BlockSpec((nb_l, tj, LANE_BLOCK), lambda h, t, s: (0, t, s))
    else:
        l_spec = pl.BlockSpec((tj, k), lambda h, t, s: (t, s))
    if rhs_kind == "bcr":
        r_spec = pl.BlockSpec((nbh, tj, LANE_BLOCK), lambda h, t, s: (h, t, s))
    else:
        r_spec = pl.BlockSpec((tj, tn), lambda h, t, s: (t, s * nh + h))
    in_specs, args = [l_spec, r_spec], [lhs, rhs]
    if vec is not None:
        in_specs.append(_full(vec.shape))
        args.append(vec)
    o_spec = pl.BlockSpec((k, tn), lambda h, t, s: (0, h))
    if has_init:
        in_specs.append(o_spec)
        args.append(init)
    out_specs, out_shape = [o_spec], [jax.ShapeDtypeStruct((k, n), BF)]
    if bias_sum:
        out_specs.append(pl.BlockSpec((8, tn), lambda h, t, s: (0, h)))
        out_shape.append(jax.ShapeDtypeStruct((8, n), F32))
    res = pl.pallas_call(
        body, name=name, grid=(nh, nt, CHUNK), in_specs=in_specs, out_specs=out_specs, out_shape=out_shape,
        scratch_shapes=[pltpu.VMEM((k, tn), F32)], compiler_params=_cparams(),
    )(*args)
    return res if bias_sum else res[0]


def _dw_cr_f32(lhs, rhs, vec, name):
    j = lhs.shape[0]
    k = lhs.shape[1] // CHUNK
    nb_r = rhs.shape[0]
    n = nb_r * LANE_BLOCK
    tj = _cr_tile(j)
    nt = j // tj

    def body(l_ref, r_ref, v_ref, o_ref):
        @pl.when(jnp.logical_and(pl.program_id(0) == 0, pl.program_id(1) == 0))
        def _():
            o_ref[...] = jnp.zeros_like(o_ref)

        lv = (l_ref[...] * v_ref[0:1, :] + v_ref[1:2, :]).astype(BF)
        rv = jnp.concatenate([r_ref[b] for b in range(nb_r)], axis=1)
        o_ref[...] += _dot_tn(lv, rv)

    return pl.pallas_call(
        body, name=name, grid=(nt, CHUNK),
        in_specs=[pl.BlockSpec((tj, k), lambda t, s: (t, s)), pl.BlockSpec((nb_r, tj, LANE_BLOCK), lambda t, s: (0, t, s)),
                  _full(vec.shape)],
        out_specs=_full((k, n)), out_shape=jax.ShapeDtypeStruct((k, n), F32), compiler_params=_cparams(),
    )(lhs, rhs, vec)


GT_ROWS = CHUNK * S5_P
ZG_W = 2 * 2 * S5_N
GROUPS_PER_STEP = 4


def _inproj1_gt(xh_cr, a1, b1, wu_t, w_z, tag):
    j, d16 = xh_cr.shape
    d = d16 // CHUNK
    e = wu_t.shape[0]
    g = e // S5_P
    tj = _cr_tile(j, 256)

    def body(x_ref, a_ref, b_ref, wu_hbm, wz_hbm, u_ref, z_ref, wu_ref, wz_ref):
        @pl.when(jnp.logical_and(pl.program_id(0) == 0, pl.program_id(1) == 0))
        def _():
            pltpu.sync_copy(wu_hbm, wu_ref)
            pltpu.sync_copy(wz_hbm, wz_ref)

        h = (x_ref[...] * a_ref[...] + b_ref[...]).astype(BF)
        u_ref[...] = _dot_nt(wu_ref[...], h).reshape(g, S5_P, tj).astype(BF)
        z_ref[...] = _dot(h, wz_ref[...]).astype(BF)

    return pl.pallas_call(
        body, name="l1_inproj_" + tag, grid=(j // tj, CHUNK),
        in_specs=[pl.BlockSpec((tj, d), lambda t, s: (t, s)), _full((1, d)), _full((1, d)), ANY, ANY],
        out_specs=[pl.BlockSpec((g, S5_P, tj), lambda t, s: (0, s, t)), pl.BlockSpec((tj, e), lambda t, s: (t, s))],
        out_shape=[jax.ShapeDtypeStruct((g, GT_ROWS, j), BF), jax.ShapeDtypeStruct((j, CHUNK * e), BF)],
        scratch_shapes=[pltpu.VMEM(wu_t.shape, BF), pltpu.VMEM(w_z.shape, BF)], compiler_params=_cparams(),
    )(xh_cr, a1, b1, wu_t, w_z)


def _gt_spec(j, gb=GROUPS_PER_STEP):
    return pl.BlockSpec((gb, GT_ROWS, j), lambda i: (i, 0, 0))


def _zg_spec(j, gb=GROUPS_PER_STEP):
    return pl.BlockSpec((j, gb * ZG_W), lambda i: (0, i))


def _w_spec(gb=GROUPS_PER_STEP):
    return pl.BlockSpec((gb, GT_ROWS, ZG_W), lambda i: (i, 0, 0))


def _s5_z(ut_l, ut_c, bc):
    g, _, jl = ut_l.shape
    jc = ut_c.shape[2]
    gb = GROUPS_PER_STEP

    def body(ul_ref, uc_ref, bc_ref, zl_ref, zc_ref):
        for k in range(gb):
            zl_ref[:, k * ZG_W:(k + 1) * ZG_W] = _dot_tn(ul_ref[k], bc_ref[k])
            zc_ref[:, k * ZG_W:(k + 1) * ZG_W] = _dot_tn(uc_ref[k], bc_ref[k])

    return pl.pallas_call(
        body, name="l1_s5_z", grid=(g // gb,), in_specs=[_gt_spec(jl), _gt_spec(jc), _w_spec()],
        out_specs=[_zg_spec(jl), _zg_spec(jc)],
        out_shape=[jax.ShapeDtypeStruct((jl, g * ZG_W), F32), jax.ShapeDtypeStruct((jc, g * ZG_W), F32)],
        compiler_params=_cparams(),
    )(ut_l, ut_c, bc)


def _s5_y(ut_l, s_l, mt_t, cct):
    g, _, jl = ut_l.shape
    gb = GROUPS_PER_STEP

    def body(u_ref, s_ref, mt_ref, cc_ref, y_ref):
        for k in range(gb):
            s_k = s_ref[:, k * ZG_W:(k + 1) * ZG_W].astype(BF)
            y_ref[k] = (_dot(mt_ref[k], u_ref[k]) + _dot_nt(cc_ref[k], s_k)).astype(BF)

    return pl.pallas_call(
        body, name="l1_s5_y", grid=(g // gb,), in_specs=[_gt_spec(jl), _zg_spec(jl), _w_spec(), _w_spec()],
        out_specs=_gt_spec(jl), out_shape=jax.ShapeDtypeStruct((g, GT_ROWS, jl), BF), compiler_params=_cparams(),
    )(ut_l, s_l, mt_t, cct)


def _s5_ds(dyt_l, cct):
    g, _, jl = dyt_l.shape
    gb = GROUPS_PER_STEP

    def body(dy_ref, cc_ref, ds_ref):
        for k in range(gb):
            ds_ref[:, k * ZG_W:(k + 1) * ZG_W] = _dot_tn(dy_ref[k], cc_ref[k])

    return pl.pallas_call(
        body, name="l1_s5_ds", grid=(g // gb,), in_specs=[_gt_spec(jl), _w_spec()], out_specs=_zg_spec(jl),
        out_shape=jax.ShapeDtypeStruct((jl, g * ZG_W), F32), compiler_params=_cparams(),
    )(dyt_l, cct)


def _s5_dx(dyt_l, dz_l, dz_c, mt, bc):
    g, _, jl = dyt_l.shape
    jc = dz_c.shape[0]
    gb = GROUPS_PER_STEP

    def body(dy_ref, dzl_ref, dzc_ref, mt_ref, bc_ref, dul_ref, duc_ref):
        for k in range(gb):
            dzl = dzl_ref[:, k * ZG_W:(k + 1) * ZG_W].astype(BF)
            dzc = dzc_ref[:, k * ZG_W:(k + 1) * ZG_W].astype(BF)
            dul_ref[k] = (_dot(mt_ref[k], dy_ref[k]) + _dot_nt(bc_ref[k], dzl)).astype(BF)
            duc_ref[k] = _dot_nt(bc_ref[k], dzc).astype(BF)

    return pl.pallas_call(
        body, name="l1_s5_dx", grid=(g // gb,),
        in_specs=[_gt_spec(jl), _zg_spec(jl), _zg_spec(jc), _w_spec(), _w_spec()],
        out_specs=[_gt_spec(jl), _gt_spec(jc)],
        out_shape=[jax.ShapeDtypeStruct((g, GT_ROWS, jl), BF), jax.ShapeDtypeStruct((g, GT_ROWS, jc), BF)],
        compiler_params=_cparams(),
    )(dyt_l, dz_l, dz_c, mt, bc)


def _s5_dw(ut_l, ut_c, dyt_l, dz_l, dz_c, s_l):
    g, _, jl = ut_l.shape
    jc = ut_c.shape[2]
    gb = GROUPS_PER_STEP

    def body(ul_ref, uc_ref, dy_ref, dzl_ref, dzc_ref, s_ref, dmt_ref, dbc_ref, dcc_ref):
        for k in range(gb):
            lanes = slice(k * ZG_W, (k + 1) * ZG_W)
            dmt_ref[k] = _dot_nt(ul_ref[k], dy_ref[k])
            dbc_ref[k] = (_dot(ul_ref[k], dzl_ref[:, lanes].astype(BF))
                          + _dot(uc_ref[k], dzc_ref[:, lanes].astype(BF)))
            dcc_ref[k] = _dot(dy_ref[k], s_ref[:, lanes].astype(BF))

    sds = jax.ShapeDtypeStruct((g, GT_ROWS, ZG_W), F32)
    return pl.pallas_call(
        body, name="l1_s5_dw", grid=(g // gb,),
        in_specs=[_gt_spec(jl), _gt_spec(jc), _gt_spec(jl), _zg_spec(jl), _zg_spec(jc), _zg_spec(jl)],
        out_specs=[_w_spec(), _w_spec(), _w_spec()], out_shape=[sds, sds, sds], compiler_params=_cparams(),
    )(ut_l, ut_c, dyt_l, dz_l, dz_c, s_l)


def _scan_g(z_l, z_c, coef, chains, conj, s_l=None, s_c=None, name="l1_scan"):
    jl, w_all = z_l.shape
    jc = z_c.shape[0]
    gb = GROUPS_PER_STEP
    wb = gb * ZG_W
    ncol = wb // 128
    with_da = s_l is not None
    sign = -1.0 if conj else 1.0

    def body(*refs):
        zl_ref, zc_ref, cf_ref = refs[:3]
        k0 = 3
        if with_da:
            sl_ref, sc_ref = refs[3:5]
            k0 = 5
        ol_ref, oc_ref = refs[k0:k0 + 2]
        rowi = lax.broadcasted_iota(jnp.int32, (8, 128), 0)

        def coefs(col, r0, nr):
            lanes = slice(col * 128, (col + 1) * 128)
            return cf_ref[0, r0:r0 + nr, lanes], sign * cf_ref[1, r0:r0 + nr, lanes]

        def cmul(cs, cw, v):
            return cs * v + cw * pltpu.roll(v, S5_N, 1)

        def shift(v, sh, rev):
            if rev:
                return jnp.where(rowi < 8 - sh, pltpu.roll(v, 8 - sh, 0), 0.0)
            return jnp.where(rowi >= sh, pltpu.roll(v, sh, 0), 0.0)

        zero_row = jnp.zeros((1, 128), F32)
        zero_tile = jnp.zeros((8, 128), F32)
        carry = [zero_row] * ncol
        da = [zero_tile] * (2 * ncol)
        n_seg = len(chains[0])
        for seg in range(n_seg):
            which = chains[0][seg][0]
            assert chains[1][seg][0] == which
            revs = (chains[0][seg][1], chains[1][seg][1])
            src, dst = (zc_ref, oc_ref) if which == "c" else (zl_ref, ol_ref)
            sref = ((sc_ref if which == "c" else sl_ref) if with_da else None)
            ng = (jc if which == "c" else jl) // 8

            def step(it, st, src=src, dst=dst, sref=sref, ng=ng, revs=revs):
                carry_, da_ = list(st[:ncol]), list(st[ncol:])
                for col in range(ncol):
                    rev = revs[col % 2]
                    lanes = slice(col * 128, (col + 1) * 128)
                    grp = (ng - 1 - it) if rev else it
                    off = pl.multiple_of(grp * 8, 8)
                    x = src[pl.ds(off, 8), lanes]
                    for sh, r0 in ((1, 0), (2, 1), (4, 2)):
                        cs, cw = coefs(col, r0, 1)
                        x = x + cmul(cs, cw, shift(x, sh, rev))
                    ts, tw = coefs(col, 16, 8) if rev else coefs(col, 8, 8)
                    cb = jnp.broadcast_to(carry_[col], (8, 128))
                    incl = x + cmul(ts, tw, cb)
                    if rev:
                        excl = jnp.where(rowi == 7, cb, pltpu.roll(incl, 7, 0))
                        carry_[col] = incl[0:1]
                    else:
                        excl = jnp.where(rowi == 0, cb, pltpu.roll(incl, 1, 0))
                        carry_[col] = incl[7:8]
                    dst[pl.ds(off, 8), lanes] = excl
                    if sref is not None:
                        sv = sref[pl.ds(off, 8), lanes]
                        da_[2 * col] = da_[2 * col] + sv * excl
                        da_[2 * col + 1] = da_[2 * col + 1] + pltpu.roll(sv, S5_N, 1) * excl
                return (*carry_, *da_)

            st = lax.fori_loop(0, ng, step, (*carry, *da))
            carry, da = list(st[:ncol]), list(st[ncol:])
        if with_da:
            da_ref = refs[k0 + 2]
            for col in range(ncol):
                da_ref[0, :, col * 128:(col + 1) * 128] = da[2 * col]
                da_ref[1, :, col * 128:(col + 1) * 128] = da[2 * col + 1]

    in_specs = [_zg_spec(jl), _zg_spec(jc), pl.BlockSpec((2, 24, wb), lambda i: (0, 0, i))]
    args = [z_l, z_c, coef]
    out_specs = [_zg_spec(jl), _zg_spec(jc)]
    out_shape = [jax.ShapeDtypeStruct(z_l.shape, F32), jax.ShapeDtypeStruct(z_c.shape, F32)]
    if with_da:
        in_specs += [_zg_spec(jl), _zg_spec(jc)]
        args += [s_l, s_c]
        out_specs.append(pl.BlockSpec((2, 8, wb), lambda i: (0, 0, i)))
        out_shape.append(jax.ShapeDtypeStruct((2, 8, w_all), F32))
    return pl.pallas_call(body, name=name, grid=(w_all // wb,), in_specs=in_specs, out_specs=out_specs,
                          out_shape=out_shape, compiler_params=_cparams())(*args)


def _gt_tok_spec(g, tj):
    return pl.BlockSpec((g, S5_P, tj), lambda t, s: (0, s, t))


def _glu_fwd_gt(yt, z_cr, w_glu, b_glu):
    g, _, j = yt.shape
    e = g * S5_P
    tj = _cr_tile(j)

    def body(y_ref, z_ref, w_hbm, b_ref, o_ref, sg_ref, w_ref):
        @pl.when(jnp.logical_and(pl.program_id(0) == 0, pl.program_id(1) == 0))
        def _():
            pltpu.sync_copy(w_hbm, w_ref)

        y = jnp.transpose(y_ref[...].reshape(e, tj).astype(F32))
        gl = _gelu_parts(y)[0]
        sg = _sigmoid(_dot(gl.astype(BF), w_ref[...]) + b_ref[...])
        z = z_ref[...].astype(F32)
        o_ref[...] = (gl * sg * (z * _sigmoid(z))).astype(BF)
        sg_ref[...] = sg.astype(BF)

    tok = pl.BlockSpec((tj, e), lambda t, s: (t, s))
    return pl.pallas_call(
        body, name="l1_glu_fwd", grid=(j // tj, CHUNK),
        in_specs=[_gt_tok_spec(g, tj), tok, ANY, _full((1, e))], out_specs=[tok, tok],
        out_shape=[jax.ShapeDtypeStruct((j, CHUNK * e), BF), jax.ShapeDtypeStruct((j, CHUNK * e), BF)],
        scratch_shapes=[pltpu.VMEM(w_glu.shape, BF)], compiler_params=_cparams(),
    )(yt, z_cr, w_glu, b_glu)


def _glu_bwd_gt(dr_cr, gt1, w_out, w_glu, yt, z_cr, sg_cr):
    g, _, j = yt.shape
    e, d = w_out.shape
    tj = _cr_tile(j)

    def body(dr_ref, g_ref, wo_hbm, wg_hbm, y_ref, z_ref, sg_ref, dz_ref, dt_ref, dy_ref, wo_ref, wg_ref):
        @pl.when(jnp.logical_and(pl.program_id(0) == 0, pl.program_id(1) == 0))
        def _():
            pltpu.sync_copy(wo_hbm, wo_ref)
            pltpu.sync_copy(wg_hbm, wg_ref)

        do = (dr_ref[...].astype(F32) * g_ref[...]).astype(BF)
        dw = _dot_nt(do, wo_ref[...])
        y = jnp.transpose(y_ref[...].reshape(e, tj).astype(F32))
        gl, dgel = _gelu_parts(y)
        z = z_ref[...].astype(F32)
        sz = _sigmoid(z)
        sg = sg_ref[...].astype(F32)
        dg2 = dw * (z * sz)
        dz_ref[...] = (dw * gl * sg * (sz * (1.0 + z * (1.0 - sz)))).astype(BF)
        dt = (dg2 * gl * sg * (1.0 - sg)).astype(BF)
        dt_ref[...] = dt
        dy = (dg2 * sg + _dot_nt(dt, wg_ref[...])) * dgel
        dy_ref[...] = jnp.transpose(dy).reshape(g, S5_P, tj).astype(BF)

    tok_e = pl.BlockSpec((tj, e), lambda t, s: (t, s))
    return pl.pallas_call(
        body, name="l1_glu_bwd", grid=(j // tj, CHUNK),
        in_specs=[pl.BlockSpec((tj, d), lambda t, s: (t, s)), _full((1, d)), ANY, ANY, _gt_tok_spec(g, tj), tok_e, tok_e],
        out_specs=[tok_e, tok_e, _gt_tok_spec(g, tj)],
        out_shape=[jax.ShapeDtypeStruct((j, CHUNK * e), BF), jax.ShapeDtypeStruct((j, CHUNK * e), BF),
                   jax.ShapeDtypeStruct((g, GT_ROWS, j), BF)],
        scratch_shapes=[pltpu.VMEM(w_out.shape, BF), pltpu.VMEM(w_glu.shape, BF)], compiler_params=_cparams(),
    )(dr_cr, gt1, w_out, w_glu, yt, z_cr, sg_cr)


def _bwd_inproj1_gt(dut, dz_cr, wu_t, w_z, xh_cr, rs_cr, dr2_cr, vecs, tag):
    g, _, j = dut.shape
    e, d = wu_t.shape
    tj = _cr_tile(j)

    def body(du_ref, dz_ref, wu_hbm, wz_hbm, xh_ref, rs_ref, dr2_ref, v_ref, dr1_ref, acc_ref, wu_ref, wz_ref):
        @pl.when(jnp.logical_and(pl.program_id(0) == 0, pl.program_id(1) == 0))
        def _():
            pltpu.sync_copy(wu_hbm, wu_ref)
            pltpu.sync_copy(wz_hbm, wz_ref)
            acc_ref[...] = jnp.zeros_like(acc_ref)

        dh = _dot_tn(du_ref[...].reshape(e, tj), wu_ref[...]) + _dot_nt(dz_ref[...], wz_ref[...])
        xh = xh_ref[...]
        x1 = xh * v_ref[0:1, :] + v_ref[1:2, :]
        dx1 = DN_ALPHA * dr2_ref[...].astype(F32) + dh * v_ref[2:3, :]
        dxh = dx1 * v_ref[0:1, :]
        rstd = rs_ref[:, 0:1]
        dr1 = rstd * (dxh - jnp.mean(dxh, axis=-1, keepdims=True) - xh * jnp.mean(dxh * xh, axis=-1, keepdims=True))
        dr1_ref[...] = dr1.astype(BF)
        acc_ref[0:1, :] += jnp.sum(dh * x1, axis=0, keepdims=True)
        acc_ref[1:2, :] += jnp.sum(dh, axis=0, keepdims=True)
        acc_ref[2:3, :] += jnp.sum(dx1 * xh, axis=0, keepdims=True)
        acc_ref[3:4, :] += jnp.sum(dx1, axis=0, keepdims=True)

    tok_d = pl.BlockSpec((tj, d), lambda t, s: (t, s))
    return pl.pallas_call(
        body, name="l1_bwd_inproj_" + tag, grid=(j // tj, CHUNK),
        in_specs=[_gt_tok_spec(g, tj), pl.BlockSpec((tj, e), lambda t, s: (t, s)), ANY, ANY, tok_d,
                  pl.BlockSpec((tj, 128), lambda t, s: (t, s)), tok_d, _full((8, d))],
        out_specs=[tok_d, _full((8, d))],
        out_shape=[jax.ShapeDtypeStruct((j, CHUNK * d), BF), jax.ShapeDtypeStruct((8, d), F32)],
        scratch_shapes=[pltpu.VMEM(wu_t.shape, BF), pltpu.VMEM(w_z.shape, BF)], compiler_params=_cparams(),
    )(dut, dz_cr, wu_t, w_z, xh_cr, rs_cr, dr2_cr, vecs)


def _dw_gt(lhs_gt, rhs_cr, lhs_gelu, vec, bias_sum, init, out_dtype, name):
    g, _, j = lhs_gt.shape
    e = g * S5_P
    n = rhs_cr.shape[1] // CHUNK
    tj = _cr_tile(j, 512 if j % 512 == 0 else 256)
    nh = 2 if e * n * 4 > (8 << 20) else 1
    tn = n // nh
    nt = j // tj
    has_init = init is not None

    def body(*refs):
        refs = list(refs)
        l_ref, r_ref = refs[0], refs[1]
        pos = 2
        v_ref = i_ref = bs_ref = None
        if vec is not None:
            v_ref = refs[pos]
            pos += 1
        if has_init:
            i_ref = refs[pos]
            pos += 1
        o_ref = refs[pos]
        pos += 1
        if bias_sum:
            bs_ref = refs[pos]
            pos += 1
        acc_ref = refs[pos]
        t, s = pl.program_id(1), pl.program_id(2)

        @pl.when(jnp.logical_and(t == 0, s == 0))
        def _():
            acc_ref[...] = i_ref[...] if has_init else jnp.zeros_like(acc_ref)
            if bias_sum:
                bs_ref[...] = jnp.zeros_like(bs_ref)

        lv = l_ref[...].reshape(e, tj)
        if lhs_gelu:
            lv = _gelu_parts(lv.astype(F32))[0].astype(BF)
        if vec is not None:
            rv = (r_ref[...] * v_ref[0:1, :] + v_ref[1:2, :]).astype(BF)
        else:
            rv = r_ref[...]
        acc_ref[...] += _dot(lv, rv)
        if bias_sum:
            bs_ref[0:1, :] += jnp.sum(rv.astype(F32), axis=0, keepdims=True)

        @pl.when(jnp.logical_and(t == nt - 1, s == CHUNK - 1))
        def _():
            o_ref[...] = acc_ref[...].astype(out_dtype)

    in_specs = [pl.BlockSpec((g, S5_P, tj), lambda h, t, s: (0, s, t)),
                pl.BlockSpec((tj, tn), lambda h, t, s: (t, s * nh + h))]
    args = [lhs_gt, rhs_cr]
    if vec is not None:
        in_specs.append(_full(vec.shape))
        args.append(vec)
    o_spec = pl.BlockSpec((e, tn), lambda h, t, s: (0, h))
    if has_init:
        in_specs.append(o_spec)
        args.append(init)
    out_specs, out_shape = [o_spec], [jax.ShapeDtypeStruct((e, n), out_dtype)]
    if bias_sum:
        out_specs.append(pl.BlockSpec((8, tn), lambda h, t, s: (0, h)))
        out_shape.append(jax.ShapeDtypeStruct((8, n), F32))
    res = pl.pallas_call(
        body, name=name, grid=(nh, nt, CHUNK), in_specs=in_specs, out_specs=out_specs, out_shape=out_shape,
        scratch_shapes=[pltpu.VMEM((e, tn), F32)], compiler_params=_cparams(),
    )(*args)
    return res if bias_sum else res[0]


def _s5_weights(lam_re, lam_im, log_step, b_re, b_im, c_re, c_im, d_skip):
    hp = lax.Precision.HIGHEST
    g = lam_re.shape[1]
    t, p, n = CHUNK, S5_P, S5_N
    dt = jnp.exp(log_step)[..., None]
    ks = jnp.arange(t + 1, dtype=F32).reshape(t + 1, 1, 1, 1)
    mag = jnp.exp(ks * (lam_re * dt)[None])
    ang = ks * (lam_im * dt)[None]
    pr, pi = mag * jnp.cos(ang), mag * jnp.sin(ang)
    ar, ai = pr[1], pi[1]
    qr, qi = ar - 1.0, ai
    den = lam_re * lam_re + lam_im * lam_im
    fr = (qr * lam_re + qi * lam_im) / den
    fi = (qi * lam_re - qr * lam_im) / den
    bt_re, bt_im = b_re.transpose(0, 1, 3, 2), b_im.transpose(0, 1, 3, 2)
    bbr = fr[:, :, None, :] * bt_re - fi[:, :, None, :] * bt_im
    bbi = fr[:, :, None, :] * bt_im + fi[:, :, None, :] * bt_re
    pk_r, pk_i = pr[:t, :, :, None, :], pi[:t, :, :, None, :]
    abr = pk_r * bbr[None] - pk_i * bbi[None]
    abi = pk_r * bbi[None] + pk_i * bbr[None]
    kd = (jnp.einsum("rgpn,krgqn->rgkpq", c_re, abr, precision=hp)
          - jnp.einsum("rgpn,krgqn->rgkpq", c_im, abi, precision=hp))
    skip = jnp.eye(p, dtype=F32)[None] * d_skip.reshape(g, p)[:, :, None]
    diag = kd[0][:, 0] + kd[1][:, 0] + skip
    qd = jnp.concatenate([jnp.flip(kd[1][:, 1:], axis=1), diag[:, None], kd[0][:, 1:]], axis=1)
    toep = jnp.stack([qd[:, t - 1 - s:2 * t - 1 - s] for s in range(t)], axis=1)
    mt = toep.transpose(0, 1, 4, 2, 3).reshape(g, t * p, t * p)
    ab = jnp.concatenate([abr, abi], axis=-1)
    bcc = jnp.stack([jnp.flip(ab[:, 0], axis=0), ab[:, 1]])
    bc = bcc.transpose(2, 1, 3, 0, 4).reshape(g, t * p, 4 * n)
    prf = jnp.stack([pr[1:, 0], jnp.flip(pr[1:, 1], axis=0)])[:, :, :, None, :]
    pif = jnp.stack([pi[1:, 0], jnp.flip(pi[1:, 1], axis=0)])[:, :, :, None, :]
    cr_t = c_re[:, None]
    ci_t = c_im[:, None]
    ccc = jnp.concatenate([cr_t * prf - ci_t * pif, -(cr_t * pif + ci_t * prf)], axis=-1)
    cct = ccc.transpose(2, 1, 3, 0, 4).reshape(g, t * p, 4 * n)
    return mt, bc, cct, pr[t], pi[t]


def _scan_coef_g(lam_re, lam_im, log_step):
    g = lam_re.shape[1]
    ms = jnp.array([1, 2, 4, 0, 0, 0, 0, 0] + list(range(1, 9)) + list(range(8, 0, -1)), F32) * CHUNK
    dt = jnp.exp(log_step)[..., None]
    mag = jnp.exp(ms.reshape(-1, 1, 1, 1) * (lam_re * dt)[None])
    ang = ms.reshape(-1, 1, 1, 1) * (lam_im * dt)[None]
    cr, ci = mag * jnp.cos(ang), mag * jnp.sin(ang)
    lay = lambda a, b: jnp.concatenate([a, b], axis=-1).transpose(0, 2, 1, 3).reshape(24, g * ZG_W)
    return jnp.stack([lay(cr, cr), lay(-ci, ci)])


def _s5_compact(lam_re, lam_im, log_step, b_re, b_im, c_re, c_im, d_skip):
    hp = lax.Precision.HIGHEST
    g = lam_re.shape[1]
    nb = g // GROUPS_PER_BLOCK
    t, p, n = CHUNK, S5_P, S5_N
    dt = jnp.exp(log_step)[..., None]
    ks = jnp.arange(t + 1, dtype=F32).reshape(t + 1, 1, 1, 1)
    mag = jnp.exp(ks * (lam_re * dt)[None])
    ang = ks * (lam_im * dt)[None]
    pr, pi = mag * jnp.cos(ang), mag * jnp.sin(ang)
    ar, ai = pr[1], pi[1]
    qr, qi = ar - 1.0, ai
    den = lam_re * lam_re + lam_im * lam_im
    fr = (qr * lam_re + qi * lam_im) / den
    fi = (qi * lam_re - qr * lam_im) / den
    bt_re, bt_im = b_re.transpose(0, 1, 3, 2), b_im.transpose(0, 1, 3, 2)
    bbr = fr[:, :, None, :] * bt_re - fi[:, :, None, :] * bt_im
    bbi = fr[:, :, None, :] * bt_im + fi[:, :, None, :] * bt_re
    pk_r, pk_i = pr[:t, :, :, None, :], pi[:t, :, :, None, :]
    abr = pk_r * bbr[None] - pk_i * bbi[None]
    abi = pk_r * bbi[None] + pk_i * bbr[None]
    kd = (jnp.einsum("rgpn,krgqn->rgkpq", c_re, abr, precision=hp)
          - jnp.einsum("rgpn,krgqn->rgkpq", c_im, abi, precision=hp))
    skip = jnp.eye(p, dtype=F32)[None] * d_skip.reshape(g, p)[:, :, None]
    diag = kd[0][:, 0] + kd[1][:, 0] + skip
    qd = jnp.concatenate([jnp.flip(kd[1][:, 1:], axis=1), diag[:, None], kd[0][:, 1:]], axis=1)
    nd = 2 * t - 1
    wc = qd.transpose(0, 1, 3, 2).reshape(nb, GROUPS_PER_BLOCK, nd, p, p).transpose(0, 2, 1, 3, 4)
    wcomp = wc.reshape(nb, nd, LANE_BLOCK, p)
    ab = jnp.concatenate([abr, abi], axis=-1)
    bcc = jnp.stack([jnp.flip(ab[:, 0], axis=0), ab[:, 1]])
    bcomp = bcc.reshape(2, t, nb, LANE_BLOCK, 2 * n)
    prf = jnp.stack([pr[1:, 0], jnp.flip(pr[1:, 1], axis=0)])[:, :, :, None, :]
    pif = jnp.stack([pi[1:, 0], jnp.flip(pi[1:, 1], axis=0)])[:, :, :, None, :]
    cr_t = c_re[:, None]
    ci_t = c_im[:, None]
    ccc = jnp.concatenate([cr_t * prf - ci_t * pif, -(cr_t * pif + ci_t * prf)], axis=-1)
    ccomp = ccc.reshape(2, t, nb, LANE_BLOCK, 2 * n)
    return wcomp, bcomp, ccomp, pr[t], pi[t]


def _scan_coef(lam_re, lam_im, log_step):
    g = lam_re.shape[1]
    nb = g // GROUPS_PER_BLOCK
    ms = jnp.array([1, 2, 4, 0, 0, 0, 0, 0] + list(range(1, 9)) + list(range(8, 0, -1)), F32) * CHUNK
    dt = jnp.exp(log_step)[..., None]
    mag = jnp.exp(ms.reshape(-1, 1, 1, 1) * (lam_re * dt)[None])
    ang = ms.reshape(-1, 1, 1, 1) * (lam_im * dt)[None]
    cr, ci = mag * jnp.cos(ang), mag * jnp.sin(ang)
    lay = lambda a: a.reshape(24, 2, nb, ZH).transpose(2, 1, 0, 3)
    return jnp.concatenate([lay(cr), lay(ci)], axis=-1)


def _to_cr(a):
    return a.reshape(a.shape[0] // CHUNK, CHUNK * a.shape[1])


def _from_cr(a, c):
    return a.reshape(a.shape[0] * CHUNK, c)


def _pad8(v):
    return jnp.concatenate([v, jnp.zeros((8 - v.shape[0], v.shape[1]), v.dtype)], axis=0)


def _local_step(x, c, ctx, c_ctx, loss_target, w, late=None, scatter=False):
    l, d = x.shape
    lc = ctx.shape[0]
    tm = min(256, lc)
    assert lc == tm and l % tm == 0 and tm % GRID_W == 0 and (tm & (tm - 1)) == 0
    nl = l // tm

    c8 = _pad8(jnp.stack([c, c_ctx]))
    mod = _ada_fwd(c8, w["ada_w"], w["ada_b"])
    sh = mod[:, :2, :d]
    sc = mod[:, :2, d:2 * d]
    gt = mod[:, :2, 2 * d:]
    ln_g, ln_b = w["ln_g"], w["ln_b"]

    a0, b0 = 1.0 + sc[0], sh[0]
    xch = _Exchange("gather", [late[n][0] for n in late], [late[n][1] for n in late]) if late else None
    p42, got = _inproj0(x, ctx, a0, b0, w["conv_w_in"], tm, xch)
    if late:
        w = dict(w, **dict(zip(late, got)))
    e = w["conv_w_out"].shape[0]
    half = e // 2
    nb = e // LANE_BLOCK
    tc = min(512, half)
    cw = w["conv_w"].reshape(3, 2, half)
    q3 = _conv_fwd(p42, cw, nl, tm, tc)
    xh1_l, xh1_c, rs1_l, rs1_c, fx = _outproj_ln0(q3, w["conv_w_out"], x, ctx, gt[0], tm)
    jl, jc = l // CHUNK, lc // CHUNK

    g0, bb0 = ln_g[0:1], ln_b[0:1]
    a1 = g0 * (1.0 + sc[1])
    b1 = bb0 * (1.0 + sc[1]) + sh[1]
    wu_t = w["ssm_w_in"][:, :e].T
    w_z = w["ssm_w_in"][:, e:]
    ut_l, z_l = _inproj1_gt(xh1_l, a1[0:1], b1[0:1], wu_t, w_z, "lat")
    ut_c, _ = _inproj1_gt(xh1_c, a1[1:2], b1[1:2], wu_t, w_z, "ctx")
    s5 = (w["ssm_lam_re"], w["ssm_lam_im"], w["ssm_log_step"], w["ssm_b_re"], w["ssm_b_im"],
          w["ssm_c_re"], w["ssm_c_im"], w["ssm_d"])
    (mt, bcw, cctw, _, _), s5_vjp = jax.vjp(_s5_weights, *s5)
    mt_b, mtt_b = mt.astype(BF), mt.transpose(0, 2, 1).astype(BF)
    bc_b, cct_b = bcw.astype(BF), cctw.astype(BF)
    coef = lax.stop_gradient(_scan_coef_g(*s5[:3]))
    zz_l, zz_c = _s5_z(ut_l, ut_c, bc_b)
    fwd_chains = ((("c", False), ("l", False)), (("c", True), ("l", True)))
    st_l, st_c = _scan_g(zz_l, zz_c, coef, fwd_chains, False, name="l1_scan_fwd")
    yt = _s5_y(ut_l, st_l, mtt_b, cct_b)
    b_glu = w["ssm_b_glu"].reshape(1, e)
    w_cr, sg_cr = _glu_fwd_gt(yt, z_l, w["ssm_w_glu"], b_glu)
    vec_f = _pad8(jnp.concatenate([g0, bb0, gt[1][0:1], ln_g[1:2], ln_b[1:2]], axis=0))
    dr2, acc_f = _final(w_cr, w["ssm_w_out"], xh1_l, _to_cr(loss_target), vec_f)
    loss = jnp.sum(acc_f[3])

    gt1 = gt[1][0:1]
    dz_l, dt_l, dyt = _glu_bwd_gt(dr2, gt1, w["ssm_w_out"], w["ssm_w_glu"], yt, z_l, sg_cr)
    g_w_out = _dw_cr(w_cr, dr2, "cr", "scaled", gt1, False, None, "l1_dw_out")
    g_w_glu, bsum = _dw_gt(yt, dt_l, True, None, True, None, BF, "l1_dw_glu")
    g_b_glu = bsum[0]
    ds_l = _s5_ds(dyt, cct_b)
    bwd_chains = ((("l", True), ("c", True)), (("l", False), ("c", False)))
    dzz_l, dzz_c, da = _scan_g(ds_l, jnp.zeros_like(zz_c), coef, bwd_chains, True, st_l, st_c, name="l1_scan_bwd")
    dut_l, dut_c = _s5_dx(dyt, dzz_l, dzz_c, mt_b, bc_b)
    d_mt, d_bc, d_cct = _s5_dw(ut_l, ut_c, dyt, dzz_l, dzz_c, st_l)
    da = jnp.sum(da, axis=1).reshape(2, e // S5_P, 2, 2, S5_N)
    da_r = (da[0, :, :, 0] + da[0, :, :, 1]).transpose(1, 0, 2)
    da_i = (da[1, :, :, 1] - da[1, :, :, 0]).transpose(1, 0, 2)
    g_s5 = s5_vjp((d_mt, d_bc, d_cct, da_r, da_i))

    vec_l = _pad8(jnp.concatenate([g0, bb0, 1.0 + sc[1][0:1]], axis=0))
    vec_c = _pad8(jnp.concatenate([g0, bb0, 1.0 + sc[1][1:2]], axis=0))
    dr1_l, acc_l = _bwd_inproj1_gt(dut_l, dz_l, wu_t, w_z, xh1_l, rs1_l, dr2, vec_l, "lat")
    dr1_c, acc_c = _bwd_inproj1_gt(dut_c, jnp.zeros((jc, CHUNK * e), BF), wu_t, w_z, xh1_c, rs1_c,
                                   jnp.zeros((jc, CHUNK * d), BF), vec_c, "ctx")
    mod_l = jnp.concatenate([a1[0:1], b1[0:1]], axis=0)
    mod_c = jnp.concatenate([a1[1:2], b1[1:2]], axis=0)
    g_ut_c = _dw_gt(dut_c, xh1_c, False, mod_c, False, None, F32, "l1_dw_in_u_ctx")
    g_ut = _dw_gt(dut_l, xh1_l, False, mod_l, False, g_ut_c, BF, "l1_dw_in_u")
    g_in_z = _dw_cr(xh1_l, dz_l, "mod", "cr", mod_l, False, None, "l1_dw_in_z")
    g_w_in1 = jnp.concatenate([g_ut.T, g_in_z], axis=1)

    dr1_ln, dr1_cn = _from_cr(dr1_l, d), _from_cr(dr1_c, d)
    dq3, acc_g0 = _bwd_outproj0(dr1_ln, dr1_cn, gt[0], w["conv_w_out"], fx, tm)
    sent1 = ["ssm_w_in", "ssm_w_glu", "ssm_w_out"]
    xch1 = _Exchange("scatter", [g_w_in1, g_w_glu, g_w_out], [BIG[n] for n in sent1]) if scatter else None
    dp42, dcw, recv1 = _conv_bwd(dq3, p42, cw, nl, tm, tc, xch1)
    g_w_in0 = _dw_inproj0(x, ctx, a0, b0, dp42, tm)
    g_w_out0 = _dw_outproj0(q3, dr1_ln, dr1_cn, gt[0], tm)
    sent0 = ["conv_w_in", "conv_w_out"]
    xch0 = _Exchange("scatter", [g_w_in0, g_w_out0], [BIG[n] for n in sent0]) if scatter else None
    grad_x, acc_0, recv0 = _bwd_inproj0(dp42, w["conv_w_in"], x, ctx, dr1_ln, dr1_cn, a0, tm, xch0)
    recv = dict(zip(sent1 + sent0, recv1 + recv0))

    zero = jnp.zeros((d,), F32)
    dm0 = jnp.stack([jnp.concatenate([acc_0[2], acc_0[0], acc_g0[0]]), jnp.concatenate([acc_0[3], acc_0[1], acc_g0[1]])])
    dm1 = jnp.stack([jnp.concatenate([acc_l[1], acc_l[0], acc_f[2]]), jnp.concatenate([acc_c[1], acc_c[0], zero])])
    dm8 = jnp.stack([_pad8(dm0), _pad8(dm1)])
    g_ada_w, dc8 = _ada_bwd(c8, w["ada_w"], dm8)

    grads = {
        "c_ctx": dc8[0, 1] + dc8[1, 1],
        "ada_w": g_ada_w,
        "ada_b": jnp.stack([dm0[0] + dm0[1], dm1[0] + dm1[1]]),
        "ln_g": jnp.stack([acc_l[2] + acc_c[2], acc_f[0]]),
        "ln_b": jnp.stack([acc_l[3] + acc_c[3], acc_f[1]]),
        "conv_w_in": g_w_in0, "conv_w": dcw[:3].reshape(3, e), "conv_w_out": g_w_out0,
        "ssm_w_in": g_w_in1,
        "ssm_lam_re": g_s5[0], "ssm_lam_im": g_s5[1], "ssm_log_step": g_s5[2],
        "ssm_b_re": g_s5[3], "ssm_b_im": g_s5[4], "ssm_c_re": g_s5[5], "ssm_c_im": g_s5[6], "ssm_d": g_s5[7],
        "ssm_w_glu": g_w_glu, "ssm_b_glu": g_b_glu, "ssm_w_out": g_w_out,
    }
    for n in recv:
        del grads[n]
    return loss, grad_x, grads, recv


WEIGHTS = ["c_ctx", "ada_w", "ada_b", "ln_g", "ln_b", "conv_w_in", "conv_w", "conv_w_out", "ssm_w_in",
           "ssm_lam_re", "ssm_lam_im", "ssm_log_step", "ssm_b_re", "ssm_b_im", "ssm_c_re", "ssm_c_im",
           "ssm_d", "ssm_w_glu", "ssm_b_glu", "ssm_w_out"]
BIG = {"ada_w": 1, "conv_w_in": 1, "conv_w_out": 0, "ssm_w_in": 1, "ssm_w_glu": 0, "ssm_w_out": 0}
SMALL_SHARDED = ["conv_w", "ssm_d", "ssm_b_glu"]
REPLICATED = ["c_ctx", "ada_b", "ln_g", "ln_b", "ssm_lam_re", "ssm_lam_im", "ssm_log_step",
              "ssm_b_re", "ssm_b_im", "ssm_c_re", "ssm_c_im"]


def _view2d(name, a):
    return a.reshape(-1, a.shape[-1])


def kernel(x, c, ctx, c_ctx, ada_w, ada_b, ln_g, ln_b, conv_w_in, conv_w, conv_w_out, ssm_w_in, ssm_lam_re, ssm_lam_im, ssm_log_step, ssm_b_re, ssm_b_im, ssm_c_re, ssm_c_im, ssm_d, ssm_w_glu, ssm_b_glu, ssm_w_out, loss_target, m_c_ctx, m_ada_w, m_ada_b, m_ln_g, m_ln_b, m_conv_w_in, m_conv_w, m_conv_w_out, m_ssm_w_in, m_ssm_lam_re, m_ssm_lam_im, m_ssm_log_step, m_ssm_b_re, m_ssm_b_im, m_ssm_c_re, m_ssm_c_im, m_ssm_d, m_ssm_w_glu, m_ssm_b_glu, m_ssm_w_out, v_c_ctx, v_ada_w, v_ada_b, v_ln_g, v_ln_b, v_conv_w_in, v_conv_w, v_conv_w_out, v_ssm_w_in, v_ssm_lam_re, v_ssm_lam_im, v_ssm_log_step, v_ssm_b_re, v_ssm_b_im, v_ssm_c_re, v_ssm_c_im, v_ssm_d, v_ssm_w_glu, v_ssm_b_glu, v_ssm_w_out):
    args = locals()
    wt = {n: args[n] for n in WEIGHTS}
    mt = {n: args["m_" + n] for n in WEIGHTS}
    vt = {n: args["v_" + n] for n in WEIGHTS}

    big_names = list(BIG)
    shard = {n: _view2d(n, wt[n]).astype(BF) for n in big_names}
    first = ["ada_w", "conv_w_in"]
    small = jnp.concatenate([wt["conv_w"][0], wt["ssm_d"], wt["ssm_b_glu"]], axis=0)
    small = jnp.concatenate([small, jnp.zeros((3, small.shape[1]), F32)], axis=0)
    gathered = _all_gather([shard[n] for n in first] + [small], [BIG[n] for n in first] + [1], "gather_weights")
    full = dict(zip(first, gathered[:-1]))
    small_full = gathered[-1]
    late = {n: (shard[n], BIG[n]) for n in big_names if n not in first}
    d = x.shape[-1]
    w = {
        "ada_w": full["ada_w"].reshape(2, d, 3 * d), "ada_b": ada_b, "ln_g": ln_g, "ln_b": ln_b,
        "conv_w_in": full["conv_w_in"], "conv_w": small_full[0:3],
        "ssm_lam_re": ssm_lam_re[0], "ssm_lam_im": ssm_lam_im[0],
        "ssm_log_step": ssm_log_step[0], "ssm_b_re": ssm_b_re[0], "ssm_b_im": ssm_b_im[0],
        "ssm_c_re": ssm_c_re[0], "ssm_c_im": ssm_c_im[0], "ssm_d": small_full[3], "ssm_b_glu": small_full[4],
    }

    loss, grad_x, g, recv_big = _local_step(x[0], c[0], ctx[0], c_ctx, loss_target[0], w, late, True)
    loss = lax.psum(loss, ("x", "y", "c"))

    blob_names = REPLICATED + SMALL_SHARDED
    flat = jnp.concatenate([g[n].reshape(-1).astype(F32) for n in blob_names])
    nflat = flat.shape[0]
    rows = -(-nflat // (N_DEV * 128 * 8)) * 8
    flat = jnp.concatenate([flat, jnp.zeros((N_DEV * rows * 128 - nflat,), F32)]).reshape(N_DEV * rows, 128)
    last = [n for n in big_names if n not in recv_big]
    recv = _all_to_all([_view2d(n, g[n]) for n in last] + [flat], [BIG[n] for n in last] + [0], "scatter_grads")
    recv_big.update(zip(last, recv[:-1]))
    blob_sum = _sum_partials(recv[-1])
    blob = _all_gather([blob_sum], [0], "gather_small_grads")[0].reshape(-1)
    small_g, off = {}, 0
    for n in blob_names:
        shape = wt[n].shape if n in REPLICATED else (*wt[n].shape[:-1], wt[n].shape[-1] * N_DEV)
        size = math.prod(shape)
        small_g[n] = blob[off:off + size].reshape(shape)
        off += size
    me = 4 * lax.axis_index("x") + 2 * lax.axis_index("y") + lax.axis_index("c")
    for n in SMALL_SHARDED:
        size = wt[n].shape[-1]
        small_g[n] = lax.dynamic_slice_in_dim(small_g[n], me * size, size, axis=small_g[n].ndim - 1)

    out_g, out_d, out_m, out_v = {}, {}, {}, {}
    for n in big_names:
        stack = recv_big[n]
        shp = wt[n].shape
        res = _adamw(stack, _view2d(n, wt[n]), _view2d(n, mt[n]), _view2d(n, vt[n]), "adamw_" + n)
        out_g[n], out_d[n], out_m[n], out_v[n] = [r.reshape(shp) for r in res]
    names = list(small_g)
    cat = lambda t: jnp.concatenate([t[n].reshape(-1) for n in names])
    gs, ws, ms, vs = cat(small_g), cat(wt), cat(mt), cat(vt)
    ns = gs.shape[0]
    rs = -(-ns // (128 * 512)) * 512
    padr = lambda a: jnp.concatenate([a, jnp.ones((rs * 128 - ns,), F32)]).reshape(rs, 128)
    res = _adamw(padr(gs)[None], padr(ws), padr(ms), padr(vs), "adamw_small")
    off = 0
    for n in names:
        size = math.prod(wt[n].shape)
        out_g[n], out_d[n], out_m[n], out_v[n] = [r.reshape(-1)[off:off + size].reshape(wt[n].shape) for r in res]
        off += size

    return (loss, grad_x[None], *[out_g[n] for n in WEIGHTS], *[out_d[n] for n in WEIGHTS],
            *[out_m[n] for n in WEIGHTS], *[out_v[n] for n in WEIGHTS])
```

```python
import math

import jax
import jax.numpy as jnp
from jax import lax
from jax.experimental import pallas as pl
from jax.experimental.pallas import tpu as pltpu

F32 = jnp.float32
BF = jnp.bfloat16
MESH = pl.DeviceIdType.MESH
N_DEV = 8

GRID_W = 64
CHUNK = 16
S5_P = 16
S5_N = 64
LANE_BLOCK = 128
GROUPS_PER_BLOCK = LANE_BLOCK // S5_P
BCR_W = CHUNK * LANE_BLOCK
ZL_W = 2 * 2 * GROUPS_PER_BLOCK * S5_N
ZH = ZL_W // 4
LN_EPS = 1e-5
DN_ALPHA = 4.0 ** 0.25
ADAM_LR, ADAM_B1, ADAM_B2, ADAM_EPS, ADAM_WD, ADAM_STEP = 1e-3, 0.9, 0.999, 1e-8, 0.01, 10
GELU_C0 = math.sqrt(2.0 / math.pi)
GELU_C1 = 0.044715
VMEM_MB = 52

ANY = pl.BlockSpec(memory_space=pl.ANY)


def _cparams():
    return pltpu.CompilerParams(vmem_limit_bytes=VMEM_MB << 20)


def _dot(a, b):
    return jnp.dot(a, b, preferred_element_type=F32)


def _dot_nt(a, b):
    return lax.dot_general(a, b, (((1,), (1,)), ((), ())), preferred_element_type=F32)


def _dot_tn(a, b):
    return lax.dot_general(a, b, (((0,), (0,)), ((), ())), preferred_element_type=F32)


def _sigmoid(x):
    return 1.0 / (1.0 + jnp.exp(-x))


def _gelu_parts(y):
    th = jnp.tanh(GELU_C0 * (y + GELU_C1 * y * y * y))
    g = 0.5 * y * (1.0 + th)
    dg = 0.5 * (1.0 + th) + 0.5 * y * (1.0 - th * th) * GELU_C0 * (1.0 + 3.0 * GELU_C1 * y * y)
    return g, dg


def _full(shape):
    nd = len(shape)
    return pl.BlockSpec(shape, lambda *_: (0,) * nd)


def _mesh_pos():
    x, y, c = lax.axis_index("x"), lax.axis_index("y"), lax.axis_index("c")
    return x, y, c


def _peer(pos, k):
    x, y, c = pos
    px = 1 - x if (k >> 2) & 1 else x
    py = 1 - y if (k >> 1) & 1 else y
    pc = 1 - c if k & 1 else c
    return (px, py, pc), 4 * px + 2 * py + pc


def _shard_at(ref, axis, idx, n):
    if axis == 0:
        return ref.at[pl.ds(idx * n, n)]
    return ref.at[:, pl.ds(idx * n, n)]


class _Exchange:
    def __init__(self, kind, arrays, axes):
        self.kind, self.axes, self.n = kind, list(axes), len(arrays)
        self.arrays = list(arrays)
        self.out_shape = []
        for s, ax in zip(arrays, axes):
            shp = list(s.shape)
            if kind == "gather":
                shp[ax] *= N_DEV
                self.out_shape.append(jax.ShapeDtypeStruct(tuple(shp), s.dtype))
            else:
                shp[ax] //= N_DEV
                self.out_shape.append(jax.ShapeDtypeStruct((N_DEV, *shp), s.dtype))
        self.scratch = [pltpu.SemaphoreType.DMA((self.n, N_DEV - 1)), pltpu.SemaphoreType.DMA((self.n, N_DEV - 1)),
                        pltpu.SemaphoreType.DMA((self.n,))]

    def _copies(self, ins, outs, sems):
        send_sems, recv_sems, local_sems = sems
        pos = _mesh_pos()
        me = 4 * pos[0] + 2 * pos[1] + pos[2]
        local, sends, recvs = [], [], []
        for i in range(self.n):
            ax = self.axes[i]
            if self.kind == "gather":
                size = ins[i].shape[ax]
                src = lambda idx, i=i: ins[i]
                dst = lambda idx, i=i, ax=ax, size=size: _shard_at(outs[i], ax, idx, size)
                mine, theirs = (lambda pidx: me), (lambda pidx: pidx)
            else:
                size = ins[i].shape[ax] // N_DEV
                src = lambda idx, i=i, ax=ax, size=size: _shard_at(ins[i], ax, idx, size)
                dst = lambda idx, i=i: outs[i].at[idx]
                mine, theirs = (lambda pidx: me), (lambda pidx: pidx)
            src_own = src(me)
            local.append(pltpu.make_async_copy(src_own, dst(me), local_sems.at[i]))
            for k in range(1, N_DEV):
                peer, pidx = _peer(pos, k)
                out_src = src(me) if self.kind == "gather" else src(pidx)
                sends.append(pltpu.make_async_remote_copy(
                    src_ref=out_src, dst_ref=dst(mine(pidx)), send_sem=send_sems.at[i, k - 1],
                    recv_sem=recv_sems.at[i, k - 1], device_id=peer, device_id_type=MESH))
                recvs.append(pltpu.make_async_remote_copy(
                    src_ref=out_src, dst_ref=dst(theirs(pidx)), send_sem=send_sems.at[i, k - 1],
                    recv_sem=recv_sems.at[i, k - 1], device_id=peer, device_id_type=MESH))
        return local, sends, recvs

    def start(self, ins, outs, sems):
        local, sends, _ = self._copies(ins, outs, sems)
        for cp in local + sends:
            cp.start()

    def wait(self, ins, outs, sems):
        local, sends, recvs = self._copies(ins, outs, sems)
        for cp in recvs:
            cp.wait_recv()
        for cp in sends:
            cp.wait_send()
        for cp in local:
            cp.wait()

    def run(self, name):
        n = self.n

        def body(*refs):
            ins, outs, sems = refs[:n], refs[n:2 * n], refs[2 * n:]
            self.start(ins, outs, sems)
            self.wait(ins, outs, sems)

        return pl.pallas_call(body, name=name, out_shape=self.out_shape, in_specs=[ANY] * n, out_specs=[ANY] * n,
                              scratch_shapes=self.scratch)(*self.arrays)


def _hosted_call(body, xch, grid, in_specs, out_specs, out_shape, scratch, args, name):
    out_specs, out_shape = list(out_specs), list(out_shape)
    n_in, n_out = len(in_specs), len(out_specs)
    if xch is None:
        res = pl.pallas_call(body, name=name, grid=grid, in_specs=in_specs, out_specs=out_specs, out_shape=out_shape,
                             scratch_shapes=list(scratch), compiler_params=_cparams())(*args)
        return list(res), []
    n = xch.n
    rank = len(grid)

    def wrapped(*refs):
        ins, x_ins = refs[:n_in], refs[n_in:n_in + n]
        outs = refs[n_in + n:n_in + n + n_out]
        x_outs = refs[n_in + n + n_out:n_in + 2 * n + n_out]
        rest = refs[n_in + 2 * n + n_out:]
        own, sems = rest[:len(rest) - 3], rest[len(rest) - 3:]
        ids = [pl.program_id(a) for a in range(rank)]
        first, last = ids[0] == 0, ids[0] == grid[0] - 1
        for a in range(1, rank):
            first = jnp.logical_and(first, ids[a] == 0)
            last = jnp.logical_and(last, ids[a] == grid[a] - 1)

        @pl.when(first)
        def _():
            xch.start(x_ins, x_outs, sems)

        body(*ins, *outs, *own)

        @pl.when(last)
        def _():
            xch.wait(x_ins, x_outs, sems)

    res = pl.pallas_call(
        wrapped, name=name, grid=grid, in_specs=list(in_specs) + [ANY] * n, out_specs=out_specs + [ANY] * n,
        out_shape=out_shape + xch.out_shape, scratch_shapes=list(scratch) + xch.scratch, compiler_params=_cparams(),
    )(*args, *xch.arrays)
    return list(res[:n_out]), list(res[n_out:])


def _all_gather(shards, axes, name):
    return _Exchange("gather", shards, axes).run(name)


def _all_to_all(parts, axes, name):
    return _Exchange("scatter", parts, axes).run(name)


def _ada_fwd(c8, ada_w, ada_b):
    nl, d, d3 = ada_w.shape

    def body(c_ref, w_ref, b_ref, o_ref):
        cv = c_ref[...]
        s = (cv * _sigmoid(cv)).astype(BF)
        o_ref[0] = _dot(s, w_ref[0]) + b_ref[0]

    return pl.pallas_call(
        body, name="ada_fwd", grid=(nl,),
        in_specs=[_full((8, d)), pl.BlockSpec((1, d, d3), lambda l: (l, 0, 0)), pl.BlockSpec((1, 1, d3), lambda l: (l, 0, 0))],
        out_specs=pl.BlockSpec((1, 8, d3), lambda l: (l, 0, 0)),
        out_shape=jax.ShapeDtypeStruct((nl, 8, d3), F32), compiler_params=_cparams(),
    )(c8, ada_w, ada_b.reshape(nl, 1, d3))


def _ada_bwd(c8, ada_w, dm8):
    nl, d, d3 = ada_w.shape

    def body(c_ref, w_ref, dm_ref, dw_ref, dc_ref):
        cv = c_ref[...]
        sg = _sigmoid(cv)
        s = (cv * sg).astype(BF)
        dm = dm_ref[0].astype(BF)
        dw_ref[0] = _dot_tn(s, dm).astype(BF)
        dc_ref[0] = _dot_nt(dm, w_ref[0]) * (sg * (1.0 + cv * (1.0 - sg)))

    return pl.pallas_call(
        body, name="ada_bwd", grid=(nl,),
        in_specs=[_full((8, d)), pl.BlockSpec((1, d, d3), lambda l: (l, 0, 0)), pl.BlockSpec((1, 8, d3), lambda l: (l, 0, 0))],
        out_specs=[pl.BlockSpec((1, d, d3), lambda l: (l, 0, 0)), pl.BlockSpec((1, 8, d), lambda l: (l, 0, 0))],
        out_shape=[jax.ShapeDtypeStruct((nl, d, d3), BF), jax.ShapeDtypeStruct((nl, 8, d), F32)],
        compiler_params=_cparams(),
    )(c8, ada_w, dm8)


def _sum_partials(stack):
    _, r, c = stack.shape

    def body(s_ref, o_ref):
        acc = s_ref[0]
        for p in range(1, N_DEV):
            acc = acc + s_ref[p]
        o_ref[...] = acc

    return pl.pallas_call(body, name="sum_partials", out_shape=jax.ShapeDtypeStruct((r, c), F32),
                          in_specs=[_full(stack.shape)], out_specs=_full((r, c)), grid=(1,),
                          compiler_params=_cparams())(stack)


def _adamw(gstack, w, m, v, name):
    p, r, c = gstack.shape
    tr = r
    for cand in (512 if c <= 256 else 256, 128, 64, 32, 16, 8):
        if r % cand == 0 and r > cand:
            tr = cand
            break
    bc1 = 1.0 - ADAM_B1 ** ADAM_STEP
    bc2 = 1.0 - ADAM_B2 ** ADAM_STEP

    def body(g_ref, w_ref, m_ref, v_ref, go_ref, d_ref, mo_ref, vo_ref):
        g = g_ref[0].astype(F32)
        for q in range(1, p):
            g = g + g_ref[q].astype(F32)
        mn = ADAM_B1 * m_ref[...] + (1.0 - ADAM_B1) * g
        vn = ADAM_B2 * v_ref[...] + (1.0 - ADAM_B2) * (g * g)
        go_ref[...] = g
        mo_ref[...] = mn
        vo_ref[...] = vn
        d_ref[...] = -ADAM_LR * ((mn / bc1) / (jnp.sqrt(vn / bc2) + ADAM_EPS) + ADAM_WD * w_ref[...])

    row = pl.BlockSpec((tr, c), lambda i: (i, 0))
    sds = jax.ShapeDtypeStruct((r, c), F32)
    return pl.pallas_call(
        body, name=name, grid=(r // tr,),
        in_specs=[pl.BlockSpec((p, tr, c), lambda i: (0, i, 0)), row, row, row],
        out_specs=[row, row, row, row], out_shape=[sds, sds, sds, sds], compiler_params=_cparams(),
    )(gstack, w, m, v)


def _lat_or_ctx_specs(tm, d, nl, grid_rank, row_axis):
    def lat(*ids):
        return (jnp.minimum(ids[row_axis], nl - 1), 0)

    def ctx(*ids):
        return (jnp.maximum(ids[row_axis] - nl, 0), 0)

    return pl.BlockSpec((tm, d), lat), pl.BlockSpec((tm, d), ctx)


def _sel_row(ref, is_ctx):
    return jnp.where(is_ctx, ref[1:2, :], ref[0:1, :])


def _inproj0(x, ctx, a2, b2, w, tm, xch=None):
    l, d = x.shape
    nl, nc = l // tm, ctx.shape[0] // tm
    e = w.shape[1] // 4
    half = e // 2

    def body(x_ref, c_ref, a_ref, b_ref, w_hbm, o_ref, w_ref):
        i = pl.program_id(0)

        @pl.when(i == 0)
        def _():
            pltpu.sync_copy(w_hbm, w_ref)

        is_ctx = i >= nl
        xv = jnp.where(is_ctx, c_ref[...], x_ref[...])
        h = (xv * _sel_row(a_ref, is_ctx) + _sel_row(b_ref, is_ctx)).astype(BF)
        for k in range(4):
            r = _dot(h, w_ref[:, k * e:(k + 1) * e])
            o_ref[k, 0] = r[:, :half].astype(BF)
            o_ref[k, 1] = r[:, half:].astype(BF)

    lat, cx = _lat_or_ctx_specs(tm, d, nl, 1, 0)
    (p42,), extra = _hosted_call(
        body, xch, grid=(nl + nc,),
        in_specs=[lat, cx, _full((2, d)), _full((2, d)), ANY],
        out_specs=[pl.BlockSpec((4, 2, tm, half), lambda i: (0, 0, i, 0))],
        out_shape=[jax.ShapeDtypeStruct((4, 2, l + ctx.shape[0], half), BF)],
        scratch=[pltpu.VMEM(w.shape, BF)], args=(x, ctx, a2, b2, w), name="l0_inproj")
    return p42, extra


def _conv_taps(u, w_up, w_mid, w_dn, pos, rl, tm):
    up = jnp.where(pos == 0, 0.0, pltpu.roll(u, 1, 0))
    dn = jnp.where(pos == rl - 1, 0.0, pltpu.roll(u, tm - 1, 0))
    return w_up * up + w_mid * u + w_dn * dn, up, dn


def _conv_halo_specs(tm, tc, nl, lead):
    hb = tm // GRID_W

    def prev(j, i):
        return (0, 1, jnp.maximum(jnp.minimum(i, nl - 1) * hb - 1, 0), j)

    def nxt(j, i):
        return (0, 1, jnp.minimum((jnp.minimum(i, nl - 1) + 1) * hb, nl * hb - 1), j)

    return pl.BlockSpec((lead, 1, GRID_W, tc), prev), pl.BlockSpec((lead, 1, GRID_W, tc), nxt)


def _conv_fwd(p42, cw, nl, tm, tc):
    _, _, r, half = p42.shape
    nt = r // tm

    def body(p_ref, hp_ref, hn_ref, cw_ref, o_ref):
        i = pl.program_id(1)
        is_ctx = i >= nl
        row = lax.broadcasted_iota(jnp.int32, (tm, tc), 0)
        rl = jnp.where(is_ctx, tm, GRID_W)
        pos = jnp.bitwise_and(row, rl - 1)

        def gate(hv, yc):
            bg = p_ref[0, hv].astype(F32)
            z = p_ref[3, hv].astype(F32)
            return (bg * yc * (z * _sigmoid(z))).astype(BF)

        u_h = p_ref[1, 0].astype(F32) * p_ref[2, 0].astype(F32)
        w_h = cw_ref[:, 0, :]
        o_ref[0] = gate(0, _conv_taps(u_h, w_h[0:1], w_h[1:2], w_h[2:3], pos, rl, tm)[0])
        u_v = p_ref[1, 1].astype(F32) * p_ref[2, 1].astype(F32)
        w_v = cw_ref[:, 1, :]

        @pl.when(is_ctx)
        def _():
            o_ref[1] = gate(1, _conv_taps(u_v, w_v[0:1], w_v[1:2], w_v[2:3], pos, rl, tm)[0])

        @pl.when(jnp.logical_not(is_ctx))
        def _():
            up = hp_ref[1, 0].astype(F32) * hp_ref[2, 0].astype(F32) * (i > 0).astype(F32)
            dn = hn_ref[1, 0].astype(F32) * hn_ref[2, 0].astype(F32) * (i < nl - 1).astype(F32)
            ext = jnp.concatenate([up, u_v, dn], axis=0)
            yc = w_v[0:1] * ext[0:tm] + w_v[1:2] * u_v + w_v[2:3] * ext[2 * GRID_W:tm + 2 * GRID_W]
            o_ref[1] = gate(1, yc)

    hp, hn = _conv_halo_specs(tm, tc, nl, 4)
    return pl.pallas_call(
        body, name="l0_conv_fwd", grid=(half // tc, nt),
        in_specs=[pl.BlockSpec((4, 2, tm, tc), lambda j, i: (0, 0, i, j)), hp, hn,
                  pl.BlockSpec((3, 2, tc), lambda j, i: (0, 0, j))],
        out_specs=pl.BlockSpec((2, tm, tc), lambda j, i: (0, i, j)),
        out_shape=jax.ShapeDtypeStruct((2, r, half), BF), compiler_params=_cparams(),
    )(p42, p42, p42, cw)


def _outproj_ln0(q3, w_out, x, ctx, gt2, tm):
    l, d = x.shape
    lc = ctx.shape[0]
    nl, nc = l // tm, lc // tm
    _, r, half = q3.shape
    tjo = tm // CHUNK

    def body(q_ref, w_hbm, x_ref, c_ref, g_ref, xl_ref, xc_ref, rl_ref, rc_ref, fx_ref, w_ref, xs_ref, rs_ref):
        i = pl.program_id(0)

        @pl.when(i == 0)
        def _():
            pltpu.sync_copy(w_hbm, w_ref)

        is_ctx = i >= nl
        fx = _dot(q_ref[0], w_ref[:half, :]) + _dot(q_ref[1], w_ref[half:, :])
        xv = jnp.where(is_ctx, c_ref[...], x_ref[...])
        rr = DN_ALPHA * xv + _sel_row(g_ref, is_ctx) * fx
        mu = jnp.mean(rr, axis=-1, keepdims=True)
        cen = rr - mu
        rstd = lax.rsqrt(jnp.mean(cen * cen, axis=-1, keepdims=True) + LN_EPS)
        xh = cen * rstd
        for lb in range(d // 128):
            xs_ref[lb] = xh[:, lb * 128:(lb + 1) * 128]
        rs_ref[...] = jnp.broadcast_to(rstd, (tm, 128))
        fx_ref[...] = fx.astype(BF)

        def to_cr(xo_ref, ro_ref):
            for s in range(CHUNK):
                for lb in range(d // 128):
                    xo_ref[:, s * d + lb * 128:s * d + (lb + 1) * 128] = xs_ref.at[lb][pl.ds(s, tjo, stride=CHUNK), :]
                ro_ref[:, s * 128:(s + 1) * 128] = rs_ref[pl.ds(s, tjo, stride=CHUNK), :]

        @pl.when(jnp.logical_not(is_ctx))
        def _():
            to_cr(xl_ref, rl_ref)

        @pl.when(is_ctx)
        def _():
            to_cr(xc_ref, rc_ref)

    lat, cx = _lat_or_ctx_specs(tm, d, nl, 1, 0)
    lat_o = lambda w_: pl.BlockSpec((tjo, CHUNK * w_), lambda i: (jnp.minimum(i, nl - 1), 0))
    ctx_o = lambda w_: pl.BlockSpec((tjo, CHUNK * w_), lambda i: (jnp.maximum(i - nl, 0), 0))
    return pl.pallas_call(
        body, name="l0_outproj_ln", grid=(nl + nc,),
        in_specs=[pl.BlockSpec((2, tm, half), lambda i: (0, i, 0)), ANY, lat, cx, _full((2, d))],
        out_specs=[lat_o(d), ctx_o(d), lat_o(128), ctx_o(128), pl.BlockSpec((tm, d), lambda i: (i, 0))],
        out_shape=[jax.ShapeDtypeStruct((l // CHUNK, CHUNK * d), F32), jax.ShapeDtypeStruct((lc // CHUNK, CHUNK * d), F32),
                   jax.ShapeDtypeStruct((l // CHUNK, CHUNK * 128), F32), jax.ShapeDtypeStruct((lc // CHUNK, CHUNK * 128), F32),
                   jax.ShapeDtypeStruct((r, d), BF)],
        scratch_shapes=[pltpu.VMEM(w_out.shape, BF), pltpu.VMEM((d // 128, tm, 128), F32), pltpu.VMEM((tm, 128), F32)],
        compiler_params=_cparams(),
    )(q3, w_out, x, ctx, gt2)


def _bwd_outproj0(dr_l, dr_c, gt2, w_out, fx, tm):
    l, d = dr_l.shape
    nl, nc = l // tm, dr_c.shape[0] // tm
    e = w_out.shape[0]
    half = e // 2
    r = l + dr_c.shape[0]

    def body(dl_ref, dc_ref, g_ref, w_hbm, fx_ref, dq_ref, acc_ref, w_ref):
        i = pl.program_id(0)

        @pl.when(i == 0)
        def _():
            pltpu.sync_copy(w_hbm, w_ref)
            acc_ref[...] = jnp.zeros_like(acc_ref)

        is_ctx = i >= nl
        dr = jnp.where(is_ctx, dc_ref[...], dl_ref[...]).astype(F32)
        dfx = (dr * _sel_row(g_ref, is_ctx)).astype(BF)
        dq_ref[0] = _dot_nt(dfx, w_ref[:half, :]).astype(BF)
        dq_ref[1] = _dot_nt(dfx, w_ref[half:, :]).astype(BF)
        s = jnp.sum(dr * fx_ref[...].astype(F32), axis=0, keepdims=True)
        sel = is_ctx.astype(F32)
        acc_ref[0:1, :] += s * (1.0 - sel)
        acc_ref[1:2, :] += s * sel

    lat, cx = _lat_or_ctx_specs(tm, d, nl, 1, 0)
    return pl.pallas_call(
        body, name="l0_bwd_outproj", grid=(nl + nc,),
        in_specs=[lat, cx, _full((2, d)), ANY, pl.BlockSpec((tm, d), lambda i: (i, 0))],
        out_specs=[pl.BlockSpec((2, tm, half), lambda i: (0, i, 0)), _full((8, d))],
        out_shape=[jax.ShapeDtypeStruct((2, r, half), BF), jax.ShapeDtypeStruct((8, d), F32)],
        scratch_shapes=[pltpu.VMEM(w_out.shape, BF)], compiler_params=_cparams(),
    )(dr_l, dr_c, gt2, w_out, fx)


def _conv_bwd(dq3, p42, cw, nl, tm, tc, xch=None):
    _, _, r, half = p42.shape
    nt = r // tm

    def body(dq_ref, dqp_ref, dqn_ref, p_ref, hp_ref, hn_ref, cw_ref, dp_ref, dw_ref):
        i = pl.program_id(1)
        is_ctx = i >= nl

        @pl.when(i == 0)
        def _():
            dw_ref[...] = jnp.zeros_like(dw_ref)

        row = lax.broadcasted_iota(jnp.int32, (tm, tc), 0)
        rl = jnp.where(is_ctx, tm, GRID_W)
        pos = jnp.bitwise_and(row, rl - 1)

        def pieces(dq, bg, z):
            sz = _sigmoid(z)
            sil = z * sz
            return dq * bg * sil, dq * sil, dq * bg * (sz * (1.0 + z * (1.0 - sz)))

        def seq_half(hv):
            bg, cg = p_ref[0, hv].astype(F32), p_ref[1, hv].astype(F32)
            v, z = p_ref[2, hv].astype(F32), p_ref[3, hv].astype(F32)
            w = cw_ref[:, hv, :]
            u = cg * v
            yc, u_up, u_dn = _conv_taps(u, w[0:1], w[1:2], w[2:3], pos, rl, tm)
            dyc, dbg_f, dz_f = pieces(dq_ref[hv].astype(F32), bg, z)
            du = _conv_taps(dyc, w[2:3], w[1:2], w[0:1], pos, rl, tm)[0]
            dp_ref[0, hv] = (dbg_f * yc).astype(BF)
            dp_ref[1, hv] = (du * v).astype(BF)
            dp_ref[2, hv] = (du * cg).astype(BF)
            dp_ref[3, hv] = (dz_f * yc).astype(BF)
            dw_ref[0:1, hv, :] += jnp.sum(dyc * u_up, axis=0, keepdims=True)
            dw_ref[1:2, hv, :] += jnp.sum(dyc * u, axis=0, keepdims=True)
            dw_ref[2:3, hv, :] += jnp.sum(dyc * u_dn, axis=0, keepdims=True)

        seq_half(0)

        @pl.when(is_ctx)
        def _():
            seq_half(1)

        @pl.when(jnp.logical_not(is_ctx))
        def _():
            bg, cg = p_ref[0, 1].astype(F32), p_ref[1, 1].astype(F32)
            v, z = p_ref[2, 1].astype(F32), p_ref[3, 1].astype(F32)
            w = cw_ref[:, 1, :]
            u = cg * v
            m_up = (i > 0).astype(F32)
            m_dn = (i < nl - 1).astype(F32)

            def halo(h_ref, dqh_ref, msk):
                hb, hc = h_ref[0, 0].astype(F32), h_ref[1, 0].astype(F32)
                hv_, hz = h_ref[2, 0].astype(F32), h_ref[3, 0].astype(F32)
                return hc * hv_ * msk, pieces(dqh_ref[0].astype(F32), hb, hz)[0] * msk

            u_p, dyc_p = halo(hp_ref, dqp_ref, m_up)
            u_n, dyc_n = halo(hn_ref, dqn_ref, m_dn)
            u_ext = jnp.concatenate([u_p, u, u_n], axis=0)
            u_up, u_dn = u_ext[0:tm], u_ext[2 * GRID_W:tm + 2 * GRID_W]
            yc = w[0:1] * u_up + w[1:2] * u + w[2:3] * u_dn
            dyc, dbg_f, dz_f = pieces(dq_ref[1].astype(F32), bg, z)
            d_ext = jnp.concatenate([dyc_p, dyc, dyc_n], axis=0)
            du = w[0:1] * d_ext[2 * GRID_W:tm + 2 * GRID_W] + w[1:2] * dyc + w[2:3] * d_ext[0:tm]
            dp_ref[0, 1] = (dbg_f * yc).astype(BF)
            dp_ref[1, 1] = (du * v).astype(BF)
            dp_ref[2, 1] = (du * cg).astype(BF)
            dp_ref[3, 1] = (dz_f * yc).astype(BF)
            dw_ref[0:1, 1, :] += jnp.sum(dyc * u_up, axis=0, keepdims=True)
            dw_ref[1:2, 1, :] += jnp.sum(dyc * u, axis=0, keepdims=True)
            dw_ref[2:3, 1, :] += jnp.sum(dyc * u_dn, axis=0, keepdims=True)

    hb = tm // GRID_W

    def dq_prev(j, i):
        return (1, jnp.maximum(jnp.minimum(i, nl - 1) * hb - 1, 0), j)

    def dq_next(j, i):
        return (1, jnp.minimum((jnp.minimum(i, nl - 1) + 1) * hb, nl * hb - 1), j)

    hp, hn = _conv_halo_specs(tm, tc, nl, 4)
    (dp42, dcw), extra = _hosted_call(
        body, xch, grid=(half // tc, nt),
        in_specs=[pl.BlockSpec((2, tm, tc), lambda j, i: (0, i, j)),
                  pl.BlockSpec((1, GRID_W, tc), dq_prev), pl.BlockSpec((1, GRID_W, tc), dq_next),
                  pl.BlockSpec((4, 2, tm, tc), lambda j, i: (0, 0, i, j)), hp, hn,
                  pl.BlockSpec((3, 2, tc), lambda j, i: (0, 0, j))],
        out_specs=[pl.BlockSpec((4, 2, tm, tc), lambda j, i: (0, 0, i, j)), pl.BlockSpec((8, 2, tc), lambda j, i: (0, 0, j))],
        out_shape=[jax.ShapeDtypeStruct(p42.shape, BF), jax.ShapeDtypeStruct((8, 2, half), F32)],
        scratch=[], args=(dq3, dq3, dq3, p42, p42, p42, cw), name="l0_conv_bwd")
    return dp42, dcw, extra


def _bwd_inproj0(dp42, w_in, x, ctx, dr_l, dr_c, a2, tm, xch=None):
    l, d = x.shape
    nl, nc = l // tm, ctx.shape[0] // tm
    e = w_in.shape[1] // 4
    half = e // 2

    def body(dp_ref, w_hbm, x_ref, c_ref, dl_ref, dc_ref, a_ref, gx_ref, acc_ref, w_ref):
        i = pl.program_id(0)

        @pl.when(i == 0)
        def _():
            pltpu.sync_copy(w_hbm, w_ref)
            acc_ref[...] = jnp.zeros_like(acc_ref)

        is_ctx = i >= nl
        dh = jnp.zeros((tm, d), F32)
        for k in range(4):
            for hv in range(2):
                c0 = k * e + hv * half
                dh = dh + _dot_nt(dp_ref[k, hv], w_ref[:, c0:c0 + half])
        xv = jnp.where(is_ctx, c_ref[...], x_ref[...])
        s_sc = jnp.sum(dh * xv, axis=0, keepdims=True)
        s_sh = jnp.sum(dh, axis=0, keepdims=True)
        sel = is_ctx.astype(F32)
        acc_ref[0:1, :] += s_sc * (1.0 - sel)
        acc_ref[1:2, :] += s_sc * sel
        acc_ref[2:3, :] += s_sh * (1.0 - sel)
        acc_ref[3:4, :] += s_sh * sel

        @pl.when(jnp.logical_not(is_ctx))
        def _():
            gx_ref[...] = DN_ALPHA * dl_ref[...].astype(F32) + dh * a_ref[0:1, :]

    lat, cx = _lat_or_ctx_specs(tm, d, nl, 1, 0)
    (gx, acc), extra = _hosted_call(
        body, xch, grid=(nl + nc,),
        in_specs=[pl.BlockSpec((4, 2, tm, half), lambda i: (0, 0, i, 0)), ANY, lat, cx, lat, cx, _full((2, d))],
        out_specs=[pl.BlockSpec((tm, d), lambda i: (jnp.minimum(i, nl - 1), 0)), _full((8, d))],
        out_shape=[jax.ShapeDtypeStruct((l, d), F32), jax.ShapeDtypeStruct((8, d), F32)],
        scratch=[pltpu.VMEM(w_in.shape, BF)], args=(dp42, w_in, x, ctx, dr_l, dr_c, a2), name="l0_bwd_inproj")
    return gx, acc, extra


def _dw_inproj0(x, ctx, a2, b2, dp42, tm):
    l, d = x.shape
    lc = ctx.shape[0]
    assert lc == tm
    tl = 4 * tm if l % (4 * tm) == 0 else tm
    nl = l // tl
    half = dp42.shape[-1]
    e = 2 * half

    def body(x_ref, c_ref, a_ref, b_ref, dpl_ref, dpc_ref, o_ref, acc_ref):
        i = pl.program_id(1)

        @pl.when(i == 0)
        def _():
            acc_ref[...] = jnp.zeros_like(acc_ref)

        def add(rows_ref, dp_ref, sel):
            h = (rows_ref[...] * a_ref[sel:sel + 1, :] + b_ref[sel:sel + 1, :]).astype(BF)
            acc_ref[:, :half] += _dot_tn(h, dp_ref[0, 0])
            acc_ref[:, half:] += _dot_tn(h, dp_ref[0, 1])

        @pl.when(i < nl)
        def _():
            add(x_ref, dpl_ref, 0)

        @pl.when(i == nl)
        def _():
            add(c_ref, dpc_ref, 1)
            o_ref[...] = acc_ref[...].astype(BF)

    return pl.pallas_call(
        body, name="l0_dw_inproj", grid=(4, nl + 1),
        in_specs=[pl.BlockSpec((tl, d), lambda k, i: (jnp.minimum(i, nl - 1), 0)), _full((lc, d)),
                  _full((2, d)), _full((2, d)),
                  pl.BlockSpec((1, 2, tl, half), lambda k, i: (k, 0, jnp.minimum(i, nl - 1), 0)),
                  pl.BlockSpec((1, 2, lc, half), lambda k, i: (k, 0, l // lc, 0))],
        out_specs=pl.BlockSpec((d, e), lambda k, i: (0, k)),
        out_shape=jax.ShapeDtypeStruct((d, 4 * e), BF),
        scratch_shapes=[pltpu.VMEM((d, e), F32)], compiler_params=_cparams(),
    )(x, ctx, a2, b2, dp42, dp42)


def _dw_outproj0(q3, dr_l, dr_c, gt2, tm):
    l, d = dr_l.shape
    nl, nc = l // tm, dr_c.shape[0] // tm
    _, r, half = q3.shape
    nt = nl + nc

    def body(q_ref, dl_ref, dc_ref, g_ref, o_ref, acc_ref):
        i = pl.program_id(0)
        is_ctx = i >= nl

        @pl.when(i == 0)
        def _():
            acc_ref[...] = jnp.zeros_like(acc_ref)

        dr = jnp.where(is_ctx, dc_ref[...], dl_ref[...]).astype(F32)
        dfx = (dr * _sel_row(g_ref, is_ctx)).astype(BF)
        acc_ref[:half, :] += _dot_tn(q_ref[0], dfx)
        acc_ref[half:, :] += _dot_tn(q_ref[1], dfx)

        @pl.when(i == nt - 1)
        def _():
            o_ref[...] = acc_ref[...].astype(BF)

    lat, cx = _lat_or_ctx_specs(tm, d, nl, 1, 0)
    return pl.pallas_call(
        body, name="l0_dw_outproj", grid=(nt,),
        in_specs=[pl.BlockSpec((2, tm, half), lambda i: (0, i, 0)), lat, cx, _full((2, d))],
        out_specs=_full((2 * half, d)), out_shape=jax.ShapeDtypeStruct((2 * half, d), BF),
        scratch_shapes=[pltpu.VMEM((2 * half, d), F32)], compiler_params=_cparams(),
    )(q3, dr_l, dr_c, gt2)


def _cr_tile(j, cap=256):
    for cand in (1024, 512, 256, 128, 64, 32, 16, 8):
        if cand <= cap and j % cand == 0:
            return cand
    raise ValueError(j)


def _inproj1(xh_cr, a1, b1, w, tag):
    j, d16 = xh_cr.shape
    d = d16 // CHUNK
    e = w.shape[1] // 2
    nb = e // LANE_BLOCK
    tj = _cr_tile(j)

    def body(x_ref, a_ref, b_ref, w_hbm, u_ref, z_ref, w_ref):
        @pl.when(jnp.logical_and(pl.program_id(0) == 0, pl.program_id(1) == 0))
        def _():
            pltpu.sync_copy(w_hbm, w_ref)

        h = (x_ref[...] * a_ref[...] + b_ref[...]).astype(BF)
        r = _dot(h, w_ref[...])
        for b in range(nb):
            u_ref[b] = r[:, b * LANE_BLOCK:(b + 1) * LANE_BLOCK].astype(BF)
        z_ref[...] = r[:, e:].astype(BF)

    return pl.pallas_call(
        body, name="l1_inproj_" + tag, grid=(j // tj, CHUNK),
        in_specs=[pl.BlockSpec((tj, d), lambda t, s: (t, s)), _full((1, d)), _full((1, d)), ANY],
        out_specs=[pl.BlockSpec((nb, tj, LANE_BLOCK), lambda t, s: (0, t, s)), pl.BlockSpec((tj, e), lambda t, s: (t, s))],
        out_shape=[jax.ShapeDtypeStruct((nb, j, BCR_W), BF), jax.ShapeDtypeStruct((j, CHUNK * e), BF)],
        scratch_shapes=[pltpu.VMEM(w.shape, BF)], compiler_params=_cparams(),
    )(xh_cr, a1, b1, w)


def _bmm(a_list, w_list, trans, out_dtype, name, ctx=None):
    nb, j, ka = a_list[0].shape
    n_out = w_list[0].shape[1] if trans[0] else w_list[0].shape[2]
    tn = n_out // 2
    tj = _cr_tile(j, 512)
    n = len(a_list)
    c_idx = [i for i in range(n) if ctx is not None and ctx[i] is not None]
    c_list = [ctx[i] for i in c_idx]
    nc = len(c_list)

    def body(*refs):
        w_refs = refs[n:2 * n]

        def product(a_refs, idx):
            acc = None
            for a_ref, i in zip(a_refs, idx):
                a = a_ref[0].astype(BF)
                t = _dot_nt(a, w_refs[i][0]) if trans[i] else _dot(a, w_refs[i][0])
                acc = t if acc is None else acc + t
            return acc.astype(out_dtype)

        refs[2 * n + nc][0] = product(refs[:n], range(n))
        if nc:
            @pl.when(pl.program_id(2) == 0)
            def _():
                refs[2 * n + nc + 1][0] = product(refs[2 * n:2 * n + nc], c_idx)

    a_specs = [pl.BlockSpec((1, tj, a.shape[2]), lambda b, h, t: (b, t, 0)) for a in a_list]
    w_specs = [pl.BlockSpec((1, tn, w.shape[2]), lambda b, h, t: (b, h, 0)) if tr
               else pl.BlockSpec((1, w.shape[1], tn), lambda b, h, t: (b, 0, h)) for w, tr in zip(w_list, trans)]
    c_specs = [pl.BlockSpec((1, a.shape[1], a.shape[2]), lambda b, h, t: (b, 0, 0)) for a in c_list]
    out_specs = [pl.BlockSpec((1, tj, tn), lambda b, h, t: (b, t, h))]
    out_shape = [jax.ShapeDtypeStruct((nb, j, n_out), out_dtype)]
    if nc:
        jc = c_list[0].shape[1]
        out_specs.append(pl.BlockSpec((1, jc, tn), lambda b, h, t: (b, 0, h)))
        out_shape.append(jax.ShapeDtypeStruct((nb, jc, n_out), out_dtype))
    res = pl.pallas_call(
        body, name=name, grid=(nb, 2, j // tj), in_specs=a_specs + w_specs + c_specs,
        out_specs=out_specs, out_shape=out_shape, compiler_params=_cparams(),
    )(*a_list, *w_list, *c_list)
    return res if nc else res[0]


def _group_mask(lane_groups):
    row = lax.broadcasted_iota(jnp.int32, (LANE_BLOCK, LANE_BLOCK), 0) // S5_P
    lane = lax.broadcasted_iota(jnp.int32, (LANE_BLOCK, LANE_BLOCK), 1)
    return row == lane_groups(lane)


def _expand_toeplitz(wcomp):
    nb = wcomp.shape[0]
    nd = 2 * CHUNK - 1

    def body(c_ref, o_ref):
        mask = _group_mask(lambda lane: lane // S5_P)
        tiles = []
        for dd in range(nd):
            m = c_ref[0, dd]
            tiles.append(jnp.where(mask, jnp.concatenate([m] * GROUPS_PER_BLOCK, axis=1), 0.0).astype(BF))
        for s in range(CHUNK):
            for t in range(CHUNK):
                o_ref[0, s * LANE_BLOCK:(s + 1) * LANE_BLOCK, t * LANE_BLOCK:(t + 1) * LANE_BLOCK] = tiles[t - s + CHUNK - 1]

    return pl.pallas_call(
        body, name="l1_expand_toeplitz", grid=(nb,),
        in_specs=[pl.BlockSpec((1, nd, LANE_BLOCK, S5_P), lambda b: (b, 0, 0, 0))],
        out_specs=pl.BlockSpec((1, BCR_W, BCR_W), lambda b: (b, 0, 0)),
        out_shape=jax.ShapeDtypeStruct((nb, BCR_W, BCR_W), BF), compiler_params=_cparams(),
    )(wcomp)


def _expand_blocks(comp, name):
    nb = comp.shape[2]
    lanes_per_dir = ZL_W // 2

    def body(c_ref, o_ref):
        masks = [_group_mask(lambda lane, lb=lb: 2 * lb + lane // S5_N) for lb in range(4)]
        for r in range(2):
            for s in range(CHUNK):
                for ri in range(2):
                    m = c_ref[r, s, 0, :, ri * S5_N:(ri + 1) * S5_N]
                    mm = jnp.concatenate([m, m], axis=1)
                    for lb in range(4):
                        c0 = r * lanes_per_dir + ri * ZH + lb * LANE_BLOCK
                        o_ref[0, s * LANE_BLOCK:(s + 1) * LANE_BLOCK, c0:c0 + LANE_BLOCK] = (
                            jnp.where(masks[lb], mm, 0.0).astype(BF))

    return pl.pallas_call(
        body, name=name, grid=(nb,),
        in_specs=[pl.BlockSpec((2, CHUNK, 1, LANE_BLOCK, LANE_BLOCK), lambda b: (0, 0, b, 0, 0))],
        out_specs=pl.BlockSpec((1, BCR_W, ZL_W), lambda b: (b, 0, 0)),
        out_shape=jax.ShapeDtypeStruct((nb, BCR_W, ZL_W), BF), compiler_params=_cparams(),
    )(comp)


def _bdw(a, b_, kind, ctx, name):
    nb, j, ka = a.shape
    kb = b_.shape[2]
    tn = kb // 2
    tj = _cr_tile(j, 1024)
    nt = j // tj
    has_ctx = ctx is not None
    nd = 2 * CHUNK - 1

    def body(*refs):
        a_ref, b_ref = refs[0], refs[1]
        o_ref, acc_ref = refs[2 + 2 * has_ctx], refs[3 + 2 * has_ctx]
        h, t = pl.program_id(1), pl.program_id(2)

        @pl.when(t == 0)
        def _():
            if has_ctx:
                acc_ref[...] = _dot_tn(refs[2][0].astype(BF), refs[3][0].astype(BF))
            else:
                acc_ref[...] = jnp.zeros_like(acc_ref)

        acc_ref[...] += _dot_tn(a_ref[0].astype(BF), b_ref[0].astype(BF))

        if kind == "toeplitz":
            diag_ref = refs[4 + 2 * has_ctx]

            @pl.when(jnp.logical_and(t == 0, h == 0))
            def _():
                diag_ref[...] = jnp.zeros_like(diag_ref)

            @pl.when(t == nt - 1)
            def _():
                for s in range(CHUNK):
                    for tl in range(CHUNK // 2):
                        dd = h * (CHUNK // 2) + (tl - s + CHUNK - 1)
                        diag_ref[dd] += acc_ref[s * LANE_BLOCK:(s + 1) * LANE_BLOCK, tl * LANE_BLOCK:(tl + 1) * LANE_BLOCK]

            @pl.when(jnp.logical_and(t == nt - 1, h == 1))
            def _():
                mask = _group_mask(lambda lane: lane // S5_P)
                for dd in range(nd):
                    v = jnp.where(mask, diag_ref[dd], 0.0)
                    acc = v[:, :S5_P]
                    for k in range(1, GROUPS_PER_BLOCK):
                        acc = acc + v[:, k * S5_P:(k + 1) * S5_P]
                    o_ref[0, dd] = acc
        else:
            @pl.when(t == nt - 1)
            def _():
                masks = [_group_mask(lambda lane, lb=lb: 2 * lb + lane // S5_N) for lb in range(4)]
                for s in range(CHUNK):
                    for ri in range(2):
                        v = None
                        for lb in range(4):
                            c0 = ri * ZH + lb * LANE_BLOCK
                            blk = acc_ref[s * LANE_BLOCK:(s + 1) * LANE_BLOCK, c0:c0 + LANE_BLOCK]
                            blk = jnp.where(masks[lb], blk, 0.0)
                            v = blk if v is None else v + blk
                        o_ref[0, s, 0, :, ri * S5_N:(ri + 1) * S5_N] = v[:, :S5_N] + v[:, S5_N:]

    in_specs = [pl.BlockSpec((1, tj, ka), lambda b, h, t: (b, t, 0)), pl.BlockSpec((1, tj, tn), lambda b, h, t: (b, t, h))]
    args = [a, b_]
    if has_ctx:
        jc = ctx[0].shape[1]
        in_specs += [pl.BlockSpec((1, jc, ka), lambda b, h, t: (b, 0, 0)), pl.BlockSpec((1, jc, tn), lambda b, h, t: (b, 0, h))]
        args += list(ctx)
    scratch = [pltpu.VMEM((ka, tn), F32)]
    if kind == "toeplitz":
        ospec = pl.BlockSpec((1, nd, LANE_BLOCK, S5_P), lambda b, h, t: (b, 0, 0, 0))
        oshape = jax.ShapeDtypeStruct((nb, nd, LANE_BLOCK, S5_P), F32)
        scratch.append(pltpu.VMEM((nd, LANE_BLOCK, LANE_BLOCK), F32))
    else:
        ospec = pl.BlockSpec((1, CHUNK, 1, LANE_BLOCK, LANE_BLOCK), lambda b, h, t: (h, 0, b, 0, 0))
        oshape = jax.ShapeDtypeStruct((2, CHUNK, nb, LANE_BLOCK, LANE_BLOCK), F32)
    return pl.pallas_call(
        body, name=name, grid=(nb, 2, nt), in_specs=in_specs, out_specs=ospec, out_shape=oshape,
        scratch_shapes=scratch, compiler_params=_cparams(),
    )(*args)


def _scan(z_l, z_c, coef, chains, conj, s_l=None, s_c=None, name="l1_scan"):
    nb, jl, _ = z_l.shape
    jc = z_c.shape[1]
    with_da = s_l is not None
    sign = -1.0 if conj else 1.0
    hw = 2 * ZH

    def body(*refs):
        zl_ref, zc_ref, cf_ref = refs[:3]
        k = 3
        if with_da:
            sl_ref, sc_ref = refs[3:5]
            k = 5
        ol_ref, oc_ref = refs[k:k + 2]
        d = pl.program_id(1)
        rowi = lax.broadcasted_iota(jnp.int32, (8, ZH), 0)

        def coef_rows(r0, nr):
            return cf_ref[0, 0, r0:r0 + nr, :ZH], sign * cf_ref[0, 0, r0:r0 + nr, ZH:]

        steps = [(1, coef_rows(0, 1)), (2, coef_rows(1, 1)), (4, coef_rows(2, 1))]

        def run(chain):
            carry = (jnp.zeros((1, ZH), F32), jnp.zeros((1, ZH), F32))
            da = (jnp.zeros((8, ZH), F32), jnp.zeros((8, ZH), F32))
            for which, rev in chain:
                src, dst = (zc_ref, oc_ref) if which == "c" else (zl_ref, ol_ref)
                sref = (sc_ref if which == "c" else sl_ref) if with_da else None
                ng = (jc if which == "c" else jl) // 8
                tr, ti = coef_rows(16, 8) if rev else coef_rows(8, 8)

                def step(it, st, src=src, dst=dst, sref=sref, ng=ng, tr=tr, ti=ti, rev=rev):
                    cr_, ci_, dar, dai = st
                    g = (ng - 1 - it) if rev else it
                    off = pl.multiple_of(g * 8, 8)
                    xr = src[0, pl.ds(off, 8), :ZH]
                    xi = src[0, pl.ds(off, 8), ZH:]
                    for sh, (ar, ai) in steps:
                        if rev:
                            keep = rowi < 8 - sh
                            sr = jnp.where(keep, pltpu.roll(xr, 8 - sh, 0), 0.0)
                            si = jnp.where(keep, pltpu.roll(xi, 8 - sh, 0), 0.0)
                        else:
                            keep = rowi >= sh
                            sr = jnp.where(keep, pltpu.roll(xr, sh, 0), 0.0)
                            si = jnp.where(keep, pltpu.roll(xi, sh, 0), 0.0)
                        xr, xi = xr + ar * sr - ai * si, xi + ar * si + ai * sr
                    ir = xr + tr * cr_ - ti * ci_
                    ii = xi + tr * ci_ + ti * cr_
                    if rev:
                        er = jnp.where(rowi == 7, cr_, pltpu.roll(ir, 7, 0))
                        ei = jnp.where(rowi == 7, ci_, pltpu.roll(ii, 7, 0))
                        ncr, nci = ir[0:1], ii[0:1]
                    else:
                        er = jnp.where(rowi == 0, cr_, pltpu.roll(ir, 1, 0))
                        ei = jnp.where(rowi == 0, ci_, pltpu.roll(ii, 1, 0))
                        ncr, nci = ir[7:8], ii[7:8]
                    dst[0, pl.ds(off, 8), :ZH] = er
                    dst[0, pl.ds(off, 8), ZH:] = ei
                    if sref is not None:
                        s_r = sref[0, pl.ds(off, 8), :ZH]
                        s_i = sref[0, pl.ds(off, 8), ZH:]
                        dar = dar + s_r * er + s_i * ei
                        dai = dai + s_r * ei - s_i * er
                    return ncr, nci, dar, dai

                carry_da = lax.fori_loop(0, ng, step, (*carry, *da))
                carry, da = carry_da[:2], carry_da[2:]
            if with_da:
                refs[k + 2][0, 0] = jnp.concatenate([da[0], da[1]], axis=1)

        for dd in range(2):
            @pl.when(d == dd)
            def _(dd=dd):
                run(chains[dd])

    zspec_l = pl.BlockSpec((1, jl, hw), lambda b, d: (b, 0, d))
    zspec_c = pl.BlockSpec((1, jc, hw), lambda b, d: (b, 0, d))
    in_specs = [zspec_l, zspec_c, pl.BlockSpec((1, 1, 24, hw), lambda b, d: (b, d, 0, 0))]
    args = [z_l, z_c, coef]
    out_specs = [zspec_l, zspec_c]
    out_shape = [jax.ShapeDtypeStruct(z_l.shape, F32), jax.ShapeDtypeStruct(z_c.shape, F32)]
    if with_da:
        in_specs += [zspec_l, zspec_c]
        args += [s_l, s_c]
        out_specs.append(pl.BlockSpec((1, 1, 8, hw), lambda b, d: (b, d, 0, 0)))
        out_shape.append(jax.ShapeDtypeStruct((nb, 2, 8, hw), F32))
    return pl.pallas_call(body, name=name, grid=(nb, 2), in_specs=in_specs, out_specs=out_specs,
                          out_shape=out_shape, compiler_params=_cparams())(*args)


def _glu_fwd(y_bcr, z_cr, w_glu, b_glu):
    nb, j, _ = y_bcr.shape
    e = nb * LANE_BLOCK
    tj = _cr_tile(j)

    def body(y_ref, z_ref, w_hbm, b_ref, o_ref, sg_ref, w_ref):
        @pl.when(jnp.logical_and(pl.program_id(0) == 0, pl.program_id(1) == 0))
        def _():
            pltpu.sync_copy(w_hbm, w_ref)

        y = jnp.concatenate([y_ref[b] for b in range(nb)], axis=1).astype(F32)
        g = _gelu_parts(y)[0]
        sg = _sigmoid(_dot(g.astype(BF), w_ref[...]) + b_ref[...])
        z = z_ref[...].astype(F32)
        o_ref[...] = (g * sg * (z * _sigmoid(z))).astype(BF)
        sg_ref[...] = sg.astype(BF)

    tok = pl.BlockSpec((tj, e), lambda t, s: (t, s))
    return pl.pallas_call(
        body, name="l1_glu_fwd", grid=(j // tj, CHUNK),
        in_specs=[pl.BlockSpec((nb, tj, LANE_BLOCK), lambda t, s: (0, t, s)), tok, ANY, _full((1, e))],
        out_specs=[tok, tok],
        out_shape=[jax.ShapeDtypeStruct((j, CHUNK * e), BF), jax.ShapeDtypeStruct((j, CHUNK * e), BF)],
        scratch_shapes=[pltpu.VMEM(w_glu.shape, BF)], compiler_params=_cparams(),
    )(y_bcr, z_cr, w_glu, b_glu)


def _final(w_cr, w_out, xh_cr, tgt_cr, vecs):
    j, e16 = w_cr.shape
    e = e16 // CHUNK
    d = w_out.shape[1]
    tj = _cr_tile(j)

    def body(w_ref, wo_hbm, xh_ref, t_ref, v_ref, dr_ref, acc_ref, wo_ref):
        @pl.when(jnp.logical_and(pl.program_id(0) == 0, pl.program_id(1) == 0))
        def _():
            pltpu.sync_copy(wo_hbm, wo_ref)
            acc_ref[...] = jnp.zeros_like(acc_ref)

        o = _dot(w_ref[...], wo_ref[...])
        x1 = xh_ref[...] * v_ref[0:1, :] + v_ref[1:2, :]
        rr = DN_ALPHA * x1 + v_ref[2:3, :] * o
        mu = jnp.mean(rr, axis=-1, keepdims=True)
        cen = rr - mu
        rstd = lax.rsqrt(jnp.mean(cen * cen, axis=-1, keepdims=True) + LN_EPS)
        xh2 = cen * rstd
        err = xh2 * v_ref[3:4, :] + v_ref[4:5, :] - t_ref[...]
        dy = err * (1.0 / d)
        dxh = dy * v_ref[3:4, :]
        dr = rstd * (dxh - jnp.mean(dxh, axis=-1, keepdims=True) - xh2 * jnp.mean(dxh * xh2, axis=-1, keepdims=True))
        dr_ref[...] = dr.astype(BF)
        acc_ref[0:1, :] += jnp.sum(dy * xh2, axis=0, keepdims=True)
        acc_ref[1:2, :] += jnp.sum(dy, axis=0, keepdims=True)
        acc_ref[2:3, :] += jnp.sum(dr * o, axis=0, keepdims=True)
        acc_ref[3:4, :] += (0.5 / d) * jnp.sum(err * err, axis=0, keepdims=True)

    tok_d = pl.BlockSpec((tj, d), lambda t, s: (t, s))
    return pl.pallas_call(
        body, name="l1_final", grid=(j // tj, CHUNK),
        in_specs=[pl.BlockSpec((tj, e), lambda t, s: (t, s)), ANY, tok_d, tok_d, _full((8, d))],
        out_specs=[tok_d, _full((8, d))],
        out_shape=[jax.ShapeDtypeStruct((j, CHUNK * d), BF), jax.ShapeDtypeStruct((8, d), F32)],
        scratch_shapes=[pltpu.VMEM(w_out.shape, BF)], compiler_params=_cparams(),
    )(w_cr, w_out, xh_cr, tgt_cr, vecs)


def _glu_bwd(dr_cr, gt1, w_out, w_glu, y_bcr, z_cr, sg_cr):
    nb, j, _ = y_bcr.shape
    e, d = w_out.shape
    tj = _cr_tile(j)

    def body(dr_ref, g_ref, wo_hbm, wg_hbm, y_ref, z_ref, sg_ref, dz_ref, dt_ref, dy_ref, wo_ref, wg_ref):
        @pl.when(jnp.logical_and(pl.program_id(0) == 0, pl.program_id(1) == 0))
        def _():
            pltpu.sync_copy(wo_hbm, wo_ref)
            pltpu.sync_copy(wg_hbm, wg_ref)

        do = (dr_ref[...].astype(F32) * g_ref[...]).astype(BF)
        dw = _dot_nt(do, wo_ref[...])
        y = jnp.concatenate([y_ref[b] for b in range(nb)], axis=1).astype(F32)
        g, dgel = _gelu_parts(y)
        z = z_ref[...].astype(F32)
        sz = _sigmoid(z)
        sg = sg_ref[...].astype(F32)
        dg2 = dw * (z * sz)
        dz_ref[...] = (dw * g * sg * (sz * (1.0 + z * (1.0 - sz)))).astype(BF)
        dt = (dg2 * g * sg * (1.0 - sg)).astype(BF)
        dt_ref[...] = dt
        dy = (dg2 * sg + _dot_nt(dt, wg_ref[...])) * dgel
        for b in range(nb):
            dy_ref[b] = dy[:, b * LANE_BLOCK:(b + 1) * LANE_BLOCK].astype(BF)

    tok_e = pl.BlockSpec((tj, e), lambda t, s: (t, s))
    blk = pl.BlockSpec((nb, tj, LANE_BLOCK), lambda t, s: (0, t, s))
    return pl.pallas_call(
        body, name="l1_glu_bwd", grid=(j // tj, CHUNK),
        in_specs=[pl.BlockSpec((tj, d), lambda t, s: (t, s)), _full((1, d)), ANY, ANY, blk, tok_e, tok_e],
        out_specs=[tok_e, tok_e, blk],
        out_shape=[jax.ShapeDtypeStruct((j, CHUNK * e), BF), jax.ShapeDtypeStruct((j, CHUNK * e), BF),
                   jax.ShapeDtypeStruct((nb, j, BCR_W), BF)],
        scratch_shapes=[pltpu.VMEM(w_out.shape, BF), pltpu.VMEM(w_glu.shape, BF)], compiler_params=_cparams(),
    )(dr_cr, gt1, w_out, w_glu, y_bcr, z_cr, sg_cr)


def _bwd_inproj1(du_bcr, dz_cr, w, xh_cr, rs_cr, dr2_cr, vecs, tag):
    nb, j, _ = du_bcr.shape
    d = w.shape[0]
    e = w.shape[1] // 2
    tj = _cr_tile(j)

    def body(du_ref, dz_ref, w_hbm, xh_ref, rs_ref, dr2_ref, v_ref, dr1_ref, acc_ref, w_ref):
        @pl.when(jnp.logical_and(pl.program_id(0) == 0, pl.program_id(1) == 0))
        def _():
            pltpu.sync_copy(w_hbm, w_ref)
            acc_ref[...] = jnp.zeros_like(acc_ref)

        du = jnp.concatenate([du_ref[b] for b in range(nb)], axis=1)
        dh = _dot_nt(du, w_ref[:, :e]) + _dot_nt(dz_ref[...], w_ref[:, e:])
        xh = xh_ref[...]
        x1 = xh * v_ref[0:1, :] + v_ref[1:2, :]
        dx1 = DN_ALPHA * dr2_ref[...].astype(F32) + dh * v_ref[2:3, :]
        dxh = dx1 * v_ref[0:1, :]
        rstd = rs_ref[:, 0:1]
        dr1 = rstd * (dxh - jnp.mean(dxh, axis=-1, keepdims=True) - xh * jnp.mean(dxh * xh, axis=-1, keepdims=True))
        dr1_ref[...] = dr1.astype(BF)
        acc_ref[0:1, :] += jnp.sum(dh * x1, axis=0, keepdims=True)
        acc_ref[1:2, :] += jnp.sum(dh, axis=0, keepdims=True)
        acc_ref[2:3, :] += jnp.sum(dx1 * xh, axis=0, keepdims=True)
        acc_ref[3:4, :] += jnp.sum(dx1, axis=0, keepdims=True)

    tok_d = pl.BlockSpec((tj, d), lambda t, s: (t, s))
    return pl.pallas_call(
        body, name="l1_bwd_inproj_" + tag, grid=(j // tj, CHUNK),
        in_specs=[pl.BlockSpec((nb, tj, LANE_BLOCK), lambda t, s: (0, t, s)), pl.BlockSpec((tj, e), lambda t, s: (t, s)),
                  ANY, tok_d, pl.BlockSpec((tj, 128), lambda t, s: (t, s)), tok_d, _full((8, d))],
        out_specs=[tok_d, _full((8, d))],
        out_shape=[jax.ShapeDtypeStruct((j, CHUNK * d), BF), jax.ShapeDtypeStruct((8, d), F32)],
        scratch_shapes=[pltpu.VMEM(w.shape, BF)], compiler_params=_cparams(),
    )(du_bcr, dz_cr, w, xh_cr, rs_cr, dr2_cr, vecs)


def _dw_cr(lhs, rhs, lhs_kind, rhs_kind, vec, bias_sum, init, name):
    if lhs_kind == "gelu_bcr":
        nb_l, j, _ = lhs.shape
        k = nb_l * LANE_BLOCK
    else:
        j = lhs.shape[0]
        k = lhs.shape[1] // CHUNK
    if rhs_kind == "bcr":
        nb_r = rhs.shape[0]
        n = nb_r * LANE_BLOCK
    else:
        n = rhs.shape[1] // CHUNK
    tj = _cr_tile(j, 512)
    nh = 2 if k * n * 4 > (8 << 20) else 1
    tn = n // nh
    nbh = tn // LANE_BLOCK
    nt = j // tj
    has_init = init is not None

    def body(*refs):
        refs = list(refs)
        l_ref, r_ref = refs[0], refs[1]
        pos = 2
        v_ref = None
        if vec is not None:
            v_ref = refs[pos]
            pos += 1
        i_ref = None
        if has_init:
            i_ref = refs[pos]
            pos += 1
        o_ref = refs[pos]
        pos += 1
        bs_ref = None
        if bias_sum:
            bs_ref = refs[pos]
            pos += 1
        acc_ref = refs[pos]
        t, s = pl.program_id(1), pl.program_id(2)
        first = jnp.logical_and(t == 0, s == 0)

        @pl.when(first)
        def _():
            acc_ref[...] = i_ref[...] if has_init else jnp.zeros_like(acc_ref)
            if bias_sum:
                bs_ref[...] = jnp.zeros_like(bs_ref)

        if lhs_kind == "gelu_bcr":
            y = jnp.concatenate([l_ref[b] for b in range(nb_l)], axis=1).astype(F32)
            lv = _gelu_parts(y)[0].astype(BF)
        elif lhs_kind == "mod":
            lv = (l_ref[...] * v_ref[0:1, :] + v_ref[1:2, :]).astype(BF)
        else:
            lv = l_ref[...]
        if rhs_kind == "bcr":
            rv = jnp.concatenate([r_ref[b] for b in range(nbh)], axis=1)
        elif rhs_kind == "scaled":
            rv = (r_ref[...].astype(F32) * v_ref[0:1, :]).astype(BF)
        else:
            rv = r_ref[...]
        acc_ref[...] += _dot_tn(lv, rv)
        if bias_sum:
            bs_ref[0:1, :] += jnp.sum(rv.astype(F32), axis=0, keepdims=True)

        @pl.when(jnp.logical_and(t == nt - 1, s == CHUNK - 1))
        def _():
            o_ref[...] = acc_ref[...].astype(BF)

    if lhs_kind == "gelu_bcr":
        l_spec = pl.BlockSpec((nb_l, tj, LANE_BLOCK), lambda h, t, s: (0, t, s))
    else:
        l_spec = pl.BlockSpec((tj, k), lambda h, t, s: (t, s))
    if rhs_kind == "bcr":
        r_spec = pl.BlockSpec((nbh, tj, LANE_BLOCK), lambda h, t, s: (h, t, s))
    else:
        r_spec = pl.BlockSpec((tj, tn), lambda h, t, s: (t, s * nh + h))
    in_specs, args = [l_spec, r_spec], [lhs, rhs]
    if vec is not None:
        in_specs.append(_full(vec.shape))
        args.append(vec)
    o_spec = pl.BlockSpec((k, tn), lambda h, t, s: (0, h))
    if has_init:
        in_specs.append(o_spec)
        args.append(init)
    out_specs, out_shape = [o_spec], [jax.ShapeDtypeStruct((k, n), BF)]
    if bias_sum:
        out_specs.append(pl.BlockSpec((8, tn), lambda h, t, s: (0, h)))
        out_shape.append(jax.ShapeDtypeStruct((8, n), F32))
    res = pl.pallas_call(
        body, name=name, grid=(nh, nt, CHUNK), in_specs=in_specs, out_specs=out_specs, out_shape=out_shape,
        scratch_shapes=[pltpu.VMEM((k, tn), F32)], compiler_params=_cparams(),
    )(*args)
    return res if bias_sum else res[0]


def _dw_cr_f32(lhs, rhs, vec, name):
    j = lhs.shape[0]
    k = lhs.shape[1] // CHUNK
    nb_r = rhs.shape[0]
    n = nb_r * LANE_BLOCK
    tj = _cr_tile(j)
    nt = j // tj

    def body(l_ref, r_ref, v_ref, o_ref):
        @pl.when(jnp.logical_and(pl.program_id(0) == 0, pl.program_id(1) == 0))
        def _():
            o_ref[...] = jnp.zeros_like(o_ref)

        lv = (l_ref[...] * v_ref[0:1, :] + v_ref[1:2, :]).astype(BF)
        rv = jnp.concatenate([r_ref[b] for b in range(nb_r)], axis=1)
        o_ref[...] += _dot_tn(lv, rv)

    return pl.pallas_call(
        body, name=name, grid=(nt, CHUNK),
        in_specs=[pl.BlockSpec((tj, k), lambda t, s: (t, s)), pl.BlockSpec((nb_r, tj, LANE_BLOCK), lambda t, s: (0, t, s)),
                  _full(vec.shape)],
        out_specs=_full((k, n)), out_shape=jax.ShapeDtypeStruct((k, n), F32), compiler_params=_cparams(),
    )(lhs, rhs, vec)


GT_ROWS = CHUNK * S5_P
ZG_W = 2 * 2 * S5_N
PAIR_W = 2 * ZG_W
GROUPS_PER_STEP = 4


def _inproj1_gt(xh_cr, a1, b1, wu_t, w_z, tag):
    j, d16 = xh_cr.shape
    d = d16 // CHUNK
    e = wu_t.shape[0]
    g = e // S5_P
    tj = _cr_tile(j, 256)

    def body(x_ref, a_ref, b_ref, wu_hbm, wz_hbm, u_ref, z_ref, wu_ref, wz_ref):
        @pl.when(jnp.logical_and(pl.program_id(0) == 0, pl.program_id(1) == 0))
        def _():
            pltpu.sync_copy(wu_hbm, wu_ref)
            pltpu.sync_copy(wz_hbm, wz_ref)

        h = (x_ref[...] * a_ref[...] + b_ref[...]).astype(BF)
        u_ref[...] = _dot_nt(wu_ref[...], h).reshape(g, S5_P, tj).astype(BF)
        z_ref[...] = _dot(h, wz_ref[...]).astype(BF)

    return pl.pallas_call(
        body, name="l1_inproj_" + tag, grid=(j // tj, CHUNK),
        in_specs=[pl.BlockSpec((tj, d), lambda t, s: (t, s)), _full((1, d)), _full((1, d)), ANY, ANY],
        out_specs=[pl.BlockSpec((g, S5_P, tj), lambda t, s: (0, s, t)), pl.BlockSpec((tj, e), lambda t, s: (t, s))],
        out_shape=[jax.ShapeDtypeStruct((g, GT_ROWS, j), BF), jax.ShapeDtypeStruct((j, CHUNK * e), BF)],
        scratch_shapes=[pltpu.VMEM(wu_t.shape, BF), pltpu.VMEM(w_z.shape, BF)], compiler_params=_cparams(),
    )(xh_cr, a1, b1, wu_t, w_z)


def _gt_spec(j, gb=GROUPS_PER_STEP):
    return pl.BlockSpec((gb, GT_ROWS, j), lambda i: (i, 0, 0))


def _zg_spec(j, gb=GROUPS_PER_STEP):
    return pl.BlockSpec((j, gb * ZG_W), lambda i: (0, i))


def _w_spec(width, gb=GROUPS_PER_STEP):
    return pl.BlockSpec((gb, GT_ROWS, width), lambda i: (i, 0, 0))


def _pair_lanes(k):
    return slice((k // 2) * PAIR_W, (k // 2 + 1) * PAIR_W)


def _s5_z(ut_l, ut_c, bc):
    g, _, jl = ut_l.shape
    jc = ut_c.shape[2]
    gb = GROUPS_PER_STEP

    def body(ul_ref, uc_ref, bc_ref, zl_ref, zc_ref):
        for k in range(0, gb, 2):
            zl_ref[:, _pair_lanes(k)] = _dot_tn(ul_ref[k], bc_ref[k]) + _dot_tn(ul_ref[k + 1], bc_ref[k + 1])
            zc_ref[:, _pair_lanes(k)] = _dot_tn(uc_ref[k], bc_ref[k]) + _dot_tn(uc_ref[k + 1], bc_ref[k + 1])

    return pl.pallas_call(
        body, name="l1_s5_z", grid=(g // gb,), in_specs=[_gt_spec(jl), _gt_spec(jc), _w_spec(PAIR_W)],
        out_specs=[_zg_spec(jl), _zg_spec(jc)],
        out_shape=[jax.ShapeDtypeStruct((jl, g * ZG_W), F32), jax.ShapeDtypeStruct((jc, g * ZG_W), F32)],
        compiler_params=_cparams(),
    )(ut_l, ut_c, bc)


def _s5_y(ut_l, s_l, mt_t, cct):
    g, _, jl = ut_l.shape
    gb = GROUPS_PER_STEP

    def body(u_ref, s_ref, mt_ref, cc_ref, y_ref):
        for k in range(gb):
            s_k = s_ref[:, _pair_lanes(k)].astype(BF)
            y_ref[k] = (_dot(mt_ref[k], u_ref[k]) + _dot_nt(cc_ref[k], s_k)).astype(BF)

    return pl.pallas_call(
        body, name="l1_s5_y", grid=(g // gb,),
        in_specs=[_gt_spec(jl), _zg_spec(jl), _w_spec(GT_ROWS), _w_spec(PAIR_W)],
        out_specs=_gt_spec(jl), out_shape=jax.ShapeDtypeStruct((g, GT_ROWS, jl), BF), compiler_params=_cparams(),
    )(ut_l, s_l, mt_t, cct)


def _s5_ds(dyt_l, cct):
    g, _, jl = dyt_l.shape
    gb = GROUPS_PER_STEP

    def body(dy_ref, cc_ref, ds_ref):
        for k in range(0, gb, 2):
            ds_ref[:, _pair_lanes(k)] = _dot_tn(dy_ref[k], cc_ref[k]) + _dot_tn(dy_ref[k + 1], cc_ref[k + 1])

    return pl.pallas_call(
        body, name="l1_s5_ds", grid=(g // gb,), in_specs=[_gt_spec(jl), _w_spec(PAIR_W)], out_specs=_zg_spec(jl),
        out_shape=jax.ShapeDtypeStruct((jl, g * ZG_W), F32), compiler_params=_cparams(),
    )(dyt_l, cct)


def _s5_dx(dyt_l, dz_l, dz_c, mt, bc):
    g, _, jl = dyt_l.shape
    jc = dz_c.shape[0]
    gb = GROUPS_PER_STEP

    def body(dy_ref, dzl_ref, dzc_ref, mt_ref, bc_ref, dul_ref, duc_ref):
        for k in range(gb):
            dzl = dzl_ref[:, _pair_lanes(k)].astype(BF)
            dzc = dzc_ref[:, _pair_lanes(k)].astype(BF)
            dul_ref[k] = (_dot(mt_ref[k], dy_ref[k]) + _dot_nt(bc_ref[k], dzl)).astype(BF)
            duc_ref[k] = _dot_nt(bc_ref[k], dzc).astype(BF)

    return pl.pallas_call(
        body, name="l1_s5_dx", grid=(g // gb,),
        in_specs=[_gt_spec(jl), _zg_spec(jl), _zg_spec(jc), _w_spec(GT_ROWS), _w_spec(PAIR_W)],
        out_specs=[_gt_spec(jl), _gt_spec(jc)],
        out_shape=[jax.ShapeDtypeStruct((g, GT_ROWS, jl), BF), jax.ShapeDtypeStruct((g, GT_ROWS, jc), BF)],
        compiler_params=_cparams(),
    )(dyt_l, dz_l, dz_c, mt, bc)


def _s5_dw(ut_l, ut_c, dyt_l, dz_l, dz_c, s_l):
    g, _, jl = ut_l.shape
    jc = ut_c.shape[2]
    gb = GROUPS_PER_STEP

    def body(ul_ref, uc_ref, dy_ref, dzl_ref, dzc_ref, s_ref, dmt_ref, dbc_ref, dcc_ref):
        for k in range(gb):
            lanes = _pair_lanes(k)
            dmt_ref[k] = _dot_nt(ul_ref[k], dy_ref[k])
            dbc_ref[k] = (_dot(ul_ref[k], dzl_ref[:, lanes].astype(BF))
                          + _dot(uc_ref[k], dzc_ref[:, lanes].astype(BF)))
            dcc_ref[k] = _dot(dy_ref[k], s_ref[:, lanes].astype(BF))

    sd_m = jax.ShapeDtypeStruct((g, GT_ROWS, GT_ROWS), F32)
    sd_p = jax.ShapeDtypeStruct((g, GT_ROWS, PAIR_W), F32)
    return pl.pallas_call(
        body, name="l1_s5_dw", grid=(g // gb,),
        in_specs=[_gt_spec(jl), _gt_spec(jc), _gt_spec(jl), _zg_spec(jl), _zg_spec(jc), _zg_spec(jl)],
        out_specs=[_w_spec(GT_ROWS), _w_spec(PAIR_W), _w_spec(PAIR_W)], out_shape=[sd_m, sd_p, sd_p],
        compiler_params=_cparams(),
    )(ut_l, ut_c, dyt_l, dz_l, dz_c, s_l)


def _scan_g(z_l, z_c, coef, chains, conj, s_l=None, s_c=None, name="l1_scan"):
    jl, w_all = z_l.shape
    jc = z_c.shape[0]
    gb = GROUPS_PER_STEP
    wb = gb * ZG_W
    nch = wb // 256
    with_da = s_l is not None
    sign = -1.0 if conj else 1.0

    def body(*refs):
        zl_ref, zc_ref, cf_ref = refs[:3]
        k0 = 3
        if with_da:
            sl_ref, sc_ref = refs[3:5]
            k0 = 5
        ol_ref, oc_ref = refs[k0:k0 + 2]
        rowi = lax.broadcasted_iota(jnp.int32, (8, 128), 0)

        def lanes_of(ch):
            return slice(ch * 256, ch * 256 + 128), slice(ch * 256 + 128, (ch + 1) * 256)

        def coefs(ch, r0, nr):
            lr, li = lanes_of(ch)
            return cf_ref[r0:r0 + nr, lr], sign * cf_ref[r0:r0 + nr, li]

        def shift(v, sh, rev):
            if rev:
                return jnp.where(rowi < 8 - sh, pltpu.roll(v, 8 - sh, 0), 0.0)
            return jnp.where(rowi >= sh, pltpu.roll(v, sh, 0), 0.0)

        zero_row = jnp.zeros((1, 128), F32)
        zero_tile = jnp.zeros((8, 128), F32)
        carry = [zero_row] * (2 * nch)
        da = [zero_tile] * (2 * nch)
        for seg in range(len(chains[0])):
            which = chains[0][seg][0]
            assert chains[1][seg][0] == which
            revs = (chains[0][seg][1], chains[1][seg][1])
            src, dst = (zc_ref, oc_ref) if which == "c" else (zl_ref, ol_ref)
            sref = ((sc_ref if which == "c" else sl_ref) if with_da else None)
            ng = (jc if which == "c" else jl) // 8

            def step(it, st, src=src, dst=dst, sref=sref, ng=ng, revs=revs):
                carry_, da_ = list(st[:2 * nch]), list(st[2 * nch:])
                for ch in range(nch):
                    rev = revs[ch % 2]
                    lr, li = lanes_of(ch)
                    grp = (ng - 1 - it) if rev else it
                    off = pl.multiple_of(grp * 8, 8)
                    xr, xi = src[pl.ds(off, 8), lr], src[pl.ds(off, 8), li]
                    for sh, r0 in ((1, 0), (2, 1), (4, 2)):
                        ar, ai = coefs(ch, r0, 1)
                        sr, si = shift(xr, sh, rev), shift(xi, sh, rev)
                        xr, xi = xr + ar * sr - ai * si, xi + ar * si + ai * sr
                    tr, ti = coefs(ch, 16, 8) if rev else coefs(ch, 8, 8)
                    cr_, ci_ = carry_[2 * ch], carry_[2 * ch + 1]
                    ir = xr + tr * cr_ - ti * ci_
                    ii = xi + tr * ci_ + ti * cr_
                    if rev:
                        er = jnp.where(rowi == 7, cr_, pltpu.roll(ir, 7, 0))
                        ei = jnp.where(rowi == 7, ci_, pltpu.roll(ii, 7, 0))
                        carry_[2 * ch], carry_[2 * ch + 1] = ir[0:1], ii[0:1]
                    else:
                        er = jnp.where(rowi == 0, cr_, pltpu.roll(ir, 1, 0))
                        ei = jnp.where(rowi == 0, ci_, pltpu.roll(ii, 1, 0))
                        carry_[2 * ch], carry_[2 * ch + 1] = ir[7:8], ii[7:8]
                    dst[pl.ds(off, 8), lr] = er
                    dst[pl.ds(off, 8), li] = ei
                    if sref is not None:
                        s_r, s_i = sref[pl.ds(off, 8), lr], sref[pl.ds(off, 8), li]
                        da_[2 * ch] = da_[2 * ch] + s_r * er + s_i * ei
                        da_[2 * ch + 1] = da_[2 * ch + 1] + s_r * ei - s_i * er
                return (*carry_, *da_)

            st = lax.fori_loop(0, ng, step, (*carry, *da))
            carry, da = list(st[:2 * nch]), list(st[2 * nch:])
        if with_da:
            da_ref = refs[k0 + 2]
            for ch in range(nch):
                lr, li = lanes_of(ch)
                da_ref[:, lr] = da[2 * ch]
                da_ref[:, li] = da[2 * ch + 1]

    in_specs = [_zg_spec(jl), _zg_spec(jc), pl.BlockSpec((24, wb), lambda i: (0, i))]
    args = [z_l, z_c, coef]
    out_specs = [_zg_spec(jl), _zg_spec(jc)]
    out_shape = [jax.ShapeDtypeStruct(z_l.shape, F32), jax.ShapeDtypeStruct(z_c.shape, F32)]
    if with_da:
        in_specs += [_zg_spec(jl), _zg_spec(jc)]
        args += [s_l, s_c]
        out_specs.append(pl.BlockSpec((8, wb), lambda i: (0, i)))
        out_shape.append(jax.ShapeDtypeStruct((8, w_all), F32))
    return pl.pallas_call(body, name=name, grid=(w_all // wb,), in_specs=in_specs, out_specs=out_specs,
                          out_shape=out_shape, compiler_params=_cparams())(*args)


def _gt_tok_spec(g, tj):
    return pl.BlockSpec((g, S5_P, tj), lambda t, s: (0, s, t))


def _glu_fwd_gt(yt, z_cr, w_glu, b_glu):
    g, _, j = yt.shape
    e = g * S5_P
    tj = _cr_tile(j)

    def body(y_ref, z_ref, w_hbm, b_ref, o_ref, sg_ref, w_ref):
        @pl.when(jnp.logical_and(pl.program_id(0) == 0, pl.program_id(1) == 0))
        def _():
            pltpu.sync_copy(w_hbm, w_ref)

        y = jnp.transpose(y_ref[...].reshape(e, tj).astype(F32))
        gl = _gelu_parts(y)[0]
        sg = _sigmoid(_dot(gl.astype(BF), w_ref[...]) + b_ref[...])
        z = z_ref[...].astype(F32)
        o_ref[...] = (gl * sg * (z * _sigmoid(z))).astype(BF)
        sg_ref[...] = sg.astype(BF)

    tok = pl.BlockSpec((tj, e), lambda t, s: (t, s))
    return pl.pallas_call(
        body, name="l1_glu_fwd", grid=(j // tj, CHUNK),
        in_specs=[_gt_tok_spec(g, tj), tok, ANY, _full((1, e))], out_specs=[tok, tok],
        out_shape=[jax.ShapeDtypeStruct((j, CHUNK * e), BF), jax.ShapeDtypeStruct((j, CHUNK * e), BF)],
        scratch_shapes=[pltpu.VMEM(w_glu.shape, BF)], compiler_params=_cparams(),
    )(yt, z_cr, w_glu, b_glu)


def _glu_bwd_gt(dr_cr, gt1, w_out, w_glu, yt, z_cr, sg_cr):
    g, _, j = yt.shape
    e, d = w_out.shape
    tj = _cr_tile(j)

    def body(dr_ref, g_ref, wo_hbm, wg_hbm, y_ref, z_ref, sg_ref, dz_ref, dt_ref, dy_ref, wo_ref, wg_ref):
        @pl.when(jnp.logical_and(pl.program_id(0) == 0, pl.program_id(1) == 0))
        def _():
            pltpu.sync_copy(wo_hbm, wo_ref)
            pltpu.sync_copy(wg_hbm, wg_ref)

        do = (dr_ref[...].astype(F32) * g_ref[...]).astype(BF)
        dw = _dot_nt(do, wo_ref[...])
        y = jnp.transpose(y_ref[...].reshape(e, tj).astype(F32))
        gl, dgel = _gelu_parts(y)
        z = z_ref[...].astype(F32)
        sz = _sigmoid(z)
        sg = sg_ref[...].astype(F32)
        dg2 = dw * (z * sz)
        dz_ref[...] = (dw * gl * sg * (sz * (1.0 + z * (1.0 - sz)))).astype(BF)
        dt = (dg2 * gl * sg * (1.0 - sg)).astype(BF)
        dt_ref[...] = dt
        dy = (dg2 * sg + _dot_nt(dt, wg_ref[...])) * dgel
        dy_ref[...] = jnp.transpose(dy).reshape(g, S5_P, tj).astype(BF)

    tok_e = pl.BlockSpec((tj, e), lambda t, s: (t, s))
    return pl.pallas_call(
        body, name="l1_glu_bwd", grid=(j // tj, CHUNK),
        in_specs=[pl.BlockSpec((tj, d), lambda t, s: (t, s)), _full((1, d)), ANY, ANY, _gt_tok_spec(g, tj), tok_e, tok_e],
        out_specs=[tok_e, tok_e, _gt_tok_spec(g, tj)],
        out_shape=[jax.ShapeDtypeStruct((j, CHUNK * e), BF), jax.ShapeDtypeStruct((j, CHUNK * e), BF),
                   jax.ShapeDtypeStruct((g, GT_ROWS, j), BF)],
        scratch_shapes=[pltpu.VMEM(w_out.shape, BF), pltpu.VMEM(w_glu.shape, BF)], compiler_params=_cparams(),
    )(dr_cr, gt1, w_out, w_glu, yt, z_cr, sg_cr)


def _bwd_inproj1_gt(dut, dz_cr, wu_t, w_z, xh_cr, rs_cr, dr2_cr, vecs, tag):
    g, _, j = dut.shape
    e, d = wu_t.shape
    tj = _cr_tile(j)

    def body(du_ref, dz_ref, wu_hbm, wz_hbm, xh_ref, rs_ref, dr2_ref, v_ref, dr1_ref, acc_ref, wu_ref, wz_ref):
        @pl.when(jnp.logical_and(pl.program_id(0) == 0, pl.program_id(1) == 0))
        def _():
            pltpu.sync_copy(wu_hbm, wu_ref)
            pltpu.sync_copy(wz_hbm, wz_ref)
            acc_ref[...] = jnp.zeros_like(acc_ref)

        dh = _dot_tn(du_ref[...].reshape(e, tj), wu_ref[...]) + _dot_nt(dz_ref[...], wz_ref[...])
        xh = xh_ref[...]
        x1 = xh * v_ref[0:1, :] + v_ref[1:2, :]
        dx1 = DN_ALPHA * dr2_ref[...].astype(F32) + dh * v_ref[2:3, :]
        dxh = dx1 * v_ref[0:1, :]
        rstd = rs_ref[:, 0:1]
        dr1 = rstd * (dxh - jnp.mean(dxh, axis=-1, keepdims=True) - xh * jnp.mean(dxh * xh, axis=-1, keepdims=True))
        dr1_ref[...] = dr1.astype(BF)
        acc_ref[0:1, :] += jnp.sum(dh * x1, axis=0, keepdims=True)
        acc_ref[1:2, :] += jnp.sum(dh, axis=0, keepdims=True)
        acc_ref[2:3, :] += jnp.sum(dx1 * xh, axis=0, keepdims=True)
        acc_ref[3:4, :] += jnp.sum(dx1, axis=0, keepdims=True)

    tok_d = pl.BlockSpec((tj, d), lambda t, s: (t, s))
    return pl.pallas_call(
        body, name="l1_bwd_inproj_" + tag, grid=(j // tj, CHUNK),
        in_specs=[_gt_tok_spec(g, tj), pl.BlockSpec((tj, e), lambda t, s: (t, s)), ANY, ANY, tok_d,
                  pl.BlockSpec((tj, 128), lambda t, s: (t, s)), tok_d, _full((8, d))],
        out_specs=[tok_d, _full((8, d))],
        out_shape=[jax.ShapeDtypeStruct((j, CHUNK * d), BF), jax.ShapeDtypeStruct((8, d), F32)],
        scratch_shapes=[pltpu.VMEM(wu_t.shape, BF), pltpu.VMEM(w_z.shape, BF)], compiler_params=_cparams(),
    )(dut, dz_cr, wu_t, w_z, xh_cr, rs_cr, dr2_cr, vecs)


def _dw_gt(lhs_gt, rhs_cr, lhs_gelu, vec, bias_sum, init, out_dtype, name):
    g, _, j = lhs_gt.shape
    e = g * S5_P
    n = rhs_cr.shape[1] // CHUNK
    tj = _cr_tile(j, 512 if j % 512 == 0 else 256)
    nh = 2 if e * n * 4 > (8 << 20) else 1
    tn = n // nh
    nt = j // tj
    has_init = init is not None

    def body(*refs):
        refs = list(refs)
        l_ref, r_ref = refs[0], refs[1]
        pos = 2
        v_ref = i_ref = bs_ref = None
        if vec is not None:
            v_ref = refs[pos]
            pos += 1
        if has_init:
            i_ref = refs[pos]
            pos += 1
        o_ref = refs[pos]
        pos += 1
        if bias_sum:
            bs_ref = refs[pos]
            pos += 1
        acc_ref = refs[pos]
        t, s = pl.program_id(1), pl.program_id(2)

        @pl.when(jnp.logical_and(t == 0, s == 0))
        def _():
            acc_ref[...] = i_ref[...] if has_init else jnp.zeros_like(acc_ref)
            if bias_sum:
                bs_ref[...] = jnp.zeros_like(bs_ref)

        lv = l_ref[...].reshape(e, tj)
        if lhs_gelu:
            lv = _gelu_parts(lv.astype(F32))[0].astype(BF)
        if vec is not None:
            rv = (r_ref[...] * v_ref[0:1, :] + v_ref[1:2, :]).astype(BF)
        else:
            rv = r_ref[...]
        acc_ref[...] += _dot(lv, rv)
        if bias_sum:
            bs_ref[0:1, :] += jnp.sum(rv.astype(F32), axis=0, keepdims=True)

        @pl.when(jnp.logical_and(t == nt - 1, s == CHUNK - 1))
        def _():
            o_ref[...] = acc_ref[...].astype(out_dtype)

    in_specs = [pl.BlockSpec((g, S5_P, tj), lambda h, t, s: (0, s, t)),
                pl.BlockSpec((tj, tn), lambda h, t, s: (t, s * nh + h))]
    args = [lhs_gt, rhs_cr]
    if vec is not None:
        in_specs.append(_full(vec.shape))
        args.append(vec)
    o_spec = pl.BlockSpec((e, tn), lambda h, t, s: (0, h))
    if has_init:
        in_specs.append(o_spec)
        args.append(init)
    out_specs, out_shape = [o_spec], [jax.ShapeDtypeStruct((e, n), out_dtype)]
    if bias_sum:
        out_specs.append(pl.BlockSpec((8, tn), lambda h, t, s: (0, h)))
        out_shape.append(jax.ShapeDtypeStruct((8, n), F32))
    res = pl.pallas_call(
        body, name=name, grid=(nh, nt, CHUNK), in_specs=in_specs, out_specs=out_specs, out_shape=out_shape,
        scratch_shapes=[pltpu.VMEM((e, tn), F32)], compiler_params=_cparams(),
    )(*args)
    return res if bias_sum else res[0]


def _s5_weights(lam_re, lam_im, log_step, b_re, b_im, c_re, c_im, d_skip):
    hp = lax.Precision.HIGHEST
    g = lam_re.shape[1]
    t, p, n = CHUNK, S5_P, S5_N
    dt = jnp.exp(log_step)[..., None]
    ks = jnp.arange(t + 1, dtype=F32).reshape(t + 1, 1, 1, 1)
    mag = jnp.exp(ks * (lam_re * dt)[None])
    ang = ks * (lam_im * dt)[None]
    pr, pi = mag * jnp.cos(ang), mag * jnp.sin(ang)
    ar, ai = pr[1], pi[1]
    qr, qi = ar - 1.0, ai
    den = lam_re * lam_re + lam_im * lam_im
    fr = (qr * lam_re + qi * lam_im) / den
    fi = (qi * lam_re - qr * lam_im) / den
    bt_re, bt_im = b_re.transpose(0, 1, 3, 2), b_im.transpose(0, 1, 3, 2)
    bbr = fr[:, :, None, :] * bt_re - fi[:, :, None, :] * bt_im
    bbi = fr[:, :, None, :] * bt_im + fi[:, :, None, :] * bt_re
    pk_r, pk_i = pr[:t, :, :, None, :], pi[:t, :, :, None, :]
    abr = pk_r * bbr[None] - pk_i * bbi[None]
    abi = pk_r * bbi[None] + pk_i * bbr[None]
    kd = (jnp.einsum("rgpn,krgqn->rgkpq", c_re, abr, precision=hp)
          - jnp.einsum("rgpn,krgqn->rgkpq", c_im, abi, precision=hp))
    skip = jnp.eye(p, dtype=F32)[None] * d_skip.reshape(g, p)[:, :, None]
    diag = kd[0][:, 0] + kd[1][:, 0] + skip
    qd = jnp.concatenate([jnp.flip(kd[1][:, 1:], axis=1), diag[:, None], kd[0][:, 1:]], axis=1)
    toep = jnp.stack([qd[:, t - 1 - s:2 * t - 1 - s] for s in range(t)], axis=1)
    mt = toep.transpose(0, 1, 4, 2, 3).reshape(g, t * p, t * p)
    ab = jnp.concatenate([abr, abi], axis=-1)
    bcc = jnp.stack([jnp.flip(ab[:, 0], axis=0), ab[:, 1]])
    bc = bcc.transpose(2, 1, 3, 0, 4).reshape(g, t * p, 4 * n)
    prf = jnp.stack([pr[1:, 0], jnp.flip(pr[1:, 1], axis=0)])[:, :, :, None, :]
    pif = jnp.stack([pi[1:, 0], jnp.flip(pi[1:, 1], axis=0)])[:, :, :, None, :]
    cr_t = c_re[:, None]
    ci_t = c_im[:, None]
    ccc = jnp.concatenate([cr_t * prf - ci_t * pif, -(cr_t * pif + ci_t * prf)], axis=-1)
    cct = ccc.transpose(2, 1, 3, 0, 4).reshape(g, t * p, 4 * n)

    def pair_lanes(a):
        a5 = a.reshape(g // 2, 2, t * p, 4, n)
        return jnp.einsum("agrcn,gh->agrchn", a5, jnp.eye(2, dtype=F32)).reshape(g, t * p, PAIR_W)

    return mt, pair_lanes(bc), pair_lanes(cct), pr[t], pi[t]


def _scan_coef_g(lam_re, lam_im, log_step):
    g = lam_re.shape[1]
    ms = jnp.array([1, 2, 4, 0, 0, 0, 0, 0] + list(range(1, 9)) + list(range(8, 0, -1)), F32) * CHUNK
    dt = jnp.exp(log_step)[..., None]
    mag = jnp.exp(ms.reshape(-1, 1, 1, 1) * (lam_re * dt)[None])
    ang = ms.reshape(-1, 1, 1, 1) * (lam_im * dt)[None]
    cr, ci = mag * jnp.cos(ang), mag * jnp.sin(ang)
    both = jnp.stack([cr, ci], axis=2).reshape(24, 2, 2, g // 2, 2, S5_N)
    return both.transpose(0, 3, 1, 2, 4, 5).reshape(24, g * ZG_W)


def _s5_compact(lam_re, lam_im, log_step, b_re, b_im, c_re, c_im, d_skip):
    hp = lax.Precision.HIGHEST
    g = lam_re.shape[1]
    nb = g // GROUPS_PER_BLOCK
    t, p, n = CHUNK, S5_P, S5_N
    dt = jnp.exp(log_step)[..., None]
    ks = jnp.arange(t + 1, dtype=F32).reshape(t + 1, 1, 1, 1)
    mag = jnp.exp(ks * (lam_re * dt)[None])
    ang = ks * (lam_im * dt)[None]
    pr, pi = mag * jnp.cos(ang), mag * jnp.sin(ang)
    ar, ai = pr[1], pi[1]
    qr, qi = ar - 1.0, ai
    den = lam_re * lam_re + lam_im * lam_im
    fr = (qr * lam_re + qi * lam_im) / den
    fi = (qi * lam_re - qr * lam_im) / den
    bt_re, bt_im = b_re.transpose(0, 1, 3, 2), b_im.transpose(0, 1, 3, 2)
    bbr = fr[:, :, None, :] * bt_re - fi[:, :, None, :] * bt_im
    bbi = fr[:, :, None, :] * bt_im + fi[:, :, None, :] * bt_re
    pk_r, pk_i = pr[:t, :, :, None, :], pi[:t, :, :, None, :]
    abr = pk_r * bbr[None] - pk_i * bbi[None]
    abi = pk_r * bbi[None] + pk_i * bbr[None]
    kd = (jnp.einsum("rgpn,krgqn->rgkpq", c_re, abr, precision=hp)
          - jnp.einsum("rgpn,krgqn->rgkpq", c_im, abi, precision=hp))
    skip = jnp.eye(p, dtype=F32)[None] * d_skip.reshape(g, p)[:, :, None]
    diag = kd[0][:, 0] + kd[1][:, 0] + skip
    qd = jnp.concatenate([jnp.flip(kd[1][:, 1:], axis=1), diag[:, None], kd[0][:, 1:]], axis=1)
    nd = 2 * t - 1
    wc = qd.transpose(0, 1, 3, 2).reshape(nb, GROUPS_PER_BLOCK, nd, p, p).transpose(0, 2, 1, 3, 4)
    wcomp = wc.reshape(nb, nd, LANE_BLOCK, p)
    ab = jnp.concatenate([abr, abi], axis=-1)
    bcc = jnp.stack([jnp.flip(ab[:, 0], axis=0), ab[:, 1]])
    bcomp = bcc.reshape(2, t, nb, LANE_BLOCK, 2 * n)
    prf = jnp.stack([pr[1:, 0], jnp.flip(pr[1:, 1], axis=0)])[:, :, :, None, :]
    pif = jnp.stack([pi[1:, 0], jnp.flip(pi[1:, 1], axis=0)])[:, :, :, None, :]
    cr_t = c_re[:, None]
    ci_t = c_im[:, None]
    ccc = jnp.concatenate([cr_t * prf - ci_t * pif, -(cr_t * pif + ci_t * prf)], axis=-1)
    ccomp = ccc.reshape(2, t, nb, LANE_BLOCK, 2 * n)
    return wcomp, bcomp, ccomp, pr[t], pi[t]


def _scan_coef(lam_re, lam_im, log_step):
    g = lam_re.shape[1]
    nb = g // GROUPS_PER_BLOCK
    ms = jnp.array([1, 2, 4, 0, 0, 0, 0, 0] + list(range(1, 9)) + list(range(8, 0, -1)), F32) * CHUNK
    dt = jnp.exp(log_step)[..., None]
    mag = jnp.exp(ms.reshape(-1, 1, 1, 1) * (lam_re * dt)[None])
    ang = ms.reshape(-1, 1, 1, 1) * (lam_im * dt)[None]
    cr, ci = mag * jnp.cos(ang), mag * jnp.sin(ang)
    lay = lambda a: a.reshape(24, 2, nb, ZH).transpose(2, 1, 0, 3)
    return jnp.concatenate([lay(cr), lay(ci)], axis=-1)


def _to_cr(a):
    return a.reshape(a.shape[0] // CHUNK, CHUNK * a.shape[1])


def _from_cr(a, c):
    return a.reshape(a.shape[0] * CHUNK, c)


def _pad8(v):
    return jnp.concatenate([v, jnp.zeros((8 - v.shape[0], v.shape[1]), v.dtype)], axis=0)


def _local_step(x, c, ctx, c_ctx, loss_target, w, late=None, scatter=False):
    l, d = x.shape
    lc = ctx.shape[0]
    tm = min(256, lc)
    assert lc == tm and l % tm == 0 and tm % GRID_W == 0 and (tm & (tm - 1)) == 0
    nl = l // tm

    c8 = _pad8(jnp.stack([c, c_ctx]))
    mod = _ada_fwd(c8, w["ada_w"], w["ada_b"])
    sh = mod[:, :2, :d]
    sc = mod[:, :2, d:2 * d]
    gt = mod[:, :2, 2 * d:]
    ln_g, ln_b = w["ln_g"], w["ln_b"]

    a0, b0 = 1.0 + sc[0], sh[0]
    xch = _Exchange("gather", [late[n][0] for n in late], [late[n][1] for n in late]) if late else None
    p42, got = _inproj0(x, ctx, a0, b0, w["conv_w_in"], tm, xch)
    if late:
        w = dict(w, **dict(zip(late, got)))
    e = w["conv_w_out"].shape[0]
    half = e // 2
    nb = e // LANE_BLOCK
    tc = min(512, half)
    cw = w["conv_w"].reshape(3, 2, half)
    q3 = _conv_fwd(p42, cw, nl, tm, tc)
    xh1_l, xh1_c, rs1_l, rs1_c, fx = _outproj_ln0(q3, w["conv_w_out"], x, ctx, gt[0], tm)
    jl, jc = l // CHUNK, lc // CHUNK

    g0, bb0 = ln_g[0:1], ln_b[0:1]
    a1 = g0 * (1.0 + sc[1])
    b1 = bb0 * (1.0 + sc[1]) + sh[1]
    wu_t = w["ssm_w_in"][:, :e].T
    w_z = w["ssm_w_in"][:, e:]
    ut_l, z_l = _inproj1_gt(xh1_l, a1[0:1], b1[0:1], wu_t, w_z, "lat")
    ut_c, _ = _inproj1_gt(xh1_c, a1[1:2], b1[1:2], wu_t, w_z, "ctx")
    s5 = (w["ssm_lam_re"], w["ssm_lam_im"], w["ssm_log_step"], w["ssm_b_re"], w["ssm_b_im"],
          w["ssm_c_re"], w["ssm_c_im"], w["ssm_d"])
    (mt, bcw, cctw, _, _), s5_vjp = jax.vjp(_s5_weights, *s5)
    mt_b, mtt_b = mt.astype(BF), mt.transpose(0, 2, 1).astype(BF)
    bc_b, cct_b = bcw.astype(BF), cctw.astype(BF)
    coef = lax.stop_gradient(_scan_coef_g(*s5[:3]))
    zz_l, zz_c = _s5_z(ut_l, ut_c, bc_b)
    fwd_chains = ((("c", False), ("l", False)), (("c", True), ("l", True)))
    st_l, st_c = _scan_g(zz_l, zz_c, coef, fwd_chains, False, name="l1_scan_fwd")
    yt = _s5_y(ut_l, st_l, mtt_b, cct_b)
    b_glu = w["ssm_b_glu"].reshape(1, e)
    w_cr, sg_cr = _glu_fwd_gt(yt, z_l, w["ssm_w_glu"], b_glu)
    vec_f = _pad8(jnp.concatenate([g0, bb0, gt[1][0:1], ln_g[1:2], ln_b[1:2]], axis=0))
    dr2, acc_f = _final(w_cr, w["ssm_w_out"], xh1_l, _to_cr(loss_target), vec_f)
    loss = jnp.sum(acc_f[3])

    gt1 = gt[1][0:1]
    dz_l, dt_l, dyt = _glu_bwd_gt(dr2, gt1, w["ssm_w_out"], w["ssm_w_glu"], yt, z_l, sg_cr)
    g_w_out = _dw_cr(w_cr, dr2, "cr", "scaled", gt1, False, None, "l1_dw_out")
    g_w_glu, bsum = _dw_gt(yt, dt_l, True, None, True, None, BF, "l1_dw_glu")
    g_b_glu = bsum[0]
    ds_l = _s5_ds(dyt, cct_b)
    bwd_chains = ((("l", True), ("c", True)), (("l", False), ("c", False)))
    dzz_l, dzz_c, da = _scan_g(ds_l, jnp.zeros_like(zz_c), coef, bwd_chains, True, st_l, st_c, name="l1_scan_bwd")
    dut_l, dut_c = _s5_dx(dyt, dzz_l, dzz_c, mt_b, bc_b)
    d_mt, d_bc, d_cct = _s5_dw(ut_l, ut_c, dyt, dzz_l, dzz_c, st_l)
    n_g = e // S5_P
    da = jnp.sum(da, axis=0).reshape(n_g // 2, 2, 2, 2, S5_N).transpose(1, 2, 0, 3, 4)
    da = da.reshape(2, 2, n_g, S5_N)
    g_s5 = s5_vjp((d_mt, d_bc, d_cct, da[:, 0], da[:, 1]))

    vec_l = _pad8(jnp.concatenate([g0, bb0, 1.0 + sc[1][0:1]], axis=0))
    vec_c = _pad8(jnp.concatenate([g0, bb0, 1.0 + sc[1][1:2]], axis=0))
    dr1_l, acc_l = _bwd_inproj1_gt(dut_l, dz_l, wu_t, w_z, xh1_l, rs1_l, dr2, vec_l, "lat")
    dr1_c, acc_c = _bwd_inproj1_gt(dut_c, jnp.zeros((jc, CHUNK * e), BF), wu_t, w_z, xh1_c, rs1_c,
                                   jnp.zeros((jc, CHUNK * d), BF), vec_c, "ctx")
    mod_l = jnp.concatenate([a1[0:1], b1[0:1]], axis=0)
    mod_c = jnp.concatenate([a1[1:2], b1[1:2]], axis=0)
    g_ut_c = _dw_gt(dut_c, xh1_c, False, mod_c, False, None, F32, "l1_dw_in_u_ctx")
    g_ut = _dw_gt(dut_l, xh1_l, False, mod_l, False, g_ut_c, BF, "l1_dw_in_u")
    g_in_z = _dw_cr(xh1_l, dz_l, "mod", "cr", mod_l, False, None, "l1_dw_in_z")
    g_w_in1 = jnp.concatenate([g_ut.T, g_in_z], axis=1)

    dr1_ln, dr1_cn = _from_cr(dr1_l, d), _from_cr(dr1_c, d)
    dq3, acc_g0 = _bwd_outproj0(dr1_ln, dr1_cn, gt[0], w["conv_w_out"], fx, tm)
    sent1 = ["ssm_w_in", "ssm_w_glu", "ssm_w_out"]
    xch1 = _Exchange("scatter", [g_w_in1, g_w_glu, g_w_out], [BIG[n] for n in sent1]) if scatter else None
    dp42, dcw, recv1 = _conv_bwd(dq3, p42, cw, nl, tm, tc, xch1)
    g_w_in0 = _dw_inproj0(x, ctx, a0, b0, dp42, tm)
    g_w_out0 = _dw_outproj0(q3, dr1_ln, dr1_cn, gt[0], tm)
    sent0 = ["conv_w_in", "conv_w_out"]
    xch0 = _Exchange("scatter", [g_w_in0, g_w_out0], [BIG[n] for n in sent0]) if scatter else None
    grad_x, acc_0, recv0 = _bwd_inproj0(dp42, w["conv_w_in"], x, ctx, dr1_ln, dr1_cn, a0, tm, xch0)
    recv = dict(zip(sent1 + sent0, recv1 + recv0))

    zero = jnp.zeros((d,), F32)
    dm0 = jnp.stack([jnp.concatenate([acc_0[2], acc_0[0], acc_g0[0]]), jnp.concatenate([acc_0[3], acc_0[1], acc_g0[1]])])
    dm1 = jnp.stack([jnp.concatenate([acc_l[1], acc_l[0], acc_f[2]]), jnp.concatenate([acc_c[1], acc_c[0], zero])])
    dm8 = jnp.stack([_pad8(dm0), _pad8(dm1)])
    g_ada_w, dc8 = _ada_bwd(c8, w["ada_w"], dm8)

    grads = {
        "c_ctx": dc8[0, 1] + dc8[1, 1],
        "ada_w": g_ada_w,
        "ada_b": jnp.stack([dm0[0] + dm0[1], dm1[0] + dm1[1]]),
        "ln_g": jnp.stack([acc_l[2] + acc_c[2], acc_f[0]]),
        "ln_b": jnp.stack([acc_l[3] + acc_c[3], acc_f[1]]),
        "conv_w_in": g_w_in0, "conv_w": dcw[:3].reshape(3, e), "conv_w_out": g_w_out0,
        "ssm_w_in": g_w_in1,
        "ssm_lam_re": g_s5[0], "ssm_lam_im": g_s5[1], "ssm_log_step": g_s5[2],
        "ssm_b_re": g_s5[3], "ssm_b_im": g_s5[4], "ssm_c_re": g_s5[5], "ssm_c_im": g_s5[6], "ssm_d": g_s5[7],
        "ssm_w_glu": g_w_glu, "ssm_b_glu": g_b_glu, "ssm_w_out": g_w_out,
    }
    for n in recv:
        del grads[n]
    return loss, grad_x, grads, recv


WEIGHTS = ["c_ctx", "ada_w", "ada_b", "ln_g", "ln_b", "conv_w_in", "conv_w", "conv_w_out", "ssm_w_in",
           "ssm_lam_re", "ssm_lam_im", "ssm_log_step", "ssm_b_re", "ssm_b_im", "ssm_c_re", "ssm_c_im",
           "ssm_d", "ssm_w_glu", "ssm_b_glu", "ssm_w_out"]
BIG = {"ada_w": 1, "conv_w_in": 1, "conv_w_out": 0, "ssm_w_in": 1, "ssm_w_glu": 0, "ssm_w_out": 0}
SMALL_SHARDED = ["conv_w", "ssm_d", "ssm_b_glu"]
REPLICATED = ["c_ctx", "ada_b", "ln_g", "ln_b", "ssm_lam_re", "ssm_lam_im", "ssm_log_step",
              "ssm_b_re", "ssm_b_im", "ssm_c_re", "ssm_c_im"]


def _view2d(name, a):
    return a.reshape(-1, a.shape[-1])


def kernel(x, c, ctx, c_ctx, ada_w, ada_b, ln_g, ln_b, conv_w_in, conv_w, conv_w_out, ssm_w_in, ssm_lam_re, ssm_lam_im, ssm_log_step, ssm_b_re, ssm_b_im, ssm_c_re, ssm_c_im, ssm_d, ssm_w_glu, ssm_b_glu, ssm_w_out, loss_target, m_c_ctx, m_ada_w, m_ada_b, m_ln_g, m_ln_b, m_conv_w_in, m_conv_w, m_conv_w_out, m_ssm_w_in, m_ssm_lam_re, m_ssm_lam_im, m_ssm_log_step, m_ssm_b_re, m_ssm_b_im, m_ssm_c_re, m_ssm_c_im, m_ssm_d, m_ssm_w_glu, m_ssm_b_glu, m_ssm_w_out, v_c_ctx, v_ada_w, v_ada_b, v_ln_g, v_ln_b, v_conv_w_in, v_conv_w, v_conv_w_out, v_ssm_w_in, v_ssm_lam_re, v_ssm_lam_im, v_ssm_log_step, v_ssm_b_re, v_ssm_b_im, v_ssm_c_re, v_ssm_c_im, v_ssm_d, v_ssm_w_glu, v_ssm_b_glu, v_ssm_w_out):
    args = locals()
    wt = {n: args[n] for n in WEIGHTS}
    mt = {n: args["m_" + n] for n in WEIGHTS}
    vt = {n: args["v_" + n] for n in WEIGHTS}

    big_names = list(BIG)
    shard = {n: _view2d(n, wt[n]).astype(BF) for n in big_names}
    first = ["ada_w", "conv_w_in"]
    small = jnp.concatenate([wt["conv_w"][0], wt["ssm_d"], wt["ssm_b_glu"]], axis=0)
    small = jnp.concatenate([small, jnp.zeros((3, small.shape[1]), F32)], axis=0)
    gathered = _all_gather([shard[n] for n in first] + [small], [BIG[n] for n in first] + [1], "gather_weights")
    full = dict(zip(first, gathered[:-1]))
    small_full = gathered[-1]
    late = {n: (shard[n], BIG[n]) for n in big_names if n not in first}
    d = x.shape[-1]
    w = {
        "ada_w": full["ada_w"].reshape(2, d, 3 * d), "ada_b": ada_b, "ln_g": ln_g, "ln_b": ln_b,
        "conv_w_in": full["conv_w_in"], "conv_w": small_full[0:3],
        "ssm_lam_re": ssm_lam_re[0], "ssm_lam_im": ssm_lam_im[0],
        "ssm_log_step": ssm_log_step[0], "ssm_b_re": ssm_b_re[0], "ssm_b_im": ssm_b_im[0],
        "ssm_c_re": ssm_c_re[0], "ssm_c_im": ssm_c_im[0], "ssm_d": small_full[3], "ssm_b_glu": small_full[4],
    }

    loss, grad_x, g, recv_big = _local_step(x[0], c[0], ctx[0], c_ctx, loss_target[0], w, late, True)
    loss = lax.psum(loss, ("x", "y", "c"))

    blob_names = REPLICATED + SMALL_SHARDED
    flat = jnp.concatenate([g[n].reshape(-1).astype(F32) for n in blob_names])
    nflat = flat.shape[0]
    rows = -(-nflat // (N_DEV * 128 * 8)) * 8
    flat = jnp.concatenate([flat, jnp.zeros((N_DEV * rows * 128 - nflat,), F32)]).reshape(N_DEV * rows, 128)
    last = [n for n in big_names if n not in recv_big]
    recv = _all_to_all([_view2d(n, g[n]) for n in last] + [flat], [BIG[n] for n in last] + [0], "scatter_grads")
    recv_big.update(zip(last, recv[:-1]))
    blob_sum = _sum_partials(recv[-1])
    blob = _all_gather([blob_sum], [0], "gather_small_grads")[0].reshape(-1)
    small_g, off = {}, 0
    for n in blob_names:
        shape = wt[n].shape if n in REPLICATED else (*wt[n].shape[:-1], wt[n].shape[-1] * N_DEV)
        size = math.prod(shape)
        small_g[n] = blob[off:off + size].reshape(shape)
        off += size
    me = 4 * lax.axis_index("x") + 2 * lax.axis_index("y") + lax.axis_index("c")
    for n in SMALL_SHARDED:
        size = wt[n].shape[-1]
        small_g[n] = lax.dynamic_slice_in_dim(small_g[n], me * size, size, axis=small_g[n].ndim - 1)

    out_g, out_d, out_m, out_v = {}, {}, {}, {}
    for n in big_names:
        stack = recv_big[n]
        shp = wt[n].shape
        res = _adamw(stack, _view2d(n, wt[n]), _view2d(n, mt[n]), _view2d(n, vt[n]), "adamw_" + n)
        out_g[n], out_d[n], out_m[n], out_v[n] = [r.reshape(shp) for r in res]
    names = list(small_g)
    cat = lambda t: jnp.concatenate([t[n].reshape(-1) for n in names])
    gs, ws, ms, vs = cat(small_g), cat(wt), cat(mt), cat(vt)
    ns = gs.shape[0]
    rs = -(-ns // (128 * 512)) * 512
    padr = lambda a: jnp.concatenate([a, jnp.ones((rs * 128 - ns,), F32)]).reshape(rs, 128)
    res = _adamw(padr(gs)[None], padr(ws), padr(ms), padr(vs), "adamw_small")
    off = 0
    for n in names:
        size = math.prod(wt[n].shape)
        out_g[n], out_d[n], out_m[n], out_v[n] = [r.reshape(-1)[off:off + size].reshape(wt[n].shape) for r in res]
        off += size

    return (loss, grad_x[None], *[out_g[n] for n in WEIGHTS], *[out_d[n] for n in WEIGHTS],
            *[out_m[n] for n in WEIGHTS], *[out_v[n] for n in WEIGHTS])
```

```python
import math

import jax
import jax.numpy as jnp
from jax import lax
from jax.experimental import pallas as pl
from jax.experimental.pallas import tpu as pltpu

F32 = jnp.float32
BF = jnp.bfloat16
MESH = pl.DeviceIdType.MESH
N_DEV = 8

GRID_W = 64
CHUNK = 16
S5_P = 16
S5_N = 64
LANE_BLOCK = 128
GROUPS_PER_BLOCK = LANE_BLOCK // S5_P
BCR_W = CHUNK * LANE_BLOCK
ZL_W = 2 * 2 * GROUPS_PER_BLOCK * S5_N
ZH = ZL_W // 4
LN_EPS = 1e-5
DN_ALPHA = 4.0 ** 0.25
ADAM_LR, ADAM_B1, ADAM_B2, ADAM_EPS, ADAM_WD, ADAM_STEP = 1e-3, 0.9, 0.999, 1e-8, 0.01, 10
GELU_C0 = math.sqrt(2.0 / math.pi)
GELU_C1 = 0.044715
VMEM_MB = 52

ANY = pl.BlockSpec(memory_space=pl.ANY)


def _cparams():
    return pltpu.CompilerParams(vmem_limit_bytes=VMEM_MB << 20)


def _dot(a, b):
    return jnp.dot(a, b, preferred_element_type=F32)


def _dot_nt(a, b):
    return lax.dot_general(a, b, (((1,), (1,)), ((), ())), preferred_element_type=F32)


def _dot_tn(a, b):
    return lax.dot_general(a, b, (((0,), (0,)), ((), ())), preferred_element_type=F32)


def _sigmoid(x):
    return 1.0 / (1.0 + jnp.exp(-x))


def _gelu_parts(y):
    th = jnp.tanh(GELU_C0 * (y + GELU_C1 * y * y * y))
    g = 0.5 * y * (1.0 + th)
    dg = 0.5 * (1.0 + th) + 0.5 * y * (1.0 - th * th) * GELU_C0 * (1.0 + 3.0 * GELU_C1 * y * y)
    return g, dg


def _full(shape):
    nd = len(shape)
    return pl.BlockSpec(shape, lambda *_: (0,) * nd)


def _mesh_pos():
    x, y, c = lax.axis_index("x"), lax.axis_index("y"), lax.axis_index("c")
    return x, y, c


def _peer(pos, k):
    x, y, c = pos
    px = 1 - x if (k >> 2) & 1 else x
    py = 1 - y if (k >> 1) & 1 else y
    pc = 1 - c if k & 1 else c
    return (px, py, pc), 4 * px + 2 * py + pc


def _shard_at(ref, axis, idx, n):
    if axis == 0:
        return ref.at[pl.ds(idx * n, n)]
    return ref.at[:, pl.ds(idx * n, n)]


class _Exchange:
    def __init__(self, kind, arrays, axes):
        self.kind, self.axes, self.n = kind, list(axes), len(arrays)
        self.arrays = list(arrays)
        self.out_shape = []
        for s, ax in zip(arrays, axes):
            shp = list(s.shape)
            if kind == "gather":
                shp[ax] *= N_DEV
                self.out_shape.append(jax.ShapeDtypeStruct(tuple(shp), s.dtype))
            else:
                shp[ax] //= N_DEV
                self.out_shape.append(jax.ShapeDtypeStruct((N_DEV, *shp), s.dtype))
        self.scratch = [pltpu.SemaphoreType.DMA((self.n, N_DEV - 1)), pltpu.SemaphoreType.DMA((self.n, N_DEV - 1)),
                        pltpu.SemaphoreType.DMA((self.n,))]

    def _copies(self, ins, outs, sems):
        send_sems, recv_sems, local_sems = sems
        pos = _mesh_pos()
        me = 4 * pos[0] + 2 * pos[1] + pos[2]
        local, sends, recvs = [], [], []
        for i in range(self.n):
            ax = self.axes[i]
            if self.kind == "gather":
                size = ins[i].shape[ax]
                src = lambda idx, i=i: ins[i]
                dst = lambda idx, i=i, ax=ax, size=size: _shard_at(outs[i], ax, idx, size)
                mine, theirs = (lambda pidx: me), (lambda pidx: pidx)
            else:
                size = ins[i].shape[ax] // N_DEV
                src = lambda idx, i=i, ax=ax, size=size: _shard_at(ins[i], ax, idx, size)
                dst = lambda idx, i=i: outs[i].at[idx]
                mine, theirs = (lambda pidx: me), (lambda pidx: pidx)
            src_own = src(me)
            local.append(pltpu.make_async_copy(src_own, dst(me), local_sems.at[i]))
            for k in range(1, N_DEV):
                peer, pidx = _peer(pos, k)
                out_src = src(me) if self.kind == "gather" else src(pidx)
                sends.append(pltpu.make_async_remote_copy(
                    src_ref=out_src, dst_ref=dst(mine(pidx)), send_sem=send_sems.at[i, k - 1],
                    recv_sem=recv_sems.at[i, k - 1], device_id=peer, device_id_type=MESH))
                recvs.append(pltpu.make_async_remote_copy(
                    src_ref=out_src, dst_ref=dst(theirs(pidx)), send_sem=send_sems.at[i, k - 1],
                    recv_sem=recv_sems.at[i, k - 1], device_id=peer, device_id_type=MESH))
        return local, sends, recvs

    def start(self, ins, outs, sems):
        local, sends, _ = self._copies(ins, outs, sems)
        for cp in local + sends:
            cp.start()

    def wait(self, ins, outs, sems):
        local, sends, recvs = self._copies(ins, outs, sems)
        for cp in recvs:
            cp.wait_recv()
        for cp in sends:
            cp.wait_send()
        for cp in local:
            cp.wait()

    def run(self, name):
        n = self.n

        def body(*refs):
            ins, outs, sems = refs[:n], refs[n:2 * n], refs[2 * n:]
            self.start(ins, outs, sems)
            self.wait(ins, outs, sems)

        return pl.pallas_call(body, name=name, out_shape=self.out_shape, in_specs=[ANY] * n, out_specs=[ANY] * n,
                              scratch_shapes=self.scratch)(*self.arrays)


def _hosted_call(body, xch, grid, in_specs, out_specs, out_shape, scratch, args, name):
    out_specs, out_shape = list(out_specs), list(out_shape)
    n_in, n_out = len(in_specs), len(out_specs)
    if xch is None:
        res = pl.pallas_call(body, name=name, grid=grid, in_specs=in_specs, out_specs=out_specs, out_shape=out_shape,
                             scratch_shapes=list(scratch), compiler_params=_cparams())(*args)
        return list(res), []
    n = xch.n
    rank = len(grid)

    def wrapped(*refs):
        ins, x_ins = refs[:n_in], refs[n_in:n_in + n]
        outs = refs[n_in + n:n_in + n + n_out]
        x_outs = refs[n_in + n + n_out:n_in + 2 * n + n_out]
        rest = refs[n_in + 2 * n + n_out:]
        own, sems = rest[:len(rest) - 3], rest[len(rest) - 3:]
        ids = [pl.program_id(a) for a in range(rank)]
        first, last = ids[0] == 0, ids[0] == grid[0] - 1
        for a in range(1, rank):
            first = jnp.logical_and(first, ids[a] == 0)
            last = jnp.logical_and(last, ids[a] == grid[a] - 1)

        @pl.when(first)
        def _():
            xch.start(x_ins, x_outs, sems)

        body(*ins, *outs, *own)

        @pl.when(last)
        def _():
            xch.wait(x_ins, x_outs, sems)

    res = pl.pallas_call(
        wrapped, name=name, grid=grid, in_specs=list(in_specs) + [ANY] * n, out_specs=out_specs + [ANY] * n,
        out_shape=out_shape + xch.out_shape, scratch_shapes=list(scratch) + xch.scratch, compiler_params=_cparams(),
    )(*args, *xch.arrays)
    return list(res[:n_out]), list(res[n_out:])


def _all_gather(shards, axes, name):
    return _Exchange("gather", shards, axes).run(name)


def _all_to_all(parts, axes, name):
    return _Exchange("scatter", parts, axes).run(name)


def _ada_fwd(c8, ada_w, ada_b):
    nl, d, d3 = ada_w.shape

    def body(c_ref, w_ref, b_ref, o_ref):
        cv = c_ref[...]
        s = (cv * _sigmoid(cv)).astype(BF)
        o_ref[0] = _dot(s, w_ref[0]) + b_ref[0]

    return pl.pallas_call(
        body, name="ada_fwd", grid=(nl,),
        in_specs=[_full((8, d)), pl.BlockSpec((1, d, d3), lambda l: (l, 0, 0)), pl.BlockSpec((1, 1, d3), lambda l: (l, 0, 0))],
        out_specs=pl.BlockSpec((1, 8, d3), lambda l: (l, 0, 0)),
        out_shape=jax.ShapeDtypeStruct((nl, 8, d3), F32), compiler_params=_cparams(),
    )(c8, ada_w, ada_b.reshape(nl, 1, d3))


def _ada_bwd(c8, ada_w, dm8):
    nl, d, d3 = ada_w.shape

    def body(c_ref, w_ref, dm_ref, dw_ref, dc_ref):
        cv = c_ref[...]
        sg = _sigmoid(cv)
        s = (cv * sg).astype(BF)
        dm = dm_ref[0].astype(BF)
        dw_ref[0] = _dot_tn(s, dm).astype(BF)
        dc_ref[0] = _dot_nt(dm, w_ref[0]) * (sg * (1.0 + cv * (1.0 - sg)))

    return pl.pallas_call(
        body, name="ada_bwd", grid=(nl,),
        in_specs=[_full((8, d)), pl.BlockSpec((1, d, d3), lambda l: (l, 0, 0)), pl.BlockSpec((1, 8, d3), lambda l: (l, 0, 0))],
        out_specs=[pl.BlockSpec((1, d, d3), lambda l: (l, 0, 0)), pl.BlockSpec((1, 8, d), lambda l: (l, 0, 0))],
        out_shape=[jax.ShapeDtypeStruct((nl, d, d3), BF), jax.ShapeDtypeStruct((nl, 8, d), F32)],
        compiler_params=_cparams(),
    )(c8, ada_w, dm8)


def _sum_partials(stack):
    _, r, c = stack.shape

    def body(s_ref, o_ref):
        acc = s_ref[0]
        for p in range(1, N_DEV):
            acc = acc + s_ref[p]
        o_ref[...] = acc

    return pl.pallas_call(body, name="sum_partials", out_shape=jax.ShapeDtypeStruct((r, c), F32),
                          in_specs=[_full(stack.shape)], out_specs=_full((r, c)), grid=(1,),
                          compiler_params=_cparams())(stack)


def _adamw(gstack, w, m, v, name):
    p, r, c = gstack.shape
    tr = r
    for cand in (512 if c <= 256 else 256, 128, 64, 32, 16, 8):
        if r % cand == 0 and r > cand:
            tr = cand
            break
    bc1 = 1.0 - ADAM_B1 ** ADAM_STEP
    bc2 = 1.0 - ADAM_B2 ** ADAM_STEP

    def body(g_ref, w_ref, m_ref, v_ref, go_ref, d_ref, mo_ref, vo_ref):
        g = g_ref[0].astype(F32)
        for q in range(1, p):
            g = g + g_ref[q].astype(F32)
        mn = ADAM_B1 * m_ref[...] + (1.0 - ADAM_B1) * g
        vn = ADAM_B2 * v_ref[...] + (1.0 - ADAM_B2) * (g * g)
        go_ref[...] = g
        mo_ref[...] = mn
        vo_ref[...] = vn
        d_ref[...] = -ADAM_LR * ((mn / bc1) / (jnp.sqrt(vn / bc2) + ADAM_EPS) + ADAM_WD * w_ref[...])

    row = pl.BlockSpec((tr, c), lambda i: (i, 0))
    sds = jax.ShapeDtypeStruct((r, c), F32)
    return pl.pallas_call(
        body, name=name, grid=(r // tr,),
        in_specs=[pl.BlockSpec((p, tr, c), lambda i: (0, i, 0)), row, row, row],
        out_specs=[row, row, row, row], out_shape=[sds, sds, sds, sds], compiler_params=_cparams(),
    )(gstack, w, m, v)


def _lat_or_ctx_specs(tm, d, nl, grid_rank, row_axis):
    def lat(*ids):
        return (jnp.minimum(ids[row_axis], nl - 1), 0)

    def ctx(*ids):
        return (jnp.maximum(ids[row_axis] - nl, 0), 0)

    return pl.BlockSpec((tm, d), lat), pl.BlockSpec((tm, d), ctx)


def _sel_row(ref, is_ctx):
    return jnp.where(is_ctx, ref[1:2, :], ref[0:1, :])


def _inproj0(x, ctx, a2, b2, w, tm, xch=None):
    l, d = x.shape
    nl, nc = l // tm, ctx.shape[0] // tm
    e = w.shape[1] // 4
    half = e // 2

    def body(x_ref, c_ref, a_ref, b_ref, w_hbm, o_ref, w_ref):
        i = pl.program_id(0)

        @pl.when(i == 0)
        def _():
            pltpu.sync_copy(w_hbm, w_ref)

        is_ctx = i >= nl
        xv = jnp.where(is_ctx, c_ref[...], x_ref[...])
        h = (xv * _sel_row(a_ref, is_ctx) + _sel_row(b_ref, is_ctx)).astype(BF)
        for k in range(4):
            r = _dot(h, w_ref[:, k * e:(k + 1) * e])
            o_ref[k, 0] = r[:, :half].astype(BF)
            o_ref[k, 1] = r[:, half:].astype(BF)

    lat, cx = _lat_or_ctx_specs(tm, d, nl, 1, 0)
    (p42,), extra = _hosted_call(
        body, xch, grid=(nl + nc,),
        in_specs=[lat, cx, _full((2, d)), _full((2, d)), ANY],
        out_specs=[pl.BlockSpec((4, 2, tm, half), lambda i: (0, 0, i, 0))],
        out_shape=[jax.ShapeDtypeStruct((4, 2, l + ctx.shape[0], half), BF)],
        scratch=[pltpu.VMEM(w.shape, BF)], args=(x, ctx, a2, b2, w), name="l0_inproj")
    return p42, extra


def _conv_taps(u, w_up, w_mid, w_dn, pos, rl, tm):
    up = jnp.where(pos == 0, 0.0, pltpu.roll(u, 1, 0))
    dn = jnp.where(pos == rl - 1, 0.0, pltpu.roll(u, tm - 1, 0))
    return w_up * up + w_mid * u + w_dn * dn, up, dn


def _conv_halo_specs(tm, tc, nl, lead):
    hb = tm // GRID_W

    def prev(j, i):
        return (0, 1, jnp.maximum(jnp.minimum(i, nl - 1) * hb - 1, 0), j)

    def nxt(j, i):
        return (0, 1, jnp.minimum((jnp.minimum(i, nl - 1) + 1) * hb, nl * hb - 1), j)

    return pl.BlockSpec((lead, 1, GRID_W, tc), prev), pl.BlockSpec((lead, 1, GRID_W, tc), nxt)


def _conv_fwd(p42, cw, nl, tm, tc):
    _, _, r, half = p42.shape
    nt = r // tm

    def body(p_ref, hp_ref, hn_ref, cw_ref, o_ref):
        i = pl.program_id(1)
        is_ctx = i >= nl
        row = lax.broadcasted_iota(jnp.int32, (tm, tc), 0)
        rl = jnp.where(is_ctx, tm, GRID_W)
        pos = jnp.bitwise_and(row, rl - 1)

        def gate(hv, yc):
            bg = p_ref[0, hv].astype(F32)
            z = p_ref[3, hv].astype(F32)
            return (bg * yc * (z * _sigmoid(z))).astype(BF)

        u_h = p_ref[1, 0].astype(F32) * p_ref[2, 0].astype(F32)
        w_h = cw_ref[:, 0, :]
        o_ref[0] = gate(0, _conv_taps(u_h, w_h[0:1], w_h[1:2], w_h[2:3], pos, rl, tm)[0])
        u_v = p_ref[1, 1].astype(F32) * p_ref[2, 1].astype(F32)
        w_v = cw_ref[:, 1, :]

        @pl.when(is_ctx)
        def _():
            o_ref[1] = gate(1, _conv_taps(u_v, w_v[0:1], w_v[1:2], w_v[2:3], pos, rl, tm)[0])

        @pl.when(jnp.logical_not(is_ctx))
        def _():
            up = hp_ref[1, 0].astype(F32) * hp_ref[2, 0].astype(F32) * (i > 0).astype(F32)
            dn = hn_ref[1, 0].astype(F32) * hn_ref[2, 0].astype(F32) * (i < nl - 1).astype(F32)
            ext = jnp.concatenate([up, u_v, dn], axis=0)
            yc = w_v[0:1] * ext[0:tm] + w_v[1:2] * u_v + w_v[2:3] * ext[2 * GRID_W:tm + 2 * GRID_W]
            o_ref[1] = gate(1, yc)

    hp, hn = _conv_halo_specs(tm, tc, nl, 4)
    return pl.pallas_call(
        body, name="l0_conv_fwd", grid=(half // tc, nt),
        in_specs=[pl.BlockSpec((4, 2, tm, tc), lambda j, i: (0, 0, i, j)), hp, hn,
                  pl.BlockSpec((3, 2, tc), lambda j, i: (0, 0, j))],
        out_specs=pl.BlockSpec((2, tm, tc), lambda j, i: (0, i, j)),
        out_shape=jax.ShapeDtypeStruct((2, r, half), BF), compiler_params=_cparams(),
    )(p42, p42, p42, cw)


def _outproj_ln0(q3, w_out, x, ctx, gt2, tm):
    l, d = x.shape
    lc = ctx.shape[0]
    nl, nc = l // tm, lc // tm
    _, r, half = q3.shape
    tjo = tm // CHUNK

    def body(q_ref, w_hbm, x_ref, c_ref, g_ref, xl_ref, xc_ref, rl_ref, rc_ref, fx_ref, w_ref, xs_ref, rs_ref):
        i = pl.program_id(0)

        @pl.when(i == 0)
        def _():
            pltpu.sync_copy(w_hbm, w_ref)

        is_ctx = i >= nl
        fx = _dot(q_ref[0], w_ref[:half, :]) + _dot(q_ref[1], w_ref[half:, :])
        xv = jnp.where(is_ctx, c_ref[...], x_ref[...])
        rr = DN_ALPHA * xv + _sel_row(g_ref, is_ctx) * fx
        mu = jnp.mean(rr, axis=-1, keepdims=True)
        cen = rr - mu
        rstd = lax.rsqrt(jnp.mean(cen * cen, axis=-1, keepdims=True) + LN_EPS)
        xh = cen * rstd
        for lb in range(d // 128):
            xs_ref[lb] = xh[:, lb * 128:(lb + 1) * 128]
        rs_ref[...] = jnp.broadcast_to(rstd, (tm, 128))
        fx_ref[...] = fx.astype(BF)

        def to_cr(xo_ref, ro_ref):
            for s in range(CHUNK):
                for lb in range(d // 128):
                    xo_ref[:, s * d + lb * 128:s * d + (lb + 1) * 128] = xs_ref.at[lb][pl.ds(s, tjo, stride=CHUNK), :]
                ro_ref[:, s * 128:(s + 1) * 128] = rs_ref[pl.ds(s, tjo, stride=CHUNK), :]

        @pl.when(jnp.logical_not(is_ctx))
        def _():
            to_cr(xl_ref, rl_ref)

        @pl.when(is_ctx)
        def _():
            to_cr(xc_ref, rc_ref)

    lat, cx = _lat_or_ctx_specs(tm, d, nl, 1, 0)
    lat_o = lambda w_: pl.BlockSpec((tjo, CHUNK * w_), lambda i: (jnp.minimum(i, nl - 1), 0))
    ctx_o = lambda w_: pl.BlockSpec((tjo, CHUNK * w_), lambda i: (jnp.maximum(i - nl, 0), 0))
    return pl.pallas_call(
        body, name="l0_outproj_ln", grid=(nl + nc,),
        in_specs=[pl.BlockSpec((2, tm, half), lambda i: (0, i, 0)), ANY, lat, cx, _full((2, d))],
        out_specs=[lat_o(d), ctx_o(d), lat_o(128), ctx_o(128), pl.BlockSpec((tm, d), lambda i: (i, 0))],
        out_shape=[jax.ShapeDtypeStruct((l // CHUNK, CHUNK * d), F32), jax.ShapeDtypeStruct((lc // CHUNK, CHUNK * d), F32),
                   jax.ShapeDtypeStruct((l // CHUNK, CHUNK * 128), F32), jax.ShapeDtypeStruct((lc // CHUNK, CHUNK * 128), F32),
                   jax.ShapeDtypeStruct((r, d), BF)],
        scratch_shapes=[pltpu.VMEM(w_out.shape, BF), pltpu.VMEM((d // 128, tm, 128), F32), pltpu.VMEM((tm, 128), F32)],
        compiler_params=_cparams(),
    )(q3, w_out, x, ctx, gt2)


def _bwd_outproj0(dr_l, dr_c, gt2, w_out, fx, tm):
    l, d = dr_l.shape
    nl, nc = l // tm, dr_c.shape[0] // tm
    e = w_out.shape[0]
    half = e // 2
    r = l + dr_c.shape[0]

    def body(dl_ref, dc_ref, g_ref, w_hbm, fx_ref, dq_ref, acc_ref, w_ref):
        i = pl.program_id(0)

        @pl.when(i == 0)
        def _():
            pltpu.sync_copy(w_hbm, w_ref)
            acc_ref[...] = jnp.zeros_like(acc_ref)

        is_ctx = i >= nl
        dr = jnp.where(is_ctx, dc_ref[...], dl_ref[...]).astype(F32)
        dfx = (dr * _sel_row(g_ref, is_ctx)).astype(BF)
        dq_ref[0] = _dot_nt(dfx, w_ref[:half, :]).astype(BF)
        dq_ref[1] = _dot_nt(dfx, w_ref[half:, :]).astype(BF)
        s = jnp.sum(dr * fx_ref[...].astype(F32), axis=0, keepdims=True)
        sel = is_ctx.astype(F32)
        acc_ref[0:1, :] += s * (1.0 - sel)
        acc_ref[1:2, :] += s * sel

    lat, cx = _lat_or_ctx_specs(tm, d, nl, 1, 0)
    return pl.pallas_call(
        body, name="l0_bwd_outproj", grid=(nl + nc,),
        in_specs=[lat, cx, _full((2, d)), ANY, pl.BlockSpec((tm, d), lambda i: (i, 0))],
        out_specs=[pl.BlockSpec((2, tm, half), lambda i: (0, i, 0)), _full((8, d))],
        out_shape=[jax.ShapeDtypeStruct((2, r, half), BF), jax.ShapeDtypeStruct((8, d), F32)],
        scratch_shapes=[pltpu.VMEM(w_out.shape, BF)], compiler_params=_cparams(),
    )(dr_l, dr_c, gt2, w_out, fx)


def _conv_bwd(dq3, p42, cw, nl, tm, tc, xch=None):
    _, _, r, half = p42.shape
    nt = r // tm

    def body(dq_ref, dqp_ref, dqn_ref, p_ref, hp_ref, hn_ref, cw_ref, dp_ref, dw_ref):
        i = pl.program_id(1)
        is_ctx = i >= nl

        @pl.when(i == 0)
        def _():
            dw_ref[...] = jnp.zeros_like(dw_ref)

        row = lax.broadcasted_iota(jnp.int32, (tm, tc), 0)
        rl = jnp.where(is_ctx, tm, GRID_W)
        pos = jnp.bitwise_and(row, rl - 1)

        def pieces(dq, bg, z):
            sz = _sigmoid(z)
            sil = z * sz
            return dq * bg * sil, dq * sil, dq * bg * (sz * (1.0 + z * (1.0 - sz)))

        def seq_half(hv):
            bg, cg = p_ref[0, hv].astype(F32), p_ref[1, hv].astype(F32)
            v, z = p_ref[2, hv].astype(F32), p_ref[3, hv].astype(F32)
            w = cw_ref[:, hv, :]
            u = cg * v
            yc, u_up, u_dn = _conv_taps(u, w[0:1], w[1:2], w[2:3], pos, rl, tm)
            dyc, dbg_f, dz_f = pieces(dq_ref[hv].astype(F32), bg, z)
            du = _conv_taps(dyc, w[2:3], w[1:2], w[0:1], pos, rl, tm)[0]
            dp_ref[0, hv] = (dbg_f * yc).astype(BF)
            dp_ref[1, hv] = (du * v).astype(BF)
            dp_ref[2, hv] = (du * cg).astype(BF)
            dp_ref[3, hv] = (dz_f * yc).astype(BF)
            dw_ref[0:1, hv, :] += jnp.sum(dyc * u_up, axis=0, keepdims=True)
            dw_ref[1:2, hv, :] += jnp.sum(dyc * u, axis=0, keepdims=True)
            dw_ref[2:3, hv, :] += jnp.sum(dyc * u_dn, axis=0, keepdims=True)

        seq_half(0)

        @pl.when(is_ctx)
        def _():
            seq_half(1)

        @pl.when(jnp.logical_not(is_ctx))
        def _():
            bg, cg = p_ref[0, 1].astype(F32), p_ref[1, 1].astype(F32)
            v, z = p_ref[2, 1].astype(F32), p_ref[3, 1].astype(F32)
            w = cw_ref[:, 1, :]
            u = cg * v
            m_up = (i > 0).astype(F32)
            m_dn = (i < nl - 1).astype(F32)

            def halo(h_ref, dqh_ref, msk):
                hb, hc = h_ref[0, 0].astype(F32), h_ref[1, 0].astype(F32)
                hv_, hz = h_ref[2, 0].astype(F32), h_ref[3, 0].astype(F32)
                return hc * hv_ * msk, pieces(dqh_ref[0].astype(F32), hb, hz)[0] * msk

            u_p, dyc_p = halo(hp_ref, dqp_ref, m_up)
            u_n, dyc_n = halo(hn_ref, dqn_ref, m_dn)
            u_ext = jnp.concatenate([u_p, u, u_n], axis=0)
            u_up, u_dn = u_ext[0:tm], u_ext[2 * GRID_W:tm + 2 * GRID_W]
            yc = w[0:1] * u_up + w[1:2] * u + w[2:3] * u_dn
            dyc, dbg_f, dz_f = pieces(dq_ref[1].astype(F32), bg, z)
            d_ext = jnp.concatenate([dyc_p, dyc, dyc_n], axis=0)
            du = w[0:1] * d_ext[2 * GRID_W:tm + 2 * GRID_W] + w[1:2] * dyc + w[2:3] * d_ext[0:tm]
            dp_ref[0, 1] = (dbg_f * yc).astype(BF)
            dp_ref[1, 1] = (du * v).astype(BF)
            dp_ref[2, 1] = (du * cg).astype(BF)
            dp_ref[3, 1] = (dz_f * yc).astype(BF)
            dw_ref[0:1, 1, :] += jnp.sum(dyc * u_up, axis=0, keepdims=True)
            dw_ref[1:2, 1, :] += jnp.sum(dyc * u, axis=0, keepdims=True)
            dw_ref[2:3, 1, :] += jnp.sum(dyc * u_dn, axis=0, keepdims=True)

    hb = tm // GRID_W

    def dq_prev(j, i):
        return (1, jnp.maximum(jnp.minimum(i, nl - 1) * hb - 1, 0), j)

    def dq_next(j, i):
        return (1, jnp.minimum((jnp.minimum(i, nl - 1) + 1) * hb, nl * hb - 1), j)

    hp, hn = _conv_halo_specs(tm, tc, nl, 4)
    (dp42, dcw), extra = _hosted_call(
        body, xch, grid=(half // tc, nt),
        in_specs=[pl.BlockSpec((2, tm, tc), lambda j, i: (0, i, j)),
                  pl.BlockSpec((1, GRID_W, tc), dq_prev), pl.BlockSpec((1, GRID_W, tc), dq_next),
                  pl.BlockSpec((4, 2, tm, tc), lambda j, i: (0, 0, i, j)), hp, hn,
                  pl.BlockSpec((3, 2, tc), lambda j, i: (0, 0, j))],
        out_specs=[pl.BlockSpec((4, 2, tm, tc), lambda j, i: (0, 0, i, j)), pl.BlockSpec((8, 2, tc), lambda j, i: (0, 0, j))],
        out_shape=[jax.ShapeDtypeStruct(p42.shape, BF), jax.ShapeDtypeStruct((8, 2, half), F32)],
        scratch=[], args=(dq3, dq3, dq3, p42, p42, p42, cw), name="l0_conv_bwd")
    return dp42, dcw, extra


def _bwd_inproj0(dp42, w_in, x, ctx, dr_l, dr_c, a2, tm, xch=None):
    l, d = x.shape
    nl, nc = l // tm, ctx.shape[0] // tm
    e = w_in.shape[1] // 4
    half = e // 2

    def body(dp_ref, w_hbm, x_ref, c_ref, dl_ref, dc_ref, a_ref, gx_ref, acc_ref, w_ref):
        i = pl.program_id(0)

        @pl.when(i == 0)
        def _():
            pltpu.sync_copy(w_hbm, w_ref)
            acc_ref[...] = jnp.zeros_like(acc_ref)

        is_ctx = i >= nl
        dh = jnp.zeros((tm, d), F32)
        for k in range(4):
            for hv in range(2):
                c0 = k * e + hv * half
                dh = dh + _dot_nt(dp_ref[k, hv], w_ref[:, c0:c0 + half])
        xv = jnp.where(is_ctx, c_ref[...], x_ref[...])
        s_sc = jnp.sum(dh * xv, axis=0, keepdims=True)
        s_sh = jnp.sum(dh, axis=0, keepdims=True)
        sel = is_ctx.astype(F32)
        acc_ref[0:1, :] += s_sc * (1.0 - sel)
        acc_ref[1:2, :] += s_sc * sel
        acc_ref[2:3, :] += s_sh * (1.0 - sel)
        acc_ref[3:4, :] += s_sh * sel

        @pl.when(jnp.logical_not(is_ctx))
        def _():
            gx_ref[...] = DN_ALPHA * dl_ref[...].astype(F32) + dh * a_ref[0:1, :]

    lat, cx = _lat_or_ctx_specs(tm, d, nl, 1, 0)
    (gx, acc), extra = _hosted_call(
        body, xch, grid=(nl + nc,),
        in_specs=[pl.BlockSpec((4, 2, tm, half), lambda i: (0, 0, i, 0)), ANY, lat, cx, lat, cx, _full((2, d))],
        out_specs=[pl.BlockSpec((tm, d), lambda i: (jnp.minimum(i, nl - 1), 0)), _full((8, d))],
        out_shape=[jax.ShapeDtypeStruct((l, d), F32), jax.ShapeDtypeStruct((8, d), F32)],
        scratch=[pltpu.VMEM(w_in.shape, BF)], args=(dp42, w_in, x, ctx, dr_l, dr_c, a2), name="l0_bwd_inproj")
    return gx, acc, extra


def _dw_inproj0(x, ctx, a2, b2, dp42, tm):
    l, d = x.shape
    lc = ctx.shape[0]
    assert lc == tm
    tl = 4 * tm if l % (4 * tm) == 0 else tm
    nl = l // tl
    half = dp42.shape[-1]
    e = 2 * half

    def body(x_ref, c_ref, a_ref, b_ref, dpl_ref, dpc_ref, o_ref, acc_ref):
        i = pl.program_id(1)

        @pl.when(i == 0)
        def _():
            acc_ref[...] = jnp.zeros_like(acc_ref)

        def add(rows_ref, dp_ref, sel):
            h = (rows_ref[...] * a_ref[sel:sel + 1, :] + b_ref[sel:sel + 1, :]).astype(BF)
            acc_ref[:, :half] += _dot_tn(h, dp_ref[0, 0])
            acc_ref[:, half:] += _dot_tn(h, dp_ref[0, 1])

        @pl.when(i < nl)
        def _():
            add(x_ref, dpl_ref, 0)

        @pl.when(i == nl)
        def _():
            add(c_ref, dpc_ref, 1)
            o_ref[...] = acc_ref[...].astype(BF)

    return pl.pallas_call(
        body, name="l0_dw_inproj", grid=(4, nl + 1),
        in_specs=[pl.BlockSpec((tl, d), lambda k, i: (jnp.minimum(i, nl - 1), 0)), _full((lc, d)),
                  _full((2, d)), _full((2, d)),
                  pl.BlockSpec((1, 2, tl, half), lambda k, i: (k, 0, jnp.minimum(i, nl - 1), 0)),
                  pl.BlockSpec((1, 2, lc, half), lambda k, i: (k, 0, l // lc, 0))],
        out_specs=pl.BlockSpec((d, e), lambda k, i: (0, k)),
        out_shape=jax.ShapeDtypeStruct((d, 4 * e), BF),
        scratch_shapes=[pltpu.VMEM((d, e), F32)], compiler_params=_cparams(),
    )(x, ctx, a2, b2, dp42, dp42)


def _dw_outproj0(q3, dr_l, dr_c, gt2, tm):
    l, d = dr_l.shape
    nl, nc = l // tm, dr_c.shape[0] // tm
    _, r, half = q3.shape
    nt = nl + nc

    def body(q_ref, dl_ref, dc_ref, g_ref, o_ref, acc_ref):
        i = pl.program_id(0)
        is_ctx = i >= nl

        @pl.when(i == 0)
        def _():
            acc_ref[...] = jnp.zeros_like(acc_ref)

        dr = jnp.where(is_ctx, dc_ref[...], dl_ref[...]).astype(F32)
        dfx = (dr * _sel_row(g_ref, is_ctx)).astype(BF)
        acc_ref[:half, :] += _dot_tn(q_ref[0], dfx)
        acc_ref[half:, :] += _dot_tn(q_ref[1], dfx)

        @pl.when(i == nt - 1)
        def _():
            o_ref[...] = acc_ref[...].astype(BF)

    lat, cx = _lat_or_ctx_specs(tm, d, nl, 1, 0)
    return pl.pallas_call(
        body, name="l0_dw_outproj", grid=(nt,),
        in_specs=[pl.BlockSpec((2, tm, half), lambda i: (0, i, 0)), lat, cx, _full((2, d))],
        out_specs=_full((2 * half, d)), out_shape=jax.ShapeDtypeStruct((2 * half, d), BF),
        scratch_shapes=[pltpu.VMEM((2 * half, d), F32)], compiler_params=_cparams(),
    )(q3, dr_l, dr_c, gt2)


def _cr_tile(j, cap=256):
    for cand in (1024, 512, 256, 128, 64, 32, 16, 8):
        if cand <= cap and j % cand == 0:
            return cand
    raise ValueError(j)


def _inproj1(xh_cr, a1, b1, w, tag):
    j, d16 = xh_cr.shape
    d = d16 // CHUNK
    e = w.shape[1] // 2
    nb = e // LANE_BLOCK
    tj = _cr_tile(j)

    def body(x_ref, a_ref, b_ref, w_hbm, u_ref, z_ref, w_ref):
        @pl.when(jnp.logical_and(pl.program_id(0) == 0, pl.program_id(1) == 0))
        def _():
            pltpu.sync_copy(w_hbm, w_ref)

        h = (x_ref[...] * a_ref[...] + b_ref[...]).astype(BF)
        r = _dot(h, w_ref[...])
        for b in range(nb):
            u_ref[b] = r[:, b * LANE_BLOCK:(b + 1) * LANE_BLOCK].astype(BF)
        z_ref[...] = r[:, e:].astype(BF)

    return pl.pallas_call(
        body, name="l1_inproj_" + tag, grid=(j // tj, CHUNK),
        in_specs=[pl.BlockSpec((tj, d), lambda t, s: (t, s)), _full((1, d)), _full((1, d)), ANY],
        out_specs=[pl.BlockSpec((nb, tj, LANE_BLOCK), lambda t, s: (0, t, s)), pl.BlockSpec((tj, e), lambda t, s: (t, s))],
        out_shape=[jax.ShapeDtypeStruct((nb, j, BCR_W), BF), jax.ShapeDtypeStruct((j, CHUNK * e), BF)],
        scratch_shapes=[pltpu.VMEM(w.shape, BF)], compiler_params=_cparams(),
    )(xh_cr, a1, b1, w)


def _bmm(a_list, w_list, trans, out_dtype, name, ctx=None):
    nb, j, ka = a_list[0].shape
    n_out = w_list[0].shape[1] if trans[0] else w_list[0].shape[2]
    tn = n_out // 2
    tj = _cr_tile(j, 512)
    n = len(a_list)
    c_idx = [i for i in range(n) if ctx is not None and ctx[i] is not None]
    c_list = [ctx[i] for i in c_idx]
    nc = len(c_list)

    def body(*refs):
        w_refs = refs[n:2 * n]

        def product(a_refs, idx):
            acc = None
            for a_ref, i in zip(a_refs, idx):
                a = a_ref[0].astype(BF)
                t = _dot_nt(a, w_refs[i][0]) if trans[i] else _dot(a, w_refs[i][0])
                acc = t if acc is None else acc + t
            return acc.astype(out_dtype)

        refs[2 * n + nc][0] = product(refs[:n], range(n))
        if nc:
            @pl.when(pl.program_id(2) == 0)
            def _():
                refs[2 * n + nc + 1][0] = product(refs[2 * n:2 * n + nc], c_idx)

    a_specs = [pl.BlockSpec((1, tj, a.shape[2]), lambda b, h, t: (b, t, 0)) for a in a_list]
    w_specs = [pl.BlockSpec((1, tn, w.shape[2]), lambda b, h, t: (b, h, 0)) if tr
               else pl.BlockSpec((1, w.shape[1], tn), lambda b, h, t: (b, 0, h)) for w, tr in zip(w_list, trans)]
    c_specs = [pl.BlockSpec((1, a.shape[1], a.shape[2]), lambda b, h, t: (b, 0, 0)) for a in c_list]
    out_specs = [pl.BlockSpec((1, tj, tn), lambda b, h, t: (b, t, h))]
    out_shape = [jax.ShapeDtypeStruct((nb, j, n_out), out_dtype)]
    if nc:
        jc = c_list[0].shape[1]
        out_specs.append(pl.BlockSpec((1, jc, tn), lambda b, h, t: (b, 0, h)))
        out_shape.append(jax.ShapeDtypeStruct((nb, jc, n_out), out_dtype))
    res = pl.pallas_call(
        body, name=name, grid=(nb, 2, j // tj), in_specs=a_specs + w_specs + c_specs,
        out_specs=out_specs, out_shape=out_shape, compiler_params=_cparams(),
    )(*a_list, *w_list, *c_list)
    return res if nc else res[0]


def _group_mask(lane_groups):
    row = lax.broadcasted_iota(jnp.int32, (LANE_BLOCK, LANE_BLOCK), 0) // S5_P
    lane = lax.broadcasted_iota(jnp.int32, (LANE_BLOCK, LANE_BLOCK), 1)
    return row == lane_groups(lane)


def _expand_toeplitz(wcomp):
    nb = wcomp.shape[0]
    nd = 2 * CHUNK - 1

    def body(c_ref, o_ref):
        mask = _group_mask(lambda lane: lane // S5_P)
        tiles = []
        for dd in range(nd):
            m = c_ref[0, dd]
            tiles.append(jnp.where(mask, jnp.concatenate([m] * GROUPS_PER_BLOCK, axis=1), 0.0).astype(BF))
        for s in range(CHUNK):
            for t in range(CHUNK):
                o_ref[0, s * LANE_BLOCK:(s + 1) * LANE_BLOCK, t * LANE_BLOCK:(t + 1) * LANE_BLOCK] = tiles[t - s + CHUNK - 1]

    return pl.pallas_call(
        body, name="l1_expand_toeplitz", grid=(nb,),
        in_specs=[pl.BlockSpec((1, nd, LANE_BLOCK, S5_P), lambda b: (b, 0, 0, 0))],
        out_specs=pl.BlockSpec((1, BCR_W, BCR_W), lambda b: (b, 0, 0)),
        out_shape=jax.ShapeDtypeStruct((nb, BCR_W, BCR_W), BF), compiler_params=_cparams(),
    )(wcomp)


def _expand_blocks(comp, name):
    nb = comp.shape[2]
    lanes_per_dir = ZL_W // 2

    def body(c_ref, o_ref):
        masks = [_group_mask(lambda lane, lb=lb: 2 * lb + lane // S5_N) for lb in range(4)]
        for r in range(2):
            for s in range(CHUNK):
                for ri in range(2):
                    m = c_ref[r, s, 0, :, ri * S5_N:(ri + 1) * S5_N]
                    mm = jnp.concatenate([m, m], axis=1)
                    for lb in range(4):
                        c0 = r * lanes_per_dir + ri * ZH + lb * LANE_BLOCK
                        o_ref[0, s * LANE_BLOCK:(s + 1) * LANE_BLOCK, c0:c0 + LANE_BLOCK] = (
                            jnp.where(masks[lb], mm, 0.0).astype(BF))

    return pl.pallas_call(
        body, name=name, grid=(nb,),
        in_specs=[pl.BlockSpec((2, CHUNK, 1, LANE_BLOCK, LANE_BLOCK), lambda b: (0, 0, b, 0, 0))],
        out_specs=pl.BlockSpec((1, BCR_W, ZL_W), lambda b: (b, 0, 0)),
        out_shape=jax.ShapeDtypeStruct((nb, BCR_W, ZL_W), BF), compiler_params=_cparams(),
    )(comp)


def _bdw(a, b_, kind, ctx, name):
    nb, j, ka = a.shape
    kb = b_.shape[2]
    tn = kb // 2
    tj = _cr_tile(j, 1024)
    nt = j // tj
    has_ctx = ctx is not None
    nd = 2 * CHUNK - 1

    def body(*refs):
        a_ref, b_ref = refs[0], refs[1]
        o_ref, acc_ref = refs[2 + 2 * has_ctx], refs[3 + 2 * has_ctx]
        h, t = pl.program_id(1), pl.program_id(2)

        @pl.when(t == 0)
        def _():
            if has_ctx:
                acc_ref[...] = _dot_tn(refs[2][0].astype(BF), refs[3][0].astype(BF))
            else:
                acc_ref[...] = jnp.zeros_like(acc_ref)

        acc_ref[...] += _dot_tn(a_ref[0].astype(BF), b_ref[0].astype(BF))

        if kind == "toeplitz":
            diag_ref = refs[4 + 2 * has_ctx]

            @pl.when(jnp.logical_and(t == 0, h == 0))
            def _():
                diag_ref[...] = jnp.zeros_like(diag_ref)

            @pl.when(t == nt - 1)
            def _():
                for s in range(CHUNK):
                    for tl in range(CHUNK // 2):
                        dd = h * (CHUNK // 2) + (tl - s + CHUNK - 1)
                        diag_ref[dd] += acc_ref[s * LANE_BLOCK:(s + 1) * LANE_BLOCK, tl * LANE_BLOCK:(tl + 1) * LANE_BLOCK]

            @pl.when(jnp.logical_and(t == nt - 1, h == 1))
            def _():
                mask = _group_mask(lambda lane: lane // S5_P)
                for dd in range(nd):
                    v = jnp.where(mask, diag_ref[dd], 0.0)
                    acc = v[:, :S5_P]
                    for k in range(1, GROUPS_PER_BLOCK):
                        acc = acc + v[:, k * S5_P:(k + 1) * S5_P]
                    o_ref[0, dd] = acc
        else:
            @pl.when(t == nt - 1)
            def _():
                masks = [_group_mask(lambda lane, lb=lb: 2 * lb + lane // S5_N) for lb in range(4)]
                for s in range(CHUNK):
                    for ri in range(2):
                        v = None
                        for lb in range(4):
                            c0 = ri * ZH + lb * LANE_BLOCK
                            blk = acc_ref[s * LANE_BLOCK:(s + 1) * LANE_BLOCK, c0:c0 + LANE_BLOCK]
                            blk = jnp.where(masks[lb], blk, 0.0)
                            v = blk if v is None else v + blk
                        o_ref[0, s, 0, :, ri * S5_N:(ri + 1) * S5_N] = v[:, :S5_N] + v[:, S5_N:]

    in_specs = [pl.BlockSpec((1, tj, ka), lambda b, h, t: (b, t, 0)), pl.BlockSpec((1, tj, tn), lambda b, h, t: (b, t, h))]
    args = [a, b_]
    if has_ctx:
        jc = ctx[0].shape[1]
        in_specs += [pl.BlockSpec((1, jc, ka), lambda b, h, t: (b, 0, 0)), pl.BlockSpec((1, jc, tn), lambda b, h, t: (b, 0, h))]
        args += list(ctx)
    scratch = [pltpu.VMEM((ka, tn), F32)]
    if kind == "toeplitz":
        ospec = pl.BlockSpec((1, nd, LANE_BLOCK, S5_P), lambda b, h, t: (b, 0, 0, 0))
        oshape = jax.ShapeDtypeStruct((nb, nd, LANE_BLOCK, S5_P), F32)
        scratch.append(pltpu.VMEM((nd, LANE_BLOCK, LANE_BLOCK), F32))
    else:
        ospec = pl.BlockSpec((1, CHUNK, 1, LANE_BLOCK, LANE_BLOCK), lambda b, h, t: (h, 0, b, 0, 0))
        oshape = jax.ShapeDtypeStruct((2, CHUNK, nb, LANE_BLOCK, LANE_BLOCK), F32)
    return pl.pallas_call(
        body, name=name, grid=(nb, 2, nt), in_specs=in_specs, out_specs=ospec, out_shape=oshape,
        scratch_shapes=scratch, compiler_params=_cparams(),
    )(*args)


def _scan(z_l, z_c, coef, chains, conj, s_l=None, s_c=None, name="l1_scan"):
    nb, jl, _ = z_l.shape
    jc = z_c.shape[1]
    with_da = s_l is not None
    sign = -1.0 if conj else 1.0
    hw = 2 * ZH

    def body(*refs):
        zl_ref, zc_ref, cf_ref = refs[:3]
        k = 3
        if with_da:
            sl_ref, sc_ref = refs[3:5]
            k = 5
        ol_ref, oc_ref = refs[k:k + 2]
        d = pl.program_id(1)
        rowi = lax.broadcasted_iota(jnp.int32, (8, ZH), 0)

        def coef_rows(r0, nr):
            return cf_ref[0, 0, r0:r0 + nr, :ZH], sign * cf_ref[0, 0, r0:r0 + nr, ZH:]

        steps = [(1, coef_rows(0, 1)), (2, coef_rows(1, 1)), (4, coef_rows(2, 1))]

        def run(chain):
            carry = (jnp.zeros((1, ZH), F32), jnp.zeros((1, ZH), F32))
            da = (jnp.zeros((8, ZH), F32), jnp.zeros((8, ZH), F32))
            for which, rev in chain:
                src, dst = (zc_ref, oc_ref) if which == "c" else (zl_ref, ol_ref)
                sref = (sc_ref if which == "c" else sl_ref) if with_da else None
                ng = (jc if which == "c" else jl) // 8
                tr, ti = coef_rows(16, 8) if rev else coef_rows(8, 8)

                def step(it, st, src=src, dst=dst, sref=sref, ng=ng, tr=tr, ti=ti, rev=rev):
                    cr_, ci_, dar, dai = st
                    g = (ng - 1 - it) if rev else it
                    off = pl.multiple_of(g * 8, 8)
                    xr = src[0, pl.ds(off, 8), :ZH]
                    xi = src[0, pl.ds(off, 8), ZH:]
                    for sh, (ar, ai) in steps:
                        if rev:
                            keep = rowi < 8 - sh
                            sr = jnp.where(keep, pltpu.roll(xr, 8 - sh, 0), 0.0)
                            si = jnp.where(keep, pltpu.roll(xi, 8 - sh, 0), 0.0)
                        else:
                            keep = rowi >= sh
                            sr = jnp.where(keep, pltpu.roll(xr, sh, 0), 0.0)
                            si = jnp.where(keep, pltpu.roll(xi, sh, 0), 0.0)
                        xr, xi = xr + ar * sr - ai * si, xi + ar * si + ai * sr
                    ir = xr + tr * cr_ - ti * ci_
                    ii = xi + tr * ci_ + ti * cr_
                    if rev:
                        er = jnp.where(rowi == 7, cr_, pltpu.roll(ir, 7, 0))
                        ei = jnp.where(rowi == 7, ci_, pltpu.roll(ii, 7, 0))
                        ncr, nci = ir[0:1], ii[0:1]
                    else:
                        er = jnp.where(rowi == 0, cr_, pltpu.roll(ir, 1, 0))
                        ei = jnp.where(rowi == 0, ci_, pltpu.roll(ii, 1, 0))
                        ncr, nci = ir[7:8], ii[7:8]
                    dst[0, pl.ds(off, 8), :ZH] = er
                    dst[0, pl.ds(off, 8), ZH:] = ei
                    if sref is not None:
                        s_r = sref[0, pl.ds(off, 8), :ZH]
                        s_i = sref[0, pl.ds(off, 8), ZH:]
                        dar = dar + s_r * er + s_i * ei
                        dai = dai + s_r * ei - s_i * er
                    return ncr, nci, dar, dai

                carry_da = lax.fori_loop(0, ng, step, (*carry, *da))
                carry, da = carry_da[:2], carry_da[2:]
            if with_da:
                refs[k + 2][0, 0] = jnp.concatenate([da[0], da[1]], axis=1)

        for dd in range(2):
            @pl.when(d == dd)
            def _(dd=dd):
                run(chains[dd])

    zspec_l = pl.BlockSpec((1, jl, hw), lambda b, d: (b, 0, d))
    zspec_c = pl.BlockSpec((1, jc, hw), lambda b, d: (b, 0, d))
    in_specs = [zspec_l, zspec_c, pl.BlockSpec((1, 1, 24, hw), lambda b, d: (b, d, 0, 0))]
    args = [z_l, z_c, coef]
    out_specs = [zspec_l, zspec_c]
    out_shape = [jax.ShapeDtypeStruct(z_l.shape, F32), jax.ShapeDtypeStruct(z_c.shape, F32)]
    if with_da:
        in_specs += [zspec_l, zspec_c]
        args += [s_l, s_c]
        out_specs.append(pl.BlockSpec((1, 1, 8, hw), lambda b, d: (b, d, 0, 0)))
        out_shape.append(jax.ShapeDtypeStruct((nb, 2, 8, hw), F32))
    return pl.pallas_call(body, name=name, grid=(nb, 2), in_specs=in_specs, out_specs=out_specs,
                          out_shape=out_shape, compiler_params=_cparams())(*args)


def _glu_fwd(y_bcr, z_cr, w_glu, b_glu):
    nb, j, _ = y_bcr.shape
    e = nb * LANE_BLOCK
    tj = _cr_tile(j)

    def body(y_ref, z_ref, w_hbm, b_ref, o_ref, sg_ref, w_ref):
        @pl.when(jnp.logical_and(pl.program_id(0) == 0, pl.program_id(1) == 0))
        def _():
            pltpu.sync_copy(w_hbm, w_ref)

        y = jnp.concatenate([y_ref[b] for b in range(nb)], axis=1).astype(F32)
        g = _gelu_parts(y)[0]
        sg = _sigmoid(_dot(g.astype(BF), w_ref[...]) + b_ref[...])
        z = z_ref[...].astype(F32)
        o_ref[...] = (g * sg * (z * _sigmoid(z))).astype(BF)
        sg_ref[...] = sg.astype(BF)

    tok = pl.BlockSpec((tj, e), lambda t, s: (t, s))
    return pl.pallas_call(
        body, name="l1_glu_fwd", grid=(j // tj, CHUNK),
        in_specs=[pl.BlockSpec((nb, tj, LANE_BLOCK), lambda t, s: (0, t, s)), tok, ANY, _full((1, e))],
        out_specs=[tok, tok],
        out_shape=[jax.ShapeDtypeStruct((j, CHUNK * e), BF), jax.ShapeDtypeStruct((j, CHUNK * e), BF)],
        scratch_shapes=[pltpu.VMEM(w_glu.shape, BF)], compiler_params=_cparams(),
    )(y_bcr, z_cr, w_glu, b_glu)


def _final(w_cr, w_out, xh_cr, tgt_cr, vecs):
    j, e16 = w_cr.shape
    e = e16 // CHUNK
    d = w_out.shape[1]
    tj = _cr_tile(j)

    def body(w_ref, wo_hbm, xh_ref, t_ref, v_ref, dr_ref, acc_ref, wo_ref):
        @pl.when(jnp.logical_and(pl.program_id(0) == 0, pl.program_id(1) == 0))
        def _():
            pltpu.sync_copy(wo_hbm, wo_ref)
            acc_ref[...] = jnp.zeros_like(acc_ref)

        o = _dot(w_ref[...], wo_ref[...])
        x1 = xh_ref[...] * v_ref[0:1, :] + v_ref[1:2, :]
        rr = DN_ALPHA * x1 + v_ref[2:3, :] * o
        mu = jnp.mean(rr, axis=-1, keepdims=True)
        cen = rr - mu
        rstd = lax.rsqrt(jnp.mean(cen * cen, axis=-1, keepdims=True) + LN_EPS)
        xh2 = cen * rstd
        err = xh2 * v_ref[3:4, :] + v_ref[4:5, :] - t_ref[...]
        dy = err * (1.0 / d)
        dxh = dy * v_ref[3:4, :]
        dr = rstd * (dxh - jnp.mean(dxh, axis=-1, keepdims=True) - xh2 * jnp.mean(dxh * xh2, axis=-1, keepdims=True))
        dr_ref[...] = dr.astype(BF)
        acc_ref[0:1, :] += jnp.sum(dy * xh2, axis=0, keepdims=True)
        acc_ref[1:2, :] += jnp.sum(dy, axis=0, keepdims=True)
        acc_ref[2:3, :] += jnp.sum(dr * o, axis=0, keepdims=True)
        acc_ref[3:4, :] += (0.5 / d) * jnp.sum(err * err, axis=0, keepdims=True)

    tok_d = pl.BlockSpec((tj, d), lambda t, s: (t, s))
    return pl.pallas_call(
        body, name="l1_final", grid=(j // tj, CHUNK),
        in_specs=[pl.BlockSpec((tj, e), lambda t, s: (t, s)), ANY, tok_d, tok_d, _full((8, d))],
        out_specs=[tok_d, _full((8, d))],
        out_shape=[jax.ShapeDtypeStruct((j, CHUNK * d), BF), jax.ShapeDtypeStruct((8, d), F32)],
        scratch_shapes=[pltpu.VMEM(w_out.shape, BF)], compiler_params=_cparams(),
    )(w_cr, w_out, xh_cr, tgt_cr, vecs)


def _glu_bwd(dr_cr, gt1, w_out, w_glu, y_bcr, z_cr, sg_cr):
    nb, j, _ = y_bcr.shape
    e, d = w_out.shape
    tj = _cr_tile(j)

    def body(dr_ref, g_ref, wo_hbm, wg_hbm, y_ref, z_ref, sg_ref, dz_ref, dt_ref, dy_ref, wo_ref, wg_ref):
        @pl.when(jnp.logical_and(pl.program_id(0) == 0, pl.program_id(1) == 0))
        def _():
            pltpu.sync_copy(wo_hbm, wo_ref)
            pltpu.sync_copy(wg_hbm, wg_ref)

        do = (dr_ref[...].astype(F32) * g_ref[...]).astype(BF)
        dw = _dot_nt(do, wo_ref[...])
        y = jnp.concatenate([y_ref[b] for b in range(nb)], axis=1).astype(F32)
        g, dgel = _gelu_parts(y)
        z = z_ref[...].astype(F32)
        sz = _sigmoid(z)
        sg = sg_ref[...].astype(F32)
        dg2 = dw * (z * sz)
        dz_ref[...] = (dw * g * sg * (sz * (1.0 + z * (1.0 - sz)))).astype(BF)
        dt = (dg2 * g * sg * (1.0 - sg)).astype(BF)
        dt_ref[...] = dt
        dy = (dg2 * sg + _dot_nt(dt, wg_ref[...])) * dgel
        for b in range(nb):
            dy_ref[b] = dy[:, b * LANE_BLOCK:(b + 1) * LANE_BLOCK].astype(BF)

    tok_e = pl.BlockSpec((tj, e), lambda t, s: (t, s))
    blk = pl.BlockSpec((nb, tj, LANE_BLOCK), lambda t, s: (0, t, s))
    return pl.pallas_call(
        body, name="l1_glu_bwd", grid=(j // tj, CHUNK),
        in_specs=[pl.BlockSpec((tj, d), lambda t, s: (t, s)), _full((1, d)), ANY, ANY, blk, tok_e, tok_e],
        out_specs=[tok_e, tok_e, blk],
        out_shape=[jax.ShapeDtypeStruct((j, CHUNK * e), BF), jax.ShapeDtypeStruct((j, CHUNK * e), BF),
                   jax.ShapeDtypeStruct((nb, j, BCR_W), BF)],
        scratch_shapes=[pltpu.VMEM(w_out.shape, BF), pltpu.VMEM(w_glu.shape, BF)], compiler_params=_cparams(),
    )(dr_cr, gt1, w_out, w_glu, y_bcr, z_cr, sg_cr)


def _bwd_inproj1(du_bcr, dz_cr, w, xh_cr, rs_cr, dr2_cr, vecs, tag):
    nb, j, _ = du_bcr.shape
    d = w.shape[0]
    e = w.shape[1] // 2
    tj = _cr_tile(j)

    def body(du_ref, dz_ref, w_hbm, xh_ref, rs_ref, dr2_ref, v_ref, dr1_ref, acc_ref, w_ref):
        @pl.when(jnp.logical_and(pl.program_id(0) == 0, pl.program_id(1) == 0))
        def _():
            pltpu.sync_copy(w_hbm, w_ref)
            acc_ref[...] = jnp.zeros_like(acc_ref)

        du = jnp.concatenate([du_ref[b] for b in range(nb)], axis=1)
        dh = _dot_nt(du, w_ref[:, :e]) + _dot_nt(dz_ref[...], w_ref[:, e:])
        xh = xh_ref[...]
        x1 = xh * v_ref[0:1, :] + v_ref[1:2, :]
        dx1 = DN_ALPHA * dr2_ref[...].astype(F32) + dh * v_ref[2:3, :]
        dxh = dx1 * v_ref[0:1, :]
        rstd = rs_ref[:, 0:1]
        dr1 = rstd * (dxh - jnp.mean(dxh, axis=-1, keepdims=True) - xh * jnp.mean(dxh * xh, axis=-1, keepdims=True))
        dr1_ref[...] = dr1.astype(BF)
        acc_ref[0:1, :] += jnp.sum(dh * x1, axis=0, keepdims=True)
        acc_ref[1:2, :] += jnp.sum(dh, axis=0, keepdims=True)
        acc_ref[2:3, :] += jnp.sum(dx1 * xh, axis=0, keepdims=True)
        acc_ref[3:4, :] += jnp.sum(dx1, axis=0, keepdims=True)

    tok_d = pl.BlockSpec((tj, d), lambda t, s: (t, s))
    return pl.pallas_call(
        body, name="l1_bwd_inproj_" + tag, grid=(j // tj, CHUNK),
        in_specs=[pl.BlockSpec((nb, tj, LANE_BLOCK), lambda t, s: (0, t, s)), pl.BlockSpec((tj, e), lambda t, s: (t, s)),
                  ANY, tok_d, pl.BlockSpec((tj, 128), lambda t, s: (t, s)), tok_d, _full((8, d))],
        out_specs=[tok_d, _full((8, d))],
        out_shape=[jax.ShapeDtypeStruct((j, CHUNK * d), BF), jax.ShapeDtypeStruct((8, d), F32)],
        scratch_shapes=[pltpu.VMEM(w.shape, BF)], compiler_params=_cparams(),
    )(du_bcr, dz_cr, w, xh_cr, rs_cr, dr2_cr, vecs)


def _dw_cr(lhs, rhs, lhs_kind, rhs_kind, vec, bias_sum, init, name):
    if lhs_kind == "gelu_bcr":
        nb_l, j, _ = lhs.shape
        k = nb_l * LANE_BLOCK
    else:
        j = lhs.shape[0]
        k = lhs.shape[1] // CHUNK
    if rhs_kind == "bcr":
        nb_r = rhs.shape[0]
        n = nb_r * LANE_BLOCK
    else:
        n = rhs.shape[1] // CHUNK
    tj = _cr_tile(j, 512)
    nh = 2 if k * n * 4 > (8 << 20) else 1
    tn = n // nh
    nbh = tn // LANE_BLOCK
    nt = j // tj
    has_init = init is not None

    def body(*refs):
        refs = list(refs)
        l_ref, r_ref = refs[0], refs[1]
        pos = 2
        v_ref = None
        if vec is not None:
            v_ref = refs[pos]
            pos += 1
        i_ref = None
        if has_init:
            i_ref = refs[pos]
            pos += 1
        o_ref = refs[pos]
        pos += 1
        bs_ref = None
        if bias_sum:
            bs_ref = refs[pos]
            pos += 1
        acc_ref = refs[pos]
        t, s = pl.program_id(1), pl.program_id(2)
        first = jnp.logical_and(t == 0, s == 0)

        @pl.when(first)
        def _():
            acc_ref[...] = i_ref[...] if has_init else jnp.zeros_like(acc_ref)
            if bias_sum:
                bs_ref[...] = jnp.zeros_like(bs_ref)

        if lhs_kind == "gelu_bcr":
            y = jnp.concatenate([l_ref[b] for b in range(nb_l)], axis=1).astype(F32)
            lv = _gelu_parts(y)[0].astype(BF)
        elif lhs_kind == "mod":
            lv = (l_ref[...] * v_ref[0:1, :] + v_ref[1:2, :]).astype(BF)
        else:
            lv = l_ref[...]
        if rhs_kind == "bcr":
            rv = jnp.concatenate([r_ref[b] for b in range(nbh)], axis=1)
        elif rhs_kind == "scaled":
            rv = (r_ref[...].astype(F32) * v_ref[0:1, :]).astype(BF)
        else:
            rv = r_ref[...]
        acc_ref[...] += _dot_tn(lv, rv)
        if bias_sum:
            bs_ref[0:1, :] += jnp.sum(rv.astype(F32), axis=0, keepdims=True)

        @pl.when(jnp.logical_and(t == nt - 1, s == CHUNK - 1))
        def _():
            o_ref[...] = acc_ref[...].astype(BF)

    if lhs_kind == "gelu_bcr":
        l_spec = pl.BlockSpec((nb_l, tj, LANE_BLOCK), lambda h, t, s: (0, t, s))
    else:
        l_spec = pl.BlockSpec((tj, k), lambda h, t, s: (t, s))
    if rhs_kind == "bcr":
        r_spec = pl.BlockSpec((nbh, tj, LANE_BLOCK), lambda h, t, s: (h, t, s))
    else:
        r_spec = pl.BlockSpec((tj, tn), lambda h, t, s: (t, s * nh + h))
    in_specs, args = [l_spec, r_spec], [lhs, rhs]
    if vec is not None:
        in_specs.append(_full(vec.shape))
        args.append(vec)
    o_spec = pl.BlockSpec((k, tn), lambda h, t, s: (0, h))
    if has_init:
        in_specs.append(o_spec)
        args.append(init)
    out_specs, out_shape = [o_spec], [jax.ShapeDtypeStruct((k, n), BF)]
    if bias_sum:
        out_specs.append(pl.BlockSpec((8, tn), lambda h, t, s: (0, h)))
        out_shape.append(jax.ShapeDtypeStruct((8, n), F32))
    res = pl.pallas_call(
        body, name=name, grid=(nh, nt, CHUNK), in_specs=in_specs, out_specs=out_specs, out_shape=out_shape,
        scratch_shapes=[pltpu.VMEM((k, tn), F32)], compiler_params=_cparams(),
    )(*args)
    return res if bias_sum else res[0]


def _dw_cr_f32(lhs, rhs, vec, name):
    j = lhs.shape[0]
    k = lhs.shape[1] // CHUNK
    nb_r = rhs.shape[0]
    n = nb_r * LANE_BLOCK
    tj = _cr_tile(j)
    nt = j // tj

    def body(l_ref, r_ref, v_ref, o_ref):
        @pl.when(jnp.logical_and(pl.program_id(0) == 0, pl.program_id(1) == 0))
        def _():
            o_ref[...] = jnp.zeros_like(o_ref)

        lv = (l_ref[...] * v_ref[0:1, :] + v_ref[1:2, :]).astype(BF)
        rv = jnp.concatenate([r_ref[b] for b in range(nb_r)], axis=1)
        o_ref[...] += _dot_tn(lv, rv)

    return pl.pallas_call(
        body, name=name, grid=(nt, CHUNK),
        in_specs=[pl.BlockSpec((tj, k), lambda t, s: (t, s)), pl.BlockSpec((nb_r, tj, LANE_BLOCK), lambda t, s: (0, t, s)),
                  _full(vec.shape)],
        out_specs=_full((k, n)), out_shape=jax.ShapeDtypeStruct((k, n), F32), compiler_params=_cparams(),
    )(lhs, rhs, vec)


GT_ROWS = CHUNK * S5_P
ZG_W = 2 * 2 * S5_N
PAIR_W = 2 * ZG_W
GROUPS_PER_STEP = 4


def _inproj1_gt(xh_cr, a1, b1, wu_t, w_z, tag):
    j, d16 = xh_cr.shape
    d = d16 // CHUNK
    e = wu_t.shape[0]
    g = e // S5_P
    tj = _cr_tile(j, 256)

    def body(x_ref, a_ref, b_ref, wu_hbm, wz_hbm, u_ref, z_ref, wu_ref, wz_ref):
        @pl.when(jnp.logical_and(pl.program_id(0) == 0, pl.program_id(1) == 0))
        def _():
            pltpu.sync_copy(wu_hbm, wu_ref)
            pltpu.sync_copy(wz_hbm, wz_ref)

        h = (x_ref[...] * a_ref[...] + b_ref[...]).astype(BF)
        u_ref[...] = _dot_nt(wu_ref[...], h).reshape(g, S5_P, tj).astype(BF)
        z_ref[...] = _dot(h, wz_ref[...]).astype(BF)

    return pl.pallas_call(
        body, name="l1_inproj_" + tag, grid=(j // tj, CHUNK),
        in_specs=[pl.BlockSpec((tj, d), lambda t, s: (t, s)), _full((1, d)), _full((1, d)), ANY, ANY],
        out_specs=[pl.BlockSpec((g, S5_P, tj), lambda t, s: (0, s, t)), pl.BlockSpec((tj, e), lambda t, s: (t, s))],
        out_shape=[jax.ShapeDtypeStruct((g, GT_ROWS, j), BF), jax.ShapeDtypeStruct((j, CHUNK * e), BF)],
        scratch_shapes=[pltpu.VMEM(wu_t.shape, BF), pltpu.VMEM(w_z.shape, BF)], compiler_params=_cparams(),
    )(xh_cr, a1, b1, wu_t, w_z)


def _gt_spec(j, gb=GROUPS_PER_STEP):
    return pl.BlockSpec((gb, GT_ROWS, j), lambda i: (i, 0, 0))


def _zg_spec(j, gb=GROUPS_PER_STEP):
    return pl.BlockSpec((j, gb * ZG_W), lambda i: (0, i))


def _w_spec(width, gb=GROUPS_PER_STEP):
    return pl.BlockSpec((gb, GT_ROWS, width), lambda i: (i, 0, 0))


def _pair_lanes(k):
    return slice((k // 2) * PAIR_W, (k // 2 + 1) * PAIR_W)


def _s5_z(ut_l, ut_c, bc):
    g, _, jl = ut_l.shape
    jc = ut_c.shape[2]
    gb = GROUPS_PER_STEP

    def body(ul_ref, uc_ref, bc_ref, zl_ref, zc_ref):
        for k in range(0, gb, 2):
            zl_ref[:, _pair_lanes(k)] = _dot_tn(ul_ref[k], bc_ref[k]) + _dot_tn(ul_ref[k + 1], bc_ref[k + 1])
            zc_ref[:, _pair_lanes(k)] = _dot_tn(uc_ref[k], bc_ref[k]) + _dot_tn(uc_ref[k + 1], bc_ref[k + 1])

    return pl.pallas_call(
        body, name="l1_s5_z", grid=(g // gb,), in_specs=[_gt_spec(jl), _gt_spec(jc), _w_spec(PAIR_W)],
        out_specs=[_zg_spec(jl), _zg_spec(jc)],
        out_shape=[jax.ShapeDtypeStruct((jl, g * ZG_W), F32), jax.ShapeDtypeStruct((jc, g * ZG_W), F32)],
        compiler_params=_cparams(),
    )(ut_l, ut_c, bc)


def _s5_y(ut_l, s_l, mt_t, cct):
    g, _, jl = ut_l.shape
    gb = GROUPS_PER_STEP

    def body(u_ref, s_ref, mt_ref, cc_ref, y_ref):
        for k in range(gb):
            s_k = s_ref[:, _pair_lanes(k)].astype(BF)
            y_ref[k] = (_dot(mt_ref[k], u_ref[k]) + _dot_nt(cc_ref[k], s_k)).astype(BF)

    return pl.pallas_call(
        body, name="l1_s5_y", grid=(g // gb,),
        in_specs=[_gt_spec(jl), _zg_spec(jl), _w_spec(GT_ROWS), _w_spec(PAIR_W)],
        out_specs=_gt_spec(jl), out_shape=jax.ShapeDtypeStruct((g, GT_ROWS, jl), BF), compiler_params=_cparams(),
    )(ut_l, s_l, mt_t, cct)


def _s5_ds(dyt_l, cct):
    g, _, jl = dyt_l.shape
    gb = GROUPS_PER_STEP

    def body(dy_ref, cc_ref, ds_ref):
        for k in range(0, gb, 2):
            ds_ref[:, _pair_lanes(k)] = _dot_tn(dy_ref[k], cc_ref[k]) + _dot_tn(dy_ref[k + 1], cc_ref[k + 1])

    return pl.pallas_call(
        body, name="l1_s5_ds", grid=(g // gb,), in_specs=[_gt_spec(jl), _w_spec(PAIR_W)], out_specs=_zg_spec(jl),
        out_shape=jax.ShapeDtypeStruct((jl, g * ZG_W), F32), compiler_params=_cparams(),
    )(dyt_l, cct)


def _s5_dx(dyt_l, dz_l, dz_c, mt, bc):
    g, _, jl = dyt_l.shape
    jc = dz_c.shape[0]
    gb = GROUPS_PER_STEP

    def body(dy_ref, dzl_ref, dzc_ref, mt_ref, bc_ref, dul_ref, duc_ref):
        for k in range(gb):
            dzl = dzl_ref[:, _pair_lanes(k)].astype(BF)
            dzc = dzc_ref[:, _pair_lanes(k)].astype(BF)
            dul_ref[k] = (_dot(mt_ref[k], dy_ref[k]) + _dot_nt(bc_ref[k], dzl)).astype(BF)
            duc_ref[k] = _dot_nt(bc_ref[k], dzc).astype(BF)

    return pl.pallas_call(
        body, name="l1_s5_dx", grid=(g // gb,),
        in_specs=[_gt_spec(jl), _zg_spec(jl), _zg_spec(jc), _w_spec(GT_ROWS), _w_spec(PAIR_W)],
        out_specs=[_gt_spec(jl), _gt_spec(jc)],
        out_shape=[jax.ShapeDtypeStruct((g, GT_ROWS, jl), BF), jax.ShapeDtypeStruct((g, GT_ROWS, jc), BF)],
        compiler_params=_cparams(),
    )(dyt_l, dz_l, dz_c, mt, bc)


def _s5_dw(ut_l, ut_c, dyt_l, dz_l, dz_c, s_l):
    g, _, jl = ut_l.shape
    jc = ut_c.shape[2]
    gb = GROUPS_PER_STEP

    def body(ul_ref, uc_ref, dy_ref, dzl_ref, dzc_ref, s_ref, dmt_ref, dbc_ref, dcc_ref):
        for k in range(gb):
            lanes = _pair_lanes(k)
            dmt_ref[k] = _dot_nt(ul_ref[k], dy_ref[k])
            dbc_ref[k] = (_dot(ul_ref[k], dzl_ref[:, lanes].astype(BF))
                          + _dot(uc_ref[k], dzc_ref[:, lanes].astype(BF)))
            dcc_ref[k] = _dot(dy_ref[k], s_ref[:, lanes].astype(BF))

    sd_m = jax.ShapeDtypeStruct((g, GT_ROWS, GT_ROWS), F32)
    sd_p = jax.ShapeDtypeStruct((g, GT_ROWS, PAIR_W), F32)
    return pl.pallas_call(
        body, name="l1_s5_dw", grid=(g // gb,),
        in_specs=[_gt_spec(jl), _gt_spec(jc), _gt_spec(jl), _zg_spec(jl), _zg_spec(jc), _zg_spec(jl)],
        out_specs=[_w_spec(GT_ROWS), _w_spec(PAIR_W), _w_spec(PAIR_W)], out_shape=[sd_m, sd_p, sd_p],
        compiler_params=_cparams(),
    )(ut_l, ut_c, dyt_l, dz_l, dz_c, s_l)


def _scan_g(z_l, z_c, coef, chains, conj, s_l=None, s_c=None, name="l1_scan"):
    jl, w_all = z_l.shape
    jc = z_c.shape[0]
    gb = GROUPS_PER_STEP
    wb = gb * ZG_W
    nch = wb // 256
    with_da = s_l is not None
    sign = -1.0 if conj else 1.0

    def body(*refs):
        zl_ref, zc_ref, cf_ref = refs[:3]
        k0 = 3
        if with_da:
            sl_ref, sc_ref = refs[3:5]
            k0 = 5
        ol_ref, oc_ref = refs[k0:k0 + 2]
        rowi = lax.broadcasted_iota(jnp.int32, (8, 128), 0)

        def lanes_of(ch):
            return slice(ch * 256, ch * 256 + 128), slice(ch * 256 + 128, (ch + 1) * 256)

        def coefs(ch, r0, nr):
            lr, li = lanes_of(ch)
            return cf_ref[r0:r0 + nr, lr], sign * cf_ref[r0:r0 + nr, li]

        def shift(v, sh, rev):
            if rev:
                return jnp.where(rowi < 8 - sh, pltpu.roll(v, 8 - sh, 0), 0.0)
            return jnp.where(rowi >= sh, pltpu.roll(v, sh, 0), 0.0)

        zero_row = jnp.zeros((1, 128), F32)
        zero_tile = jnp.zeros((8, 128), F32)
        carry = [zero_row] * (2 * nch)
        da = [zero_tile] * (2 * nch)
        for seg in range(len(chains[0])):
            which = chains[0][seg][0]
            assert chains[1][seg][0] == which
            revs = (chains[0][seg][1], chains[1][seg][1])
            src, dst = (zc_ref, oc_ref) if which == "c" else (zl_ref, ol_ref)
            sref = ((sc_ref if which == "c" else sl_ref) if with_da else None)
            ng = (jc if which == "c" else jl) // 8

            def step(it, st, src=src, dst=dst, sref=sref, ng=ng, revs=revs):
                carry_, da_ = list(st[:2 * nch]), list(st[2 * nch:])
                for ch in range(nch):
                    rev = revs[ch % 2]
                    lr, li = lanes_of(ch)
                    grp = (ng - 1 - it) if rev else it
                    off = pl.multiple_of(grp * 8, 8)
                    xr, xi = src[pl.ds(off, 8), lr], src[pl.ds(off, 8), li]
                    for sh, r0 in ((1, 0), (2, 1), (4, 2)):
                        ar, ai = coefs(ch, r0, 1)
                        sr, si = shift(xr, sh, rev), shift(xi, sh, rev)
                        xr, xi = xr + ar * sr - ai * si, xi + ar * si + ai * sr
                    tr, ti = coefs(ch, 16, 8) if rev else coefs(ch, 8, 8)
                    cr_, ci_ = carry_[2 * ch], carry_[2 * ch + 1]
                    ir = xr + tr * cr_ - ti * ci_
                    ii = xi + tr * ci_ + ti * cr_
                    if rev:
                        er = jnp.where(rowi == 7, cr_, pltpu.roll(ir, 7, 0))
                        ei = jnp.where(rowi == 7, ci_, pltpu.roll(ii, 7, 0))
                        carry_[2 * ch], carry_[2 * ch + 1] = ir[0:1], ii[0:1]
                    else:
                        er = jnp.where(rowi == 0, cr_, pltpu.roll(ir, 1, 0))
                        ei = jnp.where(rowi == 0, ci_, pltpu.roll(ii, 1, 0))
                        carry_[2 * ch], carry_[2 * ch + 1] = ir[7:8], ii[7:8]
                    dst[pl.ds(off, 8), lr] = er
                    dst[pl.ds(off, 8), li] = ei
                    if sref is not None:
                        s_r, s_i = sref[pl.ds(off, 8), lr], sref[pl.ds(off, 8), li]
                        da_[2 * ch] = da_[2 * ch] + s_r * er + s_i * ei
                        da_[2 * ch + 1] = da_[2 * ch + 1] + s_r * ei - s_i * er
                return (*carry_, *da_)

            st = lax.fori_loop(0, ng, step, (*carry, *da))
            carry, da = list(st[:2 * nch]), list(st[2 * nch:])
        if with_da:
            da_ref = refs[k0 + 2]
            for ch in range(nch):
                lr, li = lanes_of(ch)
                da_ref[:, lr] = da[2 * ch]
                da_ref[:, li] = da[2 * ch + 1]

    in_specs = [_zg_spec(jl), _zg_spec(jc), pl.BlockSpec((24, wb), lambda i: (0, i))]
    args = [z_l, z_c, coef]
    out_specs = [_zg_spec(jl), _zg_spec(jc)]
    out_shape = [jax.ShapeDtypeStruct(z_l.shape, F32), jax.ShapeDtypeStruct(z_c.shape, F32)]
    if with_da:
        in_specs += [_zg_spec(jl), _zg_spec(jc)]
        args += [s_l, s_c]
        out_specs.append(pl.BlockSpec((8, wb), lambda i: (0, i)))
        out_shape.append(jax.ShapeDtypeStruct((8, w_all), F32))
    return pl.pallas_call(body, name=name, grid=(w_all // wb,), in_specs=in_specs, out_specs=out_specs,
                          out_shape=out_shape, compiler_params=_cparams())(*args)


def _gt_tok_spec(g, tj):
    return pl.BlockSpec((g, S5_P, tj), lambda t, s: (0, s, t))


def _glu_fwd_gt(yt, z_cr, w_glu, b_glu):
    g, _, j = yt.shape
    e = g * S5_P
    tj = _cr_tile(j)

    def body(y_ref, z_ref, w_hbm, b_ref, o_ref, sg_ref, w_ref):
        @pl.when(jnp.logical_and(pl.program_id(0) == 0, pl.program_id(1) == 0))
        def _():
            pltpu.sync_copy(w_hbm, w_ref)

        y = jnp.transpose(y_ref[...].reshape(e, tj).astype(F32))
        gl = _gelu_parts(y)[0]
        sg = _sigmoid(_dot(gl.astype(BF), w_ref[...]) + b_ref[...])
        z = z_ref[...].astype(F32)
        o_ref[...] = (gl * sg * (z * _sigmoid(z))).astype(BF)
        sg_ref[...] = sg.astype(BF)

    tok = pl.BlockSpec((tj, e), lambda t, s: (t, s))
    return pl.pallas_call(
        body, name="l1_glu_fwd", grid=(j // tj, CHUNK),
        in_specs=[_gt_tok_spec(g, tj), tok, ANY, _full((1, e))], out_specs=[tok, tok],
        out_shape=[jax.ShapeDtypeStruct((j, CHUNK * e), BF), jax.ShapeDtypeStruct((j, CHUNK * e), BF)],
        scratch_shapes=[pltpu.VMEM(w_glu.shape, BF)], compiler_params=_cparams(),
    )(yt, z_cr, w_glu, b_glu)


def _glu_bwd_gt(dr_cr, gt1, w_out, w_glu, yt, z_cr, sg_cr):
    g, _, j = yt.shape
    e, d = w_out.shape
    tj = _cr_tile(j)

    def body(dr_ref, g_ref, wo_hbm, wg_hbm, y_ref, z_ref, sg_ref, dz_ref, dt_ref, dy_ref, wo_ref, wg_ref):
        @pl.when(jnp.logical_and(pl.program_id(0) == 0, pl.program_id(1) == 0))
        def _():
            pltpu.sync_copy(wo_hbm, wo_ref)
            pltpu.sync_copy(wg_hbm, wg_ref)

        do = (dr_ref[...].astype(F32) * g_ref[...]).astype(BF)
        dw = _dot_nt(do, wo_ref[...])
        y = jnp.transpose(y_ref[...].reshape(e, tj).astype(F32))
        gl, dgel = _gelu_parts(y)
        z = z_ref[...].astype(F32)
        sz = _sigmoid(z)
        sg = sg_ref[...].astype(F32)
        dg2 = dw * (z * sz)
        dz_ref[...] = (dw * gl * sg * (sz * (1.0 + z * (1.0 - sz)))).astype(BF)
        dt = (dg2 * gl * sg * (1.0 - sg)).astype(BF)
        dt_ref[...] = dt
        dy = (dg2 * sg + _dot_nt(dt, wg_ref[...])) * dgel
        dy_ref[...] = jnp.transpose(dy).reshape(g, S5_P, tj).astype(BF)

    tok_e = pl.BlockSpec((tj, e), lambda t, s: (t, s))
    return pl.pallas_call(
        body, name="l1_glu_bwd", grid=(j // tj, CHUNK),
        in_specs=[pl.BlockSpec((tj, d), lambda t, s: (t, s)), _full((1, d)), ANY, ANY, _gt_tok_spec(g, tj), tok_e, tok_e],
        out_specs=[tok_e, tok_e, _gt_tok_spec(g, tj)],
        out_shape=[jax.ShapeDtypeStruct((j, CHUNK * e), BF), jax.ShapeDtypeStruct((j, CHUNK * e), BF),
                   jax.ShapeDtypeStruct((g, GT_ROWS, j), BF)],
        scratch_shapes=[pltpu.VMEM(w_out.shape, BF), pltpu.VMEM(w_glu.shape, BF)], compiler_params=_cparams(),
    )(dr_cr, gt1, w_out, w_glu, yt, z_cr, sg_cr)


def _bwd_inproj1_gt(dut, dz_cr, wu_t, w_z, xh_cr, rs_cr, dr2_cr, vecs, tag):
    g, _, j = dut.shape
    e, d = wu_t.shape
    tj = _cr_tile(j)

    def body(du_ref, dz_ref, wu_hbm, wz_hbm, xh_ref, rs_ref, dr2_ref, v_ref, dr1_ref, acc_ref, wu_ref, wz_ref):
        @pl.when(jnp.logical_and(pl.program_id(0) == 0, pl.program_id(1) == 0))
        def _():
            pltpu.sync_copy(wu_hbm, wu_ref)
            pltpu.sync_copy(wz_hbm, wz_ref)
            acc_ref[...] = jnp.zeros_like(acc_ref)

        dh = _dot_tn(du_ref[...].reshape(e, tj), wu_ref[...]) + _dot_nt(dz_ref[...], wz_ref[...])
        xh = xh_ref[...]
        x1 = xh * v_ref[0:1, :] + v_ref[1:2, :]
        dx1 = DN_ALPHA * dr2_ref[...].astype(F32) + dh * v_ref[2:3, :]
        dxh = dx1 * v_ref[0:1, :]
        rstd = rs_ref[:, 0:1]
        dr1 = rstd * (dxh - jnp.mean(dxh, axis=-1, keepdims=True) - xh * jnp.mean(dxh * xh, axis=-1, keepdims=True))
        dr1_ref[...] = dr1.astype(BF)
        acc_ref[0:1, :] += jnp.sum(dh * x1, axis=0, keepdims=True)
        acc_ref[1:2, :] += jnp.sum(dh, axis=0, keepdims=True)
        acc_ref[2:3, :] += jnp.sum(dx1 * xh, axis=0, keepdims=True)
        acc_ref[3:4, :] += jnp.sum(dx1, axis=0, keepdims=True)

    tok_d = pl.BlockSpec((tj, d), lambda t, s: (t, s))
    return pl.pallas_call(
        body, name="l1_bwd_inproj_" + tag, grid=(j // tj, CHUNK),
        in_specs=[_gt_tok_spec(g, tj), pl.BlockSpec((tj, e), lambda t, s: (t, s)), ANY, ANY, tok_d,
                  pl.BlockSpec((tj, 128), lambda t, s: (t, s)), tok_d, _full((8, d))],
        out_specs=[tok_d, _full((8, d))],
        out_shape=[jax.ShapeDtypeStruct((j, CHUNK * d), BF), jax.ShapeDtypeStruct((8, d), F32)],
        scratch_shapes=[pltpu.VMEM(wu_t.shape, BF), pltpu.VMEM(w_z.shape, BF)], compiler_params=_cparams(),
    )(dut, dz_cr, wu_t, w_z, xh_cr, rs_cr, dr2_cr, vecs)


def _dw_gt(lhs_gt, rhs_cr, lhs_gelu, vec, bias_sum, init, out_dtype, name):
    g, _, j = lhs_gt.shape
    e = g * S5_P
    n = rhs_cr.shape[1] // CHUNK
    tj = _cr_tile(j, 512 if j % 512 == 0 else 256)
    nh = 2 if e * n * 4 > (8 << 20) else 1
    tn = n // nh
    nt = j // tj
    has_init = init is not None

    def body(*refs):
        refs = list(refs)
        l_ref, r_ref = refs[0], refs[1]
        pos = 2
        v_ref = i_ref = bs_ref = None
        if vec is not None:
            v_ref = refs[pos]
            pos += 1
        if has_init:
            i_ref = refs[pos]
            pos += 1
        o_ref = refs[pos]
        pos += 1
        if bias_sum:
            bs_ref = refs[pos]
            pos += 1
        acc_ref = refs[pos]
        t, s = pl.program_id(1), pl.program_id(2)

        @pl.when(jnp.logical_and(t == 0, s == 0))
        def _():
            acc_ref[...] = i_ref[...] if has_init else jnp.zeros_like(acc_ref)
            if bias_sum:
                bs_ref[...] = jnp.zeros_like(bs_ref)

        lv = l_ref[...].reshape(e, tj)
        if lhs_gelu:
            lv = _gelu_parts(lv.astype(F32))[0].astype(BF)
        if vec is not None:
            rv = (r_ref[...] * v_ref[0:1, :] + v_ref[1:2, :]).astype(BF)
        else:
            rv = r_ref[...]
        acc_ref[...] += _dot(lv, rv)
        if bias_sum:
            bs_ref[0:1, :] += jnp.sum(rv.astype(F32), axis=0, keepdims=True)

        @pl.when(jnp.logical_and(t == nt - 1, s == CHUNK - 1))
        def _():
            o_ref[...] = acc_ref[...].astype(out_dtype)

    in_specs = [pl.BlockSpec((g, S5_P, tj), lambda h, t, s: (0, s, t)),
                pl.BlockSpec((tj, tn), lambda h, t, s: (t, s * nh + h))]
    args = [lhs_gt, rhs_cr]
    if vec is not None:
        in_specs.append(_full(vec.shape))
        args.append(vec)
    o_spec = pl.BlockSpec((e, tn), lambda h, t, s: (0, h))
    if has_init:
        in_specs.append(o_spec)
        args.append(init)
    out_specs, out_shape = [o_spec], [jax.ShapeDtypeStruct((e, n), out_dtype)]
    if bias_sum:
        out_specs.append(pl.BlockSpec((8, tn), lambda h, t, s: (0, h)))
        out_shape.append(jax.ShapeDtypeStruct((8, n), F32))
    res = pl.pallas_call(
        body, name=name, grid=(nh, nt, CHUNK), in_specs=in_specs, out_specs=out_specs, out_shape=out_shape,
        scratch_shapes=[pltpu.VMEM((e, tn), F32)], compiler_params=_cparams(),
    )(*args)
    return res if bias_sum else res[0]


def _s5_weights(lam_re, lam_im, log_step, b_re, b_im, c_re, c_im, d_skip):
    hp = lax.Precision.HIGHEST
    g = lam_re.shape[1]
    t, p, n = CHUNK, S5_P, S5_N
    dt = jnp.exp(log_step)[..., None]
    ks = jnp.arange(t + 1, dtype=F32).reshape(t + 1, 1, 1, 1)
    mag = jnp.exp(ks * (lam_re * dt)[None])
    ang = ks * (lam_im * dt)[None]
    pr, pi = mag * jnp.cos(ang), mag * jnp.sin(ang)
    ar, ai = pr[1], pi[1]
    qr, qi = ar - 1.0, ai
    den = lam_re * lam_re + lam_im * lam_im
    fr = (qr * lam_re + qi * lam_im) / den
    fi = (qi * lam_re - qr * lam_im) / den
    bt_re, bt_im = b_re.transpose(0, 1, 3, 2), b_im.transpose(0, 1, 3, 2)
    bbr = fr[:, :, None, :] * bt_re - fi[:, :, None, :] * bt_im
    bbi = fr[:, :, None, :] * bt_im + fi[:, :, None, :] * bt_re
    pk_r, pk_i = pr[:t, :, :, None, :], pi[:t, :, :, None, :]
    abr = pk_r * bbr[None] - pk_i * bbi[None]
    abi = pk_r * bbi[None] + pk_i * bbr[None]
    kd = (jnp.einsum("rgpn,krgqn->rgkpq", c_re, abr, precision=hp)
          - jnp.einsum("rgpn,krgqn->rgkpq", c_im, abi, precision=hp))
    skip = jnp.eye(p, dtype=F32)[None] * d_skip.reshape(g, p)[:, :, None]
    diag = kd[0][:, 0] + kd[1][:, 0] + skip
    qd = jnp.concatenate([jnp.flip(kd[1][:, 1:], axis=1), diag[:, None], kd[0][:, 1:]], axis=1)
    toep = jnp.stack([qd[:, t - 1 - s:2 * t - 1 - s] for s in range(t)], axis=1)
    mt = toep.transpose(0, 1, 4, 2, 3).reshape(g, t * p, t * p)
    ab = jnp.concatenate([abr, abi], axis=-1)
    bcc = jnp.stack([jnp.flip(ab[:, 0], axis=0), ab[:, 1]])
    bc = bcc.transpose(2, 1, 3, 0, 4).reshape(g, t * p, 4 * n)
    prf = jnp.stack([pr[1:, 0], jnp.flip(pr[1:, 1], axis=0)])[:, :, :, None, :]
    pif = jnp.stack([pi[1:, 0], jnp.flip(pi[1:, 1], axis=0)])[:, :, :, None, :]
    cr_t = c_re[:, None]
    ci_t = c_im[:, None]
    ccc = jnp.concatenate([cr_t * prf - ci_t * pif, -(cr_t * pif + ci_t * prf)], axis=-1)
    cct = ccc.transpose(2, 1, 3, 0, 4).reshape(g, t * p, 4 * n)

    def pair_lanes(a):
        a5 = a.reshape(g // 2, 2, t * p, 4, n)
        return jnp.einsum("agrcn,gh->agrchn", a5, jnp.eye(2, dtype=F32)).reshape(g, t * p, PAIR_W)

    return mt, pair_lanes(bc), pair_lanes(cct), pr[t], pi[t]


def _scan_coef_g(lam_re, lam_im, log_step):
    g = lam_re.shape[1]
    ms = jnp.array([1, 2, 4, 0, 0, 0, 0, 0] + list(range(1, 9)) + list(range(8, 0, -1)), F32) * CHUNK
    dt = jnp.exp(log_step)[..., None]
    mag = jnp.exp(ms.reshape(-1, 1, 1, 1) * (lam_re * dt)[None])
    ang = ms.reshape(-1, 1, 1, 1) * (lam_im * dt)[None]
    cr, ci = mag * jnp.cos(ang), mag * jnp.sin(ang)
    both = jnp.stack([cr, ci], axis=2).reshape(24, 2, 2, g // 2, 2, S5_N)
    return both.transpose(0, 3, 1, 2, 4, 5).reshape(24, g * ZG_W)


def _s5_small(lam_re, lam_im, log_step, b_re, b_im, c_re, c_im, d_skip):
    g = lam_re.shape[1]
    t, p = CHUNK, S5_P
    dt = jnp.exp(log_step)[..., None]
    ks = jnp.arange(24, dtype=F32).reshape(24, 1, 1, 1)
    mag = jnp.exp(ks * (lam_re * dt)[None])
    ang = ks * (lam_im * dt)[None]
    pr, pi = mag * jnp.cos(ang), mag * jnp.sin(ang)
    ar, ai = pr[1], pi[1]
    qr, qi = ar - 1.0, ai
    den = lam_re * lam_re + lam_im * lam_im
    fr = (qr * lam_re + qi * lam_im) / den
    fi = (qi * lam_re - qr * lam_im) / den
    bt_re, bt_im = b_re.transpose(0, 1, 3, 2), b_im.transpose(0, 1, 3, 2)
    bbr = fr[:, :, None, :] * bt_re - fi[:, :, None, :] * bt_im
    bbi = fr[:, :, None, :] * bt_im + fi[:, :, None, :] * bt_re
    pw = jnp.stack([pr, pi], axis=0).transpose(3, 2, 0, 1, 4)
    bb = jnp.stack([bbr, bbi], axis=0).transpose(2, 1, 0, 3, 4)
    cc = jnp.stack([c_re, c_im], axis=0).transpose(2, 1, 0, 3, 4)
    dmat = jnp.eye(p, dtype=F32)[None] * d_skip.reshape(g, p)[:, :, None]
    return pw, bb, cc, dmat, pr[t], pi[t]


def _pair_cols(r, ri, g2):
    c0 = (r * 2 + ri) * 128 + g2 * S5_N
    return slice(c0, c0 + S5_N)


def _ab_powers(pw_ref, bb_ref, k, r):
    bbr, bbi = bb_ref[k, r, 0], bb_ref[k, r, 1]
    abr, abi = [], []
    for kk in range(CHUNK):
        prk, pik = pw_ref[k, r, 0, kk:kk + 1, :], pw_ref[k, r, 1, kk:kk + 1, :]
        abr.append(prk * bbr - pik * bbi)
        abi.append(prk * bbi + pik * bbr)
    return abr, abi


def _s5_weights_fwd(pw, bb, cc, dmat):
    g = pw.shape[0]
    gb = GROUPS_PER_STEP
    hp = lax.Precision.HIGHEST

    def body(pw_ref, bb_ref, cc_ref, dm_ref, mt_ref, mtt_ref, bc_ref, cct_ref):
        zeros = jnp.zeros((GT_ROWS, S5_N), BF)
        for k in range(gb):
            g2 = k % 2
            kdt = []
            for r in range(2):
                for ri in range(2):
                    bc_ref[k, :, _pair_cols(r, ri, 1 - g2)] = zeros
                    cct_ref[k, :, _pair_cols(r, ri, 1 - g2)] = zeros
                abr, abi = _ab_powers(pw_ref, bb_ref, k, r)
                cr, ci = cc_ref[k, r, 0], cc_ref[k, r, 1]
                for kk in range(CHUNK):
                    s = CHUNK - 1 - kk if r == 0 else kk
                    bc_ref[k, s * S5_P:(s + 1) * S5_P, _pair_cols(r, 0, g2)] = abr[kk].astype(BF)
                    bc_ref[k, s * S5_P:(s + 1) * S5_P, _pair_cols(r, 1, g2)] = abi[kk].astype(BF)
                for t in range(CHUNK):
                    f = t + 1 if r == 0 else CHUNK - t
                    prf, pif = pw_ref[k, r, 0, f:f + 1, :], pw_ref[k, r, 1, f:f + 1, :]
                    cct_ref[k, t * S5_P:(t + 1) * S5_P, _pair_cols(r, 0, g2)] = (cr * prf - ci * pif).astype(BF)
                    cct_ref[k, t * S5_P:(t + 1) * S5_P, _pair_cols(r, 1, g2)] = (-(cr * pif + ci * prf)).astype(BF)
                abr_all, abi_all = jnp.concatenate(abr, axis=0), jnp.concatenate(abi, axis=0)
                nt = (((1,), (1,)), ((), ()))
                kdt.append(lax.dot_general(abr_all, cr, nt, precision=hp, preferred_element_type=F32)
                           - lax.dot_general(abi_all, ci, nt, precision=hp, preferred_element_type=F32))
            blk = lambda a, d: a[d * S5_P:(d + 1) * S5_P]
            pieces = [blk(kdt[1], CHUNK - 1 - i) for i in range(CHUNK - 1)]
            pieces.append(blk(kdt[0], 0) + blk(kdt[1], 0) + dm_ref[k])
            pieces += [blk(kdt[0], d) for d in range(1, CHUNK)]
            qrow = jnp.concatenate(pieces, axis=1)
            mt = jnp.concatenate([qrow[:, (CHUNK - 1 - s) * S5_P:(CHUNK - 1 - s) * S5_P + GT_ROWS] for s in range(CHUNK)], axis=0)
            mt_ref[k] = mt.astype(BF)
            mtt_ref[k] = jnp.transpose(mt).astype(BF)

    small = lambda a: pl.BlockSpec((gb, *a.shape[1:]), lambda i: (i,) + (0,) * (a.ndim - 1))
    return pl.pallas_call(
        body, name="l1_s5_weights", grid=(g // gb,), in_specs=[small(pw), small(bb), small(cc), small(dmat)],
        out_specs=[_w_spec(GT_ROWS), _w_spec(GT_ROWS), _w_spec(PAIR_W), _w_spec(PAIR_W)],
        out_shape=[jax.ShapeDtypeStruct((g, GT_ROWS, GT_ROWS), BF), jax.ShapeDtypeStruct((g, GT_ROWS, GT_ROWS), BF),
                   jax.ShapeDtypeStruct((g, GT_ROWS, PAIR_W), BF), jax.ShapeDtypeStruct((g, GT_ROWS, PAIR_W), BF)],
        compiler_params=_cparams(),
    )(pw, bb, cc, dmat)


def _s5_weights_bwd(pw, bb, cc, d_mt, d_bc, d_cct):
    g = pw.shape[0]
    gb = GROUPS_PER_STEP
    hp = lax.Precision.HIGHEST

    def body(pw_ref, bb_ref, cc_ref, dmt_ref, dbc_ref, dcc_ref, dpw_ref, dbb_ref, dccp_ref, ddm_ref):
        for k in range(gb):
            g2 = k % 2
            dq = None
            for s in range(CHUNK):
                parts = [dmt_ref[k, s * S5_P:(s + 1) * S5_P, :]]
                if s < CHUNK - 1:
                    parts.insert(0, jnp.zeros((S5_P, (CHUNK - 1 - s) * S5_P), F32))
                if s > 0:
                    parts.append(jnp.zeros((S5_P, s * S5_P), F32))
                padded = jnp.concatenate(parts, axis=1) if len(parts) > 1 else parts[0]
                dq = padded if dq is None else dq + padded
            dblk = lambda d: dq[:, (CHUNK - 1 + d) * S5_P:(CHUNK + d) * S5_P]
            ddm_ref[k] = dblk(0)
            dkdt = [jnp.concatenate([dblk(d) for d in range(CHUNK)], axis=0),
                    jnp.concatenate([dblk(-d) for d in range(CHUNK)], axis=0)]
            for r in range(2):
                abr, abi = _ab_powers(pw_ref, bb_ref, k, r)
                bbr, bbi = bb_ref[k, r, 0], bb_ref[k, r, 1]
                cr, ci = cc_ref[k, r, 0], cc_ref[k, r, 1]
                abr_all, abi_all = jnp.concatenate(abr, axis=0), jnp.concatenate(abi, axis=0)
                tn = (((0,), (0,)), ((), ()))
                nn = (((1,), (0,)), ((), ()))
                dcr = lax.dot_general(dkdt[r], abr_all, tn, precision=hp, preferred_element_type=F32)
                dci = -lax.dot_general(dkdt[r], abi_all, tn, precision=hp, preferred_element_type=F32)
                dabr_all = lax.dot_general(dkdt[r], cr, nn, precision=hp, preferred_element_type=F32)
                dabi_all = -lax.dot_general(dkdt[r], ci, nn, precision=hp, preferred_element_type=F32)
                dbbr = jnp.zeros((S5_P, S5_N), F32)
                dbbi = jnp.zeros((S5_P, S5_N), F32)
                dpr = [jnp.zeros((1, S5_N), F32) for _ in range(24)]
                dpi = [jnp.zeros((1, S5_N), F32) for _ in range(24)]
                for kk in range(CHUNK):
                    s = CHUNK - 1 - kk if r == 0 else kk
                    dabr = dabr_all[kk * S5_P:(kk + 1) * S5_P] + dbc_ref[k, s * S5_P:(s + 1) * S5_P, _pair_cols(r, 0, g2)]
                    dabi = dabi_all[kk * S5_P:(kk + 1) * S5_P] + dbc_ref[k, s * S5_P:(s + 1) * S5_P, _pair_cols(r, 1, g2)]
                    prk, pik = pw_ref[k, r, 0, kk:kk + 1, :], pw_ref[k, r, 1, kk:kk + 1, :]
                    dbbr = dbbr + prk * dabr + pik * dabi
                    dbbi = dbbi - pik * dabr + prk * dabi
                    dpr[kk] = dpr[kk] + jnp.sum(dabr * bbr + dabi * bbi, axis=0, keepdims=True)
                    dpi[kk] = dpi[kk] + jnp.sum(dabi * bbr - dabr * bbi, axis=0, keepdims=True)
                for t in range(CHUNK):
                    f = t + 1 if r == 0 else CHUNK - t
                    prf, pif = pw_ref[k, r, 0, f:f + 1, :], pw_ref[k, r, 1, f:f + 1, :]
                    d_re = dcc_ref[k, t * S5_P:(t + 1) * S5_P, _pair_cols(r, 0, g2)]
                    d_im = dcc_ref[k, t * S5_P:(t + 1) * S5_P, _pair_cols(r, 1, g2)]
                    dcr = dcr + d_re * prf - d_im * pif
                    dci = dci - d_re * pif - d_im * prf
                    dpr[f] = dpr[f] + jnp.sum(d_re * cr - d_im * ci, axis=0, keepdims=True)
                    dpi[f] = dpi[f] - jnp.sum(d_re * ci + d_im * cr, axis=0, keepdims=True)
                dbb_ref[k, r, 0] = dbbr
                dbb_ref[k, r, 1] = dbbi
                dccp_ref[k, r, 0] = dcr
                dccp_ref[k, r, 1] = dci
                dpw_ref[k, r, 0] = jnp.concatenate(dpr, axis=0)
                dpw_ref[k, r, 1] = jnp.concatenate(dpi, axis=0)

    small = lambda a: pl.BlockSpec((gb, *a.shape[1:]), lambda i: (i,) + (0,) * (a.ndim - 1))
    dmat_sds = jax.ShapeDtypeStruct((g, S5_P, S5_P), F32)
    return pl.pallas_call(
        body, name="l1_s5_weights_bwd", grid=(g // gb,),
        in_specs=[small(pw), small(bb), small(cc), _w_spec(GT_ROWS), _w_spec(PAIR_W), _w_spec(PAIR_W)],
        out_specs=[small(pw), small(bb), small(cc), small(dmat_sds)],
        out_shape=[jax.ShapeDtypeStruct(pw.shape, F32), jax.ShapeDtypeStruct(bb.shape, F32),
                   jax.ShapeDtypeStruct(cc.shape, F32), dmat_sds],
        compiler_params=_cparams(),
    )(pw, bb, cc, d_mt, d_bc, d_cct)


def _s5_compact(lam_re, lam_im, log_step, b_re, b_im, c_re, c_im, d_skip):
    hp = lax.Precision.HIGHEST
    g = lam_re.shape[1]
    nb = g // GROUPS_PER_BLOCK
    t, p, n = CHUNK, S5_P, S5_N
    dt = jnp.exp(log_step)[..., None]
    ks = jnp.arange(t + 1, dtype=F32).reshape(t + 1, 1, 1, 1)
    mag = jnp.exp(ks * (lam_re * dt)[None])
    ang = ks * (lam_im * dt)[None]
    pr, pi = mag * jnp.cos(ang), mag * jnp.sin(ang)
    ar, ai = pr[1], pi[1]
    qr, qi = ar - 1.0, ai
    den = lam_re * lam_re + lam_im * lam_im
    fr = (qr * lam_re + qi * lam_im) / den
    fi = (qi * lam_re - qr * lam_im) / den
    bt_re, bt_im = b_re.transpose(0, 1, 3, 2), b_im.transpose(0, 1, 3, 2)
    bbr = fr[:, :, None, :] * bt_re - fi[:, :, None, :] * bt_im
    bbi = fr[:, :, None, :] * bt_im + fi[:, :, None, :] * bt_re
    pk_r, pk_i = pr[:t, :, :, None, :], pi[:t, :, :, None, :]
    abr = pk_r * bbr[None] - pk_i * bbi[None]
    abi = pk_r * bbi[None] + pk_i * bbr[None]
    kd = (jnp.einsum("rgpn,krgqn->rgkpq", c_re, abr, precision=hp)
          - jnp.einsum("rgpn,krgqn->rgkpq", c_im, abi, precision=hp))
    skip = jnp.eye(p, dtype=F32)[None] * d_skip.reshape(g, p)[:, :, None]
    diag = kd[0][:, 0] + kd[1][:, 0] + skip
    qd = jnp.concatenate([jnp.flip(kd[1][:, 1:], axis=1), diag[:, None], kd[0][:, 1:]], axis=1)
    nd = 2 * t - 1
    wc = qd.transpose(0, 1, 3, 2).reshape(nb, GROUPS_PER_BLOCK, nd, p, p).transpose(0, 2, 1, 3, 4)
    wcomp = wc.reshape(nb, nd, LANE_BLOCK, p)
    ab = jnp.concatenate([abr, abi], axis=-1)
    bcc = jnp.stack([jnp.flip(ab[:, 0], axis=0), ab[:, 1]])
    bcomp = bcc.reshape(2, t, nb, LANE_BLOCK, 2 * n)
    prf = jnp.stack([pr[1:, 0], jnp.flip(pr[1:, 1], axis=0)])[:, :, :, None, :]
    pif = jnp.stack([pi[1:, 0], jnp.flip(pi[1:, 1], axis=0)])[:, :, :, None, :]
    cr_t = c_re[:, None]
    ci_t = c_im[:, None]
    ccc = jnp.concatenate([cr_t * prf - ci_t * pif, -(cr_t * pif + ci_t * prf)], axis=-1)
    ccomp = ccc.reshape(2, t, nb, LANE_BLOCK, 2 * n)
    return wcomp, bcomp, ccomp, pr[t], pi[t]


def _scan_coef(lam_re, lam_im, log_step):
    g = lam_re.shape[1]
    nb = g // GROUPS_PER_BLOCK
    ms = jnp.array([1, 2, 4, 0, 0, 0, 0, 0] + list(range(1, 9)) + list(range(8, 0, -1)), F32) * CHUNK
    dt = jnp.exp(log_step)[..., None]
    mag = jnp.exp(ms.reshape(-1, 1, 1, 1) * (lam_re * dt)[None])
    ang = ms.reshape(-1, 1, 1, 1) * (lam_im * dt)[None]
    cr, ci = mag * jnp.cos(ang), mag * jnp.sin(ang)
    lay = lambda a: a.reshape(24, 2, nb, ZH).transpose(2, 1, 0, 3)
    return jnp.concatenate([lay(cr), lay(ci)], axis=-1)


def _to_cr(a):
    return a.reshape(a.shape[0] // CHUNK, CHUNK * a.shape[1])


def _from_cr(a, c):
    return a.reshape(a.shape[0] * CHUNK, c)


def _pad8(v):
    return jnp.concatenate([v, jnp.zeros((8 - v.shape[0], v.shape[1]), v.dtype)], axis=0)


def _local_step(x, c, ctx, c_ctx, loss_target, w, late=None, scatter=False):
    l, d = x.shape
    lc = ctx.shape[0]
    tm = min(256, lc)
    assert lc == tm and l % tm == 0 and tm % GRID_W == 0 and (tm & (tm - 1)) == 0
    nl = l // tm

    c8 = _pad8(jnp.stack([c, c_ctx]))
    mod = _ada_fwd(c8, w["ada_w"], w["ada_b"])
    sh = mod[:, :2, :d]
    sc = mod[:, :2, d:2 * d]
    gt = mod[:, :2, 2 * d:]
    ln_g, ln_b = w["ln_g"], w["ln_b"]

    a0, b0 = 1.0 + sc[0], sh[0]
    xch = _Exchange("gather", [late[n][0] for n in late], [late[n][1] for n in late]) if late else None
    p42, got = _inproj0(x, ctx, a0, b0, w["conv_w_in"], tm, xch)
    if late:
        w = dict(w, **dict(zip(late, got)))
    e = w["conv_w_out"].shape[0]
    half = e // 2
    nb = e // LANE_BLOCK
    tc = min(512, half)
    cw = w["conv_w"].reshape(3, 2, half)
    q3 = _conv_fwd(p42, cw, nl, tm, tc)
    xh1_l, xh1_c, rs1_l, rs1_c, fx = _outproj_ln0(q3, w["conv_w_out"], x, ctx, gt[0], tm)
    jl, jc = l // CHUNK, lc // CHUNK

    g0, bb0 = ln_g[0:1], ln_b[0:1]
    a1 = g0 * (1.0 + sc[1])
    b1 = bb0 * (1.0 + sc[1]) + sh[1]
    wu_t = w["ssm_w_in"][:, :e].T
    w_z = w["ssm_w_in"][:, e:]
    ut_l, z_l = _inproj1_gt(xh1_l, a1[0:1], b1[0:1], wu_t, w_z, "lat")
    ut_c, _ = _inproj1_gt(xh1_c, a1[1:2], b1[1:2], wu_t, w_z, "ctx")
    s5 = (w["ssm_lam_re"], w["ssm_lam_im"], w["ssm_log_step"], w["ssm_b_re"], w["ssm_b_im"],
          w["ssm_c_re"], w["ssm_c_im"], w["ssm_d"])
    (pw, bbw, ccw, dmat, _, _), s5_vjp = jax.vjp(_s5_small, *s5)
    mt_b, mtt_b, bc_b, cct_b = _s5_weights_fwd(pw, bbw, ccw, dmat)
    coef = lax.stop_gradient(_scan_coef_g(*s5[:3]))
    zz_l, zz_c = _s5_z(ut_l, ut_c, bc_b)
    fwd_chains = ((("c", False), ("l", False)), (("c", True), ("l", True)))
    st_l, st_c = _scan_g(zz_l, zz_c, coef, fwd_chains, False, name="l1_scan_fwd")
    yt = _s5_y(ut_l, st_l, mtt_b, cct_b)
    b_glu = w["ssm_b_glu"].reshape(1, e)
    w_cr, sg_cr = _glu_fwd_gt(yt, z_l, w["ssm_w_glu"], b_glu)
    vec_f = _pad8(jnp.concatenate([g0, bb0, gt[1][0:1], ln_g[1:2], ln_b[1:2]], axis=0))
    dr2, acc_f = _final(w_cr, w["ssm_w_out"], xh1_l, _to_cr(loss_target), vec_f)
    loss = jnp.sum(acc_f[3])

    gt1 = gt[1][0:1]
    dz_l, dt_l, dyt = _glu_bwd_gt(dr2, gt1, w["ssm_w_out"], w["ssm_w_glu"], yt, z_l, sg_cr)
    g_w_out = _dw_cr(w_cr, dr2, "cr", "scaled", gt1, False, None, "l1_dw_out")
    g_w_glu, bsum = _dw_gt(yt, dt_l, True, None, True, None, BF, "l1_dw_glu")
    g_b_glu = bsum[0]
    ds_l = _s5_ds(dyt, cct_b)
    bwd_chains = ((("l", True), ("c", True)), (("l", False), ("c", False)))
    dzz_l, dzz_c, da = _scan_g(ds_l, jnp.zeros_like(zz_c), coef, bwd_chains, True, st_l, st_c, name="l1_scan_bwd")
    dut_l, dut_c = _s5_dx(dyt, dzz_l, dzz_c, mt_b, bc_b)
    d_mt, d_bc, d_cct = _s5_dw(ut_l, ut_c, dyt, dzz_l, dzz_c, st_l)
    n_g = e // S5_P
    da = jnp.sum(da, axis=0).reshape(n_g // 2, 2, 2, 2, S5_N).transpose(1, 2, 0, 3, 4)
    da = da.reshape(2, 2, n_g, S5_N)
    d_pw, d_bb, d_ccp, d_dm = _s5_weights_bwd(pw, bbw, ccw, d_mt, d_bc, d_cct)
    g_s5 = s5_vjp((d_pw, d_bb, d_ccp, d_dm, da[:, 0], da[:, 1]))

    vec_l = _pad8(jnp.concatenate([g0, bb0, 1.0 + sc[1][0:1]], axis=0))
    vec_c = _pad8(jnp.concatenate([g0, bb0, 1.0 + sc[1][1:2]], axis=0))
    dr1_l, acc_l = _bwd_inproj1_gt(dut_l, dz_l, wu_t, w_z, xh1_l, rs1_l, dr2, vec_l, "lat")
    dr1_c, acc_c = _bwd_inproj1_gt(dut_c, jnp.zeros((jc, CHUNK * e), BF), wu_t, w_z, xh1_c, rs1_c,
                                   jnp.zeros((jc, CHUNK * d), BF), vec_c, "ctx")
    mod_l = jnp.concatenate([a1[0:1], b1[0:1]], axis=0)
    mod_c = jnp.concatenate([a1[1:2], b1[1:2]], axis=0)
    g_ut_c = _dw_gt(dut_c, xh1_c, False, mod_c, False, None, F32, "l1_dw_in_u_ctx")
    g_ut = _dw_gt(dut_l, xh1_l, False, mod_l, False, g_ut_c, BF, "l1_dw_in_u")
    g_in_z = _dw_cr(xh1_l, dz_l, "mod", "cr", mod_l, False, None, "l1_dw_in_z")
    g_w_in1 = jnp.concatenate([g_ut.T, g_in_z], axis=1)

    dr1_ln, dr1_cn = _from_cr(dr1_l, d), _from_cr(dr1_c, d)
    dq3, acc_g0 = _bwd_outproj0(dr1_ln, dr1_cn, gt[0], w["conv_w_out"], fx, tm)
    sent1 = ["ssm_w_in", "ssm_w_glu", "ssm_w_out"]
    xch1 = _Exchange("scatter", [g_w_in1, g_w_glu, g_w_out], [BIG[n] for n in sent1]) if scatter else None
    dp42, dcw, recv1 = _conv_bwd(dq3, p42, cw, nl, tm, tc, xch1)
    g_w_in0 = _dw_inproj0(x, ctx, a0, b0, dp42, tm)
    g_w_out0 = _dw_outproj0(q3, dr1_ln, dr1_cn, gt[0], tm)
    sent0 = ["conv_w_in", "conv_w_out"]
    xch0 = _Exchange("scatter", [g_w_in0, g_w_out0], [BIG[n] for n in sent0]) if scatter else None
    grad_x, acc_0, recv0 = _bwd_inproj0(dp42, w["conv_w_in"], x, ctx, dr1_ln, dr1_cn, a0, tm, xch0)
    recv = dict(zip(sent1 + sent0, recv1 + recv0))

    zero = jnp.zeros((d,), F32)
    dm0 = jnp.stack([jnp.concatenate([acc_0[2], acc_0[0], acc_g0[0]]), jnp.concatenate([acc_0[3], acc_0[1], acc_g0[1]])])
    dm1 = jnp.stack([jnp.concatenate([acc_l[1], acc_l[0], acc_f[2]]), jnp.concatenate([acc_c[1], acc_c[0], zero])])
    dm8 = jnp.stack([_pad8(dm0), _pad8(dm1)])
    g_ada_w, dc8 = _ada_bwd(c8, w["ada_w"], dm8)

    grads = {
        "c_ctx": dc8[0, 1] + dc8[1, 1],
        "ada_w": g_ada_w,
        "ada_b": jnp.stack([dm0[0] + dm0[1], dm1[0] + dm1[1]]),
        "ln_g": jnp.stack([acc_l[2] + acc_c[2], acc_f[0]]),
        "ln_b": jnp.stack([acc_l[3] + acc_c[3], acc_f[1]]),
        "conv_w_in": g_w_in0, "conv_w": dcw[:3].reshape(3, e), "conv_w_out": g_w_out0,
        "ssm_w_in": g_w_in1,
        "ssm_lam_re": g_s5[0], "ssm_lam_im": g_s5[1], "ssm_log_step": g_s5[2],
        "ssm_b_re": g_s5[3], "ssm_b_im": g_s5[4], "ssm_c_re": g_s5[5], "ssm_c_im": g_s5[6], "ssm_d": g_s5[7],
        "ssm_w_glu": g_w_glu, "ssm_b_glu": g_b_glu, "ssm_w_out": g_w_out,
    }
    for n in recv:
        del grads[n]
    return loss, grad_x, grads, recv


WEIGHTS = ["c_ctx", "ada_w", "ada_b", "ln_g", "ln_b", "conv_w_in", "conv_w", "conv_w_out", "ssm_w_in",
           "ssm_lam_re", "ssm_lam_im", "ssm_log_step", "ssm_b_re", "ssm_b_im", "ssm_c_re", "ssm_c_im",
           "ssm_d", "ssm_w_glu", "ssm_b_glu", "ssm_w_out"]
BIG = {"ada_w": 1, "conv_w_in": 1, "conv_w_out": 0, "ssm_w_in": 1, "ssm_w_glu": 0, "ssm_w_out": 0}
SMALL_SHARDED = ["conv_w", "ssm_d", "ssm_b_glu"]
REPLICATED = ["c_ctx", "ada_b", "ln_g", "ln_b", "ssm_lam_re", "ssm_lam_im", "ssm_log_step",
              "ssm_b_re", "ssm_b_im", "ssm_c_re", "ssm_c_im"]


def _view2d(name, a):
    return a.reshape(-1, a.shape[-1])


def kernel(x, c, ctx, c_ctx, ada_w, ada_b, ln_g, ln_b, conv_w_in, conv_w, conv_w_out, ssm_w_in, ssm_lam_re, ssm_lam_im, ssm_log_step, ssm_b_re, ssm_b_im, ssm_c_re, ssm_c_im, ssm_d, ssm_w_glu, ssm_b_glu, ssm_w_out, loss_target, m_c_ctx, m_ada_w, m_ada_b, m_ln_g, m_ln_b, m_conv_w_in, m_conv_w, m_conv_w_out, m_ssm_w_in, m_ssm_lam_re, m_ssm_lam_im, m_ssm_log_step, m_ssm_b_re, m_ssm_b_im, m_ssm_c_re, m_ssm_c_im, m_ssm_d, m_ssm_w_glu, m_ssm_b_glu, m_ssm_w_out, v_c_ctx, v_ada_w, v_ada_b, v_ln_g, v_ln_b, v_conv_w_in, v_conv_w, v_conv_w_out, v_ssm_w_in, v_ssm_lam_re, v_ssm_lam_im, v_ssm_log_step, v_ssm_b_re, v_ssm_b_im, v_ssm_c_re, v_ssm_c_im, v_ssm_d, v_ssm_w_glu, v_ssm_b_glu, v_ssm_w_out):
    args = locals()
    wt = {n: args[n] for n in WEIGHTS}
    mt = {n: args["m_" + n] for n in WEIGHTS}
    vt = {n: args["v_" + n] for n in WEIGHTS}

    big_names = list(BIG)
    shard = {n: _view2d(n, wt[n]).astype(BF) for n in big_names}
    first = ["ada_w", "conv_w_in"]
    small = jnp.concatenate([wt["conv_w"][0], wt["ssm_d"], wt["ssm_b_glu"]], axis=0)
    small = jnp.concatenate([small, jnp.zeros((3, small.shape[1]), F32)], axis=0)
    gathered = _all_gather([shard[n] for n in first] + [small], [BIG[n] for n in first] + [1], "gather_weights")
    full = dict(zip(first, gathered[:-1]))
    small_full = gathered[-1]
    late = {n: (shard[n], BIG[n]) for n in big_names if n not in first}
    d = x.shape[-1]
    w = {
        "ada_w": full["ada_w"].reshape(2, d, 3 * d), "ada_b": ada_b, "ln_g": ln_g, "ln_b": ln_b,
        "conv_w_in": full["conv_w_in"], "conv_w": small_full[0:3],
        "ssm_lam_re": ssm_lam_re[0], "ssm_lam_im": ssm_lam_im[0],
        "ssm_log_step": ssm_log_step[0], "ssm_b_re": ssm_b_re[0], "ssm_b_im": ssm_b_im[0],
        "ssm_c_re": ssm_c_re[0], "ssm_c_im": ssm_c_im[0], "ssm_d": small_full[3], "ssm_b_glu": small_full[4],
    }

    loss, grad_x, g, recv_big = _local_step(x[0], c[0], ctx[0], c_ctx, loss_target[0], w, late, True)
    loss = lax.psum(loss, ("x", "y", "c"))

    blob_names = REPLICATED + SMALL_SHARDED
    flat = jnp.concatenate([g[n].reshape(-1).astype(F32) for n in blob_names])
    nflat = flat.shape[0]
    rows = -(-nflat // (N_DEV * 128 * 8)) * 8
    flat = jnp.concatenate([flat, jnp.zeros((N_DEV * rows * 128 - nflat,), F32)]).reshape(N_DEV * rows, 128)
    last = [n for n in big_names if n not in recv_big]
    recv = _all_to_all([_view2d(n, g[n]) for n in last] + [flat], [BIG[n] for n in last] + [0], "scatter_grads")
    recv_big.update(zip(last, recv[:-1]))
    blob_sum = _sum_partials(recv[-1])
    blob = _all_gather([blob_sum], [0], "gather_small_grads")[0].reshape(-1)
    small_g, off = {}, 0
    for n in blob_names:
        shape = wt[n].shape if n in REPLICATED else (*wt[n].shape[:-1], wt[n].shape[-1] * N_DEV)
        size = math.prod(shape)
        small_g[n] = blob[off:off + size].reshape(shape)
        off += size
    me = 4 * lax.axis_index("x") + 2 * lax.axis_index("y") + lax.axis_index("c")
    for n in SMALL_SHARDED:
        size = wt[n].shape[-1]
        small_g[n] = lax.dynamic_slice_in_dim(small_g[n], me * size, size, axis=small_g[n].ndim - 1)

    out_g, out_d, out_m, out_v = {}, {}, {}, {}
    for n in big_names:
        stack = recv_big[n]
        shp = wt[n].shape
        res = _adamw(stack, _view2d(n, wt[n]), _view2d(n, mt[n]), _view2d(n, vt[n]), "adamw_" + n)
        out_g[n], out_d[n], out_m[n], out_v[n] = [r.reshape(shp) for r in res]
    names = list(small_g)
    cat = lambda t: jnp.concatenate([t[n].reshape(-1) for n in names])
    gs, ws, ms, vs = cat(small_g), cat(wt), cat(mt), cat(vt)
    ns = gs.shape[0]
    rs = -(-ns // (128 * 512)) * 512
    padr = lambda a: jnp.concatenate([a, jnp.ones((rs * 128 - ns,), F32)]).reshape(rs, 128)
    res = _adamw(padr(gs)[None], padr(ws), padr(ms), padr(vs), "adamw_small")
    off = 0
    for n in names:
        size = math.prod(wt[n].shape)
        out_g[n], out_d[n], out_m[n], out_v[n] = [r.reshape(-1)[off:off + size].reshape(wt[n].shape) for r in res]
        off += size

    return (loss, grad_x[None], *[out_g[n] for n in WEIGHTS], *[out_d[n] for n in WEIGHTS],
            *[out_m[n] for n in WEIGHTS], *[out_v[n] for n in WEIGHTS])
```

```python
import math

import jax
import jax.numpy as jnp
from jax import lax
from jax.experimental import pallas as pl
from jax.experimental.pallas import tpu as pltpu

F32 = jnp.float32
BF = jnp.bfloat16
MESH = pl.DeviceIdType.MESH
N_DEV = 8

GRID_W = 64
CHUNK = 16
S5_P = 16
S5_N = 64
LANE_BLOCK = 128
GROUPS_PER_BLOCK = LANE_BLOCK // S5_P
BCR_W = CHUNK * LANE_BLOCK
ZL_W = 2 * 2 * GROUPS_PER_BLOCK * S5_N
ZH = ZL_W // 4
LN_EPS = 1e-5
DN_ALPHA = 4.0 ** 0.25
ADAM_LR, ADAM_B1, ADAM_B2, ADAM_EPS, ADAM_WD, ADAM_STEP = 1e-3, 0.9, 0.999, 1e-8, 0.01, 10
GELU_C0 = math.sqrt(2.0 / math.pi)
GELU_C1 = 0.044715
VMEM_MB = 52

ANY = pl.BlockSpec(memory_space=pl.ANY)


def _cparams():
    return pltpu.CompilerParams(vmem_limit_bytes=VMEM_MB << 20)


def _dot(a, b):
    return jnp.dot(a, b, preferred_element_type=F32)


def _dot_nt(a, b):
    return lax.dot_general(a, b, (((1,), (1,)), ((), ())), preferred_element_type=F32)


def _dot_tn(a, b):
    return lax.dot_general(a, b, (((0,), (0,)), ((), ())), preferred_element_type=F32)


def _sigmoid(x):
    return 1.0 / (1.0 + jnp.exp(-x))


def _gelu_parts(y):
    th = jnp.tanh(GELU_C0 * (y + GELU_C1 * y * y * y))
    g = 0.5 * y * (1.0 + th)
    dg = 0.5 * (1.0 + th) + 0.5 * y * (1.0 - th * th) * GELU_C0 * (1.0 + 3.0 * GELU_C1 * y * y)
    return g, dg


def _full(shape):
    nd = len(shape)
    return pl.BlockSpec(shape, lambda *_: (0,) * nd)


def _mesh_pos():
    x, y, c = lax.axis_index("x"), lax.axis_index("y"), lax.axis_index("c")
    return x, y, c


def _peer(pos, k):
    x, y, c = pos
    px = 1 - x if (k >> 2) & 1 else x
    py = 1 - y if (k >> 1) & 1 else y
    pc = 1 - c if k & 1 else c
    return (px, py, pc), 4 * px + 2 * py + pc


def _shard_at(ref, axis, idx, n):
    if axis == 0:
        return ref.at[pl.ds(idx * n, n)]
    return ref.at[:, pl.ds(idx * n, n)]


class _Exchange:
    def __init__(self, kind, arrays, axes):
        self.kind, self.axes, self.n = kind, list(axes), len(arrays)
        self.arrays = list(arrays)
        self.out_shape = []
        for s, ax in zip(arrays, axes):
            shp = list(s.shape)
            if kind == "gather":
                shp[ax] *= N_DEV
                self.out_shape.append(jax.ShapeDtypeStruct(tuple(shp), s.dtype))
            else:
                shp[ax] //= N_DEV
                self.out_shape.append(jax.ShapeDtypeStruct((N_DEV, *shp), s.dtype))
        self.scratch = [pltpu.SemaphoreType.DMA((self.n, N_DEV - 1)), pltpu.SemaphoreType.DMA((self.n, N_DEV - 1)),
                        pltpu.SemaphoreType.DMA((self.n,))]

    def _copies(self, ins, outs, sems):
        send_sems, recv_sems, local_sems = sems
        pos = _mesh_pos()
        me = 4 * pos[0] + 2 * pos[1] + pos[2]
        local, sends, recvs = [], [], []
        for i in range(self.n):
            ax = self.axes[i]
            if self.kind == "gather":
                size = ins[i].shape[ax]
                src = lambda idx, i=i: ins[i]
                dst = lambda idx, i=i, ax=ax, size=size: _shard_at(outs[i], ax, idx, size)
                mine, theirs = (lambda pidx: me), (lambda pidx: pidx)
            else:
                size = ins[i].shape[ax] // N_DEV
                src = lambda idx, i=i, ax=ax, size=size: _shard_at(ins[i], ax, idx, size)
                dst = lambda idx, i=i: outs[i].at[idx]
                mine, theirs = (lambda pidx: me), (lambda pidx: pidx)
            src_own = src(me)
            local.append(pltpu.make_async_copy(src_own, dst(me), local_sems.at[i]))
            for k in range(1, N_DEV):
                peer, pidx = _peer(pos, k)
                out_src = src(me) if self.kind == "gather" else src(pidx)
                sends.append(pltpu.make_async_remote_copy(
                    src_ref=out_src, dst_ref=dst(mine(pidx)), send_sem=send_sems.at[i, k - 1],
                    recv_sem=recv_sems.at[i, k - 1], device_id=peer, device_id_type=MESH))
                recvs.append(pltpu.make_async_remote_copy(
                    src_ref=out_src, dst_ref=dst(theirs(pidx)), send_sem=send_sems.at[i, k - 1],
                    recv_sem=recv_sems.at[i, k - 1], device_id=peer, device_id_type=MESH))
        return local, sends, recvs

    def start(self, ins, outs, sems):
        local, sends, _ = self._copies(ins, outs, sems)
        for cp in local + sends:
            cp.start()

    def wait(self, ins, outs, sems):
        local, sends, recvs = self._copies(ins, outs, sems)
        for cp in recvs:
            cp.wait_recv()
        for cp in sends:
            cp.wait_send()
        for cp in local:
            cp.wait()

    def run(self, name):
        n = self.n

        def body(*refs):
            ins, outs, sems = refs[:n], refs[n:2 * n], refs[2 * n:]
            self.start(ins, outs, sems)
            self.wait(ins, outs, sems)

        return pl.pallas_call(body, name=name, out_shape=self.out_shape, in_specs=[ANY] * n, out_specs=[ANY] * n,
                              scratch_shapes=self.scratch)(*self.arrays)


def _hosted_call(body, xch, grid, in_specs, out_specs, out_shape, scratch, args, name):
    out_specs, out_shape = list(out_specs), list(out_shape)
    n_in, n_out = len(in_specs), len(out_specs)
    if xch is None:
        res = pl.pallas_call(body, name=name, grid=grid, in_specs=in_specs, out_specs=out_specs, out_shape=out_shape,
                             scratch_shapes=list(scratch), compiler_params=_cparams())(*args)
        return list(res), []
    n = xch.n
    rank = len(grid)

    def wrapped(*refs):
        ins, x_ins = refs[:n_in], refs[n_in:n_in + n]
        outs = refs[n_in + n:n_in + n + n_out]
        x_outs = refs[n_in + n + n_out:n_in + 2 * n + n_out]
        rest = refs[n_in + 2 * n + n_out:]
        own, sems = rest[:len(rest) - 3], rest[len(rest) - 3:]
        ids = [pl.program_id(a) for a in range(rank)]
        first, last = ids[0] == 0, ids[0] == grid[0] - 1
        for a in range(1, rank):
            first = jnp.logical_and(first, ids[a] == 0)
            last = jnp.logical_and(last, ids[a] == grid[a] - 1)

        @pl.when(first)
        def _():
            xch.start(x_ins, x_outs, sems)

        body(*ins, *outs, *own)

        @pl.when(last)
        def _():
            xch.wait(x_ins, x_outs, sems)

    res = pl.pallas_call(
        wrapped, name=name, grid=grid, in_specs=list(in_specs) + [ANY] * n, out_specs=out_specs + [ANY] * n,
        out_shape=out_shape + xch.out_shape, scratch_shapes=list(scratch) + xch.scratch, compiler_params=_cparams(),
    )(*args, *xch.arrays)
    return list(res[:n_out]), list(res[n_out:])


def _all_gather(shards, axes, name):
    return _Exchange("gather", shards, axes).run(name)


def _all_to_all(parts, axes, name):
    return _Exchange("scatter", parts, axes).run(name)


def _ada_fwd(cv, ada_w, ada_b):
    nl, d, wd = ada_w.shape
    r = cv.shape[0]

    def body(c_ref, w_ref, b_ref, o_ref):
        c = c_ref[...]
        s = (c * _sigmoid(c)).astype(BF)
        o_ref[0] = _dot(s, w_ref[0]) + b_ref[0]

    return pl.pallas_call(
        body, name="ada_fwd", grid=(nl,),
        in_specs=[_full((r, d)), pl.BlockSpec((1, d, wd), lambda l: (l, 0, 0)), pl.BlockSpec((1, 1, wd), lambda l: (l, 0, 0))],
        out_specs=pl.BlockSpec((1, r, wd), lambda l: (l, 0, 0)),
        out_shape=jax.ShapeDtypeStruct((nl, r, wd), F32), compiler_params=_cparams(),
    )(cv, ada_w, ada_b.reshape(nl, 1, wd))


def _ada_bwd(cv, ada_w, dm, out_dtype):
    nl, d, wd = ada_w.shape
    r = cv.shape[0]

    def body(c_ref, w_ref, dm_ref, dw_ref, dc_ref):
        c = c_ref[...]
        sg = _sigmoid(c)
        s = (c * sg).astype(BF)
        dmv = dm_ref[0].astype(BF)
        dw_ref[0] = _dot_tn(s, dmv).astype(out_dtype)
        dc_ref[0] = _dot_nt(dmv, w_ref[0]) * (sg * (1.0 + c * (1.0 - sg)))

    return pl.pallas_call(
        body, name="ada_bwd", grid=(nl,),
        in_specs=[_full((r, d)), pl.BlockSpec((1, d, wd), lambda l: (l, 0, 0)), pl.BlockSpec((1, r, wd), lambda l: (l, 0, 0))],
        out_specs=[pl.BlockSpec((1, d, wd), lambda l: (l, 0, 0)), pl.BlockSpec((1, r, d), lambda l: (l, 0, 0))],
        out_shape=[jax.ShapeDtypeStruct((nl, d, wd), out_dtype), jax.ShapeDtypeStruct((nl, r, d), F32)],
        compiler_params=_cparams(),
    )(cv, ada_w, dm)


def _sum_partials(stack):
    _, r, c = stack.shape

    def body(s_ref, o_ref):
        acc = s_ref[0]
        for p in range(1, N_DEV):
            acc = acc + s_ref[p]
        o_ref[...] = acc

    return pl.pallas_call(body, name="sum_partials", out_shape=jax.ShapeDtypeStruct((r, c), F32),
                          in_specs=[_full(stack.shape)], out_specs=_full((r, c)), grid=(1,),
                          compiler_params=_cparams())(stack)


def _adamw(gstack, w, m, v, name):
    p, r, c = gstack.shape
    tr = r
    for cand in (512 if c <= 256 else 256, 128, 64, 32, 16, 8):
        if r % cand == 0 and r > cand:
            tr = cand
            break
    bc1 = 1.0 - ADAM_B1 ** ADAM_STEP
    bc2 = 1.0 - ADAM_B2 ** ADAM_STEP

    def body(g_ref, w_ref, m_ref, v_ref, go_ref, d_ref, mo_ref, vo_ref):
        g = g_ref[0].astype(F32)
        for q in range(1, p):
            g = g + g_ref[q].astype(F32)
        mn = ADAM_B1 * m_ref[...] + (1.0 - ADAM_B1) * g
        vn = ADAM_B2 * v_ref[...] + (1.0 - ADAM_B2) * (g * g)
        go_ref[...] = g
        mo_ref[...] = mn
        vo_ref[...] = vn
        d_ref[...] = -ADAM_LR * ((mn / bc1) / (jnp.sqrt(vn / bc2) + ADAM_EPS) + ADAM_WD * w_ref[...])

    row = pl.BlockSpec((tr, c), lambda i: (i, 0))
    sds = jax.ShapeDtypeStruct((r, c), F32)
    return pl.pallas_call(
        body, name=name, grid=(r // tr,),
        in_specs=[pl.BlockSpec((p, tr, c), lambda i: (0, i, 0)), row, row, row],
        out_specs=[row, row, row, row], out_shape=[sds, sds, sds, sds], compiler_params=_cparams(),
    )(gstack, w, m, v)


def _lat_or_ctx_specs(tm, d, nl, grid_rank, row_axis):
    def lat(*ids):
        return (jnp.minimum(ids[row_axis], nl - 1), 0)

    def ctx(*ids):
        return (jnp.maximum(ids[row_axis] - nl, 0), 0)

    return pl.BlockSpec((tm, d), lat), pl.BlockSpec((tm, d), ctx)


def _sel_row(ref, is_ctx):
    return jnp.where(is_ctx, ref[1:2, :], ref[0:1, :])


def _inproj0(x, ctx, a2, b2, w, tm, xch=None):
    l, d = x.shape
    nl, nc = l // tm, ctx.shape[0] // tm
    e = w.shape[1] // 4
    half = e // 2

    def body(x_ref, c_ref, a_ref, b_ref, w_hbm, o_ref, w_ref):
        i = pl.program_id(0)

        @pl.when(i == 0)
        def _():
            pltpu.sync_copy(w_hbm, w_ref)

        is_ctx = i >= nl
        xv = jnp.where(is_ctx, c_ref[...], x_ref[...])
        h = (xv * _sel_row(a_ref, is_ctx) + _sel_row(b_ref, is_ctx)).astype(BF)
        for k in range(4):
            r = _dot(h, w_ref[:, k * e:(k + 1) * e])
            o_ref[k, 0] = r[:, :half].astype(BF)
            o_ref[k, 1] = r[:, half:].astype(BF)

    lat, cx = _lat_or_ctx_specs(tm, d, nl, 1, 0)
    (p42,), extra = _hosted_call(
        body, xch, grid=(nl + nc,),
        in_specs=[lat, cx, _full((2, d)), _full((2, d)), ANY],
        out_specs=[pl.BlockSpec((4, 2, tm, half), lambda i: (0, 0, i, 0))],
        out_shape=[jax.ShapeDtypeStruct((4, 2, l + ctx.shape[0], half), BF)],
        scratch=[pltpu.VMEM(w.shape, BF)], args=(x, ctx, a2, b2, w), name="l0_inproj")
    return p42, extra


def _conv_taps(u, w_up, w_mid, w_dn, pos, rl, tm):
    up = jnp.where(pos == 0, 0.0, pltpu.roll(u, 1, 0))
    dn = jnp.where(pos == rl - 1, 0.0, pltpu.roll(u, tm - 1, 0))
    return w_up * up + w_mid * u + w_dn * dn, up, dn


def _conv_halo_specs(tm, tc, nl, lead):
    hb = tm // GRID_W

    def prev(j, i):
        return (0, 1, jnp.maximum(jnp.minimum(i, nl - 1) * hb - 1, 0), j)

    def nxt(j, i):
        return (0, 1, jnp.minimum((jnp.minimum(i, nl - 1) + 1) * hb, nl * hb - 1), j)

    return pl.BlockSpec((lead, 1, GRID_W, tc), prev), pl.BlockSpec((lead, 1, GRID_W, tc), nxt)


def _conv_fwd(p42, cw, nl, tm, tc):
    _, _, r, half = p42.shape
    nt = r // tm

    def body(p_ref, hp_ref, hn_ref, cw_ref, o_ref):
        i = pl.program_id(1)
        is_ctx = i >= nl
        row = lax.broadcasted_iota(jnp.int32, (tm, tc), 0)
        rl = jnp.where(is_ctx, tm, GRID_W)
        pos = jnp.bitwise_and(row, rl - 1)

        def gate(hv, yc):
            bg = p_ref[0, hv].astype(F32)
            z = p_ref[3, hv].astype(F32)
            return (bg * yc * (z * _sigmoid(z))).astype(BF)

        u_h = p_ref[1, 0].astype(F32) * p_ref[2, 0].astype(F32)
        w_h = cw_ref[:, 0, :]
        o_ref[0] = gate(0, _conv_taps(u_h, w_h[0:1], w_h[1:2], w_h[2:3], pos, rl, tm)[0])
        u_v = p_ref[1, 1].astype(F32) * p_ref[2, 1].astype(F32)
        w_v = cw_ref[:, 1, :]

        @pl.when(is_ctx)
        def _():
            o_ref[1] = gate(1, _conv_taps(u_v, w_v[0:1], w_v[1:2], w_v[2:3], pos, rl, tm)[0])

        @pl.when(jnp.logical_not(is_ctx))
        def _():
            up = hp_ref[1, 0].astype(F32) * hp_ref[2, 0].astype(F32) * (i > 0).astype(F32)
            dn = hn_ref[1, 0].astype(F32) * hn_ref[2, 0].astype(F32) * (i < nl - 1).astype(F32)
            ext = jnp.concatenate([up, u_v, dn], axis=0)
            yc = w_v[0:1] * ext[0:tm] + w_v[1:2] * u_v + w_v[2:3] * ext[2 * GRID_W:tm + 2 * GRID_W]
            o_ref[1] = gate(1, yc)

    hp, hn = _conv_halo_specs(tm, tc, nl, 4)
    return pl.pallas_call(
        body, name="l0_conv_fwd", grid=(half // tc, nt),
        in_specs=[pl.BlockSpec((4, 2, tm, tc), lambda j, i: (0, 0, i, j)), hp, hn,
                  pl.BlockSpec((3, 2, tc), lambda j, i: (0, 0, j))],
        out_specs=pl.BlockSpec((2, tm, tc), lambda j, i: (0, i, j)),
        out_shape=jax.ShapeDtypeStruct((2, r, half), BF), compiler_params=_cparams(),
    )(p42, p42, p42, cw)


def _outproj_ln0(q3, w_out, x, ctx, gt2, tm):
    l, d = x.shape
    lc = ctx.shape[0]
    nl, nc = l // tm, lc // tm
    _, r, half = q3.shape
    tjo = tm // CHUNK

    def body(q_ref, w_hbm, x_ref, c_ref, g_ref, xl_ref, xc_ref, rl_ref, rc_ref, fx_ref, w_ref, xs_ref, rs_ref):
        i = pl.program_id(0)

        @pl.when(i == 0)
        def _():
            pltpu.sync_copy(w_hbm, w_ref)

        is_ctx = i >= nl
        fx = _dot(q_ref[0], w_ref[:half, :]) + _dot(q_ref[1], w_ref[half:, :])
        xv = jnp.where(is_ctx, c_ref[...], x_ref[...])
        rr = DN_ALPHA * xv + _sel_row(g_ref, is_ctx) * fx
        mu = jnp.mean(rr, axis=-1, keepdims=True)
        cen = rr - mu
        rstd = lax.rsqrt(jnp.mean(cen * cen, axis=-1, keepdims=True) + LN_EPS)
        xh = cen * rstd
        for lb in range(d // 128):
            xs_ref[lb] = xh[:, lb * 128:(lb + 1) * 128]
        rs_ref[...] = jnp.broadcast_to(rstd, (tm, 128))
        fx_ref[...] = fx.astype(BF)

        def to_cr(xo_ref, ro_ref):
            for s in range(CHUNK):
                for lb in range(d // 128):
                    xo_ref[:, s * d + lb * 128:s * d + (lb + 1) * 128] = xs_ref.at[lb][pl.ds(s, tjo, stride=CHUNK), :]
                ro_ref[:, s * 128:(s + 1) * 128] = rs_ref[pl.ds(s, tjo, stride=CHUNK), :]

        @pl.when(jnp.logical_not(is_ctx))
        def _():
            to_cr(xl_ref, rl_ref)

        @pl.when(is_ctx)
        def _():
            to_cr(xc_ref, rc_ref)

    lat, cx = _lat_or_ctx_specs(tm, d, nl, 1, 0)
    lat_o = lambda w_: pl.BlockSpec((tjo, CHUNK * w_), lambda i: (jnp.minimum(i, nl - 1), 0))
    ctx_o = lambda w_: pl.BlockSpec((tjo, CHUNK * w_), lambda i: (jnp.maximum(i - nl, 0), 0))
    return pl.pallas_call(
        body, name="l0_outproj_ln", grid=(nl + nc,),
        in_specs=[pl.BlockSpec((2, tm, half), lambda i: (0, i, 0)), ANY, lat, cx, _full((2, d))],
        out_specs=[lat_o(d), ctx_o(d), lat_o(128), ctx_o(128), pl.BlockSpec((tm, d), lambda i: (i, 0))],
        out_shape=[jax.ShapeDtypeStruct((l // CHUNK, CHUNK * d), F32), jax.ShapeDtypeStruct((lc // CHUNK, CHUNK * d), F32),
                   jax.ShapeDtypeStruct((l // CHUNK, CHUNK * 128), F32), jax.ShapeDtypeStruct((lc // CHUNK, CHUNK * 128), F32),
                   jax.ShapeDtypeStruct((r, d), BF)],
        scratch_shapes=[pltpu.VMEM(w_out.shape, BF), pltpu.VMEM((d // 128, tm, 128), F32), pltpu.VMEM((tm, 128), F32)],
        compiler_params=_cparams(),
    )(q3, w_out, x, ctx, gt2)


def _bwd_outproj0(dr_l, dr_c, gt2, w_out, fx, tm):
    l, d = dr_l.shape
    nl, nc = l // tm, dr_c.shape[0] // tm
    e = w_out.shape[0]
    half = e // 2
    r = l + dr_c.shape[0]

    def body(dl_ref, dc_ref, g_ref, w_hbm, fx_ref, dq_ref, acc_ref, w_ref):
        i = pl.program_id(0)

        @pl.when(i == 0)
        def _():
            pltpu.sync_copy(w_hbm, w_ref)
            acc_ref[...] = jnp.zeros_like(acc_ref)

        is_ctx = i >= nl
        dr = jnp.where(is_ctx, dc_ref[...], dl_ref[...]).astype(F32)
        dfx = (dr * _sel_row(g_ref, is_ctx)).astype(BF)
        dq_ref[0] = _dot_nt(dfx, w_ref[:half, :]).astype(BF)
        dq_ref[1] = _dot_nt(dfx, w_ref[half:, :]).astype(BF)
        s = jnp.sum(dr * fx_ref[...].astype(F32), axis=0, keepdims=True)
        sel = is_ctx.astype(F32)
        acc_ref[0:1, :] += s * (1.0 - sel)
        acc_ref[1:2, :] += s * sel

    lat, cx = _lat_or_ctx_specs(tm, d, nl, 1, 0)
    return pl.pallas_call(
        body, name="l0_bwd_outproj", grid=(nl + nc,),
        in_specs=[lat, cx, _full((2, d)), ANY, pl.BlockSpec((tm, d), lambda i: (i, 0))],
        out_specs=[pl.BlockSpec((2, tm, half), lambda i: (0, i, 0)), _full((8, d))],
        out_shape=[jax.ShapeDtypeStruct((2, r, half), BF), jax.ShapeDtypeStruct((8, d), F32)],
        scratch_shapes=[pltpu.VMEM(w_out.shape, BF)], compiler_params=_cparams(),
    )(dr_l, dr_c, gt2, w_out, fx)


def _conv_bwd(dq3, p42, cw, nl, tm, tc, xch=None):
    _, _, r, half = p42.shape
    nt = r // tm

    def body(dq_ref, dqp_ref, dqn_ref, p_ref, hp_ref, hn_ref, cw_ref, dp_ref, dw_ref):
        i = pl.program_id(1)
        is_ctx = i >= nl

        @pl.when(i == 0)
        def _():
            dw_ref[...] = jnp.zeros_like(dw_ref)

        row = lax.broadcasted_iota(jnp.int32, (tm, tc), 0)
        rl = jnp.where(is_ctx, tm, GRID_W)
        pos = jnp.bitwise_and(row, rl - 1)

        def pieces(dq, bg, z):
            sz = _sigmoid(z)
            sil = z * sz
            return dq * bg * sil, dq * sil, dq * bg * (sz * (1.0 + z * (1.0 - sz)))

        def seq_half(hv):
            bg, cg = p_ref[0, hv].astype(F32), p_ref[1, hv].astype(F32)
            v, z = p_ref[2, hv].astype(F32), p_ref[3, hv].astype(F32)
            w = cw_ref[:, hv, :]
            u = cg * v
            yc, u_up, u_dn = _conv_taps(u, w[0:1], w[1:2], w[2:3], pos, rl, tm)
            dyc, dbg_f, dz_f = pieces(dq_ref[hv].astype(F32), bg, z)
            du = _conv_taps(dyc, w[2:3], w[1:2], w[0:1], pos, rl, tm)[0]
            dp_ref[0, hv] = (dbg_f * yc).astype(BF)
            dp_ref[1, hv] = (du * v).astype(BF)
            dp_ref[2, hv] = (du * cg).astype(BF)
            dp_ref[3, hv] = (dz_f * yc).astype(BF)
            dw_ref[0:1, hv, :] += jnp.sum(dyc * u_up, axis=0, keepdims=True)
            dw_ref[1:2, hv, :] += jnp.sum(dyc * u, axis=0, keepdims=True)
            dw_ref[2:3, hv, :] += jnp.sum(dyc * u_dn, axis=0, keepdims=True)

        seq_half(0)

        @pl.when(is_ctx)
        def _():
            seq_half(1)

        @pl.when(jnp.logical_not(is_ctx))
        def _():
            bg, cg = p_ref[0, 1].astype(F32), p_ref[1, 1].astype(F32)
            v, z = p_ref[2, 1].astype(F32), p_ref[3, 1].astype(F32)
            w = cw_ref[:, 1, :]
            u = cg * v
            m_up = (i > 0).astype(F32)
            m_dn = (i < nl - 1).astype(F32)

            def halo(h_ref, dqh_ref, msk):
                hb, hc = h_ref[0, 0].astype(F32), h_ref[1, 0].astype(F32)
                hv_, hz = h_ref[2, 0].astype(F32), h_ref[3, 0].astype(F32)
                return hc * hv_ * msk, pieces(dqh_ref[0].astype(F32), hb, hz)[0] * msk

            u_p, dyc_p = halo(hp_ref, dqp_ref, m_up)
            u_n, dyc_n = halo(hn_ref, dqn_ref, m_dn)
            u_ext = jnp.concatenate([u_p, u, u_n], axis=0)
            u_up, u_dn = u_ext[0:tm], u_ext[2 * GRID_W:tm + 2 * GRID_W]
            yc = w[0:1] * u_up + w[1:2] * u + w[2:3] * u_dn
            dyc, dbg_f, dz_f = pieces(dq_ref[1].astype(F32), bg, z)
            d_ext = jnp.concatenate([dyc_p, dyc, dyc_n], axis=0)
            du = w[0:1] * d_ext[2 * GRID_W:tm + 2 * GRID_W] + w[1:2] * dyc + w[2:3] * d_ext[0:tm]
            dp_ref[0, 1] = (dbg_f * yc).astype(BF)
            dp_ref[1, 1] = (du * v).astype(BF)
            dp_ref[2, 1] = (du * cg).astype(BF)
            dp_ref[3, 1] = (dz_f * yc).astype(BF)
            dw_ref[0:1, 1, :] += jnp.sum(dyc * u_up, axis=0, keepdims=True)
            dw_ref[1:2, 1, :] += jnp.sum(dyc * u, axis=0, keepdims=True)
            dw_ref[2:3, 1, :] += jnp.sum(dyc * u_dn, axis=0, keepdims=True)

    hb = tm // GRID_W

    def dq_prev(j, i):
        return (1, jnp.maximum(jnp.minimum(i, nl - 1) * hb - 1, 0), j)

    def dq_next(j, i):
        return (1, jnp.minimum((jnp.minimum(i, nl - 1) + 1) * hb, nl * hb - 1), j)

    hp, hn = _conv_halo_specs(tm, tc, nl, 4)
    (dp42, dcw), extra = _hosted_call(
        body, xch, grid=(half // tc, nt),
        in_specs=[pl.BlockSpec((2, tm, tc), lambda j, i: (0, i, j)),
                  pl.BlockSpec((1, GRID_W, tc), dq_prev), pl.BlockSpec((1, GRID_W, tc), dq_next),
                  pl.BlockSpec((4, 2, tm, tc), lambda j, i: (0, 0, i, j)), hp, hn,
                  pl.BlockSpec((3, 2, tc), lambda j, i: (0, 0, j))],
        out_specs=[pl.BlockSpec((4, 2, tm, tc), lambda j, i: (0, 0, i, j)), pl.BlockSpec((8, 2, tc), lambda j, i: (0, 0, j))],
        out_shape=[jax.ShapeDtypeStruct(p42.shape, BF), jax.ShapeDtypeStruct((8, 2, half), F32)],
        scratch=[], args=(dq3, dq3, dq3, p42, p42, p42, cw), name="l0_conv_bwd")
    return dp42, dcw, extra


def _bwd_inproj0(dp42, w_in, x, ctx, dr_l, dr_c, a2, tm, xch=None):
    l, d = x.shape
    nl, nc = l // tm, ctx.shape[0] // tm
    e = w_in.shape[1] // 4
    half = e // 2

    def body(dp_ref, w_hbm, x_ref, c_ref, dl_ref, dc_ref, a_ref, gx_ref, acc_ref, w_ref):
        i = pl.program_id(0)

        @pl.when(i == 0)
        def _():
            pltpu.sync_copy(w_hbm, w_ref)
            acc_ref[...] = jnp.zeros_like(acc_ref)

        is_ctx = i >= nl
        dh = jnp.zeros((tm, d), F32)
        for k in range(4):
            for hv in range(2):
                c0 = k * e + hv * half
                dh = dh + _dot_nt(dp_ref[k, hv], w_ref[:, c0:c0 + half])
        xv = jnp.where(is_ctx, c_ref[...], x_ref[...])
        s_sc = jnp.sum(dh * xv, axis=0, keepdims=True)
        s_sh = jnp.sum(dh, axis=0, keepdims=True)
        sel = is_ctx.astype(F32)
        acc_ref[0:1, :] += s_sc * (1.0 - sel)
        acc_ref[1:2, :] += s_sc * sel
        acc_ref[2:3, :] += s_sh * (1.0 - sel)
        acc_ref[3:4, :] += s_sh * sel

        @pl.when(jnp.logical_not(is_ctx))
        def _():
            gx_ref[...] = DN_ALPHA * dl_ref[...].astype(F32) + dh * a_ref[0:1, :]

    lat, cx = _lat_or_ctx_specs(tm, d, nl, 1, 0)
    (gx, acc), extra = _hosted_call(
        body, xch, grid=(nl + nc,),
        in_specs=[pl.BlockSpec((4, 2, tm, half), lambda i: (0, 0, i, 0)), ANY, lat, cx, lat, cx, _full((2, d))],
        out_specs=[pl.BlockSpec((tm, d), lambda i: (jnp.minimum(i, nl - 1), 0)), _full((8, d))],
        out_shape=[jax.ShapeDtypeStruct((l, d), F32), jax.ShapeDtypeStruct((8, d), F32)],
        scratch=[pltpu.VMEM(w_in.shape, BF)], args=(dp42, w_in, x, ctx, dr_l, dr_c, a2), name="l0_bwd_inproj")
    return gx, acc, extra


def _dw_inproj0(x, ctx, a2, b2, dp42, tm):
    l, d = x.shape
    lc = ctx.shape[0]
    assert lc == tm
    tl = 4 * tm if l % (4 * tm) == 0 else tm
    nl = l // tl
    half = dp42.shape[-1]
    e = 2 * half

    def body(x_ref, c_ref, a_ref, b_ref, dpl_ref, dpc_ref, o_ref, acc_ref):
        i = pl.program_id(1)

        @pl.when(i == 0)
        def _():
            acc_ref[...] = jnp.zeros_like(acc_ref)

        def add(rows_ref, dp_ref, sel):
            h = (rows_ref[...] * a_ref[sel:sel + 1, :] + b_ref[sel:sel + 1, :]).astype(BF)
            acc_ref[:, :half] += _dot_tn(h, dp_ref[0, 0])
            acc_ref[:, half:] += _dot_tn(h, dp_ref[0, 1])

        @pl.when(i < nl)
        def _():
            add(x_ref, dpl_ref, 0)

        @pl.when(i == nl)
        def _():
            add(c_ref, dpc_ref, 1)
            o_ref[...] = acc_ref[...].astype(BF)

    return pl.pallas_call(
        body, name="l0_dw_inproj", grid=(4, nl + 1),
        in_specs=[pl.BlockSpec((tl, d), lambda k, i: (jnp.minimum(i, nl - 1), 0)), _full((lc, d)),
                  _full((2, d)), _full((2, d)),
                  pl.BlockSpec((1, 2, tl, half), lambda k, i: (k, 0, jnp.minimum(i, nl - 1), 0)),
                  pl.BlockSpec((1, 2, lc, half), lambda k, i: (k, 0, l // lc, 0))],
        out_specs=pl.BlockSpec((d, e), lambda k, i: (0, k)),
        out_shape=jax.ShapeDtypeStruct((d, 4 * e), BF),
        scratch_shapes=[pltpu.VMEM((d, e), F32)], compiler_params=_cparams(),
    )(x, ctx, a2, b2, dp42, dp42)


def _dw_outproj0(q3, dr_l, dr_c, gt2, tm):
    l, d = dr_l.shape
    nl, nc = l // tm, dr_c.shape[0] // tm
    _, r, half = q3.shape
    nt = nl + nc

    def body(q_ref, dl_ref, dc_ref, g_ref, o_ref, acc_ref):
        i = pl.program_id(0)
        is_ctx = i >= nl

        @pl.when(i == 0)
        def _():
            acc_ref[...] = jnp.zeros_like(acc_ref)

        dr = jnp.where(is_ctx, dc_ref[...], dl_ref[...]).astype(F32)
        dfx = (dr * _sel_row(g_ref, is_ctx)).astype(BF)
        acc_ref[:half, :] += _dot_tn(q_ref[0], dfx)
        acc_ref[half:, :] += _dot_tn(q_ref[1], dfx)

        @pl.when(i == nt - 1)
        def _():
            o_ref[...] = acc_ref[...].astype(BF)

    lat, cx = _lat_or_ctx_specs(tm, d, nl, 1, 0)
    return pl.pallas_call(
        body, name="l0_dw_outproj", grid=(nt,),
        in_specs=[pl.BlockSpec((2, tm, half), lambda i: (0, i, 0)), lat, cx, _full((2, d))],
        out_specs=_full((2 * half, d)), out_shape=jax.ShapeDtypeStruct((2 * half, d), BF),
        scratch_shapes=[pltpu.VMEM((2 * half, d), F32)], compiler_params=_cparams(),
    )(q3, dr_l, dr_c, gt2)


def _cr_tile(j, cap=256):
    for cand in (1024, 512, 256, 128, 64, 32, 16, 8):
        if cand <= cap and j % cand == 0:
            return cand
    raise ValueError(j)


def _group_mask(lane_groups):
    row = lax.broadcasted_iota(jnp.int32, (LANE_BLOCK, LANE_BLOCK), 0) // S5_P
    lane = lax.broadcasted_iota(jnp.int32, (LANE_BLOCK, LANE_BLOCK), 1)
    return row == lane_groups(lane)


def _expand_toeplitz(wcomp):
    nb = wcomp.shape[0]
    nd = 2 * CHUNK - 1

    def body(c_ref, o_ref):
        mask = _group_mask(lambda lane: lane // S5_P)
        tiles = []
        for dd in range(nd):
            m = c_ref[0, dd]
            tiles.append(jnp.where(mask, jnp.concatenate([m] * GROUPS_PER_BLOCK, axis=1), 0.0).astype(BF))
        for s in range(CHUNK):
            for t in range(CHUNK):
                o_ref[0, s * LANE_BLOCK:(s + 1) * LANE_BLOCK, t * LANE_BLOCK:(t + 1) * LANE_BLOCK] = tiles[t - s + CHUNK - 1]

    return pl.pallas_call(
        body, name="l1_expand_toeplitz", grid=(nb,),
        in_specs=[pl.BlockSpec((1, nd, LANE_BLOCK, S5_P), lambda b: (b, 0, 0, 0))],
        out_specs=pl.BlockSpec((1, BCR_W, BCR_W), lambda b: (b, 0, 0)),
        out_shape=jax.ShapeDtypeStruct((nb, BCR_W, BCR_W), BF), compiler_params=_cparams(),
    )(wcomp)


def _expand_blocks(comp, name):
    nb = comp.shape[2]
    lanes_per_dir = ZL_W // 2

    def body(c_ref, o_ref):
        masks = [_group_mask(lambda lane, lb=lb: 2 * lb + lane // S5_N) for lb in range(4)]
        for r in range(2):
            for s in range(CHUNK):
                for ri in range(2):
                    m = c_ref[r, s, 0, :, ri * S5_N:(ri + 1) * S5_N]
                    mm = jnp.concatenate([m, m], axis=1)
                    for lb in range(4):
                        c0 = r * lanes_per_dir + ri * ZH + lb * LANE_BLOCK
                        o_ref[0, s * LANE_BLOCK:(s + 1) * LANE_BLOCK, c0:c0 + LANE_BLOCK] = (
                            jnp.where(masks[lb], mm, 0.0).astype(BF))

    return pl.pallas_call(
        body, name=name, grid=(nb,),
        in_specs=[pl.BlockSpec((2, CHUNK, 1, LANE_BLOCK, LANE_BLOCK), lambda b: (0, 0, b, 0, 0))],
        out_specs=pl.BlockSpec((1, BCR_W, ZL_W), lambda b: (b, 0, 0)),
        out_shape=jax.ShapeDtypeStruct((nb, BCR_W, ZL_W), BF), compiler_params=_cparams(),
    )(comp)


def _bdw(a, b_, kind, ctx, name):
    nb, j, ka = a.shape
    kb = b_.shape[2]
    tn = kb // 2
    tj = _cr_tile(j, 1024)
    nt = j // tj
    has_ctx = ctx is not None
    nd = 2 * CHUNK - 1

    def body(*refs):
        a_ref, b_ref = refs[0], refs[1]
        o_ref, acc_ref = refs[2 + 2 * has_ctx], refs[3 + 2 * has_ctx]
        h, t = pl.program_id(1), pl.program_id(2)

        @pl.when(t == 0)
        def _():
            if has_ctx:
                acc_ref[...] = _dot_tn(refs[2][0].astype(BF), refs[3][0].astype(BF))
            else:
                acc_ref[...] = jnp.zeros_like(acc_ref)

        acc_ref[...] += _dot_tn(a_ref[0].astype(BF), b_ref[0].astype(BF))

        if kind == "toeplitz":
            diag_ref = refs[4 + 2 * has_ctx]

            @pl.when(jnp.logical_and(t == 0, h == 0))
            def _():
                diag_ref[...] = jnp.zeros_like(diag_ref)

            @pl.when(t == nt - 1)
            def _():
                for s in range(CHUNK):
                    for tl in range(CHUNK // 2):
                        dd = h * (CHUNK // 2) + (tl - s + CHUNK - 1)
                        diag_ref[dd] += acc_ref[s * LANE_BLOCK:(s + 1) * LANE_BLOCK, tl * LANE_BLOCK:(tl + 1) * LANE_BLOCK]

            @pl.when(jnp.logical_and(t == nt - 1, h == 1))
            def _():
                mask = _group_mask(lambda lane: lane // S5_P)
                for dd in range(nd):
                    v = jnp.where(mask, diag_ref[dd], 0.0)
                    acc = v[:, :S5_P]
                    for k in range(1, GROUPS_PER_BLOCK):
                        acc = acc + v[:, k * S5_P:(k + 1) * S5_P]
                    o_ref[0, dd] = acc
        else:
            @pl.when(t == nt - 1)
            def _():
                masks = [_group_mask(lambda lane, lb=lb: 2 * lb + lane // S5_N) for lb in range(4)]
                for s in range(CHUNK):
                    for ri in range(2):
                        v = None
                        for lb in range(4):
                            c0 = ri * ZH + lb * LANE_BLOCK
                            blk = acc_ref[s * LANE_BLOCK:(s + 1) * LANE_BLOCK, c0:c0 + LANE_BLOCK]
                            blk = jnp.where(masks[lb], blk, 0.0)
                            v = blk if v is None else v + blk
                        o_ref[0, s, 0, :, ri * S5_N:(ri + 1) * S5_N] = v[:, :S5_N] + v[:, S5_N:]

    in_specs = [pl.BlockSpec((1, tj, ka), lambda b, h, t: (b, t, 0)), pl.BlockSpec((1, tj, tn), lambda b, h, t: (b, t, h))]
    args = [a, b_]
    if has_ctx:
        jc = ctx[0].shape[1]
        in_specs += [pl.BlockSpec((1, jc, ka), lambda b, h, t: (b, 0, 0)), pl.BlockSpec((1, jc, tn), lambda b, h, t: (b, 0, h))]
        args += list(ctx)
    scratch = [pltpu.VMEM((ka, tn), F32)]
    if kind == "toeplitz":
        ospec = pl.BlockSpec((1, nd, LANE_BLOCK, S5_P), lambda b, h, t: (b, 0, 0, 0))
        oshape = jax.ShapeDtypeStruct((nb, nd, LANE_BLOCK, S5_P), F32)
        scratch.append(pltpu.VMEM((nd, LANE_BLOCK, LANE_BLOCK), F32))
    else:
        ospec = pl.BlockSpec((1, CHUNK, 1, LANE_BLOCK, LANE_BLOCK), lambda b, h, t: (h, 0, b, 0, 0))
        oshape = jax.ShapeDtypeStruct((2, CHUNK, nb, LANE_BLOCK, LANE_BLOCK), F32)
    return pl.pallas_call(
        body, name=name, grid=(nb, 2, nt), in_specs=in_specs, out_specs=ospec, out_shape=oshape,
        scratch_shapes=scratch, compiler_params=_cparams(),
    )(*args)


def _scan(z_l, z_c, coef, chains, conj, s_l=None, s_c=None, name="l1_scan"):
    nb, jl, _ = z_l.shape
    jc = z_c.shape[1]
    with_da = s_l is not None
    sign = -1.0 if conj else 1.0
    hw = 2 * ZH

    def body(*refs):
        zl_ref, zc_ref, cf_ref = refs[:3]
        k = 3
        if with_da:
            sl_ref, sc_ref = refs[3:5]
            k = 5
        ol_ref, oc_ref = refs[k:k + 2]
        d = pl.program_id(1)
        rowi = lax.broadcasted_iota(jnp.int32, (8, ZH), 0)

        def coef_rows(r0, nr):
            return cf_ref[0, 0, r0:r0 + nr, :ZH], sign * cf_ref[0, 0, r0:r0 + nr, ZH:]

        steps = [(1, coef_rows(0, 1)), (2, coef_rows(1, 1)), (4, coef_rows(2, 1))]

        def run(chain):
            carry = (jnp.zeros((1, ZH), F32), jnp.zeros((1, ZH), F32))
            da = (jnp.zeros((8, ZH), F32), jnp.zeros((8, ZH), F32))
            for which, rev in chain:
                src, dst = (zc_ref, oc_ref) if which == "c" else (zl_ref, ol_ref)
                sref = (sc_ref if which == "c" else sl_ref) if with_da else None
                ng = (jc if which == "c" else jl) // 8
                tr, ti = coef_rows(16, 8) if rev else coef_rows(8, 8)

                def step(it, st, src=src, dst=dst, sref=sref, ng=ng, tr=tr, ti=ti, rev=rev):
                    cr_, ci_, dar, dai = st
                    g = (ng - 1 - it) if rev else it
                    off = pl.multiple_of(g * 8, 8)
                    xr = src[0, pl.ds(off, 8), :ZH]
                    xi = src[0, pl.ds(off, 8), ZH:]
                    for sh, (ar, ai) in steps:
                        if rev:
                            keep = rowi < 8 - sh
                            sr = jnp.where(keep, pltpu.roll(xr, 8 - sh, 0), 0.0)
                            si = jnp.where(keep, pltpu.roll(xi, 8 - sh, 0), 0.0)
                        else:
                            keep = rowi >= sh
                            sr = jnp.where(keep, pltpu.roll(xr, sh, 0), 0.0)
                            si = jnp.where(keep, pltpu.roll(xi, sh, 0), 0.0)
                        xr, xi = xr + ar * sr - ai * si, xi + ar * si + ai * sr
                    ir = xr + tr * cr_ - ti * ci_
                    ii = xi + tr * ci_ + ti * cr_
                    if rev:
                        er = jnp.where(rowi == 7, cr_, pltpu.roll(ir, 7, 0))
                        ei = jnp.where(rowi == 7, ci_, pltpu.roll(ii, 7, 0))
                        ncr, nci = ir[0:1], ii[0:1]
                    else:
                        er = jnp.where(rowi == 0, cr_, pltpu.roll(ir, 1, 0))
                        ei = jnp.where(rowi == 0, ci_, pltpu.roll(ii, 1, 0))
                        ncr, nci = ir[7:8], ii[7:8]
                    dst[0, pl.ds(off, 8), :ZH] = er
                    dst[0, pl.ds(off, 8), ZH:] = ei
                    if sref is not None:
                        s_r = sref[0, pl.ds(off, 8), :ZH]
                        s_i = sref[0, pl.ds(off, 8), ZH:]
                        dar = dar + s_r * er + s_i * ei
                        dai = dai + s_r * ei - s_i * er
                    return ncr, nci, dar, dai

                carry_da = lax.fori_loop(0, ng, step, (*carry, *da))
                carry, da = carry_da[:2], carry_da[2:]
            if with_da:
                refs[k + 2][0, 0] = jnp.concatenate([da[0], da[1]], axis=1)

        for dd in range(2):
            @pl.when(d == dd)
            def _(dd=dd):
                run(chains[dd])

    zspec_l = pl.BlockSpec((1, jl, hw), lambda b, d: (b, 0, d))
    zspec_c = pl.BlockSpec((1, jc, hw), lambda b, d: (b, 0, d))
    in_specs = [zspec_l, zspec_c, pl.BlockSpec((1, 1, 24, hw), lambda b, d: (b, d, 0, 0))]
    args = [z_l, z_c, coef]
    out_specs = [zspec_l, zspec_c]
    out_shape = [jax.ShapeDtypeStruct(z_l.shape, F32), jax.ShapeDtypeStruct(z_c.shape, F32)]
    if with_da:
        in_specs += [zspec_l, zspec_c]
        args += [s_l, s_c]
        out_specs.append(pl.BlockSpec((1, 1, 8, hw), lambda b, d: (b, d, 0, 0)))
        out_shape.append(jax.ShapeDtypeStruct((nb, 2, 8, hw), F32))
    return pl.pallas_call(body, name=name, grid=(nb, 2), in_specs=in_specs, out_specs=out_specs,
                          out_shape=out_shape, compiler_params=_cparams())(*args)


def _glu_fwd(y_bcr, z_cr, w_glu, b_glu):
    nb, j, _ = y_bcr.shape
    e = nb * LANE_BLOCK
    tj = _cr_tile(j)

    def body(y_ref, z_ref, w_hbm, b_ref, o_ref, sg_ref, w_ref):
        @pl.when(jnp.logical_and(pl.program_id(0) == 0, pl.program_id(1) == 0))
        def _():
            pltpu.sync_copy(w_hbm, w_ref)

        y = jnp.concatenate([y_ref[b] for b in range(nb)], axis=1).astype(F32)
        g = _gelu_parts(y)[0]
        sg = _sigmoid(_dot(g.astype(BF), w_ref[...]) + b_ref[...])
        z = z_ref[...].astype(F32)
        o_ref[...] = (g * sg * (z * _sigmoid(z))).astype(BF)
        sg_ref[...] = sg.astype(BF)

    tok = pl.BlockSpec((tj, e), lambda t, s: (t, s))
    return pl.pallas_call(
        body, name="l1_glu_fwd", grid=(j // tj, CHUNK),
        in_specs=[pl.BlockSpec((nb, tj, LANE_BLOCK), lambda t, s: (0, t, s)), tok, ANY, _full((1, e))],
        out_specs=[tok, tok],
        out_shape=[jax.ShapeDtypeStruct((j, CHUNK * e), BF), jax.ShapeDtypeStruct((j, CHUNK * e), BF)],
        scratch_shapes=[pltpu.VMEM(w_glu.shape, BF)], compiler_params=_cparams(),
    )(y_bcr, z_cr, w_glu, b_glu)


def _final(w_cr, w_out, xh_cr, tgt_cr, vecs):
    j, e16 = w_cr.shape
    e = e16 // CHUNK
    d = w_out.shape[1]
    tj = _cr_tile(j)

    def body(w_ref, wo_hbm, xh_ref, t_ref, v_ref, dr_ref, acc_ref, wo_ref):
        @pl.when(jnp.logical_and(pl.program_id(0) == 0, pl.program_id(1) == 0))
        def _():
            pltpu.sync_copy(wo_hbm, wo_ref)
            acc_ref[...] = jnp.zeros_like(acc_ref)

        o = _dot(w_ref[...], wo_ref[...])
        x1 = xh_ref[...] * v_ref[0:1, :] + v_ref[1:2, :]
        rr = DN_ALPHA * x1 + v_ref[2:3, :] * o
        mu = jnp.mean(rr, axis=-1, keepdims=True)
        cen = rr - mu
        rstd = lax.rsqrt(jnp.mean(cen * cen, axis=-1, keepdims=True) + LN_EPS)
        xh2 = cen * rstd
        err = xh2 * v_ref[3:4, :] + v_ref[4:5, :] - t_ref[...]
        dy = err * (1.0 / d)
        dxh = dy * v_ref[3:4, :]
        dr = rstd * (dxh - jnp.mean(dxh, axis=-1, keepdims=True) - xh2 * jnp.mean(dxh * xh2, axis=-1, keepdims=True))
        dr_ref[...] = dr.astype(BF)
        acc_ref[0:1, :] += jnp.sum(dy * xh2, axis=0, keepdims=True)
        acc_ref[1:2, :] += jnp.sum(dy, axis=0, keepdims=True)
        acc_ref[2:3, :] += jnp.sum(dr * o, axis=0, keepdims=True)
        acc_ref[3:4, :] += (0.5 / d) * jnp.sum(err * err, axis=0, keepdims=True)

    tok_d = pl.BlockSpec((tj, d), lambda t, s: (t, s))
    return pl.pallas_call(
        body, name="l1_final", grid=(j // tj, CHUNK),
        in_specs=[pl.BlockSpec((tj, e), lambda t, s: (t, s)), ANY, tok_d, tok_d, _full((8, d))],
        out_specs=[tok_d, _full((8, d))],
        out_shape=[jax.ShapeDtypeStruct((j, CHUNK * d), BF), jax.ShapeDtypeStruct((8, d), F32)],
        scratch_shapes=[pltpu.VMEM(w_out.shape, BF)], compiler_params=_cparams(),
    )(w_cr, w_out, xh_cr, tgt_cr, vecs)


def _glu_bwd(dr_cr, gt1, w_out, w_glu, y_bcr, z_cr, sg_cr):
    nb, j, _ = y_bcr.shape
    e, d = w_out.shape
    tj = _cr_tile(j)

    def body(dr_ref, g_ref, wo_hbm, wg_hbm, y_ref, z_ref, sg_ref, dz_ref, dt_ref, dy_ref, wo_ref, wg_ref):
        @pl.when(jnp.logical_and(pl.program_id(0) == 0, pl.program_id(1) == 0))
        def _():
            pltpu.sync_copy(wo_hbm, wo_ref)
            pltpu.sync_copy(wg_hbm, wg_ref)

        do = (dr_ref[...].astype(F32) * g_ref[...]).astype(BF)
        dw = _dot_nt(do, wo_ref[...])
        y = jnp.concatenate([y_ref[b] for b in range(nb)], axis=1).astype(F32)
        g, dgel = _gelu_parts(y)
        z = z_ref[...].astype(F32)
        sz = _sigmoid(z)
        sg = sg_ref[...].astype(F32)
        dg2 = dw * (z * sz)
        dz_ref[...] = (dw * g * sg * (sz * (1.0 + z * (1.0 - sz)))).astype(BF)
        dt = (dg2 * g * sg * (1.0 - sg)).astype(BF)
        dt_ref[...] = dt
        dy = (dg2 * sg + _dot_nt(dt, wg_ref[...])) * dgel
        for b in range(nb):
            dy_ref[b] = dy[:, b * LANE_BLOCK:(b + 1) * LANE_BLOCK].astype(BF)

    tok_e = pl.BlockSpec((tj, e), lambda t, s: (t, s))
    blk = pl.BlockSpec((nb, tj, LANE_BLOCK), lambda t, s: (0, t, s))
    return pl.pallas_call(
        body, name="l1_glu_bwd", grid=(j // tj, CHUNK),
        in_specs=[pl.BlockSpec((tj, d), lambda t, s: (t, s)), _full((1, d)), ANY, ANY, blk, tok_e, tok_e],
        out_specs=[tok_e, tok_e, blk],
        out_shape=[jax.ShapeDtypeStruct((j, CHUNK * e), BF), jax.ShapeDtypeStruct((j, CHUNK * e), BF),
                   jax.ShapeDtypeStruct((nb, j, BCR_W), BF)],
        scratch_shapes=[pltpu.VMEM(w_out.shape, BF), pltpu.VMEM(w_glu.shape, BF)], compiler_params=_cparams(),
    )(dr_cr, gt1, w_out, w_glu, y_bcr, z_cr, sg_cr)


def _bwd_inproj1(du_bcr, dz_cr, w, xh_cr, rs_cr, dr2_cr, vecs, tag):
    nb, j, _ = du_bcr.shape
    d = w.shape[0]
    e = w.shape[1] // 2
    tj = _cr_tile(j)

    def body(du_ref, dz_ref, w_hbm, xh_ref, rs_ref, dr2_ref, v_ref, dr1_ref, acc_ref, w_ref):
        @pl.when(jnp.logical_and(pl.program_id(0) == 0, pl.program_id(1) == 0))
        def _():
            pltpu.sync_copy(w_hbm, w_ref)
            acc_ref[...] = jnp.zeros_like(acc_ref)

        du = jnp.concatenate([du_ref[b] for b in range(nb)], axis=1)
        dh = _dot_nt(du, w_ref[:, :e]) + _dot_nt(dz_ref[...], w_ref[:, e:])
        xh = xh_ref[...]
        x1 = xh * v_ref[0:1, :] + v_ref[1:2, :]
        dx1 = DN_ALPHA * dr2_ref[...].astype(F32) + dh * v_ref[2:3, :]
        dxh = dx1 * v_ref[0:1, :]
        rstd = rs_ref[:, 0:1]
        dr1 = rstd * (dxh - jnp.mean(dxh, axis=-1, keepdims=True) - xh * jnp.mean(dxh * xh, axis=-1, keepdims=True))
        dr1_ref[...] = dr1.astype(BF)
        acc_ref[0:1, :] += jnp.sum(dh * x1, axis=0, keepdims=True)
        acc_ref[1:2, :] += jnp.sum(dh, axis=0, keepdims=True)
        acc_ref[2:3, :] += jnp.sum(dx1 * xh, axis=0, keepdims=True)
        acc_ref[3:4, :] += jnp.sum(dx1, axis=0, keepdims=True)

    tok_d = pl.BlockSpec((tj, d), lambda t, s: (t, s))
    return pl.pallas_call(
        body, name="l1_bwd_inproj_" + tag, grid=(j // tj, CHUNK),
        in_specs=[pl.BlockSpec((nb, tj, LANE_BLOCK), lambda t, s: (0, t, s)), pl.BlockSpec((tj, e), lambda t, s: (t, s)),
                  ANY, tok_d, pl.BlockSpec((tj, 128), lambda t, s: (t, s)), tok_d, _full((8, d))],
        out_specs=[tok_d, _full((8, d))],
        out_shape=[jax.ShapeDtypeStruct((j, CHUNK * d), BF), jax.ShapeDtypeStruct((8, d), F32)],
        scratch_shapes=[pltpu.VMEM(w.shape, BF)], compiler_params=_cparams(),
    )(du_bcr, dz_cr, w, xh_cr, rs_cr, dr2_cr, vecs)


def _dw_cr(lhs, rhs, lhs_kind, rhs_kind, vec, bias_sum, init, name):
    if lhs_kind == "gelu_bcr":
        nb_l, j, _ = lhs.shape
        k = nb_l * LANE_BLOCK
    else:
        j = lhs.shape[0]
        k = lhs.shape[1] // CHUNK
    if rhs_kind == "bcr":
        nb_r = rhs.shape[0]
        n = nb_r * LANE_BLOCK
    else:
        n = rhs.shape[1] // CHUNK
    tj = _cr_tile(j, 512)
    nh = 2 if k * n * 4 > (8 << 20) else 1
    tn = n // nh
    nbh = tn // LANE_BLOCK
    nt = j // tj
    has_init = init is not None

    def body(*refs):
        refs = list(refs)
        l_ref, r_ref = refs[0], refs[1]
        pos = 2
        v_ref = None
        if vec is not None:
            v_ref = refs[pos]
            pos += 1
        i_ref = None
        if has_init:
            i_ref = refs[pos]
            pos += 1
        o_ref = refs[pos]
        pos += 1
        bs_ref = None
        if bias_sum:
            bs_ref = refs[pos]
            pos += 1
        acc_ref = refs[pos]
        t, s = pl.program_id(1), pl.program_id(2)
        first = jnp.logical_and(t == 0, s == 0)

        @pl.when(first)
        def _():
            acc_ref[...] = i_ref[...] if has_init else jnp.zeros_like(acc_ref)
            if bias_sum:
                bs_ref[...] = jnp.zeros_like(bs_ref)

        if lhs_kind == "gelu_bcr":
            y = jnp.concatenate([l_ref[b] for b in range(nb_l)], axis=1).astype(F32)
            lv = _gelu_parts(y)[0].astype(BF)
        elif lhs_kind == "mod":
            lv = (l_ref[...] * v_ref[0:1, :] + v_ref[1:2, :]).astype(BF)
        else:
            lv = l_ref[...]
        if rhs_kind == "bcr":
            rv = jnp.concatenate([r_ref[b] for b in range(nbh)], axis=1)
        elif rhs_kind == "scaled":
            rv = (r_ref[...].astype(F32) * v_ref[0:1, :]).astype(BF)
        else:
            rv = r_ref[...]
        acc_ref[...] += _dot_tn(lv, rv)
        if bias_sum:
            bs_ref[0:1, :] += jnp.sum(rv.astype(F32), axis=0, keepdims=True)

        @pl.when(jnp.logical_and(t == nt - 1, s == CHUNK - 1))
        def _():
            o_ref[...] = acc_ref[...].astype(BF)

    if lhs_kind == "gelu_bcr":
        l_spec = pl.BlockSpec((nb_l, tj, LANE_BLOCK), lambda h, t, s: (0, t, s))
    else:
        l_spec = pl.BlockSpec((tj, k), lambda h, t, s: (t, s))
    if rhs_kind == "bcr":
        r_spec = pl.BlockSpec((nbh, tj, LANE_BLOCK), lambda h, t, s: (h, t, s))
    else:
        r_spec = pl.BlockSpec((tj, tn), lambda h, t, s: (t, s * nh + h))
    in_specs, args = [l_spec, r_spec], [lhs, rhs]
    if vec is not None:
        in_specs.append(_full(vec.shape))
        args.append(vec)
    o_spec = pl.BlockSpec((k, tn), lambda h, t, s: (0, h))
    if has_init:
        in_specs.append(o_spec)
        args.append(init)
    out_specs, out_shape = [o_spec], [jax.ShapeDtypeStruct((k, n), BF)]
    if bias_sum:
        out_specs.append(pl.BlockSpec((8, tn), lambda h, t, s: (0, h)))
        out_shape.append(jax.ShapeDtypeStruct((8, n), F32))
    res = pl.pallas_call(
        body, name=name, grid=(nh, nt, CHUNK), in_specs=in_specs, out_specs=out_specs, out_shape=out_shape,
        scratch_shapes=[pltpu.VMEM((k, tn), F32)], compiler_params=_cparams(),
    )(*args)
    return res if bias_sum else res[0]


def _dw_cr_f32(lhs, rhs, vec, name):
    j = lhs.shape[0]
    k = lhs.shape[1] // CHUNK
    nb_r = rhs.shape[0]
    n = nb_r * LANE_BLOCK
    tj = _cr_tile(j)
    nt = j // tj

    def body(l_ref, r_ref, v_ref, o_ref):
        @pl.when(jnp.logical_and(pl.program_id(0) == 0, pl.program_id(1) == 0))
        def _():
            o_ref[...] = jnp.zeros_like(o_ref)

        lv = (l_ref[...] * v_ref[0:1, :] + v_ref[1:2, :]).astype(BF)
        rv = jnp.concatenate([r_ref[b] for b in range(nb_r)], axis=1)
        o_ref[...] += _dot_tn(lv, rv)

    return pl.pallas_call(
        body, name=name, grid=(nt, CHUNK),
        in_specs=[pl.BlockSpec((tj, k), lambda t, s: (t, s)), pl.BlockSpec((nb_r, tj, LANE_BLOCK), lambda t, s: (0, t, s)),
                  _full(vec.shape)],
        out_specs=_full((k, n)), out_shape=jax.ShapeDtypeStruct((k, n), F32), compiler_params=_cparams(),
    )(lhs, rhs, vec)


GT_ROWS = CHUNK * S5_P
ZG_W = 2 * 2 * S5_N
PAIR_W = 2 * ZG_W
GROUPS_PER_STEP = 4


def _inproj1_gt(xh_cr, a1, b1, wu_t, w_z, tag):
    j, d16 = xh_cr.shape
    d = d16 // CHUNK
    e = wu_t.shape[0]
    g = e // S5_P
    tj = _cr_tile(j, 256)

    def body(x_ref, a_ref, b_ref, wu_hbm, wz_hbm, u_ref, z_ref, wu_ref, wz_ref):
        @pl.when(jnp.logical_and(pl.program_id(0) == 0, pl.program_id(1) == 0))
        def _():
            pltpu.sync_copy(wu_hbm, wu_ref)
            pltpu.sync_copy(wz_hbm, wz_ref)

        h = (x_ref[...] * a_ref[...] + b_ref[...]).astype(BF)
        u_ref[...] = _dot_nt(wu_ref[...], h).reshape(g, S5_P, tj).astype(BF)
        z_ref[...] = _dot(h, wz_ref[...]).astype(BF)

    return pl.pallas_call(
        body, name="l1_inproj_" + tag, grid=(j // tj, CHUNK),
        in_specs=[pl.BlockSpec((tj, d), lambda t, s: (t, s)), _full((1, d)), _full((1, d)), ANY, ANY],
        out_specs=[pl.BlockSpec((g, S5_P, tj), lambda t, s: (0, s, t)), pl.BlockSpec((tj, e), lambda t, s: (t, s))],
        out_shape=[jax.ShapeDtypeStruct((g, GT_ROWS, j), BF), jax.ShapeDtypeStruct((j, CHUNK * e), BF)],
        scratch_shapes=[pltpu.VMEM(wu_t.shape, BF), pltpu.VMEM(w_z.shape, BF)], compiler_params=_cparams(),
    )(xh_cr, a1, b1, wu_t, w_z)


def _gt_spec(j, gb=GROUPS_PER_STEP):
    return pl.BlockSpec((gb, GT_ROWS, j), lambda i: (i, 0, 0))


def _zg_spec(j, gb=GROUPS_PER_STEP):
    return pl.BlockSpec((j, gb * ZG_W), lambda i: (0, i))


def _w_spec(width, gb=GROUPS_PER_STEP):
    return pl.BlockSpec((gb, GT_ROWS, width), lambda i: (i, 0, 0))


def _pair_lanes(k):
    return slice((k // 2) * PAIR_W, (k // 2 + 1) * PAIR_W)


def _s5_z(ut_l, ut_c, bc):
    g, _, jl = ut_l.shape
    jc = ut_c.shape[2]
    gb = GROUPS_PER_STEP

    def body(ul_ref, uc_ref, bc_ref, zl_ref, zc_ref):
        for k in range(0, gb, 2):
            zl_ref[:, _pair_lanes(k)] = _dot_tn(ul_ref[k], bc_ref[k]) + _dot_tn(ul_ref[k + 1], bc_ref[k + 1])
            zc_ref[:, _pair_lanes(k)] = _dot_tn(uc_ref[k], bc_ref[k]) + _dot_tn(uc_ref[k + 1], bc_ref[k + 1])

    return pl.pallas_call(
        body, name="l1_s5_z", grid=(g // gb,), in_specs=[_gt_spec(jl), _gt_spec(jc), _w_spec(PAIR_W)],
        out_specs=[_zg_spec(jl), _zg_spec(jc)],
        out_shape=[jax.ShapeDtypeStruct((jl, g * ZG_W), F32), jax.ShapeDtypeStruct((jc, g * ZG_W), F32)],
        compiler_params=_cparams(),
    )(ut_l, ut_c, bc)


def _s5_y(ut_l, s_l, mt_t, cct):
    g, _, jl = ut_l.shape
    gb = GROUPS_PER_STEP

    def body(u_ref, s_ref, mt_ref, cc_ref, y_ref):
        for k in range(gb):
            s_k = s_ref[:, _pair_lanes(k)].astype(BF)
            y_ref[k] = (_dot(mt_ref[k], u_ref[k]) + _dot_nt(cc_ref[k], s_k)).astype(BF)

    return pl.pallas_call(
        body, name="l1_s5_y", grid=(g // gb,),
        in_specs=[_gt_spec(jl), _zg_spec(jl), _w_spec(GT_ROWS), _w_spec(PAIR_W)],
        out_specs=_gt_spec(jl), out_shape=jax.ShapeDtypeStruct((g, GT_ROWS, jl), BF), compiler_params=_cparams(),
    )(ut_l, s_l, mt_t, cct)


def _s5_ds(dyt_l, cct):
    g, _, jl = dyt_l.shape
    gb = GROUPS_PER_STEP

    def body(dy_ref, cc_ref, ds_ref):
        for k in range(0, gb, 2):
            ds_ref[:, _pair_lanes(k)] = _dot_tn(dy_ref[k], cc_ref[k]) + _dot_tn(dy_ref[k + 1], cc_ref[k + 1])

    return pl.pallas_call(
        body, name="l1_s5_ds", grid=(g // gb,), in_specs=[_gt_spec(jl), _w_spec(PAIR_W)], out_specs=_zg_spec(jl),
        out_shape=jax.ShapeDtypeStruct((jl, g * ZG_W), F32), compiler_params=_cparams(),
    )(dyt_l, cct)


def _s5_dx(dyt_l, dz_l, dz_c, mt, bc):
    g, _, jl = dyt_l.shape
    jc = dz_c.shape[0]
    gb = GROUPS_PER_STEP

    def body(dy_ref, dzl_ref, dzc_ref, mt_ref, bc_ref, dul_ref, duc_ref):
        for k in range(gb):
            dzl = dzl_ref[:, _pair_lanes(k)].astype(BF)
            dzc = dzc_ref[:, _pair_lanes(k)].astype(BF)
            dul_ref[k] = (_dot(mt_ref[k], dy_ref[k]) + _dot_nt(bc_ref[k], dzl)).astype(BF)
            duc_ref[k] = _dot_nt(bc_ref[k], dzc).astype(BF)

    return pl.pallas_call(
        body, name="l1_s5_dx", grid=(g // gb,),
        in_specs=[_gt_spec(jl), _zg_spec(jl), _zg_spec(jc), _w_spec(GT_ROWS), _w_spec(PAIR_W)],
        out_specs=[_gt_spec(jl), _gt_spec(jc)],
        out_shape=[jax.ShapeDtypeStruct((g, GT_ROWS, jl), BF), jax.ShapeDtypeStruct((g, GT_ROWS, jc), BF)],
        compiler_params=_cparams(),
    )(dyt_l, dz_l, dz_c, mt, bc)


def _s5_dw(ut_l, ut_c, dyt_l, dz_l, dz_c, s_l):
    g, _, jl = ut_l.shape
    jc = ut_c.shape[2]
    gb = GROUPS_PER_STEP

    def body(ul_ref, uc_ref, dy_ref, dzl_ref, dzc_ref, s_ref, dmt_ref, dbc_ref, dcc_ref):
        for k in range(gb):
            lanes = _pair_lanes(k)
            dmt_ref[k] = _dot_nt(ul_ref[k], dy_ref[k])
            dbc_ref[k] = (_dot(ul_ref[k], dzl_ref[:, lanes].astype(BF))
                          + _dot(uc_ref[k], dzc_ref[:, lanes].astype(BF)))
            dcc_ref[k] = _dot(dy_ref[k], s_ref[:, lanes].astype(BF))

    sd_m = jax.ShapeDtypeStruct((g, GT_ROWS, GT_ROWS), F32)
    sd_p = jax.ShapeDtypeStruct((g, GT_ROWS, PAIR_W), F32)
    return pl.pallas_call(
        body, name="l1_s5_dw", grid=(g // gb,),
        in_specs=[_gt_spec(jl), _gt_spec(jc), _gt_spec(jl), _zg_spec(jl), _zg_spec(jc), _zg_spec(jl)],
        out_specs=[_w_spec(GT_ROWS), _w_spec(PAIR_W), _w_spec(PAIR_W)], out_shape=[sd_m, sd_p, sd_p],
        compiler_params=_cparams(),
    )(ut_l, ut_c, dyt_l, dz_l, dz_c, s_l)


def _scan_g(z_l, z_c, coef, chains, conj, s_l=None, s_c=None, name="l1_scan"):
    jl, w_all = z_l.shape
    jc = z_c.shape[0]
    gb = GROUPS_PER_STEP
    wb = gb * ZG_W
    nch = wb // 256
    with_da = s_l is not None
    sign = -1.0 if conj else 1.0

    def body(*refs):
        zl_ref, zc_ref, cf_ref = refs[:3]
        k0 = 3
        if with_da:
            sl_ref, sc_ref = refs[3:5]
            k0 = 5
        ol_ref, oc_ref = refs[k0:k0 + 2]
        rowi = lax.broadcasted_iota(jnp.int32, (8, 128), 0)

        def lanes_of(ch):
            return slice(ch * 256, ch * 256 + 128), slice(ch * 256 + 128, (ch + 1) * 256)

        def coefs(ch, r0, nr):
            lr, li = lanes_of(ch)
            return cf_ref[r0:r0 + nr, lr], sign * cf_ref[r0:r0 + nr, li]

        def shift(v, sh, rev):
            if rev:
                return jnp.where(rowi < 8 - sh, pltpu.roll(v, 8 - sh, 0), 0.0)
            return jnp.where(rowi >= sh, pltpu.roll(v, sh, 0), 0.0)

        zero_row = jnp.zeros((1, 128), F32)
        zero_tile = jnp.zeros((8, 128), F32)
        carry = [zero_row] * (2 * nch)
        da = [zero_tile] * (2 * nch)
        for seg in range(len(chains[0])):
            which = chains[0][seg][0]
            assert chains[1][seg][0] == which
            revs = (chains[0][seg][1], chains[1][seg][1])
            src, dst = (zc_ref, oc_ref) if which == "c" else (zl_ref, ol_ref)
            sref = ((sc_ref if which == "c" else sl_ref) if with_da else None)
            ng = (jc if which == "c" else jl) // 8

            def step(it, st, src=src, dst=dst, sref=sref, ng=ng, revs=revs):
                carry_, da_ = list(st[:2 * nch]), list(st[2 * nch:])
                for ch in range(nch):
                    rev = revs[ch % 2]
                    lr, li = lanes_of(ch)
                    grp = (ng - 1 - it) if rev else it
                    off = pl.multiple_of(grp * 8, 8)
                    xr, xi = src[pl.ds(off, 8), lr], src[pl.ds(off, 8), li]
                    for sh, r0 in ((1, 0), (2, 1), (4, 2)):
                        ar, ai = coefs(ch, r0, 1)
                        sr, si = shift(xr, sh, rev), shift(xi, sh, rev)
                        xr, xi = xr + ar * sr - ai * si, xi + ar * si + ai * sr
                    tr, ti = coefs(ch, 16, 8) if rev else coefs(ch, 8, 8)
                    cr_, ci_ = carry_[2 * ch], carry_[2 * ch + 1]
                    ir = xr + tr * cr_ - ti * ci_
                    ii = xi + tr * ci_ + ti * cr_
                    if rev:
                        er = jnp.where(rowi == 7, cr_, pltpu.roll(ir, 7, 0))
                        ei = jnp.where(rowi == 7, ci_, pltpu.roll(ii, 7, 0))
                        carry_[2 * ch], carry_[2 * ch + 1] = ir[0:1], ii[0:1]
                    else:
                        er = jnp.where(rowi == 0, cr_, pltpu.roll(ir, 1, 0))
                        ei = jnp.where(rowi == 0, ci_, pltpu.roll(ii, 1, 0))
                        carry_[2 * ch], carry_[2 * ch + 1] = ir[7:8], ii[7:8]
                    dst[pl.ds(off, 8), lr] = er
                    dst[pl.ds(off, 8), li] = ei
                    if sref is not None:
                        s_r, s_i = sref[pl.ds(off, 8), lr], sref[pl.ds(off, 8), li]
                        da_[2 * ch] = da_[2 * ch] + s_r * er + s_i * ei
                        da_[2 * ch + 1] = da_[2 * ch + 1] + s_r * ei - s_i * er
                return (*carry_, *da_)

            st = lax.fori_loop(0, ng, step, (*carry, *da))
            carry, da = list(st[:2 * nch]), list(st[2 * nch:])
        if with_da:
            da_ref = refs[k0 + 2]
            for ch in range(nch):
                lr, li = lanes_of(ch)
                da_ref[:, lr] = da[2 * ch]
                da_ref[:, li] = da[2 * ch + 1]

    in_specs = [_zg_spec(jl), _zg_spec(jc), pl.BlockSpec((24, wb), lambda i: (0, i))]
    args = [z_l, z_c, coef]
    out_specs = [_zg_spec(jl), _zg_spec(jc)]
    out_shape = [jax.ShapeDtypeStruct(z_l.shape, F32), jax.ShapeDtypeStruct(z_c.shape, F32)]
    if with_da:
        in_specs += [_zg_spec(jl), _zg_spec(jc)]
        args += [s_l, s_c]
        out_specs.append(pl.BlockSpec((8, wb), lambda i: (0, i)))
        out_shape.append(jax.ShapeDtypeStruct((8, w_all), F32))
    return pl.pallas_call(body, name=name, grid=(w_all // wb,), in_specs=in_specs, out_specs=out_specs,
                          out_shape=out_shape, compiler_params=_cparams())(*args)


def _gt_tok_spec(g, tj):
    return pl.BlockSpec((g, S5_P, tj), lambda t, s: (0, s, t))


def _glu_fwd_gt(yt, z_cr, w_glu, b_glu):
    g, _, j = yt.shape
    e = g * S5_P
    tj = _cr_tile(j)

    def body(y_ref, z_ref, w_hbm, b_ref, o_ref, sg_ref, w_ref):
        @pl.when(jnp.logical_and(pl.program_id(0) == 0, pl.program_id(1) == 0))
        def _():
            pltpu.sync_copy(w_hbm, w_ref)

        y = jnp.transpose(y_ref[...].reshape(e, tj).astype(F32))
        gl = _gelu_parts(y)[0]
        sg = _sigmoid(_dot(gl.astype(BF), w_ref[...]) + b_ref[...])
        z = z_ref[...].astype(F32)
        o_ref[...] = (gl * sg * (z * _sigmoid(z))).astype(BF)
        sg_ref[...] = sg.astype(BF)

    tok = pl.BlockSpec((tj, e), lambda t, s: (t, s))
    return pl.pallas_call(
        body, name="l1_glu_fwd", grid=(j // tj, CHUNK),
        in_specs=[_gt_tok_spec(g, tj), tok, ANY, _full((1, e))], out_specs=[tok, tok],
        out_shape=[jax.ShapeDtypeStruct((j, CHUNK * e), BF), jax.ShapeDtypeStruct((j, CHUNK * e), BF)],
        scratch_shapes=[pltpu.VMEM(w_glu.shape, BF)], compiler_params=_cparams(),
    )(yt, z_cr, w_glu, b_glu)


def _glu_bwd_gt(dr_cr, gt1, w_out, w_glu, yt, z_cr, sg_cr):
    g, _, j = yt.shape
    e, d = w_out.shape
    tj = _cr_tile(j)

    def body(dr_ref, g_ref, wo_hbm, wg_hbm, y_ref, z_ref, sg_ref, dz_ref, dt_ref, dy_ref, wo_ref, wg_ref):
        @pl.when(jnp.logical_and(pl.program_id(0) == 0, pl.program_id(1) == 0))
        def _():
            pltpu.sync_copy(wo_hbm, wo_ref)
            pltpu.sync_copy(wg_hbm, wg_ref)

        do = (dr_ref[...].astype(F32) * g_ref[...]).astype(BF)
        dw = _dot_nt(do, wo_ref[...])
        y = jnp.transpose(y_ref[...].reshape(e, tj).astype(F32))
        gl, dgel = _gelu_parts(y)
        z = z_ref[...].astype(F32)
        sz = _sigmoid(z)
        sg = sg_ref[...].astype(F32)
        dg2 = dw * (z * sz)
        dz_ref[...] = (dw * gl * sg * (sz * (1.0 + z * (1.0 - sz)))).astype(BF)
        dt = (dg2 * gl * sg * (1.0 - sg)).astype(BF)
        dt_ref[...] = dt
        dy = (dg2 * sg + _dot_nt(dt, wg_ref[...])) * dgel
        dy_ref[...] = jnp.transpose(dy).reshape(g, S5_P, tj).astype(BF)

    tok_e = pl.BlockSpec((tj, e), lambda t, s: (t, s))
    return pl.pallas_call(
        body, name="l1_glu_bwd", grid=(j // tj, CHUNK),
        in_specs=[pl.BlockSpec((tj, d), lambda t, s: (t, s)), _full((1, d)), ANY, ANY, _gt_tok_spec(g, tj), tok_e, tok_e],
        out_specs=[tok_e, tok_e, _gt_tok_spec(g, tj)],
        out_shape=[jax.ShapeDtypeStruct((j, CHUNK * e), BF), jax.ShapeDtypeStruct((j, CHUNK * e), BF),
                   jax.ShapeDtypeStruct((g, GT_ROWS, j), BF)],
        scratch_shapes=[pltpu.VMEM(w_out.shape, BF), pltpu.VMEM(w_glu.shape, BF)], compiler_params=_cparams(),
    )(dr_cr, gt1, w_out, w_glu, yt, z_cr, sg_cr)


def _bwd_inproj1_gt(dut, dz_cr, wu_t, w_z, xh_cr, rs_cr, dr2_cr, vecs, tag):
    g, _, j = dut.shape
    e, d = wu_t.shape
    tj = _cr_tile(j)

    def body(du_ref, dz_ref, wu_hbm, wz_hbm, xh_ref, rs_ref, dr2_ref, v_ref, dr1_ref, acc_ref, wu_ref, wz_ref):
        @pl.when(jnp.logical_and(pl.program_id(0) == 0, pl.program_id(1) == 0))
        def _():
            pltpu.sync_copy(wu_hbm, wu_ref)
            pltpu.sync_copy(wz_hbm, wz_ref)
            acc_ref[...] = jnp.zeros_like(acc_ref)

        dh = _dot_tn(du_ref[...].reshape(e, tj), wu_ref[...]) + _dot_nt(dz_ref[...], wz_ref[...])
        xh = xh_ref[...]
        x1 = xh * v_ref[0:1, :] + v_ref[1:2, :]
        dx1 = DN_ALPHA * dr2_ref[...].astype(F32) + dh * v_ref[2:3, :]
        dxh = dx1 * v_ref[0:1, :]
        rstd = rs_ref[:, 0:1]
        dr1 = rstd * (dxh - jnp.mean(dxh, axis=-1, keepdims=True) - xh * jnp.mean(dxh * xh, axis=-1, keepdims=True))
        dr1_ref[...] = dr1.astype(BF)
        acc_ref[0:1, :] += jnp.sum(dh * x1, axis=0, keepdims=True)
        acc_ref[1:2, :] += jnp.sum(dh, axis=0, keepdims=True)
        acc_ref[2:3, :] += jnp.sum(dx1 * xh, axis=0, keepdims=True)
        acc_ref[3:4, :] += jnp.sum(dx1, axis=0, keepdims=True)

    tok_d = pl.BlockSpec((tj, d), lambda t, s: (t, s))
    return pl.pallas_call(
        body, name="l1_bwd_inproj_" + tag, grid=(j // tj, CHUNK),
        in_specs=[_gt_tok_spec(g, tj), pl.BlockSpec((tj, e), lambda t, s: (t, s)), ANY, ANY, tok_d,
                  pl.BlockSpec((tj, 128), lambda t, s: (t, s)), tok_d, _full((8, d))],
        out_specs=[tok_d, _full((8, d))],
        out_shape=[jax.ShapeDtypeStruct((j, CHUNK * d), BF), jax.ShapeDtypeStruct((8, d), F32)],
        scratch_shapes=[pltpu.VMEM(wu_t.shape, BF), pltpu.VMEM(w_z.shape, BF)], compiler_params=_cparams(),
    )(dut, dz_cr, wu_t, w_z, xh_cr, rs_cr, dr2_cr, vecs)


def _dw_gt(lhs_gt, rhs_cr, lhs_gelu, vec, bias_sum, init, out_dtype, name):
    g, _, j = lhs_gt.shape
    e = g * S5_P
    n = rhs_cr.shape[1] // CHUNK
    tj = _cr_tile(j, 512 if j % 512 == 0 else 256)
    nh = 2 if e * n * 4 > (8 << 20) else 1
    tn = n // nh
    nt = j // tj
    has_init = init is not None

    def body(*refs):
        refs = list(refs)
        l_ref, r_ref = refs[0], refs[1]
        pos = 2
        v_ref = i_ref = bs_ref = None
        if vec is not None:
            v_ref = refs[pos]
            pos += 1
        if has_init:
            i_ref = refs[pos]
            pos += 1
        o_ref = refs[pos]
        pos += 1
        if bias_sum:
            bs_ref = refs[pos]
            pos += 1
        acc_ref = refs[pos]
        t, s = pl.program_id(1), pl.program_id(2)

        @pl.when(jnp.logical_and(t == 0, s == 0))
        def _():
            acc_ref[...] = i_ref[...] if has_init else jnp.zeros_like(acc_ref)
            if bias_sum:
                bs_ref[...] = jnp.zeros_like(bs_ref)

        lv = l_ref[...].reshape(e, tj)
        if lhs_gelu:
            lv = _gelu_parts(lv.astype(F32))[0].astype(BF)
        if vec is not None:
            rv = (r_ref[...] * v_ref[0:1, :] + v_ref[1:2, :]).astype(BF)
        else:
            rv = r_ref[...]
        acc_ref[...] += _dot(lv, rv)
        if bias_sum:
            bs_ref[0:1, :] += jnp.sum(rv.astype(F32), axis=0, keepdims=True)

        @pl.when(jnp.logical_and(t == nt - 1, s == CHUNK - 1))
        def _():
            o_ref[...] = acc_ref[...].astype(out_dtype)

    in_specs = [pl.BlockSpec((g, S5_P, tj), lambda h, t, s: (0, s, t)),
                pl.BlockSpec((tj, tn), lambda h, t, s: (t, s * nh + h))]
    args = [lhs_gt, rhs_cr]
    if vec is not None:
        in_specs.append(_full(vec.shape))
        args.append(vec)
    o_spec = pl.BlockSpec((e, tn), lambda h, t, s: (0, h))
    if has_init:
        in_specs.append(o_spec)
        args.append(init)
    out_specs, out_shape = [o_spec], [jax.ShapeDtypeStruct((e, n), out_dtype)]
    if bias_sum:
        out_specs.append(pl.BlockSpec((8, tn), lambda h, t, s: (0, h)))
        out_shape.append(jax.ShapeDtypeStruct((8, n), F32))
    res = pl.pallas_call(
        body, name=name, grid=(nh, nt, CHUNK), in_specs=in_specs, out_specs=out_specs, out_shape=out_shape,
        scratch_shapes=[pltpu.VMEM((e, tn), F32)], compiler_params=_cparams(),
    )(*args)
    return res if bias_sum else res[0]


def _s5_weights(lam_re, lam_im, log_step, b_re, b_im, c_re, c_im, d_skip):
    hp = lax.Precision.HIGHEST
    g = lam_re.shape[1]
    t, p, n = CHUNK, S5_P, S5_N
    dt = jnp.exp(log_step)[..., None]
    ks = jnp.arange(t + 1, dtype=F32).reshape(t + 1, 1, 1, 1)
    mag = jnp.exp(ks * (lam_re * dt)[None])
    ang = ks * (lam_im * dt)[None]
    pr, pi = mag * jnp.cos(ang), mag * jnp.sin(ang)
    ar, ai = pr[1], pi[1]
    qr, qi = ar - 1.0, ai
    den = lam_re * lam_re + lam_im * lam_im
    fr = (qr * lam_re + qi * lam_im) / den
    fi = (qi * lam_re - qr * lam_im) / den
    bt_re, bt_im = b_re.transpose(0, 1, 3, 2), b_im.transpose(0, 1, 3, 2)
    bbr = fr[:, :, None, :] * bt_re - fi[:, :, None, :] * bt_im
    bbi = fr[:, :, None, :] * bt_im + fi[:, :, None, :] * bt_re
    pk_r, pk_i = pr[:t, :, :, None, :], pi[:t, :, :, None, :]
    abr = pk_r * bbr[None] - pk_i * bbi[None]
    abi = pk_r * bbi[None] + pk_i * bbr[None]
    kd = (jnp.einsum("rgpn,krgqn->rgkpq", c_re, abr, precision=hp)
          - jnp.einsum("rgpn,krgqn->rgkpq", c_im, abi, precision=hp))
    skip = jnp.eye(p, dtype=F32)[None] * d_skip.reshape(g, p)[:, :, None]
    diag = kd[0][:, 0] + kd[1][:, 0] + skip
    qd = jnp.concatenate([jnp.flip(kd[1][:, 1:], axis=1), diag[:, None], kd[0][:, 1:]], axis=1)
    toep = jnp.stack([qd[:, t - 1 - s:2 * t - 1 - s] for s in range(t)], axis=1)
    mt = toep.transpose(0, 1, 4, 2, 3).reshape(g, t * p, t * p)
    ab = jnp.concatenate([abr, abi], axis=-1)
    bcc = jnp.stack([jnp.flip(ab[:, 0], axis=0), ab[:, 1]])
    bc = bcc.transpose(2, 1, 3, 0, 4).reshape(g, t * p, 4 * n)
    prf = jnp.stack([pr[1:, 0], jnp.flip(pr[1:, 1], axis=0)])[:, :, :, None, :]
    pif = jnp.stack([pi[1:, 0], jnp.flip(pi[1:, 1], axis=0)])[:, :, :, None, :]
    cr_t = c_re[:, None]
    ci_t = c_im[:, None]
    ccc = jnp.concatenate([cr_t * prf - ci_t * pif, -(cr_t * pif + ci_t * prf)], axis=-1)
    cct = ccc.transpose(2, 1, 3, 0, 4).reshape(g, t * p, 4 * n)

    def pair_lanes(a):
        a5 = a.reshape(g // 2, 2, t * p, 4, n)
        return jnp.einsum("agrcn,gh->agrchn", a5, jnp.eye(2, dtype=F32)).reshape(g, t * p, PAIR_W)

    return mt, pair_lanes(bc), pair_lanes(cct), pr[t], pi[t]


def _scan_coef_g(lam_re, lam_im, log_step):
    g = lam_re.shape[1]
    ms = jnp.array([1, 2, 4, 0, 0, 0, 0, 0] + list(range(1, 9)) + list(range(8, 0, -1)), F32) * CHUNK
    dt = jnp.exp(log_step)[..., None]
    mag = jnp.exp(ms.reshape(-1, 1, 1, 1) * (lam_re * dt)[None])
    ang = ms.reshape(-1, 1, 1, 1) * (lam_im * dt)[None]
    cr, ci = mag * jnp.cos(ang), mag * jnp.sin(ang)
    both = jnp.stack([cr, ci], axis=2).reshape(24, 2, 2, g // 2, 2, S5_N)
    return both.transpose(0, 3, 1, 2, 4, 5).reshape(24, g * ZG_W)


def _s5_small(lam_re, lam_im, log_step, b_re, b_im, c_re, c_im, d_skip):
    g = lam_re.shape[1]
    t, p = CHUNK, S5_P
    dt = jnp.exp(log_step)[..., None]
    ks = jnp.arange(24, dtype=F32).reshape(24, 1, 1, 1)
    mag = jnp.exp(ks * (lam_re * dt)[None])
    ang = ks * (lam_im * dt)[None]
    pr, pi = mag * jnp.cos(ang), mag * jnp.sin(ang)
    ar, ai = pr[1], pi[1]
    qr, qi = ar - 1.0, ai
    den = lam_re * lam_re + lam_im * lam_im
    fr = (qr * lam_re + qi * lam_im) / den
    fi = (qi * lam_re - qr * lam_im) / den
    bt_re, bt_im = b_re.transpose(0, 1, 3, 2), b_im.transpose(0, 1, 3, 2)
    bbr = fr[:, :, None, :] * bt_re - fi[:, :, None, :] * bt_im
    bbi = fr[:, :, None, :] * bt_im + fi[:, :, None, :] * bt_re
    pw = jnp.stack([pr, pi], axis=0).transpose(3, 2, 0, 1, 4)
    bb = jnp.stack([bbr, bbi], axis=0).transpose(2, 1, 0, 3, 4)
    cc = jnp.stack([c_re, c_im], axis=0).transpose(2, 1, 0, 3, 4)
    dmat = jnp.eye(p, dtype=F32)[None] * d_skip.reshape(g, p)[:, :, None]
    return pw, bb, cc, dmat, pr[t], pi[t]


def _pair_cols(r, ri, g2):
    c0 = (r * 2 + ri) * 128 + g2 * S5_N
    return slice(c0, c0 + S5_N)


def _ab_powers(pw_ref, bb_ref, k, r):
    bbr, bbi = bb_ref[k, r, 0], bb_ref[k, r, 1]
    abr, abi = [], []
    for kk in range(CHUNK):
        prk, pik = pw_ref[k, r, 0, kk:kk + 1, :], pw_ref[k, r, 1, kk:kk + 1, :]
        abr.append(prk * bbr - pik * bbi)
        abi.append(prk * bbi + pik * bbr)
    return abr, abi


def _s5_weights_fwd(pw, bb, cc, dmat):
    g = pw.shape[0]
    gb = GROUPS_PER_STEP
    hp = lax.Precision.HIGHEST

    def body(pw_ref, bb_ref, cc_ref, dm_ref, mt_ref, mtt_ref, bc_ref, cct_ref):
        zeros = jnp.zeros((GT_ROWS, S5_N), BF)
        for k in range(gb):
            g2 = k % 2
            kdt = []
            for r in range(2):
                for ri in range(2):
                    bc_ref[k, :, _pair_cols(r, ri, 1 - g2)] = zeros
                    cct_ref[k, :, _pair_cols(r, ri, 1 - g2)] = zeros
                abr, abi = _ab_powers(pw_ref, bb_ref, k, r)
                cr, ci = cc_ref[k, r, 0], cc_ref[k, r, 1]
                for kk in range(CHUNK):
                    s = CHUNK - 1 - kk if r == 0 else kk
                    bc_ref[k, s * S5_P:(s + 1) * S5_P, _pair_cols(r, 0, g2)] = abr[kk].astype(BF)
                    bc_ref[k, s * S5_P:(s + 1) * S5_P, _pair_cols(r, 1, g2)] = abi[kk].astype(BF)
                for t in range(CHUNK):
                    f = t + 1 if r == 0 else CHUNK - t
                    prf, pif = pw_ref[k, r, 0, f:f + 1, :], pw_ref[k, r, 1, f:f + 1, :]
                    cct_ref[k, t * S5_P:(t + 1) * S5_P, _pair_cols(r, 0, g2)] = (cr * prf - ci * pif).astype(BF)
                    cct_ref[k, t * S5_P:(t + 1) * S5_P, _pair_cols(r, 1, g2)] = (-(cr * pif + ci * prf)).astype(BF)
                abr_all, abi_all = jnp.concatenate(abr, axis=0), jnp.concatenate(abi, axis=0)
                nt = (((1,), (1,)), ((), ()))
                kdt.append(lax.dot_general(abr_all, cr, nt, precision=hp, preferred_element_type=F32)
                           - lax.dot_general(abi_all, ci, nt, precision=hp, preferred_element_type=F32))
            blk = lambda a, d: a[d * S5_P:(d + 1) * S5_P]
            pieces = [blk(kdt[1], CHUNK - 1 - i) for i in range(CHUNK - 1)]
            pieces.append(blk(kdt[0], 0) + blk(kdt[1], 0) + dm_ref[k])
            pieces += [blk(kdt[0], d) for d in range(1, CHUNK)]
            qrow = jnp.concatenate(pieces, axis=1)
            mt = jnp.concatenate([qrow[:, (CHUNK - 1 - s) * S5_P:(CHUNK - 1 - s) * S5_P + GT_ROWS] for s in range(CHUNK)], axis=0)
            mt_ref[k] = mt.astype(BF)
            mtt_ref[k] = jnp.transpose(mt).astype(BF)

    small = lambda a: pl.BlockSpec((gb, *a.shape[1:]), lambda i: (i,) + (0,) * (a.ndim - 1))
    return pl.pallas_call(
        body, name="l1_s5_weights", grid=(g // gb,), in_specs=[small(pw), small(bb), small(cc), small(dmat)],
        out_specs=[_w_spec(GT_ROWS), _w_spec(GT_ROWS), _w_spec(PAIR_W), _w_spec(PAIR_W)],
        out_shape=[jax.ShapeDtypeStruct((g, GT_ROWS, GT_ROWS), BF), jax.ShapeDtypeStruct((g, GT_ROWS, GT_ROWS), BF),
                   jax.ShapeDtypeStruct((g, GT_ROWS, PAIR_W), BF), jax.ShapeDtypeStruct((g, GT_ROWS, PAIR_W), BF)],
        compiler_params=_cparams(),
    )(pw, bb, cc, dmat)


def _s5_weights_bwd(pw, bb, cc, d_mt, d_bc, d_cct):
    g = pw.shape[0]
    gb = GROUPS_PER_STEP
    hp = lax.Precision.HIGHEST

    def body(pw_ref, bb_ref, cc_ref, dmt_ref, dbc_ref, dcc_ref, dpw_ref, dbb_ref, dccp_ref, ddm_ref):
        for k in range(gb):
            g2 = k % 2
            dq = None
            for s in range(CHUNK):
                parts = [dmt_ref[k, s * S5_P:(s + 1) * S5_P, :]]
                if s < CHUNK - 1:
                    parts.insert(0, jnp.zeros((S5_P, (CHUNK - 1 - s) * S5_P), F32))
                if s > 0:
                    parts.append(jnp.zeros((S5_P, s * S5_P), F32))
                padded = jnp.concatenate(parts, axis=1) if len(parts) > 1 else parts[0]
                dq = padded if dq is None else dq + padded
            dblk = lambda d: dq[:, (CHUNK - 1 + d) * S5_P:(CHUNK + d) * S5_P]
            ddm_ref[k] = dblk(0)
            dkdt = [jnp.concatenate([dblk(d) for d in range(CHUNK)], axis=0),
                    jnp.concatenate([dblk(-d) for d in range(CHUNK)], axis=0)]
            for r in range(2):
                abr, abi = _ab_powers(pw_ref, bb_ref, k, r)
                bbr, bbi = bb_ref[k, r, 0], bb_ref[k, r, 1]
                cr, ci = cc_ref[k, r, 0], cc_ref[k, r, 1]
                abr_all, abi_all = jnp.concatenate(abr, axis=0), jnp.concatenate(abi, axis=0)
                tn = (((0,), (0,)), ((), ()))
                nn = (((1,), (0,)), ((), ()))
                dcr = lax.dot_general(dkdt[r], abr_all, tn, precision=hp, preferred_element_type=F32)
                dci = -lax.dot_general(dkdt[r], abi_all, tn, precision=hp, preferred_element_type=F32)
                dabr_all = lax.dot_general(dkdt[r], cr, nn, precision=hp, preferred_element_type=F32)
                dabi_all = -lax.dot_general(dkdt[r], ci, nn, precision=hp, preferred_element_type=F32)
                dbbr = jnp.zeros((S5_P, S5_N), F32)
                dbbi = jnp.zeros((S5_P, S5_N), F32)
                dpr = [jnp.zeros((1, S5_N), F32) for _ in range(24)]
                dpi = [jnp.zeros((1, S5_N), F32) for _ in range(24)]
                for kk in range(CHUNK):
                    s = CHUNK - 1 - kk if r == 0 else kk
                    dabr = dabr_all[kk * S5_P:(kk + 1) * S5_P] + dbc_ref[k, s * S5_P:(s + 1) * S5_P, _pair_cols(r, 0, g2)]
                    dabi = dabi_all[kk * S5_P:(kk + 1) * S5_P] + dbc_ref[k, s * S5_P:(s + 1) * S5_P, _pair_cols(r, 1, g2)]
                    prk, pik = pw_ref[k, r, 0, kk:kk + 1, :], pw_ref[k, r, 1, kk:kk + 1, :]
                    dbbr = dbbr + prk * dabr + pik * dabi
                    dbbi = dbbi - pik * dabr + prk * dabi
                    dpr[kk] = dpr[kk] + jnp.sum(dabr * bbr + dabi * bbi, axis=0, keepdims=True)
                    dpi[kk] = dpi[kk] + jnp.sum(dabi * bbr - dabr * bbi, axis=0, keepdims=True)
                for t in range(CHUNK):
                    f = t + 1 if r == 0 else CHUNK - t
                    prf, pif = pw_ref[k, r, 0, f:f + 1, :], pw_ref[k, r, 1, f:f + 1, :]
                    d_re = dcc_ref[k, t * S5_P:(t + 1) * S5_P, _pair_cols(r, 0, g2)]
                    d_im = dcc_ref[k, t * S5_P:(t + 1) * S5_P, _pair_cols(r, 1, g2)]
                    dcr = dcr + d_re * prf - d_im * pif
                    dci = dci - d_re * pif - d_im * prf
                    dpr[f] = dpr[f] + jnp.sum(d_re * cr - d_im * ci, axis=0, keepdims=True)
                    dpi[f] = dpi[f] - jnp.sum(d_re * ci + d_im * cr, axis=0, keepdims=True)
                dbb_ref[k, r, 0] = dbbr
                dbb_ref[k, r, 1] = dbbi
                dccp_ref[k, r, 0] = dcr
                dccp_ref[k, r, 1] = dci
                dpw_ref[k, r, 0] = jnp.concatenate(dpr, axis=0)
                dpw_ref[k, r, 1] = jnp.concatenate(dpi, axis=0)

    small = lambda a: pl.BlockSpec((gb, *a.shape[1:]), lambda i: (i,) + (0,) * (a.ndim - 1))
    dmat_sds = jax.ShapeDtypeStruct((g, S5_P, S5_P), F32)
    return pl.pallas_call(
        body, name="l1_s5_weights_bwd", grid=(g // gb,),
        in_specs=[small(pw), small(bb), small(cc), _w_spec(GT_ROWS), _w_spec(PAIR_W), _w_spec(PAIR_W)],
        out_specs=[small(pw), small(bb), small(cc), small(dmat_sds)],
        out_shape=[jax.ShapeDtypeStruct(pw.shape, F32), jax.ShapeDtypeStruct(bb.shape, F32),
                   jax.ShapeDtypeStruct(cc.shape, F32), dmat_sds],
        compiler_params=_cparams(),
    )(pw, bb, cc, d_mt, d_bc, d_cct)


def _s5_compact(lam_re, lam_im, log_step, b_re, b_im, c_re, c_im, d_skip):
    hp = lax.Precision.HIGHEST
    g = lam_re.shape[1]
    nb = g // GROUPS_PER_BLOCK
    t, p, n = CHUNK, S5_P, S5_N
    dt = jnp.exp(log_step)[..., None]
    ks = jnp.arange(t + 1, dtype=F32).reshape(t + 1, 1, 1, 1)
    mag = jnp.exp(ks * (lam_re * dt)[None])
    ang = ks * (lam_im * dt)[None]
    pr, pi = mag * jnp.cos(ang), mag * jnp.sin(ang)
    ar, ai = pr[1], pi[1]
    qr, qi = ar - 1.0, ai
    den = lam_re * lam_re + lam_im * lam_im
    fr = (qr * lam_re + qi * lam_im) / den
    fi = (qi * lam_re - qr * lam_im) / den
    bt_re, bt_im = b_re.transpose(0, 1, 3, 2), b_im.transpose(0, 1, 3, 2)
    bbr = fr[:, :, None, :] * bt_re - fi[:, :, None, :] * bt_im
    bbi = fr[:, :, None, :] * bt_im + fi[:, :, None, :] * bt_re
    pk_r, pk_i = pr[:t, :, :, None, :], pi[:t, :, :, None, :]
    abr = pk_r * bbr[None] - pk_i * bbi[None]
    abi = pk_r * bbi[None] + pk_i * bbr[None]
    kd = (jnp.einsum("rgpn,krgqn->rgkpq", c_re, abr, precision=hp)
          - jnp.einsum("rgpn,krgqn->rgkpq", c_im, abi, precision=hp))
    skip = jnp.eye(p, dtype=F32)[None] * d_skip.reshape(g, p)[:, :, None]
    diag = kd[0][:, 0] + kd[1][:, 0] + skip
    qd = jnp.concatenate([jnp.flip(kd[1][:, 1:], axis=1), diag[:, None], kd[0][:, 1:]], axis=1)
    nd = 2 * t - 1
    wc = qd.transpose(0, 1, 3, 2).reshape(nb, GROUPS_PER_BLOCK, nd, p, p).transpose(0, 2, 1, 3, 4)
    wcomp = wc.reshape(nb, nd, LANE_BLOCK, p)
    ab = jnp.concatenate([abr, abi], axis=-1)
    bcc = jnp.stack([jnp.flip(ab[:, 0], axis=0), ab[:, 1]])
    bcomp = bcc.reshape(2, t, nb, LANE_BLOCK, 2 * n)
    prf = jnp.stack([pr[1:, 0], jnp.flip(pr[1:, 1], axis=0)])[:, :, :, None, :]
    pif = jnp.stack([pi[1:, 0], jnp.flip(pi[1:, 1], axis=0)])[:, :, :, None, :]
    cr_t = c_re[:, None]
    ci_t = c_im[:, None]
    ccc = jnp.concatenate([cr_t * prf - ci_t * pif, -(cr_t * pif + ci_t * prf)], axis=-1)
    ccomp = ccc.reshape(2, t, nb, LANE_BLOCK, 2 * n)
    return wcomp, bcomp, ccomp, pr[t], pi[t]


def _scan_coef(lam_re, lam_im, log_step):
    g = lam_re.shape[1]
    nb = g // GROUPS_PER_BLOCK
    ms = jnp.array([1, 2, 4, 0, 0, 0, 0, 0] + list(range(1, 9)) + list(range(8, 0, -1)), F32) * CHUNK
    dt = jnp.exp(log_step)[..., None]
    mag = jnp.exp(ms.reshape(-1, 1, 1, 1) * (lam_re * dt)[None])
    ang = ms.reshape(-1, 1, 1, 1) * (lam_im * dt)[None]
    cr, ci = mag * jnp.cos(ang), mag * jnp.sin(ang)
    lay = lambda a: a.reshape(24, 2, nb, ZH).transpose(2, 1, 0, 3)
    return jnp.concatenate([lay(cr), lay(ci)], axis=-1)


def _to_cr(a):
    return a.reshape(a.shape[0] // CHUNK, CHUNK * a.shape[1])


def _from_cr(a, c):
    return a.reshape(a.shape[0] * CHUNK, c)


def _pad8(v):
    return jnp.concatenate([v, jnp.zeros((8 - v.shape[0], v.shape[1]), v.dtype)], axis=0)


def _local_step(x, c, ctx, c_ctx, loss_target, w, late=None, scatter=False, mod=None):
    l, d = x.shape
    lc = ctx.shape[0]
    tm = min(256, lc)
    assert lc == tm and l % tm == 0 and tm % GRID_W == 0 and (tm & (tm - 1)) == 0
    nl = l // tm

    own_mod = mod is None
    if own_mod:
        c8 = _pad8(jnp.stack([c, c_ctx]))
        mod = _ada_fwd(c8, w["ada_w"], w["ada_b"])
    sh = mod[:, :2, :d]
    sc = mod[:, :2, d:2 * d]
    gt = mod[:, :2, 2 * d:]
    ln_g, ln_b = w["ln_g"], w["ln_b"]

    a0, b0 = 1.0 + sc[0], sh[0]
    xch = _Exchange("gather", [late[n][0] for n in late], [late[n][1] for n in late]) if late else None
    p42, got = _inproj0(x, ctx, a0, b0, w["conv_w_in"], tm, xch)
    if late:
        w = dict(w, **dict(zip(late, got)))
    e = w["conv_w_out"].shape[0]
    half = e // 2
    nb = e // LANE_BLOCK
    tc = min(512, half)
    cw = w["conv_w"].reshape(3, 2, half)
    q3 = _conv_fwd(p42, cw, nl, tm, tc)
    xh1_l, xh1_c, rs1_l, rs1_c, fx = _outproj_ln0(q3, w["conv_w_out"], x, ctx, gt[0], tm)
    jl, jc = l // CHUNK, lc // CHUNK

    g0, bb0 = ln_g[0:1], ln_b[0:1]
    a1 = g0 * (1.0 + sc[1])
    b1 = bb0 * (1.0 + sc[1]) + sh[1]
    wu_t = w["ssm_w_in"][:, :e].T
    w_z = w["ssm_w_in"][:, e:]
    ut_l, z_l = _inproj1_gt(xh1_l, a1[0:1], b1[0:1], wu_t, w_z, "lat")
    ut_c, _ = _inproj1_gt(xh1_c, a1[1:2], b1[1:2], wu_t, w_z, "ctx")
    s5 = (w["ssm_lam_re"], w["ssm_lam_im"], w["ssm_log_step"], w["ssm_b_re"], w["ssm_b_im"],
          w["ssm_c_re"], w["ssm_c_im"], w["ssm_d"])
    (pw, bbw, ccw, dmat, _, _), s5_vjp = jax.vjp(_s5_small, *s5)
    mt_b, mtt_b, bc_b, cct_b = _s5_weights_fwd(pw, bbw, ccw, dmat)
    coef = lax.stop_gradient(_scan_coef_g(*s5[:3]))
    zz_l, zz_c = _s5_z(ut_l, ut_c, bc_b)
    fwd_chains = ((("c", False), ("l", False)), (("c", True), ("l", True)))
    st_l, st_c = _scan_g(zz_l, zz_c, coef, fwd_chains, False, name="l1_scan_fwd")
    yt = _s5_y(ut_l, st_l, mtt_b, cct_b)
    b_glu = w["ssm_b_glu"].reshape(1, e)
    w_cr, sg_cr = _glu_fwd_gt(yt, z_l, w["ssm_w_glu"], b_glu)
    vec_f = _pad8(jnp.concatenate([g0, bb0, gt[1][0:1], ln_g[1:2], ln_b[1:2]], axis=0))
    dr2, acc_f = _final(w_cr, w["ssm_w_out"], xh1_l, _to_cr(loss_target), vec_f)
    loss = jnp.sum(acc_f[3])

    gt1 = gt[1][0:1]
    dz_l, dt_l, dyt = _glu_bwd_gt(dr2, gt1, w["ssm_w_out"], w["ssm_w_glu"], yt, z_l, sg_cr)
    g_w_out = _dw_cr(w_cr, dr2, "cr", "scaled", gt1, False, None, "l1_dw_out")
    g_w_glu, bsum = _dw_gt(yt, dt_l, True, None, True, None, BF, "l1_dw_glu")
    g_b_glu = bsum[0]
    ds_l = _s5_ds(dyt, cct_b)
    bwd_chains = ((("l", True), ("c", True)), (("l", False), ("c", False)))
    dzz_l, dzz_c, da = _scan_g(ds_l, jnp.zeros_like(zz_c), coef, bwd_chains, True, st_l, st_c, name="l1_scan_bwd")
    dut_l, dut_c = _s5_dx(dyt, dzz_l, dzz_c, mt_b, bc_b)
    d_mt, d_bc, d_cct = _s5_dw(ut_l, ut_c, dyt, dzz_l, dzz_c, st_l)
    n_g = e // S5_P
    da = jnp.sum(da, axis=0).reshape(n_g // 2, 2, 2, 2, S5_N).transpose(1, 2, 0, 3, 4)
    da = da.reshape(2, 2, n_g, S5_N)
    d_pw, d_bb, d_ccp, d_dm = _s5_weights_bwd(pw, bbw, ccw, d_mt, d_bc, d_cct)
    g_s5 = s5_vjp((d_pw, d_bb, d_ccp, d_dm, da[:, 0], da[:, 1]))

    vec_l = _pad8(jnp.concatenate([g0, bb0, 1.0 + sc[1][0:1]], axis=0))
    vec_c = _pad8(jnp.concatenate([g0, bb0, 1.0 + sc[1][1:2]], axis=0))
    dr1_l, acc_l = _bwd_inproj1_gt(dut_l, dz_l, wu_t, w_z, xh1_l, rs1_l, dr2, vec_l, "lat")
    dr1_c, acc_c = _bwd_inproj1_gt(dut_c, jnp.zeros((jc, CHUNK * e), BF), wu_t, w_z, xh1_c, rs1_c,
                                   jnp.zeros((jc, CHUNK * d), BF), vec_c, "ctx")
    mod_l = jnp.concatenate([a1[0:1], b1[0:1]], axis=0)
    mod_c = jnp.concatenate([a1[1:2], b1[1:2]], axis=0)
    g_ut_c = _dw_gt(dut_c, xh1_c, False, mod_c, False, None, F32, "l1_dw_in_u_ctx")
    g_ut = _dw_gt(dut_l, xh1_l, False, mod_l, False, g_ut_c, BF, "l1_dw_in_u")
    g_in_z = _dw_cr(xh1_l, dz_l, "mod", "cr", mod_l, False, None, "l1_dw_in_z")
    g_w_in1 = jnp.concatenate([g_ut.T, g_in_z], axis=1)

    dr1_ln, dr1_cn = _from_cr(dr1_l, d), _from_cr(dr1_c, d)
    dq3, acc_g0 = _bwd_outproj0(dr1_ln, dr1_cn, gt[0], w["conv_w_out"], fx, tm)
    sent1 = ["ssm_w_in", "ssm_w_glu", "ssm_w_out"]
    xch1 = _Exchange("scatter", [g_w_in1, g_w_glu, g_w_out], [BIG[n] for n in sent1]) if scatter else None
    dp42, dcw, recv1 = _conv_bwd(dq3, p42, cw, nl, tm, tc, xch1)
    g_w_in0 = _dw_inproj0(x, ctx, a0, b0, dp42, tm)
    g_w_out0 = _dw_outproj0(q3, dr1_ln, dr1_cn, gt[0], tm)
    sent0 = ["conv_w_in", "conv_w_out"]
    xch0 = _Exchange("scatter", [g_w_in0, g_w_out0], [BIG[n] for n in sent0]) if scatter else None
    grad_x, acc_0, recv0 = _bwd_inproj0(dp42, w["conv_w_in"], x, ctx, dr1_ln, dr1_cn, a0, tm, xch0)
    recv = dict(zip(sent1 + sent0, recv1 + recv0))

    zero = jnp.zeros((d,), F32)
    dm0 = jnp.stack([jnp.concatenate([acc_0[2], acc_0[0], acc_g0[0]]), jnp.concatenate([acc_0[3], acc_0[1], acc_g0[1]])])
    dm1 = jnp.stack([jnp.concatenate([acc_l[1], acc_l[0], acc_f[2]]), jnp.concatenate([acc_c[1], acc_c[0], zero])])
    if own_mod:
        g_ada_w, dc8 = _ada_bwd(c8, w["ada_w"], jnp.stack([_pad8(dm0), _pad8(dm1)]), BF)
        g_mod = {"c_ctx": dc8[0, 1] + dc8[1, 1], "ada_w": g_ada_w,
                 "ada_b": jnp.stack([dm0[0] + dm0[1], dm1[0] + dm1[1]])}
    else:
        g_mod = {"mod": jnp.stack([dm0, dm1])}

    grads = {
        **g_mod,
        "ln_g": jnp.stack([acc_l[2] + acc_c[2], acc_f[0]]),
        "ln_b": jnp.stack([acc_l[3] + acc_c[3], acc_f[1]]),
        "conv_w_in": g_w_in0, "conv_w": dcw[:3].reshape(3, e), "conv_w_out": g_w_out0,
        "ssm_w_in": g_w_in1,
        "ssm_lam_re": g_s5[0], "ssm_lam_im": g_s5[1], "ssm_log_step": g_s5[2],
        "ssm_b_re": g_s5[3], "ssm_b_im": g_s5[4], "ssm_c_re": g_s5[5], "ssm_c_im": g_s5[6], "ssm_d": g_s5[7],
        "ssm_w_glu": g_w_glu, "ssm_b_glu": g_b_glu, "ssm_w_out": g_w_out,
    }
    for n in recv:
        del grads[n]
    return loss, grad_x, grads, recv


WEIGHTS = ["c_ctx", "ada_w", "ada_b", "ln_g", "ln_b", "conv_w_in", "conv_w", "conv_w_out", "ssm_w_in",
           "ssm_lam_re", "ssm_lam_im", "ssm_log_step", "ssm_b_re", "ssm_b_im", "ssm_c_re", "ssm_c_im",
           "ssm_d", "ssm_w_glu", "ssm_b_glu", "ssm_w_out"]
BIG = {"ada_w": 1, "conv_w_in": 1, "conv_w_out": 0, "ssm_w_in": 1, "ssm_w_glu": 0, "ssm_w_out": 0}
SMALL_SHARDED = ["conv_w", "ssm_d", "ssm_b_glu"]
REPLICATED = ["c_ctx", "ada_b", "ln_g", "ln_b", "ssm_lam_re", "ssm_lam_im", "ssm_log_step",
              "ssm_b_re", "ssm_b_im", "ssm_c_re", "ssm_c_im"]
NATIVE_SMALL = ["ssm_b_re", "ssm_b_im", "ssm_c_re", "ssm_c_im"]


def _view2d(name, a):
    return a.reshape(-1, a.shape[-1])


def kernel(x, c, ctx, c_ctx, ada_w, ada_b, ln_g, ln_b, conv_w_in, conv_w, conv_w_out, ssm_w_in, ssm_lam_re, ssm_lam_im, ssm_log_step, ssm_b_re, ssm_b_im, ssm_c_re, ssm_c_im, ssm_d, ssm_w_glu, ssm_b_glu, ssm_w_out, loss_target, m_c_ctx, m_ada_w, m_ada_b, m_ln_g, m_ln_b, m_conv_w_in, m_conv_w, m_conv_w_out, m_ssm_w_in, m_ssm_lam_re, m_ssm_lam_im, m_ssm_log_step, m_ssm_b_re, m_ssm_b_im, m_ssm_c_re, m_ssm_c_im, m_ssm_d, m_ssm_w_glu, m_ssm_b_glu, m_ssm_w_out, v_c_ctx, v_ada_w, v_ada_b, v_ln_g, v_ln_b, v_conv_w_in, v_conv_w, v_conv_w_out, v_ssm_w_in, v_ssm_lam_re, v_ssm_lam_im, v_ssm_log_step, v_ssm_b_re, v_ssm_b_im, v_ssm_c_re, v_ssm_c_im, v_ssm_d, v_ssm_w_glu, v_ssm_b_glu, v_ssm_w_out):
    args = locals()
    wt = {n: args[n] for n in WEIGHTS}
    mt = {n: args["m_" + n] for n in WEIGHTS}
    vt = {n: args["v_" + n] for n in WEIGHTS}

    me = 4 * lax.axis_index("x") + 2 * lax.axis_index("y") + lax.axis_index("c")
    d = x.shape[-1]
    d3 = 3 * d
    wa = d3 // N_DEV

    big_names = [n for n in BIG if n != "ada_w"]
    shard = {n: _view2d(n, wt[n]).astype(BF) for n in big_names}
    small = jnp.concatenate([wt["conv_w"][0], wt["ssm_d"], wt["ssm_b_glu"]], axis=0)
    small = jnp.concatenate([small, jnp.zeros((3, small.shape[1]), F32)], axis=0)
    w_in_full, small_full, c_all = _all_gather([shard["conv_w_in"], small, _pad8(c)], [1, 1, 0], "gather_weights")
    late = {n: (shard[n], BIG[n]) for n in big_names if n != "conv_w_in"}
    c16 = jnp.concatenate([c_all[::8], c_ctx[None], jnp.zeros((16 - N_DEV - 1, d), F32)], axis=0)
    ada_w_b = ada_w.astype(BF)
    ada_b_mine = lax.dynamic_slice_in_dim(ada_b, me * wa, wa, axis=1)
    mod_part = _ada_fwd(c16, ada_w_b, ada_b_mine)
    mod_all = _all_gather([mod_part.reshape(32, wa)], [1], "gather_mod")[0].reshape(2, 16, d3)
    mod = jnp.stack([lax.dynamic_index_in_dim(mod_all, me, axis=1, keepdims=False), mod_all[:, N_DEV]], axis=1)
    w = {
        "ln_g": ln_g, "ln_b": ln_b, "conv_w_in": w_in_full, "conv_w": small_full[0:3],
        "ssm_lam_re": ssm_lam_re[0], "ssm_lam_im": ssm_lam_im[0],
        "ssm_log_step": ssm_log_step[0], "ssm_b_re": ssm_b_re[0], "ssm_b_im": ssm_b_im[0],
        "ssm_c_re": ssm_c_re[0], "ssm_c_im": ssm_c_im[0], "ssm_d": small_full[3], "ssm_b_glu": small_full[4],
    }

    loss, grad_x, g, recv_big = _local_step(x[0], c[0], ctx[0], c_ctx, loss_target[0], w, late, True, mod)
    loss = lax.psum(loss, ("x", "y", "c"))

    dmod_all = _all_gather([_pad8(g["mod"].reshape(4, d3))], [0], "gather_dmod")[0].reshape(N_DEV, 8, d3)
    dmod_all = dmod_all[:, :4].reshape(N_DEV, 2, 2, d3)
    dm_ctx = dmod_all[0, :, 1]
    for p in range(1, N_DEV):
        dm_ctx = dm_ctx + dmod_all[p, :, 1]
    dm16 = jnp.concatenate([dmod_all[:, :, 0].transpose(1, 0, 2), dm_ctx[:, None], jnp.zeros((2, 16 - N_DEV - 1, d3), F32)], axis=1)
    g_ada_w, dc16 = _ada_bwd(c16, ada_w_b, lax.dynamic_slice_in_dim(dm16, me * wa, wa, axis=2), F32)
    g["c_ctx"] = dc16[0, N_DEV] + dc16[1, N_DEV]
    g_ada_b = jnp.sum(dm16, axis=1)

    blob_names = [n for n in REPLICATED if n != "ada_b"] + SMALL_SHARDED
    flat = jnp.concatenate([g[n].reshape(-1).astype(F32) for n in blob_names])
    nflat = flat.shape[0]
    rows = -(-nflat // (N_DEV * 128 * 8)) * 8
    flat = jnp.concatenate([flat, jnp.zeros((N_DEV * rows * 128 - nflat,), F32)]).reshape(N_DEV * rows, 128)
    blob_sum = _sum_partials(_all_to_all([flat], [0], "scatter_grads")[0])
    blob = _all_gather([blob_sum], [0], "gather_small_grads")[0].reshape(-1)
    small_g, off = {"ada_b": g_ada_b}, 0
    for n in blob_names:
        shape = wt[n].shape if n in REPLICATED else (*wt[n].shape[:-1], wt[n].shape[-1] * N_DEV)
        size = math.prod(shape)
        small_g[n] = blob[off:off + size].reshape(shape)
        off += size
    for n in SMALL_SHARDED:
        size = wt[n].shape[-1]
        small_g[n] = lax.dynamic_slice_in_dim(small_g[n], me * size, size, axis=small_g[n].ndim - 1)

    out_g, out_d, out_m, out_v = {}, {}, {}, {}
    recv_big["ada_w"] = _view2d("ada_w", g_ada_w)[None]
    for n in BIG:
        stack = recv_big[n]
        shp = wt[n].shape
        res = _adamw(stack, _view2d(n, wt[n]), _view2d(n, mt[n]), _view2d(n, vt[n]), "adamw_" + n)
        out_g[n], out_d[n], out_m[n], out_v[n] = [r.reshape(shp) for r in res]
    for n in NATIVE_SMALL:
        shp = wt[n].shape
        v2 = lambda a: a.reshape(-1, shp[-1])
        res = _adamw(v2(small_g.pop(n))[None], v2(wt[n]), v2(mt[n]), v2(vt[n]), "adamw_" + n)
        out_g[n], out_d[n], out_m[n], out_v[n] = [r.reshape(shp) for r in res]
    names = list(small_g)
    cat = lambda t: jnp.concatenate([t[n].reshape(-1) for n in names])
    gs, ws, ms, vs = cat(small_g), cat(wt), cat(mt), cat(vt)
    ns = gs.shape[0]
    rs = -(-ns // (128 * 512)) * 512
    padr = lambda a: jnp.concatenate([a, jnp.ones((rs * 128 - ns,), F32)]).reshape(rs, 128)
    res = _adamw(padr(gs)[None], padr(ws), padr(ms), padr(vs), "adamw_small")
    off = 0
    for n in names:
        size = math.prod(wt[n].shape)
        out_g[n], out_d[n], out_m[n], out_v[n] = [r.reshape(-1)[off:off + size].reshape(wt[n].shape) for r in res]
        off += size

    return (loss, grad_x[None], *[out_g[n] for n in WEIGHTS], *[out_d[n] for n in WEIGHTS],
            *[out_m[n] for n in WEIGHTS], *[out_v[n] for n in WEIGHTS])
```

```python
import math

import jax
import jax.numpy as jnp
from jax import lax
from jax.experimental import pallas as pl
from jax.experimental.pallas import tpu as pltpu

F32 = jnp.float32
BF = jnp.bfloat16
MESH = pl.DeviceIdType.MESH
N_DEV = 8

GRID_W = 64
CHUNK = 16
S5_P = 16
S5_N = 64
LANE_BLOCK = 128
GROUPS_PER_BLOCK = LANE_BLOCK // S5_P
BCR_W = CHUNK * LANE_BLOCK
ZL_W = 2 * 2 * GROUPS_PER_BLOCK * S5_N
ZH = ZL_W // 4
LN_EPS = 1e-5
DN_ALPHA = 4.0 ** 0.25
ADAM_LR, ADAM_B1, ADAM_B2, ADAM_EPS, ADAM_WD, ADAM_STEP = 1e-3, 0.9, 0.999, 1e-8, 0.01, 10
GELU_C0 = math.sqrt(2.0 / math.pi)
GELU_C1 = 0.044715
VMEM_MB = 52

ANY = pl.BlockSpec(memory_space=pl.ANY)


def _cparams():
    return pltpu.CompilerParams(vmem_limit_bytes=VMEM_MB << 20)


def _dot(a, b):
    return jnp.dot(a, b, preferred_element_type=F32)


def _dot_nt(a, b):
    return lax.dot_general(a, b, (((1,), (1,)), ((), ())), preferred_element_type=F32)


def _dot_tn(a, b):
    return lax.dot_general(a, b, (((0,), (0,)), ((), ())), preferred_element_type=F32)


def _sigmoid(x):
    return 1.0 / (1.0 + jnp.exp(-x))


def _gelu_parts(y):
    th = jnp.tanh(GELU_C0 * (y + GELU_C1 * y * y * y))
    g = 0.5 * y * (1.0 + th)
    dg = 0.5 * (1.0 + th) + 0.5 * y * (1.0 - th * th) * GELU_C0 * (1.0 + 3.0 * GELU_C1 * y * y)
    return g, dg


def _full(shape):
    nd = len(shape)
    return pl.BlockSpec(shape, lambda *_: (0,) * nd)


def _mesh_pos():
    x, y, c = lax.axis_index("x"), lax.axis_index("y"), lax.axis_index("c")
    return x, y, c


def _peer(pos, k):
    x, y, c = pos
    px = 1 - x if (k >> 2) & 1 else x
    py = 1 - y if (k >> 1) & 1 else y
    pc = 1 - c if k & 1 else c
    return (px, py, pc), 4 * px + 2 * py + pc


def _shard_at(ref, axis, idx, n):
    if axis == 0:
        return ref.at[pl.ds(idx * n, n)]
    return ref.at[:, pl.ds(idx * n, n)]


class _Exchange:
    def __init__(self, kind, arrays, axes):
        self.kind, self.axes, self.n = kind, list(axes), len(arrays)
        self.arrays = list(arrays)
        self.out_shape = []
        for s, ax in zip(arrays, axes):
            shp = list(s.shape)
            if kind == "scatter":
                shp[ax] //= N_DEV
                self.out_shape.append(jax.ShapeDtypeStruct((N_DEV, *shp), s.dtype))
            else:
                shp[ax] *= N_DEV
                self.out_shape.append(jax.ShapeDtypeStruct(tuple(shp), s.dtype))
        self.scratch = [pltpu.SemaphoreType.DMA((self.n, N_DEV - 1)), pltpu.SemaphoreType.DMA((self.n, N_DEV - 1)),
                        pltpu.SemaphoreType.DMA((self.n,))]

    def _copies(self, ins, outs, sems):
        send_sems, recv_sems, local_sems = sems
        pos = _mesh_pos()
        x, y, c = pos
        me = 4 * x + 2 * y + c
        local, sends, chained, recvs = [], [], [], []
        for i in range(self.n):
            ax = self.axes[i]
            if self.kind == "scatter":
                size = ins[i].shape[ax] // N_DEV
                src = lambda idx, i=i, ax=ax, size=size: _shard_at(ins[i], ax, idx, size)
                dst = lambda idx, i=i: outs[i].at[idx]
            else:
                size = ins[i].shape[ax]
                src = lambda idx, i=i: ins[i]
                dst = lambda idx, i=i, ax=ax, size=size: _shard_at(outs[i], ax, idx, size)

            def copy(k, s, d, to, i=i):
                return pltpu.make_async_remote_copy(src_ref=s, dst_ref=d, send_sem=send_sems.at[i, k],
                                                    recv_sem=recv_sems.at[i, k], device_id=to, device_id_type=MESH)

            local.append(pltpu.make_async_copy(src(me), dst(me), local_sems.at[i]))
            if self.kind == "gather2":
                sib, sib_i = (x, y, 1 - c), 4 * x + 2 * y + (1 - c)
                chips = [(1 - x, y), (x, 1 - y), (1 - x, 1 - y)]
                sends.append(copy(0, src(me), dst(me), sib))
                recvs.append(copy(0, src(me), dst(sib_i), sib))
                for j, (cx, cy) in enumerate(chips):
                    same, other = 4 * cx + 2 * cy + c, 4 * cx + 2 * cy + (1 - c)
                    sends.append(copy(1 + j, src(me), dst(me), (cx, cy, c)))
                    chained.append((copy(1 + j, dst(same), dst(same), (cx, cy, c)), copy(4 + j, dst(same), dst(same), sib)))
                    recvs.append(copy(4 + j, dst(other), dst(other), sib))
            else:
                for k in range(1, N_DEV):
                    peer, pidx = _peer(pos, k)
                    out_src = src(pidx) if self.kind == "scatter" else src(me)
                    sends.append(copy(k - 1, out_src, dst(me), peer))
                    recvs.append(copy(k - 1, out_src, dst(pidx), peer))
        return local, sends, chained, recvs

    def start(self, ins, outs, sems):
        local, sends, _, _ = self._copies(ins, outs, sems)
        for cp in local + sends:
            cp.start()

    def wait(self, ins, outs, sems):
        local, sends, chained, recvs = self._copies(ins, outs, sems)
        for arrival, released in chained:
            arrival.wait_recv()
            released.start()
        for cp in recvs:
            cp.wait_recv()
        for cp in sends + [released for _, released in chained]:
            cp.wait_send()
        for cp in local:
            cp.wait()

    def run(self, name):
        n = self.n

        def body(*refs):
            ins, outs, sems = refs[:n], refs[n:2 * n], refs[2 * n:]
            self.start(ins, outs, sems)
            self.wait(ins, outs, sems)

        return pl.pallas_call(body, name=name, out_shape=self.out_shape, in_specs=[ANY] * n, out_specs=[ANY] * n,
                              scratch_shapes=self.scratch)(*self.arrays)


def _hosted_call(body, xch, grid, in_specs, out_specs, out_shape, scratch, args, name):
    out_specs, out_shape = list(out_specs), list(out_shape)
    n_in, n_out = len(in_specs), len(out_specs)
    if xch is None:
        res = pl.pallas_call(body, name=name, grid=grid, in_specs=in_specs, out_specs=out_specs, out_shape=out_shape,
                             scratch_shapes=list(scratch), compiler_params=_cparams())(*args)
        return list(res), []
    n = xch.n
    rank = len(grid)

    def wrapped(*refs):
        ins, x_ins = refs[:n_in], refs[n_in:n_in + n]
        outs = refs[n_in + n:n_in + n + n_out]
        x_outs = refs[n_in + n + n_out:n_in + 2 * n + n_out]
        rest = refs[n_in + 2 * n + n_out:]
        own, sems = rest[:len(rest) - 3], rest[len(rest) - 3:]
        ids = [pl.program_id(a) for a in range(rank)]
        first, last = ids[0] == 0, ids[0] == grid[0] - 1
        for a in range(1, rank):
            first = jnp.logical_and(first, ids[a] == 0)
            last = jnp.logical_and(last, ids[a] == grid[a] - 1)

        @pl.when(first)
        def _():
            xch.start(x_ins, x_outs, sems)

        body(*ins, *outs, *own)

        @pl.when(last)
        def _():
            xch.wait(x_ins, x_outs, sems)

    res = pl.pallas_call(
        wrapped, name=name, grid=grid, in_specs=list(in_specs) + [ANY] * n, out_specs=out_specs + [ANY] * n,
        out_shape=out_shape + xch.out_shape, scratch_shapes=list(scratch) + xch.scratch, compiler_params=_cparams(),
    )(*args, *xch.arrays)
    return list(res[:n_out]), list(res[n_out:])


def _all_gather(shards, axes, name, kind="gather"):
    return _Exchange(kind, shards, axes).run(name)


def _all_to_all(parts, axes, name):
    return _Exchange("scatter", parts, axes).run(name)


def _ada_fwd(cv, ada_w, ada_b):
    nl, d, wd = ada_w.shape
    r = cv.shape[0]

    def body(c_ref, w_ref, b_ref, o_ref):
        c = c_ref[...]
        s = (c * _sigmoid(c)).astype(BF)
        o_ref[0] = _dot(s, w_ref[0]) + b_ref[0]

    return pl.pallas_call(
        body, name="ada_fwd", grid=(nl,),
        in_specs=[_full((r, d)), pl.BlockSpec((1, d, wd), lambda l: (l, 0, 0)), pl.BlockSpec((1, 1, wd), lambda l: (l, 0, 0))],
        out_specs=pl.BlockSpec((1, r, wd), lambda l: (l, 0, 0)),
        out_shape=jax.ShapeDtypeStruct((nl, r, wd), F32), compiler_params=_cparams(),
    )(cv, ada_w, ada_b.reshape(nl, 1, wd))


def _ada_bwd(cv, ada_w, dm, out_dtype):
    nl, d, wd = ada_w.shape
    r = cv.shape[0]

    def body(c_ref, w_ref, dm_ref, dw_ref, dc_ref):
        c = c_ref[...]
        sg = _sigmoid(c)
        s = (c * sg).astype(BF)
        dmv = dm_ref[0].astype(BF)
        dw_ref[0] = _dot_tn(s, dmv).astype(out_dtype)
        dc_ref[0] = _dot_nt(dmv, w_ref[0]) * (sg * (1.0 + c * (1.0 - sg)))

    return pl.pallas_call(
        body, name="ada_bwd", grid=(nl,),
        in_specs=[_full((r, d)), pl.BlockSpec((1, d, wd), lambda l: (l, 0, 0)), pl.BlockSpec((1, r, wd), lambda l: (l, 0, 0))],
        out_specs=[pl.BlockSpec((1, d, wd), lambda l: (l, 0, 0)), pl.BlockSpec((1, r, d), lambda l: (l, 0, 0))],
        out_shape=[jax.ShapeDtypeStruct((nl, d, wd), out_dtype), jax.ShapeDtypeStruct((nl, r, d), F32)],
        compiler_params=_cparams(),
    )(cv, ada_w, dm)


def _sum_partials(stack):
    _, r, c = stack.shape

    def body(s_ref, o_ref):
        acc = s_ref[0]
        for p in range(1, N_DEV):
            acc = acc + s_ref[p]
        o_ref[...] = acc

    return pl.pallas_call(body, name="sum_partials", out_shape=jax.ShapeDtypeStruct((r, c), F32),
                          in_specs=[_full(stack.shape)], out_specs=_full((r, c)), grid=(1,),
                          compiler_params=_cparams())(stack)


def _adamw(gstack, w, m, v, name):
    p, r, c = gstack.shape
    tr = r
    for cand in (512 if c <= 256 else 256, 128, 64, 32, 16, 8):
        if r % cand == 0 and r > cand:
            tr = cand
            break
    bc1 = 1.0 - ADAM_B1 ** ADAM_STEP
    bc2 = 1.0 - ADAM_B2 ** ADAM_STEP

    def body(g_ref, w_ref, m_ref, v_ref, go_ref, d_ref, mo_ref, vo_ref):
        g = g_ref[0].astype(F32)
        for q in range(1, p):
            g = g + g_ref[q].astype(F32)
        mn = ADAM_B1 * m_ref[...] + (1.0 - ADAM_B1) * g
        vn = ADAM_B2 * v_ref[...] + (1.0 - ADAM_B2) * (g * g)
        go_ref[...] = g
        mo_ref[...] = mn
        vo_ref[...] = vn
        d_ref[...] = -ADAM_LR * ((mn / bc1) / (jnp.sqrt(vn / bc2) + ADAM_EPS) + ADAM_WD * w_ref[...])

    row = pl.BlockSpec((tr, c), lambda i: (i, 0))
    sds = jax.ShapeDtypeStruct((r, c), F32)
    return pl.pallas_call(
        body, name=name, grid=(r // tr,),
        in_specs=[pl.BlockSpec((p, tr, c), lambda i: (0, i, 0)), row, row, row],
        out_specs=[row, row, row, row], out_shape=[sds, sds, sds, sds], compiler_params=_cparams(),
    )(gstack, w, m, v)


def _lat_or_ctx_specs(tm, d, nl, grid_rank, row_axis):
    def lat(*ids):
        return (jnp.minimum(ids[row_axis], nl - 1), 0)

    def ctx(*ids):
        return (jnp.maximum(ids[row_axis] - nl, 0), 0)

    return pl.BlockSpec((tm, d), lat), pl.BlockSpec((tm, d), ctx)


def _sel_row(ref, is_ctx):
    return jnp.where(is_ctx, ref[1:2, :], ref[0:1, :])


def _inproj0(x, ctx, a2, b2, w, tm, xch=None):
    l, d = x.shape
    nl, nc = l // tm, ctx.shape[0] // tm
    e = w.shape[1] // 4
    half = e // 2

    def body(x_ref, c_ref, a_ref, b_ref, w_hbm, o_ref, w_ref):
        i = pl.program_id(0)

        @pl.when(i == 0)
        def _():
            pltpu.sync_copy(w_hbm, w_ref)

        is_ctx = i >= nl
        xv = jnp.where(is_ctx, c_ref[...], x_ref[...])
        h = (xv * _sel_row(a_ref, is_ctx) + _sel_row(b_ref, is_ctx)).astype(BF)
        for k in range(4):
            r = _dot(h, w_ref[:, k * e:(k + 1) * e])
            o_ref[k, 0] = r[:, :half].astype(BF)
            o_ref[k, 1] = r[:, half:].astype(BF)

    lat, cx = _lat_or_ctx_specs(tm, d, nl, 1, 0)
    (p42,), extra = _hosted_call(
        body, xch, grid=(nl + nc,),
        in_specs=[lat, cx, _full((2, d)), _full((2, d)), ANY],
        out_specs=[pl.BlockSpec((4, 2, tm, half), lambda i: (0, 0, i, 0))],
        out_shape=[jax.ShapeDtypeStruct((4, 2, l + ctx.shape[0], half), BF)],
        scratch=[pltpu.VMEM(w.shape, BF)], args=(x, ctx, a2, b2, w), name="l0_inproj")
    return p42, extra


def _conv_taps(u, w_up, w_mid, w_dn, pos, rl, tm):
    up = jnp.where(pos == 0, 0.0, pltpu.roll(u, 1, 0))
    dn = jnp.where(pos == rl - 1, 0.0, pltpu.roll(u, tm - 1, 0))
    return w_up * up + w_mid * u + w_dn * dn, up, dn


def _conv_halo_specs(tm, tc, nl, lead):
    hb = tm // GRID_W

    def prev(j, i):
        return (0, 1, jnp.maximum(jnp.minimum(i, nl - 1) * hb - 1, 0), j)

    def nxt(j, i):
        return (0, 1, jnp.minimum((jnp.minimum(i, nl - 1) + 1) * hb, nl * hb - 1), j)

    return pl.BlockSpec((lead, 1, GRID_W, tc), prev), pl.BlockSpec((lead, 1, GRID_W, tc), nxt)


def _conv_fwd(p42, cw, nl, tm, tc):
    _, _, r, half = p42.shape
    nt = r // tm

    def body(p_ref, hp_ref, hn_ref, cw_ref, o_ref):
        i = pl.program_id(1)
        is_ctx = i >= nl
        row = lax.broadcasted_iota(jnp.int32, (tm, tc), 0)
        rl = jnp.where(is_ctx, tm, GRID_W)
        pos = jnp.bitwise_and(row, rl - 1)

        def gate(hv, yc):
            bg = p_ref[0, hv].astype(F32)
            z = p_ref[3, hv].astype(F32)
            return (bg * yc * (z * _sigmoid(z))).astype(BF)

        u_h = p_ref[1, 0].astype(F32) * p_ref[2, 0].astype(F32)
        w_h = cw_ref[:, 0, :]
        o_ref[0] = gate(0, _conv_taps(u_h, w_h[0:1], w_h[1:2], w_h[2:3], pos, rl, tm)[0])
        u_v = p_ref[1, 1].astype(F32) * p_ref[2, 1].astype(F32)
        w_v = cw_ref[:, 1, :]

        @pl.when(is_ctx)
        def _():
            o_ref[1] = gate(1, _conv_taps(u_v, w_v[0:1], w_v[1:2], w_v[2:3], pos, rl, tm)[0])

        @pl.when(jnp.logical_not(is_ctx))
        def _():
            up = hp_ref[1, 0].astype(F32) * hp_ref[2, 0].astype(F32) * (i > 0).astype(F32)
            dn = hn_ref[1, 0].astype(F32) * hn_ref[2, 0].astype(F32) * (i < nl - 1).astype(F32)
            ext = jnp.concatenate([up, u_v, dn], axis=0)
            yc = w_v[0:1] * ext[0:tm] + w_v[1:2] * u_v + w_v[2:3] * ext[2 * GRID_W:tm + 2 * GRID_W]
            o_ref[1] = gate(1, yc)

    hp, hn = _conv_halo_specs(tm, tc, nl, 4)
    return pl.pallas_call(
        body, name="l0_conv_fwd", grid=(half // tc, nt),
        in_specs=[pl.BlockSpec((4, 2, tm, tc), lambda j, i: (0, 0, i, j)), hp, hn,
                  pl.BlockSpec((3, 2, tc), lambda j, i: (0, 0, j))],
        out_specs=pl.BlockSpec((2, tm, tc), lambda j, i: (0, i, j)),
        out_shape=jax.ShapeDtypeStruct((2, r, half), BF), compiler_params=_cparams(),
    )(p42, p42, p42, cw)


def _outproj_ln0(q3, w_out, x, ctx, gt2, tm):
    l, d = x.shape
    lc = ctx.shape[0]
    nl, nc = l // tm, lc // tm
    _, r, half = q3.shape
    tjo = tm // CHUNK

    def body(q_ref, w_hbm, x_ref, c_ref, g_ref, xl_ref, xc_ref, rl_ref, rc_ref, fx_ref, w_ref, xs_ref, rs_ref):
        i = pl.program_id(0)

        @pl.when(i == 0)
        def _():
            pltpu.sync_copy(w_hbm, w_ref)

        is_ctx = i >= nl
        fx = _dot(q_ref[0], w_ref[:half, :]) + _dot(q_ref[1], w_ref[half:, :])
        xv = jnp.where(is_ctx, c_ref[...], x_ref[...])
        rr = DN_ALPHA * xv + _sel_row(g_ref, is_ctx) * fx
        mu = jnp.mean(rr, axis=-1, keepdims=True)
        cen = rr - mu
        rstd = lax.rsqrt(jnp.mean(cen * cen, axis=-1, keepdims=True) + LN_EPS)
        xh = cen * rstd
        for lb in range(d // 128):
            xs_ref[lb] = xh[:, lb * 128:(lb + 1) * 128]
        rs_ref[...] = jnp.broadcast_to(rstd, (tm, 128))
        fx_ref[...] = fx.astype(BF)

        def to_cr(xo_ref, ro_ref):
            for s in range(CHUNK):
                for lb in range(d // 128):
                    xo_ref[:, s * d + lb * 128:s * d + (lb + 1) * 128] = xs_ref.at[lb][pl.ds(s, tjo, stride=CHUNK), :]
                ro_ref[:, s * 128:(s + 1) * 128] = rs_ref[pl.ds(s, tjo, stride=CHUNK), :]

        @pl.when(jnp.logical_not(is_ctx))
        def _():
            to_cr(xl_ref, rl_ref)

        @pl.when(is_ctx)
        def _():
            to_cr(xc_ref, rc_ref)

    lat, cx = _lat_or_ctx_specs(tm, d, nl, 1, 0)
    lat_o = lambda w_: pl.BlockSpec((tjo, CHUNK * w_), lambda i: (jnp.minimum(i, nl - 1), 0))
    ctx_o = lambda w_: pl.BlockSpec((tjo, CHUNK * w_), lambda i: (jnp.maximum(i - nl, 0), 0))
    return pl.pallas_call(
        body, name="l0_outproj_ln", grid=(nl + nc,),
        in_specs=[pl.BlockSpec((2, tm, half), lambda i: (0, i, 0)), ANY, lat, cx, _full((2, d))],
        out_specs=[lat_o(d), ctx_o(d), lat_o(128), ctx_o(128), pl.BlockSpec((tm, d), lambda i: (i, 0))],
        out_shape=[jax.ShapeDtypeStruct((l // CHUNK, CHUNK * d), F32), jax.ShapeDtypeStruct((lc // CHUNK, CHUNK * d), F32),
                   jax.ShapeDtypeStruct((l // CHUNK, CHUNK * 128), F32), jax.ShapeDtypeStruct((lc // CHUNK, CHUNK * 128), F32),
                   jax.ShapeDtypeStruct((r, d), BF)],
        scratch_shapes=[pltpu.VMEM(w_out.shape, BF), pltpu.VMEM((d // 128, tm, 128), F32), pltpu.VMEM((tm, 128), F32)],
        compiler_params=_cparams(),
    )(q3, w_out, x, ctx, gt2)


def _bwd_outproj0(dr_l, dr_c, gt2, w_out, fx, tm):
    l, d = dr_l.shape
    nl, nc = l // tm, dr_c.shape[0] // tm
    e = w_out.shape[0]
    half = e // 2
    r = l + dr_c.shape[0]

    def body(dl_ref, dc_ref, g_ref, w_hbm, fx_ref, dq_ref, acc_ref, w_ref):
        i = pl.program_id(0)

        @pl.when(i == 0)
        def _():
            pltpu.sync_copy(w_hbm, w_ref)
            acc_ref[...] = jnp.zeros_like(acc_ref)

        is_ctx = i >= nl
        dr = jnp.where(is_ctx, dc_ref[...], dl_ref[...]).astype(F32)
        dfx = (dr * _sel_row(g_ref, is_ctx)).astype(BF)
        dq_ref[0] = _dot_nt(dfx, w_ref[:half, :]).astype(BF)
        dq_ref[1] = _dot_nt(dfx, w_ref[half:, :]).astype(BF)
        s = jnp.sum(dr * fx_ref[...].astype(F32), axis=0, keepdims=True)
        sel = is_ctx.astype(F32)
        acc_ref[0:1, :] += s * (1.0 - sel)
        acc_ref[1:2, :] += s * sel

    lat, cx = _lat_or_ctx_specs(tm, d, nl, 1, 0)
    return pl.pallas_call(
        body, name="l0_bwd_outproj", grid=(nl + nc,),
        in_specs=[lat, cx, _full((2, d)), ANY, pl.BlockSpec((tm, d), lambda i: (i, 0))],
        out_specs=[pl.BlockSpec((2, tm, half), lambda i: (0, i, 0)), _full((8, d))],
        out_shape=[jax.ShapeDtypeStruct((2, r, half), BF), jax.ShapeDtypeStruct((8, d), F32)],
        scratch_shapes=[pltpu.VMEM(w_out.shape, BF)], compiler_params=_cparams(),
    )(dr_l, dr_c, gt2, w_out, fx)


def _conv_bwd(dq3, p42, cw, nl, tm, tc, xch=None):
    _, _, r, half = p42.shape
    nt = r // tm

    def body(dq_ref, dqp_ref, dqn_ref, p_ref, hp_ref, hn_ref, cw_ref, dp_ref, dw_ref):
        i = pl.program_id(1)
        is_ctx = i >= nl

        @pl.when(i == 0)
        def _():
            dw_ref[...] = jnp.zeros_like(dw_ref)

        row = lax.broadcasted_iota(jnp.int32, (tm, tc), 0)
        rl = jnp.where(is_ctx, tm, GRID_W)
        pos = jnp.bitwise_and(row, rl - 1)

        def pieces(dq, bg, z):
            sz = _sigmoid(z)
            sil = z * sz
            return dq * bg * sil, dq * sil, dq * bg * (sz * (1.0 + z * (1.0 - sz)))

        def seq_half(hv):
            bg, cg = p_ref[0, hv].astype(F32), p_ref[1, hv].astype(F32)
            v, z = p_ref[2, hv].astype(F32), p_ref[3, hv].astype(F32)
            w = cw_ref[:, hv, :]
            u = cg * v
            yc, u_up, u_dn = _conv_taps(u, w[0:1], w[1:2], w[2:3], pos, rl, tm)
            dyc, dbg_f, dz_f = pieces(dq_ref[hv].astype(F32), bg, z)
            du = _conv_taps(dyc, w[2:3], w[1:2], w[0:1], pos, rl, tm)[0]
            dp_ref[0, hv] = (dbg_f * yc).astype(BF)
            dp_ref[1, hv] = (du * v).astype(BF)
            dp_ref[2, hv] = (du * cg).astype(BF)
            dp_ref[3, hv] = (dz_f * yc).astype(BF)
            dw_ref[0:1, hv, :] += jnp.sum(dyc * u_up, axis=0, keepdims=True)
            dw_ref[1:2, hv, :] += jnp.sum(dyc * u, axis=0, keepdims=True)
            dw_ref[2:3, hv, :] += jnp.sum(dyc * u_dn, axis=0, keepdims=True)

        seq_half(0)

        @pl.when(is_ctx)
        def _():
            seq_half(1)

        @pl.when(jnp.logical_not(is_ctx))
        def _():
            bg, cg = p_ref[0, 1].astype(F32), p_ref[1, 1].astype(F32)
            v, z = p_ref[2, 1].astype(F32), p_ref[3, 1].astype(F32)
            w = cw_ref[:, 1, :]
            u = cg * v
            m_up = (i > 0).astype(F32)
            m_dn = (i < nl - 1).astype(F32)

            def halo(h_ref, dqh_ref, msk):
                hb, hc = h_ref[0, 0].astype(F32), h_ref[1, 0].astype(F32)
                hv_, hz = h_ref[2, 0].astype(F32), h_ref[3, 0].astype(F32)
                return hc * hv_ * msk, pieces(dqh_ref[0].astype(F32), hb, hz)[0] * msk

            u_p, dyc_p = halo(hp_ref, dqp_ref, m_up)
            u_n, dyc_n = halo(hn_ref, dqn_ref, m_dn)
            u_ext = jnp.concatenate([u_p, u, u_n], axis=0)
            u_up, u_dn = u_ext[0:tm], u_ext[2 * GRID_W:tm + 2 * GRID_W]
            yc = w[0:1] * u_up + w[1:2] * u + w[2:3] * u_dn
            dyc, dbg_f, dz_f = pieces(dq_ref[1].astype(F32), bg, z)
            d_ext = jnp.concatenate([dyc_p, dyc, dyc_n], axis=0)
            du = w[0:1] * d_ext[2 * GRID_W:tm + 2 * GRID_W] + w[1:2] * dyc + w[2:3] * d_ext[0:tm]
            dp_ref[0, 1] = (dbg_f * yc).astype(BF)
            dp_ref[1, 1] = (du * v).astype(BF)
            dp_ref[2, 1] = (du * cg).astype(BF)
            dp_ref[3, 1] = (dz_f * yc).astype(BF)
            dw_ref[0:1, 1, :] += jnp.sum(dyc * u_up, axis=0, keepdims=True)
            dw_ref[1:2, 1, :] += jnp.sum(dyc * u, axis=0, keepdims=True)
            dw_ref[2:3, 1, :] += jnp.sum(dyc * u_dn, axis=0, keepdims=True)

    hb = tm // GRID_W

    def dq_prev(j, i):
        return (1, jnp.maximum(jnp.minimum(i, nl - 1) * hb - 1, 0), j)

    def dq_next(j, i):
        return (1, jnp.minimum((jnp.minimum(i, nl - 1) + 1) * hb, nl * hb - 1), j)

    hp, hn = _conv_halo_specs(tm, tc, nl, 4)
    (dp42, dcw), extra = _hosted_call(
        body, xch, grid=(half // tc, nt),
        in_specs=[pl.BlockSpec((2, tm, tc), lambda j, i: (0, i, j)),
                  pl.BlockSpec((1, GRID_W, tc), dq_prev), pl.BlockSpec((1, GRID_W, tc), dq_next),
                  pl.BlockSpec((4, 2, tm, tc), lambda j, i: (0, 0, i, j)), hp, hn,
                  pl.BlockSpec((3, 2, tc), lambda j, i: (0, 0, j))],
        out_specs=[pl.BlockSpec((4, 2, tm, tc), lambda j, i: (0, 0, i, j)), pl.BlockSpec((8, 2, tc), lambda j, i: (0, 0, j))],
        out_shape=[jax.ShapeDtypeStruct(p42.shape, BF), jax.ShapeDtypeStruct((8, 2, half), F32)],
        scratch=[], args=(dq3, dq3, dq3, p42, p42, p42, cw), name="l0_conv_bwd")
    return dp42, dcw, extra


def _bwd_inproj0(dp42, w_in, x, ctx, dr_l, dr_c, a2, tm, xch=None):
    l, d = x.shape
    nl, nc = l // tm, ctx.shape[0] // tm
    e = w_in.shape[1] // 4
    half = e // 2

    def body(dp_ref, w_hbm, x_ref, c_ref, dl_ref, dc_ref, a_ref, gx_ref, acc_ref, w_ref):
        i = pl.program_id(0)

        @pl.when(i == 0)
        def _():
            pltpu.sync_copy(w_hbm, w_ref)
            acc_ref[...] = jnp.zeros_like(acc_ref)

        is_ctx = i >= nl
        dh = jnp.zeros((tm, d), F32)
        for k in range(4):
            for hv in range(2):
                c0 = k * e + hv * half
                dh = dh + _dot_nt(dp_ref[k, hv], w_ref[:, c0:c0 + half])
        xv = jnp.where(is_ctx, c_ref[...], x_ref[...])
        s_sc = jnp.sum(dh * xv, axis=0, keepdims=True)
        s_sh = jnp.sum(dh, axis=0, keepdims=True)
        sel = is_ctx.astype(F32)
        acc_ref[0:1, :] += s_sc * (1.0 - sel)
        acc_ref[1:2, :] += s_sc * sel
        acc_ref[2:3, :] += s_sh * (1.0 - sel)
        acc_ref[3:4, :] += s_sh * sel

        @pl.when(jnp.logical_not(is_ctx))
        def _():
            gx_ref[...] = DN_ALPHA * dl_ref[...].astype(F32) + dh * a_ref[0:1, :]

    lat, cx = _lat_or_ctx_specs(tm, d, nl, 1, 0)
    (gx, acc), extra = _hosted_call(
        body, xch, grid=(nl + nc,),
        in_specs=[pl.BlockSpec((4, 2, tm, half), lambda i: (0, 0, i, 0)), ANY, lat, cx, lat, cx, _full((2, d))],
        out_specs=[pl.BlockSpec((tm, d), lambda i: (jnp.minimum(i, nl - 1), 0)), _full((8, d))],
        out_shape=[jax.ShapeDtypeStruct((l, d), F32), jax.ShapeDtypeStruct((8, d), F32)],
        scratch=[pltpu.VMEM(w_in.shape, BF)], args=(dp42, w_in, x, ctx, dr_l, dr_c, a2), name="l0_bwd_inproj")
    return gx, acc, extra


def _dw_inproj0(x, ctx, a2, b2, dp42, tm):
    l, d = x.shape
    lc = ctx.shape[0]
    assert lc == tm
    tl = 4 * tm if l % (4 * tm) == 0 else tm
    nl = l // tl
    half = dp42.shape[-1]
    e = 2 * half

    def body(x_ref, c_ref, a_ref, b_ref, dpl_ref, dpc_ref, o_ref, acc_ref):
        i = pl.program_id(1)

        @pl.when(i == 0)
        def _():
            acc_ref[...] = jnp.zeros_like(acc_ref)

        def add(rows_ref, dp_ref, sel):
            h = (rows_ref[...] * a_ref[sel:sel + 1, :] + b_ref[sel:sel + 1, :]).astype(BF)
            acc_ref[:, :half] += _dot_tn(h, dp_ref[0, 0])
            acc_ref[:, half:] += _dot_tn(h, dp_ref[0, 1])

        @pl.when(i < nl)
        def _():
            add(x_ref, dpl_ref, 0)

        @pl.when(i == nl)
        def _():
            add(c_ref, dpc_ref, 1)
            o_ref[...] = acc_ref[...].astype(BF)

    return pl.pallas_call(
        body, name="l0_dw_inproj", grid=(4, nl + 1),
        in_specs=[pl.BlockSpec((tl, d), lambda k, i: (jnp.minimum(i, nl - 1), 0)), _full((lc, d)),
                  _full((2, d)), _full((2, d)),
                  pl.BlockSpec((1, 2, tl, half), lambda k, i: (k, 0, jnp.minimum(i, nl - 1), 0)),
                  pl.BlockSpec((1, 2, lc, half), lambda k, i: (k, 0, l // lc, 0))],
        out_specs=pl.BlockSpec((d, e), lambda k, i: (0, k)),
        out_shape=jax.ShapeDtypeStruct((d, 4 * e), BF),
        scratch_shapes=[pltpu.VMEM((d, e), F32)], compiler_params=_cparams(),
    )(x, ctx, a2, b2, dp42, dp42)


def _dw_outproj0(q3, dr_l, dr_c, gt2, tm):
    l, d = dr_l.shape
    nl, nc = l // tm, dr_c.shape[0] // tm
    _, r, half = q3.shape
    nt = nl + nc

    def body(q_ref, dl_ref, dc_ref, g_ref, o_ref, acc_ref):
        i = pl.program_id(0)
        is_ctx = i >= nl

        @pl.when(i == 0)
        def _():
            acc_ref[...] = jnp.zeros_like(acc_ref)

        dr = jnp.where(is_ctx, dc_ref[...], dl_ref[...]).astype(F32)
        dfx = (dr * _sel_row(g_ref, is_ctx)).astype(BF)
        acc_ref[:half, :] += _dot_tn(q_ref[0], dfx)
        acc_ref[half:, :] += _dot_tn(q_ref[1], dfx)

        @pl.when(i == nt - 1)
        def _():
            o_ref[...] = acc_ref[...].astype(BF)

    lat, cx = _lat_or_ctx_specs(tm, d, nl, 1, 0)
    return pl.pallas_call(
        body, name="l0_dw_outproj", grid=(nt,),
        in_specs=[pl.BlockSpec((2, tm, half), lambda i: (0, i, 0)), lat, cx, _full((2, d))],
        out_specs=_full((2 * half, d)), out_shape=jax.ShapeDtypeStruct((2 * half, d), BF),
        scratch_shapes=[pltpu.VMEM((2 * half, d), F32)], compiler_params=_cparams(),
    )(q3, dr_l, dr_c, gt2)


def _cr_tile(j, cap=256):
    for cand in (1024, 512, 256, 128, 64, 32, 16, 8):
        if cand <= cap and j % cand == 0:
            return cand
    raise ValueError(j)


def _group_mask(lane_groups):
    row = lax.broadcasted_iota(jnp.int32, (LANE_BLOCK, LANE_BLOCK), 0) // S5_P
    lane = lax.broadcasted_iota(jnp.int32, (LANE_BLOCK, LANE_BLOCK), 1)
    return row == lane_groups(lane)


def _expand_toeplitz(wcomp):
    nb = wcomp.shape[0]
    nd = 2 * CHUNK - 1

    def body(c_ref, o_ref):
        mask = _group_mask(lambda lane: lane // S5_P)
        tiles = []
        for dd in range(nd):
            m = c_ref[0, dd]
            tiles.append(jnp.where(mask, jnp.concatenate([m] * GROUPS_PER_BLOCK, axis=1), 0.0).astype(BF))
        for s in range(CHUNK):
            for t in range(CHUNK):
                o_ref[0, s * LANE_BLOCK:(s + 1) * LANE_BLOCK, t * LANE_BLOCK:(t + 1) * LANE_BLOCK] = tiles[t - s + CHUNK - 1]

    return pl.pallas_call(
        body, name="l1_expand_toeplitz", grid=(nb,),
        in_specs=[pl.BlockSpec((1, nd, LANE_BLOCK, S5_P), lambda b: (b, 0, 0, 0))],
        out_specs=pl.BlockSpec((1, BCR_W, BCR_W), lambda b: (b, 0, 0)),
        out_shape=jax.ShapeDtypeStruct((nb, BCR_W, BCR_W), BF), compiler_params=_cparams(),
    )(wcomp)


def _expand_blocks(comp, name):
    nb = comp.shape[2]
    lanes_per_dir = ZL_W // 2

    def body(c_ref, o_ref):
        masks = [_group_mask(lambda lane, lb=lb: 2 * lb + lane // S5_N) for lb in range(4)]
        for r in range(2):
            for s in range(CHUNK):
                for ri in range(2):
                    m = c_ref[r, s, 0, :, ri * S5_N:(ri + 1) * S5_N]
                    mm = jnp.concatenate([m, m], axis=1)
                    for lb in range(4):
                        c0 = r * lanes_per_dir + ri * ZH + lb * LANE_BLOCK
                        o_ref[0, s * LANE_BLOCK:(s + 1) * LANE_BLOCK, c0:c0 + LANE_BLOCK] = (
                            jnp.where(masks[lb], mm, 0.0).astype(BF))

    return pl.pallas_call(
        body, name=name, grid=(nb,),
        in_specs=[pl.BlockSpec((2, CHUNK, 1, LANE_BLOCK, LANE_BLOCK), lambda b: (0, 0, b, 0, 0))],
        out_specs=pl.BlockSpec((1, BCR_W, ZL_W), lambda b: (b, 0, 0)),
        out_shape=jax.ShapeDtypeStruct((nb, BCR_W, ZL_W), BF), compiler_params=_cparams(),
    )(comp)


def _bdw(a, b_, kind, ctx, name):
    nb, j, ka = a.shape
    kb = b_.shape[2]
    tn = kb // 2
    tj = _cr_tile(j, 1024)
    nt = j // tj
    has_ctx = ctx is not None
    nd = 2 * CHUNK - 1

    def body(*refs):
        a_ref, b_ref = refs[0], refs[1]
        o_ref, acc_ref = refs[2 + 2 * has_ctx], refs[3 + 2 * has_ctx]
        h, t = pl.program_id(1), pl.program_id(2)

        @pl.when(t == 0)
        def _():
            if has_ctx:
                acc_ref[...] = _dot_tn(refs[2][0].astype(BF), refs[3][0].astype(BF))
            else:
                acc_ref[...] = jnp.zeros_like(acc_ref)

        acc_ref[...] += _dot_tn(a_ref[0].astype(BF), b_ref[0].astype(BF))

        if kind == "toeplitz":
            diag_ref = refs[4 + 2 * has_ctx]

            @pl.when(jnp.logical_and(t == 0, h == 0))
            def _():
                diag_ref[...] = jnp.zeros_like(diag_ref)

            @pl.when(t == nt - 1)
            def _():
                for s in range(CHUNK):
                    for tl in range(CHUNK // 2):
                        dd = h * (CHUNK // 2) + (tl - s + CHUNK - 1)
                        diag_ref[dd] += acc_ref[s * LANE_BLOCK:(s + 1) * LANE_BLOCK, tl * LANE_BLOCK:(tl + 1) * LANE_BLOCK]

            @pl.when(jnp.logical_and(t == nt - 1, h == 1))
            def _():
                mask = _group_mask(lambda lane: lane // S5_P)
                for dd in range(nd):
                    v = jnp.where(mask, diag_ref[dd], 0.0)
                    acc = v[:, :S5_P]
                    for k in range(1, GROUPS_PER_BLOCK):
                        acc = acc + v[:, k * S5_P:(k + 1) * S5_P]
                    o_ref[0, dd] = acc
        else:
            @pl.when(t == nt - 1)
            def _():
                masks = [_group_mask(lambda lane, lb=lb: 2 * lb + lane // S5_N) for lb in range(4)]
                for s in range(CHUNK):
                    for ri in range(2):
                        v = None
                        for lb in range(4):
                            c0 = ri * ZH + lb * LANE_BLOCK
                            blk = acc_ref[s * LANE_BLOCK:(s + 1) * LANE_BLOCK, c0:c0 + LANE_BLOCK]
                            blk = jnp.where(masks[lb], blk, 0.0)
                            v = blk if v is None else v + blk
                        o_ref[0, s, 0, :, ri * S5_N:(ri + 1) * S5_N] = v[:, :S5_N] + v[:, S5_N:]

    in_specs = [pl.BlockSpec((1, tj, ka), lambda b, h, t: (b, t, 0)), pl.BlockSpec((1, tj, tn), lambda b, h, t: (b, t, h))]
    args = [a, b_]
    if has_ctx:
        jc = ctx[0].shape[1]
        in_specs += [pl.BlockSpec((1, jc, ka), lambda b, h, t: (b, 0, 0)), pl.BlockSpec((1, jc, tn), lambda b, h, t: (b, 0, h))]
        args += list(ctx)
    scratch = [pltpu.VMEM((ka, tn), F32)]
    if kind == "toeplitz":
        ospec = pl.BlockSpec((1, nd, LANE_BLOCK, S5_P), lambda b, h, t: (b, 0, 0, 0))
        oshape = jax.ShapeDtypeStruct((nb, nd, LANE_BLOCK, S5_P), F32)
        scratch.append(pltpu.VMEM((nd, LANE_BLOCK, LANE_BLOCK), F32))
    else:
        ospec = pl.BlockSpec((1, CHUNK, 1, LANE_BLOCK, LANE_BLOCK), lambda b, h, t: (h, 0, b, 0, 0))
        oshape = jax.ShapeDtypeStruct((2, CHUNK, nb, LANE_BLOCK, LANE_BLOCK), F32)
    return pl.pallas_call(
        body, name=name, grid=(nb, 2, nt), in_specs=in_specs, out_specs=ospec, out_shape=oshape,
        scratch_shapes=scratch, compiler_params=_cparams(),
    )(*args)


def _scan(z_l, z_c, coef, chains, conj, s_l=None, s_c=None, name="l1_scan"):
    nb, jl, _ = z_l.shape
    jc = z_c.shape[1]
    with_da = s_l is not None
    sign = -1.0 if conj else 1.0
    hw = 2 * ZH

    def body(*refs):
        zl_ref, zc_ref, cf_ref = refs[:3]
        k = 3
        if with_da:
            sl_ref, sc_ref = refs[3:5]
            k = 5
        ol_ref, oc_ref = refs[k:k + 2]
        d = pl.program_id(1)
        rowi = lax.broadcasted_iota(jnp.int32, (8, ZH), 0)

        def coef_rows(r0, nr):
            return cf_ref[0, 0, r0:r0 + nr, :ZH], sign * cf_ref[0, 0, r0:r0 + nr, ZH:]

        steps = [(1, coef_rows(0, 1)), (2, coef_rows(1, 1)), (4, coef_rows(2, 1))]

        def run(chain):
            carry = (jnp.zeros((1, ZH), F32), jnp.zeros((1, ZH), F32))
            da = (jnp.zeros((8, ZH), F32), jnp.zeros((8, ZH), F32))
            for which, rev in chain:
                src, dst = (zc_ref, oc_ref) if which == "c" else (zl_ref, ol_ref)
                sref = (sc_ref if which == "c" else sl_ref) if with_da else None
                ng = (jc if which == "c" else jl) // 8
                tr, ti = coef_rows(16, 8) if rev else coef_rows(8, 8)

                def step(it, st, src=src, dst=dst, sref=sref, ng=ng, tr=tr, ti=ti, rev=rev):
                    cr_, ci_, dar, dai = st
                    g = (ng - 1 - it) if rev else it
                    off = pl.multiple_of(g * 8, 8)
                    xr = src[0, pl.ds(off, 8), :ZH]
                    xi = src[0, pl.ds(off, 8), ZH:]
                    for sh, (ar, ai) in steps:
                        if rev:
                            keep = rowi < 8 - sh
                            sr = jnp.where(keep, pltpu.roll(xr, 8 - sh, 0), 0.0)
                            si = jnp.where(keep, pltpu.roll(xi, 8 - sh, 0), 0.0)
                        else:
                            keep = rowi >= sh
                            sr = jnp.where(keep, pltpu.roll(xr, sh, 0), 0.0)
                            si = jnp.where(keep, pltpu.roll(xi, sh, 0), 0.0)
                        xr, xi = xr + ar * sr - ai * si, xi + ar * si + ai * sr
                    ir = xr + tr * cr_ - ti * ci_
                    ii = xi + tr * ci_ + ti * cr_
                    if rev:
                        er = jnp.where(rowi == 7, cr_, pltpu.roll(ir, 7, 0))
                        ei = jnp.where(rowi == 7, ci_, pltpu.roll(ii, 7, 0))
                        ncr, nci = ir[0:1], ii[0:1]
                    else:
                        er = jnp.where(rowi == 0, cr_, pltpu.roll(ir, 1, 0))
                        ei = jnp.where(rowi == 0, ci_, pltpu.roll(ii, 1, 0))
                        ncr, nci = ir[7:8], ii[7:8]
                    dst[0, pl.ds(off, 8), :ZH] = er
                    dst[0, pl.ds(off, 8), ZH:] = ei
                    if sref is not None:
                        s_r = sref[0, pl.ds(off, 8), :ZH]
                        s_i = sref[0, pl.ds(off, 8), ZH:]
                        dar = dar + s_r * er + s_i * ei
                        dai = dai + s_r * ei - s_i * er
                    return ncr, nci, dar, dai

                carry_da = lax.fori_loop(0, ng, step, (*carry, *da))
                carry, da = carry_da[:2], carry_da[2:]
            if with_da:
                refs[k + 2][0, 0] = jnp.concatenate([da[0], da[1]], axis=1)

        for dd in range(2):
            @pl.when(d == dd)
            def _(dd=dd):
                run(chains[dd])

    zspec_l = pl.BlockSpec((1, jl, hw), lambda b, d: (b, 0, d))
    zspec_c = pl.BlockSpec((1, jc, hw), lambda b, d: (b, 0, d))
    in_specs = [zspec_l, zspec_c, pl.BlockSpec((1, 1, 24, hw), lambda b, d: (b, d, 0, 0))]
    args = [z_l, z_c, coef]
    out_specs = [zspec_l, zspec_c]
    out_shape = [jax.ShapeDtypeStruct(z_l.shape, F32), jax.ShapeDtypeStruct(z_c.shape, F32)]
    if with_da:
        in_specs += [zspec_l, zspec_c]
        args += [s_l, s_c]
        out_specs.append(pl.BlockSpec((1, 1, 8, hw), lambda b, d: (b, d, 0, 0)))
        out_shape.append(jax.ShapeDtypeStruct((nb, 2, 8, hw), F32))
    return pl.pallas_call(body, name=name, grid=(nb, 2), in_specs=in_specs, out_specs=out_specs,
                          out_shape=out_shape, compiler_params=_cparams())(*args)


def _glu_fwd(y_bcr, z_cr, w_glu, b_glu):
    nb, j, _ = y_bcr.shape
    e = nb * LANE_BLOCK
    tj = _cr_tile(j)

    def body(y_ref, z_ref, w_hbm, b_ref, o_ref, sg_ref, w_ref):
        @pl.when(jnp.logical_and(pl.program_id(0) == 0, pl.program_id(1) == 0))
        def _():
            pltpu.sync_copy(w_hbm, w_ref)

        y = jnp.concatenate([y_ref[b] for b in range(nb)], axis=1).astype(F32)
        g = _gelu_parts(y)[0]
        sg = _sigmoid(_dot(g.astype(BF), w_ref[...]) + b_ref[...])
        z = z_ref[...].astype(F32)
        o_ref[...] = (g * sg * (z * _sigmoid(z))).astype(BF)
        sg_ref[...] = sg.astype(BF)

    tok = pl.BlockSpec((tj, e), lambda t, s: (t, s))
    return pl.pallas_call(
        body, name="l1_glu_fwd", grid=(j // tj, CHUNK),
        in_specs=[pl.BlockSpec((nb, tj, LANE_BLOCK), lambda t, s: (0, t, s)), tok, ANY, _full((1, e))],
        out_specs=[tok, tok],
        out_shape=[jax.ShapeDtypeStruct((j, CHUNK * e), BF), jax.ShapeDtypeStruct((j, CHUNK * e), BF)],
        scratch_shapes=[pltpu.VMEM(w_glu.shape, BF)], compiler_params=_cparams(),
    )(y_bcr, z_cr, w_glu, b_glu)


def _final(w_cr, w_out, xh_cr, tgt_cr, vecs):
    j, e16 = w_cr.shape
    e = e16 // CHUNK
    d = w_out.shape[1]
    tj = _cr_tile(j)

    def body(w_ref, wo_hbm, xh_ref, t_ref, v_ref, dr_ref, acc_ref, wo_ref):
        @pl.when(jnp.logical_and(pl.program_id(0) == 0, pl.program_id(1) == 0))
        def _():
            pltpu.sync_copy(wo_hbm, wo_ref)
            acc_ref[...] = jnp.zeros_like(acc_ref)

        o = _dot(w_ref[...], wo_ref[...])
        x1 = xh_ref[...] * v_ref[0:1, :] + v_ref[1:2, :]
        rr = DN_ALPHA * x1 + v_ref[2:3, :] * o
        mu = jnp.mean(rr, axis=-1, keepdims=True)
        cen = rr - mu
        rstd = lax.rsqrt(jnp.mean(cen * cen, axis=-1, keepdims=True) + LN_EPS)
        xh2 = cen * rstd
        err = xh2 * v_ref[3:4, :] + v_ref[4:5, :] - t_ref[...]
        dy = err * (1.0 / d)
        dxh = dy * v_ref[3:4, :]
        dr = rstd * (dxh - jnp.mean(dxh, axis=-1, keepdims=True) - xh2 * jnp.mean(dxh * xh2, axis=-1, keepdims=True))
        dr_ref[...] = dr.astype(BF)
        acc_ref[0:1, :] += jnp.sum(dy * xh2, axis=0, keepdims=True)
        acc_ref[1:2, :] += jnp.sum(dy, axis=0, keepdims=True)
        acc_ref[2:3, :] += jnp.sum(dr * o, axis=0, keepdims=True)
        acc_ref[3:4, :] += (0.5 / d) * jnp.sum(err * err, axis=0, keepdims=True)

    tok_d = pl.BlockSpec((tj, d), lambda t, s: (t, s))
    return pl.pallas_call(
        body, name="l1_final", grid=(j // tj, CHUNK),
        in_specs=[pl.BlockSpec((tj, e), lambda t, s: (t, s)), ANY, tok_d, tok_d, _full((8, d))],
        out_specs=[tok_d, _full((8, d))],
        out_shape=[jax.ShapeDtypeStruct((j, CHUNK * d), BF), jax.ShapeDtypeStruct((8, d), F32)],
        scratch_shapes=[pltpu.VMEM(w_out.shape, BF)], compiler_params=_cparams(),
    )(w_cr, w_out, xh_cr, tgt_cr, vecs)


def _glu_bwd(dr_cr, gt1, w_out, w_glu, y_bcr, z_cr, sg_cr):
    nb, j, _ = y_bcr.shape
    e, d = w_out.shape
    tj = _cr_tile(j)

    def body(dr_ref, g_ref, wo_hbm, wg_hbm, y_ref, z_ref, sg_ref, dz_ref, dt_ref, dy_ref, wo_ref, wg_ref):
        @pl.when(jnp.logical_and(pl.program_id(0) == 0, pl.program_id(1) == 0))
        def _():
            pltpu.sync_copy(wo_hbm, wo_ref)
            pltpu.sync_copy(wg_hbm, wg_ref)

        do = (dr_ref[...].astype(F32) * g_ref[...]).astype(BF)
        dw = _dot_nt(do, wo_ref[...])
        y = jnp.concatenate([y_ref[b] for b in range(nb)], axis=1).astype(F32)
        g, dgel = _gelu_parts(y)
        z = z_ref[...].astype(F32)
        sz = _sigmoid(z)
        sg = sg_ref[...].astype(F32)
        dg2 = dw * (z * sz)
        dz_ref[...] = (dw * g * sg * (sz * (1.0 + z * (1.0 - sz)))).astype(BF)
        dt = (dg2 * g * sg * (1.0 - sg)).astype(BF)
        dt_ref[...] = dt
        dy = (dg2 * sg + _dot_nt(dt, wg_ref[...])) * dgel
        for b in range(nb):
            dy_ref[b] = dy[:, b * LANE_BLOCK:(b + 1) * LANE_BLOCK].astype(BF)

    tok_e = pl.BlockSpec((tj, e), lambda t, s: (t, s))
    blk = pl.BlockSpec((nb, tj, LANE_BLOCK), lambda t, s: (0, t, s))
    return pl.pallas_call(
        body, name="l1_glu_bwd", grid=(j // tj, CHUNK),
        in_specs=[pl.BlockSpec((tj, d), lambda t, s: (t, s)), _full((1, d)), ANY, ANY, blk, tok_e, tok_e],
        out_specs=[tok_e, tok_e, blk],
        out_shape=[jax.ShapeDtypeStruct((j, CHUNK * e), BF), jax.ShapeDtypeStruct((j, CHUNK * e), BF),
                   jax.ShapeDtypeStruct((nb, j, BCR_W), BF)],
        scratch_shapes=[pltpu.VMEM(w_out.shape, BF), pltpu.VMEM(w_glu.shape, BF)], compiler_params=_cparams(),
    )(dr_cr, gt1, w_out, w_glu, y_bcr, z_cr, sg_cr)


def _bwd_inproj1(du_bcr, dz_cr, w, xh_cr, rs_cr, dr2_cr, vecs, tag):
    nb, j, _ = du_bcr.shape
    d = w.shape[0]
    e = w.shape[1] // 2
    tj = _cr_tile(j)

    def body(du_ref, dz_ref, w_hbm, xh_ref, rs_ref, dr2_ref, v_ref, dr1_ref, acc_ref, w_ref):
        @pl.when(jnp.logical_and(pl.program_id(0) == 0, pl.program_id(1) == 0))
        def _():
            pltpu.sync_copy(w_hbm, w_ref)
            acc_ref[...] = jnp.zeros_like(acc_ref)

        du = jnp.concatenate([du_ref[b] for b in range(nb)], axis=1)
        dh = _dot_nt(du, w_ref[:, :e]) + _dot_nt(dz_ref[...], w_ref[:, e:])
        xh = xh_ref[...]
        x1 = xh * v_ref[0:1, :] + v_ref[1:2, :]
        dx1 = DN_ALPHA * dr2_ref[...].astype(F32) + dh * v_ref[2:3, :]
        dxh = dx1 * v_ref[0:1, :]
        rstd = rs_ref[:, 0:1]
        dr1 = rstd * (dxh - jnp.mean(dxh, axis=-1, keepdims=True) - xh * jnp.mean(dxh * xh, axis=-1, keepdims=True))
        dr1_ref[...] = dr1.astype(BF)
        acc_ref[0:1, :] += jnp.sum(dh * x1, axis=0, keepdims=True)
        acc_ref[1:2, :] += jnp.sum(dh, axis=0, keepdims=True)
        acc_ref[2:3, :] += jnp.sum(dx1 * xh, axis=0, keepdims=True)
        acc_ref[3:4, :] += jnp.sum(dx1, axis=0, keepdims=True)

    tok_d = pl.BlockSpec((tj, d), lambda t, s: (t, s))
    return pl.pallas_call(
        body, name="l1_bwd_inproj_" + tag, grid=(j // tj, CHUNK),
        in_specs=[pl.BlockSpec((nb, tj, LANE_BLOCK), lambda t, s: (0, t, s)), pl.BlockSpec((tj, e), lambda t, s: (t, s)),
                  ANY, tok_d, pl.BlockSpec((tj, 128), lambda t, s: (t, s)), tok_d, _full((8, d))],
        out_specs=[tok_d, _full((8, d))],
        out_shape=[jax.ShapeDtypeStruct((j, CHUNK * d), BF), jax.ShapeDtypeStruct((8, d), F32)],
        scratch_shapes=[pltpu.VMEM(w.shape, BF)], compiler_params=_cparams(),
    )(du_bcr, dz_cr, w, xh_cr, rs_cr, dr2_cr, vecs)


def _dw_cr(lhs, rhs, lhs_kind, rhs_kind, vec, bias_sum, init, name):
    if lhs_kind == "gelu_bcr":
        nb_l, j, _ = lhs.shape
        k = nb_l * LANE_BLOCK
    else:
        j = lhs.shape[0]
        k = lhs.shape[1] // CHUNK
    if rhs_kind == "bcr":
        nb_r = rhs.shape[0]
        n = nb_r * LANE_BLOCK
    else:
        n = rhs.shape[1] // CHUNK
    tj = _cr_tile(j, 512)
    nh = 2 if k * n * 4 > (8 << 20) else 1
    tn = n // nh
    nbh = tn // LANE_BLOCK
    nt = j // tj
    has_init = init is not None

    def body(*refs):
        refs = list(refs)
        l_ref, r_ref = refs[0], refs[1]
        pos = 2
        v_ref = None
        if vec is not None:
            v_ref = refs[pos]
            pos += 1
        i_ref = None
        if has_init:
            i_ref = refs[pos]
            pos += 1
        o_ref = refs[pos]
        pos += 1
        bs_ref = None
        if bias_sum:
            bs_ref = refs[pos]
            pos += 1
        acc_ref = refs[pos]
        t, s = pl.program_id(1), pl.program_id(2)
        first = jnp.logical_and(t == 0, s == 0)

        @pl.when(first)
        def _():
            acc_ref[...] = i_ref[...] if has_init else jnp.zeros_like(acc_ref)
            if bias_sum:
                bs_ref[...] = jnp.zeros_like(bs_ref)

        if lhs_kind == "gelu_bcr":
            y = jnp.concatenate([l_ref[b] for b in range(nb_l)], axis=1).astype(F32)
            lv = _gelu_parts(y)[0].astype(BF)
        elif lhs_kind == "mod":
            lv = (l_ref[...] * v_ref[0:1, :] + v_ref[1:2, :]).astype(BF)
        else:
            lv = l_ref[...]
        if rhs_kind == "bcr":
            rv = jnp.concatenate([r_ref[b] for b in range(nbh)], axis=1)
        elif rhs_kind == "scaled":
            rv = (r_ref[...].astype(F32) * v_ref[0:1, :]).astype(BF)
        else:
            rv = r_ref[...]
        acc_ref[...] += _dot_tn(lv, rv)
        if bias_sum:
            bs_ref[0:1, :] += jnp.sum(rv.astype(F32), axis=0, keepdims=True)

        @pl.when(jnp.logical_and(t == nt - 1, s == CHUNK - 1))
        def _():
            o_ref[...] = acc_ref[...].astype(BF)

    if lhs_kind == "gelu_bcr":
        l_spec = pl.BlockSpec((nb_l, tj, LANE_BLOCK), lambda h, t, s: (0, t, s))
    else:
        l_spec = pl.BlockSpec((tj, k), lambda h, t, s: (t, s))
    if rhs_kind == "bcr":
        r_spec = pl.BlockSpec((nbh, tj, LANE_BLOCK), lambda h, t, s: (h, t, s))
    else:
        r_spec = pl.BlockSpec((tj, tn), lambda h, t, s: (t, s * nh + h))
    in_specs, args = [l_spec, r_spec], [lhs, rhs]
    if vec is not None:
        in_specs.append(_full(vec.shape))
        args.append(vec)
    o_spec = pl.BlockSpec((k, tn), lambda h, t, s: (0, h))
    if has_init:
        in_specs.append(o_spec)
        args.append(init)
    out_specs, out_shape = [o_spec], [jax.ShapeDtypeStruct((k, n), BF)]
    if bias_sum:
        out_specs.append(pl.BlockSpec((8, tn), lambda h, t, s: (0, h)))
        out_shape.append(jax.ShapeDtypeStruct((8, n), F32))
    res = pl.pallas_call(
        body, name=name, grid=(nh, nt, CHUNK), in_specs=in_specs, out_specs=out_specs, out_shape=out_shape,
        scratch_shapes=[pltpu.VMEM((k, tn), F32)], compiler_params=_cparams(),
    )(*args)
    return res if bias_sum else res[0]


def _dw_cr_f32(lhs, rhs, vec, name):
    j = lhs.shape[0]
    k = lhs.shape[1] // CHUNK
    nb_r = rhs.shape[0]
    n = nb_r * LANE_BLOCK
    tj = _cr_tile(j)
    nt = j // tj

    def body(l_ref, r_ref, v_ref, o_ref):
        @pl.when(jnp.logical_and(pl.program_id(0) == 0, pl.program_id(1) == 0))
        def _():
            o_ref[...] = jnp.zeros_like(o_ref)

        lv = (l_ref[...] * v_ref[0:1, :] + v_ref[1:2, :]).astype(BF)
        rv = jnp.concatenate([r_ref[b] for b in range(nb_r)], axis=1)
        o_ref[...] += _dot_tn(lv, rv)

    return pl.pallas_call(
        body, name=name, grid=(nt, CHUNK),
        in_specs=[pl.BlockSpec((tj, k), lambda t, s: (t, s)), pl.BlockSpec((nb_r, tj, LANE_BLOCK), lambda t, s: (0, t, s)),
                  _full(vec.shape)],
        out_specs=_full((k, n)), out_shape=jax.ShapeDtypeStruct((k, n), F32), compiler_params=_cparams(),
    )(lhs, rhs, vec)


GT_ROWS = CHUNK * S5_P
ZG_W = 2 * 2 * S5_N
PAIR_W = 2 * ZG_W
GROUPS_PER_STEP = 4


def _inproj1_gt(xh_cr, a1, b1, wu_t, w_z, tag):
    j, d16 = xh_cr.shape
    d = d16 // CHUNK
    e = wu_t.shape[0]
    g = e // S5_P
    tj = _cr_tile(j, 256)

    def body(x_ref, a_ref, b_ref, wu_hbm, wz_hbm, u_ref, z_ref, wu_ref, wz_ref):
        @pl.when(jnp.logical_and(pl.program_id(0) == 0, pl.program_id(1) == 0))
        def _():
            pltpu.sync_copy(wu_hbm, wu_ref)
            pltpu.sync_copy(wz_hbm, wz_ref)

        h = (x_ref[...] * a_ref[...] + b_ref[...]).astype(BF)
        u_ref[...] = _dot_nt(wu_ref[...], h).reshape(g, S5_P, tj).astype(BF)
        z_ref[...] = _dot(h, wz_ref[...]).astype(BF)

    return pl.pallas_call(
        body, name="l1_inproj_" + tag, grid=(j // tj, CHUNK),
        in_specs=[pl.BlockSpec((tj, d), lambda t, s: (t, s)), _full((1, d)), _full((1, d)), ANY, ANY],
        out_specs=[pl.BlockSpec((g, S5_P, tj), lambda t, s: (0, s, t)), pl.BlockSpec((tj, e), lambda t, s: (t, s))],
        out_shape=[jax.ShapeDtypeStruct((g, GT_ROWS, j), BF), jax.ShapeDtypeStruct((j, CHUNK * e), BF)],
        scratch_shapes=[pltpu.VMEM(wu_t.shape, BF), pltpu.VMEM(w_z.shape, BF)], compiler_params=_cparams(),
    )(xh_cr, a1, b1, wu_t, w_z)


def _gt_spec(j, gb=GROUPS_PER_STEP):
    return pl.BlockSpec((gb, GT_ROWS, j), lambda i: (i, 0, 0))


def _zg_spec(j, gb=GROUPS_PER_STEP):
    return pl.BlockSpec((j, gb * ZG_W), lambda i: (0, i))


def _w_spec(width, gb=GROUPS_PER_STEP):
    return pl.BlockSpec((gb, GT_ROWS, width), lambda i: (i, 0, 0))


def _pair_lanes(k):
    return slice((k // 2) * PAIR_W, (k // 2 + 1) * PAIR_W)


def _s5_z(ut_l, ut_c, bc):
    g, _, jl = ut_l.shape
    jc = ut_c.shape[2]
    gb = GROUPS_PER_STEP

    def body(ul_ref, uc_ref, bc_ref, zl_ref, zc_ref):
        for k in range(0, gb, 2):
            zl_ref[:, _pair_lanes(k)] = _dot_tn(ul_ref[k], bc_ref[k]) + _dot_tn(ul_ref[k + 1], bc_ref[k + 1])
            zc_ref[:, _pair_lanes(k)] = _dot_tn(uc_ref[k], bc_ref[k]) + _dot_tn(uc_ref[k + 1], bc_ref[k + 1])

    return pl.pallas_call(
        body, name="l1_s5_z", grid=(g // gb,), in_specs=[_gt_spec(jl), _gt_spec(jc), _w_spec(PAIR_W)],
        out_specs=[_zg_spec(jl), _zg_spec(jc)],
        out_shape=[jax.ShapeDtypeStruct((jl, g * ZG_W), F32), jax.ShapeDtypeStruct((jc, g * ZG_W), F32)],
        compiler_params=_cparams(),
    )(ut_l, ut_c, bc)


def _s5_y(ut_l, s_l, mt_t, cct):
    g, _, jl = ut_l.shape
    gb = GROUPS_PER_STEP

    def body(u_ref, s_ref, mt_ref, cc_ref, y_ref):
        for k in range(gb):
            s_k = s_ref[:, _pair_lanes(k)].astype(BF)
            y_ref[k] = (_dot(mt_ref[k], u_ref[k]) + _dot_nt(cc_ref[k], s_k)).astype(BF)

    return pl.pallas_call(
        body, name="l1_s5_y", grid=(g // gb,),
        in_specs=[_gt_spec(jl), _zg_spec(jl), _w_spec(GT_ROWS), _w_spec(PAIR_W)],
        out_specs=_gt_spec(jl), out_shape=jax.ShapeDtypeStruct((g, GT_ROWS, jl), BF), compiler_params=_cparams(),
    )(ut_l, s_l, mt_t, cct)


def _s5_ds(dyt_l, cct):
    g, _, jl = dyt_l.shape
    gb = GROUPS_PER_STEP

    def body(dy_ref, cc_ref, ds_ref):
        for k in range(0, gb, 2):
            ds_ref[:, _pair_lanes(k)] = _dot_tn(dy_ref[k], cc_ref[k]) + _dot_tn(dy_ref[k + 1], cc_ref[k + 1])

    return pl.pallas_call(
        body, name="l1_s5_ds", grid=(g // gb,), in_specs=[_gt_spec(jl), _w_spec(PAIR_W)], out_specs=_zg_spec(jl),
        out_shape=jax.ShapeDtypeStruct((jl, g * ZG_W), F32), compiler_params=_cparams(),
    )(dyt_l, cct)


def _s5_dx(dyt_l, dz_l, dz_c, mt, bc):
    g, _, jl = dyt_l.shape
    jc = dz_c.shape[0]
    gb = GROUPS_PER_STEP

    def body(dy_ref, dzl_ref, dzc_ref, mt_ref, bc_ref, dul_ref, duc_ref):
        for k in range(gb):
            dzl = dzl_ref[:, _pair_lanes(k)].astype(BF)
            dzc = dzc_ref[:, _pair_lanes(k)].astype(BF)
            dul_ref[k] = (_dot(mt_ref[k], dy_ref[k]) + _dot_nt(bc_ref[k], dzl)).astype(BF)
            duc_ref[k] = _dot_nt(bc_ref[k], dzc).astype(BF)

    return pl.pallas_call(
        body, name="l1_s5_dx", grid=(g // gb,),
        in_specs=[_gt_spec(jl), _zg_spec(jl), _zg_spec(jc), _w_spec(GT_ROWS), _w_spec(PAIR_W)],
        out_specs=[_gt_spec(jl), _gt_spec(jc)],
        out_shape=[jax.ShapeDtypeStruct((g, GT_ROWS, jl), BF), jax.ShapeDtypeStruct((g, GT_ROWS, jc), BF)],
        compiler_params=_cparams(),
    )(dyt_l, dz_l, dz_c, mt, bc)


def _s5_dw(ut_l, ut_c, dyt_l, dz_l, dz_c, s_l):
    g, _, jl = ut_l.shape
    jc = ut_c.shape[2]
    gb = GROUPS_PER_STEP

    def body(ul_ref, uc_ref, dy_ref, dzl_ref, dzc_ref, s_ref, dmt_ref, dbc_ref, dcc_ref):
        for k in range(gb):
            lanes = _pair_lanes(k)
            dmt_ref[k] = _dot_nt(ul_ref[k], dy_ref[k])
            dbc_ref[k] = (_dot(ul_ref[k], dzl_ref[:, lanes].astype(BF))
                          + _dot(uc_ref[k], dzc_ref[:, lanes].astype(BF)))
            dcc_ref[k] = _dot(dy_ref[k], s_ref[:, lanes].astype(BF))

    sd_m = jax.ShapeDtypeStruct((g, GT_ROWS, GT_ROWS), F32)
    sd_p = jax.ShapeDtypeStruct((g, GT_ROWS, PAIR_W), F32)
    return pl.pallas_call(
        body, name="l1_s5_dw", grid=(g // gb,),
        in_specs=[_gt_spec(jl), _gt_spec(jc), _gt_spec(jl), _zg_spec(jl), _zg_spec(jc), _zg_spec(jl)],
        out_specs=[_w_spec(GT_ROWS), _w_spec(PAIR_W), _w_spec(PAIR_W)], out_shape=[sd_m, sd_p, sd_p],
        compiler_params=_cparams(),
    )(ut_l, ut_c, dyt_l, dz_l, dz_c, s_l)


def _scan_g(z_l, z_c, coef, chains, conj, s_l=None, s_c=None, name="l1_scan"):
    jl, w_all = z_l.shape
    jc = z_c.shape[0]
    gb = GROUPS_PER_STEP
    wb = gb * ZG_W
    nch = wb // 256
    with_da = s_l is not None
    sign = -1.0 if conj else 1.0

    def body(*refs):
        zl_ref, zc_ref, cf_ref = refs[:3]
        k0 = 3
        if with_da:
            sl_ref, sc_ref = refs[3:5]
            k0 = 5
        ol_ref, oc_ref = refs[k0:k0 + 2]
        rowi = lax.broadcasted_iota(jnp.int32, (8, 128), 0)

        def lanes_of(ch):
            return slice(ch * 256, ch * 256 + 128), slice(ch * 256 + 128, (ch + 1) * 256)

        def coefs(ch, r0, nr):
            lr, li = lanes_of(ch)
            return cf_ref[r0:r0 + nr, lr], sign * cf_ref[r0:r0 + nr, li]

        def shift(v, sh, rev):
            if rev:
                return jnp.where(rowi < 8 - sh, pltpu.roll(v, 8 - sh, 0), 0.0)
            return jnp.where(rowi >= sh, pltpu.roll(v, sh, 0), 0.0)

        zero_row = jnp.zeros((1, 128), F32)
        zero_tile = jnp.zeros((8, 128), F32)
        carry = [zero_row] * (2 * nch)
        da = [zero_tile] * (2 * nch)
        for seg in range(len(chains[0])):
            which = chains[0][seg][0]
            assert chains[1][seg][0] == which
            revs = (chains[0][seg][1], chains[1][seg][1])
            src, dst = (zc_ref, oc_ref) if which == "c" else (zl_ref, ol_ref)
            sref = ((sc_ref if which == "c" else sl_ref) if with_da else None)
            ng = (jc if which == "c" else jl) // 8

            def step(it, st, src=src, dst=dst, sref=sref, ng=ng, revs=revs):
                carry_, da_ = list(st[:2 * nch]), list(st[2 * nch:])
                for ch in range(nch):
                    rev = revs[ch % 2]
                    lr, li = lanes_of(ch)
                    grp = (ng - 1 - it) if rev else it
                    off = pl.multiple_of(grp * 8, 8)
                    xr, xi = src[pl.ds(off, 8), lr], src[pl.ds(off, 8), li]
                    for sh, r0 in ((1, 0), (2, 1), (4, 2)):
                        ar, ai = coefs(ch, r0, 1)
                        sr, si = shift(xr, sh, rev), shift(xi, sh, rev)
                        xr, xi = xr + ar * sr - ai * si, xi + ar * si + ai * sr
                    tr, ti = coefs(ch, 16, 8) if rev else coefs(ch, 8, 8)
                    cr_, ci_ = carry_[2 * ch], carry_[2 * ch + 1]
                    ir = xr + tr * cr_ - ti * ci_
                    ii = xi + tr * ci_ + ti * cr_
                    if rev:
                        er = jnp.where(rowi == 7, cr_, pltpu.roll(ir, 7, 0))
                        ei = jnp.where(rowi == 7, ci_, pltpu.roll(ii, 7, 0))
                        carry_[2 * ch], carry_[2 * ch + 1] = ir[0:1], ii[0:1]
                    else:
                        er = jnp.where(rowi == 0, cr_, pltpu.roll(ir, 1, 0))
                        ei = jnp.where(rowi == 0, ci_, pltpu.roll(ii, 1, 0))
                        carry_[2 * ch], carry_[2 * ch + 1] = ir[7:8], ii[7:8]
                    dst[pl.ds(off, 8), lr] = er
                    dst[pl.ds(off, 8), li] = ei
                    if sref is not None:
                        s_r, s_i = sref[pl.ds(off, 8), lr], sref[pl.ds(off, 8), li]
                        da_[2 * ch] = da_[2 * ch] + s_r * er + s_i * ei
                        da_[2 * ch + 1] = da_[2 * ch + 1] + s_r * ei - s_i * er
                return (*carry_, *da_)

            st = lax.fori_loop(0, ng, step, (*carry, *da))
            carry, da = list(st[:2 * nch]), list(st[2 * nch:])
        if with_da:
            da_ref = refs[k0 + 2]
            for ch in range(nch):
                lr, li = lanes_of(ch)
                da_ref[:, lr] = da[2 * ch]
                da_ref[:, li] = da[2 * ch + 1]

    in_specs = [_zg_spec(jl), _zg_spec(jc), pl.BlockSpec((24, wb), lambda i: (0, i))]
    args = [z_l, z_c, coef]
    out_specs = [_zg_spec(jl), _zg_spec(jc)]
    out_shape = [jax.ShapeDtypeStruct(z_l.shape, F32), jax.ShapeDtypeStruct(z_c.shape, F32)]
    if with_da:
        in_specs += [_zg_spec(jl), _zg_spec(jc)]
        args += [s_l, s_c]
        out_specs.append(pl.BlockSpec((8, wb), lambda i: (0, i)))
        out_shape.append(jax.ShapeDtypeStruct((8, w_all), F32))
    return pl.pallas_call(body, name=name, grid=(w_all // wb,), in_specs=in_specs, out_specs=out_specs,
                          out_shape=out_shape, compiler_params=_cparams())(*args)


def _gt_tok_spec(g, tj):
    return pl.BlockSpec((g, S5_P, tj), lambda t, s: (0, s, t))


def _glu_fwd_gt(yt, z_cr, w_glu, b_glu):
    g, _, j = yt.shape
    e = g * S5_P
    tj = _cr_tile(j)

    def body(y_ref, z_ref, w_hbm, b_ref, o_ref, sg_ref, w_ref):
        @pl.when(jnp.logical_and(pl.program_id(0) == 0, pl.program_id(1) == 0))
        def _():
            pltpu.sync_copy(w_hbm, w_ref)

        y = jnp.transpose(y_ref[...].reshape(e, tj).astype(F32))
        gl = _gelu_parts(y)[0]
        sg = _sigmoid(_dot(gl.astype(BF), w_ref[...]) + b_ref[...])
        z = z_ref[...].astype(F32)
        o_ref[...] = (gl * sg * (z * _sigmoid(z))).astype(BF)
        sg_ref[...] = sg.astype(BF)

    tok = pl.BlockSpec((tj, e), lambda t, s: (t, s))
    return pl.pallas_call(
        body, name="l1_glu_fwd", grid=(j // tj, CHUNK),
        in_specs=[_gt_tok_spec(g, tj), tok, ANY, _full((1, e))], out_specs=[tok, tok],
        out_shape=[jax.ShapeDtypeStruct((j, CHUNK * e), BF), jax.ShapeDtypeStruct((j, CHUNK * e), BF)],
        scratch_shapes=[pltpu.VMEM(w_glu.shape, BF)], compiler_params=_cparams(),
    )(yt, z_cr, w_glu, b_glu)


def _glu_bwd_gt(dr_cr, gt1, w_out, w_glu, yt, z_cr, sg_cr):
    g, _, j = yt.shape
    e, d = w_out.shape
    tj = _cr_tile(j)

    def body(dr_ref, g_ref, wo_hbm, wg_hbm, y_ref, z_ref, sg_ref, dz_ref, dt_ref, dy_ref, wo_ref, wg_ref):
        @pl.when(jnp.logical_and(pl.program_id(0) == 0, pl.program_id(1) == 0))
        def _():
            pltpu.sync_copy(wo_hbm, wo_ref)
            pltpu.sync_copy(wg_hbm, wg_ref)

        do = (dr_ref[...].astype(F32) * g_ref[...]).astype(BF)
        dw = _dot_nt(do, wo_ref[...])
        y = jnp.transpose(y_ref[...].reshape(e, tj).astype(F32))
        gl, dgel = _gelu_parts(y)
        z = z_ref[...].astype(F32)
        sz = _sigmoid(z)
        sg = sg_ref[...].astype(F32)
        dg2 = dw * (z * sz)
        dz_ref[...] = (dw * gl * sg * (sz * (1.0 + z * (1.0 - sz)))).astype(BF)
        dt = (dg2 * gl * sg * (1.0 - sg)).astype(BF)
        dt_ref[...] = dt
        dy = (dg2 * sg + _dot_nt(dt, wg_ref[...])) * dgel
        dy_ref[...] = jnp.transpose(dy).reshape(g, S5_P, tj).astype(BF)

    tok_e = pl.BlockSpec((tj, e), lambda t, s: (t, s))
    return pl.pallas_call(
        body, name="l1_glu_bwd", grid=(j // tj, CHUNK),
        in_specs=[pl.BlockSpec((tj, d), lambda t, s: (t, s)), _full((1, d)), ANY, ANY, _gt_tok_spec(g, tj), tok_e, tok_e],
        out_specs=[tok_e, tok_e, _gt_tok_spec(g, tj)],
        out_shape=[jax.ShapeDtypeStruct((j, CHUNK * e), BF), jax.ShapeDtypeStruct((j, CHUNK * e), BF),
                   jax.ShapeDtypeStruct((g, GT_ROWS, j), BF)],
        scratch_shapes=[pltpu.VMEM(w_out.shape, BF), pltpu.VMEM(w_glu.shape, BF)], compiler_params=_cparams(),
    )(dr_cr, gt1, w_out, w_glu, yt, z_cr, sg_cr)


def _bwd_inproj1_gt(dut, dz_cr, wu_t, w_z, xh_cr, rs_cr, dr2_cr, vecs, tag):
    g, _, j = dut.shape
    e, d = wu_t.shape
    tj = _cr_tile(j)

    def body(du_ref, dz_ref, wu_hbm, wz_hbm, xh_ref, rs_ref, dr2_ref, v_ref, dr1_ref, acc_ref, wu_ref, wz_ref):
        @pl.when(jnp.logical_and(pl.program_id(0) == 0, pl.program_id(1) == 0))
        def _():
            pltpu.sync_copy(wu_hbm, wu_ref)
            pltpu.sync_copy(wz_hbm, wz_ref)
            acc_ref[...] = jnp.zeros_like(acc_ref)

        dh = _dot_tn(du_ref[...].reshape(e, tj), wu_ref[...]) + _dot_nt(dz_ref[...], wz_ref[...])
        xh = xh_ref[...]
        x1 = xh * v_ref[0:1, :] + v_ref[1:2, :]
        dx1 = DN_ALPHA * dr2_ref[...].astype(F32) + dh * v_ref[2:3, :]
        dxh = dx1 * v_ref[0:1, :]
        rstd = rs_ref[:, 0:1]
        dr1 = rstd * (dxh - jnp.mean(dxh, axis=-1, keepdims=True) - xh * jnp.mean(dxh * xh, axis=-1, keepdims=True))
        dr1_ref[...] = dr1.astype(BF)
        acc_ref[0:1, :] += jnp.sum(dh * x1, axis=0, keepdims=True)
        acc_ref[1:2, :] += jnp.sum(dh, axis=0, keepdims=True)
        acc_ref[2:3, :] += jnp.sum(dx1 * xh, axis=0, keepdims=True)
        acc_ref[3:4, :] += jnp.sum(dx1, axis=0, keepdims=True)

    tok_d = pl.BlockSpec((tj, d), lambda t, s: (t, s))
    return pl.pallas_call(
        body, name="l1_bwd_inproj_" + tag, grid=(j // tj, CHUNK),
        in_specs=[_gt_tok_spec(g, tj), pl.BlockSpec((tj, e), lambda t, s: (t, s)), ANY, ANY, tok_d,
                  pl.BlockSpec((tj, 128), lambda t, s: (t, s)), tok_d, _full((8, d))],
        out_specs=[tok_d, _full((8, d))],
        out_shape=[jax.ShapeDtypeStruct((j, CHUNK * d), BF), jax.ShapeDtypeStruct((8, d), F32)],
        scratch_shapes=[pltpu.VMEM(wu_t.shape, BF), pltpu.VMEM(w_z.shape, BF)], compiler_params=_cparams(),
    )(dut, dz_cr, wu_t, w_z, xh_cr, rs_cr, dr2_cr, vecs)


def _dw_gt(lhs_gt, rhs_cr, lhs_gelu, vec, bias_sum, init, out_dtype, name):
    g, _, j = lhs_gt.shape
    e = g * S5_P
    n = rhs_cr.shape[1] // CHUNK
    tj = _cr_tile(j, 512 if j % 512 == 0 else 256)
    nh = 2 if e * n * 4 > (8 << 20) else 1
    tn = n // nh
    nt = j // tj
    has_init = init is not None

    def body(*refs):
        refs = list(refs)
        l_ref, r_ref = refs[0], refs[1]
        pos = 2
        v_ref = i_ref = bs_ref = None
        if vec is not None:
            v_ref = refs[pos]
            pos += 1
        if has_init:
            i_ref = refs[pos]
            pos += 1
        o_ref = refs[pos]
        pos += 1
        if bias_sum:
            bs_ref = refs[pos]
            pos += 1
        acc_ref = refs[pos]
        t, s = pl.program_id(1), pl.program_id(2)

        @pl.when(jnp.logical_and(t == 0, s == 0))
        def _():
            acc_ref[...] = i_ref[...] if has_init else jnp.zeros_like(acc_ref)
            if bias_sum:
                bs_ref[...] = jnp.zeros_like(bs_ref)

        lv = l_ref[...].reshape(e, tj)
        if lhs_gelu:
            lv = _gelu_parts(lv.astype(F32))[0].astype(BF)
        if vec is not None:
            rv = (r_ref[...] * v_ref[0:1, :] + v_ref[1:2, :]).astype(BF)
        else:
            rv = r_ref[...]
        acc_ref[...] += _dot(lv, rv)
        if bias_sum:
            bs_ref[0:1, :] += jnp.sum(rv.astype(F32), axis=0, keepdims=True)

        @pl.when(jnp.logical_and(t == nt - 1, s == CHUNK - 1))
        def _():
            o_ref[...] = acc_ref[...].astype(out_dtype)

    in_specs = [pl.BlockSpec((g, S5_P, tj), lambda h, t, s: (0, s, t)),
                pl.BlockSpec((tj, tn), lambda h, t, s: (t, s * nh + h))]
    args = [lhs_gt, rhs_cr]
    if vec is not None:
        in_specs.append(_full(vec.shape))
        args.append(vec)
    o_spec = pl.BlockSpec((e, tn), lambda h, t, s: (0, h))
    if has_init:
        in_specs.append(o_spec)
        args.append(init)
    out_specs, out_shape = [o_spec], [jax.ShapeDtypeStruct((e, n), out_dtype)]
    if bias_sum:
        out_specs.append(pl.BlockSpec((8, tn), lambda h, t, s: (0, h)))
        out_shape.append(jax.ShapeDtypeStruct((8, n), F32))
    res = pl.pallas_call(
        body, name=name, grid=(nh, nt, CHUNK), in_specs=in_specs, out_specs=out_specs, out_shape=out_shape,
        scratch_shapes=[pltpu.VMEM((e, tn), F32)], compiler_params=_cparams(),
    )(*args)
    return res if bias_sum else res[0]


def _s5_weights(lam_re, lam_im, log_step, b_re, b_im, c_re, c_im, d_skip):
    hp = lax.Precision.HIGHEST
    g = lam_re.shape[1]
    t, p, n = CHUNK, S5_P, S5_N
    dt = jnp.exp(log_step)[..., None]
    ks = jnp.arange(t + 1, dtype=F32).reshape(t + 1, 1, 1, 1)
    mag = jnp.exp(ks * (lam_re * dt)[None])
    ang = ks * (lam_im * dt)[None]
    pr, pi = mag * jnp.cos(ang), mag * jnp.sin(ang)
    ar, ai = pr[1], pi[1]
    qr, qi = ar - 1.0, ai
    den = lam_re * lam_re + lam_im * lam_im
    fr = (qr * lam_re + qi * lam_im) / den
    fi = (qi * lam_re - qr * lam_im) / den
    bt_re, bt_im = b_re.transpose(0, 1, 3, 2), b_im.transpose(0, 1, 3, 2)
    bbr = fr[:, :, None, :] * bt_re - fi[:, :, None, :] * bt_im
    bbi = fr[:, :, None, :] * bt_im + fi[:, :, None, :] * bt_re
    pk_r, pk_i = pr[:t, :, :, None, :], pi[:t, :, :, None, :]
    abr = pk_r * bbr[None] - pk_i * bbi[None]
    abi = pk_r * bbi[None] + pk_i * bbr[None]
    kd = (jnp.einsum("rgpn,krgqn->rgkpq", c_re, abr, precision=hp)
          - jnp.einsum("rgpn,krgqn->rgkpq", c_im, abi, precision=hp))
    skip = jnp.eye(p, dtype=F32)[None] * d_skip.reshape(g, p)[:, :, None]
    diag = kd[0][:, 0] + kd[1][:, 0] + skip
    qd = jnp.concatenate([jnp.flip(kd[1][:, 1:], axis=1), diag[:, None], kd[0][:, 1:]], axis=1)
    toep = jnp.stack([qd[:, t - 1 - s:2 * t - 1 - s] for s in range(t)], axis=1)
    mt = toep.transpose(0, 1, 4, 2, 3).reshape(g, t * p, t * p)
    ab = jnp.concatenate([abr, abi], axis=-1)
    bcc = jnp.stack([jnp.flip(ab[:, 0], axis=0), ab[:, 1]])
    bc = bcc.transpose(2, 1, 3, 0, 4).reshape(g, t * p, 4 * n)
    prf = jnp.stack([pr[1:, 0], jnp.flip(pr[1:, 1], axis=0)])[:, :, :, None, :]
    pif = jnp.stack([pi[1:, 0], jnp.flip(pi[1:, 1], axis=0)])[:, :, :, None, :]
    cr_t = c_re[:, None]
    ci_t = c_im[:, None]
    ccc = jnp.concatenate([cr_t * prf - ci_t * pif, -(cr_t * pif + ci_t * prf)], axis=-1)
    cct = ccc.transpose(2, 1, 3, 0, 4).reshape(g, t * p, 4 * n)

    def pair_lanes(a):
        a5 = a.reshape(g // 2, 2, t * p, 4, n)
        return jnp.einsum("agrcn,gh->agrchn", a5, jnp.eye(2, dtype=F32)).reshape(g, t * p, PAIR_W)

    return mt, pair_lanes(bc), pair_lanes(cct), pr[t], pi[t]


def _scan_coef_g(lam_re, lam_im, log_step):
    g = lam_re.shape[1]
    ms = jnp.array([1, 2, 4, 0, 0, 0, 0, 0] + list(range(1, 9)) + list(range(8, 0, -1)), F32) * CHUNK
    dt = jnp.exp(log_step)[..., None]
    mag = jnp.exp(ms.reshape(-1, 1, 1, 1) * (lam_re * dt)[None])
    ang = ms.reshape(-1, 1, 1, 1) * (lam_im * dt)[None]
    cr, ci = mag * jnp.cos(ang), mag * jnp.sin(ang)
    both = jnp.stack([cr, ci], axis=2).reshape(24, 2, 2, g // 2, 2, S5_N)
    return both.transpose(0, 3, 1, 2, 4, 5).reshape(24, g * ZG_W)


def _s5_small(lam_re, lam_im, log_step, b_re, b_im, c_re, c_im, d_skip):
    g = lam_re.shape[1]
    t, p = CHUNK, S5_P
    dt = jnp.exp(log_step)[..., None]
    ks = jnp.arange(t + 1, dtype=F32).reshape(t + 1, 1, 1, 1)
    mag = jnp.exp(ks * (lam_re * dt)[None])
    ang = ks * (lam_im * dt)[None]
    pr, pi = mag * jnp.cos(ang), mag * jnp.sin(ang)
    ar, ai = pr[1], pi[1]
    qr, qi = ar - 1.0, ai
    den = lam_re * lam_re + lam_im * lam_im
    fr = (qr * lam_re + qi * lam_im) / den
    fi = (qi * lam_re - qr * lam_im) / den
    bt_re, bt_im = b_re.transpose(0, 1, 3, 2), b_im.transpose(0, 1, 3, 2)
    bbr = fr[:, :, None, :] * bt_re - fi[:, :, None, :] * bt_im
    bbi = fr[:, :, None, :] * bt_im + fi[:, :, None, :] * bt_re
    lay = lambda a_r, a_i: jnp.stack([a_r, a_i], axis=0).transpose(3, 2, 0, 1, 4)
    by_dir = lambda a, f0, f1: jnp.stack([f0(a[:, 0]), f1(a[:, 1])], axis=1)
    rev = lambda a: jnp.flip(a, axis=0)
    same = lambda a: a
    pwb = lay(by_dir(pr[:t], rev, same), by_dir(pi[:t], rev, same))
    pwc = lay(by_dir(pr[1:], same, rev), by_dir(pi[1:], same, rev))
    bb = jnp.stack([bbr, bbi], axis=0).transpose(2, 1, 0, 3, 4)
    cc = jnp.stack([c_re, c_im], axis=0).transpose(2, 1, 0, 3, 4)
    dmat = jnp.eye(p, dtype=F32)[None] * d_skip.reshape(g, p)[:, :, None]
    return pwb, pwc, bb, cc, dmat, pr[t], pi[t]


def _pair_cols(r, ri, g2):
    c0 = (r * 2 + ri) * 128 + g2 * S5_N
    return slice(c0, c0 + S5_N)


def _rows_rep(a):
    return jnp.broadcast_to(a[:, None, :], (CHUNK, S5_P, a.shape[-1])).reshape(GT_ROWS, a.shape[-1])


def _rows_tile(a):
    return jnp.broadcast_to(a[None], (CHUNK, S5_P, a.shape[-1])).reshape(GT_ROWS, a.shape[-1])


def _sum_blocks(a):
    return jnp.sum(a.reshape(CHUNK, S5_P, a.shape[-1]), axis=0)


def _sum_in_blocks(a):
    return jnp.sum(a.reshape(CHUNK, S5_P, a.shape[-1]), axis=1)


def _ab_rows(pwb_ref, bb_ref, k, r):
    prs, pis = _rows_rep(pwb_ref[k, r, 0]), _rows_rep(pwb_ref[k, r, 1])
    bbr, bbi = _rows_tile(bb_ref[k, r, 0]), _rows_tile(bb_ref[k, r, 1])
    return prs * bbr - pis * bbi, prs * bbi + pis * bbr, prs, pis, bbr, bbi


def _s5_weights_fwd(pwb, pwc, bb, cc, dmat):
    g = pwb.shape[0]
    gb = GROUPS_PER_STEP
    hp = lax.Precision.HIGHEST

    def body(pwb_ref, pwc_ref, bb_ref, cc_ref, dm_ref, mt_ref, mtt_ref, bc_ref, cct_ref):
        zeros = jnp.zeros((GT_ROWS, S5_N), BF)
        nt = (((1,), (1,)), ((), ()))
        for k in range(gb):
            g2 = k % 2
            kds = []
            for r in range(2):
                for ri in range(2):
                    bc_ref[k, :, _pair_cols(r, ri, 1 - g2)] = zeros
                    cct_ref[k, :, _pair_cols(r, ri, 1 - g2)] = zeros
                abr, abi = _ab_rows(pwb_ref, bb_ref, k, r)[:2]
                bc_ref[k, :, _pair_cols(r, 0, g2)] = abr.astype(BF)
                bc_ref[k, :, _pair_cols(r, 1, g2)] = abi.astype(BF)
                cr, ci = cc_ref[k, r, 0], cc_ref[k, r, 1]
                crt, cit = _rows_tile(cr), _rows_tile(ci)
                prt, pit = _rows_rep(pwc_ref[k, r, 0]), _rows_rep(pwc_ref[k, r, 1])
                cct_ref[k, :, _pair_cols(r, 0, g2)] = (crt * prt - cit * pit).astype(BF)
                cct_ref[k, :, _pair_cols(r, 1, g2)] = (-(crt * pit + cit * prt)).astype(BF)
                kds.append(lax.dot_general(abr, cr, nt, precision=hp, preferred_element_type=F32)
                           - lax.dot_general(abi, ci, nt, precision=hp, preferred_element_type=F32))
            blk = lambda a, s: a[s * S5_P:(s + 1) * S5_P]
            last = CHUNK - 1
            pieces = [blk(kds[1], last - i) for i in range(last)]
            pieces.append(blk(kds[0], last) + blk(kds[1], 0) + dm_ref[k])
            pieces += [blk(kds[0], last - d) for d in range(1, CHUNK)]
            qrow = jnp.concatenate(pieces, axis=1)
            mt = jnp.concatenate([qrow[:, (last - s) * S5_P:(last - s) * S5_P + GT_ROWS] for s in range(CHUNK)], axis=0)
            mt_ref[k] = mt.astype(BF)
            mtt_ref[k] = jnp.transpose(mt).astype(BF)

    small = lambda a: pl.BlockSpec((gb, *a.shape[1:]), lambda i: (i,) + (0,) * (a.ndim - 1))
    return pl.pallas_call(
        body, name="l1_s5_weights", grid=(g // gb,),
        in_specs=[small(pwb), small(pwc), small(bb), small(cc), small(dmat)],
        out_specs=[_w_spec(GT_ROWS), _w_spec(GT_ROWS), _w_spec(PAIR_W), _w_spec(PAIR_W)],
        out_shape=[jax.ShapeDtypeStruct((g, GT_ROWS, GT_ROWS), BF), jax.ShapeDtypeStruct((g, GT_ROWS, GT_ROWS), BF),
                   jax.ShapeDtypeStruct((g, GT_ROWS, PAIR_W), BF), jax.ShapeDtypeStruct((g, GT_ROWS, PAIR_W), BF)],
        compiler_params=_cparams(),
    )(pwb, pwc, bb, cc, dmat)


def _s5_weights_bwd(pwb, pwc, bb, cc, d_mt, d_bc, d_cct):
    g = pwb.shape[0]
    gb = GROUPS_PER_STEP
    hp = lax.Precision.HIGHEST

    def body(pwb_ref, pwc_ref, bb_ref, cc_ref, dmt_ref, dbc_ref, dcc_ref, dpwb_ref, dpwc_ref, dbb_ref, dccp_ref, ddm_ref):
        tn = (((0,), (0,)), ((), ()))
        nn = (((1,), (0,)), ((), ()))
        last = CHUNK - 1
        for k in range(gb):
            g2 = k % 2
            dq = None
            for s in range(CHUNK):
                parts = [dmt_ref[k, s * S5_P:(s + 1) * S5_P, :]]
                if s < last:
                    parts.insert(0, jnp.zeros((S5_P, (last - s) * S5_P), F32))
                if s > 0:
                    parts.append(jnp.zeros((S5_P, s * S5_P), F32))
                padded = jnp.concatenate(parts, axis=1) if len(parts) > 1 else parts[0]
                dq = padded if dq is None else dq + padded
            dblk = lambda d: dq[:, (last + d) * S5_P:(CHUNK + d) * S5_P]
            ddm_ref[k] = dblk(0)
            dkds = [jnp.concatenate([dblk(last - s) for s in range(CHUNK)], axis=0),
                    jnp.concatenate([dblk(-s) for s in range(CHUNK)], axis=0)]
            for r in range(2):
                abr, abi, prs, pis, bbr, bbi = _ab_rows(pwb_ref, bb_ref, k, r)
                cr, ci = cc_ref[k, r, 0], cc_ref[k, r, 1]
                dcr = lax.dot_general(dkds[r], abr, tn, precision=hp, preferred_element_type=F32)
                dci = -lax.dot_general(dkds[r], abi, tn, precision=hp, preferred_element_type=F32)
                dabr = (lax.dot_general(dkds[r], cr, nn, precision=hp, preferred_element_type=F32)
                        + dbc_ref[k, :, _pair_cols(r, 0, g2)])
                dabi = (-lax.dot_general(dkds[r], ci, nn, precision=hp, preferred_element_type=F32)
                        + dbc_ref[k, :, _pair_cols(r, 1, g2)])
                dbb_ref[k, r, 0] = _sum_blocks(prs * dabr + pis * dabi)
                dbb_ref[k, r, 1] = _sum_blocks(prs * dabi - pis * dabr)
                dpwb_ref[k, r, 0] = _sum_in_blocks(dabr * bbr + dabi * bbi)
                dpwb_ref[k, r, 1] = _sum_in_blocks(dabi * bbr - dabr * bbi)
                crt, cit = _rows_tile(cr), _rows_tile(ci)
                prt, pit = _rows_rep(pwc_ref[k, r, 0]), _rows_rep(pwc_ref[k, r, 1])
                d_re = dcc_ref[k, :, _pair_cols(r, 0, g2)]
                d_im = dcc_ref[k, :, _pair_cols(r, 1, g2)]
                dccp_ref[k, r, 0] = dcr + _sum_blocks(d_re * prt - d_im * pit)
                dccp_ref[k, r, 1] = dci - _sum_blocks(d_re * pit + d_im * prt)
                dpwc_ref[k, r, 0] = _sum_in_blocks(d_re * crt - d_im * cit)
                dpwc_ref[k, r, 1] = -_sum_in_blocks(d_re * cit + d_im * crt)

    small = lambda a: pl.BlockSpec((gb, *a.shape[1:]), lambda i: (i,) + (0,) * (a.ndim - 1))
    dmat_sds = jax.ShapeDtypeStruct((g, S5_P, S5_P), F32)
    return pl.pallas_call(
        body, name="l1_s5_weights_bwd", grid=(g // gb,),
        in_specs=[small(pwb), small(pwc), small(bb), small(cc), _w_spec(GT_ROWS), _w_spec(PAIR_W), _w_spec(PAIR_W)],
        out_specs=[small(pwb), small(pwc), small(bb), small(cc), small(dmat_sds)],
        out_shape=[jax.ShapeDtypeStruct(pwb.shape, F32), jax.ShapeDtypeStruct(pwc.shape, F32),
                   jax.ShapeDtypeStruct(bb.shape, F32), jax.ShapeDtypeStruct(cc.shape, F32), dmat_sds],
        compiler_params=_cparams(),
    )(pwb, pwc, bb, cc, d_mt, d_bc, d_cct)


def _s5_compact(lam_re, lam_im, log_step, b_re, b_im, c_re, c_im, d_skip):
    hp = lax.Precision.HIGHEST
    g = lam_re.shape[1]
    nb = g // GROUPS_PER_BLOCK
    t, p, n = CHUNK, S5_P, S5_N
    dt = jnp.exp(log_step)[..., None]
    ks = jnp.arange(t + 1, dtype=F32).reshape(t + 1, 1, 1, 1)
    mag = jnp.exp(ks * (lam_re * dt)[None])
    ang = ks * (lam_im * dt)[None]
    pr, pi = mag * jnp.cos(ang), mag * jnp.sin(ang)
    ar, ai = pr[1], pi[1]
    qr, qi = ar - 1.0, ai
    den = lam_re * lam_re + lam_im * lam_im
    fr = (qr * lam_re + qi * lam_im) / den
    fi = (qi * lam_re - qr * lam_im) / den
    bt_re, bt_im = b_re.transpose(0, 1, 3, 2), b_im.transpose(0, 1, 3, 2)
    bbr = fr[:, :, None, :] * bt_re - fi[:, :, None, :] * bt_im
    bbi = fr[:, :, None, :] * bt_im + fi[:, :, None, :] * bt_re
    pk_r, pk_i = pr[:t, :, :, None, :], pi[:t, :, :, None, :]
    abr = pk_r * bbr[None] - pk_i * bbi[None]
    abi = pk_r * bbi[None] + pk_i * bbr[None]
    kd = (jnp.einsum("rgpn,krgqn->rgkpq", c_re, abr, precision=hp)
          - jnp.einsum("rgpn,krgqn->rgkpq", c_im, abi, precision=hp))
    skip = jnp.eye(p, dtype=F32)[None] * d_skip.reshape(g, p)[:, :, None]
    diag = kd[0][:, 0] + kd[1][:, 0] + skip
    qd = jnp.concatenate([jnp.flip(kd[1][:, 1:], axis=1), diag[:, None], kd[0][:, 1:]], axis=1)
    nd = 2 * t - 1
    wc = qd.transpose(0, 1, 3, 2).reshape(nb, GROUPS_PER_BLOCK, nd, p, p).transpose(0, 2, 1, 3, 4)
    wcomp = wc.reshape(nb, nd, LANE_BLOCK, p)
    ab = jnp.concatenate([abr, abi], axis=-1)
    bcc = jnp.stack([jnp.flip(ab[:, 0], axis=0), ab[:, 1]])
    bcomp = bcc.reshape(2, t, nb, LANE_BLOCK, 2 * n)
    prf = jnp.stack([pr[1:, 0], jnp.flip(pr[1:, 1], axis=0)])[:, :, :, None, :]
    pif = jnp.stack([pi[1:, 0], jnp.flip(pi[1:, 1], axis=0)])[:, :, :, None, :]
    cr_t = c_re[:, None]
    ci_t = c_im[:, None]
    ccc = jnp.concatenate([cr_t * prf - ci_t * pif, -(cr_t * pif + ci_t * prf)], axis=-1)
    ccomp = ccc.reshape(2, t, nb, LANE_BLOCK, 2 * n)
    return wcomp, bcomp, ccomp, pr[t], pi[t]


def _scan_coef(lam_re, lam_im, log_step):
    g = lam_re.shape[1]
    nb = g // GROUPS_PER_BLOCK
    ms = jnp.array([1, 2, 4, 0, 0, 0, 0, 0] + list(range(1, 9)) + list(range(8, 0, -1)), F32) * CHUNK
    dt = jnp.exp(log_step)[..., None]
    mag = jnp.exp(ms.reshape(-1, 1, 1, 1) * (lam_re * dt)[None])
    ang = ms.reshape(-1, 1, 1, 1) * (lam_im * dt)[None]
    cr, ci = mag * jnp.cos(ang), mag * jnp.sin(ang)
    lay = lambda a: a.reshape(24, 2, nb, ZH).transpose(2, 1, 0, 3)
    return jnp.concatenate([lay(cr), lay(ci)], axis=-1)


def _to_cr(a):
    return a.reshape(a.shape[0] // CHUNK, CHUNK * a.shape[1])


def _from_cr(a, c):
    return a.reshape(a.shape[0] * CHUNK, c)


def _pad8(v):
    return jnp.concatenate([v, jnp.zeros((8 - v.shape[0], v.shape[1]), v.dtype)], axis=0)


def _local_step(x, c, ctx, c_ctx, loss_target, w, late=None, scatter=False, mod=None):
    l, d = x.shape
    lc = ctx.shape[0]
    tm = min(256, lc)
    assert lc == tm and l % tm == 0 and tm % GRID_W == 0 and (tm & (tm - 1)) == 0
    nl = l // tm

    own_mod = mod is None
    if own_mod:
        c8 = _pad8(jnp.stack([c, c_ctx]))
        mod = _ada_fwd(c8, w["ada_w"], w["ada_b"])
    sh = mod[:, :2, :d]
    sc = mod[:, :2, d:2 * d]
    gt = mod[:, :2, 2 * d:]
    ln_g, ln_b = w["ln_g"], w["ln_b"]

    a0, b0 = 1.0 + sc[0], sh[0]
    xch = _Exchange("gather2", [late[n][0] for n in late], [late[n][1] for n in late]) if late else None
    p42, got = _inproj0(x, ctx, a0, b0, w["conv_w_in"], tm, xch)
    if late:
        w = dict(w, **dict(zip(late, got)))
    e = w["conv_w_out"].shape[0]
    half = e // 2
    nb = e // LANE_BLOCK
    tc = min(512, half)
    cw = w["conv_w"].reshape(3, 2, half)
    q3 = _conv_fwd(p42, cw, nl, tm, tc)
    xh1_l, xh1_c, rs1_l, rs1_c, fx = _outproj_ln0(q3, w["conv_w_out"], x, ctx, gt[0], tm)
    jl, jc = l // CHUNK, lc // CHUNK

    g0, bb0 = ln_g[0:1], ln_b[0:1]
    a1 = g0 * (1.0 + sc[1])
    b1 = bb0 * (1.0 + sc[1]) + sh[1]
    wu_t = w["ssm_w_in"][:, :e].T
    w_z = w["ssm_w_in"][:, e:]
    ut_l, z_l = _inproj1_gt(xh1_l, a1[0:1], b1[0:1], wu_t, w_z, "lat")
    ut_c, _ = _inproj1_gt(xh1_c, a1[1:2], b1[1:2], wu_t, w_z, "ctx")
    s5 = (w["ssm_lam_re"], w["ssm_lam_im"], w["ssm_log_step"], w["ssm_b_re"], w["ssm_b_im"],
          w["ssm_c_re"], w["ssm_c_im"], w["ssm_d"])
    (pwb, pwc, bbw, ccw, dmat, _, _), s5_vjp = jax.vjp(_s5_small, *s5)
    mt_b, mtt_b, bc_b, cct_b = _s5_weights_fwd(pwb, pwc, bbw, ccw, dmat)
    coef = lax.stop_gradient(_scan_coef_g(*s5[:3]))
    zz_l, zz_c = _s5_z(ut_l, ut_c, bc_b)
    fwd_chains = ((("c", False), ("l", False)), (("c", True), ("l", True)))
    st_l, st_c = _scan_g(zz_l, zz_c, coef, fwd_chains, False, name="l1_scan_fwd")
    yt = _s5_y(ut_l, st_l, mtt_b, cct_b)
    b_glu = w["ssm_b_glu"].reshape(1, e)
    w_cr, sg_cr = _glu_fwd_gt(yt, z_l, w["ssm_w_glu"], b_glu)
    vec_f = _pad8(jnp.concatenate([g0, bb0, gt[1][0:1], ln_g[1:2], ln_b[1:2]], axis=0))
    dr2, acc_f = _final(w_cr, w["ssm_w_out"], xh1_l, _to_cr(loss_target), vec_f)
    loss = jnp.sum(acc_f[3])

    gt1 = gt[1][0:1]
    dz_l, dt_l, dyt = _glu_bwd_gt(dr2, gt1, w["ssm_w_out"], w["ssm_w_glu"], yt, z_l, sg_cr)
    g_w_out = _dw_cr(w_cr, dr2, "cr", "scaled", gt1, False, None, "l1_dw_out")
    g_w_glu, bsum = _dw_gt(yt, dt_l, True, None, True, None, BF, "l1_dw_glu")
    g_b_glu = bsum[0]
    ds_l = _s5_ds(dyt, cct_b)
    bwd_chains = ((("l", True), ("c", True)), (("l", False), ("c", False)))
    dzz_l, dzz_c, da = _scan_g(ds_l, jnp.zeros_like(zz_c), coef, bwd_chains, True, st_l, st_c, name="l1_scan_bwd")
    dut_l, dut_c = _s5_dx(dyt, dzz_l, dzz_c, mt_b, bc_b)
    d_mt, d_bc, d_cct = _s5_dw(ut_l, ut_c, dyt, dzz_l, dzz_c, st_l)
    n_g = e // S5_P
    da = jnp.sum(da, axis=0).reshape(n_g // 2, 2, 2, 2, S5_N).transpose(1, 2, 0, 3, 4)
    da = da.reshape(2, 2, n_g, S5_N)
    d_pwb, d_pwc, d_bb, d_ccp, d_dm = _s5_weights_bwd(pwb, pwc, bbw, ccw, d_mt, d_bc, d_cct)
    g_s5 = s5_vjp((d_pwb, d_pwc, d_bb, d_ccp, d_dm, da[:, 0], da[:, 1]))

    vec_l = _pad8(jnp.concatenate([g0, bb0, 1.0 + sc[1][0:1]], axis=0))
    vec_c = _pad8(jnp.concatenate([g0, bb0, 1.0 + sc[1][1:2]], axis=0))
    dr1_l, acc_l = _bwd_inproj1_gt(dut_l, dz_l, wu_t, w_z, xh1_l, rs1_l, dr2, vec_l, "lat")
    dr1_c, acc_c = _bwd_inproj1_gt(dut_c, jnp.zeros((jc, CHUNK * e), BF), wu_t, w_z, xh1_c, rs1_c,
                                   jnp.zeros((jc, CHUNK * d), BF), vec_c, "ctx")
    mod_l = jnp.concatenate([a1[0:1], b1[0:1]], axis=0)
    mod_c = jnp.concatenate([a1[1:2], b1[1:2]], axis=0)
    g_ut_c = _dw_gt(dut_c, xh1_c, False, mod_c, False, None, F32, "l1_dw_in_u_ctx")
    g_ut = _dw_gt(dut_l, xh1_l, False, mod_l, False, g_ut_c, BF, "l1_dw_in_u")
    g_in_z = _dw_cr(xh1_l, dz_l, "mod", "cr", mod_l, False, None, "l1_dw_in_z")
    g_w_in1 = jnp.concatenate([g_ut.T, g_in_z], axis=1)

    dr1_ln, dr1_cn = _from_cr(dr1_l, d), _from_cr(dr1_c, d)
    dq3, acc_g0 = _bwd_outproj0(dr1_ln, dr1_cn, gt[0], w["conv_w_out"], fx, tm)
    sent1 = ["ssm_w_in", "ssm_w_glu", "ssm_w_out"]
    xch1 = _Exchange("scatter", [g_w_in1, g_w_glu, g_w_out], [BIG[n] for n in sent1]) if scatter else None
    dp42, dcw, recv1 = _conv_bwd(dq3, p42, cw, nl, tm, tc, xch1)
    g_w_in0 = _dw_inproj0(x, ctx, a0, b0, dp42, tm)
    g_w_out0 = _dw_outproj0(q3, dr1_ln, dr1_cn, gt[0], tm)
    sent0 = ["conv_w_in", "conv_w_out"]
    xch0 = _Exchange("scatter", [g_w_in0, g_w_out0], [BIG[n] for n in sent0]) if scatter else None
    grad_x, acc_0, recv0 = _bwd_inproj0(dp42, w["conv_w_in"], x, ctx, dr1_ln, dr1_cn, a0, tm, xch0)
    recv = dict(zip(sent1 + sent0, recv1 + recv0))

    zero = jnp.zeros((d,), F32)
    dm0 = jnp.stack([jnp.concatenate([acc_0[2], acc_0[0], acc_g0[0]]), jnp.concatenate([acc_0[3], acc_0[1], acc_g0[1]])])
    dm1 = jnp.stack([jnp.concatenate([acc_l[1], acc_l[0], acc_f[2]]), jnp.concatenate([acc_c[1], acc_c[0], zero])])
    if own_mod:
        g_ada_w, dc8 = _ada_bwd(c8, w["ada_w"], jnp.stack([_pad8(dm0), _pad8(dm1)]), BF)
        g_mod = {"c_ctx": dc8[0, 1] + dc8[1, 1], "ada_w": g_ada_w,
                 "ada_b": jnp.stack([dm0[0] + dm0[1], dm1[0] + dm1[1]])}
    else:
        g_mod = {"mod": jnp.stack([dm0, dm1])}

    grads = {
        **g_mod,
        "ln_g": jnp.stack([acc_l[2] + acc_c[2], acc_f[0]]),
        "ln_b": jnp.stack([acc_l[3] + acc_c[3], acc_f[1]]),
        "conv_w_in": g_w_in0, "conv_w": dcw[:3].reshape(3, e), "conv_w_out": g_w_out0,
        "ssm_w_in": g_w_in1,
        "ssm_lam_re": g_s5[0], "ssm_lam_im": g_s5[1], "ssm_log_step": g_s5[2],
        "ssm_b_re": g_s5[3], "ssm_b_im": g_s5[4], "ssm_c_re": g_s5[5], "ssm_c_im": g_s5[6], "ssm_d": g_s5[7],
        "ssm_w_glu": g_w_glu, "ssm_b_glu": g_b_glu, "ssm_w_out": g_w_out,
    }
    for n in recv:
        del grads[n]
    return loss, grad_x, grads, recv


WEIGHTS = ["c_ctx", "ada_w", "ada_b", "ln_g", "ln_b", "conv_w_in", "conv_w", "conv_w_out", "ssm_w_in",
           "ssm_lam_re", "ssm_lam_im", "ssm_log_step", "ssm_b_re", "ssm_b_im", "ssm_c_re", "ssm_c_im",
           "ssm_d", "ssm_w_glu", "ssm_b_glu", "ssm_w_out"]
BIG = {"ada_w": 1, "conv_w_in": 1, "conv_w_out": 0, "ssm_w_in": 1, "ssm_w_glu": 0, "ssm_w_out": 0}
SMALL_SHARDED = ["conv_w", "ssm_d", "ssm_b_glu"]
REPLICATED = ["c_ctx", "ada_b", "ln_g", "ln_b", "ssm_lam_re", "ssm_lam_im", "ssm_log_step",
              "ssm_b_re", "ssm_b_im", "ssm_c_re", "ssm_c_im"]
NATIVE_SMALL = ["ssm_b_re", "ssm_b_im", "ssm_c_re", "ssm_c_im"]


def _view2d(name, a):
    return a.reshape(-1, a.shape[-1])


def kernel(x, c, ctx, c_ctx, ada_w, ada_b, ln_g, ln_b, conv_w_in, conv_w, conv_w_out, ssm_w_in, ssm_lam_re, ssm_lam_im, ssm_log_step, ssm_b_re, ssm_b_im, ssm_c_re, ssm_c_im, ssm_d, ssm_w_glu, ssm_b_glu, ssm_w_out, loss_target, m_c_ctx, m_ada_w, m_ada_b, m_ln_g, m_ln_b, m_conv_w_in, m_conv_w, m_conv_w_out, m_ssm_w_in, m_ssm_lam_re, m_ssm_lam_im, m_ssm_log_step, m_ssm_b_re, m_ssm_b_im, m_ssm_c_re, m_ssm_c_im, m_ssm_d, m_ssm_w_glu, m_ssm_b_glu, m_ssm_w_out, v_c_ctx, v_ada_w, v_ada_b, v_ln_g, v_ln_b, v_conv_w_in, v_conv_w, v_conv_w_out, v_ssm_w_in, v_ssm_lam_re, v_ssm_lam_im, v_ssm_log_step, v_ssm_b_re, v_ssm_b_im, v_ssm_c_re, v_ssm_c_im, v_ssm_d, v_ssm_w_glu, v_ssm_b_glu, v_ssm_w_out):
    args = locals()
    wt = {n: args[n] for n in WEIGHTS}
    mt = {n: args["m_" + n] for n in WEIGHTS}
    vt = {n: args["v_" + n] for n in WEIGHTS}

    me = 4 * lax.axis_index("x") + 2 * lax.axis_index("y") + lax.axis_index("c")
    d = x.shape[-1]
    d3 = 3 * d
    wa = d3 // N_DEV

    big_names = [n for n in BIG if n != "ada_w"]
    shard = {n: _view2d(n, wt[n]).astype(BF) for n in big_names}
    small = jnp.concatenate([wt["conv_w"][0], wt["ssm_d"], wt["ssm_b_glu"]], axis=0)
    small = jnp.concatenate([small, jnp.zeros((3, small.shape[1]), F32)], axis=0)
    w_in_full, small_full, c_all = _all_gather([shard["conv_w_in"], small, _pad8(c)], [1, 1, 0], "gather_weights", "gather2")
    late = {n: (shard[n], BIG[n]) for n in big_names if n != "conv_w_in"}
    c16 = jnp.concatenate([c_all[::8], c_ctx[None], jnp.zeros((16 - N_DEV - 1, d), F32)], axis=0)
    ada_w_b = ada_w.astype(BF)
    ada_b_mine = lax.dynamic_slice_in_dim(ada_b, me * wa, wa, axis=1)
    mod_part = _ada_fwd(c16, ada_w_b, ada_b_mine)
    mod_all = _all_gather([mod_part.reshape(32, wa)], [1], "gather_mod")[0].reshape(2, 16, d3)
    mod = jnp.stack([lax.dynamic_index_in_dim(mod_all, me, axis=1, keepdims=False), mod_all[:, N_DEV]], axis=1)
    w = {
        "ln_g": ln_g, "ln_b": ln_b, "conv_w_in": w_in_full, "conv_w": small_full[0:3],
        "ssm_lam_re": ssm_lam_re[0], "ssm_lam_im": ssm_lam_im[0],
        "ssm_log_step": ssm_log_step[0], "ssm_b_re": ssm_b_re[0], "ssm_b_im": ssm_b_im[0],
        "ssm_c_re": ssm_c_re[0], "ssm_c_im": ssm_c_im[0], "ssm_d": small_full[3], "ssm_b_glu": small_full[4],
    }

    loss, grad_x, g, recv_big = _local_step(x[0], c[0], ctx[0], c_ctx, loss_target[0], w, late, True, mod)
    loss = lax.psum(loss, ("x", "y", "c"))

    dmod_all = _all_gather([_pad8(g["mod"].reshape(4, d3))], [0], "gather_dmod")[0].reshape(N_DEV, 8, d3)
    dmod_all = dmod_all[:, :4].reshape(N_DEV, 2, 2, d3)
    dm_ctx = dmod_all[0, :, 1]
    for p in range(1, N_DEV):
        dm_ctx = dm_ctx + dmod_all[p, :, 1]
    dm16 = jnp.concatenate([dmod_all[:, :, 0].transpose(1, 0, 2), dm_ctx[:, None], jnp.zeros((2, 16 - N_DEV - 1, d3), F32)], axis=1)
    g_ada_w, dc16 = _ada_bwd(c16, ada_w_b, lax.dynamic_slice_in_dim(dm16, me * wa, wa, axis=2), F32)
    g["c_ctx"] = dc16[0, N_DEV] + dc16[1, N_DEV]
    g_ada_b = jnp.sum(dm16, axis=1)

    blob_names = [n for n in REPLICATED if n != "ada_b"] + SMALL_SHARDED
    flat = jnp.concatenate([g[n].reshape(-1).astype(F32) for n in blob_names])
    nflat = flat.shape[0]
    rows = -(-nflat // (N_DEV * 128 * 8)) * 8
    flat = jnp.concatenate([flat, jnp.zeros((N_DEV * rows * 128 - nflat,), F32)]).reshape(N_DEV * rows, 128)
    blob_sum = _sum_partials(_all_to_all([flat], [0], "scatter_grads")[0])
    blob = _all_gather([blob_sum], [0], "gather_small_grads", "gather2")[0].reshape(-1)
    small_g, off = {"ada_b": g_ada_b}, 0
    for n in blob_names:
        shape = wt[n].shape if n in REPLICATED else (*wt[n].shape[:-1], wt[n].shape[-1] * N_DEV)
        size = math.prod(shape)
        small_g[n] = blob[off:off + size].reshape(shape)
        off += size
    for n in SMALL_SHARDED:
        size = wt[n].shape[-1]
        small_g[n] = lax.dynamic_slice_in_dim(small_g[n], me * size, size, axis=small_g[n].ndim - 1)

    out_g, out_d, out_m, out_v = {}, {}, {}, {}
    recv_big["ada_w"] = _view2d("ada_w", g_ada_w)[None]
    for n in BIG:
        stack = recv_big[n]
        shp = wt[n].shape
        res = _adamw(stack, _view2d(n, wt[n]), _view2d(n, mt[n]), _view2d(n, vt[n]), "adamw_" + n)
        out_g[n], out_d[n], out_m[n], out_v[n] = [r.reshape(shp) for r in res]
    for n in NATIVE_SMALL:
        shp = wt[n].shape
        v2 = lambda a: a.reshape(-1, shp[-1])
        res = _adamw(v2(small_g.pop(n))[None], v2(wt[n]), v2(mt[n]), v2(vt[n]), "adamw_" + n)
        out_g[n], out_d[n], out_m[n], out_v[n] = [r.reshape(shp) for r in res]
    names = list(small_g)
    cat = lambda t: jnp.concatenate([t[n].reshape(-1) for n in names])
    gs, ws, ms, vs = cat(small_g), cat(wt), cat(mt), cat(vt)
    ns = gs.shape[0]
    rs = -(-ns // (128 * 512)) * 512
    padr = lambda a: jnp.concatenate([a, jnp.ones((rs * 128 - ns,), F32)]).reshape(rs, 128)
    res = _adamw(padr(gs)[None], padr(ws), padr(ms), padr(vs), "adamw_small")
    off = 0
    for n in names:
        size = math.prod(wt[n].shape)
        out_g[n], out_d[n], out_m[n], out_v[n] = [r.reshape(-1)[off:off + size].reshape(wt[n].shape) for r in res]
        off += size

    return (loss, grad_x[None], *[out_g[n] for n in WEIGHTS], *[out_d[n] for n in WEIGHTS],
            *[out_m[n] for n in WEIGHTS], *[out_v[n] for n in WEIGHTS])
```

```python
import math

import jax
import jax.numpy as jnp
from jax import lax
from jax.experimental import pallas as pl
from jax.experimental.pallas import tpu as pltpu

F32 = jnp.float32
BF = jnp.bfloat16
MESH = pl.DeviceIdType.MESH
N_DEV = 8

GRID_W = 64
CHUNK = 16
S5_P = 16
S5_N = 64
LN_EPS = 1e-5
DN_ALPHA = 4.0 ** 0.25
ADAM_LR, ADAM_B1, ADAM_B2, ADAM_EPS, ADAM_WD, ADAM_STEP = 1e-3, 0.9, 0.999, 1e-8, 0.01, 10
GELU_C0 = math.sqrt(2.0 / math.pi)
GELU_C1 = 0.044715
VMEM_MB = 52

ANY = pl.BlockSpec(memory_space=pl.ANY)


def _cparams():
    return pltpu.CompilerParams(vmem_limit_bytes=VMEM_MB << 20)


def _dot(a, b):
    return jnp.dot(a, b, preferred_element_type=F32)


def _dot_nt(a, b):
    return lax.dot_general(a, b, (((1,), (1,)), ((), ())), preferred_element_type=F32)


def _dot_tn(a, b):
    return lax.dot_general(a, b, (((0,), (0,)), ((), ())), preferred_element_type=F32)


def _sigmoid(x):
    return 1.0 / (1.0 + jnp.exp(-x))


def _gelu_parts(y):
    th = jnp.tanh(GELU_C0 * (y + GELU_C1 * y * y * y))
    g = 0.5 * y * (1.0 + th)
    dg = 0.5 * (1.0 + th) + 0.5 * y * (1.0 - th * th) * GELU_C0 * (1.0 + 3.0 * GELU_C1 * y * y)
    return g, dg


def _full(shape):
    nd = len(shape)
    return pl.BlockSpec(shape, lambda *_: (0,) * nd)


def _mesh_pos():
    x, y, c = lax.axis_index("x"), lax.axis_index("y"), lax.axis_index("c")
    return x, y, c


def _peer(pos, k):
    x, y, c = pos
    px = 1 - x if (k >> 2) & 1 else x
    py = 1 - y if (k >> 1) & 1 else y
    pc = 1 - c if k & 1 else c
    return (px, py, pc), 4 * px + 2 * py + pc


def _shard_at(ref, axis, idx, n):
    if axis == 0:
        return ref.at[pl.ds(idx * n, n)]
    return ref.at[:, pl.ds(idx * n, n)]


class _Exchange:
    def __init__(self, kind, arrays, axes):
        self.kind, self.axes, self.n = kind, list(axes), len(arrays)
        self.arrays = list(arrays)
        self.out_shape = []
        for s, ax in zip(arrays, axes):
            shp = list(s.shape)
            if kind == "scatter":
                shp[ax] //= N_DEV
                self.out_shape.append(jax.ShapeDtypeStruct((N_DEV, *shp), s.dtype))
            else:
                shp[ax] *= N_DEV
                self.out_shape.append(jax.ShapeDtypeStruct(tuple(shp), s.dtype))
        self.scratch = [pltpu.SemaphoreType.DMA((self.n, N_DEV - 1)), pltpu.SemaphoreType.DMA((self.n, N_DEV - 1)),
                        pltpu.SemaphoreType.DMA((self.n,))]

    def _copies(self, ins, outs, sems):
        send_sems, recv_sems, local_sems = sems
        pos = _mesh_pos()
        x, y, c = pos
        me = 4 * x + 2 * y + c
        local, sends, chained, recvs = [], [], [], []
        for i in range(self.n):
            ax = self.axes[i]
            if self.kind == "scatter":
                size = ins[i].shape[ax] // N_DEV
                src = lambda idx, i=i, ax=ax, size=size: _shard_at(ins[i], ax, idx, size)
                dst = lambda idx, i=i: outs[i].at[idx]
            else:
                size = ins[i].shape[ax]
                src = lambda idx, i=i: ins[i]
                dst = lambda idx, i=i, ax=ax, size=size: _shard_at(outs[i], ax, idx, size)

            def copy(k, s, d, to, i=i):
                return pltpu.make_async_remote_copy(src_ref=s, dst_ref=d, send_sem=send_sems.at[i, k],
                                                    recv_sem=recv_sems.at[i, k], device_id=to, device_id_type=MESH)

            local.append(pltpu.make_async_copy(src(me), dst(me), local_sems.at[i]))
            if self.kind == "gather2":
                sib, sib_i = (x, y, 1 - c), 4 * x + 2 * y + (1 - c)
                chips = [(1 - x, y), (x, 1 - y), (1 - x, 1 - y)]
                sends.append(copy(0, src(me), dst(me), sib))
                recvs.append(copy(0, src(me), dst(sib_i), sib))
                for j, (cx, cy) in enumerate(chips):
                    same, other = 4 * cx + 2 * cy + c, 4 * cx + 2 * cy + (1 - c)
                    sends.append(copy(1 + j, src(me), dst(me), (cx, cy, c)))
                    chained.append((copy(1 + j, dst(same), dst(same), (cx, cy, c)), copy(4 + j, dst(same), dst(same), sib)))
                    recvs.append(copy(4 + j, dst(other), dst(other), sib))
            else:
                for k in range(1, N_DEV):
                    peer, pidx = _peer(pos, k)
                    out_src = src(pidx) if self.kind == "scatter" else src(me)
                    sends.append(copy(k - 1, out_src, dst(me), peer))
                    recvs.append(copy(k - 1, out_src, dst(pidx), peer))
        return local, sends, chained, recvs

    def start(self, ins, outs, sems):
        local, sends, _, _ = self._copies(ins, outs, sems)
        for cp in local + sends:
            cp.start()

    def wait(self, ins, outs, sems):
        local, sends, chained, recvs = self._copies(ins, outs, sems)
        for arrival, released in chained:
            arrival.wait_recv()
            released.start()
        for cp in recvs:
            cp.wait_recv()
        for cp in sends + [released for _, released in chained]:
            cp.wait_send()
        for cp in local:
            cp.wait()

    def run(self, name):
        n = self.n

        def body(*refs):
            ins, outs, sems = refs[:n], refs[n:2 * n], refs[2 * n:]
            self.start(ins, outs, sems)
            self.wait(ins, outs, sems)

        return pl.pallas_call(body, name=name, out_shape=self.out_shape, in_specs=[ANY] * n, out_specs=[ANY] * n,
                              scratch_shapes=self.scratch)(*self.arrays)


def _hosted_call(body, xch, grid, in_specs, out_specs, out_shape, scratch, args, name):
    out_specs, out_shape = list(out_specs), list(out_shape)
    n_in, n_out = len(in_specs), len(out_specs)
    if xch is None:
        res = pl.pallas_call(body, name=name, grid=grid, in_specs=in_specs, out_specs=out_specs, out_shape=out_shape,
                             scratch_shapes=list(scratch), compiler_params=_cparams())(*args)
        return list(res), []
    n = xch.n
    rank = len(grid)

    def wrapped(*refs):
        ins, x_ins = refs[:n_in], refs[n_in:n_in + n]
        outs = refs[n_in + n:n_in + n + n_out]
        x_outs = refs[n_in + n + n_out:n_in + 2 * n + n_out]
        rest = refs[n_in + 2 * n + n_out:]
        own, sems = rest[:len(rest) - 3], rest[len(rest) - 3:]
        ids = [pl.program_id(a) for a in range(rank)]
        first, last = ids[0] == 0, ids[0] == grid[0] - 1
        for a in range(1, rank):
            first = jnp.logical_and(first, ids[a] == 0)
            last = jnp.logical_and(last, ids[a] == grid[a] - 1)

        @pl.when(first)
        def _():
            xch.start(x_ins, x_outs, sems)

        body(*ins, *outs, *own)

        @pl.when(last)
        def _():
            xch.wait(x_ins, x_outs, sems)

    res = pl.pallas_call(
        wrapped, name=name, grid=grid, in_specs=list(in_specs) + [ANY] * n, out_specs=out_specs + [ANY] * n,
        out_shape=out_shape + xch.out_shape, scratch_shapes=list(scratch) + xch.scratch, compiler_params=_cparams(),
    )(*args, *xch.arrays)
    return list(res[:n_out]), list(res[n_out:])


def _all_gather(shards, axes, name, kind="gather"):
    return _Exchange(kind, shards, axes).run(name)


def _all_to_all(parts, axes, name):
    return _Exchange("scatter", parts, axes).run(name)


def _ada_fwd(cv, ada_w, ada_b):
    nl, d, wd = ada_w.shape
    r = cv.shape[0]

    def body(c_ref, w_ref, b_ref, o_ref):
        c = c_ref[...]
        s = (c * _sigmoid(c)).astype(BF)
        o_ref[0] = _dot(s, w_ref[0]) + b_ref[0]

    return pl.pallas_call(
        body, name="ada_fwd", grid=(nl,),
        in_specs=[_full((r, d)), pl.BlockSpec((1, d, wd), lambda l: (l, 0, 0)), pl.BlockSpec((1, 1, wd), lambda l: (l, 0, 0))],
        out_specs=pl.BlockSpec((1, r, wd), lambda l: (l, 0, 0)),
        out_shape=jax.ShapeDtypeStruct((nl, r, wd), F32), compiler_params=_cparams(),
    )(cv, ada_w, ada_b.reshape(nl, 1, wd))


def _ada_bwd(cv, ada_w, dm, out_dtype):
    nl, d, wd = ada_w.shape
    r = cv.shape[0]

    def body(c_ref, w_ref, dm_ref, dw_ref, dc_ref):
        c = c_ref[...]
        sg = _sigmoid(c)
        s = (c * sg).astype(BF)
        dmv = dm_ref[0].astype(BF)
        dw_ref[0] = _dot_tn(s, dmv).astype(out_dtype)
        dc_ref[0] = _dot_nt(dmv, w_ref[0]) * (sg * (1.0 + c * (1.0 - sg)))

    return pl.pallas_call(
        body, name="ada_bwd", grid=(nl,),
        in_specs=[_full((r, d)), pl.BlockSpec((1, d, wd), lambda l: (l, 0, 0)), pl.BlockSpec((1, r, wd), lambda l: (l, 0, 0))],
        out_specs=[pl.BlockSpec((1, d, wd), lambda l: (l, 0, 0)), pl.BlockSpec((1, r, d), lambda l: (l, 0, 0))],
        out_shape=[jax.ShapeDtypeStruct((nl, d, wd), out_dtype), jax.ShapeDtypeStruct((nl, r, d), F32)],
        compiler_params=_cparams(),
    )(cv, ada_w, dm)


def _sum_partials(stack):
    _, r, c = stack.shape

    def body(s_ref, o_ref):
        acc = s_ref[0]
        for p in range(1, N_DEV):
            acc = acc + s_ref[p]
        o_ref[...] = acc

    return pl.pallas_call(body, name="sum_partials", out_shape=jax.ShapeDtypeStruct((r, c), F32),
                          in_specs=[_full(stack.shape)], out_specs=_full((r, c)), grid=(1,),
                          compiler_params=_cparams())(stack)


def _adamw(gstack, w, m, v, name):
    p, r, c = gstack.shape
    tr = r
    for cand in (512 if c <= 256 else 256, 128, 64, 32, 16, 8):
        if r % cand == 0 and r > cand:
            tr = cand
            break
    bc1 = 1.0 - ADAM_B1 ** ADAM_STEP
    bc2 = 1.0 - ADAM_B2 ** ADAM_STEP

    def body(g_ref, w_ref, m_ref, v_ref, go_ref, d_ref, mo_ref, vo_ref):
        g = g_ref[0].astype(F32)
        for q in range(1, p):
            g = g + g_ref[q].astype(F32)
        mn = ADAM_B1 * m_ref[...] + (1.0 - ADAM_B1) * g
        vn = ADAM_B2 * v_ref[...] + (1.0 - ADAM_B2) * (g * g)
        go_ref[...] = g
        mo_ref[...] = mn
        vo_ref[...] = vn
        d_ref[...] = -ADAM_LR * ((mn / bc1) / (jnp.sqrt(vn / bc2) + ADAM_EPS) + ADAM_WD * w_ref[...])

    row = pl.BlockSpec((tr, c), lambda i: (i, 0))
    sds = jax.ShapeDtypeStruct((r, c), F32)
    return pl.pallas_call(
        body, name=name, grid=(r // tr,),
        in_specs=[pl.BlockSpec((p, tr, c), lambda i: (0, i, 0)), row, row, row],
        out_specs=[row, row, row, row], out_shape=[sds, sds, sds, sds], compiler_params=_cparams(),
    )(gstack, w, m, v)


def _lat_or_ctx_specs(tm, d, nl, grid_rank, row_axis):
    def lat(*ids):
        return (jnp.minimum(ids[row_axis], nl - 1), 0)

    def ctx(*ids):
        return (jnp.maximum(ids[row_axis] - nl, 0), 0)

    return pl.BlockSpec((tm, d), lat), pl.BlockSpec((tm, d), ctx)


def _sel_row(ref, is_ctx):
    return jnp.where(is_ctx, ref[1:2, :], ref[0:1, :])


def _inproj0(x, ctx, a2, b2, w, tm, xch=None):
    l, d = x.shape
    nl, nc = l // tm, ctx.shape[0] // tm
    e = w.shape[1] // 4
    half = e // 2

    def body(x_ref, c_ref, a_ref, b_ref, w_hbm, o_ref, w_ref):
        i = pl.program_id(0)

        @pl.when(i == 0)
        def _():
            pltpu.sync_copy(w_hbm, w_ref)

        is_ctx = i >= nl
        xv = jnp.where(is_ctx, c_ref[...], x_ref[...])
        h = (xv * _sel_row(a_ref, is_ctx) + _sel_row(b_ref, is_ctx)).astype(BF)
        for k in range(4):
            r = _dot(h, w_ref[:, k * e:(k + 1) * e])
            o_ref[k, 0] = r[:, :half].astype(BF)
            o_ref[k, 1] = r[:, half:].astype(BF)

    lat, cx = _lat_or_ctx_specs(tm, d, nl, 1, 0)
    (p42,), extra = _hosted_call(
        body, xch, grid=(nl + nc,),
        in_specs=[lat, cx, _full((2, d)), _full((2, d)), ANY],
        out_specs=[pl.BlockSpec((4, 2, tm, half), lambda i: (0, 0, i, 0))],
        out_shape=[jax.ShapeDtypeStruct((4, 2, l + ctx.shape[0], half), BF)],
        scratch=[pltpu.VMEM(w.shape, BF)], args=(x, ctx, a2, b2, w), name="l0_inproj")
    return p42, extra


def _conv_taps(u, w_up, w_mid, w_dn, pos, rl, tm):
    up = jnp.where(pos == 0, 0.0, pltpu.roll(u, 1, 0))
    dn = jnp.where(pos == rl - 1, 0.0, pltpu.roll(u, tm - 1, 0))
    return w_up * up + w_mid * u + w_dn * dn, up, dn


def _conv_halo_specs(tm, tc, nl, lead):
    hb = tm // GRID_W

    def prev(j, i):
        return (0, 1, jnp.maximum(jnp.minimum(i, nl - 1) * hb - 1, 0), j)

    def nxt(j, i):
        return (0, 1, jnp.minimum((jnp.minimum(i, nl - 1) + 1) * hb, nl * hb - 1), j)

    return pl.BlockSpec((lead, 1, GRID_W, tc), prev), pl.BlockSpec((lead, 1, GRID_W, tc), nxt)


def _conv_fwd(p42, cw, nl, tm, tc):
    _, _, r, half = p42.shape
    nt = r // tm

    def body(p_ref, hp_ref, hn_ref, cw_ref, o_ref):
        i = pl.program_id(1)
        is_ctx = i >= nl
        row = lax.broadcasted_iota(jnp.int32, (tm, tc), 0)
        rl = jnp.where(is_ctx, tm, GRID_W)
        pos = jnp.bitwise_and(row, rl - 1)

        def gate(hv, yc):
            bg = p_ref[0, hv].astype(F32)
            z = p_ref[3, hv].astype(F32)
            return (bg * yc * (z * _sigmoid(z))).astype(BF)

        u_h = p_ref[1, 0].astype(F32) * p_ref[2, 0].astype(F32)
        w_h = cw_ref[:, 0, :]
        o_ref[0] = gate(0, _conv_taps(u_h, w_h[0:1], w_h[1:2], w_h[2:3], pos, rl, tm)[0])
        u_v = p_ref[1, 1].astype(F32) * p_ref[2, 1].astype(F32)
        w_v = cw_ref[:, 1, :]

        @pl.when(is_ctx)
        def _():
            o_ref[1] = gate(1, _conv_taps(u_v, w_v[0:1], w_v[1:2], w_v[2:3], pos, rl, tm)[0])

        @pl.when(jnp.logical_not(is_ctx))
        def _():
            up = hp_ref[1, 0].astype(F32) * hp_ref[2, 0].astype(F32) * (i > 0).astype(F32)
            dn = hn_ref[1, 0].astype(F32) * hn_ref[2, 0].astype(F32) * (i < nl - 1).astype(F32)
            ext = jnp.concatenate([up, u_v, dn], axis=0)
            yc = w_v[0:1] * ext[0:tm] + w_v[1:2] * u_v + w_v[2:3] * ext[2 * GRID_W:tm + 2 * GRID_W]
            o_ref[1] = gate(1, yc)

    hp, hn = _conv_halo_specs(tm, tc, nl, 4)
    return pl.pallas_call(
        body, name="l0_conv_fwd", grid=(half // tc, nt),
        in_specs=[pl.BlockSpec((4, 2, tm, tc), lambda j, i: (0, 0, i, j)), hp, hn,
                  pl.BlockSpec((3, 2, tc), lambda j, i: (0, 0, j))],
        out_specs=pl.BlockSpec((2, tm, tc), lambda j, i: (0, i, j)),
        out_shape=jax.ShapeDtypeStruct((2, r, half), BF), compiler_params=_cparams(),
    )(p42, p42, p42, cw)


def _outproj_ln0(q3, w_out, x, ctx, gt2, tm):
    l, d = x.shape
    lc = ctx.shape[0]
    nl, nc = l // tm, lc // tm
    _, r, half = q3.shape
    tjo = tm // CHUNK

    def body(q_ref, w_hbm, x_ref, c_ref, g_ref, xl_ref, xc_ref, rl_ref, rc_ref, fx_ref, w_ref, xs_ref, rs_ref):
        i = pl.program_id(0)

        @pl.when(i == 0)
        def _():
            pltpu.sync_copy(w_hbm, w_ref)

        is_ctx = i >= nl
        fx = _dot(q_ref[0], w_ref[:half, :]) + _dot(q_ref[1], w_ref[half:, :])
        xv = jnp.where(is_ctx, c_ref[...], x_ref[...])
        rr = DN_ALPHA * xv + _sel_row(g_ref, is_ctx) * fx
        mu = jnp.mean(rr, axis=-1, keepdims=True)
        cen = rr - mu
        rstd = lax.rsqrt(jnp.mean(cen * cen, axis=-1, keepdims=True) + LN_EPS)
        xh = cen * rstd
        for lb in range(d // 128):
            xs_ref[lb] = xh[:, lb * 128:(lb + 1) * 128]
        rs_ref[...] = jnp.broadcast_to(rstd, (tm, 128))
        fx_ref[...] = fx.astype(BF)

        def to_cr(xo_ref, ro_ref):
            for s in range(CHUNK):
                for lb in range(d // 128):
                    xo_ref[:, s * d + lb * 128:s * d + (lb + 1) * 128] = xs_ref.at[lb][pl.ds(s, tjo, stride=CHUNK), :]
                ro_ref[:, s * 128:(s + 1) * 128] = rs_ref[pl.ds(s, tjo, stride=CHUNK), :]

        @pl.when(jnp.logical_not(is_ctx))
        def _():
            to_cr(xl_ref, rl_ref)

        @pl.when(is_ctx)
        def _():
            to_cr(xc_ref, rc_ref)

    lat, cx = _lat_or_ctx_specs(tm, d, nl, 1, 0)
    lat_o = lambda w_: pl.BlockSpec((tjo, CHUNK * w_), lambda i: (jnp.minimum(i, nl - 1), 0))
    ctx_o = lambda w_: pl.BlockSpec((tjo, CHUNK * w_), lambda i: (jnp.maximum(i - nl, 0), 0))
    return pl.pallas_call(
        body, name="l0_outproj_ln", grid=(nl + nc,),
        in_specs=[pl.BlockSpec((2, tm, half), lambda i: (0, i, 0)), ANY, lat, cx, _full((2, d))],
        out_specs=[lat_o(d), ctx_o(d), lat_o(128), ctx_o(128), pl.BlockSpec((tm, d), lambda i: (i, 0))],
        out_shape=[jax.ShapeDtypeStruct((l // CHUNK, CHUNK * d), F32), jax.ShapeDtypeStruct((lc // CHUNK, CHUNK * d), F32),
                   jax.ShapeDtypeStruct((l // CHUNK, CHUNK * 128), F32), jax.ShapeDtypeStruct((lc // CHUNK, CHUNK * 128), F32),
                   jax.ShapeDtypeStruct((r, d), BF)],
        scratch_shapes=[pltpu.VMEM(w_out.shape, BF), pltpu.VMEM((d // 128, tm, 128), F32), pltpu.VMEM((tm, 128), F32)],
        compiler_params=_cparams(),
    )(q3, w_out, x, ctx, gt2)


def _bwd_outproj0(dr_l, dr_c, gt2, w_out, fx, tm):
    l, d = dr_l.shape
    nl, nc = l // tm, dr_c.shape[0] // tm
    e = w_out.shape[0]
    half = e // 2
    r = l + dr_c.shape[0]

    def body(dl_ref, dc_ref, g_ref, w_hbm, fx_ref, dq_ref, acc_ref, w_ref):
        i = pl.program_id(0)

        @pl.when(i == 0)
        def _():
            pltpu.sync_copy(w_hbm, w_ref)
            acc_ref[...] = jnp.zeros_like(acc_ref)

        is_ctx = i >= nl
        dr = jnp.where(is_ctx, dc_ref[...], dl_ref[...]).astype(F32)
        dfx = (dr * _sel_row(g_ref, is_ctx)).astype(BF)
        dq_ref[0] = _dot_nt(dfx, w_ref[:half, :]).astype(BF)
        dq_ref[1] = _dot_nt(dfx, w_ref[half:, :]).astype(BF)
        s = jnp.sum(dr * fx_ref[...].astype(F32), axis=0, keepdims=True)
        sel = is_ctx.astype(F32)
        acc_ref[0:1, :] += s * (1.0 - sel)
        acc_ref[1:2, :] += s * sel

    lat, cx = _lat_or_ctx_specs(tm, d, nl, 1, 0)
    return pl.pallas_call(
        body, name="l0_bwd_outproj", grid=(nl + nc,),
        in_specs=[lat, cx, _full((2, d)), ANY, pl.BlockSpec((tm, d), lambda i: (i, 0))],
        out_specs=[pl.BlockSpec((2, tm, half), lambda i: (0, i, 0)), _full((8, d))],
        out_shape=[jax.ShapeDtypeStruct((2, r, half), BF), jax.ShapeDtypeStruct((8, d), F32)],
        scratch_shapes=[pltpu.VMEM(w_out.shape, BF)], compiler_params=_cparams(),
    )(dr_l, dr_c, gt2, w_out, fx)


def _conv_bwd(dq3, p42, cw, nl, tm, tc, xch=None):
    _, _, r, half = p42.shape
    nt = r // tm

    def body(dq_ref, dqp_ref, dqn_ref, p_ref, hp_ref, hn_ref, cw_ref, dp_ref, dw_ref):
        i = pl.program_id(1)
        is_ctx = i >= nl

        @pl.when(i == 0)
        def _():
            dw_ref[...] = jnp.zeros_like(dw_ref)

        row = lax.broadcasted_iota(jnp.int32, (tm, tc), 0)
        rl = jnp.where(is_ctx, tm, GRID_W)
        pos = jnp.bitwise_and(row, rl - 1)

        def pieces(dq, bg, z):
            sz = _sigmoid(z)
            sil = z * sz
            return dq * bg * sil, dq * sil, dq * bg * (sz * (1.0 + z * (1.0 - sz)))

        def seq_half(hv):
            bg, cg = p_ref[0, hv].astype(F32), p_ref[1, hv].astype(F32)
            v, z = p_ref[2, hv].astype(F32), p_ref[3, hv].astype(F32)
            w = cw_ref[:, hv, :]
            u = cg * v
            yc, u_up, u_dn = _conv_taps(u, w[0:1], w[1:2], w[2:3], pos, rl, tm)
            dyc, dbg_f, dz_f = pieces(dq_ref[hv].astype(F32), bg, z)
            du = _conv_taps(dyc, w[2:3], w[1:2], w[0:1], pos, rl, tm)[0]
            dp_ref[0, hv] = (dbg_f * yc).astype(BF)
            dp_ref[1, hv] = (du * v).astype(BF)
            dp_ref[2, hv] = (du * cg).astype(BF)
            dp_ref[3, hv] = (dz_f * yc).astype(BF)
            dw_ref[0:1, hv, :] += jnp.sum(dyc * u_up, axis=0, keepdims=True)
            dw_ref[1:2, hv, :] += jnp.sum(dyc * u, axis=0, keepdims=True)
            dw_ref[2:3, hv, :] += jnp.sum(dyc * u_dn, axis=0, keepdims=True)

        seq_half(0)

        @pl.when(is_ctx)
        def _():
            seq_half(1)

        @pl.when(jnp.logical_not(is_ctx))
        def _():
            bg, cg = p_ref[0, 1].astype(F32), p_ref[1, 1].astype(F32)
            v, z = p_ref[2, 1].astype(F32), p_ref[3, 1].astype(F32)
            w = cw_ref[:, 1, :]
            u = cg * v
            m_up = (i > 0).astype(F32)
            m_dn = (i < nl - 1).astype(F32)

            def halo(h_ref, dqh_ref, msk):
                hb, hc = h_ref[0, 0].astype(F32), h_ref[1, 0].astype(F32)
                hv_, hz = h_ref[2, 0].astype(F32), h_ref[3, 0].astype(F32)
                return hc * hv_ * msk, pieces(dqh_ref[0].astype(F32), hb, hz)[0] * msk

            u_p, dyc_p = halo(hp_ref, dqp_ref, m_up)
            u_n, dyc_n = halo(hn_ref, dqn_ref, m_dn)
            u_ext = jnp.concatenate([u_p, u, u_n], axis=0)
            u_up, u_dn = u_ext[0:tm], u_ext[2 * GRID_W:tm + 2 * GRID_W]
            yc = w[0:1] * u_up + w[1:2] * u + w[2:3] * u_dn
            dyc, dbg_f, dz_f = pieces(dq_ref[1].astype(F32), bg, z)
            d_ext = jnp.concatenate([dyc_p, dyc, dyc_n], axis=0)
            du = w[0:1] * d_ext[2 * GRID_W:tm + 2 * GRID_W] + w[1:2] * dyc + w[2:3] * d_ext[0:tm]
            dp_ref[0, 1] = (dbg_f * yc).astype(BF)
            dp_ref[1, 1] = (du * v).astype(BF)
            dp_ref[2, 1] = (du * cg).astype(BF)
            dp_ref[3, 1] = (dz_f * yc).astype(BF)
            dw_ref[0:1, 1, :] += jnp.sum(dyc * u_up, axis=0, keepdims=True)
            dw_ref[1:2, 1, :] += jnp.sum(dyc * u, axis=0, keepdims=True)
            dw_ref[2:3, 1, :] += jnp.sum(dyc * u_dn, axis=0, keepdims=True)

    hb = tm // GRID_W

    def dq_prev(j, i):
        return (1, jnp.maximum(jnp.minimum(i, nl - 1) * hb - 1, 0), j)

    def dq_next(j, i):
        return (1, jnp.minimum((jnp.minimum(i, nl - 1) + 1) * hb, nl * hb - 1), j)

    hp, hn = _conv_halo_specs(tm, tc, nl, 4)
    (dp42, dcw), extra = _hosted_call(
        body, xch, grid=(half // tc, nt),
        in_specs=[pl.BlockSpec((2, tm, tc), lambda j, i: (0, i, j)),
                  pl.BlockSpec((1, GRID_W, tc), dq_prev), pl.BlockSpec((1, GRID_W, tc), dq_next),
                  pl.BlockSpec((4, 2, tm, tc), lambda j, i: (0, 0, i, j)), hp, hn,
                  pl.BlockSpec((3, 2, tc), lambda j, i: (0, 0, j))],
        out_specs=[pl.BlockSpec((4, 2, tm, tc), lambda j, i: (0, 0, i, j)), pl.BlockSpec((8, 2, tc), lambda j, i: (0, 0, j))],
        out_shape=[jax.ShapeDtypeStruct(p42.shape, BF), jax.ShapeDtypeStruct((8, 2, half), F32)],
        scratch=[], args=(dq3, dq3, dq3, p42, p42, p42, cw), name="l0_conv_bwd")
    return dp42, dcw, extra


def _bwd_inproj0(dp42, w_in, x, ctx, dr_l, dr_c, a2, tm, xch=None):
    l, d = x.shape
    nl, nc = l // tm, ctx.shape[0] // tm
    e = w_in.shape[1] // 4
    half = e // 2

    def body(dp_ref, w_hbm, x_ref, c_ref, dl_ref, dc_ref, a_ref, gx_ref, acc_ref, w_ref):
        i = pl.program_id(0)

        @pl.when(i == 0)
        def _():
            pltpu.sync_copy(w_hbm, w_ref)
            acc_ref[...] = jnp.zeros_like(acc_ref)

        is_ctx = i >= nl
        dh = jnp.zeros((tm, d), F32)
        for k in range(4):
            for hv in range(2):
                c0 = k * e + hv * half
                dh = dh + _dot_nt(dp_ref[k, hv], w_ref[:, c0:c0 + half])
        xv = jnp.where(is_ctx, c_ref[...], x_ref[...])
        s_sc = jnp.sum(dh * xv, axis=0, keepdims=True)
        s_sh = jnp.sum(dh, axis=0, keepdims=True)
        sel = is_ctx.astype(F32)
        acc_ref[0:1, :] += s_sc * (1.0 - sel)
        acc_ref[1:2, :] += s_sc * sel
        acc_ref[2:3, :] += s_sh * (1.0 - sel)
        acc_ref[3:4, :] += s_sh * sel

        @pl.when(jnp.logical_not(is_ctx))
        def _():
            gx_ref[...] = DN_ALPHA * dl_ref[...].astype(F32) + dh * a_ref[0:1, :]

    lat, cx = _lat_or_ctx_specs(tm, d, nl, 1, 0)
    (gx, acc), extra = _hosted_call(
        body, xch, grid=(nl + nc,),
        in_specs=[pl.BlockSpec((4, 2, tm, half), lambda i: (0, 0, i, 0)), ANY, lat, cx, lat, cx, _full((2, d))],
        out_specs=[pl.BlockSpec((tm, d), lambda i: (jnp.minimum(i, nl - 1), 0)), _full((8, d))],
        out_shape=[jax.ShapeDtypeStruct((l, d), F32), jax.ShapeDtypeStruct((8, d), F32)],
        scratch=[pltpu.VMEM(w_in.shape, BF)], args=(dp42, w_in, x, ctx, dr_l, dr_c, a2), name="l0_bwd_inproj")
    return gx, acc, extra


def _dw_inproj0(x, ctx, a2, b2, dp42, tm):
    l, d = x.shape
    lc = ctx.shape[0]
    assert lc == tm
    tl = 4 * tm if l % (4 * tm) == 0 else tm
    nl = l // tl
    half = dp42.shape[-1]
    e = 2 * half

    def body(x_ref, c_ref, a_ref, b_ref, dpl_ref, dpc_ref, o_ref, acc_ref):
        i = pl.program_id(1)

        @pl.when(i == 0)
        def _():
            acc_ref[...] = jnp.zeros_like(acc_ref)

        def add(rows_ref, dp_ref, sel):
            h = (rows_ref[...] * a_ref[sel:sel + 1, :] + b_ref[sel:sel + 1, :]).astype(BF)
            acc_ref[:, :half] += _dot_tn(h, dp_ref[0, 0])
            acc_ref[:, half:] += _dot_tn(h, dp_ref[0, 1])

        @pl.when(i < nl)
        def _():
            add(x_ref, dpl_ref, 0)

        @pl.when(i == nl)
        def _():
            add(c_ref, dpc_ref, 1)
            o_ref[...] = acc_ref[...].astype(BF)

    return pl.pallas_call(
        body, name="l0_dw_inproj", grid=(4, nl + 1),
        in_specs=[pl.BlockSpec((tl, d), lambda k, i: (jnp.minimum(i, nl - 1), 0)), _full((lc, d)),
                  _full((2, d)), _full((2, d)),
                  pl.BlockSpec((1, 2, tl, half), lambda k, i: (k, 0, jnp.minimum(i, nl - 1), 0)),
                  pl.BlockSpec((1, 2, lc, half), lambda k, i: (k, 0, l // lc, 0))],
        out_specs=pl.BlockSpec((d, e), lambda k, i: (0, k)),
        out_shape=jax.ShapeDtypeStruct((d, 4 * e), BF),
        scratch_shapes=[pltpu.VMEM((d, e), F32)], compiler_params=_cparams(),
    )(x, ctx, a2, b2, dp42, dp42)


def _dw_outproj0(q3, dr_l, dr_c, gt2, tm):
    l, d = dr_l.shape
    nl, nc = l // tm, dr_c.shape[0] // tm
    _, r, half = q3.shape
    nt = nl + nc

    def body(q_ref, dl_ref, dc_ref, g_ref, o_ref, acc_ref):
        i = pl.program_id(0)
        is_ctx = i >= nl

        @pl.when(i == 0)
        def _():
            acc_ref[...] = jnp.zeros_like(acc_ref)

        dr = jnp.where(is_ctx, dc_ref[...], dl_ref[...]).astype(F32)
        dfx = (dr * _sel_row(g_ref, is_ctx)).astype(BF)
        acc_ref[:half, :] += _dot_tn(q_ref[0], dfx)
        acc_ref[half:, :] += _dot_tn(q_ref[1], dfx)

        @pl.when(i == nt - 1)
        def _():
            o_ref[...] = acc_ref[...].astype(BF)

    lat, cx = _lat_or_ctx_specs(tm, d, nl, 1, 0)
    return pl.pallas_call(
        body, name="l0_dw_outproj", grid=(nt,),
        in_specs=[pl.BlockSpec((2, tm, half), lambda i: (0, i, 0)), lat, cx, _full((2, d))],
        out_specs=_full((2 * half, d)), out_shape=jax.ShapeDtypeStruct((2 * half, d), BF),
        scratch_shapes=[pltpu.VMEM((2 * half, d), F32)], compiler_params=_cparams(),
    )(q3, dr_l, dr_c, gt2)


def _cr_tile(j, cap=256):
    for cand in (1024, 512, 256, 128, 64, 32, 16, 8):
        if cand <= cap and j % cand == 0:
            return cand
    raise ValueError(j)


def _final(w_cr, w_out, xh_cr, tgt_cr, vecs):
    j, e16 = w_cr.shape
    e = e16 // CHUNK
    d = w_out.shape[1]
    tj = _cr_tile(j)

    def body(w_ref, wo_hbm, xh_ref, t_ref, v_ref, dr_ref, acc_ref, wo_ref):
        @pl.when(jnp.logical_and(pl.program_id(0) == 0, pl.program_id(1) == 0))
        def _():
            pltpu.sync_copy(wo_hbm, wo_ref)
            acc_ref[...] = jnp.zeros_like(acc_ref)

        o = _dot(w_ref[...], wo_ref[...])
        x1 = xh_ref[...] * v_ref[0:1, :] + v_ref[1:2, :]
        rr = DN_ALPHA * x1 + v_ref[2:3, :] * o
        mu = jnp.mean(rr, axis=-1, keepdims=True)
        cen = rr - mu
        rstd = lax.rsqrt(jnp.mean(cen * cen, axis=-1, keepdims=True) + LN_EPS)
        xh2 = cen * rstd
        err = xh2 * v_ref[3:4, :] + v_ref[4:5, :] - t_ref[...]
        dy = err * (1.0 / d)
        dxh = dy * v_ref[3:4, :]
        dr = rstd * (dxh - jnp.mean(dxh, axis=-1, keepdims=True) - xh2 * jnp.mean(dxh * xh2, axis=-1, keepdims=True))
        dr_ref[...] = dr.astype(BF)
        acc_ref[0:1, :] += jnp.sum(dy * xh2, axis=0, keepdims=True)
        acc_ref[1:2, :] += jnp.sum(dy, axis=0, keepdims=True)
        acc_ref[2:3, :] += jnp.sum(dr * o, axis=0, keepdims=True)
        acc_ref[3:4, :] += (0.5 / d) * jnp.sum(err * err, axis=0, keepdims=True)

    tok_d = pl.BlockSpec((tj, d), lambda t, s: (t, s))
    return pl.pallas_call(
        body, name="l1_final", grid=(j // tj, CHUNK),
        in_specs=[pl.BlockSpec((tj, e), lambda t, s: (t, s)), ANY, tok_d, tok_d, _full((8, d))],
        out_specs=[tok_d, _full((8, d))],
        out_shape=[jax.ShapeDtypeStruct((j, CHUNK * d), BF), jax.ShapeDtypeStruct((8, d), F32)],
        scratch_shapes=[pltpu.VMEM(w_out.shape, BF)], compiler_params=_cparams(),
    )(w_cr, w_out, xh_cr, tgt_cr, vecs)


def _dw_cr(lhs, rhs, lhs_kind, rhs_kind, vec, bias_sum, init, name):
    j = lhs.shape[0]
    k = lhs.shape[1] // CHUNK
    n = rhs.shape[1] // CHUNK
    tj = _cr_tile(j, 512)
    nh = 2 if k * n * 4 > (8 << 20) else 1
    tn = n // nh
    nt = j // tj
    has_init = init is not None

    def body(*refs):
        refs = list(refs)
        l_ref, r_ref = refs[0], refs[1]
        pos = 2
        v_ref = None
        if vec is not None:
            v_ref = refs[pos]
            pos += 1
        i_ref = None
        if has_init:
            i_ref = refs[pos]
            pos += 1
        o_ref = refs[pos]
        pos += 1
        bs_ref = None
        if bias_sum:
            bs_ref = refs[pos]
            pos += 1
        acc_ref = refs[pos]
        t, s = pl.program_id(1), pl.program_id(2)
        first = jnp.logical_and(t == 0, s == 0)

        @pl.when(first)
        def _():
            acc_ref[...] = i_ref[...] if has_init else jnp.zeros_like(acc_ref)
            if bias_sum:
                bs_ref[...] = jnp.zeros_like(bs_ref)

        if lhs_kind == "mod":
            lv = (l_ref[...] * v_ref[0:1, :] + v_ref[1:2, :]).astype(BF)
        else:
            lv = l_ref[...]
        if rhs_kind == "scaled":
            rv = (r_ref[...].astype(F32) * v_ref[0:1, :]).astype(BF)
        else:
            rv = r_ref[...]
        acc_ref[...] += _dot_tn(lv, rv)
        if bias_sum:
            bs_ref[0:1, :] += jnp.sum(rv.astype(F32), axis=0, keepdims=True)

        @pl.when(jnp.logical_and(t == nt - 1, s == CHUNK - 1))
        def _():
            o_ref[...] = acc_ref[...].astype(BF)

    l_spec = pl.BlockSpec((tj, k), lambda h, t, s: (t, s))
    r_spec = pl.BlockSpec((tj, tn), lambda h, t, s: (t, s * nh + h))
    in_specs, args = [l_spec, r_spec], [lhs, rhs]
    if vec is not None:
        in_specs.append(_full(vec.shape))
        args.append(vec)
    o_spec = pl.BlockSpec((k, tn), lambda h, t, s: (0, h))
    if has_init:
        in_specs.append(o_spec)
        args.append(init)
    out_specs, out_shape = [o_spec], [jax.ShapeDtypeStruct((k, n), BF)]
    if bias_sum:
        out_specs.append(pl.BlockSpec((8, tn), lambda h, t, s: (0, h)))
        out_shape.append(jax.ShapeDtypeStruct((8, n), F32))
    res = pl.pallas_call(
        body, name=name, grid=(nh, nt, CHUNK), in_specs=in_specs, out_specs=out_specs, out_shape=out_shape,
        scratch_shapes=[pltpu.VMEM((k, tn), F32)], compiler_params=_cparams(),
    )(*args)
    return res if bias_sum else res[0]


GT_ROWS = CHUNK * S5_P
ZG_W = 2 * 2 * S5_N
PAIR_W = 2 * ZG_W
GROUPS_PER_STEP = 4


def _inproj1_gt(xh_cr, a1, b1, wu_t, w_z, tag):
    j, d16 = xh_cr.shape
    d = d16 // CHUNK
    e = wu_t.shape[0]
    g = e // S5_P
    tj = _cr_tile(j, 256)

    def body(x_ref, a_ref, b_ref, wu_hbm, wz_hbm, u_ref, z_ref, wu_ref, wz_ref):
        @pl.when(jnp.logical_and(pl.program_id(0) == 0, pl.program_id(1) == 0))
        def _():
            pltpu.sync_copy(wu_hbm, wu_ref)
            pltpu.sync_copy(wz_hbm, wz_ref)

        h = (x_ref[...] * a_ref[...] + b_ref[...]).astype(BF)
        u_ref[...] = _dot_nt(wu_ref[...], h).reshape(g, S5_P, tj).astype(BF)
        z_ref[...] = _dot(h, wz_ref[...]).astype(BF)

    return pl.pallas_call(
        body, name="l1_inproj_" + tag, grid=(j // tj, CHUNK),
        in_specs=[pl.BlockSpec((tj, d), lambda t, s: (t, s)), _full((1, d)), _full((1, d)), ANY, ANY],
        out_specs=[pl.BlockSpec((g, S5_P, tj), lambda t, s: (0, s, t)), pl.BlockSpec((tj, e), lambda t, s: (t, s))],
        out_shape=[jax.ShapeDtypeStruct((g, GT_ROWS, j), BF), jax.ShapeDtypeStruct((j, CHUNK * e), BF)],
        scratch_shapes=[pltpu.VMEM(wu_t.shape, BF), pltpu.VMEM(w_z.shape, BF)], compiler_params=_cparams(),
    )(xh_cr, a1, b1, wu_t, w_z)


def _gt_spec(j, gb=GROUPS_PER_STEP):
    return pl.BlockSpec((gb, GT_ROWS, j), lambda i: (i, 0, 0))


def _zg_spec(j, gb=GROUPS_PER_STEP):
    return pl.BlockSpec((j, gb * ZG_W), lambda i: (0, i))


def _w_spec(width, gb=GROUPS_PER_STEP):
    return pl.BlockSpec((gb, GT_ROWS, width), lambda i: (i, 0, 0))


def _pair_lanes(k):
    return slice((k // 2) * PAIR_W, (k // 2 + 1) * PAIR_W)


def _s5_z(ut_l, ut_c, bc):
    g, _, jl = ut_l.shape
    jc = ut_c.shape[2]
    gb = GROUPS_PER_STEP

    def body(ul_ref, uc_ref, bc_ref, zl_ref, zc_ref):
        for k in range(0, gb, 2):
            zl_ref[:, _pair_lanes(k)] = _dot_tn(ul_ref[k], bc_ref[k]) + _dot_tn(ul_ref[k + 1], bc_ref[k + 1])
            zc_ref[:, _pair_lanes(k)] = _dot_tn(uc_ref[k], bc_ref[k]) + _dot_tn(uc_ref[k + 1], bc_ref[k + 1])

    return pl.pallas_call(
        body, name="l1_s5_z", grid=(g // gb,), in_specs=[_gt_spec(jl), _gt_spec(jc), _w_spec(PAIR_W)],
        out_specs=[_zg_spec(jl), _zg_spec(jc)],
        out_shape=[jax.ShapeDtypeStruct((jl, g * ZG_W), F32), jax.ShapeDtypeStruct((jc, g * ZG_W), F32)],
        compiler_params=_cparams(),
    )(ut_l, ut_c, bc)


def _s5_y(ut_l, s_l, mt_t, cct):
    g, _, jl = ut_l.shape
    gb = GROUPS_PER_STEP

    def body(u_ref, s_ref, mt_ref, cc_ref, y_ref):
        for k in range(gb):
            s_k = s_ref[:, _pair_lanes(k)].astype(BF)
            y_ref[k] = (_dot(mt_ref[k], u_ref[k]) + _dot_nt(cc_ref[k], s_k)).astype(BF)

    return pl.pallas_call(
        body, name="l1_s5_y", grid=(g // gb,),
        in_specs=[_gt_spec(jl), _zg_spec(jl), _w_spec(GT_ROWS), _w_spec(PAIR_W)],
        out_specs=_gt_spec(jl), out_shape=jax.ShapeDtypeStruct((g, GT_ROWS, jl), BF), compiler_params=_cparams(),
    )(ut_l, s_l, mt_t, cct)


def _s5_ds(dyt_l, cct):
    g, _, jl = dyt_l.shape
    gb = GROUPS_PER_STEP

    def body(dy_ref, cc_ref, ds_ref):
        for k in range(0, gb, 2):
            ds_ref[:, _pair_lanes(k)] = _dot_tn(dy_ref[k], cc_ref[k]) + _dot_tn(dy_ref[k + 1], cc_ref[k + 1])

    return pl.pallas_call(
        body, name="l1_s5_ds", grid=(g // gb,), in_specs=[_gt_spec(jl), _w_spec(PAIR_W)], out_specs=_zg_spec(jl),
        out_shape=jax.ShapeDtypeStruct((jl, g * ZG_W), F32), compiler_params=_cparams(),
    )(dyt_l, cct)


def _s5_dx(dyt_l, dz_l, dz_c, mt, bc):
    g, _, jl = dyt_l.shape
    jc = dz_c.shape[0]
    gb = GROUPS_PER_STEP

    def body(dy_ref, dzl_ref, dzc_ref, mt_ref, bc_ref, dul_ref, duc_ref):
        for k in range(gb):
            dzl = dzl_ref[:, _pair_lanes(k)].astype(BF)
            dzc = dzc_ref[:, _pair_lanes(k)].astype(BF)
            dul_ref[k] = (_dot(mt_ref[k], dy_ref[k]) + _dot_nt(bc_ref[k], dzl)).astype(BF)
            duc_ref[k] = _dot_nt(bc_ref[k], dzc).astype(BF)

    return pl.pallas_call(
        body, name="l1_s5_dx", grid=(g // gb,),
        in_specs=[_gt_spec(jl), _zg_spec(jl), _zg_spec(jc), _w_spec(GT_ROWS), _w_spec(PAIR_W)],
        out_specs=[_gt_spec(jl), _gt_spec(jc)],
        out_shape=[jax.ShapeDtypeStruct((g, GT_ROWS, jl), BF), jax.ShapeDtypeStruct((g, GT_ROWS, jc), BF)],
        compiler_params=_cparams(),
    )(dyt_l, dz_l, dz_c, mt, bc)


def _s5_dw(ut_l, ut_c, dyt_l, dz_l, dz_c, s_l):
    g, _, jl = ut_l.shape
    jc = ut_c.shape[2]
    gb = GROUPS_PER_STEP

    def body(ul_ref, uc_ref, dy_ref, dzl_ref, dzc_ref, s_ref, dmt_ref, dbc_ref, dcc_ref):
        for k in range(gb):
            lanes = _pair_lanes(k)
            dmt_ref[k] = _dot_nt(ul_ref[k], dy_ref[k])
            dbc_ref[k] = (_dot(ul_ref[k], dzl_ref[:, lanes].astype(BF))
                          + _dot(uc_ref[k], dzc_ref[:, lanes].astype(BF)))
            dcc_ref[k] = _dot(dy_ref[k], s_ref[:, lanes].astype(BF))

    sd_m = jax.ShapeDtypeStruct((g, GT_ROWS, GT_ROWS), F32)
    sd_p = jax.ShapeDtypeStruct((g, GT_ROWS, PAIR_W), F32)
    return pl.pallas_call(
        body, name="l1_s5_dw", grid=(g // gb,),
        in_specs=[_gt_spec(jl), _gt_spec(jc), _gt_spec(jl), _zg_spec(jl), _zg_spec(jc), _zg_spec(jl)],
        out_specs=[_w_spec(GT_ROWS), _w_spec(PAIR_W), _w_spec(PAIR_W)], out_shape=[sd_m, sd_p, sd_p],
        compiler_params=_cparams(),
    )(ut_l, ut_c, dyt_l, dz_l, dz_c, s_l)


def _scan_g(z_l, z_c, coef, chains, conj, s_l=None, s_c=None, name="l1_scan"):
    jl, w_all = z_l.shape
    jc = z_c.shape[0]
    gb = 2 * GROUPS_PER_STEP if w_all % (2 * GROUPS_PER_STEP * ZG_W) == 0 else GROUPS_PER_STEP
    wb = gb * ZG_W
    nch = wb // 256
    with_da = s_l is not None
    sign = -1.0 if conj else 1.0

    def body(*refs):
        zl_ref, zc_ref, cf_ref = refs[:3]
        k0 = 3
        if with_da:
            sl_ref, sc_ref = refs[3:5]
            k0 = 5
        ol_ref, oc_ref = refs[k0:k0 + 2]
        rowi = lax.broadcasted_iota(jnp.int32, (8, 128), 0)

        def lanes_of(ch):
            return slice(ch * 256, ch * 256 + 128), slice(ch * 256 + 128, (ch + 1) * 256)

        def coefs(ch, r0, nr):
            lr, li = lanes_of(ch)
            return cf_ref[r0:r0 + nr, lr], sign * cf_ref[r0:r0 + nr, li]

        def shift(v, sh, rev):
            if rev:
                return jnp.where(rowi < 8 - sh, pltpu.roll(v, 8 - sh, 0), 0.0)
            return jnp.where(rowi >= sh, pltpu.roll(v, sh, 0), 0.0)

        zero_row = jnp.zeros((1, 128), F32)
        zero_tile = jnp.zeros((8, 128), F32)
        carry = [zero_row] * (2 * nch)
        da = [zero_tile] * (2 * nch)
        for seg in range(len(chains[0])):
            which = chains[0][seg][0]
            assert chains[1][seg][0] == which
            revs = (chains[0][seg][1], chains[1][seg][1])
            src, dst = (zc_ref, oc_ref) if which == "c" else (zl_ref, ol_ref)
            sref = ((sc_ref if which == "c" else sl_ref) if with_da else None)
            ng = (jc if which == "c" else jl) // 8

            def step(it, st, src=src, dst=dst, sref=sref, ng=ng, revs=revs):
                carry_, da_ = list(st[:2 * nch]), list(st[2 * nch:])
                for ch in range(nch):
                    rev = revs[ch % 2]
                    lr, li = lanes_of(ch)
                    grp = (ng - 1 - it) if rev else it
                    off = pl.multiple_of(grp * 8, 8)
                    xr, xi = src[pl.ds(off, 8), lr], src[pl.ds(off, 8), li]
                    for sh, r0 in ((1, 0), (2, 1), (4, 2)):
                        ar, ai = coefs(ch, r0, 1)
                        sr, si = shift(xr, sh, rev), shift(xi, sh, rev)
                        xr, xi = xr + ar * sr - ai * si, xi + ar * si + ai * sr
                    tr, ti = coefs(ch, 16, 8) if rev else coefs(ch, 8, 8)
                    cr_, ci_ = carry_[2 * ch], carry_[2 * ch + 1]
                    ir = xr + tr * cr_ - ti * ci_
                    ii = xi + tr * ci_ + ti * cr_
                    if rev:
                        er = jnp.where(rowi == 7, cr_, pltpu.roll(ir, 7, 0))
                        ei = jnp.where(rowi == 7, ci_, pltpu.roll(ii, 7, 0))
                        carry_[2 * ch], carry_[2 * ch + 1] = ir[0:1], ii[0:1]
                    else:
                        er = jnp.where(rowi == 0, cr_, pltpu.roll(ir, 1, 0))
                        ei = jnp.where(rowi == 0, ci_, pltpu.roll(ii, 1, 0))
                        carry_[2 * ch], carry_[2 * ch + 1] = ir[7:8], ii[7:8]
                    dst[pl.ds(off, 8), lr] = er
                    dst[pl.ds(off, 8), li] = ei
                    if sref is not None:
                        s_r, s_i = sref[pl.ds(off, 8), lr], sref[pl.ds(off, 8), li]
                        da_[2 * ch] = da_[2 * ch] + s_r * er + s_i * ei
                        da_[2 * ch + 1] = da_[2 * ch + 1] + s_r * ei - s_i * er
                return (*carry_, *da_)

            st = lax.fori_loop(0, ng, step, (*carry, *da))
            carry, da = list(st[:2 * nch]), list(st[2 * nch:])
        if with_da:
            da_ref = refs[k0 + 2]
            for ch in range(nch):
                lr, li = lanes_of(ch)
                da_ref[:, lr] = da[2 * ch]
                da_ref[:, li] = da[2 * ch + 1]

    in_specs = [_zg_spec(jl, gb), _zg_spec(jc, gb), pl.BlockSpec((24, wb), lambda i: (0, i))]
    args = [z_l, z_c, coef]
    out_specs = [_zg_spec(jl, gb), _zg_spec(jc, gb)]
    out_shape = [jax.ShapeDtypeStruct(z_l.shape, F32), jax.ShapeDtypeStruct(z_c.shape, F32)]
    if with_da:
        in_specs += [_zg_spec(jl, gb), _zg_spec(jc, gb)]
        args += [s_l, s_c]
        out_specs.append(pl.BlockSpec((8, wb), lambda i: (0, i)))
        out_shape.append(jax.ShapeDtypeStruct((8, w_all), F32))
    return pl.pallas_call(body, name=name, grid=(w_all // wb,), in_specs=in_specs, out_specs=out_specs,
                          out_shape=out_shape, compiler_params=_cparams())(*args)


def _gt_tok_spec(g, tj):
    return pl.BlockSpec((g, S5_P, tj), lambda t, s: (0, s, t))


def _glu_fwd_gt(yt, z_cr, w_glu, b_glu):
    g, _, j = yt.shape
    e = g * S5_P
    tj = _cr_tile(j)

    def body(y_ref, z_ref, w_hbm, b_ref, o_ref, sg_ref, w_ref):
        @pl.when(jnp.logical_and(pl.program_id(0) == 0, pl.program_id(1) == 0))
        def _():
            pltpu.sync_copy(w_hbm, w_ref)

        y = jnp.transpose(y_ref[...].reshape(e, tj).astype(F32))
        gl = _gelu_parts(y)[0]
        sg = _sigmoid(_dot(gl.astype(BF), w_ref[...]) + b_ref[...])
        z = z_ref[...].astype(F32)
        o_ref[...] = (gl * sg * (z * _sigmoid(z))).astype(BF)
        sg_ref[...] = sg.astype(BF)

    tok = pl.BlockSpec((tj, e), lambda t, s: (t, s))
    return pl.pallas_call(
        body, name="l1_glu_fwd", grid=(j // tj, CHUNK),
        in_specs=[_gt_tok_spec(g, tj), tok, ANY, _full((1, e))], out_specs=[tok, tok],
        out_shape=[jax.ShapeDtypeStruct((j, CHUNK * e), BF), jax.ShapeDtypeStruct((j, CHUNK * e), BF)],
        scratch_shapes=[pltpu.VMEM(w_glu.shape, BF)], compiler_params=_cparams(),
    )(yt, z_cr, w_glu, b_glu)


def _glu_bwd_gt(dr_cr, gt1, w_out, w_glu, yt, z_cr, sg_cr):
    g, _, j = yt.shape
    e, d = w_out.shape
    tj = _cr_tile(j)

    def body(dr_ref, g_ref, wo_hbm, wg_hbm, y_ref, z_ref, sg_ref, dz_ref, dt_ref, dy_ref, wo_ref, wg_ref):
        @pl.when(jnp.logical_and(pl.program_id(0) == 0, pl.program_id(1) == 0))
        def _():
            pltpu.sync_copy(wo_hbm, wo_ref)
            pltpu.sync_copy(wg_hbm, wg_ref)

        do = (dr_ref[...].astype(F32) * g_ref[...]).astype(BF)
        dw = _dot_nt(do, wo_ref[...])
        y = jnp.transpose(y_ref[...].reshape(e, tj).astype(F32))
        gl, dgel = _gelu_parts(y)
        z = z_ref[...].astype(F32)
        sz = _sigmoid(z)
        sg = sg_ref[...].astype(F32)
        dg2 = dw * (z * sz)
        dz_ref[...] = (dw * gl * sg * (sz * (1.0 + z * (1.0 - sz)))).astype(BF)
        dt = (dg2 * gl * sg * (1.0 - sg)).astype(BF)
        dt_ref[...] = dt
        dy = (dg2 * sg + _dot_nt(dt, wg_ref[...])) * dgel
        dy_ref[...] = jnp.transpose(dy).reshape(g, S5_P, tj).astype(BF)

    tok_e = pl.BlockSpec((tj, e), lambda t, s: (t, s))
    return pl.pallas_call(
        body, name="l1_glu_bwd", grid=(j // tj, CHUNK),
        in_specs=[pl.BlockSpec((tj, d), lambda t, s: (t, s)), _full((1, d)), ANY, ANY, _gt_tok_spec(g, tj), tok_e, tok_e],
        out_specs=[tok_e, tok_e, _gt_tok_spec(g, tj)],
        out_shape=[jax.ShapeDtypeStruct((j, CHUNK * e), BF), jax.ShapeDtypeStruct((j, CHUNK * e), BF),
                   jax.ShapeDtypeStruct((g, GT_ROWS, j), BF)],
        scratch_shapes=[pltpu.VMEM(w_out.shape, BF), pltpu.VMEM(w_glu.shape, BF)], compiler_params=_cparams(),
    )(dr_cr, gt1, w_out, w_glu, yt, z_cr, sg_cr)


def _bwd_inproj1_gt(dut, dz_cr, wu_t, w_z, xh_cr, rs_cr, dr2_cr, vecs, tag):
    g, _, j = dut.shape
    e, d = wu_t.shape
    tj = _cr_tile(j)

    def body(du_ref, dz_ref, wu_hbm, wz_hbm, xh_ref, rs_ref, dr2_ref, v_ref, dr1_ref, acc_ref, wu_ref, wz_ref):
        @pl.when(jnp.logical_and(pl.program_id(0) == 0, pl.program_id(1) == 0))
        def _():
            pltpu.sync_copy(wu_hbm, wu_ref)
            pltpu.sync_copy(wz_hbm, wz_ref)
            acc_ref[...] = jnp.zeros_like(acc_ref)

        dh = _dot_tn(du_ref[...].reshape(e, tj), wu_ref[...]) + _dot_nt(dz_ref[...], wz_ref[...])
        xh = xh_ref[...]
        x1 = xh * v_ref[0:1, :] + v_ref[1:2, :]
        dx1 = DN_ALPHA * dr2_ref[...].astype(F32) + dh * v_ref[2:3, :]
        dxh = dx1 * v_ref[0:1, :]
        rstd = rs_ref[:, 0:1]
        dr1 = rstd * (dxh - jnp.mean(dxh, axis=-1, keepdims=True) - xh * jnp.mean(dxh * xh, axis=-1, keepdims=True))
        dr1_ref[...] = dr1.astype(BF)
        acc_ref[0:1, :] += jnp.sum(dh * x1, axis=0, keepdims=True)
        acc_ref[1:2, :] += jnp.sum(dh, axis=0, keepdims=True)
        acc_ref[2:3, :] += jnp.sum(dx1 * xh, axis=0, keepdims=True)
        acc_ref[3:4, :] += jnp.sum(dx1, axis=0, keepdims=True)

    tok_d = pl.BlockSpec((tj, d), lambda t, s: (t, s))
    return pl.pallas_call(
        body, name="l1_bwd_inproj_" + tag, grid=(j // tj, CHUNK),
        in_specs=[_gt_tok_spec(g, tj), pl.BlockSpec((tj, e), lambda t, s: (t, s)), ANY, ANY, tok_d,
                  pl.BlockSpec((tj, 128), lambda t, s: (t, s)), tok_d, _full((8, d))],
        out_specs=[tok_d, _full((8, d))],
        out_shape=[jax.ShapeDtypeStruct((j, CHUNK * d), BF), jax.ShapeDtypeStruct((8, d), F32)],
        scratch_shapes=[pltpu.VMEM(wu_t.shape, BF), pltpu.VMEM(w_z.shape, BF)], compiler_params=_cparams(),
    )(dut, dz_cr, wu_t, w_z, xh_cr, rs_cr, dr2_cr, vecs)


def _dw_gt(lhs_gt, rhs_cr, lhs_gelu, vec, bias_sum, init, out_dtype, name):
    g, _, j = lhs_gt.shape
    e = g * S5_P
    n = rhs_cr.shape[1] // CHUNK
    tj = _cr_tile(j, 512 if j % 512 == 0 else 256)
    nh = 2 if e * n * 4 > (8 << 20) else 1
    tn = n // nh
    nt = j // tj
    has_init = init is not None

    def body(*refs):
        refs = list(refs)
        l_ref, r_ref = refs[0], refs[1]
        pos = 2
        v_ref = i_ref = bs_ref = None
        if vec is not None:
            v_ref = refs[pos]
            pos += 1
        if has_init:
            i_ref = refs[pos]
            pos += 1
        o_ref = refs[pos]
        pos += 1
        if bias_sum:
            bs_ref = refs[pos]
            pos += 1
        acc_ref = refs[pos]
        t, s = pl.program_id(1), pl.program_id(2)

        @pl.when(jnp.logical_and(t == 0, s == 0))
        def _():
            acc_ref[...] = i_ref[...] if has_init else jnp.zeros_like(acc_ref)
            if bias_sum:
                bs_ref[...] = jnp.zeros_like(bs_ref)

        lv = l_ref[...].reshape(e, tj)
        if lhs_gelu:
            lv = _gelu_parts(lv.astype(F32))[0].astype(BF)
        if vec is not None:
            rv = (r_ref[...] * v_ref[0:1, :] + v_ref[1:2, :]).astype(BF)
        else:
            rv = r_ref[...]
        acc_ref[...] += _dot(lv, rv)
        if bias_sum:
            bs_ref[0:1, :] += jnp.sum(rv.astype(F32), axis=0, keepdims=True)

        @pl.when(jnp.logical_and(t == nt - 1, s == CHUNK - 1))
        def _():
            o_ref[...] = acc_ref[...].astype(out_dtype)

    in_specs = [pl.BlockSpec((g, S5_P, tj), lambda h, t, s: (0, s, t)),
                pl.BlockSpec((tj, tn), lambda h, t, s: (t, s * nh + h))]
    args = [lhs_gt, rhs_cr]
    if vec is not None:
        in_specs.append(_full(vec.shape))
        args.append(vec)
    o_spec = pl.BlockSpec((e, tn), lambda h, t, s: (0, h))
    if has_init:
        in_specs.append(o_spec)
        args.append(init)
    out_specs, out_shape = [o_spec], [jax.ShapeDtypeStruct((e, n), out_dtype)]
    if bias_sum:
        out_specs.append(pl.BlockSpec((8, tn), lambda h, t, s: (0, h)))
        out_shape.append(jax.ShapeDtypeStruct((8, n), F32))
    res = pl.pallas_call(
        body, name=name, grid=(nh, nt, CHUNK), in_specs=in_specs, out_specs=out_specs, out_shape=out_shape,
        scratch_shapes=[pltpu.VMEM((e, tn), F32)], compiler_params=_cparams(),
    )(*args)
    return res if bias_sum else res[0]


def _scan_coef_g(lam_re, lam_im, log_step):
    g = lam_re.shape[1]
    ms = jnp.array([1, 2, 4, 0, 0, 0, 0, 0] + list(range(1, 9)) + list(range(8, 0, -1)), F32) * CHUNK
    dt = jnp.exp(log_step)[..., None]
    mag = jnp.exp(ms.reshape(-1, 1, 1, 1) * (lam_re * dt)[None])
    ang = ms.reshape(-1, 1, 1, 1) * (lam_im * dt)[None]
    cr, ci = mag * jnp.cos(ang), mag * jnp.sin(ang)
    both = jnp.stack([cr, ci], axis=2).reshape(24, 2, 2, g // 2, 2, S5_N)
    return both.transpose(0, 3, 1, 2, 4, 5).reshape(24, g * ZG_W)


def _s5_small(lam_re, lam_im, log_step, b_re, b_im, c_re, c_im, d_skip):
    g = lam_re.shape[1]
    t, p = CHUNK, S5_P
    dt = jnp.exp(log_step)[..., None]
    ks = jnp.arange(t + 1, dtype=F32).reshape(t + 1, 1, 1, 1)
    mag = jnp.exp(ks * (lam_re * dt)[None])
    ang = ks * (lam_im * dt)[None]
    pr, pi = mag * jnp.cos(ang), mag * jnp.sin(ang)
    ar, ai = pr[1], pi[1]
    qr, qi = ar - 1.0, ai
    den = lam_re * lam_re + lam_im * lam_im
    fr = (qr * lam_re + qi * lam_im) / den
    fi = (qi * lam_re - qr * lam_im) / den
    bt_re, bt_im = b_re.transpose(0, 1, 3, 2), b_im.transpose(0, 1, 3, 2)
    bbr = fr[:, :, None, :] * bt_re - fi[:, :, None, :] * bt_im
    bbi = fr[:, :, None, :] * bt_im + fi[:, :, None, :] * bt_re
    lay = lambda a_r, a_i: jnp.stack([a_r, a_i], axis=0).transpose(3, 2, 0, 1, 4)
    by_dir = lambda a, f0, f1: jnp.stack([f0(a[:, 0]), f1(a[:, 1])], axis=1)
    rev = lambda a: jnp.flip(a, axis=0)
    same = lambda a: a
    pwb = lay(by_dir(pr[:t], rev, same), by_dir(pi[:t], rev, same))
    pwc = lay(by_dir(pr[1:], same, rev), by_dir(pi[1:], same, rev))
    bb = jnp.stack([bbr, bbi], axis=0).transpose(2, 1, 0, 3, 4)
    cc = jnp.stack([c_re, c_im], axis=0).transpose(2, 1, 0, 3, 4)
    dmat = jnp.eye(p, dtype=F32)[None] * d_skip.reshape(g, p)[:, :, None]
    return pwb, pwc, bb, cc, dmat, pr[t], pi[t]


def _pair_cols(r, ri, g2):
    c0 = (r * 2 + ri) * 128 + g2 * S5_N
    return slice(c0, c0 + S5_N)


def _rows_rep(a):
    return jnp.broadcast_to(a[:, None, :], (CHUNK, S5_P, a.shape[-1])).reshape(GT_ROWS, a.shape[-1])


def _rows_tile(a):
    return jnp.broadcast_to(a[None], (CHUNK, S5_P, a.shape[-1])).reshape(GT_ROWS, a.shape[-1])


def _sum_blocks(a):
    return jnp.sum(a.reshape(CHUNK, S5_P, a.shape[-1]), axis=0)


def _sum_in_blocks(a):
    return jnp.sum(a.reshape(CHUNK, S5_P, a.shape[-1]), axis=1)


def _ab_rows(pwb_ref, bb_ref, k, r):
    prs, pis = _rows_rep(pwb_ref[k, r, 0]), _rows_rep(pwb_ref[k, r, 1])
    bbr, bbi = _rows_tile(bb_ref[k, r, 0]), _rows_tile(bb_ref[k, r, 1])
    return prs * bbr - pis * bbi, prs * bbi + pis * bbr, prs, pis, bbr, bbi


def _s5_weights_fwd(pwb, pwc, bb, cc, dmat):
    g = pwb.shape[0]
    gb = GROUPS_PER_STEP
    hp = lax.Precision.HIGHEST

    def body(pwb_ref, pwc_ref, bb_ref, cc_ref, dm_ref, mt_ref, mtt_ref, bc_ref, cct_ref):
        zeros = jnp.zeros((GT_ROWS, S5_N), BF)
        nt = (((1,), (1,)), ((), ()))
        for k in range(gb):
            g2 = k % 2
            kds = []
            for r in range(2):
                for ri in range(2):
                    bc_ref[k, :, _pair_cols(r, ri, 1 - g2)] = zeros
                    cct_ref[k, :, _pair_cols(r, ri, 1 - g2)] = zeros
                abr, abi = _ab_rows(pwb_ref, bb_ref, k, r)[:2]
                bc_ref[k, :, _pair_cols(r, 0, g2)] = abr.astype(BF)
                bc_ref[k, :, _pair_cols(r, 1, g2)] = abi.astype(BF)
                cr, ci = cc_ref[k, r, 0], cc_ref[k, r, 1]
                crt, cit = _rows_tile(cr), _rows_tile(ci)
                prt, pit = _rows_rep(pwc_ref[k, r, 0]), _rows_rep(pwc_ref[k, r, 1])
                cct_ref[k, :, _pair_cols(r, 0, g2)] = (crt * prt - cit * pit).astype(BF)
                cct_ref[k, :, _pair_cols(r, 1, g2)] = (-(crt * pit + cit * prt)).astype(BF)
                kds.append(lax.dot_general(abr, cr, nt, precision=hp, preferred_element_type=F32)
                           - lax.dot_general(abi, ci, nt, precision=hp, preferred_element_type=F32))
            blk = lambda a, s: a[s * S5_P:(s + 1) * S5_P]
            last = CHUNK - 1
            pieces = [blk(kds[1], last - i) for i in range(last)]
            pieces.append(blk(kds[0], last) + blk(kds[1], 0) + dm_ref[k])
            pieces += [blk(kds[0], last - d) for d in range(1, CHUNK)]
            qrow = jnp.concatenate(pieces, axis=1)
            mt = jnp.concatenate([qrow[:, (last - s) * S5_P:(last - s) * S5_P + GT_ROWS] for s in range(CHUNK)], axis=0)
            mt_ref[k] = mt.astype(BF)
            mtt_ref[k] = jnp.transpose(mt).astype(BF)

    small = lambda a: pl.BlockSpec((gb, *a.shape[1:]), lambda i: (i,) + (0,) * (a.ndim - 1))
    return pl.pallas_call(
        body, name="l1_s5_weights", grid=(g // gb,),
        in_specs=[small(pwb), small(pwc), small(bb), small(cc), small(dmat)],
        out_specs=[_w_spec(GT_ROWS), _w_spec(GT_ROWS), _w_spec(PAIR_W), _w_spec(PAIR_W)],
        out_shape=[jax.ShapeDtypeStruct((g, GT_ROWS, GT_ROWS), BF), jax.ShapeDtypeStruct((g, GT_ROWS, GT_ROWS), BF),
                   jax.ShapeDtypeStruct((g, GT_ROWS, PAIR_W), BF), jax.ShapeDtypeStruct((g, GT_ROWS, PAIR_W), BF)],
        compiler_params=_cparams(),
    )(pwb, pwc, bb, cc, dmat)


def _s5_weights_bwd(pwb, pwc, bb, cc, d_mt, d_bc, d_cct):
    g = pwb.shape[0]
    gb = GROUPS_PER_STEP
    hp = lax.Precision.HIGHEST

    def body(pwb_ref, pwc_ref, bb_ref, cc_ref, dmt_ref, dbc_ref, dcc_ref, dpwb_ref, dpwc_ref, dbb_ref, dccp_ref, ddm_ref):
        tn = (((0,), (0,)), ((), ()))
        nn = (((1,), (0,)), ((), ()))
        last = CHUNK - 1
        for k in range(gb):
            g2 = k % 2
            dq = None
            for s in range(CHUNK):
                parts = [dmt_ref[k, s * S5_P:(s + 1) * S5_P, :]]
                if s < last:
                    parts.insert(0, jnp.zeros((S5_P, (last - s) * S5_P), F32))
                if s > 0:
                    parts.append(jnp.zeros((S5_P, s * S5_P), F32))
                padded = jnp.concatenate(parts, axis=1) if len(parts) > 1 else parts[0]
                dq = padded if dq is None else dq + padded
            dblk = lambda d: dq[:, (last + d) * S5_P:(CHUNK + d) * S5_P]
            ddm_ref[k] = dblk(0)
            dkds = [jnp.concatenate([dblk(last - s) for s in range(CHUNK)], axis=0),
                    jnp.concatenate([dblk(-s) for s in range(CHUNK)], axis=0)]
            for r in range(2):
                abr, abi, prs, pis, bbr, bbi = _ab_rows(pwb_ref, bb_ref, k, r)
                cr, ci = cc_ref[k, r, 0], cc_ref[k, r, 1]
                dcr = lax.dot_general(dkds[r], abr, tn, precision=hp, preferred_element_type=F32)
                dci = -lax.dot_general(dkds[r], abi, tn, precision=hp, preferred_element_type=F32)
                dabr = (lax.dot_general(dkds[r], cr, nn, precision=hp, preferred_element_type=F32)
                        + dbc_ref[k, :, _pair_cols(r, 0, g2)])
                dabi = (-lax.dot_general(dkds[r], ci, nn, precision=hp, preferred_element_type=F32)
                        + dbc_ref[k, :, _pair_cols(r, 1, g2)])
                dbb_ref[k, r, 0] = _sum_blocks(prs * dabr + pis * dabi)
                dbb_ref[k, r, 1] = _sum_blocks(prs * dabi - pis * dabr)
                dpwb_ref[k, r, 0] = _sum_in_blocks(dabr * bbr + dabi * bbi)
                dpwb_ref[k, r, 1] = _sum_in_blocks(dabi * bbr - dabr * bbi)
                crt, cit = _rows_tile(cr), _rows_tile(ci)
                prt, pit = _rows_rep(pwc_ref[k, r, 0]), _rows_rep(pwc_ref[k, r, 1])
                d_re = dcc_ref[k, :, _pair_cols(r, 0, g2)]
                d_im = dcc_ref[k, :, _pair_cols(r, 1, g2)]
                dccp_ref[k, r, 0] = dcr + _sum_blocks(d_re * prt - d_im * pit)
                dccp_ref[k, r, 1] = dci - _sum_blocks(d_re * pit + d_im * prt)
                dpwc_ref[k, r, 0] = _sum_in_blocks(d_re * crt - d_im * cit)
                dpwc_ref[k, r, 1] = -_sum_in_blocks(d_re * cit + d_im * crt)

    small = lambda a: pl.BlockSpec((gb, *a.shape[1:]), lambda i: (i,) + (0,) * (a.ndim - 1))
    dmat_sds = jax.ShapeDtypeStruct((g, S5_P, S5_P), F32)
    return pl.pallas_call(
        body, name="l1_s5_weights_bwd", grid=(g // gb,),
        in_specs=[small(pwb), small(pwc), small(bb), small(cc), _w_spec(GT_ROWS), _w_spec(PAIR_W), _w_spec(PAIR_W)],
        out_specs=[small(pwb), small(pwc), small(bb), small(cc), small(dmat_sds)],
        out_shape=[jax.ShapeDtypeStruct(pwb.shape, F32), jax.ShapeDtypeStruct(pwc.shape, F32),
                   jax.ShapeDtypeStruct(bb.shape, F32), jax.ShapeDtypeStruct(cc.shape, F32), dmat_sds],
        compiler_params=_cparams(),
    )(pwb, pwc, bb, cc, d_mt, d_bc, d_cct)


def _to_cr(a):
    return a.reshape(a.shape[0] // CHUNK, CHUNK * a.shape[1])


def _from_cr(a, c):
    return a.reshape(a.shape[0] * CHUNK, c)


def _pad8(v):
    return jnp.concatenate([v, jnp.zeros((8 - v.shape[0], v.shape[1]), v.dtype)], axis=0)


def _local_step(x, c, ctx, c_ctx, loss_target, w, late=None, scatter=False, mod=None):
    l, d = x.shape
    lc = ctx.shape[0]
    tm = min(256, lc)
    assert lc == tm and l % tm == 0 and tm % GRID_W == 0 and (tm & (tm - 1)) == 0
    nl = l // tm

    own_mod = mod is None
    if own_mod:
        c8 = _pad8(jnp.stack([c, c_ctx]))
        mod = _ada_fwd(c8, w["ada_w"], w["ada_b"])
    sh = mod[:, :2, :d]
    sc = mod[:, :2, d:2 * d]
    gt = mod[:, :2, 2 * d:]
    ln_g, ln_b = w["ln_g"], w["ln_b"]

    a0, b0 = 1.0 + sc[0], sh[0]
    xch = _Exchange("gather2", [late[n][0] for n in late], [late[n][1] for n in late]) if late else None
    p42, got = _inproj0(x, ctx, a0, b0, w["conv_w_in"], tm, xch)
    if late:
        w = dict(w, **dict(zip(late, got)))
    e = w["conv_w_out"].shape[0]
    half = e // 2
    tc = min(512, half)
    cw = w["conv_w"].reshape(3, 2, half)
    q3 = _conv_fwd(p42, cw, nl, tm, half)
    xh1_l, xh1_c, rs1_l, rs1_c, fx = _outproj_ln0(q3, w["conv_w_out"], x, ctx, gt[0], tm)
    jl, jc = l // CHUNK, lc // CHUNK

    g0, bb0 = ln_g[0:1], ln_b[0:1]
    a1 = g0 * (1.0 + sc[1])
    b1 = bb0 * (1.0 + sc[1]) + sh[1]
    wu_t = w["ssm_w_in"][:, :e].T
    w_z = w["ssm_w_in"][:, e:]
    ut_l, z_l = _inproj1_gt(xh1_l, a1[0:1], b1[0:1], wu_t, w_z, "lat")
    ut_c, _ = _inproj1_gt(xh1_c, a1[1:2], b1[1:2], wu_t, w_z, "ctx")
    s5 = (w["ssm_lam_re"], w["ssm_lam_im"], w["ssm_log_step"], w["ssm_b_re"], w["ssm_b_im"],
          w["ssm_c_re"], w["ssm_c_im"], w["ssm_d"])
    (pwb, pwc, bbw, ccw, dmat, _, _), s5_vjp = jax.vjp(_s5_small, *s5)
    mt_b, mtt_b, bc_b, cct_b = _s5_weights_fwd(pwb, pwc, bbw, ccw, dmat)
    coef = lax.stop_gradient(_scan_coef_g(*s5[:3]))
    zz_l, zz_c = _s5_z(ut_l, ut_c, bc_b)
    fwd_chains = ((("c", False), ("l", False)), (("c", True), ("l", True)))
    st_l, st_c = _scan_g(zz_l, zz_c, coef, fwd_chains, False, name="l1_scan_fwd")
    yt = _s5_y(ut_l, st_l, mtt_b, cct_b)
    b_glu = w["ssm_b_glu"].reshape(1, e)
    w_cr, sg_cr = _glu_fwd_gt(yt, z_l, w["ssm_w_glu"], b_glu)
    vec_f = _pad8(jnp.concatenate([g0, bb0, gt[1][0:1], ln_g[1:2], ln_b[1:2]], axis=0))
    dr2, acc_f = _final(w_cr, w["ssm_w_out"], xh1_l, _to_cr(loss_target), vec_f)
    loss = jnp.sum(acc_f[3])

    gt1 = gt[1][0:1]
    dz_l, dt_l, dyt = _glu_bwd_gt(dr2, gt1, w["ssm_w_out"], w["ssm_w_glu"], yt, z_l, sg_cr)
    g_w_out = _dw_cr(w_cr, dr2, "cr", "scaled", gt1, False, None, "l1_dw_out")
    g_w_glu, bsum = _dw_gt(yt, dt_l, True, None, True, None, BF, "l1_dw_glu")
    g_b_glu = bsum[0]
    ds_l = _s5_ds(dyt, cct_b)
    bwd_chains = ((("l", True), ("c", True)), (("l", False), ("c", False)))
    dzz_l, dzz_c, da = _scan_g(ds_l, jnp.zeros_like(zz_c), coef, bwd_chains, True, st_l, st_c, name="l1_scan_bwd")
    dut_l, dut_c = _s5_dx(dyt, dzz_l, dzz_c, mt_b, bc_b)
    d_mt, d_bc, d_cct = _s5_dw(ut_l, ut_c, dyt, dzz_l, dzz_c, st_l)
    n_g = e // S5_P
    da = jnp.sum(da, axis=0).reshape(n_g // 2, 2, 2, 2, S5_N).transpose(1, 2, 0, 3, 4)
    da = da.reshape(2, 2, n_g, S5_N)
    d_pwb, d_pwc, d_bb, d_ccp, d_dm = _s5_weights_bwd(pwb, pwc, bbw, ccw, d_mt, d_bc, d_cct)
    g_s5 = s5_vjp((d_pwb, d_pwc, d_bb, d_ccp, d_dm, da[:, 0], da[:, 1]))

    vec_l = _pad8(jnp.concatenate([g0, bb0, 1.0 + sc[1][0:1]], axis=0))
    vec_c = _pad8(jnp.concatenate([g0, bb0, 1.0 + sc[1][1:2]], axis=0))
    dr1_l, acc_l = _bwd_inproj1_gt(dut_l, dz_l, wu_t, w_z, xh1_l, rs1_l, dr2, vec_l, "lat")
    dr1_c, acc_c = _bwd_inproj1_gt(dut_c, jnp.zeros((jc, CHUNK * e), BF), wu_t, w_z, xh1_c, rs1_c,
                                   jnp.zeros((jc, CHUNK * d), BF), vec_c, "ctx")
    mod_l = jnp.concatenate([a1[0:1], b1[0:1]], axis=0)
    mod_c = jnp.concatenate([a1[1:2], b1[1:2]], axis=0)
    g_ut_c = _dw_gt(dut_c, xh1_c, False, mod_c, False, None, F32, "l1_dw_in_u_ctx")
    g_ut = _dw_gt(dut_l, xh1_l, False, mod_l, False, g_ut_c, BF, "l1_dw_in_u")
    g_in_z = _dw_cr(xh1_l, dz_l, "mod", "cr", mod_l, False, None, "l1_dw_in_z")
    g_w_in1 = jnp.concatenate([g_ut.T, g_in_z], axis=1)

    dr1_ln, dr1_cn = _from_cr(dr1_l, d), _from_cr(dr1_c, d)
    dq3, acc_g0 = _bwd_outproj0(dr1_ln, dr1_cn, gt[0], w["conv_w_out"], fx, tm)
    sent1 = ["ssm_w_in", "ssm_w_glu", "ssm_w_out"]
    xch1 = _Exchange("scatter", [g_w_in1, g_w_glu, g_w_out], [BIG[n] for n in sent1]) if scatter else None
    dp42, dcw, recv1 = _conv_bwd(dq3, p42, cw, nl, tm, tc, xch1)
    g_w_in0 = _dw_inproj0(x, ctx, a0, b0, dp42, tm)
    g_w_out0 = _dw_outproj0(q3, dr1_ln, dr1_cn, gt[0], tm)
    sent0 = ["conv_w_in", "conv_w_out"]
    xch0 = _Exchange("scatter", [g_w_in0, g_w_out0], [BIG[n] for n in sent0]) if scatter else None
    grad_x, acc_0, recv0 = _bwd_inproj0(dp42, w["conv_w_in"], x, ctx, dr1_ln, dr1_cn, a0, tm, xch0)
    recv = dict(zip(sent1 + sent0, recv1 + recv0))

    zero = jnp.zeros((d,), F32)
    dm0 = jnp.stack([jnp.concatenate([acc_0[2], acc_0[0], acc_g0[0]]), jnp.concatenate([acc_0[3], acc_0[1], acc_g0[1]])])
    dm1 = jnp.stack([jnp.concatenate([acc_l[1], acc_l[0], acc_f[2]]), jnp.concatenate([acc_c[1], acc_c[0], zero])])
    if own_mod:
        g_ada_w, dc8 = _ada_bwd(c8, w["ada_w"], jnp.stack([_pad8(dm0), _pad8(dm1)]), BF)
        g_mod = {"c_ctx": dc8[0, 1] + dc8[1, 1], "ada_w": g_ada_w,
                 "ada_b": jnp.stack([dm0[0] + dm0[1], dm1[0] + dm1[1]])}
    else:
        g_mod = {"mod": jnp.stack([dm0, dm1])}

    grads = {
        **g_mod,
        "ln_g": jnp.stack([acc_l[2] + acc_c[2], acc_f[0]]),
        "ln_b": jnp.stack([acc_l[3] + acc_c[3], acc_f[1]]),
        "conv_w_in": g_w_in0, "conv_w": dcw[:3].reshape(3, e), "conv_w_out": g_w_out0,
        "ssm_w_in": g_w_in1,
        "ssm_lam_re": g_s5[0], "ssm_lam_im": g_s5[1], "ssm_log_step": g_s5[2],
        "ssm_b_re": g_s5[3], "ssm_b_im": g_s5[4], "ssm_c_re": g_s5[5], "ssm_c_im": g_s5[6], "ssm_d": g_s5[7],
        "ssm_w_glu": g_w_glu, "ssm_b_glu": g_b_glu, "ssm_w_out": g_w_out,
    }
    for n in recv:
        del grads[n]
    return loss, grad_x, grads, recv


WEIGHTS = ["c_ctx", "ada_w", "ada_b", "ln_g", "ln_b", "conv_w_in", "conv_w", "conv_w_out", "ssm_w_in",
           "ssm_lam_re", "ssm_lam_im", "ssm_log_step", "ssm_b_re", "ssm_b_im", "ssm_c_re", "ssm_c_im",
           "ssm_d", "ssm_w_glu", "ssm_b_glu", "ssm_w_out"]
BIG = {"ada_w": 1, "conv_w_in": 1, "conv_w_out": 0, "ssm_w_in": 1, "ssm_w_glu": 0, "ssm_w_out": 0}
SMALL_SHARDED = ["conv_w", "ssm_d", "ssm_b_glu"]
REPLICATED = ["c_ctx", "ada_b", "ln_g", "ln_b", "ssm_lam_re", "ssm_lam_im", "ssm_log_step",
              "ssm_b_re", "ssm_b_im", "ssm_c_re", "ssm_c_im"]
NATIVE_SMALL = ["ssm_b_re", "ssm_b_im", "ssm_c_re", "ssm_c_im"]


def _view2d(name, a):
    return a.reshape(-1, a.shape[-1])


def kernel(x, c, ctx, c_ctx, ada_w, ada_b, ln_g, ln_b, conv_w_in, conv_w, conv_w_out, ssm_w_in, ssm_lam_re, ssm_lam_im, ssm_log_step, ssm_b_re, ssm_b_im, ssm_c_re, ssm_c_im, ssm_d, ssm_w_glu, ssm_b_glu, ssm_w_out, loss_target, m_c_ctx, m_ada_w, m_ada_b, m_ln_g, m_ln_b, m_conv_w_in, m_conv_w, m_conv_w_out, m_ssm_w_in, m_ssm_lam_re, m_ssm_lam_im, m_ssm_log_step, m_ssm_b_re, m_ssm_b_im, m_ssm_c_re, m_ssm_c_im, m_ssm_d, m_ssm_w_glu, m_ssm_b_glu, m_ssm_w_out, v_c_ctx, v_ada_w, v_ada_b, v_ln_g, v_ln_b, v_conv_w_in, v_conv_w, v_conv_w_out, v_ssm_w_in, v_ssm_lam_re, v_ssm_lam_im, v_ssm_log_step, v_ssm_b_re, v_ssm_b_im, v_ssm_c_re, v_ssm_c_im, v_ssm_d, v_ssm_w_glu, v_ssm_b_glu, v_ssm_w_out):
    args = locals()
    wt = {n: args[n] for n in WEIGHTS}
    mt = {n: args["m_" + n] for n in WEIGHTS}
    vt = {n: args["v_" + n] for n in WEIGHTS}

    me = 4 * lax.axis_index("x") + 2 * lax.axis_index("y") + lax.axis_index("c")
    d = x.shape[-1]
    d3 = 3 * d
    wa = d3 // N_DEV

    big_names = [n for n in BIG if n != "ada_w"]
    shard = {n: _view2d(n, wt[n]).astype(BF) for n in big_names}
    small = jnp.concatenate([wt["conv_w"][0], wt["ssm_d"], wt["ssm_b_glu"]], axis=0)
    small = jnp.concatenate([small, jnp.zeros((3, small.shape[1]), F32)], axis=0)
    w_in_full, small_full, c_all = _all_gather([shard["conv_w_in"], small, _pad8(c)], [1, 1, 0], "gather_weights", "gather2")
    late = {n: (shard[n], BIG[n]) for n in big_names if n != "conv_w_in"}
    c16 = jnp.concatenate([c_all[::8], c_ctx[None], jnp.zeros((16 - N_DEV - 1, d), F32)], axis=0)
    ada_w_b = ada_w.astype(BF)
    ada_b_mine = lax.dynamic_slice_in_dim(ada_b, me * wa, wa, axis=1)
    mod_part = _ada_fwd(c16, ada_w_b, ada_b_mine)
    mod_all = _all_gather([mod_part.reshape(32, wa)], [1], "gather_mod")[0].reshape(2, 16, d3)
    mod = jnp.stack([lax.dynamic_index_in_dim(mod_all, me, axis=1, keepdims=False), mod_all[:, N_DEV]], axis=1)
    w = {
        "ln_g": ln_g, "ln_b": ln_b, "conv_w_in": w_in_full, "conv_w": small_full[0:3],
        "ssm_lam_re": ssm_lam_re[0], "ssm_lam_im": ssm_lam_im[0],
        "ssm_log_step": ssm_log_step[0], "ssm_b_re": ssm_b_re[0], "ssm_b_im": ssm_b_im[0],
        "ssm_c_re": ssm_c_re[0], "ssm_c_im": ssm_c_im[0], "ssm_d": small_full[3], "ssm_b_glu": small_full[4],
    }

    loss, grad_x, g, recv_big = _local_step(x[0], c[0], ctx[0], c_ctx, loss_target[0], w, late, True, mod)

    dmod_all = _all_gather([_pad8(g["mod"].reshape(4, d3))], [0], "gather_dmod")[0].reshape(N_DEV, 8, d3)
    dmod_all = dmod_all[:, :4].reshape(N_DEV, 2, 2, d3)
    dm_ctx = dmod_all[0, :, 1]
    for p in range(1, N_DEV):
        dm_ctx = dm_ctx + dmod_all[p, :, 1]
    dm16 = jnp.concatenate([dmod_all[:, :, 0].transpose(1, 0, 2), dm_ctx[:, None], jnp.zeros((2, 16 - N_DEV - 1, d3), F32)], axis=1)
    g_ada_w, dc16 = _ada_bwd(c16, ada_w_b, lax.dynamic_slice_in_dim(dm16, me * wa, wa, axis=2), F32)
    g["c_ctx"] = dc16[0, N_DEV] + dc16[1, N_DEV]
    g_ada_b = jnp.sum(dm16, axis=1)

    blob_names = [n for n in REPLICATED if n != "ada_b"] + SMALL_SHARDED
    flat = jnp.concatenate([g[n].reshape(-1).astype(F32) for n in blob_names] + [loss.reshape(1)])
    nflat = flat.shape[0]
    rows = -(-nflat // (N_DEV * 128 * 8)) * 8
    flat = jnp.concatenate([flat, jnp.zeros((N_DEV * rows * 128 - nflat,), F32)]).reshape(N_DEV * rows, 128)
    blob_sum = _sum_partials(_all_to_all([flat], [0], "scatter_grads")[0])
    blob = _all_gather([blob_sum], [0], "gather_small_grads", "gather2")[0].reshape(-1)
    small_g, off = {"ada_b": g_ada_b}, 0
    for n in blob_names:
        shape = wt[n].shape if n in REPLICATED else (*wt[n].shape[:-1], wt[n].shape[-1] * N_DEV)
        size = math.prod(shape)
        small_g[n] = blob[off:off + size].reshape(shape)
        off += size
    loss = blob[off]
    for n in SMALL_SHARDED:
        size = wt[n].shape[-1]
        small_g[n] = lax.dynamic_slice_in_dim(small_g[n], me * size, size, axis=small_g[n].ndim - 1)

    out_g, out_d, out_m, out_v = {}, {}, {}, {}
    recv_big["ada_w"] = _view2d("ada_w", g_ada_w)[None]
    for n in BIG:
        stack = recv_big[n]
        shp = wt[n].shape
        res = _adamw(stack, _view2d(n, wt[n]), _view2d(n, mt[n]), _view2d(n, vt[n]), "adamw_" + n)
        out_g[n], out_d[n], out_m[n], out_v[n] = [r.reshape(shp) for r in res]
    for n in NATIVE_SMALL:
        shp = wt[n].shape
        v2 = lambda a: a.reshape(-1, shp[-1])
        res = _adamw(v2(small_g.pop(n))[None], v2(wt[n]), v2(mt[n]), v2(vt[n]), "adamw_" + n)
        out_g[n], out_d[n], out_m[n], out_v[n] = [r.reshape(shp) for r in res]
    names = list(small_g)
    cat = lambda t: jnp.concatenate([t[n].reshape(-1) for n in names])
    gs, ws, ms, vs = cat(small_g), cat(wt), cat(mt), cat(vt)
    ns = gs.shape[0]
    rs = -(-ns // (128 * 512)) * 512
    padr = lambda a: jnp.concatenate([a, jnp.ones((rs * 128 - ns,), F32)]).reshape(rs, 128)
    res = _adamw(padr(gs)[None], padr(ws), padr(ms), padr(vs), "adamw_small")
    off = 0
    for n in names:
        size = math.prod(wt[n].shape)
        out_g[n], out_d[n], out_m[n], out_v[n] = [r.reshape(-1)[off:off + size].reshape(wt[n].shape) for r in res]
        off += size

    return (loss, grad_x[None], *[out_g[n] for n in WEIGHTS], *[out_d[n] for n in WEIGHTS],
            *[out_m[n] for n in WEIGHTS], *[out_v[n] for n in WEIGHTS])
```

```python
import math

import jax
import jax.numpy as jnp
from jax import lax
from jax.experimental import pallas as pl
from jax.experimental.pallas import tpu as pltpu

F32 = jnp.float32
BF = jnp.bfloat16
MESH = pl.DeviceIdType.MESH
N_DEV = 8

GRID_W = 64
CHUNK = 16
S5_P = 16
S5_N = 64
LN_EPS = 1e-5
DN_ALPHA = 4.0 ** 0.25
ADAM_LR, ADAM_B1, ADAM_B2, ADAM_EPS, ADAM_WD, ADAM_STEP = 1e-3, 0.9, 0.999, 1e-8, 0.01, 10
GELU_C0 = math.sqrt(2.0 / math.pi)
GELU_C1 = 0.044715
VMEM_MB = 52

ANY = pl.BlockSpec(memory_space=pl.ANY)


def _cparams():
    return pltpu.CompilerParams(vmem_limit_bytes=VMEM_MB << 20)


def _dot(a, b):
    return jnp.dot(a, b, preferred_element_type=F32)


def _dot_nt(a, b):
    return lax.dot_general(a, b, (((1,), (1,)), ((), ())), preferred_element_type=F32)


def _dot_tn(a, b):
    return lax.dot_general(a, b, (((0,), (0,)), ((), ())), preferred_element_type=F32)


def _sigmoid(x):
    return 1.0 / (1.0 + jnp.exp(-x))


def _gelu_parts(y):
    th = jnp.tanh(GELU_C0 * (y + GELU_C1 * y * y * y))
    g = 0.5 * y * (1.0 + th)
    dg = 0.5 * (1.0 + th) + 0.5 * y * (1.0 - th * th) * GELU_C0 * (1.0 + 3.0 * GELU_C1 * y * y)
    return g, dg


def _full(shape):
    nd = len(shape)
    return pl.BlockSpec(shape, lambda *_: (0,) * nd)


def _mesh_pos():
    x, y, c = lax.axis_index("x"), lax.axis_index("y"), lax.axis_index("c")
    return x, y, c


def _peer(pos, k):
    x, y, c = pos
    px = 1 - x if (k >> 2) & 1 else x
    py = 1 - y if (k >> 1) & 1 else y
    pc = 1 - c if k & 1 else c
    return (px, py, pc), 4 * px + 2 * py + pc


def _shard_at(ref, axis, idx, n):
    if axis == 0:
        return ref.at[pl.ds(idx * n, n)]
    return ref.at[:, pl.ds(idx * n, n)]


class _Exchange:
    def __init__(self, kind, arrays, axes):
        self.kind, self.axes, self.n = kind, list(axes), len(arrays)
        self.arrays = list(arrays)
        self.out_shape = []
        for s, ax in zip(arrays, axes):
            shp = list(s.shape)
            if kind == "scatter":
                shp[ax] //= N_DEV
                self.out_shape.append(jax.ShapeDtypeStruct((N_DEV, *shp), s.dtype))
            else:
                shp[ax] *= N_DEV
                self.out_shape.append(jax.ShapeDtypeStruct(tuple(shp), s.dtype))
        self.scratch = [pltpu.SemaphoreType.DMA((self.n, N_DEV - 1)), pltpu.SemaphoreType.DMA((self.n, N_DEV - 1)),
                        pltpu.SemaphoreType.DMA((self.n,))]

    def _copies(self, ins, outs, sems):
        send_sems, recv_sems, local_sems = sems
        pos = _mesh_pos()
        x, y, c = pos
        me = 4 * x + 2 * y + c
        local, sends, chained, recvs = [], [], [], []
        for i in range(self.n):
            ax = self.axes[i]
            if self.kind == "scatter":
                size = ins[i].shape[ax] // N_DEV
                src = lambda idx, i=i, ax=ax, size=size: _shard_at(ins[i], ax, idx, size)
                dst = lambda idx, i=i: outs[i].at[idx]
            else:
                size = ins[i].shape[ax]
                src = lambda idx, i=i: ins[i]
                dst = lambda idx, i=i, ax=ax, size=size: _shard_at(outs[i], ax, idx, size)

            def copy(k, s, d, to, i=i):
                return pltpu.make_async_remote_copy(src_ref=s, dst_ref=d, send_sem=send_sems.at[i, k],
                                                    recv_sem=recv_sems.at[i, k], device_id=to, device_id_type=MESH)

            local.append(pltpu.make_async_copy(src(me), dst(me), local_sems.at[i]))
            if self.kind == "gather2":
                sib, sib_i = (x, y, 1 - c), 4 * x + 2 * y + (1 - c)
                chips = [(1 - x, y), (x, 1 - y), (1 - x, 1 - y)]
                sends.append(copy(0, src(me), dst(me), sib))
                recvs.append(copy(0, src(me), dst(sib_i), sib))
                for j, (cx, cy) in enumerate(chips):
                    same, other = 4 * cx + 2 * cy + c, 4 * cx + 2 * cy + (1 - c)
                    sends.append(copy(1 + j, src(me), dst(me), (cx, cy, c)))
                    chained.append((copy(1 + j, dst(same), dst(same), (cx, cy, c)), copy(4 + j, dst(same), dst(same), sib)))
                    recvs.append(copy(4 + j, dst(other), dst(other), sib))
            else:
                for k in range(1, N_DEV):
                    peer, pidx = _peer(pos, k)
                    out_src = src(pidx) if self.kind == "scatter" else src(me)
                    sends.append(copy(k - 1, out_src, dst(me), peer))
                    recvs.append(copy(k - 1, out_src, dst(pidx), peer))
        return local, sends, chained, recvs

    def start(self, ins, outs, sems):
        local, sends, _, _ = self._copies(ins, outs, sems)
        for cp in local + sends:
            cp.start()

    def wait(self, ins, outs, sems):
        local, sends, chained, recvs = self._copies(ins, outs, sems)
        for arrival, released in chained:
            arrival.wait_recv()
            released.start()
        for cp in recvs:
            cp.wait_recv()
        for cp in sends + [released for _, released in chained]:
            cp.wait_send()
        for cp in local:
            cp.wait()

    def run(self, name):
        n = self.n

        def body(*refs):
            ins, outs, sems = refs[:n], refs[n:2 * n], refs[2 * n:]
            self.start(ins, outs, sems)
            self.wait(ins, outs, sems)

        return pl.pallas_call(body, name=name, out_shape=self.out_shape, in_specs=[ANY] * n, out_specs=[ANY] * n,
                              scratch_shapes=self.scratch)(*self.arrays)


def _hosted_call(body, xch, grid, in_specs, out_specs, out_shape, scratch, args, name):
    out_specs, out_shape = list(out_specs), list(out_shape)
    n_in, n_out = len(in_specs), len(out_specs)
    if xch is None:
        res = pl.pallas_call(body, name=name, grid=grid, in_specs=in_specs, out_specs=out_specs, out_shape=out_shape,
                             scratch_shapes=list(scratch), compiler_params=_cparams())(*args)
        return list(res), []
    n = xch.n
    rank = len(grid)

    def wrapped(*refs):
        ins, x_ins = refs[:n_in], refs[n_in:n_in + n]
        outs = refs[n_in + n:n_in + n + n_out]
        x_outs = refs[n_in + n + n_out:n_in + 2 * n + n_out]
        rest = refs[n_in + 2 * n + n_out:]
        own, sems = rest[:len(rest) - 3], rest[len(rest) - 3:]
        ids = [pl.program_id(a) for a in range(rank)]
        first, last = ids[0] == 0, ids[0] == grid[0] - 1
        for a in range(1, rank):
            first = jnp.logical_and(first, ids[a] == 0)
            last = jnp.logical_and(last, ids[a] == grid[a] - 1)

        @pl.when(first)
        def _():
            xch.start(x_ins, x_outs, sems)

        body(*ins, *outs, *own)

        @pl.when(last)
        def _():
            xch.wait(x_ins, x_outs, sems)

    res = pl.pallas_call(
        wrapped, name=name, grid=grid, in_specs=list(in_specs) + [ANY] * n, out_specs=out_specs + [ANY] * n,
        out_shape=out_shape + xch.out_shape, scratch_shapes=list(scratch) + xch.scratch, compiler_params=_cparams(),
    )(*args, *xch.arrays)
    return list(res[:n_out]), list(res[n_out:])


def _all_gather(shards, axes, name, kind="gather"):
    return _Exchange(kind, shards, axes).run(name)


def _all_to_all(parts, axes, name):
    return _Exchange("scatter", parts, axes).run(name)


def _ada_fwd(cv, ada_w, ada_b):
    nl, d, wd = ada_w.shape
    r = cv.shape[0]

    def body(c_ref, w_ref, b_ref, o_ref):
        c = c_ref[...]
        s = (c * _sigmoid(c)).astype(BF)
        o_ref[0] = _dot(s, w_ref[0]) + b_ref[0]

    return pl.pallas_call(
        body, name="ada_fwd", grid=(nl,),
        in_specs=[_full((r, d)), pl.BlockSpec((1, d, wd), lambda l: (l, 0, 0)), pl.BlockSpec((1, 1, wd), lambda l: (l, 0, 0))],
        out_specs=pl.BlockSpec((1, r, wd), lambda l: (l, 0, 0)),
        out_shape=jax.ShapeDtypeStruct((nl, r, wd), F32), compiler_params=_cparams(),
    )(cv, ada_w, ada_b.reshape(nl, 1, wd))


def _ada_bwd(cv, ada_w, dm, out_dtype):
    nl, d, wd = ada_w.shape
    r = cv.shape[0]

    def body(c_ref, w_ref, dm_ref, dw_ref, dc_ref):
        c = c_ref[...]
        sg = _sigmoid(c)
        s = (c * sg).astype(BF)
        dmv = dm_ref[0].astype(BF)
        dw_ref[0] = _dot_tn(s, dmv).astype(out_dtype)
        dc_ref[0] = _dot_nt(dmv, w_ref[0]) * (sg * (1.0 + c * (1.0 - sg)))

    return pl.pallas_call(
        body, name="ada_bwd", grid=(nl,),
        in_specs=[_full((r, d)), pl.BlockSpec((1, d, wd), lambda l: (l, 0, 0)), pl.BlockSpec((1, r, wd), lambda l: (l, 0, 0))],
        out_specs=[pl.BlockSpec((1, d, wd), lambda l: (l, 0, 0)), pl.BlockSpec((1, r, d), lambda l: (l, 0, 0))],
        out_shape=[jax.ShapeDtypeStruct((nl, d, wd), out_dtype), jax.ShapeDtypeStruct((nl, r, d), F32)],
        compiler_params=_cparams(),
    )(cv, ada_w, dm)


def _sum_partials(stack):
    _, r, c = stack.shape

    def body(s_ref, o_ref):
        acc = s_ref[0]
        for p in range(1, N_DEV):
            acc = acc + s_ref[p]
        o_ref[...] = acc

    return pl.pallas_call(body, name="sum_partials", out_shape=jax.ShapeDtypeStruct((r, c), F32),
                          in_specs=[_full(stack.shape)], out_specs=_full((r, c)), grid=(1,),
                          compiler_params=_cparams())(stack)


def _adamw(gstack, w, m, v, name):
    p, r, c = gstack.shape
    tr = r
    for cand in (512 if c <= 256 else 256, 128, 64, 32, 16, 8):
        if r % cand == 0 and r > cand:
            tr = cand
            break
    bc1 = 1.0 - ADAM_B1 ** ADAM_STEP
    bc2 = 1.0 - ADAM_B2 ** ADAM_STEP

    def body(g_ref, w_ref, m_ref, v_ref, go_ref, d_ref, mo_ref, vo_ref):
        g = g_ref[0].astype(F32)
        for q in range(1, p):
            g = g + g_ref[q].astype(F32)
        mn = ADAM_B1 * m_ref[...] + (1.0 - ADAM_B1) * g
        vn = ADAM_B2 * v_ref[...] + (1.0 - ADAM_B2) * (g * g)
        go_ref[...] = g
        mo_ref[...] = mn
        vo_ref[...] = vn
        d_ref[...] = -ADAM_LR * ((mn / bc1) / (jnp.sqrt(vn / bc2) + ADAM_EPS) + ADAM_WD * w_ref[...])

    row = pl.BlockSpec((tr, c), lambda i: (i, 0))
    sds = jax.ShapeDtypeStruct((r, c), F32)
    return pl.pallas_call(
        body, name=name, grid=(r // tr,),
        in_specs=[pl.BlockSpec((p, tr, c), lambda i: (0, i, 0)), row, row, row],
        out_specs=[row, row, row, row], out_shape=[sds, sds, sds, sds], compiler_params=_cparams(),
    )(gstack, w, m, v)


def _lat_or_ctx_specs(tm, d, nl, grid_rank, row_axis):
    def lat(*ids):
        return (jnp.minimum(ids[row_axis], nl - 1), 0)

    def ctx(*ids):
        return (jnp.maximum(ids[row_axis] - nl, 0), 0)

    return pl.BlockSpec((tm, d), lat), pl.BlockSpec((tm, d), ctx)


def _sel_row(ref, is_ctx):
    return jnp.where(is_ctx, ref[1:2, :], ref[0:1, :])


def _inproj0(x, ctx, a2, b2, w, tm, xch=None):
    l, d = x.shape
    nl, nc = l // tm, ctx.shape[0] // tm
    e = w.shape[1] // 4
    half = e // 2

    def body(x_ref, c_ref, a_ref, b_ref, w_hbm, o_ref, w_ref):
        i = pl.program_id(0)

        @pl.when(i == 0)
        def _():
            pltpu.sync_copy(w_hbm, w_ref)

        is_ctx = i >= nl
        xv = jnp.where(is_ctx, c_ref[...], x_ref[...])
        h = (xv * _sel_row(a_ref, is_ctx) + _sel_row(b_ref, is_ctx)).astype(BF)
        for k in range(4):
            r = _dot(h, w_ref[:, k * e:(k + 1) * e])
            o_ref[k, 0] = r[:, :half].astype(BF)
            o_ref[k, 1] = r[:, half:].astype(BF)

    lat, cx = _lat_or_ctx_specs(tm, d, nl, 1, 0)
    (p42,), extra = _hosted_call(
        body, xch, grid=(nl + nc,),
        in_specs=[lat, cx, _full((2, d)), _full((2, d)), ANY],
        out_specs=[pl.BlockSpec((4, 2, tm, half), lambda i: (0, 0, i, 0))],
        out_shape=[jax.ShapeDtypeStruct((4, 2, l + ctx.shape[0], half), BF)],
        scratch=[pltpu.VMEM(w.shape, BF)], args=(x, ctx, a2, b2, w), name="l0_inproj")
    return p42, extra


def _conv_taps(u, w_up, w_mid, w_dn, pos, rl, tm):
    up = jnp.where(pos == 0, 0.0, pltpu.roll(u, 1, 0))
    dn = jnp.where(pos == rl - 1, 0.0, pltpu.roll(u, tm - 1, 0))
    return w_up * up + w_mid * u + w_dn * dn, up, dn


def _conv_halo_specs(tm, tc, nl, lead):
    hb = tm // GRID_W

    def prev(j, i):
        return (0, 1, jnp.maximum(jnp.minimum(i, nl - 1) * hb - 1, 0), j)

    def nxt(j, i):
        return (0, 1, jnp.minimum((jnp.minimum(i, nl - 1) + 1) * hb, nl * hb - 1), j)

    return pl.BlockSpec((lead, 1, GRID_W, tc), prev), pl.BlockSpec((lead, 1, GRID_W, tc), nxt)


def _conv_fwd(p42, cw, nl, tm, tc):
    _, _, r, half = p42.shape
    nt = r // tm

    def body(p_ref, hp_ref, hn_ref, cw_ref, o_ref):
        i = pl.program_id(1)
        is_ctx = i >= nl
        row = lax.broadcasted_iota(jnp.int32, (tm, tc), 0)
        rl = jnp.where(is_ctx, tm, GRID_W)
        pos = jnp.bitwise_and(row, rl - 1)

        def gate(hv, yc):
            bg = p_ref[0, hv].astype(F32)
            z = p_ref[3, hv].astype(F32)
            return (bg * yc * (z * _sigmoid(z))).astype(BF)

        u_h = p_ref[1, 0].astype(F32) * p_ref[2, 0].astype(F32)
        w_h = cw_ref[:, 0, :]
        o_ref[0] = gate(0, _conv_taps(u_h, w_h[0:1], w_h[1:2], w_h[2:3], pos, rl, tm)[0])
        u_v = p_ref[1, 1].astype(F32) * p_ref[2, 1].astype(F32)
        w_v = cw_ref[:, 1, :]

        @pl.when(is_ctx)
        def _():
            o_ref[1] = gate(1, _conv_taps(u_v, w_v[0:1], w_v[1:2], w_v[2:3], pos, rl, tm)[0])

        @pl.when(jnp.logical_not(is_ctx))
        def _():
            up = hp_ref[1, 0].astype(F32) * hp_ref[2, 0].astype(F32) * (i > 0).astype(F32)
            dn = hn_ref[1, 0].astype(F32) * hn_ref[2, 0].astype(F32) * (i < nl - 1).astype(F32)
            ext = jnp.concatenate([up, u_v, dn], axis=0)
            yc = w_v[0:1] * ext[0:tm] + w_v[1:2] * u_v + w_v[2:3] * ext[2 * GRID_W:tm + 2 * GRID_W]
            o_ref[1] = gate(1, yc)

    hp, hn = _conv_halo_specs(tm, tc, nl, 4)
    return pl.pallas_call(
        body, name="l0_conv_fwd", grid=(half // tc, nt),
        in_specs=[pl.BlockSpec((4, 2, tm, tc), lambda j, i: (0, 0, i, j)), hp, hn,
                  pl.BlockSpec((3, 2, tc), lambda j, i: (0, 0, j))],
        out_specs=pl.BlockSpec((2, tm, tc), lambda j, i: (0, i, j)),
        out_shape=jax.ShapeDtypeStruct((2, r, half), BF), compiler_params=_cparams(),
    )(p42, p42, p42, cw)


def _outproj_ln0(q3, w_out, x, ctx, gt2, tm):
    l, d = x.shape
    lc = ctx.shape[0]
    nl, nc = l // tm, lc // tm
    _, r, half = q3.shape
    tjo = tm // CHUNK

    def body(q_ref, w_hbm, x_ref, c_ref, g_ref, xl_ref, xc_ref, rl_ref, rc_ref, fx_ref, w_ref, xs_ref, rs_ref):
        i = pl.program_id(0)

        @pl.when(i == 0)
        def _():
            pltpu.sync_copy(w_hbm, w_ref)

        is_ctx = i >= nl
        fx = _dot(q_ref[0], w_ref[:half, :]) + _dot(q_ref[1], w_ref[half:, :])
        xv = jnp.where(is_ctx, c_ref[...], x_ref[...])
        rr = DN_ALPHA * xv + _sel_row(g_ref, is_ctx) * fx
        mu = jnp.mean(rr, axis=-1, keepdims=True)
        cen = rr - mu
        rstd = lax.rsqrt(jnp.mean(cen * cen, axis=-1, keepdims=True) + LN_EPS)
        xh = cen * rstd
        for lb in range(d // 128):
            xs_ref[lb] = xh[:, lb * 128:(lb + 1) * 128]
        rs_ref[...] = jnp.broadcast_to(rstd, (tm, 128))
        fx_ref[...] = fx.astype(BF)

        def to_cr(xo_ref, ro_ref):
            for s in range(CHUNK):
                for lb in range(d // 128):
                    xo_ref[:, s * d + lb * 128:s * d + (lb + 1) * 128] = xs_ref.at[lb][pl.ds(s, tjo, stride=CHUNK), :]
                ro_ref[:, s * 128:(s + 1) * 128] = rs_ref[pl.ds(s, tjo, stride=CHUNK), :]

        @pl.when(jnp.logical_not(is_ctx))
        def _():
            to_cr(xl_ref, rl_ref)

        @pl.when(is_ctx)
        def _():
            to_cr(xc_ref, rc_ref)

    lat, cx = _lat_or_ctx_specs(tm, d, nl, 1, 0)
    lat_o = lambda w_: pl.BlockSpec((tjo, CHUNK * w_), lambda i: (jnp.minimum(i, nl - 1), 0))
    ctx_o = lambda w_: pl.BlockSpec((tjo, CHUNK * w_), lambda i: (jnp.maximum(i - nl, 0), 0))
    return pl.pallas_call(
        body, name="l0_outproj_ln", grid=(nl + nc,),
        in_specs=[pl.BlockSpec((2, tm, half), lambda i: (0, i, 0)), ANY, lat, cx, _full((2, d))],
        out_specs=[lat_o(d), ctx_o(d), lat_o(128), ctx_o(128), pl.BlockSpec((tm, d), lambda i: (i, 0))],
        out_shape=[jax.ShapeDtypeStruct((l // CHUNK, CHUNK * d), F32), jax.ShapeDtypeStruct((lc // CHUNK, CHUNK * d), F32),
                   jax.ShapeDtypeStruct((l // CHUNK, CHUNK * 128), F32), jax.ShapeDtypeStruct((lc // CHUNK, CHUNK * 128), F32),
                   jax.ShapeDtypeStruct((r, d), BF)],
        scratch_shapes=[pltpu.VMEM(w_out.shape, BF), pltpu.VMEM((d // 128, tm, 128), F32), pltpu.VMEM((tm, 128), F32)],
        compiler_params=_cparams(),
    )(q3, w_out, x, ctx, gt2)


def _bwd_outproj0(dr_l, dr_c, gt2, w_out, fx, tm):
    l, d = dr_l.shape
    nl, nc = l // tm, dr_c.shape[0] // tm
    e = w_out.shape[0]
    half = e // 2
    r = l + dr_c.shape[0]

    def body(dl_ref, dc_ref, g_ref, w_hbm, fx_ref, dq_ref, acc_ref, w_ref):
        i = pl.program_id(0)

        @pl.when(i == 0)
        def _():
            pltpu.sync_copy(w_hbm, w_ref)
            acc_ref[...] = jnp.zeros_like(acc_ref)

        is_ctx = i >= nl
        dr = jnp.where(is_ctx, dc_ref[...], dl_ref[...]).astype(F32)
        dfx = (dr * _sel_row(g_ref, is_ctx)).astype(BF)
        dq_ref[0] = _dot_nt(dfx, w_ref[:half, :]).astype(BF)
        dq_ref[1] = _dot_nt(dfx, w_ref[half:, :]).astype(BF)
        s = jnp.sum(dr * fx_ref[...].astype(F32), axis=0, keepdims=True)
        sel = is_ctx.astype(F32)
        acc_ref[0:1, :] += s * (1.0 - sel)
        acc_ref[1:2, :] += s * sel

    lat, cx = _lat_or_ctx_specs(tm, d, nl, 1, 0)
    return pl.pallas_call(
        body, name="l0_bwd_outproj", grid=(nl + nc,),
        in_specs=[lat, cx, _full((2, d)), ANY, pl.BlockSpec((tm, d), lambda i: (i, 0))],
        out_specs=[pl.BlockSpec((2, tm, half), lambda i: (0, i, 0)), _full((8, d))],
        out_shape=[jax.ShapeDtypeStruct((2, r, half), BF), jax.ShapeDtypeStruct((8, d), F32)],
        scratch_shapes=[pltpu.VMEM(w_out.shape, BF)], compiler_params=_cparams(),
    )(dr_l, dr_c, gt2, w_out, fx)


def _conv_bwd_inproj0(dq3, p42, cw, w_in, x, ctx, dr_l, dr_c, a2, nl, tm, xch=None):
    l, d = x.shape
    _, _, r, half = p42.shape
    nt = r // tm
    e = 2 * half
    cc = min(512, half)
    n_cc = half // cc

    def body(dq_ref, dqp_ref, dqn_ref, p_ref, hp_ref, hn_ref, cw_ref, w_hbm, x_ref, c_ref, dl_ref, dc_ref, a_ref,
             dp_ref, dw_ref, gx_ref, acc_ref, w_ref, dh_ref):
        i, hv = pl.program_id(0), pl.program_id(1)
        is_ctx = i >= nl

        @pl.when(jnp.logical_and(i == 0, hv == 0))
        def _():
            pltpu.sync_copy(w_hbm, w_ref)
            acc_ref[...] = jnp.zeros_like(acc_ref)
            dw_ref[...] = jnp.zeros_like(dw_ref)

        row = lax.broadcasted_iota(jnp.int32, (tm, cc), 0)
        rl = jnp.where(is_ctx, tm, GRID_W)
        pos = jnp.bitwise_and(row, rl - 1)

        def pieces(dq, bg, z):
            sz = _sigmoid(z)
            sil = z * sz
            return dq * bg * sil, dq * sil, dq * bg * (sz * (1.0 + z * (1.0 - sz)))

        def emit(hvs, c, parts, dyc, u_up, u, u_dn, dh):
            lanes = slice(c * cc, (c + 1) * cc)
            for k, part in enumerate(parts):
                pb = part.astype(BF)
                dp_ref[k, 0, :, lanes] = pb
                c0 = k * e + hvs * half + c * cc
                t = _dot_nt(pb, w_ref[:, c0:c0 + cc])
                dh = t if dh is None else dh + t
            dw_ref[0:1, hvs, lanes] += jnp.sum(dyc * u_up, axis=0, keepdims=True)
            dw_ref[1:2, hvs, lanes] += jnp.sum(dyc * u, axis=0, keepdims=True)
            dw_ref[2:3, hvs, lanes] += jnp.sum(dyc * u_dn, axis=0, keepdims=True)
            return dh

        def seq_half(hvs):
            dh = None
            for c in range(n_cc):
                lanes = slice(c * cc, (c + 1) * cc)
                bg, cg = p_ref[0, 0, :, lanes].astype(F32), p_ref[1, 0, :, lanes].astype(F32)
                v, z = p_ref[2, 0, :, lanes].astype(F32), p_ref[3, 0, :, lanes].astype(F32)
                w = cw_ref[:, hvs, lanes]
                u = cg * v
                yc, u_up, u_dn = _conv_taps(u, w[0:1], w[1:2], w[2:3], pos, rl, tm)
                dyc, dbg_f, dz_f = pieces(dq_ref[0, :, lanes].astype(F32), bg, z)
                du = _conv_taps(dyc, w[2:3], w[1:2], w[0:1], pos, rl, tm)[0]
                dh = emit(hvs, c, (dbg_f * yc, du * v, du * cg, dz_f * yc), dyc, u_up, u, u_dn, dh)
            return dh

        def col_half():
            m_up = (i > 0).astype(F32)
            m_dn = (i < nl - 1).astype(F32)
            dh = None
            for c in range(n_cc):
                lanes = slice(c * cc, (c + 1) * cc)
                bg, cg = p_ref[0, 0, :, lanes].astype(F32), p_ref[1, 0, :, lanes].astype(F32)
                v, z = p_ref[2, 0, :, lanes].astype(F32), p_ref[3, 0, :, lanes].astype(F32)
                w = cw_ref[:, 1, lanes]
                u = cg * v

                def halo(h_ref, dqh_ref, msk):
                    hb, hc = h_ref[0, 0, :, lanes].astype(F32), h_ref[1, 0, :, lanes].astype(F32)
                    hv_, hz = h_ref[2, 0, :, lanes].astype(F32), h_ref[3, 0, :, lanes].astype(F32)
                    return hc * hv_ * msk, pieces(dqh_ref[0, :, lanes].astype(F32), hb, hz)[0] * msk

                u_p, dyc_p = halo(hp_ref, dqp_ref, m_up)
                u_n, dyc_n = halo(hn_ref, dqn_ref, m_dn)
                u_ext = jnp.concatenate([u_p, u, u_n], axis=0)
                u_up, u_dn = u_ext[0:tm], u_ext[2 * GRID_W:tm + 2 * GRID_W]
                yc = w[0:1] * u_up + w[1:2] * u + w[2:3] * u_dn
                dyc, dbg_f, dz_f = pieces(dq_ref[0, :, lanes].astype(F32), bg, z)
                d_ext = jnp.concatenate([dyc_p, dyc, dyc_n], axis=0)
                du = w[0:1] * d_ext[2 * GRID_W:tm + 2 * GRID_W] + w[1:2] * dyc + w[2:3] * d_ext[0:tm]
                dh = emit(1, c, (dbg_f * yc, du * v, du * cg, dz_f * yc), dyc, u_up, u, u_dn, dh)
            return dh

        @pl.when(hv == 0)
        def _():
            dh_ref[...] = seq_half(0)

        @pl.when(jnp.logical_and(hv == 1, is_ctx))
        def _():
            dh_ref[...] += seq_half(1)

        @pl.when(jnp.logical_and(hv == 1, jnp.logical_not(is_ctx)))
        def _():
            dh_ref[...] += col_half()

        @pl.when(hv == 1)
        def _():
            dh = dh_ref[...]
            xv = jnp.where(is_ctx, c_ref[...], x_ref[...])
            s_sc = jnp.sum(dh * xv, axis=0, keepdims=True)
            s_sh = jnp.sum(dh, axis=0, keepdims=True)
            sel = is_ctx.astype(F32)
            acc_ref[0:1, :] += s_sc * (1.0 - sel)
            acc_ref[1:2, :] += s_sc * sel
            acc_ref[2:3, :] += s_sh * (1.0 - sel)
            acc_ref[3:4, :] += s_sh * sel

        @pl.when(jnp.logical_and(hv == 1, jnp.logical_not(is_ctx)))
        def _():
            gx_ref[...] = DN_ALPHA * dl_ref[...].astype(F32) + dh_ref[...] * a_ref[0:1, :]

    hb = tm // GRID_W
    prev_blk = lambda i: jnp.maximum(jnp.minimum(i, nl - 1) * hb - 1, 0)
    next_blk = lambda i: jnp.minimum((jnp.minimum(i, nl - 1) + 1) * hb, nl * hb - 1)
    lat, cx = _lat_or_ctx_specs(tm, d, nl, 2, 0)
    (dp42, dcw, gx, acc), extra = _hosted_call(
        body, xch, grid=(nt, 2),
        in_specs=[pl.BlockSpec((1, tm, half), lambda i, h: (h, i, 0)),
                  pl.BlockSpec((1, GRID_W, half), lambda i, h: (1, prev_blk(i), 0)),
                  pl.BlockSpec((1, GRID_W, half), lambda i, h: (1, next_blk(i), 0)),
                  pl.BlockSpec((4, 1, tm, half), lambda i, h: (0, h, i, 0)),
                  pl.BlockSpec((4, 1, GRID_W, half), lambda i, h: (0, 1, prev_blk(i), 0)),
                  pl.BlockSpec((4, 1, GRID_W, half), lambda i, h: (0, 1, next_blk(i), 0)),
                  _full((3, 2, half)), ANY, lat, cx, lat, cx, _full((2, d))],
        out_specs=[pl.BlockSpec((4, 1, tm, half), lambda i, h: (0, h, i, 0)), _full((8, 2, half)),
                   pl.BlockSpec((tm, d), lambda i, h: (jnp.minimum(i, nl - 1), 0)), _full((8, d))],
        out_shape=[jax.ShapeDtypeStruct(p42.shape, BF), jax.ShapeDtypeStruct((8, 2, half), F32),
                   jax.ShapeDtypeStruct((l, d), F32), jax.ShapeDtypeStruct((8, d), F32)],
        scratch=[pltpu.VMEM(w_in.shape, BF), pltpu.VMEM((tm, d), F32)],
        args=(dq3, dq3, dq3, p42, p42, p42, cw, w_in, x, ctx, dr_l, dr_c, a2), name="l0_conv_bwd_inproj")
    return dp42, dcw, gx, acc, extra


def _dw_inproj0(x, ctx, a2, b2, dp42, tm):
    l, d = x.shape
    lc = ctx.shape[0]
    assert lc == tm
    tl = 4 * tm if l % (4 * tm) == 0 else tm
    nl = l // tl
    half = dp42.shape[-1]
    e = 2 * half

    def body(x_ref, c_ref, a_ref, b_ref, dpl_ref, dpc_ref, o_ref, acc_ref):
        i = pl.program_id(1)

        @pl.when(i == 0)
        def _():
            acc_ref[...] = jnp.zeros_like(acc_ref)

        def add(rows_ref, dp_ref, sel):
            h = (rows_ref[...] * a_ref[sel:sel + 1, :] + b_ref[sel:sel + 1, :]).astype(BF)
            acc_ref[:, :half] += _dot_tn(h, dp_ref[0, 0])
            acc_ref[:, half:] += _dot_tn(h, dp_ref[0, 1])

        @pl.when(i < nl)
        def _():
            add(x_ref, dpl_ref, 0)

        @pl.when(i == nl)
        def _():
            add(c_ref, dpc_ref, 1)
            o_ref[...] = acc_ref[...].astype(BF)

    return pl.pallas_call(
        body, name="l0_dw_inproj", grid=(4, nl + 1),
        in_specs=[pl.BlockSpec((tl, d), lambda k, i: (jnp.minimum(i, nl - 1), 0)), _full((lc, d)),
                  _full((2, d)), _full((2, d)),
                  pl.BlockSpec((1, 2, tl, half), lambda k, i: (k, 0, jnp.minimum(i, nl - 1), 0)),
                  pl.BlockSpec((1, 2, lc, half), lambda k, i: (k, 0, l // lc, 0))],
        out_specs=pl.BlockSpec((d, e), lambda k, i: (0, k)),
        out_shape=jax.ShapeDtypeStruct((d, 4 * e), BF),
        scratch_shapes=[pltpu.VMEM((d, e), F32)], compiler_params=_cparams(),
    )(x, ctx, a2, b2, dp42, dp42)


def _dw_outproj0(q3, dr_l, dr_c, gt2, tm, xch=None):
    l, d = dr_l.shape
    nl, nc = l // tm, dr_c.shape[0] // tm
    _, r, half = q3.shape
    nt = nl + nc

    def body(q_ref, dl_ref, dc_ref, g_ref, o_ref, acc_ref):
        i = pl.program_id(0)
        is_ctx = i >= nl

        @pl.when(i == 0)
        def _():
            acc_ref[...] = jnp.zeros_like(acc_ref)

        dr = jnp.where(is_ctx, dc_ref[...], dl_ref[...]).astype(F32)
        dfx = (dr * _sel_row(g_ref, is_ctx)).astype(BF)
        acc_ref[:half, :] += _dot_tn(q_ref[0], dfx)
        acc_ref[half:, :] += _dot_tn(q_ref[1], dfx)

        @pl.when(i == nt - 1)
        def _():
            o_ref[...] = acc_ref[...].astype(BF)

    lat, cx = _lat_or_ctx_specs(tm, d, nl, 1, 0)
    (g_w,), extra = _hosted_call(
        body, xch, grid=(nt,),
        in_specs=[pl.BlockSpec((2, tm, half), lambda i: (0, i, 0)), lat, cx, _full((2, d))],
        out_specs=[_full((2 * half, d))], out_shape=[jax.ShapeDtypeStruct((2 * half, d), BF)],
        scratch=[pltpu.VMEM((2 * half, d), F32)], args=(q3, dr_l, dr_c, gt2), name="l0_dw_outproj")
    return g_w, extra


def _cr_tile(j, cap=256):
    for cand in (1024, 512, 256, 128, 64, 32, 16, 8):
        if cand <= cap and j % cand == 0:
            return cand
    raise ValueError(j)


def _final(w_cr, w_out, xh_cr, tgt_cr, vecs):
    j, e16 = w_cr.shape
    e = e16 // CHUNK
    d = w_out.shape[1]
    tj = _cr_tile(j)

    def body(w_ref, wo_hbm, xh_ref, t_ref, v_ref, dr_ref, acc_ref, wo_ref):
        @pl.when(jnp.logical_and(pl.program_id(0) == 0, pl.program_id(1) == 0))
        def _():
            pltpu.sync_copy(wo_hbm, wo_ref)
            acc_ref[...] = jnp.zeros_like(acc_ref)

        o = _dot(w_ref[...], wo_ref[...])
        x1 = xh_ref[...] * v_ref[0:1, :] + v_ref[1:2, :]
        rr = DN_ALPHA * x1 + v_ref[2:3, :] * o
        mu = jnp.mean(rr, axis=-1, keepdims=True)
        cen = rr - mu
        rstd = lax.rsqrt(jnp.mean(cen * cen, axis=-1, keepdims=True) + LN_EPS)
        xh2 = cen * rstd
        err = xh2 * v_ref[3:4, :] + v_ref[4:5, :] - t_ref[...]
        dy = err * (1.0 / d)
        dxh = dy * v_ref[3:4, :]
        dr = rstd * (dxh - jnp.mean(dxh, axis=-1, keepdims=True) - xh2 * jnp.mean(dxh * xh2, axis=-1, keepdims=True))
        dr_ref[...] = dr.astype(BF)
        acc_ref[0:1, :] += jnp.sum(dy * xh2, axis=0, keepdims=True)
        acc_ref[1:2, :] += jnp.sum(dy, axis=0, keepdims=True)
        acc_ref[2:3, :] += jnp.sum(dr * o, axis=0, keepdims=True)
        acc_ref[3:4, :] += (0.5 / d) * jnp.sum(err * err, axis=0, keepdims=True)

    tok_d = pl.BlockSpec((tj, d), lambda t, s: (t, s))
    return pl.pallas_call(
        body, name="l1_final", grid=(j // tj, CHUNK),
        in_specs=[pl.BlockSpec((tj, e), lambda t, s: (t, s)), ANY, tok_d, tok_d, _full((8, d))],
        out_specs=[tok_d, _full((8, d))],
        out_shape=[jax.ShapeDtypeStruct((j, CHUNK * d), BF), jax.ShapeDtypeStruct((8, d), F32)],
        scratch_shapes=[pltpu.VMEM(w_out.shape, BF)], compiler_params=_cparams(),
    )(w_cr, w_out, xh_cr, tgt_cr, vecs)


def _dw_cr(lhs, rhs, lhs_kind, rhs_kind, vec, bias_sum, init, name):
    j = lhs.shape[0]
    k = lhs.shape[1] // CHUNK
    n = rhs.shape[1] // CHUNK
    tj = _cr_tile(j, 512)
    nh = 2 if k * n * 4 > (8 << 20) else 1
    tn = n // nh
    nt = j // tj
    has_init = init is not None

    def body(*refs):
        refs = list(refs)
        l_ref, r_ref = refs[0], refs[1]
        pos = 2
        v_ref = None
        if vec is not None:
            v_ref = refs[pos]
            pos += 1
        i_ref = None
        if has_init:
            i_ref = refs[pos]
            pos += 1
        o_ref = refs[pos]
        pos += 1
        bs_ref = None
        if bias_sum:
            bs_ref = refs[pos]
            pos += 1
        acc_ref = refs[pos]
        t, s = pl.program_id(1), pl.program_id(2)
        first = jnp.logical_and(t == 0, s == 0)

        @pl.when(first)
        def _():
            acc_ref[...] = i_ref[...] if has_init else jnp.zeros_like(acc_ref)
            if bias_sum:
                bs_ref[...] = jnp.zeros_like(bs_ref)

        if lhs_kind == "mod":
            lv = (l_ref[...] * v_ref[0:1, :] + v_ref[1:2, :]).astype(BF)
        else:
            lv = l_ref[...]
        if rhs_kind == "scaled":
            rv = (r_ref[...].astype(F32) * v_ref[0:1, :]).astype(BF)
        else:
            rv = r_ref[...]
        acc_ref[...] += _dot_tn(lv, rv)
        if bias_sum:
            bs_ref[0:1, :] += jnp.sum(rv.astype(F32), axis=0, keepdims=True)

        @pl.when(jnp.logical_and(t == nt - 1, s == CHUNK - 1))
        def _():
            o_ref[...] = acc_ref[...].astype(BF)

    l_spec = pl.BlockSpec((tj, k), lambda h, t, s: (t, s))
    r_spec = pl.BlockSpec((tj, tn), lambda h, t, s: (t, s * nh + h))
    in_specs, args = [l_spec, r_spec], [lhs, rhs]
    if vec is not None:
        in_specs.append(_full(vec.shape))
        args.append(vec)
    o_spec = pl.BlockSpec((k, tn), lambda h, t, s: (0, h))
    if has_init:
        in_specs.append(o_spec)
        args.append(init)
    out_specs, out_shape = [o_spec], [jax.ShapeDtypeStruct((k, n), BF)]
    if bias_sum:
        out_specs.append(pl.BlockSpec((8, tn), lambda h, t, s: (0, h)))
        out_shape.append(jax.ShapeDtypeStruct((8, n), F32))
    res = pl.pallas_call(
        body, name=name, grid=(nh, nt, CHUNK), in_specs=in_specs, out_specs=out_specs, out_shape=out_shape,
        scratch_shapes=[pltpu.VMEM((k, tn), F32)], compiler_params=_cparams(),
    )(*args)
    return res if bias_sum else res[0]


GT_ROWS = CHUNK * S5_P
ZG_W = 2 * 2 * S5_N
PAIR_W = 2 * ZG_W
GROUPS_PER_STEP = 4


def _inproj1_gt(xh_cr, a1, b1, wu_t, w_z, tag):
    j, d16 = xh_cr.shape
    d = d16 // CHUNK
    e = wu_t.shape[0]
    g = e // S5_P
    tj = _cr_tile(j, 256)

    def body(x_ref, a_ref, b_ref, wu_hbm, wz_hbm, u_ref, z_ref, wu_ref, wz_ref):
        @pl.when(jnp.logical_and(pl.program_id(0) == 0, pl.program_id(1) == 0))
        def _():
            pltpu.sync_copy(wu_hbm, wu_ref)
            pltpu.sync_copy(wz_hbm, wz_ref)

        h = (x_ref[...] * a_ref[...] + b_ref[...]).astype(BF)
        u_ref[...] = _dot_nt(wu_ref[...], h).reshape(g, S5_P, tj).astype(BF)
        z_ref[...] = _dot(h, wz_ref[...]).astype(BF)

    return pl.pallas_call(
        body, name="l1_inproj_" + tag, grid=(j // tj, CHUNK),
        in_specs=[pl.BlockSpec((tj, d), lambda t, s: (t, s)), _full((1, d)), _full((1, d)), ANY, ANY],
        out_specs=[pl.BlockSpec((g, S5_P, tj), lambda t, s: (0, s, t)), pl.BlockSpec((tj, e), lambda t, s: (t, s))],
        out_shape=[jax.ShapeDtypeStruct((g, GT_ROWS, j), BF), jax.ShapeDtypeStruct((j, CHUNK * e), BF)],
        scratch_shapes=[pltpu.VMEM(wu_t.shape, BF), pltpu.VMEM(w_z.shape, BF)], compiler_params=_cparams(),
    )(xh_cr, a1, b1, wu_t, w_z)


def _gt_spec(j, gb=GROUPS_PER_STEP):
    return pl.BlockSpec((gb, GT_ROWS, j), lambda i: (i, 0, 0))


def _zg_spec(j, gb=GROUPS_PER_STEP):
    return pl.BlockSpec((j, gb * ZG_W), lambda i: (0, i))


def _w_spec(width, gb=GROUPS_PER_STEP):
    return pl.BlockSpec((gb, GT_ROWS, width), lambda i: (i, 0, 0))


def _pair_lanes(k):
    return slice((k // 2) * PAIR_W, (k // 2 + 1) * PAIR_W)


def _s5_z(ut_l, ut_c, bc):
    g, _, jl = ut_l.shape
    jc = ut_c.shape[2]
    gb = GROUPS_PER_STEP

    def body(ul_ref, uc_ref, bc_ref, zl_ref, zc_ref):
        for k in range(0, gb, 2):
            zl_ref[:, _pair_lanes(k)] = _dot_tn(ul_ref[k], bc_ref[k]) + _dot_tn(ul_ref[k + 1], bc_ref[k + 1])
            zc_ref[:, _pair_lanes(k)] = _dot_tn(uc_ref[k], bc_ref[k]) + _dot_tn(uc_ref[k + 1], bc_ref[k + 1])

    return pl.pallas_call(
        body, name="l1_s5_z", grid=(g // gb,), in_specs=[_gt_spec(jl), _gt_spec(jc), _w_spec(PAIR_W)],
        out_specs=[_zg_spec(jl), _zg_spec(jc)],
        out_shape=[jax.ShapeDtypeStruct((jl, g * ZG_W), F32), jax.ShapeDtypeStruct((jc, g * ZG_W), F32)],
        compiler_params=_cparams(),
    )(ut_l, ut_c, bc)


def _s5_y(ut_l, s_l, mt_t, cct):
    g, _, jl = ut_l.shape
    gb = GROUPS_PER_STEP

    def body(u_ref, s_ref, mt_ref, cc_ref, y_ref):
        for k in range(gb):
            s_k = s_ref[:, _pair_lanes(k)].astype(BF)
            y_ref[k] = (_dot(mt_ref[k], u_ref[k]) + _dot_nt(cc_ref[k], s_k)).astype(BF)

    return pl.pallas_call(
        body, name="l1_s5_y", grid=(g // gb,),
        in_specs=[_gt_spec(jl), _zg_spec(jl), _w_spec(GT_ROWS), _w_spec(PAIR_W)],
        out_specs=_gt_spec(jl), out_shape=jax.ShapeDtypeStruct((g, GT_ROWS, jl), BF), compiler_params=_cparams(),
    )(ut_l, s_l, mt_t, cct)


def _s5_ds(dyt_l, cct):
    g, _, jl = dyt_l.shape
    gb = GROUPS_PER_STEP

    def body(dy_ref, cc_ref, ds_ref):
        for k in range(0, gb, 2):
            ds_ref[:, _pair_lanes(k)] = _dot_tn(dy_ref[k], cc_ref[k]) + _dot_tn(dy_ref[k + 1], cc_ref[k + 1])

    return pl.pallas_call(
        body, name="l1_s5_ds", grid=(g // gb,), in_specs=[_gt_spec(jl), _w_spec(PAIR_W)], out_specs=_zg_spec(jl),
        out_shape=jax.ShapeDtypeStruct((jl, g * ZG_W), F32), compiler_params=_cparams(),
    )(dyt_l, cct)


def _s5_dx(dyt_l, dz_l, dz_c, mt, bc):
    g, _, jl = dyt_l.shape
    jc = dz_c.shape[0]
    gb = GROUPS_PER_STEP

    def body(dy_ref, dzl_ref, dzc_ref, mt_ref, bc_ref, dul_ref, duc_ref):
        for k in range(gb):
            dzl = dzl_ref[:, _pair_lanes(k)].astype(BF)
            dzc = dzc_ref[:, _pair_lanes(k)].astype(BF)
            dul_ref[k] = (_dot(mt_ref[k], dy_ref[k]) + _dot_nt(bc_ref[k], dzl)).astype(BF)
            duc_ref[k] = _dot_nt(bc_ref[k], dzc).astype(BF)

    return pl.pallas_call(
        body, name="l1_s5_dx", grid=(g // gb,),
        in_specs=[_gt_spec(jl), _zg_spec(jl), _zg_spec(jc), _w_spec(GT_ROWS), _w_spec(PAIR_W)],
        out_specs=[_gt_spec(jl), _gt_spec(jc)],
        out_shape=[jax.ShapeDtypeStruct((g, GT_ROWS, jl), BF), jax.ShapeDtypeStruct((g, GT_ROWS, jc), BF)],
        compiler_params=_cparams(),
    )(dyt_l, dz_l, dz_c, mt, bc)


def _s5_dw(ut_l, ut_c, dyt_l, dz_l, dz_c, s_l):
    g, _, jl = ut_l.shape
    jc = ut_c.shape[2]
    gb = GROUPS_PER_STEP

    def body(ul_ref, uc_ref, dy_ref, dzl_ref, dzc_ref, s_ref, dmt_ref, dbc_ref, dcc_ref):
        for k in range(gb):
            lanes = _pair_lanes(k)
            dmt_ref[k] = _dot_nt(ul_ref[k], dy_ref[k])
            dbc_ref[k] = (_dot(ul_ref[k], dzl_ref[:, lanes].astype(BF))
                          + _dot(uc_ref[k], dzc_ref[:, lanes].astype(BF)))
            dcc_ref[k] = _dot(dy_ref[k], s_ref[:, lanes].astype(BF))

    sd_m = jax.ShapeDtypeStruct((g, GT_ROWS, GT_ROWS), F32)
    sd_p = jax.ShapeDtypeStruct((g, GT_ROWS, PAIR_W), F32)
    return pl.pallas_call(
        body, name="l1_s5_dw", grid=(g // gb,),
        in_specs=[_gt_spec(jl), _gt_spec(jc), _gt_spec(jl), _zg_spec(jl), _zg_spec(jc), _zg_spec(jl)],
        out_specs=[_w_spec(GT_ROWS), _w_spec(PAIR_W), _w_spec(PAIR_W)], out_shape=[sd_m, sd_p, sd_p],
        compiler_params=_cparams(),
    )(ut_l, ut_c, dyt_l, dz_l, dz_c, s_l)


def _scan_g(z_l, z_c, coef, chains, conj, s_l=None, s_c=None, name="l1_scan"):
    jl, w_all = z_l.shape
    jc = z_c.shape[0]
    gb = 2 * GROUPS_PER_STEP if w_all % (2 * GROUPS_PER_STEP * ZG_W) == 0 else GROUPS_PER_STEP
    wb = gb * ZG_W
    nch = wb // 256
    with_da = s_l is not None
    sign = -1.0 if conj else 1.0

    def body(*refs):
        zl_ref, zc_ref, cf_ref = refs[:3]
        k0 = 3
        if with_da:
            sl_ref, sc_ref = refs[3:5]
            k0 = 5
        ol_ref, oc_ref = refs[k0:k0 + 2]
        rowi = lax.broadcasted_iota(jnp.int32, (8, 128), 0)

        def lanes_of(ch):
            return slice(ch * 256, ch * 256 + 128), slice(ch * 256 + 128, (ch + 1) * 256)

        def coefs(ch, r0, nr):
            lr, li = lanes_of(ch)
            return cf_ref[r0:r0 + nr, lr], sign * cf_ref[r0:r0 + nr, li]

        def shift(v, sh, rev):
            if rev:
                return jnp.where(rowi < 8 - sh, pltpu.roll(v, 8 - sh, 0), 0.0)
            return jnp.where(rowi >= sh, pltpu.roll(v, sh, 0), 0.0)

        zero_row = jnp.zeros((1, 128), F32)
        zero_tile = jnp.zeros((8, 128), F32)
        carry = [zero_row] * (2 * nch)
        da = [zero_tile] * (2 * nch)
        for seg in range(len(chains[0])):
            which = chains[0][seg][0]
            assert chains[1][seg][0] == which
            revs = (chains[0][seg][1], chains[1][seg][1])
            src, dst = (zc_ref, oc_ref) if which == "c" else (zl_ref, ol_ref)
            sref = ((sc_ref if which == "c" else sl_ref) if with_da else None)
            ng = (jc if which == "c" else jl) // 8

            def step(it, st, src=src, dst=dst, sref=sref, ng=ng, revs=revs):
                carry_, da_ = list(st[:2 * nch]), list(st[2 * nch:])
                for ch in range(nch):
                    rev = revs[ch % 2]
                    lr, li = lanes_of(ch)
                    grp = (ng - 1 - it) if rev else it
                    off = pl.multiple_of(grp * 8, 8)
                    xr, xi = src[pl.ds(off, 8), lr], src[pl.ds(off, 8), li]
                    for sh, r0 in ((1, 0), (2, 1), (4, 2)):
                        ar, ai = coefs(ch, r0, 1)
                        sr, si = shift(xr, sh, rev), shift(xi, sh, rev)
                        xr, xi = xr + ar * sr - ai * si, xi + ar * si + ai * sr
                    tr, ti = coefs(ch, 16, 8) if rev else coefs(ch, 8, 8)
                    cr_, ci_ = carry_[2 * ch], carry_[2 * ch + 1]
                    ir = xr + tr * cr_ - ti * ci_
                    ii = xi + tr * ci_ + ti * cr_
                    if rev:
                        er = jnp.where(rowi == 7, cr_, pltpu.roll(ir, 7, 0))
                        ei = jnp.where(rowi == 7, ci_, pltpu.roll(ii, 7, 0))
                        carry_[2 * ch], carry_[2 * ch + 1] = ir[0:1], ii[0:1]
                    else:
                        er = jnp.where(rowi == 0, cr_, pltpu.roll(ir, 1, 0))
                        ei = jnp.where(rowi == 0, ci_, pltpu.roll(ii, 1, 0))
                        carry_[2 * ch], carry_[2 * ch + 1] = ir[7:8], ii[7:8]
                    dst[pl.ds(off, 8), lr] = er
                    dst[pl.ds(off, 8), li] = ei
                    if sref is not None:
                        s_r, s_i = sref[pl.ds(off, 8), lr], sref[pl.ds(off, 8), li]
                        da_[2 * ch] = da_[2 * ch] + s_r * er + s_i * ei
                        da_[2 * ch + 1] = da_[2 * ch + 1] + s_r * ei - s_i * er
                return (*carry_, *da_)

            st = lax.fori_loop(0, ng, step, (*carry, *da))
            carry, da = list(st[:2 * nch]), list(st[2 * nch:])
        if with_da:
            da_ref = refs[k0 + 2]
            for ch in range(nch):
                lr, li = lanes_of(ch)
                da_ref[:, lr] = da[2 * ch]
                da_ref[:, li] = da[2 * ch + 1]

    in_specs = [_zg_spec(jl, gb), _zg_spec(jc, gb), pl.BlockSpec((24, wb), lambda i: (0, i))]
    args = [z_l, z_c, coef]
    out_specs = [_zg_spec(jl, gb), _zg_spec(jc, gb)]
    out_shape = [jax.ShapeDtypeStruct(z_l.shape, F32), jax.ShapeDtypeStruct(z_c.shape, F32)]
    if with_da:
        in_specs += [_zg_spec(jl, gb), _zg_spec(jc, gb)]
        args += [s_l, s_c]
        out_specs.append(pl.BlockSpec((8, wb), lambda i: (0, i)))
        out_shape.append(jax.ShapeDtypeStruct((8, w_all), F32))
    return pl.pallas_call(body, name=name, grid=(w_all // wb,), in_specs=in_specs, out_specs=out_specs,
                          out_shape=out_shape, compiler_params=_cparams())(*args)


def _gt_tok_spec(g, tj):
    return pl.BlockSpec((g, S5_P, tj), lambda t, s: (0, s, t))


def _glu_fwd_gt(yt, z_cr, w_glu, b_glu):
    g, _, j = yt.shape
    e = g * S5_P
    tj = _cr_tile(j)

    def body(y_ref, z_ref, w_hbm, b_ref, o_ref, sg_ref, w_ref):
        @pl.when(jnp.logical_and(pl.program_id(0) == 0, pl.program_id(1) == 0))
        def _():
            pltpu.sync_copy(w_hbm, w_ref)

        y = jnp.transpose(y_ref[...].reshape(e, tj).astype(F32))
        gl = _gelu_parts(y)[0]
        sg = _sigmoid(_dot(gl.astype(BF), w_ref[...]) + b_ref[...])
        z = z_ref[...].astype(F32)
        o_ref[...] = (gl * sg * (z * _sigmoid(z))).astype(BF)
        sg_ref[...] = sg.astype(BF)

    tok = pl.BlockSpec((tj, e), lambda t, s: (t, s))
    return pl.pallas_call(
        body, name="l1_glu_fwd", grid=(j // tj, CHUNK),
        in_specs=[_gt_tok_spec(g, tj), tok, ANY, _full((1, e))], out_specs=[tok, tok],
        out_shape=[jax.ShapeDtypeStruct((j, CHUNK * e), BF), jax.ShapeDtypeStruct((j, CHUNK * e), BF)],
        scratch_shapes=[pltpu.VMEM(w_glu.shape, BF)], compiler_params=_cparams(),
    )(yt, z_cr, w_glu, b_glu)


def _glu_bwd_gt(dr_cr, gt1, w_out, w_glu, yt, z_cr, sg_cr):
    g, _, j = yt.shape
    e, d = w_out.shape
    tj = _cr_tile(j)

    def body(dr_ref, g_ref, wo_hbm, wg_hbm, y_ref, z_ref, sg_ref, dz_ref, dt_ref, dy_ref, wo_ref, wg_ref):
        @pl.when(jnp.logical_and(pl.program_id(0) == 0, pl.program_id(1) == 0))
        def _():
            pltpu.sync_copy(wo_hbm, wo_ref)
            pltpu.sync_copy(wg_hbm, wg_ref)

        do = (dr_ref[...].astype(F32) * g_ref[...]).astype(BF)
        dw = _dot_nt(do, wo_ref[...])
        y = jnp.transpose(y_ref[...].reshape(e, tj).astype(F32))
        gl, dgel = _gelu_parts(y)
        z = z_ref[...].astype(F32)
        sz = _sigmoid(z)
        sg = sg_ref[...].astype(F32)
        dg2 = dw * (z * sz)
        dz_ref[...] = (dw * gl * sg * (sz * (1.0 + z * (1.0 - sz)))).astype(BF)
        dt = (dg2 * gl * sg * (1.0 - sg)).astype(BF)
        dt_ref[...] = dt
        dy = (dg2 * sg + _dot_nt(dt, wg_ref[...])) * dgel
        dy_ref[...] = jnp.transpose(dy).reshape(g, S5_P, tj).astype(BF)

    tok_e = pl.BlockSpec((tj, e), lambda t, s: (t, s))
    return pl.pallas_call(
        body, name="l1_glu_bwd", grid=(j // tj, CHUNK),
        in_specs=[pl.BlockSpec((tj, d), lambda t, s: (t, s)), _full((1, d)), ANY, ANY, _gt_tok_spec(g, tj), tok_e, tok_e],
        out_specs=[tok_e, tok_e, _gt_tok_spec(g, tj)],
        out_shape=[jax.ShapeDtypeStruct((j, CHUNK * e), BF), jax.ShapeDtypeStruct((j, CHUNK * e), BF),
                   jax.ShapeDtypeStruct((g, GT_ROWS, j), BF)],
        scratch_shapes=[pltpu.VMEM(w_out.shape, BF), pltpu.VMEM(w_glu.shape, BF)], compiler_params=_cparams(),
    )(dr_cr, gt1, w_out, w_glu, yt, z_cr, sg_cr)


def _bwd_inproj1_gt(dut, dz_cr, wu_t, w_z, xh_cr, rs_cr, dr2_cr, vecs, tag):
    g, _, j = dut.shape
    e, d = wu_t.shape
    tj = _cr_tile(j)

    def body(du_ref, dz_ref, wu_hbm, wz_hbm, xh_ref, rs_ref, dr2_ref, v_ref, dr1_ref, acc_ref, wu_ref, wz_ref):
        @pl.when(jnp.logical_and(pl.program_id(0) == 0, pl.program_id(1) == 0))
        def _():
            pltpu.sync_copy(wu_hbm, wu_ref)
            pltpu.sync_copy(wz_hbm, wz_ref)
            acc_ref[...] = jnp.zeros_like(acc_ref)

        dh = _dot_tn(du_ref[...].reshape(e, tj), wu_ref[...]) + _dot_nt(dz_ref[...], wz_ref[...])
        xh = xh_ref[...]
        x1 = xh * v_ref[0:1, :] + v_ref[1:2, :]
        dx1 = DN_ALPHA * dr2_ref[...].astype(F32) + dh * v_ref[2:3, :]
        dxh = dx1 * v_ref[0:1, :]
        rstd = rs_ref[:, 0:1]
        dr1 = rstd * (dxh - jnp.mean(dxh, axis=-1, keepdims=True) - xh * jnp.mean(dxh * xh, axis=-1, keepdims=True))
        dr1_ref[...] = dr1.astype(BF)
        acc_ref[0:1, :] += jnp.sum(dh * x1, axis=0, keepdims=True)
        acc_ref[1:2, :] += jnp.sum(dh, axis=0, keepdims=True)
        acc_ref[2:3, :] += jnp.sum(dx1 * xh, axis=0, keepdims=True)
        acc_ref[3:4, :] += jnp.sum(dx1, axis=0, keepdims=True)

    tok_d = pl.BlockSpec((tj, d), lambda t, s: (t, s))
    return pl.pallas_call(
        body, name="l1_bwd_inproj_" + tag, grid=(j // tj, CHUNK),
        in_specs=[_gt_tok_spec(g, tj), pl.BlockSpec((tj, e), lambda t, s: (t, s)), ANY, ANY, tok_d,
                  pl.BlockSpec((tj, 128), lambda t, s: (t, s)), tok_d, _full((8, d))],
        out_specs=[tok_d, _full((8, d))],
        out_shape=[jax.ShapeDtypeStruct((j, CHUNK * d), BF), jax.ShapeDtypeStruct((8, d), F32)],
        scratch_shapes=[pltpu.VMEM(wu_t.shape, BF), pltpu.VMEM(w_z.shape, BF)], compiler_params=_cparams(),
    )(dut, dz_cr, wu_t, w_z, xh_cr, rs_cr, dr2_cr, vecs)


def _dw_gt(lhs_gt, rhs_cr, lhs_gelu, vec, bias_sum, init, out_dtype, name):
    g, _, j = lhs_gt.shape
    e = g * S5_P
    n = rhs_cr.shape[1] // CHUNK
    tj = _cr_tile(j, 512 if j % 512 == 0 else 256)
    nh = 2 if e * n * 4 > (8 << 20) else 1
    tn = n // nh
    nt = j // tj
    has_init = init is not None

    def body(*refs):
        refs = list(refs)
        l_ref, r_ref = refs[0], refs[1]
        pos = 2
        v_ref = i_ref = bs_ref = None
        if vec is not None:
            v_ref = refs[pos]
            pos += 1
        if has_init:
            i_ref = refs[pos]
            pos += 1
        o_ref = refs[pos]
        pos += 1
        if bias_sum:
            bs_ref = refs[pos]
            pos += 1
        acc_ref = refs[pos]
        t, s = pl.program_id(1), pl.program_id(2)

        @pl.when(jnp.logical_and(t == 0, s == 0))
        def _():
            acc_ref[...] = i_ref[...] if has_init else jnp.zeros_like(acc_ref)
            if bias_sum:
                bs_ref[...] = jnp.zeros_like(bs_ref)

        lv = l_ref[...].reshape(e, tj)
        if lhs_gelu:
            lv = _gelu_parts(lv.astype(F32))[0].astype(BF)
        if vec is not None:
            rv = (r_ref[...] * v_ref[0:1, :] + v_ref[1:2, :]).astype(BF)
        else:
            rv = r_ref[...]
        acc_ref[...] += _dot(lv, rv)
        if bias_sum:
            bs_ref[0:1, :] += jnp.sum(rv.astype(F32), axis=0, keepdims=True)

        @pl.when(jnp.logical_and(t == nt - 1, s == CHUNK - 1))
        def _():
            o_ref[...] = acc_ref[...].astype(out_dtype)

    in_specs = [pl.BlockSpec((g, S5_P, tj), lambda h, t, s: (0, s, t)),
                pl.BlockSpec((tj, tn), lambda h, t, s: (t, s * nh + h))]
    args = [lhs_gt, rhs_cr]
    if vec is not None:
        in_specs.append(_full(vec.shape))
        args.append(vec)
    o_spec = pl.BlockSpec((e, tn), lambda h, t, s: (0, h))
    if has_init:
        in_specs.append(o_spec)
        args.append(init)
    out_specs, out_shape = [o_spec], [jax.ShapeDtypeStruct((e, n), out_dtype)]
    if bias_sum:
        out_specs.append(pl.BlockSpec((8, tn), lambda h, t, s: (0, h)))
        out_shape.append(jax.ShapeDtypeStruct((8, n), F32))
    res = pl.pallas_call(
        body, name=name, grid=(nh, nt, CHUNK), in_specs=in_specs, out_specs=out_specs, out_shape=out_shape,
        scratch_shapes=[pltpu.VMEM((e, tn), F32)], compiler_params=_cparams(),
    )(*args)
    return res if bias_sum else res[0]


def _scan_coef_g(lam_re, lam_im, log_step):
    g = lam_re.shape[1]
    ms = jnp.array([1, 2, 4, 0, 0, 0, 0, 0] + list(range(1, 9)) + list(range(8, 0, -1)), F32) * CHUNK
    dt = jnp.exp(log_step)[..., None]
    mag = jnp.exp(ms.reshape(-1, 1, 1, 1) * (lam_re * dt)[None])
    ang = ms.reshape(-1, 1, 1, 1) * (lam_im * dt)[None]
    cr, ci = mag * jnp.cos(ang), mag * jnp.sin(ang)
    both = jnp.stack([cr, ci], axis=2).reshape(24, 2, 2, g // 2, 2, S5_N)
    return both.transpose(0, 3, 1, 2, 4, 5).reshape(24, g * ZG_W)


def _s5_small(lam_re, lam_im, log_step, b_re, b_im, c_re, c_im, d_skip):
    g = lam_re.shape[1]
    t, p = CHUNK, S5_P
    dt = jnp.exp(log_step)[..., None]
    ks = jnp.arange(t + 1, dtype=F32).reshape(t + 1, 1, 1, 1)
    mag = jnp.exp(ks * (lam_re * dt)[None])
    ang = ks * (lam_im * dt)[None]
    pr, pi = mag * jnp.cos(ang), mag * jnp.sin(ang)
    ar, ai = pr[1], pi[1]
    qr, qi = ar - 1.0, ai
    den = lam_re * lam_re + lam_im * lam_im
    fr = (qr * lam_re + qi * lam_im) / den
    fi = (qi * lam_re - qr * lam_im) / den
    bt_re, bt_im = b_re.transpose(0, 1, 3, 2), b_im.transpose(0, 1, 3, 2)
    bbr = fr[:, :, None, :] * bt_re - fi[:, :, None, :] * bt_im
    bbi = fr[:, :, None, :] * bt_im + fi[:, :, None, :] * bt_re
    lay = lambda a_r, a_i: jnp.stack([a_r, a_i], axis=0).transpose(3, 2, 0, 1, 4)
    by_dir = lambda a, f0, f1: jnp.stack([f0(a[:, 0]), f1(a[:, 1])], axis=1)
    rev = lambda a: jnp.flip(a, axis=0)
    same = lambda a: a
    pwb = lay(by_dir(pr[:t], rev, same), by_dir(pi[:t], rev, same))
    pwc = lay(by_dir(pr[1:], same, rev), by_dir(pi[1:], same, rev))
    bb = jnp.stack([bbr, bbi], axis=0).transpose(2, 1, 0, 3, 4)
    cc = jnp.stack([c_re, c_im], axis=0).transpose(2, 1, 0, 3, 4)
    dmat = jnp.eye(p, dtype=F32)[None] * d_skip.reshape(g, p)[:, :, None]
    return pwb, pwc, bb, cc, dmat, pr[t], pi[t]


def _pair_cols(r, ri, g2):
    c0 = (r * 2 + ri) * 128 + g2 * S5_N
    return slice(c0, c0 + S5_N)


def _rows_rep(a):
    return jnp.broadcast_to(a[:, None, :], (CHUNK, S5_P, a.shape[-1])).reshape(GT_ROWS, a.shape[-1])


def _rows_tile(a):
    return jnp.broadcast_to(a[None], (CHUNK, S5_P, a.shape[-1])).reshape(GT_ROWS, a.shape[-1])


def _sum_blocks(a):
    return jnp.sum(a.reshape(CHUNK, S5_P, a.shape[-1]), axis=0)


def _sum_in_blocks(a):
    return jnp.sum(a.reshape(CHUNK, S5_P, a.shape[-1]), axis=1)


def _ab_rows(pwb_ref, bb_ref, k, r):
    prs, pis = _rows_rep(pwb_ref[k, r, 0]), _rows_rep(pwb_ref[k, r, 1])
    bbr, bbi = _rows_tile(bb_ref[k, r, 0]), _rows_tile(bb_ref[k, r, 1])
    return prs * bbr - pis * bbi, prs * bbi + pis * bbr, prs, pis, bbr, bbi


def _s5_weights_fwd(pwb, pwc, bb, cc, dmat):
    g = pwb.shape[0]
    gb = GROUPS_PER_STEP
    hp = lax.Precision.HIGHEST

    def body(pwb_ref, pwc_ref, bb_ref, cc_ref, dm_ref, mt_ref, mtt_ref, bc_ref, cct_ref):
        zeros = jnp.zeros((GT_ROWS, S5_N), BF)
        nt = (((1,), (1,)), ((), ()))
        for k in range(gb):
            g2 = k % 2
            kds = []
            for r in range(2):
                for ri in range(2):
                    bc_ref[k, :, _pair_cols(r, ri, 1 - g2)] = zeros
                    cct_ref[k, :, _pair_cols(r, ri, 1 - g2)] = zeros
                abr, abi = _ab_rows(pwb_ref, bb_ref, k, r)[:2]
                bc_ref[k, :, _pair_cols(r, 0, g2)] = abr.astype(BF)
                bc_ref[k, :, _pair_cols(r, 1, g2)] = abi.astype(BF)
                cr, ci = cc_ref[k, r, 0], cc_ref[k, r, 1]
                crt, cit = _rows_tile(cr), _rows_tile(ci)
                prt, pit = _rows_rep(pwc_ref[k, r, 0]), _rows_rep(pwc_ref[k, r, 1])
                cct_ref[k, :, _pair_cols(r, 0, g2)] = (crt * prt - cit * pit).astype(BF)
                cct_ref[k, :, _pair_cols(r, 1, g2)] = (-(crt * pit + cit * prt)).astype(BF)
                kds.append(lax.dot_general(abr, cr, nt, precision=hp, preferred_element_type=F32)
                           - lax.dot_general(abi, ci, nt, precision=hp, preferred_element_type=F32))
            blk = lambda a, s: a[s * S5_P:(s + 1) * S5_P]
            last = CHUNK - 1
            pieces = [blk(kds[1], last - i) for i in range(last)]
            pieces.append(blk(kds[0], last) + blk(kds[1], 0) + dm_ref[k])
            pieces += [blk(kds[0], last - d) for d in range(1, CHUNK)]
            qrow = jnp.concatenate(pieces, axis=1)
            mt = jnp.concatenate([qrow[:, (last - s) * S5_P:(last - s) * S5_P + GT_ROWS] for s in range(CHUNK)], axis=0)
            mt_ref[k] = mt.astype(BF)
            mtt_ref[k] = jnp.transpose(mt).astype(BF)

    small = lambda a: pl.BlockSpec((gb, *a.shape[1:]), lambda i: (i,) + (0,) * (a.ndim - 1))
    return pl.pallas_call(
        body, name="l1_s5_weights", grid=(g // gb,),
        in_specs=[small(pwb), small(pwc), small(bb), small(cc), small(dmat)],
        out_specs=[_w_spec(GT_ROWS), _w_spec(GT_ROWS), _w_spec(PAIR_W), _w_spec(PAIR_W)],
        out_shape=[jax.ShapeDtypeStruct((g, GT_ROWS, GT_ROWS), BF), jax.ShapeDtypeStruct((g, GT_ROWS, GT_ROWS), BF),
                   jax.ShapeDtypeStruct((g, GT_ROWS, PAIR_W), BF), jax.ShapeDtypeStruct((g, GT_ROWS, PAIR_W), BF)],
        compiler_params=_cparams(),
    )(pwb, pwc, bb, cc, dmat)


def _s5_weights_bwd(pwb, pwc, bb, cc, d_mt, d_bc, d_cct):
    g = pwb.shape[0]
    gb = GROUPS_PER_STEP
    hp = lax.Precision.HIGHEST

    def body(pwb_ref, pwc_ref, bb_ref, cc_ref, dmt_ref, dbc_ref, dcc_ref, dpwb_ref, dpwc_ref, dbb_ref, dccp_ref, ddm_ref):
        tn = (((0,), (0,)), ((), ()))
        nn = (((1,), (0,)), ((), ()))
        last = CHUNK - 1
        for k in range(gb):
            g2 = k % 2
            dq = None
            for s in range(CHUNK):
                parts = [dmt_ref[k, s * S5_P:(s + 1) * S5_P, :]]
                if s < last:
                    parts.insert(0, jnp.zeros((S5_P, (last - s) * S5_P), F32))
                if s > 0:
                    parts.append(jnp.zeros((S5_P, s * S5_P), F32))
                padded = jnp.concatenate(parts, axis=1) if len(parts) > 1 else parts[0]
                dq = padded if dq is None else dq + padded
            dblk = lambda d: dq[:, (last + d) * S5_P:(CHUNK + d) * S5_P]
            ddm_ref[k] = dblk(0)
            dkds = [jnp.concatenate([dblk(last - s) for s in range(CHUNK)], axis=0),
                    jnp.concatenate([dblk(-s) for s in range(CHUNK)], axis=0)]
            for r in range(2):
                abr, abi, prs, pis, bbr, bbi = _ab_rows(pwb_ref, bb_ref, k, r)
                cr, ci = cc_ref[k, r, 0], cc_ref[k, r, 1]
                dcr = lax.dot_general(dkds[r], abr, tn, precision=hp, preferred_element_type=F32)
                dci = -lax.dot_general(dkds[r], abi, tn, precision=hp, preferred_element_type=F32)
                dabr = (lax.dot_general(dkds[r], cr, nn, precision=hp, preferred_element_type=F32)
                        + dbc_ref[k, :, _pair_cols(r, 0, g2)])
                dabi = (-lax.dot_general(dkds[r], ci, nn, precision=hp, preferred_element_type=F32)
                        + dbc_ref[k, :, _pair_cols(r, 1, g2)])
                dbb_ref[k, r, 0] = _sum_blocks(prs * dabr + pis * dabi)
                dbb_ref[k, r, 1] = _sum_blocks(prs * dabi - pis * dabr)
                dpwb_ref[k, r, 0] = _sum_in_blocks(dabr * bbr + dabi * bbi)
                dpwb_ref[k, r, 1] = _sum_in_blocks(dabi * bbr - dabr * bbi)
                crt, cit = _rows_tile(cr), _rows_tile(ci)
                prt, pit = _rows_rep(pwc_ref[k, r, 0]), _rows_rep(pwc_ref[k, r, 1])
                d_re = dcc_ref[k, :, _pair_cols(r, 0, g2)]
                d_im = dcc_ref[k, :, _pair_cols(r, 1, g2)]
                dccp_ref[k, r, 0] = dcr + _sum_blocks(d_re * prt - d_im * pit)
                dccp_ref[k, r, 1] = dci - _sum_blocks(d_re * pit + d_im * prt)
                dpwc_ref[k, r, 0] = _sum_in_blocks(d_re * crt - d_im * cit)
                dpwc_ref[k, r, 1] = -_sum_in_blocks(d_re * cit + d_im * crt)

    small = lambda a: pl.BlockSpec((gb, *a.shape[1:]), lambda i: (i,) + (0,) * (a.ndim - 1))
    dmat_sds = jax.ShapeDtypeStruct((g, S5_P, S5_P), F32)
    return pl.pallas_call(
        body, name="l1_s5_weights_bwd", grid=(g // gb,),
        in_specs=[small(pwb), small(pwc), small(bb), small(cc), _w_spec(GT_ROWS), _w_spec(PAIR_W), _w_spec(PAIR_W)],
        out_specs=[small(pwb), small(pwc), small(bb), small(cc), small(dmat_sds)],
        out_shape=[jax.ShapeDtypeStruct(pwb.shape, F32), jax.ShapeDtypeStruct(pwc.shape, F32),
                   jax.ShapeDtypeStruct(bb.shape, F32), jax.ShapeDtypeStruct(cc.shape, F32), dmat_sds],
        compiler_params=_cparams(),
    )(pwb, pwc, bb, cc, d_mt, d_bc, d_cct)


def _to_cr(a):
    return a.reshape(a.shape[0] // CHUNK, CHUNK * a.shape[1])


def _from_cr(a, c):
    return a.reshape(a.shape[0] * CHUNK, c)


def _pad8(v):
    return jnp.concatenate([v, jnp.zeros((8 - v.shape[0], v.shape[1]), v.dtype)], axis=0)


def _local_step(x, c, ctx, c_ctx, loss_target, w, late=None, scatter=False, mod=None):
    l, d = x.shape
    lc = ctx.shape[0]
    tm = min(256, lc)
    assert lc == tm and l % tm == 0 and tm % GRID_W == 0 and (tm & (tm - 1)) == 0
    nl = l // tm

    own_mod = mod is None
    if own_mod:
        c8 = _pad8(jnp.stack([c, c_ctx]))
        mod = _ada_fwd(c8, w["ada_w"], w["ada_b"])
    sh = mod[:, :2, :d]
    sc = mod[:, :2, d:2 * d]
    gt = mod[:, :2, 2 * d:]
    ln_g, ln_b = w["ln_g"], w["ln_b"]

    a0, b0 = 1.0 + sc[0], sh[0]
    xch = _Exchange("gather2", [late[n][0] for n in late], [late[n][1] for n in late]) if late else None
    p42, got = _inproj0(x, ctx, a0, b0, w["conv_w_in"], tm, xch)
    if late:
        w = dict(w, **dict(zip(late, got)))
    e = w["conv_w_out"].shape[0]
    half = e // 2
    cw = w["conv_w"].reshape(3, 2, half)
    q3 = _conv_fwd(p42, cw, nl, tm, half)
    xh1_l, xh1_c, rs1_l, rs1_c, fx = _outproj_ln0(q3, w["conv_w_out"], x, ctx, gt[0], tm)
    jl, jc = l // CHUNK, lc // CHUNK

    g0, bb0 = ln_g[0:1], ln_b[0:1]
    a1 = g0 * (1.0 + sc[1])
    b1 = bb0 * (1.0 + sc[1]) + sh[1]
    wu_t = w["ssm_w_in"][:, :e].T
    w_z = w["ssm_w_in"][:, e:]
    ut_l, z_l = _inproj1_gt(xh1_l, a1[0:1], b1[0:1], wu_t, w_z, "lat")
    ut_c, _ = _inproj1_gt(xh1_c, a1[1:2], b1[1:2], wu_t, w_z, "ctx")
    s5 = (w["ssm_lam_re"], w["ssm_lam_im"], w["ssm_log_step"], w["ssm_b_re"], w["ssm_b_im"],
          w["ssm_c_re"], w["ssm_c_im"], w["ssm_d"])
    (pwb, pwc, bbw, ccw, dmat, _, _), s5_vjp = jax.vjp(_s5_small, *s5)
    mt_b, mtt_b, bc_b, cct_b = _s5_weights_fwd(pwb, pwc, bbw, ccw, dmat)
    coef = lax.stop_gradient(_scan_coef_g(*s5[:3]))
    zz_l, zz_c = _s5_z(ut_l, ut_c, bc_b)
    fwd_chains = ((("c", False), ("l", False)), (("c", True), ("l", True)))
    st_l, st_c = _scan_g(zz_l, zz_c, coef, fwd_chains, False, name="l1_scan_fwd")
    yt = _s5_y(ut_l, st_l, mtt_b, cct_b)
    b_glu = w["ssm_b_glu"].reshape(1, e)
    w_cr, sg_cr = _glu_fwd_gt(yt, z_l, w["ssm_w_glu"], b_glu)
    vec_f = _pad8(jnp.concatenate([g0, bb0, gt[1][0:1], ln_g[1:2], ln_b[1:2]], axis=0))
    dr2, acc_f = _final(w_cr, w["ssm_w_out"], xh1_l, _to_cr(loss_target), vec_f)
    loss = jnp.sum(acc_f[3])

    gt1 = gt[1][0:1]
    dz_l, dt_l, dyt = _glu_bwd_gt(dr2, gt1, w["ssm_w_out"], w["ssm_w_glu"], yt, z_l, sg_cr)
    g_w_out = _dw_cr(w_cr, dr2, "cr", "scaled", gt1, False, None, "l1_dw_out")
    g_w_glu, bsum = _dw_gt(yt, dt_l, True, None, True, None, BF, "l1_dw_glu")
    g_b_glu = bsum[0]
    ds_l = _s5_ds(dyt, cct_b)
    bwd_chains = ((("l", True), ("c", True)), (("l", False), ("c", False)))
    dzz_l, dzz_c, da = _scan_g(ds_l, jnp.zeros_like(zz_c), coef, bwd_chains, True, st_l, st_c, name="l1_scan_bwd")
    dut_l, dut_c = _s5_dx(dyt, dzz_l, dzz_c, mt_b, bc_b)
    d_mt, d_bc, d_cct = _s5_dw(ut_l, ut_c, dyt, dzz_l, dzz_c, st_l)
    n_g = e // S5_P
    da = jnp.sum(da, axis=0).reshape(n_g // 2, 2, 2, 2, S5_N).transpose(1, 2, 0, 3, 4)
    da = da.reshape(2, 2, n_g, S5_N)
    d_pwb, d_pwc, d_bb, d_ccp, d_dm = _s5_weights_bwd(pwb, pwc, bbw, ccw, d_mt, d_bc, d_cct)
    g_s5 = s5_vjp((d_pwb, d_pwc, d_bb, d_ccp, d_dm, da[:, 0], da[:, 1]))

    vec_l = _pad8(jnp.concatenate([g0, bb0, 1.0 + sc[1][0:1]], axis=0))
    vec_c = _pad8(jnp.concatenate([g0, bb0, 1.0 + sc[1][1:2]], axis=0))
    dr1_l, acc_l = _bwd_inproj1_gt(dut_l, dz_l, wu_t, w_z, xh1_l, rs1_l, dr2, vec_l, "lat")
    dr1_c, acc_c = _bwd_inproj1_gt(dut_c, jnp.zeros((jc, CHUNK * e), BF), wu_t, w_z, xh1_c, rs1_c,
                                   jnp.zeros((jc, CHUNK * d), BF), vec_c, "ctx")
    mod_l = jnp.concatenate([a1[0:1], b1[0:1]], axis=0)
    mod_c = jnp.concatenate([a1[1:2], b1[1:2]], axis=0)
    g_ut_c = _dw_gt(dut_c, xh1_c, False, mod_c, False, None, F32, "l1_dw_in_u_ctx")
    g_ut = _dw_gt(dut_l, xh1_l, False, mod_l, False, g_ut_c, BF, "l1_dw_in_u")
    g_in_z = _dw_cr(xh1_l, dz_l, "mod", "cr", mod_l, False, None, "l1_dw_in_z")
    g_w_in1 = jnp.concatenate([g_ut.T, g_in_z], axis=1)

    dr1_ln, dr1_cn = _from_cr(dr1_l, d), _from_cr(dr1_c, d)
    dq3, acc_g0 = _bwd_outproj0(dr1_ln, dr1_cn, gt[0], w["conv_w_out"], fx, tm)
    sent1 = ["ssm_w_in", "ssm_w_glu", "ssm_w_out"]
    xch1 = _Exchange("scatter", [g_w_in1, g_w_glu, g_w_out], [BIG[n] for n in sent1]) if scatter else None
    dp42, dcw, grad_x, acc_0, recv1 = _conv_bwd_inproj0(dq3, p42, cw, w["conv_w_in"], x, ctx, dr1_ln, dr1_cn,
                                                        a0, nl, tm, xch1)
    g_w_in0 = _dw_inproj0(x, ctx, a0, b0, dp42, tm)
    sent0 = ["conv_w_in"]
    xch0 = _Exchange("scatter", [g_w_in0], [BIG[n] for n in sent0]) if scatter else None
    g_w_out0, recv0 = _dw_outproj0(q3, dr1_ln, dr1_cn, gt[0], tm, xch0)
    recv = dict(zip(sent1 + sent0, recv1 + recv0))

    zero = jnp.zeros((d,), F32)
    dm0 = jnp.stack([jnp.concatenate([acc_0[2], acc_0[0], acc_g0[0]]), jnp.concatenate([acc_0[3], acc_0[1], acc_g0[1]])])
    dm1 = jnp.stack([jnp.concatenate([acc_l[1], acc_l[0], acc_f[2]]), jnp.concatenate([acc_c[1], acc_c[0], zero])])
    if own_mod:
        g_ada_w, dc8 = _ada_bwd(c8, w["ada_w"], jnp.stack([_pad8(dm0), _pad8(dm1)]), BF)
        g_mod = {"c_ctx": dc8[0, 1] + dc8[1, 1], "ada_w": g_ada_w,
                 "ada_b": jnp.stack([dm0[0] + dm0[1], dm1[0] + dm1[1]])}
    else:
        g_mod = {"mod": jnp.stack([dm0, dm1])}

    grads = {
        **g_mod,
        "ln_g": jnp.stack([acc_l[2] + acc_c[2], acc_f[0]]),
        "ln_b": jnp.stack([acc_l[3] + acc_c[3], acc_f[1]]),
        "conv_w_in": g_w_in0, "conv_w": dcw[:3].reshape(3, e), "conv_w_out": g_w_out0,
        "ssm_w_in": g_w_in1,
        "ssm_lam_re": g_s5[0], "ssm_lam_im": g_s5[1], "ssm_log_step": g_s5[2],
        "ssm_b_re": g_s5[3], "ssm_b_im": g_s5[4], "ssm_c_re": g_s5[5], "ssm_c_im": g_s5[6], "ssm_d": g_s5[7],
        "ssm_w_glu": g_w_glu, "ssm_b_glu": g_b_glu, "ssm_w_out": g_w_out,
    }
    for n in recv:
        del grads[n]
    return loss, grad_x, grads, recv


WEIGHTS = ["c_ctx", "ada_w", "ada_b", "ln_g", "ln_b", "conv_w_in", "conv_w", "conv_w_out", "ssm_w_in",
           "ssm_lam_re", "ssm_lam_im", "ssm_log_step", "ssm_b_re", "ssm_b_im", "ssm_c_re", "ssm_c_im",
           "ssm_d", "ssm_w_glu", "ssm_b_glu", "ssm_w_out"]
BIG = {"ada_w": 1, "conv_w_in": 1, "conv_w_out": 0, "ssm_w_in": 1, "ssm_w_glu": 0, "ssm_w_out": 0}
SMALL_SHARDED = ["conv_w", "ssm_d", "ssm_b_glu"]
REPLICATED = ["c_ctx", "ada_b", "ln_g", "ln_b", "ssm_lam_re", "ssm_lam_im", "ssm_log_step",
              "ssm_b_re", "ssm_b_im", "ssm_c_re", "ssm_c_im"]
NATIVE_SMALL = ["ssm_b_re", "ssm_b_im", "ssm_c_re", "ssm_c_im"]


def _view2d(name, a):
    return a.reshape(-1, a.shape[-1])


def kernel(x, c, ctx, c_ctx, ada_w, ada_b, ln_g, ln_b, conv_w_in, conv_w, conv_w_out, ssm_w_in, ssm_lam_re, ssm_lam_im, ssm_log_step, ssm_b_re, ssm_b_im, ssm_c_re, ssm_c_im, ssm_d, ssm_w_glu, ssm_b_glu, ssm_w_out, loss_target, m_c_ctx, m_ada_w, m_ada_b, m_ln_g, m_ln_b, m_conv_w_in, m_conv_w, m_conv_w_out, m_ssm_w_in, m_ssm_lam_re, m_ssm_lam_im, m_ssm_log_step, m_ssm_b_re, m_ssm_b_im, m_ssm_c_re, m_ssm_c_im, m_ssm_d, m_ssm_w_glu, m_ssm_b_glu, m_ssm_w_out, v_c_ctx, v_ada_w, v_ada_b, v_ln_g, v_ln_b, v_conv_w_in, v_conv_w, v_conv_w_out, v_ssm_w_in, v_ssm_lam_re, v_ssm_lam_im, v_ssm_log_step, v_ssm_b_re, v_ssm_b_im, v_ssm_c_re, v_ssm_c_im, v_ssm_d, v_ssm_w_glu, v_ssm_b_glu, v_ssm_w_out):
    args = locals()
    wt = {n: args[n] for n in WEIGHTS}
    mt = {n: args["m_" + n] for n in WEIGHTS}
    vt = {n: args["v_" + n] for n in WEIGHTS}

    me = 4 * lax.axis_index("x") + 2 * lax.axis_index("y") + lax.axis_index("c")
    d = x.shape[-1]
    d3 = 3 * d
    wa = d3 // N_DEV

    big_names = [n for n in BIG if n != "ada_w"]
    shard = {n: _view2d(n, wt[n]).astype(BF) for n in big_names}
    small = jnp.concatenate([wt["conv_w"][0], wt["ssm_d"], wt["ssm_b_glu"]], axis=0)
    small = jnp.concatenate([small, jnp.zeros((3, small.shape[1]), F32)], axis=0)
    w_in_full, small_full, c_all = _all_gather([shard["conv_w_in"], small, _pad8(c)], [1, 1, 0], "gather_weights", "gather2")
    late = {n: (shard[n], BIG[n]) for n in big_names if n != "conv_w_in"}
    c16 = jnp.concatenate([c_all[::8], c_ctx[None], jnp.zeros((16 - N_DEV - 1, d), F32)], axis=0)
    ada_w_b = ada_w.astype(BF)
    ada_b_mine = lax.dynamic_slice_in_dim(ada_b, me * wa, wa, axis=1)
    mod_part = _ada_fwd(c16, ada_w_b, ada_b_mine)
    mod_all = _all_gather([mod_part.reshape(32, wa)], [1], "gather_mod")[0].reshape(2, 16, d3)
    mod = jnp.stack([lax.dynamic_index_in_dim(mod_all, me, axis=1, keepdims=False), mod_all[:, N_DEV]], axis=1)
    w = {
        "ln_g": ln_g, "ln_b": ln_b, "conv_w_in": w_in_full, "conv_w": small_full[0:3],
        "ssm_lam_re": ssm_lam_re[0], "ssm_lam_im": ssm_lam_im[0],
        "ssm_log_step": ssm_log_step[0], "ssm_b_re": ssm_b_re[0], "ssm_b_im": ssm_b_im[0],
        "ssm_c_re": ssm_c_re[0], "ssm_c_im": ssm_c_im[0], "ssm_d": small_full[3], "ssm_b_glu": small_full[4],
    }

    loss, grad_x, g, recv_big = _local_step(x[0], c[0], ctx[0], c_ctx, loss_target[0], w, late, True, mod)

    dmod_all = _all_gather([_pad8(g["mod"].reshape(4, d3))], [0], "gather_dmod")[0].reshape(N_DEV, 8, d3)
    dmod_all = dmod_all[:, :4].reshape(N_DEV, 2, 2, d3)
    dm_ctx = dmod_all[0, :, 1]
    for p in range(1, N_DEV):
        dm_ctx = dm_ctx + dmod_all[p, :, 1]
    dm16 = jnp.concatenate([dmod_all[:, :, 0].transpose(1, 0, 2), dm_ctx[:, None], jnp.zeros((2, 16 - N_DEV - 1, d3), F32)], axis=1)
    g_ada_w, dc16 = _ada_bwd(c16, ada_w_b, lax.dynamic_slice_in_dim(dm16, me * wa, wa, axis=2), F32)
    g["c_ctx"] = dc16[0, N_DEV] + dc16[1, N_DEV]
    g_ada_b = jnp.sum(dm16, axis=1)

    blob_names = [n for n in REPLICATED if n != "ada_b"] + SMALL_SHARDED
    flat = jnp.concatenate([g[n].reshape(-1).astype(F32) for n in blob_names] + [loss.reshape(1)])
    nflat = flat.shape[0]
    rows = -(-nflat // (N_DEV * 128 * 8)) * 8
    flat = jnp.concatenate([flat, jnp.zeros((N_DEV * rows * 128 - nflat,), F32)]).reshape(N_DEV * rows, 128)
    last = [n for n in big_names if n not in recv_big]
    recv = _all_to_all([_view2d(n, g[n]) for n in last] + [flat], [BIG[n] for n in last] + [0], "scatter_grads")
    recv_big.update(zip(last, recv[:-1]))
    blob_sum = _sum_partials(recv[-1])
    blob = _all_gather([blob_sum], [0], "gather_small_grads", "gather2")[0].reshape(-1)
    small_g, off = {"ada_b": g_ada_b}, 0
    for n in blob_names:
        shape = wt[n].shape if n in REPLICATED else (*wt[n].shape[:-1], wt[n].shape[-1] * N_DEV)
        size = math.prod(shape)
        small_g[n] = blob[off:off + size].reshape(shape)
        off += size
    loss = blob[off]
    for n in SMALL_SHARDED:
        size = wt[n].shape[-1]
        small_g[n] = lax.dynamic_slice_in_dim(small_g[n], me * size, size, axis=small_g[n].ndim - 1)

    out_g, out_d, out_m, out_v = {}, {}, {}, {}
    recv_big["ada_w"] = _view2d("ada_w", g_ada_w)[None]
    for n in BIG:
        stack = recv_big[n]
        shp = wt[n].shape
        res = _adamw(stack, _view2d(n, wt[n]), _view2d(n, mt[n]), _view2d(n, vt[n]), "adamw_" + n)
        out_g[n], out_d[n], out_m[n], out_v[n] = [r.reshape(shp) for r in res]
    for n in NATIVE_SMALL:
        shp = wt[n].shape
        v2 = lambda a: a.reshape(-1, shp[-1])
        res = _adamw(v2(small_g.pop(n))[None], v2(wt[n]), v2(mt[n]), v2(vt[n]), "adamw_" + n)
        out_g[n], out_d[n], out_m[n], out_v[n] = [r.reshape(shp) for r in res]
    names = list(small_g)
    cat = lambda t: jnp.concatenate([t[n].reshape(-1) for n in names])
    gs, ws, ms, vs = cat(small_g), cat(wt), cat(mt), cat(vt)
    ns = gs.shape[0]
    rs = -(-ns // (128 * 512)) * 512
    padr = lambda a: jnp.concatenate([a, jnp.ones((rs * 128 - ns,), F32)]).reshape(rs, 128)
    res = _adamw(padr(gs)[None], padr(ws), padr(ms), padr(vs), "adamw_small")
    off = 0
    for n in names:
        size = math.prod(wt[n].shape)
        out_g[n], out_d[n], out_m[n], out_v[n] = [r.reshape(-1)[off:off + size].reshape(wt[n].shape) for r in res]
        off += size

    return (loss, grad_x[None], *[out_g[n] for n in WEIGHTS], *[out_d[n] for n in WEIGHTS],
            *[out_m[n] for n in WEIGHTS], *[out_v[n] for n in WEIGHTS])
```

```python
import math

import jax
import jax.numpy as jnp
from jax import lax
from jax.experimental import pallas as pl
from jax.experimental.pallas import tpu as pltpu

F32 = jnp.float32
BF = jnp.bfloat16
MESH = pl.DeviceIdType.MESH
N_DEV = 8

GRID_W = 64
CHUNK = 16
S5_P = 16
S5_N = 64
LN_EPS = 1e-5
DN_ALPHA = 4.0 ** 0.25
ADAM_LR, ADAM_B1, ADAM_B2, ADAM_EPS, ADAM_WD, ADAM_STEP = 1e-3, 0.9, 0.999, 1e-8, 0.01, 10
GELU_C0 = math.sqrt(2.0 / math.pi)
GELU_C1 = 0.044715
VMEM_MB = 52

ANY = pl.BlockSpec(memory_space=pl.ANY)


def _cparams():
    return pltpu.CompilerParams(vmem_limit_bytes=VMEM_MB << 20)


def _dot(a, b):
    return jnp.dot(a, b, preferred_element_type=F32)


def _dot_nt(a, b):
    return lax.dot_general(a, b, (((1,), (1,)), ((), ())), preferred_element_type=F32)


def _dot_tn(a, b):
    return lax.dot_general(a, b, (((0,), (0,)), ((), ())), preferred_element_type=F32)


def _sigmoid(x):
    return 1.0 / (1.0 + jnp.exp(-x))


def _gelu_parts(y):
    u = y * y
    th = jnp.tanh(y * (GELU_C0 + (GELU_C0 * GELU_C1) * u))
    hy = 0.5 * y
    g = hy + hy * th
    dg = (0.5 + 0.5 * th) + hy * (1.0 - th * th) * (GELU_C0 + (3.0 * GELU_C0 * GELU_C1) * u)
    return g, dg


def _full(shape):
    nd = len(shape)
    return pl.BlockSpec(shape, lambda *_: (0,) * nd)


def _mesh_pos():
    x, y, c = lax.axis_index("x"), lax.axis_index("y"), lax.axis_index("c")
    return x, y, c


def _peer(pos, k):
    x, y, c = pos
    px = 1 - x if (k >> 2) & 1 else x
    py = 1 - y if (k >> 1) & 1 else y
    pc = 1 - c if k & 1 else c
    return (px, py, pc), 4 * px + 2 * py + pc


def _shard_at(ref, axis, idx, n):
    if axis == 0:
        return ref.at[pl.ds(idx * n, n)]
    return ref.at[:, pl.ds(idx * n, n)]


class _Exchange:
    def __init__(self, kind, arrays, axes):
        self.kind, self.axes, self.n = kind, list(axes), len(arrays)
        self.arrays = list(arrays)
        self.out_shape = []
        for s, ax in zip(arrays, axes):
            shp = list(s.shape)
            if kind == "scatter":
                shp[ax] //= N_DEV
                self.out_shape.append(jax.ShapeDtypeStruct((N_DEV, *shp), s.dtype))
            else:
                shp[ax] *= N_DEV
                self.out_shape.append(jax.ShapeDtypeStruct(tuple(shp), s.dtype))
        self.scratch = [pltpu.SemaphoreType.DMA((self.n, N_DEV - 1)), pltpu.SemaphoreType.DMA((self.n, N_DEV - 1)),
                        pltpu.SemaphoreType.DMA((self.n,))]

    def _copies(self, ins, outs, sems):
        send_sems, recv_sems, local_sems = sems
        pos = _mesh_pos()
        x, y, c = pos
        me = 4 * x + 2 * y + c
        local, sends, chained, recvs = [], [], [], []
        for i in range(self.n):
            ax = self.axes[i]
            if self.kind == "scatter":
                size = ins[i].shape[ax] // N_DEV
                src = lambda idx, i=i, ax=ax, size=size: _shard_at(ins[i], ax, idx, size)
                dst = lambda idx, i=i: outs[i].at[idx]
            else:
                size = ins[i].shape[ax]
                src = lambda idx, i=i: ins[i]
                dst = lambda idx, i=i, ax=ax, size=size: _shard_at(outs[i], ax, idx, size)

            def copy(k, s, d, to, i=i):
                return pltpu.make_async_remote_copy(src_ref=s, dst_ref=d, send_sem=send_sems.at[i, k],
                                                    recv_sem=recv_sems.at[i, k], device_id=to, device_id_type=MESH)

            local.append(pltpu.make_async_copy(src(me), dst(me), local_sems.at[i]))
            if self.kind == "gather2":
                sib, sib_i = (x, y, 1 - c), 4 * x + 2 * y + (1 - c)
                chips = [(1 - x, y), (x, 1 - y), (1 - x, 1 - y)]
                sends.append(copy(0, src(me), dst(me), sib))
                recvs.append(copy(0, src(me), dst(sib_i), sib))
                for j, (cx, cy) in enumerate(chips):
                    same, other = 4 * cx + 2 * cy + c, 4 * cx + 2 * cy + (1 - c)
                    sends.append(copy(1 + j, src(me), dst(me), (cx, cy, c)))
                    chained.append((copy(1 + j, dst(same), dst(same), (cx, cy, c)), copy(4 + j, dst(same), dst(same), sib)))
                    recvs.append(copy(4 + j, dst(other), dst(other), sib))
            else:
                for k in range(1, N_DEV):
                    peer, pidx = _peer(pos, k)
                    out_src = src(pidx) if self.kind == "scatter" else src(me)
                    sends.append(copy(k - 1, out_src, dst(me), peer))
                    recvs.append(copy(k - 1, out_src, dst(pidx), peer))
        return local, sends, chained, recvs

    def start(self, ins, outs, sems):
        local, sends, _, _ = self._copies(ins, outs, sems)
        for cp in local + sends:
            cp.start()

    def wait(self, ins, outs, sems):
        local, sends, chained, recvs = self._copies(ins, outs, sems)
        for arrival, released in chained:
            arrival.wait_recv()
            released.start()
        for cp in recvs:
            cp.wait_recv()
        for cp in sends + [released for _, released in chained]:
            cp.wait_send()
        for cp in local:
            cp.wait()

    def run(self, name):
        n = self.n

        def body(*refs):
            ins, outs, sems = refs[:n], refs[n:2 * n], refs[2 * n:]
            self.start(ins, outs, sems)
            self.wait(ins, outs, sems)

        return pl.pallas_call(body, name=name, out_shape=self.out_shape, in_specs=[ANY] * n, out_specs=[ANY] * n,
                              scratch_shapes=self.scratch)(*self.arrays)


def _hosted_call(body, xch, grid, in_specs, out_specs, out_shape, scratch, args, name):
    out_specs, out_shape = list(out_specs), list(out_shape)
    n_in, n_out = len(in_specs), len(out_specs)
    if xch is None:
        res = pl.pallas_call(body, name=name, grid=grid, in_specs=in_specs, out_specs=out_specs, out_shape=out_shape,
                             scratch_shapes=list(scratch), compiler_params=_cparams())(*args)
        return list(res), []
    n = xch.n
    rank = len(grid)

    def wrapped(*refs):
        ins, x_ins = refs[:n_in], refs[n_in:n_in + n]
        outs = refs[n_in + n:n_in + n + n_out]
        x_outs = refs[n_in + n + n_out:n_in + 2 * n + n_out]
        rest = refs[n_in + 2 * n + n_out:]
        own, sems = rest[:len(rest) - 3], rest[len(rest) - 3:]
        ids = [pl.program_id(a) for a in range(rank)]
        first, last = ids[0] == 0, ids[0] == grid[0] - 1
        for a in range(1, rank):
            first = jnp.logical_and(first, ids[a] == 0)
            last = jnp.logical_and(last, ids[a] == grid[a] - 1)

        @pl.when(first)
        def _():
            xch.start(x_ins, x_outs, sems)

        body(*ins, *outs, *own)

        @pl.when(last)
        def _():
            xch.wait(x_ins, x_outs, sems)

    res = pl.pallas_call(
        wrapped, name=name, grid=grid, in_specs=list(in_specs) + [ANY] * n, out_specs=out_specs + [ANY] * n,
        out_shape=out_shape + xch.out_shape, scratch_shapes=list(scratch) + xch.scratch, compiler_params=_cparams(),
    )(*args, *xch.arrays)
    return list(res[:n_out]), list(res[n_out:])


def _all_gather(shards, axes, name, kind="gather"):
    return _Exchange(kind, shards, axes).run(name)


def _all_to_all(parts, axes, name):
    return _Exchange("scatter", parts, axes).run(name)


def _ada_fwd(cv, ada_w, ada_b):
    nl, d, wd = ada_w.shape
    r = cv.shape[0]

    def body(c_ref, w_ref, b_ref, o_ref):
        c = c_ref[...]
        s = (c * _sigmoid(c)).astype(BF)
        o_ref[0] = _dot(s, w_ref[0]) + b_ref[0]

    return pl.pallas_call(
        body, name="ada_fwd", grid=(nl,),
        in_specs=[_full((r, d)), pl.BlockSpec((1, d, wd), lambda l: (l, 0, 0)), pl.BlockSpec((1, 1, wd), lambda l: (l, 0, 0))],
        out_specs=pl.BlockSpec((1, r, wd), lambda l: (l, 0, 0)),
        out_shape=jax.ShapeDtypeStruct((nl, r, wd), F32), compiler_params=_cparams(),
    )(cv, ada_w, ada_b.reshape(nl, 1, wd))


def _ada_bwd(cv, ada_w, dm, out_dtype):
    nl, d, wd = ada_w.shape
    r = cv.shape[0]

    def body(c_ref, w_ref, dm_ref, dw_ref, dc_ref):
        c = c_ref[...]
        sg = _sigmoid(c)
        s = (c * sg).astype(BF)
        dmv = dm_ref[0].astype(BF)
        dw_ref[0] = _dot_tn(s, dmv).astype(out_dtype)
        dc_ref[0] = _dot_nt(dmv, w_ref[0]) * (sg * (1.0 + c * (1.0 - sg)))

    return pl.pallas_call(
        body, name="ada_bwd", grid=(nl,),
        in_specs=[_full((r, d)), pl.BlockSpec((1, d, wd), lambda l: (l, 0, 0)), pl.BlockSpec((1, r, wd), lambda l: (l, 0, 0))],
        out_specs=[pl.BlockSpec((1, d, wd), lambda l: (l, 0, 0)), pl.BlockSpec((1, r, d), lambda l: (l, 0, 0))],
        out_shape=[jax.ShapeDtypeStruct((nl, d, wd), out_dtype), jax.ShapeDtypeStruct((nl, r, d), F32)],
        compiler_params=_cparams(),
    )(cv, ada_w, dm)


def _sum_partials(stack):
    _, r, c = stack.shape

    def body(s_ref, o_ref):
        acc = s_ref[0]
        for p in range(1, N_DEV):
            acc = acc + s_ref[p]
        o_ref[...] = acc

    return pl.pallas_call(body, name="sum_partials", out_shape=jax.ShapeDtypeStruct((r, c), F32),
                          in_specs=[_full(stack.shape)], out_specs=_full((r, c)), grid=(1,),
                          compiler_params=_cparams())(stack)


def _adamw(gstack, w, m, v, name):
    p, r, c = gstack.shape
    tr = r
    for cand in (512 if c <= 256 else 256, 128, 64, 32, 16, 8):
        if r % cand == 0 and r > cand:
            tr = cand
            break
    bc1 = 1.0 - ADAM_B1 ** ADAM_STEP
    bc2 = 1.0 - ADAM_B2 ** ADAM_STEP

    def body(g_ref, w_ref, m_ref, v_ref, go_ref, d_ref, mo_ref, vo_ref):
        g = g_ref[0].astype(F32)
        for q in range(1, p):
            g = g + g_ref[q].astype(F32)
        mn = ADAM_B1 * m_ref[...] + (1.0 - ADAM_B1) * g
        vn = ADAM_B2 * v_ref[...] + (1.0 - ADAM_B2) * (g * g)
        go_ref[...] = g
        mo_ref[...] = mn
        vo_ref[...] = vn
        d_ref[...] = -ADAM_LR * ((mn / bc1) / (jnp.sqrt(vn / bc2) + ADAM_EPS) + ADAM_WD * w_ref[...])

    row = pl.BlockSpec((tr, c), lambda i: (i, 0))
    sds = jax.ShapeDtypeStruct((r, c), F32)
    return pl.pallas_call(
        body, name=name, grid=(r // tr,),
        in_specs=[pl.BlockSpec((p, tr, c), lambda i: (0, i, 0)), row, row, row],
        out_specs=[row, row, row, row], out_shape=[sds, sds, sds, sds], compiler_params=_cparams(),
    )(gstack, w, m, v)


def _lat_or_ctx_specs(tm, d, nl, grid_rank, row_axis):
    def lat(*ids):
        return (jnp.minimum(ids[row_axis], nl - 1), 0)

    def ctx(*ids):
        return (jnp.maximum(ids[row_axis] - nl, 0), 0)

    return pl.BlockSpec((tm, d), lat), pl.BlockSpec((tm, d), ctx)


def _sel_row(ref, is_ctx):
    return jnp.where(is_ctx, ref[1:2, :], ref[0:1, :])


def _inproj0(x, ctx, a2, b2, w, tm, xch=None):
    l, d = x.shape
    nl, nc = l // tm, ctx.shape[0] // tm
    e = w.shape[1] // 4
    half = e // 2

    def body(x_ref, c_ref, a_ref, b_ref, w_hbm, o_ref, w_ref):
        i = pl.program_id(0)

        @pl.when(i == 0)
        def _():
            pltpu.sync_copy(w_hbm, w_ref)

        is_ctx = i >= nl
        xv = jnp.where(is_ctx, c_ref[...], x_ref[...])
        h = (xv * _sel_row(a_ref, is_ctx) + _sel_row(b_ref, is_ctx)).astype(BF)
        for k in range(4):
            r = _dot(h, w_ref[:, k * e:(k + 1) * e])
            o_ref[k, 0] = r[:, :half].astype(BF)
            o_ref[k, 1] = r[:, half:].astype(BF)

    lat, cx = _lat_or_ctx_specs(tm, d, nl, 1, 0)
    (p42,), extra = _hosted_call(
        body, xch, grid=(nl + nc,),
        in_specs=[lat, cx, _full((2, d)), _full((2, d)), ANY],
        out_specs=[pl.BlockSpec((4, 2, tm, half), lambda i: (0, 0, i, 0))],
        out_shape=[jax.ShapeDtypeStruct((4, 2, l + ctx.shape[0], half), BF)],
        scratch=[pltpu.VMEM(w.shape, BF)], args=(x, ctx, a2, b2, w), name="l0_inproj")
    return p42, extra


def _conv_taps(u, w_up, w_mid, w_dn, pos, rl, tm):
    up = jnp.where(pos == 0, 0.0, pltpu.roll(u, 1, 0))
    dn = jnp.where(pos == rl - 1, 0.0, pltpu.roll(u, tm - 1, 0))
    return w_up * up + w_mid * u + w_dn * dn, up, dn


def _conv_halo_specs(tm, tc, nl, lead):
    hb = tm // GRID_W

    def prev(j, i):
        return (0, 1, jnp.maximum(jnp.minimum(i, nl - 1) * hb - 1, 0), j)

    def nxt(j, i):
        return (0, 1, jnp.minimum((jnp.minimum(i, nl - 1) + 1) * hb, nl * hb - 1), j)

    return pl.BlockSpec((lead, 1, GRID_W, tc), prev), pl.BlockSpec((lead, 1, GRID_W, tc), nxt)


def _conv_fwd(p42, cw, nl, tm, tc):
    _, _, r, half = p42.shape
    nt = r // tm

    def body(p_ref, hp_ref, hn_ref, cw_ref, o_ref):
        i = pl.program_id(1)
        is_ctx = i >= nl
        row = lax.broadcasted_iota(jnp.int32, (tm, tc), 0)
        rl = jnp.where(is_ctx, tm, GRID_W)
        pos = jnp.bitwise_and(row, rl - 1)

        def gate(hv, yc):
            bg = p_ref[0, hv].astype(F32)
            z = p_ref[3, hv].astype(F32)
            return (bg * yc * (z * _sigmoid(z))).astype(BF)

        u_h = p_ref[1, 0].astype(F32) * p_ref[2, 0].astype(F32)
        w_h = cw_ref[:, 0, :]
        o_ref[0] = gate(0, _conv_taps(u_h, w_h[0:1], w_h[1:2], w_h[2:3], pos, rl, tm)[0])
        u_v = p_ref[1, 1].astype(F32) * p_ref[2, 1].astype(F32)
        w_v = cw_ref[:, 1, :]

        @pl.when(is_ctx)
        def _():
            o_ref[1] = gate(1, _conv_taps(u_v, w_v[0:1], w_v[1:2], w_v[2:3], pos, rl, tm)[0])

        @pl.when(jnp.logical_not(is_ctx))
        def _():
            up = hp_ref[1, 0].astype(F32) * hp_ref[2, 0].astype(F32) * (i > 0).astype(F32)
            dn = hn_ref[1, 0].astype(F32) * hn_ref[2, 0].astype(F32) * (i < nl - 1).astype(F32)
            ext = jnp.concatenate([up, u_v, dn], axis=0)
            yc = w_v[0:1] * ext[0:tm] + w_v[1:2] * u_v + w_v[2:3] * ext[2 * GRID_W:tm + 2 * GRID_W]
            o_ref[1] = gate(1, yc)

    hp, hn = _conv_halo_specs(tm, tc, nl, 4)
    return pl.pallas_call(
        body, name="l0_conv_fwd", grid=(half // tc, nt),
        in_specs=[pl.BlockSpec((4, 2, tm, tc), lambda j, i: (0, 0, i, j)), hp, hn,
                  pl.BlockSpec((3, 2, tc), lambda j, i: (0, 0, j))],
        out_specs=pl.BlockSpec((2, tm, tc), lambda j, i: (0, i, j)),
        out_shape=jax.ShapeDtypeStruct((2, r, half), BF), compiler_params=_cparams(),
    )(p42, p42, p42, cw)


def _outproj_ln0(q3, w_out, x, ctx, gt2, tm):
    l, d = x.shape
    lc = ctx.shape[0]
    nl, nc = l // tm, lc // tm
    _, r, half = q3.shape
    tjo = tm // CHUNK

    def body(q_ref, w_hbm, x_ref, c_ref, g_ref, xl_ref, xc_ref, rl_ref, rc_ref, fx_ref, w_ref, xs_ref, rs_ref):
        i = pl.program_id(0)

        @pl.when(i == 0)
        def _():
            pltpu.sync_copy(w_hbm, w_ref)

        is_ctx = i >= nl
        fx = _dot(q_ref[0], w_ref[:half, :]) + _dot(q_ref[1], w_ref[half:, :])
        xv = jnp.where(is_ctx, c_ref[...], x_ref[...])
        rr = DN_ALPHA * xv + _sel_row(g_ref, is_ctx) * fx
        mu = jnp.mean(rr, axis=-1, keepdims=True)
        cen = rr - mu
        rstd = lax.rsqrt(jnp.mean(cen * cen, axis=-1, keepdims=True) + LN_EPS)
        xh = cen * rstd
        for lb in range(d // 128):
            xs_ref[lb] = xh[:, lb * 128:(lb + 1) * 128]
        rs_ref[...] = jnp.broadcast_to(rstd, (tm, 128))
        fx_ref[...] = fx.astype(BF)

        def to_cr(xo_ref, ro_ref):
            for s in range(CHUNK):
                for lb in range(d // 128):
                    xo_ref[:, s * d + lb * 128:s * d + (lb + 1) * 128] = xs_ref.at[lb][pl.ds(s, tjo, stride=CHUNK), :]
                ro_ref[:, s * 128:(s + 1) * 128] = rs_ref[pl.ds(s, tjo, stride=CHUNK), :]

        @pl.when(jnp.logical_not(is_ctx))
        def _():
            to_cr(xl_ref, rl_ref)

        @pl.when(is_ctx)
        def _():
            to_cr(xc_ref, rc_ref)

    lat, cx = _lat_or_ctx_specs(tm, d, nl, 1, 0)
    lat_o = lambda w_: pl.BlockSpec((tjo, CHUNK * w_), lambda i: (jnp.minimum(i, nl - 1), 0))
    ctx_o = lambda w_: pl.BlockSpec((tjo, CHUNK * w_), lambda i: (jnp.maximum(i - nl, 0), 0))
    return pl.pallas_call(
        body, name="l0_outproj_ln", grid=(nl + nc,),
        in_specs=[pl.BlockSpec((2, tm, half), lambda i: (0, i, 0)), ANY, lat, cx, _full((2, d))],
        out_specs=[lat_o(d), ctx_o(d), lat_o(128), ctx_o(128), pl.BlockSpec((tm, d), lambda i: (i, 0))],
        out_shape=[jax.ShapeDtypeStruct((l // CHUNK, CHUNK * d), F32), jax.ShapeDtypeStruct((lc // CHUNK, CHUNK * d), F32),
                   jax.ShapeDtypeStruct((l // CHUNK, CHUNK * 128), F32), jax.ShapeDtypeStruct((lc // CHUNK, CHUNK * 128), F32),
                   jax.ShapeDtypeStruct((r, d), BF)],
        scratch_shapes=[pltpu.VMEM(w_out.shape, BF), pltpu.VMEM((d // 128, tm, 128), F32), pltpu.VMEM((tm, 128), F32)],
        compiler_params=_cparams(),
    )(q3, w_out, x, ctx, gt2)


def _bwd_outproj0(dr_l, dr_c, gt2, w_out, fx, tm):
    l, d = dr_l.shape
    nl, nc = l // tm, dr_c.shape[0] // tm
    e = w_out.shape[0]
    half = e // 2
    r = l + dr_c.shape[0]

    def body(dl_ref, dc_ref, g_ref, w_hbm, fx_ref, dq_ref, acc_ref, w_ref):
        i = pl.program_id(0)

        @pl.when(i == 0)
        def _():
            pltpu.sync_copy(w_hbm, w_ref)
            acc_ref[...] = jnp.zeros_like(acc_ref)

        is_ctx = i >= nl
        dr = jnp.where(is_ctx, dc_ref[...], dl_ref[...]).astype(F32)
        dfx = (dr * _sel_row(g_ref, is_ctx)).astype(BF)
        dq_ref[0] = _dot_nt(dfx, w_ref[:half, :]).astype(BF)
        dq_ref[1] = _dot_nt(dfx, w_ref[half:, :]).astype(BF)
        s = jnp.sum(dr * fx_ref[...].astype(F32), axis=0, keepdims=True)
        sel = is_ctx.astype(F32)
        acc_ref[0:1, :] += s * (1.0 - sel)
        acc_ref[1:2, :] += s * sel

    lat, cx = _lat_or_ctx_specs(tm, d, nl, 1, 0)
    return pl.pallas_call(
        body, name="l0_bwd_outproj", grid=(nl + nc,),
        in_specs=[lat, cx, _full((2, d)), ANY, pl.BlockSpec((tm, d), lambda i: (i, 0))],
        out_specs=[pl.BlockSpec((2, tm, half), lambda i: (0, i, 0)), _full((8, d))],
        out_shape=[jax.ShapeDtypeStruct((2, r, half), BF), jax.ShapeDtypeStruct((8, d), F32)],
        scratch_shapes=[pltpu.VMEM(w_out.shape, BF)], compiler_params=_cparams(),
    )(dr_l, dr_c, gt2, w_out, fx)


def _conv_bwd_inproj0(dq3, p42, cw, w_in, x, ctx, dr_l, dr_c, a2, nl, tm, xch=None):
    l, d = x.shape
    _, _, r, half = p42.shape
    nt = r // tm
    e = 2 * half
    cc = min(512, half)
    n_cc = half // cc

    def body(dq_ref, dqp_ref, dqn_ref, p_ref, hp_ref, hn_ref, cw_ref, w_hbm, x_ref, c_ref, dl_ref, dc_ref, a_ref,
             dp_ref, dw_ref, gx_ref, acc_ref, w_ref, dh_ref):
        i, hv = pl.program_id(0), pl.program_id(1)
        is_ctx = i >= nl

        @pl.when(jnp.logical_and(i == 0, hv == 0))
        def _():
            pltpu.sync_copy(w_hbm, w_ref)
            acc_ref[...] = jnp.zeros_like(acc_ref)
            dw_ref[...] = jnp.zeros_like(dw_ref)

        row = lax.broadcasted_iota(jnp.int32, (tm, cc), 0)
        rl = jnp.where(is_ctx, tm, GRID_W)
        pos = jnp.bitwise_and(row, rl - 1)

        def pieces(dq, bg, z):
            sz = _sigmoid(z)
            sil = z * sz
            return dq * bg * sil, dq * sil, dq * bg * (sz * (1.0 + z * (1.0 - sz)))

        def emit(hvs, c, parts, dyc, u_up, u, u_dn, dh):
            lanes = slice(c * cc, (c + 1) * cc)
            for k, part in enumerate(parts):
                pb = part.astype(BF)
                dp_ref[k, 0, :, lanes] = pb
                c0 = k * e + hvs * half + c * cc
                t = _dot_nt(pb, w_ref[:, c0:c0 + cc])
                dh = t if dh is None else dh + t
            dw_ref[0:1, hvs, lanes] += jnp.sum(dyc * u_up, axis=0, keepdims=True)
            dw_ref[1:2, hvs, lanes] += jnp.sum(dyc * u, axis=0, keepdims=True)
            dw_ref[2:3, hvs, lanes] += jnp.sum(dyc * u_dn, axis=0, keepdims=True)
            return dh

        def seq_half(hvs):
            dh = None
            for c in range(n_cc):
                lanes = slice(c * cc, (c + 1) * cc)
                bg, cg = p_ref[0, 0, :, lanes].astype(F32), p_ref[1, 0, :, lanes].astype(F32)
                v, z = p_ref[2, 0, :, lanes].astype(F32), p_ref[3, 0, :, lanes].astype(F32)
                w = cw_ref[:, hvs, lanes]
                u = cg * v
                yc, u_up, u_dn = _conv_taps(u, w[0:1], w[1:2], w[2:3], pos, rl, tm)
                dyc, dbg_f, dz_f = pieces(dq_ref[0, :, lanes].astype(F32), bg, z)
                du = _conv_taps(dyc, w[2:3], w[1:2], w[0:1], pos, rl, tm)[0]
                dh = emit(hvs, c, (dbg_f * yc, du * v, du * cg, dz_f * yc), dyc, u_up, u, u_dn, dh)
            return dh

        def col_half():
            m_up = (i > 0).astype(F32)
            m_dn = (i < nl - 1).astype(F32)
            dh = None
            for c in range(n_cc):
                lanes = slice(c * cc, (c + 1) * cc)
                bg, cg = p_ref[0, 0, :, lanes].astype(F32), p_ref[1, 0, :, lanes].astype(F32)
                v, z = p_ref[2, 0, :, lanes].astype(F32), p_ref[3, 0, :, lanes].astype(F32)
                w = cw_ref[:, 1, lanes]
                u = cg * v

                def halo(h_ref, dqh_ref, msk):
                    hb, hc = h_ref[0, 0, :, lanes].astype(F32), h_ref[1, 0, :, lanes].astype(F32)
                    hv_, hz = h_ref[2, 0, :, lanes].astype(F32), h_ref[3, 0, :, lanes].astype(F32)
                    return hc * hv_ * msk, pieces(dqh_ref[0, :, lanes].astype(F32), hb, hz)[0] * msk

                u_p, dyc_p = halo(hp_ref, dqp_ref, m_up)
                u_n, dyc_n = halo(hn_ref, dqn_ref, m_dn)
                u_ext = jnp.concatenate([u_p, u, u_n], axis=0)
                u_up, u_dn = u_ext[0:tm], u_ext[2 * GRID_W:tm + 2 * GRID_W]
                yc = w[0:1] * u_up + w[1:2] * u + w[2:3] * u_dn
                dyc, dbg_f, dz_f = pieces(dq_ref[0, :, lanes].astype(F32), bg, z)
                d_ext = jnp.concatenate([dyc_p, dyc, dyc_n], axis=0)
                du = w[0:1] * d_ext[2 * GRID_W:tm + 2 * GRID_W] + w[1:2] * dyc + w[2:3] * d_ext[0:tm]
                dh = emit(1, c, (dbg_f * yc, du * v, du * cg, dz_f * yc), dyc, u_up, u, u_dn, dh)
            return dh

        @pl.when(hv == 0)
        def _():
            dh_ref[...] = seq_half(0)

        @pl.when(jnp.logical_and(hv == 1, is_ctx))
        def _():
            dh_ref[...] += seq_half(1)

        @pl.when(jnp.logical_and(hv == 1, jnp.logical_not(is_ctx)))
        def _():
            dh_ref[...] += col_half()

        @pl.when(hv == 1)
        def _():
            dh = dh_ref[...]
            xv = jnp.where(is_ctx, c_ref[...], x_ref[...])
            s_sc = jnp.sum(dh * xv, axis=0, keepdims=True)
            s_sh = jnp.sum(dh, axis=0, keepdims=True)
            sel = is_ctx.astype(F32)
            acc_ref[0:1, :] += s_sc * (1.0 - sel)
            acc_ref[1:2, :] += s_sc * sel
            acc_ref[2:3, :] += s_sh * (1.0 - sel)
            acc_ref[3:4, :] += s_sh * sel

        @pl.when(jnp.logical_and(hv == 1, jnp.logical_not(is_ctx)))
        def _():
            gx_ref[...] = DN_ALPHA * dl_ref[...].astype(F32) + dh_ref[...] * a_ref[0:1, :]

    hb = tm // GRID_W
    prev_blk = lambda i: jnp.maximum(jnp.minimum(i, nl - 1) * hb - 1, 0)
    next_blk = lambda i: jnp.minimum((jnp.minimum(i, nl - 1) + 1) * hb, nl * hb - 1)
    lat, cx = _lat_or_ctx_specs(tm, d, nl, 2, 0)
    (dp42, dcw, gx, acc), extra = _hosted_call(
        body, xch, grid=(nt, 2),
        in_specs=[pl.BlockSpec((1, tm, half), lambda i, h: (h, i, 0)),
                  pl.BlockSpec((1, GRID_W, half), lambda i, h: (1, prev_blk(i), 0)),
                  pl.BlockSpec((1, GRID_W, half), lambda i, h: (1, next_blk(i), 0)),
                  pl.BlockSpec((4, 1, tm, half), lambda i, h: (0, h, i, 0)),
                  pl.BlockSpec((4, 1, GRID_W, half), lambda i, h: (0, 1, prev_blk(i), 0)),
                  pl.BlockSpec((4, 1, GRID_W, half), lambda i, h: (0, 1, next_blk(i), 0)),
                  _full((3, 2, half)), ANY, lat, cx, lat, cx, _full((2, d))],
        out_specs=[pl.BlockSpec((4, 1, tm, half), lambda i, h: (0, h, i, 0)), _full((8, 2, half)),
                   pl.BlockSpec((tm, d), lambda i, h: (jnp.minimum(i, nl - 1), 0)), _full((8, d))],
        out_shape=[jax.ShapeDtypeStruct(p42.shape, BF), jax.ShapeDtypeStruct((8, 2, half), F32),
                   jax.ShapeDtypeStruct((l, d), F32), jax.ShapeDtypeStruct((8, d), F32)],
        scratch=[pltpu.VMEM(w_in.shape, BF), pltpu.VMEM((tm, d), F32)],
        args=(dq3, dq3, dq3, p42, p42, p42, cw, w_in, x, ctx, dr_l, dr_c, a2), name="l0_conv_bwd_inproj")
    return dp42, dcw, gx, acc, extra


def _dw_inproj0(x, ctx, a2, b2, dp42, tm):
    l, d = x.shape
    lc = ctx.shape[0]
    assert lc == tm
    tl = 4 * tm if l % (4 * tm) == 0 else tm
    nl = l // tl
    half = dp42.shape[-1]
    e = 2 * half

    def body(x_ref, c_ref, a_ref, b_ref, dpl_ref, dpc_ref, o_ref, acc_ref):
        i = pl.program_id(1)

        @pl.when(i == 0)
        def _():
            acc_ref[...] = jnp.zeros_like(acc_ref)

        def add(rows_ref, dp_ref, sel):
            h = (rows_ref[...] * a_ref[sel:sel + 1, :] + b_ref[sel:sel + 1, :]).astype(BF)
            acc_ref[:, :half] += _dot_tn(h, dp_ref[0, 0])
            acc_ref[:, half:] += _dot_tn(h, dp_ref[0, 1])

        @pl.when(i < nl)
        def _():
            add(x_ref, dpl_ref, 0)

        @pl.when(i == nl)
        def _():
            add(c_ref, dpc_ref, 1)
            o_ref[...] = acc_ref[...].astype(BF)

    return pl.pallas_call(
        body, name="l0_dw_inproj", grid=(4, nl + 1),
        in_specs=[pl.BlockSpec((tl, d), lambda k, i: (jnp.minimum(i, nl - 1), 0)), _full((lc, d)),
                  _full((2, d)), _full((2, d)),
                  pl.BlockSpec((1, 2, tl, half), lambda k, i: (k, 0, jnp.minimum(i, nl - 1), 0)),
                  pl.BlockSpec((1, 2, lc, half), lambda k, i: (k, 0, l // lc, 0))],
        out_specs=pl.BlockSpec((d, e), lambda k, i: (0, k)),
        out_shape=jax.ShapeDtypeStruct((d, 4 * e), BF),
        scratch_shapes=[pltpu.VMEM((d, e), F32)], compiler_params=_cparams(),
    )(x, ctx, a2, b2, dp42, dp42)


def _dw_outproj0(q3, dr_l, dr_c, gt2, tm, xch=None):
    l, d = dr_l.shape
    nl, nc = l // tm, dr_c.shape[0] // tm
    _, r, half = q3.shape
    nt = nl + nc

    def body(q_ref, dl_ref, dc_ref, g_ref, o_ref, acc_ref):
        i = pl.program_id(0)
        is_ctx = i >= nl

        @pl.when(i == 0)
        def _():
            acc_ref[...] = jnp.zeros_like(acc_ref)

        dr = jnp.where(is_ctx, dc_ref[...], dl_ref[...]).astype(F32)
        dfx = (dr * _sel_row(g_ref, is_ctx)).astype(BF)
        acc_ref[:half, :] += _dot_tn(q_ref[0], dfx)
        acc_ref[half:, :] += _dot_tn(q_ref[1], dfx)

        @pl.when(i == nt - 1)
        def _():
            o_ref[...] = acc_ref[...].astype(BF)

    lat, cx = _lat_or_ctx_specs(tm, d, nl, 1, 0)
    (g_w,), extra = _hosted_call(
        body, xch, grid=(nt,),
        in_specs=[pl.BlockSpec((2, tm, half), lambda i: (0, i, 0)), lat, cx, _full((2, d))],
        out_specs=[_full((2 * half, d))], out_shape=[jax.ShapeDtypeStruct((2 * half, d), BF)],
        scratch=[pltpu.VMEM((2 * half, d), F32)], args=(q3, dr_l, dr_c, gt2), name="l0_dw_outproj")
    return g_w, extra


def _cr_tile(j, cap=256):
    for cand in (1024, 512, 256, 128, 64, 32, 16, 8):
        if cand <= cap and j % cand == 0:
            return cand
    raise ValueError(j)


def _final(w_cr, w_out, xh_cr, tgt_cr, vecs):
    j, e16 = w_cr.shape
    e = e16 // CHUNK
    d = w_out.shape[1]
    tj = _cr_tile(j)

    def body(w_ref, wo_hbm, xh_ref, t_ref, v_ref, dr_ref, acc_ref, wo_ref):
        @pl.when(jnp.logical_and(pl.program_id(0) == 0, pl.program_id(1) == 0))
        def _():
            pltpu.sync_copy(wo_hbm, wo_ref)
            acc_ref[...] = jnp.zeros_like(acc_ref)

        o = _dot(w_ref[...], wo_ref[...])
        x1 = xh_ref[...] * v_ref[0:1, :] + v_ref[1:2, :]
        rr = DN_ALPHA * x1 + v_ref[2:3, :] * o
        mu = jnp.mean(rr, axis=-1, keepdims=True)
        cen = rr - mu
        rstd = lax.rsqrt(jnp.mean(cen * cen, axis=-1, keepdims=True) + LN_EPS)
        xh2 = cen * rstd
        err = xh2 * v_ref[3:4, :] + v_ref[4:5, :] - t_ref[...]
        dy = err * (1.0 / d)
        dxh = dy * v_ref[3:4, :]
        dr = rstd * (dxh - jnp.mean(dxh, axis=-1, keepdims=True) - xh2 * jnp.mean(dxh * xh2, axis=-1, keepdims=True))
        dr_ref[...] = dr.astype(BF)
        acc_ref[0:1, :] += jnp.sum(dy * xh2, axis=0, keepdims=True)
        acc_ref[1:2, :] += jnp.sum(dy, axis=0, keepdims=True)
        acc_ref[2:3, :] += jnp.sum(dr * o, axis=0, keepdims=True)
        acc_ref[3:4, :] += (0.5 / d) * jnp.sum(err * err, axis=0, keepdims=True)

    tok_d = pl.BlockSpec((tj, d), lambda t, s: (t, s))
    return pl.pallas_call(
        body, name="l1_final", grid=(j // tj, CHUNK),
        in_specs=[pl.BlockSpec((tj, e), lambda t, s: (t, s)), ANY, tok_d, tok_d, _full((8, d))],
        out_specs=[tok_d, _full((8, d))],
        out_shape=[jax.ShapeDtypeStruct((j, CHUNK * d), BF), jax.ShapeDtypeStruct((8, d), F32)],
        scratch_shapes=[pltpu.VMEM(w_out.shape, BF)], compiler_params=_cparams(),
    )(w_cr, w_out, xh_cr, tgt_cr, vecs)


def _dw_cr(lhs, rhs, lhs_kind, rhs_kind, vec, bias_sum, init, name):
    j = lhs.shape[0]
    k = lhs.shape[1] // CHUNK
    n = rhs.shape[1] // CHUNK
    tj = _cr_tile(j, 512)
    nh = 2 if k * n * 4 > (8 << 20) else 1
    tn = n // nh
    nt = j // tj
    has_init = init is not None

    def body(*refs):
        refs = list(refs)
        l_ref, r_ref = refs[0], refs[1]
        pos = 2
        v_ref = None
        if vec is not None:
            v_ref = refs[pos]
            pos += 1
        i_ref = None
        if has_init:
            i_ref = refs[pos]
            pos += 1
        o_ref = refs[pos]
        pos += 1
        bs_ref = None
        if bias_sum:
            bs_ref = refs[pos]
            pos += 1
        acc_ref = refs[pos]
        t, s = pl.program_id(1), pl.program_id(2)
        first = jnp.logical_and(t == 0, s == 0)

        @pl.when(first)
        def _():
            acc_ref[...] = i_ref[...] if has_init else jnp.zeros_like(acc_ref)
            if bias_sum:
                bs_ref[...] = jnp.zeros_like(bs_ref)

        if lhs_kind == "mod":
            lv = (l_ref[...] * v_ref[0:1, :] + v_ref[1:2, :]).astype(BF)
        else:
            lv = l_ref[...]
        if rhs_kind == "scaled":
            rv = (r_ref[...].astype(F32) * v_ref[0:1, :]).astype(BF)
        else:
            rv = r_ref[...]
        acc_ref[...] += _dot_tn(lv, rv)
        if bias_sum:
            bs_ref[0:1, :] += jnp.sum(rv.astype(F32), axis=0, keepdims=True)

        @pl.when(jnp.logical_and(t == nt - 1, s == CHUNK - 1))
        def _():
            o_ref[...] = acc_ref[...].astype(BF)

    l_spec = pl.BlockSpec((tj, k), lambda h, t, s: (t, s))
    r_spec = pl.BlockSpec((tj, tn), lambda h, t, s: (t, s * nh + h))
    in_specs, args = [l_spec, r_spec], [lhs, rhs]
    if vec is not None:
        in_specs.append(_full(vec.shape))
        args.append(vec)
    o_spec = pl.BlockSpec((k, tn), lambda h, t, s: (0, h))
    if has_init:
        in_specs.append(o_spec)
        args.append(init)
    out_specs, out_shape = [o_spec], [jax.ShapeDtypeStruct((k, n), BF)]
    if bias_sum:
        out_specs.append(pl.BlockSpec((8, tn), lambda h, t, s: (0, h)))
        out_shape.append(jax.ShapeDtypeStruct((8, n), F32))
    res = pl.pallas_call(
        body, name=name, grid=(nh, nt, CHUNK), in_specs=in_specs, out_specs=out_specs, out_shape=out_shape,
        scratch_shapes=[pltpu.VMEM((k, tn), F32)], compiler_params=_cparams(),
    )(*args)
    return res if bias_sum else res[0]


GT_ROWS = CHUNK * S5_P
ZG_W = 2 * 2 * S5_N
PAIR_W = 2 * ZG_W
GROUPS_PER_STEP = 4


def _inproj1_gt(xh_cr, a1, b1, wu_t, w_z, tag):
    j, d16 = xh_cr.shape
    d = d16 // CHUNK
    e = wu_t.shape[0]
    g = e // S5_P
    tj = _cr_tile(j, 256)

    def body(x_ref, a_ref, b_ref, wu_hbm, wz_hbm, u_ref, z_ref, wu_ref, wz_ref):
        @pl.when(jnp.logical_and(pl.program_id(0) == 0, pl.program_id(1) == 0))
        def _():
            pltpu.sync_copy(wu_hbm, wu_ref)
            pltpu.sync_copy(wz_hbm, wz_ref)

        h = (x_ref[...] * a_ref[...] + b_ref[...]).astype(BF)
        u_ref[...] = _dot_nt(wu_ref[...], h).reshape(g, S5_P, tj).astype(BF)
        z_ref[...] = _dot(h, wz_ref[...]).astype(BF)

    return pl.pallas_call(
        body, name="l1_inproj_" + tag, grid=(j // tj, CHUNK),
        in_specs=[pl.BlockSpec((tj, d), lambda t, s: (t, s)), _full((1, d)), _full((1, d)), ANY, ANY],
        out_specs=[pl.BlockSpec((g, S5_P, tj), lambda t, s: (0, s, t)), pl.BlockSpec((tj, e), lambda t, s: (t, s))],
        out_shape=[jax.ShapeDtypeStruct((g, GT_ROWS, j), BF), jax.ShapeDtypeStruct((j, CHUNK * e), BF)],
        scratch_shapes=[pltpu.VMEM(wu_t.shape, BF), pltpu.VMEM(w_z.shape, BF)], compiler_params=_cparams(),
    )(xh_cr, a1, b1, wu_t, w_z)


def _gt_spec(j, gb=GROUPS_PER_STEP):
    return pl.BlockSpec((gb, GT_ROWS, j), lambda i: (i, 0, 0))


def _zg_spec(j, gb=GROUPS_PER_STEP):
    return pl.BlockSpec((j, gb * ZG_W), lambda i: (0, i))


def _w_spec(width, gb=GROUPS_PER_STEP):
    return pl.BlockSpec((gb, GT_ROWS, width), lambda i: (i, 0, 0))


def _pair_lanes(k):
    return slice((k // 2) * PAIR_W, (k // 2 + 1) * PAIR_W)


def _s5_z(ut_l, ut_c, bc):
    g, _, jl = ut_l.shape
    jc = ut_c.shape[2]
    gb = GROUPS_PER_STEP

    def body(ul_ref, uc_ref, bc_ref, zl_ref, zc_ref):
        for k in range(0, gb, 2):
            zl_ref[:, _pair_lanes(k)] = _dot_tn(ul_ref[k], bc_ref[k]) + _dot_tn(ul_ref[k + 1], bc_ref[k + 1])
            zc_ref[:, _pair_lanes(k)] = _dot_tn(uc_ref[k], bc_ref[k]) + _dot_tn(uc_ref[k + 1], bc_ref[k + 1])

    return pl.pallas_call(
        body, name="l1_s5_z", grid=(g // gb,), in_specs=[_gt_spec(jl), _gt_spec(jc), _w_spec(PAIR_W)],
        out_specs=[_zg_spec(jl), _zg_spec(jc)],
        out_shape=[jax.ShapeDtypeStruct((jl, g * ZG_W), F32), jax.ShapeDtypeStruct((jc, g * ZG_W), F32)],
        compiler_params=_cparams(),
    )(ut_l, ut_c, bc)


def _s5_y(ut_l, s_l, mt_t, cct):
    g, _, jl = ut_l.shape
    gb = GROUPS_PER_STEP

    def body(u_ref, s_ref, mt_ref, cc_ref, y_ref):
        for k in range(gb):
            s_k = s_ref[:, _pair_lanes(k)].astype(BF)
            y_ref[k] = (_dot(mt_ref[k], u_ref[k]) + _dot_nt(cc_ref[k], s_k)).astype(BF)

    return pl.pallas_call(
        body, name="l1_s5_y", grid=(g // gb,),
        in_specs=[_gt_spec(jl), _zg_spec(jl), _w_spec(GT_ROWS), _w_spec(PAIR_W)],
        out_specs=_gt_spec(jl), out_shape=jax.ShapeDtypeStruct((g, GT_ROWS, jl), BF), compiler_params=_cparams(),
    )(ut_l, s_l, mt_t, cct)


def _s5_ds(dyt_l, cct):
    g, _, jl = dyt_l.shape
    gb = GROUPS_PER_STEP

    def body(dy_ref, cc_ref, ds_ref):
        for k in range(0, gb, 2):
            ds_ref[:, _pair_lanes(k)] = _dot_tn(dy_ref[k], cc_ref[k]) + _dot_tn(dy_ref[k + 1], cc_ref[k + 1])

    return pl.pallas_call(
        body, name="l1_s5_ds", grid=(g // gb,), in_specs=[_gt_spec(jl), _w_spec(PAIR_W)], out_specs=_zg_spec(jl),
        out_shape=jax.ShapeDtypeStruct((jl, g * ZG_W), F32), compiler_params=_cparams(),
    )(dyt_l, cct)


def _s5_dx(dyt_l, dz_l, dz_c, mt, bc):
    g, _, jl = dyt_l.shape
    jc = dz_c.shape[0]
    gb = GROUPS_PER_STEP

    def body(dy_ref, dzl_ref, dzc_ref, mt_ref, bc_ref, dul_ref, duc_ref):
        for k in range(gb):
            dzl = dzl_ref[:, _pair_lanes(k)].astype(BF)
            dzc = dzc_ref[:, _pair_lanes(k)].astype(BF)
            dul_ref[k] = (_dot(mt_ref[k], dy_ref[k]) + _dot_nt(bc_ref[k], dzl)).astype(BF)
            duc_ref[k] = _dot_nt(bc_ref[k], dzc).astype(BF)

    return pl.pallas_call(
        body, name="l1_s5_dx", grid=(g // gb,),
        in_specs=[_gt_spec(jl), _zg_spec(jl), _zg_spec(jc), _w_spec(GT_ROWS), _w_spec(PAIR_W)],
        out_specs=[_gt_spec(jl), _gt_spec(jc)],
        out_shape=[jax.ShapeDtypeStruct((g, GT_ROWS, jl), BF), jax.ShapeDtypeStruct((g, GT_ROWS, jc), BF)],
        compiler_params=_cparams(),
    )(dyt_l, dz_l, dz_c, mt, bc)


def _s5_dw(ut_l, ut_c, dyt_l, dz_l, dz_c, s_l):
    g, _, jl = ut_l.shape
    jc = ut_c.shape[2]
    gb = GROUPS_PER_STEP

    def body(ul_ref, uc_ref, dy_ref, dzl_ref, dzc_ref, s_ref, dmt_ref, dbc_ref, dcc_ref):
        for k in range(gb):
            lanes = _pair_lanes(k)
            dmt_ref[k] = _dot_nt(ul_ref[k], dy_ref[k])
            dbc_ref[k] = (_dot(ul_ref[k], dzl_ref[:, lanes].astype(BF))
                          + _dot(uc_ref[k], dzc_ref[:, lanes].astype(BF)))
            dcc_ref[k] = _dot(dy_ref[k], s_ref[:, lanes].astype(BF))

    sd_m = jax.ShapeDtypeStruct((g, GT_ROWS, GT_ROWS), F32)
    sd_p = jax.ShapeDtypeStruct((g, GT_ROWS, PAIR_W), F32)
    return pl.pallas_call(
        body, name="l1_s5_dw", grid=(g // gb,),
        in_specs=[_gt_spec(jl), _gt_spec(jc), _gt_spec(jl), _zg_spec(jl), _zg_spec(jc), _zg_spec(jl)],
        out_specs=[_w_spec(GT_ROWS), _w_spec(PAIR_W), _w_spec(PAIR_W)], out_shape=[sd_m, sd_p, sd_p],
        compiler_params=_cparams(),
    )(ut_l, ut_c, dyt_l, dz_l, dz_c, s_l)


def _scan_g(z_l, z_c, coef, chains, conj, s_l=None, s_c=None, name="l1_scan"):
    jl, w_all = z_l.shape
    jc = z_c.shape[0]
    gb = 2 * GROUPS_PER_STEP if w_all % (2 * GROUPS_PER_STEP * ZG_W) == 0 else GROUPS_PER_STEP
    wb = gb * ZG_W
    nch = wb // 256
    with_da = s_l is not None
    sign = -1.0 if conj else 1.0

    def body(*refs):
        zl_ref, zc_ref, cf_ref = refs[:3]
        k0 = 3
        if with_da:
            sl_ref, sc_ref = refs[3:5]
            k0 = 5
        ol_ref, oc_ref = refs[k0:k0 + 2]
        rowi = lax.broadcasted_iota(jnp.int32, (8, 128), 0)

        def lanes_of(ch):
            return slice(ch * 256, ch * 256 + 128), slice(ch * 256 + 128, (ch + 1) * 256)

        def coefs(ch, r0, nr):
            lr, li = lanes_of(ch)
            return cf_ref[r0:r0 + nr, lr], sign * cf_ref[r0:r0 + nr, li]

        def shift(v, sh, rev):
            if rev:
                return jnp.where(rowi < 8 - sh, pltpu.roll(v, 8 - sh, 0), 0.0)
            return jnp.where(rowi >= sh, pltpu.roll(v, sh, 0), 0.0)

        zero_row = jnp.zeros((1, 128), F32)
        zero_tile = jnp.zeros((8, 128), F32)
        carry = [zero_row] * (2 * nch)
        da = [zero_tile] * (2 * nch)
        for seg in range(len(chains[0])):
            which = chains[0][seg][0]
            assert chains[1][seg][0] == which
            revs = (chains[0][seg][1], chains[1][seg][1])
            src, dst = (zc_ref, oc_ref) if which == "c" else (zl_ref, ol_ref)
            sref = ((sc_ref if which == "c" else sl_ref) if with_da else None)
            ng = (jc if which == "c" else jl) // 8

            def step(it, st, src=src, dst=dst, sref=sref, ng=ng, revs=revs):
                carry_, da_ = list(st[:2 * nch]), list(st[2 * nch:])
                for ch in range(nch):
                    rev = revs[ch % 2]
                    lr, li = lanes_of(ch)
                    grp = (ng - 1 - it) if rev else it
                    off = pl.multiple_of(grp * 8, 8)
                    xr, xi = src[pl.ds(off, 8), lr], src[pl.ds(off, 8), li]
                    for sh, r0 in ((1, 0), (2, 1), (4, 2)):
                        ar, ai = coefs(ch, r0, 1)
                        sr, si = shift(xr, sh, rev), shift(xi, sh, rev)
                        xr, xi = xr + ar * sr - ai * si, xi + ar * si + ai * sr
                    tr, ti = coefs(ch, 16, 8) if rev else coefs(ch, 8, 8)
                    cr_, ci_ = carry_[2 * ch], carry_[2 * ch + 1]
                    ir = xr + tr * cr_ - ti * ci_
                    ii = xi + tr * ci_ + ti * cr_
                    if rev:
                        er = jnp.where(rowi == 7, cr_, pltpu.roll(ir, 7, 0))
                        ei = jnp.where(rowi == 7, ci_, pltpu.roll(ii, 7, 0))
                        carry_[2 * ch], carry_[2 * ch + 1] = ir[0:1], ii[0:1]
                    else:
                        er = jnp.where(rowi == 0, cr_, pltpu.roll(ir, 1, 0))
                        ei = jnp.where(rowi == 0, ci_, pltpu.roll(ii, 1, 0))
                        carry_[2 * ch], carry_[2 * ch + 1] = ir[7:8], ii[7:8]
                    dst[pl.ds(off, 8), lr] = er
                    dst[pl.ds(off, 8), li] = ei
                    if sref is not None:
                        s_r, s_i = sref[pl.ds(off, 8), lr], sref[pl.ds(off, 8), li]
                        da_[2 * ch] = da_[2 * ch] + s_r * er + s_i * ei
                        da_[2 * ch + 1] = da_[2 * ch + 1] + s_r * ei - s_i * er
                return (*carry_, *da_)

            st = lax.fori_loop(0, ng, step, (*carry, *da))
            carry, da = list(st[:2 * nch]), list(st[2 * nch:])
        if with_da:
            da_ref = refs[k0 + 2]
            for ch in range(nch):
                lr, li = lanes_of(ch)
                da_ref[:, lr] = da[2 * ch]
                da_ref[:, li] = da[2 * ch + 1]

    in_specs = [_zg_spec(jl, gb), _zg_spec(jc, gb), pl.BlockSpec((24, wb), lambda i: (0, i))]
    args = [z_l, z_c, coef]
    out_specs = [_zg_spec(jl, gb), _zg_spec(jc, gb)]
    out_shape = [jax.ShapeDtypeStruct(z_l.shape, F32), jax.ShapeDtypeStruct(z_c.shape, F32)]
    if with_da:
        in_specs += [_zg_spec(jl, gb), _zg_spec(jc, gb)]
        args += [s_l, s_c]
        out_specs.append(pl.BlockSpec((8, wb), lambda i: (0, i)))
        out_shape.append(jax.ShapeDtypeStruct((8, w_all), F32))
    return pl.pallas_call(body, name=name, grid=(w_all // wb,), in_specs=in_specs, out_specs=out_specs,
                          out_shape=out_shape, compiler_params=_cparams())(*args)


def _gt_tok_spec(g, tj):
    return pl.BlockSpec((g, S5_P, tj), lambda t, s: (0, s, t))


def _glu_fwd_gt(yt, z_cr, w_glu, b_glu):
    g, _, j = yt.shape
    e = g * S5_P
    tj = _cr_tile(j)

    def body(y_ref, z_ref, w_hbm, b_ref, o_ref, sg_ref, w_ref):
        @pl.when(jnp.logical_and(pl.program_id(0) == 0, pl.program_id(1) == 0))
        def _():
            pltpu.sync_copy(w_hbm, w_ref)

        y = jnp.transpose(y_ref[...].reshape(e, tj).astype(F32))
        gl = _gelu_parts(y)[0]
        sg = _sigmoid(_dot(gl.astype(BF), w_ref[...]) + b_ref[...])
        z = z_ref[...].astype(F32)
        o_ref[...] = (gl * sg * (z * _sigmoid(z))).astype(BF)
        sg_ref[...] = sg.astype(BF)

    tok = pl.BlockSpec((tj, e), lambda t, s: (t, s))
    return pl.pallas_call(
        body, name="l1_glu_fwd", grid=(j // tj, CHUNK),
        in_specs=[_gt_tok_spec(g, tj), tok, ANY, _full((1, e))], out_specs=[tok, tok],
        out_shape=[jax.ShapeDtypeStruct((j, CHUNK * e), BF), jax.ShapeDtypeStruct((j, CHUNK * e), BF)],
        scratch_shapes=[pltpu.VMEM(w_glu.shape, BF)], compiler_params=_cparams(),
    )(yt, z_cr, w_glu, b_glu)


def _glu_bwd_gt(dr_cr, gt1, w_out, w_glu, yt, z_cr, sg_cr):
    g, _, j = yt.shape
    e, d = w_out.shape
    tj = _cr_tile(j)

    def body(dr_ref, g_ref, wo_hbm, wg_hbm, y_ref, z_ref, sg_ref, dz_ref, dt_ref, dy_ref, wo_ref, wg_ref):
        @pl.when(jnp.logical_and(pl.program_id(0) == 0, pl.program_id(1) == 0))
        def _():
            pltpu.sync_copy(wo_hbm, wo_ref)
            pltpu.sync_copy(wg_hbm, wg_ref)

        do = (dr_ref[...].astype(F32) * g_ref[...]).astype(BF)
        dw = _dot_nt(do, wo_ref[...])
        y = jnp.transpose(y_ref[...].reshape(e, tj).astype(F32))
        gl, dgel = _gelu_parts(y)
        z = z_ref[...].astype(F32)
        sz = _sigmoid(z)
        sg = sg_ref[...].astype(F32)
        dg2 = dw * (z * sz)
        dz_ref[...] = (dw * gl * sg * (sz * (1.0 + z * (1.0 - sz)))).astype(BF)
        dt = (dg2 * gl * sg * (1.0 - sg)).astype(BF)
        dt_ref[...] = dt
        dy = (dg2 * sg + _dot_nt(dt, wg_ref[...])) * dgel
        dy_ref[...] = jnp.transpose(dy).reshape(g, S5_P, tj).astype(BF)

    tok_e = pl.BlockSpec((tj, e), lambda t, s: (t, s))
    return pl.pallas_call(
        body, name="l1_glu_bwd", grid=(j // tj, CHUNK),
        in_specs=[pl.BlockSpec((tj, d), lambda t, s: (t, s)), _full((1, d)), ANY, ANY, _gt_tok_spec(g, tj), tok_e, tok_e],
        out_specs=[tok_e, tok_e, _gt_tok_spec(g, tj)],
        out_shape=[jax.ShapeDtypeStruct((j, CHUNK * e), BF), jax.ShapeDtypeStruct((j, CHUNK * e), BF),
                   jax.ShapeDtypeStruct((g, GT_ROWS, j), BF)],
        scratch_shapes=[pltpu.VMEM(w_out.shape, BF), pltpu.VMEM(w_glu.shape, BF)], compiler_params=_cparams(),
    )(dr_cr, gt1, w_out, w_glu, yt, z_cr, sg_cr)


def _bwd_inproj1_gt(dut, dz_cr, wu_t, w_z, xh_cr, rs_cr, dr2_cr, vecs, tag):
    g, _, j = dut.shape
    e, d = wu_t.shape
    tj = _cr_tile(j)

    def body(du_ref, dz_ref, wu_hbm, wz_hbm, xh_ref, rs_ref, dr2_ref, v_ref, dr1_ref, acc_ref, wu_ref, wz_ref):
        @pl.when(jnp.logical_and(pl.program_id(0) == 0, pl.program_id(1) == 0))
        def _():
            pltpu.sync_copy(wu_hbm, wu_ref)
            pltpu.sync_copy(wz_hbm, wz_ref)
            acc_ref[...] = jnp.zeros_like(acc_ref)

        dh = _dot_tn(du_ref[...].reshape(e, tj), wu_ref[...]) + _dot_nt(dz_ref[...], wz_ref[...])
        xh = xh_ref[...]
        x1 = xh * v_ref[0:1, :] + v_ref[1:2, :]
        dx1 = DN_ALPHA * dr2_ref[...].astype(F32) + dh * v_ref[2:3, :]
        dxh = dx1 * v_ref[0:1, :]
        rstd = rs_ref[:, 0:1]
        dr1 = rstd * (dxh - jnp.mean(dxh, axis=-1, keepdims=True) - xh * jnp.mean(dxh * xh, axis=-1, keepdims=True))
        dr1_ref[...] = dr1.astype(BF)
        acc_ref[0:1, :] += jnp.sum(dh * x1, axis=0, keepdims=True)
        acc_ref[1:2, :] += jnp.sum(dh, axis=0, keepdims=True)
        acc_ref[2:3, :] += jnp.sum(dx1 * xh, axis=0, keepdims=True)
        acc_ref[3:4, :] += jnp.sum(dx1, axis=0, keepdims=True)

    tok_d = pl.BlockSpec((tj, d), lambda t, s: (t, s))
    return pl.pallas_call(
        body, name="l1_bwd_inproj_" + tag, grid=(j // tj, CHUNK),
        in_specs=[_gt_tok_spec(g, tj), pl.BlockSpec((tj, e), lambda t, s: (t, s)), ANY, ANY, tok_d,
                  pl.BlockSpec((tj, 128), lambda t, s: (t, s)), tok_d, _full((8, d))],
        out_specs=[tok_d, _full((8, d))],
        out_shape=[jax.ShapeDtypeStruct((j, CHUNK * d), BF), jax.ShapeDtypeStruct((8, d), F32)],
        scratch_shapes=[pltpu.VMEM(wu_t.shape, BF), pltpu.VMEM(w_z.shape, BF)], compiler_params=_cparams(),
    )(dut, dz_cr, wu_t, w_z, xh_cr, rs_cr, dr2_cr, vecs)


def _dw_gt(lhs_gt, rhs_cr, lhs_gelu, vec, bias_sum, init, out_dtype, name):
    g, _, j = lhs_gt.shape
    e = g * S5_P
    n = rhs_cr.shape[1] // CHUNK
    tj = _cr_tile(j, 512 if j % 512 == 0 else 256)
    nh = 2 if e * n * 4 > (8 << 20) else 1
    tn = n // nh
    nt = j // tj
    has_init = init is not None

    def body(*refs):
        refs = list(refs)
        l_ref, r_ref = refs[0], refs[1]
        pos = 2
        v_ref = i_ref = bs_ref = None
        if vec is not None:
            v_ref = refs[pos]
            pos += 1
        if has_init:
            i_ref = refs[pos]
            pos += 1
        o_ref = refs[pos]
        pos += 1
        if bias_sum:
            bs_ref = refs[pos]
            pos += 1
        acc_ref = refs[pos]
        t, s = pl.program_id(1), pl.program_id(2)

        @pl.when(jnp.logical_and(t == 0, s == 0))
        def _():
            acc_ref[...] = i_ref[...] if has_init else jnp.zeros_like(acc_ref)
            if bias_sum:
                bs_ref[...] = jnp.zeros_like(bs_ref)

        lv = l_ref[...].reshape(e, tj)
        if lhs_gelu:
            lv = _gelu_parts(lv.astype(F32))[0].astype(BF)
        if vec is not None:
            rv = (r_ref[...] * v_ref[0:1, :] + v_ref[1:2, :]).astype(BF)
        else:
            rv = r_ref[...]
        acc_ref[...] += _dot(lv, rv)
        if bias_sum:
            bs_ref[0:1, :] += jnp.sum(rv.astype(F32), axis=0, keepdims=True)

        @pl.when(jnp.logical_and(t == nt - 1, s == CHUNK - 1))
        def _():
            o_ref[...] = acc_ref[...].astype(out_dtype)

    in_specs = [pl.BlockSpec((g, S5_P, tj), lambda h, t, s: (0, s, t)),
                pl.BlockSpec((tj, tn), lambda h, t, s: (t, s * nh + h))]
    args = [lhs_gt, rhs_cr]
    if vec is not None:
        in_specs.append(_full(vec.shape))
        args.append(vec)
    o_spec = pl.BlockSpec((e, tn), lambda h, t, s: (0, h))
    if has_init:
        in_specs.append(o_spec)
        args.append(init)
    out_specs, out_shape = [o_spec], [jax.ShapeDtypeStruct((e, n), out_dtype)]
    if bias_sum:
        out_specs.append(pl.BlockSpec((8, tn), lambda h, t, s: (0, h)))
        out_shape.append(jax.ShapeDtypeStruct((8, n), F32))
    res = pl.pallas_call(
        body, name=name, grid=(nh, nt, CHUNK), in_specs=in_specs, out_specs=out_specs, out_shape=out_shape,
        scratch_shapes=[pltpu.VMEM((e, tn), F32)], compiler_params=_cparams(),
    )(*args)
    return res if bias_sum else res[0]


def _scan_coef_g(lam_re, lam_im, log_step):
    g = lam_re.shape[1]
    ms = jnp.array([1, 2, 4, 0, 0, 0, 0, 0] + list(range(1, 9)) + list(range(8, 0, -1)), F32) * CHUNK
    dt = jnp.exp(log_step)[..., None]
    mag = jnp.exp(ms.reshape(-1, 1, 1, 1) * (lam_re * dt)[None])
    ang = ms.reshape(-1, 1, 1, 1) * (lam_im * dt)[None]
    cr, ci = mag * jnp.cos(ang), mag * jnp.sin(ang)
    both = jnp.stack([cr, ci], axis=2).reshape(24, 2, 2, g // 2, 2, S5_N)
    return both.transpose(0, 3, 1, 2, 4, 5).reshape(24, g * ZG_W)


def _s5_small(lam_re, lam_im, log_step, b_re, b_im, c_re, c_im, d_skip):
    g = lam_re.shape[1]
    t, p = CHUNK, S5_P
    dt = jnp.exp(log_step)[..., None]
    ks = jnp.arange(t + 1, dtype=F32).reshape(t + 1, 1, 1, 1)
    mag = jnp.exp(ks * (lam_re * dt)[None])
    ang = ks * (lam_im * dt)[None]
    pr, pi = mag * jnp.cos(ang), mag * jnp.sin(ang)
    ar, ai = pr[1], pi[1]
    qr, qi = ar - 1.0, ai
    den = lam_re * lam_re + lam_im * lam_im
    fr = (qr * lam_re + qi * lam_im) / den
    fi = (qi * lam_re - qr * lam_im) / den
    bt_re, bt_im = b_re.transpose(0, 1, 3, 2), b_im.transpose(0, 1, 3, 2)
    bbr = fr[:, :, None, :] * bt_re - fi[:, :, None, :] * bt_im
    bbi = fr[:, :, None, :] * bt_im + fi[:, :, None, :] * bt_re
    lay = lambda a_r, a_i: jnp.stack([a_r, a_i], axis=0).transpose(3, 2, 0, 1, 4)
    by_dir = lambda a, f0, f1: jnp.stack([f0(a[:, 0]), f1(a[:, 1])], axis=1)
    rev = lambda a: jnp.flip(a, axis=0)
    same = lambda a: a
    pwb = lay(by_dir(pr[:t], rev, same), by_dir(pi[:t], rev, same))
    pwc = lay(by_dir(pr[1:], same, rev), by_dir(pi[1:], same, rev))
    bb = jnp.stack([bbr, bbi], axis=0).transpose(2, 1, 0, 3, 4)
    cc = jnp.stack([c_re, c_im], axis=0).transpose(2, 1, 0, 3, 4)
    dmat = jnp.eye(p, dtype=F32)[None] * d_skip.reshape(g, p)[:, :, None]
    return pwb, pwc, bb, cc, dmat, pr[t], pi[t]


def _pair_cols(r, ri, g2):
    c0 = (r * 2 + ri) * 128 + g2 * S5_N
    return slice(c0, c0 + S5_N)


def _rows_rep(a):
    return jnp.broadcast_to(a[:, None, :], (CHUNK, S5_P, a.shape[-1])).reshape(GT_ROWS, a.shape[-1])


def _rows_tile(a):
    return jnp.broadcast_to(a[None], (CHUNK, S5_P, a.shape[-1])).reshape(GT_ROWS, a.shape[-1])


def _sum_blocks(a):
    return jnp.sum(a.reshape(CHUNK, S5_P, a.shape[-1]), axis=0)


def _sum_in_blocks(a):
    return jnp.sum(a.reshape(CHUNK, S5_P, a.shape[-1]), axis=1)


def _ab_rows(pwb_ref, bb_ref, k, r):
    prs, pis = _rows_rep(pwb_ref[k, r, 0]), _rows_rep(pwb_ref[k, r, 1])
    bbr, bbi = _rows_tile(bb_ref[k, r, 0]), _rows_tile(bb_ref[k, r, 1])
    return prs * bbr - pis * bbi, prs * bbi + pis * bbr, prs, pis, bbr, bbi


def _s5_weights_fwd(pwb, pwc, bb, cc, dmat):
    g = pwb.shape[0]
    gb = GROUPS_PER_STEP
    hp = lax.Precision.HIGHEST

    def body(pwb_ref, pwc_ref, bb_ref, cc_ref, dm_ref, mt_ref, mtt_ref, bc_ref, cct_ref):
        zeros = jnp.zeros((GT_ROWS, S5_N), BF)
        nt = (((1,), (1,)), ((), ()))
        for k in range(gb):
            g2 = k % 2
            kds = []
            for r in range(2):
                for ri in range(2):
                    bc_ref[k, :, _pair_cols(r, ri, 1 - g2)] = zeros
                    cct_ref[k, :, _pair_cols(r, ri, 1 - g2)] = zeros
                abr, abi = _ab_rows(pwb_ref, bb_ref, k, r)[:2]
                bc_ref[k, :, _pair_cols(r, 0, g2)] = abr.astype(BF)
                bc_ref[k, :, _pair_cols(r, 1, g2)] = abi.astype(BF)
                cr, ci = cc_ref[k, r, 0], cc_ref[k, r, 1]
                crt, cit = _rows_tile(cr), _rows_tile(ci)
                prt, pit = _rows_rep(pwc_ref[k, r, 0]), _rows_rep(pwc_ref[k, r, 1])
                cct_ref[k, :, _pair_cols(r, 0, g2)] = (crt * prt - cit * pit).astype(BF)
                cct_ref[k, :, _pair_cols(r, 1, g2)] = (-(crt * pit + cit * prt)).astype(BF)
                kds.append(lax.dot_general(abr, cr, nt, precision=hp, preferred_element_type=F32)
                           - lax.dot_general(abi, ci, nt, precision=hp, preferred_element_type=F32))
            blk = lambda a, s: a[s * S5_P:(s + 1) * S5_P]
            last = CHUNK - 1
            pieces = [blk(kds[1], last - i) for i in range(last)]
            pieces.append(blk(kds[0], last) + blk(kds[1], 0) + dm_ref[k])
            pieces += [blk(kds[0], last - d) for d in range(1, CHUNK)]
            qrow = jnp.concatenate(pieces, axis=1)
            mt = jnp.concatenate([qrow[:, (last - s) * S5_P:(last - s) * S5_P + GT_ROWS] for s in range(CHUNK)], axis=0)
            mt_ref[k] = mt.astype(BF)
            mtt_ref[k] = jnp.transpose(mt).astype(BF)

    small = lambda a: pl.BlockSpec((gb, *a.shape[1:]), lambda i: (i,) + (0,) * (a.ndim - 1))
    return pl.pallas_call(
        body, name="l1_s5_weights", grid=(g // gb,),
        in_specs=[small(pwb), small(pwc), small(bb), small(cc), small(dmat)],
        out_specs=[_w_spec(GT_ROWS), _w_spec(GT_ROWS), _w_spec(PAIR_W), _w_spec(PAIR_W)],
        out_shape=[jax.ShapeDtypeStruct((g, GT_ROWS, GT_ROWS), BF), jax.ShapeDtypeStruct((g, GT_ROWS, GT_ROWS), BF),
                   jax.ShapeDtypeStruct((g, GT_ROWS, PAIR_W), BF), jax.ShapeDtypeStruct((g, GT_ROWS, PAIR_W), BF)],
        compiler_params=_cparams(),
    )(pwb, pwc, bb, cc, dmat)


def _s5_weights_bwd(pwb, pwc, bb, cc, d_mt, d_bc, d_cct):
    g = pwb.shape[0]
    gb = GROUPS_PER_STEP
    hp = lax.Precision.HIGHEST

    def body(pwb_ref, pwc_ref, bb_ref, cc_ref, dmt_ref, dbc_ref, dcc_ref, dpwb_ref, dpwc_ref, dbb_ref, dccp_ref, ddm_ref):
        tn = (((0,), (0,)), ((), ()))
        nn = (((1,), (0,)), ((), ()))
        last = CHUNK - 1
        for k in range(gb):
            g2 = k % 2
            dq = None
            for s in range(CHUNK):
                parts = [dmt_ref[k, s * S5_P:(s + 1) * S5_P, :]]
                if s < last:
                    parts.insert(0, jnp.zeros((S5_P, (last - s) * S5_P), F32))
                if s > 0:
                    parts.append(jnp.zeros((S5_P, s * S5_P), F32))
                padded = jnp.concatenate(parts, axis=1) if len(parts) > 1 else parts[0]
                dq = padded if dq is None else dq + padded
            dblk = lambda d: dq[:, (last + d) * S5_P:(CHUNK + d) * S5_P]
            ddm_ref[k] = dblk(0)
            dkds = [jnp.concatenate([dblk(last - s) for s in range(CHUNK)], axis=0),
                    jnp.concatenate([dblk(-s) for s in range(CHUNK)], axis=0)]
            for r in range(2):
                abr, abi, prs, pis, bbr, bbi = _ab_rows(pwb_ref, bb_ref, k, r)
                cr, ci = cc_ref[k, r, 0], cc_ref[k, r, 1]
                dcr = lax.dot_general(dkds[r], abr, tn, precision=hp, preferred_element_type=F32)
                dci = -lax.dot_general(dkds[r], abi, tn, precision=hp, preferred_element_type=F32)
                dabr = (lax.dot_general(dkds[r], cr, nn, precision=hp, preferred_element_type=F32)
                        + dbc_ref[k, :, _pair_cols(r, 0, g2)])
                dabi = (-lax.dot_general(dkds[r], ci, nn, precision=hp, preferred_element_type=F32)
                        + dbc_ref[k, :, _pair_cols(r, 1, g2)])
                dbb_ref[k, r, 0] = _sum_blocks(prs * dabr + pis * dabi)
                dbb_ref[k, r, 1] = _sum_blocks(prs * dabi - pis * dabr)
                dpwb_ref[k, r, 0] = _sum_in_blocks(dabr * bbr + dabi * bbi)
                dpwb_ref[k, r, 1] = _sum_in_blocks(dabi * bbr - dabr * bbi)
                crt, cit = _rows_tile(cr), _rows_tile(ci)
                prt, pit = _rows_rep(pwc_ref[k, r, 0]), _rows_rep(pwc_ref[k, r, 1])
                d_re = dcc_ref[k, :, _pair_cols(r, 0, g2)]
                d_im = dcc_ref[k, :, _pair_cols(r, 1, g2)]
                dccp_ref[k, r, 0] = dcr + _sum_blocks(d_re * prt - d_im * pit)
                dccp_ref[k, r, 1] = dci - _sum_blocks(d_re * pit + d_im * prt)
                dpwc_ref[k, r, 0] = _sum_in_blocks(d_re * crt - d_im * cit)
                dpwc_ref[k, r, 1] = -_sum_in_blocks(d_re * cit + d_im * crt)

    small = lambda a: pl.BlockSpec((gb, *a.shape[1:]), lambda i: (i,) + (0,) * (a.ndim - 1))
    dmat_sds = jax.ShapeDtypeStruct((g, S5_P, S5_P), F32)
    return pl.pallas_call(
        body, name="l1_s5_weights_bwd", grid=(g // gb,),
        in_specs=[small(pwb), small(pwc), small(bb), small(cc), _w_spec(GT_ROWS), _w_spec(PAIR_W), _w_spec(PAIR_W)],
        out_specs=[small(pwb), small(pwc), small(bb), small(cc), small(dmat_sds)],
        out_shape=[jax.ShapeDtypeStruct(pwb.shape, F32), jax.ShapeDtypeStruct(pwc.shape, F32),
                   jax.ShapeDtypeStruct(bb.shape, F32), jax.ShapeDtypeStruct(cc.shape, F32), dmat_sds],
        compiler_params=_cparams(),
    )(pwb, pwc, bb, cc, d_mt, d_bc, d_cct)


def _to_cr(a):
    return a.reshape(a.shape[0] // CHUNK, CHUNK * a.shape[1])


def _from_cr(a, c):
    return a.reshape(a.shape[0] * CHUNK, c)


def _pad8(v):
    return jnp.concatenate([v, jnp.zeros((8 - v.shape[0], v.shape[1]), v.dtype)], axis=0)


def _local_step(x, c, ctx, c_ctx, loss_target, w, late=None, scatter=False, mod=None):
    l, d = x.shape
    lc = ctx.shape[0]
    tm = min(256, lc)
    assert lc == tm and l % tm == 0 and tm % GRID_W == 0 and (tm & (tm - 1)) == 0
    nl = l // tm

    own_mod = mod is None
    if own_mod:
        c8 = _pad8(jnp.stack([c, c_ctx]))
        mod = _ada_fwd(c8, w["ada_w"], w["ada_b"])
    sh = mod[:, :2, :d]
    sc = mod[:, :2, d:2 * d]
    gt = mod[:, :2, 2 * d:]
    ln_g, ln_b = w["ln_g"], w["ln_b"]

    a0, b0 = 1.0 + sc[0], sh[0]
    xch = _Exchange("gather2", [late[n][0] for n in late], [late[n][1] for n in late]) if late else None
    p42, got = _inproj0(x, ctx, a0, b0, w["conv_w_in"], tm, xch)
    if late:
        w = dict(w, **dict(zip(late, got)))
    e = w["conv_w_out"].shape[0]
    half = e // 2
    cw = w["conv_w"].reshape(3, 2, half)
    q3 = _conv_fwd(p42, cw, nl, tm, half)
    xh1_l, xh1_c, rs1_l, rs1_c, fx = _outproj_ln0(q3, w["conv_w_out"], x, ctx, gt[0], tm)
    jl, jc = l // CHUNK, lc // CHUNK

    g0, bb0 = ln_g[0:1], ln_b[0:1]
    a1 = g0 * (1.0 + sc[1])
    b1 = bb0 * (1.0 + sc[1]) + sh[1]
    wu_t = w["ssm_w_in"][:, :e].T
    w_z = w["ssm_w_in"][:, e:]
    ut_l, z_l = _inproj1_gt(xh1_l, a1[0:1], b1[0:1], wu_t, w_z, "lat")
    ut_c, _ = _inproj1_gt(xh1_c, a1[1:2], b1[1:2], wu_t, w_z, "ctx")
    s5 = (w["ssm_lam_re"], w["ssm_lam_im"], w["ssm_log_step"], w["ssm_b_re"], w["ssm_b_im"],
          w["ssm_c_re"], w["ssm_c_im"], w["ssm_d"])
    (pwb, pwc, bbw, ccw, dmat, _, _), s5_vjp = jax.vjp(_s5_small, *s5)
    mt_b, mtt_b, bc_b, cct_b = _s5_weights_fwd(pwb, pwc, bbw, ccw, dmat)
    coef = lax.stop_gradient(_scan_coef_g(*s5[:3]))
    zz_l, zz_c = _s5_z(ut_l, ut_c, bc_b)
    fwd_chains = ((("c", False), ("l", False)), (("c", True), ("l", True)))
    st_l, st_c = _scan_g(zz_l, zz_c, coef, fwd_chains, False, name="l1_scan_fwd")
    yt = _s5_y(ut_l, st_l, mtt_b, cct_b)
    b_glu = w["ssm_b_glu"].reshape(1, e)
    w_cr, sg_cr = _glu_fwd_gt(yt, z_l, w["ssm_w_glu"], b_glu)
    vec_f = _pad8(jnp.concatenate([g0, bb0, gt[1][0:1], ln_g[1:2], ln_b[1:2]], axis=0))
    dr2, acc_f = _final(w_cr, w["ssm_w_out"], xh1_l, _to_cr(loss_target), vec_f)
    loss = jnp.sum(acc_f[3])

    gt1 = gt[1][0:1]
    dz_l, dt_l, dyt = _glu_bwd_gt(dr2, gt1, w["ssm_w_out"], w["ssm_w_glu"], yt, z_l, sg_cr)
    g_w_out = _dw_cr(w_cr, dr2, "cr", "scaled", gt1, False, None, "l1_dw_out")
    g_w_glu, bsum = _dw_gt(yt, dt_l, True, None, True, None, BF, "l1_dw_glu")
    g_b_glu = bsum[0]
    ds_l = _s5_ds(dyt, cct_b)
    bwd_chains = ((("l", True), ("c", True)), (("l", False), ("c", False)))
    dzz_l, dzz_c, da = _scan_g(ds_l, jnp.zeros_like(zz_c), coef, bwd_chains, True, st_l, st_c, name="l1_scan_bwd")
    dut_l, dut_c = _s5_dx(dyt, dzz_l, dzz_c, mt_b, bc_b)
    d_mt, d_bc, d_cct = _s5_dw(ut_l, ut_c, dyt, dzz_l, dzz_c, st_l)
    n_g = e // S5_P
    da = jnp.sum(da, axis=0).reshape(n_g // 2, 2, 2, 2, S5_N).transpose(1, 2, 0, 3, 4)
    da = da.reshape(2, 2, n_g, S5_N)
    d_pwb, d_pwc, d_bb, d_ccp, d_dm = _s5_weights_bwd(pwb, pwc, bbw, ccw, d_mt, d_bc, d_cct)
    g_s5 = s5_vjp((d_pwb, d_pwc, d_bb, d_ccp, d_dm, da[:, 0], da[:, 1]))

    vec_l = _pad8(jnp.concatenate([g0, bb0, 1.0 + sc[1][0:1]], axis=0))
    vec_c = _pad8(jnp.concatenate([g0, bb0, 1.0 + sc[1][1:2]], axis=0))
    dr1_l, acc_l = _bwd_inproj1_gt(dut_l, dz_l, wu_t, w_z, xh1_l, rs1_l, dr2, vec_l, "lat")
    dr1_c, acc_c = _bwd_inproj1_gt(dut_c, jnp.zeros((jc, CHUNK * e), BF), wu_t, w_z, xh1_c, rs1_c,
                                   jnp.zeros((jc, CHUNK * d), BF), vec_c, "ctx")
    mod_l = jnp.concatenate([a1[0:1], b1[0:1]], axis=0)
    mod_c = jnp.concatenate([a1[1:2], b1[1:2]], axis=0)
    g_ut_c = _dw_gt(dut_c, xh1_c, False, mod_c, False, None, F32, "l1_dw_in_u_ctx")
    g_ut = _dw_gt(dut_l, xh1_l, False, mod_l, False, g_ut_c, BF, "l1_dw_in_u")
    g_in_z = _dw_cr(xh1_l, dz_l, "mod", "cr", mod_l, False, None, "l1_dw_in_z")
    g_w_in1 = jnp.concatenate([g_ut.T, g_in_z], axis=1)

    dr1_ln, dr1_cn = _from_cr(dr1_l, d), _from_cr(dr1_c, d)
    dq3, acc_g0 = _bwd_outproj0(dr1_ln, dr1_cn, gt[0], w["conv_w_out"], fx, tm)
    sent1 = ["ssm_w_in", "ssm_w_glu", "ssm_w_out"]
    xch1 = _Exchange("scatter", [g_w_in1, g_w_glu, g_w_out], [BIG[n] for n in sent1]) if scatter else None
    dp42, dcw, grad_x, acc_0, recv1 = _conv_bwd_inproj0(dq3, p42, cw, w["conv_w_in"], x, ctx, dr1_ln, dr1_cn,
                                                        a0, nl, tm, xch1)
    g_w_in0 = _dw_inproj0(x, ctx, a0, b0, dp42, tm)
    sent0 = ["conv_w_in"]
    xch0 = _Exchange("scatter", [g_w_in0], [BIG[n] for n in sent0]) if scatter else None
    g_w_out0, recv0 = _dw_outproj0(q3, dr1_ln, dr1_cn, gt[0], tm, xch0)
    recv = dict(zip(sent1 + sent0, recv1 + recv0))

    zero = jnp.zeros((d,), F32)
    dm0 = jnp.stack([jnp.concatenate([acc_0[2], acc_0[0], acc_g0[0]]), jnp.concatenate([acc_0[3], acc_0[1], acc_g0[1]])])
    dm1 = jnp.stack([jnp.concatenate([acc_l[1], acc_l[0], acc_f[2]]), jnp.concatenate([acc_c[1], acc_c[0], zero])])
    if own_mod:
        g_ada_w, dc8 = _ada_bwd(c8, w["ada_w"], jnp.stack([_pad8(dm0), _pad8(dm1)]), BF)
        g_mod = {"c_ctx": dc8[0, 1] + dc8[1, 1], "ada_w": g_ada_w,
                 "ada_b": jnp.stack([dm0[0] + dm0[1], dm1[0] + dm1[1]])}
    else:
        g_mod = {"mod": jnp.stack([dm0, dm1])}

    grads = {
        **g_mod,
        "ln_g": jnp.stack([acc_l[2] + acc_c[2], acc_f[0]]),
        "ln_b": jnp.stack([acc_l[3] + acc_c[3], acc_f[1]]),
        "conv_w_in": g_w_in0, "conv_w": dcw[:3].reshape(3, e), "conv_w_out": g_w_out0,
        "ssm_w_in": g_w_in1,
        "ssm_lam_re": g_s5[0], "ssm_lam_im": g_s5[1], "ssm_log_step": g_s5[2],
        "ssm_b_re": g_s5[3], "ssm_b_im": g_s5[4], "ssm_c_re": g_s5[5], "ssm_c_im": g_s5[6], "ssm_d": g_s5[7],
        "ssm_w_glu": g_w_glu, "ssm_b_glu": g_b_glu, "ssm_w_out": g_w_out,
    }
    for n in recv:
        del grads[n]
    return loss, grad_x, grads, recv


WEIGHTS = ["c_ctx", "ada_w", "ada_b", "ln_g", "ln_b", "conv_w_in", "conv_w", "conv_w_out", "ssm_w_in",
           "ssm_lam_re", "ssm_lam_im", "ssm_log_step", "ssm_b_re", "ssm_b_im", "ssm_c_re", "ssm_c_im",
           "ssm_d", "ssm_w_glu", "ssm_b_glu", "ssm_w_out"]
BIG = {"ada_w": 1, "conv_w_in": 1, "conv_w_out": 0, "ssm_w_in": 1, "ssm_w_glu": 0, "ssm_w_out": 0}
SMALL_SHARDED = ["conv_w", "ssm_d", "ssm_b_glu"]
REPLICATED = ["c_ctx", "ada_b", "ln_g", "ln_b", "ssm_lam_re", "ssm_lam_im", "ssm_log_step",
              "ssm_b_re", "ssm_b_im", "ssm_c_re", "ssm_c_im"]
NATIVE_SMALL = ["ssm_b_re", "ssm_b_im", "ssm_c_re", "ssm_c_im"]


def _view2d(name, a):
    return a.reshape(-1, a.shape[-1])


def kernel(x, c, ctx, c_ctx, ada_w, ada_b, ln_g, ln_b, conv_w_in, conv_w, conv_w_out, ssm_w_in, ssm_lam_re, ssm_lam_im, ssm_log_step, ssm_b_re, ssm_b_im, ssm_c_re, ssm_c_im, ssm_d, ssm_w_glu, ssm_b_glu, ssm_w_out, loss_target, m_c_ctx, m_ada_w, m_ada_b, m_ln_g, m_ln_b, m_conv_w_in, m_conv_w, m_conv_w_out, m_ssm_w_in, m_ssm_lam_re, m_ssm_lam_im, m_ssm_log_step, m_ssm_b_re, m_ssm_b_im, m_ssm_c_re, m_ssm_c_im, m_ssm_d, m_ssm_w_glu, m_ssm_b_glu, m_ssm_w_out, v_c_ctx, v_ada_w, v_ada_b, v_ln_g, v_ln_b, v_conv_w_in, v_conv_w, v_conv_w_out, v_ssm_w_in, v_ssm_lam_re, v_ssm_lam_im, v_ssm_log_step, v_ssm_b_re, v_ssm_b_im, v_ssm_c_re, v_ssm_c_im, v_ssm_d, v_ssm_w_glu, v_ssm_b_glu, v_ssm_w_out):
    args = locals()
    wt = {n: args[n] for n in WEIGHTS}
    mt = {n: args["m_" + n] for n in WEIGHTS}
    vt = {n: args["v_" + n] for n in WEIGHTS}

    me = 4 * lax.axis_index("x") + 2 * lax.axis_index("y") + lax.axis_index("c")
    d = x.shape[-1]
    d3 = 3 * d
    wa = d3 // N_DEV

    big_names = [n for n in BIG if n != "ada_w"]
    shard = {n: _view2d(n, wt[n]).astype(BF) for n in big_names}
    small = jnp.concatenate([wt["conv_w"][0], wt["ssm_d"], wt["ssm_b_glu"]], axis=0)
    small = jnp.concatenate([small, jnp.zeros((3, small.shape[1]), F32)], axis=0)
    w_in_full, small_full, c_all = _all_gather([shard["conv_w_in"], small, _pad8(c)], [1, 1, 0], "gather_weights", "gather2")
    late = {n: (shard[n], BIG[n]) for n in big_names if n != "conv_w_in"}
    c16 = jnp.concatenate([c_all[::8], c_ctx[None], jnp.zeros((16 - N_DEV - 1, d), F32)], axis=0)
    ada_w_b = ada_w.astype(BF)
    ada_b_mine = lax.dynamic_slice_in_dim(ada_b, me * wa, wa, axis=1)
    mod_part = _ada_fwd(c16, ada_w_b, ada_b_mine)
    mod_all = _all_gather([mod_part.reshape(32, wa)], [1], "gather_mod")[0].reshape(2, 16, d3)
    mod = jnp.stack([lax.dynamic_index_in_dim(mod_all, me, axis=1, keepdims=False), mod_all[:, N_DEV]], axis=1)
    w = {
        "ln_g": ln_g, "ln_b": ln_b, "conv_w_in": w_in_full, "conv_w": small_full[0:3],
        "ssm_lam_re": ssm_lam_re[0], "ssm_lam_im": ssm_lam_im[0],
        "ssm_log_step": ssm_log_step[0], "ssm_b_re": ssm_b_re[0], "ssm_b_im": ssm_b_im[0],
        "ssm_c_re": ssm_c_re[0], "ssm_c_im": ssm_c_im[0], "ssm_d": small_full[3], "ssm_b_glu": small_full[4],
    }

    loss, grad_x, g, recv_big = _local_step(x[0], c[0], ctx[0], c_ctx, loss_target[0], w, late, True, mod)

    dmod_all = _all_gather([_pad8(g["mod"].reshape(4, d3))], [0], "gather_dmod")[0].reshape(N_DEV, 8, d3)
    dmod_all = dmod_all[:, :4].reshape(N_DEV, 2, 2, d3)
    dm_ctx = dmod_all[0, :, 1]
    for p in range(1, N_DEV):
        dm_ctx = dm_ctx + dmod_all[p, :, 1]
    dm16 = jnp.concatenate([dmod_all[:, :, 0].transpose(1, 0, 2), dm_ctx[:, None], jnp.zeros((2, 16 - N_DEV - 1, d3), F32)], axis=1)
    g_ada_w, dc16 = _ada_bwd(c16, ada_w_b, lax.dynamic_slice_in_dim(dm16, me * wa, wa, axis=2), F32)
    g["c_ctx"] = dc16[0, N_DEV] + dc16[1, N_DEV]
    g_ada_b = jnp.sum(dm16, axis=1)

    blob_names = [n for n in REPLICATED if n != "ada_b"] + SMALL_SHARDED
    flat = jnp.concatenate([g[n].reshape(-1).astype(F32) for n in blob_names] + [loss.reshape(1)])
    nflat = flat.shape[0]
    rows = -(-nflat // (N_DEV * 128 * 8)) * 8
    flat = jnp.concatenate([flat, jnp.zeros((N_DEV * rows * 128 - nflat,), F32)]).reshape(N_DEV * rows, 128)
    last = [n for n in big_names if n not in recv_big]
    recv = _all_to_all([_view2d(n, g[n]) for n in last] + [flat], [BIG[n] for n in last] + [0], "scatter_grads")
    recv_big.update(zip(last, recv[:-1]))
    blob_sum = _sum_partials(recv[-1])
    blob = _all_gather([blob_sum], [0], "gather_small_grads", "gather2")[0].reshape(-1)
    small_g, off = {"ada_b": g_ada_b}, 0
    for n in blob_names:
        shape = wt[n].shape if n in REPLICATED else (*wt[n].shape[:-1], wt[n].shape[-1] * N_DEV)
        size = math.prod(shape)
        small_g[n] = blob[off:off + size].reshape(shape)
        off += size
    loss = blob[off]
    for n in SMALL_SHARDED:
        size = wt[n].shape[-1]
        small_g[n] = lax.dynamic_slice_in_dim(small_g[n], me * size, size, axis=small_g[n].ndim - 1)

    out_g, out_d, out_m, out_v = {}, {}, {}, {}
    recv_big["ada_w"] = _view2d("ada_w", g_ada_w)[None]
    for n in BIG:
        stack = recv_big[n]
        shp = wt[n].shape
        res = _adamw(stack, _view2d(n, wt[n]), _view2d(n, mt[n]), _view2d(n, vt[n]), "adamw_" + n)
        out_g[n], out_d[n], out_m[n], out_v[n] = [r.reshape(shp) for r in res]
    for n in NATIVE_SMALL:
        shp = wt[n].shape
        v2 = lambda a: a.reshape(-1, shp[-1])
        res = _adamw(v2(small_g.pop(n))[None], v2(wt[n]), v2(mt[n]), v2(vt[n]), "adamw_" + n)
        out_g[n], out_d[n], out_m[n], out_v[n] = [r.reshape(shp) for r in res]
    names = list(small_g)
    cat = lambda t: jnp.concatenate([t[n].reshape(-1) for n in names])
    gs, ws, ms, vs = cat(small_g), cat(wt), cat(mt), cat(vt)
    ns = gs.shape[0]
    rs = -(-ns // (128 * 512)) * 512
    padr = lambda a: jnp.concatenate([a, jnp.ones((rs * 128 - ns,), F32)]).reshape(rs, 128)
    res = _adamw(padr(gs)[None], padr(ws), padr(ms), padr(vs), "adamw_small")
    off = 0
    for n in names:
        size = math.prod(wt[n].shape)
        out_g[n], out_d[n], out_m[n], out_v[n] = [r.reshape(-1)[off:off + size].reshape(wt[n].shape) for r in res]
        off += size

    return (loss, grad_x[None], *[out_g[n] for n in WEIGHTS], *[out_d[n] for n in WEIGHTS],
            *[out_m[n] for n in WEIGHTS], *[out_v[n] for n in WEIGHTS])
```

```python
import math

import jax
import jax.numpy as jnp
from jax import lax
from jax.experimental import pallas as pl
from jax.experimental.pallas import tpu as pltpu

F32 = jnp.float32
BF = jnp.bfloat16
MESH = pl.DeviceIdType.MESH
N_DEV = 8

GRID_W = 64
CHUNK = 16
S5_P = 16
S5_N = 64
LN_EPS = 1e-5
DN_ALPHA = 4.0 ** 0.25
ADAM_LR, ADAM_B1, ADAM_B2, ADAM_EPS, ADAM_WD, ADAM_STEP = 1e-3, 0.9, 0.999, 1e-8, 0.01, 10
GELU_C0 = math.sqrt(2.0 / math.pi)
GELU_C1 = 0.044715
VMEM_MB = 52

ANY = pl.BlockSpec(memory_space=pl.ANY)


def _cparams():
    return pltpu.CompilerParams(vmem_limit_bytes=VMEM_MB << 20)


def _dot(a, b):
    return jnp.dot(a, b, preferred_element_type=F32)


def _dot_nt(a, b):
    return lax.dot_general(a, b, (((1,), (1,)), ((), ())), preferred_element_type=F32)


def _dot_tn(a, b):
    return lax.dot_general(a, b, (((0,), (0,)), ((), ())), preferred_element_type=F32)


def _sigmoid(x):
    return 1.0 / (1.0 + jnp.exp(-x))


def _gelu_parts(y):
    u = y * y
    th = jnp.tanh(y * (GELU_C0 + (GELU_C0 * GELU_C1) * u))
    hy = 0.5 * y
    g = hy + hy * th
    dg = (0.5 + 0.5 * th) + hy * (1.0 - th * th) * (GELU_C0 + (3.0 * GELU_C0 * GELU_C1) * u)
    return g, dg


def _full(shape):
    nd = len(shape)
    return pl.BlockSpec(shape, lambda *_: (0,) * nd)


def _mesh_pos():
    x, y, c = lax.axis_index("x"), lax.axis_index("y"), lax.axis_index("c")
    return x, y, c


def _peer(pos, k):
    x, y, c = pos
    px = 1 - x if (k >> 2) & 1 else x
    py = 1 - y if (k >> 1) & 1 else y
    pc = 1 - c if k & 1 else c
    return (px, py, pc), 4 * px + 2 * py + pc


def _shard_at(ref, axis, idx, n):
    if axis == 0:
        return ref.at[pl.ds(idx * n, n)]
    return ref.at[:, pl.ds(idx * n, n)]


class _Exchange:
    def __init__(self, kind, arrays, axes):
        self.kind, self.axes, self.n = kind, list(axes), len(arrays)
        self.arrays = list(arrays)
        self.out_shape = []
        for s, ax in zip(arrays, axes):
            shp = list(s.shape)
            if kind == "scatter":
                shp[ax] //= N_DEV
                self.out_shape.append(jax.ShapeDtypeStruct((N_DEV, *shp), s.dtype))
            else:
                shp[ax] *= N_DEV
                self.out_shape.append(jax.ShapeDtypeStruct(tuple(shp), s.dtype))
        self.scratch = [pltpu.SemaphoreType.DMA((self.n, N_DEV - 1)), pltpu.SemaphoreType.DMA((self.n, N_DEV - 1)),
                        pltpu.SemaphoreType.DMA((self.n,))]

    def _copies(self, ins, outs, sems):
        send_sems, recv_sems, local_sems = sems
        pos = _mesh_pos()
        x, y, c = pos
        me = 4 * x + 2 * y + c
        local, sends, chained, recvs = [], [], [], []
        for i in range(self.n):
            ax = self.axes[i]
            if self.kind == "scatter":
                size = ins[i].shape[ax] // N_DEV
                src = lambda idx, i=i, ax=ax, size=size: _shard_at(ins[i], ax, idx, size)
                dst = lambda idx, i=i: outs[i].at[idx]
            else:
                size = ins[i].shape[ax]
                src = lambda idx, i=i: ins[i]
                dst = lambda idx, i=i, ax=ax, size=size: _shard_at(outs[i], ax, idx, size)

            def copy(k, s, d, to, i=i):
                return pltpu.make_async_remote_copy(src_ref=s, dst_ref=d, send_sem=send_sems.at[i, k],
                                                    recv_sem=recv_sems.at[i, k], device_id=to, device_id_type=MESH)

            local.append(pltpu.make_async_copy(src(me), dst(me), local_sems.at[i]))
            if self.kind == "gather2":
                sib, sib_i = (x, y, 1 - c), 4 * x + 2 * y + (1 - c)
                chips = [(1 - x, y), (x, 1 - y), (1 - x, 1 - y)]
                sends.append(copy(0, src(me), dst(me), sib))
                recvs.append(copy(0, src(me), dst(sib_i), sib))
                for j, (cx, cy) in enumerate(chips):
                    same, other = 4 * cx + 2 * cy + c, 4 * cx + 2 * cy + (1 - c)
                    sends.append(copy(1 + j, src(me), dst(me), (cx, cy, c)))
                    chained.append((copy(1 + j, dst(same), dst(same), (cx, cy, c)), copy(4 + j, dst(same), dst(same), sib)))
                    recvs.append(copy(4 + j, dst(other), dst(other), sib))
            else:
                for k in range(1, N_DEV):
                    peer, pidx = _peer(pos, k)
                    out_src = src(pidx) if self.kind == "scatter" else src(me)
                    sends.append(copy(k - 1, out_src, dst(me), peer))
                    recvs.append(copy(k - 1, out_src, dst(pidx), peer))
        return local, sends, chained, recvs

    def start(self, ins, outs, sems):
        local, sends, _, _ = self._copies(ins, outs, sems)
        for cp in local + sends:
            cp.start()

    def wait(self, ins, outs, sems):
        local, sends, chained, recvs = self._copies(ins, outs, sems)
        for arrival, released in chained:
            arrival.wait_recv()
            released.start()
        for cp in recvs:
            cp.wait_recv()
        for cp in sends + [released for _, released in chained]:
            cp.wait_send()
        for cp in local:
            cp.wait()

    def run(self, name):
        n = self.n

        def body(*refs):
            ins, outs, sems = refs[:n], refs[n:2 * n], refs[2 * n:]
            self.start(ins, outs, sems)
            self.wait(ins, outs, sems)

        return pl.pallas_call(body, name=name, out_shape=self.out_shape, in_specs=[ANY] * n, out_specs=[ANY] * n,
                              scratch_shapes=self.scratch)(*self.arrays)


def _hosted_call(body, xch, grid, in_specs, out_specs, out_shape, scratch, args, name):
    out_specs, out_shape = list(out_specs), list(out_shape)
    n_in, n_out = len(in_specs), len(out_specs)
    if xch is None:
        res = pl.pallas_call(body, name=name, grid=grid, in_specs=in_specs, out_specs=out_specs, out_shape=out_shape,
                             scratch_shapes=list(scratch), compiler_params=_cparams())(*args)
        return list(res), []
    n = xch.n
    rank = len(grid)

    def wrapped(*refs):
        ins, x_ins = refs[:n_in], refs[n_in:n_in + n]
        outs = refs[n_in + n:n_in + n + n_out]
        x_outs = refs[n_in + n + n_out:n_in + 2 * n + n_out]
        rest = refs[n_in + 2 * n + n_out:]
        own, sems = rest[:len(rest) - 3], rest[len(rest) - 3:]
        ids = [pl.program_id(a) for a in range(rank)]
        first, last = ids[0] == 0, ids[0] == grid[0] - 1
        for a in range(1, rank):
            first = jnp.logical_and(first, ids[a] == 0)
            last = jnp.logical_and(last, ids[a] == grid[a] - 1)

        @pl.when(first)
        def _():
            xch.start(x_ins, x_outs, sems)

        body(*ins, *outs, *own)

        @pl.when(last)
        def _():
            xch.wait(x_ins, x_outs, sems)

    res = pl.pallas_call(
        wrapped, name=name, grid=grid, in_specs=list(in_specs) + [ANY] * n, out_specs=out_specs + [ANY] * n,
        out_shape=out_shape + xch.out_shape, scratch_shapes=list(scratch) + xch.scratch, compiler_params=_cparams(),
    )(*args, *xch.arrays)
    return list(res[:n_out]), list(res[n_out:])


def _all_gather(shards, axes, name, kind="gather"):
    return _Exchange(kind, shards, axes).run(name)


def _all_to_all(parts, axes, name):
    return _Exchange("scatter", parts, axes).run(name)


def _ada_fwd(cv, ada_w, ada_b):
    nl, d, wd = ada_w.shape
    r = cv.shape[0]

    def body(c_ref, w_ref, b_ref, o_ref):
        c = c_ref[...]
        s = (c * _sigmoid(c)).astype(BF)
        o_ref[0] = _dot(s, w_ref[0]) + b_ref[0]

    return pl.pallas_call(
        body, name="ada_fwd", grid=(nl,),
        in_specs=[_full((r, d)), pl.BlockSpec((1, d, wd), lambda l: (l, 0, 0)), pl.BlockSpec((1, 1, wd), lambda l: (l, 0, 0))],
        out_specs=pl.BlockSpec((1, r, wd), lambda l: (l, 0, 0)),
        out_shape=jax.ShapeDtypeStruct((nl, r, wd), F32), compiler_params=_cparams(),
    )(cv, ada_w, ada_b.reshape(nl, 1, wd))


def _ada_bwd(cv, ada_w, dm, out_dtype):
    nl, d, wd = ada_w.shape
    r = cv.shape[0]

    def body(c_ref, w_ref, dm_ref, dw_ref, dc_ref):
        c = c_ref[...]
        sg = _sigmoid(c)
        s = (c * sg).astype(BF)
        dmv = dm_ref[0].astype(BF)
        dw_ref[0] = _dot_tn(s, dmv).astype(out_dtype)
        dc_ref[0] = _dot_nt(dmv, w_ref[0]) * (sg * (1.0 + c * (1.0 - sg)))

    return pl.pallas_call(
        body, name="ada_bwd", grid=(nl,),
        in_specs=[_full((r, d)), pl.BlockSpec((1, d, wd), lambda l: (l, 0, 0)), pl.BlockSpec((1, r, wd), lambda l: (l, 0, 0))],
        out_specs=[pl.BlockSpec((1, d, wd), lambda l: (l, 0, 0)), pl.BlockSpec((1, r, d), lambda l: (l, 0, 0))],
        out_shape=[jax.ShapeDtypeStruct((nl, d, wd), out_dtype), jax.ShapeDtypeStruct((nl, r, d), F32)],
        compiler_params=_cparams(),
    )(cv, ada_w, dm)


def _sum_partials(stack):
    _, r, c = stack.shape

    def body(s_ref, o_ref):
        acc = s_ref[0]
        for p in range(1, N_DEV):
            acc = acc + s_ref[p]
        o_ref[...] = acc

    return pl.pallas_call(body, name="sum_partials", out_shape=jax.ShapeDtypeStruct((r, c), F32),
                          in_specs=[_full(stack.shape)], out_specs=_full((r, c)), grid=(1,),
                          compiler_params=_cparams())(stack)


def _adamw(gstack, w, m, v, name):
    p, r, c = gstack.shape
    tr = r
    for cand in (512 if c <= 256 else 256, 128, 64, 32, 16, 8):
        if r % cand == 0 and r > cand:
            tr = cand
            break
    bc1 = 1.0 - ADAM_B1 ** ADAM_STEP
    bc2 = 1.0 - ADAM_B2 ** ADAM_STEP

    def body(g_ref, w_ref, m_ref, v_ref, go_ref, d_ref, mo_ref, vo_ref):
        g = g_ref[0].astype(F32)
        for q in range(1, p):
            g = g + g_ref[q].astype(F32)
        mn = ADAM_B1 * m_ref[...] + (1.0 - ADAM_B1) * g
        vn = ADAM_B2 * v_ref[...] + (1.0 - ADAM_B2) * (g * g)
        go_ref[...] = g
        mo_ref[...] = mn
        vo_ref[...] = vn
        d_ref[...] = -ADAM_LR * ((mn / bc1) / (jnp.sqrt(vn / bc2) + ADAM_EPS) + ADAM_WD * w_ref[...])

    row = pl.BlockSpec((tr, c), lambda i: (i, 0))
    sds = jax.ShapeDtypeStruct((r, c), F32)
    return pl.pallas_call(
        body, name=name, grid=(r // tr,),
        in_specs=[pl.BlockSpec((p, tr, c), lambda i: (0, i, 0)), row, row, row],
        out_specs=[row, row, row, row], out_shape=[sds, sds, sds, sds], compiler_params=_cparams(),
    )(gstack, w, m, v)


def _lat_or_ctx_specs(tm, d, nl, grid_rank, row_axis):
    def lat(*ids):
        return (jnp.minimum(ids[row_axis], nl - 1), 0)

    def ctx(*ids):
        return (jnp.maximum(ids[row_axis] - nl, 0), 0)

    return pl.BlockSpec((tm, d), lat), pl.BlockSpec((tm, d), ctx)


def _sel_row(ref, is_ctx):
    return jnp.where(is_ctx, ref[1:2, :], ref[0:1, :])


def _inproj0(x, ctx, a2, b2, w, tm, xch=None):
    l, d = x.shape
    nl, nc = l // tm, ctx.shape[0] // tm
    e = w.shape[1] // 4
    half = e // 2

    def body(x_ref, c_ref, a_ref, b_ref, w_hbm, o_ref, w_ref):
        i = pl.program_id(0)

        @pl.when(i == 0)
        def _():
            pltpu.sync_copy(w_hbm, w_ref)

        is_ctx = i >= nl
        xv = jnp.where(is_ctx, c_ref[...], x_ref[...])
        h = (xv * _sel_row(a_ref, is_ctx) + _sel_row(b_ref, is_ctx)).astype(BF)
        for k in range(4):
            r = _dot(h, w_ref[:, k * e:(k + 1) * e])
            o_ref[k, 0] = r[:, :half].astype(BF)
            o_ref[k, 1] = r[:, half:].astype(BF)

    lat, cx = _lat_or_ctx_specs(tm, d, nl, 1, 0)
    (p42,), extra = _hosted_call(
        body, xch, grid=(nl + nc,),
        in_specs=[lat, cx, _full((2, d)), _full((2, d)), ANY],
        out_specs=[pl.BlockSpec((4, 2, tm, half), lambda i: (0, 0, i, 0))],
        out_shape=[jax.ShapeDtypeStruct((4, 2, l + ctx.shape[0], half), BF)],
        scratch=[pltpu.VMEM(w.shape, BF)], args=(x, ctx, a2, b2, w), name="l0_inproj")
    return p42, extra


def _conv_taps(u, w_up, w_mid, w_dn, pos, rl, tm):
    up = jnp.where(pos == 0, 0.0, pltpu.roll(u, 1, 0))
    dn = jnp.where(pos == rl - 1, 0.0, pltpu.roll(u, tm - 1, 0))
    return w_up * up + w_mid * u + w_dn * dn, up, dn


def _conv_halo_specs(tm, tc, nl, lead):
    hb = tm // GRID_W

    def prev(j, i):
        return (0, 1, jnp.maximum(jnp.minimum(i, nl - 1) * hb - 1, 0), j)

    def nxt(j, i):
        return (0, 1, jnp.minimum((jnp.minimum(i, nl - 1) + 1) * hb, nl * hb - 1), j)

    return pl.BlockSpec((lead, 1, GRID_W, tc), prev), pl.BlockSpec((lead, 1, GRID_W, tc), nxt)


def _conv_fwd(p42, cw, nl, tm, tc):
    _, _, r, half = p42.shape
    nt = r // tm

    def body(p_ref, hp_ref, hn_ref, cw_ref, o_ref):
        i = pl.program_id(1)
        is_ctx = i >= nl
        row = lax.broadcasted_iota(jnp.int32, (tm, tc), 0)
        rl = jnp.where(is_ctx, tm, GRID_W)
        pos = jnp.bitwise_and(row, rl - 1)

        def gate(hv, yc):
            bg = p_ref[0, hv].astype(F32)
            z = p_ref[3, hv].astype(F32)
            return (bg * yc * (z * _sigmoid(z))).astype(BF)

        u_h = p_ref[1, 0].astype(F32) * p_ref[2, 0].astype(F32)
        w_h = cw_ref[:, 0, :]
        o_ref[0] = gate(0, _conv_taps(u_h, w_h[0:1], w_h[1:2], w_h[2:3], pos, rl, tm)[0])
        u_v = p_ref[1, 1].astype(F32) * p_ref[2, 1].astype(F32)
        w_v = cw_ref[:, 1, :]

        @pl.when(is_ctx)
        def _():
            o_ref[1] = gate(1, _conv_taps(u_v, w_v[0:1], w_v[1:2], w_v[2:3], pos, rl, tm)[0])

        @pl.when(jnp.logical_not(is_ctx))
        def _():
            up = hp_ref[1, 0].astype(F32) * hp_ref[2, 0].astype(F32) * (i > 0).astype(F32)
            dn = hn_ref[1, 0].astype(F32) * hn_ref[2, 0].astype(F32) * (i < nl - 1).astype(F32)
            ext = jnp.concatenate([up, u_v, dn], axis=0)
            yc = w_v[0:1] * ext[0:tm] + w_v[1:2] * u_v + w_v[2:3] * ext[2 * GRID_W:tm + 2 * GRID_W]
            o_ref[1] = gate(1, yc)

    hp, hn = _conv_halo_specs(tm, tc, nl, 4)
    return pl.pallas_call(
        body, name="l0_conv_fwd", grid=(half // tc, nt),
        in_specs=[pl.BlockSpec((4, 2, tm, tc), lambda j, i: (0, 0, i, j)), hp, hn,
                  pl.BlockSpec((3, 2, tc), lambda j, i: (0, 0, j))],
        out_specs=pl.BlockSpec((2, tm, tc), lambda j, i: (0, i, j)),
        out_shape=jax.ShapeDtypeStruct((2, r, half), BF), compiler_params=_cparams(),
    )(p42, p42, p42, cw)


def _outproj_ln0(q3, w_out, x, ctx, gt2, tm):
    l, d = x.shape
    lc = ctx.shape[0]
    nl, nc = l // tm, lc // tm
    _, r, half = q3.shape
    tjo = tm // CHUNK

    def body(q_ref, w_hbm, x_ref, c_ref, g_ref, xl_ref, xc_ref, rl_ref, rc_ref, fx_ref, w_ref, xs_ref, rs_ref):
        i = pl.program_id(0)

        @pl.when(i == 0)
        def _():
            pltpu.sync_copy(w_hbm, w_ref)

        is_ctx = i >= nl
        fx = _dot(q_ref[0], w_ref[:half, :]) + _dot(q_ref[1], w_ref[half:, :])
        xv = jnp.where(is_ctx, c_ref[...], x_ref[...])
        rr = DN_ALPHA * xv + _sel_row(g_ref, is_ctx) * fx
        mu = jnp.mean(rr, axis=-1, keepdims=True)
        cen = rr - mu
        rstd = lax.rsqrt(jnp.mean(cen * cen, axis=-1, keepdims=True) + LN_EPS)
        xh = cen * rstd
        for lb in range(d // 128):
            xs_ref[lb] = xh[:, lb * 128:(lb + 1) * 128]
        rs_ref[...] = jnp.broadcast_to(rstd, (tm, 128))
        fx_ref[...] = fx.astype(BF)

        def to_cr(xo_ref, ro_ref):
            for s in range(CHUNK):
                for lb in range(d // 128):
                    xo_ref[:, s * d + lb * 128:s * d + (lb + 1) * 128] = xs_ref.at[lb][pl.ds(s, tjo, stride=CHUNK), :]
                ro_ref[:, s * 128:(s + 1) * 128] = rs_ref[pl.ds(s, tjo, stride=CHUNK), :]

        @pl.when(jnp.logical_not(is_ctx))
        def _():
            to_cr(xl_ref, rl_ref)

        @pl.when(is_ctx)
        def _():
            to_cr(xc_ref, rc_ref)

    lat, cx = _lat_or_ctx_specs(tm, d, nl, 1, 0)
    lat_o = lambda w_: pl.BlockSpec((tjo, CHUNK * w_), lambda i: (jnp.minimum(i, nl - 1), 0))
    ctx_o = lambda w_: pl.BlockSpec((tjo, CHUNK * w_), lambda i: (jnp.maximum(i - nl, 0), 0))
    return pl.pallas_call(
        body, name="l0_outproj_ln", grid=(nl + nc,),
        in_specs=[pl.BlockSpec((2, tm, half), lambda i: (0, i, 0)), ANY, lat, cx, _full((2, d))],
        out_specs=[lat_o(d), ctx_o(d), lat_o(128), ctx_o(128), pl.BlockSpec((tm, d), lambda i: (i, 0))],
        out_shape=[jax.ShapeDtypeStruct((l // CHUNK, CHUNK * d), F32), jax.ShapeDtypeStruct((lc // CHUNK, CHUNK * d), F32),
                   jax.ShapeDtypeStruct((l // CHUNK, CHUNK * 128), F32), jax.ShapeDtypeStruct((lc // CHUNK, CHUNK * 128), F32),
                   jax.ShapeDtypeStruct((r, d), BF)],
        scratch_shapes=[pltpu.VMEM(w_out.shape, BF), pltpu.VMEM((d // 128, tm, 128), F32), pltpu.VMEM((tm, 128), F32)],
        compiler_params=_cparams(),
    )(q3, w_out, x, ctx, gt2)


def _bwd_outproj0(dr_l, dr_c, gt2, w_out, fx, tm):
    l, d = dr_l.shape
    nl, nc = l // tm, dr_c.shape[0] // tm
    e = w_out.shape[0]
    half = e // 2
    r = l + dr_c.shape[0]

    def body(dl_ref, dc_ref, g_ref, w_hbm, fx_ref, dq_ref, acc_ref, w_ref):
        i = pl.program_id(0)

        @pl.when(i == 0)
        def _():
            pltpu.sync_copy(w_hbm, w_ref)
            acc_ref[...] = jnp.zeros_like(acc_ref)

        is_ctx = i >= nl
        dr = jnp.where(is_ctx, dc_ref[...], dl_ref[...]).astype(F32)
        dfx = (dr * _sel_row(g_ref, is_ctx)).astype(BF)
        dq_ref[0] = _dot_nt(dfx, w_ref[:half, :]).astype(BF)
        dq_ref[1] = _dot_nt(dfx, w_ref[half:, :]).astype(BF)
        s = jnp.sum(dr * fx_ref[...].astype(F32), axis=0, keepdims=True)
        sel = is_ctx.astype(F32)
        acc_ref[0:1, :] += s * (1.0 - sel)
        acc_ref[1:2, :] += s * sel

    lat, cx = _lat_or_ctx_specs(tm, d, nl, 1, 0)
    return pl.pallas_call(
        body, name="l0_bwd_outproj", grid=(nl + nc,),
        in_specs=[lat, cx, _full((2, d)), ANY, pl.BlockSpec((tm, d), lambda i: (i, 0))],
        out_specs=[pl.BlockSpec((2, tm, half), lambda i: (0, i, 0)), _full((8, d))],
        out_shape=[jax.ShapeDtypeStruct((2, r, half), BF), jax.ShapeDtypeStruct((8, d), F32)],
        scratch_shapes=[pltpu.VMEM(w_out.shape, BF)], compiler_params=_cparams(),
    )(dr_l, dr_c, gt2, w_out, fx)


def _conv_bwd_inproj0(dq3, p42, cw, w_in, x, ctx, dr_l, dr_c, a2, nl, tm, xch=None):
    l, d = x.shape
    _, _, r, half = p42.shape
    nt = r // tm
    e = 2 * half
    cc = min(512, half)
    n_cc = half // cc

    def body(dq_ref, dqp_ref, dqn_ref, p_ref, hp_ref, hn_ref, cw_ref, w_hbm, x_ref, c_ref, dl_ref, dc_ref, a_ref,
             dp_ref, dw_ref, gx_ref, acc_ref, w_ref, dh_ref):
        i, hv = pl.program_id(0), pl.program_id(1)
        is_ctx = i >= nl

        @pl.when(jnp.logical_and(i == 0, hv == 0))
        def _():
            pltpu.sync_copy(w_hbm, w_ref)
            acc_ref[...] = jnp.zeros_like(acc_ref)
            dw_ref[...] = jnp.zeros_like(dw_ref)

        row = lax.broadcasted_iota(jnp.int32, (tm, cc), 0)
        rl = jnp.where(is_ctx, tm, GRID_W)
        pos = jnp.bitwise_and(row, rl - 1)

        def pieces(dq, bg, z):
            sz = _sigmoid(z)
            sil = z * sz
            return dq * bg * sil, dq * sil, dq * bg * (sz * (1.0 + z * (1.0 - sz)))

        def emit(hvs, c, parts, dyc, u_up, u, u_dn, dh):
            lanes = slice(c * cc, (c + 1) * cc)
            for k, part in enumerate(parts):
                pb = part.astype(BF)
                dp_ref[k, 0, :, lanes] = pb
                c0 = k * e + hvs * half + c * cc
                t = _dot_nt(pb, w_ref[:, c0:c0 + cc])
                dh = t if dh is None else dh + t
            dw_ref[0:1, hvs, lanes] += jnp.sum(dyc * u_up, axis=0, keepdims=True)
            dw_ref[1:2, hvs, lanes] += jnp.sum(dyc * u, axis=0, keepdims=True)
            dw_ref[2:3, hvs, lanes] += jnp.sum(dyc * u_dn, axis=0, keepdims=True)
            return dh

        def seq_half(hvs):
            dh = None
            for c in range(n_cc):
                lanes = slice(c * cc, (c + 1) * cc)
                bg, cg = p_ref[0, 0, :, lanes].astype(F32), p_ref[1, 0, :, lanes].astype(F32)
                v, z = p_ref[2, 0, :, lanes].astype(F32), p_ref[3, 0, :, lanes].astype(F32)
                w = cw_ref[:, hvs, lanes]
                u = cg * v
                yc, u_up, u_dn = _conv_taps(u, w[0:1], w[1:2], w[2:3], pos, rl, tm)
                dyc, dbg_f, dz_f = pieces(dq_ref[0, :, lanes].astype(F32), bg, z)
                du = _conv_taps(dyc, w[2:3], w[1:2], w[0:1], pos, rl, tm)[0]
                dh = emit(hvs, c, (dbg_f * yc, du * v, du * cg, dz_f * yc), dyc, u_up, u, u_dn, dh)
            return dh

        def col_half():
            m_up = (i > 0).astype(F32)
            m_dn = (i < nl - 1).astype(F32)
            dh = None
            for c in range(n_cc):
                lanes = slice(c * cc, (c + 1) * cc)
                bg, cg = p_ref[0, 0, :, lanes].astype(F32), p_ref[1, 0, :, lanes].astype(F32)
                v, z = p_ref[2, 0, :, lanes].astype(F32), p_ref[3, 0, :, lanes].astype(F32)
                w = cw_ref[:, 1, lanes]
                u = cg * v

                def halo(h_ref, dqh_ref, msk):
                    hb, hc = h_ref[0, 0, :, lanes].astype(F32), h_ref[1, 0, :, lanes].astype(F32)
                    hv_, hz = h_ref[2, 0, :, lanes].astype(F32), h_ref[3, 0, :, lanes].astype(F32)
                    return hc * hv_ * msk, pieces(dqh_ref[0, :, lanes].astype(F32), hb, hz)[0] * msk

                u_p, dyc_p = halo(hp_ref, dqp_ref, m_up)
                u_n, dyc_n = halo(hn_ref, dqn_ref, m_dn)
                u_ext = jnp.concatenate([u_p, u, u_n], axis=0)
                u_up, u_dn = u_ext[0:tm], u_ext[2 * GRID_W:tm + 2 * GRID_W]
                yc = w[0:1] * u_up + w[1:2] * u + w[2:3] * u_dn
                dyc, dbg_f, dz_f = pieces(dq_ref[0, :, lanes].astype(F32), bg, z)
                d_ext = jnp.concatenate([dyc_p, dyc, dyc_n], axis=0)
                du = w[0:1] * d_ext[2 * GRID_W:tm + 2 * GRID_W] + w[1:2] * dyc + w[2:3] * d_ext[0:tm]
                dh = emit(1, c, (dbg_f * yc, du * v, du * cg, dz_f * yc), dyc, u_up, u, u_dn, dh)
            return dh

        @pl.when(hv == 0)
        def _():
            dh_ref[...] = seq_half(0)

        @pl.when(jnp.logical_and(hv == 1, is_ctx))
        def _():
            dh_ref[...] += seq_half(1)

        @pl.when(jnp.logical_and(hv == 1, jnp.logical_not(is_ctx)))
        def _():
            dh_ref[...] += col_half()

        @pl.when(hv == 1)
        def _():
            dh = dh_ref[...]
            xv = jnp.where(is_ctx, c_ref[...], x_ref[...])
            s_sc = jnp.sum(dh * xv, axis=0, keepdims=True)
            s_sh = jnp.sum(dh, axis=0, keepdims=True)
            sel = is_ctx.astype(F32)
            acc_ref[0:1, :] += s_sc * (1.0 - sel)
            acc_ref[1:2, :] += s_sc * sel
            acc_ref[2:3, :] += s_sh * (1.0 - sel)
            acc_ref[3:4, :] += s_sh * sel

        @pl.when(jnp.logical_and(hv == 1, jnp.logical_not(is_ctx)))
        def _():
            gx_ref[...] = DN_ALPHA * dl_ref[...].astype(F32) + dh_ref[...] * a_ref[0:1, :]

    hb = tm // GRID_W
    prev_blk = lambda i: jnp.maximum(jnp.minimum(i, nl - 1) * hb - 1, 0)
    next_blk = lambda i: jnp.minimum((jnp.minimum(i, nl - 1) + 1) * hb, nl * hb - 1)
    lat, cx = _lat_or_ctx_specs(tm, d, nl, 2, 0)
    (dp42, dcw, gx, acc), extra = _hosted_call(
        body, xch, grid=(nt, 2),
        in_specs=[pl.BlockSpec((1, tm, half), lambda i, h: (h, i, 0)),
                  pl.BlockSpec((1, GRID_W, half), lambda i, h: (1, prev_blk(i), 0)),
                  pl.BlockSpec((1, GRID_W, half), lambda i, h: (1, next_blk(i), 0)),
                  pl.BlockSpec((4, 1, tm, half), lambda i, h: (0, h, i, 0)),
                  pl.BlockSpec((4, 1, GRID_W, half), lambda i, h: (0, 1, prev_blk(i), 0)),
                  pl.BlockSpec((4, 1, GRID_W, half), lambda i, h: (0, 1, next_blk(i), 0)),
                  _full((3, 2, half)), ANY, lat, cx, lat, cx, _full((2, d))],
        out_specs=[pl.BlockSpec((4, 1, tm, half), lambda i, h: (0, h, i, 0)), _full((8, 2, half)),
                   pl.BlockSpec((tm, d), lambda i, h: (jnp.minimum(i, nl - 1), 0)), _full((8, d))],
        out_shape=[jax.ShapeDtypeStruct(p42.shape, BF), jax.ShapeDtypeStruct((8, 2, half), F32),
                   jax.ShapeDtypeStruct((l, d), F32), jax.ShapeDtypeStruct((8, d), F32)],
        scratch=[pltpu.VMEM(w_in.shape, BF), pltpu.VMEM((tm, d), F32)],
        args=(dq3, dq3, dq3, p42, p42, p42, cw, w_in, x, ctx, dr_l, dr_c, a2), name="l0_conv_bwd_inproj")
    return dp42, dcw, gx, acc, extra


def _dw_inproj0(x, ctx, a2, b2, dp42, tm):
    l, d = x.shape
    lc = ctx.shape[0]
    assert lc == tm
    tl = 4 * tm if l % (4 * tm) == 0 else tm
    nl = l // tl
    half = dp42.shape[-1]
    e = 2 * half

    def body(x_ref, c_ref, a_ref, b_ref, dpl_ref, dpc_ref, o_ref, acc_ref):
        i = pl.program_id(1)

        @pl.when(i == 0)
        def _():
            acc_ref[...] = jnp.zeros_like(acc_ref)

        def add(rows_ref, dp_ref, sel):
            h = (rows_ref[...] * a_ref[sel:sel + 1, :] + b_ref[sel:sel + 1, :]).astype(BF)
            acc_ref[:, :half] += _dot_tn(h, dp_ref[0, 0])
            acc_ref[:, half:] += _dot_tn(h, dp_ref[0, 1])

        @pl.when(i < nl)
        def _():
            add(x_ref, dpl_ref, 0)

        @pl.when(i == nl)
        def _():
            add(c_ref, dpc_ref, 1)
            o_ref[...] = acc_ref[...].astype(BF)

    return pl.pallas_call(
        body, name="l0_dw_inproj", grid=(4, nl + 1),
        in_specs=[pl.BlockSpec((tl, d), lambda k, i: (jnp.minimum(i, nl - 1), 0)), _full((lc, d)),
                  _full((2, d)), _full((2, d)),
                  pl.BlockSpec((1, 2, tl, half), lambda k, i: (k, 0, jnp.minimum(i, nl - 1), 0)),
                  pl.BlockSpec((1, 2, lc, half), lambda k, i: (k, 0, l // lc, 0))],
        out_specs=pl.BlockSpec((d, e), lambda k, i: (0, k)),
        out_shape=jax.ShapeDtypeStruct((d, 4 * e), BF),
        scratch_shapes=[pltpu.VMEM((d, e), F32)], compiler_params=_cparams(),
    )(x, ctx, a2, b2, dp42, dp42)


def _dw_outproj0(q3, dr_l, dr_c, gt2, tm, xch=None):
    l, d = dr_l.shape
    nl, nc = l // tm, dr_c.shape[0] // tm
    _, r, half = q3.shape
    nt = nl + nc

    def body(q_ref, dl_ref, dc_ref, g_ref, o_ref, acc_ref):
        i = pl.program_id(0)
        is_ctx = i >= nl

        @pl.when(i == 0)
        def _():
            acc_ref[...] = jnp.zeros_like(acc_ref)

        dr = jnp.where(is_ctx, dc_ref[...], dl_ref[...]).astype(F32)
        dfx = (dr * _sel_row(g_ref, is_ctx)).astype(BF)
        acc_ref[:half, :] += _dot_tn(q_ref[0], dfx)
        acc_ref[half:, :] += _dot_tn(q_ref[1], dfx)

        @pl.when(i == nt - 1)
        def _():
            o_ref[...] = acc_ref[...].astype(BF)

    lat, cx = _lat_or_ctx_specs(tm, d, nl, 1, 0)
    (g_w,), extra = _hosted_call(
        body, xch, grid=(nt,),
        in_specs=[pl.BlockSpec((2, tm, half), lambda i: (0, i, 0)), lat, cx, _full((2, d))],
        out_specs=[_full((2 * half, d))], out_shape=[jax.ShapeDtypeStruct((2 * half, d), BF)],
        scratch=[pltpu.VMEM((2 * half, d), F32)], args=(q3, dr_l, dr_c, gt2), name="l0_dw_outproj")
    return g_w, extra


def _cr_tile(j, cap=256):
    for cand in (1024, 512, 256, 128, 64, 32, 16, 8):
        if cand <= cap and j % cand == 0:
            return cand
    raise ValueError(j)


def _final(w_cr, w_out, xh_cr, tgt_cr, vecs):
    j, e16 = w_cr.shape
    e = e16 // CHUNK
    d = w_out.shape[1]
    tj = _cr_tile(j)

    def body(w_ref, wo_hbm, xh_ref, t_ref, v_ref, dr_ref, acc_ref, wo_ref):
        @pl.when(jnp.logical_and(pl.program_id(0) == 0, pl.program_id(1) == 0))
        def _():
            pltpu.sync_copy(wo_hbm, wo_ref)
            acc_ref[...] = jnp.zeros_like(acc_ref)

        o = _dot(w_ref[...], wo_ref[...])
        x1 = xh_ref[...] * v_ref[0:1, :] + v_ref[1:2, :]
        rr = DN_ALPHA * x1 + v_ref[2:3, :] * o
        mu = jnp.mean(rr, axis=-1, keepdims=True)
        cen = rr - mu
        rstd = lax.rsqrt(jnp.mean(cen * cen, axis=-1, keepdims=True) + LN_EPS)
        xh2 = cen * rstd
        err = xh2 * v_ref[3:4, :] + v_ref[4:5, :] - t_ref[...]
        dy = err * (1.0 / d)
        dxh = dy * v_ref[3:4, :]
        dr = rstd * (dxh - jnp.mean(dxh, axis=-1, keepdims=True) - xh2 * jnp.mean(dxh * xh2, axis=-1, keepdims=True))
        dr_ref[...] = dr.astype(BF)
        acc_ref[0:1, :] += jnp.sum(dy * xh2, axis=0, keepdims=True)
        acc_ref[1:2, :] += jnp.sum(dy, axis=0, keepdims=True)
        acc_ref[2:3, :] += jnp.sum(dr * o, axis=0, keepdims=True)
        acc_ref[3:4, :] += (0.5 / d) * jnp.sum(err * err, axis=0, keepdims=True)

    tok_d = pl.BlockSpec((tj, d), lambda t, s: (t, s))
    return pl.pallas_call(
        body, name="l1_final", grid=(j // tj, CHUNK),
        in_specs=[pl.BlockSpec((tj, e), lambda t, s: (t, s)), ANY, tok_d, tok_d, _full((8, d))],
        out_specs=[tok_d, _full((8, d))],
        out_shape=[jax.ShapeDtypeStruct((j, CHUNK * d), BF), jax.ShapeDtypeStruct((8, d), F32)],
        scratch_shapes=[pltpu.VMEM(w_out.shape, BF)], compiler_params=_cparams(),
    )(w_cr, w_out, xh_cr, tgt_cr, vecs)


def _dw_cr(lhs, rhs, lhs_kind, rhs_kind, vec, bias_sum, init, name):
    j = lhs.shape[0]
    k = lhs.shape[1] // CHUNK
    n = rhs.shape[1] // CHUNK
    tj = _cr_tile(j, 512)
    nh = 2 if k * n * 4 > (8 << 20) else 1
    tn = n // nh
    nt = j // tj
    has_init = init is not None

    def body(*refs):
        refs = list(refs)
        l_ref, r_ref = refs[0], refs[1]
        pos = 2
        v_ref = None
        if vec is not None:
            v_ref = refs[pos]
            pos += 1
        i_ref = None
        if has_init:
            i_ref = refs[pos]
            pos += 1
        o_ref = refs[pos]
        pos += 1
        bs_ref = None
        if bias_sum:
            bs_ref = refs[pos]
            pos += 1
        acc_ref = refs[pos]
        t, s = pl.program_id(1), pl.program_id(2)
        first = jnp.logical_and(t == 0, s == 0)

        @pl.when(first)
        def _():
            acc_ref[...] = i_ref[...] if has_init else jnp.zeros_like(acc_ref)
            if bias_sum:
                bs_ref[...] = jnp.zeros_like(bs_ref)

        if lhs_kind == "mod":
            lv = (l_ref[...] * v_ref[0:1, :] + v_ref[1:2, :]).astype(BF)
        else:
            lv = l_ref[...]
        if rhs_kind == "scaled":
            rv = (r_ref[...].astype(F32) * v_ref[0:1, :]).astype(BF)
        else:
            rv = r_ref[...]
        acc_ref[...] += _dot_tn(lv, rv)
        if bias_sum:
            bs_ref[0:1, :] += jnp.sum(rv.astype(F32), axis=0, keepdims=True)

        @pl.when(jnp.logical_and(t == nt - 1, s == CHUNK - 1))
        def _():
            o_ref[...] = acc_ref[...].astype(BF)

    l_spec = pl.BlockSpec((tj, k), lambda h, t, s: (t, s))
    r_spec = pl.BlockSpec((tj, tn), lambda h, t, s: (t, s * nh + h))
    in_specs, args = [l_spec, r_spec], [lhs, rhs]
    if vec is not None:
        in_specs.append(_full(vec.shape))
        args.append(vec)
    o_spec = pl.BlockSpec((k, tn), lambda h, t, s: (0, h))
    if has_init:
        in_specs.append(o_spec)
        args.append(init)
    out_specs, out_shape = [o_spec], [jax.ShapeDtypeStruct((k, n), BF)]
    if bias_sum:
        out_specs.append(pl.BlockSpec((8, tn), lambda h, t, s: (0, h)))
        out_shape.append(jax.ShapeDtypeStruct((8, n), F32))
    res = pl.pallas_call(
        body, name=name, grid=(nh, nt, CHUNK), in_specs=in_specs, out_specs=out_specs, out_shape=out_shape,
        scratch_shapes=[pltpu.VMEM((k, tn), F32)], compiler_params=_cparams(),
    )(*args)
    return res if bias_sum else res[0]


GT_ROWS = CHUNK * S5_P
ZG_W = 2 * 2 * S5_N
PAIR_W = 2 * ZG_W
GROUPS_PER_STEP = 4


def _inproj1_gt(xh_cr, a1, b1, wu_t, w_z, tag):
    j, d16 = xh_cr.shape
    d = d16 // CHUNK
    e = wu_t.shape[0]
    g = e // S5_P
    tj = _cr_tile(j, 256)

    def body(x_ref, a_ref, b_ref, wu_hbm, wz_hbm, u_ref, z_ref, wu_ref, wz_ref):
        @pl.when(jnp.logical_and(pl.program_id(0) == 0, pl.program_id(1) == 0))
        def _():
            pltpu.sync_copy(wu_hbm, wu_ref)
            pltpu.sync_copy(wz_hbm, wz_ref)

        h = (x_ref[...] * a_ref[...] + b_ref[...]).astype(BF)
        u_ref[...] = _dot_nt(wu_ref[...], h).reshape(g, S5_P, tj).astype(BF)
        z_ref[...] = _dot(h, wz_ref[...]).astype(BF)

    return pl.pallas_call(
        body, name="l1_inproj_" + tag, grid=(j // tj, CHUNK),
        in_specs=[pl.BlockSpec((tj, d), lambda t, s: (t, s)), _full((1, d)), _full((1, d)), ANY, ANY],
        out_specs=[pl.BlockSpec((g, S5_P, tj), lambda t, s: (0, s, t)), pl.BlockSpec((tj, e), lambda t, s: (t, s))],
        out_shape=[jax.ShapeDtypeStruct((g, GT_ROWS, j), BF), jax.ShapeDtypeStruct((j, CHUNK * e), BF)],
        scratch_shapes=[pltpu.VMEM(wu_t.shape, BF), pltpu.VMEM(w_z.shape, BF)], compiler_params=_cparams(),
    )(xh_cr, a1, b1, wu_t, w_z)


def _gt_spec(j, gb=GROUPS_PER_STEP):
    return pl.BlockSpec((gb, GT_ROWS, j), lambda i: (i, 0, 0))


def _zg_spec(j, gb=GROUPS_PER_STEP):
    return pl.BlockSpec((j, gb * ZG_W), lambda i: (0, i))


def _w_spec(width, gb=GROUPS_PER_STEP):
    return pl.BlockSpec((gb, GT_ROWS, width), lambda i: (i, 0, 0))


def _pair_lanes(k):
    return slice((k // 2) * PAIR_W, (k // 2 + 1) * PAIR_W)


def _s5_z(ut_l, ut_c, bc):
    g, _, jl = ut_l.shape
    jc = ut_c.shape[2]
    gb = GROUPS_PER_STEP

    def body(ul_ref, uc_ref, bc_ref, zl_ref, zc_ref):
        for k in range(0, gb, 2):
            zl_ref[:, _pair_lanes(k)] = _dot_tn(ul_ref[k], bc_ref[k]) + _dot_tn(ul_ref[k + 1], bc_ref[k + 1])
            zc_ref[:, _pair_lanes(k)] = _dot_tn(uc_ref[k], bc_ref[k]) + _dot_tn(uc_ref[k + 1], bc_ref[k + 1])

    return pl.pallas_call(
        body, name="l1_s5_z", grid=(g // gb,), in_specs=[_gt_spec(jl), _gt_spec(jc), _w_spec(PAIR_W)],
        out_specs=[_zg_spec(jl), _zg_spec(jc)],
        out_shape=[jax.ShapeDtypeStruct((jl, g * ZG_W), F32), jax.ShapeDtypeStruct((jc, g * ZG_W), F32)],
        compiler_params=_cparams(),
    )(ut_l, ut_c, bc)


def _s5_y(ut_l, s_l, mt_t, cct):
    g, _, jl = ut_l.shape
    gb = GROUPS_PER_STEP

    def body(u_ref, s_ref, mt_ref, cc_ref, y_ref):
        for k in range(gb):
            s_k = s_ref[:, _pair_lanes(k)].astype(BF)
            y_ref[k] = (_dot(mt_ref[k], u_ref[k]) + _dot_nt(cc_ref[k], s_k)).astype(BF)

    return pl.pallas_call(
        body, name="l1_s5_y", grid=(g // gb,),
        in_specs=[_gt_spec(jl), _zg_spec(jl), _w_spec(GT_ROWS), _w_spec(PAIR_W)],
        out_specs=_gt_spec(jl), out_shape=jax.ShapeDtypeStruct((g, GT_ROWS, jl), BF), compiler_params=_cparams(),
    )(ut_l, s_l, mt_t, cct)


def _s5_ds(dyt_l, cct):
    g, _, jl = dyt_l.shape
    gb = GROUPS_PER_STEP

    def body(dy_ref, cc_ref, ds_ref):
        for k in range(0, gb, 2):
            ds_ref[:, _pair_lanes(k)] = _dot_tn(dy_ref[k], cc_ref[k]) + _dot_tn(dy_ref[k + 1], cc_ref[k + 1])

    return pl.pallas_call(
        body, name="l1_s5_ds", grid=(g // gb,), in_specs=[_gt_spec(jl), _w_spec(PAIR_W)], out_specs=_zg_spec(jl),
        out_shape=jax.ShapeDtypeStruct((jl, g * ZG_W), F32), compiler_params=_cparams(),
    )(dyt_l, cct)


def _s5_dx(dyt_l, dz_l, dz_c, mt, bc):
    g, _, jl = dyt_l.shape
    jc = dz_c.shape[0]
    gb = GROUPS_PER_STEP

    def body(dy_ref, dzl_ref, dzc_ref, mt_ref, bc_ref, dul_ref, duc_ref):
        for k in range(gb):
            dzl = dzl_ref[:, _pair_lanes(k)].astype(BF)
            dzc = dzc_ref[:, _pair_lanes(k)].astype(BF)
            dul_ref[k] = (_dot(mt_ref[k], dy_ref[k]) + _dot_nt(bc_ref[k], dzl)).astype(BF)
            duc_ref[k] = _dot_nt(bc_ref[k], dzc).astype(BF)

    return pl.pallas_call(
        body, name="l1_s5_dx", grid=(g // gb,),
        in_specs=[_gt_spec(jl), _zg_spec(jl), _zg_spec(jc), _w_spec(GT_ROWS), _w_spec(PAIR_W)],
        out_specs=[_gt_spec(jl), _gt_spec(jc)],
        out_shape=[jax.ShapeDtypeStruct((g, GT_ROWS, jl), BF), jax.ShapeDtypeStruct((g, GT_ROWS, jc), BF)],
        compiler_params=_cparams(),
    )(dyt_l, dz_l, dz_c, mt, bc)


def _s5_dw(ut_l, ut_c, dyt_l, dz_l, dz_c, s_l):
    g, _, jl = ut_l.shape
    jc = ut_c.shape[2]
    gb = GROUPS_PER_STEP

    def body(ul_ref, uc_ref, dy_ref, dzl_ref, dzc_ref, s_ref, dmt_ref, dbc_ref, dcc_ref):
        for k in range(gb):
            lanes = _pair_lanes(k)
            dmt_ref[k] = _dot_nt(ul_ref[k], dy_ref[k])
            dbc_ref[k] = (_dot(ul_ref[k], dzl_ref[:, lanes].astype(BF))
                          + _dot(uc_ref[k], dzc_ref[:, lanes].astype(BF)))
            dcc_ref[k] = _dot(dy_ref[k], s_ref[:, lanes].astype(BF))

    sd_m = jax.ShapeDtypeStruct((g, GT_ROWS, GT_ROWS), F32)
    sd_p = jax.ShapeDtypeStruct((g, GT_ROWS, PAIR_W), F32)
    return pl.pallas_call(
        body, name="l1_s5_dw", grid=(g // gb,),
        in_specs=[_gt_spec(jl), _gt_spec(jc), _gt_spec(jl), _zg_spec(jl), _zg_spec(jc), _zg_spec(jl)],
        out_specs=[_w_spec(GT_ROWS), _w_spec(PAIR_W), _w_spec(PAIR_W)], out_shape=[sd_m, sd_p, sd_p],
        compiler_params=_cparams(),
    )(ut_l, ut_c, dyt_l, dz_l, dz_c, s_l)


def _scan_g(z_l, z_c, coef, chains, conj, s_l=None, s_c=None, name="l1_scan"):
    jl, w_all = z_l.shape
    jc = z_c.shape[0]
    gb = 2 * GROUPS_PER_STEP if w_all % (2 * GROUPS_PER_STEP * ZG_W) == 0 else GROUPS_PER_STEP
    wb = gb * ZG_W
    nch = wb // 256
    with_da = s_l is not None
    sign = -1.0 if conj else 1.0

    def body(*refs):
        zl_ref, zc_ref, cf_ref = refs[:3]
        k0 = 3
        if with_da:
            sl_ref, sc_ref = refs[3:5]
            k0 = 5
        ol_ref, oc_ref = refs[k0:k0 + 2]
        rowi = lax.broadcasted_iota(jnp.int32, (8, 128), 0)

        def lanes_of(ch):
            return slice(ch * 256, ch * 256 + 128), slice(ch * 256 + 128, (ch + 1) * 256)

        def coefs(ch, r0, nr):
            lr, li = lanes_of(ch)
            return cf_ref[r0:r0 + nr, lr], sign * cf_ref[r0:r0 + nr, li]

        def shift(v, sh, rev):
            if rev:
                return jnp.where(rowi < 8 - sh, pltpu.roll(v, 8 - sh, 0), 0.0)
            return jnp.where(rowi >= sh, pltpu.roll(v, sh, 0), 0.0)

        zero_row = jnp.zeros((1, 128), F32)
        zero_tile = jnp.zeros((8, 128), F32)
        carry = [zero_row] * (2 * nch)
        da = [zero_tile] * (2 * nch)
        for seg in range(len(chains[0])):
            which = chains[0][seg][0]
            assert chains[1][seg][0] == which
            revs = (chains[0][seg][1], chains[1][seg][1])
            src, dst = (zc_ref, oc_ref) if which == "c" else (zl_ref, ol_ref)
            sref = ((sc_ref if which == "c" else sl_ref) if with_da else None)
            ng = (jc if which == "c" else jl) // 8

            def step(it, st, src=src, dst=dst, sref=sref, ng=ng, revs=revs):
                carry_, da_ = list(st[:2 * nch]), list(st[2 * nch:])
                for ch in range(nch):
                    rev = revs[ch % 2]
                    lr, li = lanes_of(ch)
                    grp = (ng - 1 - it) if rev else it
                    off = pl.multiple_of(grp * 8, 8)
                    xr, xi = src[pl.ds(off, 8), lr], src[pl.ds(off, 8), li]
                    for sh, r0 in ((1, 0), (2, 1), (4, 2)):
                        ar, ai = coefs(ch, r0, 1)
                        sr, si = shift(xr, sh, rev), shift(xi, sh, rev)
                        xr, xi = xr + ar * sr - ai * si, xi + ar * si + ai * sr
                    tr, ti = coefs(ch, 16, 8) if rev else coefs(ch, 8, 8)
                    cr_, ci_ = carry_[2 * ch], carry_[2 * ch + 1]
                    ir = xr + tr * cr_ - ti * ci_
                    ii = xi + tr * ci_ + ti * cr_
                    if rev:
                        er = jnp.where(rowi == 7, cr_, pltpu.roll(ir, 7, 0))
                        ei = jnp.where(rowi == 7, ci_, pltpu.roll(ii, 7, 0))
                        carry_[2 * ch], carry_[2 * ch + 1] = ir[0:1], ii[0:1]
                    else:
                        er = jnp.where(rowi == 0, cr_, pltpu.roll(ir, 1, 0))
                        ei = jnp.where(rowi == 0, ci_, pltpu.roll(ii, 1, 0))
                        carry_[2 * ch], carry_[2 * ch + 1] = ir[7:8], ii[7:8]
                    dst[pl.ds(off, 8), lr] = er
                    dst[pl.ds(off, 8), li] = ei
                    if sref is not None:
                        s_r, s_i = sref[pl.ds(off, 8), lr], sref[pl.ds(off, 8), li]
                        da_[2 * ch] = da_[2 * ch] + s_r * er + s_i * ei
                        da_[2 * ch + 1] = da_[2 * ch + 1] + s_r * ei - s_i * er
                return (*carry_, *da_)

            st = lax.fori_loop(0, ng, step, (*carry, *da))
            carry, da = list(st[:2 * nch]), list(st[2 * nch:])
        if with_da:
            da_ref = refs[k0 + 2]
            for ch in range(nch):
                lr, li = lanes_of(ch)
                da_ref[:, lr] = da[2 * ch]
                da_ref[:, li] = da[2 * ch + 1]

    in_specs = [_zg_spec(jl, gb), _zg_spec(jc, gb), pl.BlockSpec((24, wb), lambda i: (0, i))]
    args = [z_l, z_c, coef]
    out_specs = [_zg_spec(jl, gb), _zg_spec(jc, gb)]
    out_shape = [jax.ShapeDtypeStruct(z_l.shape, F32), jax.ShapeDtypeStruct(z_c.shape, F32)]
    if with_da:
        in_specs += [_zg_spec(jl, gb), _zg_spec(jc, gb)]
        args += [s_l, s_c]
        out_specs.append(pl.BlockSpec((8, wb), lambda i: (0, i)))
        out_shape.append(jax.ShapeDtypeStruct((8, w_all), F32))
    return pl.pallas_call(body, name=name, grid=(w_all // wb,), in_specs=in_specs, out_specs=out_specs,
                          out_shape=out_shape, compiler_params=_cparams())(*args)


def _gt_tok_spec(g, tj):
    return pl.BlockSpec((g, S5_P, tj), lambda t, s: (0, s, t))


def _glu_fwd_gt(yt, z_cr, w_glu, b_glu):
    g, _, j = yt.shape
    e = g * S5_P
    tj = _cr_tile(j)

    def body(y_ref, z_ref, w_hbm, b_ref, o_ref, sg_ref, w_ref):
        @pl.when(jnp.logical_and(pl.program_id(0) == 0, pl.program_id(1) == 0))
        def _():
            pltpu.sync_copy(w_hbm, w_ref)

        y = jnp.transpose(y_ref[...].reshape(e, tj).astype(F32))
        gl = _gelu_parts(y)[0]
        sg = _sigmoid(_dot(gl.astype(BF), w_ref[...]) + b_ref[...])
        z = z_ref[...].astype(F32)
        o_ref[...] = (gl * sg * (z * _sigmoid(z))).astype(BF)
        sg_ref[...] = sg.astype(BF)

    tok = pl.BlockSpec((tj, e), lambda t, s: (t, s))
    return pl.pallas_call(
        body, name="l1_glu_fwd", grid=(j // tj, CHUNK),
        in_specs=[_gt_tok_spec(g, tj), tok, ANY, _full((1, e))], out_specs=[tok, tok],
        out_shape=[jax.ShapeDtypeStruct((j, CHUNK * e), BF), jax.ShapeDtypeStruct((j, CHUNK * e), BF)],
        scratch_shapes=[pltpu.VMEM(w_glu.shape, BF)], compiler_params=_cparams(),
    )(yt, z_cr, w_glu, b_glu)


def _glu_bwd_gt(dr_cr, gt1, w_out, w_glu, yt, z_cr, sg_cr):
    g, _, j = yt.shape
    e, d = w_out.shape
    tj = _cr_tile(j)

    def body(dr_ref, g_ref, wo_hbm, wg_hbm, y_ref, z_ref, sg_ref, dz_ref, dt_ref, dy_ref, wo_ref, wg_ref):
        @pl.when(jnp.logical_and(pl.program_id(0) == 0, pl.program_id(1) == 0))
        def _():
            pltpu.sync_copy(wo_hbm, wo_ref)
            pltpu.sync_copy(wg_hbm, wg_ref)

        do = (dr_ref[...].astype(F32) * g_ref[...]).astype(BF)
        dw = _dot_nt(do, wo_ref[...])
        y = jnp.transpose(y_ref[...].reshape(e, tj).astype(F32))
        gl, dgel = _gelu_parts(y)
        z = z_ref[...].astype(F32)
        sz = _sigmoid(z)
        sg = sg_ref[...].astype(F32)
        dg2 = dw * (z * sz)
        dz_ref[...] = (dw * gl * sg * (sz * (1.0 + z * (1.0 - sz)))).astype(BF)
        dt = (dg2 * gl * sg * (1.0 - sg)).astype(BF)
        dt_ref[...] = dt
        dy = (dg2 * sg + _dot_nt(dt, wg_ref[...])) * dgel
        dy_ref[...] = jnp.transpose(dy).reshape(g, S5_P, tj).astype(BF)

    tok_e = pl.BlockSpec((tj, e), lambda t, s: (t, s))
    return pl.pallas_call(
        body, name="l1_glu_bwd", grid=(j // tj, CHUNK),
        in_specs=[pl.BlockSpec((tj, d), lambda t, s: (t, s)), _full((1, d)), ANY, ANY, _gt_tok_spec(g, tj), tok_e, tok_e],
        out_specs=[tok_e, tok_e, _gt_tok_spec(g, tj)],
        out_shape=[jax.ShapeDtypeStruct((j, CHUNK * e), BF), jax.ShapeDtypeStruct((j, CHUNK * e), BF),
                   jax.ShapeDtypeStruct((g, GT_ROWS, j), BF)],
        scratch_shapes=[pltpu.VMEM(w_out.shape, BF), pltpu.VMEM(w_glu.shape, BF)], compiler_params=_cparams(),
    )(dr_cr, gt1, w_out, w_glu, yt, z_cr, sg_cr)


def _bwd_inproj1_gt(dut, dz_cr, wu_t, w_z, xh_cr, rs_cr, dr2_cr, vecs, tag):
    g, _, j = dut.shape
    e, d = wu_t.shape
    tj = _cr_tile(j)

    def body(du_ref, dz_ref, wu_hbm, wz_hbm, xh_ref, rs_ref, dr2_ref, v_ref, dr1_ref, acc_ref, wu_ref, wz_ref):
        @pl.when(jnp.logical_and(pl.program_id(0) == 0, pl.program_id(1) == 0))
        def _():
            pltpu.sync_copy(wu_hbm, wu_ref)
            pltpu.sync_copy(wz_hbm, wz_ref)
            acc_ref[...] = jnp.zeros_like(acc_ref)

        dh = _dot_tn(du_ref[...].reshape(e, tj), wu_ref[...]) + _dot_nt(dz_ref[...], wz_ref[...])
        xh = xh_ref[...]
        x1 = xh * v_ref[0:1, :] + v_ref[1:2, :]
        dx1 = DN_ALPHA * dr2_ref[...].astype(F32) + dh * v_ref[2:3, :]
        dxh = dx1 * v_ref[0:1, :]
        rstd = rs_ref[:, 0:1]
        dr1 = rstd * (dxh - jnp.mean(dxh, axis=-1, keepdims=True) - xh * jnp.mean(dxh * xh, axis=-1, keepdims=True))
        dr1_ref[...] = dr1.astype(BF)
        acc_ref[0:1, :] += jnp.sum(dh * x1, axis=0, keepdims=True)
        acc_ref[1:2, :] += jnp.sum(dh, axis=0, keepdims=True)
        acc_ref[2:3, :] += jnp.sum(dx1 * xh, axis=0, keepdims=True)
        acc_ref[3:4, :] += jnp.sum(dx1, axis=0, keepdims=True)

    tok_d = pl.BlockSpec((tj, d), lambda t, s: (t, s))
    return pl.pallas_call(
        body, name="l1_bwd_inproj_" + tag, grid=(j // tj, CHUNK),
        in_specs=[_gt_tok_spec(g, tj), pl.BlockSpec((tj, e), lambda t, s: (t, s)), ANY, ANY, tok_d,
                  pl.BlockSpec((tj, 128), lambda t, s: (t, s)), tok_d, _full((8, d))],
        out_specs=[tok_d, _full((8, d))],
        out_shape=[jax.ShapeDtypeStruct((j, CHUNK * d), BF), jax.ShapeDtypeStruct((8, d), F32)],
        scratch_shapes=[pltpu.VMEM(wu_t.shape, BF), pltpu.VMEM(w_z.shape, BF)], compiler_params=_cparams(),
    )(dut, dz_cr, wu_t, w_z, xh_cr, rs_cr, dr2_cr, vecs)


def _dw_gt(lhs_gt, rhs_cr, lhs_gelu, vec, bias_sum, init, out_dtype, name, xch=None):
    g, _, j = lhs_gt.shape
    e = g * S5_P
    n = rhs_cr.shape[1] // CHUNK
    tj = _cr_tile(j, 512 if j % 512 == 0 else 256)
    nh = 2 if e * n * 4 > (8 << 20) else 1
    tn = n // nh
    nt = j // tj
    has_init = init is not None

    def body(*refs):
        refs = list(refs)
        l_ref, r_ref = refs[0], refs[1]
        pos = 2
        v_ref = i_ref = bs_ref = None
        if vec is not None:
            v_ref = refs[pos]
            pos += 1
        if has_init:
            i_ref = refs[pos]
            pos += 1
        o_ref = refs[pos]
        pos += 1
        if bias_sum:
            bs_ref = refs[pos]
            pos += 1
        acc_ref = refs[pos]
        t, s = pl.program_id(1), pl.program_id(2)

        @pl.when(jnp.logical_and(t == 0, s == 0))
        def _():
            acc_ref[...] = i_ref[...] if has_init else jnp.zeros_like(acc_ref)
            if bias_sum:
                bs_ref[...] = jnp.zeros_like(bs_ref)

        lv = l_ref[...].reshape(e, tj)
        if lhs_gelu:
            lv = _gelu_parts(lv.astype(F32))[0].astype(BF)
        if vec is not None:
            rv = (r_ref[...] * v_ref[0:1, :] + v_ref[1:2, :]).astype(BF)
        else:
            rv = r_ref[...]
        acc_ref[...] += _dot(lv, rv)
        if bias_sum:
            bs_ref[0:1, :] += jnp.sum(rv.astype(F32), axis=0, keepdims=True)

        @pl.when(jnp.logical_and(t == nt - 1, s == CHUNK - 1))
        def _():
            o_ref[...] = acc_ref[...].astype(out_dtype)

    in_specs = [pl.BlockSpec((g, S5_P, tj), lambda h, t, s: (0, s, t)),
                pl.BlockSpec((tj, tn), lambda h, t, s: (t, s * nh + h))]
    args = [lhs_gt, rhs_cr]
    if vec is not None:
        in_specs.append(_full(vec.shape))
        args.append(vec)
    o_spec = pl.BlockSpec((e, tn), lambda h, t, s: (0, h))
    if has_init:
        in_specs.append(o_spec)
        args.append(init)
    out_specs, out_shape = [o_spec], [jax.ShapeDtypeStruct((e, n), out_dtype)]
    if bias_sum:
        out_specs.append(pl.BlockSpec((8, tn), lambda h, t, s: (0, h)))
        out_shape.append(jax.ShapeDtypeStruct((8, n), F32))
    res, extra = _hosted_call(body, xch, grid=(nh, nt, CHUNK), in_specs=in_specs, out_specs=out_specs,
                              out_shape=out_shape, scratch=[pltpu.VMEM((e, tn), F32)], args=args, name=name)
    if xch is not None:
        return (*res, extra) if bias_sum else (res[0], extra)
    return res if bias_sum else res[0]


def _scan_coef_g(lam_re, lam_im, log_step):
    g = lam_re.shape[1]
    ms = jnp.array([1, 2, 4, 0, 0, 0, 0, 0] + list(range(1, 9)) + list(range(8, 0, -1)), F32) * CHUNK
    dt = jnp.exp(log_step)[..., None]
    mag = jnp.exp(ms.reshape(-1, 1, 1, 1) * (lam_re * dt)[None])
    ang = ms.reshape(-1, 1, 1, 1) * (lam_im * dt)[None]
    cr, ci = mag * jnp.cos(ang), mag * jnp.sin(ang)
    both = jnp.stack([cr, ci], axis=2).reshape(24, 2, 2, g // 2, 2, S5_N)
    return both.transpose(0, 3, 1, 2, 4, 5).reshape(24, g * ZG_W)


def _s5_small(lam_re, lam_im, log_step, b_re, b_im, c_re, c_im, d_skip):
    g = lam_re.shape[1]
    t, p = CHUNK, S5_P
    dt = jnp.exp(log_step)[..., None]
    ks = jnp.arange(t + 1, dtype=F32).reshape(t + 1, 1, 1, 1)
    mag = jnp.exp(ks * (lam_re * dt)[None])
    ang = ks * (lam_im * dt)[None]
    pr, pi = mag * jnp.cos(ang), mag * jnp.sin(ang)
    ar, ai = pr[1], pi[1]
    qr, qi = ar - 1.0, ai
    den = lam_re * lam_re + lam_im * lam_im
    fr = (qr * lam_re + qi * lam_im) / den
    fi = (qi * lam_re - qr * lam_im) / den
    bt_re, bt_im = b_re.transpose(0, 1, 3, 2), b_im.transpose(0, 1, 3, 2)
    bbr = fr[:, :, None, :] * bt_re - fi[:, :, None, :] * bt_im
    bbi = fr[:, :, None, :] * bt_im + fi[:, :, None, :] * bt_re
    lay = lambda a_r, a_i: jnp.stack([a_r, a_i], axis=0).transpose(3, 2, 0, 1, 4)
    by_dir = lambda a, f0, f1: jnp.stack([f0(a[:, 0]), f1(a[:, 1])], axis=1)
    rev = lambda a: jnp.flip(a, axis=0)
    same = lambda a: a
    pwb = lay(by_dir(pr[:t], rev, same), by_dir(pi[:t], rev, same))
    pwc = lay(by_dir(pr[1:], same, rev), by_dir(pi[1:], same, rev))
    bb = jnp.stack([bbr, bbi], axis=0).transpose(2, 1, 0, 3, 4)
    cc = jnp.stack([c_re, c_im], axis=0).transpose(2, 1, 0, 3, 4)
    dmat = jnp.eye(p, dtype=F32)[None] * d_skip.reshape(g, p)[:, :, None]
    return pwb, pwc, bb, cc, dmat, pr[t], pi[t]


def _pair_cols(r, ri, g2):
    c0 = (r * 2 + ri) * 128 + g2 * S5_N
    return slice(c0, c0 + S5_N)


def _rows_rep(a):
    return jnp.broadcast_to(a[:, None, :], (CHUNK, S5_P, a.shape[-1])).reshape(GT_ROWS, a.shape[-1])


def _rows_tile(a):
    return jnp.broadcast_to(a[None], (CHUNK, S5_P, a.shape[-1])).reshape(GT_ROWS, a.shape[-1])


def _sum_blocks(a):
    return jnp.sum(a.reshape(CHUNK, S5_P, a.shape[-1]), axis=0)


def _sum_in_blocks(a):
    return jnp.sum(a.reshape(CHUNK, S5_P, a.shape[-1]), axis=1)


def _ab_rows(pwb_ref, bb_ref, k, r):
    prs, pis = _rows_rep(pwb_ref[k, r, 0]), _rows_rep(pwb_ref[k, r, 1])
    bbr, bbi = _rows_tile(bb_ref[k, r, 0]), _rows_tile(bb_ref[k, r, 1])
    return prs * bbr - pis * bbi, prs * bbi + pis * bbr, prs, pis, bbr, bbi


def _s5_weights_fwd(pwb, pwc, bb, cc, dmat):
    g = pwb.shape[0]
    gb = GROUPS_PER_STEP
    hp = lax.Precision.HIGHEST

    def body(pwb_ref, pwc_ref, bb_ref, cc_ref, dm_ref, mt_ref, mtt_ref, bc_ref, cct_ref):
        zeros = jnp.zeros((GT_ROWS, S5_N), BF)
        nt = (((1,), (1,)), ((), ()))
        for k in range(gb):
            g2 = k % 2
            kds = []
            for r in range(2):
                for ri in range(2):
                    bc_ref[k, :, _pair_cols(r, ri, 1 - g2)] = zeros
                    cct_ref[k, :, _pair_cols(r, ri, 1 - g2)] = zeros
                abr, abi = _ab_rows(pwb_ref, bb_ref, k, r)[:2]
                bc_ref[k, :, _pair_cols(r, 0, g2)] = abr.astype(BF)
                bc_ref[k, :, _pair_cols(r, 1, g2)] = abi.astype(BF)
                cr, ci = cc_ref[k, r, 0], cc_ref[k, r, 1]
                crt, cit = _rows_tile(cr), _rows_tile(ci)
                prt, pit = _rows_rep(pwc_ref[k, r, 0]), _rows_rep(pwc_ref[k, r, 1])
                cct_ref[k, :, _pair_cols(r, 0, g2)] = (crt * prt - cit * pit).astype(BF)
                cct_ref[k, :, _pair_cols(r, 1, g2)] = (-(crt * pit + cit * prt)).astype(BF)
                kds.append(lax.dot_general(abr, cr, nt, precision=hp, preferred_element_type=F32)
                           - lax.dot_general(abi, ci, nt, precision=hp, preferred_element_type=F32))
            blk = lambda a, s: a[s * S5_P:(s + 1) * S5_P]
            last = CHUNK - 1
            pieces = [blk(kds[1], last - i) for i in range(last)]
            pieces.append(blk(kds[0], last) + blk(kds[1], 0) + dm_ref[k])
            pieces += [blk(kds[0], last - d) for d in range(1, CHUNK)]
            qrow = jnp.concatenate(pieces, axis=1)
            mt = jnp.concatenate([qrow[:, (last - s) * S5_P:(last - s) * S5_P + GT_ROWS] for s in range(CHUNK)], axis=0)
            mt_ref[k] = mt.astype(BF)
            mtt_ref[k] = jnp.transpose(mt).astype(BF)

    small = lambda a: pl.BlockSpec((gb, *a.shape[1:]), lambda i: (i,) + (0,) * (a.ndim - 1))
    return pl.pallas_call(
        body, name="l1_s5_weights", grid=(g // gb,),
        in_specs=[small(pwb), small(pwc), small(bb), small(cc), small(dmat)],
        out_specs=[_w_spec(GT_ROWS), _w_spec(GT_ROWS), _w_spec(PAIR_W), _w_spec(PAIR_W)],
        out_shape=[jax.ShapeDtypeStruct((g, GT_ROWS, GT_ROWS), BF), jax.ShapeDtypeStruct((g, GT_ROWS, GT_ROWS), BF),
                   jax.ShapeDtypeStruct((g, GT_ROWS, PAIR_W), BF), jax.ShapeDtypeStruct((g, GT_ROWS, PAIR_W), BF)],
        compiler_params=_cparams(),
    )(pwb, pwc, bb, cc, dmat)


def _s5_weights_bwd(pwb, pwc, bb, cc, d_mt, d_bc, d_cct):
    g = pwb.shape[0]
    gb = GROUPS_PER_STEP
    hp = lax.Precision.HIGHEST

    def body(pwb_ref, pwc_ref, bb_ref, cc_ref, dmt_ref, dbc_ref, dcc_ref, dpwb_ref, dpwc_ref, dbb_ref, dccp_ref, ddm_ref):
        tn = (((0,), (0,)), ((), ()))
        nn = (((1,), (0,)), ((), ()))
        last = CHUNK - 1
        for k in range(gb):
            g2 = k % 2
            dq = None
            for s in range(CHUNK):
                parts = [dmt_ref[k, s * S5_P:(s + 1) * S5_P, :]]
                if s < last:
                    parts.insert(0, jnp.zeros((S5_P, (last - s) * S5_P), F32))
                if s > 0:
                    parts.append(jnp.zeros((S5_P, s * S5_P), F32))
                padded = jnp.concatenate(parts, axis=1) if len(parts) > 1 else parts[0]
                dq = padded if dq is None else dq + padded
            dblk = lambda d: dq[:, (last + d) * S5_P:(CHUNK + d) * S5_P]
            ddm_ref[k] = dblk(0)
            dkds = [jnp.concatenate([dblk(last - s) for s in range(CHUNK)], axis=0),
                    jnp.concatenate([dblk(-s) for s in range(CHUNK)], axis=0)]
            for r in range(2):
                abr, abi, prs, pis, bbr, bbi = _ab_rows(pwb_ref, bb_ref, k, r)
                cr, ci = cc_ref[k, r, 0], cc_ref[k, r, 1]
                dcr = lax.dot_general(dkds[r], abr, tn, precision=hp, preferred_element_type=F32)
                dci = -lax.dot_general(dkds[r], abi, tn, precision=hp, preferred_element_type=F32)
                dabr = (lax.dot_general(dkds[r], cr, nn, precision=hp, preferred_element_type=F32)
                        + dbc_ref[k, :, _pair_cols(r, 0, g2)])
                dabi = (-lax.dot_general(dkds[r], ci, nn, precision=hp, preferred_element_type=F32)
                        + dbc_ref[k, :, _pair_cols(r, 1, g2)])
                dbb_ref[k, r, 0] = _sum_blocks(prs * dabr + pis * dabi)
                dbb_ref[k, r, 1] = _sum_blocks(prs * dabi - pis * dabr)
                dpwb_ref[k, r, 0] = _sum_in_blocks(dabr * bbr + dabi * bbi)
                dpwb_ref[k, r, 1] = _sum_in_blocks(dabi * bbr - dabr * bbi)
                crt, cit = _rows_tile(cr), _rows_tile(ci)
                prt, pit = _rows_rep(pwc_ref[k, r, 0]), _rows_rep(pwc_ref[k, r, 1])
                d_re = dcc_ref[k, :, _pair_cols(r, 0, g2)]
                d_im = dcc_ref[k, :, _pair_cols(r, 1, g2)]
                dccp_ref[k, r, 0] = dcr + _sum_blocks(d_re * prt - d_im * pit)
                dccp_ref[k, r, 1] = dci - _sum_blocks(d_re * pit + d_im * prt)
                dpwc_ref[k, r, 0] = _sum_in_blocks(d_re * crt - d_im * cit)
                dpwc_ref[k, r, 1] = -_sum_in_blocks(d_re * cit + d_im * crt)

    small = lambda a: pl.BlockSpec((gb, *a.shape[1:]), lambda i: (i,) + (0,) * (a.ndim - 1))
    dmat_sds = jax.ShapeDtypeStruct((g, S5_P, S5_P), F32)
    return pl.pallas_call(
        body, name="l1_s5_weights_bwd", grid=(g // gb,),
        in_specs=[small(pwb), small(pwc), small(bb), small(cc), _w_spec(GT_ROWS), _w_spec(PAIR_W), _w_spec(PAIR_W)],
        out_specs=[small(pwb), small(pwc), small(bb), small(cc), small(dmat_sds)],
        out_shape=[jax.ShapeDtypeStruct(pwb.shape, F32), jax.ShapeDtypeStruct(pwc.shape, F32),
                   jax.ShapeDtypeStruct(bb.shape, F32), jax.ShapeDtypeStruct(cc.shape, F32), dmat_sds],
        compiler_params=_cparams(),
    )(pwb, pwc, bb, cc, d_mt, d_bc, d_cct)


def _to_cr(a):
    return a.reshape(a.shape[0] // CHUNK, CHUNK * a.shape[1])


def _from_cr(a, c):
    return a.reshape(a.shape[0] * CHUNK, c)


def _pad8(v):
    return jnp.concatenate([v, jnp.zeros((8 - v.shape[0], v.shape[1]), v.dtype)], axis=0)


def _local_step(x, c, ctx, c_ctx, loss_target, w, late=None, scatter=False, mod=None):
    l, d = x.shape
    lc = ctx.shape[0]
    tm = min(256, lc)
    assert lc == tm and l % tm == 0 and tm % GRID_W == 0 and (tm & (tm - 1)) == 0
    nl = l // tm

    own_mod = mod is None
    if own_mod:
        c8 = _pad8(jnp.stack([c, c_ctx]))
        mod = _ada_fwd(c8, w["ada_w"], w["ada_b"])
    sh = mod[:, :2, :d]
    sc = mod[:, :2, d:2 * d]
    gt = mod[:, :2, 2 * d:]
    ln_g, ln_b = w["ln_g"], w["ln_b"]

    a0, b0 = 1.0 + sc[0], sh[0]
    xch = _Exchange("gather2", [late[n][0] for n in late], [late[n][1] for n in late]) if late else None
    p42, got = _inproj0(x, ctx, a0, b0, w["conv_w_in"], tm, xch)
    if late:
        w = dict(w, **dict(zip(late, got)))
    e = w["conv_w_out"].shape[0]
    half = e // 2
    cw = w["conv_w"].reshape(3, 2, half)
    q3 = _conv_fwd(p42, cw, nl, tm, half)
    xh1_l, xh1_c, rs1_l, rs1_c, fx = _outproj_ln0(q3, w["conv_w_out"], x, ctx, gt[0], tm)
    jl, jc = l // CHUNK, lc // CHUNK

    g0, bb0 = ln_g[0:1], ln_b[0:1]
    a1 = g0 * (1.0 + sc[1])
    b1 = bb0 * (1.0 + sc[1]) + sh[1]
    wu_t = w["ssm_w_in"][:, :e].T
    w_z = w["ssm_w_in"][:, e:]
    ut_l, z_l = _inproj1_gt(xh1_l, a1[0:1], b1[0:1], wu_t, w_z, "lat")
    ut_c, _ = _inproj1_gt(xh1_c, a1[1:2], b1[1:2], wu_t, w_z, "ctx")
    s5 = (w["ssm_lam_re"], w["ssm_lam_im"], w["ssm_log_step"], w["ssm_b_re"], w["ssm_b_im"],
          w["ssm_c_re"], w["ssm_c_im"], w["ssm_d"])
    (pwb, pwc, bbw, ccw, dmat, _, _), s5_vjp = jax.vjp(_s5_small, *s5)
    mt_b, mtt_b, bc_b, cct_b = _s5_weights_fwd(pwb, pwc, bbw, ccw, dmat)
    coef = lax.stop_gradient(_scan_coef_g(*s5[:3]))
    zz_l, zz_c = _s5_z(ut_l, ut_c, bc_b)
    fwd_chains = ((("c", False), ("l", False)), (("c", True), ("l", True)))
    st_l, st_c = _scan_g(zz_l, zz_c, coef, fwd_chains, False, name="l1_scan_fwd")
    yt = _s5_y(ut_l, st_l, mtt_b, cct_b)
    b_glu = w["ssm_b_glu"].reshape(1, e)
    w_cr, sg_cr = _glu_fwd_gt(yt, z_l, w["ssm_w_glu"], b_glu)
    vec_f = _pad8(jnp.concatenate([g0, bb0, gt[1][0:1], ln_g[1:2], ln_b[1:2]], axis=0))
    dr2, acc_f = _final(w_cr, w["ssm_w_out"], xh1_l, _to_cr(loss_target), vec_f)
    loss = jnp.sum(acc_f[3])

    gt1 = gt[1][0:1]
    dz_l, dt_l, dyt = _glu_bwd_gt(dr2, gt1, w["ssm_w_out"], w["ssm_w_glu"], yt, z_l, sg_cr)
    g_w_out = _dw_cr(w_cr, dr2, "cr", "scaled", gt1, False, None, "l1_dw_out")
    ds_l = _s5_ds(dyt, cct_b)
    bwd_chains = ((("l", True), ("c", True)), (("l", False), ("c", False)))
    dzz_l, dzz_c, da = _scan_g(ds_l, jnp.zeros_like(zz_c), coef, bwd_chains, True, st_l, st_c, name="l1_scan_bwd")
    dut_l, dut_c = _s5_dx(dyt, dzz_l, dzz_c, mt_b, bc_b)
    d_mt, d_bc, d_cct = _s5_dw(ut_l, ut_c, dyt, dzz_l, dzz_c, st_l)
    n_g = e // S5_P
    da = jnp.sum(da, axis=0).reshape(n_g // 2, 2, 2, 2, S5_N).transpose(1, 2, 0, 3, 4)
    da = da.reshape(2, 2, n_g, S5_N)
    d_pwb, d_pwc, d_bb, d_ccp, d_dm = _s5_weights_bwd(pwb, pwc, bbw, ccw, d_mt, d_bc, d_cct)
    g_s5 = s5_vjp((d_pwb, d_pwc, d_bb, d_ccp, d_dm, da[:, 0], da[:, 1]))

    vec_l = _pad8(jnp.concatenate([g0, bb0, 1.0 + sc[1][0:1]], axis=0))
    vec_c = _pad8(jnp.concatenate([g0, bb0, 1.0 + sc[1][1:2]], axis=0))
    dr1_l, acc_l = _bwd_inproj1_gt(dut_l, dz_l, wu_t, w_z, xh1_l, rs1_l, dr2, vec_l, "lat")
    dr1_c, acc_c = _bwd_inproj1_gt(dut_c, jnp.zeros((jc, CHUNK * e), BF), wu_t, w_z, xh1_c, rs1_c,
                                   jnp.zeros((jc, CHUNK * d), BF), vec_c, "ctx")
    mod_l = jnp.concatenate([a1[0:1], b1[0:1]], axis=0)
    mod_c = jnp.concatenate([a1[1:2], b1[1:2]], axis=0)
    g_ut_c = _dw_gt(dut_c, xh1_c, False, mod_c, False, None, F32, "l1_dw_in_u_ctx")
    g_ut = _dw_gt(dut_l, xh1_l, False, mod_l, False, g_ut_c, BF, "l1_dw_in_u")
    g_in_z = _dw_cr(xh1_l, dz_l, "mod", "cr", mod_l, False, None, "l1_dw_in_z")
    g_w_in1 = jnp.concatenate([g_ut.T, g_in_z], axis=1)

    dr1_ln, dr1_cn = _from_cr(dr1_l, d), _from_cr(dr1_c, d)
    dq3, acc_g0 = _bwd_outproj0(dr1_ln, dr1_cn, gt[0], w["conv_w_out"], fx, tm)
    def carried(names, parts):
        return _Exchange("scatter", parts, [BIG[n] for n in names]) if scatter else None

    dp42, dcw, grad_x, acc_0, recv1 = _conv_bwd_inproj0(
        dq3, p42, cw, w["conv_w_in"], x, ctx, dr1_ln, dr1_cn, a0, nl, tm, carried(["ssm_w_in", "ssm_w_out"], [g_w_in1, g_w_out]))
    g_w_in0 = _dw_inproj0(x, ctx, a0, b0, dp42, tm)
    res = _dw_gt(yt, dt_l, True, None, True, None, BF, "l1_dw_glu", carried(["conv_w_in"], [g_w_in0]))
    g_w_glu, bsum, recv2 = res if scatter else (*res, [])
    g_b_glu = bsum[0]
    g_w_out0, recv3 = _dw_outproj0(q3, dr1_ln, dr1_cn, gt[0], tm, carried(["ssm_w_glu"], [g_w_glu]))
    recv = dict(zip(["ssm_w_in", "ssm_w_out", "conv_w_in", "ssm_w_glu"], recv1 + recv2 + recv3))

    zero = jnp.zeros((d,), F32)
    dm0 = jnp.stack([jnp.concatenate([acc_0[2], acc_0[0], acc_g0[0]]), jnp.concatenate([acc_0[3], acc_0[1], acc_g0[1]])])
    dm1 = jnp.stack([jnp.concatenate([acc_l[1], acc_l[0], acc_f[2]]), jnp.concatenate([acc_c[1], acc_c[0], zero])])
    if own_mod:
        g_ada_w, dc8 = _ada_bwd(c8, w["ada_w"], jnp.stack([_pad8(dm0), _pad8(dm1)]), BF)
        g_mod = {"c_ctx": dc8[0, 1] + dc8[1, 1], "ada_w": g_ada_w,
                 "ada_b": jnp.stack([dm0[0] + dm0[1], dm1[0] + dm1[1]])}
    else:
        g_mod = {"mod": jnp.stack([dm0, dm1])}

    grads = {
        **g_mod,
        "ln_g": jnp.stack([acc_l[2] + acc_c[2], acc_f[0]]),
        "ln_b": jnp.stack([acc_l[3] + acc_c[3], acc_f[1]]),
        "conv_w_in": g_w_in0, "conv_w": dcw[:3].reshape(3, e), "conv_w_out": g_w_out0,
        "ssm_w_in": g_w_in1,
        "ssm_lam_re": g_s5[0], "ssm_lam_im": g_s5[1], "ssm_log_step": g_s5[2],
        "ssm_b_re": g_s5[3], "ssm_b_im": g_s5[4], "ssm_c_re": g_s5[5], "ssm_c_im": g_s5[6], "ssm_d": g_s5[7],
        "ssm_w_glu": g_w_glu, "ssm_b_glu": g_b_glu, "ssm_w_out": g_w_out,
    }
    for n in recv:
        del grads[n]
    return loss, grad_x, grads, recv


WEIGHTS = ["c_ctx", "ada_w", "ada_b", "ln_g", "ln_b", "conv_w_in", "conv_w", "conv_w_out", "ssm_w_in",
           "ssm_lam_re", "ssm_lam_im", "ssm_log_step", "ssm_b_re", "ssm_b_im", "ssm_c_re", "ssm_c_im",
           "ssm_d", "ssm_w_glu", "ssm_b_glu", "ssm_w_out"]
BIG = {"ada_w": 1, "conv_w_in": 1, "conv_w_out": 0, "ssm_w_in": 1, "ssm_w_glu": 0, "ssm_w_out": 0}
SMALL_SHARDED = ["conv_w", "ssm_d", "ssm_b_glu"]
REPLICATED = ["c_ctx", "ada_b", "ln_g", "ln_b", "ssm_lam_re", "ssm_lam_im", "ssm_log_step",
              "ssm_b_re", "ssm_b_im", "ssm_c_re", "ssm_c_im"]
NATIVE_SMALL = ["ssm_b_re", "ssm_b_im", "ssm_c_re", "ssm_c_im"]


def _view2d(name, a):
    return a.reshape(-1, a.shape[-1])


def kernel(x, c, ctx, c_ctx, ada_w, ada_b, ln_g, ln_b, conv_w_in, conv_w, conv_w_out, ssm_w_in, ssm_lam_re, ssm_lam_im, ssm_log_step, ssm_b_re, ssm_b_im, ssm_c_re, ssm_c_im, ssm_d, ssm_w_glu, ssm_b_glu, ssm_w_out, loss_target, m_c_ctx, m_ada_w, m_ada_b, m_ln_g, m_ln_b, m_conv_w_in, m_conv_w, m_conv_w_out, m_ssm_w_in, m_ssm_lam_re, m_ssm_lam_im, m_ssm_log_step, m_ssm_b_re, m_ssm_b_im, m_ssm_c_re, m_ssm_c_im, m_ssm_d, m_ssm_w_glu, m_ssm_b_glu, m_ssm_w_out, v_c_ctx, v_ada_w, v_ada_b, v_ln_g, v_ln_b, v_conv_w_in, v_conv_w, v_conv_w_out, v_ssm_w_in, v_ssm_lam_re, v_ssm_lam_im, v_ssm_log_step, v_ssm_b_re, v_ssm_b_im, v_ssm_c_re, v_ssm_c_im, v_ssm_d, v_ssm_w_glu, v_ssm_b_glu, v_ssm_w_out):
    args = locals()
    wt = {n: args[n] for n in WEIGHTS}
    mt = {n: args["m_" + n] for n in WEIGHTS}
    vt = {n: args["v_" + n] for n in WEIGHTS}

    me = 4 * lax.axis_index("x") + 2 * lax.axis_index("y") + lax.axis_index("c")
    d = x.shape[-1]
    d3 = 3 * d
    wa = d3 // N_DEV

    big_names = [n for n in BIG if n != "ada_w"]
    shard = {n: _view2d(n, wt[n]).astype(BF) for n in big_names}
    small = jnp.concatenate([wt["conv_w"][0], wt["ssm_d"], wt["ssm_b_glu"]], axis=0)
    small = jnp.concatenate([small, jnp.zeros((3, small.shape[1]), F32)], axis=0)
    w_in_full, small_full, c_all = _all_gather([shard["conv_w_in"], small, _pad8(c)], [1, 1, 0], "gather_weights", "gather2")
    late = {n: (shard[n], BIG[n]) for n in big_names if n != "conv_w_in"}
    c16 = jnp.concatenate([c_all[::8], c_ctx[None], jnp.zeros((16 - N_DEV - 1, d), F32)], axis=0)
    ada_w_b = ada_w.astype(BF)
    ada_b_mine = lax.dynamic_slice_in_dim(ada_b, me * wa, wa, axis=1)
    mod_part = _ada_fwd(c16, ada_w_b, ada_b_mine)
    mod_all = _all_gather([mod_part.reshape(32, wa)], [1], "gather_mod")[0].reshape(2, 16, d3)
    mod = jnp.stack([lax.dynamic_index_in_dim(mod_all, me, axis=1, keepdims=False), mod_all[:, N_DEV]], axis=1)
    w = {
        "ln_g": ln_g, "ln_b": ln_b, "conv_w_in": w_in_full, "conv_w": small_full[0:3],
        "ssm_lam_re": ssm_lam_re[0], "ssm_lam_im": ssm_lam_im[0],
        "ssm_log_step": ssm_log_step[0], "ssm_b_re": ssm_b_re[0], "ssm_b_im": ssm_b_im[0],
        "ssm_c_re": ssm_c_re[0], "ssm_c_im": ssm_c_im[0], "ssm_d": small_full[3], "ssm_b_glu": small_full[4],
    }

    loss, grad_x, g, recv_big = _local_step(x[0], c[0], ctx[0], c_ctx, loss_target[0], w, late, True, mod)

    dmod_all = _all_gather([_pad8(g["mod"].reshape(4, d3))], [0], "gather_dmod")[0].reshape(N_DEV, 8, d3)
    dmod_all = dmod_all[:, :4].reshape(N_DEV, 2, 2, d3)
    dm_ctx = dmod_all[0, :, 1]
    for p in range(1, N_DEV):
        dm_ctx = dm_ctx + dmod_all[p, :, 1]
    dm16 = jnp.concatenate([dmod_all[:, :, 0].transpose(1, 0, 2), dm_ctx[:, None], jnp.zeros((2, 16 - N_DEV - 1, d3), F32)], axis=1)
    g_ada_w, dc16 = _ada_bwd(c16, ada_w_b, lax.dynamic_slice_in_dim(dm16, me * wa, wa, axis=2), F32)
    g["c_ctx"] = dc16[0, N_DEV] + dc16[1, N_DEV]
    g_ada_b = jnp.sum(dm16, axis=1)

    blob_names = [n for n in REPLICATED if n != "ada_b"] + SMALL_SHARDED
    flat = jnp.concatenate([g[n].reshape(-1).astype(F32) for n in blob_names] + [loss.reshape(1)])
    nflat = flat.shape[0]
    rows = -(-nflat // (N_DEV * 128 * 8)) * 8
    flat = jnp.concatenate([flat, jnp.zeros((N_DEV * rows * 128 - nflat,), F32)]).reshape(N_DEV * rows, 128)
    last = [n for n in big_names if n not in recv_big]
    recv = _all_to_all([_view2d(n, g[n]) for n in last] + [flat], [BIG[n] for n in last] + [0], "scatter_grads")
    recv_big.update(zip(last, recv[:-1]))
    blob_sum = _sum_partials(recv[-1])
    blob = _all_gather([blob_sum], [0], "gather_small_grads", "gather2")[0].reshape(-1)
    small_g, off = {"ada_b": g_ada_b}, 0
    for n in blob_names:
        shape = wt[n].shape if n in REPLICATED else (*wt[n].shape[:-1], wt[n].shape[-1] * N_DEV)
        size = math.prod(shape)
        small_g[n] = blob[off:off + size].reshape(shape)
        off += size
    loss = blob[off]
    for n in SMALL_SHARDED:
        size = wt[n].shape[-1]
        small_g[n] = lax.dynamic_slice_in_dim(small_g[n], me * size, size, axis=small_g[n].ndim - 1)

    out_g, out_d, out_m, out_v = {}, {}, {}, {}
    recv_big["ada_w"] = _view2d("ada_w", g_ada_w)[None]
    for n in BIG:
        stack = recv_big[n]
        shp = wt[n].shape
        res = _adamw(stack, _view2d(n, wt[n]), _view2d(n, mt[n]), _view2d(n, vt[n]), "adamw_" + n)
        out_g[n], out_d[n], out_m[n], out_v[n] = [r.reshape(shp) for r in res]
    for n in NATIVE_SMALL:
        shp = wt[n].shape
        v2 = lambda a: a.reshape(-1, shp[-1])
        res = _adamw(v2(small_g.pop(n))[None], v2(wt[n]), v2(mt[n]), v2(vt[n]), "adamw_" + n)
        out_g[n], out_d[n], out_m[n], out_v[n] = [r.reshape(shp) for r in res]
    names = list(small_g)
    cat = lambda t: jnp.concatenate([t[n].reshape(-1) for n in names])
    gs, ws, ms, vs = cat(small_g), cat(wt), cat(mt), cat(vt)
    ns = gs.shape[0]
    rs = -(-ns // (128 * 512)) * 512
    padr = lambda a: jnp.concatenate([a, jnp.ones((rs * 128 - ns,), F32)]).reshape(rs, 128)
    res = _adamw(padr(gs)[None], padr(ws), padr(ms), padr(vs), "adamw_small")
    off = 0
    for n in names:
        size = math.prod(wt[n].shape)
        out_g[n], out_d[n], out_m[n], out_v[n] = [r.reshape(-1)[off:off + size].reshape(wt[n].shape) for r in res]
        off += size

    return (loss, grad_x[None], *[out_g[n] for n in WEIGHTS], *[out_d[n] for n in WEIGHTS],
            *[out_m[n] for n in WEIGHTS], *[out_v[n] for n in WEIGHTS])
```

```python
import math

import jax
import jax.numpy as jnp
from jax import lax
from jax.experimental import pallas as pl
from jax.experimental.pallas import tpu as pltpu

F32 = jnp.float32
BF = jnp.bfloat16
MESH = pl.DeviceIdType.MESH
N_DEV = 8

GRID_W = 64
CHUNK = 16
S5_P = 16
S5_N = 64
LN_EPS = 1e-5
DN_ALPHA = 4.0 ** 0.25
ADAM_LR, ADAM_B1, ADAM_B2, ADAM_EPS, ADAM_WD, ADAM_STEP = 1e-3, 0.9, 0.999, 1e-8, 0.01, 10
GELU_C0 = math.sqrt(2.0 / math.pi)
GELU_C1 = 0.044715
VMEM_MB = 52

ANY = pl.BlockSpec(memory_space=pl.ANY)


def _cparams():
    return pltpu.CompilerParams(vmem_limit_bytes=VMEM_MB << 20)


def _dot(a, b):
    return jnp.dot(a, b, preferred_element_type=F32)


def _dot_nt(a, b):
    return lax.dot_general(a, b, (((1,), (1,)), ((), ())), preferred_element_type=F32)


def _dot_tn(a, b):
    return lax.dot_general(a, b, (((0,), (0,)), ((), ())), preferred_element_type=F32)


def _sigmoid(x):
    return 1.0 / (1.0 + jnp.exp(-x))


def _gelu_parts(y):
    u = y * y
    th = jnp.tanh(y * (GELU_C0 + (GELU_C0 * GELU_C1) * u))
    hy = 0.5 * y
    g = hy + hy * th
    dg = (0.5 + 0.5 * th) + hy * (1.0 - th * th) * (GELU_C0 + (3.0 * GELU_C0 * GELU_C1) * u)
    return g, dg


def _full(shape):
    nd = len(shape)
    return pl.BlockSpec(shape, lambda *_: (0,) * nd)


def _mesh_pos():
    x, y, c = lax.axis_index("x"), lax.axis_index("y"), lax.axis_index("c")
    return x, y, c


def _peer(pos, k):
    x, y, c = pos
    px = 1 - x if (k >> 2) & 1 else x
    py = 1 - y if (k >> 1) & 1 else y
    pc = 1 - c if k & 1 else c
    return (px, py, pc), 4 * px + 2 * py + pc


def _shard_at(ref, axis, idx, n):
    if axis == 0:
        return ref.at[pl.ds(idx * n, n)]
    return ref.at[:, pl.ds(idx * n, n)]


class _Exchange:
    def __init__(self, kind, arrays, axes):
        self.kind, self.axes, self.n = kind, list(axes), len(arrays)
        self.arrays = list(arrays)
        self.out_shape = []
        for s, ax in zip(arrays, axes):
            shp = list(s.shape)
            if kind == "scatter":
                shp[ax] //= N_DEV
                self.out_shape.append(jax.ShapeDtypeStruct((N_DEV, *shp), s.dtype))
            else:
                shp[ax] *= N_DEV
                self.out_shape.append(jax.ShapeDtypeStruct(tuple(shp), s.dtype))
        self.scratch = [pltpu.SemaphoreType.DMA((self.n, N_DEV - 1)), pltpu.SemaphoreType.DMA((self.n, N_DEV - 1)),
                        pltpu.SemaphoreType.DMA((self.n,))]

    def _copies(self, ins, outs, sems):
        send_sems, recv_sems, local_sems = sems
        pos = _mesh_pos()
        x, y, c = pos
        me = 4 * x + 2 * y + c
        local, sends, chained, recvs = [], [], [], []
        for i in range(self.n):
            ax = self.axes[i]
            if self.kind == "scatter":
                size = ins[i].shape[ax] // N_DEV
                src = lambda idx, i=i, ax=ax, size=size: _shard_at(ins[i], ax, idx, size)
                dst = lambda idx, i=i: outs[i].at[idx]
            else:
                size = ins[i].shape[ax]
                src = lambda idx, i=i: ins[i]
                dst = lambda idx, i=i, ax=ax, size=size: _shard_at(outs[i], ax, idx, size)

            def copy(k, s, d, to, i=i):
                return pltpu.make_async_remote_copy(src_ref=s, dst_ref=d, send_sem=send_sems.at[i, k],
                                                    recv_sem=recv_sems.at[i, k], device_id=to, device_id_type=MESH)

            local.append(pltpu.make_async_copy(src(me), dst(me), local_sems.at[i]))
            if self.kind == "gather2":
                sib, sib_i = (x, y, 1 - c), 4 * x + 2 * y + (1 - c)
                chips = [(1 - x, y), (x, 1 - y), (1 - x, 1 - y)]
                sends.append(copy(0, src(me), dst(me), sib))
                recvs.append(copy(0, src(me), dst(sib_i), sib))
                for j, (cx, cy) in enumerate(chips):
                    same, other = 4 * cx + 2 * cy + c, 4 * cx + 2 * cy + (1 - c)
                    sends.append(copy(1 + j, src(me), dst(me), (cx, cy, c)))
                    chained.append((copy(1 + j, dst(same), dst(same), (cx, cy, c)), copy(4 + j, dst(same), dst(same), sib)))
                    recvs.append(copy(4 + j, dst(other), dst(other), sib))
            else:
                for k in range(1, N_DEV):
                    peer, pidx = _peer(pos, k)
                    out_src = src(pidx) if self.kind == "scatter" else src(me)
                    sends.append(copy(k - 1, out_src, dst(me), peer))
                    recvs.append(copy(k - 1, out_src, dst(pidx), peer))
        return local, sends, chained, recvs

    def start(self, ins, outs, sems):
        local, sends, _, _ = self._copies(ins, outs, sems)
        for cp in local + sends:
            cp.start()

    def wait(self, ins, outs, sems):
        local, sends, chained, recvs = self._copies(ins, outs, sems)
        for arrival, released in chained:
            arrival.wait_recv()
            released.start()
        for cp in recvs:
            cp.wait_recv()
        for cp in sends + [released for _, released in chained]:
            cp.wait_send()
        for cp in local:
            cp.wait()

    def run(self, name):
        n = self.n

        def body(*refs):
            ins, outs, sems = refs[:n], refs[n:2 * n], refs[2 * n:]
            self.start(ins, outs, sems)
            self.wait(ins, outs, sems)

        return pl.pallas_call(body, name=name, out_shape=self.out_shape, in_specs=[ANY] * n, out_specs=[ANY] * n,
                              scratch_shapes=self.scratch)(*self.arrays)


def _hosted_call(body, xch, grid, in_specs, out_specs, out_shape, scratch, args, name):
    out_specs, out_shape = list(out_specs), list(out_shape)
    n_in, n_out = len(in_specs), len(out_specs)
    if xch is None:
        res = pl.pallas_call(body, name=name, grid=grid, in_specs=in_specs, out_specs=out_specs, out_shape=out_shape,
                             scratch_shapes=list(scratch), compiler_params=_cparams())(*args)
        return list(res), []
    n = xch.n
    rank = len(grid)

    def wrapped(*refs):
        ins, x_ins = refs[:n_in], refs[n_in:n_in + n]
        outs = refs[n_in + n:n_in + n + n_out]
        x_outs = refs[n_in + n + n_out:n_in + 2 * n + n_out]
        rest = refs[n_in + 2 * n + n_out:]
        own, sems = rest[:len(rest) - 3], rest[len(rest) - 3:]
        ids = [pl.program_id(a) for a in range(rank)]
        first, last = ids[0] == 0, ids[0] == grid[0] - 1
        for a in range(1, rank):
            first = jnp.logical_and(first, ids[a] == 0)
            last = jnp.logical_and(last, ids[a] == grid[a] - 1)

        @pl.when(first)
        def _():
            xch.start(x_ins, x_outs, sems)

        body(*ins, *outs, *own)

        @pl.when(last)
        def _():
            xch.wait(x_ins, x_outs, sems)

    res = pl.pallas_call(
        wrapped, name=name, grid=grid, in_specs=list(in_specs) + [ANY] * n, out_specs=out_specs + [ANY] * n,
        out_shape=out_shape + xch.out_shape, scratch_shapes=list(scratch) + xch.scratch, compiler_params=_cparams(),
    )(*args, *xch.arrays)
    return list(res[:n_out]), list(res[n_out:])


def _all_gather(shards, axes, name, kind="gather"):
    return _Exchange(kind, shards, axes).run(name)


def _all_to_all(parts, axes, name):
    return _Exchange("scatter", parts, axes).run(name)


def _ada_fwd(cv, ada_w, ada_b):
    nl, d, wd = ada_w.shape
    r = cv.shape[0]

    def body(c_ref, w_ref, b_ref, o_ref):
        c = c_ref[...]
        s = (c * _sigmoid(c)).astype(BF)
        o_ref[0] = _dot(s, w_ref[0]) + b_ref[0]

    return pl.pallas_call(
        body, name="ada_fwd", grid=(nl,),
        in_specs=[_full((r, d)), pl.BlockSpec((1, d, wd), lambda l: (l, 0, 0)), pl.BlockSpec((1, 1, wd), lambda l: (l, 0, 0))],
        out_specs=pl.BlockSpec((1, r, wd), lambda l: (l, 0, 0)),
        out_shape=jax.ShapeDtypeStruct((nl, r, wd), F32), compiler_params=_cparams(),
    )(cv, ada_w, ada_b.reshape(nl, 1, wd))


def _ada_bwd(cv, ada_w, dm, out_dtype):
    nl, d, wd = ada_w.shape
    r = cv.shape[0]

    def body(c_ref, w_ref, dm_ref, dw_ref, dc_ref):
        c = c_ref[...]
        sg = _sigmoid(c)
        s = (c * sg).astype(BF)
        dmv = dm_ref[0].astype(BF)
        dw_ref[0] = _dot_tn(s, dmv).astype(out_dtype)
        dc_ref[0] = _dot_nt(dmv, w_ref[0]) * (sg * (1.0 + c * (1.0 - sg)))

    return pl.pallas_call(
        body, name="ada_bwd", grid=(nl,),
        in_specs=[_full((r, d)), pl.BlockSpec((1, d, wd), lambda l: (l, 0, 0)), pl.BlockSpec((1, r, wd), lambda l: (l, 0, 0))],
        out_specs=[pl.BlockSpec((1, d, wd), lambda l: (l, 0, 0)), pl.BlockSpec((1, r, d), lambda l: (l, 0, 0))],
        out_shape=[jax.ShapeDtypeStruct((nl, d, wd), out_dtype), jax.ShapeDtypeStruct((nl, r, d), F32)],
        compiler_params=_cparams(),
    )(cv, ada_w, dm)


def _sum_partials(stack):
    _, r, c = stack.shape

    def body(s_ref, o_ref):
        acc = s_ref[0]
        for p in range(1, N_DEV):
            acc = acc + s_ref[p]
        o_ref[...] = acc

    return pl.pallas_call(body, name="sum_partials", out_shape=jax.ShapeDtypeStruct((r, c), F32),
                          in_specs=[_full(stack.shape)], out_specs=_full((r, c)), grid=(1,),
                          compiler_params=_cparams())(stack)


def _adamw(gstack, w, m, v, name):
    p, r, c = gstack.shape
    tr = r
    for cand in (512 if c <= 256 else 256, 128, 64, 32, 16, 8):
        if r % cand == 0 and r > cand:
            tr = cand
            break
    bc1 = 1.0 - ADAM_B1 ** ADAM_STEP
    bc2 = 1.0 - ADAM_B2 ** ADAM_STEP

    def body(g_ref, w_ref, m_ref, v_ref, go_ref, d_ref, mo_ref, vo_ref):
        g = g_ref[0].astype(F32)
        for q in range(1, p):
            g = g + g_ref[q].astype(F32)
        mn = ADAM_B1 * m_ref[...] + (1.0 - ADAM_B1) * g
        vn = ADAM_B2 * v_ref[...] + (1.0 - ADAM_B2) * (g * g)
        go_ref[...] = g
        mo_ref[...] = mn
        vo_ref[...] = vn
        d_ref[...] = -ADAM_LR * ((mn / bc1) / (jnp.sqrt(vn / bc2) + ADAM_EPS) + ADAM_WD * w_ref[...])

    row = pl.BlockSpec((tr, c), lambda i: (i, 0))
    sds = jax.ShapeDtypeStruct((r, c), F32)
    return pl.pallas_call(
        body, name=name, grid=(r // tr,),
        in_specs=[pl.BlockSpec((p, tr, c), lambda i: (0, i, 0)), row, row, row],
        out_specs=[row, row, row, row], out_shape=[sds, sds, sds, sds], compiler_params=_cparams(),
    )(gstack, w, m, v)


def _lat_or_ctx_specs(tm, d, nl, grid_rank, row_axis):
    def lat(*ids):
        return (jnp.minimum(ids[row_axis], nl - 1), 0)

    def ctx(*ids):
        return (jnp.maximum(ids[row_axis] - nl, 0), 0)

    return pl.BlockSpec((tm, d), lat), pl.BlockSpec((tm, d), ctx)


def _sel_row(ref, is_ctx):
    return jnp.where(is_ctx, ref[1:2, :], ref[0:1, :])


def _inproj0(x, ctx, a2, b2, w, tgt, tm, xch=None):
    l, d = x.shape
    nl, nc = l // tm, ctx.shape[0] // tm
    e = w.shape[1] // 4
    half = e // 2
    tjo = tm // CHUNK

    def body(x_ref, c_ref, a_ref, b_ref, w_hbm, t_ref, o_ref, tc_ref, w_ref, ts_ref):
        i = pl.program_id(0)

        @pl.when(i == 0)
        def _():
            pltpu.sync_copy(w_hbm, w_ref)

        is_ctx = i >= nl
        xv = jnp.where(is_ctx, c_ref[...], x_ref[...])
        h = (xv * _sel_row(a_ref, is_ctx) + _sel_row(b_ref, is_ctx)).astype(BF)
        for k in range(4):
            r = _dot(h, w_ref[:, k * e:(k + 1) * e])
            o_ref[k, 0] = r[:, :half].astype(BF)
            o_ref[k, 1] = r[:, half:].astype(BF)

        @pl.when(jnp.logical_not(is_ctx))
        def _():
            for lb in range(d // 128):
                ts_ref[lb] = t_ref[:, lb * 128:(lb + 1) * 128]
            for s in range(CHUNK):
                for lb in range(d // 128):
                    tc_ref[:, s * d + lb * 128:s * d + (lb + 1) * 128] = ts_ref.at[lb][pl.ds(s, tjo, stride=CHUNK), :]

    lat, cx = _lat_or_ctx_specs(tm, d, nl, 1, 0)
    (p42, tgt_cr), extra = _hosted_call(
        body, xch, grid=(nl + nc,),
        in_specs=[lat, cx, _full((2, d)), _full((2, d)), ANY, lat],
        out_specs=[pl.BlockSpec((4, 2, tm, half), lambda i: (0, 0, i, 0)),
                   pl.BlockSpec((tjo, CHUNK * d), lambda i: (jnp.minimum(i, nl - 1), 0))],
        out_shape=[jax.ShapeDtypeStruct((4, 2, l + ctx.shape[0], half), BF),
                   jax.ShapeDtypeStruct((l // CHUNK, CHUNK * d), F32)],
        scratch=[pltpu.VMEM(w.shape, BF), pltpu.VMEM((d // 128, tm, 128), F32)],
        args=(x, ctx, a2, b2, w, tgt), name="l0_inproj")
    return p42, tgt_cr, extra


def _conv_taps(u, w_up, w_mid, w_dn, pos, rl, tm):
    up = jnp.where(pos == 0, 0.0, pltpu.roll(u, 1, 0))
    dn = jnp.where(pos == rl - 1, 0.0, pltpu.roll(u, tm - 1, 0))
    return w_up * up + w_mid * u + w_dn * dn, up, dn


def _conv_halo_specs(tm, tc, nl, lead):
    hb = tm // GRID_W

    def prev(j, i):
        return (0, 1, jnp.maximum(jnp.minimum(i, nl - 1) * hb - 1, 0), j)

    def nxt(j, i):
        return (0, 1, jnp.minimum((jnp.minimum(i, nl - 1) + 1) * hb, nl * hb - 1), j)

    return pl.BlockSpec((lead, 1, GRID_W, tc), prev), pl.BlockSpec((lead, 1, GRID_W, tc), nxt)


def _conv_fwd(p42, cw, nl, tm, tc):
    _, _, r, half = p42.shape
    nt = r // tm

    def body(p_ref, hp_ref, hn_ref, cw_ref, o_ref):
        i = pl.program_id(1)
        is_ctx = i >= nl
        row = lax.broadcasted_iota(jnp.int32, (tm, tc), 0)
        rl = jnp.where(is_ctx, tm, GRID_W)
        pos = jnp.bitwise_and(row, rl - 1)

        def gate(hv, yc):
            bg = p_ref[0, hv].astype(F32)
            z = p_ref[3, hv].astype(F32)
            return (bg * yc * (z * _sigmoid(z))).astype(BF)

        u_h = p_ref[1, 0].astype(F32) * p_ref[2, 0].astype(F32)
        w_h = cw_ref[:, 0, :]
        o_ref[0] = gate(0, _conv_taps(u_h, w_h[0:1], w_h[1:2], w_h[2:3], pos, rl, tm)[0])
        u_v = p_ref[1, 1].astype(F32) * p_ref[2, 1].astype(F32)
        w_v = cw_ref[:, 1, :]

        @pl.when(is_ctx)
        def _():
            o_ref[1] = gate(1, _conv_taps(u_v, w_v[0:1], w_v[1:2], w_v[2:3], pos, rl, tm)[0])

        @pl.when(jnp.logical_not(is_ctx))
        def _():
            up = hp_ref[1, 0].astype(F32) * hp_ref[2, 0].astype(F32) * (i > 0).astype(F32)
            dn = hn_ref[1, 0].astype(F32) * hn_ref[2, 0].astype(F32) * (i < nl - 1).astype(F32)
            ext = jnp.concatenate([up, u_v, dn], axis=0)
            yc = w_v[0:1] * ext[0:tm] + w_v[1:2] * u_v + w_v[2:3] * ext[2 * GRID_W:tm + 2 * GRID_W]
            o_ref[1] = gate(1, yc)

    hp, hn = _conv_halo_specs(tm, tc, nl, 4)
    return pl.pallas_call(
        body, name="l0_conv_fwd", grid=(half // tc, nt),
        in_specs=[pl.BlockSpec((4, 2, tm, tc), lambda j, i: (0, 0, i, j)), hp, hn,
                  pl.BlockSpec((3, 2, tc), lambda j, i: (0, 0, j))],
        out_specs=pl.BlockSpec((2, tm, tc), lambda j, i: (0, i, j)),
        out_shape=jax.ShapeDtypeStruct((2, r, half), BF), compiler_params=_cparams(),
    )(p42, p42, p42, cw)


def _outproj_ln0(q3, w_out, x, ctx, gt2, tm):
    l, d = x.shape
    lc = ctx.shape[0]
    nl, nc = l // tm, lc // tm
    _, r, half = q3.shape
    tjo = tm // CHUNK

    def body(q_ref, w_hbm, x_ref, c_ref, g_ref, xl_ref, xc_ref, rl_ref, rc_ref, fx_ref, w_ref, xs_ref, rs_ref):
        i = pl.program_id(0)

        @pl.when(i == 0)
        def _():
            pltpu.sync_copy(w_hbm, w_ref)

        is_ctx = i >= nl
        fx = _dot(q_ref[0], w_ref[:half, :]) + _dot(q_ref[1], w_ref[half:, :])
        xv = jnp.where(is_ctx, c_ref[...], x_ref[...])
        rr = DN_ALPHA * xv + _sel_row(g_ref, is_ctx) * fx
        mu = jnp.mean(rr, axis=-1, keepdims=True)
        cen = rr - mu
        rstd = lax.rsqrt(jnp.mean(cen * cen, axis=-1, keepdims=True) + LN_EPS)
        xh = cen * rstd
        for lb in range(d // 128):
            xs_ref[lb] = xh[:, lb * 128:(lb + 1) * 128]
        rs_ref[...] = jnp.broadcast_to(rstd, (tm, 128))
        fx_ref[...] = fx.astype(BF)

        def to_cr(xo_ref, ro_ref):
            for s in range(CHUNK):
                for lb in range(d // 128):
                    xo_ref[:, s * d + lb * 128:s * d + (lb + 1) * 128] = xs_ref.at[lb][pl.ds(s, tjo, stride=CHUNK), :]
                ro_ref[:, s * 128:(s + 1) * 128] = rs_ref[pl.ds(s, tjo, stride=CHUNK), :]

        @pl.when(jnp.logical_not(is_ctx))
        def _():
            to_cr(xl_ref, rl_ref)

        @pl.when(is_ctx)
        def _():
            to_cr(xc_ref, rc_ref)

    lat, cx = _lat_or_ctx_specs(tm, d, nl, 1, 0)
    lat_o = lambda w_: pl.BlockSpec((tjo, CHUNK * w_), lambda i: (jnp.minimum(i, nl - 1), 0))
    ctx_o = lambda w_: pl.BlockSpec((tjo, CHUNK * w_), lambda i: (jnp.maximum(i - nl, 0), 0))
    return pl.pallas_call(
        body, name="l0_outproj_ln", grid=(nl + nc,),
        in_specs=[pl.BlockSpec((2, tm, half), lambda i: (0, i, 0)), ANY, lat, cx, _full((2, d))],
        out_specs=[lat_o(d), ctx_o(d), lat_o(128), ctx_o(128), pl.BlockSpec((tm, d), lambda i: (i, 0))],
        out_shape=[jax.ShapeDtypeStruct((l // CHUNK, CHUNK * d), F32), jax.ShapeDtypeStruct((lc // CHUNK, CHUNK * d), F32),
                   jax.ShapeDtypeStruct((l // CHUNK, CHUNK * 128), F32), jax.ShapeDtypeStruct((lc // CHUNK, CHUNK * 128), F32),
                   jax.ShapeDtypeStruct((r, d), BF)],
        scratch_shapes=[pltpu.VMEM(w_out.shape, BF), pltpu.VMEM((d // 128, tm, 128), F32), pltpu.VMEM((tm, 128), F32)],
        compiler_params=_cparams(),
    )(q3, w_out, x, ctx, gt2)


def _bwd_outproj0(dr_l, dr_c, gt2, w_out, fx, tm):
    l, d = dr_l.shape
    nl, nc = l // tm, dr_c.shape[0] // tm
    e = w_out.shape[0]
    half = e // 2
    r = l + dr_c.shape[0]

    def body(dl_ref, dc_ref, g_ref, w_hbm, fx_ref, dq_ref, acc_ref, w_ref):
        i = pl.program_id(0)

        @pl.when(i == 0)
        def _():
            pltpu.sync_copy(w_hbm, w_ref)
            acc_ref[...] = jnp.zeros_like(acc_ref)

        is_ctx = i >= nl
        dr = jnp.where(is_ctx, dc_ref[...], dl_ref[...]).astype(F32)
        dfx = (dr * _sel_row(g_ref, is_ctx)).astype(BF)
        dq_ref[0] = _dot_nt(dfx, w_ref[:half, :]).astype(BF)
        dq_ref[1] = _dot_nt(dfx, w_ref[half:, :]).astype(BF)
        s = jnp.sum(dr * fx_ref[...].astype(F32), axis=0, keepdims=True)
        sel = is_ctx.astype(F32)
        acc_ref[0:1, :] += s * (1.0 - sel)
        acc_ref[1:2, :] += s * sel

    lat, cx = _lat_or_ctx_specs(tm, d, nl, 1, 0)
    return pl.pallas_call(
        body, name="l0_bwd_outproj", grid=(nl + nc,),
        in_specs=[lat, cx, _full((2, d)), ANY, pl.BlockSpec((tm, d), lambda i: (i, 0))],
        out_specs=[pl.BlockSpec((2, tm, half), lambda i: (0, i, 0)), _full((8, d))],
        out_shape=[jax.ShapeDtypeStruct((2, r, half), BF), jax.ShapeDtypeStruct((8, d), F32)],
        scratch_shapes=[pltpu.VMEM(w_out.shape, BF)], compiler_params=_cparams(),
    )(dr_l, dr_c, gt2, w_out, fx)


def _conv_bwd_inproj0(dq3, p42, cw, w_in, x, ctx, dr_l, dr_c, a2, nl, tm, xch=None):
    l, d = x.shape
    _, _, r, half = p42.shape
    nt = r // tm
    e = 2 * half
    cc = min(512, half)
    n_cc = half // cc

    def body(dq_ref, dqp_ref, dqn_ref, p_ref, hp_ref, hn_ref, cw_ref, w_hbm, x_ref, c_ref, dl_ref, dc_ref, a_ref,
             dp_ref, dw_ref, gx_ref, acc_ref, w_ref, dh_ref):
        i, hv = pl.program_id(0), pl.program_id(1)
        is_ctx = i >= nl

        @pl.when(jnp.logical_and(i == 0, hv == 0))
        def _():
            pltpu.sync_copy(w_hbm, w_ref)
            acc_ref[...] = jnp.zeros_like(acc_ref)
            dw_ref[...] = jnp.zeros_like(dw_ref)

        row = lax.broadcasted_iota(jnp.int32, (tm, cc), 0)
        rl = jnp.where(is_ctx, tm, GRID_W)
        pos = jnp.bitwise_and(row, rl - 1)

        def pieces(dq, bg, z):
            sz = _sigmoid(z)
            sil = z * sz
            return dq * bg * sil, dq * sil, dq * bg * (sz * (1.0 + z * (1.0 - sz)))

        def emit(hvs, c, parts, dyc, u_up, u, u_dn, dh):
            lanes = slice(c * cc, (c + 1) * cc)
            for k, part in enumerate(parts):
                pb = part.astype(BF)
                dp_ref[k, 0, :, lanes] = pb
                c0 = k * e + hvs * half + c * cc
                t = _dot_nt(pb, w_ref[:, c0:c0 + cc])
                dh = t if dh is None else dh + t
            dw_ref[0:1, hvs, lanes] += jnp.sum(dyc * u_up, axis=0, keepdims=True)
            dw_ref[1:2, hvs, lanes] += jnp.sum(dyc * u, axis=0, keepdims=True)
            dw_ref[2:3, hvs, lanes] += jnp.sum(dyc * u_dn, axis=0, keepdims=True)
            return dh

        def seq_half(hvs):
            dh = None
            for c in range(n_cc):
                lanes = slice(c * cc, (c + 1) * cc)
                bg, cg = p_ref[0, 0, :, lanes].astype(F32), p_ref[1, 0, :, lanes].astype(F32)
                v, z = p_ref[2, 0, :, lanes].astype(F32), p_ref[3, 0, :, lanes].astype(F32)
                w = cw_ref[:, hvs, lanes]
                u = cg * v
                yc, u_up, u_dn = _conv_taps(u, w[0:1], w[1:2], w[2:3], pos, rl, tm)
                dyc, dbg_f, dz_f = pieces(dq_ref[0, :, lanes].astype(F32), bg, z)
                du = _conv_taps(dyc, w[2:3], w[1:2], w[0:1], pos, rl, tm)[0]
                dh = emit(hvs, c, (dbg_f * yc, du * v, du * cg, dz_f * yc), dyc, u_up, u, u_dn, dh)
            return dh

        def col_half():
            m_up = (i > 0).astype(F32)
            m_dn = (i < nl - 1).astype(F32)
            dh = None
            for c in range(n_cc):
                lanes = slice(c * cc, (c + 1) * cc)
                bg, cg = p_ref[0, 0, :, lanes].astype(F32), p_ref[1, 0, :, lanes].astype(F32)
                v, z = p_ref[2, 0, :, lanes].astype(F32), p_ref[3, 0, :, lanes].astype(F32)
                w = cw_ref[:, 1, lanes]
                u = cg * v

                def halo(h_ref, dqh_ref, msk):
                    hb, hc = h_ref[0, 0, :, lanes].astype(F32), h_ref[1, 0, :, lanes].astype(F32)
                    hv_, hz = h_ref[2, 0, :, lanes].astype(F32), h_ref[3, 0, :, lanes].astype(F32)
                    return hc * hv_ * msk, pieces(dqh_ref[0, :, lanes].astype(F32), hb, hz)[0] * msk

                u_p, dyc_p = halo(hp_ref, dqp_ref, m_up)
                u_n, dyc_n = halo(hn_ref, dqn_ref, m_dn)
                u_ext = jnp.concatenate([u_p, u, u_n], axis=0)
                u_up, u_dn = u_ext[0:tm], u_ext[2 * GRID_W:tm + 2 * GRID_W]
                yc = w[0:1] * u_up + w[1:2] * u + w[2:3] * u_dn
                dyc, dbg_f, dz_f = pieces(dq_ref[0, :, lanes].astype(F32), bg, z)
                d_ext = jnp.concatenate([dyc_p, dyc, dyc_n], axis=0)
                du = w[0:1] * d_ext[2 * GRID_W:tm + 2 * GRID_W] + w[1:2] * dyc + w[2:3] * d_ext[0:tm]
                dh = emit(1, c, (dbg_f * yc, du * v, du * cg, dz_f * yc), dyc, u_up, u, u_dn, dh)
            return dh

        @pl.when(hv == 0)
        def _():
            dh_ref[...] = seq_half(0)

        @pl.when(jnp.logical_and(hv == 1, is_ctx))
        def _():
            dh_ref[...] += seq_half(1)

        @pl.when(jnp.logical_and(hv == 1, jnp.logical_not(is_ctx)))
        def _():
            dh_ref[...] += col_half()

        @pl.when(hv == 1)
        def _():
            dh = dh_ref[...]
            xv = jnp.where(is_ctx, c_ref[...], x_ref[...])
            s_sc = jnp.sum(dh * xv, axis=0, keepdims=True)
            s_sh = jnp.sum(dh, axis=0, keepdims=True)
            sel = is_ctx.astype(F32)
            acc_ref[0:1, :] += s_sc * (1.0 - sel)
            acc_ref[1:2, :] += s_sc * sel
            acc_ref[2:3, :] += s_sh * (1.0 - sel)
            acc_ref[3:4, :] += s_sh * sel

        @pl.when(jnp.logical_and(hv == 1, jnp.logical_not(is_ctx)))
        def _():
            gx_ref[...] = DN_ALPHA * dl_ref[...].astype(F32) + dh_ref[...] * a_ref[0:1, :]

    hb = tm // GRID_W
    prev_blk = lambda i: jnp.maximum(jnp.minimum(i, nl - 1) * hb - 1, 0)
    next_blk = lambda i: jnp.minimum((jnp.minimum(i, nl - 1) + 1) * hb, nl * hb - 1)
    lat, cx = _lat_or_ctx_specs(tm, d, nl, 2, 0)
    (dp42, dcw, gx, acc), extra = _hosted_call(
        body, xch, grid=(nt, 2),
        in_specs=[pl.BlockSpec((1, tm, half), lambda i, h: (h, i, 0)),
                  pl.BlockSpec((1, GRID_W, half), lambda i, h: (1, prev_blk(i), 0)),
                  pl.BlockSpec((1, GRID_W, half), lambda i, h: (1, next_blk(i), 0)),
                  pl.BlockSpec((4, 1, tm, half), lambda i, h: (0, h, i, 0)),
                  pl.BlockSpec((4, 1, GRID_W, half), lambda i, h: (0, 1, prev_blk(i), 0)),
                  pl.BlockSpec((4, 1, GRID_W, half), lambda i, h: (0, 1, next_blk(i), 0)),
                  _full((3, 2, half)), ANY, lat, cx, lat, cx, _full((2, d))],
        out_specs=[pl.BlockSpec((4, 1, tm, half), lambda i, h: (0, h, i, 0)), _full((8, 2, half)),
                   pl.BlockSpec((tm, d), lambda i, h: (jnp.minimum(i, nl - 1), 0)), _full((8, d))],
        out_shape=[jax.ShapeDtypeStruct(p42.shape, BF), jax.ShapeDtypeStruct((8, 2, half), F32),
                   jax.ShapeDtypeStruct((l, d), F32), jax.ShapeDtypeStruct((8, d), F32)],
        scratch=[pltpu.VMEM(w_in.shape, BF), pltpu.VMEM((tm, d), F32)],
        args=(dq3, dq3, dq3, p42, p42, p42, cw, w_in, x, ctx, dr_l, dr_c, a2), name="l0_conv_bwd_inproj")
    return dp42, dcw, gx, acc, extra


def _dw_inproj0(x, ctx, a2, b2, dp42, tm):
    l, d = x.shape
    lc = ctx.shape[0]
    assert lc == tm
    tl = 4 * tm if l % (4 * tm) == 0 else tm
    nl = l // tl
    half = dp42.shape[-1]
    e = 2 * half

    def body(x_ref, c_ref, a_ref, b_ref, dpl_ref, dpc_ref, o_ref, acc_ref):
        i = pl.program_id(1)

        @pl.when(i == 0)
        def _():
            acc_ref[...] = jnp.zeros_like(acc_ref)

        def add(rows_ref, dp_ref, sel):
            h = (rows_ref[...] * a_ref[sel:sel + 1, :] + b_ref[sel:sel + 1, :]).astype(BF)
            acc_ref[:, :half] += _dot_tn(h, dp_ref[0, 0])
            acc_ref[:, half:] += _dot_tn(h, dp_ref[0, 1])

        @pl.when(i < nl)
        def _():
            add(x_ref, dpl_ref, 0)

        @pl.when(i == nl)
        def _():
            add(c_ref, dpc_ref, 1)
            o_ref[...] = acc_ref[...].astype(BF)

    return pl.pallas_call(
        body, name="l0_dw_inproj", grid=(4, nl + 1),
        in_specs=[pl.BlockSpec((tl, d), lambda k, i: (jnp.minimum(i, nl - 1), 0)), _full((lc, d)),
                  _full((2, d)), _full((2, d)),
                  pl.BlockSpec((1, 2, tl, half), lambda k, i: (k, 0, jnp.minimum(i, nl - 1), 0)),
                  pl.BlockSpec((1, 2, lc, half), lambda k, i: (k, 0, l // lc, 0))],
        out_specs=pl.BlockSpec((d, e), lambda k, i: (0, k)),
        out_shape=jax.ShapeDtypeStruct((d, 4 * e), BF),
        scratch_shapes=[pltpu.VMEM((d, e), F32)], compiler_params=_cparams(),
    )(x, ctx, a2, b2, dp42, dp42)


def _dw_outproj0(q3, dr_l, dr_c, gt2, tm, xch=None):
    l, d = dr_l.shape
    nl, nc = l // tm, dr_c.shape[0] // tm
    _, r, half = q3.shape
    nt = nl + nc

    def body(q_ref, dl_ref, dc_ref, g_ref, o_ref, acc_ref):
        i = pl.program_id(0)
        is_ctx = i >= nl

        @pl.when(i == 0)
        def _():
            acc_ref[...] = jnp.zeros_like(acc_ref)

        dr = jnp.where(is_ctx, dc_ref[...], dl_ref[...]).astype(F32)
        dfx = (dr * _sel_row(g_ref, is_ctx)).astype(BF)
        acc_ref[:half, :] += _dot_tn(q_ref[0], dfx)
        acc_ref[half:, :] += _dot_tn(q_ref[1], dfx)

        @pl.when(i == nt - 1)
        def _():
            o_ref[...] = acc_ref[...].astype(BF)

    lat, cx = _lat_or_ctx_specs(tm, d, nl, 1, 0)
    (g_w,), extra = _hosted_call(
        body, xch, grid=(nt,),
        in_specs=[pl.BlockSpec((2, tm, half), lambda i: (0, i, 0)), lat, cx, _full((2, d))],
        out_specs=[_full((2 * half, d))], out_shape=[jax.ShapeDtypeStruct((2 * half, d), BF)],
        scratch=[pltpu.VMEM((2 * half, d), F32)], args=(q3, dr_l, dr_c, gt2), name="l0_dw_outproj")
    return g_w, extra


def _cr_tile(j, cap=256):
    for cand in (1024, 512, 256, 128, 64, 32, 16, 8):
        if cand <= cap and j % cand == 0:
            return cand
    raise ValueError(j)


def _final(w_cr, w_out, xh_cr, tgt_cr, vecs):
    j, e16 = w_cr.shape
    e = e16 // CHUNK
    d = w_out.shape[1]
    tj = _cr_tile(j)

    def body(w_ref, wo_hbm, xh_ref, t_ref, v_ref, dr_ref, acc_ref, wo_ref):
        @pl.when(jnp.logical_and(pl.program_id(0) == 0, pl.program_id(1) == 0))
        def _():
            pltpu.sync_copy(wo_hbm, wo_ref)
            acc_ref[...] = jnp.zeros_like(acc_ref)

        o = _dot(w_ref[...], wo_ref[...])
        x1 = xh_ref[...] * v_ref[0:1, :] + v_ref[1:2, :]
        rr = DN_ALPHA * x1 + v_ref[2:3, :] * o
        mu = jnp.mean(rr, axis=-1, keepdims=True)
        cen = rr - mu
        rstd = lax.rsqrt(jnp.mean(cen * cen, axis=-1, keepdims=True) + LN_EPS)
        xh2 = cen * rstd
        err = xh2 * v_ref[3:4, :] + v_ref[4:5, :] - t_ref[...]
        dy = err * (1.0 / d)
        dxh = dy * v_ref[3:4, :]
        dr = rstd * (dxh - jnp.mean(dxh, axis=-1, keepdims=True) - xh2 * jnp.mean(dxh * xh2, axis=-1, keepdims=True))
        dr_ref[...] = dr.astype(BF)
        acc_ref[0:1, :] += jnp.sum(dy * xh2, axis=0, keepdims=True)
        acc_ref[1:2, :] += jnp.sum(dy, axis=0, keepdims=True)
        acc_ref[2:3, :] += jnp.sum(dr * o, axis=0, keepdims=True)
        acc_ref[3:4, :] += (0.5 / d) * jnp.sum(err * err, axis=0, keepdims=True)

    tok_d = pl.BlockSpec((tj, d), lambda t, s: (t, s))
    return pl.pallas_call(
        body, name="l1_final", grid=(j // tj, CHUNK),
        in_specs=[pl.BlockSpec((tj, e), lambda t, s: (t, s)), ANY, tok_d, tok_d, _full((8, d))],
        out_specs=[tok_d, _full((8, d))],
        out_shape=[jax.ShapeDtypeStruct((j, CHUNK * d), BF), jax.ShapeDtypeStruct((8, d), F32)],
        scratch_shapes=[pltpu.VMEM(w_out.shape, BF)], compiler_params=_cparams(),
    )(w_cr, w_out, xh_cr, tgt_cr, vecs)


def _dw_cr(lhs, rhs, lhs_kind, rhs_kind, vec, bias_sum, init, name):
    j = lhs.shape[0]
    k = lhs.shape[1] // CHUNK
    n = rhs.shape[1] // CHUNK
    tj = _cr_tile(j, 512)
    nh = 2 if k * n * 4 > (8 << 20) else 1
    tn = n // nh
    nt = j // tj
    has_init = init is not None

    def body(*refs):
        refs = list(refs)
        l_ref, r_ref = refs[0], refs[1]
        pos = 2
        v_ref = None
        if vec is not None:
            v_ref = refs[pos]
            pos += 1
        i_ref = None
        if has_init:
            i_ref = refs[pos]
            pos += 1
        o_ref = refs[pos]
        pos += 1
        bs_ref = None
        if bias_sum:
            bs_ref = refs[pos]
            pos += 1
        acc_ref = refs[pos]
        t, s = pl.program_id(1), pl.program_id(2)
        first = jnp.logical_and(t == 0, s == 0)

        @pl.when(first)
        def _():
            acc_ref[...] = i_ref[...] if has_init else jnp.zeros_like(acc_ref)
            if bias_sum:
                bs_ref[...] = jnp.zeros_like(bs_ref)

        if lhs_kind == "mod":
            lv = (l_ref[...] * v_ref[0:1, :] + v_ref[1:2, :]).astype(BF)
        else:
            lv = l_ref[...]
        if rhs_kind == "scaled":
            rv = (r_ref[...].astype(F32) * v_ref[0:1, :]).astype(BF)
        else:
            rv = r_ref[...]
        acc_ref[...] += _dot_tn(lv, rv)
        if bias_sum:
            bs_ref[0:1, :] += jnp.sum(rv.astype(F32), axis=0, keepdims=True)

        @pl.when(jnp.logical_and(t == nt - 1, s == CHUNK - 1))
        def _():
            o_ref[...] = acc_ref[...].astype(BF)

    l_spec = pl.BlockSpec((tj, k), lambda h, t, s: (t, s))
    r_spec = pl.BlockSpec((tj, tn), lambda h, t, s: (t, s * nh + h))
    in_specs, args = [l_spec, r_spec], [lhs, rhs]
    if vec is not None:
        in_specs.append(_full(vec.shape))
        args.append(vec)
    o_spec = pl.BlockSpec((k, tn), lambda h, t, s: (0, h))
    if has_init:
        in_specs.append(o_spec)
        args.append(init)
    out_specs, out_shape = [o_spec], [jax.ShapeDtypeStruct((k, n), BF)]
    if bias_sum:
        out_specs.append(pl.BlockSpec((8, tn), lambda h, t, s: (0, h)))
        out_shape.append(jax.ShapeDtypeStruct((8, n), F32))
    res = pl.pallas_call(
        body, name=name, grid=(nh, nt, CHUNK), in_specs=in_specs, out_specs=out_specs, out_shape=out_shape,
        scratch_shapes=[pltpu.VMEM((k, tn), F32)], compiler_params=_cparams(),
    )(*args)
    return res if bias_sum else res[0]


GT_ROWS = CHUNK * S5_P
ZG_W = 2 * 2 * S5_N
PAIR_W = 2 * ZG_W
GROUPS_PER_STEP = 4


def _inproj1_gt(xh_cr, a1, b1, wu_t, w_z, tag):
    j, d16 = xh_cr.shape
    d = d16 // CHUNK
    e = wu_t.shape[0]
    g = e // S5_P
    tj = _cr_tile(j, 256)

    def body(x_ref, a_ref, b_ref, wu_hbm, wz_hbm, u_ref, z_ref, wu_ref, wz_ref):
        @pl.when(jnp.logical_and(pl.program_id(0) == 0, pl.program_id(1) == 0))
        def _():
            pltpu.sync_copy(wu_hbm, wu_ref)
            pltpu.sync_copy(wz_hbm, wz_ref)

        h = (x_ref[...] * a_ref[...] + b_ref[...]).astype(BF)
        u_ref[...] = _dot_nt(wu_ref[...], h).reshape(g, S5_P, tj).astype(BF)
        z_ref[...] = _dot(h, wz_ref[...]).astype(BF)

    return pl.pallas_call(
        body, name="l1_inproj_" + tag, grid=(j // tj, CHUNK),
        in_specs=[pl.BlockSpec((tj, d), lambda t, s: (t, s)), _full((1, d)), _full((1, d)), ANY, ANY],
        out_specs=[pl.BlockSpec((g, S5_P, tj), lambda t, s: (0, s, t)), pl.BlockSpec((tj, e), lambda t, s: (t, s))],
        out_shape=[jax.ShapeDtypeStruct((g, GT_ROWS, j), BF), jax.ShapeDtypeStruct((j, CHUNK * e), BF)],
        scratch_shapes=[pltpu.VMEM(wu_t.shape, BF), pltpu.VMEM(w_z.shape, BF)], compiler_params=_cparams(),
    )(xh_cr, a1, b1, wu_t, w_z)


def _gt_spec(j, gb=GROUPS_PER_STEP):
    return pl.BlockSpec((gb, GT_ROWS, j), lambda i: (i, 0, 0))


def _zg_spec(j, gb=GROUPS_PER_STEP):
    return pl.BlockSpec((j, gb * ZG_W), lambda i: (0, i))


def _w_spec(width, gb=GROUPS_PER_STEP):
    return pl.BlockSpec((gb, GT_ROWS, width), lambda i: (i, 0, 0))


def _pair_lanes(k):
    return slice((k // 2) * PAIR_W, (k // 2 + 1) * PAIR_W)


def _s5_z(ut_l, ut_c, bc):
    g, _, jl = ut_l.shape
    jc = ut_c.shape[2]
    gb = GROUPS_PER_STEP

    def body(ul_ref, uc_ref, bc_ref, zl_ref, zc_ref):
        for k in range(0, gb, 2):
            zl_ref[:, _pair_lanes(k)] = _dot_tn(ul_ref[k], bc_ref[k]) + _dot_tn(ul_ref[k + 1], bc_ref[k + 1])
            zc_ref[:, _pair_lanes(k)] = _dot_tn(uc_ref[k], bc_ref[k]) + _dot_tn(uc_ref[k + 1], bc_ref[k + 1])

    return pl.pallas_call(
        body, name="l1_s5_z", grid=(g // gb,), in_specs=[_gt_spec(jl), _gt_spec(jc), _w_spec(PAIR_W)],
        out_specs=[_zg_spec(jl), _zg_spec(jc)],
        out_shape=[jax.ShapeDtypeStruct((jl, g * ZG_W), F32), jax.ShapeDtypeStruct((jc, g * ZG_W), F32)],
        compiler_params=_cparams(),
    )(ut_l, ut_c, bc)


def _s5_y(ut_l, s_l, mt_t, cct):
    g, _, jl = ut_l.shape
    gb = GROUPS_PER_STEP

    def body(u_ref, s_ref, mt_ref, cc_ref, y_ref):
        for k in range(gb):
            s_k = s_ref[:, _pair_lanes(k)].astype(BF)
            y_ref[k] = (_dot(mt_ref[k], u_ref[k]) + _dot_nt(cc_ref[k], s_k)).astype(BF)

    return pl.pallas_call(
        body, name="l1_s5_y", grid=(g // gb,),
        in_specs=[_gt_spec(jl), _zg_spec(jl), _w_spec(GT_ROWS), _w_spec(PAIR_W)],
        out_specs=_gt_spec(jl), out_shape=jax.ShapeDtypeStruct((g, GT_ROWS, jl), BF), compiler_params=_cparams(),
    )(ut_l, s_l, mt_t, cct)


def _s5_ds(dyt_l, cct):
    g, _, jl = dyt_l.shape
    gb = GROUPS_PER_STEP

    def body(dy_ref, cc_ref, ds_ref):
        for k in range(0, gb, 2):
            ds_ref[:, _pair_lanes(k)] = _dot_tn(dy_ref[k], cc_ref[k]) + _dot_tn(dy_ref[k + 1], cc_ref[k + 1])

    return pl.pallas_call(
        body, name="l1_s5_ds", grid=(g // gb,), in_specs=[_gt_spec(jl), _w_spec(PAIR_W)], out_specs=_zg_spec(jl),
        out_shape=jax.ShapeDtypeStruct((jl, g * ZG_W), F32), compiler_params=_cparams(),
    )(dyt_l, cct)


def _s5_dx(dyt_l, dz_l, dz_c, mt, bc):
    g, _, jl = dyt_l.shape
    jc = dz_c.shape[0]
    gb = GROUPS_PER_STEP

    def body(dy_ref, dzl_ref, dzc_ref, mt_ref, bc_ref, dul_ref, duc_ref):
        for k in range(gb):
            dzl = dzl_ref[:, _pair_lanes(k)].astype(BF)
            dzc = dzc_ref[:, _pair_lanes(k)].astype(BF)
            dul_ref[k] = (_dot(mt_ref[k], dy_ref[k]) + _dot_nt(bc_ref[k], dzl)).astype(BF)
            duc_ref[k] = _dot_nt(bc_ref[k], dzc).astype(BF)

    return pl.pallas_call(
        body, name="l1_s5_dx", grid=(g // gb,),
        in_specs=[_gt_spec(jl), _zg_spec(jl), _zg_spec(jc), _w_spec(GT_ROWS), _w_spec(PAIR_W)],
        out_specs=[_gt_spec(jl), _gt_spec(jc)],
        out_shape=[jax.ShapeDtypeStruct((g, GT_ROWS, jl), BF), jax.ShapeDtypeStruct((g, GT_ROWS, jc), BF)],
        compiler_params=_cparams(),
    )(dyt_l, dz_l, dz_c, mt, bc)


def _s5_dw(ut_l, ut_c, dyt_l, dz_l, dz_c, s_l):
    g, _, jl = ut_l.shape
    jc = ut_c.shape[2]
    gb = GROUPS_PER_STEP

    def body(ul_ref, uc_ref, dy_ref, dzl_ref, dzc_ref, s_ref, dmt_ref, dbc_ref, dcc_ref):
        for k in range(gb):
            lanes = _pair_lanes(k)
            dmt_ref[k] = _dot_nt(ul_ref[k], dy_ref[k])
            dbc_ref[k] = (_dot(ul_ref[k], dzl_ref[:, lanes].astype(BF))
                          + _dot(uc_ref[k], dzc_ref[:, lanes].astype(BF)))
            dcc_ref[k] = _dot(dy_ref[k], s_ref[:, lanes].astype(BF))

    sd_m = jax.ShapeDtypeStruct((g, GT_ROWS, GT_ROWS), F32)
    sd_p = jax.ShapeDtypeStruct((g, GT_ROWS, PAIR_W), F32)
    return pl.pallas_call(
        body, name="l1_s5_dw", grid=(g // gb,),
        in_specs=[_gt_spec(jl), _gt_spec(jc), _gt_spec(jl), _zg_spec(jl), _zg_spec(jc), _zg_spec(jl)],
        out_specs=[_w_spec(GT_ROWS), _w_spec(PAIR_W), _w_spec(PAIR_W)], out_shape=[sd_m, sd_p, sd_p],
        compiler_params=_cparams(),
    )(ut_l, ut_c, dyt_l, dz_l, dz_c, s_l)


def _scan_g(z_l, z_c, coef, chains, conj, s_l=None, s_c=None, name="l1_scan"):
    jl, w_all = z_l.shape
    jc = z_c.shape[0]
    gb = 2 * GROUPS_PER_STEP if w_all % (2 * GROUPS_PER_STEP * ZG_W) == 0 else GROUPS_PER_STEP
    wb = gb * ZG_W
    nch = wb // 256
    with_da = s_l is not None
    sign = -1.0 if conj else 1.0

    def body(*refs):
        zl_ref, zc_ref, cf_ref = refs[:3]
        k0 = 3
        if with_da:
            sl_ref, sc_ref = refs[3:5]
            k0 = 5
        ol_ref, oc_ref = refs[k0:k0 + 2]
        rowi = lax.broadcasted_iota(jnp.int32, (8, 128), 0)

        def lanes_of(ch):
            return slice(ch * 256, ch * 256 + 128), slice(ch * 256 + 128, (ch + 1) * 256)

        def coefs(ch, r0, nr):
            lr, li = lanes_of(ch)
            return cf_ref[r0:r0 + nr, lr], sign * cf_ref[r0:r0 + nr, li]

        def shift(v, sh, rev):
            if rev:
                return jnp.where(rowi < 8 - sh, pltpu.roll(v, 8 - sh, 0), 0.0)
            return jnp.where(rowi >= sh, pltpu.roll(v, sh, 0), 0.0)

        zero_row = jnp.zeros((1, 128), F32)
        zero_tile = jnp.zeros((8, 128), F32)
        carry = [zero_row] * (2 * nch)
        da = [zero_tile] * (2 * nch)
        for seg in range(len(chains[0])):
            which = chains[0][seg][0]
            assert chains[1][seg][0] == which
            revs = (chains[0][seg][1], chains[1][seg][1])
            src, dst = (zc_ref, oc_ref) if which == "c" else (zl_ref, ol_ref)
            sref = ((sc_ref if which == "c" else sl_ref) if with_da else None)
            ng = (jc if which == "c" else jl) // 8

            def step(it, st, src=src, dst=dst, sref=sref, ng=ng, revs=revs):
                carry_, da_ = list(st[:2 * nch]), list(st[2 * nch:])
                for ch in range(nch):
                    rev = revs[ch % 2]
                    lr, li = lanes_of(ch)
                    grp = (ng - 1 - it) if rev else it
                    off = pl.multiple_of(grp * 8, 8)
                    xr, xi = src[pl.ds(off, 8), lr], src[pl.ds(off, 8), li]
                    for sh, r0 in ((1, 0), (2, 1), (4, 2)):
                        ar, ai = coefs(ch, r0, 1)
                        sr, si = shift(xr, sh, rev), shift(xi, sh, rev)
                        xr, xi = xr + ar * sr - ai * si, xi + ar * si + ai * sr
                    tr, ti = coefs(ch, 16, 8) if rev else coefs(ch, 8, 8)
                    cr_, ci_ = carry_[2 * ch], carry_[2 * ch + 1]
                    ir = xr + tr * cr_ - ti * ci_
                    ii = xi + tr * ci_ + ti * cr_
                    if rev:
                        er = jnp.where(rowi == 7, cr_, pltpu.roll(ir, 7, 0))
                        ei = jnp.where(rowi == 7, ci_, pltpu.roll(ii, 7, 0))
                        carry_[2 * ch], carry_[2 * ch + 1] = ir[0:1], ii[0:1]
                    else:
                        er = jnp.where(rowi == 0, cr_, pltpu.roll(ir, 1, 0))
                        ei = jnp.where(rowi == 0, ci_, pltpu.roll(ii, 1, 0))
                        carry_[2 * ch], carry_[2 * ch + 1] = ir[7:8], ii[7:8]
                    dst[pl.ds(off, 8), lr] = er
                    dst[pl.ds(off, 8), li] = ei
                    if sref is not None:
                        s_r, s_i = sref[pl.ds(off, 8), lr], sref[pl.ds(off, 8), li]
                        da_[2 * ch] = da_[2 * ch] + s_r * er + s_i * ei
                        da_[2 * ch + 1] = da_[2 * ch + 1] + s_r * ei - s_i * er
                return (*carry_, *da_)

            st = lax.fori_loop(0, ng, step, (*carry, *da))
            carry, da = list(st[:2 * nch]), list(st[2 * nch:])
        if with_da:
            da_ref = refs[k0 + 2]
            for ch in range(nch):
                lr, li = lanes_of(ch)
                da_ref[:, lr] = da[2 * ch]
                da_ref[:, li] = da[2 * ch + 1]

    in_specs = [_zg_spec(jl, gb), _zg_spec(jc, gb), pl.BlockSpec((24, wb), lambda i: (0, i))]
    args = [z_l, z_c, coef]
    out_specs = [_zg_spec(jl, gb), _zg_spec(jc, gb)]
    out_shape = [jax.ShapeDtypeStruct(z_l.shape, F32), jax.ShapeDtypeStruct(z_c.shape, F32)]
    if with_da:
        in_specs += [_zg_spec(jl, gb), _zg_spec(jc, gb)]
        args += [s_l, s_c]
        out_specs.append(pl.BlockSpec((8, wb), lambda i: (0, i)))
        out_shape.append(jax.ShapeDtypeStruct((8, w_all), F32))
    return pl.pallas_call(body, name=name, grid=(w_all // wb,), in_specs=in_specs, out_specs=out_specs,
                          out_shape=out_shape, compiler_params=_cparams())(*args)


def _gt_tok_spec(g, tj):
    return pl.BlockSpec((g, S5_P, tj), lambda t, s: (0, s, t))


def _glu_fwd_gt(yt, z_cr, w_glu, b_glu):
    g, _, j = yt.shape
    e = g * S5_P
    tj = _cr_tile(j)

    def body(y_ref, z_ref, w_hbm, b_ref, o_ref, sg_ref, w_ref):
        @pl.when(jnp.logical_and(pl.program_id(0) == 0, pl.program_id(1) == 0))
        def _():
            pltpu.sync_copy(w_hbm, w_ref)

        y = jnp.transpose(y_ref[...].reshape(e, tj).astype(F32))
        gl = _gelu_parts(y)[0]
        sg = _sigmoid(_dot(gl.astype(BF), w_ref[...]) + b_ref[...])
        z = z_ref[...].astype(F32)
        o_ref[...] = (gl * sg * (z * _sigmoid(z))).astype(BF)
        sg_ref[...] = sg.astype(BF)

    tok = pl.BlockSpec((tj, e), lambda t, s: (t, s))
    return pl.pallas_call(
        body, name="l1_glu_fwd", grid=(j // tj, CHUNK),
        in_specs=[_gt_tok_spec(g, tj), tok, ANY, _full((1, e))], out_specs=[tok, tok],
        out_shape=[jax.ShapeDtypeStruct((j, CHUNK * e), BF), jax.ShapeDtypeStruct((j, CHUNK * e), BF)],
        scratch_shapes=[pltpu.VMEM(w_glu.shape, BF)], compiler_params=_cparams(),
    )(yt, z_cr, w_glu, b_glu)


def _glu_bwd_gt(dr_cr, gt1, w_out, w_glu, yt, z_cr, sg_cr):
    g, _, j = yt.shape
    e, d = w_out.shape
    tj = _cr_tile(j)

    def body(dr_ref, g_ref, wo_hbm, wg_hbm, y_ref, z_ref, sg_ref, dz_ref, dt_ref, dy_ref, wo_ref, wg_ref):
        @pl.when(jnp.logical_and(pl.program_id(0) == 0, pl.program_id(1) == 0))
        def _():
            pltpu.sync_copy(wo_hbm, wo_ref)
            pltpu.sync_copy(wg_hbm, wg_ref)

        do = (dr_ref[...].astype(F32) * g_ref[...]).astype(BF)
        dw = _dot_nt(do, wo_ref[...])
        y = jnp.transpose(y_ref[...].reshape(e, tj).astype(F32))
        gl, dgel = _gelu_parts(y)
        z = z_ref[...].astype(F32)
        sz = _sigmoid(z)
        sg = sg_ref[...].astype(F32)
        dg2 = dw * (z * sz)
        dz_ref[...] = (dw * gl * sg * (sz * (1.0 + z * (1.0 - sz)))).astype(BF)
        dt = (dg2 * gl * sg * (1.0 - sg)).astype(BF)
        dt_ref[...] = dt
        dy = (dg2 * sg + _dot_nt(dt, wg_ref[...])) * dgel
        dy_ref[...] = jnp.transpose(dy).reshape(g, S5_P, tj).astype(BF)

    tok_e = pl.BlockSpec((tj, e), lambda t, s: (t, s))
    return pl.pallas_call(
        body, name="l1_glu_bwd", grid=(j // tj, CHUNK),
        in_specs=[pl.BlockSpec((tj, d), lambda t, s: (t, s)), _full((1, d)), ANY, ANY, _gt_tok_spec(g, tj), tok_e, tok_e],
        out_specs=[tok_e, tok_e, _gt_tok_spec(g, tj)],
        out_shape=[jax.ShapeDtypeStruct((j, CHUNK * e), BF), jax.ShapeDtypeStruct((j, CHUNK * e), BF),
                   jax.ShapeDtypeStruct((g, GT_ROWS, j), BF)],
        scratch_shapes=[pltpu.VMEM(w_out.shape, BF), pltpu.VMEM(w_glu.shape, BF)], compiler_params=_cparams(),
    )(dr_cr, gt1, w_out, w_glu, yt, z_cr, sg_cr)


def _bwd_inproj1_gt(dut, dz_cr, wu_t, w_z, xh_cr, rs_cr, dr2_cr, vecs, tag):
    g, _, j = dut.shape
    e, d = wu_t.shape
    tj = _cr_tile(j)

    def body(du_ref, dz_ref, wu_hbm, wz_hbm, xh_ref, rs_ref, dr2_ref, v_ref, dr1_ref, acc_ref, wu_ref, wz_ref):
        @pl.when(jnp.logical_and(pl.program_id(0) == 0, pl.program_id(1) == 0))
        def _():
            pltpu.sync_copy(wu_hbm, wu_ref)
            pltpu.sync_copy(wz_hbm, wz_ref)
            acc_ref[...] = jnp.zeros_like(acc_ref)

        dh = _dot_tn(du_ref[...].reshape(e, tj), wu_ref[...]) + _dot_nt(dz_ref[...], wz_ref[...])
        xh = xh_ref[...]
        x1 = xh * v_ref[0:1, :] + v_ref[1:2, :]
        dx1 = DN_ALPHA * dr2_ref[...].astype(F32) + dh * v_ref[2:3, :]
        dxh = dx1 * v_ref[0:1, :]
        rstd = rs_ref[:, 0:1]
        dr1 = rstd * (dxh - jnp.mean(dxh, axis=-1, keepdims=True) - xh * jnp.mean(dxh * xh, axis=-1, keepdims=True))
        dr1_ref[...] = dr1.astype(BF)
        acc_ref[0:1, :] += jnp.sum(dh * x1, axis=0, keepdims=True)
        acc_ref[1:2, :] += jnp.sum(dh, axis=0, keepdims=True)
        acc_ref[2:3, :] += jnp.sum(dx1 * xh, axis=0, keepdims=True)
        acc_ref[3:4, :] += jnp.sum(dx1, axis=0, keepdims=True)

    tok_d = pl.BlockSpec((tj, d), lambda t, s: (t, s))
    return pl.pallas_call(
        body, name="l1_bwd_inproj_" + tag, grid=(j // tj, CHUNK),
        in_specs=[_gt_tok_spec(g, tj), pl.BlockSpec((tj, e), lambda t, s: (t, s)), ANY, ANY, tok_d,
                  pl.BlockSpec((tj, 128), lambda t, s: (t, s)), tok_d, _full((8, d))],
        out_specs=[tok_d, _full((8, d))],
        out_shape=[jax.ShapeDtypeStruct((j, CHUNK * d), BF), jax.ShapeDtypeStruct((8, d), F32)],
        scratch_shapes=[pltpu.VMEM(wu_t.shape, BF), pltpu.VMEM(w_z.shape, BF)], compiler_params=_cparams(),
    )(dut, dz_cr, wu_t, w_z, xh_cr, rs_cr, dr2_cr, vecs)


def _dw_gt(lhs_gt, rhs_cr, lhs_gelu, vec, bias_sum, init, out_dtype, name, xch=None):
    g, _, j = lhs_gt.shape
    e = g * S5_P
    n = rhs_cr.shape[1] // CHUNK
    tj = _cr_tile(j, 512 if j % 512 == 0 else 256)
    nh = 2 if e * n * 4 > (8 << 20) else 1
    tn = n // nh
    nt = j // tj
    has_init = init is not None

    def body(*refs):
        refs = list(refs)
        l_ref, r_ref = refs[0], refs[1]
        pos = 2
        v_ref = i_ref = bs_ref = None
        if vec is not None:
            v_ref = refs[pos]
            pos += 1
        if has_init:
            i_ref = refs[pos]
            pos += 1
        o_ref = refs[pos]
        pos += 1
        if bias_sum:
            bs_ref = refs[pos]
            pos += 1
        acc_ref = refs[pos]
        t, s = pl.program_id(1), pl.program_id(2)

        @pl.when(jnp.logical_and(t == 0, s == 0))
        def _():
            acc_ref[...] = i_ref[...] if has_init else jnp.zeros_like(acc_ref)
            if bias_sum:
                bs_ref[...] = jnp.zeros_like(bs_ref)

        lv = l_ref[...].reshape(e, tj)
        if lhs_gelu:
            lv = _gelu_parts(lv.astype(F32))[0].astype(BF)
        if vec is not None:
            rv = (r_ref[...] * v_ref[0:1, :] + v_ref[1:2, :]).astype(BF)
        else:
            rv = r_ref[...]
        acc_ref[...] += _dot(lv, rv)
        if bias_sum:
            bs_ref[0:1, :] += jnp.sum(rv.astype(F32), axis=0, keepdims=True)

        @pl.when(jnp.logical_and(t == nt - 1, s == CHUNK - 1))
        def _():
            o_ref[...] = acc_ref[...].astype(out_dtype)

    in_specs = [pl.BlockSpec((g, S5_P, tj), lambda h, t, s: (0, s, t)),
                pl.BlockSpec((tj, tn), lambda h, t, s: (t, s * nh + h))]
    args = [lhs_gt, rhs_cr]
    if vec is not None:
        in_specs.append(_full(vec.shape))
        args.append(vec)
    o_spec = pl.BlockSpec((e, tn), lambda h, t, s: (0, h))
    if has_init:
        in_specs.append(o_spec)
        args.append(init)
    out_specs, out_shape = [o_spec], [jax.ShapeDtypeStruct((e, n), out_dtype)]
    if bias_sum:
        out_specs.append(pl.BlockSpec((8, tn), lambda h, t, s: (0, h)))
        out_shape.append(jax.ShapeDtypeStruct((8, n), F32))
    res, extra = _hosted_call(body, xch, grid=(nh, nt, CHUNK), in_specs=in_specs, out_specs=out_specs,
                              out_shape=out_shape, scratch=[pltpu.VMEM((e, tn), F32)], args=args, name=name)
    if xch is not None:
        return (*res, extra) if bias_sum else (res[0], extra)
    return res if bias_sum else res[0]


def _scan_coef_g(lam_re, lam_im, log_step):
    g = lam_re.shape[1]
    ms = jnp.array([1, 2, 4, 0, 0, 0, 0, 0] + list(range(1, 9)) + list(range(8, 0, -1)), F32) * CHUNK
    dt = jnp.exp(log_step)[..., None]
    mag = jnp.exp(ms.reshape(-1, 1, 1, 1) * (lam_re * dt)[None])
    ang = ms.reshape(-1, 1, 1, 1) * (lam_im * dt)[None]
    cr, ci = mag * jnp.cos(ang), mag * jnp.sin(ang)
    both = jnp.stack([cr, ci], axis=2).reshape(24, 2, 2, g // 2, 2, S5_N)
    return both.transpose(0, 3, 1, 2, 4, 5).reshape(24, g * ZG_W)


def _s5_small(lam_re, lam_im, log_step, b_re, b_im, c_re, c_im, d_skip):
    g = lam_re.shape[1]
    t, p = CHUNK, S5_P
    dt = jnp.exp(log_step)[..., None]
    ks = jnp.arange(t + 1, dtype=F32).reshape(t + 1, 1, 1, 1)
    mag = jnp.exp(ks * (lam_re * dt)[None])
    ang = ks * (lam_im * dt)[None]
    pr, pi = mag * jnp.cos(ang), mag * jnp.sin(ang)
    ar, ai = pr[1], pi[1]
    qr, qi = ar - 1.0, ai
    den = lam_re * lam_re + lam_im * lam_im
    fr = (qr * lam_re + qi * lam_im) / den
    fi = (qi * lam_re - qr * lam_im) / den
    bt_re, bt_im = b_re.transpose(0, 1, 3, 2), b_im.transpose(0, 1, 3, 2)
    bbr = fr[:, :, None, :] * bt_re - fi[:, :, None, :] * bt_im
    bbi = fr[:, :, None, :] * bt_im + fi[:, :, None, :] * bt_re
    lay = lambda a_r, a_i: jnp.stack([a_r, a_i], axis=0).transpose(3, 2, 0, 1, 4)
    by_dir = lambda a, f0, f1: jnp.stack([f0(a[:, 0]), f1(a[:, 1])], axis=1)
    rev = lambda a: jnp.flip(a, axis=0)
    same = lambda a: a
    pwb = lay(by_dir(pr[:t], rev, same), by_dir(pi[:t], rev, same))
    pwc = lay(by_dir(pr[1:], same, rev), by_dir(pi[1:], same, rev))
    bb = jnp.stack([bbr, bbi], axis=0).transpose(2, 1, 0, 3, 4)
    cc = jnp.stack([c_re, c_im], axis=0).transpose(2, 1, 0, 3, 4)
    dmat = jnp.eye(p, dtype=F32)[None] * d_skip.reshape(g, p)[:, :, None]
    return pwb, pwc, bb, cc, dmat, pr[t], pi[t]


def _pair_cols(r, ri, g2):
    c0 = (r * 2 + ri) * 128 + g2 * S5_N
    return slice(c0, c0 + S5_N)


def _rows_rep(a):
    return jnp.broadcast_to(a[:, None, :], (CHUNK, S5_P, a.shape[-1])).reshape(GT_ROWS, a.shape[-1])


def _rows_tile(a):
    return jnp.broadcast_to(a[None], (CHUNK, S5_P, a.shape[-1])).reshape(GT_ROWS, a.shape[-1])


def _sum_blocks(a):
    return jnp.sum(a.reshape(CHUNK, S5_P, a.shape[-1]), axis=0)


def _sum_in_blocks(a):
    return jnp.sum(a.reshape(CHUNK, S5_P, a.shape[-1]), axis=1)


def _ab_rows(pwb_ref, bb_ref, k, r):
    prs, pis = _rows_rep(pwb_ref[k, r, 0]), _rows_rep(pwb_ref[k, r, 1])
    bbr, bbi = _rows_tile(bb_ref[k, r, 0]), _rows_tile(bb_ref[k, r, 1])
    return prs * bbr - pis * bbi, prs * bbi + pis * bbr, prs, pis, bbr, bbi


def _s5_weights_fwd(pwb, pwc, bb, cc, dmat):
    g = pwb.shape[0]
    gb = GROUPS_PER_STEP
    hp = lax.Precision.HIGHEST

    def body(pwb_ref, pwc_ref, bb_ref, cc_ref, dm_ref, mt_ref, mtt_ref, bc_ref, cct_ref):
        zeros = jnp.zeros((GT_ROWS, S5_N), BF)
        nt = (((1,), (1,)), ((), ()))
        for k in range(gb):
            g2 = k % 2
            kds = []
            for r in range(2):
                for ri in range(2):
                    bc_ref[k, :, _pair_cols(r, ri, 1 - g2)] = zeros
                    cct_ref[k, :, _pair_cols(r, ri, 1 - g2)] = zeros
                abr, abi = _ab_rows(pwb_ref, bb_ref, k, r)[:2]
                bc_ref[k, :, _pair_cols(r, 0, g2)] = abr.astype(BF)
                bc_ref[k, :, _pair_cols(r, 1, g2)] = abi.astype(BF)
                cr, ci = cc_ref[k, r, 0], cc_ref[k, r, 1]
                crt, cit = _rows_tile(cr), _rows_tile(ci)
                prt, pit = _rows_rep(pwc_ref[k, r, 0]), _rows_rep(pwc_ref[k, r, 1])
                cct_ref[k, :, _pair_cols(r, 0, g2)] = (crt * prt - cit * pit).astype(BF)
                cct_ref[k, :, _pair_cols(r, 1, g2)] = (-(crt * pit + cit * prt)).astype(BF)
                kds.append(lax.dot_general(abr, cr, nt, precision=hp, preferred_element_type=F32)
                           - lax.dot_general(abi, ci, nt, precision=hp, preferred_element_type=F32))
            blk = lambda a, s: a[s * S5_P:(s + 1) * S5_P]
            last = CHUNK - 1
            pieces = [blk(kds[1], last - i) for i in range(last)]
            pieces.append(blk(kds[0], last) + blk(kds[1], 0) + dm_ref[k])
            pieces += [blk(kds[0], last - d) for d in range(1, CHUNK)]
            qrow = jnp.concatenate(pieces, axis=1)
            mt = jnp.concatenate([qrow[:, (last - s) * S5_P:(last - s) * S5_P + GT_ROWS] for s in range(CHUNK)], axis=0)
            mt_ref[k] = mt.astype(BF)
            mtt_ref[k] = jnp.transpose(mt).astype(BF)

    small = lambda a: pl.BlockSpec((gb, *a.shape[1:]), lambda i: (i,) + (0,) * (a.ndim - 1))
    return pl.pallas_call(
        body, name="l1_s5_weights", grid=(g // gb,),
        in_specs=[small(pwb), small(pwc), small(bb), small(cc), small(dmat)],
        out_specs=[_w_spec(GT_ROWS), _w_spec(GT_ROWS), _w_spec(PAIR_W), _w_spec(PAIR_W)],
        out_shape=[jax.ShapeDtypeStruct((g, GT_ROWS, GT_ROWS), BF), jax.ShapeDtypeStruct((g, GT_ROWS, GT_ROWS), BF),
                   jax.ShapeDtypeStruct((g, GT_ROWS, PAIR_W), BF), jax.ShapeDtypeStruct((g, GT_ROWS, PAIR_W), BF)],
        compiler_params=_cparams(),
    )(pwb, pwc, bb, cc, dmat)


def _s5_weights_bwd(pwb, pwc, bb, cc, d_mt, d_bc, d_cct):
    g = pwb.shape[0]
    gb = GROUPS_PER_STEP
    hp = lax.Precision.HIGHEST

    def body(pwb_ref, pwc_ref, bb_ref, cc_ref, dmt_ref, dbc_ref, dcc_ref, dpwb_ref, dpwc_ref, dbb_ref, dccp_ref, ddm_ref):
        tn = (((0,), (0,)), ((), ()))
        nn = (((1,), (0,)), ((), ()))
        last = CHUNK - 1
        for k in range(gb):
            g2 = k % 2
            dq = None
            for s in range(CHUNK):
                parts = [dmt_ref[k, s * S5_P:(s + 1) * S5_P, :]]
                if s < last:
                    parts.insert(0, jnp.zeros((S5_P, (last - s) * S5_P), F32))
                if s > 0:
                    parts.append(jnp.zeros((S5_P, s * S5_P), F32))
                padded = jnp.concatenate(parts, axis=1) if len(parts) > 1 else parts[0]
                dq = padded if dq is None else dq + padded
            dblk = lambda d: dq[:, (last + d) * S5_P:(CHUNK + d) * S5_P]
            ddm_ref[k] = dblk(0)
            dkds = [jnp.concatenate([dblk(last - s) for s in range(CHUNK)], axis=0),
                    jnp.concatenate([dblk(-s) for s in range(CHUNK)], axis=0)]
            for r in range(2):
                abr, abi, prs, pis, bbr, bbi = _ab_rows(pwb_ref, bb_ref, k, r)
                cr, ci = cc_ref[k, r, 0], cc_ref[k, r, 1]
                dcr = lax.dot_general(dkds[r], abr, tn, precision=hp, preferred_element_type=F32)
                dci = -lax.dot_general(dkds[r], abi, tn, precision=hp, preferred_element_type=F32)
                dabr = (lax.dot_general(dkds[r], cr, nn, precision=hp, preferred_element_type=F32)
                        + dbc_ref[k, :, _pair_cols(r, 0, g2)])
                dabi = (-lax.dot_general(dkds[r], ci, nn, precision=hp, preferred_element_type=F32)
                        + dbc_ref[k, :, _pair_cols(r, 1, g2)])
                dbb_ref[k, r, 0] = _sum_blocks(prs * dabr + pis * dabi)
                dbb_ref[k, r, 1] = _sum_blocks(prs * dabi - pis * dabr)
                dpwb_ref[k, r, 0] = _sum_in_blocks(dabr * bbr + dabi * bbi)
                dpwb_ref[k, r, 1] = _sum_in_blocks(dabi * bbr - dabr * bbi)
                crt, cit = _rows_tile(cr), _rows_tile(ci)
                prt, pit = _rows_rep(pwc_ref[k, r, 0]), _rows_rep(pwc_ref[k, r, 1])
                d_re = dcc_ref[k, :, _pair_cols(r, 0, g2)]
                d_im = dcc_ref[k, :, _pair_cols(r, 1, g2)]
                dccp_ref[k, r, 0] = dcr + _sum_blocks(d_re * prt - d_im * pit)
                dccp_ref[k, r, 1] = dci - _sum_blocks(d_re * pit + d_im * prt)
                dpwc_ref[k, r, 0] = _sum_in_blocks(d_re * crt - d_im * cit)
                dpwc_ref[k, r, 1] = -_sum_in_blocks(d_re * cit + d_im * crt)

    small = lambda a: pl.BlockSpec((gb, *a.shape[1:]), lambda i: (i,) + (0,) * (a.ndim - 1))
    dmat_sds = jax.ShapeDtypeStruct((g, S5_P, S5_P), F32)
    return pl.pallas_call(
        body, name="l1_s5_weights_bwd", grid=(g // gb,),
        in_specs=[small(pwb), small(pwc), small(bb), small(cc), _w_spec(GT_ROWS), _w_spec(PAIR_W), _w_spec(PAIR_W)],
        out_specs=[small(pwb), small(pwc), small(bb), small(cc), small(dmat_sds)],
        out_shape=[jax.ShapeDtypeStruct(pwb.shape, F32), jax.ShapeDtypeStruct(pwc.shape, F32),
                   jax.ShapeDtypeStruct(bb.shape, F32), jax.ShapeDtypeStruct(cc.shape, F32), dmat_sds],
        compiler_params=_cparams(),
    )(pwb, pwc, bb, cc, d_mt, d_bc, d_cct)


def _from_cr(a, c):
    return a.reshape(a.shape[0] * CHUNK, c)


def _pad8(v):
    return jnp.concatenate([v, jnp.zeros((8 - v.shape[0], v.shape[1]), v.dtype)], axis=0)


def _local_step(x, c, ctx, c_ctx, loss_target, w, late=None, scatter=False, mod=None):
    l, d = x.shape
    lc = ctx.shape[0]
    tm = min(256, lc)
    assert lc == tm and l % tm == 0 and tm % GRID_W == 0 and (tm & (tm - 1)) == 0
    nl = l // tm

    own_mod = mod is None
    if own_mod:
        c8 = _pad8(jnp.stack([c, c_ctx]))
        mod = _ada_fwd(c8, w["ada_w"], w["ada_b"])
    sh = mod[:, :2, :d]
    sc = mod[:, :2, d:2 * d]
    gt = mod[:, :2, 2 * d:]
    ln_g, ln_b = w["ln_g"], w["ln_b"]

    a0, b0 = 1.0 + sc[0], sh[0]
    xch = _Exchange("gather2", [late[n][0] for n in late], [late[n][1] for n in late]) if late else None
    p42, tgt_cr, got = _inproj0(x, ctx, a0, b0, w["conv_w_in"], loss_target, tm, xch)
    if late:
        w = dict(w, **dict(zip(late, got)))
    e = w["conv_w_out"].shape[0]
    half = e // 2
    cw = w["conv_w"].reshape(3, 2, half)
    q3 = _conv_fwd(p42, cw, nl, tm, half)
    xh1_l, xh1_c, rs1_l, rs1_c, fx = _outproj_ln0(q3, w["conv_w_out"], x, ctx, gt[0], tm)
    jl, jc = l // CHUNK, lc // CHUNK

    g0, bb0 = ln_g[0:1], ln_b[0:1]
    a1 = g0 * (1.0 + sc[1])
    b1 = bb0 * (1.0 + sc[1]) + sh[1]
    wu_t = w["ssm_w_in"][:, :e].T
    w_z = w["ssm_w_in"][:, e:]
    ut_l, z_l = _inproj1_gt(xh1_l, a1[0:1], b1[0:1], wu_t, w_z, "lat")
    ut_c, _ = _inproj1_gt(xh1_c, a1[1:2], b1[1:2], wu_t, w_z, "ctx")
    s5 = (w["ssm_lam_re"], w["ssm_lam_im"], w["ssm_log_step"], w["ssm_b_re"], w["ssm_b_im"],
          w["ssm_c_re"], w["ssm_c_im"], w["ssm_d"])
    (pwb, pwc, bbw, ccw, dmat, _, _), s5_vjp = jax.vjp(_s5_small, *s5)
    mt_b, mtt_b, bc_b, cct_b = _s5_weights_fwd(pwb, pwc, bbw, ccw, dmat)
    coef = lax.stop_gradient(_scan_coef_g(*s5[:3]))
    zz_l, zz_c = _s5_z(ut_l, ut_c, bc_b)
    fwd_chains = ((("c", False), ("l", False)), (("c", True), ("l", True)))
    st_l, st_c = _scan_g(zz_l, zz_c, coef, fwd_chains, False, name="l1_scan_fwd")
    yt = _s5_y(ut_l, st_l, mtt_b, cct_b)
    b_glu = w["ssm_b_glu"].reshape(1, e)
    w_cr, sg_cr = _glu_fwd_gt(yt, z_l, w["ssm_w_glu"], b_glu)
    vec_f = _pad8(jnp.concatenate([g0, bb0, gt[1][0:1], ln_g[1:2], ln_b[1:2]], axis=0))
    dr2, acc_f = _final(w_cr, w["ssm_w_out"], xh1_l, tgt_cr, vec_f)
    loss = jnp.sum(acc_f[3])

    gt1 = gt[1][0:1]
    dz_l, dt_l, dyt = _glu_bwd_gt(dr2, gt1, w["ssm_w_out"], w["ssm_w_glu"], yt, z_l, sg_cr)
    g_w_out = _dw_cr(w_cr, dr2, "cr", "scaled", gt1, False, None, "l1_dw_out")
    ds_l = _s5_ds(dyt, cct_b)
    bwd_chains = ((("l", True), ("c", True)), (("l", False), ("c", False)))
    dzz_l, dzz_c, da = _scan_g(ds_l, jnp.zeros_like(zz_c), coef, bwd_chains, True, st_l, st_c, name="l1_scan_bwd")
    dut_l, dut_c = _s5_dx(dyt, dzz_l, dzz_c, mt_b, bc_b)
    d_mt, d_bc, d_cct = _s5_dw(ut_l, ut_c, dyt, dzz_l, dzz_c, st_l)
    n_g = e // S5_P
    da = jnp.sum(da, axis=0).reshape(n_g // 2, 2, 2, 2, S5_N).transpose(1, 2, 0, 3, 4)
    da = da.reshape(2, 2, n_g, S5_N)
    d_pwb, d_pwc, d_bb, d_ccp, d_dm = _s5_weights_bwd(pwb, pwc, bbw, ccw, d_mt, d_bc, d_cct)
    g_s5 = s5_vjp((d_pwb, d_pwc, d_bb, d_ccp, d_dm, da[:, 0], da[:, 1]))

    vec_l = _pad8(jnp.concatenate([g0, bb0, 1.0 + sc[1][0:1]], axis=0))
    vec_c = _pad8(jnp.concatenate([g0, bb0, 1.0 + sc[1][1:2]], axis=0))
    dr1_l, acc_l = _bwd_inproj1_gt(dut_l, dz_l, wu_t, w_z, xh1_l, rs1_l, dr2, vec_l, "lat")
    dr1_c, acc_c = _bwd_inproj1_gt(dut_c, jnp.zeros((jc, CHUNK * e), BF), wu_t, w_z, xh1_c, rs1_c,
                                   jnp.zeros((jc, CHUNK * d), BF), vec_c, "ctx")
    mod_l = jnp.concatenate([a1[0:1], b1[0:1]], axis=0)
    mod_c = jnp.concatenate([a1[1:2], b1[1:2]], axis=0)
    g_ut_c = _dw_gt(dut_c, xh1_c, False, mod_c, False, None, F32, "l1_dw_in_u_ctx")
    g_ut = _dw_gt(dut_l, xh1_l, False, mod_l, False, g_ut_c, BF, "l1_dw_in_u")
    g_in_z = _dw_cr(xh1_l, dz_l, "mod", "cr", mod_l, False, None, "l1_dw_in_z")
    g_w_in1 = jnp.concatenate([g_ut.T, g_in_z], axis=1)

    dr1_ln, dr1_cn = _from_cr(dr1_l, d), _from_cr(dr1_c, d)
    dq3, acc_g0 = _bwd_outproj0(dr1_ln, dr1_cn, gt[0], w["conv_w_out"], fx, tm)
    def carried(names, parts):
        return _Exchange("scatter", parts, [BIG[n] for n in names]) if scatter else None

    dp42, dcw, grad_x, acc_0, recv1 = _conv_bwd_inproj0(
        dq3, p42, cw, w["conv_w_in"], x, ctx, dr1_ln, dr1_cn, a0, nl, tm, carried(["ssm_w_in", "ssm_w_out"], [g_w_in1, g_w_out]))
    g_w_in0 = _dw_inproj0(x, ctx, a0, b0, dp42, tm)
    res = _dw_gt(yt, dt_l, True, None, True, None, BF, "l1_dw_glu", carried(["conv_w_in"], [g_w_in0]))
    g_w_glu, bsum, recv2 = res if scatter else (*res, [])
    g_b_glu = bsum[0]
    g_w_out0, recv3 = _dw_outproj0(q3, dr1_ln, dr1_cn, gt[0], tm, carried(["ssm_w_glu"], [g_w_glu]))
    recv = dict(zip(["ssm_w_in", "ssm_w_out", "conv_w_in", "ssm_w_glu"], recv1 + recv2 + recv3))

    zero = jnp.zeros((d,), F32)
    dm0 = jnp.stack([jnp.concatenate([acc_0[2], acc_0[0], acc_g0[0]]), jnp.concatenate([acc_0[3], acc_0[1], acc_g0[1]])])
    dm1 = jnp.stack([jnp.concatenate([acc_l[1], acc_l[0], acc_f[2]]), jnp.concatenate([acc_c[1], acc_c[0], zero])])
    if own_mod:
        g_ada_w, dc8 = _ada_bwd(c8, w["ada_w"], jnp.stack([_pad8(dm0), _pad8(dm1)]), BF)
        g_mod = {"c_ctx": dc8[0, 1] + dc8[1, 1], "ada_w": g_ada_w,
                 "ada_b": jnp.stack([dm0[0] + dm0[1], dm1[0] + dm1[1]])}
    else:
        g_mod = {"mod": jnp.stack([dm0, dm1])}

    grads = {
        **g_mod,
        "ln_g": jnp.stack([acc_l[2] + acc_c[2], acc_f[0]]),
        "ln_b": jnp.stack([acc_l[3] + acc_c[3], acc_f[1]]),
        "conv_w_in": g_w_in0, "conv_w": dcw[:3].reshape(3, e), "conv_w_out": g_w_out0,
        "ssm_w_in": g_w_in1,
        "ssm_lam_re": g_s5[0], "ssm_lam_im": g_s5[1], "ssm_log_step": g_s5[2],
        "ssm_b_re": g_s5[3], "ssm_b_im": g_s5[4], "ssm_c_re": g_s5[5], "ssm_c_im": g_s5[6], "ssm_d": g_s5[7],
        "ssm_w_glu": g_w_glu, "ssm_b_glu": g_b_glu, "ssm_w_out": g_w_out,
    }
    for n in recv:
        del grads[n]
    return loss, grad_x, grads, recv


WEIGHTS = ["c_ctx", "ada_w", "ada_b", "ln_g", "ln_b", "conv_w_in", "conv_w", "conv_w_out", "ssm_w_in",
           "ssm_lam_re", "ssm_lam_im", "ssm_log_step", "ssm_b_re", "ssm_b_im", "ssm_c_re", "ssm_c_im",
           "ssm_d", "ssm_w_glu", "ssm_b_glu", "ssm_w_out"]
BIG = {"ada_w": 1, "conv_w_in": 1, "conv_w_out": 0, "ssm_w_in": 1, "ssm_w_glu": 0, "ssm_w_out": 0}
SMALL_SHARDED = ["conv_w", "ssm_d", "ssm_b_glu"]
REPLICATED = ["c_ctx", "ada_b", "ln_g", "ln_b", "ssm_lam_re", "ssm_lam_im", "ssm_log_step",
              "ssm_b_re", "ssm_b_im", "ssm_c_re", "ssm_c_im"]
NATIVE_SMALL = ["ssm_b_re", "ssm_b_im", "ssm_c_re", "ssm_c_im"]


def _view2d(name, a):
    return a.reshape(-1, a.shape[-1])


def kernel(x, c, ctx, c_ctx, ada_w, ada_b, ln_g, ln_b, conv_w_in, conv_w, conv_w_out, ssm_w_in, ssm_lam_re, ssm_lam_im, ssm_log_step, ssm_b_re, ssm_b_im, ssm_c_re, ssm_c_im, ssm_d, ssm_w_glu, ssm_b_glu, ssm_w_out, loss_target, m_c_ctx, m_ada_w, m_ada_b, m_ln_g, m_ln_b, m_conv_w_in, m_conv_w, m_conv_w_out, m_ssm_w_in, m_ssm_lam_re, m_ssm_lam_im, m_ssm_log_step, m_ssm_b_re, m_ssm_b_im, m_ssm_c_re, m_ssm_c_im, m_ssm_d, m_ssm_w_glu, m_ssm_b_glu, m_ssm_w_out, v_c_ctx, v_ada_w, v_ada_b, v_ln_g, v_ln_b, v_conv_w_in, v_conv_w, v_conv_w_out, v_ssm_w_in, v_ssm_lam_re, v_ssm_lam_im, v_ssm_log_step, v_ssm_b_re, v_ssm_b_im, v_ssm_c_re, v_ssm_c_im, v_ssm_d, v_ssm_w_glu, v_ssm_b_glu, v_ssm_w_out):
    args = locals()
    wt = {n: args[n] for n in WEIGHTS}
    mt = {n: args["m_" + n] for n in WEIGHTS}
    vt = {n: args["v_" + n] for n in WEIGHTS}

    me = 4 * lax.axis_index("x") + 2 * lax.axis_index("y") + lax.axis_index("c")
    d = x.shape[-1]
    d3 = 3 * d
    wa = d3 // N_DEV

    big_names = [n for n in BIG if n != "ada_w"]
    shard = {n: _view2d(n, wt[n]).astype(BF) for n in big_names}
    small = jnp.concatenate([wt["conv_w"][0], wt["ssm_d"], wt["ssm_b_glu"]], axis=0)
    small = jnp.concatenate([small, jnp.zeros((3, small.shape[1]), F32)], axis=0)
    w_in_full, small_full, c_all = _all_gather([shard["conv_w_in"], small, _pad8(c)], [1, 1, 0], "gather_weights", "gather2")
    late = {n: (shard[n], BIG[n]) for n in big_names if n != "conv_w_in"}
    c16 = jnp.concatenate([c_all[::8], c_ctx[None], jnp.zeros((16 - N_DEV - 1, d), F32)], axis=0)
    ada_w_b = ada_w.astype(BF)
    ada_b_mine = lax.dynamic_slice_in_dim(ada_b, me * wa, wa, axis=1)
    mod_part = _ada_fwd(c16, ada_w_b, ada_b_mine)
    mod_all = _all_gather([mod_part.reshape(32, wa)], [1], "gather_mod")[0].reshape(2, 16, d3)
    mod = jnp.stack([lax.dynamic_index_in_dim(mod_all, me, axis=1, keepdims=False), mod_all[:, N_DEV]], axis=1)
    w = {
        "ln_g": ln_g, "ln_b": ln_b, "conv_w_in": w_in_full, "conv_w": small_full[0:3],
        "ssm_lam_re": ssm_lam_re[0], "ssm_lam_im": ssm_lam_im[0],
        "ssm_log_step": ssm_log_step[0], "ssm_b_re": ssm_b_re[0], "ssm_b_im": ssm_b_im[0],
        "ssm_c_re": ssm_c_re[0], "ssm_c_im": ssm_c_im[0], "ssm_d": small_full[3], "ssm_b_glu": small_full[4],
    }

    loss, grad_x, g, recv_big = _local_step(x[0], c[0], ctx[0], c_ctx, loss_target[0], w, late, True, mod)

    dmod_all = _all_gather([_pad8(g["mod"].reshape(4, d3))], [0], "gather_dmod")[0].reshape(N_DEV, 8, d3)
    dmod_all = dmod_all[:, :4].reshape(N_DEV, 2, 2, d3)
    dm_ctx = dmod_all[0, :, 1]
    for p in range(1, N_DEV):
        dm_ctx = dm_ctx + dmod_all[p, :, 1]
    dm16 = jnp.concatenate([dmod_all[:, :, 0].transpose(1, 0, 2), dm_ctx[:, None], jnp.zeros((2, 16 - N_DEV - 1, d3), F32)], axis=1)
    g_ada_w, dc16 = _ada_bwd(c16, ada_w_b, lax.dynamic_slice_in_dim(dm16, me * wa, wa, axis=2), F32)
    g["c_ctx"] = dc16[0, N_DEV] + dc16[1, N_DEV]
    g_ada_b = jnp.sum(dm16, axis=1)

    blob_names = [n for n in REPLICATED if n != "ada_b"] + SMALL_SHARDED
    flat = jnp.concatenate([g[n].reshape(-1).astype(F32) for n in blob_names] + [loss.reshape(1)])
    nflat = flat.shape[0]
    rows = -(-nflat // (N_DEV * 128 * 8)) * 8
    flat = jnp.concatenate([flat, jnp.zeros((N_DEV * rows * 128 - nflat,), F32)]).reshape(N_DEV * rows, 128)
    last = [n for n in big_names if n not in recv_big]
    recv = _all_to_all([_view2d(n, g[n]) for n in last] + [flat], [BIG[n] for n in last] + [0], "scatter_grads")
    recv_big.update(zip(last, recv[:-1]))
    blob_sum = _sum_partials(recv[-1])
    blob = _all_gather([blob_sum], [0], "gather_small_grads", "gather2")[0].reshape(-1)
    small_g, off = {"ada_b": g_ada_b}, 0
    for n in blob_names:
        shape = wt[n].shape if n in REPLICATED else (*wt[n].shape[:-1], wt[n].shape[-1] * N_DEV)
        size = math.prod(shape)
        small_g[n] = blob[off:off + size].reshape(shape)
        off += size
    loss = blob[off]
    for n in SMALL_SHARDED:
        size = wt[n].shape[-1]
        small_g[n] = lax.dynamic_slice_in_dim(small_g[n], me * size, size, axis=small_g[n].ndim - 1)

    out_g, out_d, out_m, out_v = {}, {}, {}, {}
    recv_big["ada_w"] = _view2d("ada_w", g_ada_w)[None]
    for n in BIG:
        stack = recv_big[n]
        shp = wt[n].shape
        res = _adamw(stack, _view2d(n, wt[n]), _view2d(n, mt[n]), _view2d(n, vt[n]), "adamw_" + n)
        out_g[n], out_d[n], out_m[n], out_v[n] = [r.reshape(shp) for r in res]
    for n in NATIVE_SMALL:
        shp = wt[n].shape
        v2 = lambda a: a.reshape(-1, shp[-1])
        res = _adamw(v2(small_g.pop(n))[None], v2(wt[n]), v2(mt[n]), v2(vt[n]), "adamw_" + n)
        out_g[n], out_d[n], out_m[n], out_v[n] = [r.reshape(shp) for r in res]
    names = list(small_g)
    cat = lambda t: jnp.concatenate([t[n].reshape(-1) for n in names])
    gs, ws, ms, vs = cat(small_g), cat(wt), cat(mt), cat(vt)
    ns = gs.shape[0]
    rs = -(-ns // (128 * 512)) * 512
    padr = lambda a: jnp.concatenate([a, jnp.ones((rs * 128 - ns,), F32)]).reshape(rs, 128)
    res = _adamw(padr(gs)[None], padr(ws), padr(ms), padr(vs), "adamw_small")
    off = 0
    for n in names:
        size = math.prod(wt[n].shape)
        out_g[n], out_d[n], out_m[n], out_v[n] = [r.reshape(-1)[off:off + size].reshape(wt[n].shape) for r in res]
        off += size

    return (loss, grad_x[None], *[out_g[n] for n in WEIGHTS], *[out_d[n] for n in WEIGHTS],
            *[out_m[n] for n in WEIGHTS], *[out_v[n] for n in WEIGHTS])
```

```python
import math

import jax
import jax.numpy as jnp
from jax import lax
from jax.experimental import pallas as pl
from jax.experimental.pallas import tpu as pltpu

F32 = jnp.float32
BF = jnp.bfloat16
MESH = pl.DeviceIdType.MESH
N_DEV = 8

GRID_W = 64
CHUNK = 16
S5_P = 16
S5_N = 64
LN_EPS = 1e-5
DN_ALPHA = 4.0 ** 0.25
ADAM_LR, ADAM_B1, ADAM_B2, ADAM_EPS, ADAM_WD, ADAM_STEP = 1e-3, 0.9, 0.999, 1e-8, 0.01, 10
GELU_C0 = math.sqrt(2.0 / math.pi)
GELU_C1 = 0.044715
VMEM_MB = 52

ANY = pl.BlockSpec(memory_space=pl.ANY)


def _cparams():
    return pltpu.CompilerParams(vmem_limit_bytes=VMEM_MB << 20)


def _dot(a, b):
    return jnp.dot(a, b, preferred_element_type=F32)


def _dot_nt(a, b):
    return lax.dot_general(a, b, (((1,), (1,)), ((), ())), preferred_element_type=F32)


def _dot_tn(a, b):
    return lax.dot_general(a, b, (((0,), (0,)), ((), ())), preferred_element_type=F32)


def _sigmoid(x):
    return 1.0 / (1.0 + jnp.exp(-x))


def _gelu_parts(y):
    u = y * y
    th = jnp.tanh(y * (GELU_C0 + (GELU_C0 * GELU_C1) * u))
    hy = 0.5 * y
    g = hy + hy * th
    dg = (0.5 + 0.5 * th) + hy * (1.0 - th * th) * (GELU_C0 + (3.0 * GELU_C0 * GELU_C1) * u)
    return g, dg


def _full(shape):
    nd = len(shape)
    return pl.BlockSpec(shape, lambda *_: (0,) * nd)


def _mesh_pos():
    x, y, c = lax.axis_index("x"), lax.axis_index("y"), lax.axis_index("c")
    return x, y, c


def _peer(pos, k):
    x, y, c = pos
    px = 1 - x if (k >> 2) & 1 else x
    py = 1 - y if (k >> 1) & 1 else y
    pc = 1 - c if k & 1 else c
    return (px, py, pc), 4 * px + 2 * py + pc


def _shard_at(ref, axis, idx, n):
    if axis == 0:
        return ref.at[pl.ds(idx * n, n)]
    return ref.at[:, pl.ds(idx * n, n)]


class _Exchange:
    def __init__(self, kind, arrays, axes):
        self.kind, self.axes, self.n = kind, list(axes), len(arrays)
        self.arrays = list(arrays)
        self.out_shape = []
        for s, ax in zip(arrays, axes):
            shp = list(s.shape)
            if kind == "scatter":
                shp[ax] //= N_DEV
                self.out_shape.append(jax.ShapeDtypeStruct((N_DEV, *shp), s.dtype))
            else:
                shp[ax] *= N_DEV
                self.out_shape.append(jax.ShapeDtypeStruct(tuple(shp), s.dtype))
        self.scratch = [pltpu.SemaphoreType.DMA((self.n, N_DEV - 1)), pltpu.SemaphoreType.DMA((self.n, N_DEV - 1)),
                        pltpu.SemaphoreType.DMA((self.n,))]

    def _copies(self, ins, outs, sems):
        send_sems, recv_sems, local_sems = sems
        pos = _mesh_pos()
        x, y, c = pos
        me = 4 * x + 2 * y + c
        local, sends, chained, recvs = [], [], [], []
        for i in range(self.n):
            ax = self.axes[i]
            if self.kind == "scatter":
                size = ins[i].shape[ax] // N_DEV
                src = lambda idx, i=i, ax=ax, size=size: _shard_at(ins[i], ax, idx, size)
                dst = lambda idx, i=i: outs[i].at[idx]
            else:
                size = ins[i].shape[ax]
                src = lambda idx, i=i: ins[i]
                dst = lambda idx, i=i, ax=ax, size=size: _shard_at(outs[i], ax, idx, size)

            def copy(k, s, d, to, i=i):
                return pltpu.make_async_remote_copy(src_ref=s, dst_ref=d, send_sem=send_sems.at[i, k],
                                                    recv_sem=recv_sems.at[i, k], device_id=to, device_id_type=MESH)

            local.append(pltpu.make_async_copy(src(me), dst(me), local_sems.at[i]))
            if self.kind == "gather2":
                sib, sib_i = (x, y, 1 - c), 4 * x + 2 * y + (1 - c)
                chips = [(1 - x, y), (x, 1 - y), (1 - x, 1 - y)]
                sends.append(copy(0, src(me), dst(me), sib))
                recvs.append(copy(0, src(me), dst(sib_i), sib))
                for j, (cx, cy) in enumerate(chips):
                    same, other = 4 * cx + 2 * cy + c, 4 * cx + 2 * cy + (1 - c)
                    sends.append(copy(1 + j, src(me), dst(me), (cx, cy, c)))
                    chained.append((copy(1 + j, dst(same), dst(same), (cx, cy, c)), copy(4 + j, dst(same), dst(same), sib)))
                    recvs.append(copy(4 + j, dst(other), dst(other), sib))
            else:
                for k in range(1, N_DEV):
                    peer, pidx = _peer(pos, k)
                    out_src = src(pidx) if self.kind == "scatter" else src(me)
                    sends.append(copy(k - 1, out_src, dst(me), peer))
                    recvs.append(copy(k - 1, out_src, dst(pidx), peer))
        return local, sends, chained, recvs

    def start(self, ins, outs, sems):
        local, sends, _, _ = self._copies(ins, outs, sems)
        for cp in local + sends:
            cp.start()

    def wait(self, ins, outs, sems):
        local, sends, chained, recvs = self._copies(ins, outs, sems)
        for arrival, released in chained:
            arrival.wait_recv()
            released.start()
        for cp in recvs:
            cp.wait_recv()
        for cp in sends + [released for _, released in chained]:
            cp.wait_send()
        for cp in local:
            cp.wait()

    def run(self, name):
        n = self.n

        def body(*refs):
            ins, outs, sems = refs[:n], refs[n:2 * n], refs[2 * n:]
            self.start(ins, outs, sems)
            self.wait(ins, outs, sems)

        return pl.pallas_call(body, name=name, out_shape=self.out_shape, in_specs=[ANY] * n, out_specs=[ANY] * n,
                              scratch_shapes=self.scratch)(*self.arrays)


def _hosted_call(body, xch, grid, in_specs, out_specs, out_shape, scratch, args, name):
    out_specs, out_shape = list(out_specs), list(out_shape)
    n_in, n_out = len(in_specs), len(out_specs)
    if xch is None:
        res = pl.pallas_call(body, name=name, grid=grid, in_specs=in_specs, out_specs=out_specs, out_shape=out_shape,
                             scratch_shapes=list(scratch), compiler_params=_cparams())(*args)
        return list(res), []
    n = xch.n
    rank = len(grid)

    def wrapped(*refs):
        ins, x_ins = refs[:n_in], refs[n_in:n_in + n]
        outs = refs[n_in + n:n_in + n + n_out]
        x_outs = refs[n_in + n + n_out:n_in + 2 * n + n_out]
        rest = refs[n_in + 2 * n + n_out:]
        own, sems = rest[:len(rest) - 3], rest[len(rest) - 3:]
        ids = [pl.program_id(a) for a in range(rank)]
        first, last = ids[0] == 0, ids[0] == grid[0] - 1
        for a in range(1, rank):
            first = jnp.logical_and(first, ids[a] == 0)
            last = jnp.logical_and(last, ids[a] == grid[a] - 1)

        @pl.when(first)
        def _():
            xch.start(x_ins, x_outs, sems)

        body(*ins, *outs, *own)

        @pl.when(last)
        def _():
            xch.wait(x_ins, x_outs, sems)

    res = pl.pallas_call(
        wrapped, name=name, grid=grid, in_specs=list(in_specs) + [ANY] * n, out_specs=out_specs + [ANY] * n,
        out_shape=out_shape + xch.out_shape, scratch_shapes=list(scratch) + xch.scratch, compiler_params=_cparams(),
    )(*args, *xch.arrays)
    return list(res[:n_out]), list(res[n_out:])


def _all_gather(shards, axes, name, kind="gather"):
    return _Exchange(kind, shards, axes).run(name)


def _all_to_all(parts, axes, name):
    return _Exchange("scatter", parts, axes).run(name)


def _ada_fwd(cv, ada_w, ada_b):
    nl, d, wd = ada_w.shape
    r = cv.shape[0]

    def body(c_ref, w_ref, b_ref, o_ref):
        c = c_ref[...]
        s = (c * _sigmoid(c)).astype(BF)
        o_ref[0] = _dot(s, w_ref[0]) + b_ref[0]

    return pl.pallas_call(
        body, name="ada_fwd", grid=(nl,),
        in_specs=[_full((r, d)), pl.BlockSpec((1, d, wd), lambda l: (l, 0, 0)), pl.BlockSpec((1, 1, wd), lambda l: (l, 0, 0))],
        out_specs=pl.BlockSpec((1, r, wd), lambda l: (l, 0, 0)),
        out_shape=jax.ShapeDtypeStruct((nl, r, wd), F32), compiler_params=_cparams(),
    )(cv, ada_w, ada_b.reshape(nl, 1, wd))


def _ada_bwd(cv, ada_w, dm, out_dtype):
    nl, d, wd = ada_w.shape
    r = cv.shape[0]

    def body(c_ref, w_ref, dm_ref, dw_ref, dc_ref):
        c = c_ref[...]
        sg = _sigmoid(c)
        s = (c * sg).astype(BF)
        dmv = dm_ref[0].astype(BF)
        dw_ref[0] = _dot_tn(s, dmv).astype(out_dtype)
        dc_ref[0] = _dot_nt(dmv, w_ref[0]) * (sg * (1.0 + c * (1.0 - sg)))

    return pl.pallas_call(
        body, name="ada_bwd", grid=(nl,),
        in_specs=[_full((r, d)), pl.BlockSpec((1, d, wd), lambda l: (l, 0, 0)), pl.BlockSpec((1, r, wd), lambda l: (l, 0, 0))],
        out_specs=[pl.BlockSpec((1, d, wd), lambda l: (l, 0, 0)), pl.BlockSpec((1, r, d), lambda l: (l, 0, 0))],
        out_shape=[jax.ShapeDtypeStruct((nl, d, wd), out_dtype), jax.ShapeDtypeStruct((nl, r, d), F32)],
        compiler_params=_cparams(),
    )(cv, ada_w, dm)


def _sum_partials(stack):
    _, r, c = stack.shape

    def body(s_ref, o_ref):
        acc = s_ref[0]
        for p in range(1, N_DEV):
            acc = acc + s_ref[p]
        o_ref[...] = acc

    return pl.pallas_call(body, name="sum_partials", out_shape=jax.ShapeDtypeStruct((r, c), F32),
                          in_specs=[_full(stack.shape)], out_specs=_full((r, c)), grid=(1,),
                          compiler_params=_cparams())(stack)


def _adamw(gstack, w, m, v, name):
    p, r, c = gstack.shape
    tr = r
    for cand in (512 if c <= 256 else 256, 128, 64, 32, 16, 8):
        if r % cand == 0 and r > cand:
            tr = cand
            break
    bc1 = 1.0 - ADAM_B1 ** ADAM_STEP
    bc2 = 1.0 - ADAM_B2 ** ADAM_STEP

    def body(g_ref, w_ref, m_ref, v_ref, go_ref, d_ref, mo_ref, vo_ref):
        g = g_ref[0].astype(F32)
        for q in range(1, p):
            g = g + g_ref[q].astype(F32)
        mn = ADAM_B1 * m_ref[...] + (1.0 - ADAM_B1) * g
        vn = ADAM_B2 * v_ref[...] + (1.0 - ADAM_B2) * (g * g)
        go_ref[...] = g
        mo_ref[...] = mn
        vo_ref[...] = vn
        d_ref[...] = -ADAM_LR * ((mn / bc1) / (jnp.sqrt(vn / bc2) + ADAM_EPS) + ADAM_WD * w_ref[...])

    row = pl.BlockSpec((tr, c), lambda i: (i, 0))
    sds = jax.ShapeDtypeStruct((r, c), F32)
    return pl.pallas_call(
        body, name=name, grid=(r // tr,),
        in_specs=[pl.BlockSpec((p, tr, c), lambda i: (0, i, 0)), row, row, row],
        out_specs=[row, row, row, row], out_shape=[sds, sds, sds, sds], compiler_params=_cparams(),
    )(gstack, w, m, v)


def _lat_or_ctx_specs(tm, d, nl, grid_rank, row_axis):
    def lat(*ids):
        return (jnp.minimum(ids[row_axis], nl - 1), 0)

    def ctx(*ids):
        return (jnp.maximum(ids[row_axis] - nl, 0), 0)

    return pl.BlockSpec((tm, d), lat), pl.BlockSpec((tm, d), ctx)


def _sel_row(ref, is_ctx):
    return jnp.where(is_ctx, ref[1:2, :], ref[0:1, :])


def _inproj0(x, ctx, a2, b2, w, tgt, tm, xch=None):
    l, d = x.shape
    nl, nc = l // tm, ctx.shape[0] // tm
    e = w.shape[1] // 4
    half = e // 2
    tjo = tm // CHUNK

    def body(x_ref, c_ref, a_ref, b_ref, w_hbm, t_ref, o_ref, tc_ref, w_ref, ts_ref):
        i = pl.program_id(0)

        @pl.when(i == 0)
        def _():
            pltpu.sync_copy(w_hbm, w_ref)

        is_ctx = i >= nl
        xv = jnp.where(is_ctx, c_ref[...], x_ref[...])
        h = (xv * _sel_row(a_ref, is_ctx) + _sel_row(b_ref, is_ctx)).astype(BF)
        for k in range(4):
            r = _dot(h, w_ref[:, k * e:(k + 1) * e])
            o_ref[k, 0] = r[:, :half].astype(BF)
            o_ref[k, 1] = r[:, half:].astype(BF)

        @pl.when(jnp.logical_not(is_ctx))
        def _():
            for lb in range(d // 128):
                ts_ref[lb] = t_ref[:, lb * 128:(lb + 1) * 128]
            for s in range(CHUNK):
                for lb in range(d // 128):
                    tc_ref[:, s * d + lb * 128:s * d + (lb + 1) * 128] = ts_ref.at[lb][pl.ds(s, tjo, stride=CHUNK), :]

    lat, cx = _lat_or_ctx_specs(tm, d, nl, 1, 0)
    (p42, tgt_cr), extra = _hosted_call(
        body, xch, grid=(nl + nc,),
        in_specs=[lat, cx, _full((2, d)), _full((2, d)), ANY, lat],
        out_specs=[pl.BlockSpec((4, 2, tm, half), lambda i: (0, 0, i, 0)),
                   pl.BlockSpec((tjo, CHUNK * d), lambda i: (jnp.minimum(i, nl - 1), 0))],
        out_shape=[jax.ShapeDtypeStruct((4, 2, l + ctx.shape[0], half), BF),
                   jax.ShapeDtypeStruct((l // CHUNK, CHUNK * d), F32)],
        scratch=[pltpu.VMEM(w.shape, BF), pltpu.VMEM((d // 128, tm, 128), F32)],
        args=(x, ctx, a2, b2, w, tgt), name="l0_inproj")
    return p42, tgt_cr, extra


def _conv_taps(u, w_up, w_mid, w_dn, pos, rl, tm):
    up = jnp.where(pos == 0, 0.0, pltpu.roll(u, 1, 0))
    dn = jnp.where(pos == rl - 1, 0.0, pltpu.roll(u, tm - 1, 0))
    return w_up * up + w_mid * u + w_dn * dn, up, dn


def _conv_halo_specs(tm, tc, nl, lead):
    hb = tm // GRID_W

    def prev(j, i):
        return (0, 1, jnp.maximum(jnp.minimum(i, nl - 1) * hb - 1, 0), j)

    def nxt(j, i):
        return (0, 1, jnp.minimum((jnp.minimum(i, nl - 1) + 1) * hb, nl * hb - 1), j)

    return pl.BlockSpec((lead, 1, GRID_W, tc), prev), pl.BlockSpec((lead, 1, GRID_W, tc), nxt)


def _conv_fwd(p42, cw, nl, tm, tc):
    _, _, r, half = p42.shape
    nt = r // tm

    def body(p_ref, hp_ref, hn_ref, cw_ref, o_ref):
        i = pl.program_id(1)
        is_ctx = i >= nl
        row = lax.broadcasted_iota(jnp.int32, (tm, tc), 0)
        rl = jnp.where(is_ctx, tm, GRID_W)
        pos = jnp.bitwise_and(row, rl - 1)

        def gate(hv, yc):
            bg = p_ref[0, hv].astype(F32)
            z = p_ref[3, hv].astype(F32)
            return (bg * yc * (z * _sigmoid(z))).astype(BF)

        u_h = p_ref[1, 0].astype(F32) * p_ref[2, 0].astype(F32)
        w_h = cw_ref[:, 0, :]
        o_ref[0] = gate(0, _conv_taps(u_h, w_h[0:1], w_h[1:2], w_h[2:3], pos, rl, tm)[0])
        u_v = p_ref[1, 1].astype(F32) * p_ref[2, 1].astype(F32)
        w_v = cw_ref[:, 1, :]

        @pl.when(is_ctx)
        def _():
            o_ref[1] = gate(1, _conv_taps(u_v, w_v[0:1], w_v[1:2], w_v[2:3], pos, rl, tm)[0])

        @pl.when(jnp.logical_not(is_ctx))
        def _():
            up = hp_ref[1, 0].astype(F32) * hp_ref[2, 0].astype(F32) * (i > 0).astype(F32)
            dn = hn_ref[1, 0].astype(F32) * hn_ref[2, 0].astype(F32) * (i < nl - 1).astype(F32)
            ext = jnp.concatenate([up, u_v, dn], axis=0)
            yc = w_v[0:1] * ext[0:tm] + w_v[1:2] * u_v + w_v[2:3] * ext[2 * GRID_W:tm + 2 * GRID_W]
            o_ref[1] = gate(1, yc)

    hp, hn = _conv_halo_specs(tm, tc, nl, 4)
    return pl.pallas_call(
        body, name="l0_conv_fwd", grid=(half // tc, nt),
        in_specs=[pl.BlockSpec((4, 2, tm, tc), lambda j, i: (0, 0, i, j)), hp, hn,
                  pl.BlockSpec((3, 2, tc), lambda j, i: (0, 0, j))],
        out_specs=pl.BlockSpec((2, tm, tc), lambda j, i: (0, i, j)),
        out_shape=jax.ShapeDtypeStruct((2, r, half), BF), compiler_params=_cparams(),
    )(p42, p42, p42, cw)


def _outproj_ln0(q3, w_out, x, ctx, gt2, tm):
    l, d = x.shape
    lc = ctx.shape[0]
    nl, nc = l // tm, lc // tm
    _, r, half = q3.shape
    tjo = tm // CHUNK

    def body(q_ref, w_hbm, x_ref, c_ref, g_ref, xl_ref, xc_ref, rl_ref, rc_ref, fx_ref, w_ref, xs_ref, rs_ref):
        i = pl.program_id(0)

        @pl.when(i == 0)
        def _():
            pltpu.sync_copy(w_hbm, w_ref)

        is_ctx = i >= nl
        fx = _dot(q_ref[0], w_ref[:half, :]) + _dot(q_ref[1], w_ref[half:, :])
        xv = jnp.where(is_ctx, c_ref[...], x_ref[...])
        rr = DN_ALPHA * xv + _sel_row(g_ref, is_ctx) * fx
        mu = jnp.mean(rr, axis=-1, keepdims=True)
        cen = rr - mu
        rstd = lax.rsqrt(jnp.mean(cen * cen, axis=-1, keepdims=True) + LN_EPS)
        xh = cen * rstd
        for lb in range(d // 128):
            xs_ref[lb] = xh[:, lb * 128:(lb + 1) * 128]
        rs_ref[...] = jnp.broadcast_to(rstd, (tm, 128))
        fx_ref[...] = fx.astype(BF)

        def to_cr(xo_ref, ro_ref):
            for s in range(CHUNK):
                for lb in range(d // 128):
                    xo_ref[:, s * d + lb * 128:s * d + (lb + 1) * 128] = xs_ref.at[lb][pl.ds(s, tjo, stride=CHUNK), :]
                ro_ref[:, s * 128:(s + 1) * 128] = rs_ref[pl.ds(s, tjo, stride=CHUNK), :]

        @pl.when(jnp.logical_not(is_ctx))
        def _():
            to_cr(xl_ref, rl_ref)

        @pl.when(is_ctx)
        def _():
            to_cr(xc_ref, rc_ref)

    lat, cx = _lat_or_ctx_specs(tm, d, nl, 1, 0)
    lat_o = lambda w_: pl.BlockSpec((tjo, CHUNK * w_), lambda i: (jnp.minimum(i, nl - 1), 0))
    ctx_o = lambda w_: pl.BlockSpec((tjo, CHUNK * w_), lambda i: (jnp.maximum(i - nl, 0), 0))
    return pl.pallas_call(
        body, name="l0_outproj_ln", grid=(nl + nc,),
        in_specs=[pl.BlockSpec((2, tm, half), lambda i: (0, i, 0)), ANY, lat, cx, _full((2, d))],
        out_specs=[lat_o(d), ctx_o(d), lat_o(128), ctx_o(128), pl.BlockSpec((tm, d), lambda i: (i, 0))],
        out_shape=[jax.ShapeDtypeStruct((l // CHUNK, CHUNK * d), F32), jax.ShapeDtypeStruct((lc // CHUNK, CHUNK * d), F32),
                   jax.ShapeDtypeStruct((l // CHUNK, CHUNK * 128), F32), jax.ShapeDtypeStruct((lc // CHUNK, CHUNK * 128), F32),
                   jax.ShapeDtypeStruct((r, d), BF)],
        scratch_shapes=[pltpu.VMEM(w_out.shape, BF), pltpu.VMEM((d // 128, tm, 128), F32), pltpu.VMEM((tm, 128), F32)],
        compiler_params=_cparams(),
    )(q3, w_out, x, ctx, gt2)


def _bwd_outproj0(dr_l, dr_c, gt2, w_out, fx, tm):
    l, d = dr_l.shape
    nl, nc = l // tm, dr_c.shape[0] // tm
    e = w_out.shape[0]
    half = e // 2
    r = l + dr_c.shape[0]

    def body(dl_ref, dc_ref, g_ref, w_hbm, fx_ref, dq_ref, acc_ref, w_ref):
        i = pl.program_id(0)

        @pl.when(i == 0)
        def _():
            pltpu.sync_copy(w_hbm, w_ref)
            acc_ref[...] = jnp.zeros_like(acc_ref)

        is_ctx = i >= nl
        dr = jnp.where(is_ctx, dc_ref[...], dl_ref[...]).astype(F32)
        dfx = (dr * _sel_row(g_ref, is_ctx)).astype(BF)
        dq_ref[0] = _dot_nt(dfx, w_ref[:half, :]).astype(BF)
        dq_ref[1] = _dot_nt(dfx, w_ref[half:, :]).astype(BF)
        s = jnp.sum(dr * fx_ref[...].astype(F32), axis=0, keepdims=True)
        sel = is_ctx.astype(F32)
        acc_ref[0:1, :] += s * (1.0 - sel)
        acc_ref[1:2, :] += s * sel

    lat, cx = _lat_or_ctx_specs(tm, d, nl, 1, 0)
    return pl.pallas_call(
        body, name="l0_bwd_outproj", grid=(nl + nc,),
        in_specs=[lat, cx, _full((2, d)), ANY, pl.BlockSpec((tm, d), lambda i: (i, 0))],
        out_specs=[pl.BlockSpec((2, tm, half), lambda i: (0, i, 0)), _full((8, d))],
        out_shape=[jax.ShapeDtypeStruct((2, r, half), BF), jax.ShapeDtypeStruct((8, d), F32)],
        scratch_shapes=[pltpu.VMEM(w_out.shape, BF)], compiler_params=_cparams(),
    )(dr_l, dr_c, gt2, w_out, fx)


def _conv_bwd_inproj0(dq3, p42, cw, w_in, x, ctx, dr_l, dr_c, a2, nl, tm, xch=None):
    l, d = x.shape
    _, _, r, half = p42.shape
    nt = r // tm
    e = 2 * half
    cc = min(512, half)
    n_cc = half // cc

    def body(dq_ref, dqp_ref, dqn_ref, p_ref, hp_ref, hn_ref, cw_ref, w_hbm, x_ref, c_ref, dl_ref, dc_ref, a_ref,
             dp_ref, dw_ref, gx_ref, acc_ref, w_ref, dh_ref):
        i, hv = pl.program_id(0), pl.program_id(1)
        is_ctx = i >= nl

        @pl.when(jnp.logical_and(i == 0, hv == 0))
        def _():
            pltpu.sync_copy(w_hbm, w_ref)
            acc_ref[...] = jnp.zeros_like(acc_ref)
            dw_ref[...] = jnp.zeros_like(dw_ref)

        row = lax.broadcasted_iota(jnp.int32, (tm, cc), 0)
        rl = jnp.where(is_ctx, tm, GRID_W)
        pos = jnp.bitwise_and(row, rl - 1)

        def pieces(dq, bg, z):
            sz = _sigmoid(z)
            sil = z * sz
            return dq * bg * sil, dq * sil, dq * bg * (sz * (1.0 + z * (1.0 - sz)))

        def emit(hvs, c, parts, dyc, u_up, u, u_dn, dh):
            lanes = slice(c * cc, (c + 1) * cc)
            for k, part in enumerate(parts):
                pb = part.astype(BF)
                dp_ref[k, 0, :, lanes] = pb
                c0 = k * e + hvs * half + c * cc
                t = _dot_nt(pb, w_ref[:, c0:c0 + cc])
                dh = t if dh is None else dh + t
            dw_ref[0:1, hvs, lanes] += jnp.sum(dyc * u_up, axis=0, keepdims=True)
            dw_ref[1:2, hvs, lanes] += jnp.sum(dyc * u, axis=0, keepdims=True)
            dw_ref[2:3, hvs, lanes] += jnp.sum(dyc * u_dn, axis=0, keepdims=True)
            return dh

        def seq_half(hvs):
            dh = None
            for c in range(n_cc):
                lanes = slice(c * cc, (c + 1) * cc)
                bg, cg = p_ref[0, 0, :, lanes].astype(F32), p_ref[1, 0, :, lanes].astype(F32)
                v, z = p_ref[2, 0, :, lanes].astype(F32), p_ref[3, 0, :, lanes].astype(F32)
                w = cw_ref[:, hvs, lanes]
                u = cg * v
                yc, u_up, u_dn = _conv_taps(u, w[0:1], w[1:2], w[2:3], pos, rl, tm)
                dyc, dbg_f, dz_f = pieces(dq_ref[0, :, lanes].astype(F32), bg, z)
                du = _conv_taps(dyc, w[2:3], w[1:2], w[0:1], pos, rl, tm)[0]
                dh = emit(hvs, c, (dbg_f * yc, du * v, du * cg, dz_f * yc), dyc, u_up, u, u_dn, dh)
            return dh

        def col_half():
            m_up = (i > 0).astype(F32)
            m_dn = (i < nl - 1).astype(F32)
            dh = None
            for c in range(n_cc):
                lanes = slice(c * cc, (c + 1) * cc)
                bg, cg = p_ref[0, 0, :, lanes].astype(F32), p_ref[1, 0, :, lanes].astype(F32)
                v, z = p_ref[2, 0, :, lanes].astype(F32), p_ref[3, 0, :, lanes].astype(F32)
                w = cw_ref[:, 1, lanes]
                u = cg * v

                def halo(h_ref, dqh_ref, msk):
                    hb, hc = h_ref[0, 0, :, lanes].astype(F32), h_ref[1, 0, :, lanes].astype(F32)
                    hv_, hz = h_ref[2, 0, :, lanes].astype(F32), h_ref[3, 0, :, lanes].astype(F32)
                    return hc * hv_ * msk, pieces(dqh_ref[0, :, lanes].astype(F32), hb, hz)[0] * msk

                u_p, dyc_p = halo(hp_ref, dqp_ref, m_up)
                u_n, dyc_n = halo(hn_ref, dqn_ref, m_dn)
                u_ext = jnp.concatenate([u_p, u, u_n], axis=0)
                u_up, u_dn = u_ext[0:tm], u_ext[2 * GRID_W:tm + 2 * GRID_W]
                yc = w[0:1] * u_up + w[1:2] * u + w[2:3] * u_dn
                dyc, dbg_f, dz_f = pieces(dq_ref[0, :, lanes].astype(F32), bg, z)
                d_ext = jnp.concatenate([dyc_p, dyc, dyc_n], axis=0)
                du = w[0:1] * d_ext[2 * GRID_W:tm + 2 * GRID_W] + w[1:2] * dyc + w[2:3] * d_ext[0:tm]
                dh = emit(1, c, (dbg_f * yc, du * v, du * cg, dz_f * yc), dyc, u_up, u, u_dn, dh)
            return dh

        @pl.when(hv == 0)
        def _():
            dh_ref[...] = seq_half(0)

        @pl.when(jnp.logical_and(hv == 1, is_ctx))
        def _():
            dh_ref[...] += seq_half(1)

        @pl.when(jnp.logical_and(hv == 1, jnp.logical_not(is_ctx)))
        def _():
            dh_ref[...] += col_half()

        @pl.when(hv == 1)
        def _():
            dh = dh_ref[...]
            xv = jnp.where(is_ctx, c_ref[...], x_ref[...])
            s_sc = jnp.sum(dh * xv, axis=0, keepdims=True)
            s_sh = jnp.sum(dh, axis=0, keepdims=True)
            sel = is_ctx.astype(F32)
            acc_ref[0:1, :] += s_sc * (1.0 - sel)
            acc_ref[1:2, :] += s_sc * sel
            acc_ref[2:3, :] += s_sh * (1.0 - sel)
            acc_ref[3:4, :] += s_sh * sel

        @pl.when(jnp.logical_and(hv == 1, jnp.logical_not(is_ctx)))
        def _():
            gx_ref[...] = DN_ALPHA * dl_ref[...].astype(F32) + dh_ref[...] * a_ref[0:1, :]

    hb = tm // GRID_W
    prev_blk = lambda i: jnp.maximum(jnp.minimum(i, nl - 1) * hb - 1, 0)
    next_blk = lambda i: jnp.minimum((jnp.minimum(i, nl - 1) + 1) * hb, nl * hb - 1)
    lat, cx = _lat_or_ctx_specs(tm, d, nl, 2, 0)
    (dp42, dcw, gx, acc), extra = _hosted_call(
        body, xch, grid=(nt, 2),
        in_specs=[pl.BlockSpec((1, tm, half), lambda i, h: (h, i, 0)),
                  pl.BlockSpec((1, GRID_W, half), lambda i, h: (1, prev_blk(i), 0)),
                  pl.BlockSpec((1, GRID_W, half), lambda i, h: (1, next_blk(i), 0)),
                  pl.BlockSpec((4, 1, tm, half), lambda i, h: (0, h, i, 0)),
                  pl.BlockSpec((4, 1, GRID_W, half), lambda i, h: (0, 1, prev_blk(i), 0)),
                  pl.BlockSpec((4, 1, GRID_W, half), lambda i, h: (0, 1, next_blk(i), 0)),
                  _full((3, 2, half)), ANY, lat, cx, lat, cx, _full((2, d))],
        out_specs=[pl.BlockSpec((4, 1, tm, half), lambda i, h: (0, h, i, 0)), _full((8, 2, half)),
                   pl.BlockSpec((tm, d), lambda i, h: (jnp.minimum(i, nl - 1), 0)), _full((8, d))],
        out_shape=[jax.ShapeDtypeStruct(p42.shape, BF), jax.ShapeDtypeStruct((8, 2, half), F32),
                   jax.ShapeDtypeStruct((l, d), F32), jax.ShapeDtypeStruct((8, d), F32)],
        scratch=[pltpu.VMEM(w_in.shape, BF), pltpu.VMEM((tm, d), F32)],
        args=(dq3, dq3, dq3, p42, p42, p42, cw, w_in, x, ctx, dr_l, dr_c, a2), name="l0_conv_bwd_inproj")
    return dp42, dcw, gx, acc, extra


def _dw_inproj0(x, ctx, a2, b2, dp42, tm):
    l, d = x.shape
    lc = ctx.shape[0]
    assert lc == tm
    tl = 4 * tm if l % (4 * tm) == 0 else tm
    nl = l // tl
    half = dp42.shape[-1]
    e = 2 * half

    def body(x_ref, c_ref, a_ref, b_ref, dpl_ref, dpc_ref, o_ref, acc_ref):
        i = pl.program_id(1)

        @pl.when(i == 0)
        def _():
            acc_ref[...] = jnp.zeros_like(acc_ref)

        def add(rows_ref, dp_ref, sel):
            h = (rows_ref[...] * a_ref[sel:sel + 1, :] + b_ref[sel:sel + 1, :]).astype(BF)
            acc_ref[:, :half] += _dot_tn(h, dp_ref[0, 0])
            acc_ref[:, half:] += _dot_tn(h, dp_ref[0, 1])

        @pl.when(i < nl)
        def _():
            add(x_ref, dpl_ref, 0)

        @pl.when(i == nl)
        def _():
            add(c_ref, dpc_ref, 1)
            o_ref[...] = acc_ref[...].astype(BF)

    return pl.pallas_call(
        body, name="l0_dw_inproj", grid=(4, nl + 1),
        in_specs=[pl.BlockSpec((tl, d), lambda k, i: (jnp.minimum(i, nl - 1), 0)), _full((lc, d)),
                  _full((2, d)), _full((2, d)),
                  pl.BlockSpec((1, 2, tl, half), lambda k, i: (k, 0, jnp.minimum(i, nl - 1), 0)),
                  pl.BlockSpec((1, 2, lc, half), lambda k, i: (k, 0, l // lc, 0))],
        out_specs=pl.BlockSpec((d, e), lambda k, i: (0, k)),
        out_shape=jax.ShapeDtypeStruct((d, 4 * e), BF),
        scratch_shapes=[pltpu.VMEM((d, e), F32)], compiler_params=_cparams(),
    )(x, ctx, a2, b2, dp42, dp42)


def _dw_outproj0(q3, dr_l, dr_c, gt2, tm, xch=None):
    l, d = dr_l.shape
    nl, nc = l // tm, dr_c.shape[0] // tm
    _, r, half = q3.shape
    nt = nl + nc

    def body(q_ref, dl_ref, dc_ref, g_ref, o_ref, acc_ref):
        i = pl.program_id(0)
        is_ctx = i >= nl

        @pl.when(i == 0)
        def _():
            acc_ref[...] = jnp.zeros_like(acc_ref)

        dr = jnp.where(is_ctx, dc_ref[...], dl_ref[...]).astype(F32)
        dfx = (dr * _sel_row(g_ref, is_ctx)).astype(BF)
        acc_ref[:half, :] += _dot_tn(q_ref[0], dfx)
        acc_ref[half:, :] += _dot_tn(q_ref[1], dfx)

        @pl.when(i == nt - 1)
        def _():
            o_ref[...] = acc_ref[...].astype(BF)

    lat, cx = _lat_or_ctx_specs(tm, d, nl, 1, 0)
    (g_w,), extra = _hosted_call(
        body, xch, grid=(nt,),
        in_specs=[pl.BlockSpec((2, tm, half), lambda i: (0, i, 0)), lat, cx, _full((2, d))],
        out_specs=[_full((2 * half, d))], out_shape=[jax.ShapeDtypeStruct((2 * half, d), BF)],
        scratch=[pltpu.VMEM((2 * half, d), F32)], args=(q3, dr_l, dr_c, gt2), name="l0_dw_outproj")
    return g_w, extra


def _cr_tile(j, cap=256):
    for cand in (1024, 512, 256, 128, 64, 32, 16, 8):
        if cand <= cap and j % cand == 0:
            return cand
    raise ValueError(j)


CR_ROW_TILE = 16


def _cr_row_tile(j):
    assert j % CR_ROW_TILE == 0, j
    return CR_ROW_TILE


def _cr_row_spec(tj, width):
    return pl.BlockSpec((tj, CHUNK * width), lambda t: (t, 0))


def _cr_rows(ref, width):
    return jnp.concatenate([ref[:, s * width:(s + 1) * width] for s in range(CHUNK)], axis=0)


def _cr_store(ref, val, width):
    tj = ref.shape[0]
    for s in range(CHUNK):
        ref[:, s * width:(s + 1) * width] = val[s * tj:(s + 1) * tj, :]


def _final(w_cr, w_out, xh_cr, tgt_cr, vecs):
    j, e16 = w_cr.shape
    e = e16 // CHUNK
    d = w_out.shape[1]
    tj = _cr_row_tile(j)

    def body(w_ref, wo_hbm, xh_ref, t_ref, v_ref, dr_ref, acc_ref, wo_ref):
        @pl.when(pl.program_id(0) == 0)
        def _():
            pltpu.sync_copy(wo_hbm, wo_ref)
            acc_ref[...] = jnp.zeros_like(acc_ref)

        o = _dot(_cr_rows(w_ref, e), wo_ref[...])
        x1 = _cr_rows(xh_ref, d) * v_ref[0:1, :] + v_ref[1:2, :]
        rr = DN_ALPHA * x1 + v_ref[2:3, :] * o
        mu = jnp.mean(rr, axis=-1, keepdims=True)
        cen = rr - mu
        rstd = lax.rsqrt(jnp.mean(cen * cen, axis=-1, keepdims=True) + LN_EPS)
        xh2 = cen * rstd
        err = xh2 * v_ref[3:4, :] + v_ref[4:5, :] - _cr_rows(t_ref, d)
        dy = err * (1.0 / d)
        dxh = dy * v_ref[3:4, :]
        dr = rstd * (dxh - jnp.mean(dxh, axis=-1, keepdims=True) - xh2 * jnp.mean(dxh * xh2, axis=-1, keepdims=True))
        _cr_store(dr_ref, dr.astype(BF), d)
        acc_ref[0:1, :] += jnp.sum(dy * xh2, axis=0, keepdims=True)
        acc_ref[1:2, :] += jnp.sum(dy, axis=0, keepdims=True)
        acc_ref[2:3, :] += jnp.sum(dr * o, axis=0, keepdims=True)
        acc_ref[3:4, :] += (0.5 / d) * jnp.sum(err * err, axis=0, keepdims=True)

    tok_d = _cr_row_spec(tj, d)
    return pl.pallas_call(
        body, name="l1_final", grid=(j // tj,),
        in_specs=[_cr_row_spec(tj, e), ANY, tok_d, tok_d, _full((8, d))],
        out_specs=[tok_d, _full((8, d))],
        out_shape=[jax.ShapeDtypeStruct((j, CHUNK * d), BF), jax.ShapeDtypeStruct((8, d), F32)],
        scratch_shapes=[pltpu.VMEM(w_out.shape, BF)], compiler_params=_cparams(),
    )(w_cr, w_out, xh_cr, tgt_cr, vecs)


def _dw_cr(lhs, rhs, lhs_kind, rhs_kind, vec, bias_sum, init, name):
    j = lhs.shape[0]
    k = lhs.shape[1] // CHUNK
    n = rhs.shape[1] // CHUNK
    tj = _cr_row_tile(j)
    nt = j // tj
    has_init = init is not None

    def body(*refs):
        refs = list(refs)
        l_ref, r_ref = refs[0], refs[1]
        pos = 2
        v_ref = None
        if vec is not None:
            v_ref = refs[pos]
            pos += 1
        i_ref = None
        if has_init:
            i_ref = refs[pos]
            pos += 1
        o_ref = refs[pos]
        pos += 1
        bs_ref = None
        if bias_sum:
            bs_ref = refs[pos]
            pos += 1
        acc_ref = refs[pos]
        t = pl.program_id(0)

        @pl.when(t == 0)
        def _():
            acc_ref[...] = i_ref[...] if has_init else jnp.zeros_like(acc_ref)
            if bias_sum:
                bs_ref[...] = jnp.zeros_like(bs_ref)

        lv, rv = _cr_rows(l_ref, k), _cr_rows(r_ref, n)
        if lhs_kind == "mod":
            lv = (lv * v_ref[0:1, :] + v_ref[1:2, :]).astype(BF)
        if rhs_kind == "scaled":
            rv = (rv.astype(F32) * v_ref[0:1, :]).astype(BF)
        acc_ref[...] += _dot_tn(lv, rv)
        if bias_sum:
            bs_ref[0:1, :] += jnp.sum(rv.astype(F32), axis=0, keepdims=True)

        @pl.when(t == nt - 1)
        def _():
            o_ref[...] = acc_ref[...].astype(BF)

    in_specs, args = [_cr_row_spec(tj, k), _cr_row_spec(tj, n)], [lhs, rhs]
    if vec is not None:
        in_specs.append(_full(vec.shape))
        args.append(vec)
    o_spec = _full((k, n))
    if has_init:
        in_specs.append(o_spec)
        args.append(init)
    out_specs, out_shape = [o_spec], [jax.ShapeDtypeStruct((k, n), BF)]
    if bias_sum:
        out_specs.append(_full((8, n)))
        out_shape.append(jax.ShapeDtypeStruct((8, n), F32))
    res = pl.pallas_call(
        body, name=name, grid=(nt,), in_specs=in_specs, out_specs=out_specs, out_shape=out_shape,
        scratch_shapes=[pltpu.VMEM((k, n), F32)], compiler_params=_cparams(),
    )(*args)
    return res if bias_sum else res[0]


GT_ROWS = CHUNK * S5_P
ZG_W = 2 * 2 * S5_N
PAIR_W = 2 * ZG_W
GROUPS_PER_STEP = 4


def _inproj1_gt(xh_cr, a1, b1, wu_t, w_z, tag):
    j, d16 = xh_cr.shape
    d = d16 // CHUNK
    e = wu_t.shape[0]
    g = e // S5_P
    tj = _cr_tile(j, 256)

    def body(x_ref, a_ref, b_ref, wu_hbm, wz_hbm, u_ref, z_ref, wu_ref, wz_ref):
        @pl.when(jnp.logical_and(pl.program_id(0) == 0, pl.program_id(1) == 0))
        def _():
            pltpu.sync_copy(wu_hbm, wu_ref)
            pltpu.sync_copy(wz_hbm, wz_ref)

        h = (x_ref[...] * a_ref[...] + b_ref[...]).astype(BF)
        u_ref[...] = _dot_nt(wu_ref[...], h).reshape(g, S5_P, tj).astype(BF)
        z_ref[...] = _dot(h, wz_ref[...]).astype(BF)

    return pl.pallas_call(
        body, name="l1_inproj_" + tag, grid=(j // tj, CHUNK),
        in_specs=[pl.BlockSpec((tj, d), lambda t, s: (t, s)), _full((1, d)), _full((1, d)), ANY, ANY],
        out_specs=[pl.BlockSpec((g, S5_P, tj), lambda t, s: (0, s, t)), pl.BlockSpec((tj, e), lambda t, s: (t, s))],
        out_shape=[jax.ShapeDtypeStruct((g, GT_ROWS, j), BF), jax.ShapeDtypeStruct((j, CHUNK * e), BF)],
        scratch_shapes=[pltpu.VMEM(wu_t.shape, BF), pltpu.VMEM(w_z.shape, BF)], compiler_params=_cparams(),
    )(xh_cr, a1, b1, wu_t, w_z)


def _gt_spec(j, gb=GROUPS_PER_STEP):
    return pl.BlockSpec((gb, GT_ROWS, j), lambda i: (i, 0, 0))


def _zg_spec(j, gb=GROUPS_PER_STEP):
    return pl.BlockSpec((j, gb * ZG_W), lambda i: (0, i))


def _w_spec(width, gb=GROUPS_PER_STEP):
    return pl.BlockSpec((gb, GT_ROWS, width), lambda i: (i, 0, 0))


def _pair_lanes(k):
    return slice((k // 2) * PAIR_W, (k // 2 + 1) * PAIR_W)


def _s5_z(ut_l, ut_c, bc):
    g, _, jl = ut_l.shape
    jc = ut_c.shape[2]
    gb = GROUPS_PER_STEP

    def body(ul_ref, uc_ref, bc_ref, zl_ref, zc_ref):
        for k in range(0, gb, 2):
            zl_ref[:, _pair_lanes(k)] = _dot_tn(ul_ref[k], bc_ref[k]) + _dot_tn(ul_ref[k + 1], bc_ref[k + 1])
            zc_ref[:, _pair_lanes(k)] = _dot_tn(uc_ref[k], bc_ref[k]) + _dot_tn(uc_ref[k + 1], bc_ref[k + 1])

    return pl.pallas_call(
        body, name="l1_s5_z", grid=(g // gb,), in_specs=[_gt_spec(jl), _gt_spec(jc), _w_spec(PAIR_W)],
        out_specs=[_zg_spec(jl), _zg_spec(jc)],
        out_shape=[jax.ShapeDtypeStruct((jl, g * ZG_W), F32), jax.ShapeDtypeStruct((jc, g * ZG_W), F32)],
        compiler_params=_cparams(),
    )(ut_l, ut_c, bc)


def _s5_y(ut_l, s_l, mt_t, cct):
    g, _, jl = ut_l.shape
    gb = GROUPS_PER_STEP

    def body(u_ref, s_ref, mt_ref, cc_ref, y_ref):
        for k in range(gb):
            s_k = s_ref[:, _pair_lanes(k)].astype(BF)
            y_ref[k] = (_dot(mt_ref[k], u_ref[k]) + _dot_nt(cc_ref[k], s_k)).astype(BF)

    return pl.pallas_call(
        body, name="l1_s5_y", grid=(g // gb,),
        in_specs=[_gt_spec(jl), _zg_spec(jl), _w_spec(GT_ROWS), _w_spec(PAIR_W)],
        out_specs=_gt_spec(jl), out_shape=jax.ShapeDtypeStruct((g, GT_ROWS, jl), BF), compiler_params=_cparams(),
    )(ut_l, s_l, mt_t, cct)


def _s5_ds(dyt_l, cct):
    g, _, jl = dyt_l.shape
    gb = GROUPS_PER_STEP

    def body(dy_ref, cc_ref, ds_ref):
        for k in range(0, gb, 2):
            ds_ref[:, _pair_lanes(k)] = _dot_tn(dy_ref[k], cc_ref[k]) + _dot_tn(dy_ref[k + 1], cc_ref[k + 1])

    return pl.pallas_call(
        body, name="l1_s5_ds", grid=(g // gb,), in_specs=[_gt_spec(jl), _w_spec(PAIR_W)], out_specs=_zg_spec(jl),
        out_shape=jax.ShapeDtypeStruct((jl, g * ZG_W), F32), compiler_params=_cparams(),
    )(dyt_l, cct)


def _s5_dx(dyt_l, dz_l, dz_c, mt, bc):
    g, _, jl = dyt_l.shape
    jc = dz_c.shape[0]
    gb = GROUPS_PER_STEP

    def body(dy_ref, dzl_ref, dzc_ref, mt_ref, bc_ref, dul_ref, duc_ref):
        for k in range(gb):
            dzl = dzl_ref[:, _pair_lanes(k)].astype(BF)
            dzc = dzc_ref[:, _pair_lanes(k)].astype(BF)
            dul_ref[k] = (_dot(mt_ref[k], dy_ref[k]) + _dot_nt(bc_ref[k], dzl)).astype(BF)
            duc_ref[k] = _dot_nt(bc_ref[k], dzc).astype(BF)

    return pl.pallas_call(
        body, name="l1_s5_dx", grid=(g // gb,),
        in_specs=[_gt_spec(jl), _zg_spec(jl), _zg_spec(jc), _w_spec(GT_ROWS), _w_spec(PAIR_W)],
        out_specs=[_gt_spec(jl), _gt_spec(jc)],
        out_shape=[jax.ShapeDtypeStruct((g, GT_ROWS, jl), BF), jax.ShapeDtypeStruct((g, GT_ROWS, jc), BF)],
        compiler_params=_cparams(),
    )(dyt_l, dz_l, dz_c, mt, bc)


def _s5_dw(ut_l, ut_c, dyt_l, dz_l, dz_c, s_l):
    g, _, jl = ut_l.shape
    jc = ut_c.shape[2]
    gb = GROUPS_PER_STEP

    def body(ul_ref, uc_ref, dy_ref, dzl_ref, dzc_ref, s_ref, dmt_ref, dbc_ref, dcc_ref):
        for k in range(gb):
            lanes = _pair_lanes(k)
            dmt_ref[k] = _dot_nt(ul_ref[k], dy_ref[k])
            dbc_ref[k] = (_dot(ul_ref[k], dzl_ref[:, lanes].astype(BF))
                          + _dot(uc_ref[k], dzc_ref[:, lanes].astype(BF)))
            dcc_ref[k] = _dot(dy_ref[k], s_ref[:, lanes].astype(BF))

    sd_m = jax.ShapeDtypeStruct((g, GT_ROWS, GT_ROWS), F32)
    sd_p = jax.ShapeDtypeStruct((g, GT_ROWS, PAIR_W), F32)
    return pl.pallas_call(
        body, name="l1_s5_dw", grid=(g // gb,),
        in_specs=[_gt_spec(jl), _gt_spec(jc), _gt_spec(jl), _zg_spec(jl), _zg_spec(jc), _zg_spec(jl)],
        out_specs=[_w_spec(GT_ROWS), _w_spec(PAIR_W), _w_spec(PAIR_W)], out_shape=[sd_m, sd_p, sd_p],
        compiler_params=_cparams(),
    )(ut_l, ut_c, dyt_l, dz_l, dz_c, s_l)


def _scan_g(z_l, z_c, coef, chains, conj, s_l=None, s_c=None, name="l1_scan"):
    jl, w_all = z_l.shape
    jc = z_c.shape[0]
    gb = 2 * GROUPS_PER_STEP if w_all % (2 * GROUPS_PER_STEP * ZG_W) == 0 else GROUPS_PER_STEP
    wb = gb * ZG_W
    nch = wb // 256
    with_da = s_l is not None
    sign = -1.0 if conj else 1.0

    def body(*refs):
        zl_ref, zc_ref, cf_ref = refs[:3]
        k0 = 3
        if with_da:
            sl_ref, sc_ref = refs[3:5]
            k0 = 5
        ol_ref, oc_ref = refs[k0:k0 + 2]
        rowi = lax.broadcasted_iota(jnp.int32, (8, 128), 0)

        def lanes_of(ch):
            return slice(ch * 256, ch * 256 + 128), slice(ch * 256 + 128, (ch + 1) * 256)

        def coefs(ch, r0, nr):
            lr, li = lanes_of(ch)
            return cf_ref[r0:r0 + nr, lr], sign * cf_ref[r0:r0 + nr, li]

        def shift(v, sh, rev):
            if rev:
                return jnp.where(rowi < 8 - sh, pltpu.roll(v, 8 - sh, 0), 0.0)
            return jnp.where(rowi >= sh, pltpu.roll(v, sh, 0), 0.0)

        zero_row = jnp.zeros((1, 128), F32)
        zero_tile = jnp.zeros((8, 128), F32)
        carry = [zero_row] * (2 * nch)
        da = [zero_tile] * (2 * nch)
        for seg in range(len(chains[0])):
            which = chains[0][seg][0]
            assert chains[1][seg][0] == which
            revs = (chains[0][seg][1], chains[1][seg][1])
            src, dst = (zc_ref, oc_ref) if which == "c" else (zl_ref, ol_ref)
            sref = ((sc_ref if which == "c" else sl_ref) if with_da else None)
            ng = (jc if which == "c" else jl) // 8

            def step(it, st, src=src, dst=dst, sref=sref, ng=ng, revs=revs):
                carry_, da_ = list(st[:2 * nch]), list(st[2 * nch:])
                for ch in range(nch):
                    rev = revs[ch % 2]
                    lr, li = lanes_of(ch)
                    grp = (ng - 1 - it) if rev else it
                    off = pl.multiple_of(grp * 8, 8)
                    xr, xi = src[pl.ds(off, 8), lr], src[pl.ds(off, 8), li]
                    for sh, r0 in ((1, 0), (2, 1), (4, 2)):
                        ar, ai = coefs(ch, r0, 1)
                        sr, si = shift(xr, sh, rev), shift(xi, sh, rev)
                        xr, xi = xr + ar * sr - ai * si, xi + ar * si + ai * sr
                    tr, ti = coefs(ch, 16, 8) if rev else coefs(ch, 8, 8)
                    cr_, ci_ = carry_[2 * ch], carry_[2 * ch + 1]
                    ir = xr + tr * cr_ - ti * ci_
                    ii = xi + tr * ci_ + ti * cr_
                    if rev:
                        er = jnp.where(rowi == 7, cr_, pltpu.roll(ir, 7, 0))
                        ei = jnp.where(rowi == 7, ci_, pltpu.roll(ii, 7, 0))
                        carry_[2 * ch], carry_[2 * ch + 1] = ir[0:1], ii[0:1]
                    else:
                        er = jnp.where(rowi == 0, cr_, pltpu.roll(ir, 1, 0))
                        ei = jnp.where(rowi == 0, ci_, pltpu.roll(ii, 1, 0))
                        carry_[2 * ch], carry_[2 * ch + 1] = ir[7:8], ii[7:8]
                    dst[pl.ds(off, 8), lr] = er
                    dst[pl.ds(off, 8), li] = ei
                    if sref is not None:
                        s_r, s_i = sref[pl.ds(off, 8), lr], sref[pl.ds(off, 8), li]
                        da_[2 * ch] = da_[2 * ch] + s_r * er + s_i * ei
                        da_[2 * ch + 1] = da_[2 * ch + 1] + s_r * ei - s_i * er
                return (*carry_, *da_)

            st = lax.fori_loop(0, ng, step, (*carry, *da))
            carry, da = list(st[:2 * nch]), list(st[2 * nch:])
        if with_da:
            da_ref = refs[k0 + 2]
            for ch in range(nch):
                lr, li = lanes_of(ch)
                da_ref[:, lr] = da[2 * ch]
                da_ref[:, li] = da[2 * ch + 1]

    in_specs = [_zg_spec(jl, gb), _zg_spec(jc, gb), pl.BlockSpec((24, wb), lambda i: (0, i))]
    args = [z_l, z_c, coef]
    out_specs = [_zg_spec(jl, gb), _zg_spec(jc, gb)]
    out_shape = [jax.ShapeDtypeStruct(z_l.shape, F32), jax.ShapeDtypeStruct(z_c.shape, F32)]
    if with_da:
        in_specs += [_zg_spec(jl, gb), _zg_spec(jc, gb)]
        args += [s_l, s_c]
        out_specs.append(pl.BlockSpec((8, wb), lambda i: (0, i)))
        out_shape.append(jax.ShapeDtypeStruct((8, w_all), F32))
    return pl.pallas_call(body, name=name, grid=(w_all // wb,), in_specs=in_specs, out_specs=out_specs,
                          out_shape=out_shape, compiler_params=_cparams())(*args)


def _gt_tok_spec(g, tj):
    return pl.BlockSpec((g, S5_P, tj), lambda t, s: (0, s, t))


def _glu_fwd_gt(yt, z_cr, w_glu, b_glu):
    g, _, j = yt.shape
    e = g * S5_P
    tj = _cr_tile(j)

    def body(y_ref, z_ref, w_hbm, b_ref, o_ref, sg_ref, w_ref):
        @pl.when(jnp.logical_and(pl.program_id(0) == 0, pl.program_id(1) == 0))
        def _():
            pltpu.sync_copy(w_hbm, w_ref)

        y = jnp.transpose(y_ref[...].reshape(e, tj).astype(F32))
        gl = _gelu_parts(y)[0]
        sg = _sigmoid(_dot(gl.astype(BF), w_ref[...]) + b_ref[...])
        z = z_ref[...].astype(F32)
        o_ref[...] = (gl * sg * (z * _sigmoid(z))).astype(BF)
        sg_ref[...] = sg.astype(BF)

    tok = pl.BlockSpec((tj, e), lambda t, s: (t, s))
    return pl.pallas_call(
        body, name="l1_glu_fwd", grid=(j // tj, CHUNK),
        in_specs=[_gt_tok_spec(g, tj), tok, ANY, _full((1, e))], out_specs=[tok, tok],
        out_shape=[jax.ShapeDtypeStruct((j, CHUNK * e), BF), jax.ShapeDtypeStruct((j, CHUNK * e), BF)],
        scratch_shapes=[pltpu.VMEM(w_glu.shape, BF)], compiler_params=_cparams(),
    )(yt, z_cr, w_glu, b_glu)


def _glu_bwd_gt(dr_cr, gt1, w_out, w_glu, yt, z_cr, sg_cr):
    g, _, j = yt.shape
    e, d = w_out.shape
    tj = _cr_tile(j)

    def body(dr_ref, g_ref, wo_hbm, wg_hbm, y_ref, z_ref, sg_ref, dz_ref, dt_ref, dy_ref, wo_ref, wg_ref):
        @pl.when(jnp.logical_and(pl.program_id(0) == 0, pl.program_id(1) == 0))
        def _():
            pltpu.sync_copy(wo_hbm, wo_ref)
            pltpu.sync_copy(wg_hbm, wg_ref)

        do = (dr_ref[...].astype(F32) * g_ref[...]).astype(BF)
        dw = _dot_nt(do, wo_ref[...])
        y = jnp.transpose(y_ref[...].reshape(e, tj).astype(F32))
        gl, dgel = _gelu_parts(y)
        z = z_ref[...].astype(F32)
        sz = _sigmoid(z)
        sg = sg_ref[...].astype(F32)
        dg2 = dw * (z * sz)
        dz_ref[...] = (dw * gl * sg * (sz * (1.0 + z * (1.0 - sz)))).astype(BF)
        dt = (dg2 * gl * sg * (1.0 - sg)).astype(BF)
        dt_ref[...] = dt
        dy = (dg2 * sg + _dot_nt(dt, wg_ref[...])) * dgel
        dy_ref[...] = jnp.transpose(dy).reshape(g, S5_P, tj).astype(BF)

    tok_e = pl.BlockSpec((tj, e), lambda t, s: (t, s))
    return pl.pallas_call(
        body, name="l1_glu_bwd", grid=(j // tj, CHUNK),
        in_specs=[pl.BlockSpec((tj, d), lambda t, s: (t, s)), _full((1, d)), ANY, ANY, _gt_tok_spec(g, tj), tok_e, tok_e],
        out_specs=[tok_e, tok_e, _gt_tok_spec(g, tj)],
        out_shape=[jax.ShapeDtypeStruct((j, CHUNK * e), BF), jax.ShapeDtypeStruct((j, CHUNK * e), BF),
                   jax.ShapeDtypeStruct((g, GT_ROWS, j), BF)],
        scratch_shapes=[pltpu.VMEM(w_out.shape, BF), pltpu.VMEM(w_glu.shape, BF)], compiler_params=_cparams(),
    )(dr_cr, gt1, w_out, w_glu, yt, z_cr, sg_cr)


def _bwd_inproj1_gt(dut, dz_cr, wu_t, w_z, xh_cr, rs_cr, dr2_cr, vecs, tag):
    g, _, j = dut.shape
    e, d = wu_t.shape
    tj = _cr_tile(j)

    def body(du_ref, dz_ref, wu_hbm, wz_hbm, xh_ref, rs_ref, dr2_ref, v_ref, dr1_ref, acc_ref, wu_ref, wz_ref):
        @pl.when(jnp.logical_and(pl.program_id(0) == 0, pl.program_id(1) == 0))
        def _():
            pltpu.sync_copy(wu_hbm, wu_ref)
            pltpu.sync_copy(wz_hbm, wz_ref)
            acc_ref[...] = jnp.zeros_like(acc_ref)

        dh = _dot_tn(du_ref[...].reshape(e, tj), wu_ref[...]) + _dot_nt(dz_ref[...], wz_ref[...])
        xh = xh_ref[...]
        x1 = xh * v_ref[0:1, :] + v_ref[1:2, :]
        dx1 = DN_ALPHA * dr2_ref[...].astype(F32) + dh * v_ref[2:3, :]
        dxh = dx1 * v_ref[0:1, :]
        rstd = rs_ref[:, 0:1]
        dr1 = rstd * (dxh - jnp.mean(dxh, axis=-1, keepdims=True) - xh * jnp.mean(dxh * xh, axis=-1, keepdims=True))
        dr1_ref[...] = dr1.astype(BF)
        acc_ref[0:1, :] += jnp.sum(dh * x1, axis=0, keepdims=True)
        acc_ref[1:2, :] += jnp.sum(dh, axis=0, keepdims=True)
        acc_ref[2:3, :] += jnp.sum(dx1 * xh, axis=0, keepdims=True)
        acc_ref[3:4, :] += jnp.sum(dx1, axis=0, keepdims=True)

    tok_d = pl.BlockSpec((tj, d), lambda t, s: (t, s))
    return pl.pallas_call(
        body, name="l1_bwd_inproj_" + tag, grid=(j // tj, CHUNK),
        in_specs=[_gt_tok_spec(g, tj), pl.BlockSpec((tj, e), lambda t, s: (t, s)), ANY, ANY, tok_d,
                  pl.BlockSpec((tj, 128), lambda t, s: (t, s)), tok_d, _full((8, d))],
        out_specs=[tok_d, _full((8, d))],
        out_shape=[jax.ShapeDtypeStruct((j, CHUNK * d), BF), jax.ShapeDtypeStruct((8, d), F32)],
        scratch_shapes=[pltpu.VMEM(wu_t.shape, BF), pltpu.VMEM(w_z.shape, BF)], compiler_params=_cparams(),
    )(dut, dz_cr, wu_t, w_z, xh_cr, rs_cr, dr2_cr, vecs)


def _dw_gt(lhs_gt, rhs_cr, lhs_gelu, vec, bias_sum, init, out_dtype, name, xch=None):
    g, _, j = lhs_gt.shape
    e = g * S5_P
    n = rhs_cr.shape[1] // CHUNK
    tj = _cr_tile(j, 512 if j % 512 == 0 else 256)
    nh = 2 if e * n * 4 > (8 << 20) else 1
    tn = n // nh
    nt = j // tj
    has_init = init is not None

    def body(*refs):
        refs = list(refs)
        l_ref, r_ref = refs[0], refs[1]
        pos = 2
        v_ref = i_ref = bs_ref = None
        if vec is not None:
            v_ref = refs[pos]
            pos += 1
        if has_init:
            i_ref = refs[pos]
            pos += 1
        o_ref = refs[pos]
        pos += 1
        if bias_sum:
            bs_ref = refs[pos]
            pos += 1
        acc_ref = refs[pos]
        t, s = pl.program_id(1), pl.program_id(2)

        @pl.when(jnp.logical_and(t == 0, s == 0))
        def _():
            acc_ref[...] = i_ref[...] if has_init else jnp.zeros_like(acc_ref)
            if bias_sum:
                bs_ref[...] = jnp.zeros_like(bs_ref)

        lv = l_ref[...].reshape(e, tj)
        if lhs_gelu:
            lv = _gelu_parts(lv.astype(F32))[0].astype(BF)
        if vec is not None:
            rv = (r_ref[...] * v_ref[0:1, :] + v_ref[1:2, :]).astype(BF)
        else:
            rv = r_ref[...]
        acc_ref[...] += _dot(lv, rv)
        if bias_sum:
            bs_ref[0:1, :] += jnp.sum(rv.astype(F32), axis=0, keepdims=True)

        @pl.when(jnp.logical_and(t == nt - 1, s == CHUNK - 1))
        def _():
            o_ref[...] = acc_ref[...].astype(out_dtype)

    in_specs = [pl.BlockSpec((g, S5_P, tj), lambda h, t, s: (0, s, t)),
                pl.BlockSpec((tj, tn), lambda h, t, s: (t, s * nh + h))]
    args = [lhs_gt, rhs_cr]
    if vec is not None:
        in_specs.append(_full(vec.shape))
        args.append(vec)
    o_spec = pl.BlockSpec((e, tn), lambda h, t, s: (0, h))
    if has_init:
        in_specs.append(o_spec)
        args.append(init)
    out_specs, out_shape = [o_spec], [jax.ShapeDtypeStruct((e, n), out_dtype)]
    if bias_sum:
        out_specs.append(pl.BlockSpec((8, tn), lambda h, t, s: (0, h)))
        out_shape.append(jax.ShapeDtypeStruct((8, n), F32))
    res, extra = _hosted_call(body, xch, grid=(nh, nt, CHUNK), in_specs=in_specs, out_specs=out_specs,
                              out_shape=out_shape, scratch=[pltpu.VMEM((e, tn), F32)], args=args, name=name)
    if xch is not None:
        return (*res, extra) if bias_sum else (res[0], extra)
    return res if bias_sum else res[0]


def _scan_coef_g(lam_re, lam_im, log_step):
    g = lam_re.shape[1]
    ms = jnp.array([1, 2, 4, 0, 0, 0, 0, 0] + list(range(1, 9)) + list(range(8, 0, -1)), F32) * CHUNK
    dt = jnp.exp(log_step)[..., None]
    mag = jnp.exp(ms.reshape(-1, 1, 1, 1) * (lam_re * dt)[None])
    ang = ms.reshape(-1, 1, 1, 1) * (lam_im * dt)[None]
    cr, ci = mag * jnp.cos(ang), mag * jnp.sin(ang)
    both = jnp.stack([cr, ci], axis=2).reshape(24, 2, 2, g // 2, 2, S5_N)
    return both.transpose(0, 3, 1, 2, 4, 5).reshape(24, g * ZG_W)


def _s5_small(lam_re, lam_im, log_step, b_re, b_im, c_re, c_im, d_skip):
    g = lam_re.shape[1]
    t, p = CHUNK, S5_P
    dt = jnp.exp(log_step)[..., None]
    ks = jnp.arange(t + 1, dtype=F32).reshape(t + 1, 1, 1, 1)
    mag = jnp.exp(ks * (lam_re * dt)[None])
    ang = ks * (lam_im * dt)[None]
    pr, pi = mag * jnp.cos(ang), mag * jnp.sin(ang)
    ar, ai = pr[1], pi[1]
    qr, qi = ar - 1.0, ai
    den = lam_re * lam_re + lam_im * lam_im
    fr = (qr * lam_re + qi * lam_im) / den
    fi = (qi * lam_re - qr * lam_im) / den
    bt_re, bt_im = b_re.transpose(0, 1, 3, 2), b_im.transpose(0, 1, 3, 2)
    bbr = fr[:, :, None, :] * bt_re - fi[:, :, None, :] * bt_im
    bbi = fr[:, :, None, :] * bt_im + fi[:, :, None, :] * bt_re
    lay = lambda a_r, a_i: jnp.stack([a_r, a_i], axis=0).transpose(3, 2, 0, 1, 4)
    by_dir = lambda a, f0, f1: jnp.stack([f0(a[:, 0]), f1(a[:, 1])], axis=1)
    rev = lambda a: jnp.flip(a, axis=0)
    same = lambda a: a
    pwb = lay(by_dir(pr[:t], rev, same), by_dir(pi[:t], rev, same))
    pwc = lay(by_dir(pr[1:], same, rev), by_dir(pi[1:], same, rev))
    bb = jnp.stack([bbr, bbi], axis=0).transpose(2, 1, 0, 3, 4)
    cc = jnp.stack([c_re, c_im], axis=0).transpose(2, 1, 0, 3, 4)
    dmat = jnp.eye(p, dtype=F32)[None] * d_skip.reshape(g, p)[:, :, None]
    return pwb, pwc, bb, cc, dmat, pr[t], pi[t]


def _pair_cols(r, ri, g2):
    c0 = (r * 2 + ri) * 128 + g2 * S5_N
    return slice(c0, c0 + S5_N)


def _rows_rep(a):
    return jnp.broadcast_to(a[:, None, :], (CHUNK, S5_P, a.shape[-1])).reshape(GT_ROWS, a.shape[-1])


def _rows_tile(a):
    return jnp.broadcast_to(a[None], (CHUNK, S5_P, a.shape[-1])).reshape(GT_ROWS, a.shape[-1])


def _sum_blocks(a):
    return jnp.sum(a.reshape(CHUNK, S5_P, a.shape[-1]), axis=0)


def _sum_in_blocks(a):
    return jnp.sum(a.reshape(CHUNK, S5_P, a.shape[-1]), axis=1)


def _ab_rows(pwb_ref, bb_ref, k, r):
    prs, pis = _rows_rep(pwb_ref[k, r, 0]), _rows_rep(pwb_ref[k, r, 1])
    bbr, bbi = _rows_tile(bb_ref[k, r, 0]), _rows_tile(bb_ref[k, r, 1])
    return prs * bbr - pis * bbi, prs * bbi + pis * bbr, prs, pis, bbr, bbi


def _s5_weights_fwd(pwb, pwc, bb, cc, dmat):
    g = pwb.shape[0]
    gb = GROUPS_PER_STEP
    hp = lax.Precision.HIGHEST

    def body(pwb_ref, pwc_ref, bb_ref, cc_ref, dm_ref, mt_ref, mtt_ref, bc_ref, cct_ref):
        zeros = jnp.zeros((GT_ROWS, S5_N), BF)
        nt = (((1,), (1,)), ((), ()))
        for k in range(gb):
            g2 = k % 2
            kds = []
            for r in range(2):
                for ri in range(2):
                    bc_ref[k, :, _pair_cols(r, ri, 1 - g2)] = zeros
                    cct_ref[k, :, _pair_cols(r, ri, 1 - g2)] = zeros
                abr, abi = _ab_rows(pwb_ref, bb_ref, k, r)[:2]
                bc_ref[k, :, _pair_cols(r, 0, g2)] = abr.astype(BF)
                bc_ref[k, :, _pair_cols(r, 1, g2)] = abi.astype(BF)
                cr, ci = cc_ref[k, r, 0], cc_ref[k, r, 1]
                crt, cit = _rows_tile(cr), _rows_tile(ci)
                prt, pit = _rows_rep(pwc_ref[k, r, 0]), _rows_rep(pwc_ref[k, r, 1])
                cct_ref[k, :, _pair_cols(r, 0, g2)] = (crt * prt - cit * pit).astype(BF)
                cct_ref[k, :, _pair_cols(r, 1, g2)] = (-(crt * pit + cit * prt)).astype(BF)
                kds.append(lax.dot_general(abr, cr, nt, precision=hp, preferred_element_type=F32)
                           - lax.dot_general(abi, ci, nt, precision=hp, preferred_element_type=F32))
            blk = lambda a, s: a[s * S5_P:(s + 1) * S5_P]
            last = CHUNK - 1
            pieces = [blk(kds[1], last - i) for i in range(last)]
            pieces.append(blk(kds[0], last) + blk(kds[1], 0) + dm_ref[k])
            pieces += [blk(kds[0], last - d) for d in range(1, CHUNK)]
            qrow = jnp.concatenate(pieces, axis=1)
            mt = jnp.concatenate([qrow[:, (last - s) * S5_P:(last - s) * S5_P + GT_ROWS] for s in range(CHUNK)], axis=0)
            mt_ref[k] = mt.astype(BF)
            mtt_ref[k] = jnp.transpose(mt).astype(BF)

    small = lambda a: pl.BlockSpec((gb, *a.shape[1:]), lambda i: (i,) + (0,) * (a.ndim - 1))
    return pl.pallas_call(
        body, name="l1_s5_weights", grid=(g // gb,),
        in_specs=[small(pwb), small(pwc), small(bb), small(cc), small(dmat)],
        out_specs=[_w_spec(GT_ROWS), _w_spec(GT_ROWS), _w_spec(PAIR_W), _w_spec(PAIR_W)],
        out_shape=[jax.ShapeDtypeStruct((g, GT_ROWS, GT_ROWS), BF), jax.ShapeDtypeStruct((g, GT_ROWS, GT_ROWS), BF),
                   jax.ShapeDtypeStruct((g, GT_ROWS, PAIR_W), BF), jax.ShapeDtypeStruct((g, GT_ROWS, PAIR_W), BF)],
        compiler_params=_cparams(),
    )(pwb, pwc, bb, cc, dmat)


def _s5_weights_bwd(pwb, pwc, bb, cc, d_mt, d_bc, d_cct):
    g = pwb.shape[0]
    gb = GROUPS_PER_STEP
    hp = lax.Precision.HIGHEST

    def body(pwb_ref, pwc_ref, bb_ref, cc_ref, dmt_ref, dbc_ref, dcc_ref, dpwb_ref, dpwc_ref, dbb_ref, dccp_ref, ddm_ref):
        tn = (((0,), (0,)), ((), ()))
        nn = (((1,), (0,)), ((), ()))
        last = CHUNK - 1
        for k in range(gb):
            g2 = k % 2
            dq = None
            for s in range(CHUNK):
                parts = [dmt_ref[k, s * S5_P:(s + 1) * S5_P, :]]
                if s < last:
                    parts.insert(0, jnp.zeros((S5_P, (last - s) * S5_P), F32))
                if s > 0:
                    parts.append(jnp.zeros((S5_P, s * S5_P), F32))
                padded = jnp.concatenate(parts, axis=1) if len(parts) > 1 else parts[0]
                dq = padded if dq is None else dq + padded
            dblk = lambda d: dq[:, (last + d) * S5_P:(CHUNK + d) * S5_P]
            ddm_ref[k] = dblk(0)
            dkds = [jnp.concatenate([dblk(last - s) for s in range(CHUNK)], axis=0),
                    jnp.concatenate([dblk(-s) for s in range(CHUNK)], axis=0)]
            for r in range(2):
                abr, abi, prs, pis, bbr, bbi = _ab_rows(pwb_ref, bb_ref, k, r)
                cr, ci = cc_ref[k, r, 0], cc_ref[k, r, 1]
                dcr = lax.dot_general(dkds[r], abr, tn, precision=hp, preferred_element_type=F32)
                dci = -lax.dot_general(dkds[r], abi, tn, precision=hp, preferred_element_type=F32)
                dabr = (lax.dot_general(dkds[r], cr, nn, precision=hp, preferred_element_type=F32)
                        + dbc_ref[k, :, _pair_cols(r, 0, g2)])
                dabi = (-lax.dot_general(dkds[r], ci, nn, precision=hp, preferred_element_type=F32)
                        + dbc_ref[k, :, _pair_cols(r, 1, g2)])
                dbb_ref[k, r, 0] = _sum_blocks(prs * dabr + pis * dabi)
                dbb_ref[k, r, 1] = _sum_blocks(prs * dabi - pis * dabr)
                dpwb_ref[k, r, 0] = _sum_in_blocks(dabr * bbr + dabi * bbi)
                dpwb_ref[k, r, 1] = _sum_in_blocks(dabi * bbr - dabr * bbi)
                crt, cit = _rows_tile(cr), _rows_tile(ci)
                prt, pit = _rows_rep(pwc_ref[k, r, 0]), _rows_rep(pwc_ref[k, r, 1])
                d_re = dcc_ref[k, :, _pair_cols(r, 0, g2)]
                d_im = dcc_ref[k, :, _pair_cols(r, 1, g2)]
                dccp_ref[k, r, 0] = dcr + _sum_blocks(d_re * prt - d_im * pit)
                dccp_ref[k, r, 1] = dci - _sum_blocks(d_re * pit + d_im * prt)
                dpwc_ref[k, r, 0] = _sum_in_blocks(d_re * crt - d_im * cit)
                dpwc_ref[k, r, 1] = -_sum_in_blocks(d_re * cit + d_im * crt)

    small = lambda a: pl.BlockSpec((gb, *a.shape[1:]), lambda i: (i,) + (0,) * (a.ndim - 1))
    dmat_sds = jax.ShapeDtypeStruct((g, S5_P, S5_P), F32)
    return pl.pallas_call(
        body, name="l1_s5_weights_bwd", grid=(g // gb,),
        in_specs=[small(pwb), small(pwc), small(bb), small(cc), _w_spec(GT_ROWS), _w_spec(PAIR_W), _w_spec(PAIR_W)],
        out_specs=[small(pwb), small(pwc), small(bb), small(cc), small(dmat_sds)],
        out_shape=[jax.ShapeDtypeStruct(pwb.shape, F32), jax.ShapeDtypeStruct(pwc.shape, F32),
                   jax.ShapeDtypeStruct(bb.shape, F32), jax.ShapeDtypeStruct(cc.shape, F32), dmat_sds],
        compiler_params=_cparams(),
    )(pwb, pwc, bb, cc, d_mt, d_bc, d_cct)


def _from_cr(a, c):
    return a.reshape(a.shape[0] * CHUNK, c)


def _pad8(v):
    return jnp.concatenate([v, jnp.zeros((8 - v.shape[0], v.shape[1]), v.dtype)], axis=0)


def _local_step(x, c, ctx, c_ctx, loss_target, w, late=None, scatter=False, mod=None):
    l, d = x.shape
    lc = ctx.shape[0]
    tm = min(256, lc)
    assert lc == tm and l % tm == 0 and tm % GRID_W == 0 and (tm & (tm - 1)) == 0
    nl = l // tm

    own_mod = mod is None
    if own_mod:
        c8 = _pad8(jnp.stack([c, c_ctx]))
        mod = _ada_fwd(c8, w["ada_w"], w["ada_b"])
    sh = mod[:, :2, :d]
    sc = mod[:, :2, d:2 * d]
    gt = mod[:, :2, 2 * d:]
    ln_g, ln_b = w["ln_g"], w["ln_b"]

    a0, b0 = 1.0 + sc[0], sh[0]
    xch = _Exchange("gather2", [late[n][0] for n in late], [late[n][1] for n in late]) if late else None
    p42, tgt_cr, got = _inproj0(x, ctx, a0, b0, w["conv_w_in"], loss_target, tm, xch)
    if late:
        w = dict(w, **dict(zip(late, got)))
    e = w["conv_w_out"].shape[0]
    half = e // 2
    cw = w["conv_w"].reshape(3, 2, half)
    q3 = _conv_fwd(p42, cw, nl, tm, half)
    xh1_l, xh1_c, rs1_l, rs1_c, fx = _outproj_ln0(q3, w["conv_w_out"], x, ctx, gt[0], tm)
    jl, jc = l // CHUNK, lc // CHUNK

    g0, bb0 = ln_g[0:1], ln_b[0:1]
    a1 = g0 * (1.0 + sc[1])
    b1 = bb0 * (1.0 + sc[1]) + sh[1]
    wu_t = w["ssm_w_in"][:, :e].T
    w_z = w["ssm_w_in"][:, e:]
    ut_l, z_l = _inproj1_gt(xh1_l, a1[0:1], b1[0:1], wu_t, w_z, "lat")
    ut_c, _ = _inproj1_gt(xh1_c, a1[1:2], b1[1:2], wu_t, w_z, "ctx")
    s5 = (w["ssm_lam_re"], w["ssm_lam_im"], w["ssm_log_step"], w["ssm_b_re"], w["ssm_b_im"],
          w["ssm_c_re"], w["ssm_c_im"], w["ssm_d"])
    (pwb, pwc, bbw, ccw, dmat, _, _), s5_vjp = jax.vjp(_s5_small, *s5)
    mt_b, mtt_b, bc_b, cct_b = _s5_weights_fwd(pwb, pwc, bbw, ccw, dmat)
    coef = lax.stop_gradient(_scan_coef_g(*s5[:3]))
    zz_l, zz_c = _s5_z(ut_l, ut_c, bc_b)
    fwd_chains = ((("c", False), ("l", False)), (("c", True), ("l", True)))
    st_l, st_c = _scan_g(zz_l, zz_c, coef, fwd_chains, False, name="l1_scan_fwd")
    yt = _s5_y(ut_l, st_l, mtt_b, cct_b)
    b_glu = w["ssm_b_glu"].reshape(1, e)
    w_cr, sg_cr = _glu_fwd_gt(yt, z_l, w["ssm_w_glu"], b_glu)
    vec_f = _pad8(jnp.concatenate([g0, bb0, gt[1][0:1], ln_g[1:2], ln_b[1:2]], axis=0))
    dr2, acc_f = _final(w_cr, w["ssm_w_out"], xh1_l, tgt_cr, vec_f)
    loss = jnp.sum(acc_f[3])

    gt1 = gt[1][0:1]
    dz_l, dt_l, dyt = _glu_bwd_gt(dr2, gt1, w["ssm_w_out"], w["ssm_w_glu"], yt, z_l, sg_cr)
    g_w_out = _dw_cr(w_cr, dr2, "cr", "scaled", gt1, False, None, "l1_dw_out")
    ds_l = _s5_ds(dyt, cct_b)
    bwd_chains = ((("l", True), ("c", True)), (("l", False), ("c", False)))
    dzz_l, dzz_c, da = _scan_g(ds_l, jnp.zeros_like(zz_c), coef, bwd_chains, True, st_l, st_c, name="l1_scan_bwd")
    dut_l, dut_c = _s5_dx(dyt, dzz_l, dzz_c, mt_b, bc_b)
    d_mt, d_bc, d_cct = _s5_dw(ut_l, ut_c, dyt, dzz_l, dzz_c, st_l)
    n_g = e // S5_P
    da = jnp.sum(da, axis=0).reshape(n_g // 2, 2, 2, 2, S5_N).transpose(1, 2, 0, 3, 4)
    da = da.reshape(2, 2, n_g, S5_N)
    d_pwb, d_pwc, d_bb, d_ccp, d_dm = _s5_weights_bwd(pwb, pwc, bbw, ccw, d_mt, d_bc, d_cct)
    g_s5 = s5_vjp((d_pwb, d_pwc, d_bb, d_ccp, d_dm, da[:, 0], da[:, 1]))

    vec_l = _pad8(jnp.concatenate([g0, bb0, 1.0 + sc[1][0:1]], axis=0))
    vec_c = _pad8(jnp.concatenate([g0, bb0, 1.0 + sc[1][1:2]], axis=0))
    dr1_l, acc_l = _bwd_inproj1_gt(dut_l, dz_l, wu_t, w_z, xh1_l, rs1_l, dr2, vec_l, "lat")
    dr1_c, acc_c = _bwd_inproj1_gt(dut_c, jnp.zeros((jc, CHUNK * e), BF), wu_t, w_z, xh1_c, rs1_c,
                                   jnp.zeros((jc, CHUNK * d), BF), vec_c, "ctx")
    mod_l = jnp.concatenate([a1[0:1], b1[0:1]], axis=0)
    mod_c = jnp.concatenate([a1[1:2], b1[1:2]], axis=0)
    g_ut_c = _dw_gt(dut_c, xh1_c, False, mod_c, False, None, F32, "l1_dw_in_u_ctx")
    g_ut = _dw_gt(dut_l, xh1_l, False, mod_l, False, g_ut_c, BF, "l1_dw_in_u")
    g_in_z = _dw_cr(xh1_l, dz_l, "mod", "cr", mod_l, False, None, "l1_dw_in_z")
    g_w_in1 = jnp.concatenate([g_ut.T, g_in_z], axis=1)

    dr1_ln, dr1_cn = _from_cr(dr1_l, d), _from_cr(dr1_c, d)
    dq3, acc_g0 = _bwd_outproj0(dr1_ln, dr1_cn, gt[0], w["conv_w_out"], fx, tm)
    def carried(names, parts):
        return _Exchange("scatter", parts, [BIG[n] for n in names]) if scatter else None

    dp42, dcw, grad_x, acc_0, recv1 = _conv_bwd_inproj0(
        dq3, p42, cw, w["conv_w_in"], x, ctx, dr1_ln, dr1_cn, a0, nl, tm, carried(["ssm_w_in", "ssm_w_out"], [g_w_in1, g_w_out]))
    g_w_in0 = _dw_inproj0(x, ctx, a0, b0, dp42, tm)
    res = _dw_gt(yt, dt_l, True, None, True, None, BF, "l1_dw_glu", carried(["conv_w_in"], [g_w_in0]))
    g_w_glu, bsum, recv2 = res if scatter else (*res, [])
    g_b_glu = bsum[0]
    g_w_out0, recv3 = _dw_outproj0(q3, dr1_ln, dr1_cn, gt[0], tm, carried(["ssm_w_glu"], [g_w_glu]))
    recv = dict(zip(["ssm_w_in", "ssm_w_out", "conv_w_in", "ssm_w_glu"], recv1 + recv2 + recv3))

    zero = jnp.zeros((d,), F32)
    dm0 = jnp.stack([jnp.concatenate([acc_0[2], acc_0[0], acc_g0[0]]), jnp.concatenate([acc_0[3], acc_0[1], acc_g0[1]])])
    dm1 = jnp.stack([jnp.concatenate([acc_l[1], acc_l[0], acc_f[2]]), jnp.concatenate([acc_c[1], acc_c[0], zero])])
    if own_mod:
        g_ada_w, dc8 = _ada_bwd(c8, w["ada_w"], jnp.stack([_pad8(dm0), _pad8(dm1)]), BF)
        g_mod = {"c_ctx": dc8[0, 1] + dc8[1, 1], "ada_w": g_ada_w,
                 "ada_b": jnp.stack([dm0[0] + dm0[1], dm1[0] + dm1[1]])}
    else:
        g_mod = {"mod": jnp.stack([dm0, dm1])}

    grads = {
        **g_mod,
        "ln_g": jnp.stack([acc_l[2] + acc_c[2], acc_f[0]]),
        "ln_b": jnp.stack([acc_l[3] + acc_c[3], acc_f[1]]),
        "conv_w_in": g_w_in0, "conv_w": dcw[:3].reshape(3, e), "conv_w_out": g_w_out0,
        "ssm_w_in": g_w_in1,
        "ssm_lam_re": g_s5[0], "ssm_lam_im": g_s5[1], "ssm_log_step": g_s5[2],
        "ssm_b_re": g_s5[3], "ssm_b_im": g_s5[4], "ssm_c_re": g_s5[5], "ssm_c_im": g_s5[6], "ssm_d": g_s5[7],
        "ssm_w_glu": g_w_glu, "ssm_b_glu": g_b_glu, "ssm_w_out": g_w_out,
    }
    for n in recv:
        del grads[n]
    return loss, grad_x, grads, recv


WEIGHTS = ["c_ctx", "ada_w", "ada_b", "ln_g", "ln_b", "conv_w_in", "conv_w", "conv_w_out", "ssm_w_in",
           "ssm_lam_re", "ssm_lam_im", "ssm_log_step", "ssm_b_re", "ssm_b_im", "ssm_c_re", "ssm_c_im",
           "ssm_d", "ssm_w_glu", "ssm_b_glu", "ssm_w_out"]
BIG = {"ada_w": 1, "conv_w_in": 1, "conv_w_out": 0, "ssm_w_in": 1, "ssm_w_glu": 0, "ssm_w_out": 0}
SMALL_SHARDED = ["conv_w", "ssm_d", "ssm_b_glu"]
REPLICATED = ["c_ctx", "ada_b", "ln_g", "ln_b", "ssm_lam_re", "ssm_lam_im", "ssm_log_step",
              "ssm_b_re", "ssm_b_im", "ssm_c_re", "ssm_c_im"]
NATIVE_SMALL = ["ssm_b_re", "ssm_b_im", "ssm_c_re", "ssm_c_im"]


def _view2d(name, a):
    return a.reshape(-1, a.shape[-1])


def kernel(x, c, ctx, c_ctx, ada_w, ada_b, ln_g, ln_b, conv_w_in, conv_w, conv_w_out, ssm_w_in, ssm_lam_re, ssm_lam_im, ssm_log_step, ssm_b_re, ssm_b_im, ssm_c_re, ssm_c_im, ssm_d, ssm_w_glu, ssm_b_glu, ssm_w_out, loss_target, m_c_ctx, m_ada_w, m_ada_b, m_ln_g, m_ln_b, m_conv_w_in, m_conv_w, m_conv_w_out, m_ssm_w_in, m_ssm_lam_re, m_ssm_lam_im, m_ssm_log_step, m_ssm_b_re, m_ssm_b_im, m_ssm_c_re, m_ssm_c_im, m_ssm_d, m_ssm_w_glu, m_ssm_b_glu, m_ssm_w_out, v_c_ctx, v_ada_w, v_ada_b, v_ln_g, v_ln_b, v_conv_w_in, v_conv_w, v_conv_w_out, v_ssm_w_in, v_ssm_lam_re, v_ssm_lam_im, v_ssm_log_step, v_ssm_b_re, v_ssm_b_im, v_ssm_c_re, v_ssm_c_im, v_ssm_d, v_ssm_w_glu, v_ssm_b_glu, v_ssm_w_out):
    args = locals()
    wt = {n: args[n] for n in WEIGHTS}
    mt = {n: args["m_" + n] for n in WEIGHTS}
    vt = {n: args["v_" + n] for n in WEIGHTS}

    me = 4 * lax.axis_index("x") + 2 * lax.axis_index("y") + lax.axis_index("c")
    d = x.shape[-1]
    d3 = 3 * d
    wa = d3 // N_DEV

    big_names = [n for n in BIG if n != "ada_w"]
    shard = {n: _view2d(n, wt[n]).astype(BF) for n in big_names}
    small = jnp.concatenate([wt["conv_w"][0], wt["ssm_d"], wt["ssm_b_glu"]], axis=0)
    small = jnp.concatenate([small, jnp.zeros((3, small.shape[1]), F32)], axis=0)
    w_in_full, small_full, c_all = _all_gather([shard["conv_w_in"], small, _pad8(c)], [1, 1, 0], "gather_weights", "gather2")
    late = {n: (shard[n], BIG[n]) for n in big_names if n != "conv_w_in"}
    c16 = jnp.concatenate([c_all[::8], c_ctx[None], jnp.zeros((16 - N_DEV - 1, d), F32)], axis=0)
    ada_w_b = ada_w.astype(BF)
    ada_b_mine = lax.dynamic_slice_in_dim(ada_b, me * wa, wa, axis=1)
    mod_part = _ada_fwd(c16, ada_w_b, ada_b_mine)
    mod_all = _all_gather([mod_part.reshape(32, wa)], [1], "gather_mod")[0].reshape(2, 16, d3)
    mod = jnp.stack([lax.dynamic_index_in_dim(mod_all, me, axis=1, keepdims=False), mod_all[:, N_DEV]], axis=1)
    w = {
        "ln_g": ln_g, "ln_b": ln_b, "conv_w_in": w_in_full, "conv_w": small_full[0:3],
        "ssm_lam_re": ssm_lam_re[0], "ssm_lam_im": ssm_lam_im[0],
        "ssm_log_step": ssm_log_step[0], "ssm_b_re": ssm_b_re[0], "ssm_b_im": ssm_b_im[0],
        "ssm_c_re": ssm_c_re[0], "ssm_c_im": ssm_c_im[0], "ssm_d": small_full[3], "ssm_b_glu": small_full[4],
    }

    loss, grad_x, g, recv_big = _local_step(x[0], c[0], ctx[0], c_ctx, loss_target[0], w, late, True, mod)

    dmod_all = _all_gather([_pad8(g["mod"].reshape(4, d3))], [0], "gather_dmod")[0].reshape(N_DEV, 8, d3)
    dmod_all = dmod_all[:, :4].reshape(N_DEV, 2, 2, d3)
    dm_ctx = dmod_all[0, :, 1]
    for p in range(1, N_DEV):
        dm_ctx = dm_ctx + dmod_all[p, :, 1]
    dm16 = jnp.concatenate([dmod_all[:, :, 0].transpose(1, 0, 2), dm_ctx[:, None], jnp.zeros((2, 16 - N_DEV - 1, d3), F32)], axis=1)
    g_ada_w, dc16 = _ada_bwd(c16, ada_w_b, lax.dynamic_slice_in_dim(dm16, me * wa, wa, axis=2), F32)
    g["c_ctx"] = dc16[0, N_DEV] + dc16[1, N_DEV]
    g_ada_b = jnp.sum(dm16, axis=1)

    blob_names = [n for n in REPLICATED if n != "ada_b"] + SMALL_SHARDED
    flat = jnp.concatenate([g[n].reshape(-1).astype(F32) for n in blob_names] + [loss.reshape(1)])
    nflat = flat.shape[0]
    rows = -(-nflat // (N_DEV * 128 * 8)) * 8
    flat = jnp.concatenate([flat, jnp.zeros((N_DEV * rows * 128 - nflat,), F32)]).reshape(N_DEV * rows, 128)
    last = [n for n in big_names if n not in recv_big]
    recv = _all_to_all([_view2d(n, g[n]) for n in last] + [flat], [BIG[n] for n in last] + [0], "scatter_grads")
    recv_big.update(zip(last, recv[:-1]))
    blob_sum = _sum_partials(recv[-1])
    blob = _all_gather([blob_sum], [0], "gather_small_grads", "gather2")[0].reshape(-1)
    small_g, off = {"ada_b": g_ada_b}, 0
    for n in blob_names:
        shape = wt[n].shape if n in REPLICATED else (*wt[n].shape[:-1], wt[n].shape[-1] * N_DEV)
        size = math.prod(shape)
        small_g[n] = blob[off:off + size].reshape(shape)
        off += size
    loss = blob[off]
    for n in SMALL_SHARDED:
        size = wt[n].shape[-1]
        small_g[n] = lax.dynamic_slice_in_dim(small_g[n], me * size, size, axis=small_g[n].ndim - 1)

    out_g, out_d, out_m, out_v = {}, {}, {}, {}
    recv_big["ada_w"] = _view2d("ada_w", g_ada_w)[None]
    for n in BIG:
        stack = recv_big[n]
        shp = wt[n].shape
        res = _adamw(stack, _view2d(n, wt[n]), _view2d(n, mt[n]), _view2d(n, vt[n]), "adamw_" + n)
        out_g[n], out_d[n], out_m[n], out_v[n] = [r.reshape(shp) for r in res]
    for n in NATIVE_SMALL:
        shp = wt[n].shape
        v2 = lambda a: a.reshape(-1, shp[-1])
        res = _adamw(v2(small_g.pop(n))[None], v2(wt[n]), v2(mt[n]), v2(vt[n]), "adamw_" + n)
        out_g[n], out_d[n], out_m[n], out_v[n] = [r.reshape(shp) for r in res]
    names = list(small_g)
    cat = lambda t: jnp.concatenate([t[n].reshape(-1) for n in names])
    gs, ws, ms, vs = cat(small_g), cat(wt), cat(mt), cat(vt)
    ns = gs.shape[0]
    rs = -(-ns // (128 * 512)) * 512
    padr = lambda a: jnp.concatenate([a, jnp.ones((rs * 128 - ns,), F32)]).reshape(rs, 128)
    res = _adamw(padr(gs)[None], padr(ws), padr(ms), padr(vs), "adamw_small")
    off = 0
    for n in names:
        size = math.prod(wt[n].shape)
        out_g[n], out_d[n], out_m[n], out_v[n] = [r.reshape(-1)[off:off + size].reshape(wt[n].shape) for r in res]
        off += size

    return (loss, grad_x[None], *[out_g[n] for n in WEIGHTS], *[out_d[n] for n in WEIGHTS],
            *[out_m[n] for n in WEIGHTS], *[out_v[n] for n in WEIGHTS])
```

```python
import math

import jax
import jax.numpy as jnp
from jax import lax
from jax.experimental import pallas as pl
from jax.experimental.pallas import tpu as pltpu

F32 = jnp.float32
BF = jnp.bfloat16
MESH = pl.DeviceIdType.MESH
N_DEV = 8

GRID_W = 64
CHUNK = 16
S5_P = 16
S5_N = 64
LN_EPS = 1e-5
DN_ALPHA = 4.0 ** 0.25
ADAM_LR, ADAM_B1, ADAM_B2, ADAM_EPS, ADAM_WD, ADAM_STEP = 1e-3, 0.9, 0.999, 1e-8, 0.01, 10
GELU_C0 = math.sqrt(2.0 / math.pi)
GELU_C1 = 0.044715
VMEM_MB = 52

ANY = pl.BlockSpec(memory_space=pl.ANY)


def _cparams():
    return pltpu.CompilerParams(vmem_limit_bytes=VMEM_MB << 20)


def _dot(a, b):
    return jnp.dot(a, b, preferred_element_type=F32)


def _dot_nt(a, b):
    return lax.dot_general(a, b, (((1,), (1,)), ((), ())), preferred_element_type=F32)


def _dot_tn(a, b):
    return lax.dot_general(a, b, (((0,), (0,)), ((), ())), preferred_element_type=F32)


def _sigmoid(x):
    return 1.0 / (1.0 + jnp.exp(-x))


def _gelu_parts(y):
    u = y * y
    th = jnp.tanh(y * (GELU_C0 + (GELU_C0 * GELU_C1) * u))
    hy = 0.5 * y
    g = hy + hy * th
    dg = (0.5 + 0.5 * th) + hy * (1.0 - th * th) * (GELU_C0 + (3.0 * GELU_C0 * GELU_C1) * u)
    return g, dg


def _full(shape):
    nd = len(shape)
    return pl.BlockSpec(shape, lambda *_: (0,) * nd)


def _mesh_pos():
    x, y, c = lax.axis_index("x"), lax.axis_index("y"), lax.axis_index("c")
    return x, y, c


def _peer(pos, k):
    x, y, c = pos
    px = 1 - x if (k >> 2) & 1 else x
    py = 1 - y if (k >> 1) & 1 else y
    pc = 1 - c if k & 1 else c
    return (px, py, pc), 4 * px + 2 * py + pc


def _shard_at(ref, axis, idx, n):
    if axis == 0:
        return ref.at[pl.ds(idx * n, n)]
    return ref.at[:, pl.ds(idx * n, n)]


class _Exchange:
    def __init__(self, kind, arrays, axes):
        self.kind, self.axes, self.n = kind, list(axes), len(arrays)
        self.arrays = list(arrays)
        self.out_shape = []
        for s, ax in zip(arrays, axes):
            shp = list(s.shape)
            if kind == "scatter":
                shp[ax] //= N_DEV
                self.out_shape.append(jax.ShapeDtypeStruct((N_DEV, *shp), s.dtype))
            else:
                shp[ax] *= N_DEV
                self.out_shape.append(jax.ShapeDtypeStruct(tuple(shp), s.dtype))
        self.scratch = [pltpu.SemaphoreType.DMA((self.n, N_DEV - 1)), pltpu.SemaphoreType.DMA((self.n, N_DEV - 1)),
                        pltpu.SemaphoreType.DMA((self.n,))]

    def _copies(self, ins, outs, sems):
        send_sems, recv_sems, local_sems = sems
        pos = _mesh_pos()
        x, y, c = pos
        me = 4 * x + 2 * y + c
        local, sends, chained, recvs = [], [], [], []
        for i in range(self.n):
            ax = self.axes[i]
            if self.kind == "scatter":
                size = ins[i].shape[ax] // N_DEV
                src = lambda idx, i=i, ax=ax, size=size: _shard_at(ins[i], ax, idx, size)
                dst = lambda idx, i=i: outs[i].at[idx]
            else:
                size = ins[i].shape[ax]
                src = lambda idx, i=i: ins[i]
                dst = lambda idx, i=i, ax=ax, size=size: _shard_at(outs[i], ax, idx, size)

            def copy(k, s, d, to, i=i):
                return pltpu.make_async_remote_copy(src_ref=s, dst_ref=d, send_sem=send_sems.at[i, k],
                                                    recv_sem=recv_sems.at[i, k], device_id=to, device_id_type=MESH)

            local.append(pltpu.make_async_copy(src(me), dst(me), local_sems.at[i]))
            if self.kind == "gather2":
                sib, sib_i = (x, y, 1 - c), 4 * x + 2 * y + (1 - c)
                chips = [(1 - x, y), (x, 1 - y), (1 - x, 1 - y)]
                sends.append(copy(0, src(me), dst(me), sib))
                recvs.append(copy(0, src(me), dst(sib_i), sib))
                for j, (cx, cy) in enumerate(chips):
                    same, other = 4 * cx + 2 * cy + c, 4 * cx + 2 * cy + (1 - c)
                    sends.append(copy(1 + j, src(me), dst(me), (cx, cy, c)))
                    chained.append((copy(1 + j, dst(same), dst(same), (cx, cy, c)), copy(4 + j, dst(same), dst(same), sib)))
                    recvs.append(copy(4 + j, dst(other), dst(other), sib))
            else:
                for k in range(1, N_DEV):
                    peer, pidx = _peer(pos, k)
                    out_src = src(pidx) if self.kind == "scatter" else src(me)
                    sends.append(copy(k - 1, out_src, dst(me), peer))
                    recvs.append(copy(k - 1, out_src, dst(pidx), peer))
        return local, sends, chained, recvs

    def start(self, ins, outs, sems):
        local, sends, _, _ = self._copies(ins, outs, sems)
        for cp in local + sends:
            cp.start()

    def wait(self, ins, outs, sems):
        local, sends, chained, recvs = self._copies(ins, outs, sems)
        for arrival, released in chained:
            arrival.wait_recv()
            released.start()
        for cp in recvs:
            cp.wait_recv()
        for cp in sends + [released for _, released in chained]:
            cp.wait_send()
        for cp in local:
            cp.wait()

    def run(self, name):
        n = self.n

        def body(*refs):
            ins, outs, sems = refs[:n], refs[n:2 * n], refs[2 * n:]
            self.start(ins, outs, sems)
            self.wait(ins, outs, sems)

        return pl.pallas_call(body, name=name, out_shape=self.out_shape, in_specs=[ANY] * n, out_specs=[ANY] * n,
                              scratch_shapes=self.scratch)(*self.arrays)


def _hosted_call(body, xch, grid, in_specs, out_specs, out_shape, scratch, args, name):
    out_specs, out_shape = list(out_specs), list(out_shape)
    n_in, n_out = len(in_specs), len(out_specs)
    if xch is None:
        res = pl.pallas_call(body, name=name, grid=grid, in_specs=in_specs, out_specs=out_specs, out_shape=out_shape,
                             scratch_shapes=list(scratch), compiler_params=_cparams())(*args)
        return list(res), []
    n = xch.n
    rank = len(grid)

    def wrapped(*refs):
        ins, x_ins = refs[:n_in], refs[n_in:n_in + n]
        outs = refs[n_in + n:n_in + n + n_out]
        x_outs = refs[n_in + n + n_out:n_in + 2 * n + n_out]
        rest = refs[n_in + 2 * n + n_out:]
        own, sems = rest[:len(rest) - 3], rest[len(rest) - 3:]
        ids = [pl.program_id(a) for a in range(rank)]
        first, last = ids[0] == 0, ids[0] == grid[0] - 1
        for a in range(1, rank):
            first = jnp.logical_and(first, ids[a] == 0)
            last = jnp.logical_and(last, ids[a] == grid[a] - 1)

        @pl.when(first)
        def _():
            xch.start(x_ins, x_outs, sems)

        body(*ins, *outs, *own)

        @pl.when(last)
        def _():
            xch.wait(x_ins, x_outs, sems)

    res = pl.pallas_call(
        wrapped, name=name, grid=grid, in_specs=list(in_specs) + [ANY] * n, out_specs=out_specs + [ANY] * n,
        out_shape=out_shape + xch.out_shape, scratch_shapes=list(scratch) + xch.scratch, compiler_params=_cparams(),
    )(*args, *xch.arrays)
    return list(res[:n_out]), list(res[n_out:])


def _all_gather(shards, axes, name, kind="gather"):
    return _Exchange(kind, shards, axes).run(name)


def _all_to_all(parts, axes, name):
    return _Exchange("scatter", parts, axes).run(name)


def _ada_fwd(cv, ada_w, ada_b):
    nl, d, wd = ada_w.shape
    r = cv.shape[0]

    def body(c_ref, w_ref, b_ref, o_ref):
        c = c_ref[...]
        s = (c * _sigmoid(c)).astype(BF)
        o_ref[0] = _dot(s, w_ref[0]) + b_ref[0]

    return pl.pallas_call(
        body, name="ada_fwd", grid=(nl,),
        in_specs=[_full((r, d)), pl.BlockSpec((1, d, wd), lambda l: (l, 0, 0)), pl.BlockSpec((1, 1, wd), lambda l: (l, 0, 0))],
        out_specs=pl.BlockSpec((1, r, wd), lambda l: (l, 0, 0)),
        out_shape=jax.ShapeDtypeStruct((nl, r, wd), F32), compiler_params=_cparams(),
    )(cv, ada_w, ada_b.reshape(nl, 1, wd))


def _ada_bwd(cv, ada_w, dm, out_dtype):
    nl, d, wd = ada_w.shape
    r = cv.shape[0]

    def body(c_ref, w_ref, dm_ref, dw_ref, dc_ref):
        c = c_ref[...]
        sg = _sigmoid(c)
        s = (c * sg).astype(BF)
        dmv = dm_ref[0].astype(BF)
        dw_ref[0] = _dot_tn(s, dmv).astype(out_dtype)
        dc_ref[0] = _dot_nt(dmv, w_ref[0]) * (sg * (1.0 + c * (1.0 - sg)))

    return pl.pallas_call(
        body, name="ada_bwd", grid=(nl,),
        in_specs=[_full((r, d)), pl.BlockSpec((1, d, wd), lambda l: (l, 0, 0)), pl.BlockSpec((1, r, wd), lambda l: (l, 0, 0))],
        out_specs=[pl.BlockSpec((1, d, wd), lambda l: (l, 0, 0)), pl.BlockSpec((1, r, d), lambda l: (l, 0, 0))],
        out_shape=[jax.ShapeDtypeStruct((nl, d, wd), out_dtype), jax.ShapeDtypeStruct((nl, r, d), F32)],
        compiler_params=_cparams(),
    )(cv, ada_w, dm)


def _sum_partials(stack):
    _, r, c = stack.shape

    def body(s_ref, o_ref):
        acc = s_ref[0]
        for p in range(1, N_DEV):
            acc = acc + s_ref[p]
        o_ref[...] = acc

    return pl.pallas_call(body, name="sum_partials", out_shape=jax.ShapeDtypeStruct((r, c), F32),
                          in_specs=[_full(stack.shape)], out_specs=_full((r, c)), grid=(1,),
                          compiler_params=_cparams())(stack)


def _adamw(gstack, w, m, v, name):
    p, r, c = gstack.shape
    tr = r
    for cand in (512 if c <= 256 else 256, 128, 64, 32, 16, 8):
        if r % cand == 0 and r > cand:
            tr = cand
            break
    bc1 = 1.0 - ADAM_B1 ** ADAM_STEP
    bc2 = 1.0 - ADAM_B2 ** ADAM_STEP

    def body(g_ref, w_ref, m_ref, v_ref, go_ref, d_ref, mo_ref, vo_ref):
        g = g_ref[0].astype(F32)
        for q in range(1, p):
            g = g + g_ref[q].astype(F32)
        mn = ADAM_B1 * m_ref[...] + (1.0 - ADAM_B1) * g
        vn = ADAM_B2 * v_ref[...] + (1.0 - ADAM_B2) * (g * g)
        go_ref[...] = g
        mo_ref[...] = mn
        vo_ref[...] = vn
        d_ref[...] = -ADAM_LR * ((mn / bc1) / (jnp.sqrt(vn / bc2) + ADAM_EPS) + ADAM_WD * w_ref[...])

    row = pl.BlockSpec((tr, c), lambda i: (i, 0))
    sds = jax.ShapeDtypeStruct((r, c), F32)
    return pl.pallas_call(
        body, name=name, grid=(r // tr,),
        in_specs=[pl.BlockSpec((p, tr, c), lambda i: (0, i, 0)), row, row, row],
        out_specs=[row, row, row, row], out_shape=[sds, sds, sds, sds], compiler_params=_cparams(),
    )(gstack, w, m, v)


def _lat_or_ctx_specs(tm, d, nl, grid_rank, row_axis):
    def lat(*ids):
        return (jnp.minimum(ids[row_axis], nl - 1), 0)

    def ctx(*ids):
        return (jnp.maximum(ids[row_axis] - nl, 0), 0)

    return pl.BlockSpec((tm, d), lat), pl.BlockSpec((tm, d), ctx)


def _sel_row(ref, is_ctx):
    return jnp.where(is_ctx, ref[1:2, :], ref[0:1, :])


def _inproj0(x, ctx, a2, b2, w, tgt, tm, xch=None):
    l, d = x.shape
    nl, nc = l // tm, ctx.shape[0] // tm
    e = w.shape[1] // 4
    half = e // 2
    tjo = tm // CHUNK

    def body(x_ref, c_ref, a_ref, b_ref, w_hbm, t_ref, o_ref, tc_ref, w_ref, ts_ref):
        i = pl.program_id(0)

        @pl.when(i == 0)
        def _():
            pltpu.sync_copy(w_hbm, w_ref)

        is_ctx = i >= nl
        xv = jnp.where(is_ctx, c_ref[...], x_ref[...])
        h = (xv * _sel_row(a_ref, is_ctx) + _sel_row(b_ref, is_ctx)).astype(BF)
        for k in range(4):
            r = _dot(h, w_ref[:, k * e:(k + 1) * e])
            o_ref[k, 0] = r[:, :half].astype(BF)
            o_ref[k, 1] = r[:, half:].astype(BF)

        @pl.when(jnp.logical_not(is_ctx))
        def _():
            for lb in range(d // 128):
                ts_ref[lb] = t_ref[:, lb * 128:(lb + 1) * 128]
            for s in range(CHUNK):
                for lb in range(d // 128):
                    tc_ref[:, s * d + lb * 128:s * d + (lb + 1) * 128] = ts_ref.at[lb][pl.ds(s, tjo, stride=CHUNK), :]

    lat, cx = _lat_or_ctx_specs(tm, d, nl, 1, 0)
    (p42, tgt_cr), extra = _hosted_call(
        body, xch, grid=(nl + nc,),
        in_specs=[lat, cx, _full((2, d)), _full((2, d)), ANY, lat],
        out_specs=[pl.BlockSpec((4, 2, tm, half), lambda i: (0, 0, i, 0)),
                   pl.BlockSpec((tjo, CHUNK * d), lambda i: (jnp.minimum(i, nl - 1), 0))],
        out_shape=[jax.ShapeDtypeStruct((4, 2, l + ctx.shape[0], half), BF),
                   jax.ShapeDtypeStruct((l // CHUNK, CHUNK * d), F32)],
        scratch=[pltpu.VMEM(w.shape, BF), pltpu.VMEM((d // 128, tm, 128), F32)],
        args=(x, ctx, a2, b2, w, tgt), name="l0_inproj")
    return p42, tgt_cr, extra


def _conv_taps(u, w_up, w_mid, w_dn, pos, rl, tm):
    up = jnp.where(pos == 0, 0.0, pltpu.roll(u, 1, 0))
    dn = jnp.where(pos == rl - 1, 0.0, pltpu.roll(u, tm - 1, 0))
    return w_up * up + w_mid * u + w_dn * dn, up, dn


def _conv_halo_specs(tm, tc, nl, lead):
    hb = tm // GRID_W

    def prev(j, i):
        return (0, 1, jnp.maximum(jnp.minimum(i, nl - 1) * hb - 1, 0), j)

    def nxt(j, i):
        return (0, 1, jnp.minimum((jnp.minimum(i, nl - 1) + 1) * hb, nl * hb - 1), j)

    return pl.BlockSpec((lead, 1, GRID_W, tc), prev), pl.BlockSpec((lead, 1, GRID_W, tc), nxt)


def _conv_fwd(p42, cw, nl, tm, tc):
    _, _, r, half = p42.shape
    nt = r // tm

    def body(p_ref, hp_ref, hn_ref, cw_ref, o_ref):
        i = pl.program_id(1)
        is_ctx = i >= nl
        row = lax.broadcasted_iota(jnp.int32, (tm, tc), 0)
        rl = jnp.where(is_ctx, tm, GRID_W)
        pos = jnp.bitwise_and(row, rl - 1)

        def gate(hv, yc):
            bg = p_ref[0, hv].astype(F32)
            z = p_ref[3, hv].astype(F32)
            return (bg * yc * (z * _sigmoid(z))).astype(BF)

        u_h = p_ref[1, 0].astype(F32) * p_ref[2, 0].astype(F32)
        w_h = cw_ref[:, 0, :]
        o_ref[0] = gate(0, _conv_taps(u_h, w_h[0:1], w_h[1:2], w_h[2:3], pos, rl, tm)[0])
        u_v = p_ref[1, 1].astype(F32) * p_ref[2, 1].astype(F32)
        w_v = cw_ref[:, 1, :]

        @pl.when(is_ctx)
        def _():
            o_ref[1] = gate(1, _conv_taps(u_v, w_v[0:1], w_v[1:2], w_v[2:3], pos, rl, tm)[0])

        @pl.when(jnp.logical_not(is_ctx))
        def _():
            up = hp_ref[1, 0].astype(F32) * hp_ref[2, 0].astype(F32) * (i > 0).astype(F32)
            dn = hn_ref[1, 0].astype(F32) * hn_ref[2, 0].astype(F32) * (i < nl - 1).astype(F32)
            ext = jnp.concatenate([up, u_v, dn], axis=0)
            yc = w_v[0:1] * ext[0:tm] + w_v[1:2] * u_v + w_v[2:3] * ext[2 * GRID_W:tm + 2 * GRID_W]
            o_ref[1] = gate(1, yc)

    hp, hn = _conv_halo_specs(tm, tc, nl, 4)
    return pl.pallas_call(
        body, name="l0_conv_fwd", grid=(half // tc, nt),
        in_specs=[pl.BlockSpec((4, 2, tm, tc), lambda j, i: (0, 0, i, j)), hp, hn,
                  pl.BlockSpec((3, 2, tc), lambda j, i: (0, 0, j))],
        out_specs=pl.BlockSpec((2, tm, tc), lambda j, i: (0, i, j)),
        out_shape=jax.ShapeDtypeStruct((2, r, half), BF), compiler_params=_cparams(),
    )(p42, p42, p42, cw)


def _outproj_ln0(q3, w_out, x, ctx, gt2, tm):
    l, d = x.shape
    lc = ctx.shape[0]
    nl, nc = l // tm, lc // tm
    _, r, half = q3.shape
    tjo = tm // CHUNK

    def body(q_ref, w_hbm, x_ref, c_ref, g_ref, xl_ref, xc_ref, rl_ref, rc_ref, fx_ref, w_ref, xs_ref, rs_ref):
        i = pl.program_id(0)

        @pl.when(i == 0)
        def _():
            pltpu.sync_copy(w_hbm, w_ref)

        is_ctx = i >= nl
        fx = _dot(q_ref[0], w_ref[:half, :]) + _dot(q_ref[1], w_ref[half:, :])
        xv = jnp.where(is_ctx, c_ref[...], x_ref[...])
        rr = DN_ALPHA * xv + _sel_row(g_ref, is_ctx) * fx
        mu = jnp.mean(rr, axis=-1, keepdims=True)
        cen = rr - mu
        rstd = lax.rsqrt(jnp.mean(cen * cen, axis=-1, keepdims=True) + LN_EPS)
        xh = cen * rstd
        for lb in range(d // 128):
            xs_ref[lb] = xh[:, lb * 128:(lb + 1) * 128]
        rs_ref[...] = jnp.broadcast_to(rstd, (tm, 128))
        fx_ref[...] = fx.astype(BF)

        def to_cr(xo_ref, ro_ref):
            for s in range(CHUNK):
                for lb in range(d // 128):
                    xo_ref[:, s * d + lb * 128:s * d + (lb + 1) * 128] = xs_ref.at[lb][pl.ds(s, tjo, stride=CHUNK), :]
                ro_ref[:, s * 128:(s + 1) * 128] = rs_ref[pl.ds(s, tjo, stride=CHUNK), :]

        @pl.when(jnp.logical_not(is_ctx))
        def _():
            to_cr(xl_ref, rl_ref)

        @pl.when(is_ctx)
        def _():
            to_cr(xc_ref, rc_ref)

    lat, cx = _lat_or_ctx_specs(tm, d, nl, 1, 0)
    lat_o = lambda w_: pl.BlockSpec((tjo, CHUNK * w_), lambda i: (jnp.minimum(i, nl - 1), 0))
    ctx_o = lambda w_: pl.BlockSpec((tjo, CHUNK * w_), lambda i: (jnp.maximum(i - nl, 0), 0))
    return pl.pallas_call(
        body, name="l0_outproj_ln", grid=(nl + nc,),
        in_specs=[pl.BlockSpec((2, tm, half), lambda i: (0, i, 0)), ANY, lat, cx, _full((2, d))],
        out_specs=[lat_o(d), ctx_o(d), lat_o(128), ctx_o(128), pl.BlockSpec((tm, d), lambda i: (i, 0))],
        out_shape=[jax.ShapeDtypeStruct((l // CHUNK, CHUNK * d), F32), jax.ShapeDtypeStruct((lc // CHUNK, CHUNK * d), F32),
                   jax.ShapeDtypeStruct((l // CHUNK, CHUNK * 128), F32), jax.ShapeDtypeStruct((lc // CHUNK, CHUNK * 128), F32),
                   jax.ShapeDtypeStruct((r, d), BF)],
        scratch_shapes=[pltpu.VMEM(w_out.shape, BF), pltpu.VMEM((d // 128, tm, 128), F32), pltpu.VMEM((tm, 128), F32)],
        compiler_params=_cparams(),
    )(q3, w_out, x, ctx, gt2)


def _bwd_outproj0(dr_l, dr_c, gt2, w_out, fx, tm):
    l, d = dr_l.shape
    nl, nc = l // tm, dr_c.shape[0] // tm
    e = w_out.shape[0]
    half = e // 2
    r = l + dr_c.shape[0]

    def body(dl_ref, dc_ref, g_ref, w_hbm, fx_ref, dq_ref, acc_ref, w_ref):
        i = pl.program_id(0)

        @pl.when(i == 0)
        def _():
            pltpu.sync_copy(w_hbm, w_ref)
            acc_ref[...] = jnp.zeros_like(acc_ref)

        is_ctx = i >= nl
        dr = jnp.where(is_ctx, dc_ref[...], dl_ref[...]).astype(F32)
        dfx = (dr * _sel_row(g_ref, is_ctx)).astype(BF)
        dq_ref[0] = _dot_nt(dfx, w_ref[:half, :]).astype(BF)
        dq_ref[1] = _dot_nt(dfx, w_ref[half:, :]).astype(BF)
        s = jnp.sum(dr * fx_ref[...].astype(F32), axis=0, keepdims=True)
        sel = is_ctx.astype(F32)
        acc_ref[0:1, :] += s * (1.0 - sel)
        acc_ref[1:2, :] += s * sel

    lat, cx = _lat_or_ctx_specs(tm, d, nl, 1, 0)
    return pl.pallas_call(
        body, name="l0_bwd_outproj", grid=(nl + nc,),
        in_specs=[lat, cx, _full((2, d)), ANY, pl.BlockSpec((tm, d), lambda i: (i, 0))],
        out_specs=[pl.BlockSpec((2, tm, half), lambda i: (0, i, 0)), _full((8, d))],
        out_shape=[jax.ShapeDtypeStruct((2, r, half), BF), jax.ShapeDtypeStruct((8, d), F32)],
        scratch_shapes=[pltpu.VMEM(w_out.shape, BF)], compiler_params=_cparams(),
    )(dr_l, dr_c, gt2, w_out, fx)


def _conv_bwd_inproj0(dq3, p42, cw, w_in, x, ctx, dr_l, dr_c, a2, nl, tm, xch=None):
    l, d = x.shape
    _, _, r, half = p42.shape
    nt = r // tm
    e = 2 * half
    cc = min(512, half)
    n_cc = half // cc

    def body(dq_ref, dqp_ref, dqn_ref, p_ref, hp_ref, hn_ref, cw_ref, w_hbm, x_ref, c_ref, dl_ref, dc_ref, a_ref,
             dp_ref, dw_ref, gx_ref, acc_ref, w_ref, dh_ref):
        i, hv = pl.program_id(0), pl.program_id(1)
        is_ctx = i >= nl

        @pl.when(jnp.logical_and(i == 0, hv == 0))
        def _():
            pltpu.sync_copy(w_hbm, w_ref)
            acc_ref[...] = jnp.zeros_like(acc_ref)
            dw_ref[...] = jnp.zeros_like(dw_ref)

        row = lax.broadcasted_iota(jnp.int32, (tm, cc), 0)
        rl = jnp.where(is_ctx, tm, GRID_W)
        pos = jnp.bitwise_and(row, rl - 1)

        def pieces(dq, bg, z):
            sz = _sigmoid(z)
            sil = z * sz
            return dq * bg * sil, dq * sil, dq * bg * (sz * (1.0 + z * (1.0 - sz)))

        def emit(hvs, c, parts, dyc, u_up, u, u_dn, dh):
            lanes = slice(c * cc, (c + 1) * cc)
            for k, part in enumerate(parts):
                pb = part.astype(BF)
                dp_ref[k, 0, :, lanes] = pb
                c0 = k * e + hvs * half + c * cc
                t = _dot_nt(pb, w_ref[:, c0:c0 + cc])
                dh = t if dh is None else dh + t
            dw_ref[0:1, hvs, lanes] += jnp.sum(dyc * u_up, axis=0, keepdims=True)
            dw_ref[1:2, hvs, lanes] += jnp.sum(dyc * u, axis=0, keepdims=True)
            dw_ref[2:3, hvs, lanes] += jnp.sum(dyc * u_dn, axis=0, keepdims=True)
            return dh

        def seq_half(hvs):
            dh = None
            for c in range(n_cc):
                lanes = slice(c * cc, (c + 1) * cc)
                bg, cg = p_ref[0, 0, :, lanes].astype(F32), p_ref[1, 0, :, lanes].astype(F32)
                v, z = p_ref[2, 0, :, lanes].astype(F32), p_ref[3, 0, :, lanes].astype(F32)
                w = cw_ref[:, hvs, lanes]
                u = cg * v
                yc, u_up, u_dn = _conv_taps(u, w[0:1], w[1:2], w[2:3], pos, rl, tm)
                dyc, dbg_f, dz_f = pieces(dq_ref[0, :, lanes].astype(F32), bg, z)
                du = _conv_taps(dyc, w[2:3], w[1:2], w[0:1], pos, rl, tm)[0]
                dh = emit(hvs, c, (dbg_f * yc, du * v, du * cg, dz_f * yc), dyc, u_up, u, u_dn, dh)
            return dh

        def col_half():
            m_up = (i > 0).astype(F32)
            m_dn = (i < nl - 1).astype(F32)
            dh = None
            for c in range(n_cc):
                lanes = slice(c * cc, (c + 1) * cc)
                bg, cg = p_ref[0, 0, :, lanes].astype(F32), p_ref[1, 0, :, lanes].astype(F32)
                v, z = p_ref[2, 0, :, lanes].astype(F32), p_ref[3, 0, :, lanes].astype(F32)
                w = cw_ref[:, 1, lanes]
                u = cg * v

                def halo(h_ref, dqh_ref, msk):
                    hb, hc = h_ref[0, 0, :, lanes].astype(F32), h_ref[1, 0, :, lanes].astype(F32)
                    hv_, hz = h_ref[2, 0, :, lanes].astype(F32), h_ref[3, 0, :, lanes].astype(F32)
                    return hc * hv_ * msk, pieces(dqh_ref[0, :, lanes].astype(F32), hb, hz)[0] * msk

                u_p, dyc_p = halo(hp_ref, dqp_ref, m_up)
                u_n, dyc_n = halo(hn_ref, dqn_ref, m_dn)
                u_ext = jnp.concatenate([u_p, u, u_n], axis=0)
                u_up, u_dn = u_ext[0:tm], u_ext[2 * GRID_W:tm + 2 * GRID_W]
                yc = w[0:1] * u_up + w[1:2] * u + w[2:3] * u_dn
                dyc, dbg_f, dz_f = pieces(dq_ref[0, :, lanes].astype(F32), bg, z)
                d_ext = jnp.concatenate([dyc_p, dyc, dyc_n], axis=0)
                du = w[0:1] * d_ext[2 * GRID_W:tm + 2 * GRID_W] + w[1:2] * dyc + w[2:3] * d_ext[0:tm]
                dh = emit(1, c, (dbg_f * yc, du * v, du * cg, dz_f * yc), dyc, u_up, u, u_dn, dh)
            return dh

        @pl.when(hv == 0)
        def _():
            dh_ref[...] = seq_half(0)

        @pl.when(jnp.logical_and(hv == 1, is_ctx))
        def _():
            dh_ref[...] += seq_half(1)

        @pl.when(jnp.logical_and(hv == 1, jnp.logical_not(is_ctx)))
        def _():
            dh_ref[...] += col_half()

        @pl.when(hv == 1)
        def _():
            dh = dh_ref[...]
            xv = jnp.where(is_ctx, c_ref[...], x_ref[...])
            s_sc = jnp.sum(dh * xv, axis=0, keepdims=True)
            s_sh = jnp.sum(dh, axis=0, keepdims=True)
            sel = is_ctx.astype(F32)
            acc_ref[0:1, :] += s_sc * (1.0 - sel)
            acc_ref[1:2, :] += s_sc * sel
            acc_ref[2:3, :] += s_sh * (1.0 - sel)
            acc_ref[3:4, :] += s_sh * sel

        @pl.when(jnp.logical_and(hv == 1, jnp.logical_not(is_ctx)))
        def _():
            gx_ref[...] = DN_ALPHA * dl_ref[...].astype(F32) + dh_ref[...] * a_ref[0:1, :]

    hb = tm // GRID_W
    prev_blk = lambda i: jnp.maximum(jnp.minimum(i, nl - 1) * hb - 1, 0)
    next_blk = lambda i: jnp.minimum((jnp.minimum(i, nl - 1) + 1) * hb, nl * hb - 1)
    lat, cx = _lat_or_ctx_specs(tm, d, nl, 2, 0)
    (dp42, dcw, gx, acc), extra = _hosted_call(
        body, xch, grid=(nt, 2),
        in_specs=[pl.BlockSpec((1, tm, half), lambda i, h: (h, i, 0)),
                  pl.BlockSpec((1, GRID_W, half), lambda i, h: (1, prev_blk(i), 0)),
                  pl.BlockSpec((1, GRID_W, half), lambda i, h: (1, next_blk(i), 0)),
                  pl.BlockSpec((4, 1, tm, half), lambda i, h: (0, h, i, 0)),
                  pl.BlockSpec((4, 1, GRID_W, half), lambda i, h: (0, 1, prev_blk(i), 0)),
                  pl.BlockSpec((4, 1, GRID_W, half), lambda i, h: (0, 1, next_blk(i), 0)),
                  _full((3, 2, half)), ANY, lat, cx, lat, cx, _full((2, d))],
        out_specs=[pl.BlockSpec((4, 1, tm, half), lambda i, h: (0, h, i, 0)), _full((8, 2, half)),
                   pl.BlockSpec((tm, d), lambda i, h: (jnp.minimum(i, nl - 1), 0)), _full((8, d))],
        out_shape=[jax.ShapeDtypeStruct(p42.shape, BF), jax.ShapeDtypeStruct((8, 2, half), F32),
                   jax.ShapeDtypeStruct((l, d), F32), jax.ShapeDtypeStruct((8, d), F32)],
        scratch=[pltpu.VMEM(w_in.shape, BF), pltpu.VMEM((tm, d), F32)],
        args=(dq3, dq3, dq3, p42, p42, p42, cw, w_in, x, ctx, dr_l, dr_c, a2), name="l0_conv_bwd_inproj")
    return dp42, dcw, gx, acc, extra


def _dw_inproj0(x, ctx, a2, b2, dp42, tm):
    l, d = x.shape
    lc = ctx.shape[0]
    assert lc == tm
    tl = 4 * tm if l % (4 * tm) == 0 else tm
    nl = l // tl
    half = dp42.shape[-1]
    e = 2 * half

    def body(x_ref, c_ref, a_ref, b_ref, dpl_ref, dpc_ref, o_ref, acc_ref):
        i = pl.program_id(1)

        @pl.when(i == 0)
        def _():
            acc_ref[...] = jnp.zeros_like(acc_ref)

        def add(rows_ref, dp_ref, sel):
            h = (rows_ref[...] * a_ref[sel:sel + 1, :] + b_ref[sel:sel + 1, :]).astype(BF)
            acc_ref[:, :half] += _dot_tn(h, dp_ref[0, 0])
            acc_ref[:, half:] += _dot_tn(h, dp_ref[0, 1])

        @pl.when(i < nl)
        def _():
            add(x_ref, dpl_ref, 0)

        @pl.when(i == nl)
        def _():
            add(c_ref, dpc_ref, 1)
            o_ref[...] = acc_ref[...].astype(BF)

    return pl.pallas_call(
        body, name="l0_dw_inproj", grid=(4, nl + 1),
        in_specs=[pl.BlockSpec((tl, d), lambda k, i: (jnp.minimum(i, nl - 1), 0)), _full((lc, d)),
                  _full((2, d)), _full((2, d)),
                  pl.BlockSpec((1, 2, tl, half), lambda k, i: (k, 0, jnp.minimum(i, nl - 1), 0)),
                  pl.BlockSpec((1, 2, lc, half), lambda k, i: (k, 0, l // lc, 0))],
        out_specs=pl.BlockSpec((d, e), lambda k, i: (0, k)),
        out_shape=jax.ShapeDtypeStruct((d, 4 * e), BF),
        scratch_shapes=[pltpu.VMEM((d, e), F32)], compiler_params=_cparams(),
    )(x, ctx, a2, b2, dp42, dp42)


def _dw_outproj0(q3, dr_l, dr_c, gt2, tm, xch=None):
    l, d = dr_l.shape
    nl, nc = l // tm, dr_c.shape[0] // tm
    _, r, half = q3.shape
    nt = nl + nc

    def body(q_ref, dl_ref, dc_ref, g_ref, o_ref, acc_ref):
        i = pl.program_id(0)
        is_ctx = i >= nl

        @pl.when(i == 0)
        def _():
            acc_ref[...] = jnp.zeros_like(acc_ref)

        dr = jnp.where(is_ctx, dc_ref[...], dl_ref[...]).astype(F32)
        dfx = (dr * _sel_row(g_ref, is_ctx)).astype(BF)
        acc_ref[:half, :] += _dot_tn(q_ref[0], dfx)
        acc_ref[half:, :] += _dot_tn(q_ref[1], dfx)

        @pl.when(i == nt - 1)
        def _():
            o_ref[...] = acc_ref[...].astype(BF)

    lat, cx = _lat_or_ctx_specs(tm, d, nl, 1, 0)
    (g_w,), extra = _hosted_call(
        body, xch, grid=(nt,),
        in_specs=[pl.BlockSpec((2, tm, half), lambda i: (0, i, 0)), lat, cx, _full((2, d))],
        out_specs=[_full((2 * half, d))], out_shape=[jax.ShapeDtypeStruct((2 * half, d), BF)],
        scratch=[pltpu.VMEM((2 * half, d), F32)], args=(q3, dr_l, dr_c, gt2), name="l0_dw_outproj")
    return g_w, extra


def _cr_tile(j, cap=256):
    for cand in (1024, 512, 256, 128, 64, 32, 16, 8):
        if cand <= cap and j % cand == 0:
            return cand
    raise ValueError(j)


def _final(w_cr, w_out, xh_cr, tgt_cr, vecs):
    j, e16 = w_cr.shape
    e = e16 // CHUNK
    d = w_out.shape[1]
    tj = _cr_tile(j)

    def body(w_ref, wo_hbm, xh_ref, t_ref, v_ref, dr_ref, acc_ref, wo_ref):
        @pl.when(jnp.logical_and(pl.program_id(0) == 0, pl.program_id(1) == 0))
        def _():
            pltpu.sync_copy(wo_hbm, wo_ref)
            acc_ref[...] = jnp.zeros_like(acc_ref)

        o = _dot(w_ref[...], wo_ref[...])
        x1 = xh_ref[...] * v_ref[0:1, :] + v_ref[1:2, :]
        rr = DN_ALPHA * x1 + v_ref[2:3, :] * o
        mu = jnp.mean(rr, axis=-1, keepdims=True)
        cen = rr - mu
        rstd = lax.rsqrt(jnp.mean(cen * cen, axis=-1, keepdims=True) + LN_EPS)
        xh2 = cen * rstd
        err = xh2 * v_ref[3:4, :] + v_ref[4:5, :] - t_ref[...]
        dy = err * (1.0 / d)
        dxh = dy * v_ref[3:4, :]
        dr = rstd * (dxh - jnp.mean(dxh, axis=-1, keepdims=True) - xh2 * jnp.mean(dxh * xh2, axis=-1, keepdims=True))
        dr_ref[...] = dr.astype(BF)
        acc_ref[0:1, :] += jnp.sum(dy * xh2, axis=0, keepdims=True)
        acc_ref[1:2, :] += jnp.sum(dy, axis=0, keepdims=True)
        acc_ref[2:3, :] += jnp.sum(dr * o, axis=0, keepdims=True)
        acc_ref[3:4, :] += (0.5 / d) * jnp.sum(err * err, axis=0, keepdims=True)

    tok_d = pl.BlockSpec((tj, d), lambda t, s: (t, s))
    return pl.pallas_call(
        body, name="l1_final", grid=(j // tj, CHUNK),
        in_specs=[pl.BlockSpec((tj, e), lambda t, s: (t, s)), ANY, tok_d, tok_d, _full((8, d))],
        out_specs=[tok_d, _full((8, d))],
        out_shape=[jax.ShapeDtypeStruct((j, CHUNK * d), BF), jax.ShapeDtypeStruct((8, d), F32)],
        scratch_shapes=[pltpu.VMEM(w_out.shape, BF)], compiler_params=_cparams(),
    )(w_cr, w_out, xh_cr, tgt_cr, vecs)


def _dw_cr(lhs, rhs, lhs_kind, rhs_kind, vec, bias_sum, init, name):
    j = lhs.shape[0]
    k = lhs.shape[1] // CHUNK
    n = rhs.shape[1] // CHUNK
    tj = _cr_tile(j, 512)
    nh = 2 if k * n * 4 > (8 << 20) else 1
    tn = n // nh
    nt = j // tj
    has_init = init is not None

    def body(*refs):
        refs = list(refs)
        l_ref, r_ref = refs[0], refs[1]
        pos = 2
        v_ref = None
        if vec is not None:
            v_ref = refs[pos]
            pos += 1
        i_ref = None
        if has_init:
            i_ref = refs[pos]
            pos += 1
        o_ref = refs[pos]
        pos += 1
        bs_ref = None
        if bias_sum:
            bs_ref = refs[pos]
            pos += 1
        acc_ref = refs[pos]
        t, s = pl.program_id(1), pl.program_id(2)
        first = jnp.logical_and(t == 0, s == 0)

        @pl.when(first)
        def _():
            acc_ref[...] = i_ref[...] if has_init else jnp.zeros_like(acc_ref)
            if bias_sum:
                bs_ref[...] = jnp.zeros_like(bs_ref)

        if lhs_kind == "mod":
            lv = (l_ref[...] * v_ref[0:1, :] + v_ref[1:2, :]).astype(BF)
        else:
            lv = l_ref[...]
        if rhs_kind == "scaled":
            rv = (r_ref[...].astype(F32) * v_ref[0:1, :]).astype(BF)
        else:
            rv = r_ref[...]
        acc_ref[...] += _dot_tn(lv, rv)
        if bias_sum:
            bs_ref[0:1, :] += jnp.sum(rv.astype(F32), axis=0, keepdims=True)

        @pl.when(jnp.logical_and(t == nt - 1, s == CHUNK - 1))
        def _():
            o_ref[...] = acc_ref[...].astype(BF)

    l_spec = pl.BlockSpec((tj, k), lambda h, t, s: (t, s))
    r_spec = pl.BlockSpec((tj, tn), lambda h, t, s: (t, s * nh + h))
    in_specs, args = [l_spec, r_spec], [lhs, rhs]
    if vec is not None:
        in_specs.append(_full(vec.shape))
        args.append(vec)
    o_spec = pl.BlockSpec((k, tn), lambda h, t, s: (0, h))
    if has_init:
        in_specs.append(o_spec)
        args.append(init)
    out_specs, out_shape = [o_spec], [jax.ShapeDtypeStruct((k, n), BF)]
    if bias_sum:
        out_specs.append(pl.BlockSpec((8, tn), lambda h, t, s: (0, h)))
        out_shape.append(jax.ShapeDtypeStruct((8, n), F32))
    res = pl.pallas_call(
        body, name=name, grid=(nh, nt, CHUNK), in_specs=in_specs, out_specs=out_specs, out_shape=out_shape,
        scratch_shapes=[pltpu.VMEM((k, tn), F32)], compiler_params=_cparams(),
    )(*args)
    return res if bias_sum else res[0]


GT_ROWS = CHUNK * S5_P
ZG_W = 2 * 2 * S5_N
PAIR_W = 2 * ZG_W
GROUPS_PER_STEP = 4


def _inproj1_gt(xh_cr, a1, b1, wu_t, w_z, tag):
    j, d16 = xh_cr.shape
    d = d16 // CHUNK
    e = wu_t.shape[0]
    g = e // S5_P
    tj = _cr_tile(j, 256)

    def body(x_ref, a_ref, b_ref, wu_hbm, wz_hbm, u_ref, z_ref, wu_ref, wz_ref):
        @pl.when(jnp.logical_and(pl.program_id(0) == 0, pl.program_id(1) == 0))
        def _():
            pltpu.sync_copy(wu_hbm, wu_ref)
            pltpu.sync_copy(wz_hbm, wz_ref)

        h = (x_ref[...] * a_ref[...] + b_ref[...]).astype(BF)
        u_ref[...] = _dot_nt(wu_ref[...], h).reshape(g, S5_P, tj).astype(BF)
        z_ref[...] = _dot(h, wz_ref[...]).astype(BF)

    return pl.pallas_call(
        body, name="l1_inproj_" + tag, grid=(j // tj, CHUNK),
        in_specs=[pl.BlockSpec((tj, d), lambda t, s: (t, s)), _full((1, d)), _full((1, d)), ANY, ANY],
        out_specs=[pl.BlockSpec((g, S5_P, tj), lambda t, s: (0, s, t)), pl.BlockSpec((tj, e), lambda t, s: (t, s))],
        out_shape=[jax.ShapeDtypeStruct((g, GT_ROWS, j), BF), jax.ShapeDtypeStruct((j, CHUNK * e), BF)],
        scratch_shapes=[pltpu.VMEM(wu_t.shape, BF), pltpu.VMEM(w_z.shape, BF)], compiler_params=_cparams(),
    )(xh_cr, a1, b1, wu_t, w_z)


def _gt_spec(j, gb=GROUPS_PER_STEP):
    return pl.BlockSpec((gb, GT_ROWS, j), lambda i: (i, 0, 0))


def _zg_spec(j, gb=GROUPS_PER_STEP):
    return pl.BlockSpec((j, gb * ZG_W), lambda i: (0, i))


def _w_spec(width, gb=GROUPS_PER_STEP):
    return pl.BlockSpec((gb, GT_ROWS, width), lambda i: (i, 0, 0))


def _pair_lanes(k):
    return slice((k // 2) * PAIR_W, (k // 2 + 1) * PAIR_W)


def _s5_z(ut_l, ut_c, bc):
    g, _, jl = ut_l.shape
    jc = ut_c.shape[2]
    gb = GROUPS_PER_STEP

    def body(ul_ref, uc_ref, bc_ref, zl_ref, zc_ref):
        for k in range(0, gb, 2):
            zl_ref[:, _pair_lanes(k)] = _dot_tn(ul_ref[k], bc_ref[k]) + _dot_tn(ul_ref[k + 1], bc_ref[k + 1])
            zc_ref[:, _pair_lanes(k)] = _dot_tn(uc_ref[k], bc_ref[k]) + _dot_tn(uc_ref[k + 1], bc_ref[k + 1])

    return pl.pallas_call(
        body, name="l1_s5_z", grid=(g // gb,), in_specs=[_gt_spec(jl), _gt_spec(jc), _w_spec(PAIR_W)],
        out_specs=[_zg_spec(jl), _zg_spec(jc)],
        out_shape=[jax.ShapeDtypeStruct((jl, g * ZG_W), F32), jax.ShapeDtypeStruct((jc, g * ZG_W), F32)],
        compiler_params=_cparams(),
    )(ut_l, ut_c, bc)


def _s5_y(ut_l, s_l, mt_t, cct):
    g, _, jl = ut_l.shape
    gb = GROUPS_PER_STEP

    def body(u_ref, s_ref, mt_ref, cc_ref, y_ref):
        for k in range(gb):
            s_k = s_ref[:, _pair_lanes(k)].astype(BF)
            y_ref[k] = (_dot(mt_ref[k], u_ref[k]) + _dot_nt(cc_ref[k], s_k)).astype(BF)

    return pl.pallas_call(
        body, name="l1_s5_y", grid=(g // gb,),
        in_specs=[_gt_spec(jl), _zg_spec(jl), _w_spec(GT_ROWS), _w_spec(PAIR_W)],
        out_specs=_gt_spec(jl), out_shape=jax.ShapeDtypeStruct((g, GT_ROWS, jl), BF), compiler_params=_cparams(),
    )(ut_l, s_l, mt_t, cct)


def _s5_ds(dyt_l, cct):
    g, _, jl = dyt_l.shape
    gb = GROUPS_PER_STEP

    def body(dy_ref, cc_ref, ds_ref):
        for k in range(0, gb, 2):
            ds_ref[:, _pair_lanes(k)] = _dot_tn(dy_ref[k], cc_ref[k]) + _dot_tn(dy_ref[k + 1], cc_ref[k + 1])

    return pl.pallas_call(
        body, name="l1_s5_ds", grid=(g // gb,), in_specs=[_gt_spec(jl), _w_spec(PAIR_W)], out_specs=_zg_spec(jl),
        out_shape=jax.ShapeDtypeStruct((jl, g * ZG_W), F32), compiler_params=_cparams(),
    )(dyt_l, cct)


def _s5_dx(dyt_l, dz_l, dz_c, mt, bc):
    g, _, jl = dyt_l.shape
    jc = dz_c.shape[0]
    gb = GROUPS_PER_STEP

    def body(dy_ref, dzl_ref, dzc_ref, mt_ref, bc_ref, dul_ref, duc_ref):
        for k in range(gb):
            dzl = dzl_ref[:, _pair_lanes(k)].astype(BF)
            dzc = dzc_ref[:, _pair_lanes(k)].astype(BF)
            dul_ref[k] = (_dot(mt_ref[k], dy_ref[k]) + _dot_nt(bc_ref[k], dzl)).astype(BF)
            duc_ref[k] = _dot_nt(bc_ref[k], dzc).astype(BF)

    return pl.pallas_call(
        body, name="l1_s5_dx", grid=(g // gb,),
        in_specs=[_gt_spec(jl), _zg_spec(jl), _zg_spec(jc), _w_spec(GT_ROWS), _w_spec(PAIR_W)],
        out_specs=[_gt_spec(jl), _gt_spec(jc)],
        out_shape=[jax.ShapeDtypeStruct((g, GT_ROWS, jl), BF), jax.ShapeDtypeStruct((g, GT_ROWS, jc), BF)],
        compiler_params=_cparams(),
    )(dyt_l, dz_l, dz_c, mt, bc)


def _s5_dw(ut_l, ut_c, dyt_l, dz_l, dz_c, s_l):
    g, _, jl = ut_l.shape
    jc = ut_c.shape[2]
    gb = GROUPS_PER_STEP

    def body(ul_ref, uc_ref, dy_ref, dzl_ref, dzc_ref, s_ref, dmt_ref, dbc_ref, dcc_ref):
        for k in range(gb):
            lanes = _pair_lanes(k)
            dmt_ref[k] = _dot_nt(ul_ref[k], dy_ref[k])
            dbc_ref[k] = (_dot(ul_ref[k], dzl_ref[:, lanes].astype(BF))
                          + _dot(uc_ref[k], dzc_ref[:, lanes].astype(BF)))
            dcc_ref[k] = _dot(dy_ref[k], s_ref[:, lanes].astype(BF))

    sd_m = jax.ShapeDtypeStruct((g, GT_ROWS, GT_ROWS), F32)
    sd_p = jax.ShapeDtypeStruct((g, GT_ROWS, PAIR_W), F32)
    return pl.pallas_call(
        body, name="l1_s5_dw", grid=(g // gb,),
        in_specs=[_gt_spec(jl), _gt_spec(jc), _gt_spec(jl), _zg_spec(jl), _zg_spec(jc), _zg_spec(jl)],
        out_specs=[_w_spec(GT_ROWS), _w_spec(PAIR_W), _w_spec(PAIR_W)], out_shape=[sd_m, sd_p, sd_p],
        compiler_params=_cparams(),
    )(ut_l, ut_c, dyt_l, dz_l, dz_c, s_l)


def _scan_g(z_l, z_c, coef, chains, conj, s_l=None, s_c=None, name="l1_scan"):
    jl, w_all = z_l.shape
    jc = z_c.shape[0]
    gb = 2 * GROUPS_PER_STEP if w_all % (2 * GROUPS_PER_STEP * ZG_W) == 0 else GROUPS_PER_STEP
    wb = gb * ZG_W
    nch = wb // 256
    with_da = s_l is not None
    sign = -1.0 if conj else 1.0

    def body(*refs):
        zl_ref, zc_ref, cf_ref = refs[:3]
        k0 = 3
        if with_da:
            sl_ref, sc_ref = refs[3:5]
            k0 = 5
        ol_ref, oc_ref = refs[k0:k0 + 2]
        rowi = lax.broadcasted_iota(jnp.int32, (8, 128), 0)

        def lanes_of(ch):
            return slice(ch * 256, ch * 256 + 128), slice(ch * 256 + 128, (ch + 1) * 256)

        def coefs(ch, r0, nr):
            lr, li = lanes_of(ch)
            return cf_ref[r0:r0 + nr, lr], sign * cf_ref[r0:r0 + nr, li]

        def shift(v, sh, rev):
            if rev:
                return jnp.where(rowi < 8 - sh, pltpu.roll(v, 8 - sh, 0), 0.0)
            return jnp.where(rowi >= sh, pltpu.roll(v, sh, 0), 0.0)

        zero_row = jnp.zeros((1, 128), F32)
        zero_tile = jnp.zeros((8, 128), F32)
        carry = [zero_row] * (2 * nch)
        da = [zero_tile] * (2 * nch)
        for seg in range(len(chains[0])):
            which = chains[0][seg][0]
            assert chains[1][seg][0] == which
            revs = (chains[0][seg][1], chains[1][seg][1])
            src, dst = (zc_ref, oc_ref) if which == "c" else (zl_ref, ol_ref)
            sref = ((sc_ref if which == "c" else sl_ref) if with_da else None)
            ng = (jc if which == "c" else jl) // 8

            def step(it, st, src=src, dst=dst, sref=sref, ng=ng, revs=revs):
                carry_, da_ = list(st[:2 * nch]), list(st[2 * nch:])
                for ch in range(nch):
                    rev = revs[ch % 2]
                    lr, li = lanes_of(ch)
                    grp = (ng - 1 - it) if rev else it
                    off = pl.multiple_of(grp * 8, 8)
                    xr, xi = src[pl.ds(off, 8), lr], src[pl.ds(off, 8), li]
                    for sh, r0 in ((1, 0), (2, 1), (4, 2)):
                        ar, ai = coefs(ch, r0, 1)
                        sr, si = shift(xr, sh, rev), shift(xi, sh, rev)
                        xr, xi = xr + ar * sr - ai * si, xi + ar * si + ai * sr
                    tr, ti = coefs(ch, 16, 8) if rev else coefs(ch, 8, 8)
                    cr_, ci_ = carry_[2 * ch], carry_[2 * ch + 1]
                    ir = xr + tr * cr_ - ti * ci_
                    ii = xi + tr * ci_ + ti * cr_
                    if rev:
                        er = jnp.where(rowi == 7, cr_, pltpu.roll(ir, 7, 0))
                        ei = jnp.where(rowi == 7, ci_, pltpu.roll(ii, 7, 0))
                        carry_[2 * ch], carry_[2 * ch + 1] = ir[0:1], ii[0:1]
                    else:
                        er = jnp.where(rowi == 0, cr_, pltpu.roll(ir, 1, 0))
                        ei = jnp.where(rowi == 0, ci_, pltpu.roll(ii, 1, 0))
                        carry_[2 * ch], carry_[2 * ch + 1] = ir[7:8], ii[7:8]
                    dst[pl.ds(off, 8), lr] = er
                    dst[pl.ds(off, 8), li] = ei
                    if sref is not None:
                        s_r, s_i = sref[pl.ds(off, 8), lr], sref[pl.ds(off, 8), li]
                        da_[2 * ch] = da_[2 * ch] + s_r * er + s_i * ei
                        da_[2 * ch + 1] = da_[2 * ch + 1] + s_r * ei - s_i * er
                return (*carry_, *da_)

            st = lax.fori_loop(0, ng, step, (*carry, *da))
            carry, da = list(st[:2 * nch]), list(st[2 * nch:])
        if with_da:
            da_ref = refs[k0 + 2]
            for ch in range(nch):
                lr, li = lanes_of(ch)
                da_ref[:, lr] = da[2 * ch]
                da_ref[:, li] = da[2 * ch + 1]

    in_specs = [_zg_spec(jl, gb), _zg_spec(jc, gb), pl.BlockSpec((24, wb), lambda i: (0, i))]
    args = [z_l, z_c, coef]
    out_specs = [_zg_spec(jl, gb), _zg_spec(jc, gb)]
    out_shape = [jax.ShapeDtypeStruct(z_l.shape, F32), jax.ShapeDtypeStruct(z_c.shape, F32)]
    if with_da:
        in_specs += [_zg_spec(jl, gb), _zg_spec(jc, gb)]
        args += [s_l, s_c]
        out_specs.append(pl.BlockSpec((8, wb), lambda i: (0, i)))
        out_shape.append(jax.ShapeDtypeStruct((8, w_all), F32))
    return pl.pallas_call(body, name=name, grid=(w_all // wb,), in_specs=in_specs, out_specs=out_specs,
                          out_shape=out_shape, compiler_params=_cparams())(*args)


def _gt_tok_spec(g, tj):
    return pl.BlockSpec((g, S5_P, tj), lambda t, s: (0, s, t))


def _glu_fwd_gt(yt, z_cr, w_glu, b_glu):
    g, _, j = yt.shape
    e = g * S5_P
    tj = _cr_tile(j)

    def body(y_ref, z_ref, w_hbm, b_ref, o_ref, sg_ref, w_ref):
        @pl.when(jnp.logical_and(pl.program_id(0) == 0, pl.program_id(1) == 0))
        def _():
            pltpu.sync_copy(w_hbm, w_ref)

        y = jnp.transpose(y_ref[...].reshape(e, tj).astype(F32))
        gl = _gelu_parts(y)[0]
        sg = _sigmoid(_dot(gl.astype(BF), w_ref[...]) + b_ref[...])
        z = z_ref[...].astype(F32)
        o_ref[...] = (gl * sg * (z * _sigmoid(z))).astype(BF)
        sg_ref[...] = sg.astype(BF)

    tok = pl.BlockSpec((tj, e), lambda t, s: (t, s))
    return pl.pallas_call(
        body, name="l1_glu_fwd", grid=(j // tj, CHUNK),
        in_specs=[_gt_tok_spec(g, tj), tok, ANY, _full((1, e))], out_specs=[tok, tok],
        out_shape=[jax.ShapeDtypeStruct((j, CHUNK * e), BF), jax.ShapeDtypeStruct((j, CHUNK * e), BF)],
        scratch_shapes=[pltpu.VMEM(w_glu.shape, BF)], compiler_params=_cparams(),
    )(yt, z_cr, w_glu, b_glu)


def _glu_bwd_gt(dr_cr, gt1, w_out, w_glu, yt, z_cr, sg_cr):
    g, _, j = yt.shape
    e, d = w_out.shape
    tj = _cr_tile(j)

    def body(dr_ref, g_ref, wo_hbm, wg_hbm, y_ref, z_ref, sg_ref, dz_ref, dt_ref, dy_ref, wo_ref, wg_ref):
        @pl.when(jnp.logical_and(pl.program_id(0) == 0, pl.program_id(1) == 0))
        def _():
            pltpu.sync_copy(wo_hbm, wo_ref)
            pltpu.sync_copy(wg_hbm, wg_ref)

        do = (dr_ref[...].astype(F32) * g_ref[...]).astype(BF)
        dw = _dot_nt(do, wo_ref[...])
        y = jnp.transpose(y_ref[...].reshape(e, tj).astype(F32))
        gl, dgel = _gelu_parts(y)
        z = z_ref[...].astype(F32)
        sz = _sigmoid(z)
        sg = sg_ref[...].astype(F32)
        dg2 = dw * (z * sz)
        dz_ref[...] = (dw * gl * sg * (sz * (1.0 + z * (1.0 - sz)))).astype(BF)
        dt = (dg2 * gl * sg * (1.0 - sg)).astype(BF)
        dt_ref[...] = dt
        dy = (dg2 * sg + _dot_nt(dt, wg_ref[...])) * dgel
        dy_ref[...] = jnp.transpose(dy).reshape(g, S5_P, tj).astype(BF)

    tok_e = pl.BlockSpec((tj, e), lambda t, s: (t, s))
    return pl.pallas_call(
        body, name="l1_glu_bwd", grid=(j // tj, CHUNK),
        in_specs=[pl.BlockSpec((tj, d), lambda t, s: (t, s)), _full((1, d)), ANY, ANY, _gt_tok_spec(g, tj), tok_e, tok_e],
        out_specs=[tok_e, tok_e, _gt_tok_spec(g, tj)],
        out_shape=[jax.ShapeDtypeStruct((j, CHUNK * e), BF), jax.ShapeDtypeStruct((j, CHUNK * e), BF),
                   jax.ShapeDtypeStruct((g, GT_ROWS, j), BF)],
        scratch_shapes=[pltpu.VMEM(w_out.shape, BF), pltpu.VMEM(w_glu.shape, BF)], compiler_params=_cparams(),
    )(dr_cr, gt1, w_out, w_glu, yt, z_cr, sg_cr)


def _bwd_inproj1_gt(dut, dz_cr, wu_t, w_z, xh_cr, rs_cr, dr2_cr, vecs, tag):
    g, _, j = dut.shape
    e, d = wu_t.shape
    tj = _cr_tile(j)

    def body(du_ref, dz_ref, wu_hbm, wz_hbm, xh_ref, rs_ref, dr2_ref, v_ref, dr1_ref, acc_ref, wu_ref, wz_ref):
        @pl.when(jnp.logical_and(pl.program_id(0) == 0, pl.program_id(1) == 0))
        def _():
            pltpu.sync_copy(wu_hbm, wu_ref)
            pltpu.sync_copy(wz_hbm, wz_ref)
            acc_ref[...] = jnp.zeros_like(acc_ref)

        dh = _dot_tn(du_ref[...].reshape(e, tj), wu_ref[...]) + _dot_nt(dz_ref[...], wz_ref[...])
        xh = xh_ref[...]
        x1 = xh * v_ref[0:1, :] + v_ref[1:2, :]
        dx1 = DN_ALPHA * dr2_ref[...].astype(F32) + dh * v_ref[2:3, :]
        dxh = dx1 * v_ref[0:1, :]
        rstd = rs_ref[:, 0:1]
        dr1 = rstd * (dxh - jnp.mean(dxh, axis=-1, keepdims=True) - xh * jnp.mean(dxh * xh, axis=-1, keepdims=True))
        dr1_ref[...] = dr1.astype(BF)
        acc_ref[0:1, :] += jnp.sum(dh * x1, axis=0, keepdims=True)
        acc_ref[1:2, :] += jnp.sum(dh, axis=0, keepdims=True)
        acc_ref[2:3, :] += jnp.sum(dx1 * xh, axis=0, keepdims=True)
        acc_ref[3:4, :] += jnp.sum(dx1, axis=0, keepdims=True)

    tok_d = pl.BlockSpec((tj, d), lambda t, s: (t, s))
    return pl.pallas_call(
        body, name="l1_bwd_inproj_" + tag, grid=(j // tj, CHUNK),
        in_specs=[_gt_tok_spec(g, tj), pl.BlockSpec((tj, e), lambda t, s: (t, s)), ANY, ANY, tok_d,
                  pl.BlockSpec((tj, 128), lambda t, s: (t, s)), tok_d, _full((8, d))],
        out_specs=[tok_d, _full((8, d))],
        out_shape=[jax.ShapeDtypeStruct((j, CHUNK * d), BF), jax.ShapeDtypeStruct((8, d), F32)],
        scratch_shapes=[pltpu.VMEM(wu_t.shape, BF), pltpu.VMEM(w_z.shape, BF)], compiler_params=_cparams(),
    )(dut, dz_cr, wu_t, w_z, xh_cr, rs_cr, dr2_cr, vecs)


def _dw_gt(lhs_gt, rhs_cr, lhs_gelu, vec, bias_sum, init, out_dtype, name, xch=None):
    g, _, j = lhs_gt.shape
    e = g * S5_P
    n = rhs_cr.shape[1] // CHUNK
    tj = _cr_tile(j, 512 if j % 512 == 0 else 256)
    nh = 2 if e * n * 4 > (8 << 20) else 1
    tn = n // nh
    nt = j // tj
    has_init = init is not None

    def body(*refs):
        refs = list(refs)
        l_ref, r_ref = refs[0], refs[1]
        pos = 2
        v_ref = i_ref = bs_ref = None
        if vec is not None:
            v_ref = refs[pos]
            pos += 1
        if has_init:
            i_ref = refs[pos]
            pos += 1
        o_ref = refs[pos]
        pos += 1
        if bias_sum:
            bs_ref = refs[pos]
            pos += 1
        acc_ref = refs[pos]
        t, s = pl.program_id(1), pl.program_id(2)

        @pl.when(jnp.logical_and(t == 0, s == 0))
        def _():
            acc_ref[...] = i_ref[...] if has_init else jnp.zeros_like(acc_ref)
            if bias_sum:
                bs_ref[...] = jnp.zeros_like(bs_ref)

        lv = l_ref[...].reshape(e, tj)
        if lhs_gelu:
            lv = _gelu_parts(lv.astype(F32))[0].astype(BF)
        if vec is not None:
            rv = (r_ref[...] * v_ref[0:1, :] + v_ref[1:2, :]).astype(BF)
        else:
            rv = r_ref[...]
        acc_ref[...] += _dot(lv, rv)
        if bias_sum:
            bs_ref[0:1, :] += jnp.sum(rv.astype(F32), axis=0, keepdims=True)

        @pl.when(jnp.logical_and(t == nt - 1, s == CHUNK - 1))
        def _():
            o_ref[...] = acc_ref[...].astype(out_dtype)

    in_specs = [pl.BlockSpec((g, S5_P, tj), lambda h, t, s: (0, s, t)),
                pl.BlockSpec((tj, tn), lambda h, t, s: (t, s * nh + h))]
    args = [lhs_gt, rhs_cr]
    if vec is not None:
        in_specs.append(_full(vec.shape))
        args.append(vec)
    o_spec = pl.BlockSpec((e, tn), lambda h, t, s: (0, h))
    if has_init:
        in_specs.append(o_spec)
        args.append(init)
    out_specs, out_shape = [o_spec], [jax.ShapeDtypeStruct((e, n), out_dtype)]
    if bias_sum:
        out_specs.append(pl.BlockSpec((8, tn), lambda h, t, s: (0, h)))
        out_shape.append(jax.ShapeDtypeStruct((8, n), F32))
    res, extra = _hosted_call(body, xch, grid=(nh, nt, CHUNK), in_specs=in_specs, out_specs=out_specs,
                              out_shape=out_shape, scratch=[pltpu.VMEM((e, tn), F32)], args=args, name=name)
    if xch is not None:
        return (*res, extra) if bias_sum else (res[0], extra)
    return res if bias_sum else res[0]


def _scan_coef_g(lam_re, lam_im, log_step):
    g = lam_re.shape[1]
    ms = jnp.array([1, 2, 4, 0, 0, 0, 0, 0] + list(range(1, 9)) + list(range(8, 0, -1)), F32) * CHUNK
    dt = jnp.exp(log_step)[..., None]
    mag = jnp.exp(ms.reshape(-1, 1, 1, 1) * (lam_re * dt)[None])
    ang = ms.reshape(-1, 1, 1, 1) * (lam_im * dt)[None]
    cr, ci = mag * jnp.cos(ang), mag * jnp.sin(ang)
    both = jnp.stack([cr, ci], axis=2).reshape(24, 2, 2, g // 2, 2, S5_N)
    return both.transpose(0, 3, 1, 2, 4, 5).reshape(24, g * ZG_W)


def _s5_small(lam_re, lam_im, log_step, b_re, b_im, c_re, c_im, d_skip):
    g = lam_re.shape[1]
    t, p = CHUNK, S5_P
    dt = jnp.exp(log_step)[..., None]
    ks = jnp.arange(t + 1, dtype=F32).reshape(t + 1, 1, 1, 1)
    mag = jnp.exp(ks * (lam_re * dt)[None])
    ang = ks * (lam_im * dt)[None]
    pr, pi = mag * jnp.cos(ang), mag * jnp.sin(ang)
    ar, ai = pr[1], pi[1]
    qr, qi = ar - 1.0, ai
    den = lam_re * lam_re + lam_im * lam_im
    fr = (qr * lam_re + qi * lam_im) / den
    fi = (qi * lam_re - qr * lam_im) / den
    bt_re, bt_im = b_re.transpose(0, 1, 3, 2), b_im.transpose(0, 1, 3, 2)
    bbr = fr[:, :, None, :] * bt_re - fi[:, :, None, :] * bt_im
    bbi = fr[:, :, None, :] * bt_im + fi[:, :, None, :] * bt_re
    lay = lambda a_r, a_i: jnp.stack([a_r, a_i], axis=0).transpose(3, 2, 0, 1, 4)
    by_dir = lambda a, f0, f1: jnp.stack([f0(a[:, 0]), f1(a[:, 1])], axis=1)
    rev = lambda a: jnp.flip(a, axis=0)
    same = lambda a: a
    pwb = lay(by_dir(pr[:t], rev, same), by_dir(pi[:t], rev, same))
    pwc = lay(by_dir(pr[1:], same, rev), by_dir(pi[1:], same, rev))
    bb = jnp.stack([bbr, bbi], axis=0).transpose(2, 1, 0, 3, 4)
    cc = jnp.stack([c_re, c_im], axis=0).transpose(2, 1, 0, 3, 4)
    dmat = jnp.eye(p, dtype=F32)[None] * d_skip.reshape(g, p)[:, :, None]
    return pwb, pwc, bb, cc, dmat, pr[t], pi[t]


def _pair_cols(r, ri, g2):
    c0 = (r * 2 + ri) * 128 + g2 * S5_N
    return slice(c0, c0 + S5_N)


def _rows_rep(a):
    return jnp.broadcast_to(a[:, None, :], (CHUNK, S5_P, a.shape[-1])).reshape(GT_ROWS, a.shape[-1])


def _rows_tile(a):
    return jnp.broadcast_to(a[None], (CHUNK, S5_P, a.shape[-1])).reshape(GT_ROWS, a.shape[-1])


def _sum_blocks(a):
    return jnp.sum(a.reshape(CHUNK, S5_P, a.shape[-1]), axis=0)


def _sum_in_blocks(a):
    return jnp.sum(a.reshape(CHUNK, S5_P, a.shape[-1]), axis=1)


def _ab_rows(pwb_ref, bb_ref, k, r):
    prs, pis = _rows_rep(pwb_ref[k, r, 0]), _rows_rep(pwb_ref[k, r, 1])
    bbr, bbi = _rows_tile(bb_ref[k, r, 0]), _rows_tile(bb_ref[k, r, 1])
    return prs * bbr - pis * bbi, prs * bbi + pis * bbr, prs, pis, bbr, bbi


def _s5_weights_fwd(pwb, pwc, bb, cc, dmat):
    g = pwb.shape[0]
    gb = GROUPS_PER_STEP
    hp = lax.Precision.HIGHEST

    def body(pwb_ref, pwc_ref, bb_ref, cc_ref, dm_ref, mt_ref, mtt_ref, bc_ref, cct_ref):
        zeros = jnp.zeros((GT_ROWS, S5_N), BF)
        nt = (((1,), (1,)), ((), ()))
        for k in range(gb):
            g2 = k % 2
            kds = []
            for r in range(2):
                for ri in range(2):
                    bc_ref[k, :, _pair_cols(r, ri, 1 - g2)] = zeros
                    cct_ref[k, :, _pair_cols(r, ri, 1 - g2)] = zeros
                abr, abi = _ab_rows(pwb_ref, bb_ref, k, r)[:2]
                bc_ref[k, :, _pair_cols(r, 0, g2)] = abr.astype(BF)
                bc_ref[k, :, _pair_cols(r, 1, g2)] = abi.astype(BF)
                cr, ci = cc_ref[k, r, 0], cc_ref[k, r, 1]
                crt, cit = _rows_tile(cr), _rows_tile(ci)
                prt, pit = _rows_rep(pwc_ref[k, r, 0]), _rows_rep(pwc_ref[k, r, 1])
                cct_ref[k, :, _pair_cols(r, 0, g2)] = (crt * prt - cit * pit).astype(BF)
                cct_ref[k, :, _pair_cols(r, 1, g2)] = (-(crt * pit + cit * prt)).astype(BF)
                kds.append(lax.dot_general(abr, cr, nt, precision=hp, preferred_element_type=F32)
                           - lax.dot_general(abi, ci, nt, precision=hp, preferred_element_type=F32))
            blk = lambda a, s: a[s * S5_P:(s + 1) * S5_P]
            last = CHUNK - 1
            pieces = [blk(kds[1], last - i) for i in range(last)]
            pieces.append(blk(kds[0], last) + blk(kds[1], 0) + dm_ref[k])
            pieces += [blk(kds[0], last - d) for d in range(1, CHUNK)]
            qrow = jnp.concatenate(pieces, axis=1)
            mt = jnp.concatenate([qrow[:, (last - s) * S5_P:(last - s) * S5_P + GT_ROWS] for s in range(CHUNK)], axis=0)
            mt_ref[k] = mt.astype(BF)
            mtt_ref[k] = jnp.transpose(mt).astype(BF)

    small = lambda a: pl.BlockSpec((gb, *a.shape[1:]), lambda i: (i,) + (0,) * (a.ndim - 1))
    return pl.pallas_call(
        body, name="l1_s5_weights", grid=(g // gb,),
        in_specs=[small(pwb), small(pwc), small(bb), small(cc), small(dmat)],
        out_specs=[_w_spec(GT_ROWS), _w_spec(GT_ROWS), _w_spec(PAIR_W), _w_spec(PAIR_W)],
        out_shape=[jax.ShapeDtypeStruct((g, GT_ROWS, GT_ROWS), BF), jax.ShapeDtypeStruct((g, GT_ROWS, GT_ROWS), BF),
                   jax.ShapeDtypeStruct((g, GT_ROWS, PAIR_W), BF), jax.ShapeDtypeStruct((g, GT_ROWS, PAIR_W), BF)],
        compiler_params=_cparams(),
    )(pwb, pwc, bb, cc, dmat)


def _s5_weights_bwd(pwb, pwc, bb, cc, d_mt, d_bc, d_cct):
    g = pwb.shape[0]
    gb = GROUPS_PER_STEP
    hp = lax.Precision.HIGHEST

    def body(pwb_ref, pwc_ref, bb_ref, cc_ref, dmt_ref, dbc_ref, dcc_ref, dpwb_ref, dpwc_ref, dbb_ref, dccp_ref, ddm_ref):
        tn = (((0,), (0,)), ((), ()))
        nn = (((1,), (0,)), ((), ()))
        last = CHUNK - 1
        for k in range(gb):
            g2 = k % 2
            dq = None
            for s in range(CHUNK):
                parts = [dmt_ref[k, s * S5_P:(s + 1) * S5_P, :]]
                if s < last:
                    parts.insert(0, jnp.zeros((S5_P, (last - s) * S5_P), F32))
                if s > 0:
                    parts.append(jnp.zeros((S5_P, s * S5_P), F32))
                padded = jnp.concatenate(parts, axis=1) if len(parts) > 1 else parts[0]
                dq = padded if dq is None else dq + padded
            dblk = lambda d: dq[:, (last + d) * S5_P:(CHUNK + d) * S5_P]
            ddm_ref[k] = dblk(0)
            dkds = [jnp.concatenate([dblk(last - s) for s in range(CHUNK)], axis=0),
                    jnp.concatenate([dblk(-s) for s in range(CHUNK)], axis=0)]
            for r in range(2):
                abr, abi, prs, pis, bbr, bbi = _ab_rows(pwb_ref, bb_ref, k, r)
                cr, ci = cc_ref[k, r, 0], cc_ref[k, r, 1]
                dcr = lax.dot_general(dkds[r], abr, tn, precision=hp, preferred_element_type=F32)
                dci = -lax.dot_general(dkds[r], abi, tn, precision=hp, preferred_element_type=F32)
                dabr = (lax.dot_general(dkds[r], cr, nn, precision=hp, preferred_element_type=F32)
                        + dbc_ref[k, :, _pair_cols(r, 0, g2)])
                dabi = (-lax.dot_general(dkds[r], ci, nn, precision=hp, preferred_element_type=F32)
                        + dbc_ref[k, :, _pair_cols(r, 1, g2)])
                dbb_ref[k, r, 0] = _sum_blocks(prs * dabr + pis * dabi)
                dbb_ref[k, r, 1] = _sum_blocks(prs * dabi - pis * dabr)
                dpwb_ref[k, r, 0] = _sum_in_blocks(dabr * bbr + dabi * bbi)
                dpwb_ref[k, r, 1] = _sum_in_blocks(dabi * bbr - dabr * bbi)
                crt, cit = _rows_tile(cr), _rows_tile(ci)
                prt, pit = _rows_rep(pwc_ref[k, r, 0]), _rows_rep(pwc_ref[k, r, 1])
                d_re = dcc_ref[k, :, _pair_cols(r, 0, g2)]
                d_im = dcc_ref[k, :, _pair_cols(r, 1, g2)]
                dccp_ref[k, r, 0] = dcr + _sum_blocks(d_re * prt - d_im * pit)
                dccp_ref[k, r, 1] = dci - _sum_blocks(d_re * pit + d_im * prt)
                dpwc_ref[k, r, 0] = _sum_in_blocks(d_re * crt - d_im * cit)
                dpwc_ref[k, r, 1] = -_sum_in_blocks(d_re * cit + d_im * crt)

    small = lambda a: pl.BlockSpec((gb, *a.shape[1:]), lambda i: (i,) + (0,) * (a.ndim - 1))
    dmat_sds = jax.ShapeDtypeStruct((g, S5_P, S5_P), F32)
    return pl.pallas_call(
        body, name="l1_s5_weights_bwd", grid=(g // gb,),
        in_specs=[small(pwb), small(pwc), small(bb), small(cc), _w_spec(GT_ROWS), _w_spec(PAIR_W), _w_spec(PAIR_W)],
        out_specs=[small(pwb), small(pwc), small(bb), small(cc), small(dmat_sds)],
        out_shape=[jax.ShapeDtypeStruct(pwb.shape, F32), jax.ShapeDtypeStruct(pwc.shape, F32),
                   jax.ShapeDtypeStruct(bb.shape, F32), jax.ShapeDtypeStruct(cc.shape, F32), dmat_sds],
        compiler_params=_cparams(),
    )(pwb, pwc, bb, cc, d_mt, d_bc, d_cct)


def _from_cr(a, c):
    return a.reshape(a.shape[0] * CHUNK, c)


TR_COLS = 256


def _transpose_cols(a, ncols, name):
    r = a.shape[0]

    def body(a_ref, o_ref):
        o_ref[...] = jnp.transpose(a_ref[...].astype(F32)).astype(o_ref.dtype)

    return pl.pallas_call(
        body, name=name, grid=(ncols // TR_COLS,), in_specs=[pl.BlockSpec((r, TR_COLS), lambda i: (0, i))],
        out_specs=pl.BlockSpec((TR_COLS, r), lambda i: (i, 0)), out_shape=jax.ShapeDtypeStruct((ncols, r), a.dtype),
        compiler_params=_cparams(),
    )(a)


def _join_transposed(left_t, right, name):
    e, d = left_t.shape
    nu = e // TR_COLS

    def body(l_ref, r_ref, o_ref):
        i = pl.program_id(0)

        @pl.when(i < nu)
        def _():
            o_ref[...] = jnp.transpose(l_ref[...].astype(F32)).astype(o_ref.dtype)

        @pl.when(i >= nu)
        def _():
            o_ref[...] = r_ref[...]

    return pl.pallas_call(
        body, name=name, grid=(2 * nu,),
        in_specs=[pl.BlockSpec((TR_COLS, d), lambda i: (jnp.minimum(i, nu - 1), 0)),
                  pl.BlockSpec((d, TR_COLS), lambda i: (0, jnp.maximum(i - nu, 0)))],
        out_specs=pl.BlockSpec((d, TR_COLS), lambda i: (0, i)), out_shape=jax.ShapeDtypeStruct((d, 2 * e), right.dtype),
        compiler_params=_cparams(),
    )(left_t, right)


def _pad8(v):
    return jnp.concatenate([v, jnp.zeros((8 - v.shape[0], v.shape[1]), v.dtype)], axis=0)


def _local_step(x, c, ctx, c_ctx, loss_target, w, late=None, scatter=False, mod=None):
    l, d = x.shape
    lc = ctx.shape[0]
    tm = min(256, lc)
    assert lc == tm and l % tm == 0 and tm % GRID_W == 0 and (tm & (tm - 1)) == 0
    nl = l // tm

    own_mod = mod is None
    if own_mod:
        c8 = _pad8(jnp.stack([c, c_ctx]))
        mod = _ada_fwd(c8, w["ada_w"], w["ada_b"])
    sh = mod[:, :2, :d]
    sc = mod[:, :2, d:2 * d]
    gt = mod[:, :2, 2 * d:]
    ln_g, ln_b = w["ln_g"], w["ln_b"]

    a0, b0 = 1.0 + sc[0], sh[0]
    xch = _Exchange("gather2", [late[n][0] for n in late], [late[n][1] for n in late]) if late else None
    p42, tgt_cr, got = _inproj0(x, ctx, a0, b0, w["conv_w_in"], loss_target, tm, xch)
    if late:
        w = dict(w, **dict(zip(late, got)))
    e = w["conv_w_out"].shape[0]
    half = e // 2
    cw = w["conv_w"].reshape(3, 2, half)
    q3 = _conv_fwd(p42, cw, nl, tm, half)
    xh1_l, xh1_c, rs1_l, rs1_c, fx = _outproj_ln0(q3, w["conv_w_out"], x, ctx, gt[0], tm)
    jl, jc = l // CHUNK, lc // CHUNK

    g0, bb0 = ln_g[0:1], ln_b[0:1]
    a1 = g0 * (1.0 + sc[1])
    b1 = bb0 * (1.0 + sc[1]) + sh[1]
    wu_t = _transpose_cols(w["ssm_w_in"], e, "l1_w_in_u_t")
    w_z = w["ssm_w_in"][:, e:]
    ut_l, z_l = _inproj1_gt(xh1_l, a1[0:1], b1[0:1], wu_t, w_z, "lat")
    ut_c, _ = _inproj1_gt(xh1_c, a1[1:2], b1[1:2], wu_t, w_z, "ctx")
    s5 = (w["ssm_lam_re"], w["ssm_lam_im"], w["ssm_log_step"], w["ssm_b_re"], w["ssm_b_im"],
          w["ssm_c_re"], w["ssm_c_im"], w["ssm_d"])
    (pwb, pwc, bbw, ccw, dmat, _, _), s5_vjp = jax.vjp(_s5_small, *s5)
    mt_b, mtt_b, bc_b, cct_b = _s5_weights_fwd(pwb, pwc, bbw, ccw, dmat)
    coef = lax.stop_gradient(_scan_coef_g(*s5[:3]))
    zz_l, zz_c = _s5_z(ut_l, ut_c, bc_b)
    fwd_chains = ((("c", False), ("l", False)), (("c", True), ("l", True)))
    st_l, st_c = _scan_g(zz_l, zz_c, coef, fwd_chains, False, name="l1_scan_fwd")
    yt = _s5_y(ut_l, st_l, mtt_b, cct_b)
    b_glu = w["ssm_b_glu"].reshape(1, e)
    w_cr, sg_cr = _glu_fwd_gt(yt, z_l, w["ssm_w_glu"], b_glu)
    vec_f = _pad8(jnp.concatenate([g0, bb0, gt[1][0:1], ln_g[1:2], ln_b[1:2]], axis=0))
    dr2, acc_f = _final(w_cr, w["ssm_w_out"], xh1_l, tgt_cr, vec_f)
    loss = jnp.sum(acc_f[3])

    gt1 = gt[1][0:1]
    dz_l, dt_l, dyt = _glu_bwd_gt(dr2, gt1, w["ssm_w_out"], w["ssm_w_glu"], yt, z_l, sg_cr)
    g_w_out = _dw_cr(w_cr, dr2, "cr", "scaled", gt1, False, None, "l1_dw_out")
    ds_l = _s5_ds(dyt, cct_b)
    bwd_chains = ((("l", True), ("c", True)), (("l", False), ("c", False)))
    dzz_l, dzz_c, da = _scan_g(ds_l, jnp.zeros_like(zz_c), coef, bwd_chains, True, st_l, st_c, name="l1_scan_bwd")
    dut_l, dut_c = _s5_dx(dyt, dzz_l, dzz_c, mt_b, bc_b)
    d_mt, d_bc, d_cct = _s5_dw(ut_l, ut_c, dyt, dzz_l, dzz_c, st_l)
    n_g = e // S5_P
    da = jnp.sum(da, axis=0).reshape(n_g // 2, 2, 2, 2, S5_N).transpose(1, 2, 0, 3, 4)
    da = da.reshape(2, 2, n_g, S5_N)
    d_pwb, d_pwc, d_bb, d_ccp, d_dm = _s5_weights_bwd(pwb, pwc, bbw, ccw, d_mt, d_bc, d_cct)
    g_s5 = s5_vjp((d_pwb, d_pwc, d_bb, d_ccp, d_dm, da[:, 0], da[:, 1]))

    vec_l = _pad8(jnp.concatenate([g0, bb0, 1.0 + sc[1][0:1]], axis=0))
    vec_c = _pad8(jnp.concatenate([g0, bb0, 1.0 + sc[1][1:2]], axis=0))
    dr1_l, acc_l = _bwd_inproj1_gt(dut_l, dz_l, wu_t, w_z, xh1_l, rs1_l, dr2, vec_l, "lat")
    dr1_c, acc_c = _bwd_inproj1_gt(dut_c, jnp.zeros((jc, CHUNK * e), BF), wu_t, w_z, xh1_c, rs1_c,
                                   jnp.zeros((jc, CHUNK * d), BF), vec_c, "ctx")
    mod_l = jnp.concatenate([a1[0:1], b1[0:1]], axis=0)
    mod_c = jnp.concatenate([a1[1:2], b1[1:2]], axis=0)
    g_ut_c = _dw_gt(dut_c, xh1_c, False, mod_c, False, None, F32, "l1_dw_in_u_ctx")
    g_ut = _dw_gt(dut_l, xh1_l, False, mod_l, False, g_ut_c, BF, "l1_dw_in_u")
    g_in_z = _dw_cr(xh1_l, dz_l, "mod", "cr", mod_l, False, None, "l1_dw_in_z")
    g_w_in1 = _join_transposed(g_ut, g_in_z, "l1_dw_in_join")

    dr1_ln, dr1_cn = _from_cr(dr1_l, d), _from_cr(dr1_c, d)
    dq3, acc_g0 = _bwd_outproj0(dr1_ln, dr1_cn, gt[0], w["conv_w_out"], fx, tm)
    def carried(names, parts):
        return _Exchange("scatter", parts, [BIG[n] for n in names]) if scatter else None

    dp42, dcw, grad_x, acc_0, recv1 = _conv_bwd_inproj0(
        dq3, p42, cw, w["conv_w_in"], x, ctx, dr1_ln, dr1_cn, a0, nl, tm, carried(["ssm_w_in", "ssm_w_out"], [g_w_in1, g_w_out]))
    g_w_in0 = _dw_inproj0(x, ctx, a0, b0, dp42, tm)
    res = _dw_gt(yt, dt_l, True, None, True, None, BF, "l1_dw_glu", carried(["conv_w_in"], [g_w_in0]))
    g_w_glu, bsum, recv2 = res if scatter else (*res, [])
    g_b_glu = bsum[0]
    g_w_out0, recv3 = _dw_outproj0(q3, dr1_ln, dr1_cn, gt[0], tm, carried(["ssm_w_glu"], [g_w_glu]))
    recv = dict(zip(["ssm_w_in", "ssm_w_out", "conv_w_in", "ssm_w_glu"], recv1 + recv2 + recv3))

    zero = jnp.zeros((d,), F32)
    dm0 = jnp.stack([jnp.concatenate([acc_0[2], acc_0[0], acc_g0[0]]), jnp.concatenate([acc_0[3], acc_0[1], acc_g0[1]])])
    dm1 = jnp.stack([jnp.concatenate([acc_l[1], acc_l[0], acc_f[2]]), jnp.concatenate([acc_c[1], acc_c[0], zero])])
    if own_mod:
        g_ada_w, dc8 = _ada_bwd(c8, w["ada_w"], jnp.stack([_pad8(dm0), _pad8(dm1)]), BF)
        g_mod = {"c_ctx": dc8[0, 1] + dc8[1, 1], "ada_w": g_ada_w,
                 "ada_b": jnp.stack([dm0[0] + dm0[1], dm1[0] + dm1[1]])}
    else:
        g_mod = {"mod": jnp.stack([dm0, dm1])}

    grads = {
        **g_mod,
        "ln_g": jnp.stack([acc_l[2] + acc_c[2], acc_f[0]]),
        "ln_b": jnp.stack([acc_l[3] + acc_c[3], acc_f[1]]),
        "conv_w_in": g_w_in0, "conv_w": dcw[:3].reshape(3, e), "conv_w_out": g_w_out0,
        "ssm_w_in": g_w_in1,
        "ssm_lam_re": g_s5[0], "ssm_lam_im": g_s5[1], "ssm_log_step": g_s5[2],
        "ssm_b_re": g_s5[3], "ssm_b_im": g_s5[4], "ssm_c_re": g_s5[5], "ssm_c_im": g_s5[6], "ssm_d": g_s5[7],
        "ssm_w_glu": g_w_glu, "ssm_b_glu": g_b_glu, "ssm_w_out": g_w_out,
    }
    for n in recv:
        del grads[n]
    return loss, grad_x, grads, recv


WEIGHTS = ["c_ctx", "ada_w", "ada_b", "ln_g", "ln_b", "conv_w_in", "conv_w", "conv_w_out", "ssm_w_in",
           "ssm_lam_re", "ssm_lam_im", "ssm_log_step", "ssm_b_re", "ssm_b_im", "ssm_c_re", "ssm_c_im",
           "ssm_d", "ssm_w_glu", "ssm_b_glu", "ssm_w_out"]
BIG = {"ada_w": 1, "conv_w_in": 1, "conv_w_out": 0, "ssm_w_in": 1, "ssm_w_glu": 0, "ssm_w_out": 0}
SMALL_SHARDED = ["conv_w", "ssm_d", "ssm_b_glu"]
REPLICATED = ["c_ctx", "ada_b", "ln_g", "ln_b", "ssm_lam_re", "ssm_lam_im", "ssm_log_step",
              "ssm_b_re", "ssm_b_im", "ssm_c_re", "ssm_c_im"]
NATIVE_SMALL = ["ssm_b_re", "ssm_b_im", "ssm_c_re", "ssm_c_im"]


def _view2d(name, a):
    return a.reshape(-1, a.shape[-1])


def kernel(x, c, ctx, c_ctx, ada_w, ada_b, ln_g, ln_b, conv_w_in, conv_w, conv_w_out, ssm_w_in, ssm_lam_re, ssm_lam_im, ssm_log_step, ssm_b_re, ssm_b_im, ssm_c_re, ssm_c_im, ssm_d, ssm_w_glu, ssm_b_glu, ssm_w_out, loss_target, m_c_ctx, m_ada_w, m_ada_b, m_ln_g, m_ln_b, m_conv_w_in, m_conv_w, m_conv_w_out, m_ssm_w_in, m_ssm_lam_re, m_ssm_lam_im, m_ssm_log_step, m_ssm_b_re, m_ssm_b_im, m_ssm_c_re, m_ssm_c_im, m_ssm_d, m_ssm_w_glu, m_ssm_b_glu, m_ssm_w_out, v_c_ctx, v_ada_w, v_ada_b, v_ln_g, v_ln_b, v_conv_w_in, v_conv_w, v_conv_w_out, v_ssm_w_in, v_ssm_lam_re, v_ssm_lam_im, v_ssm_log_step, v_ssm_b_re, v_ssm_b_im, v_ssm_c_re, v_ssm_c_im, v_ssm_d, v_ssm_w_glu, v_ssm_b_glu, v_ssm_w_out):
    args = locals()
    wt = {n: args[n] for n in WEIGHTS}
    mt = {n: args["m_" + n] for n in WEIGHTS}
    vt = {n: args["v_" + n] for n in WEIGHTS}

    me = 4 * lax.axis_index("x") + 2 * lax.axis_index("y") + lax.axis_index("c")
    d = x.shape[-1]
    d3 = 3 * d
    wa = d3 // N_DEV

    big_names = [n for n in BIG if n != "ada_w"]
    shard = {n: _view2d(n, wt[n]).astype(BF) for n in big_names}
    small = jnp.concatenate([wt["conv_w"][0], wt["ssm_d"], wt["ssm_b_glu"]], axis=0)
    small = jnp.concatenate([small, jnp.zeros((3, small.shape[1]), F32)], axis=0)
    w_in_full, small_full, c_all = _all_gather([shard["conv_w_in"], small, _pad8(c)], [1, 1, 0], "gather_weights", "gather2")
    late = {n: (shard[n], BIG[n]) for n in big_names if n != "conv_w_in"}
    c16 = jnp.concatenate([c_all[::8], c_ctx[None], jnp.zeros((16 - N_DEV - 1, d), F32)], axis=0)
    ada_w_b = ada_w.astype(BF)
    ada_b_mine = lax.dynamic_slice_in_dim(ada_b, me * wa, wa, axis=1)
    mod_part = _ada_fwd(c16, ada_w_b, ada_b_mine)
    mod_all = _all_gather([mod_part.reshape(32, wa)], [1], "gather_mod")[0].reshape(2, 16, d3)
    mod = jnp.stack([lax.dynamic_index_in_dim(mod_all, me, axis=1, keepdims=False), mod_all[:, N_DEV]], axis=1)
    w = {
        "ln_g": ln_g, "ln_b": ln_b, "conv_w_in": w_in_full, "conv_w": small_full[0:3],
        "ssm_lam_re": ssm_lam_re[0], "ssm_lam_im": ssm_lam_im[0],
        "ssm_log_step": ssm_log_step[0], "ssm_b_re": ssm_b_re[0], "ssm_b_im": ssm_b_im[0],
        "ssm_c_re": ssm_c_re[0], "ssm_c_im": ssm_c_im[0], "ssm_d": small_full[3], "ssm_b_glu": small_full[4],
    }

    loss, grad_x, g, recv_big = _local_step(x[0], c[0], ctx[0], c_ctx, loss_target[0], w, late, True, mod)

    dmod_all = _all_gather([_pad8(g["mod"].reshape(4, d3))], [0], "gather_dmod")[0].reshape(N_DEV, 8, d3)
    dmod_all = dmod_all[:, :4].reshape(N_DEV, 2, 2, d3)
    dm_ctx = dmod_all[0, :, 1]
    for p in range(1, N_DEV):
        dm_ctx = dm_ctx + dmod_all[p, :, 1]
    dm16 = jnp.concatenate([dmod_all[:, :, 0].transpose(1, 0, 2), dm_ctx[:, None], jnp.zeros((2, 16 - N_DEV - 1, d3), F32)], axis=1)
    g_ada_w, dc16 = _ada_bwd(c16, ada_w_b, lax.dynamic_slice_in_dim(dm16, me * wa, wa, axis=2), F32)
    g["c_ctx"] = dc16[0, N_DEV] + dc16[1, N_DEV]
    g_ada_b = jnp.sum(dm16, axis=1)

    blob_names = [n for n in REPLICATED if n != "ada_b"] + SMALL_SHARDED
    flat = jnp.concatenate([g[n].reshape(-1).astype(F32) for n in blob_names] + [loss.reshape(1)])
    nflat = flat.shape[0]
    rows = -(-nflat // (N_DEV * 128 * 8)) * 8
    flat = jnp.concatenate([flat, jnp.zeros((N_DEV * rows * 128 - nflat,), F32)]).reshape(N_DEV * rows, 128)
    last = [n for n in big_names if n not in recv_big]
    recv = _all_to_all([_view2d(n, g[n]) for n in last] + [flat], [BIG[n] for n in last] + [0], "scatter_grads")
    recv_big.update(zip(last, recv[:-1]))
    blob_sum = _sum_partials(recv[-1])
    blob = _all_gather([blob_sum], [0], "gather_small_grads", "gather2")[0].reshape(-1)
    small_g, off = {"ada_b": g_ada_b}, 0
    for n in blob_names:
        shape = wt[n].shape if n in REPLICATED else (*wt[n].shape[:-1], wt[n].shape[-1] * N_DEV)
        size = math.prod(shape)
        small_g[n] = blob[off:off + size].reshape(shape)
        off += size
    loss = blob[off]
    for n in SMALL_SHARDED:
        size = wt[n].shape[-1]
        small_g[n] = lax.dynamic_slice_in_dim(small_g[n], me * size, size, axis=small_g[n].ndim - 1)

    out_g, out_d, out_m, out_v = {}, {}, {}, {}
    recv_big["ada_w"] = _view2d("ada_w", g_ada_w)[None]
    for n in BIG:
        stack = recv_big[n]
        shp = wt[n].shape
        res = _adamw(stack, _view2d(n, wt[n]), _view2d(n, mt[n]), _view2d(n, vt[n]), "adamw_" + n)
        out_g[n], out_d[n], out_m[n], out_v[n] = [r.reshape(shp) for r in res]
    for n in NATIVE_SMALL:
        shp = wt[n].shape
        v2 = lambda a: a.reshape(-1, shp[-1])
        res = _adamw(v2(small_g.pop(n))[None], v2(wt[n]), v2(mt[n]), v2(vt[n]), "adamw_" + n)
        out_g[n], out_d[n], out_m[n], out_v[n] = [r.reshape(shp) for r in res]
    names = list(small_g)
    cat = lambda t: jnp.concatenate([t[n].reshape(-1) for n in names])
    gs, ws, ms, vs = cat(small_g), cat(wt), cat(mt), cat(vt)
    ns = gs.shape[0]
    rs = -(-ns // (128 * 512)) * 512
    padr = lambda a: jnp.concatenate([a, jnp.ones((rs * 128 - ns,), F32)]).reshape(rs, 128)
    res = _adamw(padr(gs)[None], padr(ws), padr(ms), padr(vs), "adamw_small")
    off = 0
    for n in names:
        size = math.prod(wt[n].shape)
        out_g[n], out_d[n], out_m[n], out_v[n] = [r.reshape(-1)[off:off + size].reshape(wt[n].shape) for r in res]
        off += size

    return (loss, grad_x[None], *[out_g[n] for n in WEIGHTS], *[out_d[n] for n in WEIGHTS],
            *[out_m[n] for n in WEIGHTS], *[out_v[n] for n in WEIGHTS])
```

```python
import math

import jax
import jax.numpy as jnp
from jax import lax
from jax.experimental import pallas as pl
from jax.experimental.pallas import tpu as pltpu

F32 = jnp.float32
BF = jnp.bfloat16
MESH = pl.DeviceIdType.MESH
N_DEV = 8

GRID_W = 64
CHUNK = 16
S5_P = 16
S5_N = 64
LN_EPS = 1e-5
DN_ALPHA = 4.0 ** 0.25
ADAM_LR, ADAM_B1, ADAM_B2, ADAM_EPS, ADAM_WD, ADAM_STEP = 1e-3, 0.9, 0.999, 1e-8, 0.01, 10
GELU_C0 = math.sqrt(2.0 / math.pi)
GELU_C1 = 0.044715
VMEM_MB = 52

ANY = pl.BlockSpec(memory_space=pl.ANY)


def _cparams():
    return pltpu.CompilerParams(vmem_limit_bytes=VMEM_MB << 20)


def _dot(a, b):
    return jnp.dot(a, b, preferred_element_type=F32)


def _dot_nt(a, b):
    return lax.dot_general(a, b, (((1,), (1,)), ((), ())), preferred_element_type=F32)


def _dot_tn(a, b):
    return lax.dot_general(a, b, (((0,), (0,)), ((), ())), preferred_element_type=F32)


def _sigmoid(x):
    return 1.0 / (1.0 + jnp.exp(-x))


def _gelu_parts(y):
    u = y * y
    th = jnp.tanh(y * (GELU_C0 + (GELU_C0 * GELU_C1) * u))
    hy = 0.5 * y
    g = hy + hy * th
    dg = (0.5 + 0.5 * th) + hy * (1.0 - th * th) * (GELU_C0 + (3.0 * GELU_C0 * GELU_C1) * u)
    return g, dg


def _full(shape):
    nd = len(shape)
    return pl.BlockSpec(shape, lambda *_: (0,) * nd)


def _mesh_pos():
    x, y, c = lax.axis_index("x"), lax.axis_index("y"), lax.axis_index("c")
    return x, y, c


def _peer(pos, k):
    x, y, c = pos
    px = 1 - x if (k >> 2) & 1 else x
    py = 1 - y if (k >> 1) & 1 else y
    pc = 1 - c if k & 1 else c
    return (px, py, pc), 4 * px + 2 * py + pc


def _shard_at(ref, axis, idx, n):
    if axis == 0:
        return ref.at[pl.ds(idx * n, n)]
    return ref.at[:, pl.ds(idx * n, n)]


class _Exchange:
    def __init__(self, kind, arrays, axes):
        self.kind, self.axes, self.n = kind, list(axes), len(arrays)
        self.arrays = list(arrays)
        self.out_shape = []
        for s, ax in zip(arrays, axes):
            shp = list(s.shape)
            if kind == "scatter":
                shp[ax] //= N_DEV
                self.out_shape.append(jax.ShapeDtypeStruct((N_DEV, *shp), s.dtype))
            else:
                shp[ax] *= N_DEV
                self.out_shape.append(jax.ShapeDtypeStruct(tuple(shp), s.dtype))
        self.scratch = [pltpu.SemaphoreType.DMA((self.n, N_DEV - 1)), pltpu.SemaphoreType.DMA((self.n, N_DEV - 1)),
                        pltpu.SemaphoreType.DMA((self.n,))]

    def _copies(self, ins, outs, sems):
        send_sems, recv_sems, local_sems = sems
        pos = _mesh_pos()
        x, y, c = pos
        me = 4 * x + 2 * y + c
        local, sends, chained, recvs = [], [], [], []
        for i in range(self.n):
            ax = self.axes[i]
            if self.kind == "scatter":
                size = ins[i].shape[ax] // N_DEV
                src = lambda idx, i=i, ax=ax, size=size: _shard_at(ins[i], ax, idx, size)
                dst = lambda idx, i=i: outs[i].at[idx]
            else:
                size = ins[i].shape[ax]
                src = lambda idx, i=i: ins[i]
                dst = lambda idx, i=i, ax=ax, size=size: _shard_at(outs[i], ax, idx, size)

            def copy(k, s, d, to, i=i):
                return pltpu.make_async_remote_copy(src_ref=s, dst_ref=d, send_sem=send_sems.at[i, k],
                                                    recv_sem=recv_sems.at[i, k], device_id=to, device_id_type=MESH)

            local.append(pltpu.make_async_copy(src(me), dst(me), local_sems.at[i]))
            if self.kind == "gather2":
                sib, sib_i = (x, y, 1 - c), 4 * x + 2 * y + (1 - c)
                chips = [(1 - x, y), (x, 1 - y), (1 - x, 1 - y)]
                sends.append(copy(0, src(me), dst(me), sib))
                recvs.append(copy(0, src(me), dst(sib_i), sib))
                for j, (cx, cy) in enumerate(chips):
                    same, other = 4 * cx + 2 * cy + c, 4 * cx + 2 * cy + (1 - c)
                    sends.append(copy(1 + j, src(me), dst(me), (cx, cy, c)))
                    chained.append((copy(1 + j, dst(same), dst(same), (cx, cy, c)), copy(4 + j, dst(same), dst(same), sib)))
                    recvs.append(copy(4 + j, dst(other), dst(other), sib))
            else:
                for k in range(1, N_DEV):
                    peer, pidx = _peer(pos, k)
                    out_src = src(pidx) if self.kind == "scatter" else src(me)
                    sends.append(copy(k - 1, out_src, dst(me), peer))
                    recvs.append(copy(k - 1, out_src, dst(pidx), peer))
        return local, sends, chained, recvs

    def start(self, ins, outs, sems):
        local, sends, _, _ = self._copies(ins, outs, sems)
        for cp in local + sends:
            cp.start()

    def wait(self, ins, outs, sems):
        local, sends, chained, recvs = self._copies(ins, outs, sems)
        for arrival, released in chained:
            arrival.wait_recv()
            released.start()
        for cp in recvs:
            cp.wait_recv()
        for cp in sends + [released for _, released in chained]:
            cp.wait_send()
        for cp in local:
            cp.wait()

    def run(self, name):
        n = self.n

        def body(*refs):
            ins, outs, sems = refs[:n], refs[n:2 * n], refs[2 * n:]
            self.start(ins, outs, sems)
            self.wait(ins, outs, sems)

        return pl.pallas_call(body, name=name, out_shape=self.out_shape, in_specs=[ANY] * n, out_specs=[ANY] * n,
                              scratch_shapes=self.scratch)(*self.arrays)


def _hosted_call(body, xch, grid, in_specs, out_specs, out_shape, scratch, args, name):
    out_specs, out_shape = list(out_specs), list(out_shape)
    n_in, n_out = len(in_specs), len(out_specs)
    if xch is None:
        res = pl.pallas_call(body, name=name, grid=grid, in_specs=in_specs, out_specs=out_specs, out_shape=out_shape,
                             scratch_shapes=list(scratch), compiler_params=_cparams())(*args)
        return list(res), []
    n = xch.n
    rank = len(grid)

    def wrapped(*refs):
        ins, x_ins = refs[:n_in], refs[n_in:n_in + n]
        outs = refs[n_in + n:n_in + n + n_out]
        x_outs = refs[n_in + n + n_out:n_in + 2 * n + n_out]
        rest = refs[n_in + 2 * n + n_out:]
        own, sems = rest[:len(rest) - 3], rest[len(rest) - 3:]
        ids = [pl.program_id(a) for a in range(rank)]
        first, last = ids[0] == 0, ids[0] == grid[0] - 1
        for a in range(1, rank):
            first = jnp.logical_and(first, ids[a] == 0)
            last = jnp.logical_and(last, ids[a] == grid[a] - 1)

        @pl.when(first)
        def _():
            xch.start(x_ins, x_outs, sems)

        body(*ins, *outs, *own)

        @pl.when(last)
        def _():
            xch.wait(x_ins, x_outs, sems)

    res = pl.pallas_call(
        wrapped, name=name, grid=grid, in_specs=list(in_specs) + [ANY] * n, out_specs=out_specs + [ANY] * n,
        out_shape=out_shape + xch.out_shape, scratch_shapes=list(scratch) + xch.scratch, compiler_params=_cparams(),
    )(*args, *xch.arrays)
    return list(res[:n_out]), list(res[n_out:])


def _all_gather(shards, axes, name, kind="gather"):
    return _Exchange(kind, shards, axes).run(name)


def _all_to_all(parts, axes, name):
    return _Exchange("scatter", parts, axes).run(name)


def _ada_fwd(cv, ada_w, ada_b):
    nl, d, wd = ada_w.shape
    r = cv.shape[0]

    def body(c_ref, w_ref, b_ref, o_ref):
        c = c_ref[...]
        s = (c * _sigmoid(c)).astype(BF)
        o_ref[0] = _dot(s, w_ref[0]) + b_ref[0]

    return pl.pallas_call(
        body, name="ada_fwd", grid=(nl,),
        in_specs=[_full((r, d)), pl.BlockSpec((1, d, wd), lambda l: (l, 0, 0)), pl.BlockSpec((1, 1, wd), lambda l: (l, 0, 0))],
        out_specs=pl.BlockSpec((1, r, wd), lambda l: (l, 0, 0)),
        out_shape=jax.ShapeDtypeStruct((nl, r, wd), F32), compiler_params=_cparams(),
    )(cv, ada_w, ada_b.reshape(nl, 1, wd))


def _ada_bwd(cv, ada_w, dm, out_dtype):
    nl, d, wd = ada_w.shape
    r = cv.shape[0]

    def body(c_ref, w_ref, dm_ref, dw_ref, dc_ref):
        c = c_ref[...]
        sg = _sigmoid(c)
        s = (c * sg).astype(BF)
        dmv = dm_ref[0].astype(BF)
        dw_ref[0] = _dot_tn(s, dmv).astype(out_dtype)
        dc_ref[0] = _dot_nt(dmv, w_ref[0]) * (sg * (1.0 + c * (1.0 - sg)))

    return pl.pallas_call(
        body, name="ada_bwd", grid=(nl,),
        in_specs=[_full((r, d)), pl.BlockSpec((1, d, wd), lambda l: (l, 0, 0)), pl.BlockSpec((1, r, wd), lambda l: (l, 0, 0))],
        out_specs=[pl.BlockSpec((1, d, wd), lambda l: (l, 0, 0)), pl.BlockSpec((1, r, d), lambda l: (l, 0, 0))],
        out_shape=[jax.ShapeDtypeStruct((nl, d, wd), out_dtype), jax.ShapeDtypeStruct((nl, r, d), F32)],
        compiler_params=_cparams(),
    )(cv, ada_w, dm)


def _sum_partials(stack):
    _, r, c = stack.shape

    def body(s_ref, o_ref):
        acc = s_ref[0]
        for p in range(1, N_DEV):
            acc = acc + s_ref[p]
        o_ref[...] = acc

    return pl.pallas_call(body, name="sum_partials", out_shape=jax.ShapeDtypeStruct((r, c), F32),
                          in_specs=[_full(stack.shape)], out_specs=_full((r, c)), grid=(1,),
                          compiler_params=_cparams())(stack)


def _adamw(gstack, w, m, v, name):
    p, r, c = gstack.shape
    tr = r
    for cand in (512 if c <= 256 else 256, 128, 64, 32, 16, 8):
        if r % cand == 0 and r > cand:
            tr = cand
            break
    bc1 = 1.0 - ADAM_B1 ** ADAM_STEP
    bc2 = 1.0 - ADAM_B2 ** ADAM_STEP

    def body(g_ref, w_ref, m_ref, v_ref, go_ref, d_ref, mo_ref, vo_ref):
        g = g_ref[0].astype(F32)
        for q in range(1, p):
            g = g + g_ref[q].astype(F32)
        mn = ADAM_B1 * m_ref[...] + (1.0 - ADAM_B1) * g
        vn = ADAM_B2 * v_ref[...] + (1.0 - ADAM_B2) * (g * g)
        go_ref[...] = g
        mo_ref[...] = mn
        vo_ref[...] = vn
        d_ref[...] = -ADAM_LR * ((mn / bc1) / (jnp.sqrt(vn / bc2) + ADAM_EPS) + ADAM_WD * w_ref[...])

    row = pl.BlockSpec((tr, c), lambda i: (i, 0))
    sds = jax.ShapeDtypeStruct((r, c), F32)
    return pl.pallas_call(
        body, name=name, grid=(r // tr,),
        in_specs=[pl.BlockSpec((p, tr, c), lambda i: (0, i, 0)), row, row, row],
        out_specs=[row, row, row, row], out_shape=[sds, sds, sds, sds], compiler_params=_cparams(),
    )(gstack, w, m, v)


def _lat_or_ctx_specs(tm, d, nl, grid_rank, row_axis):
    def lat(*ids):
        return (jnp.minimum(ids[row_axis], nl - 1), 0)

    def ctx(*ids):
        return (jnp.maximum(ids[row_axis] - nl, 0), 0)

    return pl.BlockSpec((tm, d), lat), pl.BlockSpec((tm, d), ctx)


def _sel_row(ref, is_ctx):
    return jnp.where(is_ctx, ref[1:2, :], ref[0:1, :])


def _inproj0(x, ctx, a2, b2, w, tgt, tm, xch=None):
    l, d = x.shape
    nl, nc = l // tm, ctx.shape[0] // tm
    e = w.shape[1] // 4
    half = e // 2
    tjo = tm // CHUNK

    def body(x_ref, c_ref, a_ref, b_ref, w_hbm, t_ref, o_ref, tc_ref, w_ref, ts_ref):
        i = pl.program_id(0)

        @pl.when(i == 0)
        def _():
            pltpu.sync_copy(w_hbm, w_ref)

        is_ctx = i >= nl
        xv = jnp.where(is_ctx, c_ref[...], x_ref[...])
        h = (xv * _sel_row(a_ref, is_ctx) + _sel_row(b_ref, is_ctx)).astype(BF)
        for k in range(4):
            r = _dot(h, w_ref[:, k * e:(k + 1) * e])
            o_ref[k, 0] = r[:, :half].astype(BF)
            o_ref[k, 1] = r[:, half:].astype(BF)

        @pl.when(jnp.logical_not(is_ctx))
        def _():
            for lb in range(d // 128):
                ts_ref[lb] = t_ref[:, lb * 128:(lb + 1) * 128]
            for s in range(CHUNK):
                for lb in range(d // 128):
                    tc_ref[:, s * d + lb * 128:s * d + (lb + 1) * 128] = ts_ref.at[lb][pl.ds(s, tjo, stride=CHUNK), :]

    lat, cx = _lat_or_ctx_specs(tm, d, nl, 1, 0)
    (p42, tgt_cr), extra = _hosted_call(
        body, xch, grid=(nl + nc,),
        in_specs=[lat, cx, _full((2, d)), _full((2, d)), ANY, lat],
        out_specs=[pl.BlockSpec((4, 2, tm, half), lambda i: (0, 0, i, 0)),
                   pl.BlockSpec((tjo, CHUNK * d), lambda i: (jnp.minimum(i, nl - 1), 0))],
        out_shape=[jax.ShapeDtypeStruct((4, 2, l + ctx.shape[0], half), BF),
                   jax.ShapeDtypeStruct((l // CHUNK, CHUNK * d), F32)],
        scratch=[pltpu.VMEM(w.shape, BF), pltpu.VMEM((d // 128, tm, 128), F32)],
        args=(x, ctx, a2, b2, w, tgt), name="l0_inproj")
    return p42, tgt_cr, extra


def _conv_taps(u, w_up, w_mid, w_dn, pos, rl, tm):
    up = jnp.where(pos == 0, 0.0, pltpu.roll(u, 1, 0))
    dn = jnp.where(pos == rl - 1, 0.0, pltpu.roll(u, tm - 1, 0))
    return w_up * up + w_mid * u + w_dn * dn, up, dn


def _conv_halo_specs(tm, tc, nl, lead):
    hb = tm // GRID_W

    def prev(j, i):
        return (0, 1, jnp.maximum(jnp.minimum(i, nl - 1) * hb - 1, 0), j)

    def nxt(j, i):
        return (0, 1, jnp.minimum((jnp.minimum(i, nl - 1) + 1) * hb, nl * hb - 1), j)

    return pl.BlockSpec((lead, 1, GRID_W, tc), prev), pl.BlockSpec((lead, 1, GRID_W, tc), nxt)


def _conv_fwd(p42, cw, nl, tm, tc):
    _, _, r, half = p42.shape
    nt = r // tm

    def body(p_ref, hp_ref, hn_ref, cw_ref, o_ref):
        i = pl.program_id(1)
        is_ctx = i >= nl
        row = lax.broadcasted_iota(jnp.int32, (tm, tc), 0)
        rl = jnp.where(is_ctx, tm, GRID_W)
        pos = jnp.bitwise_and(row, rl - 1)

        def gate(hv, yc):
            bg = p_ref[0, hv].astype(F32)
            z = p_ref[3, hv].astype(F32)
            return (bg * yc * (z * _sigmoid(z))).astype(BF)

        u_h = p_ref[1, 0].astype(F32) * p_ref[2, 0].astype(F32)
        w_h = cw_ref[:, 0, :]
        o_ref[0] = gate(0, _conv_taps(u_h, w_h[0:1], w_h[1:2], w_h[2:3], pos, rl, tm)[0])
        u_v = p_ref[1, 1].astype(F32) * p_ref[2, 1].astype(F32)
        w_v = cw_ref[:, 1, :]

        @pl.when(is_ctx)
        def _():
            o_ref[1] = gate(1, _conv_taps(u_v, w_v[0:1], w_v[1:2], w_v[2:3], pos, rl, tm)[0])

        @pl.when(jnp.logical_not(is_ctx))
        def _():
            up = hp_ref[1, 0].astype(F32) * hp_ref[2, 0].astype(F32) * (i > 0).astype(F32)
            dn = hn_ref[1, 0].astype(F32) * hn_ref[2, 0].astype(F32) * (i < nl - 1).astype(F32)
            ext = jnp.concatenate([up, u_v, dn], axis=0)
            yc = w_v[0:1] * ext[0:tm] + w_v[1:2] * u_v + w_v[2:3] * ext[2 * GRID_W:tm + 2 * GRID_W]
            o_ref[1] = gate(1, yc)

    hp, hn = _conv_halo_specs(tm, tc, nl, 4)
    return pl.pallas_call(
        body, name="l0_conv_fwd", grid=(half // tc, nt),
        in_specs=[pl.BlockSpec((4, 2, tm, tc), lambda j, i: (0, 0, i, j)), hp, hn,
                  pl.BlockSpec((3, 2, tc), lambda j, i: (0, 0, j))],
        out_specs=pl.BlockSpec((2, tm, tc), lambda j, i: (0, i, j)),
        out_shape=jax.ShapeDtypeStruct((2, r, half), BF), compiler_params=_cparams(),
    )(p42, p42, p42, cw)


def _outproj_ln0(q3, w_out, x, ctx, gt2, tm):
    l, d = x.shape
    lc = ctx.shape[0]
    nl, nc = l // tm, lc // tm
    _, r, half = q3.shape
    tjo = tm // CHUNK

    def body(q_ref, w_hbm, x_ref, c_ref, g_ref, xl_ref, xc_ref, rl_ref, rc_ref, fx_ref, w_ref, xs_ref, rs_ref):
        i = pl.program_id(0)

        @pl.when(i == 0)
        def _():
            pltpu.sync_copy(w_hbm, w_ref)

        is_ctx = i >= nl
        fx = _dot(q_ref[0], w_ref[:half, :]) + _dot(q_ref[1], w_ref[half:, :])
        xv = jnp.where(is_ctx, c_ref[...], x_ref[...])
        rr = DN_ALPHA * xv + _sel_row(g_ref, is_ctx) * fx
        mu = jnp.mean(rr, axis=-1, keepdims=True)
        cen = rr - mu
        rstd = lax.rsqrt(jnp.mean(cen * cen, axis=-1, keepdims=True) + LN_EPS)
        xh = cen * rstd
        for lb in range(d // 128):
            xs_ref[lb] = xh[:, lb * 128:(lb + 1) * 128]
        rs_ref[...] = jnp.broadcast_to(rstd, (tm, 128))
        fx_ref[...] = fx.astype(BF)

        def to_cr(xo_ref, ro_ref):
            for s in range(CHUNK):
                for lb in range(d // 128):
                    xo_ref[:, s * d + lb * 128:s * d + (lb + 1) * 128] = xs_ref.at[lb][pl.ds(s, tjo, stride=CHUNK), :]
                ro_ref[:, s * 128:(s + 1) * 128] = rs_ref[pl.ds(s, tjo, stride=CHUNK), :]

        @pl.when(jnp.logical_not(is_ctx))
        def _():
            to_cr(xl_ref, rl_ref)

        @pl.when(is_ctx)
        def _():
            to_cr(xc_ref, rc_ref)

    lat, cx = _lat_or_ctx_specs(tm, d, nl, 1, 0)
    lat_o = lambda w_: pl.BlockSpec((tjo, CHUNK * w_), lambda i: (jnp.minimum(i, nl - 1), 0))
    ctx_o = lambda w_: pl.BlockSpec((tjo, CHUNK * w_), lambda i: (jnp.maximum(i - nl, 0), 0))
    return pl.pallas_call(
        body, name="l0_outproj_ln", grid=(nl + nc,),
        in_specs=[pl.BlockSpec((2, tm, half), lambda i: (0, i, 0)), ANY, lat, cx, _full((2, d))],
        out_specs=[lat_o(d), ctx_o(d), lat_o(128), ctx_o(128), pl.BlockSpec((tm, d), lambda i: (i, 0))],
        out_shape=[jax.ShapeDtypeStruct((l // CHUNK, CHUNK * d), F32), jax.ShapeDtypeStruct((lc // CHUNK, CHUNK * d), F32),
                   jax.ShapeDtypeStruct((l // CHUNK, CHUNK * 128), F32), jax.ShapeDtypeStruct((lc // CHUNK, CHUNK * 128), F32),
                   jax.ShapeDtypeStruct((r, d), BF)],
        scratch_shapes=[pltpu.VMEM(w_out.shape, BF), pltpu.VMEM((d // 128, tm, 128), F32), pltpu.VMEM((tm, 128), F32)],
        compiler_params=_cparams(),
    )(q3, w_out, x, ctx, gt2)


def _bwd_outproj0(dr_l, dr_c, gt2, w_out, fx, tm):
    l, d = dr_l.shape
    nl, nc = l // tm, dr_c.shape[0] // tm
    e = w_out.shape[0]
    half = e // 2
    r = l + dr_c.shape[0]

    def body(dl_ref, dc_ref, g_ref, w_hbm, fx_ref, dq_ref, acc_ref, w_ref):
        i = pl.program_id(0)

        @pl.when(i == 0)
        def _():
            pltpu.sync_copy(w_hbm, w_ref)
            acc_ref[...] = jnp.zeros_like(acc_ref)

        is_ctx = i >= nl
        dr = jnp.where(is_ctx, dc_ref[...], dl_ref[...]).astype(F32)
        dfx = (dr * _sel_row(g_ref, is_ctx)).astype(BF)
        dq_ref[0] = _dot_nt(dfx, w_ref[:half, :]).astype(BF)
        dq_ref[1] = _dot_nt(dfx, w_ref[half:, :]).astype(BF)
        s = jnp.sum(dr * fx_ref[...].astype(F32), axis=0, keepdims=True)
        sel = is_ctx.astype(F32)
        acc_ref[0:1, :] += s * (1.0 - sel)
        acc_ref[1:2, :] += s * sel

    lat, cx = _lat_or_ctx_specs(tm, d, nl, 1, 0)
    return pl.pallas_call(
        body, name="l0_bwd_outproj", grid=(nl + nc,),
        in_specs=[lat, cx, _full((2, d)), ANY, pl.BlockSpec((tm, d), lambda i: (i, 0))],
        out_specs=[pl.BlockSpec((2, tm, half), lambda i: (0, i, 0)), _full((8, d))],
        out_shape=[jax.ShapeDtypeStruct((2, r, half), BF), jax.ShapeDtypeStruct((8, d), F32)],
        scratch_shapes=[pltpu.VMEM(w_out.shape, BF)], compiler_params=_cparams(),
    )(dr_l, dr_c, gt2, w_out, fx)


def _conv_bwd_inproj0(dq3, p42, cw, w_in, x, ctx, dr_l, dr_c, a2, nl, tm, xch=None):
    l, d = x.shape
    _, _, r, half = p42.shape
    nt = r // tm
    e = 2 * half
    cc = min(512, half)
    n_cc = half // cc

    def body(dq_ref, dqp_ref, dqn_ref, p_ref, hp_ref, hn_ref, cw_ref, w_hbm, x_ref, c_ref, dl_ref, dc_ref, a_ref,
             dp_ref, dw_ref, gx_ref, acc_ref, w_ref, dh_ref):
        i, hv = pl.program_id(0), pl.program_id(1)
        is_ctx = i >= nl

        @pl.when(jnp.logical_and(i == 0, hv == 0))
        def _():
            pltpu.sync_copy(w_hbm, w_ref)
            acc_ref[...] = jnp.zeros_like(acc_ref)
            dw_ref[...] = jnp.zeros_like(dw_ref)

        row = lax.broadcasted_iota(jnp.int32, (tm, cc), 0)
        rl = jnp.where(is_ctx, tm, GRID_W)
        pos = jnp.bitwise_and(row, rl - 1)

        def pieces(dq, bg, z):
            sz = _sigmoid(z)
            sil = z * sz
            return dq * bg * sil, dq * sil, dq * bg * (sz * (1.0 + z * (1.0 - sz)))

        def emit(hvs, c, parts, dyc, u_up, u, u_dn, dh):
            lanes = slice(c * cc, (c + 1) * cc)
            for k, part in enumerate(parts):
                pb = part.astype(BF)
                dp_ref[k, 0, :, lanes] = pb
                c0 = k * e + hvs * half + c * cc
                t = _dot_nt(pb, w_ref[:, c0:c0 + cc])
                dh = t if dh is None else dh + t
            dw_ref[0:1, hvs, lanes] += jnp.sum(dyc * u_up, axis=0, keepdims=True)
            dw_ref[1:2, hvs, lanes] += jnp.sum(dyc * u, axis=0, keepdims=True)
            dw_ref[2:3, hvs, lanes] += jnp.sum(dyc * u_dn, axis=0, keepdims=True)
            return dh

        def seq_half(hvs):
            dh = None
            for c in range(n_cc):
                lanes = slice(c * cc, (c + 1) * cc)
                bg, cg = p_ref[0, 0, :, lanes].astype(F32), p_ref[1, 0, :, lanes].astype(F32)
                v, z = p_ref[2, 0, :, lanes].astype(F32), p_ref[3, 0, :, lanes].astype(F32)
                w = cw_ref[:, hvs, lanes]
                u = cg * v
                yc, u_up, u_dn = _conv_taps(u, w[0:1], w[1:2], w[2:3], pos, rl, tm)
                dyc, dbg_f, dz_f = pieces(dq_ref[0, :, lanes].astype(F32), bg, z)
                du = _conv_taps(dyc, w[2:3], w[1:2], w[0:1], pos, rl, tm)[0]
                dh = emit(hvs, c, (dbg_f * yc, du * v, du * cg, dz_f * yc), dyc, u_up, u, u_dn, dh)
            return dh

        def col_half():
            m_up = (i > 0).astype(F32)
            m_dn = (i < nl - 1).astype(F32)
            dh = None
            for c in range(n_cc):
                lanes = slice(c * cc, (c + 1) * cc)
                bg, cg = p_ref[0, 0, :, lanes].astype(F32), p_ref[1, 0, :, lanes].astype(F32)
                v, z = p_ref[2, 0, :, lanes].astype(F32), p_ref[3, 0, :, lanes].astype(F32)
                w = cw_ref[:, 1, lanes]
                u = cg * v

                def halo(h_ref, dqh_ref, msk):
                    hb, hc = h_ref[0, 0, :, lanes].astype(F32), h_ref[1, 0, :, lanes].astype(F32)
                    hv_, hz = h_ref[2, 0, :, lanes].astype(F32), h_ref[3, 0, :, lanes].astype(F32)
                    return hc * hv_ * msk, pieces(dqh_ref[0, :, lanes].astype(F32), hb, hz)[0] * msk

                u_p, dyc_p = halo(hp_ref, dqp_ref, m_up)
                u_n, dyc_n = halo(hn_ref, dqn_ref, m_dn)
                u_ext = jnp.concatenate([u_p, u, u_n], axis=0)
                u_up, u_dn = u_ext[0:tm], u_ext[2 * GRID_W:tm + 2 * GRID_W]
                yc = w[0:1] * u_up + w[1:2] * u + w[2:3] * u_dn
                dyc, dbg_f, dz_f = pieces(dq_ref[0, :, lanes].astype(F32), bg, z)
                d_ext = jnp.concatenate([dyc_p, dyc, dyc_n], axis=0)
                du = w[0:1] * d_ext[2 * GRID_W:tm + 2 * GRID_W] + w[1:2] * dyc + w[2:3] * d_ext[0:tm]
                dh = emit(1, c, (dbg_f * yc, du * v, du * cg, dz_f * yc), dyc, u_up, u, u_dn, dh)
            return dh

        @pl.when(hv == 0)
        def _():
            dh_ref[...] = seq_half(0)

        @pl.when(jnp.logical_and(hv == 1, is_ctx))
        def _():
            dh_ref[...] += seq_half(1)

        @pl.when(jnp.logical_and(hv == 1, jnp.logical_not(is_ctx)))
        def _():
            dh_ref[...] += col_half()

        @pl.when(hv == 1)
        def _():
            dh = dh_ref[...]
            xv = jnp.where(is_ctx, c_ref[...], x_ref[...])
            s_sc = jnp.sum(dh * xv, axis=0, keepdims=True)
            s_sh = jnp.sum(dh, axis=0, keepdims=True)
            sel = is_ctx.astype(F32)
            acc_ref[0:1, :] += s_sc * (1.0 - sel)
            acc_ref[1:2, :] += s_sc * sel
            acc_ref[2:3, :] += s_sh * (1.0 - sel)
            acc_ref[3:4, :] += s_sh * sel

        @pl.when(jnp.logical_and(hv == 1, jnp.logical_not(is_ctx)))
        def _():
            gx_ref[...] = DN_ALPHA * dl_ref[...].astype(F32) + dh_ref[...] * a_ref[0:1, :]

    hb = tm // GRID_W
    prev_blk = lambda i: jnp.maximum(jnp.minimum(i, nl - 1) * hb - 1, 0)
    next_blk = lambda i: jnp.minimum((jnp.minimum(i, nl - 1) + 1) * hb, nl * hb - 1)
    lat, cx = _lat_or_ctx_specs(tm, d, nl, 2, 0)
    (dp42, dcw, gx, acc), extra = _hosted_call(
        body, xch, grid=(nt, 2),
        in_specs=[pl.BlockSpec((1, tm, half), lambda i, h: (h, i, 0)),
                  pl.BlockSpec((1, GRID_W, half), lambda i, h: (1, prev_blk(i), 0)),
                  pl.BlockSpec((1, GRID_W, half), lambda i, h: (1, next_blk(i), 0)),
                  pl.BlockSpec((4, 1, tm, half), lambda i, h: (0, h, i, 0)),
                  pl.BlockSpec((4, 1, GRID_W, half), lambda i, h: (0, 1, prev_blk(i), 0)),
                  pl.BlockSpec((4, 1, GRID_W, half), lambda i, h: (0, 1, next_blk(i), 0)),
                  _full((3, 2, half)), ANY, lat, cx, lat, cx, _full((2, d))],
        out_specs=[pl.BlockSpec((4, 1, tm, half), lambda i, h: (0, h, i, 0)), _full((8, 2, half)),
                   pl.BlockSpec((tm, d), lambda i, h: (jnp.minimum(i, nl - 1), 0)), _full((8, d))],
        out_shape=[jax.ShapeDtypeStruct(p42.shape, BF), jax.ShapeDtypeStruct((8, 2, half), F32),
                   jax.ShapeDtypeStruct((l, d), F32), jax.ShapeDtypeStruct((8, d), F32)],
        scratch=[pltpu.VMEM(w_in.shape, BF), pltpu.VMEM((tm, d), F32)],
        args=(dq3, dq3, dq3, p42, p42, p42, cw, w_in, x, ctx, dr_l, dr_c, a2), name="l0_conv_bwd_inproj")
    return dp42, dcw, gx, acc, extra


def _dw_inproj0(x, ctx, a2, b2, dp42, tm, xch=None):
    l, d = x.shape
    lc = ctx.shape[0]
    assert lc == tm
    tl = 4 * tm if l % (4 * tm) == 0 else tm
    nl = l // tl
    half = dp42.shape[-1]
    e = 2 * half

    def body(x_ref, c_ref, a_ref, b_ref, dpl_ref, dpc_ref, o_ref, acc_ref):
        i = pl.program_id(1)

        @pl.when(i == 0)
        def _():
            acc_ref[...] = jnp.zeros_like(acc_ref)

        def add(rows_ref, dp_ref, sel):
            h = (rows_ref[...] * a_ref[sel:sel + 1, :] + b_ref[sel:sel + 1, :]).astype(BF)
            acc_ref[:, :half] += _dot_tn(h, dp_ref[0, 0])
            acc_ref[:, half:] += _dot_tn(h, dp_ref[0, 1])

        @pl.when(i < nl)
        def _():
            add(x_ref, dpl_ref, 0)

        @pl.when(i == nl)
        def _():
            add(c_ref, dpc_ref, 1)
            o_ref[...] = acc_ref[...].astype(BF)

    (g_w,), extra = _hosted_call(
        body, xch, grid=(4, nl + 1),
        in_specs=[pl.BlockSpec((tl, d), lambda k, i: (jnp.minimum(i, nl - 1), 0)), _full((lc, d)),
                  _full((2, d)), _full((2, d)),
                  pl.BlockSpec((1, 2, tl, half), lambda k, i: (k, 0, jnp.minimum(i, nl - 1), 0)),
                  pl.BlockSpec((1, 2, lc, half), lambda k, i: (k, 0, l // lc, 0))],
        out_specs=[pl.BlockSpec((d, e), lambda k, i: (0, k))],
        out_shape=[jax.ShapeDtypeStruct((d, 4 * e), BF)],
        scratch=[pltpu.VMEM((d, e), F32)], args=(x, ctx, a2, b2, dp42, dp42), name="l0_dw_inproj")
    return g_w, extra


def _dw_outproj0(q3, dr_l, dr_c, gt2, tm, xch=None):
    l, d = dr_l.shape
    nl, nc = l // tm, dr_c.shape[0] // tm
    _, r, half = q3.shape
    nt = nl + nc

    def body(q_ref, dl_ref, dc_ref, g_ref, o_ref, acc_ref):
        i = pl.program_id(0)
        is_ctx = i >= nl

        @pl.when(i == 0)
        def _():
            acc_ref[...] = jnp.zeros_like(acc_ref)

        dr = jnp.where(is_ctx, dc_ref[...], dl_ref[...]).astype(F32)
        dfx = (dr * _sel_row(g_ref, is_ctx)).astype(BF)
        acc_ref[:half, :] += _dot_tn(q_ref[0], dfx)
        acc_ref[half:, :] += _dot_tn(q_ref[1], dfx)

        @pl.when(i == nt - 1)
        def _():
            o_ref[...] = acc_ref[...].astype(BF)

    lat, cx = _lat_or_ctx_specs(tm, d, nl, 1, 0)
    (g_w,), extra = _hosted_call(
        body, xch, grid=(nt,),
        in_specs=[pl.BlockSpec((2, tm, half), lambda i: (0, i, 0)), lat, cx, _full((2, d))],
        out_specs=[_full((2 * half, d))], out_shape=[jax.ShapeDtypeStruct((2 * half, d), BF)],
        scratch=[pltpu.VMEM((2 * half, d), F32)], args=(q3, dr_l, dr_c, gt2), name="l0_dw_outproj")
    return g_w, extra


def _cr_tile(j, cap=256):
    for cand in (1024, 512, 256, 128, 64, 32, 16, 8):
        if cand <= cap and j % cand == 0:
            return cand
    raise ValueError(j)


def _final(w_cr, w_out, xh_cr, tgt_cr, vecs):
    j, e16 = w_cr.shape
    e = e16 // CHUNK
    d = w_out.shape[1]
    tj = _cr_tile(j)

    def body(w_ref, wo_hbm, xh_ref, t_ref, v_ref, dr_ref, acc_ref, wo_ref):
        @pl.when(jnp.logical_and(pl.program_id(0) == 0, pl.program_id(1) == 0))
        def _():
            pltpu.sync_copy(wo_hbm, wo_ref)
            acc_ref[...] = jnp.zeros_like(acc_ref)

        o = _dot(w_ref[...], wo_ref[...])
        x1 = xh_ref[...] * v_ref[0:1, :] + v_ref[1:2, :]
        rr = DN_ALPHA * x1 + v_ref[2:3, :] * o
        mu = jnp.mean(rr, axis=-1, keepdims=True)
        cen = rr - mu
        rstd = lax.rsqrt(jnp.mean(cen * cen, axis=-1, keepdims=True) + LN_EPS)
        xh2 = cen * rstd
        err = xh2 * v_ref[3:4, :] + v_ref[4:5, :] - t_ref[...]
        dy = err * (1.0 / d)
        dxh = dy * v_ref[3:4, :]
        dr = rstd * (dxh - jnp.mean(dxh, axis=-1, keepdims=True) - xh2 * jnp.mean(dxh * xh2, axis=-1, keepdims=True))
        dr_ref[...] = dr.astype(BF)
        acc_ref[0:1, :] += jnp.sum(dy * xh2, axis=0, keepdims=True)
        acc_ref[1:2, :] += jnp.sum(dy, axis=0, keepdims=True)
        acc_ref[2:3, :] += jnp.sum(dr * o, axis=0, keepdims=True)
        acc_ref[3:4, :] += (0.5 / d) * jnp.sum(err * err, axis=0, keepdims=True)

    tok_d = pl.BlockSpec((tj, d), lambda t, s: (t, s))
    return pl.pallas_call(
        body, name="l1_final", grid=(j // tj, CHUNK),
        in_specs=[pl.BlockSpec((tj, e), lambda t, s: (t, s)), ANY, tok_d, tok_d, _full((8, d))],
        out_specs=[tok_d, _full((8, d))],
        out_shape=[jax.ShapeDtypeStruct((j, CHUNK * d), BF), jax.ShapeDtypeStruct((8, d), F32)],
        scratch_shapes=[pltpu.VMEM(w_out.shape, BF)], compiler_params=_cparams(),
    )(w_cr, w_out, xh_cr, tgt_cr, vecs)


def _dw_cr(lhs, rhs, lhs_kind, rhs_kind, vec, bias_sum, init, name):
    j = lhs.shape[0]
    k = lhs.shape[1] // CHUNK
    n = rhs.shape[1] // CHUNK
    tj = _cr_tile(j, 512)
    nh = 2 if k * n * 4 > (8 << 20) else 1
    tn = n // nh
    nt = j // tj
    has_init = init is not None

    def body(*refs):
        refs = list(refs)
        l_ref, r_ref = refs[0], refs[1]
        pos = 2
        v_ref = None
        if vec is not None:
            v_ref = refs[pos]
            pos += 1
        i_ref = None
        if has_init:
            i_ref = refs[pos]
            pos += 1
        o_ref = refs[pos]
        pos += 1
        bs_ref = None
        if bias_sum:
            bs_ref = refs[pos]
            pos += 1
        acc_ref = refs[pos]
        t, s = pl.program_id(1), pl.program_id(2)
        first = jnp.logical_and(t == 0, s == 0)

        @pl.when(first)
        def _():
            acc_ref[...] = i_ref[...] if has_init else jnp.zeros_like(acc_ref)
            if bias_sum:
                bs_ref[...] = jnp.zeros_like(bs_ref)

        if lhs_kind == "mod":
            lv = (l_ref[...] * v_ref[0:1, :] + v_ref[1:2, :]).astype(BF)
        else:
            lv = l_ref[...]
        if rhs_kind == "scaled":
            rv = (r_ref[...].astype(F32) * v_ref[0:1, :]).astype(BF)
        else:
            rv = r_ref[...]
        acc_ref[...] += _dot_tn(lv, rv)
        if bias_sum:
            bs_ref[0:1, :] += jnp.sum(rv.astype(F32), axis=0, keepdims=True)

        @pl.when(jnp.logical_and(t == nt - 1, s == CHUNK - 1))
        def _():
            o_ref[...] = acc_ref[...].astype(BF)

    l_spec = pl.BlockSpec((tj, k), lambda h, t, s: (t, s))
    r_spec = pl.BlockSpec((tj, tn), lambda h, t, s: (t, s * nh + h))
    in_specs, args = [l_spec, r_spec], [lhs, rhs]
    if vec is not None:
        in_specs.append(_full(vec.shape))
        args.append(vec)
    o_spec = pl.BlockSpec((k, tn), lambda h, t, s: (0, h))
    if has_init:
        in_specs.append(o_spec)
        args.append(init)
    out_specs, out_shape = [o_spec], [jax.ShapeDtypeStruct((k, n), BF)]
    if bias_sum:
        out_specs.append(pl.BlockSpec((8, tn), lambda h, t, s: (0, h)))
        out_shape.append(jax.ShapeDtypeStruct((8, n), F32))
    res = pl.pallas_call(
        body, name=name, grid=(nh, nt, CHUNK), in_specs=in_specs, out_specs=out_specs, out_shape=out_shape,
        scratch_shapes=[pltpu.VMEM((k, tn), F32)], compiler_params=_cparams(),
    )(*args)
    return res if bias_sum else res[0]


GT_ROWS = CHUNK * S5_P
ZG_W = 2 * 2 * S5_N
PAIR_W = 2 * ZG_W
GROUPS_PER_STEP = 4


def _inproj1_gt(xh_cr, a1, b1, wu_t, w_z, tag):
    j, d16 = xh_cr.shape
    d = d16 // CHUNK
    e = wu_t.shape[0]
    g = e // S5_P
    tj = _cr_tile(j, 256)

    def body(x_ref, a_ref, b_ref, wu_hbm, wz_hbm, u_ref, z_ref, wu_ref, wz_ref):
        @pl.when(jnp.logical_and(pl.program_id(0) == 0, pl.program_id(1) == 0))
        def _():
            pltpu.sync_copy(wu_hbm, wu_ref)
            pltpu.sync_copy(wz_hbm, wz_ref)

        h = (x_ref[...] * a_ref[...] + b_ref[...]).astype(BF)
        u_ref[...] = _dot_nt(wu_ref[...], h).reshape(g, S5_P, tj).astype(BF)
        z_ref[...] = _dot(h, wz_ref[...]).astype(BF)

    return pl.pallas_call(
        body, name="l1_inproj_" + tag, grid=(j // tj, CHUNK),
        in_specs=[pl.BlockSpec((tj, d), lambda t, s: (t, s)), _full((1, d)), _full((1, d)), ANY, ANY],
        out_specs=[pl.BlockSpec((g, S5_P, tj), lambda t, s: (0, s, t)), pl.BlockSpec((tj, e), lambda t, s: (t, s))],
        out_shape=[jax.ShapeDtypeStruct((g, GT_ROWS, j), BF), jax.ShapeDtypeStruct((j, CHUNK * e), BF)],
        scratch_shapes=[pltpu.VMEM(wu_t.shape, BF), pltpu.VMEM(w_z.shape, BF)], compiler_params=_cparams(),
    )(xh_cr, a1, b1, wu_t, w_z)


def _gt_spec(j, gb=GROUPS_PER_STEP):
    return pl.BlockSpec((gb, GT_ROWS, j), lambda i: (i, 0, 0))


def _zg_spec(j, gb=GROUPS_PER_STEP):
    return pl.BlockSpec((j, gb * ZG_W), lambda i: (0, i))


def _w_spec(width, gb=GROUPS_PER_STEP):
    return pl.BlockSpec((gb, GT_ROWS, width), lambda i: (i, 0, 0))


def _pair_lanes(k):
    return slice((k // 2) * PAIR_W, (k // 2 + 1) * PAIR_W)


def _s5_z(ut_l, ut_c, bc):
    g, _, jl = ut_l.shape
    jc = ut_c.shape[2]
    gb = GROUPS_PER_STEP

    def body(ul_ref, uc_ref, bc_ref, zl_ref, zc_ref):
        for k in range(0, gb, 2):
            zl_ref[:, _pair_lanes(k)] = _dot_tn(ul_ref[k], bc_ref[k]) + _dot_tn(ul_ref[k + 1], bc_ref[k + 1])
            zc_ref[:, _pair_lanes(k)] = _dot_tn(uc_ref[k], bc_ref[k]) + _dot_tn(uc_ref[k + 1], bc_ref[k + 1])

    return pl.pallas_call(
        body, name="l1_s5_z", grid=(g // gb,), in_specs=[_gt_spec(jl), _gt_spec(jc), _w_spec(PAIR_W)],
        out_specs=[_zg_spec(jl), _zg_spec(jc)],
        out_shape=[jax.ShapeDtypeStruct((jl, g * ZG_W), F32), jax.ShapeDtypeStruct((jc, g * ZG_W), F32)],
        compiler_params=_cparams(),
    )(ut_l, ut_c, bc)


def _s5_y(ut_l, s_l, mt_t, cct):
    g, _, jl = ut_l.shape
    gb = GROUPS_PER_STEP

    def body(u_ref, s_ref, mt_ref, cc_ref, y_ref):
        for k in range(gb):
            s_k = s_ref[:, _pair_lanes(k)].astype(BF)
            y_ref[k] = (_dot(mt_ref[k], u_ref[k]) + _dot_nt(cc_ref[k], s_k)).astype(BF)

    return pl.pallas_call(
        body, name="l1_s5_y", grid=(g // gb,),
        in_specs=[_gt_spec(jl), _zg_spec(jl), _w_spec(GT_ROWS), _w_spec(PAIR_W)],
        out_specs=_gt_spec(jl), out_shape=jax.ShapeDtypeStruct((g, GT_ROWS, jl), BF), compiler_params=_cparams(),
    )(ut_l, s_l, mt_t, cct)


def _s5_ds(dyt_l, cct):
    g, _, jl = dyt_l.shape
    gb = GROUPS_PER_STEP

    def body(dy_ref, cc_ref, ds_ref):
        for k in range(0, gb, 2):
            ds_ref[:, _pair_lanes(k)] = _dot_tn(dy_ref[k], cc_ref[k]) + _dot_tn(dy_ref[k + 1], cc_ref[k + 1])

    return pl.pallas_call(
        body, name="l1_s5_ds", grid=(g // gb,), in_specs=[_gt_spec(jl), _w_spec(PAIR_W)], out_specs=_zg_spec(jl),
        out_shape=jax.ShapeDtypeStruct((jl, g * ZG_W), F32), compiler_params=_cparams(),
    )(dyt_l, cct)


def _s5_dx(dyt_l, dz_l, dz_c, mt, bc):
    g, _, jl = dyt_l.shape
    jc = dz_c.shape[0]
    gb = GROUPS_PER_STEP

    def body(dy_ref, dzl_ref, dzc_ref, mt_ref, bc_ref, dul_ref, duc_ref):
        for k in range(gb):
            dzl = dzl_ref[:, _pair_lanes(k)].astype(BF)
            dzc = dzc_ref[:, _pair_lanes(k)].astype(BF)
            dul_ref[k] = (_dot(mt_ref[k], dy_ref[k]) + _dot_nt(bc_ref[k], dzl)).astype(BF)
            duc_ref[k] = _dot_nt(bc_ref[k], dzc).astype(BF)

    return pl.pallas_call(
        body, name="l1_s5_dx", grid=(g // gb,),
        in_specs=[_gt_spec(jl), _zg_spec(jl), _zg_spec(jc), _w_spec(GT_ROWS), _w_spec(PAIR_W)],
        out_specs=[_gt_spec(jl), _gt_spec(jc)],
        out_shape=[jax.ShapeDtypeStruct((g, GT_ROWS, jl), BF), jax.ShapeDtypeStruct((g, GT_ROWS, jc), BF)],
        compiler_params=_cparams(),
    )(dyt_l, dz_l, dz_c, mt, bc)


def _s5_dw(ut_l, ut_c, dyt_l, dz_l, dz_c, s_l):
    g, _, jl = ut_l.shape
    jc = ut_c.shape[2]
    gb = GROUPS_PER_STEP

    def body(ul_ref, uc_ref, dy_ref, dzl_ref, dzc_ref, s_ref, dmt_ref, dbc_ref, dcc_ref):
        for k in range(gb):
            lanes = _pair_lanes(k)
            dmt_ref[k] = _dot_nt(ul_ref[k], dy_ref[k])
            dbc_ref[k] = (_dot(ul_ref[k], dzl_ref[:, lanes].astype(BF))
                          + _dot(uc_ref[k], dzc_ref[:, lanes].astype(BF)))
            dcc_ref[k] = _dot(dy_ref[k], s_ref[:, lanes].astype(BF))

    sd_m = jax.ShapeDtypeStruct((g, GT_ROWS, GT_ROWS), F32)
    sd_p = jax.ShapeDtypeStruct((g, GT_ROWS, PAIR_W), F32)
    return pl.pallas_call(
        body, name="l1_s5_dw", grid=(g // gb,),
        in_specs=[_gt_spec(jl), _gt_spec(jc), _gt_spec(jl), _zg_spec(jl), _zg_spec(jc), _zg_spec(jl)],
        out_specs=[_w_spec(GT_ROWS), _w_spec(PAIR_W), _w_spec(PAIR_W)], out_shape=[sd_m, sd_p, sd_p],
        compiler_params=_cparams(),
    )(ut_l, ut_c, dyt_l, dz_l, dz_c, s_l)


def _scan_g(z_l, z_c, coef, chains, conj, s_l=None, s_c=None, name="l1_scan"):
    jl, w_all = z_l.shape
    jc = z_c.shape[0]
    gb = 2 * GROUPS_PER_STEP if w_all % (2 * GROUPS_PER_STEP * ZG_W) == 0 else GROUPS_PER_STEP
    wb = gb * ZG_W
    nch = wb // 256
    with_da = s_l is not None
    sign = -1.0 if conj else 1.0

    def body(*refs):
        zl_ref, zc_ref, cf_ref = refs[:3]
        k0 = 3
        if with_da:
            sl_ref, sc_ref = refs[3:5]
            k0 = 5
        ol_ref, oc_ref = refs[k0:k0 + 2]
        rowi = lax.broadcasted_iota(jnp.int32, (8, 128), 0)

        def lanes_of(ch):
            return slice(ch * 256, ch * 256 + 128), slice(ch * 256 + 128, (ch + 1) * 256)

        def coefs(ch, r0, nr):
            lr, li = lanes_of(ch)
            return cf_ref[r0:r0 + nr, lr], sign * cf_ref[r0:r0 + nr, li]

        def shift(v, sh, rev):
            if rev:
                return jnp.where(rowi < 8 - sh, pltpu.roll(v, 8 - sh, 0), 0.0)
            return jnp.where(rowi >= sh, pltpu.roll(v, sh, 0), 0.0)

        zero_row = jnp.zeros((1, 128), F32)
        zero_tile = jnp.zeros((8, 128), F32)
        carry = [zero_row] * (2 * nch)
        da = [zero_tile] * (2 * nch)
        for seg in range(len(chains[0])):
            which = chains[0][seg][0]
            assert chains[1][seg][0] == which
            revs = (chains[0][seg][1], chains[1][seg][1])
            src, dst = (zc_ref, oc_ref) if which == "c" else (zl_ref, ol_ref)
            sref = ((sc_ref if which == "c" else sl_ref) if with_da else None)
            ng = (jc if which == "c" else jl) // 8

            def step(it, st, src=src, dst=dst, sref=sref, ng=ng, revs=revs):
                carry_, da_ = list(st[:2 * nch]), list(st[2 * nch:])
                for ch in range(nch):
                    rev = revs[ch % 2]
                    lr, li = lanes_of(ch)
                    grp = (ng - 1 - it) if rev else it
                    off = pl.multiple_of(grp * 8, 8)
                    xr, xi = src[pl.ds(off, 8), lr], src[pl.ds(off, 8), li]
                    for sh, r0 in ((1, 0), (2, 1), (4, 2)):
                        ar, ai = coefs(ch, r0, 1)
                        sr, si = shift(xr, sh, rev), shift(xi, sh, rev)
                        xr, xi = xr + ar * sr - ai * si, xi + ar * si + ai * sr
                    tr, ti = coefs(ch, 16, 8) if rev else coefs(ch, 8, 8)
                    cr_, ci_ = carry_[2 * ch], carry_[2 * ch + 1]
                    ir = xr + tr * cr_ - ti * ci_
                    ii = xi + tr * ci_ + ti * cr_
                    if rev:
                        er = jnp.where(rowi == 7, cr_, pltpu.roll(ir, 7, 0))
                        ei = jnp.where(rowi == 7, ci_, pltpu.roll(ii, 7, 0))
                        carry_[2 * ch], carry_[2 * ch + 1] = ir[0:1], ii[0:1]
                    else:
                        er = jnp.where(rowi == 0, cr_, pltpu.roll(ir, 1, 0))
                        ei = jnp.where(rowi == 0, ci_, pltpu.roll(ii, 1, 0))
                        carry_[2 * ch], carry_[2 * ch + 1] = ir[7:8], ii[7:8]
                    dst[pl.ds(off, 8), lr] = er
                    dst[pl.ds(off, 8), li] = ei
                    if sref is not None:
                        s_r, s_i = sref[pl.ds(off, 8), lr], sref[pl.ds(off, 8), li]
                        da_[2 * ch] = da_[2 * ch] + s_r * er + s_i * ei
                        da_[2 * ch + 1] = da_[2 * ch + 1] + s_r * ei - s_i * er
                return (*carry_, *da_)

            st = lax.fori_loop(0, ng, step, (*carry, *da))
            carry, da = list(st[:2 * nch]), list(st[2 * nch:])
        if with_da:
            da_ref = refs[k0 + 2]
            for ch in range(nch):
                lr, li = lanes_of(ch)
                da_ref[:, lr] = da[2 * ch]
                da_ref[:, li] = da[2 * ch + 1]

    in_specs = [_zg_spec(jl, gb), _zg_spec(jc, gb), pl.BlockSpec((24, wb), lambda i: (0, i))]
    args = [z_l, z_c, coef]
    out_specs = [_zg_spec(jl, gb), _zg_spec(jc, gb)]
    out_shape = [jax.ShapeDtypeStruct(z_l.shape, F32), jax.ShapeDtypeStruct(z_c.shape, F32)]
    if with_da:
        in_specs += [_zg_spec(jl, gb), _zg_spec(jc, gb)]
        args += [s_l, s_c]
        out_specs.append(pl.BlockSpec((8, wb), lambda i: (0, i)))
        out_shape.append(jax.ShapeDtypeStruct((8, w_all), F32))
    return pl.pallas_call(body, name=name, grid=(w_all // wb,), in_specs=in_specs, out_specs=out_specs,
                          out_shape=out_shape, compiler_params=_cparams())(*args)


def _gt_tok_spec(g, tj):
    return pl.BlockSpec((g, S5_P, tj), lambda t, s: (0, s, t))


def _glu_fwd_gt(yt, z_cr, w_glu, b_glu):
    g, _, j = yt.shape
    e = g * S5_P
    tj = _cr_tile(j)

    def body(y_ref, z_ref, w_hbm, b_ref, o_ref, sg_ref, w_ref):
        @pl.when(jnp.logical_and(pl.program_id(0) == 0, pl.program_id(1) == 0))
        def _():
            pltpu.sync_copy(w_hbm, w_ref)

        y = jnp.transpose(y_ref[...].reshape(e, tj).astype(F32))
        gl = _gelu_parts(y)[0]
        sg = _sigmoid(_dot(gl.astype(BF), w_ref[...]) + b_ref[...])
        z = z_ref[...].astype(F32)
        o_ref[...] = (gl * sg * (z * _sigmoid(z))).astype(BF)
        sg_ref[...] = sg.astype(BF)

    tok = pl.BlockSpec((tj, e), lambda t, s: (t, s))
    return pl.pallas_call(
        body, name="l1_glu_fwd", grid=(j // tj, CHUNK),
        in_specs=[_gt_tok_spec(g, tj), tok, ANY, _full((1, e))], out_specs=[tok, tok],
        out_shape=[jax.ShapeDtypeStruct((j, CHUNK * e), BF), jax.ShapeDtypeStruct((j, CHUNK * e), BF)],
        scratch_shapes=[pltpu.VMEM(w_glu.shape, BF)], compiler_params=_cparams(),
    )(yt, z_cr, w_glu, b_glu)


def _glu_bwd_gt(dr_cr, gt1, w_out, w_glu, yt, z_cr, sg_cr):
    g, _, j = yt.shape
    e, d = w_out.shape
    tj = _cr_tile(j)

    def body(dr_ref, g_ref, wo_hbm, wg_hbm, y_ref, z_ref, sg_ref, dz_ref, dt_ref, dy_ref, wo_ref, wg_ref):
        @pl.when(jnp.logical_and(pl.program_id(0) == 0, pl.program_id(1) == 0))
        def _():
            pltpu.sync_copy(wo_hbm, wo_ref)
            pltpu.sync_copy(wg_hbm, wg_ref)

        do = (dr_ref[...].astype(F32) * g_ref[...]).astype(BF)
        dw = _dot_nt(do, wo_ref[...])
        y = jnp.transpose(y_ref[...].reshape(e, tj).astype(F32))
        gl, dgel = _gelu_parts(y)
        z = z_ref[...].astype(F32)
        sz = _sigmoid(z)
        sg = sg_ref[...].astype(F32)
        dg2 = dw * (z * sz)
        dz_ref[...] = (dw * gl * sg * (sz * (1.0 + z * (1.0 - sz)))).astype(BF)
        dt = (dg2 * gl * sg * (1.0 - sg)).astype(BF)
        dt_ref[...] = dt
        dy = (dg2 * sg + _dot_nt(dt, wg_ref[...])) * dgel
        dy_ref[...] = jnp.transpose(dy).reshape(g, S5_P, tj).astype(BF)

    tok_e = pl.BlockSpec((tj, e), lambda t, s: (t, s))
    return pl.pallas_call(
        body, name="l1_glu_bwd", grid=(j // tj, CHUNK),
        in_specs=[pl.BlockSpec((tj, d), lambda t, s: (t, s)), _full((1, d)), ANY, ANY, _gt_tok_spec(g, tj), tok_e, tok_e],
        out_specs=[tok_e, tok_e, _gt_tok_spec(g, tj)],
        out_shape=[jax.ShapeDtypeStruct((j, CHUNK * e), BF), jax.ShapeDtypeStruct((j, CHUNK * e), BF),
                   jax.ShapeDtypeStruct((g, GT_ROWS, j), BF)],
        scratch_shapes=[pltpu.VMEM(w_out.shape, BF), pltpu.VMEM(w_glu.shape, BF)], compiler_params=_cparams(),
    )(dr_cr, gt1, w_out, w_glu, yt, z_cr, sg_cr)


def _bwd_inproj1_gt(dut, dz_cr, wu_t, w_z, xh_cr, rs_cr, dr2_cr, vecs, tag):
    g, _, j = dut.shape
    e, d = wu_t.shape
    tj = _cr_tile(j)

    def body(du_ref, dz_ref, wu_hbm, wz_hbm, xh_ref, rs_ref, dr2_ref, v_ref, dr1_ref, acc_ref, wu_ref, wz_ref):
        @pl.when(jnp.logical_and(pl.program_id(0) == 0, pl.program_id(1) == 0))
        def _():
            pltpu.sync_copy(wu_hbm, wu_ref)
            pltpu.sync_copy(wz_hbm, wz_ref)
            acc_ref[...] = jnp.zeros_like(acc_ref)

        dh = _dot_tn(du_ref[...].reshape(e, tj), wu_ref[...]) + _dot_nt(dz_ref[...], wz_ref[...])
        xh = xh_ref[...]
        x1 = xh * v_ref[0:1, :] + v_ref[1:2, :]
        dx1 = DN_ALPHA * dr2_ref[...].astype(F32) + dh * v_ref[2:3, :]
        dxh = dx1 * v_ref[0:1, :]
        rstd = rs_ref[:, 0:1]
        dr1 = rstd * (dxh - jnp.mean(dxh, axis=-1, keepdims=True) - xh * jnp.mean(dxh * xh, axis=-1, keepdims=True))
        dr1_ref[...] = dr1.astype(BF)
        acc_ref[0:1, :] += jnp.sum(dh * x1, axis=0, keepdims=True)
        acc_ref[1:2, :] += jnp.sum(dh, axis=0, keepdims=True)
        acc_ref[2:3, :] += jnp.sum(dx1 * xh, axis=0, keepdims=True)
        acc_ref[3:4, :] += jnp.sum(dx1, axis=0, keepdims=True)

    tok_d = pl.BlockSpec((tj, d), lambda t, s: (t, s))
    return pl.pallas_call(
        body, name="l1_bwd_inproj_" + tag, grid=(j // tj, CHUNK),
        in_specs=[_gt_tok_spec(g, tj), pl.BlockSpec((tj, e), lambda t, s: (t, s)), ANY, ANY, tok_d,
                  pl.BlockSpec((tj, 128), lambda t, s: (t, s)), tok_d, _full((8, d))],
        out_specs=[tok_d, _full((8, d))],
        out_shape=[jax.ShapeDtypeStruct((j, CHUNK * d), BF), jax.ShapeDtypeStruct((8, d), F32)],
        scratch_shapes=[pltpu.VMEM(wu_t.shape, BF), pltpu.VMEM(w_z.shape, BF)], compiler_params=_cparams(),
    )(dut, dz_cr, wu_t, w_z, xh_cr, rs_cr, dr2_cr, vecs)


def _dw_gt(lhs_gt, rhs_cr, lhs_gelu, vec, bias_sum, init, out_dtype, name, xch=None):
    g, _, j = lhs_gt.shape
    e = g * S5_P
    n = rhs_cr.shape[1] // CHUNK
    tj = _cr_tile(j, 512 if j % 512 == 0 else 256)
    nh = 2 if e * n * 4 > (8 << 20) else 1
    tn = n // nh
    nt = j // tj
    has_init = init is not None

    def body(*refs):
        refs = list(refs)
        l_ref, r_ref = refs[0], refs[1]
        pos = 2
        v_ref = i_ref = bs_ref = None
        if vec is not None:
            v_ref = refs[pos]
            pos += 1
        if has_init:
            i_ref = refs[pos]
            pos += 1
        o_ref = refs[pos]
        pos += 1
        if bias_sum:
            bs_ref = refs[pos]
            pos += 1
        acc_ref = refs[pos]
        t, s = pl.program_id(1), pl.program_id(2)

        @pl.when(jnp.logical_and(t == 0, s == 0))
        def _():
            acc_ref[...] = i_ref[...] if has_init else jnp.zeros_like(acc_ref)
            if bias_sum:
                bs_ref[...] = jnp.zeros_like(bs_ref)

        lv = l_ref[...].reshape(e, tj)
        if lhs_gelu:
            lv = _gelu_parts(lv.astype(F32))[0].astype(BF)
        if vec is not None:
            rv = (r_ref[...] * v_ref[0:1, :] + v_ref[1:2, :]).astype(BF)
        else:
            rv = r_ref[...]
        acc_ref[...] += _dot(lv, rv)
        if bias_sum:
            bs_ref[0:1, :] += jnp.sum(rv.astype(F32), axis=0, keepdims=True)

        @pl.when(jnp.logical_and(t == nt - 1, s == CHUNK - 1))
        def _():
            o_ref[...] = acc_ref[...].astype(out_dtype)

    in_specs = [pl.BlockSpec((g, S5_P, tj), lambda h, t, s: (0, s, t)),
                pl.BlockSpec((tj, tn), lambda h, t, s: (t, s * nh + h))]
    args = [lhs_gt, rhs_cr]
    if vec is not None:
        in_specs.append(_full(vec.shape))
        args.append(vec)
    o_spec = pl.BlockSpec((e, tn), lambda h, t, s: (0, h))
    if has_init:
        in_specs.append(o_spec)
        args.append(init)
    out_specs, out_shape = [o_spec], [jax.ShapeDtypeStruct((e, n), out_dtype)]
    if bias_sum:
        out_specs.append(pl.BlockSpec((8, tn), lambda h, t, s: (0, h)))
        out_shape.append(jax.ShapeDtypeStruct((8, n), F32))
    res, extra = _hosted_call(body, xch, grid=(nh, nt, CHUNK), in_specs=in_specs, out_specs=out_specs,
                              out_shape=out_shape, scratch=[pltpu.VMEM((e, tn), F32)], args=args, name=name)
    if xch is not None:
        return (*res, extra) if bias_sum else (res[0], extra)
    return res if bias_sum else res[0]


def _scan_coef_g(lam_re, lam_im, log_step):
    g = lam_re.shape[1]
    ms = jnp.array([1, 2, 4, 0, 0, 0, 0, 0] + list(range(1, 9)) + list(range(8, 0, -1)), F32) * CHUNK
    dt = jnp.exp(log_step)[..., None]
    mag = jnp.exp(ms.reshape(-1, 1, 1, 1) * (lam_re * dt)[None])
    ang = ms.reshape(-1, 1, 1, 1) * (lam_im * dt)[None]
    cr, ci = mag * jnp.cos(ang), mag * jnp.sin(ang)
    both = jnp.stack([cr, ci], axis=2).reshape(24, 2, 2, g // 2, 2, S5_N)
    return both.transpose(0, 3, 1, 2, 4, 5).reshape(24, g * ZG_W)


def _s5_small(lam_re, lam_im, log_step, b_re, b_im, c_re, c_im, d_skip):
    g = lam_re.shape[1]
    t, p = CHUNK, S5_P
    dt = jnp.exp(log_step)[..., None]
    ks = jnp.arange(t + 1, dtype=F32).reshape(t + 1, 1, 1, 1)
    mag = jnp.exp(ks * (lam_re * dt)[None])
    ang = ks * (lam_im * dt)[None]
    pr, pi = mag * jnp.cos(ang), mag * jnp.sin(ang)
    ar, ai = pr[1], pi[1]
    qr, qi = ar - 1.0, ai
    den = lam_re * lam_re + lam_im * lam_im
    fr = (qr * lam_re + qi * lam_im) / den
    fi = (qi * lam_re - qr * lam_im) / den
    bt_re, bt_im = b_re.transpose(0, 1, 3, 2), b_im.transpose(0, 1, 3, 2)
    bbr = fr[:, :, None, :] * bt_re - fi[:, :, None, :] * bt_im
    bbi = fr[:, :, None, :] * bt_im + fi[:, :, None, :] * bt_re
    lay = lambda a_r, a_i: jnp.stack([a_r, a_i], axis=0).transpose(3, 2, 0, 1, 4)
    by_dir = lambda a, f0, f1: jnp.stack([f0(a[:, 0]), f1(a[:, 1])], axis=1)
    rev = lambda a: jnp.flip(a, axis=0)
    same = lambda a: a
    pwb = lay(by_dir(pr[:t], rev, same), by_dir(pi[:t], rev, same))
    pwc = lay(by_dir(pr[1:], same, rev), by_dir(pi[1:], same, rev))
    bb = jnp.stack([bbr, bbi], axis=0).transpose(2, 1, 0, 3, 4)
    cc = jnp.stack([c_re, c_im], axis=0).transpose(2, 1, 0, 3, 4)
    dmat = jnp.eye(p, dtype=F32)[None] * d_skip.reshape(g, p)[:, :, None]
    return pwb, pwc, bb, cc, dmat, pr[t], pi[t]


def _pair_cols(r, ri, g2):
    c0 = (r * 2 + ri) * 128 + g2 * S5_N
    return slice(c0, c0 + S5_N)


def _rows_rep(a):
    return jnp.broadcast_to(a[:, None, :], (CHUNK, S5_P, a.shape[-1])).reshape(GT_ROWS, a.shape[-1])


def _rows_tile(a):
    return jnp.broadcast_to(a[None], (CHUNK, S5_P, a.shape[-1])).reshape(GT_ROWS, a.shape[-1])


def _sum_blocks(a):
    return jnp.sum(a.reshape(CHUNK, S5_P, a.shape[-1]), axis=0)


def _sum_in_blocks(a):
    return jnp.sum(a.reshape(CHUNK, S5_P, a.shape[-1]), axis=1)


def _ab_rows(pwb_ref, bb_ref, k, r):
    prs, pis = _rows_rep(pwb_ref[k, r, 0]), _rows_rep(pwb_ref[k, r, 1])
    bbr, bbi = _rows_tile(bb_ref[k, r, 0]), _rows_tile(bb_ref[k, r, 1])
    return prs * bbr - pis * bbi, prs * bbi + pis * bbr, prs, pis, bbr, bbi


def _s5_weights_fwd(pwb, pwc, bb, cc, dmat):
    g = pwb.shape[0]
    gb = GROUPS_PER_STEP
    hp = lax.Precision.HIGHEST

    def body(pwb_ref, pwc_ref, bb_ref, cc_ref, dm_ref, mt_ref, mtt_ref, bc_ref, cct_ref):
        zeros = jnp.zeros((GT_ROWS, S5_N), BF)
        nt = (((1,), (1,)), ((), ()))
        for k in range(gb):
            g2 = k % 2
            kds = []
            for r in range(2):
                for ri in range(2):
                    bc_ref[k, :, _pair_cols(r, ri, 1 - g2)] = zeros
                    cct_ref[k, :, _pair_cols(r, ri, 1 - g2)] = zeros
                abr, abi = _ab_rows(pwb_ref, bb_ref, k, r)[:2]
                bc_ref[k, :, _pair_cols(r, 0, g2)] = abr.astype(BF)
                bc_ref[k, :, _pair_cols(r, 1, g2)] = abi.astype(BF)
                cr, ci = cc_ref[k, r, 0], cc_ref[k, r, 1]
                crt, cit = _rows_tile(cr), _rows_tile(ci)
                prt, pit = _rows_rep(pwc_ref[k, r, 0]), _rows_rep(pwc_ref[k, r, 1])
                cct_ref[k, :, _pair_cols(r, 0, g2)] = (crt * prt - cit * pit).astype(BF)
                cct_ref[k, :, _pair_cols(r, 1, g2)] = (-(crt * pit + cit * prt)).astype(BF)
                kds.append(lax.dot_general(abr, cr, nt, precision=hp, preferred_element_type=F32)
                           - lax.dot_general(abi, ci, nt, precision=hp, preferred_element_type=F32))
            blk = lambda a, s: a[s * S5_P:(s + 1) * S5_P]
            last = CHUNK - 1
            pieces = [blk(kds[1], last - i) for i in range(last)]
            pieces.append(blk(kds[0], last) + blk(kds[1], 0) + dm_ref[k])
            pieces += [blk(kds[0], last - d) for d in range(1, CHUNK)]
            qrow = jnp.concatenate(pieces, axis=1)
            mt = jnp.concatenate([qrow[:, (last - s) * S5_P:(last - s) * S5_P + GT_ROWS] for s in range(CHUNK)], axis=0)
            mt_ref[k] = mt.astype(BF)
            mtt_ref[k] = jnp.transpose(mt).astype(BF)

    small = lambda a: pl.BlockSpec((gb, *a.shape[1:]), lambda i: (i,) + (0,) * (a.ndim - 1))
    return pl.pallas_call(
        body, name="l1_s5_weights", grid=(g // gb,),
        in_specs=[small(pwb), small(pwc), small(bb), small(cc), small(dmat)],
        out_specs=[_w_spec(GT_ROWS), _w_spec(GT_ROWS), _w_spec(PAIR_W), _w_spec(PAIR_W)],
        out_shape=[jax.ShapeDtypeStruct((g, GT_ROWS, GT_ROWS), BF), jax.ShapeDtypeStruct((g, GT_ROWS, GT_ROWS), BF),
                   jax.ShapeDtypeStruct((g, GT_ROWS, PAIR_W), BF), jax.ShapeDtypeStruct((g, GT_ROWS, PAIR_W), BF)],
        compiler_params=_cparams(),
    )(pwb, pwc, bb, cc, dmat)


def _s5_weights_bwd(pwb, pwc, bb, cc, d_mt, d_bc, d_cct):
    g = pwb.shape[0]
    gb = GROUPS_PER_STEP
    hp = lax.Precision.HIGHEST

    def body(pwb_ref, pwc_ref, bb_ref, cc_ref, dmt_ref, dbc_ref, dcc_ref, dpwb_ref, dpwc_ref, dbb_ref, dccp_ref, ddm_ref):
        tn = (((0,), (0,)), ((), ()))
        nn = (((1,), (0,)), ((), ()))
        last = CHUNK - 1
        for k in range(gb):
            g2 = k % 2
            dq = None
            for s in range(CHUNK):
                parts = [dmt_ref[k, s * S5_P:(s + 1) * S5_P, :]]
                if s < last:
                    parts.insert(0, jnp.zeros((S5_P, (last - s) * S5_P), F32))
                if s > 0:
                    parts.append(jnp.zeros((S5_P, s * S5_P), F32))
                padded = jnp.concatenate(parts, axis=1) if len(parts) > 1 else parts[0]
                dq = padded if dq is None else dq + padded
            dblk = lambda d: dq[:, (last + d) * S5_P:(CHUNK + d) * S5_P]
            ddm_ref[k] = dblk(0)
            dkds = [jnp.concatenate([dblk(last - s) for s in range(CHUNK)], axis=0),
                    jnp.concatenate([dblk(-s) for s in range(CHUNK)], axis=0)]
            for r in range(2):
                abr, abi, prs, pis, bbr, bbi = _ab_rows(pwb_ref, bb_ref, k, r)
                cr, ci = cc_ref[k, r, 0], cc_ref[k, r, 1]
                dcr = lax.dot_general(dkds[r], abr, tn, precision=hp, preferred_element_type=F32)
                dci = -lax.dot_general(dkds[r], abi, tn, precision=hp, preferred_element_type=F32)
                dabr = (lax.dot_general(dkds[r], cr, nn, precision=hp, preferred_element_type=F32)
                        + dbc_ref[k, :, _pair_cols(r, 0, g2)])
                dabi = (-lax.dot_general(dkds[r], ci, nn, precision=hp, preferred_element_type=F32)
                        + dbc_ref[k, :, _pair_cols(r, 1, g2)])
                dbb_ref[k, r, 0] = _sum_blocks(prs * dabr + pis * dabi)
                dbb_ref[k, r, 1] = _sum_blocks(prs * dabi - pis * dabr)
                dpwb_ref[k, r, 0] = _sum_in_blocks(dabr * bbr + dabi * bbi)
                dpwb_ref[k, r, 1] = _sum_in_blocks(dabi * bbr - dabr * bbi)
                crt, cit = _rows_tile(cr), _rows_tile(ci)
                prt, pit = _rows_rep(pwc_ref[k, r, 0]), _rows_rep(pwc_ref[k, r, 1])
                d_re = dcc_ref[k, :, _pair_cols(r, 0, g2)]
                d_im = dcc_ref[k, :, _pair_cols(r, 1, g2)]
                dccp_ref[k, r, 0] = dcr + _sum_blocks(d_re * prt - d_im * pit)
                dccp_ref[k, r, 1] = dci - _sum_blocks(d_re * pit + d_im * prt)
                dpwc_ref[k, r, 0] = _sum_in_blocks(d_re * crt - d_im * cit)
                dpwc_ref[k, r, 1] = -_sum_in_blocks(d_re * cit + d_im * crt)

    small = lambda a: pl.BlockSpec((gb, *a.shape[1:]), lambda i: (i,) + (0,) * (a.ndim - 1))
    dmat_sds = jax.ShapeDtypeStruct((g, S5_P, S5_P), F32)
    return pl.pallas_call(
        body, name="l1_s5_weights_bwd", grid=(g // gb,),
        in_specs=[small(pwb), small(pwc), small(bb), small(cc), _w_spec(GT_ROWS), _w_spec(PAIR_W), _w_spec(PAIR_W)],
        out_specs=[small(pwb), small(pwc), small(bb), small(cc), small(dmat_sds)],
        out_shape=[jax.ShapeDtypeStruct(pwb.shape, F32), jax.ShapeDtypeStruct(pwc.shape, F32),
                   jax.ShapeDtypeStruct(bb.shape, F32), jax.ShapeDtypeStruct(cc.shape, F32), dmat_sds],
        compiler_params=_cparams(),
    )(pwb, pwc, bb, cc, d_mt, d_bc, d_cct)


def _from_cr(a, c):
    return a.reshape(a.shape[0] * CHUNK, c)


def _pad8(v):
    return jnp.concatenate([v, jnp.zeros((8 - v.shape[0], v.shape[1]), v.dtype)], axis=0)


def _local_step(x, c, ctx, c_ctx, loss_target, w, late=None, scatter=False, mod=None):
    l, d = x.shape
    lc = ctx.shape[0]
    tm = min(256, lc)
    assert lc == tm and l % tm == 0 and tm % GRID_W == 0 and (tm & (tm - 1)) == 0
    nl = l // tm

    own_mod = mod is None
    if own_mod:
        c8 = _pad8(jnp.stack([c, c_ctx]))
        mod = _ada_fwd(c8, w["ada_w"], w["ada_b"])
    sh = mod[:, :2, :d]
    sc = mod[:, :2, d:2 * d]
    gt = mod[:, :2, 2 * d:]
    ln_g, ln_b = w["ln_g"], w["ln_b"]

    a0, b0 = 1.0 + sc[0], sh[0]
    xch = _Exchange("gather2", [late[n][0] for n in late], [late[n][1] for n in late]) if late else None
    p42, tgt_cr, got = _inproj0(x, ctx, a0, b0, w["conv_w_in"], loss_target, tm, xch)
    if late:
        w = dict(w, **dict(zip(late, got)))
    e = w["conv_w_out"].shape[0]
    half = e // 2
    cw = w["conv_w"].reshape(3, 2, half)
    q3 = _conv_fwd(p42, cw, nl, tm, half)
    xh1_l, xh1_c, rs1_l, rs1_c, fx = _outproj_ln0(q3, w["conv_w_out"], x, ctx, gt[0], tm)
    jl, jc = l // CHUNK, lc // CHUNK

    g0, bb0 = ln_g[0:1], ln_b[0:1]
    a1 = g0 * (1.0 + sc[1])
    b1 = bb0 * (1.0 + sc[1]) + sh[1]
    wu_t = w["ssm_w_in"][:, :e].T
    w_z = w["ssm_w_in"][:, e:]
    ut_l, z_l = _inproj1_gt(xh1_l, a1[0:1], b1[0:1], wu_t, w_z, "lat")
    ut_c, _ = _inproj1_gt(xh1_c, a1[1:2], b1[1:2], wu_t, w_z, "ctx")
    s5 = (w["ssm_lam_re"], w["ssm_lam_im"], w["ssm_log_step"], w["ssm_b_re"], w["ssm_b_im"],
          w["ssm_c_re"], w["ssm_c_im"], w["ssm_d"])
    (pwb, pwc, bbw, ccw, dmat, _, _), s5_vjp = jax.vjp(_s5_small, *s5)
    mt_b, mtt_b, bc_b, cct_b = _s5_weights_fwd(pwb, pwc, bbw, ccw, dmat)
    coef = lax.stop_gradient(_scan_coef_g(*s5[:3]))
    zz_l, zz_c = _s5_z(ut_l, ut_c, bc_b)
    fwd_chains = ((("c", False), ("l", False)), (("c", True), ("l", True)))
    st_l, st_c = _scan_g(zz_l, zz_c, coef, fwd_chains, False, name="l1_scan_fwd")
    yt = _s5_y(ut_l, st_l, mtt_b, cct_b)
    b_glu = w["ssm_b_glu"].reshape(1, e)
    w_cr, sg_cr = _glu_fwd_gt(yt, z_l, w["ssm_w_glu"], b_glu)
    vec_f = _pad8(jnp.concatenate([g0, bb0, gt[1][0:1], ln_g[1:2], ln_b[1:2]], axis=0))
    dr2, acc_f = _final(w_cr, w["ssm_w_out"], xh1_l, tgt_cr, vec_f)
    loss = jnp.sum(acc_f[3])

    gt1 = gt[1][0:1]
    dz_l, dt_l, dyt = _glu_bwd_gt(dr2, gt1, w["ssm_w_out"], w["ssm_w_glu"], yt, z_l, sg_cr)
    g_w_out = _dw_cr(w_cr, dr2, "cr", "scaled", gt1, False, None, "l1_dw_out")
    ds_l = _s5_ds(dyt, cct_b)
    bwd_chains = ((("l", True), ("c", True)), (("l", False), ("c", False)))
    dzz_l, dzz_c, da = _scan_g(ds_l, jnp.zeros_like(zz_c), coef, bwd_chains, True, st_l, st_c, name="l1_scan_bwd")
    dut_l, dut_c = _s5_dx(dyt, dzz_l, dzz_c, mt_b, bc_b)
    d_mt, d_bc, d_cct = _s5_dw(ut_l, ut_c, dyt, dzz_l, dzz_c, st_l)
    n_g = e // S5_P
    da = jnp.sum(da, axis=0).reshape(n_g // 2, 2, 2, 2, S5_N).transpose(1, 2, 0, 3, 4)
    da = da.reshape(2, 2, n_g, S5_N)
    d_pwb, d_pwc, d_bb, d_ccp, d_dm = _s5_weights_bwd(pwb, pwc, bbw, ccw, d_mt, d_bc, d_cct)
    g_s5 = s5_vjp((d_pwb, d_pwc, d_bb, d_ccp, d_dm, da[:, 0], da[:, 1]))

    vec_l = _pad8(jnp.concatenate([g0, bb0, 1.0 + sc[1][0:1]], axis=0))
    vec_c = _pad8(jnp.concatenate([g0, bb0, 1.0 + sc[1][1:2]], axis=0))
    dr1_l, acc_l = _bwd_inproj1_gt(dut_l, dz_l, wu_t, w_z, xh1_l, rs1_l, dr2, vec_l, "lat")
    dr1_c, acc_c = _bwd_inproj1_gt(dut_c, jnp.zeros((jc, CHUNK * e), BF), wu_t, w_z, xh1_c, rs1_c,
                                   jnp.zeros((jc, CHUNK * d), BF), vec_c, "ctx")
    mod_l = jnp.concatenate([a1[0:1], b1[0:1]], axis=0)
    mod_c = jnp.concatenate([a1[1:2], b1[1:2]], axis=0)
    g_ut_c = _dw_gt(dut_c, xh1_c, False, mod_c, False, None, F32, "l1_dw_in_u_ctx")
    g_ut = _dw_gt(dut_l, xh1_l, False, mod_l, False, g_ut_c, BF, "l1_dw_in_u")
    g_in_z = _dw_cr(xh1_l, dz_l, "mod", "cr", mod_l, False, None, "l1_dw_in_z")
    g_w_in1 = jnp.concatenate([g_ut.T, g_in_z], axis=1)

    dr1_ln, dr1_cn = _from_cr(dr1_l, d), _from_cr(dr1_c, d)
    dq3, acc_g0 = _bwd_outproj0(dr1_ln, dr1_cn, gt[0], w["conv_w_out"], fx, tm)
    def carried(names, parts):
        return _Exchange("scatter", parts, [BIG[n] for n in names]) if scatter else None

    dp42, dcw, grad_x, acc_0, _ = _conv_bwd_inproj0(dq3, p42, cw, w["conv_w_in"], x, ctx, dr1_ln, dr1_cn, a0, nl, tm, None)
    g_w_in0, recv1 = _dw_inproj0(x, ctx, a0, b0, dp42, tm, carried(["ssm_w_in", "ssm_w_out"], [g_w_in1, g_w_out]))
    res = _dw_gt(yt, dt_l, True, None, True, None, BF, "l1_dw_glu", carried(["conv_w_in"], [g_w_in0]))
    g_w_glu, bsum, recv2 = res if scatter else (*res, [])
    g_b_glu = bsum[0]
    g_w_out0, recv3 = _dw_outproj0(q3, dr1_ln, dr1_cn, gt[0], tm, carried(["ssm_w_glu"], [g_w_glu]))
    recv = dict(zip(["ssm_w_in", "ssm_w_out", "conv_w_in", "ssm_w_glu"], recv1 + recv2 + recv3))

    zero = jnp.zeros((d,), F32)
    dm0 = jnp.stack([jnp.concatenate([acc_0[2], acc_0[0], acc_g0[0]]), jnp.concatenate([acc_0[3], acc_0[1], acc_g0[1]])])
    dm1 = jnp.stack([jnp.concatenate([acc_l[1], acc_l[0], acc_f[2]]), jnp.concatenate([acc_c[1], acc_c[0], zero])])
    if own_mod:
        g_ada_w, dc8 = _ada_bwd(c8, w["ada_w"], jnp.stack([_pad8(dm0), _pad8(dm1)]), BF)
        g_mod = {"c_ctx": dc8[0, 1] + dc8[1, 1], "ada_w": g_ada_w,
                 "ada_b": jnp.stack([dm0[0] + dm0[1], dm1[0] + dm1[1]])}
    else:
        g_mod = {"mod": jnp.stack([dm0, dm1])}

    grads = {
        **g_mod,
        "ln_g": jnp.stack([acc_l[2] + acc_c[2], acc_f[0]]),
        "ln_b": jnp.stack([acc_l[3] + acc_c[3], acc_f[1]]),
        "conv_w_in": g_w_in0, "conv_w": dcw[:3].reshape(3, e), "conv_w_out": g_w_out0,
        "ssm_w_in": g_w_in1,
        "ssm_lam_re": g_s5[0], "ssm_lam_im": g_s5[1], "ssm_log_step": g_s5[2],
        "ssm_b_re": g_s5[3], "ssm_b_im": g_s5[4], "ssm_c_re": g_s5[5], "ssm_c_im": g_s5[6], "ssm_d": g_s5[7],
        "ssm_w_glu": g_w_glu, "ssm_b_glu": g_b_glu, "ssm_w_out": g_w_out,
    }
    for n in recv:
        del grads[n]
    return loss, grad_x, grads, recv


WEIGHTS = ["c_ctx", "ada_w", "ada_b", "ln_g", "ln_b", "conv_w_in", "conv_w", "conv_w_out", "ssm_w_in",
           "ssm_lam_re", "ssm_lam_im", "ssm_log_step", "ssm_b_re", "ssm_b_im", "ssm_c_re", "ssm_c_im",
           "ssm_d", "ssm_w_glu", "ssm_b_glu", "ssm_w_out"]
BIG = {"ada_w": 1, "conv_w_in": 1, "conv_w_out": 0, "ssm_w_in": 1, "ssm_w_glu": 0, "ssm_w_out": 0}
SMALL_SHARDED = ["conv_w", "ssm_d", "ssm_b_glu"]
REPLICATED = ["c_ctx", "ada_b", "ln_g", "ln_b", "ssm_lam_re", "ssm_lam_im", "ssm_log_step",
              "ssm_b_re", "ssm_b_im", "ssm_c_re", "ssm_c_im"]
NATIVE_SMALL = ["ssm_b_re", "ssm_b_im", "ssm_c_re", "ssm_c_im"]


def _view2d(name, a):
    return a.reshape(-1, a.shape[-1])


def kernel(x, c, ctx, c_ctx, ada_w, ada_b, ln_g, ln_b, conv_w_in, conv_w, conv_w_out, ssm_w_in, ssm_lam_re, ssm_lam_im, ssm_log_step, ssm_b_re, ssm_b_im, ssm_c_re, ssm_c_im, ssm_d, ssm_w_glu, ssm_b_glu, ssm_w_out, loss_target, m_c_ctx, m_ada_w, m_ada_b, m_ln_g, m_ln_b, m_conv_w_in, m_conv_w, m_conv_w_out, m_ssm_w_in, m_ssm_lam_re, m_ssm_lam_im, m_ssm_log_step, m_ssm_b_re, m_ssm_b_im, m_ssm_c_re, m_ssm_c_im, m_ssm_d, m_ssm_w_glu, m_ssm_b_glu, m_ssm_w_out, v_c_ctx, v_ada_w, v_ada_b, v_ln_g, v_ln_b, v_conv_w_in, v_conv_w, v_conv_w_out, v_ssm_w_in, v_ssm_lam_re, v_ssm_lam_im, v_ssm_log_step, v_ssm_b_re, v_ssm_b_im, v_ssm_c_re, v_ssm_c_im, v_ssm_d, v_ssm_w_glu, v_ssm_b_glu, v_ssm_w_out):
    args = locals()
    wt = {n: args[n] for n in WEIGHTS}
    mt = {n: args["m_" + n] for n in WEIGHTS}
    vt = {n: args["v_" + n] for n in WEIGHTS}

    me = 4 * lax.axis_index("x") + 2 * lax.axis_index("y") + lax.axis_index("c")
    d = x.shape[-1]
    d3 = 3 * d
    wa = d3 // N_DEV

    big_names = [n for n in BIG if n != "ada_w"]
    shard = {n: _view2d(n, wt[n]).astype(BF) for n in big_names}
    small = jnp.concatenate([wt["conv_w"][0], wt["ssm_d"], wt["ssm_b_glu"]], axis=0)
    small = jnp.concatenate([small, jnp.zeros((3, small.shape[1]), F32)], axis=0)
    w_in_full, small_full, c_all = _all_gather([shard["conv_w_in"], small, _pad8(c)], [1, 1, 0], "gather_weights", "gather2")
    late = {n: (shard[n], BIG[n]) for n in big_names if n != "conv_w_in"}
    c16 = jnp.concatenate([c_all[::8], c_ctx[None], jnp.zeros((16 - N_DEV - 1, d), F32)], axis=0)
    ada_w_b = ada_w.astype(BF)
    ada_b_mine = lax.dynamic_slice_in_dim(ada_b, me * wa, wa, axis=1)
    mod_part = _ada_fwd(c16, ada_w_b, ada_b_mine)
    mod_all = _all_gather([mod_part.reshape(32, wa)], [1], "gather_mod")[0].reshape(2, 16, d3)
    mod = jnp.stack([lax.dynamic_index_in_dim(mod_all, me, axis=1, keepdims=False), mod_all[:, N_DEV]], axis=1)
    w = {
        "ln_g": ln_g, "ln_b": ln_b, "conv_w_in": w_in_full, "conv_w": small_full[0:3],
        "ssm_lam_re": ssm_lam_re[0], "ssm_lam_im": ssm_lam_im[0],
        "ssm_log_step": ssm_log_step[0], "ssm_b_re": ssm_b_re[0], "ssm_b_im": ssm_b_im[0],
        "ssm_c_re": ssm_c_re[0], "ssm_c_im": ssm_c_im[0], "ssm_d": small_full[3], "ssm_b_glu": small_full[4],
    }

    loss, grad_x, g, recv_big = _local_step(x[0], c[0], ctx[0], c_ctx, loss_target[0], w, late, True, mod)

    dmod_all = _all_gather([_pad8(g["mod"].reshape(4, d3))], [0], "gather_dmod")[0].reshape(N_DEV, 8, d3)
    dmod_all = dmod_all[:, :4].reshape(N_DEV, 2, 2, d3)
    dm_ctx = dmod_all[0, :, 1]
    for p in range(1, N_DEV):
        dm_ctx = dm_ctx + dmod_all[p, :, 1]
    dm16 = jnp.concatenate([dmod_all[:, :, 0].transpose(1, 0, 2), dm_ctx[:, None], jnp.zeros((2, 16 - N_DEV - 1, d3), F32)], axis=1)
    g_ada_w, dc16 = _ada_bwd(c16, ada_w_b, lax.dynamic_slice_in_dim(dm16, me * wa, wa, axis=2), F32)
    g["c_ctx"] = dc16[0, N_DEV] + dc16[1, N_DEV]
    g_ada_b = jnp.sum(dm16, axis=1)

    blob_names = [n for n in REPLICATED if n != "ada_b"] + SMALL_SHARDED
    flat = jnp.concatenate([g[n].reshape(-1).astype(F32) for n in blob_names] + [loss.reshape(1)])
    nflat = flat.shape[0]
    rows = -(-nflat // (N_DEV * 128 * 8)) * 8
    flat = jnp.concatenate([flat, jnp.zeros((N_DEV * rows * 128 - nflat,), F32)]).reshape(N_DEV * rows, 128)
    last = [n for n in big_names if n not in recv_big]
    recv = _all_to_all([_view2d(n, g[n]) for n in last] + [flat], [BIG[n] for n in last] + [0], "scatter_grads")
    recv_big.update(zip(last, recv[:-1]))
    blob_sum = _sum_partials(recv[-1])
    blob = _all_gather([blob_sum], [0], "gather_small_grads", "gather2")[0].reshape(-1)
    small_g, off = {"ada_b": g_ada_b}, 0
    for n in blob_names:
        shape = wt[n].shape if n in REPLICATED else (*wt[n].shape[:-1], wt[n].shape[-1] * N_DEV)
        size = math.prod(shape)
        small_g[n] = blob[off:off + size].reshape(shape)
        off += size
    loss = blob[off]
    for n in SMALL_SHARDED:
        size = wt[n].shape[-1]
        small_g[n] = lax.dynamic_slice_in_dim(small_g[n], me * size, size, axis=small_g[n].ndim - 1)

    out_g, out_d, out_m, out_v = {}, {}, {}, {}
    recv_big["ada_w"] = _view2d("ada_w", g_ada_w)[None]
    for n in BIG:
        stack = recv_big[n]
        shp = wt[n].shape
        res = _adamw(stack, _view2d(n, wt[n]), _view2d(n, mt[n]), _view2d(n, vt[n]), "adamw_" + n)
        out_g[n], out_d[n], out_m[n], out_v[n] = [r.reshape(shp) for r in res]
    for n in NATIVE_SMALL:
        shp = wt[n].shape
        v2 = lambda a: a.reshape(-1, shp[-1])
        res = _adamw(v2(small_g.pop(n))[None], v2(wt[n]), v2(mt[n]), v2(vt[n]), "adamw_" + n)
        out_g[n], out_d[n], out_m[n], out_v[n] = [r.reshape(shp) for r in res]
    names = list(small_g)
    cat = lambda t: jnp.concatenate([t[n].reshape(-1) for n in names])
    gs, ws, ms, vs = cat(small_g), cat(wt), cat(mt), cat(vt)
    ns = gs.shape[0]
    rs = -(-ns // (128 * 512)) * 512
    padr = lambda a: jnp.concatenate([a, jnp.ones((rs * 128 - ns,), F32)]).reshape(rs, 128)
    res = _adamw(padr(gs)[None], padr(ws), padr(ms), padr(vs), "adamw_small")
    off = 0
    for n in names:
        size = math.prod(wt[n].shape)
        out_g[n], out_d[n], out_m[n], out_v[n] = [r.reshape(-1)[off:off + size].reshape(wt[n].shape) for r in res]
        off += size

    return (loss, grad_x[None], *[out_g[n] for n in WEIGHTS], *[out_d[n] for n in WEIGHTS],
            *[out_m[n] for n in WEIGHTS], *[out_v[n] for n in WEIGHTS])
```

```python
import math

import jax
import jax.numpy as jnp
from jax import lax
from jax.experimental import pallas as pl
from jax.experimental.pallas import tpu as pltpu

F32 = jnp.float32
BF = jnp.bfloat16
MESH = pl.DeviceIdType.MESH
N_DEV = 8

GRID_W = 64
CHUNK = 16
S5_P = 16
S5_N = 64
LN_EPS = 1e-5
DN_ALPHA = 4.0 ** 0.25
ADAM_LR, ADAM_B1, ADAM_B2, ADAM_EPS, ADAM_WD, ADAM_STEP = 1e-3, 0.9, 0.999, 1e-8, 0.01, 10
GELU_C0 = math.sqrt(2.0 / math.pi)
GELU_C1 = 0.044715
VMEM_MB = 52

ANY = pl.BlockSpec(memory_space=pl.ANY)


def _cparams():
    return pltpu.CompilerParams(vmem_limit_bytes=VMEM_MB << 20)


def _dot(a, b):
    return jnp.dot(a, b, preferred_element_type=F32)


def _dot_nt(a, b):
    return lax.dot_general(a, b, (((1,), (1,)), ((), ())), preferred_element_type=F32)


def _dot_tn(a, b):
    return lax.dot_general(a, b, (((0,), (0,)), ((), ())), preferred_element_type=F32)


def _sigmoid(x):
    return 1.0 / (1.0 + jnp.exp(-x))


def _gelu_parts(y):
    u = y * y
    th = jnp.tanh(y * (GELU_C0 + (GELU_C0 * GELU_C1) * u))
    hy = 0.5 * y
    g = hy + hy * th
    dg = (0.5 + 0.5 * th) + hy * (1.0 - th * th) * (GELU_C0 + (3.0 * GELU_C0 * GELU_C1) * u)
    return g, dg


def _full(shape):
    nd = len(shape)
    return pl.BlockSpec(shape, lambda *_: (0,) * nd)


def _mesh_pos():
    x, y, c = lax.axis_index("x"), lax.axis_index("y"), lax.axis_index("c")
    return x, y, c


def _peer(pos, k):
    x, y, c = pos
    px = 1 - x if (k >> 2) & 1 else x
    py = 1 - y if (k >> 1) & 1 else y
    pc = 1 - c if k & 1 else c
    return (px, py, pc), 4 * px + 2 * py + pc


def _shard_at(ref, axis, idx, n):
    if axis == 0:
        return ref.at[pl.ds(idx * n, n)]
    return ref.at[:, pl.ds(idx * n, n)]


class _Exchange:
    def __init__(self, kind, arrays, axes):
        self.kind, self.axes, self.n = kind, list(axes), len(arrays)
        self.arrays = list(arrays)
        self.out_shape = []
        for s, ax in zip(arrays, axes):
            shp = list(s.shape)
            if kind == "scatter":
                shp[ax] //= N_DEV
                self.out_shape.append(jax.ShapeDtypeStruct((N_DEV, *shp), s.dtype))
            else:
                shp[ax] *= N_DEV
                self.out_shape.append(jax.ShapeDtypeStruct(tuple(shp), s.dtype))
        self.scratch = [pltpu.SemaphoreType.DMA((self.n, N_DEV - 1)), pltpu.SemaphoreType.DMA((self.n, N_DEV - 1)),
                        pltpu.SemaphoreType.DMA((self.n,))]

    def _copies(self, ins, outs, sems):
        send_sems, recv_sems, local_sems = sems
        pos = _mesh_pos()
        x, y, c = pos
        me = 4 * x + 2 * y + c
        local, sends, chained, recvs = [], [], [], []
        for i in range(self.n):
            ax = self.axes[i]
            if self.kind == "scatter":
                size = ins[i].shape[ax] // N_DEV
                src = lambda idx, i=i, ax=ax, size=size: _shard_at(ins[i], ax, idx, size)
                dst = lambda idx, i=i: outs[i].at[idx]
            else:
                size = ins[i].shape[ax]
                src = lambda idx, i=i: ins[i]
                dst = lambda idx, i=i, ax=ax, size=size: _shard_at(outs[i], ax, idx, size)

            def copy(k, s, d, to, i=i):
                return pltpu.make_async_remote_copy(src_ref=s, dst_ref=d, send_sem=send_sems.at[i, k],
                                                    recv_sem=recv_sems.at[i, k], device_id=to, device_id_type=MESH)

            local.append(pltpu.make_async_copy(src(me), dst(me), local_sems.at[i]))
            if self.kind == "gather2":
                sib, sib_i = (x, y, 1 - c), 4 * x + 2 * y + (1 - c)
                chips = [(1 - x, y), (x, 1 - y), (1 - x, 1 - y)]
                sends.append(copy(0, src(me), dst(me), sib))
                recvs.append(copy(0, src(me), dst(sib_i), sib))
                for j, (cx, cy) in enumerate(chips):
                    same, other = 4 * cx + 2 * cy + c, 4 * cx + 2 * cy + (1 - c)
                    sends.append(copy(1 + j, src(me), dst(me), (cx, cy, c)))
                    chained.append((copy(1 + j, dst(same), dst(same), (cx, cy, c)), copy(4 + j, dst(same), dst(same), sib)))
                    recvs.append(copy(4 + j, dst(other), dst(other), sib))
            else:
                for k in range(1, N_DEV):
                    peer, pidx = _peer(pos, k)
                    out_src = src(pidx) if self.kind == "scatter" else src(me)
                    sends.append(copy(k - 1, out_src, dst(me), peer))
                    recvs.append(copy(k - 1, out_src, dst(pidx), peer))
        return local, sends, chained, recvs

    def start(self, ins, outs, sems):
        local, sends, _, _ = self._copies(ins, outs, sems)
        for cp in local + sends:
            cp.start()

    def wait(self, ins, outs, sems):
        local, sends, chained, recvs = self._copies(ins, outs, sems)
        for arrival, released in chained:
            arrival.wait_recv()
            released.start()
        for cp in recvs:
            cp.wait_recv()
        for cp in sends + [released for _, released in chained]:
            cp.wait_send()
        for cp in local:
            cp.wait()

    def run(self, name):
        n = self.n

        def body(*refs):
            ins, outs, sems = refs[:n], refs[n:2 * n], refs[2 * n:]
            self.start(ins, outs, sems)
            self.wait(ins, outs, sems)

        return pl.pallas_call(body, name=name, out_shape=self.out_shape, in_specs=[ANY] * n, out_specs=[ANY] * n,
                              scratch_shapes=self.scratch)(*self.arrays)


def _hosted_call(body, xch, grid, in_specs, out_specs, out_shape, scratch, args, name):
    out_specs, out_shape = list(out_specs), list(out_shape)
    n_in, n_out = len(in_specs), len(out_specs)
    if xch is None:
        res = pl.pallas_call(body, name=name, grid=grid, in_specs=in_specs, out_specs=out_specs, out_shape=out_shape,
                             scratch_shapes=list(scratch), compiler_params=_cparams())(*args)
        return list(res), []
    n = xch.n
    rank = len(grid)

    def wrapped(*refs):
        ins, x_ins = refs[:n_in], refs[n_in:n_in + n]
        outs = refs[n_in + n:n_in + n + n_out]
        x_outs = refs[n_in + n + n_out:n_in + 2 * n + n_out]
        rest = refs[n_in + 2 * n + n_out:]
        own, sems = rest[:len(rest) - 3], rest[len(rest) - 3:]
        ids = [pl.program_id(a) for a in range(rank)]
        first, last = ids[0] == 0, ids[0] == grid[0] - 1
        for a in range(1, rank):
            first = jnp.logical_and(first, ids[a] == 0)
            last = jnp.logical_and(last, ids[a] == grid[a] - 1)

        @pl.when(first)
        def _():
            xch.start(x_ins, x_outs, sems)

        body(*ins, *outs, *own)

        @pl.when(last)
        def _():
            xch.wait(x_ins, x_outs, sems)

    res = pl.pallas_call(
        wrapped, name=name, grid=grid, in_specs=list(in_specs) + [ANY] * n, out_specs=out_specs + [ANY] * n,
        out_shape=out_shape + xch.out_shape, scratch_shapes=list(scratch) + xch.scratch, compiler_params=_cparams(),
    )(*args, *xch.arrays)
    return list(res[:n_out]), list(res[n_out:])


def _all_gather(shards, axes, name, kind="gather"):
    return _Exchange(kind, shards, axes).run(name)


def _all_to_all(parts, axes, name):
    return _Exchange("scatter", parts, axes).run(name)


def _ada_fwd(cv, ada_w, ada_b):
    nl, d, wd = ada_w.shape
    r = cv.shape[0]

    def body(c_ref, w_ref, b_ref, o_ref):
        c = c_ref[...]
        s = (c * _sigmoid(c)).astype(BF)
        o_ref[0] = _dot(s, w_ref[0]) + b_ref[0]

    return pl.pallas_call(
        body, name="ada_fwd", grid=(nl,),
        in_specs=[_full((r, d)), pl.BlockSpec((1, d, wd), lambda l: (l, 0, 0)), pl.BlockSpec((1, 1, wd), lambda l: (l, 0, 0))],
        out_specs=pl.BlockSpec((1, r, wd), lambda l: (l, 0, 0)),
        out_shape=jax.ShapeDtypeStruct((nl, r, wd), F32), compiler_params=_cparams(),
    )(cv, ada_w, ada_b.reshape(nl, 1, wd))


def _ada_bwd(cv, ada_w, dm, out_dtype):
    nl, d, wd = ada_w.shape
    r = cv.shape[0]

    def body(c_ref, w_ref, dm_ref, dw_ref, dc_ref):
        c = c_ref[...]
        sg = _sigmoid(c)
        s = (c * sg).astype(BF)
        dmv = dm_ref[0].astype(BF)
        dw_ref[0] = _dot_tn(s, dmv).astype(out_dtype)
        dc_ref[0] = _dot_nt(dmv, w_ref[0]) * (sg * (1.0 + c * (1.0 - sg)))

    return pl.pallas_call(
        body, name="ada_bwd", grid=(nl,),
        in_specs=[_full((r, d)), pl.BlockSpec((1, d, wd), lambda l: (l, 0, 0)), pl.BlockSpec((1, r, wd), lambda l: (l, 0, 0))],
        out_specs=[pl.BlockSpec((1, d, wd), lambda l: (l, 0, 0)), pl.BlockSpec((1, r, d), lambda l: (l, 0, 0))],
        out_shape=[jax.ShapeDtypeStruct((nl, d, wd), out_dtype), jax.ShapeDtypeStruct((nl, r, d), F32)],
        compiler_params=_cparams(),
    )(cv, ada_w, dm)


def _sum_partials(stack):
    _, r, c = stack.shape

    def body(s_ref, o_ref):
        acc = s_ref[0]
        for p in range(1, N_DEV):
            acc = acc + s_ref[p]
        o_ref[...] = acc

    return pl.pallas_call(body, name="sum_partials", out_shape=jax.ShapeDtypeStruct((r, c), F32),
                          in_specs=[_full(stack.shape)], out_specs=_full((r, c)), grid=(1,),
                          compiler_params=_cparams())(stack)


def _adamw(gstack, w, m, v, name):
    p, r, c = gstack.shape
    tr = r
    for cand in (512 if c <= 256 else 256, 128, 64, 32, 16, 8):
        if r % cand == 0 and r > cand:
            tr = cand
            break
    bc1 = 1.0 - ADAM_B1 ** ADAM_STEP
    bc2 = 1.0 - ADAM_B2 ** ADAM_STEP

    def body(g_ref, w_ref, m_ref, v_ref, go_ref, d_ref, mo_ref, vo_ref):
        g = g_ref[0].astype(F32)
        for q in range(1, p):
            g = g + g_ref[q].astype(F32)
        mn = ADAM_B1 * m_ref[...] + (1.0 - ADAM_B1) * g
        vn = ADAM_B2 * v_ref[...] + (1.0 - ADAM_B2) * (g * g)
        go_ref[...] = g
        mo_ref[...] = mn
        vo_ref[...] = vn
        d_ref[...] = -ADAM_LR * ((mn / bc1) / (jnp.sqrt(vn / bc2) + ADAM_EPS) + ADAM_WD * w_ref[...])

    row = pl.BlockSpec((tr, c), lambda i: (i, 0))
    sds = jax.ShapeDtypeStruct((r, c), F32)
    return pl.pallas_call(
        body, name=name, grid=(r // tr,),
        in_specs=[pl.BlockSpec((p, tr, c), lambda i: (0, i, 0)), row, row, row],
        out_specs=[row, row, row, row], out_shape=[sds, sds, sds, sds], compiler_params=_cparams(),
    )(gstack, w, m, v)


def _lat_or_ctx_specs(tm, d, nl, grid_rank, row_axis):
    def lat(*ids):
        return (jnp.minimum(ids[row_axis], nl - 1), 0)

    def ctx(*ids):
        return (jnp.maximum(ids[row_axis] - nl, 0), 0)

    return pl.BlockSpec((tm, d), lat), pl.BlockSpec((tm, d), ctx)


def _sel_row(ref, is_ctx):
    return jnp.where(is_ctx, ref[1:2, :], ref[0:1, :])


def _inproj0(x, ctx, a2, b2, w, tgt, tm, xch=None):
    l, d = x.shape
    nl, nc = l // tm, ctx.shape[0] // tm
    e = w.shape[1] // 4
    half = e // 2
    tjo = tm // CHUNK

    def body(x_ref, c_ref, a_ref, b_ref, w_hbm, t_ref, o_ref, tc_ref, w_ref, ts_ref):
        i = pl.program_id(0)

        @pl.when(i == 0)
        def _():
            pltpu.sync_copy(w_hbm, w_ref)

        is_ctx = i >= nl
        xv = jnp.where(is_ctx, c_ref[...], x_ref[...])
        h = (xv * _sel_row(a_ref, is_ctx) + _sel_row(b_ref, is_ctx)).astype(BF)
        for k in range(4):
            r = _dot(h, w_ref[:, k * e:(k + 1) * e])
            o_ref[k, 0] = r[:, :half].astype(BF)
            o_ref[k, 1] = r[:, half:].astype(BF)

        @pl.when(jnp.logical_not(is_ctx))
        def _():
            for lb in range(d // 128):
                ts_ref[lb] = t_ref[:, lb * 128:(lb + 1) * 128]
            for s in range(CHUNK):
                for lb in range(d // 128):
                    tc_ref[:, s * d + lb * 128:s * d + (lb + 1) * 128] = ts_ref.at[lb][pl.ds(s, tjo, stride=CHUNK), :]

    lat, cx = _lat_or_ctx_specs(tm, d, nl, 1, 0)
    (p42, tgt_cr), extra = _hosted_call(
        body, xch, grid=(nl + nc,),
        in_specs=[lat, cx, _full((2, d)), _full((2, d)), ANY, lat],
        out_specs=[pl.BlockSpec((4, 2, tm, half), lambda i: (0, 0, i, 0)),
                   pl.BlockSpec((tjo, CHUNK * d), lambda i: (jnp.minimum(i, nl - 1), 0))],
        out_shape=[jax.ShapeDtypeStruct((4, 2, l + ctx.shape[0], half), BF),
                   jax.ShapeDtypeStruct((l // CHUNK, CHUNK * d), F32)],
        scratch=[pltpu.VMEM(w.shape, BF), pltpu.VMEM((d // 128, tm, 128), F32)],
        args=(x, ctx, a2, b2, w, tgt), name="l0_inproj")
    return p42, tgt_cr, extra


def _conv_taps(u, w_up, w_mid, w_dn, pos, rl, tm):
    up = jnp.where(pos == 0, 0.0, pltpu.roll(u, 1, 0))
    dn = jnp.where(pos == rl - 1, 0.0, pltpu.roll(u, tm - 1, 0))
    return w_up * up + w_mid * u + w_dn * dn, up, dn


def _conv_halo_specs(tm, tc, nl, lead):
    hb = tm // GRID_W

    def prev(j, i):
        return (0, 1, jnp.maximum(jnp.minimum(i, nl - 1) * hb - 1, 0), j)

    def nxt(j, i):
        return (0, 1, jnp.minimum((jnp.minimum(i, nl - 1) + 1) * hb, nl * hb - 1), j)

    return pl.BlockSpec((lead, 1, GRID_W, tc), prev), pl.BlockSpec((lead, 1, GRID_W, tc), nxt)


def _conv_fwd(p42, cw, nl, tm, tc):
    _, _, r, half = p42.shape
    nt = r // tm

    def body(p_ref, hp_ref, hn_ref, cw_ref, o_ref):
        i = pl.program_id(1)
        is_ctx = i >= nl
        row = lax.broadcasted_iota(jnp.int32, (tm, tc), 0)
        rl = jnp.where(is_ctx, tm, GRID_W)
        pos = jnp.bitwise_and(row, rl - 1)

        def gate(hv, yc):
            bg = p_ref[0, hv].astype(F32)
            z = p_ref[3, hv].astype(F32)
            return (bg * yc * (z * _sigmoid(z))).astype(BF)

        u_h = p_ref[1, 0].astype(F32) * p_ref[2, 0].astype(F32)
        w_h = cw_ref[:, 0, :]
        o_ref[0] = gate(0, _conv_taps(u_h, w_h[0:1], w_h[1:2], w_h[2:3], pos, rl, tm)[0])
        u_v = p_ref[1, 1].astype(F32) * p_ref[2, 1].astype(F32)
        w_v = cw_ref[:, 1, :]

        @pl.when(is_ctx)
        def _():
            o_ref[1] = gate(1, _conv_taps(u_v, w_v[0:1], w_v[1:2], w_v[2:3], pos, rl, tm)[0])

        @pl.when(jnp.logical_not(is_ctx))
        def _():
            up = hp_ref[1, 0].astype(F32) * hp_ref[2, 0].astype(F32) * (i > 0).astype(F32)
            dn = hn_ref[1, 0].astype(F32) * hn_ref[2, 0].astype(F32) * (i < nl - 1).astype(F32)
            ext = jnp.concatenate([up, u_v, dn], axis=0)
            yc = w_v[0:1] * ext[0:tm] + w_v[1:2] * u_v + w_v[2:3] * ext[2 * GRID_W:tm + 2 * GRID_W]
            o_ref[1] = gate(1, yc)

    hp, hn = _conv_halo_specs(tm, tc, nl, 4)
    return pl.pallas_call(
        body, name="l0_conv_fwd", grid=(half // tc, nt),
        in_specs=[pl.BlockSpec((4, 2, tm, tc), lambda j, i: (0, 0, i, j)), hp, hn,
                  pl.BlockSpec((3, 2, tc), lambda j, i: (0, 0, j))],
        out_specs=pl.BlockSpec((2, tm, tc), lambda j, i: (0, i, j)),
        out_shape=jax.ShapeDtypeStruct((2, r, half), BF), compiler_params=_cparams(),
    )(p42, p42, p42, cw)


def _outproj_ln0(q3, w_out, x, ctx, gt2, tm):
    l, d = x.shape
    lc = ctx.shape[0]
    nl, nc = l // tm, lc // tm
    _, r, half = q3.shape
    tjo = tm // CHUNK

    def body(q_ref, w_hbm, x_ref, c_ref, g_ref, xl_ref, xc_ref, rl_ref, rc_ref, fx_ref, w_ref, xs_ref, rs_ref):
        i = pl.program_id(0)

        @pl.when(i == 0)
        def _():
            pltpu.sync_copy(w_hbm, w_ref)

        is_ctx = i >= nl
        fx = _dot(q_ref[0], w_ref[:half, :]) + _dot(q_ref[1], w_ref[half:, :])
        xv = jnp.where(is_ctx, c_ref[...], x_ref[...])
        rr = DN_ALPHA * xv + _sel_row(g_ref, is_ctx) * fx
        mu = jnp.mean(rr, axis=-1, keepdims=True)
        cen = rr - mu
        rstd = lax.rsqrt(jnp.mean(cen * cen, axis=-1, keepdims=True) + LN_EPS)
        xh = cen * rstd
        for lb in range(d // 128):
            xs_ref[lb] = xh[:, lb * 128:(lb + 1) * 128]
        rs_ref[...] = jnp.broadcast_to(rstd, (tm, 128))
        fx_ref[...] = fx.astype(BF)

        def to_cr(xo_ref, ro_ref):
            for s in range(CHUNK):
                for lb in range(d // 128):
                    xo_ref[:, s * d + lb * 128:s * d + (lb + 1) * 128] = xs_ref.at[lb][pl.ds(s, tjo, stride=CHUNK), :]
                ro_ref[:, s * 128:(s + 1) * 128] = rs_ref[pl.ds(s, tjo, stride=CHUNK), :]

        @pl.when(jnp.logical_not(is_ctx))
        def _():
            to_cr(xl_ref, rl_ref)

        @pl.when(is_ctx)
        def _():
            to_cr(xc_ref, rc_ref)

    lat, cx = _lat_or_ctx_specs(tm, d, nl, 1, 0)
    lat_o = lambda w_: pl.BlockSpec((tjo, CHUNK * w_), lambda i: (jnp.minimum(i, nl - 1), 0))
    ctx_o = lambda w_: pl.BlockSpec((tjo, CHUNK * w_), lambda i: (jnp.maximum(i - nl, 0), 0))
    return pl.pallas_call(
        body, name="l0_outproj_ln", grid=(nl + nc,),
        in_specs=[pl.BlockSpec((2, tm, half), lambda i: (0, i, 0)), ANY, lat, cx, _full((2, d))],
        out_specs=[lat_o(d), ctx_o(d), lat_o(128), ctx_o(128), pl.BlockSpec((tm, d), lambda i: (i, 0))],
        out_shape=[jax.ShapeDtypeStruct((l // CHUNK, CHUNK * d), F32), jax.ShapeDtypeStruct((lc // CHUNK, CHUNK * d), F32),
                   jax.ShapeDtypeStruct((l // CHUNK, CHUNK * 128), F32), jax.ShapeDtypeStruct((lc // CHUNK, CHUNK * 128), F32),
                   jax.ShapeDtypeStruct((r, d), BF)],
        scratch_shapes=[pltpu.VMEM(w_out.shape, BF), pltpu.VMEM((d // 128, tm, 128), F32), pltpu.VMEM((tm, 128), F32)],
        compiler_params=_cparams(),
    )(q3, w_out, x, ctx, gt2)


def _bwd_outproj0(dr_l, dr_c, gt2, w_out, fx, tm):
    l, d = dr_l.shape
    nl, nc = l // tm, dr_c.shape[0] // tm
    e = w_out.shape[0]
    half = e // 2
    r = l + dr_c.shape[0]

    def body(dl_ref, dc_ref, g_ref, w_hbm, fx_ref, dq_ref, acc_ref, w_ref):
        i = pl.program_id(0)

        @pl.when(i == 0)
        def _():
            pltpu.sync_copy(w_hbm, w_ref)
            acc_ref[...] = jnp.zeros_like(acc_ref)

        is_ctx = i >= nl
        dr = jnp.where(is_ctx, dc_ref[...], dl_ref[...]).astype(F32)
        dfx = (dr * _sel_row(g_ref, is_ctx)).astype(BF)
        dq_ref[0] = _dot_nt(dfx, w_ref[:half, :]).astype(BF)
        dq_ref[1] = _dot_nt(dfx, w_ref[half:, :]).astype(BF)
        s = jnp.sum(dr * fx_ref[...].astype(F32), axis=0, keepdims=True)
        sel = is_ctx.astype(F32)
        acc_ref[0:1, :] += s * (1.0 - sel)
        acc_ref[1:2, :] += s * sel

    lat, cx = _lat_or_ctx_specs(tm, d, nl, 1, 0)
    return pl.pallas_call(
        body, name="l0_bwd_outproj", grid=(nl + nc,),
        in_specs=[lat, cx, _full((2, d)), ANY, pl.BlockSpec((tm, d), lambda i: (i, 0))],
        out_specs=[pl.BlockSpec((2, tm, half), lambda i: (0, i, 0)), _full((8, d))],
        out_shape=[jax.ShapeDtypeStruct((2, r, half), BF), jax.ShapeDtypeStruct((8, d), F32)],
        scratch_shapes=[pltpu.VMEM(w_out.shape, BF)], compiler_params=_cparams(),
    )(dr_l, dr_c, gt2, w_out, fx)


def _conv_bwd_inproj0(dq3, p42, cw, w_in, x, ctx, dr_l, dr_c, a2, nl, tm, xch=None):
    l, d = x.shape
    _, _, r, half = p42.shape
    nt = r // tm
    e = 2 * half
    cc = min(512, half)
    n_cc = half // cc

    def body(dq_ref, dqp_ref, dqn_ref, p_ref, hp_ref, hn_ref, cw_ref, w_hbm, x_ref, c_ref, dl_ref, dc_ref, a_ref,
             dp_ref, dw_ref, gx_ref, acc_ref, w_ref, dh_ref):
        i, hv = pl.program_id(0), pl.program_id(1)
        is_ctx = i >= nl

        @pl.when(jnp.logical_and(i == 0, hv == 0))
        def _():
            pltpu.sync_copy(w_hbm, w_ref)
            acc_ref[...] = jnp.zeros_like(acc_ref)
            dw_ref[...] = jnp.zeros_like(dw_ref)

        row = lax.broadcasted_iota(jnp.int32, (tm, cc), 0)
        rl = jnp.where(is_ctx, tm, GRID_W)
        pos = jnp.bitwise_and(row, rl - 1)

        def pieces(dq, bg, z):
            sz = _sigmoid(z)
            sil = z * sz
            return dq * bg * sil, dq * sil, dq * bg * (sz * (1.0 + z * (1.0 - sz)))

        def emit(hvs, c, parts, dyc, u_up, u, u_dn, dh):
            lanes = slice(c * cc, (c + 1) * cc)
            for k, part in enumerate(parts):
                pb = part.astype(BF)
                dp_ref[k, 0, :, lanes] = pb
                c0 = k * e + hvs * half + c * cc
                t = _dot_nt(pb, w_ref[:, c0:c0 + cc])
                dh = t if dh is None else dh + t
            dw_ref[0:1, hvs, lanes] += jnp.sum(dyc * u_up, axis=0, keepdims=True)
            dw_ref[1:2, hvs, lanes] += jnp.sum(dyc * u, axis=0, keepdims=True)
            dw_ref[2:3, hvs, lanes] += jnp.sum(dyc * u_dn, axis=0, keepdims=True)
            return dh

        def seq_half(hvs):
            dh = None
            for c in range(n_cc):
                lanes = slice(c * cc, (c + 1) * cc)
                bg, cg = p_ref[0, 0, :, lanes].astype(F32), p_ref[1, 0, :, lanes].astype(F32)
                v, z = p_ref[2, 0, :, lanes].astype(F32), p_ref[3, 0, :, lanes].astype(F32)
                w = cw_ref[:, hvs, lanes]
                u = cg * v
                yc, u_up, u_dn = _conv_taps(u, w[0:1], w[1:2], w[2:3], pos, rl, tm)
                dyc, dbg_f, dz_f = pieces(dq_ref[0, :, lanes].astype(F32), bg, z)
                du = _conv_taps(dyc, w[2:3], w[1:2], w[0:1], pos, rl, tm)[0]
                dh = emit(hvs, c, (dbg_f * yc, du * v, du * cg, dz_f * yc), dyc, u_up, u, u_dn, dh)
            return dh

        def col_half():
            m_up = (i > 0).astype(F32)
            m_dn = (i < nl - 1).astype(F32)
            dh = None
            for c in range(n_cc):
                lanes = slice(c * cc, (c + 1) * cc)
                bg, cg = p_ref[0, 0, :, lanes].astype(F32), p_ref[1, 0, :, lanes].astype(F32)
                v, z = p_ref[2, 0, :, lanes].astype(F32), p_ref[3, 0, :, lanes].astype(F32)
                w = cw_ref[:, 1, lanes]
                u = cg * v

                def halo(h_ref, dqh_ref, msk):
                    hb, hc = h_ref[0, 0, :, lanes].astype(F32), h_ref[1, 0, :, lanes].astype(F32)
                    hv_, hz = h_ref[2, 0, :, lanes].astype(F32), h_ref[3, 0, :, lanes].astype(F32)
                    return hc * hv_ * msk, pieces(dqh_ref[0, :, lanes].astype(F32), hb, hz)[0] * msk

                u_p, dyc_p = halo(hp_ref, dqp_ref, m_up)
                u_n, dyc_n = halo(hn_ref, dqn_ref, m_dn)
                u_ext = jnp.concatenate([u_p, u, u_n], axis=0)
                u_up, u_dn = u_ext[0:tm], u_ext[2 * GRID_W:tm + 2 * GRID_W]
                yc = w[0:1] * u_up + w[1:2] * u + w[2:3] * u_dn
                dyc, dbg_f, dz_f = pieces(dq_ref[0, :, lanes].astype(F32), bg, z)
                d_ext = jnp.concatenate([dyc_p, dyc, dyc_n], axis=0)
                du = w[0:1] * d_ext[2 * GRID_W:tm + 2 * GRID_W] + w[1:2] * dyc + w[2:3] * d_ext[0:tm]
                dh = emit(1, c, (dbg_f * yc, du * v, du * cg, dz_f * yc), dyc, u_up, u, u_dn, dh)
            return dh

        @pl.when(hv == 0)
        def _():
            dh_ref[...] = seq_half(0)

        @pl.when(jnp.logical_and(hv == 1, is_ctx))
        def _():
            dh_ref[...] += seq_half(1)

        @pl.when(jnp.logical_and(hv == 1, jnp.logical_not(is_ctx)))
        def _():
            dh_ref[...] += col_half()

        @pl.when(hv == 1)
        def _():
            dh = dh_ref[...]
            xv = jnp.where(is_ctx, c_ref[...], x_ref[...])
            s_sc = jnp.sum(dh * xv, axis=0, keepdims=True)
            s_sh = jnp.sum(dh, axis=0, keepdims=True)
            sel = is_ctx.astype(F32)
            acc_ref[0:1, :] += s_sc * (1.0 - sel)
            acc_ref[1:2, :] += s_sc * sel
            acc_ref[2:3, :] += s_sh * (1.0 - sel)
            acc_ref[3:4, :] += s_sh * sel

        @pl.when(jnp.logical_and(hv == 1, jnp.logical_not(is_ctx)))
        def _():
            gx_ref[...] = DN_ALPHA * dl_ref[...].astype(F32) + dh_ref[...] * a_ref[0:1, :]

    hb = tm // GRID_W
    prev_blk = lambda i: jnp.maximum(jnp.minimum(i, nl - 1) * hb - 1, 0)
    next_blk = lambda i: jnp.minimum((jnp.minimum(i, nl - 1) + 1) * hb, nl * hb - 1)
    lat, cx = _lat_or_ctx_specs(tm, d, nl, 2, 0)
    (dp42, dcw, gx, acc), extra = _hosted_call(
        body, xch, grid=(nt, 2),
        in_specs=[pl.BlockSpec((1, tm, half), lambda i, h: (h, i, 0)),
                  pl.BlockSpec((1, GRID_W, half), lambda i, h: (1, prev_blk(i), 0)),
                  pl.BlockSpec((1, GRID_W, half), lambda i, h: (1, next_blk(i), 0)),
                  pl.BlockSpec((4, 1, tm, half), lambda i, h: (0, h, i, 0)),
                  pl.BlockSpec((4, 1, GRID_W, half), lambda i, h: (0, 1, prev_blk(i), 0)),
                  pl.BlockSpec((4, 1, GRID_W, half), lambda i, h: (0, 1, next_blk(i), 0)),
                  _full((3, 2, half)), ANY, lat, cx, lat, cx, _full((2, d))],
        out_specs=[pl.BlockSpec((4, 1, tm, half), lambda i, h: (0, h, i, 0)), _full((8, 2, half)),
                   pl.BlockSpec((tm, d), lambda i, h: (jnp.minimum(i, nl - 1), 0)), _full((8, d))],
        out_shape=[jax.ShapeDtypeStruct(p42.shape, BF), jax.ShapeDtypeStruct((8, 2, half), F32),
                   jax.ShapeDtypeStruct((l, d), F32), jax.ShapeDtypeStruct((8, d), F32)],
        scratch=[pltpu.VMEM(w_in.shape, BF), pltpu.VMEM((tm, d), F32)],
        args=(dq3, dq3, dq3, p42, p42, p42, cw, w_in, x, ctx, dr_l, dr_c, a2), name="l0_conv_bwd_inproj")
    return dp42, dcw, gx, acc, extra


def _dw_inproj0(x, ctx, a2, b2, dp42, tm, xch=None):
    l, d = x.shape
    lc = ctx.shape[0]
    assert lc == tm
    tl = 4 * tm if l % (4 * tm) == 0 else tm
    nl = l // tl
    half = dp42.shape[-1]
    e = 2 * half

    def body(x_ref, c_ref, a_ref, b_ref, dpl_ref, dpc_ref, o_ref, acc_ref):
        i = pl.program_id(1)

        @pl.when(i == 0)
        def _():
            acc_ref[...] = jnp.zeros_like(acc_ref)

        def add(rows_ref, dp_ref, sel):
            h = (rows_ref[...] * a_ref[sel:sel + 1, :] + b_ref[sel:sel + 1, :]).astype(BF)
            acc_ref[:, :half] += _dot_tn(h, dp_ref[0, 0])
            acc_ref[:, half:] += _dot_tn(h, dp_ref[0, 1])

        @pl.when(i < nl)
        def _():
            add(x_ref, dpl_ref, 0)

        @pl.when(i == nl)
        def _():
            add(c_ref, dpc_ref, 1)
            o_ref[...] = acc_ref[...].astype(BF)

    (g_w,), extra = _hosted_call(
        body, xch, grid=(4, nl + 1),
        in_specs=[pl.BlockSpec((tl, d), lambda k, i: (jnp.minimum(i, nl - 1), 0)), _full((lc, d)),
                  _full((2, d)), _full((2, d)),
                  pl.BlockSpec((1, 2, tl, half), lambda k, i: (k, 0, jnp.minimum(i, nl - 1), 0)),
                  pl.BlockSpec((1, 2, lc, half), lambda k, i: (k, 0, l // lc, 0))],
        out_specs=[pl.BlockSpec((d, e), lambda k, i: (0, k))],
        out_shape=[jax.ShapeDtypeStruct((d, 4 * e), BF)],
        scratch=[pltpu.VMEM((d, e), F32)], args=(x, ctx, a2, b2, dp42, dp42), name="l0_dw_inproj")
    return g_w, extra


def _dw_outproj0(q3, dr_l, dr_c, gt2, tm, xch=None):
    l, d = dr_l.shape
    nl, nc = l // tm, dr_c.shape[0] // tm
    _, r, half = q3.shape
    nt = nl + nc

    def body(q_ref, dl_ref, dc_ref, g_ref, o_ref, acc_ref):
        i = pl.program_id(0)
        is_ctx = i >= nl

        @pl.when(i == 0)
        def _():
            acc_ref[...] = jnp.zeros_like(acc_ref)

        dr = jnp.where(is_ctx, dc_ref[...], dl_ref[...]).astype(F32)
        dfx = (dr * _sel_row(g_ref, is_ctx)).astype(BF)
        acc_ref[:half, :] += _dot_tn(q_ref[0], dfx)
        acc_ref[half:, :] += _dot_tn(q_ref[1], dfx)

        @pl.when(i == nt - 1)
        def _():
            o_ref[...] = acc_ref[...].astype(BF)

    lat, cx = _lat_or_ctx_specs(tm, d, nl, 1, 0)
    (g_w,), extra = _hosted_call(
        body, xch, grid=(nt,),
        in_specs=[pl.BlockSpec((2, tm, half), lambda i: (0, i, 0)), lat, cx, _full((2, d))],
        out_specs=[_full((2 * half, d))], out_shape=[jax.ShapeDtypeStruct((2 * half, d), BF)],
        scratch=[pltpu.VMEM((2 * half, d), F32)], args=(q3, dr_l, dr_c, gt2), name="l0_dw_outproj")
    return g_w, extra


def _cr_tile(j, cap=256):
    for cand in (1024, 512, 256, 128, 64, 32, 16, 8):
        if cand <= cap and j % cand == 0:
            return cand
    raise ValueError(j)


def _final(w_cr, w_out, xh_cr, tgt_cr, vecs):
    j, e16 = w_cr.shape
    e = e16 // CHUNK
    d = w_out.shape[1]
    tj = _cr_tile(j)

    def body(w_ref, wo_hbm, xh_ref, t_ref, v_ref, dr_ref, acc_ref, wo_ref):
        @pl.when(jnp.logical_and(pl.program_id(0) == 0, pl.program_id(1) == 0))
        def _():
            pltpu.sync_copy(wo_hbm, wo_ref)
            acc_ref[...] = jnp.zeros_like(acc_ref)

        o = _dot(w_ref[...], wo_ref[...])
        x1 = xh_ref[...] * v_ref[0:1, :] + v_ref[1:2, :]
        rr = DN_ALPHA * x1 + v_ref[2:3, :] * o
        mu = jnp.mean(rr, axis=-1, keepdims=True)
        cen = rr - mu
        rstd = lax.rsqrt(jnp.mean(cen * cen, axis=-1, keepdims=True) + LN_EPS)
        xh2 = cen * rstd
        err = xh2 * v_ref[3:4, :] + v_ref[4:5, :] - t_ref[...]
        dy = err * (1.0 / d)
        dxh = dy * v_ref[3:4, :]
        dr = rstd * (dxh - jnp.mean(dxh, axis=-1, keepdims=True) - xh2 * jnp.mean(dxh * xh2, axis=-1, keepdims=True))
        dr_ref[...] = dr.astype(BF)
        acc_ref[0:1, :] += jnp.sum(dy * xh2, axis=0, keepdims=True)
        acc_ref[1:2, :] += jnp.sum(dy, axis=0, keepdims=True)
        acc_ref[2:3, :] += jnp.sum(dr * o, axis=0, keepdims=True)
        acc_ref[3:4, :] += (0.5 / d) * jnp.sum(err * err, axis=0, keepdims=True)

    tok_d = pl.BlockSpec((tj, d), lambda t, s: (t, s))
    return pl.pallas_call(
        body, name="l1_final", grid=(j // tj, CHUNK),
        in_specs=[pl.BlockSpec((tj, e), lambda t, s: (t, s)), ANY, tok_d, tok_d, _full((8, d))],
        out_specs=[tok_d, _full((8, d))],
        out_shape=[jax.ShapeDtypeStruct((j, CHUNK * d), BF), jax.ShapeDtypeStruct((8, d), F32)],
        scratch_shapes=[pltpu.VMEM(w_out.shape, BF)], compiler_params=_cparams(),
    )(w_cr, w_out, xh_cr, tgt_cr, vecs)


def _dw_cr(lhs, rhs, lhs_kind, rhs_kind, vec, bias_sum, init, name):
    j = lhs.shape[0]
    k = lhs.shape[1] // CHUNK
    n = rhs.shape[1] // CHUNK
    tj = _cr_tile(j, 1024)
    nh = 2 if k * n * 4 > (8 << 20) else 1
    tn = n // nh
    nt = j // tj
    has_init = init is not None

    def body(*refs):
        refs = list(refs)
        l_ref, r_ref = refs[0], refs[1]
        pos = 2
        v_ref = None
        if vec is not None:
            v_ref = refs[pos]
            pos += 1
        i_ref = None
        if has_init:
            i_ref = refs[pos]
            pos += 1
        o_ref = refs[pos]
        pos += 1
        bs_ref = None
        if bias_sum:
            bs_ref = refs[pos]
            pos += 1
        acc_ref = refs[pos]
        t, s = pl.program_id(1), pl.program_id(2)
        first = jnp.logical_and(t == 0, s == 0)

        @pl.when(first)
        def _():
            acc_ref[...] = i_ref[...] if has_init else jnp.zeros_like(acc_ref)
            if bias_sum:
                bs_ref[...] = jnp.zeros_like(bs_ref)

        if lhs_kind == "mod":
            lv = (l_ref[...] * v_ref[0:1, :] + v_ref[1:2, :]).astype(BF)
        else:
            lv = l_ref[...]
        if rhs_kind == "scaled":
            rv = (r_ref[...].astype(F32) * v_ref[0:1, :]).astype(BF)
        else:
            rv = r_ref[...]
        acc_ref[...] += _dot_tn(lv, rv)
        if bias_sum:
            bs_ref[0:1, :] += jnp.sum(rv.astype(F32), axis=0, keepdims=True)

        @pl.when(jnp.logical_and(t == nt - 1, s == CHUNK - 1))
        def _():
            o_ref[...] = acc_ref[...].astype(BF)

    l_spec = pl.BlockSpec((tj, k), lambda h, t, s: (t, s))
    r_spec = pl.BlockSpec((tj, tn), lambda h, t, s: (t, s * nh + h))
    in_specs, args = [l_spec, r_spec], [lhs, rhs]
    if vec is not None:
        in_specs.append(_full(vec.shape))
        args.append(vec)
    o_spec = pl.BlockSpec((k, tn), lambda h, t, s: (0, h))
    if has_init:
        in_specs.append(o_spec)
        args.append(init)
    out_specs, out_shape = [o_spec], [jax.ShapeDtypeStruct((k, n), BF)]
    if bias_sum:
        out_specs.append(pl.BlockSpec((8, tn), lambda h, t, s: (0, h)))
        out_shape.append(jax.ShapeDtypeStruct((8, n), F32))
    res = pl.pallas_call(
        body, name=name, grid=(nh, nt, CHUNK), in_specs=in_specs, out_specs=out_specs, out_shape=out_shape,
        scratch_shapes=[pltpu.VMEM((k, tn), F32)], compiler_params=_cparams(),
    )(*args)
    return res if bias_sum else res[0]


GT_ROWS = CHUNK * S5_P
ZG_W = 2 * 2 * S5_N
PAIR_W = 2 * ZG_W
GROUPS_PER_STEP = 4


def _inproj1_gt(xh_cr, a1, b1, wu_t, w_z, tag):
    j, d16 = xh_cr.shape
    d = d16 // CHUNK
    e = wu_t.shape[0]
    g = e // S5_P
    tj = _cr_tile(j, 256)

    def body(x_ref, a_ref, b_ref, wu_hbm, wz_hbm, u_ref, z_ref, wu_ref, wz_ref):
        @pl.when(jnp.logical_and(pl.program_id(0) == 0, pl.program_id(1) == 0))
        def _():
            pltpu.sync_copy(wu_hbm, wu_ref)
            pltpu.sync_copy(wz_hbm, wz_ref)

        h = (x_ref[...] * a_ref[...] + b_ref[...]).astype(BF)
        u_ref[...] = _dot_nt(wu_ref[...], h).reshape(g, S5_P, tj).astype(BF)
        z_ref[...] = _dot(h, wz_ref[...]).astype(BF)

    return pl.pallas_call(
        body, name="l1_inproj_" + tag, grid=(j // tj, CHUNK),
        in_specs=[pl.BlockSpec((tj, d), lambda t, s: (t, s)), _full((1, d)), _full((1, d)), ANY, ANY],
        out_specs=[pl.BlockSpec((g, S5_P, tj), lambda t, s: (0, s, t)), pl.BlockSpec((tj, e), lambda t, s: (t, s))],
        out_shape=[jax.ShapeDtypeStruct((g, GT_ROWS, j), BF), jax.ShapeDtypeStruct((j, CHUNK * e), BF)],
        scratch_shapes=[pltpu.VMEM(wu_t.shape, BF), pltpu.VMEM(w_z.shape, BF)], compiler_params=_cparams(),
    )(xh_cr, a1, b1, wu_t, w_z)


def _gt_spec(j, gb=GROUPS_PER_STEP):
    return pl.BlockSpec((gb, GT_ROWS, j), lambda i: (i, 0, 0))


def _zg_spec(j, gb=GROUPS_PER_STEP):
    return pl.BlockSpec((j, gb * ZG_W), lambda i: (0, i))


def _w_spec(width, gb=GROUPS_PER_STEP):
    return pl.BlockSpec((gb, GT_ROWS, width), lambda i: (i, 0, 0))


def _pair_lanes(k):
    return slice((k // 2) * PAIR_W, (k // 2 + 1) * PAIR_W)


def _s5_z(ut_l, ut_c, bc):
    g, _, jl = ut_l.shape
    jc = ut_c.shape[2]
    gb = GROUPS_PER_STEP

    def body(ul_ref, uc_ref, bc_ref, zl_ref, zc_ref):
        for k in range(0, gb, 2):
            zl_ref[:, _pair_lanes(k)] = _dot_tn(ul_ref[k], bc_ref[k]) + _dot_tn(ul_ref[k + 1], bc_ref[k + 1])
            zc_ref[:, _pair_lanes(k)] = _dot_tn(uc_ref[k], bc_ref[k]) + _dot_tn(uc_ref[k + 1], bc_ref[k + 1])

    return pl.pallas_call(
        body, name="l1_s5_z", grid=(g // gb,), in_specs=[_gt_spec(jl), _gt_spec(jc), _w_spec(PAIR_W)],
        out_specs=[_zg_spec(jl), _zg_spec(jc)],
        out_shape=[jax.ShapeDtypeStruct((jl, g * ZG_W), F32), jax.ShapeDtypeStruct((jc, g * ZG_W), F32)],
        compiler_params=_cparams(),
    )(ut_l, ut_c, bc)


def _s5_y(ut_l, s_l, mt_t, cct):
    g, _, jl = ut_l.shape
    gb = GROUPS_PER_STEP

    def body(u_ref, s_ref, mt_ref, cc_ref, y_ref):
        for k in range(gb):
            s_k = s_ref[:, _pair_lanes(k)].astype(BF)
            y_ref[k] = (_dot(mt_ref[k], u_ref[k]) + _dot_nt(cc_ref[k], s_k)).astype(BF)

    return pl.pallas_call(
        body, name="l1_s5_y", grid=(g // gb,),
        in_specs=[_gt_spec(jl), _zg_spec(jl), _w_spec(GT_ROWS), _w_spec(PAIR_W)],
        out_specs=_gt_spec(jl), out_shape=jax.ShapeDtypeStruct((g, GT_ROWS, jl), BF), compiler_params=_cparams(),
    )(ut_l, s_l, mt_t, cct)


def _s5_ds(dyt_l, cct):
    g, _, jl = dyt_l.shape
    gb = GROUPS_PER_STEP

    def body(dy_ref, cc_ref, ds_ref):
        for k in range(0, gb, 2):
            ds_ref[:, _pair_lanes(k)] = _dot_tn(dy_ref[k], cc_ref[k]) + _dot_tn(dy_ref[k + 1], cc_ref[k + 1])

    return pl.pallas_call(
        body, name="l1_s5_ds", grid=(g // gb,), in_specs=[_gt_spec(jl), _w_spec(PAIR_W)], out_specs=_zg_spec(jl),
        out_shape=jax.ShapeDtypeStruct((jl, g * ZG_W), F32), compiler_params=_cparams(),
    )(dyt_l, cct)


def _s5_dx(dyt_l, dz_l, dz_c, mt, bc):
    g, _, jl = dyt_l.shape
    jc = dz_c.shape[0]
    gb = GROUPS_PER_STEP

    def body(dy_ref, dzl_ref, dzc_ref, mt_ref, bc_ref, dul_ref, duc_ref):
        for k in range(gb):
            dzl = dzl_ref[:, _pair_lanes(k)].astype(BF)
            dzc = dzc_ref[:, _pair_lanes(k)].astype(BF)
            dul_ref[k] = (_dot(mt_ref[k], dy_ref[k]) + _dot_nt(bc_ref[k], dzl)).astype(BF)
            duc_ref[k] = _dot_nt(bc_ref[k], dzc).astype(BF)

    return pl.pallas_call(
        body, name="l1_s5_dx", grid=(g // gb,),
        in_specs=[_gt_spec(jl), _zg_spec(jl), _zg_spec(jc), _w_spec(GT_ROWS), _w_spec(PAIR_W)],
        out_specs=[_gt_spec(jl), _gt_spec(jc)],
        out_shape=[jax.ShapeDtypeStruct((g, GT_ROWS, jl), BF), jax.ShapeDtypeStruct((g, GT_ROWS, jc), BF)],
        compiler_params=_cparams(),
    )(dyt_l, dz_l, dz_c, mt, bc)


def _s5_dw(ut_l, ut_c, dyt_l, dz_l, dz_c, s_l):
    g, _, jl = ut_l.shape
    jc = ut_c.shape[2]
    gb = GROUPS_PER_STEP

    def body(ul_ref, uc_ref, dy_ref, dzl_ref, dzc_ref, s_ref, dmt_ref, dbc_ref, dcc_ref):
        for k in range(gb):
            lanes = _pair_lanes(k)
            dmt_ref[k] = _dot_nt(ul_ref[k], dy_ref[k])
            dbc_ref[k] = (_dot(ul_ref[k], dzl_ref[:, lanes].astype(BF))
                          + _dot(uc_ref[k], dzc_ref[:, lanes].astype(BF)))
            dcc_ref[k] = _dot(dy_ref[k], s_ref[:, lanes].astype(BF))

    sd_m = jax.ShapeDtypeStruct((g, GT_ROWS, GT_ROWS), F32)
    sd_p = jax.ShapeDtypeStruct((g, GT_ROWS, PAIR_W), F32)
    return pl.pallas_call(
        body, name="l1_s5_dw", grid=(g // gb,),
        in_specs=[_gt_spec(jl), _gt_spec(jc), _gt_spec(jl), _zg_spec(jl), _zg_spec(jc), _zg_spec(jl)],
        out_specs=[_w_spec(GT_ROWS), _w_spec(PAIR_W), _w_spec(PAIR_W)], out_shape=[sd_m, sd_p, sd_p],
        compiler_params=_cparams(),
    )(ut_l, ut_c, dyt_l, dz_l, dz_c, s_l)


def _scan_g(z_l, z_c, coef, chains, conj, s_l=None, s_c=None, name="l1_scan"):
    jl, w_all = z_l.shape
    jc = z_c.shape[0]
    gb = 2 * GROUPS_PER_STEP if w_all % (2 * GROUPS_PER_STEP * ZG_W) == 0 else GROUPS_PER_STEP
    wb = gb * ZG_W
    nch = wb // 256
    with_da = s_l is not None
    sign = -1.0 if conj else 1.0

    def body(*refs):
        zl_ref, zc_ref, cf_ref = refs[:3]
        k0 = 3
        if with_da:
            sl_ref, sc_ref = refs[3:5]
            k0 = 5
        ol_ref, oc_ref = refs[k0:k0 + 2]
        rowi = lax.broadcasted_iota(jnp.int32, (8, 128), 0)

        def lanes_of(ch):
            return slice(ch * 256, ch * 256 + 128), slice(ch * 256 + 128, (ch + 1) * 256)

        def coefs(ch, r0, nr):
            lr, li = lanes_of(ch)
            return cf_ref[r0:r0 + nr, lr], sign * cf_ref[r0:r0 + nr, li]

        def shift(v, sh, rev):
            if rev:
                return jnp.where(rowi < 8 - sh, pltpu.roll(v, 8 - sh, 0), 0.0)
            return jnp.where(rowi >= sh, pltpu.roll(v, sh, 0), 0.0)

        zero_row = jnp.zeros((1, 128), F32)
        zero_tile = jnp.zeros((8, 128), F32)
        carry = [zero_row] * (2 * nch)
        da = [zero_tile] * (2 * nch)
        for seg in range(len(chains[0])):
            which = chains[0][seg][0]
            assert chains[1][seg][0] == which
            revs = (chains[0][seg][1], chains[1][seg][1])
            src, dst = (zc_ref, oc_ref) if which == "c" else (zl_ref, ol_ref)
            sref = ((sc_ref if which == "c" else sl_ref) if with_da else None)
            ng = (jc if which == "c" else jl) // 8

            def step(it, st, src=src, dst=dst, sref=sref, ng=ng, revs=revs):
                carry_, da_ = list(st[:2 * nch]), list(st[2 * nch:])
                for ch in range(nch):
                    rev = revs[ch % 2]
                    lr, li = lanes_of(ch)
                    grp = (ng - 1 - it) if rev else it
                    off = pl.multiple_of(grp * 8, 8)
                    xr, xi = src[pl.ds(off, 8), lr], src[pl.ds(off, 8), li]
                    for sh, r0 in ((1, 0), (2, 1), (4, 2)):
                        ar, ai = coefs(ch, r0, 1)
                        sr, si = shift(xr, sh, rev), shift(xi, sh, rev)
                        xr, xi = xr + ar * sr - ai * si, xi + ar * si + ai * sr
                    tr, ti = coefs(ch, 16, 8) if rev else coefs(ch, 8, 8)
                    cr_, ci_ = carry_[2 * ch], carry_[2 * ch + 1]
                    ir = xr + tr * cr_ - ti * ci_
                    ii = xi + tr * ci_ + ti * cr_
                    if rev:
                        er = jnp.where(rowi == 7, cr_, pltpu.roll(ir, 7, 0))
                        ei = jnp.where(rowi == 7, ci_, pltpu.roll(ii, 7, 0))
                        carry_[2 * ch], carry_[2 * ch + 1] = ir[0:1], ii[0:1]
                    else:
                        er = jnp.where(rowi == 0, cr_, pltpu.roll(ir, 1, 0))
                        ei = jnp.where(rowi == 0, ci_, pltpu.roll(ii, 1, 0))
                        carry_[2 * ch], carry_[2 * ch + 1] = ir[7:8], ii[7:8]
                    dst[pl.ds(off, 8), lr] = er
                    dst[pl.ds(off, 8), li] = ei
                    if sref is not None:
                        s_r, s_i = sref[pl.ds(off, 8), lr], sref[pl.ds(off, 8), li]
                        da_[2 * ch] = da_[2 * ch] + s_r * er + s_i * ei
                        da_[2 * ch + 1] = da_[2 * ch + 1] + s_r * ei - s_i * er
                return (*carry_, *da_)

            st = lax.fori_loop(0, ng, step, (*carry, *da))
            carry, da = list(st[:2 * nch]), list(st[2 * nch:])
        if with_da:
            da_ref = refs[k0 + 2]
            for ch in range(nch):
                lr, li = lanes_of(ch)
                da_ref[:, lr] = da[2 * ch]
                da_ref[:, li] = da[2 * ch + 1]

    in_specs = [_zg_spec(jl, gb), _zg_spec(jc, gb), pl.BlockSpec((24, wb), lambda i: (0, i))]
    args = [z_l, z_c, coef]
    out_specs = [_zg_spec(jl, gb), _zg_spec(jc, gb)]
    out_shape = [jax.ShapeDtypeStruct(z_l.shape, F32), jax.ShapeDtypeStruct(z_c.shape, F32)]
    if with_da:
        in_specs += [_zg_spec(jl, gb), _zg_spec(jc, gb)]
        args += [s_l, s_c]
        out_specs.append(pl.BlockSpec((8, wb), lambda i: (0, i)))
        out_shape.append(jax.ShapeDtypeStruct((8, w_all), F32))
    return pl.pallas_call(body, name=name, grid=(w_all // wb,), in_specs=in_specs, out_specs=out_specs,
                          out_shape=out_shape, compiler_params=_cparams())(*args)


def _gt_tok_spec(g, tj):
    return pl.BlockSpec((g, S5_P, tj), lambda t, s: (0, s, t))


def _glu_fwd_gt(yt, z_cr, w_glu, b_glu):
    g, _, j = yt.shape
    e = g * S5_P
    tj = _cr_tile(j)

    def body(y_ref, z_ref, w_hbm, b_ref, o_ref, sg_ref, w_ref):
        @pl.when(jnp.logical_and(pl.program_id(0) == 0, pl.program_id(1) == 0))
        def _():
            pltpu.sync_copy(w_hbm, w_ref)

        y = jnp.transpose(y_ref[...].reshape(e, tj).astype(F32))
        gl = _gelu_parts(y)[0]
        sg = _sigmoid(_dot(gl.astype(BF), w_ref[...]) + b_ref[...])
        z = z_ref[...].astype(F32)
        o_ref[...] = (gl * sg * (z * _sigmoid(z))).astype(BF)
        sg_ref[...] = sg.astype(BF)

    tok = pl.BlockSpec((tj, e), lambda t, s: (t, s))
    return pl.pallas_call(
        body, name="l1_glu_fwd", grid=(j // tj, CHUNK),
        in_specs=[_gt_tok_spec(g, tj), tok, ANY, _full((1, e))], out_specs=[tok, tok],
        out_shape=[jax.ShapeDtypeStruct((j, CHUNK * e), BF), jax.ShapeDtypeStruct((j, CHUNK * e), BF)],
        scratch_shapes=[pltpu.VMEM(w_glu.shape, BF)], compiler_params=_cparams(),
    )(yt, z_cr, w_glu, b_glu)


def _glu_bwd_gt(dr_cr, gt1, w_out, w_glu, yt, z_cr, sg_cr):
    g, _, j = yt.shape
    e, d = w_out.shape
    tj = _cr_tile(j)

    def body(dr_ref, g_ref, wo_hbm, wg_hbm, y_ref, z_ref, sg_ref, dz_ref, dt_ref, dy_ref, wo_ref, wg_ref):
        @pl.when(jnp.logical_and(pl.program_id(0) == 0, pl.program_id(1) == 0))
        def _():
            pltpu.sync_copy(wo_hbm, wo_ref)
            pltpu.sync_copy(wg_hbm, wg_ref)

        do = (dr_ref[...].astype(F32) * g_ref[...]).astype(BF)
        dw = _dot_nt(do, wo_ref[...])
        y = jnp.transpose(y_ref[...].reshape(e, tj).astype(F32))
        gl, dgel = _gelu_parts(y)
        z = z_ref[...].astype(F32)
        sz = _sigmoid(z)
        sg = sg_ref[...].astype(F32)
        dg2 = dw * (z * sz)
        dz_ref[...] = (dw * gl * sg * (sz * (1.0 + z * (1.0 - sz)))).astype(BF)
        dt = (dg2 * gl * sg * (1.0 - sg)).astype(BF)
        dt_ref[...] = dt
        dy = (dg2 * sg + _dot_nt(dt, wg_ref[...])) * dgel
        dy_ref[...] = jnp.transpose(dy).reshape(g, S5_P, tj).astype(BF)

    tok_e = pl.BlockSpec((tj, e), lambda t, s: (t, s))
    return pl.pallas_call(
        body, name="l1_glu_bwd", grid=(j // tj, CHUNK),
        in_specs=[pl.BlockSpec((tj, d), lambda t, s: (t, s)), _full((1, d)), ANY, ANY, _gt_tok_spec(g, tj), tok_e, tok_e],
        out_specs=[tok_e, tok_e, _gt_tok_spec(g, tj)],
        out_shape=[jax.ShapeDtypeStruct((j, CHUNK * e), BF), jax.ShapeDtypeStruct((j, CHUNK * e), BF),
                   jax.ShapeDtypeStruct((g, GT_ROWS, j), BF)],
        scratch_shapes=[pltpu.VMEM(w_out.shape, BF), pltpu.VMEM(w_glu.shape, BF)], compiler_params=_cparams(),
    )(dr_cr, gt1, w_out, w_glu, yt, z_cr, sg_cr)


def _bwd_inproj1_gt(dut, dz_cr, wu_t, w_z, xh_cr, rs_cr, dr2_cr, vecs, tag):
    g, _, j = dut.shape
    e, d = wu_t.shape
    tj = _cr_tile(j)

    def body(du_ref, dz_ref, wu_hbm, wz_hbm, xh_ref, rs_ref, dr2_ref, v_ref, dr1_ref, acc_ref, wu_ref, wz_ref):
        @pl.when(jnp.logical_and(pl.program_id(0) == 0, pl.program_id(1) == 0))
        def _():
            pltpu.sync_copy(wu_hbm, wu_ref)
            pltpu.sync_copy(wz_hbm, wz_ref)
            acc_ref[...] = jnp.zeros_like(acc_ref)

        dh = _dot_tn(du_ref[...].reshape(e, tj), wu_ref[...]) + _dot_nt(dz_ref[...], wz_ref[...])
        xh = xh_ref[...]
        x1 = xh * v_ref[0:1, :] + v_ref[1:2, :]
        dx1 = DN_ALPHA * dr2_ref[...].astype(F32) + dh * v_ref[2:3, :]
        dxh = dx1 * v_ref[0:1, :]
        rstd = rs_ref[:, 0:1]
        dr1 = rstd * (dxh - jnp.mean(dxh, axis=-1, keepdims=True) - xh * jnp.mean(dxh * xh, axis=-1, keepdims=True))
        dr1_ref[...] = dr1.astype(BF)
        acc_ref[0:1, :] += jnp.sum(dh * x1, axis=0, keepdims=True)
        acc_ref[1:2, :] += jnp.sum(dh, axis=0, keepdims=True)
        acc_ref[2:3, :] += jnp.sum(dx1 * xh, axis=0, keepdims=True)
        acc_ref[3:4, :] += jnp.sum(dx1, axis=0, keepdims=True)

    tok_d = pl.BlockSpec((tj, d), lambda t, s: (t, s))
    return pl.pallas_call(
        body, name="l1_bwd_inproj_" + tag, grid=(j // tj, CHUNK),
        in_specs=[_gt_tok_spec(g, tj), pl.BlockSpec((tj, e), lambda t, s: (t, s)), ANY, ANY, tok_d,
                  pl.BlockSpec((tj, 128), lambda t, s: (t, s)), tok_d, _full((8, d))],
        out_specs=[tok_d, _full((8, d))],
        out_shape=[jax.ShapeDtypeStruct((j, CHUNK * d), BF), jax.ShapeDtypeStruct((8, d), F32)],
        scratch_shapes=[pltpu.VMEM(wu_t.shape, BF), pltpu.VMEM(w_z.shape, BF)], compiler_params=_cparams(),
    )(dut, dz_cr, wu_t, w_z, xh_cr, rs_cr, dr2_cr, vecs)


def _dw_gt(lhs_gt, rhs_cr, lhs_gelu, vec, bias_sum, init, out_dtype, name, xch=None):
    g, _, j = lhs_gt.shape
    e = g * S5_P
    n = rhs_cr.shape[1] // CHUNK
    tj = _cr_tile(j, 512 if j % 512 == 0 else 256)
    nh = 2 if e * n * 4 > (8 << 20) else 1
    tn = n // nh
    nt = j // tj
    has_init = init is not None

    def body(*refs):
        refs = list(refs)
        l_ref, r_ref = refs[0], refs[1]
        pos = 2
        v_ref = i_ref = bs_ref = None
        if vec is not None:
            v_ref = refs[pos]
            pos += 1
        if has_init:
            i_ref = refs[pos]
            pos += 1
        o_ref = refs[pos]
        pos += 1
        if bias_sum:
            bs_ref = refs[pos]
            pos += 1
        acc_ref = refs[pos]
        t, s = pl.program_id(1), pl.program_id(2)

        @pl.when(jnp.logical_and(t == 0, s == 0))
        def _():
            acc_ref[...] = i_ref[...] if has_init else jnp.zeros_like(acc_ref)
            if bias_sum:
                bs_ref[...] = jnp.zeros_like(bs_ref)

        lv = l_ref[...].reshape(e, tj)
        if lhs_gelu:
            lv = _gelu_parts(lv.astype(F32))[0].astype(BF)
        if vec is not None:
            rv = (r_ref[...] * v_ref[0:1, :] + v_ref[1:2, :]).astype(BF)
        else:
            rv = r_ref[...]
        acc_ref[...] += _dot(lv, rv)
        if bias_sum:
            bs_ref[0:1, :] += jnp.sum(rv.astype(F32), axis=0, keepdims=True)

        @pl.when(jnp.logical_and(t == nt - 1, s == CHUNK - 1))
        def _():
            o_ref[...] = acc_ref[...].astype(out_dtype)

    in_specs = [pl.BlockSpec((g, S5_P, tj), lambda h, t, s: (0, s, t)),
                pl.BlockSpec((tj, tn), lambda h, t, s: (t, s * nh + h))]
    args = [lhs_gt, rhs_cr]
    if vec is not None:
        in_specs.append(_full(vec.shape))
        args.append(vec)
    o_spec = pl.BlockSpec((e, tn), lambda h, t, s: (0, h))
    if has_init:
        in_specs.append(o_spec)
        args.append(init)
    out_specs, out_shape = [o_spec], [jax.ShapeDtypeStruct((e, n), out_dtype)]
    if bias_sum:
        out_specs.append(pl.BlockSpec((8, tn), lambda h, t, s: (0, h)))
        out_shape.append(jax.ShapeDtypeStruct((8, n), F32))
    res, extra = _hosted_call(body, xch, grid=(nh, nt, CHUNK), in_specs=in_specs, out_specs=out_specs,
                              out_shape=out_shape, scratch=[pltpu.VMEM((e, tn), F32)], args=args, name=name)
    if xch is not None:
        return (*res, extra) if bias_sum else (res[0], extra)
    return res if bias_sum else res[0]


def _scan_coef_g(lam_re, lam_im, log_step):
    g = lam_re.shape[1]
    ms = jnp.array([1, 2, 4, 0, 0, 0, 0, 0] + list(range(1, 9)) + list(range(8, 0, -1)), F32) * CHUNK
    dt = jnp.exp(log_step)[..., None]
    mag = jnp.exp(ms.reshape(-1, 1, 1, 1) * (lam_re * dt)[None])
    ang = ms.reshape(-1, 1, 1, 1) * (lam_im * dt)[None]
    cr, ci = mag * jnp.cos(ang), mag * jnp.sin(ang)
    both = jnp.stack([cr, ci], axis=2).reshape(24, 2, 2, g // 2, 2, S5_N)
    return both.transpose(0, 3, 1, 2, 4, 5).reshape(24, g * ZG_W)


def _s5_small(lam_re, lam_im, log_step, b_re, b_im, c_re, c_im, d_skip):
    g = lam_re.shape[1]
    t, p = CHUNK, S5_P
    dt = jnp.exp(log_step)[..., None]
    ks = jnp.arange(t + 1, dtype=F32).reshape(t + 1, 1, 1, 1)
    mag = jnp.exp(ks * (lam_re * dt)[None])
    ang = ks * (lam_im * dt)[None]
    pr, pi = mag * jnp.cos(ang), mag * jnp.sin(ang)
    ar, ai = pr[1], pi[1]
    qr, qi = ar - 1.0, ai
    den = lam_re * lam_re + lam_im * lam_im
    fr = (qr * lam_re + qi * lam_im) / den
    fi = (qi * lam_re - qr * lam_im) / den
    bt_re, bt_im = b_re.transpose(0, 1, 3, 2), b_im.transpose(0, 1, 3, 2)
    bbr = fr[:, :, None, :] * bt_re - fi[:, :, None, :] * bt_im
    bbi = fr[:, :, None, :] * bt_im + fi[:, :, None, :] * bt_re
    lay = lambda a_r, a_i: jnp.stack([a_r, a_i], axis=0).transpose(3, 2, 0, 1, 4)
    by_dir = lambda a, f0, f1: jnp.stack([f0(a[:, 0]), f1(a[:, 1])], axis=1)
    rev = lambda a: jnp.flip(a, axis=0)
    same = lambda a: a
    pwb = lay(by_dir(pr[:t], rev, same), by_dir(pi[:t], rev, same))
    pwc = lay(by_dir(pr[1:], same, rev), by_dir(pi[1:], same, rev))
    bb = jnp.stack([bbr, bbi], axis=0).transpose(2, 1, 0, 3, 4)
    cc = jnp.stack([c_re, c_im], axis=0).transpose(2, 1, 0, 3, 4)
    dmat = jnp.eye(p, dtype=F32)[None] * d_skip.reshape(g, p)[:, :, None]
    return pwb, pwc, bb, cc, dmat, pr[t], pi[t]


def _pair_cols(r, ri, g2):
    c0 = (r * 2 + ri) * 128 + g2 * S5_N
    return slice(c0, c0 + S5_N)


def _rows_rep(a):
    return jnp.broadcast_to(a[:, None, :], (CHUNK, S5_P, a.shape[-1])).reshape(GT_ROWS, a.shape[-1])


def _rows_tile(a):
    return jnp.broadcast_to(a[None], (CHUNK, S5_P, a.shape[-1])).reshape(GT_ROWS, a.shape[-1])


def _sum_blocks(a):
    return jnp.sum(a.reshape(CHUNK, S5_P, a.shape[-1]), axis=0)


def _sum_in_blocks(a):
    return jnp.sum(a.reshape(CHUNK, S5_P, a.shape[-1]), axis=1)


def _ab_rows(pwb_ref, bb_ref, k, r):
    prs, pis = _rows_rep(pwb_ref[k, r, 0]), _rows_rep(pwb_ref[k, r, 1])
    bbr, bbi = _rows_tile(bb_ref[k, r, 0]), _rows_tile(bb_ref[k, r, 1])
    return prs * bbr - pis * bbi, prs * bbi + pis * bbr, prs, pis, bbr, bbi


def _s5_weights_fwd(pwb, pwc, bb, cc, dmat):
    g = pwb.shape[0]
    gb = GROUPS_PER_STEP
    hp = lax.Precision.HIGHEST

    def body(pwb_ref, pwc_ref, bb_ref, cc_ref, dm_ref, mt_ref, mtt_ref, bc_ref, cct_ref):
        zeros = jnp.zeros((GT_ROWS, S5_N), BF)
        nt = (((1,), (1,)), ((), ()))
        for k in range(gb):
            g2 = k % 2
            kds = []
            for r in range(2):
                for ri in range(2):
                    bc_ref[k, :, _pair_cols(r, ri, 1 - g2)] = zeros
                    cct_ref[k, :, _pair_cols(r, ri, 1 - g2)] = zeros
                abr, abi = _ab_rows(pwb_ref, bb_ref, k, r)[:2]
                bc_ref[k, :, _pair_cols(r, 0, g2)] = abr.astype(BF)
                bc_ref[k, :, _pair_cols(r, 1, g2)] = abi.astype(BF)
                cr, ci = cc_ref[k, r, 0], cc_ref[k, r, 1]
                crt, cit = _rows_tile(cr), _rows_tile(ci)
                prt, pit = _rows_rep(pwc_ref[k, r, 0]), _rows_rep(pwc_ref[k, r, 1])
                cct_ref[k, :, _pair_cols(r, 0, g2)] = (crt * prt - cit * pit).astype(BF)
                cct_ref[k, :, _pair_cols(r, 1, g2)] = (-(crt * pit + cit * prt)).astype(BF)
                kds.append(lax.dot_general(abr, cr, nt, precision=hp, preferred_element_type=F32)
                           - lax.dot_general(abi, ci, nt, precision=hp, preferred_element_type=F32))
            blk = lambda a, s: a[s * S5_P:(s + 1) * S5_P]
            last = CHUNK - 1
            pieces = [blk(kds[1], last - i) for i in range(last)]
            pieces.append(blk(kds[0], last) + blk(kds[1], 0) + dm_ref[k])
            pieces += [blk(kds[0], last - d) for d in range(1, CHUNK)]
            qrow = jnp.concatenate(pieces, axis=1)
            mt = jnp.concatenate([qrow[:, (last - s) * S5_P:(last - s) * S5_P + GT_ROWS] for s in range(CHUNK)], axis=0)
            mt_ref[k] = mt.astype(BF)
            mtt_ref[k] = jnp.transpose(mt).astype(BF)

    small = lambda a: pl.BlockSpec((gb, *a.shape[1:]), lambda i: (i,) + (0,) * (a.ndim - 1))
    return pl.pallas_call(
        body, name="l1_s5_weights", grid=(g // gb,),
        in_specs=[small(pwb), small(pwc), small(bb), small(cc), small(dmat)],
        out_specs=[_w_spec(GT_ROWS), _w_spec(GT_ROWS), _w_spec(PAIR_W), _w_spec(PAIR_W)],
        out_shape=[jax.ShapeDtypeStruct((g, GT_ROWS, GT_ROWS), BF), jax.ShapeDtypeStruct((g, GT_ROWS, GT_ROWS), BF),
                   jax.ShapeDtypeStruct((g, GT_ROWS, PAIR_W), BF), jax.ShapeDtypeStruct((g, GT_ROWS, PAIR_W), BF)],
        compiler_params=_cparams(),
    )(pwb, pwc, bb, cc, dmat)


def _s5_weights_bwd(pwb, pwc, bb, cc, d_mt, d_bc, d_cct):
    g = pwb.shape[0]
    gb = GROUPS_PER_STEP
    hp = lax.Precision.HIGHEST

    def body(pwb_ref, pwc_ref, bb_ref, cc_ref, dmt_ref, dbc_ref, dcc_ref, dpwb_ref, dpwc_ref, dbb_ref, dccp_ref, ddm_ref):
        tn = (((0,), (0,)), ((), ()))
        nn = (((1,), (0,)), ((), ()))
        last = CHUNK - 1
        for k in range(gb):
            g2 = k % 2
            dq = None
            for s in range(CHUNK):
                parts = [dmt_ref[k, s * S5_P:(s + 1) * S5_P, :]]
                if s < last:
                    parts.insert(0, jnp.zeros((S5_P, (last - s) * S5_P), F32))
                if s > 0:
                    parts.append(jnp.zeros((S5_P, s * S5_P), F32))
                padded = jnp.concatenate(parts, axis=1) if len(parts) > 1 else parts[0]
                dq = padded if dq is None else dq + padded
            dblk = lambda d: dq[:, (last + d) * S5_P:(CHUNK + d) * S5_P]
            ddm_ref[k] = dblk(0)
            dkds = [jnp.concatenate([dblk(last - s) for s in range(CHUNK)], axis=0),
                    jnp.concatenate([dblk(-s) for s in range(CHUNK)], axis=0)]
            for r in range(2):
                abr, abi, prs, pis, bbr, bbi = _ab_rows(pwb_ref, bb_ref, k, r)
                cr, ci = cc_ref[k, r, 0], cc_ref[k, r, 1]
                dcr = lax.dot_general(dkds[r], abr, tn, precision=hp, preferred_element_type=F32)
                dci = -lax.dot_general(dkds[r], abi, tn, precision=hp, preferred_element_type=F32)
                dabr = (lax.dot_general(dkds[r], cr, nn, precision=hp, preferred_element_type=F32)
                        + dbc_ref[k, :, _pair_cols(r, 0, g2)])
                dabi = (-lax.dot_general(dkds[r], ci, nn, precision=hp, preferred_element_type=F32)
                        + dbc_ref[k, :, _pair_cols(r, 1, g2)])
                dbb_ref[k, r, 0] = _sum_blocks(prs * dabr + pis * dabi)
                dbb_ref[k, r, 1] = _sum_blocks(prs * dabi - pis * dabr)
                dpwb_ref[k, r, 0] = _sum_in_blocks(dabr * bbr + dabi * bbi)
                dpwb_ref[k, r, 1] = _sum_in_blocks(dabi * bbr - dabr * bbi)
                crt, cit = _rows_tile(cr), _rows_tile(ci)
                prt, pit = _rows_rep(pwc_ref[k, r, 0]), _rows_rep(pwc_ref[k, r, 1])
                d_re = dcc_ref[k, :, _pair_cols(r, 0, g2)]
                d_im = dcc_ref[k, :, _pair_cols(r, 1, g2)]
                dccp_ref[k, r, 0] = dcr + _sum_blocks(d_re * prt - d_im * pit)
                dccp_ref[k, r, 1] = dci - _sum_blocks(d_re * pit + d_im * prt)
                dpwc_ref[k, r, 0] = _sum_in_blocks(d_re * crt - d_im * cit)
                dpwc_ref[k, r, 1] = -_sum_in_blocks(d_re * cit + d_im * crt)

    small = lambda a: pl.BlockSpec((gb, *a.shape[1:]), lambda i: (i,) + (0,) * (a.ndim - 1))
    dmat_sds = jax.ShapeDtypeStruct((g, S5_P, S5_P), F32)
    return pl.pallas_call(
        body, name="l1_s5_weights_bwd", grid=(g // gb,),
        in_specs=[small(pwb), small(pwc), small(bb), small(cc), _w_spec(GT_ROWS), _w_spec(PAIR_W), _w_spec(PAIR_W)],
        out_specs=[small(pwb), small(pwc), small(bb), small(cc), small(dmat_sds)],
        out_shape=[jax.ShapeDtypeStruct(pwb.shape, F32), jax.ShapeDtypeStruct(pwc.shape, F32),
                   jax.ShapeDtypeStruct(bb.shape, F32), jax.ShapeDtypeStruct(cc.shape, F32), dmat_sds],
        compiler_params=_cparams(),
    )(pwb, pwc, bb, cc, d_mt, d_bc, d_cct)


def _from_cr(a, c):
    return a.reshape(a.shape[0] * CHUNK, c)


def _pad8(v):
    return jnp.concatenate([v, jnp.zeros((8 - v.shape[0], v.shape[1]), v.dtype)], axis=0)


def _local_step(x, c, ctx, c_ctx, loss_target, w, late=None, scatter=False, mod=None):
    l, d = x.shape
    lc = ctx.shape[0]
    tm = min(256, lc)
    assert lc == tm and l % tm == 0 and tm % GRID_W == 0 and (tm & (tm - 1)) == 0
    nl = l // tm

    own_mod = mod is None
    if own_mod:
        c8 = _pad8(jnp.stack([c, c_ctx]))
        mod = _ada_fwd(c8, w["ada_w"], w["ada_b"])
    sh = mod[:, :2, :d]
    sc = mod[:, :2, d:2 * d]
    gt = mod[:, :2, 2 * d:]
    ln_g, ln_b = w["ln_g"], w["ln_b"]

    a0, b0 = 1.0 + sc[0], sh[0]
    xch = _Exchange("gather2", [late[n][0] for n in late], [late[n][1] for n in late]) if late else None
    p42, tgt_cr, got = _inproj0(x, ctx, a0, b0, w["conv_w_in"], loss_target, tm, xch)
    if late:
        w = dict(w, **dict(zip(late, got)))
    e = w["conv_w_out"].shape[0]
    half = e // 2
    cw = w["conv_w"].reshape(3, 2, half)
    q3 = _conv_fwd(p42, cw, nl, tm, half)
    xh1_l, xh1_c, rs1_l, rs1_c, fx = _outproj_ln0(q3, w["conv_w_out"], x, ctx, gt[0], tm)
    jl, jc = l // CHUNK, lc // CHUNK

    g0, bb0 = ln_g[0:1], ln_b[0:1]
    a1 = g0 * (1.0 + sc[1])
    b1 = bb0 * (1.0 + sc[1]) + sh[1]
    wu_t = w["ssm_w_in"][:, :e].T
    w_z = w["ssm_w_in"][:, e:]
    ut_l, z_l = _inproj1_gt(xh1_l, a1[0:1], b1[0:1], wu_t, w_z, "lat")
    ut_c, _ = _inproj1_gt(xh1_c, a1[1:2], b1[1:2], wu_t, w_z, "ctx")
    s5 = (w["ssm_lam_re"], w["ssm_lam_im"], w["ssm_log_step"], w["ssm_b_re"], w["ssm_b_im"],
          w["ssm_c_re"], w["ssm_c_im"], w["ssm_d"])
    (pwb, pwc, bbw, ccw, dmat, _, _), s5_vjp = jax.vjp(_s5_small, *s5)
    mt_b, mtt_b, bc_b, cct_b = _s5_weights_fwd(pwb, pwc, bbw, ccw, dmat)
    coef = lax.stop_gradient(_scan_coef_g(*s5[:3]))
    zz_l, zz_c = _s5_z(ut_l, ut_c, bc_b)
    fwd_chains = ((("c", False), ("l", False)), (("c", True), ("l", True)))
    st_l, st_c = _scan_g(zz_l, zz_c, coef, fwd_chains, False, name="l1_scan_fwd")
    yt = _s5_y(ut_l, st_l, mtt_b, cct_b)
    b_glu = w["ssm_b_glu"].reshape(1, e)
    w_cr, sg_cr = _glu_fwd_gt(yt, z_l, w["ssm_w_glu"], b_glu)
    vec_f = _pad8(jnp.concatenate([g0, bb0, gt[1][0:1], ln_g[1:2], ln_b[1:2]], axis=0))
    dr2, acc_f = _final(w_cr, w["ssm_w_out"], xh1_l, tgt_cr, vec_f)
    loss = jnp.sum(acc_f[3])

    gt1 = gt[1][0:1]
    dz_l, dt_l, dyt = _glu_bwd_gt(dr2, gt1, w["ssm_w_out"], w["ssm_w_glu"], yt, z_l, sg_cr)
    g_w_out = _dw_cr(w_cr, dr2, "cr", "scaled", gt1, False, None, "l1_dw_out")
    ds_l = _s5_ds(dyt, cct_b)
    bwd_chains = ((("l", True), ("c", True)), (("l", False), ("c", False)))
    dzz_l, dzz_c, da = _scan_g(ds_l, jnp.zeros_like(zz_c), coef, bwd_chains, True, st_l, st_c, name="l1_scan_bwd")
    dut_l, dut_c = _s5_dx(dyt, dzz_l, dzz_c, mt_b, bc_b)
    d_mt, d_bc, d_cct = _s5_dw(ut_l, ut_c, dyt, dzz_l, dzz_c, st_l)
    n_g = e // S5_P
    da = jnp.sum(da, axis=0).reshape(n_g // 2, 2, 2, 2, S5_N).transpose(1, 2, 0, 3, 4)
    da = da.reshape(2, 2, n_g, S5_N)
    d_pwb, d_pwc, d_bb, d_ccp, d_dm = _s5_weights_bwd(pwb, pwc, bbw, ccw, d_mt, d_bc, d_cct)
    g_s5 = s5_vjp((d_pwb, d_pwc, d_bb, d_ccp, d_dm, da[:, 0], da[:, 1]))

    vec_l = _pad8(jnp.concatenate([g0, bb0, 1.0 + sc[1][0:1]], axis=0))
    vec_c = _pad8(jnp.concatenate([g0, bb0, 1.0 + sc[1][1:2]], axis=0))
    dr1_l, acc_l = _bwd_inproj1_gt(dut_l, dz_l, wu_t, w_z, xh1_l, rs1_l, dr2, vec_l, "lat")
    dr1_c, acc_c = _bwd_inproj1_gt(dut_c, jnp.zeros((jc, CHUNK * e), BF), wu_t, w_z, xh1_c, rs1_c,
                                   jnp.zeros((jc, CHUNK * d), BF), vec_c, "ctx")
    mod_l = jnp.concatenate([a1[0:1], b1[0:1]], axis=0)
    mod_c = jnp.concatenate([a1[1:2], b1[1:2]], axis=0)
    g_ut_c = _dw_gt(dut_c, xh1_c, False, mod_c, False, None, F32, "l1_dw_in_u_ctx")
    g_ut = _dw_gt(dut_l, xh1_l, False, mod_l, False, g_ut_c, BF, "l1_dw_in_u")
    g_in_z = _dw_cr(xh1_l, dz_l, "mod", "cr", mod_l, False, None, "l1_dw_in_z")
    g_w_in1 = jnp.concatenate([g_ut.T, g_in_z], axis=1)

    dr1_ln, dr1_cn = _from_cr(dr1_l, d), _from_cr(dr1_c, d)
    dq3, acc_g0 = _bwd_outproj0(dr1_ln, dr1_cn, gt[0], w["conv_w_out"], fx, tm)
    def carried(names, parts):
        return _Exchange("scatter", parts, [BIG[n] for n in names]) if scatter else None

    dp42, dcw, grad_x, acc_0, _ = _conv_bwd_inproj0(dq3, p42, cw, w["conv_w_in"], x, ctx, dr1_ln, dr1_cn, a0, nl, tm, None)
    g_w_in0, recv1 = _dw_inproj0(x, ctx, a0, b0, dp42, tm, carried(["ssm_w_in", "ssm_w_out"], [g_w_in1, g_w_out]))
    res = _dw_gt(yt, dt_l, True, None, True, None, BF, "l1_dw_glu", carried(["conv_w_in"], [g_w_in0]))
    g_w_glu, bsum, recv2 = res if scatter else (*res, [])
    g_b_glu = bsum[0]
    g_w_out0, recv3 = _dw_outproj0(q3, dr1_ln, dr1_cn, gt[0], tm, carried(["ssm_w_glu"], [g_w_glu]))
    recv = dict(zip(["ssm_w_in", "ssm_w_out", "conv_w_in", "ssm_w_glu"], recv1 + recv2 + recv3))

    zero = jnp.zeros((d,), F32)
    dm0 = jnp.stack([jnp.concatenate([acc_0[2], acc_0[0], acc_g0[0]]), jnp.concatenate([acc_0[3], acc_0[1], acc_g0[1]])])
    dm1 = jnp.stack([jnp.concatenate([acc_l[1], acc_l[0], acc_f[2]]), jnp.concatenate([acc_c[1], acc_c[0], zero])])
    if own_mod:
        g_ada_w, dc8 = _ada_bwd(c8, w["ada_w"], jnp.stack([_pad8(dm0), _pad8(dm1)]), BF)
        g_mod = {"c_ctx": dc8[0, 1] + dc8[1, 1], "ada_w": g_ada_w,
                 "ada_b": jnp.stack([dm0[0] + dm0[1], dm1[0] + dm1[1]])}
    else:
        g_mod = {"mod": jnp.stack([dm0, dm1])}

    grads = {
        **g_mod,
        "ln_g": jnp.stack([acc_l[2] + acc_c[2], acc_f[0]]),
        "ln_b": jnp.stack([acc_l[3] + acc_c[3], acc_f[1]]),
        "conv_w_in": g_w_in0, "conv_w": dcw[:3].reshape(3, e), "conv_w_out": g_w_out0,
        "ssm_w_in": g_w_in1,
        "ssm_lam_re": g_s5[0], "ssm_lam_im": g_s5[1], "ssm_log_step": g_s5[2],
        "ssm_b_re": g_s5[3], "ssm_b_im": g_s5[4], "ssm_c_re": g_s5[5], "ssm_c_im": g_s5[6], "ssm_d": g_s5[7],
        "ssm_w_glu": g_w_glu, "ssm_b_glu": g_b_glu, "ssm_w_out": g_w_out,
    }
    for n in recv:
        del grads[n]
    return loss, grad_x, grads, recv


WEIGHTS = ["c_ctx", "ada_w", "ada_b", "ln_g", "ln_b", "conv_w_in", "conv_w", "conv_w_out", "ssm_w_in",
           "ssm_lam_re", "ssm_lam_im", "ssm_log_step", "ssm_b_re", "ssm_b_im", "ssm_c_re", "ssm_c_im",
           "ssm_d", "ssm_w_glu", "ssm_b_glu", "ssm_w_out"]
BIG = {"ada_w": 1, "conv_w_in": 1, "conv_w_out": 0, "ssm_w_in": 1, "ssm_w_glu": 0, "ssm_w_out": 0}
SMALL_SHARDED = ["conv_w", "ssm_d", "ssm_b_glu"]
REPLICATED = ["c_ctx", "ada_b", "ln_g", "ln_b", "ssm_lam_re", "ssm_lam_im", "ssm_log_step",
              "ssm_b_re", "ssm_b_im", "ssm_c_re", "ssm_c_im"]
NATIVE_SMALL = ["ssm_b_re", "ssm_b_im", "ssm_c_re", "ssm_c_im"]


def _view2d(name, a):
    return a.reshape(-1, a.shape[-1])


def kernel(x, c, ctx, c_ctx, ada_w, ada_b, ln_g, ln_b, conv_w_in, conv_w, conv_w_out, ssm_w_in, ssm_lam_re, ssm_lam_im, ssm_log_step, ssm_b_re, ssm_b_im, ssm_c_re, ssm_c_im, ssm_d, ssm_w_glu, ssm_b_glu, ssm_w_out, loss_target, m_c_ctx, m_ada_w, m_ada_b, m_ln_g, m_ln_b, m_conv_w_in, m_conv_w, m_conv_w_out, m_ssm_w_in, m_ssm_lam_re, m_ssm_lam_im, m_ssm_log_step, m_ssm_b_re, m_ssm_b_im, m_ssm_c_re, m_ssm_c_im, m_ssm_d, m_ssm_w_glu, m_ssm_b_glu, m_ssm_w_out, v_c_ctx, v_ada_w, v_ada_b, v_ln_g, v_ln_b, v_conv_w_in, v_conv_w, v_conv_w_out, v_ssm_w_in, v_ssm_lam_re, v_ssm_lam_im, v_ssm_log_step, v_ssm_b_re, v_ssm_b_im, v_ssm_c_re, v_ssm_c_im, v_ssm_d, v_ssm_w_glu, v_ssm_b_glu, v_ssm_w_out):
    args = locals()
    wt = {n: args[n] for n in WEIGHTS}
    mt = {n: args["m_" + n] for n in WEIGHTS}
    vt = {n: args["v_" + n] for n in WEIGHTS}

    me = 4 * lax.axis_index("x") + 2 * lax.axis_index("y") + lax.axis_index("c")
    d = x.shape[-1]
    d3 = 3 * d
    wa = d3 // N_DEV

    big_names = [n for n in BIG if n != "ada_w"]
    shard = {n: _view2d(n, wt[n]).astype(BF) for n in big_names}
    small = jnp.concatenate([wt["conv_w"][0], wt["ssm_d"], wt["ssm_b_glu"]], axis=0)
    small = jnp.concatenate([small, jnp.zeros((3, small.shape[1]), F32)], axis=0)
    w_in_full, small_full, c_all = _all_gather([shard["conv_w_in"], small, _pad8(c)], [1, 1, 0], "gather_weights", "gather2")
    late = {n: (shard[n], BIG[n]) for n in big_names if n != "conv_w_in"}
    c16 = jnp.concatenate([c_all[::8], c_ctx[None], jnp.zeros((16 - N_DEV - 1, d), F32)], axis=0)
    ada_w_b = ada_w.astype(BF)
    ada_b_mine = lax.dynamic_slice_in_dim(ada_b, me * wa, wa, axis=1)
    mod_part = _ada_fwd(c16, ada_w_b, ada_b_mine)
    mod_all = _all_gather([mod_part.reshape(32, wa)], [1], "gather_mod")[0].reshape(2, 16, d3)
    mod = jnp.stack([lax.dynamic_index_in_dim(mod_all, me, axis=1, keepdims=False), mod_all[:, N_DEV]], axis=1)
    w = {
        "ln_g": ln_g, "ln_b": ln_b, "conv_w_in": w_in_full, "conv_w": small_full[0:3],
        "ssm_lam_re": ssm_lam_re[0], "ssm_lam_im": ssm_lam_im[0],
        "ssm_log_step": ssm_log_step[0], "ssm_b_re": ssm_b_re[0], "ssm_b_im": ssm_b_im[0],
        "ssm_c_re": ssm_c_re[0], "ssm_c_im": ssm_c_im[0], "ssm_d": small_full[3], "ssm_b_glu": small_full[4],
    }

    loss, grad_x, g, recv_big = _local_step(x[0], c[0], ctx[0], c_ctx, loss_target[0], w, late, True, mod)

    dmod_all = _all_gather([_pad8(g["mod"].reshape(4, d3))], [0], "gather_dmod")[0].reshape(N_DEV, 8, d3)
    dmod_all = dmod_all[:, :4].reshape(N_DEV, 2, 2, d3)
    dm_ctx = dmod_all[0, :, 1]
    for p in range(1, N_DEV):
        dm_ctx = dm_ctx + dmod_all[p, :, 1]
    dm16 = jnp.concatenate([dmod_all[:, :, 0].transpose(1, 0, 2), dm_ctx[:, None], jnp.zeros((2, 16 - N_DEV - 1, d3), F32)], axis=1)
    g_ada_w, dc16 = _ada_bwd(c16, ada_w_b, lax.dynamic_slice_in_dim(dm16, me * wa, wa, axis=2), F32)
    g["c_ctx"] = dc16[0, N_DEV] + dc16[1, N_DEV]
    g_ada_b = jnp.sum(dm16, axis=1)

    blob_names = [n for n in REPLICATED if n != "ada_b"] + SMALL_SHARDED
    flat = jnp.concatenate([g[n].reshape(-1).astype(F32) for n in blob_names] + [loss.reshape(1)])
    nflat = flat.shape[0]
    rows = -(-nflat // (N_DEV * 128 * 8)) * 8
    flat = jnp.concatenate([flat, jnp.zeros((N_DEV * rows * 128 - nflat,), F32)]).reshape(N_DEV * rows, 128)
    last = [n for n in big_names if n not in recv_big]
    recv = _all_to_all([_view2d(n, g[n]) for n in last] + [flat], [BIG[n] for n in last] + [0], "scatter_grads")
    recv_big.update(zip(last, recv[:-1]))
    blob_sum = _sum_partials(recv[-1])
    blob = _all_gather([blob_sum], [0], "gather_small_grads", "gather2")[0].reshape(-1)
    small_g, off = {"ada_b": g_ada_b}, 0
    for n in blob_names:
        shape = wt[n].shape if n in REPLICATED else (*wt[n].shape[:-1], wt[n].shape[-1] * N_DEV)
        size = math.prod(shape)
        small_g[n] = blob[off:off + size].reshape(shape)
        off += size
    loss = blob[off]
    for n in SMALL_SHARDED:
        size = wt[n].shape[-1]
        small_g[n] = lax.dynamic_slice_in_dim(small_g[n], me * size, size, axis=small_g[n].ndim - 1)

    out_g, out_d, out_m, out_v = {}, {}, {}, {}
    recv_big["ada_w"] = _view2d("ada_w", g_ada_w)[None]
    for n in BIG:
        stack = recv_big[n]
        shp = wt[n].shape
        res = _adamw(stack, _view2d(n, wt[n]), _view2d(n, mt[n]), _view2d(n, vt[n]), "adamw_" + n)
        out_g[n], out_d[n], out_m[n], out_v[n] = [r.reshape(shp) for r in res]
    for n in NATIVE_SMALL:
        shp = wt[n].shape
        v2 = lambda a: a.reshape(-1, shp[-1])
        res = _adamw(v2(small_g.pop(n))[None], v2(wt[n]), v2(mt[n]), v2(vt[n]), "adamw_" + n)
        out_g[n], out_d[n], out_m[n], out_v[n] = [r.reshape(shp) for r in res]
    names = list(small_g)
    cat = lambda t: jnp.concatenate([t[n].reshape(-1) for n in names])
    gs, ws, ms, vs = cat(small_g), cat(wt), cat(mt), cat(vt)
    ns = gs.shape[0]
    rs = -(-ns // (128 * 512)) * 512
    padr = lambda a: jnp.concatenate([a, jnp.ones((rs * 128 - ns,), F32)]).reshape(rs, 128)
    res = _adamw(padr(gs)[None], padr(ws), padr(ms), padr(vs), "adamw_small")
    off = 0
    for n in names:
        size = math.prod(wt[n].shape)
        out_g[n], out_d[n], out_m[n], out_v[n] = [r.reshape(-1)[off:off + size].reshape(wt[n].shape) for r in res]
        off += size

    return (loss, grad_x[None], *[out_g[n] for n in WEIGHTS], *[out_d[n] for n in WEIGHTS],
            *[out_m[n] for n in WEIGHTS], *[out_v[n] for n in WEIGHTS])
```
